```python
import math
import jax, jax.numpy as jnp
from jax import lax
import numpy as np

D_MODEL = 1024
BATCH = 8
SEQ = 2048
DEPTH = 2

N_A = DEPTH // 2
N_B = DEPTH - N_A
HG_DK = 128
HG_HEADS = D_MODEL // HG_DK
HG_DV = D_MODEL // HG_HEADS
HG_WIDTH = HG_HEADS * HG_DK
HG_VWIDTH = HG_HEADS * HG_DV
HG_CHUNK = 64
ATT_HD = 64
ATT_QH = D_MODEL // ATT_HD
ATT_KVH = 4
ATT_G = ATT_QH // ATT_KVH
WINDOW = 128
FFN_HIDDEN = -(-8 * D_MODEL // (3 * 256)) * 256
PLE_DIM = 256
DEEPNORM_ALPHA = (2.0 * DEPTH) ** 0.25
DEEPNORM_BETA = (8.0 * DEPTH) ** -0.25
LN_EPS = 1e-5
RMS_EPS = 1e-6

kernel_name = 'yoco_hgrn2_swa_sink_alibi_deepnorm'


def layer_norm(x, g, b):
    xf = x.astype(jnp.float32)
    mu = jnp.mean(xf, axis=-1, keepdims=True)
    var = jnp.mean(jnp.square(xf - mu), axis=-1, keepdims=True)
    return ((xf - mu) * lax.rsqrt(var + LN_EPS) * g.astype(jnp.float32) + b.astype(jnp.float32)).astype(x.dtype)


def rms_norm(x, g):
    xf = x.astype(jnp.float32)
    return xf * lax.rsqrt(jnp.mean(jnp.square(xf), axis=-1, keepdims=True) + RMS_EPS) * g.astype(jnp.float32)


def hgrn2_mixer(x, w_in, lb, norm_gain, w_out):
    B, S, _ = x.shape
    f32 = jnp.float32
    proj = x @ w_in
    q = proj[..., :HG_WIDTH]
    f = proj[..., HG_WIDTH:2 * HG_WIDTH].astype(f32)
    v = proj[..., 2 * HG_WIDTH:2 * HG_WIDTH + HG_VWIDTH].astype(f32)
    g = proj[..., 2 * HG_WIDTH + HG_VWIDTH:]
    lb = lb.astype(f32)
    forget = lb + (1.0 - lb) * jax.nn.sigmoid(f)
    log_f = jnp.log(forget)
    k = (1.0 - lb) * jax.nn.sigmoid(-f)
    q = jax.nn.silu(q.astype(f32)) * (HG_DK ** -0.5)
    nc = S // HG_CHUNK

    def to_chunks(t, d):
        return t.reshape(B, nc, HG_CHUNK, HG_HEADS, d).transpose(1, 0, 3, 2, 4)

    causal = jnp.tril(jnp.ones((HG_CHUNK, HG_CHUNK), dtype=bool))[:, :, None]

    def step(state, inp):
        qc, kc, gc, vc = inp
        b = jnp.cumsum(gc, axis=2)
        diff = b[:, :, :, None, :] - b[:, :, None, :, :]
        decay = jnp.exp(jnp.where(causal, diff, -jnp.inf))
        scores = jnp.einsum('bhtc,bhsc,bhtsc->bhts', qc, kc, decay)
        o = jnp.einsum('bhts,bhsv->bhtv', scores, vc) + jnp.einsum('bhtc,bhcv->bhtv', qc * jnp.exp(b), state)
        b_last = b[:, :, -1:, :]
        state = jnp.exp(b_last[:, :, 0, :, None]) * state + jnp.einsum('bhsc,bhsv->bhcv', kc * jnp.exp(b_last - b), vc)
        return state, o

    s0 = jnp.zeros((B, HG_HEADS, HG_DK, HG_DV), f32)
    _, o = lax.scan(step, s0, (to_chunks(q, HG_DK), to_chunks(k, HG_DK), to_chunks(log_f, HG_DK), to_chunks(v, HG_DV)))
    o = o.transpose(1, 0, 3, 2, 4).reshape(B, S, HG_HEADS, HG_DV)
    o = rms_norm(o, norm_gain) * jax.nn.silu(g.astype(f32)).reshape(B, S, HG_HEADS, HG_DV)
    return o.reshape(B, S, HG_VWIDTH).astype(x.dtype) @ w_out


def shared_kv(h, kv_w, kv_b):
    B, S, _ = h.shape
    kv = h @ kv_w + kv_b
    kdim = ATT_KVH * ATT_HD
    return kv[..., :kdim].reshape(B, S, ATT_KVH, ATT_HD), kv[..., kdim:].reshape(B, S, ATT_KVH, ATT_HD)


def band_blocks(t):
    B, S = t.shape[:2]
    tp = jnp.pad(t, ((0, 0), (WINDOW, 0), (0, 0), (0, 0)))
    tr = tp.reshape(B, S // WINDOW + 1, WINDOW, ATT_KVH, ATT_HD)
    return jnp.concatenate([tr[:, :-1], tr[:, 1:]], axis=2)


def alibi_slopes(n_heads):
    return jnp.exp2(-8.0 * jnp.arange(1, n_heads + 1, dtype=jnp.float32) / n_heads)


def swa_sink_mixer(x, k_blocks, v_blocks, w_q, b_q, sinks, w_out, b_out):
    B, S, _ = x.shape
    nb = S // WINDOW
    f32 = jnp.float32
    q = (x @ w_q + b_q).reshape(B, nb, WINDOW, ATT_KVH, ATT_G, ATT_HD)
    s = jnp.einsum('bnqkgd,bnskd->bnkgqs', q, k_blocks).astype(f32) * (ATT_HD ** -0.5)
    qi = jnp.arange(WINDOW)[:, None]
    si = jnp.arange(2 * WINDOW)[None, :]
    dist = qi - si + WINDOW
    blk = jnp.arange(nb)[:, None, None]
    valid = (dist >= 0) & (dist < WINDOW) & (blk * WINDOW - WINDOW + si >= 0)
    slopes = alibi_slopes(ATT_QH).reshape(ATT_KVH, ATT_G)
    s = s - slopes[:, :, None, None] * dist.astype(f32)
    s = jnp.where(valid[None, :, None, None], s, -jnp.inf)
    sink = jnp.broadcast_to(sinks.astype(f32).reshape(ATT_KVH, ATT_G)[None, None, :, :, None, None], s.shape[:-1] + (1,))
    probs = jax.nn.softmax(jnp.concatenate([s, sink], axis=-1), axis=-1)[..., :-1]
    o = jnp.einsum('bnkgqs,bnskd->bnqkgd', probs.astype(v_blocks.dtype), v_blocks).reshape(B, S, ATT_QH * ATT_HD)
    return o @ w_out + b_out


def swiglu(x, w_gate_up, w_down):
    gu = x @ w_gate_up
    return (jax.nn.silu(gu[..., :FFN_HIDDEN]) * gu[..., FFN_HIDDEN:]) @ w_down


def per_layer_embedding(x, p_i, w_up, w_gate, b_gate):
    return jax.nn.sigmoid(x @ w_gate + b_gate) * (p_i @ w_up)


def _fwd_setup_inputs(seed: int = 0) -> dict:
    key = jax.random.key(seed)
    ks = jax.random.split(key, 24)
    D = D_MODEL
    nrm = lambda k, shape, scale: jax.random.normal(k, shape, jnp.float32) * scale
    kvd = ATT_KVH * ATT_HD
    qd = ATT_QH * ATT_HD
    return {
        'x': nrm(ks[0], (BATCH, SEQ, D), 1.0),
        'p': nrm(ks[1], (DEPTH, BATCH, SEQ, PLE_DIM), 1.0),
        'a_w_in': jnp.concatenate([
            nrm(ks[2], (N_A, D, 2 * HG_WIDTH), D ** -0.5),
            nrm(ks[3], (N_A, D, HG_VWIDTH), D ** -0.5 * DEEPNORM_BETA),
            nrm(ks[4], (N_A, D, HG_VWIDTH), D ** -0.5)], axis=-1),
        'a_lower_bound': nrm(ks[5], (N_A + 1, HG_WIDTH), 0.5),
        'a_norm_gain': 1.0 + nrm(ks[6], (N_A, HG_DV), 0.05),
        'a_w_out': nrm(ks[7], (N_A, HG_VWIDTH, D), HG_VWIDTH ** -0.5 * DEEPNORM_BETA),
        'kv_w': jnp.concatenate([
            nrm(ks[8], (D, kvd), D ** -0.5),
            nrm(ks[9], (D, kvd), D ** -0.5 * DEEPNORM_BETA)], axis=-1),
        'kv_b': nrm(ks[10], (2 * kvd,), 0.02),
        'b_w_q': nrm(ks[11], (N_B, D, qd), D ** -0.5),
        'b_b_q': nrm(ks[12], (N_B, qd), 0.02),
        'b_sinks': nrm(ks[13], (N_B, ATT_QH), 1.0),
        'b_w_out': nrm(ks[14], (N_B, qd, D), qd ** -0.5 * DEEPNORM_BETA),
        'b_b_out': nrm(ks[15], (N_B, D), 0.02),
        'ffn_w_gate_up': nrm(ks[16], (DEPTH, D, 2 * FFN_HIDDEN), D ** -0.5),
        'ffn_w_down': nrm(ks[17], (DEPTH, FFN_HIDDEN, D), FFN_HIDDEN ** -0.5 * DEEPNORM_BETA),
        'ple_w_up': nrm(ks[18], (DEPTH, PLE_DIM, D), PLE_DIM ** -0.5 * DEEPNORM_BETA),
        'ple_w_gate': nrm(ks[19], (DEPTH, D, D), D ** -0.5),
        'ple_b_gate': nrm(ks[20], (DEPTH, D), 0.02),
        'ln_gain': 1.0 + nrm(ks[21], (DEPTH, 3, D), 0.05),
        'ln_bias': nrm(ks[22], (DEPTH, 3, D), 0.02),
    }


def _fwd_reference(x, p, a_w_in, a_lower_bound, a_norm_gain, a_w_out, kv_w, kv_b, b_w_q, b_b_q, b_sinks, b_w_out, b_b_out,
              ffn_w_gate_up, ffn_w_down, ple_w_up, ple_w_gate, ple_b_gate, ln_gain, ln_bias):
    lower_bounds = jnp.cumsum(jax.nn.softmax(a_lower_bound.astype(jnp.float32), axis=0), axis=0)
    k_blocks = None
    v_blocks = None
    for i in range(DEPTH):
        if i < N_A:
            h = hgrn2_mixer(x, a_w_in[i], lower_bounds[i], a_norm_gain[i], a_w_out[i])
        else:
            if i == N_A:
                k_sh, v_sh = shared_kv(x, kv_w, kv_b)
                k_blocks = band_blocks(k_sh)
                v_blocks = band_blocks(v_sh)
            j = i - N_A
            h = swa_sink_mixer(x, k_blocks, v_blocks, b_w_q[j], b_b_q[j], b_sinks[j], b_w_out[j], b_b_out[j])
        x = layer_norm(DEEPNORM_ALPHA * x + h, ln_gain[i, 0], ln_bias[i, 0])
        x = layer_norm(DEEPNORM_ALPHA * x + swiglu(x, ffn_w_gate_up[i], ffn_w_down[i]), ln_gain[i, 1], ln_bias[i, 1])
        x = layer_norm(DEEPNORM_ALPHA * x + per_layer_embedding(x, p[i], ple_w_up[i], ple_w_gate[i], ple_b_gate[i]),
                       ln_gain[i, 2], ln_bias[i, 2])
    return x


import jax as _jax
import jax.numpy as _jnp

TWIN_FORMAT = 'train_step'
FWD_PARAMS = ['x', 'p', 'a_w_in', 'a_lower_bound', 'a_norm_gain', 'a_w_out', 'kv_w', 'kv_b', 'b_w_q', 'b_b_q', 'b_sinks', 'b_w_out', 'b_b_out', 'ffn_w_gate_up', 'ffn_w_down', 'ple_w_up', 'ple_w_gate', 'ple_b_gate', 'ln_gain', 'ln_bias']
TWIN_WEIGHTS = ['a_w_in', 'a_lower_bound', 'a_norm_gain', 'a_w_out', 'kv_w', 'kv_b', 'b_w_q', 'b_b_q', 'b_sinks', 'b_w_out', 'b_b_out', 'ffn_w_gate_up', 'ffn_w_down', 'ple_w_up', 'ple_w_gate', 'ple_b_gate', 'ln_gain', 'ln_bias']
TWIN_DIFF_INPUT = 'x'
TWIN_INPUTS = ['x', 'p', 'a_w_in', 'a_lower_bound', 'a_norm_gain', 'a_w_out', 'kv_w', 'kv_b', 'b_w_q', 'b_b_q', 'b_sinks', 'b_w_out', 'b_b_out', 'ffn_w_gate_up', 'ffn_w_down', 'ple_w_up', 'ple_w_gate', 'ple_b_gate', 'ln_gain', 'ln_bias', 'loss_target', 'm_a_w_in', 'm_a_lower_bound', 'm_a_norm_gain', 'm_a_w_out', 'm_kv_w', 'm_kv_b', 'm_b_w_q', 'm_b_b_q', 'm_b_sinks', 'm_b_w_out', 'm_b_b_out', 'm_ffn_w_gate_up', 'm_ffn_w_down', 'm_ple_w_up', 'm_ple_w_gate', 'm_ple_b_gate', 'm_ln_gain', 'm_ln_bias', 'v_a_w_in', 'v_a_lower_bound', 'v_a_norm_gain', 'v_a_w_out', 'v_kv_w', 'v_kv_b', 'v_b_w_q', 'v_b_b_q', 'v_b_sinks', 'v_b_w_out', 'v_b_b_out', 'v_ffn_w_gate_up', 'v_ffn_w_down', 'v_ple_w_up', 'v_ple_w_gate', 'v_ple_b_gate', 'v_ln_gain', 'v_ln_bias']
TWIN_OUTPUTS = ['loss', 'grad_x', 'grad_a_w_in', 'grad_a_lower_bound', 'grad_a_norm_gain', 'grad_a_w_out', 'grad_kv_w', 'grad_kv_b', 'grad_b_w_q', 'grad_b_b_q', 'grad_b_sinks', 'grad_b_w_out', 'grad_b_b_out', 'grad_ffn_w_gate_up', 'grad_ffn_w_down', 'grad_ple_w_up', 'grad_ple_w_gate', 'grad_ple_b_gate', 'grad_ln_gain', 'grad_ln_bias', 'delta_a_w_in', 'delta_a_lower_bound', 'delta_a_norm_gain', 'delta_a_w_out', 'delta_kv_w', 'delta_kv_b', 'delta_b_w_q', 'delta_b_b_q', 'delta_b_sinks', 'delta_b_w_out', 'delta_b_b_out', 'delta_ffn_w_gate_up', 'delta_ffn_w_down', 'delta_ple_w_up', 'delta_ple_w_gate', 'delta_ple_b_gate', 'delta_ln_gain', 'delta_ln_bias', 'new_m_a_w_in', 'new_m_a_lower_bound', 'new_m_a_norm_gain', 'new_m_a_w_out', 'new_m_kv_w', 'new_m_kv_b', 'new_m_b_w_q', 'new_m_b_b_q', 'new_m_b_sinks', 'new_m_b_w_out', 'new_m_b_b_out', 'new_m_ffn_w_gate_up', 'new_m_ffn_w_down', 'new_m_ple_w_up', 'new_m_ple_w_gate', 'new_m_ple_b_gate', 'new_m_ln_gain', 'new_m_ln_bias', 'new_v_a_w_in', 'new_v_a_lower_bound', 'new_v_a_norm_gain', 'new_v_a_w_out', 'new_v_kv_w', 'new_v_kv_b', 'new_v_b_w_q', 'new_v_b_b_q', 'new_v_b_sinks', 'new_v_b_w_out', 'new_v_b_b_out', 'new_v_ffn_w_gate_up', 'new_v_ffn_w_down', 'new_v_ple_w_up', 'new_v_ple_w_gate', 'new_v_ple_b_gate', 'new_v_ln_gain', 'new_v_ln_bias']
TWIN_LEAF_KINDS = {'loss': 'loss', 'grad_x': 'grad_x', 'grad_a_w_in': 'grad_w', 'grad_a_lower_bound': 'grad_w', 'grad_a_norm_gain': 'grad_w', 'grad_a_w_out': 'grad_w', 'grad_kv_w': 'grad_w', 'grad_kv_b': 'grad_w', 'grad_b_w_q': 'grad_w', 'grad_b_b_q': 'grad_w', 'grad_b_sinks': 'grad_w', 'grad_b_w_out': 'grad_w', 'grad_b_b_out': 'grad_w', 'grad_ffn_w_gate_up': 'grad_w', 'grad_ffn_w_down': 'grad_w', 'grad_ple_w_up': 'grad_w', 'grad_ple_w_gate': 'grad_w', 'grad_ple_b_gate': 'grad_w', 'grad_ln_gain': 'grad_w', 'grad_ln_bias': 'grad_w', 'delta_a_w_in': 'delta_w', 'delta_a_lower_bound': 'delta_w', 'delta_a_norm_gain': 'delta_w', 'delta_a_w_out': 'delta_w', 'delta_kv_w': 'delta_w', 'delta_kv_b': 'delta_w', 'delta_b_w_q': 'delta_w', 'delta_b_b_q': 'delta_w', 'delta_b_sinks': 'delta_w', 'delta_b_w_out': 'delta_w', 'delta_b_b_out': 'delta_w', 'delta_ffn_w_gate_up': 'delta_w', 'delta_ffn_w_down': 'delta_w', 'delta_ple_w_up': 'delta_w', 'delta_ple_w_gate': 'delta_w', 'delta_ple_b_gate': 'delta_w', 'delta_ln_gain': 'delta_w', 'delta_ln_bias': 'delta_w', 'new_m_a_w_in': 'new_m', 'new_m_a_lower_bound': 'new_m', 'new_m_a_norm_gain': 'new_m', 'new_m_a_w_out': 'new_m', 'new_m_kv_w': 'new_m', 'new_m_kv_b': 'new_m', 'new_m_b_w_q': 'new_m', 'new_m_b_b_q': 'new_m', 'new_m_b_sinks': 'new_m', 'new_m_b_w_out': 'new_m', 'new_m_b_b_out': 'new_m', 'new_m_ffn_w_gate_up': 'new_m', 'new_m_ffn_w_down': 'new_m', 'new_m_ple_w_up': 'new_m', 'new_m_ple_w_gate': 'new_m', 'new_m_ple_b_gate': 'new_m', 'new_m_ln_gain': 'new_m', 'new_m_ln_bias': 'new_m', 'new_v_a_w_in': 'new_v', 'new_v_a_lower_bound': 'new_v', 'new_v_a_norm_gain': 'new_v', 'new_v_a_w_out': 'new_v', 'new_v_kv_w': 'new_v', 'new_v_kv_b': 'new_v', 'new_v_b_w_q': 'new_v', 'new_v_b_b_q': 'new_v', 'new_v_b_sinks': 'new_v', 'new_v_b_w_out': 'new_v', 'new_v_b_b_out': 'new_v', 'new_v_ffn_w_gate_up': 'new_v', 'new_v_ffn_w_down': 'new_v', 'new_v_ple_w_up': 'new_v', 'new_v_ple_w_gate': 'new_v', 'new_v_ple_b_gate': 'new_v', 'new_v_ln_gain': 'new_v', 'new_v_ln_bias': 'new_v'}


def _forward(args):
    return _fwd_reference(*[args[k] for k in FWD_PARAMS])


def _output_shape():
    out = _jax.eval_shape(lambda: _forward(_fwd_setup_inputs(0)))
    return out.shape, out.dtype

N_MICROBATCH = 1
ADAM_LR = 0.001
ADAM_B1 = 0.9
ADAM_B2 = 0.999
ADAM_EPS = 1e-08
ADAM_WD = 0.01
ADAM_STEP = 10
PER_EXAMPLE_BATCH_AXIS = {'x': 0, 'p': 1, 'loss_target': 0}
SHARED_INPUTS = []
_WEIGHT_DTYPES = {'a_w_in': _jnp.float32, 'a_lower_bound': _jnp.float32, 'a_norm_gain': _jnp.float32, 'a_w_out': _jnp.float32, 'kv_w': _jnp.float32, 'kv_b': _jnp.float32, 'b_w_q': _jnp.float32, 'b_b_q': _jnp.float32, 'b_sinks': _jnp.float32, 'b_w_out': _jnp.float32, 'b_b_out': _jnp.float32, 'ffn_w_gate_up': _jnp.float32, 'ffn_w_down': _jnp.float32, 'ple_w_up': _jnp.float32, 'ple_w_gate': _jnp.float32, 'ple_b_gate': _jnp.float32, 'ln_gain': _jnp.float32, 'ln_bias': _jnp.float32}
MOMENT_SCALE = {'a_w_in': 3.002075e-02, 'a_lower_bound': 2.357868e-03, 'a_norm_gain': 8.294808e-02, 'a_w_out': 5.298442e-02, 'kv_w': 2.017332e-02, 'kv_b': 1.030064e-01, 'b_w_q': 5.074232e-03, 'b_b_q': 4.943315e-03, 'b_sinks': 1.244502e-02, 'b_w_out': 1.340316e-02, 'b_b_out': 1.924036e-01, 'ffn_w_gate_up': 1.581740e-02, 'ffn_w_down': 5.175542e-02, 'ple_w_up': 4.771340e-02, 'ple_w_gate': 9.284208e-03, 'ple_b_gate': 1.791043e-02, 'ln_gain': 6.825247e+00, 'ln_bias': 3.823721e-01}


def _to_microbatches(a, axis):
    t = _jnp.moveaxis(a, axis, 0)
    t = t.reshape((N_MICROBATCH, t.shape[0] // N_MICROBATCH) + t.shape[1:])
    return _jnp.moveaxis(t, 1, axis + 1)


def setup_inputs(seed: int = 0) -> dict:
    inp = _fwd_setup_inputs(seed)
    key = _jax.random.fold_in(_jax.random.key(seed), 7919)
    shape, _ = _output_shape()
    out = dict(inp)
    out["loss_target"] = _jax.random.normal(_jax.random.fold_in(key, 0), shape, _jnp.float32)
    for i, name in enumerate(TWIN_WEIGHTS):
        w = inp[name].astype(_jnp.float32)
        if MOMENT_SCALE is None:
            s = _jnp.sqrt(_jnp.mean(_jnp.square(w)) + 1e-30)
        else:
            s = MOMENT_SCALE[name]
        km, kv = _jax.random.split(_jax.random.fold_in(key, i + 1))
        out[name] = w
        out["m_" + name] = s * _jax.random.normal(km, w.shape, _jnp.float32)
        out["v_" + name] = (s * s) * _jax.random.uniform(kv, w.shape, _jnp.float32, 0.5, 1.5)
    if N_MICROBATCH > 1:
        for name, axis in PER_EXAMPLE_BATCH_AXIS.items():
            out[name] = _to_microbatches(out[name], axis)
    return {'x': out['x'], 'p': out['p'], 'a_w_in': out['a_w_in'], 'a_lower_bound': out['a_lower_bound'], 'a_norm_gain': out['a_norm_gain'], 'a_w_out': out['a_w_out'], 'kv_w': out['kv_w'], 'kv_b': out['kv_b'], 'b_w_q': out['b_w_q'], 'b_b_q': out['b_b_q'], 'b_sinks': out['b_sinks'], 'b_w_out': out['b_w_out'], 'b_b_out': out['b_b_out'], 'ffn_w_gate_up': out['ffn_w_gate_up'], 'ffn_w_down': out['ffn_w_down'], 'ple_w_up': out['ple_w_up'], 'ple_w_gate': out['ple_w_gate'], 'ple_b_gate': out['ple_b_gate'], 'ln_gain': out['ln_gain'], 'ln_bias': out['ln_bias'], 'loss_target': out['loss_target'], 'm_a_w_in': out['m_a_w_in'], 'm_a_lower_bound': out['m_a_lower_bound'], 'm_a_norm_gain': out['m_a_norm_gain'], 'm_a_w_out': out['m_a_w_out'], 'm_kv_w': out['m_kv_w'], 'm_kv_b': out['m_kv_b'], 'm_b_w_q': out['m_b_w_q'], 'm_b_b_q': out['m_b_b_q'], 'm_b_sinks': out['m_b_sinks'], 'm_b_w_out': out['m_b_w_out'], 'm_b_b_out': out['m_b_b_out'], 'm_ffn_w_gate_up': out['m_ffn_w_gate_up'], 'm_ffn_w_down': out['m_ffn_w_down'], 'm_ple_w_up': out['m_ple_w_up'], 'm_ple_w_gate': out['m_ple_w_gate'], 'm_ple_b_gate': out['m_ple_b_gate'], 'm_ln_gain': out['m_ln_gain'], 'm_ln_bias': out['m_ln_bias'], 'v_a_w_in': out['v_a_w_in'], 'v_a_lower_bound': out['v_a_lower_bound'], 'v_a_norm_gain': out['v_a_norm_gain'], 'v_a_w_out': out['v_a_w_out'], 'v_kv_w': out['v_kv_w'], 'v_kv_b': out['v_kv_b'], 'v_b_w_q': out['v_b_w_q'], 'v_b_b_q': out['v_b_b_q'], 'v_b_sinks': out['v_b_sinks'], 'v_b_w_out': out['v_b_w_out'], 'v_b_b_out': out['v_b_b_out'], 'v_ffn_w_gate_up': out['v_ffn_w_gate_up'], 'v_ffn_w_down': out['v_ffn_w_down'], 'v_ple_w_up': out['v_ple_w_up'], 'v_ple_w_gate': out['v_ple_w_gate'], 'v_ple_b_gate': out['v_ple_b_gate'], 'v_ln_gain': out['v_ln_gain'], 'v_ln_bias': out['v_ln_bias']}


def _loss(weights, diff, rest, loss_target):
    with _jax.named_scope("forward"):
        args = {**rest, TWIN_DIFF_INPUT: diff, **{k: w.astype(_WEIGHT_DTYPES[k]) for k, w in weights.items()}}
        y = _forward(args)
    with _jax.named_scope("loss_head"):
        err = _jnp.square(y.astype(_jnp.float32) - loss_target)
        return 0.5 * _jnp.sum(_jnp.mean(err, axis=-1)) if err.ndim else 0.5 * err


def _adamw(w, g, m, v):
    m = ADAM_B1 * m + (1.0 - ADAM_B1) * g
    v = ADAM_B2 * v + (1.0 - ADAM_B2) * _jnp.square(g)
    m_hat = m / (1.0 - ADAM_B1 ** ADAM_STEP)
    v_hat = v / (1.0 - ADAM_B2 ** ADAM_STEP)
    delta = -ADAM_LR * (m_hat / (_jnp.sqrt(v_hat) + ADAM_EPS) + ADAM_WD * w)
    return delta, m, v


def reference(x, p, a_w_in, a_lower_bound, a_norm_gain, a_w_out, kv_w, kv_b, b_w_q, b_b_q, b_sinks, b_w_out, b_b_out, ffn_w_gate_up, ffn_w_down, ple_w_up, ple_w_gate, ple_b_gate, ln_gain, ln_bias, loss_target, m_a_w_in, m_a_lower_bound, m_a_norm_gain, m_a_w_out, m_kv_w, m_kv_b, m_b_w_q, m_b_b_q, m_b_sinks, m_b_w_out, m_b_b_out, m_ffn_w_gate_up, m_ffn_w_down, m_ple_w_up, m_ple_w_gate, m_ple_b_gate, m_ln_gain, m_ln_bias, v_a_w_in, v_a_lower_bound, v_a_norm_gain, v_a_w_out, v_kv_w, v_kv_b, v_b_w_q, v_b_b_q, v_b_sinks, v_b_w_out, v_b_b_out, v_ffn_w_gate_up, v_ffn_w_down, v_ple_w_up, v_ple_w_gate, v_ple_b_gate, v_ln_gain, v_ln_bias):
    given = dict(x=x, p=p, a_w_in=a_w_in, a_lower_bound=a_lower_bound, a_norm_gain=a_norm_gain, a_w_out=a_w_out, kv_w=kv_w, kv_b=kv_b, b_w_q=b_w_q, b_b_q=b_b_q, b_sinks=b_sinks, b_w_out=b_w_out, b_b_out=b_b_out, ffn_w_gate_up=ffn_w_gate_up, ffn_w_down=ffn_w_down, ple_w_up=ple_w_up, ple_w_gate=ple_w_gate, ple_b_gate=ple_b_gate, ln_gain=ln_gain, ln_bias=ln_bias, loss_target=loss_target, m_a_w_in=m_a_w_in, m_a_lower_bound=m_a_lower_bound, m_a_norm_gain=m_a_norm_gain, m_a_w_out=m_a_w_out, m_kv_w=m_kv_w, m_kv_b=m_kv_b, m_b_w_q=m_b_w_q, m_b_b_q=m_b_b_q, m_b_sinks=m_b_sinks, m_b_w_out=m_b_w_out, m_b_b_out=m_b_b_out, m_ffn_w_gate_up=m_ffn_w_gate_up, m_ffn_w_down=m_ffn_w_down, m_ple_w_up=m_ple_w_up, m_ple_w_gate=m_ple_w_gate, m_ple_b_gate=m_ple_b_gate, m_ln_gain=m_ln_gain, m_ln_bias=m_ln_bias, v_a_w_in=v_a_w_in, v_a_lower_bound=v_a_lower_bound, v_a_norm_gain=v_a_norm_gain, v_a_w_out=v_a_w_out, v_kv_w=v_kv_w, v_kv_b=v_kv_b, v_b_w_q=v_b_w_q, v_b_b_q=v_b_b_q, v_b_sinks=v_b_sinks, v_b_w_out=v_b_w_out, v_b_b_out=v_b_b_out, v_ffn_w_gate_up=v_ffn_w_gate_up, v_ffn_w_down=v_ffn_w_down, v_ple_w_up=v_ple_w_up, v_ple_w_gate=v_ple_w_gate, v_ple_b_gate=v_ple_b_gate, v_ln_gain=v_ln_gain, v_ln_bias=v_ln_bias)
    weights = {n: given[n] for n in TWIN_WEIGHTS}
    shared = {n: given[n] for n in SHARED_INPUTS}
    per_example = {n: given[n] for n in ['x', 'p']}
    grad_fn = _jax.value_and_grad(_loss, argnums=(0, 1))

    def one_microbatch(ex, loss_target):
        ex = dict(ex)
        diff = ex.pop(TWIN_DIFF_INPUT)
        return grad_fn(weights, diff, {**shared, **ex}, loss_target)

    if N_MICROBATCH == 1:
        loss, (grad_w, grad_x) = one_microbatch(per_example, given["loss_target"])
    else:
        def body(carry, xs):
            loss_sum, grad_sum = carry
            l_k, (gw_k, gx_k) = one_microbatch(xs[0], xs[1])
            with _jax.named_scope("update"):
                return (loss_sum + l_k, _jax.tree.map(_jnp.add, grad_sum, gw_k)), gx_k

        init = (_jnp.zeros((), _jnp.float32), _jax.tree.map(_jnp.zeros_like, weights))
        (loss, grad_w), grad_x = _jax.lax.scan(body, init, (per_example, given["loss_target"]))
    with _jax.named_scope("update"):
        delta_w, new_m, new_v = {}, {}, {}
        for n in TWIN_WEIGHTS:
            delta_w[n], new_m[n], new_v[n] = _adamw(weights[n], grad_w[n], given["m_" + n], given["v_" + n])
    return (loss, grad_x, *[grad_w[n] for n in TWIN_WEIGHTS], *[delta_w[n] for n in TWIN_WEIGHTS],
            *[new_m[n] for n in TWIN_WEIGHTS], *[new_v[n] for n in TWIN_WEIGHTS])
```

```python
import functools

import jax
import jax.numpy as jnp
from jax import lax
from jax.experimental import pallas as pl
from jax.experimental.pallas import tpu as pltpu

F32 = jnp.float32
BF16 = jnp.bfloat16
MESH = pl.DeviceIdType.MESH

LANES = 128
HG_DK = 128
HG_CHUNK = 64
HG_SUB = 16
HG_ROWS = 512
ATT_HD = 64
ATT_G = 4
WINDOW = 128
DEPTH = 2
ALPHA = (2.0 * DEPTH) ** 0.25
LN_EPS = 1e-5
RMS_EPS = 1e-6
ADAM_LR, ADAM_B1, ADAM_B2, ADAM_EPS, ADAM_WD, ADAM_STEP = 0.001, 0.9, 0.999, 1e-08, 0.01, 10
N_CHIPS = 4
N_DEV = 8
VMEM_LIMIT = 56 * 1024 * 1024
NEG = -1e30


def _pick(n, cap):
    best = None
    for d in range(LANES, min(n, cap) + 1, LANES):
        if n % d == 0:
            best = d
    return n if best is None else best


def _pick_rows(m, cap):
    best = None
    for d in range(16, min(m, cap) + 1, 16):
        if m % d == 0:
            best = d
    return m if best is None else best


def _params(sem):
    return pltpu.CompilerParams(dimension_semantics=sem, vmem_limit_bytes=VMEM_LIMIT)


def _mm(a, b, *, name, la=None, lb=None, ta=False, tb=False, bias=None, add=None, out_dtype=F32,
        out_layers=None, out_layer=None, out_into=None, caps=(512, 1536, 2048)):
    ar, ac = a.shape[-2:]
    br, bc = b.shape[-2:]
    m, k = (ac, ar) if ta else (ar, ac)
    k2, n = (bc, br) if tb else (br, bc)
    assert k == k2, (a.shape, b.shape, ta, tb)
    tm, tn, tk = _pick(m, caps[0]), _pick(n, caps[1]), _pick(k, caps[2])
    nk = k // tk
    grid = (m // tm, n // tn, nk)

    def spec(block, idx, layer):
        if layer is None:
            return pl.BlockSpec(block, idx)
        return pl.BlockSpec((None,) + block, lambda i, j, kk: (layer,) + idx(i, j, kk))

    a_spec = spec((tk, tm), lambda i, j, kk: (kk, i), la) if ta else spec((tm, tk), lambda i, j, kk: (i, kk), la)
    b_spec = spec((tn, tk), lambda i, j, kk: (j, kk), lb) if tb else spec((tk, tn), lambda i, j, kk: (kk, j), lb)
    in_specs, operands = [a_spec, b_spec], [a, b]
    if bias is not None:
        in_specs.append(pl.BlockSpec((1, tn), lambda i, j, kk: (0, j)))
        operands.append(bias)
    if add is not None:
        in_specs.append(pl.BlockSpec((tm, tn), lambda i, j, kk: (i, j)))
        operands.append(add)
    aliases = {}
    if out_into is not None:
        in_specs.append(pl.BlockSpec(memory_space=pl.ANY))
        aliases = {len(operands): 0}
        operands.append(out_into)
    if out_layers is None:
        out_shape = jax.ShapeDtypeStruct((m, n), out_dtype)
    else:
        out_shape = jax.ShapeDtypeStruct((out_layers, m, n), out_dtype)
    out_spec = spec((tm, tn), lambda i, j, kk: (i, j), out_layer)
    dims = (((0 if ta else 1,), (1 if tb else 0,)), ((), ()))
    has_bias, has_add, has_alias = bias is not None, add is not None, out_into is not None

    def body(*refs):
        a_ref, b_ref = refs[0], refs[1]
        pos = 2
        bias_ref = add_ref = None
        if has_bias:
            bias_ref = refs[pos]
            pos += 1
        if has_add:
            add_ref = refs[pos]
            pos += 1
        if has_alias:
            pos += 1
        o_ref = refs[pos]
        acc_ref = refs[pos + 1] if nk > 1 else None
        part = lax.dot_general(a_ref[...].astype(BF16), b_ref[...].astype(BF16), dims, preferred_element_type=F32)

        def finish(total):
            if has_bias:
                total = total + bias_ref[...]
            if has_add:
                total = total + add_ref[...]
            o_ref[...] = total.astype(o_ref.dtype)

        if nk == 1:
            finish(part)
        else:
            kk = pl.program_id(2)

            @pl.when(kk == 0)
            def _():
                acc_ref[...] = part

            @pl.when(kk > 0)
            def _():
                acc_ref[...] += part

            @pl.when(kk == nk - 1)
            def _():
                finish(acc_ref[...])

    return pl.pallas_call(
        body, name=name, grid=grid, in_specs=in_specs, out_specs=out_spec, out_shape=out_shape,
        scratch_shapes=[pltpu.VMEM((tm, tn), F32)] if nk > 1 else [],
        input_output_aliases=aliases,
        compiler_params=_params(("parallel", "parallel", "arbitrary")),
    )(*operands)


def _rowwise(fn, rows, whole, outs, sums=(), *, name, tm=256):
    m = rows[0].shape[-2]
    tm = _pick_rows(m, tm)
    n_rows, n_whole, n_outs, n_sums = len(rows), len(whole), len(outs), len(sums)

    def rspec(shape):
        lead = len(shape) - 2
        return pl.BlockSpec(tuple(shape[:-2]) + (tm, shape[-1]), lambda i: (0,) * lead + (i, 0))

    def wspec(shape):
        return pl.BlockSpec(tuple(shape), lambda i: (0,) * len(shape))

    def body(*refs):
        vals = [r[...] for r in refs[:n_rows + n_whole]]
        out_refs = refs[n_rows + n_whole:n_rows + n_whole + n_outs]
        sum_refs = refs[n_rows + n_whole + n_outs:]
        res = fn(*vals)
        for ref, val in zip(out_refs, res[:n_outs]):
            ref[...] = val.astype(ref.dtype)
        if n_sums:
            @pl.when(pl.program_id(0) == 0)
            def _():
                for ref in sum_refs:
                    ref[...] = jnp.zeros(ref.shape, ref.dtype)

            for ref, val in zip(sum_refs, res[n_outs:]):
                ref[...] += val

    result = pl.pallas_call(
        body, name=name, grid=(m // tm,),
        in_specs=[rspec(r.shape) for r in rows] + [wspec(w.shape) for w in whole],
        out_specs=[rspec(s) for s, _ in outs] + [wspec(s) for s, _ in sums],
        out_shape=[jax.ShapeDtypeStruct(s, d) for s, d in list(outs) + list(sums)],
        compiler_params=_params(("arbitrary",)),
    )(*rows, *whole)
    return result


def _sigmoid(v):
    return jax.nn.sigmoid(v)


def _col_sum(v):
    return jnp.sum(v, axis=0, keepdims=True)


def _ln_stats(z):
    mu = jnp.mean(z, axis=-1, keepdims=True)
    zc = z - mu
    var = jnp.mean(zc * zc, axis=-1, keepdims=True)
    rstd = lax.rsqrt(var + LN_EPS)
    return zc * rstd, rstd


def _ln_fwd_fn(xin, h, gain, bias):
    xhat, _ = _ln_stats(ALPHA * xin + h)
    y = xhat * gain + bias
    return y, y


def _ple_ln_fwd_fn(xin, pg, pu, gain, bias):
    xhat, _ = _ln_stats(ALPHA * xin + _sigmoid(pg) * pu)
    y = xhat * gain + bias
    return y, y


def _ln_dz(dy, z, gain):
    xhat, rstd = _ln_stats(z)
    dxhat = dy * gain
    dz = rstd * (dxhat - jnp.mean(dxhat, axis=-1, keepdims=True)
                 - xhat * jnp.mean(dxhat * xhat, axis=-1, keepdims=True))
    return dz, _col_sum(dy * xhat), _col_sum(dy)


def _ln_bwd_fn(dy, xin, h, gain):
    dz, dgain, dbias = _ln_dz(dy, ALPHA * xin + h, gain)
    return ALPHA * dz, dz, dgain, dbias, _col_sum(dz)


def _ple_ln_bwd_fn(dy, xin, pg, pu, gain):
    sg = _sigmoid(pg)
    dz, dgain, dbias = _ln_dz(dy, ALPHA * xin + sg * pu, gain)
    dpg = dz * pu * sg * (1.0 - sg)
    return ALPHA * dz, dpg, dz * sg, dgain, dbias, _col_sum(dpg)


def _swiglu_fwd_fn(gu):
    hid = gu.shape[-1] // 2
    gate, up = gu[:, :hid], gu[:, hid:]
    return (gate * _sigmoid(gate) * up,)


def _swiglu_bwd_fn(gu, dact):
    hid = gu.shape[-1] // 2
    gate, up = gu[:, :hid], gu[:, hid:]
    sg = _sigmoid(gate)
    dgate = dact * up * sg * (1.0 + gate * (1.0 - sg))
    dup = dact * gate * sg
    return (jnp.concatenate([dgate, dup], axis=-1),)


def _loss_fn(y, target):
    err = y - target
    inv = 1.0 / y.shape[-1]
    part = 0.5 * inv * jnp.sum(jnp.sum(err * err, axis=-1, keepdims=True), axis=0, keepdims=True)
    return err * inv, jnp.broadcast_to(part, (1, LANES))


def _adam_fn(w, mom, vel, p_own, p_sib):
    g = p_own + p_sib
    m_new = ADAM_B1 * mom + (1.0 - ADAM_B1) * g
    v_new = ADAM_B2 * vel + (1.0 - ADAM_B2) * (g * g)
    m_hat = m_new / (1.0 - ADAM_B1 ** ADAM_STEP)
    v_hat = v_new / (1.0 - ADAM_B2 ** ADAM_STEP)
    delta = -ADAM_LR * (m_hat / (jnp.sqrt(v_hat) + ADAM_EPS) + ADAM_WD * w)
    return g, delta, m_new, v_new


def _sum_slots_fn(slots):
    acc = slots[0].astype(F32)
    for s in range(1, slots.shape[0]):
        acc = acc + slots[s].astype(F32)
    return (acc,)


def _hdot(a, b):
    return lax.dot_general(a, b, (((1,), (0,)), ((), ())), precision=lax.Precision.HIGHEST,
                           preferred_element_type=F32)


def _hdot_nt(a, b):
    return lax.dot_general(a, b, (((1,), (1,)), ((), ())), precision=lax.Precision.HIGHEST,
                           preferred_element_type=F32)


def _hdot_tn(a, b):
    return lax.dot_general(a, b, (((0,), (0,)), ((), ())), precision=lax.Precision.HIGHEST,
                           preferred_element_type=F32)


def _dot(a, b):
    return lax.dot_general(a.astype(BF16), b.astype(BF16), (((1,), (0,)), ((), ())), preferred_element_type=F32)


def _dot_nt(a, b):
    return lax.dot_general(a.astype(BF16), b.astype(BF16), (((1,), (1,)), ((), ())), preferred_element_type=F32)


def _dot_tn(a, b):
    return lax.dot_general(a.astype(BF16), b.astype(BF16), (((0,), (0,)), ((), ())), preferred_element_type=F32)


def _hg_masks():
    c = HG_CHUNK
    row = lax.broadcasted_iota(jnp.int32, (c, c), 0)
    col = lax.broadcasted_iota(jnp.int32, (c, c), 1)
    base = row & (-HG_SUB)
    return row, col, base, col <= row, col < base


def _hg_gates(qr, fr, alb):
    lbound = _sigmoid(alb[0:1, :] - alb[1:2, :])
    sig = _sigmoid(fr)
    forget = lbound + (1.0 - lbound) * sig
    kk = (1.0 - lbound) * _sigmoid(-fr)
    qt = qr * _sigmoid(qr) * (HG_DK ** -0.5)
    return qt, kk, jnp.log(forget), lbound, sig, forget


def _hg_scores(qt, kk, g):
    c, nsub = HG_CHUNK, HG_CHUNK // HG_SUB
    row, col, base, causal, below = _hg_masks()
    b = _hdot(causal.astype(F32), g)
    rr = _hdot(below.astype(F32), g)
    bq = b - rr
    qh = qt * jnp.exp(bq)
    edecs = [None]
    parts = [jnp.zeros((HG_SUB, c), F32)]
    for i in range(1, nsub):
        edec = jnp.exp(jnp.minimum(rr[i * HG_SUB:i * HG_SUB + 1, :] - b, 0.0))
        edecs.append(edec)
        parts.append(_hdot_nt(qh[i * HG_SUB:(i + 1) * HG_SUB, :], kk * edec))
    a = jnp.where(below, jnp.concatenate(parts, axis=0), 0.0)
    b3 = b.reshape(nsub, HG_SUB, HG_DK)
    q3 = qt.reshape(nsub, HG_SUB, HG_DK)
    k3 = kk.reshape(nsub, HG_SUB, HG_DK)
    for j in range(HG_SUB):
        e = jnp.exp(jnp.minimum(b3 - b3[:, j:j + 1, :], 0.0))
        colv = jnp.sum(q3 * e * k3[:, j:j + 1, :], axis=-1, keepdims=True).reshape(c, 1)
        a = jnp.where(col == base + j, colv, a)
    a = jnp.where(causal, a, 0.0)
    return a, b, bq, qh, edecs, (b3, q3, k3)


def _hg_norm(o, gr, gain):
    r = lax.rsqrt(jnp.mean(o * o, axis=-1, keepdims=True) + RMS_EPS)
    sg = _sigmoid(gr)
    return o * r * gain, r, sg


def _hgrn2_fwd(proj, alb, gain, *, rb):
    m, d4 = proj.shape
    d = d4 // 4
    heads = d // HG_DK
    rb = min(rb, m)
    cpb = rb // HG_CHUNK
    nrb = m // rb

    def body(q_ref, f_ref, v_ref, g_ref, alb_ref, gain_ref, o_ref, og_ref, st_ref, state):
        @pl.when(pl.program_id(1) == 0)
        def _():
            state[...] = jnp.zeros(state.shape, F32)

        def chunk(ci, carry):
            sl = pl.ds(pl.multiple_of(ci * HG_CHUNK, HG_CHUNK), HG_CHUNK)
            qt, kk, g, _, _, _ = _hg_gates(q_ref[sl, :], f_ref[sl, :], alb_ref[...])
            v = v_ref[sl, :]
            st = state[...]
            st_ref[ci] = st
            a, b, _, _, _, _ = _hg_scores(qt, kk, g)
            o = _hdot(a, v) + _hdot_nt(qt * jnp.exp(b), st)
            b_last = b[HG_CHUNK - 1:HG_CHUNK, :]
            state[...] = st * jnp.exp(b_last) + _hdot_tn(v, kk * jnp.exp(b_last - b))
            o_ref[sl, :] = o
            n, _, sg = _hg_norm(o, g_ref[sl, :], gain_ref[...])
            og_ref[sl, :] = (n * g_ref[sl, :] * sg).astype(og_ref.dtype)
            return carry

        lax.fori_loop(0, cpb, chunk, 0)

    def col(cidx):
        return pl.BlockSpec((rb, HG_DK), lambda h, r: (r, cidx * heads + h))

    return pl.pallas_call(
        body, name="hgrn2_fwd", grid=(heads, nrb),
        in_specs=[col(0), col(1), col(2), col(3),
                  pl.BlockSpec((2, HG_DK), lambda h, r: (0, h)),
                  pl.BlockSpec((1, HG_DK), lambda h, r: (0, 0))],
        out_specs=[pl.BlockSpec((rb, HG_DK), lambda h, r: (r, h)),
                   pl.BlockSpec((rb, HG_DK), lambda h, r: (r, h)),
                   pl.BlockSpec((None, cpb, HG_DK, HG_DK), lambda h, r: (h, r, 0, 0))],
        out_shape=[jax.ShapeDtypeStruct((m, d), F32), jax.ShapeDtypeStruct((m, d), BF16),
                   jax.ShapeDtypeStruct((heads, m // HG_CHUNK, HG_DK, HG_DK), F32)],
        scratch_shapes=[pltpu.VMEM((HG_DK, HG_DK), F32)],
        compiler_params=_params(("parallel", "arbitrary")),
    )(proj, proj, proj, proj, alb, gain)


def _hgrn2_bwd(proj, o_pre, states, dog, alb, gain, *, rb):
    m, d4 = proj.shape
    d = d4 // 4
    heads = d // HG_DK
    rb = min(rb, m)
    cpb = rb // HG_CHUNK
    nrb = m // rb
    c, nsub = HG_CHUNK, HG_CHUNK // HG_SUB

    def body(q_ref, f_ref, v_ref, g_ref, o_ref, st_ref, dog_ref, alb_ref, gain_ref,
             dq_ref, df_ref, dv_ref, dg_ref, dalb_ref, dgain_ref, dstate, carry_ref):
        first = (pl.program_id(0) == 0) & (pl.program_id(1) == 0)

        @pl.when(first)
        def _():
            dgain_ref[...] = jnp.zeros(dgain_ref.shape, F32)

        @pl.when(pl.program_id(1) == 0)
        def _():
            dstate[...] = jnp.zeros(dstate.shape, F32)
            carry_ref[...] = jnp.zeros(carry_ref.shape, F32)
            dalb_ref[...] = jnp.zeros(dalb_ref.shape, F32)

        row, col, base, causal, below = _hg_masks()
        sub_iota = lax.broadcasted_iota(jnp.int32, (nsub, HG_SUB, HG_DK), 1)
        row_k = lax.broadcasted_iota(jnp.int32, (c, HG_DK), 0)
        upper = (col >= row).astype(F32)

        def chunk(step, carry):
            ci = cpb - 1 - step
            sl = pl.ds(pl.multiple_of(ci * HG_CHUNK, HG_CHUNK), HG_CHUNK)
            qr, fr, v, gr = q_ref[sl, :], f_ref[sl, :], v_ref[sl, :], g_ref[sl, :]
            qt, kk, g, lbound, sig, forget = _hg_gates(qr, fr, alb_ref[...])
            o = o_ref[sl, :]
            dogv = dog_ref[sl, :]
            gain_v = gain_ref[...]
            n, r, sg = _hg_norm(o, gr, gain_v)
            dgr = dogv * n * sg * (1.0 + gr * (1.0 - sg))
            dn = dogv * gr * sg
            dgain_ref[...] += _col_sum(dn * o * r)
            u = dn * gain_v
            d_o = r * u - o * (r * r * r) * jnp.mean(u * o, axis=-1, keepdims=True)
            st0 = st_ref[ci]
            dst = dstate[...]
            a, b, bq, qh, edecs, (b3, q3, k3) = _hg_scores(qt, kk, g)
            eb = jnp.exp(b)
            b_last = b[c - 1:c, :]
            kdl_dec = jnp.exp(b_last - b)
            kdl = kk * kdl_dec
            d_a = jnp.where(causal, _hdot_nt(d_o, v), 0.0)
            d_at = _hdot_nt(v, d_o)
            dv = _hdot_tn(a, d_o) + _hdot_nt(kdl, dst)
            dq = eb * _hdot(d_o, st0)
            dk = _hdot(v, dst) * kdl_dec
            d_a_below = jnp.where(below, d_a, 0.0)
            dq_parts = [jnp.zeros((HG_SUB, HG_DK), F32)]
            for i in range(1, nsub):
                lo, hi = i * HG_SUB, (i + 1) * HG_SUB
                dq_parts.append(_hdot(d_a_below[lo:hi, :], kk * edecs[i]))
                gi = _hdot(d_at[:, lo:hi], qh[lo:hi, :])
                dk = dk + jnp.where(row_k < lo, edecs[i] * gi, 0.0)
            dq = dq + jnp.concatenate(dq_parts, axis=0) * jnp.exp(bq)
            dq3 = jnp.zeros((nsub, HG_SUB, HG_DK), F32)
            dk3 = jnp.zeros((nsub, HG_SUB, HG_DK), F32)
            for j in range(HG_SUB):
                e = jnp.exp(jnp.minimum(b3 - b3[:, j:j + 1, :], 0.0))
                dcol = jnp.sum(jnp.where(col == base + j, d_a, 0.0), axis=-1, keepdims=True)
                t1 = dcol.reshape(nsub, HG_SUB, 1) * e
                dq3 = dq3 + t1 * k3[:, j:j + 1, :]
                dk3 = jnp.where(sub_iota == j, jnp.sum(t1 * q3, axis=1, keepdims=True), dk3)
            dq = dq + dq3.reshape(c, HG_DK)
            dk = dk + dk3.reshape(c, HG_DK)
            dstate[...] = dst * jnp.exp(b_last) + _hdot_tn(d_o, qt * eb)
            dglog = _hdot(upper, qt * dq - kk * dk) + carry_ref[...]
            carry_ref[...] = dglog[0:1, :]
            dforget = dglog / forget
            one_m_lb = 1.0 - lbound
            dsig = (dforget - dk) * one_m_lb
            sneg = _sigmoid(-fr)
            dlb = _col_sum(dforget * (1.0 - sig) - dk * sneg)
            dalb0 = dlb * lbound * one_m_lb
            dalb_ref[...] += jnp.concatenate([dalb0, -dalb0], axis=0)
            sq = _sigmoid(qr)
            dq_ref[sl, :] = (dq * (HG_DK ** -0.5) * sq * (1.0 + qr * (1.0 - sq))).astype(dq_ref.dtype)
            df_ref[sl, :] = (dsig * sig * (1.0 - sig)).astype(df_ref.dtype)
            dv_ref[sl, :] = dv.astype(dv_ref.dtype)
            dg_ref[sl, :] = dgr.astype(dg_ref.dtype)
            return carry

        lax.fori_loop(0, cpb, chunk, 0)

    def rev(r):
        return nrb - 1 - r

    def col(cidx):
        return pl.BlockSpec((rb, HG_DK), lambda h, r: (rev(r), cidx * heads + h))

    def head_rows():
        return pl.BlockSpec((rb, HG_DK), lambda h, r: (rev(r), h))

    return pl.pallas_call(
        body, name="hgrn2_bwd", grid=(heads, nrb),
        in_specs=[col(0), col(1), col(2), col(3), head_rows(),
                  pl.BlockSpec((None, cpb, HG_DK, HG_DK), lambda h, r: (h, rev(r), 0, 0)),
                  head_rows(),
                  pl.BlockSpec((2, HG_DK), lambda h, r: (0, h)),
                  pl.BlockSpec((1, HG_DK), lambda h, r: (0, 0))],
        out_specs=[head_rows(), head_rows(), head_rows(), head_rows(),
                   pl.BlockSpec((2, HG_DK), lambda h, r: (0, h)),
                   pl.BlockSpec((1, HG_DK), lambda h, r: (0, 0))],
        out_shape=[jax.ShapeDtypeStruct((m, d), BF16)] * 4
                  + [jax.ShapeDtypeStruct((2, d), F32), jax.ShapeDtypeStruct((1, HG_DK), F32)],
        scratch_shapes=[pltpu.VMEM((HG_DK, HG_DK), F32), pltpu.VMEM((1, HG_DK), F32)],
        compiler_params=_params(("arbitrary", "arbitrary")),
    )(proj, proj, proj, proj, o_pre, states, dog, alb, gain)


def _swa_probs(qh, kp, kc, sink, slope, has_prev):
    qi = lax.broadcasted_iota(jnp.int32, (WINDOW, WINDOW), 0)
    si = lax.broadcasted_iota(jnp.int32, (WINDOW, WINDOW), 1)
    scale = ATT_HD ** -0.5
    dist_c = (qi - si).astype(F32)
    s_p = _dot_nt(qh, kp) * scale - slope * (dist_c + float(WINDOW))
    s_c = _dot_nt(qh, kc) * scale - slope * dist_c
    s_p = jnp.where((si > qi) & has_prev, s_p, NEG)
    s_c = jnp.where(si <= qi, s_c, NEG)
    mx = jnp.maximum(jnp.maximum(jnp.max(s_p, axis=-1, keepdims=True), jnp.max(s_c, axis=-1, keepdims=True)), sink)
    e_p, e_c, e_s = jnp.exp(s_p - mx), jnp.exp(s_c - mx), jnp.exp(sink - mx)
    inv = 1.0 / (jnp.sum(e_p, axis=-1, keepdims=True) + jnp.sum(e_c, axis=-1, keepdims=True) + e_s)
    return e_p * inv, e_c * inv, e_s * inv


def _slope(h, n_heads):
    return float(2.0 ** (-8.0 * (h + 1) / n_heads))


def _swa_fwd(q, kv, sinks):
    m, d = q.shape
    n_heads = d // ATT_HD
    kvh = n_heads // ATT_G
    kd = kvh * ATT_HD
    nb = m // WINDOW

    def body(q_ref, kvp_ref, kvc_ref, sink_ref, o_ref):
        has_prev = pl.program_id(0) > 0
        qv, kvp, kvc = q_ref[...], kvp_ref[...], kvc_ref[...]
        outs = []
        for h in range(n_heads):
            kh = h // ATT_G
            ks = slice(kh * ATT_HD, (kh + 1) * ATT_HD)
            vs = slice(kd + kh * ATT_HD, kd + (kh + 1) * ATT_HD)
            p_p, p_c, _ = _swa_probs(qv[:, h * ATT_HD:(h + 1) * ATT_HD], kvp[:, ks], kvc[:, ks],
                                     sink_ref[:, h:h + 1], _slope(h, n_heads), has_prev)
            outs.append(_dot(p_p, kvp[:, vs]) + _dot(p_c, kvc[:, vs]))
        o_ref[...] = jnp.concatenate(outs, axis=-1).astype(o_ref.dtype)

    return pl.pallas_call(
        body, name="swa_fwd", grid=(nb,),
        in_specs=[pl.BlockSpec((WINDOW, d), lambda n: (n, 0)),
                  pl.BlockSpec((WINDOW, 2 * kd), lambda n: (jnp.maximum(n - 1, 0), 0)),
                  pl.BlockSpec((WINDOW, 2 * kd), lambda n: (n, 0)),
                  pl.BlockSpec((1, n_heads), lambda n: (0, 0))],
        out_specs=pl.BlockSpec((WINDOW, d), lambda n: (n, 0)),
        out_shape=jax.ShapeDtypeStruct((m, d), BF16),
        compiler_params=_params(("arbitrary",)),
    )(q, kv, kv, sinks)


def _swa_bwd(q, kv, sinks, dao):
    m, d = q.shape
    n_heads = d // ATT_HD
    kvh = n_heads // ATT_G
    kd = kvh * ATT_HD
    nb = m // WINDOW
    scale = ATT_HD ** -0.5

    def body(q_ref, kvp_ref, kvc_ref, sink_ref, do_ref, dq_ref, dkvc_ref, dkvp_ref, dqsum_ref, dsink_ref):
        @pl.when(pl.program_id(0) == 0)
        def _():
            dqsum_ref[...] = jnp.zeros(dqsum_ref.shape, F32)
            dsink_ref[...] = jnp.zeros(dsink_ref.shape, F32)

        has_prev = pl.program_id(0) > 0
        qv, kvp, kvc, dov = q_ref[...], kvp_ref[...], kvc_ref[...], do_ref[...]
        lane_h = lax.broadcasted_iota(jnp.int32, (1, n_heads), 1)
        dsink = jnp.zeros((1, n_heads), F32)
        dq_parts, dk_p, dk_c, dv_p, dv_c = [], [], [], [], []
        for kh in range(kvh):
            ks = slice(kh * ATT_HD, (kh + 1) * ATT_HD)
            vs = slice(kd + kh * ATT_HD, kd + (kh + 1) * ATT_HD)
            kp, kc, vp, vc = kvp[:, ks], kvc[:, ks], kvp[:, vs], kvc[:, vs]
            acc = [jnp.zeros((WINDOW, ATT_HD), F32) for _ in range(4)]
            for gq in range(ATT_G):
                h = kh * ATT_G + gq
                hs = slice(h * ATT_HD, (h + 1) * ATT_HD)
                qh, doh = qv[:, hs], dov[:, hs]
                p_p, p_c, p_s = _swa_probs(qh, kp, kc, sink_ref[:, h:h + 1], _slope(h, n_heads), has_prev)
                dp_p, dp_c = _dot_nt(doh, vp), _dot_nt(doh, vc)
                delta = jnp.sum(p_p * dp_p, axis=-1, keepdims=True) + jnp.sum(p_c * dp_c, axis=-1, keepdims=True)
                ds_p, ds_c = p_p * (dp_p - delta), p_c * (dp_c - delta)
                dsink = dsink + jnp.where(lane_h == h, -_col_sum(p_s * delta), 0.0)
                dq_parts.append((_dot(ds_p, kp) + _dot(ds_c, kc)) * scale)
                acc[0] = acc[0] + _dot_tn(ds_p, qh) * scale
                acc[1] = acc[1] + _dot_tn(ds_c, qh) * scale
                acc[2] = acc[2] + _dot_tn(p_p, doh)
                acc[3] = acc[3] + _dot_tn(p_c, doh)
            dk_p.append(acc[0])
            dk_c.append(acc[1])
            dv_p.append(acc[2])
            dv_c.append(acc[3])
        dq = jnp.concatenate(dq_parts, axis=-1)
        dq_ref[...] = dq.astype(dq_ref.dtype)
        dqsum_ref[...] += _col_sum(dq)
        dsink_ref[...] += dsink
        dkvc_ref[...] = jnp.concatenate(dk_c + dv_c, axis=-1)
        dkvp_ref[...] = jnp.concatenate(dk_p + dv_p, axis=-1)

    return pl.pallas_call(
        body, name="swa_bwd", grid=(nb,),
        in_specs=[pl.BlockSpec((WINDOW, d), lambda n: (n, 0)),
                  pl.BlockSpec((WINDOW, 2 * kd), lambda n: (jnp.maximum(n - 1, 0), 0)),
                  pl.BlockSpec((WINDOW, 2 * kd), lambda n: (n, 0)),
                  pl.BlockSpec((1, n_heads), lambda n: (0, 0)),
                  pl.BlockSpec((WINDOW, d), lambda n: (n, 0))],
        out_specs=[pl.BlockSpec((WINDOW, d), lambda n: (n, 0)),
                   pl.BlockSpec((WINDOW, 2 * kd), lambda n: (n, 0)),
                   pl.BlockSpec((WINDOW, 2 * kd), lambda n: (n, 0)),
                   pl.BlockSpec((1, d), lambda n: (0, 0)),
                   pl.BlockSpec((1, n_heads), lambda n: (0, 0))],
        out_shape=[jax.ShapeDtypeStruct((m, d), BF16), jax.ShapeDtypeStruct((m, 2 * kd), F32),
                   jax.ShapeDtypeStruct((m, 2 * kd), F32), jax.ShapeDtypeStruct((1, d), F32),
                   jax.ShapeDtypeStruct((1, n_heads), F32)],
        compiler_params=_params(("arbitrary",)),
    )(q, kv, kv, sinks, dao)


def _kv_grad_combine(dkv_cur, dkv_prev):
    m, w = dkv_cur.shape
    nb = m // WINDOW

    def body(cur_ref, nxt_ref, o_ref, sum_ref):
        @pl.when(pl.program_id(0) == 0)
        def _():
            sum_ref[...] = jnp.zeros(sum_ref.shape, F32)

        total = cur_ref[...] + jnp.where(pl.program_id(0) < nb - 1, nxt_ref[...], 0.0)
        o_ref[...] = total.astype(o_ref.dtype)
        sum_ref[...] += _col_sum(total)

    return pl.pallas_call(
        body, name="kv_grad_combine", grid=(nb,),
        in_specs=[pl.BlockSpec((WINDOW, w), lambda n: (n, 0)),
                  pl.BlockSpec((WINDOW, w), lambda n: (jnp.minimum(n + 1, nb - 1), 0))],
        out_specs=[pl.BlockSpec((WINDOW, w), lambda n: (n, 0)), pl.BlockSpec((1, w), lambda n: (0, 0))],
        out_shape=[jax.ShapeDtypeStruct((m, w), BF16), jax.ShapeDtypeStruct((1, w), F32)],
        compiler_params=_params(("arbitrary",)),
    )(dkv_cur, dkv_prev)


def _row(v):
    return v.reshape(1, -1)


def _local_step(x, p, target, w, ln_gain, ln_bias, alb, norm_gain, kv_b, b_q, sinks, b_out, ple_b):
    gw, gs = {}, {}
    gains = [[_row(ln_gain[i, j]) for j in range(3)] for i in range(DEPTH)]
    biases = [[_row(ln_bias[i, j]) for j in range(3)] for i in range(DEPTH)]
    sd = x.shape

    def ln_fwd(xin, h, i, j, nm):
        return _rowwise(_ln_fwd_fn, [xin, h], [gains[i][j], biases[i][j]], [(sd, F32), (sd, BF16)], name=nm)

    def tail_fwd(xa, i):
        gu = _mm(xa[1], w["ffn_w_gate_up"], lb=i, name=f"ffn_up{i}")
        act, = _rowwise(_swiglu_fwd_fn, [gu], [], [((sd[0], gu.shape[1] // 2), BF16)], name=f"swiglu{i}", tm=128)
        f = _mm(act, w["ffn_w_down"], lb=i, name=f"ffn_down{i}")
        xb = ln_fwd(xa[0], f, i, 1, f"ln_ffn{i}")
        pg = _mm(xb[1], w["ple_w_gate"], lb=i, bias=_row(ple_b[i]), name=f"ple_gate{i}")
        pu = _mm(p[i], w["ple_w_up"], lb=i, name=f"ple_up{i}")
        xc = _rowwise(_ple_ln_fwd_fn, [xb[0], pg, pu], [gains[i][2], biases[i][2]], [(sd, F32), (sd, BF16)],
                      name=f"ln_ple{i}")
        return dict(xa=xa, gu=gu, act=act, f=f, xb=xb, pg=pg, pu=pu), xc

    def tail_bwd(dxc, sv, i):
        xa, xb = sv["xa"], sv["xb"]
        dxb_part, dpg, dpu, dg2, db2, dbg = _rowwise(
            _ple_ln_bwd_fn, [dxc, xb[0], sv["pg"], sv["pu"]], [gains[i][2]],
            [(sd, F32), (sd, BF16), (sd, BF16)], [((1, sd[1]), F32)] * 3, name=f"ln_ple_bwd{i}")
        gs[("ple_b", i)] = dbg
        gs[("ln_gain", i, 2)], gs[("ln_bias", i, 2)] = dg2, db2
        grad_into("ple_w_gate", i, xb[1], dpg)
        grad_into("ple_w_up", i, p[i], dpu)
        dxb = _mm(dpg, w["ple_w_gate"], lb=i, tb=True, add=dxb_part, name=f"ple_gate_dx{i}")
        dxa_part, df, dg1, db1, _ = _rowwise(
            _ln_bwd_fn, [dxb, xa[0], sv["f"]], [gains[i][1]],
            [(sd, F32), (sd, BF16)], [((1, sd[1]), F32)] * 3, name=f"ln_ffn_bwd{i}")
        gs[("ln_gain", i, 1)], gs[("ln_bias", i, 1)] = dg1, db1
        grad_into("ffn_w_down", i, sv["act"], df)
        dact = _mm(df, w["ffn_w_down"], lb=i, tb=True, name=f"ffn_down_dx{i}")
        dgu, = _rowwise(_swiglu_bwd_fn, [sv["gu"], dact], [], [(sv["gu"].shape, BF16)], name=f"swiglu_bwd{i}", tm=128)
        grad_into("ffn_w_gate_up", i, xa[1], dgu)
        return _mm(dgu, w["ffn_w_gate_up"], lb=i, tb=True, add=dxa_part, name=f"ffn_up_dx{i}")

    def grad_into(nm, i, act, dout):
        layers = w[nm].shape[0]
        gw[nm] = _mm(act, dout, ta=True, out_dtype=BF16, out_layers=layers, out_layer=i, out_into=gw.get(nm),
                     name=f"grad_{nm}{i}")

    def mixer_ln_bwd(dxa, xin, h, i):
        dx_part, dh, dg0, db0, dhsum = _rowwise(
            _ln_bwd_fn, [dxa, xin, h], [gains[i][0]],
            [(sd, F32), (sd, BF16)], [((1, sd[1]), F32)] * 3, name=f"ln_mix_bwd{i}")
        gs[("ln_gain", i, 0)], gs[("ln_bias", i, 0)] = dg0, db0
        return dx_part, dh, dhsum

    proj = _mm(x, w["a_w_in"], lb=0, name="hg_proj")
    o_pre, og, states = _hgrn2_fwd(proj, alb, norm_gain, rb=HG_ROWS)
    h0 = _mm(og, w["a_w_out"], lb=0, name="hg_out")
    x1 = ln_fwd(x, h0, 0, 0, "ln_mix0")
    sv0, x3 = tail_fwd(x1, 0)
    kv = _mm(x3[1], w["kv_w"], lb=0, bias=_row(kv_b), name="kv_proj")
    q = _mm(x3[1], w["b_w_q"], lb=0, bias=b_q, name="q_proj")
    ao = _swa_fwd(q, kv, sinks)
    h1 = _mm(ao, w["b_w_out"], lb=0, bias=b_out, name="att_out")
    x4 = ln_fwd(x3[0], h1, 1, 0, "ln_mix1")
    sv1, y = tail_fwd(x4, 1)
    dy, loss = _rowwise(_loss_fn, [y[0], target], [], [(sd, F32)], [((1, LANES), F32)], name="loss")

    dx4 = tail_bwd(dy, sv1, 1)
    dx3_part, dh1, dh1sum = mixer_ln_bwd(dx4, x3[0], h1, 1)
    gs["b_out"] = dh1sum
    grad_into("b_w_out", 0, ao, dh1)
    dao = _mm(dh1, w["b_w_out"], lb=0, tb=True, name="att_out_dx")
    dq, dkv_cur, dkv_prev, dqsum, dsinks = _swa_bwd(q, kv, sinks, dao)
    gs["b_q"], gs["sinks"] = dqsum, dsinks
    dkv, dkvsum = _kv_grad_combine(dkv_cur, dkv_prev)
    gs["kv_b"] = dkvsum
    grad_into("b_w_q", 0, x3[1], dq)
    grad_into("kv_w", 0, x3[1], dkv)
    dx3 = _mm(dq, w["b_w_q"], lb=0, tb=True, add=dx3_part, name="q_proj_dx")
    dx3 = _mm(dkv, w["kv_w"], lb=0, tb=True, add=dx3, name="kv_proj_dx")
    dx1 = tail_bwd(dx3, sv0, 0)
    dx_part, dh0, _ = mixer_ln_bwd(dx1, x, h0, 0)
    grad_into("a_w_out", 0, og, dh0)
    dog = _mm(dh0, w["a_w_out"], lb=0, tb=True, name="hg_out_dx")
    dqr, dfr, dvr, dgr, dalb, dgain = _hgrn2_bwd(proj, o_pre, states, dog, alb, norm_gain, rb=HG_ROWS)
    gs["alb"], gs["norm_gain"] = dalb, dgain
    dproj = jnp.concatenate([dqr, dfr, dvr, dgr], axis=1)
    grad_into("a_w_in", 0, x, dproj)
    grad_x = _mm(dproj, w["a_w_in"], lb=0, tb=True, add=dx_part, name="hg_proj_dx")
    return loss, grad_x, gw, gs


HBM_SPEC = pl.BlockSpec(memory_space=pl.ANY)


def _piece(ref, kind, j):
    _, r, c = ref.shape
    if kind == "row":
        return ref.at[:, pl.ds(j * (r // N_CHIPS), r // N_CHIPS), :]
    return ref.at[:, :, pl.ds(j * (c // N_CHIPS), c // N_CHIPS)]


def _chip_of(j, c):
    return (j // 2, j % 2, c)


def _gather_chips(shards, kinds):
    n = len(shards)

    def full_shape(s, kind):
        l, r, c = s.shape
        return (l, r * N_CHIPS, c) if kind == "row" else (l, r, c * N_CHIPS)

    def body(*refs):
        ins, outs = refs[:n], refs[n:2 * n]
        send_sems, recv_sems, local_sems = refs[2 * n:]
        x, y, c = lax.axis_index("x"), lax.axis_index("y"), lax.axis_index("c")
        me = 2 * x + y
        for j in range(N_CHIPS):
            @pl.when(me == j)
            def _():
                for a in range(n):
                    dst = _piece(outs[a], kinds[a], j)
                    pltpu.make_async_copy(ins[a], dst, local_sems.at[a]).start()
                    for t in range(N_CHIPS):
                        if t != j:
                            pltpu.make_async_remote_copy(
                                src_ref=ins[a], dst_ref=dst, send_sem=send_sems.at[a, t], recv_sem=recv_sems.at[a, j],
                                device_id=_chip_of(t, c), device_id_type=MESH).start()
        for j in range(N_CHIPS):
            @pl.when(me != j)
            def _():
                for a in range(n):
                    dst = _piece(outs[a], kinds[a], j)
                    cp = pltpu.make_async_remote_copy(
                        src_ref=ins[a], dst_ref=dst, send_sem=send_sems.at[a, j], recv_sem=recv_sems.at[a, j],
                        device_id=_chip_of(j, c), device_id_type=MESH)
                    cp.wait_recv()
                    cp.wait_send()
        for j in range(N_CHIPS):
            @pl.when(me == j)
            def _():
                for a in range(n):
                    pltpu.make_async_copy(ins[a], _piece(outs[a], kinds[a], j), local_sems.at[a]).wait()

    return pl.pallas_call(
        body, name="gather_weights",
        in_specs=[HBM_SPEC] * n, out_specs=[HBM_SPEC] * n,
        out_shape=[jax.ShapeDtypeStruct(full_shape(s, k), s.dtype) for s, k in zip(shards, kinds)],
        scratch_shapes=[pltpu.SemaphoreType.DMA((n, N_CHIPS)), pltpu.SemaphoreType.DMA((n, N_CHIPS)),
                        pltpu.SemaphoreType.DMA((n,))],
    )(*shards)


def _scatter_chips(fulls, kinds):
    n = len(fulls)

    def slot_shape(s, kind):
        l, r, c = s.shape
        return (N_CHIPS, l, r // N_CHIPS, c) if kind == "row" else (N_CHIPS, l, r, c // N_CHIPS)

    def body(*refs):
        ins, outs = refs[:n], refs[n:2 * n]
        send_sems, recv_sems, local_sems = refs[2 * n:]
        x, y, c = lax.axis_index("x"), lax.axis_index("y"), lax.axis_index("c")
        me = 2 * x + y
        for j in range(N_CHIPS):
            @pl.when(me == j)
            def _():
                for a in range(n):
                    pltpu.make_async_copy(_piece(ins[a], kinds[a], j), outs[a].at[j], local_sems.at[a]).start()
                    for t in range(N_CHIPS):
                        if t != j:
                            pltpu.make_async_remote_copy(
                                src_ref=_piece(ins[a], kinds[a], t), dst_ref=outs[a].at[j],
                                send_sem=send_sems.at[a, t], recv_sem=recv_sems.at[a, j],
                                device_id=_chip_of(t, c), device_id_type=MESH).start()
        for j in range(N_CHIPS):
            @pl.when(me != j)
            def _():
                for a in range(n):
                    cp = pltpu.make_async_remote_copy(
                        src_ref=_piece(ins[a], kinds[a], j), dst_ref=outs[a].at[j],
                        send_sem=send_sems.at[a, j], recv_sem=recv_sems.at[a, j],
                        device_id=_chip_of(j, c), device_id_type=MESH)
                    cp.wait_recv()
                    cp.wait_send()
        for j in range(N_CHIPS):
            @pl.when(me == j)
            def _():
                for a in range(n):
                    pltpu.make_async_copy(_piece(ins[a], kinds[a], j), outs[a].at[j], local_sems.at[a]).wait()

    return pl.pallas_call(
        body, name="scatter_grads",
        in_specs=[HBM_SPEC] * n, out_specs=[HBM_SPEC] * n,
        out_shape=[jax.ShapeDtypeStruct(slot_shape(s, k), s.dtype) for s, k in zip(fulls, kinds)],
        scratch_shapes=[pltpu.SemaphoreType.DMA((n, N_CHIPS)), pltpu.SemaphoreType.DMA((n, N_CHIPS)),
                        pltpu.SemaphoreType.DMA((n,))],
    )(*fulls)


def _sibling_swap(arrays):
    n = len(arrays)

    def body(*refs):
        ins, outs = refs[:n], refs[n:2 * n]
        send_sems, recv_sems = refs[2 * n:]
        sibling = (lax.axis_index("x"), lax.axis_index("y"), 1 - lax.axis_index("c"))
        copies = [pltpu.make_async_remote_copy(src_ref=ins[a], dst_ref=outs[a], send_sem=send_sems.at[a],
                                               recv_sem=recv_sems.at[a], device_id=sibling, device_id_type=MESH)
                  for a in range(n)]
        for cp in copies:
            cp.start()
        for cp in copies:
            cp.wait()

    return pl.pallas_call(
        body, name="sibling_swap",
        in_specs=[HBM_SPEC] * n, out_specs=[HBM_SPEC] * n,
        out_shape=[jax.ShapeDtypeStruct(a.shape, a.dtype) for a in arrays],
        scratch_shapes=[pltpu.SemaphoreType.DMA((n,)), pltpu.SemaphoreType.DMA((n,))],
    )(*arrays)


def _gather_devices(vec):
    def body(in_ref, out_ref, send_sems, recv_sems, local_sem):
        x, y, c = lax.axis_index("x"), lax.axis_index("y"), lax.axis_index("c")
        me = 4 * x + 2 * y + c
        mine = pltpu.make_async_copy(in_ref, out_ref.at[me], local_sem)
        mine.start()
        copies = []
        for rel in range(1, N_DEV):
            peer = (x ^ (rel >> 2), y ^ ((rel >> 1) & 1), c ^ (rel & 1))
            copies.append(pltpu.make_async_remote_copy(
                src_ref=in_ref, dst_ref=out_ref.at[me], send_sem=send_sems.at[rel], recv_sem=recv_sems.at[rel],
                device_id=peer, device_id_type=MESH))
        for cp in copies:
            cp.start()
        for cp in copies:
            cp.wait()
        mine.wait()

    return pl.pallas_call(
        body, name="gather_small",
        in_specs=[HBM_SPEC], out_specs=HBM_SPEC,
        out_shape=jax.ShapeDtypeStruct((N_DEV,) + vec.shape, vec.dtype),
        scratch_shapes=[pltpu.SemaphoreType.DMA((N_DEV,)), pltpu.SemaphoreType.DMA((N_DEV,)),
                        pltpu.SemaphoreType.DMA],
    )(vec)


BIG = [("a_w_in", "col"), ("a_w_out", "row"), ("kv_w", "row"), ("b_w_q", "row"), ("b_w_out", "row"),
       ("ffn_w_gate_up", "col"), ("ffn_w_down", "row"), ("ple_w_up", "col"), ("ple_w_gate", "row")]
SMALL_SHARDED = ["ln_gain", "ln_bias", "a_lower_bound"]
SMALL_REPLICATED = ["a_norm_gain", "kv_b", "b_b_q", "b_sinks", "b_b_out", "ple_b_gate"]
WEIGHT_ORDER = ["a_w_in", "a_lower_bound", "a_norm_gain", "a_w_out", "kv_w", "kv_b", "b_w_q", "b_b_q", "b_sinks",
                "b_w_out", "b_b_out", "ffn_w_gate_up", "ffn_w_down", "ple_w_up", "ple_w_gate", "ple_b_gate",
                "ln_gain", "ln_bias"]


def _as3(a):
    return a.reshape((-1,) + a.shape[-2:]) if a.ndim >= 3 else a.reshape((1,) + a.shape)


def _pad_lanes(v):
    n = v.shape[-1]
    return jnp.pad(v, ((0, 0), (0, (-n) % LANES)))


def _adam_small_fn(w, mom, vel, g):
    return _adam_fn(w, mom, vel, g, jnp.zeros_like(g))[1:]


def _sum_rows_fn(slots):
    acc = slots[0]
    for s in range(1, slots.shape[0]):
        acc = acc + slots[s]
    return (acc,)


def kernel(x, p, a_w_in, a_lower_bound, a_norm_gain, a_w_out, kv_w, kv_b, b_w_q, b_b_q, b_sinks, b_w_out, b_b_out, ffn_w_gate_up, ffn_w_down, ple_w_up, ple_w_gate, ple_b_gate, ln_gain, ln_bias, loss_target, m_a_w_in, m_a_lower_bound, m_a_norm_gain, m_a_w_out, m_kv_w, m_kv_b, m_b_w_q, m_b_b_q, m_b_sinks, m_b_w_out, m_b_b_out, m_ffn_w_gate_up, m_ffn_w_down, m_ple_w_up, m_ple_w_gate, m_ple_b_gate, m_ln_gain, m_ln_bias, v_a_w_in, v_a_lower_bound, v_a_norm_gain, v_a_w_out, v_kv_w, v_kv_b, v_b_w_q, v_b_b_q, v_b_sinks, v_b_w_out, v_b_b_out, v_ffn_w_gate_up, v_ffn_w_down, v_ple_w_up, v_ple_w_gate, v_ple_b_gate, v_ln_gain, v_ln_bias):
    args = dict(locals())
    wts = {n: args[n] for n in WEIGHT_ORDER}
    mom = {n: args["m_" + n] for n in WEIGHT_ORDER}
    vel = {n: args["v_" + n] for n in WEIGHT_ORDER}
    chip = 2 * lax.axis_index("x") + lax.axis_index("y")
    d = x.shape[-1]
    dq = d // N_CHIPS

    big_shards = [_as3(wts[n]).astype(BF16) for n, _ in BIG]
    small_pack = jnp.concatenate([wts[n].reshape(-1, dq) for n in SMALL_SHARDED], axis=0)[None]
    gathered = _gather_chips(big_shards + [small_pack], [k for _, k in BIG] + ["col"])
    wfull = {n: g for (n, _), g in zip(BIG, gathered)}
    small_full = gathered[-1][0]
    ln_gain_f = small_full[0:6].reshape(DEPTH, 3, d)
    ln_bias_f = small_full[6:12].reshape(DEPTH, 3, d)
    alb_f = small_full[12:14]

    loss, grad_x, gw, gs = _local_step(
        x[0], p[:, 0], loss_target[0], wfull, ln_gain_f, ln_bias_f, alb_f, a_norm_gain, kv_b, b_b_q, b_sinks,
        b_b_out, ple_b_gate)
    loss = lax.psum(loss[0, 0], ("x", "y", "c"))

    slots = _scatter_chips([gw[n] for n, _ in BIG], [k for _, k in BIG])
    partial = []
    for (n, _), s in zip(BIG, slots):
        s2 = s.reshape(N_CHIPS, -1, s.shape[-1])
        partial.append(_rowwise(_sum_slots_fn, [s2], [], [(s2.shape[1:], F32)], name=f"sum_{n}")[0])
    sibling = _sibling_swap(partial)
    res = {}
    for (n, _), own, sib in zip(BIG, partial, sibling):
        shp = wts[n].shape
        flat = lambda a: a.reshape(-1, shp[-1])
        out = _rowwise(_adam_fn, [flat(wts[n]), flat(mom[n]), flat(vel[n]), own, sib], [],
                       [(own.shape, F32)] * 4, name=f"adam_{n}")
        res[n] = [o.reshape(shp) for o in out]

    ln_g = jnp.concatenate([gs[("ln_gain", i, j)] for i in range(DEPTH) for j in range(3)], axis=0)
    ln_b = jnp.concatenate([gs[("ln_bias", i, j)] for i in range(DEPTH) for j in range(3)], axis=0)
    ple_bg = jnp.concatenate([gs[("ple_b", i)] for i in range(DEPTH)], axis=0)
    small_list = [ln_g.reshape(1, -1), ln_b.reshape(1, -1), gs["alb"].reshape(1, -1), gs["norm_gain"],
                  gs["kv_b"], gs["b_q"], _pad_lanes(gs["sinks"]), gs["b_out"], ple_bg.reshape(1, -1)]
    small_vec = jnp.concatenate(small_list, axis=1)
    everyone = _gather_devices(small_vec)
    total, = _rowwise(_sum_rows_fn, [everyone], [], [(small_vec.shape, F32)], name="sum_small")
    offs, pos = [], 0
    for v in small_list:
        offs.append((pos, v.shape[1]))
        pos += v.shape[1]

    def seg(k):
        return total[0, offs[k][0]:offs[k][0] + offs[k][1]]

    def my_cols(full, rows):
        return lax.dynamic_slice_in_dim(full.reshape(rows, N_CHIPS, dq), chip, 1, axis=1).reshape(rows, dq)

    n_sink = b_sinks.shape[-1]
    small_grads = {
        "ln_gain": my_cols(seg(0), 6).reshape(ln_gain.shape), "ln_bias": my_cols(seg(1), 6).reshape(ln_bias.shape),
        "a_lower_bound": my_cols(seg(2), 2), "a_norm_gain": seg(3).reshape(a_norm_gain.shape),
        "kv_b": seg(4).reshape(kv_b.shape), "b_b_q": seg(5).reshape(b_b_q.shape),
        "b_sinks": seg(6)[:n_sink].reshape(b_sinks.shape), "b_b_out": seg(7).reshape(b_b_out.shape),
        "ple_b_gate": seg(8).reshape(ple_b_gate.shape)}
    names = SMALL_SHARDED + SMALL_REPLICATED
    pack = lambda dct: _pad_lanes(jnp.concatenate([dct[n].reshape(1, -1) for n in names], axis=1))
    g_pack = pack(small_grads)
    upd = _rowwise(_adam_small_fn, [pack(wts), pack(mom), pack(vel), g_pack], [], [(g_pack.shape, F32)] * 3,
                   name="adam_small")
    pos = 0
    for n in names:
        size = wts[n].size
        res[n] = [small_grads[n]] + [u[0, pos:pos + size].reshape(wts[n].shape) for u in upd]
        pos += size

    outs = [loss, grad_x[None]]
    for k in range(4):
        outs += [res[n][k] for n in WEIGHT_ORDER]
    return tuple(outs)
```

```python
import functools

import jax
import jax.numpy as jnp
from jax import lax
from jax.experimental import pallas as pl
from jax.experimental.pallas import tpu as pltpu

F32 = jnp.float32
BF16 = jnp.bfloat16
MESH = pl.DeviceIdType.MESH

LANES = 128
HG_DK = 128
HG_CHUNK = 64
HG_SUB = 16
HG_ROWS = 512
ATT_HD = 64
ATT_G = 4
WINDOW = 128
DEPTH = 2
ALPHA = (2.0 * DEPTH) ** 0.25
LN_EPS = 1e-5
RMS_EPS = 1e-6
ADAM_LR, ADAM_B1, ADAM_B2, ADAM_EPS, ADAM_WD, ADAM_STEP = 0.001, 0.9, 0.999, 1e-08, 0.01, 10
N_CHIPS = 4
N_DEV = 8
VMEM_LIMIT = 56 * 1024 * 1024
NEG = -1e30


def _pick(n, cap):
    best = None
    for d in range(LANES, min(n, cap) + 1, LANES):
        if n % d == 0:
            best = d
    return n if best is None else best


def _pick_rows(m, cap):
    best = None
    for d in range(16, min(m, cap) + 1, 16):
        if m % d == 0:
            best = d
    return m if best is None else best


def _params(sem):
    return pltpu.CompilerParams(dimension_semantics=sem, vmem_limit_bytes=VMEM_LIMIT)


def _mm(a, b, *, name, la=None, lb=None, ta=False, tb=False, bias=None, add=None, out_dtype=F32,
        out_layers=None, out_layer=None, after=None, caps=(512, 1536, 2048)):
    ar, ac = a.shape[-2:]
    br, bc = b.shape[-2:]
    m, k = (ac, ar) if ta else (ar, ac)
    k2, n = (bc, br) if tb else (br, bc)
    assert k == k2, (a.shape, b.shape, ta, tb)
    tm, tn, tk = _pick(m, caps[0]), _pick(n, caps[1]), _pick(k, caps[2])
    nk = k // tk
    grid = (m // tm, n // tn, nk)

    def spec(block, idx, layer):
        if layer is None:
            return pl.BlockSpec(block, idx)
        return pl.BlockSpec((None,) + block, lambda i, j, kk: (layer,) + idx(i, j, kk))

    a_spec = spec((tk, tm), lambda i, j, kk: (kk, i), la) if ta else spec((tm, tk), lambda i, j, kk: (i, kk), la)
    b_spec = spec((tn, tk), lambda i, j, kk: (j, kk), lb) if tb else spec((tk, tn), lambda i, j, kk: (kk, j), lb)
    in_specs, operands = [a_spec, b_spec], [a, b]
    if bias is not None:
        in_specs.append(pl.BlockSpec((1, tn), lambda i, j, kk: (0, j)))
        operands.append(bias)
    if add is not None:
        in_specs.append(pl.BlockSpec((tm, tn), lambda i, j, kk: (i, j)))
        operands.append(add)
    if after is not None:
        in_specs.append(pl.BlockSpec(memory_space=pl.ANY))
        operands.append(after)
    if out_layers is None:
        out_shape = jax.ShapeDtypeStruct((m, n), out_dtype)
    else:
        out_shape = jax.ShapeDtypeStruct((out_layers, m, n), out_dtype)
    out_spec = spec((tm, tn), lambda i, j, kk: (i, j), out_layer)
    dims = (((0 if ta else 1,), (1 if tb else 0,)), ((), ()))
    has_bias, has_add, has_alias = bias is not None, add is not None, after is not None

    def body(*refs):
        a_ref, b_ref = refs[0], refs[1]
        pos = 2
        bias_ref = add_ref = None
        if has_bias:
            bias_ref = refs[pos]
            pos += 1
        if has_add:
            add_ref = refs[pos]
            pos += 1
        if has_alias:
            pos += 1
        o_ref = refs[pos]
        acc_ref = refs[pos + 1] if nk > 1 else None
        part = lax.dot_general(a_ref[...].astype(BF16), b_ref[...].astype(BF16), dims, preferred_element_type=F32)

        def finish(total):
            if has_bias:
                total = total + bias_ref[...]
            if has_add:
                total = total + add_ref[...]
            o_ref[...] = total.astype(o_ref.dtype)

        if nk == 1:
            finish(part)
        else:
            kk = pl.program_id(2)

            @pl.when(kk == 0)
            def _():
                acc_ref[...] = part

            @pl.when(kk > 0)
            def _():
                acc_ref[...] += part

            @pl.when(kk == nk - 1)
            def _():
                finish(acc_ref[...])

    return pl.pallas_call(
        body, name=name, grid=grid, in_specs=in_specs, out_specs=out_spec, out_shape=out_shape,
        scratch_shapes=[pltpu.VMEM((tm, tn), F32)] if nk > 1 else [],
        compiler_params=_params(("parallel", "parallel", "arbitrary")),
    )(*operands)


def _rowwise(fn, rows, whole, outs, sums=(), *, name, tm=256):
    m = rows[0].shape[-2]
    tm = _pick_rows(m, tm)
    n_rows, n_whole, n_outs, n_sums = len(rows), len(whole), len(outs), len(sums)

    def rspec(shape):
        lead = len(shape) - 2
        return pl.BlockSpec(tuple(shape[:-2]) + (tm, shape[-1]), lambda i: (0,) * lead + (i, 0))

    def wspec(shape):
        return pl.BlockSpec(tuple(shape), lambda i: (0,) * len(shape))

    def body(*refs):
        vals = [r[...] for r in refs[:n_rows + n_whole]]
        out_refs = refs[n_rows + n_whole:n_rows + n_whole + n_outs]
        sum_refs = refs[n_rows + n_whole + n_outs:]
        res = fn(*vals)
        for ref, val in zip(out_refs, res[:n_outs]):
            ref[...] = val.astype(ref.dtype)
        if n_sums:
            @pl.when(pl.program_id(0) == 0)
            def _():
                for ref in sum_refs:
                    ref[...] = jnp.zeros(ref.shape, ref.dtype)

            for ref, val in zip(sum_refs, res[n_outs:]):
                ref[...] += val

    result = pl.pallas_call(
        body, name=name, grid=(m // tm,),
        in_specs=[rspec(r.shape) for r in rows] + [wspec(w.shape) for w in whole],
        out_specs=[rspec(s) for s, _ in outs] + [wspec(s) for s, _ in sums],
        out_shape=[jax.ShapeDtypeStruct(s, d) for s, d in list(outs) + list(sums)],
        compiler_params=_params(("arbitrary",)),
    )(*rows, *whole)
    return result


def _sigmoid(v):
    return jax.nn.sigmoid(v)


def _col_sum(v):
    return jnp.sum(v, axis=0, keepdims=True)


def _ln_stats(z):
    mu = jnp.mean(z, axis=-1, keepdims=True)
    zc = z - mu
    var = jnp.mean(zc * zc, axis=-1, keepdims=True)
    rstd = lax.rsqrt(var + LN_EPS)
    return zc * rstd, rstd


def _ln_fwd_fn(xin, h, gain, bias):
    xhat, _ = _ln_stats(ALPHA * xin + h)
    y = xhat * gain + bias
    return y, y


def _ple_ln_fwd_fn(xin, pg, pu, gain, bias):
    xhat, _ = _ln_stats(ALPHA * xin + _sigmoid(pg) * pu)
    y = xhat * gain + bias
    return y, y


def _ln_dz(dy, z, gain):
    xhat, rstd = _ln_stats(z)
    dxhat = dy * gain
    dz = rstd * (dxhat - jnp.mean(dxhat, axis=-1, keepdims=True)
                 - xhat * jnp.mean(dxhat * xhat, axis=-1, keepdims=True))
    return dz, _col_sum(dy * xhat), _col_sum(dy)


def _ln_bwd_fn(dy, xin, h, gain):
    dz, dgain, dbias = _ln_dz(dy, ALPHA * xin + h, gain)
    return ALPHA * dz, dz, dgain, dbias, _col_sum(dz)


def _ple_ln_bwd_fn(dy, xin, pg, pu, gain):
    sg = _sigmoid(pg)
    dz, dgain, dbias = _ln_dz(dy, ALPHA * xin + sg * pu, gain)
    dpg = dz * pu * sg * (1.0 - sg)
    return ALPHA * dz, dpg, dz * sg, dgain, dbias, _col_sum(dpg)


def _swiglu_fwd_fn(gu):
    hid = gu.shape[-1] // 2
    gate, up = gu[:, :hid], gu[:, hid:]
    return (gate * _sigmoid(gate) * up,)


def _swiglu_bwd_fn(gu, dact):
    hid = gu.shape[-1] // 2
    gate, up = gu[:, :hid], gu[:, hid:]
    sg = _sigmoid(gate)
    dgate = dact * up * sg * (1.0 + gate * (1.0 - sg))
    dup = dact * gate * sg
    return (jnp.concatenate([dgate, dup], axis=-1),)


def _loss_fn(y, target):
    err = y - target
    inv = 1.0 / y.shape[-1]
    part = 0.5 * inv * jnp.sum(jnp.sum(err * err, axis=-1, keepdims=True), axis=0, keepdims=True)
    return err * inv, jnp.broadcast_to(part, (1, LANES))


def _adam_fn(w, mom, vel, p_own, p_sib):
    g = p_own + p_sib
    m_new = ADAM_B1 * mom + (1.0 - ADAM_B1) * g
    v_new = ADAM_B2 * vel + (1.0 - ADAM_B2) * (g * g)
    m_hat = m_new / (1.0 - ADAM_B1 ** ADAM_STEP)
    v_hat = v_new / (1.0 - ADAM_B2 ** ADAM_STEP)
    delta = -ADAM_LR * (m_hat / (jnp.sqrt(v_hat) + ADAM_EPS) + ADAM_WD * w)
    return g, delta, m_new, v_new


def _sum_slots_fn(slots):
    acc = slots[0].astype(F32)
    for s in range(1, slots.shape[0]):
        acc = acc + slots[s].astype(F32)
    return (acc,)


def _hdot(a, b):
    return lax.dot_general(a, b, (((1,), (0,)), ((), ())), precision=lax.Precision.HIGHEST,
                           preferred_element_type=F32)


def _hdot_nt(a, b):
    return lax.dot_general(a, b, (((1,), (1,)), ((), ())), precision=lax.Precision.HIGHEST,
                           preferred_element_type=F32)


def _hdot_tn(a, b):
    return lax.dot_general(a, b, (((0,), (0,)), ((), ())), precision=lax.Precision.HIGHEST,
                           preferred_element_type=F32)


def _dot(a, b):
    return lax.dot_general(a.astype(BF16), b.astype(BF16), (((1,), (0,)), ((), ())), preferred_element_type=F32)


def _dot_nt(a, b):
    return lax.dot_general(a.astype(BF16), b.astype(BF16), (((1,), (1,)), ((), ())), preferred_element_type=F32)


def _dot_tn(a, b):
    return lax.dot_general(a.astype(BF16), b.astype(BF16), (((0,), (0,)), ((), ())), preferred_element_type=F32)


def _hg_masks():
    c = HG_CHUNK
    row = lax.broadcasted_iota(jnp.int32, (c, c), 0)
    col = lax.broadcasted_iota(jnp.int32, (c, c), 1)
    base = row & (-HG_SUB)
    return row, col, base, col <= row, col < base


def _hg_gates(qr, fr, alb):
    lbound = _sigmoid(alb[0:1, :] - alb[1:2, :])
    sig = _sigmoid(fr)
    forget = lbound + (1.0 - lbound) * sig
    kk = (1.0 - lbound) * _sigmoid(-fr)
    qt = qr * _sigmoid(qr) * (HG_DK ** -0.5)
    return qt, kk, jnp.log(forget), lbound, sig, forget


def _hg_scores(qt, kk, g):
    c, nsub = HG_CHUNK, HG_CHUNK // HG_SUB
    row, col, base, causal, below = _hg_masks()
    b = _hdot(causal.astype(F32), g)
    rr = _hdot(below.astype(F32), g)
    bq = b - rr
    qh = qt * jnp.exp(bq)
    edecs = [None]
    parts = [jnp.zeros((HG_SUB, c), F32)]
    for i in range(1, nsub):
        edec = jnp.exp(jnp.minimum(rr[i * HG_SUB:i * HG_SUB + 1, :] - b, 0.0))
        edecs.append(edec)
        parts.append(_hdot_nt(qh[i * HG_SUB:(i + 1) * HG_SUB, :], kk * edec))
    a = jnp.where(below, jnp.concatenate(parts, axis=0), 0.0)
    b3 = b.reshape(nsub, HG_SUB, HG_DK)
    q3 = qt.reshape(nsub, HG_SUB, HG_DK)
    k3 = kk.reshape(nsub, HG_SUB, HG_DK)
    for j in range(HG_SUB):
        e = jnp.exp(jnp.minimum(b3 - b3[:, j:j + 1, :], 0.0))
        colv = jnp.sum(q3 * e * k3[:, j:j + 1, :], axis=-1, keepdims=True).reshape(c, 1)
        a = jnp.where(col == base + j, colv, a)
    a = jnp.where(causal, a, 0.0)
    return a, b, bq, qh, edecs, (b3, q3, k3)


def _hg_norm(o, gr, gain):
    r = lax.rsqrt(jnp.mean(o * o, axis=-1, keepdims=True) + RMS_EPS)
    sg = _sigmoid(gr)
    return o * r * gain, r, sg


def _hgrn2_fwd(proj, alb, gain, *, rb):
    m, d4 = proj.shape
    d = d4 // 4
    heads = d // HG_DK
    rb = min(rb, m)
    cpb = rb // HG_CHUNK
    nrb = m // rb

    def body(q_ref, f_ref, v_ref, g_ref, alb_ref, gain_ref, o_ref, og_ref, st_ref, state):
        @pl.when(pl.program_id(1) == 0)
        def _():
            state[...] = jnp.zeros(state.shape, F32)

        def chunk(ci, carry):
            sl = pl.ds(pl.multiple_of(ci * HG_CHUNK, HG_CHUNK), HG_CHUNK)
            qt, kk, g, _, _, _ = _hg_gates(q_ref[sl, :], f_ref[sl, :], alb_ref[...])
            v = v_ref[sl, :]
            st = state[...]
            st_ref[ci] = st
            a, b, _, _, _, _ = _hg_scores(qt, kk, g)
            o = _hdot(a, v) + _hdot_nt(qt * jnp.exp(b), st)
            b_last = b[HG_CHUNK - 1:HG_CHUNK, :]
            state[...] = st * jnp.exp(b_last) + _hdot_tn(v, kk * jnp.exp(b_last - b))
            o_ref[sl, :] = o
            n, _, sg = _hg_norm(o, g_ref[sl, :], gain_ref[...])
            og_ref[sl, :] = (n * g_ref[sl, :] * sg).astype(og_ref.dtype)
            return carry

        lax.fori_loop(0, cpb, chunk, 0)

    def col(cidx):
        return pl.BlockSpec((rb, HG_DK), lambda h, r: (r, cidx * heads + h))

    return pl.pallas_call(
        body, name="hgrn2_fwd", grid=(heads, nrb),
        in_specs=[col(0), col(1), col(2), col(3),
                  pl.BlockSpec((2, HG_DK), lambda h, r: (0, h)),
                  pl.BlockSpec((1, HG_DK), lambda h, r: (0, 0))],
        out_specs=[pl.BlockSpec((rb, HG_DK), lambda h, r: (r, h)),
                   pl.BlockSpec((rb, HG_DK), lambda h, r: (r, h)),
                   pl.BlockSpec((None, cpb, HG_DK, HG_DK), lambda h, r: (h, r, 0, 0))],
        out_shape=[jax.ShapeDtypeStruct((m, d), F32), jax.ShapeDtypeStruct((m, d), BF16),
                   jax.ShapeDtypeStruct((heads, m // HG_CHUNK, HG_DK, HG_DK), F32)],
        scratch_shapes=[pltpu.VMEM((HG_DK, HG_DK), F32)],
        compiler_params=_params(("parallel", "arbitrary")),
    )(proj, proj, proj, proj, alb, gain)


def _hgrn2_bwd(proj, o_pre, states, dog, alb, gain, *, rb):
    m, d4 = proj.shape
    d = d4 // 4
    heads = d // HG_DK
    rb = min(rb, m)
    cpb = rb // HG_CHUNK
    nrb = m // rb
    c, nsub = HG_CHUNK, HG_CHUNK // HG_SUB

    def body(q_ref, f_ref, v_ref, g_ref, o_ref, st_ref, dog_ref, alb_ref, gain_ref,
             dq_ref, df_ref, dv_ref, dg_ref, dalb_ref, dgain_ref, dstate, carry_ref):
        first = (pl.program_id(0) == 0) & (pl.program_id(1) == 0)

        @pl.when(first)
        def _():
            dgain_ref[...] = jnp.zeros(dgain_ref.shape, F32)

        @pl.when(pl.program_id(1) == 0)
        def _():
            dstate[...] = jnp.zeros(dstate.shape, F32)
            carry_ref[...] = jnp.zeros(carry_ref.shape, F32)
            dalb_ref[...] = jnp.zeros(dalb_ref.shape, F32)

        row, col, base, causal, below = _hg_masks()
        sub_iota = lax.broadcasted_iota(jnp.int32, (nsub, HG_SUB, HG_DK), 1)
        row_k = lax.broadcasted_iota(jnp.int32, (c, HG_DK), 0)
        upper = (col >= row).astype(F32)

        def chunk(step, carry):
            ci = cpb - 1 - step
            sl = pl.ds(pl.multiple_of(ci * HG_CHUNK, HG_CHUNK), HG_CHUNK)
            qr, fr, v, gr = q_ref[sl, :], f_ref[sl, :], v_ref[sl, :], g_ref[sl, :]
            qt, kk, g, lbound, sig, forget = _hg_gates(qr, fr, alb_ref[...])
            o = o_ref[sl, :]
            dogv = dog_ref[sl, :]
            gain_v = gain_ref[...]
            n, r, sg = _hg_norm(o, gr, gain_v)
            dgr = dogv * n * sg * (1.0 + gr * (1.0 - sg))
            dn = dogv * gr * sg
            dgain_ref[...] += _col_sum(dn * o * r)
            u = dn * gain_v
            d_o = r * u - o * (r * r * r) * jnp.mean(u * o, axis=-1, keepdims=True)
            st0 = st_ref[ci]
            dst = dstate[...]
            a, b, bq, qh, edecs, (b3, q3, k3) = _hg_scores(qt, kk, g)
            eb = jnp.exp(b)
            b_last = b[c - 1:c, :]
            kdl_dec = jnp.exp(b_last - b)
            kdl = kk * kdl_dec
            d_a = jnp.where(causal, _hdot_nt(d_o, v), 0.0)
            d_at = _hdot_nt(v, d_o)
            dv = _hdot_tn(a, d_o) + _hdot_nt(kdl, dst)
            dq = eb * _hdot(d_o, st0)
            dk = _hdot(v, dst) * kdl_dec
            d_a_below = jnp.where(below, d_a, 0.0)
            dq_parts = [jnp.zeros((HG_SUB, HG_DK), F32)]
            for i in range(1, nsub):
                lo, hi = i * HG_SUB, (i + 1) * HG_SUB
                dq_parts.append(_hdot(d_a_below[lo:hi, :], kk * edecs[i]))
                gi = _hdot(d_at[:, lo:hi], qh[lo:hi, :])
                dk = dk + jnp.where(row_k < lo, edecs[i] * gi, 0.0)
            dq = dq + jnp.concatenate(dq_parts, axis=0) * jnp.exp(bq)
            dq3 = jnp.zeros((nsub, HG_SUB, HG_DK), F32)
            dk3 = jnp.zeros((nsub, HG_SUB, HG_DK), F32)
            for j in range(HG_SUB):
                e = jnp.exp(jnp.minimum(b3 - b3[:, j:j + 1, :], 0.0))
                dcol = jnp.sum(jnp.where(col == base + j, d_a, 0.0), axis=-1, keepdims=True)
                t1 = dcol.reshape(nsub, HG_SUB, 1) * e
                dq3 = dq3 + t1 * k3[:, j:j + 1, :]
                dk3 = jnp.where(sub_iota == j, jnp.sum(t1 * q3, axis=1, keepdims=True), dk3)
            dq = dq + dq3.reshape(c, HG_DK)
            dk = dk + dk3.reshape(c, HG_DK)
            dstate[...] = dst * jnp.exp(b_last) + _hdot_tn(d_o, qt * eb)
            dglog = _hdot(upper, qt * dq - kk * dk) + carry_ref[...]
            carry_ref[...] = dglog[0:1, :]
            dforget = dglog / forget
            one_m_lb = 1.0 - lbound
            dsig = (dforget - dk) * one_m_lb
            sneg = _sigmoid(-fr)
            dlb = _col_sum(dforget * (1.0 - sig) - dk * sneg)
            dalb0 = dlb * lbound * one_m_lb
            dalb_ref[...] += jnp.concatenate([dalb0, -dalb0], axis=0)
            sq = _sigmoid(qr)
            dq_ref[sl, :] = (dq * (HG_DK ** -0.5) * sq * (1.0 + qr * (1.0 - sq))).astype(dq_ref.dtype)
            df_ref[sl, :] = (dsig * sig * (1.0 - sig)).astype(df_ref.dtype)
            dv_ref[sl, :] = dv.astype(dv_ref.dtype)
            dg_ref[sl, :] = dgr.astype(dg_ref.dtype)
            return carry

        lax.fori_loop(0, cpb, chunk, 0)

    def rev(r):
        return nrb - 1 - r

    def col(cidx):
        return pl.BlockSpec((rb, HG_DK), lambda h, r: (rev(r), cidx * heads + h))

    def head_rows():
        return pl.BlockSpec((rb, HG_DK), lambda h, r: (rev(r), h))

    return pl.pallas_call(
        body, name="hgrn2_bwd", grid=(heads, nrb),
        in_specs=[col(0), col(1), col(2), col(3), head_rows(),
                  pl.BlockSpec((None, cpb, HG_DK, HG_DK), lambda h, r: (h, rev(r), 0, 0)),
                  head_rows(),
                  pl.BlockSpec((2, HG_DK), lambda h, r: (0, h)),
                  pl.BlockSpec((1, HG_DK), lambda h, r: (0, 0))],
        out_specs=[head_rows(), head_rows(), head_rows(), head_rows(),
                   pl.BlockSpec((2, HG_DK), lambda h, r: (0, h)),
                   pl.BlockSpec((1, HG_DK), lambda h, r: (0, 0))],
        out_shape=[jax.ShapeDtypeStruct((m, d), BF16)] * 4
                  + [jax.ShapeDtypeStruct((2, d), F32), jax.ShapeDtypeStruct((1, HG_DK), F32)],
        scratch_shapes=[pltpu.VMEM((HG_DK, HG_DK), F32), pltpu.VMEM((1, HG_DK), F32)],
        compiler_params=_params(("arbitrary", "arbitrary")),
    )(proj, proj, proj, proj, o_pre, states, dog, alb, gain)


def _swa_probs(qh, kp, kc, sink, slope, has_prev):
    qi = lax.broadcasted_iota(jnp.int32, (WINDOW, WINDOW), 0)
    si = lax.broadcasted_iota(jnp.int32, (WINDOW, WINDOW), 1)
    scale = ATT_HD ** -0.5
    dist_c = (qi - si).astype(F32)
    s_p = _dot_nt(qh, kp) * scale - slope * (dist_c + float(WINDOW))
    s_c = _dot_nt(qh, kc) * scale - slope * dist_c
    s_p = jnp.where((si > qi) & has_prev, s_p, NEG)
    s_c = jnp.where(si <= qi, s_c, NEG)
    mx = jnp.maximum(jnp.maximum(jnp.max(s_p, axis=-1, keepdims=True), jnp.max(s_c, axis=-1, keepdims=True)), sink)
    e_p, e_c, e_s = jnp.exp(s_p - mx), jnp.exp(s_c - mx), jnp.exp(sink - mx)
    inv = 1.0 / (jnp.sum(e_p, axis=-1, keepdims=True) + jnp.sum(e_c, axis=-1, keepdims=True) + e_s)
    return e_p * inv, e_c * inv, e_s * inv


def _slope(h, n_heads):
    return float(2.0 ** (-8.0 * (h + 1) / n_heads))


def _swa_fwd(q, kv, sinks):
    m, d = q.shape
    n_heads = d // ATT_HD
    kvh = n_heads // ATT_G
    kd = kvh * ATT_HD
    nb = m // WINDOW

    def body(q_ref, kvp_ref, kvc_ref, sink_ref, o_ref):
        has_prev = pl.program_id(0) > 0
        qv, kvp, kvc = q_ref[...], kvp_ref[...], kvc_ref[...]
        outs = []
        for h in range(n_heads):
            kh = h // ATT_G
            ks = slice(kh * ATT_HD, (kh + 1) * ATT_HD)
            vs = slice(kd + kh * ATT_HD, kd + (kh + 1) * ATT_HD)
            p_p, p_c, _ = _swa_probs(qv[:, h * ATT_HD:(h + 1) * ATT_HD], kvp[:, ks], kvc[:, ks],
                                     sink_ref[:, h:h + 1], _slope(h, n_heads), has_prev)
            outs.append(_dot(p_p, kvp[:, vs]) + _dot(p_c, kvc[:, vs]))
        o_ref[...] = jnp.concatenate(outs, axis=-1).astype(o_ref.dtype)

    return pl.pallas_call(
        body, name="swa_fwd", grid=(nb,),
        in_specs=[pl.BlockSpec((WINDOW, d), lambda n: (n, 0)),
                  pl.BlockSpec((WINDOW, 2 * kd), lambda n: (jnp.maximum(n - 1, 0), 0)),
                  pl.BlockSpec((WINDOW, 2 * kd), lambda n: (n, 0)),
                  pl.BlockSpec((1, n_heads), lambda n: (0, 0))],
        out_specs=pl.BlockSpec((WINDOW, d), lambda n: (n, 0)),
        out_shape=jax.ShapeDtypeStruct((m, d), BF16),
        compiler_params=_params(("arbitrary",)),
    )(q, kv, kv, sinks)


def _swa_bwd(q, kv, sinks, dao):
    m, d = q.shape
    n_heads = d // ATT_HD
    kvh = n_heads // ATT_G
    kd = kvh * ATT_HD
    nb = m // WINDOW
    scale = ATT_HD ** -0.5

    def body(q_ref, kvp_ref, kvc_ref, sink_ref, do_ref, dq_ref, dkvc_ref, dkvp_ref, dqsum_ref, dsink_ref):
        @pl.when(pl.program_id(0) == 0)
        def _():
            dqsum_ref[...] = jnp.zeros(dqsum_ref.shape, F32)
            dsink_ref[...] = jnp.zeros(dsink_ref.shape, F32)

        has_prev = pl.program_id(0) > 0
        qv, kvp, kvc, dov = q_ref[...], kvp_ref[...], kvc_ref[...], do_ref[...]
        lane_h = lax.broadcasted_iota(jnp.int32, (1, n_heads), 1)
        dsink = jnp.zeros((1, n_heads), F32)
        dq_parts, dk_p, dk_c, dv_p, dv_c = [], [], [], [], []
        for kh in range(kvh):
            ks = slice(kh * ATT_HD, (kh + 1) * ATT_HD)
            vs = slice(kd + kh * ATT_HD, kd + (kh + 1) * ATT_HD)
            kp, kc, vp, vc = kvp[:, ks], kvc[:, ks], kvp[:, vs], kvc[:, vs]
            acc = [jnp.zeros((WINDOW, ATT_HD), F32) for _ in range(4)]
            for gq in range(ATT_G):
                h = kh * ATT_G + gq
                hs = slice(h * ATT_HD, (h + 1) * ATT_HD)
                qh, doh = qv[:, hs], dov[:, hs]
                p_p, p_c, p_s = _swa_probs(qh, kp, kc, sink_ref[:, h:h + 1], _slope(h, n_heads), has_prev)
                dp_p, dp_c = _dot_nt(doh, vp), _dot_nt(doh, vc)
                delta = jnp.sum(p_p * dp_p, axis=-1, keepdims=True) + jnp.sum(p_c * dp_c, axis=-1, keepdims=True)
                ds_p, ds_c = p_p * (dp_p - delta), p_c * (dp_c - delta)
                dsink = dsink + jnp.where(lane_h == h, -_col_sum(p_s * delta), 0.0)
                dq_parts.append((_dot(ds_p, kp) + _dot(ds_c, kc)) * scale)
                acc[0] = acc[0] + _dot_tn(ds_p, qh) * scale
                acc[1] = acc[1] + _dot_tn(ds_c, qh) * scale
                acc[2] = acc[2] + _dot_tn(p_p, doh)
                acc[3] = acc[3] + _dot_tn(p_c, doh)
            dk_p.append(acc[0])
            dk_c.append(acc[1])
            dv_p.append(acc[2])
            dv_c.append(acc[3])
        dq = jnp.concatenate(dq_parts, axis=-1)
        dq_ref[...] = dq.astype(dq_ref.dtype)
        dqsum_ref[...] += _col_sum(dq)
        dsink_ref[...] += dsink
        dkvc_ref[...] = jnp.concatenate(dk_c + dv_c, axis=-1)
        dkvp_ref[...] = jnp.concatenate(dk_p + dv_p, axis=-1)

    return pl.pallas_call(
        body, name="swa_bwd", grid=(nb,),
        in_specs=[pl.BlockSpec((WINDOW, d), lambda n: (n, 0)),
                  pl.BlockSpec((WINDOW, 2 * kd), lambda n: (jnp.maximum(n - 1, 0), 0)),
                  pl.BlockSpec((WINDOW, 2 * kd), lambda n: (n, 0)),
                  pl.BlockSpec((1, n_heads), lambda n: (0, 0)),
                  pl.BlockSpec((WINDOW, d), lambda n: (n, 0))],
        out_specs=[pl.BlockSpec((WINDOW, d), lambda n: (n, 0)),
                   pl.BlockSpec((WINDOW, 2 * kd), lambda n: (n, 0)),
                   pl.BlockSpec((WINDOW, 2 * kd), lambda n: (n, 0)),
                   pl.BlockSpec((1, d), lambda n: (0, 0)),
                   pl.BlockSpec((1, n_heads), lambda n: (0, 0))],
        out_shape=[jax.ShapeDtypeStruct((m, d), BF16), jax.ShapeDtypeStruct((m, 2 * kd), F32),
                   jax.ShapeDtypeStruct((m, 2 * kd), F32), jax.ShapeDtypeStruct((1, d), F32),
                   jax.ShapeDtypeStruct((1, n_heads), F32)],
        compiler_params=_params(("arbitrary",)),
    )(q, kv, kv, sinks, dao)


def _kv_grad_combine(dkv_cur, dkv_prev):
    m, w = dkv_cur.shape
    nb = m // WINDOW

    def body(cur_ref, nxt_ref, o_ref, sum_ref):
        @pl.when(pl.program_id(0) == 0)
        def _():
            sum_ref[...] = jnp.zeros(sum_ref.shape, F32)

        total = cur_ref[...] + jnp.where(pl.program_id(0) < nb - 1, nxt_ref[...], 0.0)
        o_ref[...] = total.astype(o_ref.dtype)
        sum_ref[...] += _col_sum(total)

    return pl.pallas_call(
        body, name="kv_grad_combine", grid=(nb,),
        in_specs=[pl.BlockSpec((WINDOW, w), lambda n: (n, 0)),
                  pl.BlockSpec((WINDOW, w), lambda n: (jnp.minimum(n + 1, nb - 1), 0))],
        out_specs=[pl.BlockSpec((WINDOW, w), lambda n: (n, 0)), pl.BlockSpec((1, w), lambda n: (0, 0))],
        out_shape=[jax.ShapeDtypeStruct((m, w), BF16), jax.ShapeDtypeStruct((1, w), F32)],
        compiler_params=_params(("arbitrary",)),
    )(dkv_cur, dkv_prev)


def _row(v):
    return v.reshape(1, -1)


def _local_step(x, p, target, wget, grad_sink, ln_gain, ln_bias, alb, norm_gain, kv_b, b_q, sinks, b_out, ple_b):
    gs = {}
    gains = [[_row(ln_gain[i, j]) for j in range(3)] for i in range(DEPTH)]
    biases = [[_row(ln_bias[i, j]) for j in range(3)] for i in range(DEPTH)]
    sd = x.shape
    pending = [None]

    def mm(a, b, lb=0, **kw):
        after, pending[0] = pending[0], None
        return _mm(a, b, lb=lb, after=after, **kw)

    def ln_fwd(xin, h, i, j, nm):
        return _rowwise(_ln_fwd_fn, [xin, h], [gains[i][j], biases[i][j]], [(sd, F32), (sd, BF16)], name=nm)

    def tail_fwd(xa, i):
        gu = _mm(xa[1], wget("ffn_w_gate_up", i, xa[1]), lb=0, name=f"ffn_up{i}")
        act, = _rowwise(_swiglu_fwd_fn, [gu], [], [((sd[0], gu.shape[1] // 2), BF16)], name=f"swiglu{i}", tm=128)
        f = _mm(act, wget("ffn_w_down", i, act), lb=0, name=f"ffn_down{i}")
        xb = ln_fwd(xa[0], f, i, 1, f"ln_ffn{i}")
        pg = _mm(xb[1], wget("ple_w_gate", i, act), lb=0, bias=_row(ple_b[i]), name=f"ple_gate{i}")
        pu = _mm(p[i], wget("ple_w_up", i, act), lb=0, name=f"ple_up{i}")
        xc = _rowwise(_ple_ln_fwd_fn, [xb[0], pg, pu], [gains[i][2], biases[i][2]], [(sd, F32), (sd, BF16)],
                      name=f"ln_ple{i}")
        return dict(xa=xa, gu=gu, act=act, f=f, xb=xb, pg=pg, pu=pu), xc

    def tail_bwd(dxc, sv, i):
        xa, xb = sv["xa"], sv["xb"]
        dxb_part, dpg, dpu, dg2, db2, dbg = _rowwise(
            _ple_ln_bwd_fn, [dxc, xb[0], sv["pg"], sv["pu"]], [gains[i][2]],
            [(sd, F32), (sd, BF16), (sd, BF16)], [((1, sd[1]), F32)] * 3, name=f"ln_ple_bwd{i}")
        gs[f"ple_b_{i}"] = dbg
        gs[f"ln_gain_{i}_2"], gs[f"ln_bias_{i}_2"] = dg2, db2
        grad_of("ple_w_gate", i, xb[1], dpg)
        grad_of("ple_w_up", i, p[i], dpu)
        dxb = mm(dpg, wget("ple_w_gate", i, None), tb=True, add=dxb_part, name=f"ple_gate_dx{i}")
        dxa_part, df, dg1, db1, _ = _rowwise(
            _ln_bwd_fn, [dxb, xa[0], sv["f"]], [gains[i][1]],
            [(sd, F32), (sd, BF16)], [((1, sd[1]), F32)] * 3, name=f"ln_ffn_bwd{i}")
        gs[f"ln_gain_{i}_1"], gs[f"ln_bias_{i}_1"] = dg1, db1
        grad_of("ffn_w_down", i, sv["act"], df)
        dact = mm(df, wget("ffn_w_down", i, None), tb=True, name=f"ffn_down_dx{i}")
        dgu, = _rowwise(_swiglu_bwd_fn, [sv["gu"], dact], [], [(sv["gu"].shape, BF16)], name=f"swiglu_bwd{i}", tm=128)
        grad_of("ffn_w_gate_up", i, xa[1], dgu)
        return mm(dgu, wget("ffn_w_gate_up", i, None), tb=True, add=dxa_part, name=f"ffn_up_dx{i}")

    def grad_of(nm, i, act, dout):
        grad = mm(act, dout, lb=None, ta=True, out_dtype=BF16, out_layers=1, out_layer=0, name=f"grad_{nm}{i}")
        token = grad_sink(nm, i, grad)
        if token is not None:
            pending[0] = token

    def mixer_ln_bwd(dxa, xin, h, i):
        dx_part, dh, dg0, db0, dhsum = _rowwise(
            _ln_bwd_fn, [dxa, xin, h], [gains[i][0]],
            [(sd, F32), (sd, BF16)], [((1, sd[1]), F32)] * 3, name=f"ln_mix_bwd{i}")
        gs[f"ln_gain_{i}_0"], gs[f"ln_bias_{i}_0"] = dg0, db0
        return dx_part, dh, dhsum

    proj = _mm(x, wget("a_w_in", 0, None), lb=0, name="hg_proj")
    o_pre, og, states = _hgrn2_fwd(proj, alb, norm_gain, rb=HG_ROWS)
    h0 = _mm(og, wget("a_w_out", 0, og), lb=0, name="hg_out")
    x1 = ln_fwd(x, h0, 0, 0, "ln_mix0")
    sv0, x3 = tail_fwd(x1, 0)
    kv = _mm(x3[1], wget("kv_w", 0, x3[1]), lb=0, bias=_row(kv_b), name="kv_proj")
    q = _mm(x3[1], wget("b_w_q", 0, x3[1]), lb=0, bias=b_q, name="q_proj")
    ao = _swa_fwd(q, kv, sinks)
    h1 = _mm(ao, wget("b_w_out", 0, x3[1]), lb=0, bias=b_out, name="att_out")
    x4 = ln_fwd(x3[0], h1, 1, 0, "ln_mix1")
    sv1, y = tail_fwd(x4, 1)
    dy, loss = _rowwise(_loss_fn, [y[0], target], [], [(sd, F32)], [((1, LANES), F32)], name="loss")

    dx4 = tail_bwd(dy, sv1, 1)
    dx3_part, dh1, dh1sum = mixer_ln_bwd(dx4, x3[0], h1, 1)
    gs["b_out"] = dh1sum
    grad_of("b_w_out", 0, ao, dh1)
    dao = mm(dh1, wget("b_w_out", 0, None), tb=True, name="att_out_dx")
    dq, dkv_cur, dkv_prev, dqsum, dsinks = _swa_bwd(q, kv, sinks, dao)
    gs["b_q"], gs["sinks"] = dqsum, dsinks
    dkv, dkvsum = _kv_grad_combine(dkv_cur, dkv_prev)
    gs["kv_b"] = dkvsum
    grad_of("b_w_q", 0, x3[1], dq)
    grad_of("kv_w", 0, x3[1], dkv)
    dx3 = mm(dq, wget("b_w_q", 0, None), tb=True, add=dx3_part, name="q_proj_dx")
    dx3 = mm(dkv, wget("kv_w", 0, None), tb=True, add=dx3, name="kv_proj_dx")
    dx1 = tail_bwd(dx3, sv0, 0)
    dx_part, dh0, _ = mixer_ln_bwd(dx1, x, h0, 0)
    grad_of("a_w_out", 0, og, dh0)
    dog = mm(dh0, wget("a_w_out", 0, None), tb=True, name="hg_out_dx")
    dqr, dfr, dvr, dgr, dalb, dgain = _hgrn2_bwd(proj, o_pre, states, dog, alb, norm_gain, rb=HG_ROWS)
    gs["alb"], gs["norm_gain"] = dalb, dgain
    dproj = jnp.concatenate([dqr, dfr, dvr, dgr], axis=1)
    grad_of("a_w_in", 0, x, dproj)
    grad_x = mm(dproj, wget("a_w_in", 0, None), tb=True, add=dx_part, name="hg_proj_dx")
    return loss, grad_x, gs


HBM_SPEC = pl.BlockSpec(memory_space=pl.ANY)
HBM_ONLY = pl.BlockSpec(memory_space=pltpu.HBM)
SEM_SPEC = pl.BlockSpec(memory_space=pltpu.SEMAPHORE)
SIDE_EFFECT = pltpu.SideEffectType.DATAFLOW_SIDE_EFFECTING


def _piece(ref, kind, j):
    _, r, c = ref.shape
    if kind == "row":
        return ref.at[:, pl.ds(j * (r // N_CHIPS), r // N_CHIPS), :]
    return ref.at[:, :, pl.ds(j * (c // N_CHIPS), c // N_CHIPS)]


def _chip_of(j, c):
    return (j // 2, j % 2, c)


def _in_hbm(a):
    return pltpu.with_memory_space_constraint(a, pltpu.HBM)


class _Exchange:
    def __init__(self, mode, srcs, lands, kinds, layers, name):
        self.mode, self.kinds, self.layers, self.name, self.n = mode, kinds, layers, name, len(srcs)
        n = self.n
        sem_shape = pltpu.SemaphoreType.DMA((n * N_CHIPS,))

        def body(*refs):
            src_refs, land_refs = refs[:n], refs[n:2 * n]
            send_sems, recv_sems = refs[2 * n], refs[2 * n + 1]
            token = refs[-1]
            c = lax.axis_index("c")
            me = 2 * lax.axis_index("x") + lax.axis_index("y")
            for j in range(N_CHIPS):
                @pl.when(me == j)
                def _():
                    for a in range(n):
                        for t in range(N_CHIPS):
                            if t != j:
                                src, dst = self._ends(src_refs[a], land_refs[a], a, j, t)
                                pltpu.make_async_remote_copy(
                                    src_ref=src, dst_ref=dst, send_sem=send_sems.at[a * N_CHIPS + t], recv_sem=recv_sems.at[a * N_CHIPS + j],
                                    device_id=_chip_of(t, c), device_id_type=MESH).start()
            token[...] = jnp.zeros(token.shape, token.dtype)

        outs = pl.pallas_call(
            body, name=name + "_start",
            in_specs=[HBM_ONLY] * (2 * n),
            out_specs=[SEM_SPEC, SEM_SPEC] + [HBM_ONLY] * (2 * n) + [pl.BlockSpec(memory_space=pltpu.VMEM)],
            out_shape=[sem_shape, sem_shape] + [pltpu.HBM(a.shape, a.dtype) for a in list(srcs) + list(lands)]
                      + [jax.ShapeDtypeStruct((8, LANES), F32)],
            input_output_aliases={i: i + 2 for i in range(2 * n)},
            compiler_params=pltpu.CompilerParams(has_side_effects=SIDE_EFFECT),
        )(*[_in_hbm(a) for a in list(srcs) + list(lands)])
        self.send_sems, self.recv_sems = outs[0], outs[1]
        self.srcs, self.lands = list(outs[2:2 + n]), list(outs[2 + n:2 + 2 * n])
        self.token = outs[-1]

    def _ends(self, src, land, a, me_j, peer):
        if self.mode == "gather":
            return src, _piece(land, self.kinds[a], me_j)
        return _piece(src, self.kinds[a], peer), land.at[me_j, pl.ds(self.layers[a], 1)]

    def wait(self, after, lands=None):
        n = self.n
        lands = self.lands if lands is None else lands

        def body(*refs):
            src_refs, land_refs = refs[:n], refs[n:2 * n]
            send_sems, recv_sems = refs[2 * n], refs[2 * n + 1]
            local_sems = refs[-1]
            c = lax.axis_index("c")
            me = 2 * lax.axis_index("x") + lax.axis_index("y")
            for j in range(N_CHIPS):
                @pl.when(me == j)
                def _():
                    for a in range(n):
                        src, dst = self._ends(src_refs[a], land_refs[a], a, j, j)
                        pltpu.make_async_copy(src, dst, local_sems.at[a]).start()
            for j in range(N_CHIPS):
                @pl.when(me != j)
                def _():
                    for a in range(n):
                        sent, _ = self._ends(src_refs[a], land_refs[a], a, 0, j)
                        _, landed = self._ends(src_refs[a], land_refs[a], a, j, 0)
                        cp = pltpu.make_async_remote_copy(
                            src_ref=sent, dst_ref=landed, send_sem=send_sems.at[a * N_CHIPS + j], recv_sem=recv_sems.at[a * N_CHIPS + j],
                            device_id=_chip_of(j, c), device_id_type=MESH)
                        cp.wait_send()
                        cp.wait_recv()
            for j in range(N_CHIPS):
                @pl.when(me == j)
                def _():
                    for a in range(n):
                        src, dst = self._ends(src_refs[a], land_refs[a], a, j, j)
                        pltpu.make_async_copy(src, dst, local_sems.at[a]).wait()

        operands = [_in_hbm(a) for a in self.srcs + list(lands)] + [self.send_sems, self.recv_sems]
        in_specs = [HBM_ONLY] * (2 * n) + [SEM_SPEC, SEM_SPEC]
        if after is not None:
            operands.append(after)
            in_specs.append(HBM_SPEC)
        outs = pl.pallas_call(
            body, name=self.name + "_wait",
            in_specs=in_specs, out_specs=[HBM_ONLY] * (2 * n),
            out_shape=[pltpu.HBM(a.shape, a.dtype) for a in self.srcs + list(lands)],
            input_output_aliases={i: i for i in range(2 * n)},
            scratch_shapes=[pltpu.SemaphoreType.DMA((n,))],
            compiler_params=pltpu.CompilerParams(has_side_effects=SIDE_EFFECT),
        )(*operands)
        return list(outs[n:])


def _sibling_swap(arrays):
    n = len(arrays)

    def body(*refs):
        ins, outs = refs[:n], refs[n:2 * n]
        send_sems, recv_sems = refs[2 * n:]
        sibling = (lax.axis_index("x"), lax.axis_index("y"), 1 - lax.axis_index("c"))
        copies = [pltpu.make_async_remote_copy(src_ref=ins[a], dst_ref=outs[a], send_sem=send_sems.at[a],
                                               recv_sem=recv_sems.at[a], device_id=sibling, device_id_type=MESH)
                  for a in range(n)]
        for cp in copies:
            cp.start()
        for cp in copies:
            cp.wait()

    return pl.pallas_call(
        body, name="sibling_swap",
        in_specs=[HBM_SPEC] * n, out_specs=[HBM_SPEC] * n,
        out_shape=[jax.ShapeDtypeStruct(a.shape, a.dtype) for a in arrays],
        scratch_shapes=[pltpu.SemaphoreType.DMA((n,)), pltpu.SemaphoreType.DMA((n,))],
    )(*arrays)


def _gather_devices(vec):
    def body(in_ref, out_ref, send_sems, recv_sems, local_sem):
        x, y, c = lax.axis_index("x"), lax.axis_index("y"), lax.axis_index("c")
        me = 4 * x + 2 * y + c
        mine = pltpu.make_async_copy(in_ref, out_ref.at[me], local_sem)
        mine.start()
        copies = []
        for rel in range(1, N_DEV):
            peer = (x ^ (rel >> 2), y ^ ((rel >> 1) & 1), c ^ (rel & 1))
            copies.append(pltpu.make_async_remote_copy(
                src_ref=in_ref, dst_ref=out_ref.at[me], send_sem=send_sems.at[rel], recv_sem=recv_sems.at[rel],
                device_id=peer, device_id_type=MESH))
        for cp in copies:
            cp.start()
        for cp in copies:
            cp.wait()
        mine.wait()

    return pl.pallas_call(
        body, name="gather_small",
        in_specs=[HBM_SPEC], out_specs=HBM_SPEC,
        out_shape=jax.ShapeDtypeStruct((N_DEV,) + vec.shape, vec.dtype),
        scratch_shapes=[pltpu.SemaphoreType.DMA((N_DEV,)), pltpu.SemaphoreType.DMA((N_DEV,)),
                        pltpu.SemaphoreType.DMA],
    )(vec)


BIG = [("a_w_in", "col"), ("a_w_out", "row"), ("kv_w", "row"), ("b_w_q", "row"), ("b_w_out", "row"),
       ("ffn_w_gate_up", "col"), ("ffn_w_down", "row"), ("ple_w_up", "col"), ("ple_w_gate", "row")]
GATHER_GROUPS = [[("a_w_in", 0), ("small", 0)], [("a_w_out", 0), ("ffn_w_gate_up", 0)],
                 [("ffn_w_down", 0), ("ple_w_gate", 0), ("ple_w_up", 0)], [("kv_w", 0), ("b_w_q", 0), ("b_w_out", 0)],
                 [("ffn_w_gate_up", 1)], [("ffn_w_down", 1), ("ple_w_gate", 1), ("ple_w_up", 1)]]
SCATTER_GROUPS = [[("ple_w_gate", 1), ("ple_w_up", 1), ("ffn_w_down", 1)], [("ffn_w_gate_up", 1)],
                  [("b_w_out", 0), ("b_w_q", 0), ("kv_w", 0)], [("ple_w_gate", 0), ("ple_w_up", 0), ("ffn_w_down", 0)],
                  [("ffn_w_gate_up", 0), ("a_w_out", 0)], [("a_w_in", 0)]]
SMALL_SHARDED = ["ln_gain", "ln_bias", "a_lower_bound"]
SMALL_REPLICATED = ["a_norm_gain", "kv_b", "b_b_q", "b_sinks", "b_b_out", "ple_b_gate"]
WEIGHT_ORDER = ["a_w_in", "a_lower_bound", "a_norm_gain", "a_w_out", "kv_w", "kv_b", "b_w_q", "b_b_q", "b_sinks",
                "b_w_out", "b_b_out", "ffn_w_gate_up", "ffn_w_down", "ple_w_up", "ple_w_gate", "ple_b_gate",
                "ln_gain", "ln_bias"]


def _as3(a):
    return a.reshape((-1,) + a.shape[-2:]) if a.ndim >= 3 else a.reshape((1,) + a.shape)


def _pad_lanes(v):
    n = v.shape[-1]
    return jnp.pad(v, ((0, 0), (0, (-n) % LANES)))


def _adam_small_fn(w, mom, vel, g):
    return _adam_fn(w, mom, vel, g, jnp.zeros_like(g))[1:]


def _sum_rows_fn(slots):
    acc = slots[0]
    for s in range(1, slots.shape[0]):
        acc = acc + slots[s]
    return (acc,)


def kernel(x, p, a_w_in, a_lower_bound, a_norm_gain, a_w_out, kv_w, kv_b, b_w_q, b_b_q, b_sinks, b_w_out, b_b_out, ffn_w_gate_up, ffn_w_down, ple_w_up, ple_w_gate, ple_b_gate, ln_gain, ln_bias, loss_target, m_a_w_in, m_a_lower_bound, m_a_norm_gain, m_a_w_out, m_kv_w, m_kv_b, m_b_w_q, m_b_b_q, m_b_sinks, m_b_w_out, m_b_b_out, m_ffn_w_gate_up, m_ffn_w_down, m_ple_w_up, m_ple_w_gate, m_ple_b_gate, m_ln_gain, m_ln_bias, v_a_w_in, v_a_lower_bound, v_a_norm_gain, v_a_w_out, v_kv_w, v_kv_b, v_b_w_q, v_b_b_q, v_b_sinks, v_b_w_out, v_b_b_out, v_ffn_w_gate_up, v_ffn_w_down, v_ple_w_up, v_ple_w_gate, v_ple_b_gate, v_ln_gain, v_ln_bias):
    args = dict(locals())
    wts = {n: args[n] for n in WEIGHT_ORDER}
    mom = {n: args["m_" + n] for n in WEIGHT_ORDER}
    vel = {n: args["v_" + n] for n in WEIGHT_ORDER}
    chip = 2 * lax.axis_index("x") + lax.axis_index("y")
    d = x.shape[-1]
    dq = d // N_CHIPS

    kind_of = dict(BIG)
    kind_of["small"] = "col"
    shards = {}
    for n, _ in BIG:
        s3 = _as3(wts[n]).astype(BF16)
        for layer in range(s3.shape[0]):
            shards[(n, layer)] = s3[layer:layer + 1]
    shards[("small", 0)] = jnp.concatenate([wts[n].reshape(-1, dq) for n in SMALL_SHARDED], axis=0)[None]

    def full_shape(shape, kind):
        l, r, c = shape
        return (l, r * N_CHIPS, c) if kind == "row" else (l, r, c * N_CHIPS)

    gathers, where = [], {}
    for gi, group in enumerate(GATHER_GROUPS):
        srcs = [shards[k] for k in group]
        kinds = [kind_of[k[0]] for k in group]
        lands = [lax.empty(full_shape(s.shape, kd), s.dtype) for s, kd in zip(srcs, kinds)]
        gathers.append(_Exchange("gather", srcs, lands, kinds, [0] * len(group), f"gather{gi}"))
        for k in group:
            where[k] = gi
    all_started = jnp.concatenate([g.token for g in gathers[1:]], axis=0)
    ready = {}

    def wget(name, layer, after):
        key = (name, layer)
        if key not in ready:
            gi = where[key]
            outs = gathers[gi].wait(all_started if gi == 0 else after)
            for k, arr in zip(GATHER_GROUPS[gi], outs):
                ready[k] = arr
        return ready[key]

    small_full = wget("small", 0, None)[0]
    ln_gain_f = small_full[0:6].reshape(DEPTH, 3, d)
    ln_bias_f = small_full[6:12].reshape(DEPTH, 3, d)
    alb_f = small_full[12:14]

    group_of = {k: gi for gi, group in enumerate(SCATTER_GROUPS) for k in group}
    grads_done, zones, scatters = {}, {}, []

    def grad_sink(name, layer, grad):
        grads_done[(name, layer)] = grad
        gi = group_of[(name, layer)]
        group = SCATTER_GROUPS[gi]
        if not all(k in grads_done for k in group):
            return None
        for n, _ in group:
            if n not in zones:
                zones[n] = lax.empty((N_CHIPS,) + _as3(wts[n]).shape, BF16)
        ex = _Exchange("scatter", [grads_done[k] for k in group], [zones[k[0]] for k in group],
                       [kind_of[k[0]] for k in group], [k[1] for k in group], f"scatter{gi}")
        for k, zone in zip(group, ex.lands):
            zones[k[0]] = zone
        scatters.append((ex, group))
        return ex.token

    loss, grad_x, gs = _local_step(
        x[0], p[:, 0], loss_target[0], wget, grad_sink, ln_gain_f, ln_bias_f, alb_f, a_norm_gain, kv_b, b_b_q,
        b_sinks, b_b_out, ple_b_gate)
    loss = lax.psum(loss[0, 0], ("x", "y", "c"))

    for ex, group in scatters:
        outs = ex.wait(grad_x, lands=[zones[k[0]] for k in group])
        for k, zone in zip(group, outs):
            zones[k[0]] = zone
    partial = []
    for n, _ in BIG:
        s2 = zones[n].reshape(N_CHIPS, -1, zones[n].shape[-1])
        partial.append(_rowwise(_sum_slots_fn, [s2], [], [(s2.shape[1:], F32)], name=f"sum_{n}")[0])
    sibling = _sibling_swap(partial)
    res = {}
    for (n, _), own, sib in zip(BIG, partial, sibling):
        shp = wts[n].shape
        flat = lambda a: a.reshape(-1, shp[-1])
        out = _rowwise(_adam_fn, [flat(wts[n]), flat(mom[n]), flat(vel[n]), own, sib], [],
                       [(own.shape, F32)] * 4, name=f"adam_{n}")
        res[n] = [o.reshape(shp) for o in out]

    ln_g = jnp.concatenate([gs[f"ln_gain_{i}_{j}"] for i in range(DEPTH) for j in range(3)], axis=0)
    ln_b = jnp.concatenate([gs[f"ln_bias_{i}_{j}"] for i in range(DEPTH) for j in range(3)], axis=0)
    ple_bg = jnp.concatenate([gs[f"ple_b_{i}"] for i in range(DEPTH)], axis=0)
    small_list = [ln_g.reshape(1, -1), ln_b.reshape(1, -1), gs["alb"].reshape(1, -1), gs["norm_gain"],
                  gs["kv_b"], gs["b_q"], _pad_lanes(gs["sinks"]), gs["b_out"], ple_bg.reshape(1, -1)]
    small_vec = jnp.concatenate(small_list, axis=1)
    everyone = _gather_devices(small_vec)
    total, = _rowwise(_sum_rows_fn, [everyone], [], [(small_vec.shape, F32)], name="sum_small")
    offs, pos = [], 0
    for v in small_list:
        offs.append((pos, v.shape[1]))
        pos += v.shape[1]

    def seg(k):
        return total[0, offs[k][0]:offs[k][0] + offs[k][1]]

    def my_cols(full, rows):
        return lax.dynamic_slice_in_dim(full.reshape(rows, N_CHIPS, dq), chip, 1, axis=1).reshape(rows, dq)

    n_sink = b_sinks.shape[-1]
    small_grads = {
        "ln_gain": my_cols(seg(0), 6).reshape(ln_gain.shape), "ln_bias": my_cols(seg(1), 6).reshape(ln_bias.shape),
        "a_lower_bound": my_cols(seg(2), 2), "a_norm_gain": seg(3).reshape(a_norm_gain.shape),
        "kv_b": seg(4).reshape(kv_b.shape), "b_b_q": seg(5).reshape(b_b_q.shape),
        "b_sinks": seg(6)[:n_sink].reshape(b_sinks.shape), "b_b_out": seg(7).reshape(b_b_out.shape),
        "ple_b_gate": seg(8).reshape(ple_b_gate.shape)}
    names = SMALL_SHARDED + SMALL_REPLICATED
    pack = lambda dct: _pad_lanes(jnp.concatenate([dct[n].reshape(1, -1) for n in names], axis=1))
    g_pack = pack(small_grads)
    upd = _rowwise(_adam_small_fn, [pack(wts), pack(mom), pack(vel), g_pack], [], [(g_pack.shape, F32)] * 3,
                   name="adam_small")
    pos = 0
    for n in names:
        size = wts[n].size
        res[n] = [small_grads[n]] + [u[0, pos:pos + size].reshape(wts[n].shape) for u in upd]
        pos += size

    outs = [loss, grad_x[None]]
    for k in range(4):
        outs += [res[n][k] for n in WEIGHT_ORDER]
    return tuple(outs)
```

```python
import functools

import jax
import jax.numpy as jnp
from jax import lax
from jax.experimental import pallas as pl
from jax.experimental.pallas import tpu as pltpu

F32 = jnp.float32
BF16 = jnp.bfloat16
MESH = pl.DeviceIdType.MESH

LANES = 128
HG_DK = 128
HG_CHUNK = 64
HG_SUB = 16
HG_ROWS = 512
ATT_HD = 64
ATT_G = 4
WINDOW = 128
DEPTH = 2
ALPHA = (2.0 * DEPTH) ** 0.25
LN_EPS = 1e-5
RMS_EPS = 1e-6
ADAM_LR, ADAM_B1, ADAM_B2, ADAM_EPS, ADAM_WD, ADAM_STEP = 0.001, 0.9, 0.999, 1e-08, 0.01, 10
N_CHIPS = 4
N_DEV = 8
VMEM_LIMIT = 56 * 1024 * 1024
NEG = -1e30


def _pick(n, cap):
    best = None
    for d in range(LANES, min(n, cap) + 1, LANES):
        if n % d == 0:
            best = d
    return n if best is None else best


def _pick_rows(m, cap):
    best = None
    for d in range(16, min(m, cap) + 1, 16):
        if m % d == 0:
            best = d
    return m if best is None else best


def _params(sem):
    return pltpu.CompilerParams(dimension_semantics=sem, vmem_limit_bytes=VMEM_LIMIT)


def _mm(a, b, *, name, la=None, lb=None, ta=False, tb=False, bias=None, add=None, out_dtype=F32,
        out_layers=None, out_layer=None, after=None, caps=(512, 1536, 2048)):
    ar, ac = a.shape[-2:]
    br, bc = b.shape[-2:]
    m, k = (ac, ar) if ta else (ar, ac)
    k2, n = (bc, br) if tb else (br, bc)
    assert k == k2, (a.shape, b.shape, ta, tb)
    tm, tn, tk = _pick(m, caps[0]), _pick(n, caps[1]), _pick(k, caps[2])
    nk = k // tk
    grid = (m // tm, n // tn, nk)

    def spec(block, idx, layer):
        if layer is None:
            return pl.BlockSpec(block, idx)
        return pl.BlockSpec((None,) + block, lambda i, j, kk: (layer,) + idx(i, j, kk))

    a_spec = spec((tk, tm), lambda i, j, kk: (kk, i), la) if ta else spec((tm, tk), lambda i, j, kk: (i, kk), la)
    b_spec = spec((tn, tk), lambda i, j, kk: (j, kk), lb) if tb else spec((tk, tn), lambda i, j, kk: (kk, j), lb)
    in_specs, operands = [a_spec, b_spec], [a, b]
    if bias is not None:
        in_specs.append(pl.BlockSpec((1, tn), lambda i, j, kk: (0, j)))
        operands.append(bias)
    if add is not None:
        in_specs.append(pl.BlockSpec((tm, tn), lambda i, j, kk: (i, j)))
        operands.append(add)
    if after is not None:
        in_specs.append(pl.BlockSpec(memory_space=pl.ANY))
        operands.append(after)
    if out_layers is None:
        out_shape = jax.ShapeDtypeStruct((m, n), out_dtype)
    else:
        out_shape = jax.ShapeDtypeStruct((out_layers, m, n), out_dtype)
    out_spec = spec((tm, tn), lambda i, j, kk: (i, j), out_layer)
    dims = (((0 if ta else 1,), (1 if tb else 0,)), ((), ()))
    has_bias, has_add, has_alias = bias is not None, add is not None, after is not None

    def body(*refs):
        a_ref, b_ref = refs[0], refs[1]
        pos = 2
        bias_ref = add_ref = None
        if has_bias:
            bias_ref = refs[pos]
            pos += 1
        if has_add:
            add_ref = refs[pos]
            pos += 1
        if has_alias:
            pos += 1
        o_ref = refs[pos]
        acc_ref = refs[pos + 1] if nk > 1 else None
        part = lax.dot_general(a_ref[...].astype(BF16), b_ref[...].astype(BF16), dims, preferred_element_type=F32)

        def finish(total):
            if has_bias:
                total = total + bias_ref[...]
            if has_add:
                total = total + add_ref[...]
            o_ref[...] = total.astype(o_ref.dtype)

        if nk == 1:
            finish(part)
        else:
            kk = pl.program_id(2)

            @pl.when(kk == 0)
            def _():
                acc_ref[...] = part

            @pl.when(kk > 0)
            def _():
                acc_ref[...] += part

            @pl.when(kk == nk - 1)
            def _():
                finish(acc_ref[...])

    return pl.pallas_call(
        body, name=name, grid=grid, in_specs=in_specs, out_specs=out_spec, out_shape=out_shape,
        scratch_shapes=[pltpu.VMEM((tm, tn), F32)] if nk > 1 else [],
        compiler_params=_params(("parallel", "parallel", "arbitrary")),
    )(*operands)


def _rowwise(fn, rows, whole, outs, sums=(), *, name, tm=256):
    m = rows[0].shape[-2]
    tm = _pick_rows(m, tm)
    n_rows, n_whole, n_outs, n_sums = len(rows), len(whole), len(outs), len(sums)

    def rspec(shape):
        lead = len(shape) - 2
        return pl.BlockSpec(tuple(shape[:-2]) + (tm, shape[-1]), lambda i: (0,) * lead + (i, 0))

    def wspec(shape):
        return pl.BlockSpec(tuple(shape), lambda i: (0,) * len(shape))

    def body(*refs):
        vals = [r[...] for r in refs[:n_rows + n_whole]]
        out_refs = refs[n_rows + n_whole:n_rows + n_whole + n_outs]
        sum_refs = refs[n_rows + n_whole + n_outs:]
        res = fn(*vals)
        for ref, val in zip(out_refs, res[:n_outs]):
            ref[...] = val.astype(ref.dtype)
        if n_sums:
            @pl.when(pl.program_id(0) == 0)
            def _():
                for ref in sum_refs:
                    ref[...] = jnp.zeros(ref.shape, ref.dtype)

            for ref, val in zip(sum_refs, res[n_outs:]):
                ref[...] += val

    result = pl.pallas_call(
        body, name=name, grid=(m // tm,),
        in_specs=[rspec(r.shape) for r in rows] + [wspec(w.shape) for w in whole],
        out_specs=[rspec(s) for s, _ in outs] + [wspec(s) for s, _ in sums],
        out_shape=[jax.ShapeDtypeStruct(s, d) for s, d in list(outs) + list(sums)],
        compiler_params=_params(("arbitrary",)),
    )(*rows, *whole)
    return result


def _sigmoid(v):
    return jax.nn.sigmoid(v)


def _col_sum(v):
    return jnp.sum(v, axis=0, keepdims=True)


def _ln_stats(z):
    mu = jnp.mean(z, axis=-1, keepdims=True)
    zc = z - mu
    var = jnp.mean(zc * zc, axis=-1, keepdims=True)
    rstd = lax.rsqrt(var + LN_EPS)
    return zc * rstd, rstd


def _ln_fwd_fn(xin, h, gain, bias):
    xhat, _ = _ln_stats(ALPHA * xin + h)
    y = xhat * gain + bias
    return y, y


def _ple_ln_fwd_fn(xin, pg, pu, gain, bias):
    xhat, _ = _ln_stats(ALPHA * xin + _sigmoid(pg) * pu)
    y = xhat * gain + bias
    return y, y


def _ln_dz(dy, z, gain):
    xhat, rstd = _ln_stats(z)
    dxhat = dy * gain
    dz = rstd * (dxhat - jnp.mean(dxhat, axis=-1, keepdims=True)
                 - xhat * jnp.mean(dxhat * xhat, axis=-1, keepdims=True))
    return dz, _col_sum(dy * xhat), _col_sum(dy)


def _ln_bwd_fn(dy, xin, h, gain):
    dz, dgain, dbias = _ln_dz(dy, ALPHA * xin + h, gain)
    return ALPHA * dz, dz, dgain, dbias, _col_sum(dz)


def _ple_ln_bwd_fn(dy, xin, pg, pu, gain):
    sg = _sigmoid(pg)
    dz, dgain, dbias = _ln_dz(dy, ALPHA * xin + sg * pu, gain)
    dpg = dz * pu * sg * (1.0 - sg)
    return ALPHA * dz, dpg, dz * sg, dgain, dbias, _col_sum(dpg)


def _swiglu_fwd_fn(gu):
    hid = gu.shape[-1] // 2
    gate, up = gu[:, :hid], gu[:, hid:]
    return (gate * _sigmoid(gate) * up,)


def _swiglu_bwd_fn(gu, dact):
    hid = gu.shape[-1] // 2
    gate, up = gu[:, :hid], gu[:, hid:]
    sg = _sigmoid(gate)
    dgate = dact * up * sg * (1.0 + gate * (1.0 - sg))
    dup = dact * gate * sg
    return (jnp.concatenate([dgate, dup], axis=-1),)


def _loss_fn(y, target):
    err = y - target
    inv = 1.0 / y.shape[-1]
    part = 0.5 * inv * jnp.sum(jnp.sum(err * err, axis=-1, keepdims=True), axis=0, keepdims=True)
    return err * inv, jnp.broadcast_to(part, (1, LANES))


def _adam_fn(w, mom, vel, p_own, p_sib):
    g = p_own + p_sib
    m_new = ADAM_B1 * mom + (1.0 - ADAM_B1) * g
    v_new = ADAM_B2 * vel + (1.0 - ADAM_B2) * (g * g)
    m_hat = m_new / (1.0 - ADAM_B1 ** ADAM_STEP)
    v_hat = v_new / (1.0 - ADAM_B2 ** ADAM_STEP)
    delta = -ADAM_LR * (m_hat / (jnp.sqrt(v_hat) + ADAM_EPS) + ADAM_WD * w)
    return g, delta, m_new, v_new


def _sum_slots_fn(slots):
    acc = slots[0].astype(F32)
    for s in range(1, slots.shape[0]):
        acc = acc + slots[s].astype(F32)
    return (acc,)


def _hdot(a, b):
    return lax.dot_general(a, b, (((1,), (0,)), ((), ())), precision=lax.Precision.HIGHEST,
                           preferred_element_type=F32)


def _hdot_nt(a, b):
    return lax.dot_general(a, b, (((1,), (1,)), ((), ())), precision=lax.Precision.HIGHEST,
                           preferred_element_type=F32)


def _hdot_tn(a, b):
    return lax.dot_general(a, b, (((0,), (0,)), ((), ())), precision=lax.Precision.HIGHEST,
                           preferred_element_type=F32)


def _dot(a, b):
    return lax.dot_general(a.astype(BF16), b.astype(BF16), (((1,), (0,)), ((), ())), preferred_element_type=F32)


def _dot_nt(a, b):
    return lax.dot_general(a.astype(BF16), b.astype(BF16), (((1,), (1,)), ((), ())), preferred_element_type=F32)


def _dot_tn(a, b):
    return lax.dot_general(a.astype(BF16), b.astype(BF16), (((0,), (0,)), ((), ())), preferred_element_type=F32)


def _hg_masks():
    c = HG_CHUNK
    row = lax.broadcasted_iota(jnp.int32, (c, c), 0)
    col = lax.broadcasted_iota(jnp.int32, (c, c), 1)
    base = row & (-HG_SUB)
    return row, col, base, col <= row, col < base


def _hg_gates(qr, fr, alb):
    lbound = _sigmoid(alb[0:1, :] - alb[1:2, :])
    sig = _sigmoid(fr)
    forget = lbound + (1.0 - lbound) * sig
    kk = (1.0 - lbound) * _sigmoid(-fr)
    qt = qr * _sigmoid(qr) * (HG_DK ** -0.5)
    return qt, kk, jnp.log(forget), lbound, sig, forget


def _hg_scores(qt, kk, g):
    c, nsub = HG_CHUNK, HG_CHUNK // HG_SUB
    row, col, base, causal, below = _hg_masks()
    b = _hdot(causal.astype(F32), g)
    rr = _hdot(below.astype(F32), g)
    bq = b - rr
    qh = qt * jnp.exp(bq)
    edecs = [None]
    parts = [jnp.zeros((HG_SUB, c), F32)]
    for i in range(1, nsub):
        edec = jnp.exp(jnp.minimum(rr[i * HG_SUB:i * HG_SUB + 1, :] - b, 0.0))
        edecs.append(edec)
        parts.append(_hdot_nt(qh[i * HG_SUB:(i + 1) * HG_SUB, :], kk * edec))
    a = jnp.where(below, jnp.concatenate(parts, axis=0), 0.0)
    b3 = b.reshape(nsub, HG_SUB, HG_DK)
    q3 = qt.reshape(nsub, HG_SUB, HG_DK)
    k3 = kk.reshape(nsub, HG_SUB, HG_DK)
    for j in range(HG_SUB):
        e = jnp.exp(jnp.minimum(b3 - b3[:, j:j + 1, :], 0.0))
        colv = jnp.sum(q3 * e * k3[:, j:j + 1, :], axis=-1, keepdims=True).reshape(c, 1)
        a = jnp.where(col == base + j, colv, a)
    a = jnp.where(causal, a, 0.0)
    return a, b, bq, qh, edecs, (b3, q3, k3)


def _hg_norm(o, gr, gain):
    r = lax.rsqrt(jnp.mean(o * o, axis=-1, keepdims=True) + RMS_EPS)
    sg = _sigmoid(gr)
    return o * r * gain, r, sg


def _hgrn2_fwd(proj, alb, gain, *, rb):
    m, d4 = proj.shape
    d = d4 // 4
    heads = d // HG_DK
    rb = min(rb, m)
    cpb = rb // HG_CHUNK
    nrb = m // rb

    def body(q_ref, f_ref, v_ref, g_ref, alb_ref, gain_ref, o_ref, og_ref, st_ref, state):
        @pl.when(pl.program_id(1) == 0)
        def _():
            state[...] = jnp.zeros(state.shape, F32)

        def chunk(ci, carry):
            sl = pl.ds(pl.multiple_of(ci * HG_CHUNK, HG_CHUNK), HG_CHUNK)
            qt, kk, g, _, _, _ = _hg_gates(q_ref[sl, :], f_ref[sl, :], alb_ref[...])
            v = v_ref[sl, :]
            st = state[...]
            st_ref[ci] = st
            a, b, _, _, _, _ = _hg_scores(qt, kk, g)
            o = _hdot(a, v) + _hdot_nt(qt * jnp.exp(b), st)
            b_last = b[HG_CHUNK - 1:HG_CHUNK, :]
            state[...] = st * jnp.exp(b_last) + _hdot_tn(v, kk * jnp.exp(b_last - b))
            o_ref[sl, :] = o
            n, _, sg = _hg_norm(o, g_ref[sl, :], gain_ref[...])
            og_ref[sl, :] = (n * g_ref[sl, :] * sg).astype(og_ref.dtype)
            return carry

        lax.fori_loop(0, cpb, chunk, 0)

    def col(cidx):
        return pl.BlockSpec((rb, HG_DK), lambda h, r: (r, cidx * heads + h))

    return pl.pallas_call(
        body, name="hgrn2_fwd", grid=(heads, nrb),
        in_specs=[col(0), col(1), col(2), col(3),
                  pl.BlockSpec((2, HG_DK), lambda h, r: (0, h)),
                  pl.BlockSpec((1, HG_DK), lambda h, r: (0, 0))],
        out_specs=[pl.BlockSpec((rb, HG_DK), lambda h, r: (r, h)),
                   pl.BlockSpec((rb, HG_DK), lambda h, r: (r, h)),
                   pl.BlockSpec((None, cpb, HG_DK, HG_DK), lambda h, r: (h, r, 0, 0))],
        out_shape=[jax.ShapeDtypeStruct((m, d), F32), jax.ShapeDtypeStruct((m, d), BF16),
                   jax.ShapeDtypeStruct((heads, m // HG_CHUNK, HG_DK, HG_DK), F32)],
        scratch_shapes=[pltpu.VMEM((HG_DK, HG_DK), F32)],
        compiler_params=_params(("parallel", "arbitrary")),
    )(proj, proj, proj, proj, alb, gain)


def _hgrn2_bwd(proj, o_pre, states, dog, alb, gain, *, rb):
    m, d4 = proj.shape
    d = d4 // 4
    heads = d // HG_DK
    rb = min(rb, m)
    cpb = rb // HG_CHUNK
    nrb = m // rb
    c, nsub = HG_CHUNK, HG_CHUNK // HG_SUB

    def body(q_ref, f_ref, v_ref, g_ref, o_ref, st_ref, dog_ref, alb_ref, gain_ref,
             dq_ref, df_ref, dv_ref, dg_ref, dalb_ref, dgain_ref, dstate, carry_ref):
        first = (pl.program_id(0) == 0) & (pl.program_id(1) == 0)

        @pl.when(first)
        def _():
            dgain_ref[...] = jnp.zeros(dgain_ref.shape, F32)

        @pl.when(pl.program_id(1) == 0)
        def _():
            dstate[...] = jnp.zeros(dstate.shape, F32)
            carry_ref[...] = jnp.zeros(carry_ref.shape, F32)
            dalb_ref[...] = jnp.zeros(dalb_ref.shape, F32)

        row, col, base, causal, below = _hg_masks()
        sub_iota = lax.broadcasted_iota(jnp.int32, (nsub, HG_SUB, HG_DK), 1)
        row_k = lax.broadcasted_iota(jnp.int32, (c, HG_DK), 0)
        upper = (col >= row).astype(F32)

        def chunk(step, carry):
            ci = cpb - 1 - step
            sl = pl.ds(pl.multiple_of(ci * HG_CHUNK, HG_CHUNK), HG_CHUNK)
            qr, fr, v, gr = q_ref[sl, :], f_ref[sl, :], v_ref[sl, :], g_ref[sl, :]
            qt, kk, g, lbound, sig, forget = _hg_gates(qr, fr, alb_ref[...])
            o = o_ref[sl, :]
            dogv = dog_ref[sl, :]
            gain_v = gain_ref[...]
            n, r, sg = _hg_norm(o, gr, gain_v)
            dgr = dogv * n * sg * (1.0 + gr * (1.0 - sg))
            dn = dogv * gr * sg
            dgain_ref[...] += _col_sum(dn * o * r)
            u = dn * gain_v
            d_o = r * u - o * (r * r * r) * jnp.mean(u * o, axis=-1, keepdims=True)
            st0 = st_ref[ci]
            dst = dstate[...]
            a, b, bq, qh, edecs, (b3, q3, k3) = _hg_scores(qt, kk, g)
            eb = jnp.exp(b)
            b_last = b[c - 1:c, :]
            kdl_dec = jnp.exp(b_last - b)
            kdl = kk * kdl_dec
            d_a = jnp.where(causal, _hdot_nt(d_o, v), 0.0)
            d_at = _hdot_nt(v, d_o)
            dv = _hdot_tn(a, d_o) + _hdot_nt(kdl, dst)
            dq = eb * _hdot(d_o, st0)
            dk = _hdot(v, dst) * kdl_dec
            d_a_below = jnp.where(below, d_a, 0.0)
            dq_parts = [jnp.zeros((HG_SUB, HG_DK), F32)]
            for i in range(1, nsub):
                lo, hi = i * HG_SUB, (i + 1) * HG_SUB
                dq_parts.append(_hdot(d_a_below[lo:hi, :], kk * edecs[i]))
                gi = _hdot(d_at[:, lo:hi], qh[lo:hi, :])
                dk = dk + jnp.where(row_k < lo, edecs[i] * gi, 0.0)
            dq = dq + jnp.concatenate(dq_parts, axis=0) * jnp.exp(bq)
            dq3 = jnp.zeros((nsub, HG_SUB, HG_DK), F32)
            dk3 = jnp.zeros((nsub, HG_SUB, HG_DK), F32)
            for j in range(HG_SUB):
                e = jnp.exp(jnp.minimum(b3 - b3[:, j:j + 1, :], 0.0))
                dcol = jnp.sum(jnp.where(col == base + j, d_a, 0.0), axis=-1, keepdims=True)
                t1 = dcol.reshape(nsub, HG_SUB, 1) * e
                dq3 = dq3 + t1 * k3[:, j:j + 1, :]
                dk3 = jnp.where(sub_iota == j, jnp.sum(t1 * q3, axis=1, keepdims=True), dk3)
            dq = dq + dq3.reshape(c, HG_DK)
            dk = dk + dk3.reshape(c, HG_DK)
            dstate[...] = dst * jnp.exp(b_last) + _hdot_tn(d_o, qt * eb)
            dglog = _hdot(upper, qt * dq - kk * dk) + carry_ref[...]
            carry_ref[...] = dglog[0:1, :]
            dforget = dglog / forget
            one_m_lb = 1.0 - lbound
            dsig = (dforget - dk) * one_m_lb
            sneg = _sigmoid(-fr)
            dlb = _col_sum(dforget * (1.0 - sig) - dk * sneg)
            dalb0 = dlb * lbound * one_m_lb
            dalb_ref[...] += jnp.concatenate([dalb0, -dalb0], axis=0)
            sq = _sigmoid(qr)
            dq_ref[sl, :] = (dq * (HG_DK ** -0.5) * sq * (1.0 + qr * (1.0 - sq))).astype(dq_ref.dtype)
            df_ref[sl, :] = (dsig * sig * (1.0 - sig)).astype(df_ref.dtype)
            dv_ref[sl, :] = dv.astype(dv_ref.dtype)
            dg_ref[sl, :] = dgr.astype(dg_ref.dtype)
            return carry

        lax.fori_loop(0, cpb, chunk, 0)

    def rev(r):
        return nrb - 1 - r

    def col(cidx):
        return pl.BlockSpec((rb, HG_DK), lambda h, r: (rev(r), cidx * heads + h))

    def head_rows():
        return pl.BlockSpec((rb, HG_DK), lambda h, r: (rev(r), h))

    return pl.pallas_call(
        body, name="hgrn2_bwd", grid=(heads, nrb),
        in_specs=[col(0), col(1), col(2), col(3), head_rows(),
                  pl.BlockSpec((None, cpb, HG_DK, HG_DK), lambda h, r: (h, rev(r), 0, 0)),
                  head_rows(),
                  pl.BlockSpec((2, HG_DK), lambda h, r: (0, h)),
                  pl.BlockSpec((1, HG_DK), lambda h, r: (0, 0))],
        out_specs=[head_rows(), head_rows(), head_rows(), head_rows(),
                   pl.BlockSpec((2, HG_DK), lambda h, r: (0, h)),
                   pl.BlockSpec((1, HG_DK), lambda h, r: (0, 0))],
        out_shape=[jax.ShapeDtypeStruct((m, d), BF16)] * 4
                  + [jax.ShapeDtypeStruct((2, d), F32), jax.ShapeDtypeStruct((1, HG_DK), F32)],
        scratch_shapes=[pltpu.VMEM((HG_DK, HG_DK), F32), pltpu.VMEM((1, HG_DK), F32)],
        compiler_params=_params(("arbitrary", "arbitrary")),
    )(proj, proj, proj, proj, o_pre, states, dog, alb, gain)


def _swa_probs(qh, kp, kc, sink, slope, has_prev):
    qi = lax.broadcasted_iota(jnp.int32, (WINDOW, WINDOW), 0)
    si = lax.broadcasted_iota(jnp.int32, (WINDOW, WINDOW), 1)
    scale = ATT_HD ** -0.5
    dist_c = (qi - si).astype(F32)
    s_p = _dot_nt(qh, kp) * scale - slope * (dist_c + float(WINDOW))
    s_c = _dot_nt(qh, kc) * scale - slope * dist_c
    s_p = jnp.where((si > qi) & has_prev, s_p, NEG)
    s_c = jnp.where(si <= qi, s_c, NEG)
    mx = jnp.maximum(jnp.maximum(jnp.max(s_p, axis=-1, keepdims=True), jnp.max(s_c, axis=-1, keepdims=True)), sink)
    e_p, e_c, e_s = jnp.exp(s_p - mx), jnp.exp(s_c - mx), jnp.exp(sink - mx)
    inv = 1.0 / (jnp.sum(e_p, axis=-1, keepdims=True) + jnp.sum(e_c, axis=-1, keepdims=True) + e_s)
    return e_p * inv, e_c * inv, e_s * inv


def _slope(h, n_heads):
    return float(2.0 ** (-8.0 * (h + 1) / n_heads))


def _swa_fwd(q, kv, sinks):
    m, d = q.shape
    n_heads = d // ATT_HD
    kvh = n_heads // ATT_G
    kd = kvh * ATT_HD
    nb = m // WINDOW

    def body(q_ref, kvp_ref, kvc_ref, sink_ref, o_ref):
        has_prev = pl.program_id(0) > 0
        qv, kvp, kvc = q_ref[...], kvp_ref[...], kvc_ref[...]
        outs = []
        for h in range(n_heads):
            kh = h // ATT_G
            ks = slice(kh * ATT_HD, (kh + 1) * ATT_HD)
            vs = slice(kd + kh * ATT_HD, kd + (kh + 1) * ATT_HD)
            p_p, p_c, _ = _swa_probs(qv[:, h * ATT_HD:(h + 1) * ATT_HD], kvp[:, ks], kvc[:, ks],
                                     sink_ref[:, h:h + 1], _slope(h, n_heads), has_prev)
            outs.append(_dot(p_p, kvp[:, vs]) + _dot(p_c, kvc[:, vs]))
        o_ref[...] = jnp.concatenate(outs, axis=-1).astype(o_ref.dtype)

    return pl.pallas_call(
        body, name="swa_fwd", grid=(nb,),
        in_specs=[pl.BlockSpec((WINDOW, d), lambda n: (n, 0)),
                  pl.BlockSpec((WINDOW, 2 * kd), lambda n: (jnp.maximum(n - 1, 0), 0)),
                  pl.BlockSpec((WINDOW, 2 * kd), lambda n: (n, 0)),
                  pl.BlockSpec((1, n_heads), lambda n: (0, 0))],
        out_specs=pl.BlockSpec((WINDOW, d), lambda n: (n, 0)),
        out_shape=jax.ShapeDtypeStruct((m, d), BF16),
        compiler_params=_params(("arbitrary",)),
    )(q, kv, kv, sinks)


def _swa_bwd(q, kv, sinks, dao):
    m, d = q.shape
    n_heads = d // ATT_HD
    kvh = n_heads // ATT_G
    kd = kvh * ATT_HD
    nb = m // WINDOW
    scale = ATT_HD ** -0.5

    def body(q_ref, kvp_ref, kvc_ref, sink_ref, do_ref, dq_ref, dkvc_ref, dkvp_ref, dqsum_ref, dsink_ref):
        @pl.when(pl.program_id(0) == 0)
        def _():
            dqsum_ref[...] = jnp.zeros(dqsum_ref.shape, F32)
            dsink_ref[...] = jnp.zeros(dsink_ref.shape, F32)

        has_prev = pl.program_id(0) > 0
        qv, kvp, kvc, dov = q_ref[...], kvp_ref[...], kvc_ref[...], do_ref[...]
        lane_h = lax.broadcasted_iota(jnp.int32, (1, n_heads), 1)
        dsink = jnp.zeros((1, n_heads), F32)
        dq_parts, dk_p, dk_c, dv_p, dv_c = [], [], [], [], []
        for kh in range(kvh):
            ks = slice(kh * ATT_HD, (kh + 1) * ATT_HD)
            vs = slice(kd + kh * ATT_HD, kd + (kh + 1) * ATT_HD)
            kp, kc, vp, vc = kvp[:, ks], kvc[:, ks], kvp[:, vs], kvc[:, vs]
            acc = [jnp.zeros((WINDOW, ATT_HD), F32) for _ in range(4)]
            for gq in range(ATT_G):
                h = kh * ATT_G + gq
                hs = slice(h * ATT_HD, (h + 1) * ATT_HD)
                qh, doh = qv[:, hs], dov[:, hs]
                p_p, p_c, p_s = _swa_probs(qh, kp, kc, sink_ref[:, h:h + 1], _slope(h, n_heads), has_prev)
                dp_p, dp_c = _dot_nt(doh, vp), _dot_nt(doh, vc)
                delta = jnp.sum(p_p * dp_p, axis=-1, keepdims=True) + jnp.sum(p_c * dp_c, axis=-1, keepdims=True)
                ds_p, ds_c = p_p * (dp_p - delta), p_c * (dp_c - delta)
                dsink = dsink + jnp.where(lane_h == h, -_col_sum(p_s * delta), 0.0)
                dq_parts.append((_dot(ds_p, kp) + _dot(ds_c, kc)) * scale)
                acc[0] = acc[0] + _dot_tn(ds_p, qh) * scale
                acc[1] = acc[1] + _dot_tn(ds_c, qh) * scale
                acc[2] = acc[2] + _dot_tn(p_p, doh)
                acc[3] = acc[3] + _dot_tn(p_c, doh)
            dk_p.append(acc[0])
            dk_c.append(acc[1])
            dv_p.append(acc[2])
            dv_c.append(acc[3])
        dq = jnp.concatenate(dq_parts, axis=-1)
        dq_ref[...] = dq.astype(dq_ref.dtype)
        dqsum_ref[...] += _col_sum(dq)
        dsink_ref[...] += dsink
        dkvc_ref[...] = jnp.concatenate(dk_c + dv_c, axis=-1)
        dkvp_ref[...] = jnp.concatenate(dk_p + dv_p, axis=-1)

    return pl.pallas_call(
        body, name="swa_bwd", grid=(nb,),
        in_specs=[pl.BlockSpec((WINDOW, d), lambda n: (n, 0)),
                  pl.BlockSpec((WINDOW, 2 * kd), lambda n: (jnp.maximum(n - 1, 0), 0)),
                  pl.BlockSpec((WINDOW, 2 * kd), lambda n: (n, 0)),
                  pl.BlockSpec((1, n_heads), lambda n: (0, 0)),
                  pl.BlockSpec((WINDOW, d), lambda n: (n, 0))],
        out_specs=[pl.BlockSpec((WINDOW, d), lambda n: (n, 0)),
                   pl.BlockSpec((WINDOW, 2 * kd), lambda n: (n, 0)),
                   pl.BlockSpec((WINDOW, 2 * kd), lambda n: (n, 0)),
                   pl.BlockSpec((1, d), lambda n: (0, 0)),
                   pl.BlockSpec((1, n_heads), lambda n: (0, 0))],
        out_shape=[jax.ShapeDtypeStruct((m, d), BF16), jax.ShapeDtypeStruct((m, 2 * kd), F32),
                   jax.ShapeDtypeStruct((m, 2 * kd), F32), jax.ShapeDtypeStruct((1, d), F32),
                   jax.ShapeDtypeStruct((1, n_heads), F32)],
        compiler_params=_params(("arbitrary",)),
    )(q, kv, kv, sinks, dao)


def _kv_grad_combine(dkv_cur, dkv_prev):
    m, w = dkv_cur.shape
    nb = m // WINDOW

    def body(cur_ref, nxt_ref, o_ref, sum_ref):
        @pl.when(pl.program_id(0) == 0)
        def _():
            sum_ref[...] = jnp.zeros(sum_ref.shape, F32)

        total = cur_ref[...] + jnp.where(pl.program_id(0) < nb - 1, nxt_ref[...], 0.0)
        o_ref[...] = total.astype(o_ref.dtype)
        sum_ref[...] += _col_sum(total)

    return pl.pallas_call(
        body, name="kv_grad_combine", grid=(nb,),
        in_specs=[pl.BlockSpec((WINDOW, w), lambda n: (n, 0)),
                  pl.BlockSpec((WINDOW, w), lambda n: (jnp.minimum(n + 1, nb - 1), 0))],
        out_specs=[pl.BlockSpec((WINDOW, w), lambda n: (n, 0)), pl.BlockSpec((1, w), lambda n: (0, 0))],
        out_shape=[jax.ShapeDtypeStruct((m, w), BF16), jax.ShapeDtypeStruct((1, w), F32)],
        compiler_params=_params(("arbitrary",)),
    )(dkv_cur, dkv_prev)


def _row(v):
    return v.reshape(1, -1)


def _local_step(x, p, target, wget, grad_sink, ln_gain, ln_bias, alb, norm_gain, kv_b, b_q, sinks, b_out, ple_b):
    gs = {}
    gains = [[_row(ln_gain[i, j]) for j in range(3)] for i in range(DEPTH)]
    biases = [[_row(ln_bias[i, j]) for j in range(3)] for i in range(DEPTH)]
    sd = x.shape
    pending = [None]

    def mm(a, b, lb=0, **kw):
        after, pending[0] = pending[0], None
        return _mm(a, b, lb=lb, after=after, **kw)

    def ln_fwd(xin, h, i, j, nm):
        return _rowwise(_ln_fwd_fn, [xin, h], [gains[i][j], biases[i][j]], [(sd, F32), (sd, BF16)], name=nm)

    def tail_fwd(xa, i):
        gu = _mm(xa[1], wget("ffn_w_gate_up", i, xa[1]), lb=0, name=f"ffn_up{i}")
        act, = _rowwise(_swiglu_fwd_fn, [gu], [], [((sd[0], gu.shape[1] // 2), BF16)], name=f"swiglu{i}", tm=128)
        f = _mm(act, wget("ffn_w_down", i, act), lb=0, name=f"ffn_down{i}")
        xb = ln_fwd(xa[0], f, i, 1, f"ln_ffn{i}")
        pg = _mm(xb[1], wget("ple_w_gate", i, act), lb=0, bias=_row(ple_b[i]), name=f"ple_gate{i}")
        pu = _mm(p[i], wget("ple_w_up", i, act), lb=0, name=f"ple_up{i}")
        xc = _rowwise(_ple_ln_fwd_fn, [xb[0], pg, pu], [gains[i][2], biases[i][2]], [(sd, F32), (sd, BF16)],
                      name=f"ln_ple{i}")
        return dict(xa=xa, gu=gu, act=act, f=f, xb=xb, pg=pg, pu=pu), xc

    def tail_bwd(dxc, sv, i):
        xa, xb = sv["xa"], sv["xb"]
        dxb_part, dpg, dpu, dg2, db2, dbg = _rowwise(
            _ple_ln_bwd_fn, [dxc, xb[0], sv["pg"], sv["pu"]], [gains[i][2]],
            [(sd, F32), (sd, BF16), (sd, BF16)], [((1, sd[1]), F32)] * 3, name=f"ln_ple_bwd{i}")
        gs[f"ple_b_{i}"] = dbg
        gs[f"ln_gain_{i}_2"], gs[f"ln_bias_{i}_2"] = dg2, db2
        grad_of("ple_w_gate", i, xb[1], dpg)
        grad_of("ple_w_up", i, p[i], dpu)
        dxb = mm(dpg, wget("ple_w_gate", i, None), tb=True, add=dxb_part, name=f"ple_gate_dx{i}")
        dxa_part, df, dg1, db1, _ = _rowwise(
            _ln_bwd_fn, [dxb, xa[0], sv["f"]], [gains[i][1]],
            [(sd, F32), (sd, BF16)], [((1, sd[1]), F32)] * 3, name=f"ln_ffn_bwd{i}")
        gs[f"ln_gain_{i}_1"], gs[f"ln_bias_{i}_1"] = dg1, db1
        grad_of("ffn_w_down", i, sv["act"], df)
        dact = mm(df, wget("ffn_w_down", i, None), tb=True, name=f"ffn_down_dx{i}")
        dgu, = _rowwise(_swiglu_bwd_fn, [sv["gu"], dact], [], [(sv["gu"].shape, BF16)], name=f"swiglu_bwd{i}", tm=128)
        grad_of("ffn_w_gate_up", i, xa[1], dgu)
        return mm(dgu, wget("ffn_w_gate_up", i, None), tb=True, add=dxa_part, name=f"ffn_up_dx{i}")

    def grad_of(nm, i, act, dout):
        grad = mm(act, dout, lb=None, ta=True, out_dtype=BF16, out_layers=1, out_layer=0, name=f"grad_{nm}{i}")
        token = grad_sink(nm, i, grad)
        if token is not None:
            pending[0] = token

    def mixer_ln_bwd(dxa, xin, h, i):
        dx_part, dh, dg0, db0, dhsum = _rowwise(
            _ln_bwd_fn, [dxa, xin, h], [gains[i][0]],
            [(sd, F32), (sd, BF16)], [((1, sd[1]), F32)] * 3, name=f"ln_mix_bwd{i}")
        gs[f"ln_gain_{i}_0"], gs[f"ln_bias_{i}_0"] = dg0, db0
        return dx_part, dh, dhsum

    proj = _mm(x, wget("a_w_in", 0, None), lb=0, name="hg_proj")
    o_pre, og, states = _hgrn2_fwd(proj, alb, norm_gain, rb=HG_ROWS)
    h0 = _mm(og, wget("a_w_out", 0, og), lb=0, name="hg_out")
    x1 = ln_fwd(x, h0, 0, 0, "ln_mix0")
    sv0, x3 = tail_fwd(x1, 0)
    kv = _mm(x3[1], wget("kv_w", 0, x3[1]), lb=0, bias=_row(kv_b), name="kv_proj")
    q = _mm(x3[1], wget("b_w_q", 0, x3[1]), lb=0, bias=b_q, name="q_proj")
    ao = _swa_fwd(q, kv, sinks)
    h1 = _mm(ao, wget("b_w_out", 0, x3[1]), lb=0, bias=b_out, name="att_out")
    x4 = ln_fwd(x3[0], h1, 1, 0, "ln_mix1")
    sv1, y = tail_fwd(x4, 1)
    dy, loss = _rowwise(_loss_fn, [y[0], target], [], [(sd, F32)], [((1, LANES), F32)], name="loss")

    dx4 = tail_bwd(dy, sv1, 1)
    dx3_part, dh1, dh1sum = mixer_ln_bwd(dx4, x3[0], h1, 1)
    gs["b_out"] = dh1sum
    grad_of("b_w_out", 0, ao, dh1)
    dao = mm(dh1, wget("b_w_out", 0, None), tb=True, name="att_out_dx")
    dq, dkv_cur, dkv_prev, dqsum, dsinks = _swa_bwd(q, kv, sinks, dao)
    gs["b_q"], gs["sinks"] = dqsum, dsinks
    dkv, dkvsum = _kv_grad_combine(dkv_cur, dkv_prev)
    gs["kv_b"] = dkvsum
    grad_of("b_w_q", 0, x3[1], dq)
    grad_of("kv_w", 0, x3[1], dkv)
    dx3 = mm(dq, wget("b_w_q", 0, None), tb=True, add=dx3_part, name="q_proj_dx")
    dx3 = mm(dkv, wget("kv_w", 0, None), tb=True, add=dx3, name="kv_proj_dx")
    dx1 = tail_bwd(dx3, sv0, 0)
    dx_part, dh0, _ = mixer_ln_bwd(dx1, x, h0, 0)
    grad_of("a_w_out", 0, og, dh0)
    dog = mm(dh0, wget("a_w_out", 0, None), tb=True, name="hg_out_dx")
    dqr, dfr, dvr, dgr, dalb, dgain = _hgrn2_bwd(proj, o_pre, states, dog, alb, norm_gain, rb=HG_ROWS)
    gs["alb"], gs["norm_gain"] = dalb, dgain
    dproj = jnp.concatenate([dqr, dfr, dvr, dgr], axis=1)
    grad_of("a_w_in", 0, x, dproj)
    grad_x = mm(dproj, wget("a_w_in", 0, None), tb=True, add=dx_part, name="hg_proj_dx")
    return loss, grad_x, gs


HBM_SPEC = pl.BlockSpec(memory_space=pl.ANY)
HBM_ONLY = pl.BlockSpec(memory_space=pltpu.HBM)
SEM_SPEC = pl.BlockSpec(memory_space=pltpu.SEMAPHORE)
SIDE_EFFECT = pltpu.SideEffectType.DATAFLOW_SIDE_EFFECTING


def _piece(ref, kind, j):
    _, r, c = ref.shape
    if kind == "row":
        return ref.at[:, pl.ds(j * (r // N_CHIPS), r // N_CHIPS), :]
    return ref.at[:, :, pl.ds(j * (c // N_CHIPS), c // N_CHIPS)]


def _chip_of(j, c):
    return (j // 2, j % 2, c)


def _in_hbm(a):
    return pltpu.with_memory_space_constraint(a, pltpu.HBM)


def _place(src, kind, chip, *, mode, name, out_dtype, zone=None, zone_shape=None, layer=0):
    if mode == "gather":
        _, r, c = src.shape
        out_shape = (1, r * N_CHIPS, c) if kind == "row" else (1, r, c * N_CHIPS)
    else:
        out_shape = tuple(zone.shape) if zone is not None else tuple(zone_shape)
        r, c = out_shape[-2:]
    tm = _pick_rows(r, 512)
    nb = r // tm

    def full_idx(i, chip_ref):
        return (0, chip_ref[0] * nb + i, 0) if kind == "row" else (0, i, chip_ref[0])

    if mode == "gather":
        in_spec = pl.BlockSpec((None, tm, c), lambda i, chip_ref: (0, i, 0))
        out_spec = pl.BlockSpec((None, tm, c), full_idx)
    else:
        in_spec = pl.BlockSpec((None, tm, c), full_idx)
        out_spec = pl.BlockSpec((None, None, tm, c), lambda i, chip_ref: (chip_ref[0], layer, i, 0))
    in_specs, operands, aliases = [in_spec], [src], {}
    if zone is not None:
        in_specs.append(HBM_SPEC)
        operands.append(zone)
        aliases = {2: 0}

    def body(chip_ref, src_ref, *rest):
        rest[-1][...] = src_ref[...].astype(rest[-1].dtype)

    return pl.pallas_call(
        body, name=name,
        grid_spec=pltpu.PrefetchScalarGridSpec(num_scalar_prefetch=1, grid=(nb,), in_specs=in_specs,
                                               out_specs=out_spec),
        out_shape=jax.ShapeDtypeStruct(out_shape, out_dtype),
        input_output_aliases=aliases,
        compiler_params=_params(("arbitrary",)),
    )(chip, *operands)


class _Exchange:
    def __init__(self, mode, srcs, lands, kinds, layers, name):
        self.mode, self.kinds, self.layers, self.name, self.n = mode, kinds, layers, name, len(lands)
        n, ns = self.n, len(srcs)
        sem_shape = pltpu.SemaphoreType.DMA((n * N_CHIPS,))

        def body(*refs):
            src_refs, land_refs = refs[:ns], refs[ns:ns + n]
            send_sems, recv_sems = refs[ns + n], refs[ns + n + 1]
            token = refs[-1]
            c = lax.axis_index("c")
            me = 2 * lax.axis_index("x") + lax.axis_index("y")
            for j in range(N_CHIPS):
                @pl.when(me == j)
                def _():
                    for a in range(n):
                        for t in range(N_CHIPS):
                            if t != j:
                                src, dst = self._ends(src_refs, land_refs, a, j, t)
                                pltpu.make_async_remote_copy(
                                    src_ref=src, dst_ref=dst, send_sem=send_sems.at[a * N_CHIPS + t],
                                    recv_sem=recv_sems.at[a * N_CHIPS + j],
                                    device_id=_chip_of(t, c), device_id_type=MESH).start()
            token[...] = jnp.zeros(token.shape, token.dtype)

        arrays = list(srcs) + list(lands)
        outs = pl.pallas_call(
            body, name=name + "_start",
            in_specs=[HBM_ONLY] * (ns + n),
            out_specs=[SEM_SPEC, SEM_SPEC] + [HBM_ONLY] * (ns + n) + [pl.BlockSpec(memory_space=pltpu.VMEM)],
            out_shape=[sem_shape, sem_shape] + [pltpu.HBM(a.shape, a.dtype) for a in arrays]
                      + [jax.ShapeDtypeStruct((8, LANES), F32)],
            input_output_aliases={i: i + 2 for i in range(ns + n)},
            compiler_params=pltpu.CompilerParams(has_side_effects=SIDE_EFFECT),
        )(*[_in_hbm(a) for a in arrays])
        self.send_sems, self.recv_sems = outs[0], outs[1]
        self.srcs, self.lands = list(outs[2:2 + ns]), list(outs[2 + ns:2 + ns + n])
        self.token = outs[-1]

    def _ends(self, src_refs, land_refs, a, me_j, peer):
        if self.mode == "gather":
            mine = _piece(land_refs[a], self.kinds[a], me_j)
            return mine, mine
        return _piece(src_refs[a], self.kinds[a], peer), land_refs[a].at[me_j, pl.ds(self.layers[a], 1)]

    def wait(self, after, lands=None):
        n, ns = self.n, len(self.srcs)
        lands = self.lands if lands is None else lands

        def body(*refs):
            src_refs, land_refs = refs[:ns], refs[ns:ns + n]
            send_sems, recv_sems = refs[ns + n], refs[ns + n + 1]
            c = lax.axis_index("c")
            me = 2 * lax.axis_index("x") + lax.axis_index("y")
            for j in range(N_CHIPS):
                @pl.when(me != j)
                def _():
                    for a in range(n):
                        sent, _ = self._ends(src_refs, land_refs, a, 0, j)
                        _, landed = self._ends(src_refs, land_refs, a, j, 0)
                        cp = pltpu.make_async_remote_copy(
                            src_ref=sent, dst_ref=landed, send_sem=send_sems.at[a * N_CHIPS + j],
                            recv_sem=recv_sems.at[a * N_CHIPS + j],
                            device_id=_chip_of(j, c), device_id_type=MESH)
                        cp.wait_send()
                        cp.wait_recv()

        arrays = self.srcs + list(lands)
        operands = [_in_hbm(a) for a in arrays] + [self.send_sems, self.recv_sems]
        in_specs = [HBM_ONLY] * (ns + n) + [SEM_SPEC, SEM_SPEC]
        if after is not None:
            operands.append(after)
            in_specs.append(HBM_SPEC)
        outs = pl.pallas_call(
            body, name=self.name + "_wait",
            in_specs=in_specs, out_specs=[HBM_ONLY] * (ns + n),
            out_shape=[pltpu.HBM(a.shape, a.dtype) for a in arrays],
            input_output_aliases={i: i for i in range(ns + n)},
            compiler_params=pltpu.CompilerParams(has_side_effects=SIDE_EFFECT),
        )(*operands)
        return list(outs[ns:])


def _sibling_swap(arrays):
    n = len(arrays)

    def body(*refs):
        ins, outs = refs[:n], refs[n:2 * n]
        send_sems, recv_sems = refs[2 * n:]
        sibling = (lax.axis_index("x"), lax.axis_index("y"), 1 - lax.axis_index("c"))
        copies = [pltpu.make_async_remote_copy(src_ref=ins[a], dst_ref=outs[a], send_sem=send_sems.at[a],
                                               recv_sem=recv_sems.at[a], device_id=sibling, device_id_type=MESH)
                  for a in range(n)]
        for cp in copies:
            cp.start()
        for cp in copies:
            cp.wait()

    return pl.pallas_call(
        body, name="sibling_swap",
        in_specs=[HBM_SPEC] * n, out_specs=[HBM_SPEC] * n,
        out_shape=[jax.ShapeDtypeStruct(a.shape, a.dtype) for a in arrays],
        scratch_shapes=[pltpu.SemaphoreType.DMA((n,)), pltpu.SemaphoreType.DMA((n,))],
    )(*arrays)


def _gather_devices(vec):
    def body(in_ref, out_ref, send_sems, recv_sems, local_sem):
        x, y, c = lax.axis_index("x"), lax.axis_index("y"), lax.axis_index("c")
        me = 4 * x + 2 * y + c
        mine = pltpu.make_async_copy(in_ref, out_ref.at[me], local_sem)
        mine.start()
        copies = []
        for rel in range(1, N_DEV):
            peer = (x ^ (rel >> 2), y ^ ((rel >> 1) & 1), c ^ (rel & 1))
            copies.append(pltpu.make_async_remote_copy(
                src_ref=in_ref, dst_ref=out_ref.at[me], send_sem=send_sems.at[rel], recv_sem=recv_sems.at[rel],
                device_id=peer, device_id_type=MESH))
        for cp in copies:
            cp.start()
        for cp in copies:
            cp.wait()
        mine.wait()

    return pl.pallas_call(
        body, name="gather_small",
        in_specs=[HBM_SPEC], out_specs=HBM_SPEC,
        out_shape=jax.ShapeDtypeStruct((N_DEV,) + vec.shape, vec.dtype),
        scratch_shapes=[pltpu.SemaphoreType.DMA((N_DEV,)), pltpu.SemaphoreType.DMA((N_DEV,)),
                        pltpu.SemaphoreType.DMA],
    )(vec)


BIG = [("a_w_in", "col"), ("a_w_out", "row"), ("kv_w", "row"), ("b_w_q", "row"), ("b_w_out", "row"),
       ("ffn_w_gate_up", "col"), ("ffn_w_down", "row"), ("ple_w_up", "col"), ("ple_w_gate", "row")]
GATHER_GROUPS = [[("a_w_in", 0), ("small", 0)], [("a_w_out", 0), ("ffn_w_gate_up", 0)],
                 [("ffn_w_down", 0), ("ple_w_gate", 0), ("ple_w_up", 0)], [("kv_w", 0), ("b_w_q", 0), ("b_w_out", 0)],
                 [("ffn_w_gate_up", 1)], [("ffn_w_down", 1), ("ple_w_gate", 1), ("ple_w_up", 1)]]
SCATTER_GROUPS = [[("ple_w_gate", 1), ("ple_w_up", 1), ("ffn_w_down", 1)], [("ffn_w_gate_up", 1)],
                  [("b_w_out", 0), ("b_w_q", 0), ("kv_w", 0)], [("ple_w_gate", 0), ("ple_w_up", 0), ("ffn_w_down", 0)],
                  [("ffn_w_gate_up", 0), ("a_w_out", 0)], [("a_w_in", 0)]]
SMALL_SHARDED = ["ln_gain", "ln_bias", "a_lower_bound"]
SMALL_REPLICATED = ["a_norm_gain", "kv_b", "b_b_q", "b_sinks", "b_b_out", "ple_b_gate"]
WEIGHT_ORDER = ["a_w_in", "a_lower_bound", "a_norm_gain", "a_w_out", "kv_w", "kv_b", "b_w_q", "b_b_q", "b_sinks",
                "b_w_out", "b_b_out", "ffn_w_gate_up", "ffn_w_down", "ple_w_up", "ple_w_gate", "ple_b_gate",
                "ln_gain", "ln_bias"]


def _as3(a):
    return a.reshape((-1,) + a.shape[-2:]) if a.ndim >= 3 else a.reshape((1,) + a.shape)


def _pad_lanes(v):
    n = v.shape[-1]
    return jnp.pad(v, ((0, 0), (0, (-n) % LANES)))


def _adam_small_fn(w, mom, vel, g):
    return _adam_fn(w, mom, vel, g, jnp.zeros_like(g))[1:]


def _sum_rows_fn(slots):
    acc = slots[0]
    for s in range(1, slots.shape[0]):
        acc = acc + slots[s]
    return (acc,)


def kernel(x, p, a_w_in, a_lower_bound, a_norm_gain, a_w_out, kv_w, kv_b, b_w_q, b_b_q, b_sinks, b_w_out, b_b_out, ffn_w_gate_up, ffn_w_down, ple_w_up, ple_w_gate, ple_b_gate, ln_gain, ln_bias, loss_target, m_a_w_in, m_a_lower_bound, m_a_norm_gain, m_a_w_out, m_kv_w, m_kv_b, m_b_w_q, m_b_b_q, m_b_sinks, m_b_w_out, m_b_b_out, m_ffn_w_gate_up, m_ffn_w_down, m_ple_w_up, m_ple_w_gate, m_ple_b_gate, m_ln_gain, m_ln_bias, v_a_w_in, v_a_lower_bound, v_a_norm_gain, v_a_w_out, v_kv_w, v_kv_b, v_b_w_q, v_b_b_q, v_b_sinks, v_b_w_out, v_b_b_out, v_ffn_w_gate_up, v_ffn_w_down, v_ple_w_up, v_ple_w_gate, v_ple_b_gate, v_ln_gain, v_ln_bias):
    args = dict(locals())
    wts = {n: args[n] for n in WEIGHT_ORDER}
    mom = {n: args["m_" + n] for n in WEIGHT_ORDER}
    vel = {n: args["v_" + n] for n in WEIGHT_ORDER}
    chip = 2 * lax.axis_index("x") + lax.axis_index("y")
    d = x.shape[-1]
    dq = d // N_CHIPS

    kind_of = dict(BIG)
    kind_of["small"] = "col"
    chip_arr = chip.reshape(1).astype(jnp.int32)
    placed = {}
    for n, _ in BIG:
        s3 = _as3(wts[n])
        for layer in range(s3.shape[0]):
            placed[(n, layer)] = _place(s3[layer:layer + 1], kind_of[n], chip_arr, mode="gather",
                                        name=f"place_{n}{layer}", out_dtype=BF16)
    small_pack = jnp.concatenate([wts[n].reshape(-1, dq) for n in SMALL_SHARDED], axis=0)[None]
    placed[("small", 0)] = _place(small_pack, "col", chip_arr, mode="gather", name="place_small", out_dtype=F32)

    gathers, where = [], {}
    for gi, group in enumerate(GATHER_GROUPS):
        gathers.append(_Exchange("gather", [], [placed[k] for k in group], [kind_of[k[0]] for k in group],
                                 [0] * len(group), f"gather{gi}"))
        for k in group:
            where[k] = gi
    all_started = jnp.concatenate([g.token for g in gathers[1:]], axis=0)
    ready = {}

    def wget(name, layer, after):
        key = (name, layer)
        if key not in ready:
            gi = where[key]
            outs = gathers[gi].wait(all_started if gi == 0 else after)
            for k, arr in zip(GATHER_GROUPS[gi], outs):
                ready[k] = arr
        return ready[key]

    small_full = wget("small", 0, None)[0]
    ln_gain_f = small_full[0:6].reshape(DEPTH, 3, d)
    ln_bias_f = small_full[6:12].reshape(DEPTH, 3, d)
    alb_f = small_full[12:14]

    group_of = {k: gi for gi, group in enumerate(SCATTER_GROUPS) for k in group}
    grads_done, zones, scatters = {}, {}, []

    def grad_sink(name, layer, grad):
        grads_done[(name, layer)] = grad
        zones[name] = _place(grad, kind_of[name], chip_arr, mode="scatter", name=f"place_grad_{name}{layer}",
                             out_dtype=BF16, zone=zones.get(name), zone_shape=(N_CHIPS,) + _as3(wts[name]).shape,
                             layer=layer)
        gi = group_of[(name, layer)]
        group = SCATTER_GROUPS[gi]
        if not all(k in grads_done for k in group):
            return None
        ex = _Exchange("scatter", [grads_done[k] for k in group], [zones[k[0]] for k in group],
                       [kind_of[k[0]] for k in group], [k[1] for k in group], f"scatter{gi}")
        for k, zone in zip(group, ex.lands):
            zones[k[0]] = zone
        scatters.append((ex, group))
        return ex.token

    loss, grad_x, gs = _local_step(
        x[0], p[:, 0], loss_target[0], wget, grad_sink, ln_gain_f, ln_bias_f, alb_f, a_norm_gain, kv_b, b_b_q,
        b_sinks, b_b_out, ple_b_gate)
    loss = lax.psum(loss[0, 0], ("x", "y", "c"))

    for ex, group in scatters:
        outs = ex.wait(grad_x, lands=[zones[k[0]] for k in group])
        for k, zone in zip(group, outs):
            zones[k[0]] = zone
    partial = []
    for n, _ in BIG:
        s2 = zones[n].reshape(N_CHIPS, -1, zones[n].shape[-1])
        partial.append(_rowwise(_sum_slots_fn, [s2], [], [(s2.shape[1:], F32)], name=f"sum_{n}")[0])
    sibling = _sibling_swap(partial)
    res = {}
    for (n, _), own, sib in zip(BIG, partial, sibling):
        shp = wts[n].shape
        flat = lambda a: a.reshape(-1, shp[-1])
        out = _rowwise(_adam_fn, [flat(wts[n]), flat(mom[n]), flat(vel[n]), own, sib], [],
                       [(own.shape, F32)] * 4, name=f"adam_{n}")
        res[n] = [o.reshape(shp) for o in out]

    ln_g = jnp.concatenate([gs[f"ln_gain_{i}_{j}"] for i in range(DEPTH) for j in range(3)], axis=0)
    ln_b = jnp.concatenate([gs[f"ln_bias_{i}_{j}"] for i in range(DEPTH) for j in range(3)], axis=0)
    ple_bg = jnp.concatenate([gs[f"ple_b_{i}"] for i in range(DEPTH)], axis=0)
    small_list = [ln_g.reshape(1, -1), ln_b.reshape(1, -1), gs["alb"].reshape(1, -1), gs["norm_gain"],
                  gs["kv_b"], gs["b_q"], _pad_lanes(gs["sinks"]), gs["b_out"], ple_bg.reshape(1, -1)]
    small_vec = jnp.concatenate(small_list, axis=1)
    everyone = _gather_devices(small_vec)
    total, = _rowwise(_sum_rows_fn, [everyone], [], [(small_vec.shape, F32)], name="sum_small")
    offs, pos = [], 0
    for v in small_list:
        offs.append((pos, v.shape[1]))
        pos += v.shape[1]

    def seg(k):
        return total[0, offs[k][0]:offs[k][0] + offs[k][1]]

    def my_cols(full, rows):
        return lax.dynamic_slice_in_dim(full.reshape(rows, N_CHIPS, dq), chip, 1, axis=1).reshape(rows, dq)

    n_sink = b_sinks.shape[-1]
    small_grads = {
        "ln_gain": my_cols(seg(0), 6).reshape(ln_gain.shape), "ln_bias": my_cols(seg(1), 6).reshape(ln_bias.shape),
        "a_lower_bound": my_cols(seg(2), 2), "a_norm_gain": seg(3).reshape(a_norm_gain.shape),
        "kv_b": seg(4).reshape(kv_b.shape), "b_b_q": seg(5).reshape(b_b_q.shape),
        "b_sinks": seg(6)[:n_sink].reshape(b_sinks.shape), "b_b_out": seg(7).reshape(b_b_out.shape),
        "ple_b_gate": seg(8).reshape(ple_b_gate.shape)}
    names = SMALL_SHARDED + SMALL_REPLICATED
    pack = lambda dct: _pad_lanes(jnp.concatenate([dct[n].reshape(1, -1) for n in names], axis=1))
    g_pack = pack(small_grads)
    upd = _rowwise(_adam_small_fn, [pack(wts), pack(mom), pack(vel), g_pack], [], [(g_pack.shape, F32)] * 3,
                   name="adam_small")
    pos = 0
    for n in names:
        size = wts[n].size
        res[n] = [small_grads[n]] + [u[0, pos:pos + size].reshape(wts[n].shape) for u in upd]
        pos += size

    outs = [loss, grad_x[None]]
    for k in range(4):
        outs += [res[n][k] for n in WEIGHT_ORDER]
    return tuple(outs)
```

```python
import functools

import jax
import jax.numpy as jnp
from jax import lax
from jax.experimental import pallas as pl
from jax.experimental.pallas import tpu as pltpu

F32 = jnp.float32
BF16 = jnp.bfloat16
MESH = pl.DeviceIdType.MESH

LANES = 128
HG_DK = 128
HG_CHUNK = 64
HG_SUB = 16
HG_ROWS = 512
ATT_HD = 64
ATT_G = 4
WINDOW = 128
DEPTH = 2
ALPHA = (2.0 * DEPTH) ** 0.25
LN_EPS = 1e-5
RMS_EPS = 1e-6
ADAM_LR, ADAM_B1, ADAM_B2, ADAM_EPS, ADAM_WD, ADAM_STEP = 0.001, 0.9, 0.999, 1e-08, 0.01, 10
N_CHIPS = 4
N_DEV = 8
VMEM_LIMIT = 56 * 1024 * 1024
NEG = -1e30


def _pick(n, cap):
    best = None
    for d in range(LANES, min(n, cap) + 1, LANES):
        if n % d == 0:
            best = d
    return n if best is None else best


def _pick_rows(m, cap):
    best = None
    for d in range(16, min(m, cap) + 1, 16):
        if m % d == 0:
            best = d
    return m if best is None else best


def _params(sem):
    return pltpu.CompilerParams(dimension_semantics=sem, vmem_limit_bytes=VMEM_LIMIT)


def _mm(a, b, *, name, la=None, lb=None, ta=False, tb=False, bias=None, add=None, out_dtype=F32,
        out_layers=None, out_layer=None, after=None, caps=(512, 1536, 2048)):
    ar, ac = a.shape[-2:]
    br, bc = b.shape[-2:]
    m, k = (ac, ar) if ta else (ar, ac)
    k2, n = (bc, br) if tb else (br, bc)
    assert k == k2, (a.shape, b.shape, ta, tb)
    tm, tn, tk = _pick(m, caps[0]), _pick(n, caps[1]), _pick(k, caps[2])
    nk = k // tk
    grid = (m // tm, n // tn, nk)

    def spec(block, idx, layer):
        if layer is None:
            return pl.BlockSpec(block, idx)
        return pl.BlockSpec((None,) + block, lambda i, j, kk: (layer,) + idx(i, j, kk))

    a_spec = spec((tk, tm), lambda i, j, kk: (kk, i), la) if ta else spec((tm, tk), lambda i, j, kk: (i, kk), la)
    b_spec = spec((tn, tk), lambda i, j, kk: (j, kk), lb) if tb else spec((tk, tn), lambda i, j, kk: (kk, j), lb)
    in_specs, operands = [a_spec, b_spec], [a, b]
    if bias is not None:
        in_specs.append(pl.BlockSpec((1, tn), lambda i, j, kk: (0, j)))
        operands.append(bias)
    if add is not None:
        in_specs.append(pl.BlockSpec((tm, tn), lambda i, j, kk: (i, j)))
        operands.append(add)
    if after is not None:
        in_specs.append(pl.BlockSpec(memory_space=pl.ANY))
        operands.append(after)
    if out_layers is None:
        out_shape = jax.ShapeDtypeStruct((m, n), out_dtype)
    else:
        out_shape = jax.ShapeDtypeStruct((out_layers, m, n), out_dtype)
    out_spec = spec((tm, tn), lambda i, j, kk: (i, j), out_layer)
    dims = (((0 if ta else 1,), (1 if tb else 0,)), ((), ()))
    has_bias, has_add, has_alias = bias is not None, add is not None, after is not None

    def body(*refs):
        a_ref, b_ref = refs[0], refs[1]
        pos = 2
        bias_ref = add_ref = None
        if has_bias:
            bias_ref = refs[pos]
            pos += 1
        if has_add:
            add_ref = refs[pos]
            pos += 1
        if has_alias:
            pos += 1
        o_ref = refs[pos]
        acc_ref = refs[pos + 1] if nk > 1 else None
        part = lax.dot_general(a_ref[...].astype(BF16), b_ref[...].astype(BF16), dims, preferred_element_type=F32)

        def finish(total):
            if has_bias:
                total = total + bias_ref[...]
            if has_add:
                total = total + add_ref[...]
            o_ref[...] = total.astype(o_ref.dtype)

        if nk == 1:
            finish(part)
        else:
            kk = pl.program_id(2)

            @pl.when(kk == 0)
            def _():
                acc_ref[...] = part

            @pl.when(kk > 0)
            def _():
                acc_ref[...] += part

            @pl.when(kk == nk - 1)
            def _():
                finish(acc_ref[...])

    return pl.pallas_call(
        body, name=name, grid=grid, in_specs=in_specs, out_specs=out_spec, out_shape=out_shape,
        scratch_shapes=[pltpu.VMEM((tm, tn), F32)] if nk > 1 else [],
        compiler_params=_params(("parallel", "parallel", "arbitrary")),
    )(*operands)


def _rowwise(fn, rows, whole, outs, sums=(), *, name, tm=256):
    m = rows[0].shape[-2]
    tm = _pick_rows(m, tm)
    n_rows, n_whole, n_outs, n_sums = len(rows), len(whole), len(outs), len(sums)

    def rspec(shape):
        lead = len(shape) - 2
        return pl.BlockSpec(tuple(shape[:-2]) + (tm, shape[-1]), lambda i: (0,) * lead + (i, 0))

    def wspec(shape):
        return pl.BlockSpec(tuple(shape), lambda i: (0,) * len(shape))

    def body(*refs):
        vals = [r[...] for r in refs[:n_rows + n_whole]]
        out_refs = refs[n_rows + n_whole:n_rows + n_whole + n_outs]
        sum_refs = refs[n_rows + n_whole + n_outs:]
        res = fn(*vals)
        for ref, val in zip(out_refs, res[:n_outs]):
            ref[...] = val.astype(ref.dtype)
        if n_sums:
            @pl.when(pl.program_id(0) == 0)
            def _():
                for ref in sum_refs:
                    ref[...] = jnp.zeros(ref.shape, ref.dtype)

            for ref, val in zip(sum_refs, res[n_outs:]):
                ref[...] += val

    result = pl.pallas_call(
        body, name=name, grid=(m // tm,),
        in_specs=[rspec(r.shape) for r in rows] + [wspec(w.shape) for w in whole],
        out_specs=[rspec(s) for s, _ in outs] + [wspec(s) for s, _ in sums],
        out_shape=[jax.ShapeDtypeStruct(s, d) for s, d in list(outs) + list(sums)],
        compiler_params=_params(("arbitrary",)),
    )(*rows, *whole)
    return result


def _sigmoid(v):
    return jax.nn.sigmoid(v)


def _col_sum(v):
    return jnp.sum(v, axis=0, keepdims=True)


def _ln_stats(z):
    mu = jnp.mean(z, axis=-1, keepdims=True)
    zc = z - mu
    var = jnp.mean(zc * zc, axis=-1, keepdims=True)
    rstd = lax.rsqrt(var + LN_EPS)
    return zc * rstd, rstd


def _ln_fwd_fn(xin, h, gain, bias):
    xhat, _ = _ln_stats(ALPHA * xin + h)
    y = xhat * gain + bias
    return y, y


def _ple_ln_fwd_fn(xin, pg, pu, gain, bias):
    xhat, _ = _ln_stats(ALPHA * xin + _sigmoid(pg) * pu)
    y = xhat * gain + bias
    return y, y


def _ln_dz(dy, z, gain):
    xhat, rstd = _ln_stats(z)
    dxhat = dy * gain
    dz = rstd * (dxhat - jnp.mean(dxhat, axis=-1, keepdims=True)
                 - xhat * jnp.mean(dxhat * xhat, axis=-1, keepdims=True))
    return dz, _col_sum(dy * xhat), _col_sum(dy)


def _ln_bwd_fn(dy, xin, h, gain):
    dz, dgain, dbias = _ln_dz(dy, ALPHA * xin + h, gain)
    return ALPHA * dz, dz, dgain, dbias, _col_sum(dz)


def _ple_ln_bwd_fn(dy, xin, pg, pu, gain):
    sg = _sigmoid(pg)
    dz, dgain, dbias = _ln_dz(dy, ALPHA * xin + sg * pu, gain)
    dpg = dz * pu * sg * (1.0 - sg)
    return ALPHA * dz, dpg, dz * sg, dgain, dbias, _col_sum(dpg)


def _swiglu_fwd_fn(gu):
    hid = gu.shape[-1] // 2
    gate, up = gu[:, :hid], gu[:, hid:]
    return (gate * _sigmoid(gate) * up,)


def _swiglu_bwd_fn(gu, dact):
    hid = gu.shape[-1] // 2
    gate, up = gu[:, :hid], gu[:, hid:]
    sg = _sigmoid(gate)
    dgate = dact * up * sg * (1.0 + gate * (1.0 - sg))
    dup = dact * gate * sg
    return (jnp.concatenate([dgate, dup], axis=-1),)


def _loss_fn(y, target):
    err = y - target
    inv = 1.0 / y.shape[-1]
    part = 0.5 * inv * jnp.sum(jnp.sum(err * err, axis=-1, keepdims=True), axis=0, keepdims=True)
    return err * inv, jnp.broadcast_to(part, (1, LANES))


def _adam_fn(w, mom, vel, p_own, p_sib):
    g = p_own + p_sib
    m_new = ADAM_B1 * mom + (1.0 - ADAM_B1) * g
    v_new = ADAM_B2 * vel + (1.0 - ADAM_B2) * (g * g)
    m_hat = m_new / (1.0 - ADAM_B1 ** ADAM_STEP)
    v_hat = v_new / (1.0 - ADAM_B2 ** ADAM_STEP)
    delta = -ADAM_LR * (m_hat / (jnp.sqrt(v_hat) + ADAM_EPS) + ADAM_WD * w)
    return g, delta, m_new, v_new


def _sum_slots_fn(slots):
    acc = slots[0].astype(F32)
    for s in range(1, slots.shape[0]):
        acc = acc + slots[s].astype(F32)
    return (acc,)


def _split2(x):
    hi = x.astype(BF16)
    return hi, (x - hi.astype(F32)).astype(BF16)


def _dot3(a, b, dims):
    a_hi, a_lo = _split2(a)
    b_hi, b_lo = _split2(b)
    dn = (dims, ((), ()))
    return (lax.dot_general(a_hi, b_hi, dn, preferred_element_type=F32)
            + (lax.dot_general(a_hi, b_lo, dn, preferred_element_type=F32)
               + lax.dot_general(a_lo, b_hi, dn, preferred_element_type=F32)))


def _tdot(mask01, b):
    m = mask01.astype(BF16)
    b_hi = b.astype(BF16)
    rest = b - b_hi.astype(F32)
    b_mid = rest.astype(BF16)
    b_lo = (rest - b_mid.astype(F32)).astype(BF16)
    dn = (((1,), (0,)), ((), ()))
    return (lax.dot_general(m, b_hi, dn, preferred_element_type=F32)
            + (lax.dot_general(m, b_mid, dn, preferred_element_type=F32)
               + lax.dot_general(m, b_lo, dn, preferred_element_type=F32)))


def _hdot(a, b):
    return _dot3(a, b, ((1,), (0,)))


def _hdot_nt(a, b):
    return _dot3(a, b, ((1,), (1,)))


def _hdot_tn(a, b):
    return _dot3(a, b, ((0,), (0,)))


def _dot(a, b):
    return lax.dot_general(a.astype(BF16), b.astype(BF16), (((1,), (0,)), ((), ())), preferred_element_type=F32)


def _dot_nt(a, b):
    return lax.dot_general(a.astype(BF16), b.astype(BF16), (((1,), (1,)), ((), ())), preferred_element_type=F32)


def _dot_tn(a, b):
    return lax.dot_general(a.astype(BF16), b.astype(BF16), (((0,), (0,)), ((), ())), preferred_element_type=F32)


def _hg_masks():
    c = HG_CHUNK
    row = lax.broadcasted_iota(jnp.int32, (c, c), 0)
    col = lax.broadcasted_iota(jnp.int32, (c, c), 1)
    base = row & (-HG_SUB)
    return row, col, base, col <= row, col < base


def _hg_gates(qr, fr, alb):
    lbound = _sigmoid(alb[0:1, :] - alb[1:2, :])
    sig = _sigmoid(fr)
    forget = lbound + (1.0 - lbound) * sig
    kk = (1.0 - lbound) * _sigmoid(-fr)
    qt = qr * _sigmoid(qr) * (HG_DK ** -0.5)
    return qt, kk, jnp.log(forget), lbound, sig, forget


def _hg_scores(qt, kk, g):
    c, nsub = HG_CHUNK, HG_CHUNK // HG_SUB
    row, col, base, causal, below = _hg_masks()
    b = _tdot(causal, g)
    rr = _tdot(below, g)
    bq = b - rr
    qh = qt * jnp.exp(bq)
    edecs = [None]
    parts = [jnp.zeros((HG_SUB, c), F32)]
    for i in range(1, nsub):
        edec = jnp.exp(jnp.minimum(rr[i * HG_SUB:i * HG_SUB + 1, :] - b, 0.0))
        edecs.append(edec)
        parts.append(_dot_nt(qh[i * HG_SUB:(i + 1) * HG_SUB, :], kk * edec))
    a = jnp.where(below, jnp.concatenate(parts, axis=0), 0.0)
    b3 = b.reshape(nsub, HG_SUB, HG_DK)
    q3 = qt.reshape(nsub, HG_SUB, HG_DK)
    k3 = kk.reshape(nsub, HG_SUB, HG_DK)
    for j in range(HG_SUB):
        e = jnp.exp(jnp.minimum(b3 - b3[:, j:j + 1, :], 0.0))
        colv = jnp.sum(q3 * e * k3[:, j:j + 1, :], axis=-1, keepdims=True).reshape(c, 1)
        a = jnp.where(col == base + j, colv, a)
    a = jnp.where(causal, a, 0.0)
    return a, b, bq, qh, edecs, (b3, q3, k3)


def _hg_norm(o, gr, gain):
    r = lax.rsqrt(jnp.mean(o * o, axis=-1, keepdims=True) + RMS_EPS)
    sg = _sigmoid(gr)
    return o * r * gain, r, sg


def _hgrn2_fwd(proj, alb, gain, *, rb):
    m, d4 = proj.shape
    d = d4 // 4
    heads = d // HG_DK
    rb = min(rb, m)
    cpb = rb // HG_CHUNK
    nrb = m // rb

    def body(q_ref, f_ref, v_ref, g_ref, alb_ref, gain_ref, o_ref, og_ref, st_ref, state):
        @pl.when(pl.program_id(1) == 0)
        def _():
            state[...] = jnp.zeros(state.shape, F32)

        def chunk(ci, carry):
            sl = pl.ds(pl.multiple_of(ci * HG_CHUNK, HG_CHUNK), HG_CHUNK)
            qt, kk, g, _, _, _ = _hg_gates(q_ref[sl, :], f_ref[sl, :], alb_ref[...])
            v = v_ref[sl, :]
            st = state[...]
            st_ref[ci] = st
            a, b, _, _, _, _ = _hg_scores(qt, kk, g)
            o = _dot(a, v) + _dot_nt(qt * jnp.exp(b), st)
            b_last = b[HG_CHUNK - 1:HG_CHUNK, :]
            state[...] = st * jnp.exp(b_last) + _hdot_tn(v, kk * jnp.exp(b_last - b))
            o_ref[sl, :] = o
            n, _, sg = _hg_norm(o, g_ref[sl, :], gain_ref[...])
            og_ref[sl, :] = (n * g_ref[sl, :] * sg).astype(og_ref.dtype)
            return carry

        lax.fori_loop(0, cpb, chunk, 0, unroll=2)

    def col(cidx):
        return pl.BlockSpec((rb, HG_DK), lambda h, r: (r, cidx * heads + h))

    return pl.pallas_call(
        body, name="hgrn2_fwd", grid=(heads, nrb),
        in_specs=[col(0), col(1), col(2), col(3),
                  pl.BlockSpec((2, HG_DK), lambda h, r: (0, h)),
                  pl.BlockSpec((1, HG_DK), lambda h, r: (0, 0))],
        out_specs=[pl.BlockSpec((rb, HG_DK), lambda h, r: (r, h)),
                   pl.BlockSpec((rb, HG_DK), lambda h, r: (r, h)),
                   pl.BlockSpec((None, cpb, HG_DK, HG_DK), lambda h, r: (h, r, 0, 0))],
        out_shape=[jax.ShapeDtypeStruct((m, d), F32), jax.ShapeDtypeStruct((m, d), BF16),
                   jax.ShapeDtypeStruct((heads, m // HG_CHUNK, HG_DK, HG_DK), F32)],
        scratch_shapes=[pltpu.VMEM((HG_DK, HG_DK), F32)],
        compiler_params=_params(("parallel", "arbitrary")),
    )(proj, proj, proj, proj, alb, gain)


def _hgrn2_bwd(proj, o_pre, states, dog, alb, gain, *, rb):
    m, d4 = proj.shape
    d = d4 // 4
    heads = d // HG_DK
    rb = min(rb, m)
    cpb = rb // HG_CHUNK
    nrb = m // rb
    c, nsub = HG_CHUNK, HG_CHUNK // HG_SUB

    def body(q_ref, f_ref, v_ref, g_ref, o_ref, st_ref, dog_ref, alb_ref, gain_ref,
             dq_ref, df_ref, dv_ref, dg_ref, dalb_ref, dgain_ref, dstate, carry_ref):
        first = (pl.program_id(0) == 0) & (pl.program_id(1) == 0)

        @pl.when(first)
        def _():
            dgain_ref[...] = jnp.zeros(dgain_ref.shape, F32)

        @pl.when(pl.program_id(1) == 0)
        def _():
            dstate[...] = jnp.zeros(dstate.shape, F32)
            carry_ref[...] = jnp.zeros(carry_ref.shape, F32)
            dalb_ref[...] = jnp.zeros(dalb_ref.shape, F32)

        row, col, base, causal, below = _hg_masks()
        sub_iota = lax.broadcasted_iota(jnp.int32, (nsub, HG_SUB, HG_DK), 1)
        row_k = lax.broadcasted_iota(jnp.int32, (c, HG_DK), 0)
        upper = col >= row

        def chunk(step, carry):
            ci = cpb - 1 - step
            sl = pl.ds(pl.multiple_of(ci * HG_CHUNK, HG_CHUNK), HG_CHUNK)
            qr, fr, v, gr = q_ref[sl, :], f_ref[sl, :], v_ref[sl, :], g_ref[sl, :]
            qt, kk, g, lbound, sig, forget = _hg_gates(qr, fr, alb_ref[...])
            o = o_ref[sl, :]
            dogv = dog_ref[sl, :]
            gain_v = gain_ref[...]
            n, r, sg = _hg_norm(o, gr, gain_v)
            dgr = dogv * n * sg * (1.0 + gr * (1.0 - sg))
            dn = dogv * gr * sg
            dgain_ref[...] += _col_sum(dn * o * r)
            u = dn * gain_v
            d_o = r * u - o * (r * r * r) * jnp.mean(u * o, axis=-1, keepdims=True)
            st0 = st_ref[ci]
            dst = dstate[...]
            a, b, bq, qh, edecs, (b3, q3, k3) = _hg_scores(qt, kk, g)
            eb = jnp.exp(b)
            b_last = b[c - 1:c, :]
            kdl_dec = jnp.exp(b_last - b)
            kdl = kk * kdl_dec
            d_a = jnp.where(causal, _dot_nt(d_o, v), 0.0)
            d_at = _dot_nt(v, d_o)
            dv = _dot_tn(a, d_o) + _dot_nt(kdl, dst)
            dq = eb * _hdot(d_o, st0)
            dk = _hdot(v, dst) * kdl_dec
            d_a_below = jnp.where(below, d_a, 0.0)
            dq_parts = [jnp.zeros((HG_SUB, HG_DK), F32)]
            for i in range(1, nsub):
                lo, hi = i * HG_SUB, (i + 1) * HG_SUB
                dq_parts.append(_hdot(d_a_below[lo:hi, :], kk * edecs[i]))
                gi = _hdot(d_at[:, lo:hi], qh[lo:hi, :])
                dk = dk + jnp.where(row_k < lo, edecs[i] * gi, 0.0)
            dq = dq + jnp.concatenate(dq_parts, axis=0) * jnp.exp(bq)
            dq3 = jnp.zeros((nsub, HG_SUB, HG_DK), F32)
            dk3 = jnp.zeros((nsub, HG_SUB, HG_DK), F32)
            for j in range(HG_SUB):
                e = jnp.exp(jnp.minimum(b3 - b3[:, j:j + 1, :], 0.0))
                dcol = jnp.sum(jnp.where(col == base + j, d_a, 0.0), axis=-1, keepdims=True)
                t1 = dcol.reshape(nsub, HG_SUB, 1) * e
                dq3 = dq3 + t1 * k3[:, j:j + 1, :]
                dk3 = jnp.where(sub_iota == j, jnp.sum(t1 * q3, axis=1, keepdims=True), dk3)
            dq = dq + dq3.reshape(c, HG_DK)
            dk = dk + dk3.reshape(c, HG_DK)
            dstate[...] = dst * jnp.exp(b_last) + _hdot_tn(d_o, qt * eb)
            dglog = _tdot(upper, qt * dq - kk * dk) + carry_ref[...]
            carry_ref[...] = dglog[0:1, :]
            dforget = dglog / forget
            one_m_lb = 1.0 - lbound
            dsig = (dforget - dk) * one_m_lb
            sneg = _sigmoid(-fr)
            dlb = _col_sum(dforget * (1.0 - sig) - dk * sneg)
            dalb0 = dlb * lbound * one_m_lb
            dalb_ref[...] += jnp.concatenate([dalb0, -dalb0], axis=0)
            sq = _sigmoid(qr)
            dq_ref[sl, :] = (dq * (HG_DK ** -0.5) * sq * (1.0 + qr * (1.0 - sq))).astype(dq_ref.dtype)
            df_ref[sl, :] = (dsig * sig * (1.0 - sig)).astype(df_ref.dtype)
            dv_ref[sl, :] = dv.astype(dv_ref.dtype)
            dg_ref[sl, :] = dgr.astype(dg_ref.dtype)
            return carry

        lax.fori_loop(0, cpb, chunk, 0, unroll=2)

    def rev(r):
        return nrb - 1 - r

    def col(cidx):
        return pl.BlockSpec((rb, HG_DK), lambda h, r: (rev(r), cidx * heads + h))

    def head_rows():
        return pl.BlockSpec((rb, HG_DK), lambda h, r: (rev(r), h))

    return pl.pallas_call(
        body, name="hgrn2_bwd", grid=(heads, nrb),
        in_specs=[col(0), col(1), col(2), col(3), head_rows(),
                  pl.BlockSpec((None, cpb, HG_DK, HG_DK), lambda h, r: (h, rev(r), 0, 0)),
                  head_rows(),
                  pl.BlockSpec((2, HG_DK), lambda h, r: (0, h)),
                  pl.BlockSpec((1, HG_DK), lambda h, r: (0, 0))],
        out_specs=[head_rows(), head_rows(), head_rows(), head_rows(),
                   pl.BlockSpec((2, HG_DK), lambda h, r: (0, h)),
                   pl.BlockSpec((1, HG_DK), lambda h, r: (0, 0))],
        out_shape=[jax.ShapeDtypeStruct((m, d), BF16)] * 4
                  + [jax.ShapeDtypeStruct((2, d), F32), jax.ShapeDtypeStruct((1, HG_DK), F32)],
        scratch_shapes=[pltpu.VMEM((HG_DK, HG_DK), F32), pltpu.VMEM((1, HG_DK), F32)],
        compiler_params=_params(("arbitrary", "arbitrary")),
    )(proj, proj, proj, proj, o_pre, states, dog, alb, gain)


def _swa_probs(qh, kp, kc, sink, slope, has_prev):
    rows = qh.shape[0]
    qi = lax.broadcasted_iota(jnp.int32, (rows, WINDOW), 0) & (WINDOW - 1)
    si = lax.broadcasted_iota(jnp.int32, (rows, WINDOW), 1)
    scale = ATT_HD ** -0.5
    dist_c = (qi - si).astype(F32)
    s_p = _dot_nt(qh, kp) * scale - slope * (dist_c + float(WINDOW))
    s_c = _dot_nt(qh, kc) * scale - slope * dist_c
    s_p = jnp.where((si > qi) & has_prev, s_p, NEG)
    s_c = jnp.where(si <= qi, s_c, NEG)
    mx = jnp.maximum(jnp.maximum(jnp.max(s_p, axis=-1, keepdims=True), jnp.max(s_c, axis=-1, keepdims=True)), sink)
    e_p, e_c, e_s = jnp.exp(s_p - mx), jnp.exp(s_c - mx), jnp.exp(sink - mx)
    inv = 1.0 / (jnp.sum(e_p, axis=-1, keepdims=True) + jnp.sum(e_c, axis=-1, keepdims=True) + e_s)
    return e_p * inv, e_c * inv, e_s * inv


def _slope(h, n_heads):
    return float(2.0 ** (-8.0 * (h + 1) / n_heads))


def _swa_group(ref_vals, sink_ref, kh, n_heads):
    heads = [kh * ATT_G + g for g in range(ATT_G)]
    stacked = [jnp.concatenate([v[:, h * ATT_HD:(h + 1) * ATT_HD] for h in heads], axis=0) for v in ref_vals]
    grp = lax.shift_right_logical(lax.broadcasted_iota(jnp.int32, (ATT_G * WINDOW, 1), 0), WINDOW.bit_length() - 1)
    slope = jnp.zeros((ATT_G * WINDOW, 1), F32)
    sink = jnp.zeros((ATT_G * WINDOW, 1), F32)
    for g, h in enumerate(heads):
        slope = jnp.where(grp == g, _slope(h, n_heads), slope)
        sink = jnp.where(grp == g, sink_ref[:, h:h + 1], sink)
    return stacked, slope, sink


def _swa_fwd(q, kv, sinks):
    m, d = q.shape
    n_heads = d // ATT_HD
    kvh = n_heads // ATT_G
    kd = kvh * ATT_HD
    nb = m // WINDOW

    def body(q_ref, kvp_ref, kvc_ref, sink_ref, o_ref):
        has_prev = pl.program_id(0) > 0
        qv, kvp, kvc = q_ref[...], kvp_ref[...], kvc_ref[...]
        outs = []
        for kh in range(kvh):
            ks = slice(kh * ATT_HD, (kh + 1) * ATT_HD)
            vs = slice(kd + kh * ATT_HD, kd + (kh + 1) * ATT_HD)
            (q4,), slope, sink = _swa_group([qv], sink_ref, kh, n_heads)
            p_p, p_c, _ = _swa_probs(q4, kvp[:, ks], kvc[:, ks], sink, slope, has_prev)
            o4 = _dot(p_p, kvp[:, vs]) + _dot(p_c, kvc[:, vs])
            outs += [o4[g * WINDOW:(g + 1) * WINDOW, :] for g in range(ATT_G)]
        o_ref[...] = jnp.concatenate(outs, axis=-1).astype(o_ref.dtype)

    return pl.pallas_call(
        body, name="swa_fwd", grid=(nb,),
        in_specs=[pl.BlockSpec((WINDOW, d), lambda n: (n, 0)),
                  pl.BlockSpec((WINDOW, 2 * kd), lambda n: (jnp.maximum(n - 1, 0), 0)),
                  pl.BlockSpec((WINDOW, 2 * kd), lambda n: (n, 0)),
                  pl.BlockSpec((1, n_heads), lambda n: (0, 0))],
        out_specs=pl.BlockSpec((WINDOW, d), lambda n: (n, 0)),
        out_shape=jax.ShapeDtypeStruct((m, d), BF16),
        compiler_params=_params(("arbitrary",)),
    )(q, kv, kv, sinks)


def _swa_bwd(q, kv, sinks, dao):
    m, d = q.shape
    n_heads = d // ATT_HD
    kvh = n_heads // ATT_G
    kd = kvh * ATT_HD
    nb = m // WINDOW
    scale = ATT_HD ** -0.5

    def body(q_ref, kvp_ref, kvc_ref, sink_ref, do_ref, dq_ref, dkvc_ref, dkvp_ref, dqsum_ref, dsink_ref):
        @pl.when(pl.program_id(0) == 0)
        def _():
            dqsum_ref[...] = jnp.zeros(dqsum_ref.shape, F32)
            dsink_ref[...] = jnp.zeros(dsink_ref.shape, F32)

        has_prev = pl.program_id(0) > 0
        qv, kvp, kvc, dov = q_ref[...], kvp_ref[...], kvc_ref[...], do_ref[...]
        lane_h = lax.broadcasted_iota(jnp.int32, (1, n_heads), 1)
        dsink = jnp.zeros((1, n_heads), F32)
        dq_parts, dk_p, dk_c, dv_p, dv_c = [], [], [], [], []
        for kh in range(kvh):
            ks = slice(kh * ATT_HD, (kh + 1) * ATT_HD)
            vs = slice(kd + kh * ATT_HD, kd + (kh + 1) * ATT_HD)
            kp, kc, vp, vc = kvp[:, ks], kvc[:, ks], kvp[:, vs], kvc[:, vs]
            (q4, do4), slope, sink = _swa_group([qv, dov], sink_ref, kh, n_heads)
            p_p, p_c, p_s = _swa_probs(q4, kp, kc, sink, slope, has_prev)
            dp_p, dp_c = _dot_nt(do4, vp), _dot_nt(do4, vc)
            delta = jnp.sum(p_p * dp_p, axis=-1, keepdims=True) + jnp.sum(p_c * dp_c, axis=-1, keepdims=True)
            ds_p, ds_c = p_p * (dp_p - delta), p_c * (dp_c - delta)
            sink_term = p_s * delta
            dq4 = (_dot(ds_p, kp) + _dot(ds_c, kc)) * scale
            for g in range(ATT_G):
                rows = slice(g * WINDOW, (g + 1) * WINDOW)
                dsink = dsink + jnp.where(lane_h == kh * ATT_G + g, -_col_sum(sink_term[rows, :]), 0.0)
                dq_parts.append(dq4[rows, :])
            dk_p.append(_dot_tn(ds_p, q4) * scale)
            dk_c.append(_dot_tn(ds_c, q4) * scale)
            dv_p.append(_dot_tn(p_p, do4))
            dv_c.append(_dot_tn(p_c, do4))
        dq = jnp.concatenate(dq_parts, axis=-1)
        dq_ref[...] = dq.astype(dq_ref.dtype)
        dqsum_ref[...] += _col_sum(dq)
        dsink_ref[...] += dsink
        dkvc_ref[...] = jnp.concatenate(dk_c + dv_c, axis=-1)
        dkvp_ref[...] = jnp.concatenate(dk_p + dv_p, axis=-1)

    return pl.pallas_call(
        body, name="swa_bwd", grid=(nb,),
        in_specs=[pl.BlockSpec((WINDOW, d), lambda n: (n, 0)),
                  pl.BlockSpec((WINDOW, 2 * kd), lambda n: (jnp.maximum(n - 1, 0), 0)),
                  pl.BlockSpec((WINDOW, 2 * kd), lambda n: (n, 0)),
                  pl.BlockSpec((1, n_heads), lambda n: (0, 0)),
                  pl.BlockSpec((WINDOW, d), lambda n: (n, 0))],
        out_specs=[pl.BlockSpec((WINDOW, d), lambda n: (n, 0)),
                   pl.BlockSpec((WINDOW, 2 * kd), lambda n: (n, 0)),
                   pl.BlockSpec((WINDOW, 2 * kd), lambda n: (n, 0)),
                   pl.BlockSpec((1, d), lambda n: (0, 0)),
                   pl.BlockSpec((1, n_heads), lambda n: (0, 0))],
        out_shape=[jax.ShapeDtypeStruct((m, d), BF16), jax.ShapeDtypeStruct((m, 2 * kd), F32),
                   jax.ShapeDtypeStruct((m, 2 * kd), F32), jax.ShapeDtypeStruct((1, d), F32),
                   jax.ShapeDtypeStruct((1, n_heads), F32)],
        compiler_params=_params(("arbitrary",)),
    )(q, kv, kv, sinks, dao)


def _kv_grad_combine(dkv_cur, dkv_prev):
    m, w = dkv_cur.shape
    nb = m // WINDOW

    def body(cur_ref, nxt_ref, o_ref, sum_ref):
        @pl.when(pl.program_id(0) == 0)
        def _():
            sum_ref[...] = jnp.zeros(sum_ref.shape, F32)

        total = cur_ref[...] + jnp.where(pl.program_id(0) < nb - 1, nxt_ref[...], 0.0)
        o_ref[...] = total.astype(o_ref.dtype)
        sum_ref[...] += _col_sum(total)

    return pl.pallas_call(
        body, name="kv_grad_combine", grid=(nb,),
        in_specs=[pl.BlockSpec((WINDOW, w), lambda n: (n, 0)),
                  pl.BlockSpec((WINDOW, w), lambda n: (jnp.minimum(n + 1, nb - 1), 0))],
        out_specs=[pl.BlockSpec((WINDOW, w), lambda n: (n, 0)), pl.BlockSpec((1, w), lambda n: (0, 0))],
        out_shape=[jax.ShapeDtypeStruct((m, w), BF16), jax.ShapeDtypeStruct((1, w), F32)],
        compiler_params=_params(("arbitrary",)),
    )(dkv_cur, dkv_prev)


def _row(v):
    return v.reshape(1, -1)


def _local_step(x, p, target, wget, grad_sink, ln_gain, ln_bias, alb, norm_gain, kv_b, b_q, sinks, b_out, ple_b):
    gs = {}
    gains = [[_row(ln_gain[i, j]) for j in range(3)] for i in range(DEPTH)]
    biases = [[_row(ln_bias[i, j]) for j in range(3)] for i in range(DEPTH)]
    sd = x.shape
    pending = [None]

    def mm(a, b, lb=0, **kw):
        after, pending[0] = pending[0], None
        return _mm(a, b, lb=lb, after=after, **kw)

    def ln_fwd(xin, h, i, j, nm):
        return _rowwise(_ln_fwd_fn, [xin, h], [gains[i][j], biases[i][j]], [(sd, F32), (sd, BF16)], name=nm)

    def tail_fwd(xa, i):
        gu = _mm(xa[1], wget("ffn_w_gate_up", i, xa[1]), lb=0, name=f"ffn_up{i}")
        act, = _rowwise(_swiglu_fwd_fn, [gu], [], [((sd[0], gu.shape[1] // 2), BF16)], name=f"swiglu{i}", tm=128)
        f = _mm(act, wget("ffn_w_down", i, act), lb=0, name=f"ffn_down{i}")
        xb = ln_fwd(xa[0], f, i, 1, f"ln_ffn{i}")
        pg = _mm(xb[1], wget("ple_w_gate", i, act), lb=0, bias=_row(ple_b[i]), name=f"ple_gate{i}")
        pu = _mm(p[i], wget("ple_w_up", i, act), lb=0, name=f"ple_up{i}")
        xc = _rowwise(_ple_ln_fwd_fn, [xb[0], pg, pu], [gains[i][2], biases[i][2]], [(sd, F32), (sd, BF16)],
                      name=f"ln_ple{i}")
        return dict(xa=xa, gu=gu, act=act, f=f, xb=xb, pg=pg, pu=pu), xc

    def tail_bwd(dxc, sv, i):
        xa, xb = sv["xa"], sv["xb"]
        dxb_part, dpg, dpu, dg2, db2, dbg = _rowwise(
            _ple_ln_bwd_fn, [dxc, xb[0], sv["pg"], sv["pu"]], [gains[i][2]],
            [(sd, F32), (sd, BF16), (sd, BF16)], [((1, sd[1]), F32)] * 3, name=f"ln_ple_bwd{i}")
        gs[f"ple_b_{i}"] = dbg
        gs[f"ln_gain_{i}_2"], gs[f"ln_bias_{i}_2"] = dg2, db2
        grad_of("ple_w_gate", i, xb[1], dpg)
        grad_of("ple_w_up", i, p[i], dpu)
        dxb = mm(dpg, wget("ple_w_gate", i, None), tb=True, add=dxb_part, name=f"ple_gate_dx{i}")
        dxa_part, df, dg1, db1, _ = _rowwise(
            _ln_bwd_fn, [dxb, xa[0], sv["f"]], [gains[i][1]],
            [(sd, F32), (sd, BF16)], [((1, sd[1]), F32)] * 3, name=f"ln_ffn_bwd{i}")
        gs[f"ln_gain_{i}_1"], gs[f"ln_bias_{i}_1"] = dg1, db1
        grad_of("ffn_w_down", i, sv["act"], df)
        dact = mm(df, wget("ffn_w_down", i, None), tb=True, name=f"ffn_down_dx{i}")
        dgu, = _rowwise(_swiglu_bwd_fn, [sv["gu"], dact], [], [(sv["gu"].shape, BF16)], name=f"swiglu_bwd{i}", tm=128)
        grad_of("ffn_w_gate_up", i, xa[1], dgu)
        return mm(dgu, wget("ffn_w_gate_up", i, None), tb=True, add=dxa_part, name=f"ffn_up_dx{i}")

    def grad_of(nm, i, act, dout):
        grad = mm(act, dout, lb=None, ta=True, out_dtype=BF16, out_layers=1, out_layer=0, name=f"grad_{nm}{i}")
        token = grad_sink(nm, i, grad)
        if token is not None:
            pending[0] = token

    def mixer_ln_bwd(dxa, xin, h, i):
        dx_part, dh, dg0, db0, dhsum = _rowwise(
            _ln_bwd_fn, [dxa, xin, h], [gains[i][0]],
            [(sd, F32), (sd, BF16)], [((1, sd[1]), F32)] * 3, name=f"ln_mix_bwd{i}")
        gs[f"ln_gain_{i}_0"], gs[f"ln_bias_{i}_0"] = dg0, db0
        return dx_part, dh, dhsum

    proj = _mm(x, wget("a_w_in", 0, None), lb=0, name="hg_proj")
    o_pre, og, states = _hgrn2_fwd(proj, alb, norm_gain, rb=HG_ROWS)
    h0 = _mm(og, wget("a_w_out", 0, og), lb=0, name="hg_out")
    x1 = ln_fwd(x, h0, 0, 0, "ln_mix0")
    sv0, x3 = tail_fwd(x1, 0)
    kv = _mm(x3[1], wget("kv_w", 0, x3[1]), lb=0, bias=_row(kv_b), name="kv_proj")
    q = _mm(x3[1], wget("b_w_q", 0, x3[1]), lb=0, bias=b_q, name="q_proj")
    ao = _swa_fwd(q, kv, sinks)
    h1 = _mm(ao, wget("b_w_out", 0, x3[1]), lb=0, bias=b_out, name="att_out")
    x4 = ln_fwd(x3[0], h1, 1, 0, "ln_mix1")
    sv1, y = tail_fwd(x4, 1)
    dy, loss = _rowwise(_loss_fn, [y[0], target], [], [(sd, F32)], [((1, LANES), F32)], name="loss")

    dx4 = tail_bwd(dy, sv1, 1)
    dx3_part, dh1, dh1sum = mixer_ln_bwd(dx4, x3[0], h1, 1)
    gs["b_out"] = dh1sum
    grad_of("b_w_out", 0, ao, dh1)
    dao = mm(dh1, wget("b_w_out", 0, None), tb=True, name="att_out_dx")
    dq, dkv_cur, dkv_prev, dqsum, dsinks = _swa_bwd(q, kv, sinks, dao)
    gs["b_q"], gs["sinks"] = dqsum, dsinks
    dkv, dkvsum = _kv_grad_combine(dkv_cur, dkv_prev)
    gs["kv_b"] = dkvsum
    grad_of("b_w_q", 0, x3[1], dq)
    grad_of("kv_w", 0, x3[1], dkv)
    dx3 = mm(dq, wget("b_w_q", 0, None), tb=True, add=dx3_part, name="q_proj_dx")
    dx3 = mm(dkv, wget("kv_w", 0, None), tb=True, add=dx3, name="kv_proj_dx")
    dx1 = tail_bwd(dx3, sv0, 0)
    dx_part, dh0, _ = mixer_ln_bwd(dx1, x, h0, 0)
    grad_of("a_w_out", 0, og, dh0)
    dog = mm(dh0, wget("a_w_out", 0, None), tb=True, name="hg_out_dx")
    dqr, dfr, dvr, dgr, dalb, dgain = _hgrn2_bwd(proj, o_pre, states, dog, alb, norm_gain, rb=HG_ROWS)
    gs["alb"], gs["norm_gain"] = dalb, dgain
    dproj = jnp.concatenate([dqr, dfr, dvr, dgr], axis=1)
    grad_of("a_w_in", 0, x, dproj)
    grad_x = mm(dproj, wget("a_w_in", 0, None), tb=True, add=dx_part, name="hg_proj_dx")
    return loss, grad_x, gs


HBM_SPEC = pl.BlockSpec(memory_space=pl.ANY)
HBM_ONLY = pl.BlockSpec(memory_space=pltpu.HBM)
SEM_SPEC = pl.BlockSpec(memory_space=pltpu.SEMAPHORE)
SIDE_EFFECT = pltpu.SideEffectType.DATAFLOW_SIDE_EFFECTING


def _piece(ref, kind, j):
    _, r, c = ref.shape
    if kind == "row":
        return ref.at[:, pl.ds(j * (r // N_CHIPS), r // N_CHIPS), :]
    return ref.at[:, :, pl.ds(j * (c // N_CHIPS), c // N_CHIPS)]


def _chip_of(j, c):
    return (j // 2, j % 2, c)


def _in_hbm(a):
    return pltpu.with_memory_space_constraint(a, pltpu.HBM)


def _place(src, kind, chip, *, mode, name, out_dtype, zone=None, zone_shape=None, layer=0):
    if mode == "gather":
        _, r, c = src.shape
        out_shape = (1, r * N_CHIPS, c) if kind == "row" else (1, r, c * N_CHIPS)
    else:
        out_shape = tuple(zone.shape) if zone is not None else tuple(zone_shape)
        r, c = out_shape[-2:]
    tm = _pick_rows(r, 512)
    nb = r // tm

    def full_idx(i, chip_ref):
        return (0, chip_ref[0] * nb + i, 0) if kind == "row" else (0, i, chip_ref[0])

    if mode == "gather":
        in_spec = pl.BlockSpec((None, tm, c), lambda i, chip_ref: (0, i, 0))
        out_spec = pl.BlockSpec((None, tm, c), full_idx)
    else:
        in_spec = pl.BlockSpec((None, tm, c), full_idx)
        out_spec = pl.BlockSpec((None, None, tm, c), lambda i, chip_ref: (chip_ref[0], layer, i, 0))
    in_specs, operands, aliases = [in_spec], [src], {}
    if zone is not None:
        in_specs.append(HBM_SPEC)
        operands.append(zone)
        aliases = {2: 0}

    def body(chip_ref, src_ref, *rest):
        rest[-1][...] = src_ref[...].astype(rest[-1].dtype)

    return pl.pallas_call(
        body, name=name,
        grid_spec=pltpu.PrefetchScalarGridSpec(num_scalar_prefetch=1, grid=(nb,), in_specs=in_specs,
                                               out_specs=out_spec),
        out_shape=jax.ShapeDtypeStruct(out_shape, out_dtype),
        input_output_aliases=aliases,
        compiler_params=_params(("arbitrary",)),
    )(chip, *operands)


class _Exchange:
    def __init__(self, mode, srcs, lands, kinds, layers, name):
        self.mode, self.kinds, self.layers, self.name, self.n = mode, kinds, layers, name, len(lands)
        n, ns = self.n, len(srcs)
        sem_shape = pltpu.SemaphoreType.DMA((n * N_CHIPS,))

        def body(*refs):
            src_refs, land_refs = refs[:ns], refs[ns:ns + n]
            send_sems, recv_sems = refs[ns + n], refs[ns + n + 1]
            token = refs[-1]
            c = lax.axis_index("c")
            me = 2 * lax.axis_index("x") + lax.axis_index("y")
            for j in range(N_CHIPS):
                @pl.when(me == j)
                def _():
                    for a in range(n):
                        for t in range(N_CHIPS):
                            if t != j:
                                src, dst = self._ends(src_refs, land_refs, a, j, t)
                                pltpu.make_async_remote_copy(
                                    src_ref=src, dst_ref=dst, send_sem=send_sems.at[a * N_CHIPS + t],
                                    recv_sem=recv_sems.at[a * N_CHIPS + j],
                                    device_id=_chip_of(t, c), device_id_type=MESH).start()
            token[...] = jnp.zeros(token.shape, token.dtype)

        arrays = list(srcs) + list(lands)
        outs = pl.pallas_call(
            body, name=name + "_start",
            in_specs=[HBM_ONLY] * (ns + n),
            out_specs=[SEM_SPEC, SEM_SPEC] + [HBM_ONLY] * (ns + n) + [pl.BlockSpec(memory_space=pltpu.VMEM)],
            out_shape=[sem_shape, sem_shape] + [pltpu.HBM(a.shape, a.dtype) for a in arrays]
                      + [jax.ShapeDtypeStruct((8, LANES), F32)],
            input_output_aliases={i: i + 2 for i in range(ns + n)},
            compiler_params=pltpu.CompilerParams(has_side_effects=SIDE_EFFECT),
        )(*[_in_hbm(a) for a in arrays])
        self.send_sems, self.recv_sems = outs[0], outs[1]
        self.srcs, self.lands = list(outs[2:2 + ns]), list(outs[2 + ns:2 + ns + n])
        self.token = outs[-1]

    def _ends(self, src_refs, land_refs, a, me_j, peer):
        if self.mode == "gather":
            mine = _piece(land_refs[a], self.kinds[a], me_j)
            return mine, mine
        return _piece(src_refs[a], self.kinds[a], peer), land_refs[a].at[me_j, pl.ds(self.layers[a], 1)]

    def wait(self, after, lands=None):
        n, ns = self.n, len(self.srcs)
        lands = self.lands if lands is None else lands

        def body(*refs):
            src_refs, land_refs = refs[:ns], refs[ns:ns + n]
            send_sems, recv_sems = refs[ns + n], refs[ns + n + 1]
            c = lax.axis_index("c")
            me = 2 * lax.axis_index("x") + lax.axis_index("y")
            for j in range(N_CHIPS):
                @pl.when(me != j)
                def _():
                    for a in range(n):
                        sent, _ = self._ends(src_refs, land_refs, a, 0, j)
                        _, landed = self._ends(src_refs, land_refs, a, j, 0)
                        cp = pltpu.make_async_remote_copy(
                            src_ref=sent, dst_ref=landed, send_sem=send_sems.at[a * N_CHIPS + j],
                            recv_sem=recv_sems.at[a * N_CHIPS + j],
                            device_id=_chip_of(j, c), device_id_type=MESH)
                        cp.wait_send()
                        cp.wait_recv()

        arrays = self.srcs + list(lands)
        operands = [_in_hbm(a) for a in arrays] + [self.send_sems, self.recv_sems]
        in_specs = [HBM_ONLY] * (ns + n) + [SEM_SPEC, SEM_SPEC]
        if after is not None:
            operands.append(after)
            in_specs.append(HBM_SPEC)
        outs = pl.pallas_call(
            body, name=self.name + "_wait",
            in_specs=in_specs, out_specs=[HBM_ONLY] * (ns + n),
            out_shape=[pltpu.HBM(a.shape, a.dtype) for a in arrays],
            input_output_aliases={i: i for i in range(ns + n)},
            compiler_params=pltpu.CompilerParams(has_side_effects=SIDE_EFFECT),
        )(*operands)
        return list(outs[ns:])


def _sibling_swap(arrays):
    n = len(arrays)

    def body(*refs):
        ins, outs = refs[:n], refs[n:2 * n]
        send_sems, recv_sems = refs[2 * n:]
        sibling = (lax.axis_index("x"), lax.axis_index("y"), 1 - lax.axis_index("c"))
        copies = [pltpu.make_async_remote_copy(src_ref=ins[a], dst_ref=outs[a], send_sem=send_sems.at[a],
                                               recv_sem=recv_sems.at[a], device_id=sibling, device_id_type=MESH)
                  for a in range(n)]
        for cp in copies:
            cp.start()
        for cp in copies:
            cp.wait()

    return pl.pallas_call(
        body, name="sibling_swap",
        in_specs=[HBM_SPEC] * n, out_specs=[HBM_SPEC] * n,
        out_shape=[jax.ShapeDtypeStruct(a.shape, a.dtype) for a in arrays],
        scratch_shapes=[pltpu.SemaphoreType.DMA((n,)), pltpu.SemaphoreType.DMA((n,))],
    )(*arrays)


def _gather_devices(vec):
    def body(in_ref, out_ref, send_sems, recv_sems, local_sem):
        x, y, c = lax.axis_index("x"), lax.axis_index("y"), lax.axis_index("c")
        me = 4 * x + 2 * y + c
        mine = pltpu.make_async_copy(in_ref, out_ref.at[me], local_sem)
        mine.start()
        copies = []
        for rel in range(1, N_DEV):
            peer = (x ^ (rel >> 2), y ^ ((rel >> 1) & 1), c ^ (rel & 1))
            copies.append(pltpu.make_async_remote_copy(
                src_ref=in_ref, dst_ref=out_ref.at[me], send_sem=send_sems.at[rel], recv_sem=recv_sems.at[rel],
                device_id=peer, device_id_type=MESH))
        for cp in copies:
            cp.start()
        for cp in copies:
            cp.wait()
        mine.wait()

    return pl.pallas_call(
        body, name="gather_small",
        in_specs=[HBM_SPEC], out_specs=HBM_SPEC,
        out_shape=jax.ShapeDtypeStruct((N_DEV,) + vec.shape, vec.dtype),
        scratch_shapes=[pltpu.SemaphoreType.DMA((N_DEV,)), pltpu.SemaphoreType.DMA((N_DEV,)),
                        pltpu.SemaphoreType.DMA],
    )(vec)


BIG = [("a_w_in", "col"), ("a_w_out", "row"), ("kv_w", "row"), ("b_w_q", "row"), ("b_w_out", "row"),
       ("ffn_w_gate_up", "col"), ("ffn_w_down", "row"), ("ple_w_up", "col"), ("ple_w_gate", "row")]
GATHER_GROUPS = [[("a_w_in", 0), ("small", 0)], [("a_w_out", 0), ("ffn_w_gate_up", 0)],
                 [("ffn_w_down", 0), ("ple_w_gate", 0), ("ple_w_up", 0)], [("kv_w", 0), ("b_w_q", 0), ("b_w_out", 0)],
                 [("ffn_w_gate_up", 1)], [("ffn_w_down", 1), ("ple_w_gate", 1), ("ple_w_up", 1)]]
SCATTER_GROUPS = [[("ple_w_gate", 1), ("ple_w_up", 1), ("ffn_w_down", 1)], [("ffn_w_gate_up", 1)],
                  [("b_w_out", 0), ("b_w_q", 0), ("kv_w", 0)], [("ple_w_gate", 0), ("ple_w_up", 0), ("ffn_w_down", 0)],
                  [("ffn_w_gate_up", 0), ("a_w_out", 0)], [("a_w_in", 0)]]
SMALL_SHARDED = ["ln_gain", "ln_bias", "a_lower_bound"]
SMALL_REPLICATED = ["a_norm_gain", "kv_b", "b_b_q", "b_sinks", "b_b_out", "ple_b_gate"]
WEIGHT_ORDER = ["a_w_in", "a_lower_bound", "a_norm_gain", "a_w_out", "kv_w", "kv_b", "b_w_q", "b_b_q", "b_sinks",
                "b_w_out", "b_b_out", "ffn_w_gate_up", "ffn_w_down", "ple_w_up", "ple_w_gate", "ple_b_gate",
                "ln_gain", "ln_bias"]


def _as3(a):
    return a.reshape((-1,) + a.shape[-2:]) if a.ndim >= 3 else a.reshape((1,) + a.shape)


def _pad_lanes(v):
    n = v.shape[-1]
    return jnp.pad(v, ((0, 0), (0, (-n) % LANES)))


def _adam_small_fn(w, mom, vel, g):
    return _adam_fn(w, mom, vel, g, jnp.zeros_like(g))[1:]


def _sum_rows_fn(slots):
    acc = slots[0]
    for s in range(1, slots.shape[0]):
        acc = acc + slots[s]
    return (acc,)


def kernel(x, p, a_w_in, a_lower_bound, a_norm_gain, a_w_out, kv_w, kv_b, b_w_q, b_b_q, b_sinks, b_w_out, b_b_out, ffn_w_gate_up, ffn_w_down, ple_w_up, ple_w_gate, ple_b_gate, ln_gain, ln_bias, loss_target, m_a_w_in, m_a_lower_bound, m_a_norm_gain, m_a_w_out, m_kv_w, m_kv_b, m_b_w_q, m_b_b_q, m_b_sinks, m_b_w_out, m_b_b_out, m_ffn_w_gate_up, m_ffn_w_down, m_ple_w_up, m_ple_w_gate, m_ple_b_gate, m_ln_gain, m_ln_bias, v_a_w_in, v_a_lower_bound, v_a_norm_gain, v_a_w_out, v_kv_w, v_kv_b, v_b_w_q, v_b_b_q, v_b_sinks, v_b_w_out, v_b_b_out, v_ffn_w_gate_up, v_ffn_w_down, v_ple_w_up, v_ple_w_gate, v_ple_b_gate, v_ln_gain, v_ln_bias):
    args = dict(locals())
    wts = {n: args[n] for n in WEIGHT_ORDER}
    mom = {n: args["m_" + n] for n in WEIGHT_ORDER}
    vel = {n: args["v_" + n] for n in WEIGHT_ORDER}
    chip = 2 * lax.axis_index("x") + lax.axis_index("y")
    d = x.shape[-1]
    dq = d // N_CHIPS

    kind_of = dict(BIG)
    kind_of["small"] = "col"
    chip_arr = chip.reshape(1).astype(jnp.int32)
    placed = {}
    for n, _ in BIG:
        s3 = _as3(wts[n])
        for layer in range(s3.shape[0]):
            placed[(n, layer)] = _place(s3[layer:layer + 1], kind_of[n], chip_arr, mode="gather",
                                        name=f"place_{n}{layer}", out_dtype=BF16)
    small_pack = jnp.concatenate([wts[n].reshape(-1, dq) for n in SMALL_SHARDED], axis=0)[None]
    placed[("small", 0)] = _place(small_pack, "col", chip_arr, mode="gather", name="place_small", out_dtype=F32)

    gathers, where = [], {}
    for gi, group in enumerate(GATHER_GROUPS):
        gathers.append(_Exchange("gather", [], [placed[k] for k in group], [kind_of[k[0]] for k in group],
                                 [0] * len(group), f"gather{gi}"))
        for k in group:
            where[k] = gi
    all_started = jnp.concatenate([g.token for g in gathers[1:]], axis=0)
    ready = {}

    def wget(name, layer, after):
        key = (name, layer)
        if key not in ready:
            gi = where[key]
            outs = gathers[gi].wait(all_started if gi == 0 else after)
            for k, arr in zip(GATHER_GROUPS[gi], outs):
                ready[k] = arr
        return ready[key]

    small_full = wget("small", 0, None)[0]
    ln_gain_f = small_full[0:6].reshape(DEPTH, 3, d)
    ln_bias_f = small_full[6:12].reshape(DEPTH, 3, d)
    alb_f = small_full[12:14]

    group_of = {k: gi for gi, group in enumerate(SCATTER_GROUPS) for k in group}
    grads_done, zones, scatters = {}, {}, []

    def grad_sink(name, layer, grad):
        grads_done[(name, layer)] = grad
        zones[name] = _place(grad, kind_of[name], chip_arr, mode="scatter", name=f"place_grad_{name}{layer}",
                             out_dtype=BF16, zone=zones.get(name), zone_shape=(N_CHIPS,) + _as3(wts[name]).shape,
                             layer=layer)
        gi = group_of[(name, layer)]
        group = SCATTER_GROUPS[gi]
        if not all(k in grads_done for k in group):
            return None
        ex = _Exchange("scatter", [grads_done[k] for k in group], [zones[k[0]] for k in group],
                       [kind_of[k[0]] for k in group], [k[1] for k in group], f"scatter{gi}")
        for k, zone in zip(group, ex.lands):
            zones[k[0]] = zone
        scatters.append((ex, group))
        return ex.token

    loss, grad_x, gs = _local_step(
        x[0], p[:, 0], loss_target[0], wget, grad_sink, ln_gain_f, ln_bias_f, alb_f, a_norm_gain, kv_b, b_b_q,
        b_sinks, b_b_out, ple_b_gate)
    loss = lax.psum(loss[0, 0], ("x", "y", "c"))

    for ex, group in scatters:
        outs = ex.wait(grad_x, lands=[zones[k[0]] for k in group])
        for k, zone in zip(group, outs):
            zones[k[0]] = zone
    partial = []
    for n, _ in BIG:
        s2 = zones[n].reshape(N_CHIPS, -1, zones[n].shape[-1])
        partial.append(_rowwise(_sum_slots_fn, [s2], [], [(s2.shape[1:], F32)], name=f"sum_{n}")[0])
    sibling = _sibling_swap(partial)
    res = {}
    for (n, _), own, sib in zip(BIG, partial, sibling):
        shp = wts[n].shape
        flat = lambda a: a.reshape(-1, shp[-1])
        out = _rowwise(_adam_fn, [flat(wts[n]), flat(mom[n]), flat(vel[n]), own, sib], [],
                       [(own.shape, F32)] * 4, name=f"adam_{n}")
        res[n] = [o.reshape(shp) for o in out]

    ln_g = jnp.concatenate([gs[f"ln_gain_{i}_{j}"] for i in range(DEPTH) for j in range(3)], axis=0)
    ln_b = jnp.concatenate([gs[f"ln_bias_{i}_{j}"] for i in range(DEPTH) for j in range(3)], axis=0)
    ple_bg = jnp.concatenate([gs[f"ple_b_{i}"] for i in range(DEPTH)], axis=0)
    small_list = [ln_g.reshape(1, -1), ln_b.reshape(1, -1), gs["alb"].reshape(1, -1), gs["norm_gain"],
                  gs["kv_b"], gs["b_q"], _pad_lanes(gs["sinks"]), gs["b_out"], ple_bg.reshape(1, -1)]
    small_vec = jnp.concatenate(small_list, axis=1)
    everyone = _gather_devices(small_vec)
    total, = _rowwise(_sum_rows_fn, [everyone], [], [(small_vec.shape, F32)], name="sum_small")
    offs, pos = [], 0
    for v in small_list:
        offs.append((pos, v.shape[1]))
        pos += v.shape[1]

    def seg(k):
        return total[0, offs[k][0]:offs[k][0] + offs[k][1]]

    def my_cols(full, rows):
        return lax.dynamic_slice_in_dim(full.reshape(rows, N_CHIPS, dq), chip, 1, axis=1).reshape(rows, dq)

    n_sink = b_sinks.shape[-1]
    small_grads = {
        "ln_gain": my_cols(seg(0), 6).reshape(ln_gain.shape), "ln_bias": my_cols(seg(1), 6).reshape(ln_bias.shape),
        "a_lower_bound": my_cols(seg(2), 2), "a_norm_gain": seg(3).reshape(a_norm_gain.shape),
        "kv_b": seg(4).reshape(kv_b.shape), "b_b_q": seg(5).reshape(b_b_q.shape),
        "b_sinks": seg(6)[:n_sink].reshape(b_sinks.shape), "b_b_out": seg(7).reshape(b_b_out.shape),
        "ple_b_gate": seg(8).reshape(ple_b_gate.shape)}
    names = SMALL_SHARDED + SMALL_REPLICATED
    pack = lambda dct: _pad_lanes(jnp.concatenate([dct[n].reshape(1, -1) for n in names], axis=1))
    g_pack = pack(small_grads)
    upd = _rowwise(_adam_small_fn, [pack(wts), pack(mom), pack(vel), g_pack], [], [(g_pack.shape, F32)] * 3,
                   name="adam_small")
    pos = 0
    for n in names:
        size = wts[n].size
        res[n] = [small_grads[n]] + [u[0, pos:pos + size].reshape(wts[n].shape) for u in upd]
        pos += size

    outs = [loss, grad_x[None]]
    for k in range(4):
        outs += [res[n][k] for n in WEIGHT_ORDER]
    return tuple(outs)
```

```python
import functools

import jax
import jax.numpy as jnp
from jax import lax
from jax.experimental import pallas as pl
from jax.experimental.pallas import tpu as pltpu

F32 = jnp.float32
BF16 = jnp.bfloat16
MESH = pl.DeviceIdType.MESH

LANES = 128
HG_DK = 128
HG_CHUNK = 64
HG_SUB = 16
HG_ROWS = 512
ATT_HD = 64
ATT_G = 4
WINDOW = 128
DEPTH = 2
ALPHA = (2.0 * DEPTH) ** 0.25
LN_EPS = 1e-5
RMS_EPS = 1e-6
ADAM_LR, ADAM_B1, ADAM_B2, ADAM_EPS, ADAM_WD, ADAM_STEP = 0.001, 0.9, 0.999, 1e-08, 0.01, 10
N_CHIPS = 4
N_DEV = 8
VMEM_LIMIT = 56 * 1024 * 1024
NEG = -1e30


def _pick(n, cap):
    best = None
    for d in range(LANES, min(n, cap) + 1, LANES):
        if n % d == 0:
            best = d
    return n if best is None else best


def _pick_rows(m, cap):
    best = None
    for d in range(16, min(m, cap) + 1, 16):
        if m % d == 0:
            best = d
    return m if best is None else best


def _params(sem):
    return pltpu.CompilerParams(dimension_semantics=sem, vmem_limit_bytes=VMEM_LIMIT)


def _mm(a, b, *, name, la=None, lb=None, ta=False, tb=False, bias=None, add=None, out_dtype=F32,
        out_layers=None, out_layer=None, after=None, caps=(1024, 1536, 2048)):
    ar, ac = a.shape[-2:]
    br, bc = b.shape[-2:]
    m, k = (ac, ar) if ta else (ar, ac)
    k2, n = (bc, br) if tb else (br, bc)
    assert k == k2, (a.shape, b.shape, ta, tb)
    tm, tn, tk = _pick(m, caps[0]), _pick(n, caps[1]), _pick(k, caps[2])
    nk = k // tk
    grid = (m // tm, n // tn, nk)

    def spec(block, idx, layer):
        if layer is None:
            return pl.BlockSpec(block, idx)
        return pl.BlockSpec((None,) + block, lambda i, j, kk: (layer,) + idx(i, j, kk))

    a_spec = spec((tk, tm), lambda i, j, kk: (kk, i), la) if ta else spec((tm, tk), lambda i, j, kk: (i, kk), la)
    b_spec = spec((tn, tk), lambda i, j, kk: (j, kk), lb) if tb else spec((tk, tn), lambda i, j, kk: (kk, j), lb)
    in_specs, operands = [a_spec, b_spec], [a, b]
    if bias is not None:
        in_specs.append(pl.BlockSpec((1, tn), lambda i, j, kk: (0, j)))
        operands.append(bias)
    if add is not None:
        in_specs.append(pl.BlockSpec((tm, tn), lambda i, j, kk: (i, j)))
        operands.append(add)
    if after is not None:
        in_specs.append(pl.BlockSpec(memory_space=pl.ANY))
        operands.append(after)
    if out_layers is None:
        out_shape = jax.ShapeDtypeStruct((m, n), out_dtype)
    else:
        out_shape = jax.ShapeDtypeStruct((out_layers, m, n), out_dtype)
    out_spec = spec((tm, tn), lambda i, j, kk: (i, j), out_layer)
    dims = (((0 if ta else 1,), (1 if tb else 0,)), ((), ()))
    has_bias, has_add, has_alias = bias is not None, add is not None, after is not None

    def body(*refs):
        a_ref, b_ref = refs[0], refs[1]
        pos = 2
        bias_ref = add_ref = None
        if has_bias:
            bias_ref = refs[pos]
            pos += 1
        if has_add:
            add_ref = refs[pos]
            pos += 1
        if has_alias:
            pos += 1
        o_ref = refs[pos]
        acc_ref = refs[pos + 1] if nk > 1 else None
        part = lax.dot_general(a_ref[...].astype(BF16), b_ref[...].astype(BF16), dims, preferred_element_type=F32)

        def finish(total):
            if has_bias:
                total = total + bias_ref[...]
            if has_add:
                total = total + add_ref[...]
            o_ref[...] = total.astype(o_ref.dtype)

        if nk == 1:
            finish(part)
        else:
            kk = pl.program_id(2)

            @pl.when(kk == 0)
            def _():
                acc_ref[...] = part

            @pl.when(kk > 0)
            def _():
                acc_ref[...] += part

            @pl.when(kk == nk - 1)
            def _():
                finish(acc_ref[...])

    return pl.pallas_call(
        body, name=name, grid=grid, in_specs=in_specs, out_specs=out_spec, out_shape=out_shape,
        scratch_shapes=[pltpu.VMEM((tm, tn), F32)] if nk > 1 else [],
        compiler_params=_params(("parallel", "parallel", "arbitrary")),
    )(*operands)


def _rowwise(fn, rows, whole, outs, sums=(), *, name, tm=256):
    m = rows[0].shape[-2]
    tm = _pick_rows(m, tm)
    n_rows, n_whole, n_outs, n_sums = len(rows), len(whole), len(outs), len(sums)

    def rspec(shape):
        lead = len(shape) - 2
        return pl.BlockSpec(tuple(shape[:-2]) + (tm, shape[-1]), lambda i: (0,) * lead + (i, 0))

    def wspec(shape):
        return pl.BlockSpec(tuple(shape), lambda i: (0,) * len(shape))

    def body(*refs):
        vals = [r[...] for r in refs[:n_rows + n_whole]]
        out_refs = refs[n_rows + n_whole:n_rows + n_whole + n_outs]
        sum_refs = refs[n_rows + n_whole + n_outs:]
        res = fn(*vals)
        for ref, val in zip(out_refs, res[:n_outs]):
            ref[...] = val.astype(ref.dtype)
        if n_sums:
            @pl.when(pl.program_id(0) == 0)
            def _():
                for ref in sum_refs:
                    ref[...] = jnp.zeros(ref.shape, ref.dtype)

            for ref, val in zip(sum_refs, res[n_outs:]):
                ref[...] += val

    result = pl.pallas_call(
        body, name=name, grid=(m // tm,),
        in_specs=[rspec(r.shape) for r in rows] + [wspec(w.shape) for w in whole],
        out_specs=[rspec(s) for s, _ in outs] + [wspec(s) for s, _ in sums],
        out_shape=[jax.ShapeDtypeStruct(s, d) for s, d in list(outs) + list(sums)],
        compiler_params=_params(("arbitrary",)),
    )(*rows, *whole)
    return result


def _sigmoid(v):
    return jax.nn.sigmoid(v)


def _col_sum(v):
    return jnp.sum(v, axis=0, keepdims=True)


def _ln_stats(z):
    mu = jnp.mean(z, axis=-1, keepdims=True)
    zc = z - mu
    var = jnp.mean(zc * zc, axis=-1, keepdims=True)
    rstd = lax.rsqrt(var + LN_EPS)
    return zc * rstd, rstd


def _ln_fwd_fn(xin, h, gain, bias):
    xhat, _ = _ln_stats(ALPHA * xin + h)
    y = xhat * gain + bias
    return y, y


def _ple_ln_fwd_fn(xin, pg, pu, gain, bias):
    xhat, _ = _ln_stats(ALPHA * xin + _sigmoid(pg) * pu)
    y = xhat * gain + bias
    return y, y


def _ln_dz(dy, z, gain):
    xhat, rstd = _ln_stats(z)
    dxhat = dy * gain
    dz = rstd * (dxhat - jnp.mean(dxhat, axis=-1, keepdims=True)
                 - xhat * jnp.mean(dxhat * xhat, axis=-1, keepdims=True))
    return dz, _col_sum(dy * xhat), _col_sum(dy)


def _ln_bwd_fn(dy, xin, h, gain):
    dz, dgain, dbias = _ln_dz(dy, ALPHA * xin + h, gain)
    return ALPHA * dz, dz, dgain, dbias, _col_sum(dz)


def _ple_ln_bwd_fn(dy, xin, pg, pu, gain):
    sg = _sigmoid(pg)
    dz, dgain, dbias = _ln_dz(dy, ALPHA * xin + sg * pu, gain)
    dpg = dz * pu * sg * (1.0 - sg)
    return ALPHA * dz, dpg, dz * sg, dgain, dbias, _col_sum(dpg)


def _swiglu_fwd_fn(gu):
    hid = gu.shape[-1] // 2
    gate, up = gu[:, :hid], gu[:, hid:]
    return (gate * _sigmoid(gate) * up,)


def _swiglu_bwd_fn(gu, dact):
    hid = gu.shape[-1] // 2
    gate, up = gu[:, :hid], gu[:, hid:]
    sg = _sigmoid(gate)
    dgate = dact * up * sg * (1.0 + gate * (1.0 - sg))
    dup = dact * gate * sg
    return (jnp.concatenate([dgate, dup], axis=-1),)


def _loss_fn(y, target):
    err = y - target
    inv = 1.0 / y.shape[-1]
    part = 0.5 * inv * jnp.sum(jnp.sum(err * err, axis=-1, keepdims=True), axis=0, keepdims=True)
    return err * inv, jnp.broadcast_to(part, (1, LANES))


def _adam_fn(w, mom, vel, p_own, p_sib):
    g = p_own + p_sib
    m_new = ADAM_B1 * mom + (1.0 - ADAM_B1) * g
    v_new = ADAM_B2 * vel + (1.0 - ADAM_B2) * (g * g)
    m_hat = m_new / (1.0 - ADAM_B1 ** ADAM_STEP)
    v_hat = v_new / (1.0 - ADAM_B2 ** ADAM_STEP)
    delta = -ADAM_LR * (m_hat / (jnp.sqrt(v_hat) + ADAM_EPS) + ADAM_WD * w)
    return g, delta, m_new, v_new


def _sum_slots_fn(slots):
    acc = slots[0].astype(F32)
    for s in range(1, slots.shape[0]):
        acc = acc + slots[s].astype(F32)
    return (acc,)


def _split2(x):
    hi = x.astype(BF16)
    return hi, (x - hi.astype(F32)).astype(BF16)


def _dot3(a, b, dims):
    a_hi, a_lo = _split2(a)
    b_hi, b_lo = _split2(b)
    dn = (dims, ((), ()))
    return (lax.dot_general(a_hi, b_hi, dn, preferred_element_type=F32)
            + (lax.dot_general(a_hi, b_lo, dn, preferred_element_type=F32)
               + lax.dot_general(a_lo, b_hi, dn, preferred_element_type=F32)))


def _tdot(mask01, b):
    m = mask01.astype(BF16)
    b_hi = b.astype(BF16)
    rest = b - b_hi.astype(F32)
    b_mid = rest.astype(BF16)
    b_lo = (rest - b_mid.astype(F32)).astype(BF16)
    dn = (((1,), (0,)), ((), ()))
    return (lax.dot_general(m, b_hi, dn, preferred_element_type=F32)
            + (lax.dot_general(m, b_mid, dn, preferred_element_type=F32)
               + lax.dot_general(m, b_lo, dn, preferred_element_type=F32)))


def _hdot(a, b):
    return _dot3(a, b, ((1,), (0,)))


def _hdot_nt(a, b):
    return _dot3(a, b, ((1,), (1,)))


def _hdot_tn(a, b):
    return _dot3(a, b, ((0,), (0,)))


def _dot(a, b):
    return lax.dot_general(a.astype(BF16), b.astype(BF16), (((1,), (0,)), ((), ())), preferred_element_type=F32)


def _dot_nt(a, b):
    return lax.dot_general(a.astype(BF16), b.astype(BF16), (((1,), (1,)), ((), ())), preferred_element_type=F32)


def _dot_tn(a, b):
    return lax.dot_general(a.astype(BF16), b.astype(BF16), (((0,), (0,)), ((), ())), preferred_element_type=F32)


def _hg_masks():
    c = HG_CHUNK
    row = lax.broadcasted_iota(jnp.int32, (c, c), 0)
    col = lax.broadcasted_iota(jnp.int32, (c, c), 1)
    base = row & (-HG_SUB)
    return row, col, base, col <= row, col < base


def _hg_gates(qr, fr, alb):
    lbound = _sigmoid(alb[0:1, :] - alb[1:2, :])
    sig = _sigmoid(fr)
    forget = lbound + (1.0 - lbound) * sig
    kk = (1.0 - lbound) * _sigmoid(-fr)
    qt = qr * _sigmoid(qr) * (HG_DK ** -0.5)
    return qt, kk, jnp.log(forget), lbound, sig, forget


def _hg_scores(qt, kk, g):
    c, nsub = HG_CHUNK, HG_CHUNK // HG_SUB
    row, col, base, causal, below = _hg_masks()
    b = _tdot(causal, g)
    rr = _tdot(below, g)
    bq = b - rr
    qh = qt * jnp.exp(bq)
    edecs = [None]
    parts = [jnp.zeros((HG_SUB, c), F32)]
    for i in range(1, nsub):
        edec = jnp.exp(jnp.minimum(rr[i * HG_SUB:i * HG_SUB + 1, :] - b, 0.0))
        edecs.append(edec)
        parts.append(_dot_nt(qh[i * HG_SUB:(i + 1) * HG_SUB, :], kk * edec))
    a = jnp.where(below, jnp.concatenate(parts, axis=0), 0.0)
    b3 = b.reshape(nsub, HG_SUB, HG_DK)
    q3 = qt.reshape(nsub, HG_SUB, HG_DK)
    k3 = kk.reshape(nsub, HG_SUB, HG_DK)
    for j in range(HG_SUB):
        e = jnp.exp(jnp.minimum(b3 - b3[:, j:j + 1, :], 0.0))
        colv = jnp.sum(q3 * e * k3[:, j:j + 1, :], axis=-1, keepdims=True).reshape(c, 1)
        a = jnp.where(col == base + j, colv, a)
    a = jnp.where(causal, a, 0.0)
    return a, b, bq, qh, edecs, (b3, q3, k3)


def _hg_norm(o, gr, gain):
    r = lax.rsqrt(jnp.mean(o * o, axis=-1, keepdims=True) + RMS_EPS)
    sg = _sigmoid(gr)
    return o * r * gain, r, sg


def _hgrn2_fwd(proj, alb, gain, *, rb):
    m, d4 = proj.shape
    d = d4 // 4
    heads = d // HG_DK
    rb = min(rb, m)
    cpb = rb // HG_CHUNK
    nrb = m // rb

    def body(q_ref, f_ref, v_ref, g_ref, alb_ref, gain_ref, o_ref, og_ref, st_ref, state):
        @pl.when(pl.program_id(1) == 0)
        def _():
            state[...] = jnp.zeros(state.shape, F32)

        def chunk(ci, carry):
            sl = pl.ds(pl.multiple_of(ci * HG_CHUNK, HG_CHUNK), HG_CHUNK)
            qt, kk, g, _, _, _ = _hg_gates(q_ref[sl, :], f_ref[sl, :], alb_ref[...])
            v = v_ref[sl, :]
            st = state[...]
            st_ref[ci] = st
            a, b, _, _, _, _ = _hg_scores(qt, kk, g)
            o = _dot(a, v) + _dot_nt(qt * jnp.exp(b), st)
            b_last = b[HG_CHUNK - 1:HG_CHUNK, :]
            state[...] = st * jnp.exp(b_last) + _hdot_tn(v, kk * jnp.exp(b_last - b))
            o_ref[sl, :] = o
            n, _, sg = _hg_norm(o, g_ref[sl, :], gain_ref[...])
            og_ref[sl, :] = (n * g_ref[sl, :] * sg).astype(og_ref.dtype)
            return carry

        lax.fori_loop(0, cpb, chunk, 0, unroll=2)

    def col(cidx):
        return pl.BlockSpec((rb, HG_DK), lambda h, r: (r, cidx * heads + h))

    return pl.pallas_call(
        body, name="hgrn2_fwd", grid=(heads, nrb),
        in_specs=[col(0), col(1), col(2), col(3),
                  pl.BlockSpec((2, HG_DK), lambda h, r: (0, h)),
                  pl.BlockSpec((1, HG_DK), lambda h, r: (0, 0))],
        out_specs=[pl.BlockSpec((rb, HG_DK), lambda h, r: (r, h)),
                   pl.BlockSpec((rb, HG_DK), lambda h, r: (r, h)),
                   pl.BlockSpec((None, cpb, HG_DK, HG_DK), lambda h, r: (h, r, 0, 0))],
        out_shape=[jax.ShapeDtypeStruct((m, d), F32), jax.ShapeDtypeStruct((m, d), BF16),
                   jax.ShapeDtypeStruct((heads, m // HG_CHUNK, HG_DK, HG_DK), F32)],
        scratch_shapes=[pltpu.VMEM((HG_DK, HG_DK), F32)],
        compiler_params=_params(("parallel", "arbitrary")),
    )(proj, proj, proj, proj, alb, gain)


def _hgrn2_bwd(proj, o_pre, states, dog, alb, gain, *, rb):
    m, d4 = proj.shape
    d = d4 // 4
    heads = d // HG_DK
    rb = min(rb, m)
    cpb = rb // HG_CHUNK
    nrb = m // rb
    c, nsub = HG_CHUNK, HG_CHUNK // HG_SUB

    def body(q_ref, f_ref, v_ref, g_ref, o_ref, st_ref, dog_ref, alb_ref, gain_ref,
             dq_ref, df_ref, dv_ref, dg_ref, dalb_ref, dgain_ref, dstate, carry_ref):
        first = (pl.program_id(0) == 0) & (pl.program_id(1) == 0)

        @pl.when(first)
        def _():
            dgain_ref[...] = jnp.zeros(dgain_ref.shape, F32)

        @pl.when(pl.program_id(1) == 0)
        def _():
            dstate[...] = jnp.zeros(dstate.shape, F32)
            carry_ref[...] = jnp.zeros(carry_ref.shape, F32)
            dalb_ref[...] = jnp.zeros(dalb_ref.shape, F32)

        row, col, base, causal, below = _hg_masks()
        sub_iota = lax.broadcasted_iota(jnp.int32, (nsub, HG_SUB, HG_DK), 1)
        row_k = lax.broadcasted_iota(jnp.int32, (c, HG_DK), 0)
        upper = col >= row

        def chunk(step, carry):
            ci = cpb - 1 - step
            sl = pl.ds(pl.multiple_of(ci * HG_CHUNK, HG_CHUNK), HG_CHUNK)
            qr, fr, v, gr = q_ref[sl, :], f_ref[sl, :], v_ref[sl, :], g_ref[sl, :]
            qt, kk, g, lbound, sig, forget = _hg_gates(qr, fr, alb_ref[...])
            o = o_ref[sl, :]
            dogv = dog_ref[sl, :]
            gain_v = gain_ref[...]
            n, r, sg = _hg_norm(o, gr, gain_v)
            dgr = dogv * n * sg * (1.0 + gr * (1.0 - sg))
            dn = dogv * gr * sg
            dgain_ref[...] += _col_sum(dn * o * r)
            u = dn * gain_v
            d_o = r * u - o * (r * r * r) * jnp.mean(u * o, axis=-1, keepdims=True)
            st0 = st_ref[ci]
            dst = dstate[...]
            a, b, bq, qh, edecs, (b3, q3, k3) = _hg_scores(qt, kk, g)
            eb = jnp.exp(b)
            b_last = b[c - 1:c, :]
            kdl_dec = jnp.exp(b_last - b)
            kdl = kk * kdl_dec
            d_a = jnp.where(causal, _dot_nt(d_o, v), 0.0)
            d_at = _dot_nt(v, d_o)
            dv = _dot_tn(a, d_o) + _dot_nt(kdl, dst)
            dq = eb * _hdot(d_o, st0)
            dk = _hdot(v, dst) * kdl_dec
            d_a_below = jnp.where(below, d_a, 0.0)
            dq_parts = [jnp.zeros((HG_SUB, HG_DK), F32)]
            for i in range(1, nsub):
                lo, hi = i * HG_SUB, (i + 1) * HG_SUB
                dq_parts.append(_hdot(d_a_below[lo:hi, :], kk * edecs[i]))
                gi = _hdot(d_at[:, lo:hi], qh[lo:hi, :])
                dk = dk + jnp.where(row_k < lo, edecs[i] * gi, 0.0)
            dq = dq + jnp.concatenate(dq_parts, axis=0) * jnp.exp(bq)
            dq3 = jnp.zeros((nsub, HG_SUB, HG_DK), F32)
            dk3 = jnp.zeros((nsub, HG_SUB, HG_DK), F32)
            for j in range(HG_SUB):
                e = jnp.exp(jnp.minimum(b3 - b3[:, j:j + 1, :], 0.0))
                dcol = jnp.sum(jnp.where(col == base + j, d_a, 0.0), axis=-1, keepdims=True)
                t1 = dcol.reshape(nsub, HG_SUB, 1) * e
                dq3 = dq3 + t1 * k3[:, j:j + 1, :]
                dk3 = jnp.where(sub_iota == j, jnp.sum(t1 * q3, axis=1, keepdims=True), dk3)
            dq = dq + dq3.reshape(c, HG_DK)
            dk = dk + dk3.reshape(c, HG_DK)
            dstate[...] = dst * jnp.exp(b_last) + _hdot_tn(d_o, qt * eb)
            dglog = _tdot(upper, qt * dq - kk * dk) + carry_ref[...]
            carry_ref[...] = dglog[0:1, :]
            dforget = dglog / forget
            one_m_lb = 1.0 - lbound
            dsig = (dforget - dk) * one_m_lb
            sneg = _sigmoid(-fr)
            dlb = _col_sum(dforget * (1.0 - sig) - dk * sneg)
            dalb0 = dlb * lbound * one_m_lb
            dalb_ref[...] += jnp.concatenate([dalb0, -dalb0], axis=0)
            sq = _sigmoid(qr)
            dq_ref[sl, :] = (dq * (HG_DK ** -0.5) * sq * (1.0 + qr * (1.0 - sq))).astype(dq_ref.dtype)
            df_ref[sl, :] = (dsig * sig * (1.0 - sig)).astype(df_ref.dtype)
            dv_ref[sl, :] = dv.astype(dv_ref.dtype)
            dg_ref[sl, :] = dgr.astype(dg_ref.dtype)
            return carry

        lax.fori_loop(0, cpb, chunk, 0, unroll=2)

    def rev(r):
        return nrb - 1 - r

    def col(cidx):
        return pl.BlockSpec((rb, HG_DK), lambda h, r: (rev(r), cidx * heads + h))

    def head_rows():
        return pl.BlockSpec((rb, HG_DK), lambda h, r: (rev(r), h))

    return pl.pallas_call(
        body, name="hgrn2_bwd", grid=(heads, nrb),
        in_specs=[col(0), col(1), col(2), col(3), head_rows(),
                  pl.BlockSpec((None, cpb, HG_DK, HG_DK), lambda h, r: (h, rev(r), 0, 0)),
                  head_rows(),
                  pl.BlockSpec((2, HG_DK), lambda h, r: (0, h)),
                  pl.BlockSpec((1, HG_DK), lambda h, r: (0, 0))],
        out_specs=[head_rows(), head_rows(), head_rows(), head_rows(),
                   pl.BlockSpec((2, HG_DK), lambda h, r: (0, h)),
                   pl.BlockSpec((1, HG_DK), lambda h, r: (0, 0))],
        out_shape=[jax.ShapeDtypeStruct((m, d), BF16)] * 4
                  + [jax.ShapeDtypeStruct((2, d), F32), jax.ShapeDtypeStruct((1, HG_DK), F32)],
        scratch_shapes=[pltpu.VMEM((HG_DK, HG_DK), F32), pltpu.VMEM((1, HG_DK), F32)],
        compiler_params=_params(("arbitrary", "arbitrary")),
    )(proj, proj, proj, proj, o_pre, states, dog, alb, gain)


def _swa_probs(qh, kp, kc, sink, slope, has_prev):
    rows = qh.shape[0]
    qi = lax.broadcasted_iota(jnp.int32, (rows, WINDOW), 0) & (WINDOW - 1)
    si = lax.broadcasted_iota(jnp.int32, (rows, WINDOW), 1)
    scale = ATT_HD ** -0.5
    dist_c = (qi - si).astype(F32)
    s_p = _dot_nt(qh, kp) * scale - slope * (dist_c + float(WINDOW))
    s_c = _dot_nt(qh, kc) * scale - slope * dist_c
    s_p = jnp.where((si > qi) & has_prev, s_p, NEG)
    s_c = jnp.where(si <= qi, s_c, NEG)
    mx = jnp.maximum(jnp.maximum(jnp.max(s_p, axis=-1, keepdims=True), jnp.max(s_c, axis=-1, keepdims=True)), sink)
    e_p, e_c, e_s = jnp.exp(s_p - mx), jnp.exp(s_c - mx), jnp.exp(sink - mx)
    inv = 1.0 / (jnp.sum(e_p, axis=-1, keepdims=True) + jnp.sum(e_c, axis=-1, keepdims=True) + e_s)
    return e_p * inv, e_c * inv, e_s * inv


def _slope(h, n_heads):
    return float(2.0 ** (-8.0 * (h + 1) / n_heads))


def _swa_group(ref_vals, sink_ref, kh, n_heads):
    heads = [kh * ATT_G + g for g in range(ATT_G)]
    stacked = [jnp.concatenate([v[:, h * ATT_HD:(h + 1) * ATT_HD] for h in heads], axis=0) for v in ref_vals]
    grp = lax.shift_right_logical(lax.broadcasted_iota(jnp.int32, (ATT_G * WINDOW, 1), 0), WINDOW.bit_length() - 1)
    slope = jnp.zeros((ATT_G * WINDOW, 1), F32)
    sink = jnp.zeros((ATT_G * WINDOW, 1), F32)
    for g, h in enumerate(heads):
        slope = jnp.where(grp == g, _slope(h, n_heads), slope)
        sink = jnp.where(grp == g, sink_ref[:, h:h + 1], sink)
    return stacked, slope, sink


def _swa_fwd(q, kv, sinks):
    m, d = q.shape
    n_heads = d // ATT_HD
    kvh = n_heads // ATT_G
    kd = kvh * ATT_HD
    nb = m // WINDOW

    def body(q_ref, kvp_ref, kvc_ref, sink_ref, o_ref):
        has_prev = pl.program_id(0) > 0
        qv, kvp, kvc = q_ref[...], kvp_ref[...], kvc_ref[...]
        outs = []
        for kh in range(kvh):
            ks = slice(kh * ATT_HD, (kh + 1) * ATT_HD)
            vs = slice(kd + kh * ATT_HD, kd + (kh + 1) * ATT_HD)
            (q4,), slope, sink = _swa_group([qv], sink_ref, kh, n_heads)
            p_p, p_c, _ = _swa_probs(q4, kvp[:, ks], kvc[:, ks], sink, slope, has_prev)
            o4 = _dot(p_p, kvp[:, vs]) + _dot(p_c, kvc[:, vs])
            outs += [o4[g * WINDOW:(g + 1) * WINDOW, :] for g in range(ATT_G)]
        o_ref[...] = jnp.concatenate(outs, axis=-1).astype(o_ref.dtype)

    return pl.pallas_call(
        body, name="swa_fwd", grid=(nb,),
        in_specs=[pl.BlockSpec((WINDOW, d), lambda n: (n, 0)),
                  pl.BlockSpec((WINDOW, 2 * kd), lambda n: (jnp.maximum(n - 1, 0), 0)),
                  pl.BlockSpec((WINDOW, 2 * kd), lambda n: (n, 0)),
                  pl.BlockSpec((1, n_heads), lambda n: (0, 0))],
        out_specs=pl.BlockSpec((WINDOW, d), lambda n: (n, 0)),
        out_shape=jax.ShapeDtypeStruct((m, d), BF16),
        compiler_params=_params(("arbitrary",)),
    )(q, kv, kv, sinks)


def _swa_bwd(q, kv, sinks, dao):
    m, d = q.shape
    n_heads = d // ATT_HD
    kvh = n_heads // ATT_G
    kd = kvh * ATT_HD
    nb = m // WINDOW
    scale = ATT_HD ** -0.5

    def body(q_ref, kvp_ref, kvc_ref, sink_ref, do_ref, dq_ref, dkvc_ref, dkvp_ref, dqsum_ref, dsink_ref):
        @pl.when(pl.program_id(0) == 0)
        def _():
            dqsum_ref[...] = jnp.zeros(dqsum_ref.shape, F32)
            dsink_ref[...] = jnp.zeros(dsink_ref.shape, F32)

        has_prev = pl.program_id(0) > 0
        qv, kvp, kvc, dov = q_ref[...], kvp_ref[...], kvc_ref[...], do_ref[...]
        lane_h = lax.broadcasted_iota(jnp.int32, (1, n_heads), 1)
        dsink = jnp.zeros((1, n_heads), F32)
        dq_parts, dk_p, dk_c, dv_p, dv_c = [], [], [], [], []
        for kh in range(kvh):
            ks = slice(kh * ATT_HD, (kh + 1) * ATT_HD)
            vs = slice(kd + kh * ATT_HD, kd + (kh + 1) * ATT_HD)
            kp, kc, vp, vc = kvp[:, ks], kvc[:, ks], kvp[:, vs], kvc[:, vs]
            (q4, do4), slope, sink = _swa_group([qv, dov], sink_ref, kh, n_heads)
            p_p, p_c, p_s = _swa_probs(q4, kp, kc, sink, slope, has_prev)
            dp_p, dp_c = _dot_nt(do4, vp), _dot_nt(do4, vc)
            delta = jnp.sum(p_p * dp_p, axis=-1, keepdims=True) + jnp.sum(p_c * dp_c, axis=-1, keepdims=True)
            ds_p, ds_c = p_p * (dp_p - delta), p_c * (dp_c - delta)
            sink_term = p_s * delta
            dq4 = (_dot(ds_p, kp) + _dot(ds_c, kc)) * scale
            for g in range(ATT_G):
                rows = slice(g * WINDOW, (g + 1) * WINDOW)
                dsink = dsink + jnp.where(lane_h == kh * ATT_G + g, -_col_sum(sink_term[rows, :]), 0.0)
                dq_parts.append(dq4[rows, :])
            dk_p.append(_dot_tn(ds_p, q4) * scale)
            dk_c.append(_dot_tn(ds_c, q4) * scale)
            dv_p.append(_dot_tn(p_p, do4))
            dv_c.append(_dot_tn(p_c, do4))
        dq = jnp.concatenate(dq_parts, axis=-1)
        dq_ref[...] = dq.astype(dq_ref.dtype)
        dqsum_ref[...] += _col_sum(dq)
        dsink_ref[...] += dsink
        dkvc_ref[...] = jnp.concatenate(dk_c + dv_c, axis=-1)
        dkvp_ref[...] = jnp.concatenate(dk_p + dv_p, axis=-1)

    return pl.pallas_call(
        body, name="swa_bwd", grid=(nb,),
        in_specs=[pl.BlockSpec((WINDOW, d), lambda n: (n, 0)),
                  pl.BlockSpec((WINDOW, 2 * kd), lambda n: (jnp.maximum(n - 1, 0), 0)),
                  pl.BlockSpec((WINDOW, 2 * kd), lambda n: (n, 0)),
                  pl.BlockSpec((1, n_heads), lambda n: (0, 0)),
                  pl.BlockSpec((WINDOW, d), lambda n: (n, 0))],
        out_specs=[pl.BlockSpec((WINDOW, d), lambda n: (n, 0)),
                   pl.BlockSpec((WINDOW, 2 * kd), lambda n: (n, 0)),
                   pl.BlockSpec((WINDOW, 2 * kd), lambda n: (n, 0)),
                   pl.BlockSpec((1, d), lambda n: (0, 0)),
                   pl.BlockSpec((1, n_heads), lambda n: (0, 0))],
        out_shape=[jax.ShapeDtypeStruct((m, d), BF16), jax.ShapeDtypeStruct((m, 2 * kd), F32),
                   jax.ShapeDtypeStruct((m, 2 * kd), F32), jax.ShapeDtypeStruct((1, d), F32),
                   jax.ShapeDtypeStruct((1, n_heads), F32)],
        compiler_params=_params(("arbitrary",)),
    )(q, kv, kv, sinks, dao)


def _kv_grad_combine(dkv_cur, dkv_prev):
    m, w = dkv_cur.shape
    nb = m // WINDOW

    def body(cur_ref, nxt_ref, o_ref, sum_ref):
        @pl.when(pl.program_id(0) == 0)
        def _():
            sum_ref[...] = jnp.zeros(sum_ref.shape, F32)

        total = cur_ref[...] + jnp.where(pl.program_id(0) < nb - 1, nxt_ref[...], 0.0)
        o_ref[...] = total.astype(o_ref.dtype)
        sum_ref[...] += _col_sum(total)

    return pl.pallas_call(
        body, name="kv_grad_combine", grid=(nb,),
        in_specs=[pl.BlockSpec((WINDOW, w), lambda n: (n, 0)),
                  pl.BlockSpec((WINDOW, w), lambda n: (jnp.minimum(n + 1, nb - 1), 0))],
        out_specs=[pl.BlockSpec((WINDOW, w), lambda n: (n, 0)), pl.BlockSpec((1, w), lambda n: (0, 0))],
        out_shape=[jax.ShapeDtypeStruct((m, w), BF16), jax.ShapeDtypeStruct((1, w), F32)],
        compiler_params=_params(("arbitrary",)),
    )(dkv_cur, dkv_prev)


def _row(v):
    return v.reshape(1, -1)


def _local_step(x, p, target, wget, grad_sink, ln_gain, ln_bias, alb, norm_gain, kv_b, b_q, sinks, b_out, ple_b):
    gs = {}
    gains = [[_row(ln_gain[i, j]) for j in range(3)] for i in range(DEPTH)]
    biases = [[_row(ln_bias[i, j]) for j in range(3)] for i in range(DEPTH)]
    sd = x.shape
    pending = [None]

    def mm(a, b, lb=0, **kw):
        after, pending[0] = pending[0], None
        return _mm(a, b, lb=lb, after=after, **kw)

    def ln_fwd(xin, h, i, j, nm):
        return _rowwise(_ln_fwd_fn, [xin, h], [gains[i][j], biases[i][j]], [(sd, F32), (sd, BF16)], name=nm)

    def tail_fwd(xa, i):
        gu = _mm(xa[1], wget("ffn_w_gate_up", i, xa[1]), lb=0, name=f"ffn_up{i}")
        act, = _rowwise(_swiglu_fwd_fn, [gu], [], [((sd[0], gu.shape[1] // 2), BF16)], name=f"swiglu{i}", tm=128)
        f = _mm(act, wget("ffn_w_down", i, act), lb=0, name=f"ffn_down{i}")
        xb = ln_fwd(xa[0], f, i, 1, f"ln_ffn{i}")
        pg = _mm(xb[1], wget("ple_w_gate", i, act), lb=0, bias=_row(ple_b[i]), name=f"ple_gate{i}")
        pu = _mm(p[i], wget("ple_w_up", i, act), lb=0, name=f"ple_up{i}")
        xc = _rowwise(_ple_ln_fwd_fn, [xb[0], pg, pu], [gains[i][2], biases[i][2]], [(sd, F32), (sd, BF16)],
                      name=f"ln_ple{i}")
        return dict(xa=xa, gu=gu, act=act, f=f, xb=xb, pg=pg, pu=pu), xc

    def tail_bwd(dxc, sv, i):
        xa, xb = sv["xa"], sv["xb"]
        dxb_part, dpg, dpu, dg2, db2, dbg = _rowwise(
            _ple_ln_bwd_fn, [dxc, xb[0], sv["pg"], sv["pu"]], [gains[i][2]],
            [(sd, F32), (sd, BF16), (sd, BF16)], [((1, sd[1]), F32)] * 3, name=f"ln_ple_bwd{i}")
        gs[f"ple_b_{i}"] = dbg
        gs[f"ln_gain_{i}_2"], gs[f"ln_bias_{i}_2"] = dg2, db2
        grad_of("ple_w_gate", i, xb[1], dpg)
        grad_of("ple_w_up", i, p[i], dpu)
        dxb = mm(dpg, wget("ple_w_gate", i, None), tb=True, add=dxb_part, name=f"ple_gate_dx{i}")
        dxa_part, df, dg1, db1, _ = _rowwise(
            _ln_bwd_fn, [dxb, xa[0], sv["f"]], [gains[i][1]],
            [(sd, F32), (sd, BF16)], [((1, sd[1]), F32)] * 3, name=f"ln_ffn_bwd{i}")
        gs[f"ln_gain_{i}_1"], gs[f"ln_bias_{i}_1"] = dg1, db1
        grad_of("ffn_w_down", i, sv["act"], df)
        dact = mm(df, wget("ffn_w_down", i, None), tb=True, name=f"ffn_down_dx{i}")
        dgu, = _rowwise(_swiglu_bwd_fn, [sv["gu"], dact], [], [(sv["gu"].shape, BF16)], name=f"swiglu_bwd{i}", tm=128)
        grad_of("ffn_w_gate_up", i, xa[1], dgu)
        return mm(dgu, wget("ffn_w_gate_up", i, None), tb=True, add=dxa_part, name=f"ffn_up_dx{i}")

    def grad_of(nm, i, act, dout):
        grad = mm(act, dout, lb=None, ta=True, out_dtype=BF16, out_layers=1, out_layer=0, name=f"grad_{nm}{i}")
        token = grad_sink(nm, i, grad)
        if token is not None:
            pending[0] = token

    def mixer_ln_bwd(dxa, xin, h, i):
        dx_part, dh, dg0, db0, dhsum = _rowwise(
            _ln_bwd_fn, [dxa, xin, h], [gains[i][0]],
            [(sd, F32), (sd, BF16)], [((1, sd[1]), F32)] * 3, name=f"ln_mix_bwd{i}")
        gs[f"ln_gain_{i}_0"], gs[f"ln_bias_{i}_0"] = dg0, db0
        return dx_part, dh, dhsum

    proj = _mm(x, wget("a_w_in", 0, None), lb=0, name="hg_proj")
    o_pre, og, states = _hgrn2_fwd(proj, alb, norm_gain, rb=HG_ROWS)
    h0 = _mm(og, wget("a_w_out", 0, og), lb=0, name="hg_out")
    x1 = ln_fwd(x, h0, 0, 0, "ln_mix0")
    sv0, x3 = tail_fwd(x1, 0)
    kv = _mm(x3[1], wget("kv_w", 0, x3[1]), lb=0, bias=_row(kv_b), name="kv_proj")
    q = _mm(x3[1], wget("b_w_q", 0, x3[1]), lb=0, bias=b_q, name="q_proj")
    ao = _swa_fwd(q, kv, sinks)
    h1 = _mm(ao, wget("b_w_out", 0, x3[1]), lb=0, bias=b_out, name="att_out")
    x4 = ln_fwd(x3[0], h1, 1, 0, "ln_mix1")
    sv1, y = tail_fwd(x4, 1)
    dy, loss = _rowwise(_loss_fn, [y[0], target], [], [(sd, F32)], [((1, LANES), F32)], name="loss")

    dx4 = tail_bwd(dy, sv1, 1)
    dx3_part, dh1, dh1sum = mixer_ln_bwd(dx4, x3[0], h1, 1)
    gs["b_out"] = dh1sum
    grad_of("b_w_out", 0, ao, dh1)
    dao = mm(dh1, wget("b_w_out", 0, None), tb=True, name="att_out_dx")
    dq, dkv_cur, dkv_prev, dqsum, dsinks = _swa_bwd(q, kv, sinks, dao)
    gs["b_q"], gs["sinks"] = dqsum, dsinks
    dkv, dkvsum = _kv_grad_combine(dkv_cur, dkv_prev)
    gs["kv_b"] = dkvsum
    grad_of("b_w_q", 0, x3[1], dq)
    grad_of("kv_w", 0, x3[1], dkv)
    dx3 = mm(dq, wget("b_w_q", 0, None), tb=True, add=dx3_part, name="q_proj_dx")
    dx3 = mm(dkv, wget("kv_w", 0, None), tb=True, add=dx3, name="kv_proj_dx")
    dx1 = tail_bwd(dx3, sv0, 0)
    dx_part, dh0, _ = mixer_ln_bwd(dx1, x, h0, 0)
    grad_of("a_w_out", 0, og, dh0)
    dog = mm(dh0, wget("a_w_out", 0, None), tb=True, name="hg_out_dx")
    dqr, dfr, dvr, dgr, dalb, dgain = _hgrn2_bwd(proj, o_pre, states, dog, alb, norm_gain, rb=HG_ROWS)
    gs["alb"], gs["norm_gain"] = dalb, dgain
    dproj = jnp.concatenate([dqr, dfr, dvr, dgr], axis=1)
    grad_of("a_w_in", 0, x, dproj)
    grad_x = mm(dproj, wget("a_w_in", 0, None), tb=True, add=dx_part, name="hg_proj_dx")
    return loss, grad_x, gs


HBM_SPEC = pl.BlockSpec(memory_space=pl.ANY)
HBM_ONLY = pl.BlockSpec(memory_space=pltpu.HBM)
SEM_SPEC = pl.BlockSpec(memory_space=pltpu.SEMAPHORE)
SIDE_EFFECT = pltpu.SideEffectType.DATAFLOW_SIDE_EFFECTING


def _piece(ref, kind, j):
    _, r, c = ref.shape
    if kind == "row":
        return ref.at[:, pl.ds(j * (r // N_CHIPS), r // N_CHIPS), :]
    return ref.at[:, :, pl.ds(j * (c // N_CHIPS), c // N_CHIPS)]


def _chip_of(j, c):
    return (j // 2, j % 2, c)


def _in_hbm(a):
    return pltpu.with_memory_space_constraint(a, pltpu.HBM)


def _place(src, kind, chip, *, mode, name, out_dtype, zone=None, zone_shape=None, layer=0):
    if mode == "gather":
        _, r, c = src.shape
        out_shape = (1, r * N_CHIPS, c) if kind == "row" else (1, r, c * N_CHIPS)
    else:
        out_shape = tuple(zone.shape) if zone is not None else tuple(zone_shape)
        r, c = out_shape[-2:]
    tm = _pick_rows(r, 512)
    nb = r // tm

    def full_idx(i, chip_ref):
        return (0, chip_ref[0] * nb + i, 0) if kind == "row" else (0, i, chip_ref[0])

    if mode == "gather":
        in_spec = pl.BlockSpec((None, tm, c), lambda i, chip_ref: (0, i, 0))
        out_spec = pl.BlockSpec((None, tm, c), full_idx)
    else:
        in_spec = pl.BlockSpec((None, tm, c), full_idx)
        out_spec = pl.BlockSpec((None, None, tm, c), lambda i, chip_ref: (chip_ref[0], layer, i, 0))
    in_specs, operands, aliases = [in_spec], [src], {}
    if zone is not None:
        in_specs.append(HBM_SPEC)
        operands.append(zone)
        aliases = {2: 0}

    def body(chip_ref, src_ref, *rest):
        rest[-1][...] = src_ref[...].astype(rest[-1].dtype)

    return pl.pallas_call(
        body, name=name,
        grid_spec=pltpu.PrefetchScalarGridSpec(num_scalar_prefetch=1, grid=(nb,), in_specs=in_specs,
                                               out_specs=out_spec),
        out_shape=jax.ShapeDtypeStruct(out_shape, out_dtype),
        input_output_aliases=aliases,
        compiler_params=_params(("arbitrary",)),
    )(chip, *operands)


class _Exchange:
    def __init__(self, mode, srcs, lands, kinds, layers, name):
        self.mode, self.kinds, self.layers, self.name, self.n = mode, kinds, layers, name, len(lands)
        n, ns = self.n, len(srcs)
        sem_shape = pltpu.SemaphoreType.DMA((n * N_CHIPS,))

        def body(*refs):
            src_refs, land_refs = refs[:ns], refs[ns:ns + n]
            send_sems, recv_sems = refs[ns + n], refs[ns + n + 1]
            token = refs[-1]
            c = lax.axis_index("c")
            me = 2 * lax.axis_index("x") + lax.axis_index("y")
            for j in range(N_CHIPS):
                @pl.when(me == j)
                def _():
                    for a in range(n):
                        for t in range(N_CHIPS):
                            if t != j:
                                src, dst = self._ends(src_refs, land_refs, a, j, t)
                                pltpu.make_async_remote_copy(
                                    src_ref=src, dst_ref=dst, send_sem=send_sems.at[a * N_CHIPS + t],
                                    recv_sem=recv_sems.at[a * N_CHIPS + j],
                                    device_id=_chip_of(t, c), device_id_type=MESH).start()
            token[...] = jnp.zeros(token.shape, token.dtype)

        arrays = list(srcs) + list(lands)
        outs = pl.pallas_call(
            body, name=name + "_start",
            in_specs=[HBM_ONLY] * (ns + n),
            out_specs=[SEM_SPEC, SEM_SPEC] + [HBM_ONLY] * (ns + n) + [pl.BlockSpec(memory_space=pltpu.VMEM)],
            out_shape=[sem_shape, sem_shape] + [pltpu.HBM(a.shape, a.dtype) for a in arrays]
                      + [jax.ShapeDtypeStruct((8, LANES), F32)],
            input_output_aliases={i: i + 2 for i in range(ns + n)},
            compiler_params=pltpu.CompilerParams(has_side_effects=SIDE_EFFECT),
        )(*[_in_hbm(a) for a in arrays])
        self.send_sems, self.recv_sems = outs[0], outs[1]
        self.srcs, self.lands = list(outs[2:2 + ns]), list(outs[2 + ns:2 + ns + n])
        self.token = outs[-1]

    def _ends(self, src_refs, land_refs, a, me_j, peer):
        if self.mode == "gather":
            mine = _piece(land_refs[a], self.kinds[a], me_j)
            return mine, mine
        return _piece(src_refs[a], self.kinds[a], peer), land_refs[a].at[me_j, pl.ds(self.layers[a], 1)]

    def wait(self, after, lands=None):
        n, ns = self.n, len(self.srcs)
        lands = self.lands if lands is None else lands

        def body(*refs):
            src_refs, land_refs = refs[:ns], refs[ns:ns + n]
            send_sems, recv_sems = refs[ns + n], refs[ns + n + 1]
            c = lax.axis_index("c")
            me = 2 * lax.axis_index("x") + lax.axis_index("y")
            for j in range(N_CHIPS):
                @pl.when(me != j)
                def _():
                    for a in range(n):
                        sent, _ = self._ends(src_refs, land_refs, a, 0, j)
                        _, landed = self._ends(src_refs, land_refs, a, j, 0)
                        cp = pltpu.make_async_remote_copy(
                            src_ref=sent, dst_ref=landed, send_sem=send_sems.at[a * N_CHIPS + j],
                            recv_sem=recv_sems.at[a * N_CHIPS + j],
                            device_id=_chip_of(j, c), device_id_type=MESH)
                        cp.wait_send()
                        cp.wait_recv()

        arrays = self.srcs + list(lands)
        operands = [_in_hbm(a) for a in arrays] + [self.send_sems, self.recv_sems]
        in_specs = [HBM_ONLY] * (ns + n) + [SEM_SPEC, SEM_SPEC]
        if after is not None:
            operands.append(after)
            in_specs.append(HBM_SPEC)
        outs = pl.pallas_call(
            body, name=self.name + "_wait",
            in_specs=in_specs, out_specs=[HBM_ONLY] * (ns + n),
            out_shape=[pltpu.HBM(a.shape, a.dtype) for a in arrays],
            input_output_aliases={i: i for i in range(ns + n)},
            compiler_params=pltpu.CompilerParams(has_side_effects=SIDE_EFFECT),
        )(*operands)
        return list(outs[ns:])


def _sibling_swap(arrays, name):
    n = len(arrays)

    def body(*refs):
        ins, outs = refs[:n], refs[n:2 * n]
        send_sems, recv_sems = refs[2 * n:]
        sibling = (lax.axis_index("x"), lax.axis_index("y"), 1 - lax.axis_index("c"))
        copies = [pltpu.make_async_remote_copy(src_ref=ins[a], dst_ref=outs[a], send_sem=send_sems.at[a],
                                               recv_sem=recv_sems.at[a], device_id=sibling, device_id_type=MESH)
                  for a in range(n)]
        for cp in copies:
            cp.start()
        for cp in copies:
            cp.wait()

    return pl.pallas_call(
        body, name=name,
        in_specs=[HBM_SPEC] * n, out_specs=[HBM_SPEC] * n,
        out_shape=[jax.ShapeDtypeStruct(a.shape, a.dtype) for a in arrays],
        scratch_shapes=[pltpu.SemaphoreType.DMA((n,)), pltpu.SemaphoreType.DMA((n,))],
    )(*arrays)


def _gather_devices(vec):
    def body(in_ref, out_ref, send_sems, recv_sems, local_sem):
        x, y, c = lax.axis_index("x"), lax.axis_index("y"), lax.axis_index("c")
        me = 4 * x + 2 * y + c
        mine = pltpu.make_async_copy(in_ref, out_ref.at[me], local_sem)
        mine.start()
        copies = []
        for rel in range(1, N_DEV):
            peer = (x ^ (rel >> 2), y ^ ((rel >> 1) & 1), c ^ (rel & 1))
            copies.append(pltpu.make_async_remote_copy(
                src_ref=in_ref, dst_ref=out_ref.at[me], send_sem=send_sems.at[rel], recv_sem=recv_sems.at[rel],
                device_id=peer, device_id_type=MESH))
        for cp in copies:
            cp.start()
        for cp in copies:
            cp.wait()
        mine.wait()

    return pl.pallas_call(
        body, name="gather_small",
        in_specs=[HBM_SPEC], out_specs=HBM_SPEC,
        out_shape=jax.ShapeDtypeStruct((N_DEV,) + vec.shape, vec.dtype),
        scratch_shapes=[pltpu.SemaphoreType.DMA((N_DEV,)), pltpu.SemaphoreType.DMA((N_DEV,)),
                        pltpu.SemaphoreType.DMA],
    )(vec)


BIG = [("a_w_in", "col"), ("a_w_out", "row"), ("kv_w", "row"), ("b_w_q", "row"), ("b_w_out", "row"),
       ("ffn_w_gate_up", "col"), ("ffn_w_down", "row"), ("ple_w_up", "col"), ("ple_w_gate", "row")]
GATHER_GROUPS = [[("a_w_in", 0), ("small", 0)], [("a_w_out", 0), ("ffn_w_gate_up", 0)],
                 [("ffn_w_down", 0), ("ple_w_gate", 0), ("ple_w_up", 0)], [("kv_w", 0), ("b_w_q", 0), ("b_w_out", 0)],
                 [("ffn_w_gate_up", 1)], [("ffn_w_down", 1), ("ple_w_gate", 1), ("ple_w_up", 1)]]
SCATTER_GROUPS = [[("ple_w_gate", 1), ("ple_w_up", 1), ("ffn_w_down", 1)], [("ffn_w_gate_up", 1)],
                  [("b_w_out", 0), ("b_w_q", 0), ("kv_w", 0)], [("ple_w_gate", 0), ("ple_w_up", 0), ("ffn_w_down", 0)],
                  [("ffn_w_gate_up", 0), ("a_w_out", 0)], [("a_w_in", 0)]]
SMALL_SHARDED = ["ln_gain", "ln_bias", "a_lower_bound"]
SMALL_REPLICATED = ["a_norm_gain", "kv_b", "b_b_q", "b_sinks", "b_b_out", "ple_b_gate"]
WEIGHT_ORDER = ["a_w_in", "a_lower_bound", "a_norm_gain", "a_w_out", "kv_w", "kv_b", "b_w_q", "b_b_q", "b_sinks",
                "b_w_out", "b_b_out", "ffn_w_gate_up", "ffn_w_down", "ple_w_up", "ple_w_gate", "ple_b_gate",
                "ln_gain", "ln_bias"]


def _as3(a):
    return a.reshape((-1,) + a.shape[-2:]) if a.ndim >= 3 else a.reshape((1,) + a.shape)


def _pad_lanes(v):
    n = v.shape[-1]
    return jnp.pad(v, ((0, 0), (0, (-n) % LANES)))


def _adam_small_fn(w, mom, vel, g):
    return _adam_fn(w, mom, vel, g, jnp.zeros_like(g))[1:]


def _sum_rows_fn(slots):
    acc = slots[0]
    for s in range(1, slots.shape[0]):
        acc = acc + slots[s]
    return (acc,)


def kernel(x, p, a_w_in, a_lower_bound, a_norm_gain, a_w_out, kv_w, kv_b, b_w_q, b_b_q, b_sinks, b_w_out, b_b_out, ffn_w_gate_up, ffn_w_down, ple_w_up, ple_w_gate, ple_b_gate, ln_gain, ln_bias, loss_target, m_a_w_in, m_a_lower_bound, m_a_norm_gain, m_a_w_out, m_kv_w, m_kv_b, m_b_w_q, m_b_b_q, m_b_sinks, m_b_w_out, m_b_b_out, m_ffn_w_gate_up, m_ffn_w_down, m_ple_w_up, m_ple_w_gate, m_ple_b_gate, m_ln_gain, m_ln_bias, v_a_w_in, v_a_lower_bound, v_a_norm_gain, v_a_w_out, v_kv_w, v_kv_b, v_b_w_q, v_b_b_q, v_b_sinks, v_b_w_out, v_b_b_out, v_ffn_w_gate_up, v_ffn_w_down, v_ple_w_up, v_ple_w_gate, v_ple_b_gate, v_ln_gain, v_ln_bias):
    args = dict(locals())
    wts = {n: args[n] for n in WEIGHT_ORDER}
    mom = {n: args["m_" + n] for n in WEIGHT_ORDER}
    vel = {n: args["v_" + n] for n in WEIGHT_ORDER}
    chip = 2 * lax.axis_index("x") + lax.axis_index("y")
    d = x.shape[-1]
    dq = d // N_CHIPS

    kind_of = dict(BIG)
    kind_of["small"] = "col"
    chip_arr = chip.reshape(1).astype(jnp.int32)
    placed = {}
    for n, _ in BIG:
        s3 = _as3(wts[n])
        for layer in range(s3.shape[0]):
            placed[(n, layer)] = _place(s3[layer:layer + 1], kind_of[n], chip_arr, mode="gather",
                                        name=f"place_{n}{layer}", out_dtype=BF16)
    small_pack = jnp.concatenate([wts[n].reshape(-1, dq) for n in SMALL_SHARDED], axis=0)[None]
    placed[("small", 0)] = _place(small_pack, "col", chip_arr, mode="gather", name="place_small", out_dtype=F32)

    gathers, where = [], {}
    for gi, group in enumerate(GATHER_GROUPS):
        gathers.append(_Exchange("gather", [], [placed[k] for k in group], [kind_of[k[0]] for k in group],
                                 [0] * len(group), f"gather{gi}"))
        for k in group:
            where[k] = gi
    all_started = jnp.concatenate([g.token for g in gathers[1:]], axis=0)
    ready = {}

    def wget(name, layer, after):
        key = (name, layer)
        if key not in ready:
            gi = where[key]
            outs = gathers[gi].wait(all_started if gi == 0 else after)
            for k, arr in zip(GATHER_GROUPS[gi], outs):
                ready[k] = arr
        return ready[key]

    small_full = wget("small", 0, None)[0]
    ln_gain_f = small_full[0:6].reshape(DEPTH, 3, d)
    ln_bias_f = small_full[6:12].reshape(DEPTH, 3, d)
    alb_f = small_full[12:14]

    group_of = {k: gi for gi, group in enumerate(SCATTER_GROUPS) for k in group}
    grads_done, zones, scatters = {}, {}, []

    def grad_sink(name, layer, grad):
        grads_done[(name, layer)] = grad
        zones[name] = _place(grad, kind_of[name], chip_arr, mode="scatter", name=f"place_grad_{name}{layer}",
                             out_dtype=BF16, zone=zones.get(name), zone_shape=(N_CHIPS,) + _as3(wts[name]).shape,
                             layer=layer)
        gi = group_of[(name, layer)]
        group = SCATTER_GROUPS[gi]
        if not all(k in grads_done for k in group):
            return None
        ex = _Exchange("scatter", [grads_done[k] for k in group], [zones[k[0]] for k in group],
                       [kind_of[k[0]] for k in group], [k[1] for k in group], f"scatter{gi}")
        for k, zone in zip(group, ex.lands):
            zones[k[0]] = zone
        scatters.append((ex, group))
        return ex.token

    loss, grad_x, gs = _local_step(
        x[0], p[:, 0], loss_target[0], wget, grad_sink, ln_gain_f, ln_bias_f, alb_f, a_norm_gain, kv_b, b_b_q,
        b_sinks, b_b_out, ple_b_gate)
    loss = lax.psum(loss[0, 0], ("x", "y", "c"))

    res = {}

    def arrive(batch, after):
        for ex, group in batch:
            outs = ex.wait(after, lands=[zones[k[0]] for k in group])
            for k, zone in zip(group, outs):
                zones[k[0]] = zone

    def update(names, tag):
        partial = []
        for n in names:
            s2 = zones[n].reshape(N_CHIPS, -1, zones[n].shape[-1])
            partial.append(_rowwise(_sum_slots_fn, [s2], [], [(s2.shape[1:], F32)], name=f"sum_{n}")[0])
        sibling = _sibling_swap(partial, tag)
        for n, own, sib in zip(names, partial, sibling):
            shp = wts[n].shape
            flat = lambda a: a.reshape(-1, shp[-1])
            out = _rowwise(_adam_fn, [flat(wts[n]), flat(mom[n]), flat(vel[n]), own, sib], [],
                           [(own.shape, F32)] * 4, name=f"adam_{n}")
            res[n] = [o.reshape(shp) for o in out]
        return res[names[-1]][1]

    last_names = [k[0] for k in SCATTER_GROUPS[-1]]
    arrive(scatters[:-1], grad_x)
    updated = update([n for n, _ in BIG if n not in last_names], "sibling_swap")
    arrive(scatters[-1:], updated)
    update(last_names, "sibling_swap_last")

    ln_g = jnp.concatenate([gs[f"ln_gain_{i}_{j}"] for i in range(DEPTH) for j in range(3)], axis=0)
    ln_b = jnp.concatenate([gs[f"ln_bias_{i}_{j}"] for i in range(DEPTH) for j in range(3)], axis=0)
    ple_bg = jnp.concatenate([gs[f"ple_b_{i}"] for i in range(DEPTH)], axis=0)
    small_list = [ln_g.reshape(1, -1), ln_b.reshape(1, -1), gs["alb"].reshape(1, -1), gs["norm_gain"],
                  gs["kv_b"], gs["b_q"], _pad_lanes(gs["sinks"]), gs["b_out"], ple_bg.reshape(1, -1)]
    small_vec = jnp.concatenate(small_list, axis=1)
    everyone = _gather_devices(small_vec)
    total, = _rowwise(_sum_rows_fn, [everyone], [], [(small_vec.shape, F32)], name="sum_small")
    offs, pos = [], 0
    for v in small_list:
        offs.append((pos, v.shape[1]))
        pos += v.shape[1]

    def seg(k):
        return total[0, offs[k][0]:offs[k][0] + offs[k][1]]

    def my_cols(full, rows):
        return lax.dynamic_slice_in_dim(full.reshape(rows, N_CHIPS, dq), chip, 1, axis=1).reshape(rows, dq)

    n_sink = b_sinks.shape[-1]
    small_grads = {
        "ln_gain": my_cols(seg(0), 6).reshape(ln_gain.shape), "ln_bias": my_cols(seg(1), 6).reshape(ln_bias.shape),
        "a_lower_bound": my_cols(seg(2), 2), "a_norm_gain": seg(3).reshape(a_norm_gain.shape),
        "kv_b": seg(4).reshape(kv_b.shape), "b_b_q": seg(5).reshape(b_b_q.shape),
        "b_sinks": seg(6)[:n_sink].reshape(b_sinks.shape), "b_b_out": seg(7).reshape(b_b_out.shape),
        "ple_b_gate": seg(8).reshape(ple_b_gate.shape)}
    names = SMALL_SHARDED + SMALL_REPLICATED
    pack = lambda dct: _pad_lanes(jnp.concatenate([dct[n].reshape(1, -1) for n in names], axis=1))
    g_pack = pack(small_grads)
    upd = _rowwise(_adam_small_fn, [pack(wts), pack(mom), pack(vel), g_pack], [], [(g_pack.shape, F32)] * 3,
                   name="adam_small")
    pos = 0
    for n in names:
        size = wts[n].size
        res[n] = [small_grads[n]] + [u[0, pos:pos + size].reshape(wts[n].shape) for u in upd]
        pos += size

    outs = [loss, grad_x[None]]
    for k in range(4):
        outs += [res[n][k] for n in WEIGHT_ORDER]
    return tuple(outs)
```

```python
import functools

import jax
import jax.numpy as jnp
from jax import lax
from jax.experimental import pallas as pl
from jax.experimental.pallas import tpu as pltpu

F32 = jnp.float32
BF16 = jnp.bfloat16
MESH = pl.DeviceIdType.MESH

LANES = 128
HG_DK = 128
HG_CHUNK = 64
HG_SUB = 16
HG_ROWS = 512
ATT_HD = 64
ATT_G = 4
WINDOW = 128
DEPTH = 2
ALPHA = (2.0 * DEPTH) ** 0.25
LN_EPS = 1e-5
RMS_EPS = 1e-6
ADAM_LR, ADAM_B1, ADAM_B2, ADAM_EPS, ADAM_WD, ADAM_STEP = 0.001, 0.9, 0.999, 1e-08, 0.01, 10
N_CHIPS = 4
N_DEV = 8
VMEM_LIMIT = 56 * 1024 * 1024
NEG = -1e30


def _pick(n, cap):
    best = None
    for d in range(LANES, min(n, cap) + 1, LANES):
        if n % d == 0:
            best = d
    return n if best is None else best


def _pick_rows(m, cap):
    best = None
    for d in range(16, min(m, cap) + 1, 16):
        if m % d == 0:
            best = d
    return m if best is None else best


def _params(sem):
    return pltpu.CompilerParams(dimension_semantics=sem, vmem_limit_bytes=VMEM_LIMIT)


def _mm(a, b, *, name, la=None, lb=None, ta=False, tb=False, bias=None, add=None, out_dtype=F32,
        out_layers=None, out_layer=None, after=None, caps=(1024, 1536, 2048)):
    ar, ac = a.shape[-2:]
    br, bc = b.shape[-2:]
    m, k = (ac, ar) if ta else (ar, ac)
    k2, n = (bc, br) if tb else (br, bc)
    assert k == k2, (a.shape, b.shape, ta, tb)
    tm, tn, tk = _pick(m, caps[0]), _pick(n, caps[1]), _pick(k, caps[2])
    nk = k // tk
    grid = (m // tm, n // tn, nk)

    def spec(block, idx, layer):
        if layer is None:
            return pl.BlockSpec(block, idx)
        return pl.BlockSpec((None,) + block, lambda i, j, kk: (layer,) + idx(i, j, kk))

    a_spec = spec((tk, tm), lambda i, j, kk: (kk, i), la) if ta else spec((tm, tk), lambda i, j, kk: (i, kk), la)
    b_spec = spec((tn, tk), lambda i, j, kk: (j, kk), lb) if tb else spec((tk, tn), lambda i, j, kk: (kk, j), lb)
    in_specs, operands = [a_spec, b_spec], [a, b]
    if bias is not None:
        in_specs.append(pl.BlockSpec((1, tn), lambda i, j, kk: (0, j)))
        operands.append(bias)
    if add is not None:
        in_specs.append(pl.BlockSpec((tm, tn), lambda i, j, kk: (i, j)))
        operands.append(add)
    if after is not None:
        in_specs.append(pl.BlockSpec(memory_space=pl.ANY))
        operands.append(after)
    if out_layers is None:
        out_shape = jax.ShapeDtypeStruct((m, n), out_dtype)
    else:
        out_shape = jax.ShapeDtypeStruct((out_layers, m, n), out_dtype)
    out_spec = spec((tm, tn), lambda i, j, kk: (i, j), out_layer)
    dims = (((0 if ta else 1,), (1 if tb else 0,)), ((), ()))
    has_bias, has_add, has_alias = bias is not None, add is not None, after is not None

    def body(*refs):
        a_ref, b_ref = refs[0], refs[1]
        pos = 2
        bias_ref = add_ref = None
        if has_bias:
            bias_ref = refs[pos]
            pos += 1
        if has_add:
            add_ref = refs[pos]
            pos += 1
        if has_alias:
            pos += 1
        o_ref = refs[pos]
        acc_ref = refs[pos + 1] if nk > 1 else None
        part = lax.dot_general(a_ref[...].astype(BF16), b_ref[...].astype(BF16), dims, preferred_element_type=F32)

        def finish(total):
            if has_bias:
                total = total + bias_ref[...]
            if has_add:
                total = total + add_ref[...]
            o_ref[...] = total.astype(o_ref.dtype)

        if nk == 1:
            finish(part)
        else:
            kk = pl.program_id(2)

            @pl.when(kk == 0)
            def _():
                acc_ref[...] = part

            @pl.when(kk > 0)
            def _():
                acc_ref[...] += part

            @pl.when(kk == nk - 1)
            def _():
                finish(acc_ref[...])

    return pl.pallas_call(
        body, name=name, grid=grid, in_specs=in_specs, out_specs=out_spec, out_shape=out_shape,
        scratch_shapes=[pltpu.VMEM((tm, tn), F32)] if nk > 1 else [],
        compiler_params=_params(("parallel", "parallel", "arbitrary")),
    )(*operands)


def _rowwise(fn, rows, whole, outs, sums=(), *, name, tm=256):
    m = rows[0].shape[-2]
    tm = _pick_rows(m, tm)
    n_rows, n_whole, n_outs, n_sums = len(rows), len(whole), len(outs), len(sums)

    def rspec(shape):
        lead = len(shape) - 2
        return pl.BlockSpec(tuple(shape[:-2]) + (tm, shape[-1]), lambda i: (0,) * lead + (i, 0))

    def wspec(shape):
        return pl.BlockSpec(tuple(shape), lambda i: (0,) * len(shape))

    def body(*refs):
        vals = [r[...] for r in refs[:n_rows + n_whole]]
        out_refs = refs[n_rows + n_whole:n_rows + n_whole + n_outs]
        sum_refs = refs[n_rows + n_whole + n_outs:]
        res = fn(*vals)
        for ref, val in zip(out_refs, res[:n_outs]):
            ref[...] = val.astype(ref.dtype)
        if n_sums:
            @pl.when(pl.program_id(0) == 0)
            def _():
                for ref in sum_refs:
                    ref[...] = jnp.zeros(ref.shape, ref.dtype)

            for ref, val in zip(sum_refs, res[n_outs:]):
                ref[...] += val

    result = pl.pallas_call(
        body, name=name, grid=(m // tm,),
        in_specs=[rspec(r.shape) for r in rows] + [wspec(w.shape) for w in whole],
        out_specs=[rspec(s) for s, _ in outs] + [wspec(s) for s, _ in sums],
        out_shape=[jax.ShapeDtypeStruct(s, d) for s, d in list(outs) + list(sums)],
        compiler_params=_params(("arbitrary",)),
    )(*rows, *whole)
    return result


def _sigmoid(v):
    return jax.nn.sigmoid(v)


def _col_sum(v):
    return jnp.sum(v, axis=0, keepdims=True)


def _ln_stats(z):
    mu = jnp.mean(z, axis=-1, keepdims=True)
    zc = z - mu
    var = jnp.mean(zc * zc, axis=-1, keepdims=True)
    rstd = lax.rsqrt(var + LN_EPS)
    return zc * rstd, rstd


def _ln_fwd_fn(xin, h, gain, bias):
    xhat, _ = _ln_stats(ALPHA * xin + h)
    y = xhat * gain + bias
    return y, y


def _ple_ln_fwd_fn(xin, pg, pu, gain, bias):
    xhat, _ = _ln_stats(ALPHA * xin + _sigmoid(pg) * pu)
    y = xhat * gain + bias
    return y, y


def _ln_dz(dy, z, gain):
    xhat, rstd = _ln_stats(z)
    dxhat = dy * gain
    dz = rstd * (dxhat - jnp.mean(dxhat, axis=-1, keepdims=True)
                 - xhat * jnp.mean(dxhat * xhat, axis=-1, keepdims=True))
    return dz, _col_sum(dy * xhat), _col_sum(dy)


def _ln_bwd_fn(dy, xin, h, gain):
    dz, dgain, dbias = _ln_dz(dy, ALPHA * xin + h, gain)
    return ALPHA * dz, dz, dgain, dbias, _col_sum(dz)


def _ple_ln_bwd_fn(dy, xin, pg, pu, gain):
    sg = _sigmoid(pg)
    dz, dgain, dbias = _ln_dz(dy, ALPHA * xin + sg * pu, gain)
    dpg = dz * pu * sg * (1.0 - sg)
    return ALPHA * dz, dpg, dz * sg, dgain, dbias, _col_sum(dpg)


def _swiglu_fwd_fn(gu):
    hid = gu.shape[-1] // 2
    gate, up = gu[:, :hid], gu[:, hid:]
    return (gate * _sigmoid(gate) * up,)


def _swiglu_bwd_fn(gu, dact):
    hid = gu.shape[-1] // 2
    gate, up = gu[:, :hid], gu[:, hid:]
    sg = _sigmoid(gate)
    dgate = dact * up * sg * (1.0 + gate * (1.0 - sg))
    dup = dact * gate * sg
    return (jnp.concatenate([dgate, dup], axis=-1),)


def _loss_fn(y, target):
    err = y - target
    inv = 1.0 / y.shape[-1]
    part = 0.5 * inv * jnp.sum(jnp.sum(err * err, axis=-1, keepdims=True), axis=0, keepdims=True)
    return err * inv, jnp.broadcast_to(part, (1, LANES))


def _adam_fn(w, mom, vel, p_own, p_sib):
    g = p_own + p_sib
    m_new = ADAM_B1 * mom + (1.0 - ADAM_B1) * g
    v_new = ADAM_B2 * vel + (1.0 - ADAM_B2) * (g * g)
    m_hat = m_new / (1.0 - ADAM_B1 ** ADAM_STEP)
    v_hat = v_new / (1.0 - ADAM_B2 ** ADAM_STEP)
    delta = -ADAM_LR * (m_hat / (jnp.sqrt(v_hat) + ADAM_EPS) + ADAM_WD * w)
    return g, delta, m_new, v_new


def _sum_slots_fn(slots):
    acc = slots[0].astype(F32)
    for s in range(1, slots.shape[0]):
        acc = acc + slots[s].astype(F32)
    return (acc,)


def _split2(x):
    hi = x.astype(BF16)
    return hi, (x - hi.astype(F32)).astype(BF16)


def _dot3(a, b, dims):
    a_hi, a_lo = _split2(a)
    b_hi, b_lo = _split2(b)
    dn = (dims, ((), ()))
    return (lax.dot_general(a_hi, b_hi, dn, preferred_element_type=F32)
            + (lax.dot_general(a_hi, b_lo, dn, preferred_element_type=F32)
               + lax.dot_general(a_lo, b_hi, dn, preferred_element_type=F32)))


def _tdot(mask01, b):
    m = mask01.astype(BF16)
    b_hi = b.astype(BF16)
    rest = b - b_hi.astype(F32)
    b_mid = rest.astype(BF16)
    b_lo = (rest - b_mid.astype(F32)).astype(BF16)
    dn = (((1,), (0,)), ((), ()))
    return (lax.dot_general(m, b_hi, dn, preferred_element_type=F32)
            + (lax.dot_general(m, b_mid, dn, preferred_element_type=F32)
               + lax.dot_general(m, b_lo, dn, preferred_element_type=F32)))


def _hdot(a, b):
    return _dot3(a, b, ((1,), (0,)))


def _hdot_nt(a, b):
    return _dot3(a, b, ((1,), (1,)))


def _hdot_tn(a, b):
    return _dot3(a, b, ((0,), (0,)))


def _dot(a, b):
    return lax.dot_general(a.astype(BF16), b.astype(BF16), (((1,), (0,)), ((), ())), preferred_element_type=F32)


def _dot_nt(a, b):
    return lax.dot_general(a.astype(BF16), b.astype(BF16), (((1,), (1,)), ((), ())), preferred_element_type=F32)


def _dot_tn(a, b):
    return lax.dot_general(a.astype(BF16), b.astype(BF16), (((0,), (0,)), ((), ())), preferred_element_type=F32)


def _hg_masks():
    c = HG_CHUNK
    row = lax.broadcasted_iota(jnp.int32, (c, c), 0)
    col = lax.broadcasted_iota(jnp.int32, (c, c), 1)
    base = row & (-HG_SUB)
    return row, col, base, col <= row, col < base


def _hg_gates(qr, fr, alb):
    lbound = _sigmoid(alb[0:1, :] - alb[1:2, :])
    sig = _sigmoid(fr)
    forget = lbound + (1.0 - lbound) * sig
    kk = (1.0 - lbound) * _sigmoid(-fr)
    qt = qr * _sigmoid(qr) * (HG_DK ** -0.5)
    return qt, kk, jnp.log(forget), lbound, sig, forget


def _hg_scores(qt, kk, g, scores=True):
    c, nsub = HG_CHUNK, HG_CHUNK // HG_SUB
    row, col, base, causal, below = _hg_masks()
    b = _tdot(causal, g)
    rr = _tdot(below, g)
    bq = b - rr
    qh = qt * jnp.exp(bq)
    edecs = [None]
    parts = [jnp.zeros((HG_SUB, c), F32)]
    for i in range(1, nsub):
        edec = jnp.exp(jnp.minimum(rr[i * HG_SUB:i * HG_SUB + 1, :] - b, 0.0))
        edecs.append(edec)
        if scores:
            parts.append(_dot_nt(qh[i * HG_SUB:(i + 1) * HG_SUB, :], kk * edec))
    b3 = b.reshape(nsub, HG_SUB, HG_DK)
    q3 = qt.reshape(nsub, HG_SUB, HG_DK)
    k3 = kk.reshape(nsub, HG_SUB, HG_DK)
    if not scores:
        return None, b, bq, qh, edecs, (b3, q3, k3)
    a = jnp.where(below, jnp.concatenate(parts, axis=0), 0.0)
    for j in range(HG_SUB):
        e = jnp.exp(jnp.minimum(b3 - b3[:, j:j + 1, :], 0.0))
        colv = jnp.sum(q3 * e * k3[:, j:j + 1, :], axis=-1, keepdims=True).reshape(c, 1)
        a = jnp.where(col == base + j, colv, a)
    a = jnp.where(causal, a, 0.0)
    return a, b, bq, qh, edecs, (b3, q3, k3)


def _hg_norm(o, gr, gain):
    r = lax.rsqrt(jnp.mean(o * o, axis=-1, keepdims=True) + RMS_EPS)
    sg = _sigmoid(gr)
    return o * r * gain, r, sg


def _hgrn2_fwd(proj, alb, gain, *, rb):
    m, d4 = proj.shape
    d = d4 // 4
    heads = d // HG_DK
    rb = min(rb, m)
    cpb = rb // HG_CHUNK
    nrb = m // rb

    def body(q_ref, f_ref, v_ref, g_ref, alb_ref, gain_ref, o_ref, og_ref, st_ref, a_ref, state):
        @pl.when(pl.program_id(1) == 0)
        def _():
            state[...] = jnp.zeros(state.shape, F32)

        def chunk(ci, carry):
            sl = pl.ds(pl.multiple_of(ci * HG_CHUNK, HG_CHUNK), HG_CHUNK)
            qt, kk, g, _, _, _ = _hg_gates(q_ref[sl, :], f_ref[sl, :], alb_ref[...])
            v = v_ref[sl, :]
            st = state[...]
            st_ref[ci] = st
            a, b, _, _, _, _ = _hg_scores(qt, kk, g)
            a_ref[ci] = a.astype(a_ref.dtype)
            o = _dot(a, v) + _dot_nt(qt * jnp.exp(b), st)
            b_last = b[HG_CHUNK - 1:HG_CHUNK, :]
            state[...] = st * jnp.exp(b_last) + _hdot_tn(v, kk * jnp.exp(b_last - b))
            o_ref[sl, :] = o
            n, _, sg = _hg_norm(o, g_ref[sl, :], gain_ref[...])
            og_ref[sl, :] = (n * g_ref[sl, :] * sg).astype(og_ref.dtype)
            return carry

        lax.fori_loop(0, cpb, chunk, 0, unroll=2)

    def col(cidx):
        return pl.BlockSpec((rb, HG_DK), lambda h, r: (r, cidx * heads + h))

    return pl.pallas_call(
        body, name="hgrn2_fwd", grid=(heads, nrb),
        in_specs=[col(0), col(1), col(2), col(3),
                  pl.BlockSpec((2, HG_DK), lambda h, r: (0, h)),
                  pl.BlockSpec((1, HG_DK), lambda h, r: (0, 0))],
        out_specs=[pl.BlockSpec((rb, HG_DK), lambda h, r: (r, h)),
                   pl.BlockSpec((rb, HG_DK), lambda h, r: (r, h)),
                   pl.BlockSpec((None, cpb, HG_DK, HG_DK), lambda h, r: (h, r, 0, 0)),
                   pl.BlockSpec((None, cpb, HG_CHUNK, HG_CHUNK), lambda h, r: (h, r, 0, 0))],
        out_shape=[jax.ShapeDtypeStruct((m, d), F32), jax.ShapeDtypeStruct((m, d), BF16),
                   jax.ShapeDtypeStruct((heads, m // HG_CHUNK, HG_DK, HG_DK), F32),
                   jax.ShapeDtypeStruct((heads, m // HG_CHUNK, HG_CHUNK, HG_CHUNK), BF16)],
        scratch_shapes=[pltpu.VMEM((HG_DK, HG_DK), F32)],
        compiler_params=_params(("parallel", "arbitrary")),
    )(proj, proj, proj, proj, alb, gain)


def _hgrn2_bwd(proj, o_pre, states, scores, dog, alb, gain, *, rb):
    m, d4 = proj.shape
    d = d4 // 4
    heads = d // HG_DK
    rb = min(rb, m)
    cpb = rb // HG_CHUNK
    nrb = m // rb
    c, nsub = HG_CHUNK, HG_CHUNK // HG_SUB

    def body(q_ref, f_ref, v_ref, g_ref, o_ref, st_ref, a_ref, dog_ref, alb_ref, gain_ref,
             dq_ref, df_ref, dv_ref, dg_ref, dalb_ref, dgain_ref, dstate, carry_ref):
        first = (pl.program_id(0) == 0) & (pl.program_id(1) == 0)

        @pl.when(first)
        def _():
            dgain_ref[...] = jnp.zeros(dgain_ref.shape, F32)

        @pl.when(pl.program_id(1) == 0)
        def _():
            dstate[...] = jnp.zeros(dstate.shape, F32)
            carry_ref[...] = jnp.zeros(carry_ref.shape, F32)
            dalb_ref[...] = jnp.zeros(dalb_ref.shape, F32)

        row, col, base, causal, below = _hg_masks()
        sub_iota = lax.broadcasted_iota(jnp.int32, (nsub, HG_SUB, HG_DK), 1)
        row_k = lax.broadcasted_iota(jnp.int32, (c, HG_DK), 0)
        upper = col >= row

        def chunk(step, carry):
            ci = cpb - 1 - step
            sl = pl.ds(pl.multiple_of(ci * HG_CHUNK, HG_CHUNK), HG_CHUNK)
            qr, fr, v, gr = q_ref[sl, :], f_ref[sl, :], v_ref[sl, :], g_ref[sl, :]
            qt, kk, g, lbound, sig, forget = _hg_gates(qr, fr, alb_ref[...])
            o = o_ref[sl, :]
            dogv = dog_ref[sl, :]
            gain_v = gain_ref[...]
            n, r, sg = _hg_norm(o, gr, gain_v)
            dgr = dogv * n * sg * (1.0 + gr * (1.0 - sg))
            dn = dogv * gr * sg
            dgain_ref[...] += _col_sum(dn * o * r)
            u = dn * gain_v
            d_o = r * u - o * (r * r * r) * jnp.mean(u * o, axis=-1, keepdims=True)
            st0 = st_ref[ci]
            dst = dstate[...]
            _, b, bq, qh, edecs, (b3, q3, k3) = _hg_scores(qt, kk, g, scores=False)
            a = a_ref[ci]
            eb = jnp.exp(b)
            b_last = b[c - 1:c, :]
            kdl_dec = jnp.exp(b_last - b)
            kdl = kk * kdl_dec
            d_a = jnp.where(causal, _dot_nt(d_o, v), 0.0)
            d_at = _dot_nt(v, d_o)
            dv = _dot_tn(a, d_o) + _dot_nt(kdl, dst)
            dq = eb * _hdot(d_o, st0)
            dk = _hdot(v, dst) * kdl_dec
            d_a_below = jnp.where(below, d_a, 0.0)
            dq_parts = [jnp.zeros((HG_SUB, HG_DK), F32)]
            for i in range(1, nsub):
                lo, hi = i * HG_SUB, (i + 1) * HG_SUB
                dq_parts.append(_hdot(d_a_below[lo:hi, :], kk * edecs[i]))
                gi = _hdot(d_at[:, lo:hi], qh[lo:hi, :])
                dk = dk + jnp.where(row_k < lo, edecs[i] * gi, 0.0)
            dq = dq + jnp.concatenate(dq_parts, axis=0) * jnp.exp(bq)
            dq3 = jnp.zeros((nsub, HG_SUB, HG_DK), F32)
            dk3 = jnp.zeros((nsub, HG_SUB, HG_DK), F32)
            for j in range(HG_SUB):
                e = jnp.exp(jnp.minimum(b3 - b3[:, j:j + 1, :], 0.0))
                dcol = jnp.sum(jnp.where(col == base + j, d_a, 0.0), axis=-1, keepdims=True)
                t1 = dcol.reshape(nsub, HG_SUB, 1) * e
                dq3 = dq3 + t1 * k3[:, j:j + 1, :]
                dk3 = jnp.where(sub_iota == j, jnp.sum(t1 * q3, axis=1, keepdims=True), dk3)
            dq = dq + dq3.reshape(c, HG_DK)
            dk = dk + dk3.reshape(c, HG_DK)
            dstate[...] = dst * jnp.exp(b_last) + _hdot_tn(d_o, qt * eb)
            dglog = _tdot(upper, qt * dq - kk * dk) + carry_ref[...]
            carry_ref[...] = dglog[0:1, :]
            dforget = dglog / forget
            one_m_lb = 1.0 - lbound
            dsig = (dforget - dk) * one_m_lb
            sneg = _sigmoid(-fr)
            dlb = _col_sum(dforget * (1.0 - sig) - dk * sneg)
            dalb0 = dlb * lbound * one_m_lb
            dalb_ref[...] += jnp.concatenate([dalb0, -dalb0], axis=0)
            sq = _sigmoid(qr)
            dq_ref[sl, :] = (dq * (HG_DK ** -0.5) * sq * (1.0 + qr * (1.0 - sq))).astype(dq_ref.dtype)
            df_ref[sl, :] = (dsig * sig * (1.0 - sig)).astype(df_ref.dtype)
            dv_ref[sl, :] = dv.astype(dv_ref.dtype)
            dg_ref[sl, :] = dgr.astype(dg_ref.dtype)
            return carry

        lax.fori_loop(0, cpb, chunk, 0, unroll=2)

    def rev(r):
        return nrb - 1 - r

    def col(cidx):
        return pl.BlockSpec((rb, HG_DK), lambda h, r: (rev(r), cidx * heads + h))

    def head_rows():
        return pl.BlockSpec((rb, HG_DK), lambda h, r: (rev(r), h))

    return pl.pallas_call(
        body, name="hgrn2_bwd", grid=(heads, nrb),
        in_specs=[col(0), col(1), col(2), col(3), head_rows(),
                  pl.BlockSpec((None, cpb, HG_DK, HG_DK), lambda h, r: (h, rev(r), 0, 0)),
                  pl.BlockSpec((None, cpb, HG_CHUNK, HG_CHUNK), lambda h, r: (h, rev(r), 0, 0)),
                  head_rows(),
                  pl.BlockSpec((2, HG_DK), lambda h, r: (0, h)),
                  pl.BlockSpec((1, HG_DK), lambda h, r: (0, 0))],
        out_specs=[head_rows(), head_rows(), head_rows(), head_rows(),
                   pl.BlockSpec((2, HG_DK), lambda h, r: (0, h)),
                   pl.BlockSpec((1, HG_DK), lambda h, r: (0, 0))],
        out_shape=[jax.ShapeDtypeStruct((m, d), BF16)] * 4
                  + [jax.ShapeDtypeStruct((2, d), F32), jax.ShapeDtypeStruct((1, HG_DK), F32)],
        scratch_shapes=[pltpu.VMEM((HG_DK, HG_DK), F32), pltpu.VMEM((1, HG_DK), F32)],
        compiler_params=_params(("arbitrary", "arbitrary")),
    )(proj, proj, proj, proj, o_pre, states, scores, dog, alb, gain)


def _swa_probs(qh, kp, kc, sink, slope, has_prev):
    rows = qh.shape[0]
    qi = lax.broadcasted_iota(jnp.int32, (rows, WINDOW), 0) & (WINDOW - 1)
    si = lax.broadcasted_iota(jnp.int32, (rows, WINDOW), 1)
    scale = ATT_HD ** -0.5
    dist_c = (qi - si).astype(F32)
    s_p = _dot_nt(qh, kp) * scale - slope * (dist_c + float(WINDOW))
    s_c = _dot_nt(qh, kc) * scale - slope * dist_c
    s_p = jnp.where((si > qi) & has_prev, s_p, NEG)
    s_c = jnp.where(si <= qi, s_c, NEG)
    mx = jnp.maximum(jnp.maximum(jnp.max(s_p, axis=-1, keepdims=True), jnp.max(s_c, axis=-1, keepdims=True)), sink)
    e_p, e_c, e_s = jnp.exp(s_p - mx), jnp.exp(s_c - mx), jnp.exp(sink - mx)
    inv = 1.0 / (jnp.sum(e_p, axis=-1, keepdims=True) + jnp.sum(e_c, axis=-1, keepdims=True) + e_s)
    return e_p * inv, e_c * inv, e_s * inv


def _slope(h, n_heads):
    return float(2.0 ** (-8.0 * (h + 1) / n_heads))


def _swa_group(ref_vals, sink_ref, kh, n_heads):
    heads = [kh * ATT_G + g for g in range(ATT_G)]
    stacked = [jnp.concatenate([v[:, h * ATT_HD:(h + 1) * ATT_HD] for h in heads], axis=0) for v in ref_vals]
    grp = lax.shift_right_logical(lax.broadcasted_iota(jnp.int32, (ATT_G * WINDOW, 1), 0), WINDOW.bit_length() - 1)
    slope = jnp.zeros((ATT_G * WINDOW, 1), F32)
    sink = jnp.zeros((ATT_G * WINDOW, 1), F32)
    for g, h in enumerate(heads):
        slope = jnp.where(grp == g, _slope(h, n_heads), slope)
        sink = jnp.where(grp == g, sink_ref[:, h:h + 1], sink)
    return stacked, slope, sink


def _swa_fwd(q, kv, sinks):
    m, d = q.shape
    n_heads = d // ATT_HD
    kvh = n_heads // ATT_G
    kd = kvh * ATT_HD
    nb = m // WINDOW

    def body(q_ref, kvp_ref, kvc_ref, sink_ref, o_ref):
        has_prev = pl.program_id(0) > 0
        qv, kvp, kvc = q_ref[...], kvp_ref[...], kvc_ref[...]
        outs = []
        for kh in range(kvh):
            ks = slice(kh * ATT_HD, (kh + 1) * ATT_HD)
            vs = slice(kd + kh * ATT_HD, kd + (kh + 1) * ATT_HD)
            (q4,), slope, sink = _swa_group([qv], sink_ref, kh, n_heads)
            p_p, p_c, _ = _swa_probs(q4, kvp[:, ks], kvc[:, ks], sink, slope, has_prev)
            o4 = _dot(p_p, kvp[:, vs]) + _dot(p_c, kvc[:, vs])
            outs += [o4[g * WINDOW:(g + 1) * WINDOW, :] for g in range(ATT_G)]
        o_ref[...] = jnp.concatenate(outs, axis=-1).astype(o_ref.dtype)

    return pl.pallas_call(
        body, name="swa_fwd", grid=(nb,),
        in_specs=[pl.BlockSpec((WINDOW, d), lambda n: (n, 0)),
                  pl.BlockSpec((WINDOW, 2 * kd), lambda n: (jnp.maximum(n - 1, 0), 0)),
                  pl.BlockSpec((WINDOW, 2 * kd), lambda n: (n, 0)),
                  pl.BlockSpec((1, n_heads), lambda n: (0, 0))],
        out_specs=pl.BlockSpec((WINDOW, d), lambda n: (n, 0)),
        out_shape=jax.ShapeDtypeStruct((m, d), BF16),
        compiler_params=_params(("arbitrary",)),
    )(q, kv, kv, sinks)


def _swa_bwd(q, kv, sinks, dao):
    m, d = q.shape
    n_heads = d // ATT_HD
    kvh = n_heads // ATT_G
    kd = kvh * ATT_HD
    nb = m // WINDOW
    scale = ATT_HD ** -0.5

    def body(q_ref, kvp_ref, kvc_ref, sink_ref, do_ref, dq_ref, dkvc_ref, dkvp_ref, dqsum_ref, dsink_ref):
        @pl.when(pl.program_id(0) == 0)
        def _():
            dqsum_ref[...] = jnp.zeros(dqsum_ref.shape, F32)
            dsink_ref[...] = jnp.zeros(dsink_ref.shape, F32)

        has_prev = pl.program_id(0) > 0
        qv, kvp, kvc, dov = q_ref[...], kvp_ref[...], kvc_ref[...], do_ref[...]
        lane_h = lax.broadcasted_iota(jnp.int32, (1, n_heads), 1)
        dsink = jnp.zeros((1, n_heads), F32)
        dq_parts, dk_p, dk_c, dv_p, dv_c = [], [], [], [], []
        for kh in range(kvh):
            ks = slice(kh * ATT_HD, (kh + 1) * ATT_HD)
            vs = slice(kd + kh * ATT_HD, kd + (kh + 1) * ATT_HD)
            kp, kc, vp, vc = kvp[:, ks], kvc[:, ks], kvp[:, vs], kvc[:, vs]
            (q4, do4), slope, sink = _swa_group([qv, dov], sink_ref, kh, n_heads)
            p_p, p_c, p_s = _swa_probs(q4, kp, kc, sink, slope, has_prev)
            dp_p, dp_c = _dot_nt(do4, vp), _dot_nt(do4, vc)
            delta = jnp.sum(p_p * dp_p, axis=-1, keepdims=True) + jnp.sum(p_c * dp_c, axis=-1, keepdims=True)
            ds_p, ds_c = p_p * (dp_p - delta), p_c * (dp_c - delta)
            sink_term = p_s * delta
            dq4 = (_dot(ds_p, kp) + _dot(ds_c, kc)) * scale
            for g in range(ATT_G):
                rows = slice(g * WINDOW, (g + 1) * WINDOW)
                dsink = dsink + jnp.where(lane_h == kh * ATT_G + g, -_col_sum(sink_term[rows, :]), 0.0)
                dq_parts.append(dq4[rows, :])
            dk_p.append(_dot_tn(ds_p, q4) * scale)
            dk_c.append(_dot_tn(ds_c, q4) * scale)
            dv_p.append(_dot_tn(p_p, do4))
            dv_c.append(_dot_tn(p_c, do4))
        dq = jnp.concatenate(dq_parts, axis=-1)
        dq_ref[...] = dq.astype(dq_ref.dtype)
        dqsum_ref[...] += _col_sum(dq)
        dsink_ref[...] += dsink
        dkvc_ref[...] = jnp.concatenate(dk_c + dv_c, axis=-1)
        dkvp_ref[...] = jnp.concatenate(dk_p + dv_p, axis=-1)

    return pl.pallas_call(
        body, name="swa_bwd", grid=(nb,),
        in_specs=[pl.BlockSpec((WINDOW, d), lambda n: (n, 0)),
                  pl.BlockSpec((WINDOW, 2 * kd), lambda n: (jnp.maximum(n - 1, 0), 0)),
                  pl.BlockSpec((WINDOW, 2 * kd), lambda n: (n, 0)),
                  pl.BlockSpec((1, n_heads), lambda n: (0, 0)),
                  pl.BlockSpec((WINDOW, d), lambda n: (n, 0))],
        out_specs=[pl.BlockSpec((WINDOW, d), lambda n: (n, 0)),
                   pl.BlockSpec((WINDOW, 2 * kd), lambda n: (n, 0)),
                   pl.BlockSpec((WINDOW, 2 * kd), lambda n: (n, 0)),
                   pl.BlockSpec((1, d), lambda n: (0, 0)),
                   pl.BlockSpec((1, n_heads), lambda n: (0, 0))],
        out_shape=[jax.ShapeDtypeStruct((m, d), BF16), jax.ShapeDtypeStruct((m, 2 * kd), F32),
                   jax.ShapeDtypeStruct((m, 2 * kd), F32), jax.ShapeDtypeStruct((1, d), F32),
                   jax.ShapeDtypeStruct((1, n_heads), F32)],
        compiler_params=_params(("arbitrary",)),
    )(q, kv, kv, sinks, dao)


def _kv_grad_combine(dkv_cur, dkv_prev):
    m, w = dkv_cur.shape
    nb = m // WINDOW

    def body(cur_ref, nxt_ref, o_ref, sum_ref):
        @pl.when(pl.program_id(0) == 0)
        def _():
            sum_ref[...] = jnp.zeros(sum_ref.shape, F32)

        total = cur_ref[...] + jnp.where(pl.program_id(0) < nb - 1, nxt_ref[...], 0.0)
        o_ref[...] = total.astype(o_ref.dtype)
        sum_ref[...] += _col_sum(total)

    return pl.pallas_call(
        body, name="kv_grad_combine", grid=(nb,),
        in_specs=[pl.BlockSpec((WINDOW, w), lambda n: (n, 0)),
                  pl.BlockSpec((WINDOW, w), lambda n: (jnp.minimum(n + 1, nb - 1), 0))],
        out_specs=[pl.BlockSpec((WINDOW, w), lambda n: (n, 0)), pl.BlockSpec((1, w), lambda n: (0, 0))],
        out_shape=[jax.ShapeDtypeStruct((m, w), BF16), jax.ShapeDtypeStruct((1, w), F32)],
        compiler_params=_params(("arbitrary",)),
    )(dkv_cur, dkv_prev)


def _row(v):
    return v.reshape(1, -1)


def _local_step(x, p, target, wget, grad_sink, ln_gain, ln_bias, alb, norm_gain, kv_b, b_q, sinks, b_out, ple_b):
    gs = {}
    gains = [[_row(ln_gain[i, j]) for j in range(3)] for i in range(DEPTH)]
    biases = [[_row(ln_bias[i, j]) for j in range(3)] for i in range(DEPTH)]
    sd = x.shape
    pending = [None]

    def mm(a, b, lb=0, **kw):
        after, pending[0] = pending[0], None
        return _mm(a, b, lb=lb, after=after, **kw)

    def ln_fwd(xin, h, i, j, nm):
        return _rowwise(_ln_fwd_fn, [xin, h], [gains[i][j], biases[i][j]], [(sd, F32), (sd, BF16)], name=nm)

    def tail_fwd(xa, i):
        gu = _mm(xa[1], wget("ffn_w_gate_up", i, xa[1]), lb=0, name=f"ffn_up{i}")
        act, = _rowwise(_swiglu_fwd_fn, [gu], [], [((sd[0], gu.shape[1] // 2), BF16)], name=f"swiglu{i}", tm=128)
        f = _mm(act, wget("ffn_w_down", i, act), lb=0, name=f"ffn_down{i}")
        xb = ln_fwd(xa[0], f, i, 1, f"ln_ffn{i}")
        pg = _mm(xb[1], wget("ple_w_gate", i, act), lb=0, bias=_row(ple_b[i]), name=f"ple_gate{i}")
        pu = _mm(p[i], wget("ple_w_up", i, act), lb=0, name=f"ple_up{i}")
        xc = _rowwise(_ple_ln_fwd_fn, [xb[0], pg, pu], [gains[i][2], biases[i][2]], [(sd, F32), (sd, BF16)],
                      name=f"ln_ple{i}")
        return dict(xa=xa, gu=gu, act=act, f=f, xb=xb, pg=pg, pu=pu), xc

    def tail_bwd(dxc, sv, i):
        xa, xb = sv["xa"], sv["xb"]
        dxb_part, dpg, dpu, dg2, db2, dbg = _rowwise(
            _ple_ln_bwd_fn, [dxc, xb[0], sv["pg"], sv["pu"]], [gains[i][2]],
            [(sd, F32), (sd, BF16), (sd, BF16)], [((1, sd[1]), F32)] * 3, name=f"ln_ple_bwd{i}")
        gs[f"ple_b_{i}"] = dbg
        gs[f"ln_gain_{i}_2"], gs[f"ln_bias_{i}_2"] = dg2, db2
        grad_of("ple_w_gate", i, xb[1], dpg)
        grad_of("ple_w_up", i, p[i], dpu)
        dxb = mm(dpg, wget("ple_w_gate", i, None), tb=True, add=dxb_part, name=f"ple_gate_dx{i}")
        dxa_part, df, dg1, db1, _ = _rowwise(
            _ln_bwd_fn, [dxb, xa[0], sv["f"]], [gains[i][1]],
            [(sd, F32), (sd, BF16)], [((1, sd[1]), F32)] * 3, name=f"ln_ffn_bwd{i}")
        gs[f"ln_gain_{i}_1"], gs[f"ln_bias_{i}_1"] = dg1, db1
        grad_of("ffn_w_down", i, sv["act"], df)
        dact = mm(df, wget("ffn_w_down", i, None), tb=True, name=f"ffn_down_dx{i}")
        dgu, = _rowwise(_swiglu_bwd_fn, [sv["gu"], dact], [], [(sv["gu"].shape, BF16)], name=f"swiglu_bwd{i}", tm=128)
        grad_of("ffn_w_gate_up", i, xa[1], dgu)
        return mm(dgu, wget("ffn_w_gate_up", i, None), tb=True, add=dxa_part, name=f"ffn_up_dx{i}")

    def grad_of(nm, i, act, dout):
        grad = mm(act, dout, lb=None, ta=True, out_dtype=BF16, out_layers=1, out_layer=0, name=f"grad_{nm}{i}")
        token = grad_sink(nm, i, grad)
        if token is not None:
            pending[0] = token

    def mixer_ln_bwd(dxa, xin, h, i):
        dx_part, dh, dg0, db0, dhsum = _rowwise(
            _ln_bwd_fn, [dxa, xin, h], [gains[i][0]],
            [(sd, F32), (sd, BF16)], [((1, sd[1]), F32)] * 3, name=f"ln_mix_bwd{i}")
        gs[f"ln_gain_{i}_0"], gs[f"ln_bias_{i}_0"] = dg0, db0
        return dx_part, dh, dhsum

    proj = _mm(x, wget("a_w_in", 0, None), lb=0, name="hg_proj")
    o_pre, og, states, scores = _hgrn2_fwd(proj, alb, norm_gain, rb=HG_ROWS)
    h0 = _mm(og, wget("a_w_out", 0, og), lb=0, name="hg_out")
    x1 = ln_fwd(x, h0, 0, 0, "ln_mix0")
    sv0, x3 = tail_fwd(x1, 0)
    kv = _mm(x3[1], wget("kv_w", 0, x3[1]), lb=0, bias=_row(kv_b), name="kv_proj")
    q = _mm(x3[1], wget("b_w_q", 0, x3[1]), lb=0, bias=b_q, name="q_proj")
    ao = _swa_fwd(q, kv, sinks)
    h1 = _mm(ao, wget("b_w_out", 0, x3[1]), lb=0, bias=b_out, name="att_out")
    x4 = ln_fwd(x3[0], h1, 1, 0, "ln_mix1")
    sv1, y = tail_fwd(x4, 1)
    dy, loss = _rowwise(_loss_fn, [y[0], target], [], [(sd, F32)], [((1, LANES), F32)], name="loss")

    dx4 = tail_bwd(dy, sv1, 1)
    dx3_part, dh1, dh1sum = mixer_ln_bwd(dx4, x3[0], h1, 1)
    gs["b_out"] = dh1sum
    grad_of("b_w_out", 0, ao, dh1)
    dao = mm(dh1, wget("b_w_out", 0, None), tb=True, name="att_out_dx")
    dq, dkv_cur, dkv_prev, dqsum, dsinks = _swa_bwd(q, kv, sinks, dao)
    gs["b_q"], gs["sinks"] = dqsum, dsinks
    dkv, dkvsum = _kv_grad_combine(dkv_cur, dkv_prev)
    gs["kv_b"] = dkvsum
    grad_of("b_w_q", 0, x3[1], dq)
    grad_of("kv_w", 0, x3[1], dkv)
    dx3 = mm(dq, wget("b_w_q", 0, None), tb=True, add=dx3_part, name="q_proj_dx")
    dx3 = mm(dkv, wget("kv_w", 0, None), tb=True, add=dx3, name="kv_proj_dx")
    dx1 = tail_bwd(dx3, sv0, 0)
    dx_part, dh0, _ = mixer_ln_bwd(dx1, x, h0, 0)
    grad_of("a_w_out", 0, og, dh0)
    dog = mm(dh0, wget("a_w_out", 0, None), tb=True, name="hg_out_dx")
    dqr, dfr, dvr, dgr, dalb, dgain = _hgrn2_bwd(proj, o_pre, states, scores, dog, alb, norm_gain, rb=HG_ROWS)
    gs["alb"], gs["norm_gain"] = dalb, dgain
    dproj = jnp.concatenate([dqr, dfr, dvr, dgr], axis=1)
    grad_of("a_w_in", 0, x, dproj)
    grad_x = mm(dproj, wget("a_w_in", 0, None), tb=True, add=dx_part, name="hg_proj_dx")
    return loss, grad_x, gs


HBM_SPEC = pl.BlockSpec(memory_space=pl.ANY)
HBM_ONLY = pl.BlockSpec(memory_space=pltpu.HBM)
SEM_SPEC = pl.BlockSpec(memory_space=pltpu.SEMAPHORE)
SIDE_EFFECT = pltpu.SideEffectType.DATAFLOW_SIDE_EFFECTING


def _piece(ref, kind, j):
    _, r, c = ref.shape
    if kind == "row":
        return ref.at[:, pl.ds(j * (r // N_CHIPS), r // N_CHIPS), :]
    return ref.at[:, :, pl.ds(j * (c // N_CHIPS), c // N_CHIPS)]


def _chip_of(j, c):
    return (j // 2, j % 2, c)


def _in_hbm(a):
    return pltpu.with_memory_space_constraint(a, pltpu.HBM)


def _place(src, kind, chip, *, mode, name, out_dtype, zone=None, zone_shape=None, layer=0):
    if mode == "gather":
        _, r, c = src.shape
        out_shape = (1, r * N_CHIPS, c) if kind == "row" else (1, r, c * N_CHIPS)
    else:
        out_shape = tuple(zone.shape) if zone is not None else tuple(zone_shape)
        r, c = out_shape[-2:]
    tm = _pick_rows(r, 512)
    nb = r // tm

    def full_idx(i, chip_ref):
        return (0, chip_ref[0] * nb + i, 0) if kind == "row" else (0, i, chip_ref[0])

    if mode == "gather":
        in_spec = pl.BlockSpec((None, tm, c), lambda i, chip_ref: (0, i, 0))
        out_spec = pl.BlockSpec((None, tm, c), full_idx)
    else:
        in_spec = pl.BlockSpec((None, tm, c), full_idx)
        out_spec = pl.BlockSpec((None, None, tm, c), lambda i, chip_ref: (chip_ref[0], layer, i, 0))
    in_specs, operands, aliases = [in_spec], [src], {}
    if zone is not None:
        in_specs.append(HBM_SPEC)
        operands.append(zone)
        aliases = {2: 0}

    def body(chip_ref, src_ref, *rest):
        rest[-1][...] = src_ref[...].astype(rest[-1].dtype)

    return pl.pallas_call(
        body, name=name,
        grid_spec=pltpu.PrefetchScalarGridSpec(num_scalar_prefetch=1, grid=(nb,), in_specs=in_specs,
                                               out_specs=out_spec),
        out_shape=jax.ShapeDtypeStruct(out_shape, out_dtype),
        input_output_aliases=aliases,
        compiler_params=_params(("arbitrary",)),
    )(chip, *operands)


class _Exchange:
    def __init__(self, mode, srcs, lands, kinds, layers, name):
        self.mode, self.kinds, self.layers, self.name, self.n = mode, kinds, layers, name, len(lands)
        n, ns = self.n, len(srcs)
        sem_shape = pltpu.SemaphoreType.DMA((n * N_CHIPS,))

        def body(*refs):
            src_refs, land_refs = refs[:ns], refs[ns:ns + n]
            send_sems, recv_sems = refs[ns + n], refs[ns + n + 1]
            token = refs[-1]
            c = lax.axis_index("c")
            me = 2 * lax.axis_index("x") + lax.axis_index("y")
            for j in range(N_CHIPS):
                @pl.when(me == j)
                def _():
                    for a in range(n):
                        for t in range(N_CHIPS):
                            if t != j:
                                src, dst = self._ends(src_refs, land_refs, a, j, t)
                                pltpu.make_async_remote_copy(
                                    src_ref=src, dst_ref=dst, send_sem=send_sems.at[a * N_CHIPS + t],
                                    recv_sem=recv_sems.at[a * N_CHIPS + j],
                                    device_id=_chip_of(t, c), device_id_type=MESH).start()
            token[...] = jnp.zeros(token.shape, token.dtype)

        arrays = list(srcs) + list(lands)
        outs = pl.pallas_call(
            body, name=name + "_start",
            in_specs=[HBM_ONLY] * (ns + n),
            out_specs=[SEM_SPEC, SEM_SPEC] + [HBM_ONLY] * (ns + n) + [pl.BlockSpec(memory_space=pltpu.VMEM)],
            out_shape=[sem_shape, sem_shape] + [pltpu.HBM(a.shape, a.dtype) for a in arrays]
                      + [jax.ShapeDtypeStruct((8, LANES), F32)],
            input_output_aliases={i: i + 2 for i in range(ns + n)},
            compiler_params=pltpu.CompilerParams(has_side_effects=SIDE_EFFECT),
        )(*[_in_hbm(a) for a in arrays])
        self.send_sems, self.recv_sems = outs[0], outs[1]
        self.srcs, self.lands = list(outs[2:2 + ns]), list(outs[2 + ns:2 + ns + n])
        self.token = outs[-1]

    def _ends(self, src_refs, land_refs, a, me_j, peer):
        if self.mode == "gather":
            mine = _piece(land_refs[a], self.kinds[a], me_j)
            return mine, mine
        return _piece(src_refs[a], self.kinds[a], peer), land_refs[a].at[me_j, pl.ds(self.layers[a], 1)]

    def wait(self, after, lands=None):
        n, ns = self.n, len(self.srcs)
        lands = self.lands if lands is None else lands

        def body(*refs):
            src_refs, land_refs = refs[:ns], refs[ns:ns + n]
            send_sems, recv_sems = refs[ns + n], refs[ns + n + 1]
            c = lax.axis_index("c")
            me = 2 * lax.axis_index("x") + lax.axis_index("y")
            for j in range(N_CHIPS):
                @pl.when(me != j)
                def _():
                    for a in range(n):
                        sent, _ = self._ends(src_refs, land_refs, a, 0, j)
                        _, landed = self._ends(src_refs, land_refs, a, j, 0)
                        cp = pltpu.make_async_remote_copy(
                            src_ref=sent, dst_ref=landed, send_sem=send_sems.at[a * N_CHIPS + j],
                            recv_sem=recv_sems.at[a * N_CHIPS + j],
                            device_id=_chip_of(j, c), device_id_type=MESH)
                        cp.wait_send()
                        cp.wait_recv()

        arrays = self.srcs + list(lands)
        operands = [_in_hbm(a) for a in arrays] + [self.send_sems, self.recv_sems]
        in_specs = [HBM_ONLY] * (ns + n) + [SEM_SPEC, SEM_SPEC]
        if after is not None:
            operands.append(after)
            in_specs.append(HBM_SPEC)
        outs = pl.pallas_call(
            body, name=self.name + "_wait",
            in_specs=in_specs, out_specs=[HBM_ONLY] * (ns + n),
            out_shape=[pltpu.HBM(a.shape, a.dtype) for a in arrays],
            input_output_aliases={i: i for i in range(ns + n)},
            compiler_params=pltpu.CompilerParams(has_side_effects=SIDE_EFFECT),
        )(*operands)
        return list(outs[ns:])


def _sibling_swap(arrays, name):
    n = len(arrays)

    def body(*refs):
        ins, outs = refs[:n], refs[n:2 * n]
        send_sems, recv_sems = refs[2 * n:]
        sibling = (lax.axis_index("x"), lax.axis_index("y"), 1 - lax.axis_index("c"))
        copies = [pltpu.make_async_remote_copy(src_ref=ins[a], dst_ref=outs[a], send_sem=send_sems.at[a],
                                               recv_sem=recv_sems.at[a], device_id=sibling, device_id_type=MESH)
                  for a in range(n)]
        for cp in copies:
            cp.start()
        for cp in copies:
            cp.wait()

    return pl.pallas_call(
        body, name=name,
        in_specs=[HBM_SPEC] * n, out_specs=[HBM_SPEC] * n,
        out_shape=[jax.ShapeDtypeStruct(a.shape, a.dtype) for a in arrays],
        scratch_shapes=[pltpu.SemaphoreType.DMA((n,)), pltpu.SemaphoreType.DMA((n,))],
    )(*arrays)


def _gather_devices(vec):
    def body(in_ref, out_ref, send_sems, recv_sems, local_sem):
        x, y, c = lax.axis_index("x"), lax.axis_index("y"), lax.axis_index("c")
        me = 4 * x + 2 * y + c
        mine = pltpu.make_async_copy(in_ref, out_ref.at[me], local_sem)
        mine.start()
        copies = []
        for rel in range(1, N_DEV):
            peer = (x ^ (rel >> 2), y ^ ((rel >> 1) & 1), c ^ (rel & 1))
            copies.append(pltpu.make_async_remote_copy(
                src_ref=in_ref, dst_ref=out_ref.at[me], send_sem=send_sems.at[rel], recv_sem=recv_sems.at[rel],
                device_id=peer, device_id_type=MESH))
        for cp in copies:
            cp.start()
        for cp in copies:
            cp.wait()
        mine.wait()

    return pl.pallas_call(
        body, name="gather_small",
        in_specs=[HBM_SPEC], out_specs=HBM_SPEC,
        out_shape=jax.ShapeDtypeStruct((N_DEV,) + vec.shape, vec.dtype),
        scratch_shapes=[pltpu.SemaphoreType.DMA((N_DEV,)), pltpu.SemaphoreType.DMA((N_DEV,)),
                        pltpu.SemaphoreType.DMA],
    )(vec)


BIG = [("a_w_in", "col"), ("a_w_out", "row"), ("kv_w", "row"), ("b_w_q", "row"), ("b_w_out", "row"),
       ("ffn_w_gate_up", "col"), ("ffn_w_down", "row"), ("ple_w_up", "col"), ("ple_w_gate", "row")]
GATHER_GROUPS = [[("a_w_in", 0), ("small", 0)], [("a_w_out", 0), ("ffn_w_gate_up", 0)],
                 [("ffn_w_down", 0), ("ple_w_gate", 0), ("ple_w_up", 0)], [("kv_w", 0), ("b_w_q", 0), ("b_w_out", 0)],
                 [("ffn_w_gate_up", 1)], [("ffn_w_down", 1), ("ple_w_gate", 1), ("ple_w_up", 1)]]
SCATTER_GROUPS = [[("ple_w_gate", 1), ("ple_w_up", 1), ("ffn_w_down", 1)], [("ffn_w_gate_up", 1)],
                  [("b_w_out", 0), ("b_w_q", 0), ("kv_w", 0)], [("ple_w_gate", 0), ("ple_w_up", 0), ("ffn_w_down", 0)],
                  [("ffn_w_gate_up", 0), ("a_w_out", 0)], [("a_w_in", 0)]]
SMALL_SHARDED = ["ln_gain", "ln_bias", "a_lower_bound"]
SMALL_REPLICATED = ["a_norm_gain", "kv_b", "b_b_q", "b_sinks", "b_b_out", "ple_b_gate"]
WEIGHT_ORDER = ["a_w_in", "a_lower_bound", "a_norm_gain", "a_w_out", "kv_w", "kv_b", "b_w_q", "b_b_q", "b_sinks",
                "b_w_out", "b_b_out", "ffn_w_gate_up", "ffn_w_down", "ple_w_up", "ple_w_gate", "ple_b_gate",
                "ln_gain", "ln_bias"]


def _as3(a):
    return a.reshape((-1,) + a.shape[-2:]) if a.ndim >= 3 else a.reshape((1,) + a.shape)


def _pad_lanes(v):
    n = v.shape[-1]
    return jnp.pad(v, ((0, 0), (0, (-n) % LANES)))


def _adam_small_fn(w, mom, vel, g):
    return _adam_fn(w, mom, vel, g, jnp.zeros_like(g))[1:]


def _sum_rows_fn(slots):
    acc = slots[0]
    for s in range(1, slots.shape[0]):
        acc = acc + slots[s]
    return (acc,)


def kernel(x, p, a_w_in, a_lower_bound, a_norm_gain, a_w_out, kv_w, kv_b, b_w_q, b_b_q, b_sinks, b_w_out, b_b_out, ffn_w_gate_up, ffn_w_down, ple_w_up, ple_w_gate, ple_b_gate, ln_gain, ln_bias, loss_target, m_a_w_in, m_a_lower_bound, m_a_norm_gain, m_a_w_out, m_kv_w, m_kv_b, m_b_w_q, m_b_b_q, m_b_sinks, m_b_w_out, m_b_b_out, m_ffn_w_gate_up, m_ffn_w_down, m_ple_w_up, m_ple_w_gate, m_ple_b_gate, m_ln_gain, m_ln_bias, v_a_w_in, v_a_lower_bound, v_a_norm_gain, v_a_w_out, v_kv_w, v_kv_b, v_b_w_q, v_b_b_q, v_b_sinks, v_b_w_out, v_b_b_out, v_ffn_w_gate_up, v_ffn_w_down, v_ple_w_up, v_ple_w_gate, v_ple_b_gate, v_ln_gain, v_ln_bias):
    args = dict(locals())
    wts = {n: args[n] for n in WEIGHT_ORDER}
    mom = {n: args["m_" + n] for n in WEIGHT_ORDER}
    vel = {n: args["v_" + n] for n in WEIGHT_ORDER}
    chip = 2 * lax.axis_index("x") + lax.axis_index("y")
    d = x.shape[-1]
    dq = d // N_CHIPS

    kind_of = dict(BIG)
    kind_of["small"] = "col"
    chip_arr = chip.reshape(1).astype(jnp.int32)
    placed = {}
    for n, _ in BIG:
        s3 = _as3(wts[n])
        for layer in range(s3.shape[0]):
            placed[(n, layer)] = _place(s3[layer:layer + 1], kind_of[n], chip_arr, mode="gather",
                                        name=f"place_{n}{layer}", out_dtype=BF16)
    small_pack = jnp.concatenate([wts[n].reshape(-1, dq) for n in SMALL_SHARDED], axis=0)[None]
    placed[("small", 0)] = _place(small_pack, "col", chip_arr, mode="gather", name="place_small", out_dtype=F32)

    gathers, where = [], {}
    for gi, group in enumerate(GATHER_GROUPS):
        gathers.append(_Exchange("gather", [], [placed[k] for k in group], [kind_of[k[0]] for k in group],
                                 [0] * len(group), f"gather{gi}"))
        for k in group:
            where[k] = gi
    all_started = jnp.concatenate([g.token for g in gathers[1:]], axis=0)
    ready = {}

    def wget(name, layer, after):
        key = (name, layer)
        if key not in ready:
            gi = where[key]
            outs = gathers[gi].wait(all_started if gi == 0 else after)
            for k, arr in zip(GATHER_GROUPS[gi], outs):
                ready[k] = arr
        return ready[key]

    small_full = wget("small", 0, None)[0]
    ln_gain_f = small_full[0:6].reshape(DEPTH, 3, d)
    ln_bias_f = small_full[6:12].reshape(DEPTH, 3, d)
    alb_f = small_full[12:14]

    group_of = {k: gi for gi, group in enumerate(SCATTER_GROUPS) for k in group}
    grads_done, zones, scatters = {}, {}, []

    def grad_sink(name, layer, grad):
        grads_done[(name, layer)] = grad
        zones[name] = _place(grad, kind_of[name], chip_arr, mode="scatter", name=f"place_grad_{name}{layer}",
                             out_dtype=BF16, zone=zones.get(name), zone_shape=(N_CHIPS,) + _as3(wts[name]).shape,
                             layer=layer)
        gi = group_of[(name, layer)]
        group = SCATTER_GROUPS[gi]
        if not all(k in grads_done for k in group):
            return None
        ex = _Exchange("scatter", [grads_done[k] for k in group], [zones[k[0]] for k in group],
                       [kind_of[k[0]] for k in group], [k[1] for k in group], f"scatter{gi}")
        for k, zone in zip(group, ex.lands):
            zones[k[0]] = zone
        scatters.append((ex, group))
        return ex.token

    loss, grad_x, gs = _local_step(
        x[0], p[:, 0], loss_target[0], wget, grad_sink, ln_gain_f, ln_bias_f, alb_f, a_norm_gain, kv_b, b_b_q,
        b_sinks, b_b_out, ple_b_gate)

    res = {}

    def arrive(batch, after):
        for ex, group in batch:
            outs = ex.wait(after, lands=[zones[k[0]] for k in group])
            for k, zone in zip(group, outs):
                zones[k[0]] = zone

    def update(names, tag):
        partial = []
        for n in names:
            s2 = zones[n].reshape(N_CHIPS, -1, zones[n].shape[-1])
            partial.append(_rowwise(_sum_slots_fn, [s2], [], [(s2.shape[1:], F32)], name=f"sum_{n}")[0])
        sibling = _sibling_swap(partial, tag)
        for n, own, sib in zip(names, partial, sibling):
            shp = wts[n].shape
            flat = lambda a: a.reshape(-1, shp[-1])
            out = _rowwise(_adam_fn, [flat(wts[n]), flat(mom[n]), flat(vel[n]), own, sib], [],
                           [(own.shape, F32)] * 4, name=f"adam_{n}")
            res[n] = [o.reshape(shp) for o in out]
        return res[names[-1]][1]

    last_names = [k[0] for k in SCATTER_GROUPS[-1]]
    arrive(scatters[:-1], grad_x)
    updated = update([n for n, _ in BIG if n not in last_names], "sibling_swap")
    arrive(scatters[-1:], updated)
    update(last_names, "sibling_swap_last")

    ln_g = jnp.concatenate([gs[f"ln_gain_{i}_{j}"] for i in range(DEPTH) for j in range(3)], axis=0)
    ln_b = jnp.concatenate([gs[f"ln_bias_{i}_{j}"] for i in range(DEPTH) for j in range(3)], axis=0)
    ple_bg = jnp.concatenate([gs[f"ple_b_{i}"] for i in range(DEPTH)], axis=0)
    small_list = [ln_g.reshape(1, -1), ln_b.reshape(1, -1), gs["alb"].reshape(1, -1), gs["norm_gain"],
                  gs["kv_b"], gs["b_q"], _pad_lanes(gs["sinks"]), gs["b_out"], ple_bg.reshape(1, -1), loss]
    small_vec = jnp.concatenate(small_list, axis=1)
    everyone = _gather_devices(small_vec)
    total, = _rowwise(_sum_rows_fn, [everyone], [], [(small_vec.shape, F32)], name="sum_small")
    offs, pos = [], 0
    for v in small_list:
        offs.append((pos, v.shape[1]))
        pos += v.shape[1]

    def seg(k):
        return total[0, offs[k][0]:offs[k][0] + offs[k][1]]

    def my_cols(full, rows):
        return lax.dynamic_slice_in_dim(full.reshape(rows, N_CHIPS, dq), chip, 1, axis=1).reshape(rows, dq)

    n_sink = b_sinks.shape[-1]
    small_grads = {
        "ln_gain": my_cols(seg(0), 6).reshape(ln_gain.shape), "ln_bias": my_cols(seg(1), 6).reshape(ln_bias.shape),
        "a_lower_bound": my_cols(seg(2), 2), "a_norm_gain": seg(3).reshape(a_norm_gain.shape),
        "kv_b": seg(4).reshape(kv_b.shape), "b_b_q": seg(5).reshape(b_b_q.shape),
        "b_sinks": seg(6)[:n_sink].reshape(b_sinks.shape), "b_b_out": seg(7).reshape(b_b_out.shape),
        "ple_b_gate": seg(8).reshape(ple_b_gate.shape)}
    names = SMALL_SHARDED + SMALL_REPLICATED
    pack = lambda dct: _pad_lanes(jnp.concatenate([dct[n].reshape(1, -1) for n in names], axis=1))
    g_pack = pack(small_grads)
    upd = _rowwise(_adam_small_fn, [pack(wts), pack(mom), pack(vel), g_pack], [], [(g_pack.shape, F32)] * 3,
                   name="adam_small")
    pos = 0
    for n in names:
        size = wts[n].size
        res[n] = [small_grads[n]] + [u[0, pos:pos + size].reshape(wts[n].shape) for u in upd]
        pos += size

    outs = [seg(9)[0], grad_x[None]]
    for k in range(4):
        outs += [res[n][k] for n in WEIGHT_ORDER]
    return tuple(outs)
```

```python
import functools

import jax
import jax.numpy as jnp
from jax import lax
from jax.experimental import pallas as pl
from jax.experimental.pallas import tpu as pltpu

F32 = jnp.float32
BF16 = jnp.bfloat16
MESH = pl.DeviceIdType.MESH

LANES = 128
HG_DK = 128
HG_CHUNK = 64
HG_SUB = 16
HG_ROWS = 512
ATT_HD = 64
ATT_G = 4
WINDOW = 128
DEPTH = 2
ALPHA = (2.0 * DEPTH) ** 0.25
LN_EPS = 1e-5
RMS_EPS = 1e-6
ADAM_LR, ADAM_B1, ADAM_B2, ADAM_EPS, ADAM_WD, ADAM_STEP = 0.001, 0.9, 0.999, 1e-08, 0.01, 10
N_CHIPS = 4
N_DEV = 8
VMEM_LIMIT = 56 * 1024 * 1024
NEG = -1e30


def _pick(n, cap):
    best = None
    for d in range(LANES, min(n, cap) + 1, LANES):
        if n % d == 0:
            best = d
    return n if best is None else best


def _pick_rows(m, cap):
    best = None
    for d in range(16, min(m, cap) + 1, 16):
        if m % d == 0:
            best = d
    return m if best is None else best


def _params(sem):
    return pltpu.CompilerParams(dimension_semantics=sem, vmem_limit_bytes=VMEM_LIMIT)


def _mm(a, b, *, name, la=None, lb=None, ta=False, tb=False, bias=None, add=None, out_dtype=F32,
        out_layers=None, out_layer=None, after=None, caps=(1024, 1536, 2048)):
    ar, ac = a.shape[-2:]
    br, bc = b.shape[-2:]
    m, k = (ac, ar) if ta else (ar, ac)
    k2, n = (bc, br) if tb else (br, bc)
    assert k == k2, (a.shape, b.shape, ta, tb)
    tm, tn, tk = _pick(m, caps[0]), _pick(n, caps[1]), _pick(k, caps[2])
    nk = k // tk
    grid = (m // tm, n // tn, nk)

    def spec(block, idx, layer):
        if layer is None:
            return pl.BlockSpec(block, idx)
        return pl.BlockSpec((None,) + block, lambda i, j, kk: (layer,) + idx(i, j, kk))

    a_spec = spec((tk, tm), lambda i, j, kk: (kk, i), la) if ta else spec((tm, tk), lambda i, j, kk: (i, kk), la)
    b_spec = spec((tn, tk), lambda i, j, kk: (j, kk), lb) if tb else spec((tk, tn), lambda i, j, kk: (kk, j), lb)
    in_specs, operands = [a_spec, b_spec], [a, b]
    if bias is not None:
        in_specs.append(pl.BlockSpec((1, tn), lambda i, j, kk: (0, j)))
        operands.append(bias)
    if add is not None:
        in_specs.append(pl.BlockSpec((tm, tn), lambda i, j, kk: (i, j)))
        operands.append(add)
    if after is not None:
        in_specs.append(pl.BlockSpec(memory_space=pl.ANY))
        operands.append(after)
    if out_layers is None:
        out_shape = jax.ShapeDtypeStruct((m, n), out_dtype)
    else:
        out_shape = jax.ShapeDtypeStruct((out_layers, m, n), out_dtype)
    out_spec = spec((tm, tn), lambda i, j, kk: (i, j), out_layer)
    dims = (((0 if ta else 1,), (1 if tb else 0,)), ((), ()))
    has_bias, has_add, has_alias = bias is not None, add is not None, after is not None

    def body(*refs):
        a_ref, b_ref = refs[0], refs[1]
        pos = 2
        bias_ref = add_ref = None
        if has_bias:
            bias_ref = refs[pos]
            pos += 1
        if has_add:
            add_ref = refs[pos]
            pos += 1
        if has_alias:
            pos += 1
        o_ref = refs[pos]
        acc_ref = refs[pos + 1] if nk > 1 else None
        part = lax.dot_general(a_ref[...].astype(BF16), b_ref[...].astype(BF16), dims, preferred_element_type=F32)

        def finish(total):
            if has_bias:
                total = total + bias_ref[...]
            if has_add:
                total = total + add_ref[...]
            o_ref[...] = total.astype(o_ref.dtype)

        if nk == 1:
            finish(part)
        else:
            kk = pl.program_id(2)

            @pl.when(kk == 0)
            def _():
                acc_ref[...] = part

            @pl.when(kk > 0)
            def _():
                acc_ref[...] += part

            @pl.when(kk == nk - 1)
            def _():
                finish(acc_ref[...])

    return pl.pallas_call(
        body, name=name, grid=grid, in_specs=in_specs, out_specs=out_spec, out_shape=out_shape,
        scratch_shapes=[pltpu.VMEM((tm, tn), F32)] if nk > 1 else [],
        compiler_params=_params(("parallel", "parallel", "arbitrary")),
    )(*operands)


def _rowwise(fn, rows, whole, outs, sums=(), *, name, tm=256):
    m = rows[0].shape[-2]
    tm = _pick_rows(m, tm)
    n_rows, n_whole, n_outs, n_sums = len(rows), len(whole), len(outs), len(sums)

    def rspec(shape):
        lead = len(shape) - 2
        return pl.BlockSpec(tuple(shape[:-2]) + (tm, shape[-1]), lambda i: (0,) * lead + (i, 0))

    def wspec(shape):
        return pl.BlockSpec(tuple(shape), lambda i: (0,) * len(shape))

    def body(*refs):
        vals = [r[...] for r in refs[:n_rows + n_whole]]
        out_refs = refs[n_rows + n_whole:n_rows + n_whole + n_outs]
        sum_refs = refs[n_rows + n_whole + n_outs:]
        res = fn(*vals)
        for ref, val in zip(out_refs, res[:n_outs]):
            ref[...] = val.astype(ref.dtype)
        if n_sums:
            @pl.when(pl.program_id(0) == 0)
            def _():
                for ref in sum_refs:
                    ref[...] = jnp.zeros(ref.shape, ref.dtype)

            for ref, val in zip(sum_refs, res[n_outs:]):
                ref[...] += val

    result = pl.pallas_call(
        body, name=name, grid=(m // tm,),
        in_specs=[rspec(r.shape) for r in rows] + [wspec(w.shape) for w in whole],
        out_specs=[rspec(s) for s, _ in outs] + [wspec(s) for s, _ in sums],
        out_shape=[jax.ShapeDtypeStruct(s, d) for s, d in list(outs) + list(sums)],
        compiler_params=_params(("arbitrary",)),
    )(*rows, *whole)
    return result


def _sigmoid(v):
    return jax.nn.sigmoid(v)


def _col_sum(v):
    return jnp.sum(v, axis=0, keepdims=True)


def _ln_stats(z):
    mu = jnp.mean(z, axis=-1, keepdims=True)
    zc = z - mu
    var = jnp.mean(zc * zc, axis=-1, keepdims=True)
    rstd = lax.rsqrt(var + LN_EPS)
    return zc * rstd, rstd


def _ln_fwd_fn(xin, h, gain, bias):
    xhat, _ = _ln_stats(ALPHA * xin + h)
    y = xhat * gain + bias
    return y, y


def _ple_ln_fwd_fn(xin, pg, pu, gain, bias):
    xhat, _ = _ln_stats(ALPHA * xin + _sigmoid(pg) * pu)
    y = xhat * gain + bias
    return y, y


def _ln_dz(dy, z, gain):
    xhat, rstd = _ln_stats(z)
    dxhat = dy * gain
    dz = rstd * (dxhat - jnp.mean(dxhat, axis=-1, keepdims=True)
                 - xhat * jnp.mean(dxhat * xhat, axis=-1, keepdims=True))
    return dz, _col_sum(dy * xhat), _col_sum(dy)


def _ln_bwd_fn(dy, xin, h, gain):
    dz, dgain, dbias = _ln_dz(dy, ALPHA * xin + h, gain)
    return ALPHA * dz, dz, dgain, dbias, _col_sum(dz)


def _ple_ln_bwd_fn(dy, xin, pg, pu, gain):
    sg = _sigmoid(pg)
    dz, dgain, dbias = _ln_dz(dy, ALPHA * xin + sg * pu, gain)
    dpg = dz * pu * sg * (1.0 - sg)
    return ALPHA * dz, dpg, dz * sg, dgain, dbias, _col_sum(dpg)


def _swiglu_fwd_fn(gu):
    hid = gu.shape[-1] // 2
    gate, up = gu[:, :hid], gu[:, hid:]
    return (gate * _sigmoid(gate) * up,)


def _swiglu_bwd_fn(gu, dact):
    hid = gu.shape[-1] // 2
    gate, up = gu[:, :hid], gu[:, hid:]
    sg = _sigmoid(gate)
    dgate = dact * up * sg * (1.0 + gate * (1.0 - sg))
    dup = dact * gate * sg
    return (jnp.concatenate([dgate, dup], axis=-1),)


def _loss_fn(y, target):
    err = y - target
    inv = 1.0 / y.shape[-1]
    part = 0.5 * inv * jnp.sum(jnp.sum(err * err, axis=-1, keepdims=True), axis=0, keepdims=True)
    return err * inv, jnp.broadcast_to(part, (1, LANES))


def _adam_fn(w, mom, vel, p_own, p_sib):
    g = p_own + p_sib
    m_new = ADAM_B1 * mom + (1.0 - ADAM_B1) * g
    v_new = ADAM_B2 * vel + (1.0 - ADAM_B2) * (g * g)
    m_hat = m_new / (1.0 - ADAM_B1 ** ADAM_STEP)
    v_hat = v_new / (1.0 - ADAM_B2 ** ADAM_STEP)
    delta = -ADAM_LR * (m_hat / (jnp.sqrt(v_hat) + ADAM_EPS) + ADAM_WD * w)
    return g, delta, m_new, v_new


def _sum_slots_fn(slots):
    acc = slots[0].astype(F32)
    for s in range(1, slots.shape[0]):
        acc = acc + slots[s].astype(F32)
    return (acc,)


def _split2(x):
    hi = x.astype(BF16)
    return hi, (x - hi.astype(F32)).astype(BF16)


def _dot3(a, b, dims):
    a_hi, a_lo = _split2(a)
    b_hi, b_lo = _split2(b)
    dn = (dims, ((), ()))
    return (lax.dot_general(a_hi, b_hi, dn, preferred_element_type=F32)
            + (lax.dot_general(a_hi, b_lo, dn, preferred_element_type=F32)
               + lax.dot_general(a_lo, b_hi, dn, preferred_element_type=F32)))


def _tdot(mask01, b):
    m = mask01.astype(BF16)
    b_hi = b.astype(BF16)
    rest = b - b_hi.astype(F32)
    b_mid = rest.astype(BF16)
    b_lo = (rest - b_mid.astype(F32)).astype(BF16)
    dn = (((1,), (0,)), ((), ()))
    return (lax.dot_general(m, b_hi, dn, preferred_element_type=F32)
            + (lax.dot_general(m, b_mid, dn, preferred_element_type=F32)
               + lax.dot_general(m, b_lo, dn, preferred_element_type=F32)))


def _hdot(a, b):
    return _dot3(a, b, ((1,), (0,)))


def _hdot_nt(a, b):
    return _dot3(a, b, ((1,), (1,)))


def _hdot_tn(a, b):
    return _dot3(a, b, ((0,), (0,)))


def _dot(a, b):
    return lax.dot_general(a.astype(BF16), b.astype(BF16), (((1,), (0,)), ((), ())), preferred_element_type=F32)


def _dot_nt(a, b):
    return lax.dot_general(a.astype(BF16), b.astype(BF16), (((1,), (1,)), ((), ())), preferred_element_type=F32)


def _dot_tn(a, b):
    return lax.dot_general(a.astype(BF16), b.astype(BF16), (((0,), (0,)), ((), ())), preferred_element_type=F32)


def _hg_masks():
    c = HG_CHUNK
    row = lax.broadcasted_iota(jnp.int32, (c, c), 0)
    col = lax.broadcasted_iota(jnp.int32, (c, c), 1)
    base = row & (-HG_SUB)
    return row, col, base, col <= row, col < base


def _hg_gates(qr, fr, alb):
    lbound = _sigmoid(alb[0:1, :] - alb[1:2, :])
    sig = _sigmoid(fr)
    forget = lbound + (1.0 - lbound) * sig
    kk = (1.0 - lbound) * _sigmoid(-fr)
    qt = qr * _sigmoid(qr) * (HG_DK ** -0.5)
    return qt, kk, jnp.log(forget), lbound, sig, forget


def _hg_scores(qt, kk, g, scores=True):
    c, nsub = HG_CHUNK, HG_CHUNK // HG_SUB
    row, col, base, causal, below = _hg_masks()
    b = _tdot(causal, g)
    rr = _tdot(below, g)
    bq = b - rr
    qh = qt * jnp.exp(bq)
    edecs = [None]
    parts = [jnp.zeros((HG_SUB, c), F32)]
    for i in range(1, nsub):
        edec = jnp.exp(jnp.minimum(rr[i * HG_SUB:i * HG_SUB + 1, :] - b, 0.0))
        edecs.append(edec)
        if scores:
            parts.append(_dot_nt(qh[i * HG_SUB:(i + 1) * HG_SUB, :], kk * edec))
    b3 = b.reshape(nsub, HG_SUB, HG_DK)
    q3 = qt.reshape(nsub, HG_SUB, HG_DK)
    k3 = kk.reshape(nsub, HG_SUB, HG_DK)
    if not scores:
        return None, b, bq, qh, edecs, (b3, q3, k3)
    a = jnp.where(below, jnp.concatenate(parts, axis=0), 0.0)
    for j in range(HG_SUB):
        e = jnp.exp(jnp.minimum(b3 - b3[:, j:j + 1, :], 0.0))
        colv = jnp.sum(q3 * e * k3[:, j:j + 1, :], axis=-1, keepdims=True).reshape(c, 1)
        a = jnp.where(col == base + j, colv, a)
    a = jnp.where(causal, a, 0.0)
    return a, b, bq, qh, edecs, (b3, q3, k3)


def _hg_norm(o, gr, gain):
    r = lax.rsqrt(jnp.mean(o * o, axis=-1, keepdims=True) + RMS_EPS)
    sg = _sigmoid(gr)
    return o * r * gain, r, sg


def _hgrn2_fwd(proj, alb, gain, *, rb):
    m, d4 = proj.shape
    d = d4 // 4
    heads = d // HG_DK
    rb = min(rb, m)
    cpb = rb // HG_CHUNK
    nrb = m // rb

    def body(q_ref, f_ref, v_ref, g_ref, alb_ref, gain_ref, o_ref, og_ref, st_ref, a_ref, state):
        @pl.when(pl.program_id(1) == 0)
        def _():
            state[...] = jnp.zeros(state.shape, F32)

        def chunk(ci, carry):
            sl = pl.ds(pl.multiple_of(ci * HG_CHUNK, HG_CHUNK), HG_CHUNK)
            qt, kk, g, _, _, _ = _hg_gates(q_ref[sl, :], f_ref[sl, :], alb_ref[...])
            v = v_ref[sl, :]
            st = state[...]
            st_ref[ci] = st
            a, b, _, _, _, _ = _hg_scores(qt, kk, g)
            a_ref[ci] = a.astype(a_ref.dtype)
            o = _dot(a, v) + _dot_nt(qt * jnp.exp(b), st)
            b_last = b[HG_CHUNK - 1:HG_CHUNK, :]
            state[...] = st * jnp.exp(b_last) + _hdot_tn(v, kk * jnp.exp(b_last - b))
            o_ref[sl, :] = o
            n, _, sg = _hg_norm(o, g_ref[sl, :], gain_ref[...])
            og_ref[sl, :] = (n * g_ref[sl, :] * sg).astype(og_ref.dtype)
            return carry

        lax.fori_loop(0, cpb, chunk, 0, unroll=2)

    def col(cidx):
        return pl.BlockSpec((rb, HG_DK), lambda h, r: (r, cidx * heads + h))

    return pl.pallas_call(
        body, name="hgrn2_fwd", grid=(heads, nrb),
        in_specs=[col(0), col(1), col(2), col(3),
                  pl.BlockSpec((2, HG_DK), lambda h, r: (0, h)),
                  pl.BlockSpec((1, HG_DK), lambda h, r: (0, 0))],
        out_specs=[pl.BlockSpec((rb, HG_DK), lambda h, r: (r, h)),
                   pl.BlockSpec((rb, HG_DK), lambda h, r: (r, h)),
                   pl.BlockSpec((None, cpb, HG_DK, HG_DK), lambda h, r: (h, r, 0, 0)),
                   pl.BlockSpec((None, cpb, HG_CHUNK, HG_CHUNK), lambda h, r: (h, r, 0, 0))],
        out_shape=[jax.ShapeDtypeStruct((m, d), F32), jax.ShapeDtypeStruct((m, d), BF16),
                   jax.ShapeDtypeStruct((heads, m // HG_CHUNK, HG_DK, HG_DK), F32),
                   jax.ShapeDtypeStruct((heads, m // HG_CHUNK, HG_CHUNK, HG_CHUNK), BF16)],
        scratch_shapes=[pltpu.VMEM((HG_DK, HG_DK), F32)],
        compiler_params=_params(("parallel", "arbitrary")),
    )(proj, proj, proj, proj, alb, gain)


def _hgrn2_bwd(proj, o_pre, states, scores, dog, alb, gain, *, rb):
    m, d4 = proj.shape
    d = d4 // 4
    heads = d // HG_DK
    rb = min(rb, m)
    cpb = rb // HG_CHUNK
    nrb = m // rb
    c, nsub = HG_CHUNK, HG_CHUNK // HG_SUB

    def body(q_ref, f_ref, v_ref, g_ref, o_ref, st_ref, a_ref, dog_ref, alb_ref, gain_ref,
             dq_ref, df_ref, dv_ref, dg_ref, dalb_ref, dgain_ref, dstate, carry_ref):
        first = (pl.program_id(0) == 0) & (pl.program_id(1) == 0)

        @pl.when(first)
        def _():
            dgain_ref[...] = jnp.zeros(dgain_ref.shape, F32)

        @pl.when(pl.program_id(1) == 0)
        def _():
            dstate[...] = jnp.zeros(dstate.shape, F32)
            carry_ref[...] = jnp.zeros(carry_ref.shape, F32)
            dalb_ref[...] = jnp.zeros(dalb_ref.shape, F32)

        row, col, base, causal, below = _hg_masks()
        sub_iota = lax.broadcasted_iota(jnp.int32, (nsub, HG_SUB, HG_DK), 1)
        row_k = lax.broadcasted_iota(jnp.int32, (c, HG_DK), 0)
        upper = col >= row

        def chunk(step, carry):
            ci = cpb - 1 - step
            sl = pl.ds(pl.multiple_of(ci * HG_CHUNK, HG_CHUNK), HG_CHUNK)
            qr, fr, v, gr = q_ref[sl, :], f_ref[sl, :], v_ref[sl, :], g_ref[sl, :]
            qt, kk, g, lbound, sig, forget = _hg_gates(qr, fr, alb_ref[...])
            o = o_ref[sl, :]
            dogv = dog_ref[sl, :]
            gain_v = gain_ref[...]
            n, r, sg = _hg_norm(o, gr, gain_v)
            dgr = dogv * n * sg * (1.0 + gr * (1.0 - sg))
            dn = dogv * gr * sg
            dgain_ref[...] += _col_sum(dn * o * r)
            u = dn * gain_v
            d_o = r * u - o * (r * r * r) * jnp.mean(u * o, axis=-1, keepdims=True)
            st0 = st_ref[ci]
            dst = dstate[...]
            _, b, bq, qh, edecs, (b3, q3, k3) = _hg_scores(qt, kk, g, scores=False)
            a = a_ref[ci]
            eb = jnp.exp(b)
            b_last = b[c - 1:c, :]
            kdl_dec = jnp.exp(b_last - b)
            kdl = kk * kdl_dec
            d_a = jnp.where(causal, _dot_nt(d_o, v), 0.0)
            d_at = _dot_nt(v, d_o)
            dv = _dot_tn(a, d_o) + _dot_nt(kdl, dst)
            dq = eb * _hdot(d_o, st0)
            dk = _hdot(v, dst) * kdl_dec
            d_a_below = jnp.where(below, d_a, 0.0)
            dq_parts = [jnp.zeros((HG_SUB, HG_DK), F32)]
            for i in range(1, nsub):
                lo, hi = i * HG_SUB, (i + 1) * HG_SUB
                dq_parts.append(_hdot(d_a_below[lo:hi, :], kk * edecs[i]))
                gi = _hdot(d_at[:, lo:hi], qh[lo:hi, :])
                dk = dk + jnp.where(row_k < lo, edecs[i] * gi, 0.0)
            dq = dq + jnp.concatenate(dq_parts, axis=0) * jnp.exp(bq)
            dq3 = jnp.zeros((nsub, HG_SUB, HG_DK), F32)
            dk3 = jnp.zeros((nsub, HG_SUB, HG_DK), F32)
            for j in range(HG_SUB):
                e = jnp.exp(jnp.minimum(b3 - b3[:, j:j + 1, :], 0.0))
                dcol = jnp.sum(jnp.where(col == base + j, d_a, 0.0), axis=-1, keepdims=True)
                t1 = dcol.reshape(nsub, HG_SUB, 1) * e
                dq3 = dq3 + t1 * k3[:, j:j + 1, :]
                dk3 = jnp.where(sub_iota == j, jnp.sum(t1 * q3, axis=1, keepdims=True), dk3)
            dq = dq + dq3.reshape(c, HG_DK)
            dk = dk + dk3.reshape(c, HG_DK)
            dstate[...] = dst * jnp.exp(b_last) + _hdot_tn(d_o, qt * eb)
            dglog = _tdot(upper, qt * dq - kk * dk) + carry_ref[...]
            carry_ref[...] = dglog[0:1, :]
            dforget = dglog / forget
            one_m_lb = 1.0 - lbound
            dsig = (dforget - dk) * one_m_lb
            sneg = _sigmoid(-fr)
            dlb = _col_sum(dforget * (1.0 - sig) - dk * sneg)
            dalb0 = dlb * lbound * one_m_lb
            dalb_ref[...] += jnp.concatenate([dalb0, -dalb0], axis=0)
            sq = _sigmoid(qr)
            dq_ref[sl, :] = (dq * (HG_DK ** -0.5) * sq * (1.0 + qr * (1.0 - sq))).astype(dq_ref.dtype)
            df_ref[sl, :] = (dsig * sig * (1.0 - sig)).astype(df_ref.dtype)
            dv_ref[sl, :] = dv.astype(dv_ref.dtype)
            dg_ref[sl, :] = dgr.astype(dg_ref.dtype)
            return carry

        lax.fori_loop(0, cpb, chunk, 0, unroll=2)

    def rev(r):
        return nrb - 1 - r

    def col(cidx):
        return pl.BlockSpec((rb, HG_DK), lambda h, r: (rev(r), cidx * heads + h))

    def head_rows():
        return pl.BlockSpec((rb, HG_DK), lambda h, r: (rev(r), h))

    return pl.pallas_call(
        body, name="hgrn2_bwd", grid=(heads, nrb),
        in_specs=[col(0), col(1), col(2), col(3), head_rows(),
                  pl.BlockSpec((None, cpb, HG_DK, HG_DK), lambda h, r: (h, rev(r), 0, 0)),
                  pl.BlockSpec((None, cpb, HG_CHUNK, HG_CHUNK), lambda h, r: (h, rev(r), 0, 0)),
                  head_rows(),
                  pl.BlockSpec((2, HG_DK), lambda h, r: (0, h)),
                  pl.BlockSpec((1, HG_DK), lambda h, r: (0, 0))],
        out_specs=[head_rows(), head_rows(), head_rows(), head_rows(),
                   pl.BlockSpec((2, HG_DK), lambda h, r: (0, h)),
                   pl.BlockSpec((1, HG_DK), lambda h, r: (0, 0))],
        out_shape=[jax.ShapeDtypeStruct((m, d), BF16)] * 4
                  + [jax.ShapeDtypeStruct((2, d), F32), jax.ShapeDtypeStruct((1, HG_DK), F32)],
        scratch_shapes=[pltpu.VMEM((HG_DK, HG_DK), F32), pltpu.VMEM((1, HG_DK), F32)],
        compiler_params=_params(("arbitrary", "arbitrary")),
    )(proj, proj, proj, proj, o_pre, states, scores, dog, alb, gain)


def _swa_probs(qh, kp, kc, sink, slope, has_prev):
    rows = qh.shape[0]
    qi = lax.broadcasted_iota(jnp.int32, (rows, WINDOW), 0) & (WINDOW - 1)
    si = lax.broadcasted_iota(jnp.int32, (rows, WINDOW), 1)
    scale = ATT_HD ** -0.5
    dist_c = (qi - si).astype(F32)
    s_p = _dot_nt(qh, kp) * scale - slope * (dist_c + float(WINDOW))
    s_c = _dot_nt(qh, kc) * scale - slope * dist_c
    s_p = jnp.where((si > qi) & has_prev, s_p, NEG)
    s_c = jnp.where(si <= qi, s_c, NEG)
    mx = jnp.maximum(jnp.maximum(jnp.max(s_p, axis=-1, keepdims=True), jnp.max(s_c, axis=-1, keepdims=True)), sink)
    e_p, e_c, e_s = jnp.exp(s_p - mx), jnp.exp(s_c - mx), jnp.exp(sink - mx)
    inv = 1.0 / (jnp.sum(e_p, axis=-1, keepdims=True) + jnp.sum(e_c, axis=-1, keepdims=True) + e_s)
    return e_p * inv, e_c * inv, e_s * inv


def _slope(h, n_heads):
    return float(2.0 ** (-8.0 * (h + 1) / n_heads))


def _swa_group(ref_vals, sink_ref, kh, n_heads):
    heads = [kh * ATT_G + g for g in range(ATT_G)]
    stacked = [jnp.concatenate([v[:, h * ATT_HD:(h + 1) * ATT_HD] for h in heads], axis=0) for v in ref_vals]
    grp = lax.shift_right_logical(lax.broadcasted_iota(jnp.int32, (ATT_G * WINDOW, 1), 0), WINDOW.bit_length() - 1)
    slope = jnp.zeros((ATT_G * WINDOW, 1), F32)
    sink = jnp.zeros((ATT_G * WINDOW, 1), F32)
    for g, h in enumerate(heads):
        slope = jnp.where(grp == g, _slope(h, n_heads), slope)
        sink = jnp.where(grp == g, sink_ref[:, h:h + 1], sink)
    return stacked, slope, sink


def _swa_fwd(q, kv, sinks):
    m, d = q.shape
    n_heads = d // ATT_HD
    kvh = n_heads // ATT_G
    kd = kvh * ATT_HD
    nb = m // WINDOW

    def body(q_ref, kvp_ref, kvc_ref, sink_ref, o_ref):
        has_prev = pl.program_id(0) > 0
        qv, kvp, kvc = q_ref[...], kvp_ref[...], kvc_ref[...]
        outs = []
        for kh in range(kvh):
            ks = slice(kh * ATT_HD, (kh + 1) * ATT_HD)
            vs = slice(kd + kh * ATT_HD, kd + (kh + 1) * ATT_HD)
            (q4,), slope, sink = _swa_group([qv], sink_ref, kh, n_heads)
            p_p, p_c, _ = _swa_probs(q4, kvp[:, ks], kvc[:, ks], sink, slope, has_prev)
            o4 = _dot(p_p, kvp[:, vs]) + _dot(p_c, kvc[:, vs])
            outs += [o4[g * WINDOW:(g + 1) * WINDOW, :] for g in range(ATT_G)]
        o_ref[...] = jnp.concatenate(outs, axis=-1).astype(o_ref.dtype)

    return pl.pallas_call(
        body, name="swa_fwd", grid=(nb,),
        in_specs=[pl.BlockSpec((WINDOW, d), lambda n: (n, 0)),
                  pl.BlockSpec((WINDOW, 2 * kd), lambda n: (jnp.maximum(n - 1, 0), 0)),
                  pl.BlockSpec((WINDOW, 2 * kd), lambda n: (n, 0)),
                  pl.BlockSpec((1, n_heads), lambda n: (0, 0))],
        out_specs=pl.BlockSpec((WINDOW, d), lambda n: (n, 0)),
        out_shape=jax.ShapeDtypeStruct((m, d), BF16),
        compiler_params=_params(("arbitrary",)),
    )(q, kv, kv, sinks)


def _swa_bwd(q, kv, sinks, dao):
    m, d = q.shape
    n_heads = d // ATT_HD
    kvh = n_heads // ATT_G
    kd = kvh * ATT_HD
    nb = m // WINDOW
    scale = ATT_HD ** -0.5

    def body(q_ref, kvp_ref, kvc_ref, sink_ref, do_ref, dq_ref, dkvc_ref, dkvp_ref, dqsum_ref, dsink_ref):
        @pl.when(pl.program_id(0) == 0)
        def _():
            dqsum_ref[...] = jnp.zeros(dqsum_ref.shape, F32)
            dsink_ref[...] = jnp.zeros(dsink_ref.shape, F32)

        has_prev = pl.program_id(0) > 0
        qv, kvp, kvc, dov = q_ref[...], kvp_ref[...], kvc_ref[...], do_ref[...]
        lane_h = lax.broadcasted_iota(jnp.int32, (1, n_heads), 1)
        dsink = jnp.zeros((1, n_heads), F32)
        dq_parts, dk_p, dk_c, dv_p, dv_c = [], [], [], [], []
        for kh in range(kvh):
            ks = slice(kh * ATT_HD, (kh + 1) * ATT_HD)
            vs = slice(kd + kh * ATT_HD, kd + (kh + 1) * ATT_HD)
            kp, kc, vp, vc = kvp[:, ks], kvc[:, ks], kvp[:, vs], kvc[:, vs]
            (q4, do4), slope, sink = _swa_group([qv, dov], sink_ref, kh, n_heads)
            p_p, p_c, p_s = _swa_probs(q4, kp, kc, sink, slope, has_prev)
            dp_p, dp_c = _dot_nt(do4, vp), _dot_nt(do4, vc)
            delta = jnp.sum(p_p * dp_p, axis=-1, keepdims=True) + jnp.sum(p_c * dp_c, axis=-1, keepdims=True)
            ds_p, ds_c = p_p * (dp_p - delta), p_c * (dp_c - delta)
            sink_term = p_s * delta
            dq4 = (_dot(ds_p, kp) + _dot(ds_c, kc)) * scale
            for g in range(ATT_G):
                rows = slice(g * WINDOW, (g + 1) * WINDOW)
                dsink = dsink + jnp.where(lane_h == kh * ATT_G + g, -_col_sum(sink_term[rows, :]), 0.0)
                dq_parts.append(dq4[rows, :])
            dk_p.append(_dot_tn(ds_p, q4) * scale)
            dk_c.append(_dot_tn(ds_c, q4) * scale)
            dv_p.append(_dot_tn(p_p, do4))
            dv_c.append(_dot_tn(p_c, do4))
        dq = jnp.concatenate(dq_parts, axis=-1)
        dq_ref[...] = dq.astype(dq_ref.dtype)
        dqsum_ref[...] += _col_sum(dq)
        dsink_ref[...] += dsink
        dkvc_ref[...] = jnp.concatenate(dk_c + dv_c, axis=-1)
        dkvp_ref[...] = jnp.concatenate(dk_p + dv_p, axis=-1)

    return pl.pallas_call(
        body, name="swa_bwd", grid=(nb,),
        in_specs=[pl.BlockSpec((WINDOW, d), lambda n: (n, 0)),
                  pl.BlockSpec((WINDOW, 2 * kd), lambda n: (jnp.maximum(n - 1, 0), 0)),
                  pl.BlockSpec((WINDOW, 2 * kd), lambda n: (n, 0)),
                  pl.BlockSpec((1, n_heads), lambda n: (0, 0)),
                  pl.BlockSpec((WINDOW, d), lambda n: (n, 0))],
        out_specs=[pl.BlockSpec((WINDOW, d), lambda n: (n, 0)),
                   pl.BlockSpec((WINDOW, 2 * kd), lambda n: (n, 0)),
                   pl.BlockSpec((WINDOW, 2 * kd), lambda n: (n, 0)),
                   pl.BlockSpec((1, d), lambda n: (0, 0)),
                   pl.BlockSpec((1, n_heads), lambda n: (0, 0))],
        out_shape=[jax.ShapeDtypeStruct((m, d), BF16), jax.ShapeDtypeStruct((m, 2 * kd), F32),
                   jax.ShapeDtypeStruct((m, 2 * kd), F32), jax.ShapeDtypeStruct((1, d), F32),
                   jax.ShapeDtypeStruct((1, n_heads), F32)],
        compiler_params=_params(("arbitrary",)),
    )(q, kv, kv, sinks, dao)


def _kv_grad_combine(dkv_cur, dkv_prev):
    m, w = dkv_cur.shape
    nb = m // WINDOW

    def body(cur_ref, nxt_ref, o_ref, sum_ref):
        @pl.when(pl.program_id(0) == 0)
        def _():
            sum_ref[...] = jnp.zeros(sum_ref.shape, F32)

        total = cur_ref[...] + jnp.where(pl.program_id(0) < nb - 1, nxt_ref[...], 0.0)
        o_ref[...] = total.astype(o_ref.dtype)
        sum_ref[...] += _col_sum(total)

    return pl.pallas_call(
        body, name="kv_grad_combine", grid=(nb,),
        in_specs=[pl.BlockSpec((WINDOW, w), lambda n: (n, 0)),
                  pl.BlockSpec((WINDOW, w), lambda n: (jnp.minimum(n + 1, nb - 1), 0))],
        out_specs=[pl.BlockSpec((WINDOW, w), lambda n: (n, 0)), pl.BlockSpec((1, w), lambda n: (0, 0))],
        out_shape=[jax.ShapeDtypeStruct((m, w), BF16), jax.ShapeDtypeStruct((1, w), F32)],
        compiler_params=_params(("arbitrary",)),
    )(dkv_cur, dkv_prev)


def _row(v):
    return v.reshape(1, -1)


def _local_step(x, p, target, wget, grad_sink, ln_gain, ln_bias, alb, norm_gain, kv_b, b_q, sinks, b_out, ple_b):
    gs = {}
    gains = [[_row(ln_gain[i, j]) for j in range(3)] for i in range(DEPTH)]
    biases = [[_row(ln_bias[i, j]) for j in range(3)] for i in range(DEPTH)]
    sd = x.shape
    pending = [None]

    def mm(a, b, lb=0, **kw):
        after, pending[0] = pending[0], None
        return _mm(a, b, lb=lb, after=after, **kw)

    def ln_fwd(xin, h, i, j, nm):
        return _rowwise(_ln_fwd_fn, [xin, h], [gains[i][j], biases[i][j]], [(sd, F32), (sd, BF16)], name=nm)

    def tail_fwd(xa, i):
        gu = _mm(xa[1], wget("ffn_w_gate_up", i, xa[1]), lb=0, name=f"ffn_up{i}")
        act, = _rowwise(_swiglu_fwd_fn, [gu], [], [((sd[0], gu.shape[1] // 2), BF16)], name=f"swiglu{i}", tm=128)
        f = _mm(act, wget("ffn_w_down", i, act), lb=0, name=f"ffn_down{i}")
        xb = ln_fwd(xa[0], f, i, 1, f"ln_ffn{i}")
        pg = _mm(xb[1], wget("ple_w_gate", i, act), lb=0, bias=_row(ple_b[i]), name=f"ple_gate{i}")
        pu = _mm(p[i], wget("ple_w_up", i, act), lb=0, name=f"ple_up{i}")
        xc = _rowwise(_ple_ln_fwd_fn, [xb[0], pg, pu], [gains[i][2], biases[i][2]], [(sd, F32), (sd, BF16)],
                      name=f"ln_ple{i}")
        return dict(xa=xa, gu=gu, act=act, f=f, xb=xb, pg=pg, pu=pu), xc

    def tail_bwd(dxc, sv, i):
        xa, xb = sv["xa"], sv["xb"]
        dxb_part, dpg, dpu, dg2, db2, dbg = _rowwise(
            _ple_ln_bwd_fn, [dxc, xb[0], sv["pg"], sv["pu"]], [gains[i][2]],
            [(sd, F32), (sd, BF16), (sd, BF16)], [((1, sd[1]), F32)] * 3, name=f"ln_ple_bwd{i}")
        gs[f"ple_b_{i}"] = dbg
        gs[f"ln_gain_{i}_2"], gs[f"ln_bias_{i}_2"] = dg2, db2
        grad_of("ple_w_gate", i, xb[1], dpg)
        grad_of("ple_w_up", i, p[i], dpu)
        dxb = mm(dpg, wget("ple_w_gate", i, None), tb=True, add=dxb_part, name=f"ple_gate_dx{i}")
        dxa_part, df, dg1, db1, _ = _rowwise(
            _ln_bwd_fn, [dxb, xa[0], sv["f"]], [gains[i][1]],
            [(sd, F32), (sd, BF16)], [((1, sd[1]), F32)] * 3, name=f"ln_ffn_bwd{i}")
        gs[f"ln_gain_{i}_1"], gs[f"ln_bias_{i}_1"] = dg1, db1
        grad_of("ffn_w_down", i, sv["act"], df)
        dact = mm(df, wget("ffn_w_down", i, None), tb=True, name=f"ffn_down_dx{i}")
        dgu, = _rowwise(_swiglu_bwd_fn, [sv["gu"], dact], [], [(sv["gu"].shape, BF16)], name=f"swiglu_bwd{i}", tm=128)
        grad_of("ffn_w_gate_up", i, xa[1], dgu)
        return mm(dgu, wget("ffn_w_gate_up", i, None), tb=True, add=dxa_part, name=f"ffn_up_dx{i}")

    def grad_of(nm, i, act, dout):
        grad = mm(act, dout, lb=None, ta=True, out_dtype=BF16, out_layers=1, out_layer=0, name=f"grad_{nm}{i}")
        token = grad_sink(nm, i, grad)
        if token is not None:
            pending[0] = token

    def mixer_ln_bwd(dxa, xin, h, i):
        dx_part, dh, dg0, db0, dhsum = _rowwise(
            _ln_bwd_fn, [dxa, xin, h], [gains[i][0]],
            [(sd, F32), (sd, BF16)], [((1, sd[1]), F32)] * 3, name=f"ln_mix_bwd{i}")
        gs[f"ln_gain_{i}_0"], gs[f"ln_bias_{i}_0"] = dg0, db0
        return dx_part, dh, dhsum

    proj = _mm(x, wget("a_w_in", 0, None), lb=0, name="hg_proj")
    o_pre, og, states, scores = _hgrn2_fwd(proj, alb, norm_gain, rb=HG_ROWS)
    h0 = _mm(og, wget("a_w_out", 0, og), lb=0, name="hg_out")
    x1 = ln_fwd(x, h0, 0, 0, "ln_mix0")
    sv0, x3 = tail_fwd(x1, 0)
    kv = _mm(x3[1], wget("kv_w", 0, x3[1]), lb=0, bias=_row(kv_b), name="kv_proj")
    q = _mm(x3[1], wget("b_w_q", 0, x3[1]), lb=0, bias=b_q, name="q_proj")
    ao = _swa_fwd(q, kv, sinks)
    h1 = _mm(ao, wget("b_w_out", 0, x3[1]), lb=0, bias=b_out, name="att_out")
    x4 = ln_fwd(x3[0], h1, 1, 0, "ln_mix1")
    sv1, y = tail_fwd(x4, 1)
    dy, loss = _rowwise(_loss_fn, [y[0], target], [], [(sd, F32)], [((1, LANES), F32)], name="loss")

    dx4 = tail_bwd(dy, sv1, 1)
    dx3_part, dh1, dh1sum = mixer_ln_bwd(dx4, x3[0], h1, 1)
    gs["b_out"] = dh1sum
    grad_of("b_w_out", 0, ao, dh1)
    dao = mm(dh1, wget("b_w_out", 0, None), tb=True, name="att_out_dx")
    dq, dkv_cur, dkv_prev, dqsum, dsinks = _swa_bwd(q, kv, sinks, dao)
    gs["b_q"], gs["sinks"] = dqsum, dsinks
    dkv, dkvsum = _kv_grad_combine(dkv_cur, dkv_prev)
    gs["kv_b"] = dkvsum
    grad_of("b_w_q", 0, x3[1], dq)
    grad_of("kv_w", 0, x3[1], dkv)
    dx3 = mm(dq, wget("b_w_q", 0, None), tb=True, add=dx3_part, name="q_proj_dx")
    dx3 = mm(dkv, wget("kv_w", 0, None), tb=True, add=dx3, name="kv_proj_dx")
    dx1 = tail_bwd(dx3, sv0, 0)
    dx_part, dh0, _ = mixer_ln_bwd(dx1, x, h0, 0)
    grad_of("a_w_out", 0, og, dh0)
    dog = mm(dh0, wget("a_w_out", 0, None), tb=True, name="hg_out_dx")
    dqr, dfr, dvr, dgr, dalb, dgain = _hgrn2_bwd(proj, o_pre, states, scores, dog, alb, norm_gain, rb=HG_ROWS)
    gs["alb"], gs["norm_gain"] = dalb, dgain
    dproj = jnp.concatenate([dqr, dfr, dvr, dgr], axis=1)
    grad_of("a_w_in", 0, x, dproj)
    grad_x = mm(dproj, wget("a_w_in", 0, None), tb=True, add=dx_part, name="hg_proj_dx")
    return loss, grad_x, gs


HBM_SPEC = pl.BlockSpec(memory_space=pl.ANY)
HBM_ONLY = pl.BlockSpec(memory_space=pltpu.HBM)
SEM_SPEC = pl.BlockSpec(memory_space=pltpu.SEMAPHORE)
SIDE_EFFECT = pltpu.SideEffectType.DATAFLOW_SIDE_EFFECTING


def _piece(ref, kind, j):
    _, r, c = ref.shape
    if kind == "row":
        return ref.at[:, pl.ds(j * (r // N_CHIPS), r // N_CHIPS), :]
    return ref.at[:, :, pl.ds(j * (c // N_CHIPS), c // N_CHIPS)]


def _chip_of(j, c):
    return (j // 2, j % 2, c)


def _in_hbm(a):
    return pltpu.with_memory_space_constraint(a, pltpu.HBM)


def _place(src, kind, chip, *, mode, name, out_dtype, zone=None, zone_shape=None, layer=0, after=None):
    if mode == "gather":
        _, r, c = src.shape
        out_shape = (1, r * N_CHIPS, c) if kind == "row" else (1, r, c * N_CHIPS)
    else:
        out_shape = tuple(zone.shape) if zone is not None else tuple(zone_shape)
        r, c = out_shape[-2:]
    tm = _pick_rows(r, 512)
    nb = r // tm

    def full_idx(i, chip_ref):
        return (0, chip_ref[0] * nb + i, 0) if kind == "row" else (0, i, chip_ref[0])

    if mode == "gather":
        in_spec = pl.BlockSpec((None, tm, c), lambda i, chip_ref: (0, i, 0))
        out_spec = pl.BlockSpec((None, tm, c), full_idx)
    else:
        in_spec = pl.BlockSpec((None, tm, c), full_idx)
        out_spec = pl.BlockSpec((None, None, tm, c), lambda i, chip_ref: (chip_ref[0], layer, i, 0))
    in_specs, operands, aliases = [in_spec], [src], {}
    if zone is not None:
        in_specs.append(HBM_SPEC)
        operands.append(zone)
        aliases = {2: 0}
    if after is not None:
        in_specs.append(HBM_SPEC)
        operands.append(after)

    def body(chip_ref, src_ref, *rest):
        rest[-1][...] = src_ref[...].astype(rest[-1].dtype)

    return pl.pallas_call(
        body, name=name,
        grid_spec=pltpu.PrefetchScalarGridSpec(num_scalar_prefetch=1, grid=(nb,), in_specs=in_specs,
                                               out_specs=out_spec),
        out_shape=jax.ShapeDtypeStruct(out_shape, out_dtype),
        input_output_aliases=aliases,
        compiler_params=_params(("arbitrary",)),
    )(chip, *operands)


class _Exchange:
    def __init__(self, mode, srcs, lands, kinds, layers, name, after=None):
        self.mode, self.kinds, self.layers, self.name, self.n = mode, kinds, layers, name, len(lands)
        n, ns = self.n, len(srcs)
        n_in = ns + n + (after is not None)
        sem_shape = pltpu.SemaphoreType.DMA((n * N_CHIPS,))

        def body(*refs):
            src_refs, land_refs = refs[:ns], refs[ns:ns + n]
            send_sems, recv_sems = refs[n_in], refs[n_in + 1]
            token = refs[-1]
            c = lax.axis_index("c")
            me = 2 * lax.axis_index("x") + lax.axis_index("y")
            for j in range(N_CHIPS):
                @pl.when(me == j)
                def _():
                    for a in range(n):
                        for t in range(N_CHIPS):
                            if t != j:
                                src, dst = self._ends(src_refs, land_refs, a, j, t)
                                pltpu.make_async_remote_copy(
                                    src_ref=src, dst_ref=dst, send_sem=send_sems.at[a * N_CHIPS + t],
                                    recv_sem=recv_sems.at[a * N_CHIPS + j],
                                    device_id=_chip_of(t, c), device_id_type=MESH).start()
            token[...] = jnp.zeros(token.shape, token.dtype)

        arrays = list(srcs) + list(lands)
        operands = [_in_hbm(a) for a in arrays]
        in_specs = [HBM_ONLY] * (ns + n)
        if after is not None:
            operands.append(after)
            in_specs.append(HBM_SPEC)
        outs = pl.pallas_call(
            body, name=name + "_start",
            in_specs=in_specs,
            out_specs=[SEM_SPEC, SEM_SPEC] + [HBM_ONLY] * (ns + n) + [pl.BlockSpec(memory_space=pltpu.VMEM)],
            out_shape=[sem_shape, sem_shape] + [pltpu.HBM(a.shape, a.dtype) for a in arrays]
                      + [jax.ShapeDtypeStruct((8, LANES), F32)],
            input_output_aliases={i: i + 2 for i in range(ns + n)},
            compiler_params=pltpu.CompilerParams(has_side_effects=SIDE_EFFECT),
        )(*operands)
        self.send_sems, self.recv_sems = outs[0], outs[1]
        self.srcs, self.lands = list(outs[2:2 + ns]), list(outs[2 + ns:2 + ns + n])
        self.token = outs[-1]

    def _ends(self, src_refs, land_refs, a, me_j, peer):
        if self.mode == "gather":
            mine = _piece(land_refs[a], self.kinds[a], me_j)
            return mine, mine
        return _piece(src_refs[a], self.kinds[a], peer), land_refs[a].at[me_j, pl.ds(self.layers[a], 1)]

    def wait(self, after, lands=None):
        n, ns = self.n, len(self.srcs)
        lands = self.lands if lands is None else lands

        def body(*refs):
            src_refs, land_refs = refs[:ns], refs[ns:ns + n]
            send_sems, recv_sems = refs[ns + n], refs[ns + n + 1]
            c = lax.axis_index("c")
            me = 2 * lax.axis_index("x") + lax.axis_index("y")
            for j in range(N_CHIPS):
                @pl.when(me != j)
                def _():
                    for a in range(n):
                        sent, _ = self._ends(src_refs, land_refs, a, 0, j)
                        _, landed = self._ends(src_refs, land_refs, a, j, 0)
                        cp = pltpu.make_async_remote_copy(
                            src_ref=sent, dst_ref=landed, send_sem=send_sems.at[a * N_CHIPS + j],
                            recv_sem=recv_sems.at[a * N_CHIPS + j],
                            device_id=_chip_of(j, c), device_id_type=MESH)
                        cp.wait_send()
                        cp.wait_recv()

        arrays = self.srcs + list(lands)
        operands = [_in_hbm(a) for a in arrays] + [self.send_sems, self.recv_sems]
        in_specs = [HBM_ONLY] * (ns + n) + [SEM_SPEC, SEM_SPEC]
        if after is not None:
            operands.append(after)
            in_specs.append(HBM_SPEC)
        outs = pl.pallas_call(
            body, name=self.name + "_wait",
            in_specs=in_specs, out_specs=[HBM_ONLY] * (ns + n),
            out_shape=[pltpu.HBM(a.shape, a.dtype) for a in arrays],
            input_output_aliases={i: i for i in range(ns + n)},
            compiler_params=pltpu.CompilerParams(has_side_effects=SIDE_EFFECT),
        )(*operands)
        return list(outs[ns:])


def _sibling_swap(arrays, name):
    n = len(arrays)

    def body(*refs):
        ins, outs = refs[:n], refs[n:2 * n]
        send_sems, recv_sems = refs[2 * n:]
        sibling = (lax.axis_index("x"), lax.axis_index("y"), 1 - lax.axis_index("c"))
        copies = [pltpu.make_async_remote_copy(src_ref=ins[a], dst_ref=outs[a], send_sem=send_sems.at[a],
                                               recv_sem=recv_sems.at[a], device_id=sibling, device_id_type=MESH)
                  for a in range(n)]
        for cp in copies:
            cp.start()
        for cp in copies:
            cp.wait()

    return pl.pallas_call(
        body, name=name,
        in_specs=[HBM_SPEC] * n, out_specs=[HBM_SPEC] * n,
        out_shape=[jax.ShapeDtypeStruct(a.shape, a.dtype) for a in arrays],
        scratch_shapes=[pltpu.SemaphoreType.DMA((n,)), pltpu.SemaphoreType.DMA((n,))],
    )(*arrays)


def _gather_devices(vec):
    def body(in_ref, out_ref, send_sems, recv_sems, local_sem):
        x, y, c = lax.axis_index("x"), lax.axis_index("y"), lax.axis_index("c")
        me = 4 * x + 2 * y + c
        mine = pltpu.make_async_copy(in_ref, out_ref.at[me], local_sem)
        mine.start()
        copies = []
        for rel in range(1, N_DEV):
            peer = (x ^ (rel >> 2), y ^ ((rel >> 1) & 1), c ^ (rel & 1))
            copies.append(pltpu.make_async_remote_copy(
                src_ref=in_ref, dst_ref=out_ref.at[me], send_sem=send_sems.at[rel], recv_sem=recv_sems.at[rel],
                device_id=peer, device_id_type=MESH))
        for cp in copies:
            cp.start()
        for cp in copies:
            cp.wait()
        mine.wait()

    return pl.pallas_call(
        body, name="gather_small",
        in_specs=[HBM_SPEC], out_specs=HBM_SPEC,
        out_shape=jax.ShapeDtypeStruct((N_DEV,) + vec.shape, vec.dtype),
        scratch_shapes=[pltpu.SemaphoreType.DMA((N_DEV,)), pltpu.SemaphoreType.DMA((N_DEV,)),
                        pltpu.SemaphoreType.DMA],
    )(vec)


BIG = [("a_w_in", "col"), ("a_w_out", "row"), ("kv_w", "row"), ("b_w_q", "row"), ("b_w_out", "row"),
       ("ffn_w_gate_up", "col"), ("ffn_w_down", "row"), ("ple_w_up", "col"), ("ple_w_gate", "row")]
GATHER_GROUPS = [[("a_w_in", 0), ("small", 0)], [("a_w_out", 0), ("ffn_w_gate_up", 0)],
                 [("ffn_w_down", 0), ("ple_w_gate", 0), ("ple_w_up", 0)], [("kv_w", 0), ("b_w_q", 0), ("b_w_out", 0)],
                 [("ffn_w_gate_up", 1)], [("ffn_w_down", 1), ("ple_w_gate", 1), ("ple_w_up", 1)]]
SCATTER_GROUPS = [[("ple_w_gate", 1), ("ple_w_up", 1), ("ffn_w_down", 1)], [("ffn_w_gate_up", 1)],
                  [("b_w_out", 0), ("b_w_q", 0), ("kv_w", 0)], [("ple_w_gate", 0), ("ple_w_up", 0), ("ffn_w_down", 0)],
                  [("ffn_w_gate_up", 0), ("a_w_out", 0)], [("a_w_in", 0)]]
SMALL_SHARDED = ["ln_gain", "ln_bias", "a_lower_bound"]
SMALL_REPLICATED = ["a_norm_gain", "kv_b", "b_b_q", "b_sinks", "b_b_out", "ple_b_gate"]
WEIGHT_ORDER = ["a_w_in", "a_lower_bound", "a_norm_gain", "a_w_out", "kv_w", "kv_b", "b_w_q", "b_b_q", "b_sinks",
                "b_w_out", "b_b_out", "ffn_w_gate_up", "ffn_w_down", "ple_w_up", "ple_w_gate", "ple_b_gate",
                "ln_gain", "ln_bias"]


def _as3(a):
    return a.reshape((-1,) + a.shape[-2:]) if a.ndim >= 3 else a.reshape((1,) + a.shape)


def _pad_lanes(v):
    n = v.shape[-1]
    return jnp.pad(v, ((0, 0), (0, (-n) % LANES)))


def _adam_small_fn(w, mom, vel, g):
    return _adam_fn(w, mom, vel, g, jnp.zeros_like(g))[1:]


def _sum_rows_fn(slots):
    acc = slots[0]
    for s in range(1, slots.shape[0]):
        acc = acc + slots[s]
    return (acc,)


def kernel(x, p, a_w_in, a_lower_bound, a_norm_gain, a_w_out, kv_w, kv_b, b_w_q, b_b_q, b_sinks, b_w_out, b_b_out, ffn_w_gate_up, ffn_w_down, ple_w_up, ple_w_gate, ple_b_gate, ln_gain, ln_bias, loss_target, m_a_w_in, m_a_lower_bound, m_a_norm_gain, m_a_w_out, m_kv_w, m_kv_b, m_b_w_q, m_b_b_q, m_b_sinks, m_b_w_out, m_b_b_out, m_ffn_w_gate_up, m_ffn_w_down, m_ple_w_up, m_ple_w_gate, m_ple_b_gate, m_ln_gain, m_ln_bias, v_a_w_in, v_a_lower_bound, v_a_norm_gain, v_a_w_out, v_kv_w, v_kv_b, v_b_w_q, v_b_b_q, v_b_sinks, v_b_w_out, v_b_b_out, v_ffn_w_gate_up, v_ffn_w_down, v_ple_w_up, v_ple_w_gate, v_ple_b_gate, v_ln_gain, v_ln_bias):
    args = dict(locals())
    wts = {n: args[n] for n in WEIGHT_ORDER}
    mom = {n: args["m_" + n] for n in WEIGHT_ORDER}
    vel = {n: args["v_" + n] for n in WEIGHT_ORDER}
    chip = 2 * lax.axis_index("x") + lax.axis_index("y")
    d = x.shape[-1]
    dq = d // N_CHIPS

    kind_of = dict(BIG)
    kind_of["small"] = "col"
    chip_arr = chip.reshape(1).astype(jnp.int32)
    small_pack = jnp.concatenate([wts[n].reshape(-1, dq) for n in SMALL_SHARDED], axis=0)[None]

    def place(key, after):
        n, layer = key
        if n == "small":
            return _place(small_pack, "col", chip_arr, mode="gather", name="place_small", out_dtype=F32, after=after)
        return _place(_as3(wts[n])[layer:layer + 1], kind_of[n], chip_arr, mode="gather",
                      name=f"place_{n}{layer}", out_dtype=BF16, after=after)

    gathers, where = [], {}
    for gi, group in enumerate(GATHER_GROUPS):
        prev = gathers[-1].token if gathers else None
        gathers.append(_Exchange("gather", [], [place(k, prev) for k in group], [kind_of[k[0]] for k in group],
                                 [0] * len(group), f"gather{gi}", after=prev))
        for k in group:
            where[k] = gi
    all_started = gathers[-1].token
    ready = {}

    def wget(name, layer, after):
        key = (name, layer)
        if key not in ready:
            gi = where[key]
            outs = gathers[gi].wait(all_started if gi == 0 else after)
            for k, arr in zip(GATHER_GROUPS[gi], outs):
                ready[k] = arr
        return ready[key]

    small_full = wget("small", 0, None)[0]
    ln_gain_f = small_full[0:6].reshape(DEPTH, 3, d)
    ln_bias_f = small_full[6:12].reshape(DEPTH, 3, d)
    alb_f = small_full[12:14]

    group_of = {k: gi for gi, group in enumerate(SCATTER_GROUPS) for k in group}
    grads_done, zones, scatters = {}, {}, []

    def grad_sink(name, layer, grad):
        grads_done[(name, layer)] = grad
        zones[name] = _place(grad, kind_of[name], chip_arr, mode="scatter", name=f"place_grad_{name}{layer}",
                             out_dtype=BF16, zone=zones.get(name), zone_shape=(N_CHIPS,) + _as3(wts[name]).shape,
                             layer=layer)
        gi = group_of[(name, layer)]
        group = SCATTER_GROUPS[gi]
        if not all(k in grads_done for k in group):
            return None
        ex = _Exchange("scatter", [grads_done[k] for k in group], [zones[k[0]] for k in group],
                       [kind_of[k[0]] for k in group], [k[1] for k in group], f"scatter{gi}")
        for k, zone in zip(group, ex.lands):
            zones[k[0]] = zone
        scatters.append((ex, group))
        return ex.token

    loss, grad_x, gs = _local_step(
        x[0], p[:, 0], loss_target[0], wget, grad_sink, ln_gain_f, ln_bias_f, alb_f, a_norm_gain, kv_b, b_b_q,
        b_sinks, b_b_out, ple_b_gate)

    res = {}

    def arrive(batch, after):
        for ex, group in batch:
            outs = ex.wait(after, lands=[zones[k[0]] for k in group])
            for k, zone in zip(group, outs):
                zones[k[0]] = zone

    def update(names, tag):
        partial = []
        for n in names:
            s2 = zones[n].reshape(N_CHIPS, -1, zones[n].shape[-1])
            partial.append(_rowwise(_sum_slots_fn, [s2], [], [(s2.shape[1:], F32)], name=f"sum_{n}")[0])
        sibling = _sibling_swap(partial, tag)
        for n, own, sib in zip(names, partial, sibling):
            shp = wts[n].shape
            flat = lambda a: a.reshape(-1, shp[-1])
            out = _rowwise(_adam_fn, [flat(wts[n]), flat(mom[n]), flat(vel[n]), own, sib], [],
                           [(own.shape, F32)] * 4, name=f"adam_{n}")
            res[n] = [o.reshape(shp) for o in out]
        return res[names[-1]][1]

    last_names = [k[0] for k in SCATTER_GROUPS[-1]]
    arrive(scatters[:-1], grad_x)
    updated = update([n for n, _ in BIG if n not in last_names], "sibling_swap")
    arrive(scatters[-1:], updated)
    update(last_names, "sibling_swap_last")

    ln_g = jnp.concatenate([gs[f"ln_gain_{i}_{j}"] for i in range(DEPTH) for j in range(3)], axis=0)
    ln_b = jnp.concatenate([gs[f"ln_bias_{i}_{j}"] for i in range(DEPTH) for j in range(3)], axis=0)
    ple_bg = jnp.concatenate([gs[f"ple_b_{i}"] for i in range(DEPTH)], axis=0)
    small_list = [ln_g.reshape(1, -1), ln_b.reshape(1, -1), gs["alb"].reshape(1, -1), gs["norm_gain"],
                  gs["kv_b"], gs["b_q"], _pad_lanes(gs["sinks"]), gs["b_out"], ple_bg.reshape(1, -1), loss]
    small_vec = jnp.concatenate(small_list, axis=1)
    everyone = _gather_devices(small_vec)
    total, = _rowwise(_sum_rows_fn, [everyone], [], [(small_vec.shape, F32)], name="sum_small")
    offs, pos = [], 0
    for v in small_list:
        offs.append((pos, v.shape[1]))
        pos += v.shape[1]

    def seg(k):
        return total[0, offs[k][0]:offs[k][0] + offs[k][1]]

    def my_cols(full, rows):
        return lax.dynamic_slice_in_dim(full.reshape(rows, N_CHIPS, dq), chip, 1, axis=1).reshape(rows, dq)

    n_sink = b_sinks.shape[-1]
    small_grads = {
        "ln_gain": my_cols(seg(0), 6).reshape(ln_gain.shape), "ln_bias": my_cols(seg(1), 6).reshape(ln_bias.shape),
        "a_lower_bound": my_cols(seg(2), 2), "a_norm_gain": seg(3).reshape(a_norm_gain.shape),
        "kv_b": seg(4).reshape(kv_b.shape), "b_b_q": seg(5).reshape(b_b_q.shape),
        "b_sinks": seg(6)[:n_sink].reshape(b_sinks.shape), "b_b_out": seg(7).reshape(b_b_out.shape),
        "ple_b_gate": seg(8).reshape(ple_b_gate.shape)}
    names = SMALL_SHARDED + SMALL_REPLICATED
    pack = lambda dct: _pad_lanes(jnp.concatenate([dct[n].reshape(1, -1) for n in names], axis=1))
    g_pack = pack(small_grads)
    upd = _rowwise(_adam_small_fn, [pack(wts), pack(mom), pack(vel), g_pack], [], [(g_pack.shape, F32)] * 3,
                   name="adam_small")
    pos = 0
    for n in names:
        size = wts[n].size
        res[n] = [small_grads[n]] + [u[0, pos:pos + size].reshape(wts[n].shape) for u in upd]
        pos += size

    outs = [seg(9)[0], grad_x[None]]
    for k in range(4):
        outs += [res[n][k] for n in WEIGHT_ORDER]
    return tuple(outs)
```

```python
import functools

import jax
import jax.numpy as jnp
from jax import lax
from jax.experimental import pallas as pl
from jax.experimental.pallas import tpu as pltpu

F32 = jnp.float32
BF16 = jnp.bfloat16
MESH = pl.DeviceIdType.MESH

LANES = 128
HG_DK = 128
HG_CHUNK = 64
HG_SUB = 16
HG_ROWS = 512
ATT_HD = 64
ATT_G = 4
WINDOW = 128
DEPTH = 2
ALPHA = (2.0 * DEPTH) ** 0.25
LN_EPS = 1e-5
RMS_EPS = 1e-6
ADAM_LR, ADAM_B1, ADAM_B2, ADAM_EPS, ADAM_WD, ADAM_STEP = 0.001, 0.9, 0.999, 1e-08, 0.01, 10
N_CHIPS = 4
N_DEV = 8
VMEM_LIMIT = 56 * 1024 * 1024
NEG = -1e30


def _pick(n, cap):
    best = None
    for d in range(LANES, min(n, cap) + 1, LANES):
        if n % d == 0:
            best = d
    return n if best is None else best


def _pick_rows(m, cap):
    best = None
    for d in range(16, min(m, cap) + 1, 16):
        if m % d == 0:
            best = d
    return m if best is None else best


MIN_GRID_STEPS = 8


def _tiles(m, n, k, caps):
    tm, tn, tk = _pick(m, caps[0]), _pick(n, caps[1]), _pick(k, caps[2])

    def halves(t, floor):
        return t // 2 >= floor and (t // 2) % LANES == 0

    while (m // tm) * (n // tn) * (k // tk) < MIN_GRID_STEPS:
        if tm >= tn and halves(tm, 256):
            tm //= 2
        elif halves(tn, 256):
            tn //= 2
        elif halves(tm, 256):
            tm //= 2
        elif halves(tk, 512):
            tk //= 2
        else:
            break
    return tm, tn, tk


def _params(sem):
    return pltpu.CompilerParams(dimension_semantics=sem, vmem_limit_bytes=VMEM_LIMIT)


def _mm(a, b, *, name, la=None, lb=None, ta=False, tb=False, bias=None, add=None, out_dtype=F32,
        out_layers=None, out_layer=None, after=None, caps=(1024, 1536, 2048)):
    ar, ac = a.shape[-2:]
    br, bc = b.shape[-2:]
    m, k = (ac, ar) if ta else (ar, ac)
    k2, n = (bc, br) if tb else (br, bc)
    assert k == k2, (a.shape, b.shape, ta, tb)
    tm, tn, tk = _tiles(m, n, k, caps)
    nk = k // tk
    grid = (m // tm, n // tn, nk)

    def spec(block, idx, layer):
        if layer is None:
            return pl.BlockSpec(block, idx)
        return pl.BlockSpec((None,) + block, lambda i, j, kk: (layer,) + idx(i, j, kk))

    a_spec = spec((tk, tm), lambda i, j, kk: (kk, i), la) if ta else spec((tm, tk), lambda i, j, kk: (i, kk), la)
    b_spec = spec((tn, tk), lambda i, j, kk: (j, kk), lb) if tb else spec((tk, tn), lambda i, j, kk: (kk, j), lb)
    in_specs, operands = [a_spec, b_spec], [a, b]
    if bias is not None:
        in_specs.append(pl.BlockSpec((1, tn), lambda i, j, kk: (0, j)))
        operands.append(bias)
    if add is not None:
        in_specs.append(pl.BlockSpec((tm, tn), lambda i, j, kk: (i, j)))
        operands.append(add)
    if after is not None:
        in_specs.append(pl.BlockSpec(memory_space=pl.ANY))
        operands.append(after)
    if out_layers is None:
        out_shape = jax.ShapeDtypeStruct((m, n), out_dtype)
    else:
        out_shape = jax.ShapeDtypeStruct((out_layers, m, n), out_dtype)
    out_spec = spec((tm, tn), lambda i, j, kk: (i, j), out_layer)
    dims = (((0 if ta else 1,), (1 if tb else 0,)), ((), ()))
    has_bias, has_add, has_alias = bias is not None, add is not None, after is not None

    def body(*refs):
        a_ref, b_ref = refs[0], refs[1]
        pos = 2
        bias_ref = add_ref = None
        if has_bias:
            bias_ref = refs[pos]
            pos += 1
        if has_add:
            add_ref = refs[pos]
            pos += 1
        if has_alias:
            pos += 1
        o_ref = refs[pos]
        acc_ref = refs[pos + 1] if nk > 1 else None
        part = lax.dot_general(a_ref[...].astype(BF16), b_ref[...].astype(BF16), dims, preferred_element_type=F32)

        def finish(total):
            if has_bias:
                total = total + bias_ref[...]
            if has_add:
                total = total + add_ref[...]
            o_ref[...] = total.astype(o_ref.dtype)

        if nk == 1:
            finish(part)
        else:
            kk = pl.program_id(2)

            @pl.when(kk == 0)
            def _():
                acc_ref[...] = part

            @pl.when(kk > 0)
            def _():
                acc_ref[...] += part

            @pl.when(kk == nk - 1)
            def _():
                finish(acc_ref[...])

    return pl.pallas_call(
        body, name=name, grid=grid, in_specs=in_specs, out_specs=out_spec, out_shape=out_shape,
        scratch_shapes=[pltpu.VMEM((tm, tn), F32)] if nk > 1 else [],
        compiler_params=_params(("parallel", "parallel", "arbitrary")),
    )(*operands)


def _rowwise(fn, rows, whole, outs, sums=(), *, name, tm=256):
    m = rows[0].shape[-2]
    tm = _pick_rows(m, tm)
    n_rows, n_whole, n_outs, n_sums = len(rows), len(whole), len(outs), len(sums)

    def rspec(shape):
        lead = len(shape) - 2
        return pl.BlockSpec(tuple(shape[:-2]) + (tm, shape[-1]), lambda i: (0,) * lead + (i, 0))

    def wspec(shape):
        return pl.BlockSpec(tuple(shape), lambda i: (0,) * len(shape))

    def body(*refs):
        vals = [r[...] for r in refs[:n_rows + n_whole]]
        out_refs = refs[n_rows + n_whole:n_rows + n_whole + n_outs]
        sum_refs = refs[n_rows + n_whole + n_outs:]
        res = fn(*vals)
        for ref, val in zip(out_refs, res[:n_outs]):
            ref[...] = val.astype(ref.dtype)
        if n_sums:
            @pl.when(pl.program_id(0) == 0)
            def _():
                for ref in sum_refs:
                    ref[...] = jnp.zeros(ref.shape, ref.dtype)

            for ref, val in zip(sum_refs, res[n_outs:]):
                ref[...] += val

    result = pl.pallas_call(
        body, name=name, grid=(m // tm,),
        in_specs=[rspec(r.shape) for r in rows] + [wspec(w.shape) for w in whole],
        out_specs=[rspec(s) for s, _ in outs] + [wspec(s) for s, _ in sums],
        out_shape=[jax.ShapeDtypeStruct(s, d) for s, d in list(outs) + list(sums)],
        compiler_params=_params(("arbitrary",)),
    )(*rows, *whole)
    return result


def _sigmoid(v):
    return jax.nn.sigmoid(v)


def _col_sum(v):
    return jnp.sum(v, axis=0, keepdims=True)


def _ln_stats(z):
    mu = jnp.mean(z, axis=-1, keepdims=True)
    zc = z - mu
    var = jnp.mean(zc * zc, axis=-1, keepdims=True)
    rstd = lax.rsqrt(var + LN_EPS)
    return zc * rstd, rstd


def _ln_fwd_fn(xin, h, gain, bias):
    xhat, _ = _ln_stats(ALPHA * xin + h)
    y = xhat * gain + bias
    return y, y


def _ple_ln_fwd_fn(xin, pg, pu, gain, bias):
    xhat, _ = _ln_stats(ALPHA * xin + _sigmoid(pg) * pu)
    y = xhat * gain + bias
    return y, y


def _ln_dz(dy, z, gain):
    xhat, rstd = _ln_stats(z)
    dxhat = dy * gain
    dz = rstd * (dxhat - jnp.mean(dxhat, axis=-1, keepdims=True)
                 - xhat * jnp.mean(dxhat * xhat, axis=-1, keepdims=True))
    return dz, _col_sum(dy * xhat), _col_sum(dy)


def _ln_bwd_fn(dy, xin, h, gain):
    dz, dgain, dbias = _ln_dz(dy, ALPHA * xin + h, gain)
    return ALPHA * dz, dz, dgain, dbias, _col_sum(dz)


def _ple_ln_bwd_fn(dy, xin, pg, pu, gain):
    sg = _sigmoid(pg)
    dz, dgain, dbias = _ln_dz(dy, ALPHA * xin + sg * pu, gain)
    dpg = dz * pu * sg * (1.0 - sg)
    return ALPHA * dz, dpg, dz * sg, dgain, dbias, _col_sum(dpg)


def _swiglu_fwd_fn(gu):
    hid = gu.shape[-1] // 2
    gate, up = gu[:, :hid], gu[:, hid:]
    return (gate * _sigmoid(gate) * up,)


def _swiglu_bwd_fn(gu, dact):
    hid = gu.shape[-1] // 2
    gate, up = gu[:, :hid], gu[:, hid:]
    sg = _sigmoid(gate)
    dgate = dact * up * sg * (1.0 + gate * (1.0 - sg))
    dup = dact * gate * sg
    return (jnp.concatenate([dgate, dup], axis=-1),)


def _loss_fn(y, target):
    err = y - target
    inv = 1.0 / y.shape[-1]
    part = 0.5 * inv * jnp.sum(jnp.sum(err * err, axis=-1, keepdims=True), axis=0, keepdims=True)
    return err * inv, jnp.broadcast_to(part, (1, LANES))


def _adam_fn(w, mom, vel, p_own, p_sib):
    g = p_own.astype(F32) + p_sib.astype(F32)
    m_new = ADAM_B1 * mom + (1.0 - ADAM_B1) * g
    v_new = ADAM_B2 * vel + (1.0 - ADAM_B2) * (g * g)
    m_hat = m_new / (1.0 - ADAM_B1 ** ADAM_STEP)
    v_hat = v_new / (1.0 - ADAM_B2 ** ADAM_STEP)
    delta = -ADAM_LR * (m_hat / (jnp.sqrt(v_hat) + ADAM_EPS) + ADAM_WD * w)
    return g, delta, m_new, v_new


def _sum_slots_fn(slots):
    acc = slots[0].astype(F32)
    for s in range(1, slots.shape[0]):
        acc = acc + slots[s].astype(F32)
    return (acc,)


def _split2(x):
    hi = x.astype(BF16)
    return hi, (x - hi.astype(F32)).astype(BF16)


def _dot3(a, b, dims):
    a_hi, a_lo = _split2(a)
    b_hi, b_lo = _split2(b)
    dn = (dims, ((), ()))
    return (lax.dot_general(a_hi, b_hi, dn, preferred_element_type=F32)
            + (lax.dot_general(a_hi, b_lo, dn, preferred_element_type=F32)
               + lax.dot_general(a_lo, b_hi, dn, preferred_element_type=F32)))


def _tdot(mask01, b):
    m = mask01.astype(BF16)
    b_hi = b.astype(BF16)
    rest = b - b_hi.astype(F32)
    b_mid = rest.astype(BF16)
    b_lo = (rest - b_mid.astype(F32)).astype(BF16)
    dn = (((1,), (0,)), ((), ()))
    return (lax.dot_general(m, b_hi, dn, preferred_element_type=F32)
            + (lax.dot_general(m, b_mid, dn, preferred_element_type=F32)
               + lax.dot_general(m, b_lo, dn, preferred_element_type=F32)))


def _hdot(a, b):
    return _dot3(a, b, ((1,), (0,)))


def _hdot_nt(a, b):
    return _dot3(a, b, ((1,), (1,)))


def _hdot_tn(a, b):
    return _dot3(a, b, ((0,), (0,)))


def _dot(a, b):
    return lax.dot_general(a.astype(BF16), b.astype(BF16), (((1,), (0,)), ((), ())), preferred_element_type=F32)


def _dot_nt(a, b):
    return lax.dot_general(a.astype(BF16), b.astype(BF16), (((1,), (1,)), ((), ())), preferred_element_type=F32)


def _dot_tn(a, b):
    return lax.dot_general(a.astype(BF16), b.astype(BF16), (((0,), (0,)), ((), ())), preferred_element_type=F32)


def _hg_masks():
    c = HG_CHUNK
    row = lax.broadcasted_iota(jnp.int32, (c, c), 0)
    col = lax.broadcasted_iota(jnp.int32, (c, c), 1)
    base = row & (-HG_SUB)
    return row, col, base, col <= row, col < base


def _hg_gates(qr, fr, alb):
    lbound = _sigmoid(alb[0:1, :] - alb[1:2, :])
    sig = _sigmoid(fr)
    forget = lbound + (1.0 - lbound) * sig
    kk = (1.0 - lbound) * _sigmoid(-fr)
    qt = qr * _sigmoid(qr) * (HG_DK ** -0.5)
    return qt, kk, jnp.log(forget), lbound, sig, forget


def _hg_scores(qt, kk, g, scores=True):
    c, nsub = HG_CHUNK, HG_CHUNK // HG_SUB
    row, col, base, causal, below = _hg_masks()
    b = _tdot(causal, g)
    rr = _tdot(below, g)
    bq = b - rr
    qh = qt * jnp.exp(bq)
    edecs = [None]
    parts = [jnp.zeros((HG_SUB, c), F32)]
    for i in range(1, nsub):
        edec = jnp.exp(jnp.minimum(rr[i * HG_SUB:i * HG_SUB + 1, :] - b, 0.0))
        edecs.append(edec)
        if scores:
            parts.append(_dot_nt(qh[i * HG_SUB:(i + 1) * HG_SUB, :], kk * edec))
    b3 = b.reshape(nsub, HG_SUB, HG_DK)
    q3 = qt.reshape(nsub, HG_SUB, HG_DK)
    k3 = kk.reshape(nsub, HG_SUB, HG_DK)
    if not scores:
        return None, b, bq, qh, edecs, (b3, q3, k3)
    a = jnp.where(below, jnp.concatenate(parts, axis=0), 0.0)
    for j in range(HG_SUB):
        e = jnp.exp(jnp.minimum(b3 - b3[:, j:j + 1, :], 0.0))
        colv = jnp.sum(q3 * e * k3[:, j:j + 1, :], axis=-1, keepdims=True).reshape(c, 1)
        a = jnp.where(col == base + j, colv, a)
    a = jnp.where(causal, a, 0.0)
    return a, b, bq, qh, edecs, (b3, q3, k3)


def _hg_norm(o, gr, gain):
    r = lax.rsqrt(jnp.mean(o * o, axis=-1, keepdims=True) + RMS_EPS)
    sg = _sigmoid(gr)
    return o * r * gain, r, sg


def _hgrn2_fwd(proj, alb, gain, *, rb):
    m, d4 = proj.shape
    d = d4 // 4
    heads = d // HG_DK
    rb = min(rb, m)
    cpb = rb // HG_CHUNK
    nrb = m // rb

    def body(q_ref, f_ref, v_ref, g_ref, alb_ref, gain_ref, o_ref, og_ref, st_ref, a_ref, state):
        @pl.when(pl.program_id(1) == 0)
        def _():
            state[...] = jnp.zeros(state.shape, F32)

        def chunk(ci, carry):
            sl = pl.ds(pl.multiple_of(ci * HG_CHUNK, HG_CHUNK), HG_CHUNK)
            qt, kk, g, _, _, _ = _hg_gates(q_ref[sl, :], f_ref[sl, :], alb_ref[...])
            v = v_ref[sl, :]
            st = state[...]
            st_ref[ci] = st
            a, b, _, _, _, _ = _hg_scores(qt, kk, g)
            a_ref[ci] = a.astype(a_ref.dtype)
            o = _dot(a, v) + _dot_nt(qt * jnp.exp(b), st)
            b_last = b[HG_CHUNK - 1:HG_CHUNK, :]
            state[...] = st * jnp.exp(b_last) + _hdot_tn(v, kk * jnp.exp(b_last - b))
            o_ref[sl, :] = o
            n, _, sg = _hg_norm(o, g_ref[sl, :], gain_ref[...])
            og_ref[sl, :] = (n * g_ref[sl, :] * sg).astype(og_ref.dtype)
            return carry

        lax.fori_loop(0, cpb, chunk, 0, unroll=2)

    def col(cidx):
        return pl.BlockSpec((rb, HG_DK), lambda h, r: (r, cidx * heads + h))

    return pl.pallas_call(
        body, name="hgrn2_fwd", grid=(heads, nrb),
        in_specs=[col(0), col(1), col(2), col(3),
                  pl.BlockSpec((2, HG_DK), lambda h, r: (0, h)),
                  pl.BlockSpec((1, HG_DK), lambda h, r: (0, 0))],
        out_specs=[pl.BlockSpec((rb, HG_DK), lambda h, r: (r, h)),
                   pl.BlockSpec((rb, HG_DK), lambda h, r: (r, h)),
                   pl.BlockSpec((None, cpb, HG_DK, HG_DK), lambda h, r: (h, r, 0, 0)),
                   pl.BlockSpec((None, cpb, HG_CHUNK, HG_CHUNK), lambda h, r: (h, r, 0, 0))],
        out_shape=[jax.ShapeDtypeStruct((m, d), F32), jax.ShapeDtypeStruct((m, d), BF16),
                   jax.ShapeDtypeStruct((heads, m // HG_CHUNK, HG_DK, HG_DK), F32),
                   jax.ShapeDtypeStruct((heads, m // HG_CHUNK, HG_CHUNK, HG_CHUNK), BF16)],
        scratch_shapes=[pltpu.VMEM((HG_DK, HG_DK), F32)],
        compiler_params=_params(("parallel", "arbitrary")),
    )(proj, proj, proj, proj, alb, gain)


def _hgrn2_bwd(proj, o_pre, states, scores, dog, alb, gain, *, rb):
    m, d4 = proj.shape
    d = d4 // 4
    heads = d // HG_DK
    rb = min(rb, m)
    cpb = rb // HG_CHUNK
    nrb = m // rb
    c, nsub = HG_CHUNK, HG_CHUNK // HG_SUB

    def body(q_ref, f_ref, v_ref, g_ref, o_ref, st_ref, a_ref, dog_ref, alb_ref, gain_ref,
             dq_ref, df_ref, dv_ref, dg_ref, dalb_ref, dgain_ref, dstate, carry_ref):
        first = (pl.program_id(0) == 0) & (pl.program_id(1) == 0)

        @pl.when(first)
        def _():
            dgain_ref[...] = jnp.zeros(dgain_ref.shape, F32)

        @pl.when(pl.program_id(1) == 0)
        def _():
            dstate[...] = jnp.zeros(dstate.shape, F32)
            carry_ref[...] = jnp.zeros(carry_ref.shape, F32)
            dalb_ref[...] = jnp.zeros(dalb_ref.shape, F32)

        row, col, base, causal, below = _hg_masks()
        sub_iota = lax.broadcasted_iota(jnp.int32, (nsub, HG_SUB, HG_DK), 1)
        row_k = lax.broadcasted_iota(jnp.int32, (c, HG_DK), 0)
        upper = col >= row

        def chunk(step, carry):
            ci = cpb - 1 - step
            sl = pl.ds(pl.multiple_of(ci * HG_CHUNK, HG_CHUNK), HG_CHUNK)
            qr, fr, v, gr = q_ref[sl, :], f_ref[sl, :], v_ref[sl, :], g_ref[sl, :]
            qt, kk, g, lbound, sig, forget = _hg_gates(qr, fr, alb_ref[...])
            o = o_ref[sl, :]
            dogv = dog_ref[sl, :]
            gain_v = gain_ref[...]
            n, r, sg = _hg_norm(o, gr, gain_v)
            dgr = dogv * n * sg * (1.0 + gr * (1.0 - sg))
            dn = dogv * gr * sg
            dgain_ref[...] += _col_sum(dn * o * r)
            u = dn * gain_v
            d_o = r * u - o * (r * r * r) * jnp.mean(u * o, axis=-1, keepdims=True)
            st0 = st_ref[ci]
            dst = dstate[...]
            _, b, bq, qh, edecs, (b3, q3, k3) = _hg_scores(qt, kk, g, scores=False)
            a = a_ref[ci]
            eb = jnp.exp(b)
            b_last = b[c - 1:c, :]
            kdl_dec = jnp.exp(b_last - b)
            kdl = kk * kdl_dec
            d_a = jnp.where(causal, _dot_nt(d_o, v), 0.0)
            d_at = _dot_nt(v, d_o)
            dv = _dot_tn(a, d_o) + _dot_nt(kdl, dst)
            dq = eb * _hdot(d_o, st0)
            dk = _hdot(v, dst) * kdl_dec
            d_a_below = jnp.where(below, d_a, 0.0)
            dq_parts = [jnp.zeros((HG_SUB, HG_DK), F32)]
            for i in range(1, nsub):
                lo, hi = i * HG_SUB, (i + 1) * HG_SUB
                dq_parts.append(_hdot(d_a_below[lo:hi, :], kk * edecs[i]))
                gi = _hdot(d_at[:, lo:hi], qh[lo:hi, :])
                dk = dk + jnp.where(row_k < lo, edecs[i] * gi, 0.0)
            dq = dq + jnp.concatenate(dq_parts, axis=0) * jnp.exp(bq)
            dq3 = jnp.zeros((nsub, HG_SUB, HG_DK), F32)
            dk3 = jnp.zeros((nsub, HG_SUB, HG_DK), F32)
            for j in range(HG_SUB):
                e = jnp.exp(jnp.minimum(b3 - b3[:, j:j + 1, :], 0.0))
                dcol = jnp.sum(jnp.where(col == base + j, d_a, 0.0), axis=-1, keepdims=True)
                t1 = dcol.reshape(nsub, HG_SUB, 1) * e
                dq3 = dq3 + t1 * k3[:, j:j + 1, :]
                dk3 = jnp.where(sub_iota == j, jnp.sum(t1 * q3, axis=1, keepdims=True), dk3)
            dq = dq + dq3.reshape(c, HG_DK)
            dk = dk + dk3.reshape(c, HG_DK)
            dstate[...] = dst * jnp.exp(b_last) + _hdot_tn(d_o, qt * eb)
            dglog = _tdot(upper, qt * dq - kk * dk) + carry_ref[...]
            carry_ref[...] = dglog[0:1, :]
            dforget = dglog / forget
            one_m_lb = 1.0 - lbound
            dsig = (dforget - dk) * one_m_lb
            sneg = _sigmoid(-fr)
            dlb = _col_sum(dforget * (1.0 - sig) - dk * sneg)
            dalb0 = dlb * lbound * one_m_lb
            dalb_ref[...] += jnp.concatenate([dalb0, -dalb0], axis=0)
            sq = _sigmoid(qr)
            dq_ref[sl, :] = (dq * (HG_DK ** -0.5) * sq * (1.0 + qr * (1.0 - sq))).astype(dq_ref.dtype)
            df_ref[sl, :] = (dsig * sig * (1.0 - sig)).astype(df_ref.dtype)
            dv_ref[sl, :] = dv.astype(dv_ref.dtype)
            dg_ref[sl, :] = dgr.astype(dg_ref.dtype)
            return carry

        lax.fori_loop(0, cpb, chunk, 0, unroll=2)

    def rev(r):
        return nrb - 1 - r

    def col(cidx):
        return pl.BlockSpec((rb, HG_DK), lambda h, r: (rev(r), cidx * heads + h))

    def head_rows():
        return pl.BlockSpec((rb, HG_DK), lambda h, r: (rev(r), h))

    return pl.pallas_call(
        body, name="hgrn2_bwd", grid=(heads, nrb),
        in_specs=[col(0), col(1), col(2), col(3), head_rows(),
                  pl.BlockSpec((None, cpb, HG_DK, HG_DK), lambda h, r: (h, rev(r), 0, 0)),
                  pl.BlockSpec((None, cpb, HG_CHUNK, HG_CHUNK), lambda h, r: (h, rev(r), 0, 0)),
                  head_rows(),
                  pl.BlockSpec((2, HG_DK), lambda h, r: (0, h)),
                  pl.BlockSpec((1, HG_DK), lambda h, r: (0, 0))],
        out_specs=[head_rows(), head_rows(), head_rows(), head_rows(),
                   pl.BlockSpec((2, HG_DK), lambda h, r: (0, h)),
                   pl.BlockSpec((1, HG_DK), lambda h, r: (0, 0))],
        out_shape=[jax.ShapeDtypeStruct((m, d), BF16)] * 4
                  + [jax.ShapeDtypeStruct((2, d), F32), jax.ShapeDtypeStruct((1, HG_DK), F32)],
        scratch_shapes=[pltpu.VMEM((HG_DK, HG_DK), F32), pltpu.VMEM((1, HG_DK), F32)],
        compiler_params=_params(("arbitrary", "arbitrary")),
    )(proj, proj, proj, proj, o_pre, states, scores, dog, alb, gain)


def _swa_probs(qh, kp, kc, sink, slope, has_prev):
    rows = qh.shape[0]
    qi = lax.broadcasted_iota(jnp.int32, (rows, WINDOW), 0) & (WINDOW - 1)
    si = lax.broadcasted_iota(jnp.int32, (rows, WINDOW), 1)
    scale = ATT_HD ** -0.5
    dist_c = (qi - si).astype(F32)
    s_p = _dot_nt(qh, kp) * scale - slope * (dist_c + float(WINDOW))
    s_c = _dot_nt(qh, kc) * scale - slope * dist_c
    s_p = jnp.where((si > qi) & has_prev, s_p, NEG)
    s_c = jnp.where(si <= qi, s_c, NEG)
    mx = jnp.maximum(jnp.maximum(jnp.max(s_p, axis=-1, keepdims=True), jnp.max(s_c, axis=-1, keepdims=True)), sink)
    e_p, e_c, e_s = jnp.exp(s_p - mx), jnp.exp(s_c - mx), jnp.exp(sink - mx)
    inv = 1.0 / (jnp.sum(e_p, axis=-1, keepdims=True) + jnp.sum(e_c, axis=-1, keepdims=True) + e_s)
    return e_p * inv, e_c * inv, e_s * inv


def _slope(h, n_heads):
    return float(2.0 ** (-8.0 * (h + 1) / n_heads))


def _swa_group(ref_vals, sink_ref, kh, n_heads):
    heads = [kh * ATT_G + g for g in range(ATT_G)]
    stacked = [jnp.concatenate([v[:, h * ATT_HD:(h + 1) * ATT_HD] for h in heads], axis=0) for v in ref_vals]
    grp = lax.shift_right_logical(lax.broadcasted_iota(jnp.int32, (ATT_G * WINDOW, 1), 0), WINDOW.bit_length() - 1)
    slope = jnp.zeros((ATT_G * WINDOW, 1), F32)
    sink = jnp.zeros((ATT_G * WINDOW, 1), F32)
    for g, h in enumerate(heads):
        slope = jnp.where(grp == g, _slope(h, n_heads), slope)
        sink = jnp.where(grp == g, sink_ref[:, h:h + 1], sink)
    return stacked, slope, sink


def _swa_fwd(q, kv, sinks):
    m, d = q.shape
    n_heads = d // ATT_HD
    kvh = n_heads // ATT_G
    kd = kvh * ATT_HD
    nb = m // WINDOW

    def body(q_ref, kvp_ref, kvc_ref, sink_ref, o_ref):
        has_prev = pl.program_id(0) > 0
        qv, kvp, kvc = q_ref[...], kvp_ref[...], kvc_ref[...]
        outs = []
        for kh in range(kvh):
            ks = slice(kh * ATT_HD, (kh + 1) * ATT_HD)
            vs = slice(kd + kh * ATT_HD, kd + (kh + 1) * ATT_HD)
            (q4,), slope, sink = _swa_group([qv], sink_ref, kh, n_heads)
            p_p, p_c, _ = _swa_probs(q4, kvp[:, ks], kvc[:, ks], sink, slope, has_prev)
            o4 = _dot(p_p, kvp[:, vs]) + _dot(p_c, kvc[:, vs])
            outs += [o4[g * WINDOW:(g + 1) * WINDOW, :] for g in range(ATT_G)]
        o_ref[...] = jnp.concatenate(outs, axis=-1).astype(o_ref.dtype)

    return pl.pallas_call(
        body, name="swa_fwd", grid=(nb,),
        in_specs=[pl.BlockSpec((WINDOW, d), lambda n: (n, 0)),
                  pl.BlockSpec((WINDOW, 2 * kd), lambda n: (jnp.maximum(n - 1, 0), 0)),
                  pl.BlockSpec((WINDOW, 2 * kd), lambda n: (n, 0)),
                  pl.BlockSpec((1, n_heads), lambda n: (0, 0))],
        out_specs=pl.BlockSpec((WINDOW, d), lambda n: (n, 0)),
        out_shape=jax.ShapeDtypeStruct((m, d), BF16),
        compiler_params=_params(("arbitrary",)),
    )(q, kv, kv, sinks)


def _swa_bwd(q, kv, sinks, dao):
    m, d = q.shape
    n_heads = d // ATT_HD
    kvh = n_heads // ATT_G
    kd = kvh * ATT_HD
    nb = m // WINDOW
    scale = ATT_HD ** -0.5

    def body(q_ref, kvp_ref, kvc_ref, sink_ref, do_ref, dq_ref, dkvc_ref, dkvp_ref, dqsum_ref, dsink_ref):
        @pl.when(pl.program_id(0) == 0)
        def _():
            dqsum_ref[...] = jnp.zeros(dqsum_ref.shape, F32)
            dsink_ref[...] = jnp.zeros(dsink_ref.shape, F32)

        has_prev = pl.program_id(0) > 0
        qv, kvp, kvc, dov = q_ref[...], kvp_ref[...], kvc_ref[...], do_ref[...]
        lane_h = lax.broadcasted_iota(jnp.int32, (1, n_heads), 1)
        dsink = jnp.zeros((1, n_heads), F32)
        dq_parts, dk_p, dk_c, dv_p, dv_c = [], [], [], [], []
        for kh in range(kvh):
            ks = slice(kh * ATT_HD, (kh + 1) * ATT_HD)
            vs = slice(kd + kh * ATT_HD, kd + (kh + 1) * ATT_HD)
            kp, kc, vp, vc = kvp[:, ks], kvc[:, ks], kvp[:, vs], kvc[:, vs]
            (q4, do4), slope, sink = _swa_group([qv, dov], sink_ref, kh, n_heads)
            p_p, p_c, p_s = _swa_probs(q4, kp, kc, sink, slope, has_prev)
            dp_p, dp_c = _dot_nt(do4, vp), _dot_nt(do4, vc)
            delta = jnp.sum(p_p * dp_p, axis=-1, keepdims=True) + jnp.sum(p_c * dp_c, axis=-1, keepdims=True)
            ds_p, ds_c = p_p * (dp_p - delta), p_c * (dp_c - delta)
            sink_term = p_s * delta
            dq4 = (_dot(ds_p, kp) + _dot(ds_c, kc)) * scale
            for g in range(ATT_G):
                rows = slice(g * WINDOW, (g + 1) * WINDOW)
                dsink = dsink + jnp.where(lane_h == kh * ATT_G + g, -_col_sum(sink_term[rows, :]), 0.0)
                dq_parts.append(dq4[rows, :])
            dk_p.append(_dot_tn(ds_p, q4) * scale)
            dk_c.append(_dot_tn(ds_c, q4) * scale)
            dv_p.append(_dot_tn(p_p, do4))
            dv_c.append(_dot_tn(p_c, do4))
        dq = jnp.concatenate(dq_parts, axis=-1)
        dq_ref[...] = dq.astype(dq_ref.dtype)
        dqsum_ref[...] += _col_sum(dq)
        dsink_ref[...] += dsink
        dkvc_ref[...] = jnp.concatenate(dk_c + dv_c, axis=-1)
        dkvp_ref[...] = jnp.concatenate(dk_p + dv_p, axis=-1)

    return pl.pallas_call(
        body, name="swa_bwd", grid=(nb,),
        in_specs=[pl.BlockSpec((WINDOW, d), lambda n: (n, 0)),
                  pl.BlockSpec((WINDOW, 2 * kd), lambda n: (jnp.maximum(n - 1, 0), 0)),
                  pl.BlockSpec((WINDOW, 2 * kd), lambda n: (n, 0)),
                  pl.BlockSpec((1, n_heads), lambda n: (0, 0)),
                  pl.BlockSpec((WINDOW, d), lambda n: (n, 0))],
        out_specs=[pl.BlockSpec((WINDOW, d), lambda n: (n, 0)),
                   pl.BlockSpec((WINDOW, 2 * kd), lambda n: (n, 0)),
                   pl.BlockSpec((WINDOW, 2 * kd), lambda n: (n, 0)),
                   pl.BlockSpec((1, d), lambda n: (0, 0)),
                   pl.BlockSpec((1, n_heads), lambda n: (0, 0))],
        out_shape=[jax.ShapeDtypeStruct((m, d), BF16), jax.ShapeDtypeStruct((m, 2 * kd), F32),
                   jax.ShapeDtypeStruct((m, 2 * kd), F32), jax.ShapeDtypeStruct((1, d), F32),
                   jax.ShapeDtypeStruct((1, n_heads), F32)],
        compiler_params=_params(("arbitrary",)),
    )(q, kv, kv, sinks, dao)


def _kv_grad_combine(dkv_cur, dkv_prev):
    m, w = dkv_cur.shape
    nb = m // WINDOW

    def body(cur_ref, nxt_ref, o_ref, sum_ref):
        @pl.when(pl.program_id(0) == 0)
        def _():
            sum_ref[...] = jnp.zeros(sum_ref.shape, F32)

        total = cur_ref[...] + jnp.where(pl.program_id(0) < nb - 1, nxt_ref[...], 0.0)
        o_ref[...] = total.astype(o_ref.dtype)
        sum_ref[...] += _col_sum(total)

    return pl.pallas_call(
        body, name="kv_grad_combine", grid=(nb,),
        in_specs=[pl.BlockSpec((WINDOW, w), lambda n: (n, 0)),
                  pl.BlockSpec((WINDOW, w), lambda n: (jnp.minimum(n + 1, nb - 1), 0))],
        out_specs=[pl.BlockSpec((WINDOW, w), lambda n: (n, 0)), pl.BlockSpec((1, w), lambda n: (0, 0))],
        out_shape=[jax.ShapeDtypeStruct((m, w), BF16), jax.ShapeDtypeStruct((1, w), F32)],
        compiler_params=_params(("arbitrary",)),
    )(dkv_cur, dkv_prev)


def _row(v):
    return v.reshape(1, -1)


def _local_step(x, p, target, wget, grad_sink, ln_gain, ln_bias, alb, norm_gain, kv_b, b_q, sinks, b_out, ple_b):
    gs = {}
    gains = [[_row(ln_gain[i, j]) for j in range(3)] for i in range(DEPTH)]
    biases = [[_row(ln_bias[i, j]) for j in range(3)] for i in range(DEPTH)]
    sd = x.shape
    pending = [None]

    def mm(a, b, lb=0, **kw):
        after, pending[0] = pending[0], None
        return _mm(a, b, lb=lb, after=after, **kw)

    def ln_fwd(xin, h, i, j, nm):
        return _rowwise(_ln_fwd_fn, [xin, h], [gains[i][j], biases[i][j]], [(sd, F32), (sd, BF16)], name=nm)

    def tail_fwd(xa, i):
        gu = _mm(xa[1], wget("ffn_w_gate_up", i, xa[1]), lb=0, name=f"ffn_up{i}")
        act, = _rowwise(_swiglu_fwd_fn, [gu], [], [((sd[0], gu.shape[1] // 2), BF16)], name=f"swiglu{i}", tm=128)
        f = _mm(act, wget("ffn_w_down", i, act), lb=0, name=f"ffn_down{i}")
        xb = ln_fwd(xa[0], f, i, 1, f"ln_ffn{i}")
        pg = _mm(xb[1], wget("ple_w_gate", i, act), lb=0, bias=_row(ple_b[i]), name=f"ple_gate{i}")
        pu = _mm(p[i], wget("ple_w_up", i, act), lb=0, name=f"ple_up{i}")
        xc = _rowwise(_ple_ln_fwd_fn, [xb[0], pg, pu], [gains[i][2], biases[i][2]], [(sd, F32), (sd, BF16)],
                      name=f"ln_ple{i}")
        return dict(xa=xa, gu=gu, act=act, f=f, xb=xb, pg=pg, pu=pu), xc

    def tail_bwd(dxc, sv, i):
        xa, xb = sv["xa"], sv["xb"]
        dxb_part, dpg, dpu, dg2, db2, dbg = _rowwise(
            _ple_ln_bwd_fn, [dxc, xb[0], sv["pg"], sv["pu"]], [gains[i][2]],
            [(sd, F32), (sd, BF16), (sd, BF16)], [((1, sd[1]), F32)] * 3, name=f"ln_ple_bwd{i}")
        gs[f"ple_b_{i}"] = dbg
        gs[f"ln_gain_{i}_2"], gs[f"ln_bias_{i}_2"] = dg2, db2
        grad_of("ple_w_gate", i, xb[1], dpg)
        grad_of("ple_w_up", i, p[i], dpu)
        dxb = mm(dpg, wget("ple_w_gate", i, None), tb=True, add=dxb_part, name=f"ple_gate_dx{i}")
        dxa_part, df, dg1, db1, _ = _rowwise(
            _ln_bwd_fn, [dxb, xa[0], sv["f"]], [gains[i][1]],
            [(sd, F32), (sd, BF16)], [((1, sd[1]), F32)] * 3, name=f"ln_ffn_bwd{i}")
        gs[f"ln_gain_{i}_1"], gs[f"ln_bias_{i}_1"] = dg1, db1
        grad_of("ffn_w_down", i, sv["act"], df)
        dact = mm(df, wget("ffn_w_down", i, None), tb=True, name=f"ffn_down_dx{i}")
        dgu, = _rowwise(_swiglu_bwd_fn, [sv["gu"], dact], [], [(sv["gu"].shape, BF16)], name=f"swiglu_bwd{i}", tm=128)
        grad_of("ffn_w_gate_up", i, xa[1], dgu)
        return mm(dgu, wget("ffn_w_gate_up", i, None), tb=True, add=dxa_part, name=f"ffn_up_dx{i}")

    def grad_of(nm, i, act, dout):
        grad = mm(act, dout, lb=None, ta=True, out_dtype=BF16, out_layers=1, out_layer=0, name=f"grad_{nm}{i}")
        token = grad_sink(nm, i, grad)
        if token is not None:
            pending[0] = token

    def mixer_ln_bwd(dxa, xin, h, i):
        dx_part, dh, dg0, db0, dhsum = _rowwise(
            _ln_bwd_fn, [dxa, xin, h], [gains[i][0]],
            [(sd, F32), (sd, BF16)], [((1, sd[1]), F32)] * 3, name=f"ln_mix_bwd{i}")
        gs[f"ln_gain_{i}_0"], gs[f"ln_bias_{i}_0"] = dg0, db0
        return dx_part, dh, dhsum

    proj = _mm(x, wget("a_w_in", 0, None), lb=0, name="hg_proj")
    o_pre, og, states, scores = _hgrn2_fwd(proj, alb, norm_gain, rb=HG_ROWS)
    h0 = _mm(og, wget("a_w_out", 0, og), lb=0, name="hg_out")
    x1 = ln_fwd(x, h0, 0, 0, "ln_mix0")
    sv0, x3 = tail_fwd(x1, 0)
    kv = _mm(x3[1], wget("kv_w", 0, x3[1]), lb=0, bias=_row(kv_b), name="kv_proj")
    q = _mm(x3[1], wget("b_w_q", 0, x3[1]), lb=0, bias=b_q, name="q_proj")
    ao = _swa_fwd(q, kv, sinks)
    h1 = _mm(ao, wget("b_w_out", 0, x3[1]), lb=0, bias=b_out, name="att_out")
    x4 = ln_fwd(x3[0], h1, 1, 0, "ln_mix1")
    sv1, y = tail_fwd(x4, 1)
    dy, loss = _rowwise(_loss_fn, [y[0], target], [], [(sd, F32)], [((1, LANES), F32)], name="loss")

    dx4 = tail_bwd(dy, sv1, 1)
    dx3_part, dh1, dh1sum = mixer_ln_bwd(dx4, x3[0], h1, 1)
    gs["b_out"] = dh1sum
    grad_of("b_w_out", 0, ao, dh1)
    dao = mm(dh1, wget("b_w_out", 0, None), tb=True, name="att_out_dx")
    dq, dkv_cur, dkv_prev, dqsum, dsinks = _swa_bwd(q, kv, sinks, dao)
    gs["b_q"], gs["sinks"] = dqsum, dsinks
    dkv, dkvsum = _kv_grad_combine(dkv_cur, dkv_prev)
    gs["kv_b"] = dkvsum
    grad_of("b_w_q", 0, x3[1], dq)
    grad_of("kv_w", 0, x3[1], dkv)
    dx3 = mm(dq, wget("b_w_q", 0, None), tb=True, add=dx3_part, name="q_proj_dx")
    dx3 = mm(dkv, wget("kv_w", 0, None), tb=True, add=dx3, name="kv_proj_dx")
    dx1 = tail_bwd(dx3, sv0, 0)
    dx_part, dh0, _ = mixer_ln_bwd(dx1, x, h0, 0)
    grad_of("a_w_out", 0, og, dh0)
    dog = mm(dh0, wget("a_w_out", 0, None), tb=True, name="hg_out_dx")
    dqr, dfr, dvr, dgr, dalb, dgain = _hgrn2_bwd(proj, o_pre, states, scores, dog, alb, norm_gain, rb=HG_ROWS)
    gs["alb"], gs["norm_gain"] = dalb, dgain
    dproj = jnp.concatenate([dqr, dfr, dvr, dgr], axis=1)
    grad_of("a_w_in", 0, x, dproj)
    grad_x = mm(dproj, wget("a_w_in", 0, None), tb=True, add=dx_part, name="hg_proj_dx")
    return loss, grad_x, gs


HBM_SPEC = pl.BlockSpec(memory_space=pl.ANY)
HBM_ONLY = pl.BlockSpec(memory_space=pltpu.HBM)
SEM_SPEC = pl.BlockSpec(memory_space=pltpu.SEMAPHORE)
SIDE_EFFECT = pltpu.SideEffectType.DATAFLOW_SIDE_EFFECTING


def _piece(ref, kind, j):
    _, r, c = ref.shape
    if kind == "row":
        return ref.at[:, pl.ds(j * (r // N_CHIPS), r // N_CHIPS), :]
    return ref.at[:, :, pl.ds(j * (c // N_CHIPS), c // N_CHIPS)]


def _chip_of(j, c):
    return (j // 2, j % 2, c)


def _in_hbm(a):
    return pltpu.with_memory_space_constraint(a, pltpu.HBM)


def _place(src, kind, chip, *, mode, name, out_dtype, zone=None, zone_shape=None, layer=0, after=None):
    if mode == "gather":
        _, r, c = src.shape
        out_shape = (1, r * N_CHIPS, c) if kind == "row" else (1, r, c * N_CHIPS)
    else:
        out_shape = tuple(zone.shape) if zone is not None else tuple(zone_shape)
        r, c = out_shape[-2:]
    tm = _pick_rows(r, 512)
    nb = r // tm

    def full_idx(i, chip_ref):
        return (0, chip_ref[0] * nb + i, 0) if kind == "row" else (0, i, chip_ref[0])

    if mode == "gather":
        in_spec = pl.BlockSpec((None, tm, c), lambda i, chip_ref: (0, i, 0))
        out_spec = pl.BlockSpec((None, tm, c), full_idx)
    else:
        in_spec = pl.BlockSpec((None, tm, c), full_idx)
        out_spec = pl.BlockSpec((None, None, tm, c), lambda i, chip_ref: (chip_ref[0], layer, i, 0))
    in_specs, operands, aliases = [in_spec], [src], {}
    if zone is not None:
        in_specs.append(HBM_SPEC)
        operands.append(zone)
        aliases = {2: 0}
    if after is not None:
        in_specs.append(HBM_SPEC)
        operands.append(after)

    def body(chip_ref, src_ref, *rest):
        rest[-1][...] = src_ref[...].astype(rest[-1].dtype)

    return pl.pallas_call(
        body, name=name,
        grid_spec=pltpu.PrefetchScalarGridSpec(num_scalar_prefetch=1, grid=(nb,), in_specs=in_specs,
                                               out_specs=out_spec),
        out_shape=jax.ShapeDtypeStruct(out_shape, out_dtype),
        input_output_aliases=aliases,
        compiler_params=_params(("arbitrary",)),
    )(chip, *operands)


class _Exchange:
    def __init__(self, mode, srcs, lands, kinds, layers, name, after=None):
        self.mode, self.kinds, self.layers, self.name, self.n = mode, kinds, layers, name, len(lands)
        n, ns = self.n, len(srcs)
        n_in = ns + n + (after is not None)
        sem_shape = pltpu.SemaphoreType.DMA((n * N_CHIPS,))

        def body(*refs):
            src_refs, land_refs = refs[:ns], refs[ns:ns + n]
            send_sems, recv_sems = refs[n_in], refs[n_in + 1]
            token = refs[-1]
            c = lax.axis_index("c")
            me = 2 * lax.axis_index("x") + lax.axis_index("y")
            for j in range(N_CHIPS):
                @pl.when(me == j)
                def _():
                    for a in range(n):
                        for t in range(N_CHIPS):
                            if t != j:
                                src, dst = self._ends(src_refs, land_refs, a, j, t)
                                pltpu.make_async_remote_copy(
                                    src_ref=src, dst_ref=dst, send_sem=send_sems.at[a * N_CHIPS + t],
                                    recv_sem=recv_sems.at[a * N_CHIPS + j],
                                    device_id=_chip_of(t, c), device_id_type=MESH).start()
            token[...] = jnp.zeros(token.shape, token.dtype)

        arrays = list(srcs) + list(lands)
        operands = [_in_hbm(a) for a in arrays]
        in_specs = [HBM_ONLY] * (ns + n)
        if after is not None:
            operands.append(after)
            in_specs.append(HBM_SPEC)
        outs = pl.pallas_call(
            body, name=name + "_start",
            in_specs=in_specs,
            out_specs=[SEM_SPEC, SEM_SPEC] + [HBM_ONLY] * (ns + n) + [pl.BlockSpec(memory_space=pltpu.VMEM)],
            out_shape=[sem_shape, sem_shape] + [pltpu.HBM(a.shape, a.dtype) for a in arrays]
                      + [jax.ShapeDtypeStruct((8, LANES), F32)],
            input_output_aliases={i: i + 2 for i in range(ns + n)},
            compiler_params=pltpu.CompilerParams(has_side_effects=SIDE_EFFECT),
        )(*operands)
        self.send_sems, self.recv_sems = outs[0], outs[1]
        self.srcs, self.lands = list(outs[2:2 + ns]), list(outs[2 + ns:2 + ns + n])
        self.token = outs[-1]

    def _ends(self, src_refs, land_refs, a, me_j, peer):
        if self.mode == "gather":
            mine = _piece(land_refs[a], self.kinds[a], me_j)
            return mine, mine
        return _piece(src_refs[a], self.kinds[a], peer), land_refs[a].at[me_j, pl.ds(self.layers[a], 1)]

    def wait(self, after, lands=None):
        n, ns = self.n, len(self.srcs)
        lands = self.lands if lands is None else lands

        def body(*refs):
            src_refs, land_refs = refs[:ns], refs[ns:ns + n]
            send_sems, recv_sems = refs[ns + n], refs[ns + n + 1]
            c = lax.axis_index("c")
            me = 2 * lax.axis_index("x") + lax.axis_index("y")
            for j in range(N_CHIPS):
                @pl.when(me != j)
                def _():
                    for a in range(n):
                        sent, _ = self._ends(src_refs, land_refs, a, 0, j)
                        _, landed = self._ends(src_refs, land_refs, a, j, 0)
                        cp = pltpu.make_async_remote_copy(
                            src_ref=sent, dst_ref=landed, send_sem=send_sems.at[a * N_CHIPS + j],
                            recv_sem=recv_sems.at[a * N_CHIPS + j],
                            device_id=_chip_of(j, c), device_id_type=MESH)
                        cp.wait_send()
                        cp.wait_recv()

        arrays = self.srcs + list(lands)
        operands = [_in_hbm(a) for a in arrays] + [self.send_sems, self.recv_sems]
        in_specs = [HBM_ONLY] * (ns + n) + [SEM_SPEC, SEM_SPEC]
        if after is not None:
            operands.append(after)
            in_specs.append(HBM_SPEC)
        outs = pl.pallas_call(
            body, name=self.name + "_wait",
            in_specs=in_specs, out_specs=[HBM_ONLY] * (ns + n),
            out_shape=[pltpu.HBM(a.shape, a.dtype) for a in arrays],
            input_output_aliases={i: i for i in range(ns + n)},
            compiler_params=pltpu.CompilerParams(has_side_effects=SIDE_EFFECT),
        )(*operands)
        return list(outs[ns:])


def _sibling_swap(arrays, name):
    n = len(arrays)

    def body(*refs):
        ins, outs = refs[:n], refs[n:2 * n]
        send_sems, recv_sems = refs[2 * n:]
        sibling = (lax.axis_index("x"), lax.axis_index("y"), 1 - lax.axis_index("c"))
        copies = [pltpu.make_async_remote_copy(src_ref=ins[a], dst_ref=outs[a], send_sem=send_sems.at[a],
                                               recv_sem=recv_sems.at[a], device_id=sibling, device_id_type=MESH)
                  for a in range(n)]
        for cp in copies:
            cp.start()
        for cp in copies:
            cp.wait()

    return pl.pallas_call(
        body, name=name,
        in_specs=[HBM_SPEC] * n, out_specs=[HBM_SPEC] * n,
        out_shape=[jax.ShapeDtypeStruct(a.shape, a.dtype) for a in arrays],
        scratch_shapes=[pltpu.SemaphoreType.DMA((n,)), pltpu.SemaphoreType.DMA((n,))],
    )(*arrays)


def _gather_devices(vec):
    def body(in_ref, out_ref, send_sems, recv_sems, local_sem):
        x, y, c = lax.axis_index("x"), lax.axis_index("y"), lax.axis_index("c")
        me = 4 * x + 2 * y + c
        mine = pltpu.make_async_copy(in_ref, out_ref.at[me], local_sem)
        mine.start()
        copies = []
        for rel in range(1, N_DEV):
            peer = (x ^ (rel >> 2), y ^ ((rel >> 1) & 1), c ^ (rel & 1))
            copies.append(pltpu.make_async_remote_copy(
                src_ref=in_ref, dst_ref=out_ref.at[me], send_sem=send_sems.at[rel], recv_sem=recv_sems.at[rel],
                device_id=peer, device_id_type=MESH))
        for cp in copies:
            cp.start()
        for cp in copies:
            cp.wait()
        mine.wait()

    return pl.pallas_call(
        body, name="gather_small",
        in_specs=[HBM_SPEC], out_specs=HBM_SPEC,
        out_shape=jax.ShapeDtypeStruct((N_DEV,) + vec.shape, vec.dtype),
        scratch_shapes=[pltpu.SemaphoreType.DMA((N_DEV,)), pltpu.SemaphoreType.DMA((N_DEV,)),
                        pltpu.SemaphoreType.DMA],
    )(vec)


BIG = [("a_w_in", "col"), ("a_w_out", "row"), ("kv_w", "row"), ("b_w_q", "row"), ("b_w_out", "row"),
       ("ffn_w_gate_up", "col"), ("ffn_w_down", "row"), ("ple_w_up", "col"), ("ple_w_gate", "row")]
GATHER_GROUPS = [[("a_w_in", 0), ("small", 0)], [("a_w_out", 0), ("ffn_w_gate_up", 0)],
                 [("ffn_w_down", 0), ("ple_w_gate", 0), ("ple_w_up", 0)], [("kv_w", 0), ("b_w_q", 0), ("b_w_out", 0)],
                 [("ffn_w_gate_up", 1)], [("ffn_w_down", 1), ("ple_w_gate", 1), ("ple_w_up", 1)]]
SCATTER_GROUPS = [[("ple_w_gate", 1), ("ple_w_up", 1), ("ffn_w_down", 1)], [("ffn_w_gate_up", 1)],
                  [("b_w_out", 0), ("b_w_q", 0), ("kv_w", 0)], [("ple_w_gate", 0), ("ple_w_up", 0), ("ffn_w_down", 0)],
                  [("ffn_w_gate_up", 0), ("a_w_out", 0)], [("a_w_in", 0)]]
SMALL_SHARDED = ["ln_gain", "ln_bias", "a_lower_bound"]
SMALL_REPLICATED = ["a_norm_gain", "kv_b", "b_b_q", "b_sinks", "b_b_out", "ple_b_gate"]
WEIGHT_ORDER = ["a_w_in", "a_lower_bound", "a_norm_gain", "a_w_out", "kv_w", "kv_b", "b_w_q", "b_b_q", "b_sinks",
                "b_w_out", "b_b_out", "ffn_w_gate_up", "ffn_w_down", "ple_w_up", "ple_w_gate", "ple_b_gate",
                "ln_gain", "ln_bias"]


def _as3(a):
    return a.reshape((-1,) + a.shape[-2:]) if a.ndim >= 3 else a.reshape((1,) + a.shape)


def _pad_lanes(v):
    n = v.shape[-1]
    return jnp.pad(v, ((0, 0), (0, (-n) % LANES)))


def _adam_small_fn(w, mom, vel, g):
    return _adam_fn(w, mom, vel, g, jnp.zeros_like(g))[1:]


def _sum_rows_fn(slots):
    acc = slots[0]
    for s in range(1, slots.shape[0]):
        acc = acc + slots[s]
    return (acc,)


def kernel(x, p, a_w_in, a_lower_bound, a_norm_gain, a_w_out, kv_w, kv_b, b_w_q, b_b_q, b_sinks, b_w_out, b_b_out, ffn_w_gate_up, ffn_w_down, ple_w_up, ple_w_gate, ple_b_gate, ln_gain, ln_bias, loss_target, m_a_w_in, m_a_lower_bound, m_a_norm_gain, m_a_w_out, m_kv_w, m_kv_b, m_b_w_q, m_b_b_q, m_b_sinks, m_b_w_out, m_b_b_out, m_ffn_w_gate_up, m_ffn_w_down, m_ple_w_up, m_ple_w_gate, m_ple_b_gate, m_ln_gain, m_ln_bias, v_a_w_in, v_a_lower_bound, v_a_norm_gain, v_a_w_out, v_kv_w, v_kv_b, v_b_w_q, v_b_b_q, v_b_sinks, v_b_w_out, v_b_b_out, v_ffn_w_gate_up, v_ffn_w_down, v_ple_w_up, v_ple_w_gate, v_ple_b_gate, v_ln_gain, v_ln_bias):
    args = dict(locals())
    wts = {n: args[n] for n in WEIGHT_ORDER}
    mom = {n: args["m_" + n] for n in WEIGHT_ORDER}
    vel = {n: args["v_" + n] for n in WEIGHT_ORDER}
    chip = 2 * lax.axis_index("x") + lax.axis_index("y")
    d = x.shape[-1]
    dq = d // N_CHIPS

    kind_of = dict(BIG)
    kind_of["small"] = "col"
    chip_arr = chip.reshape(1).astype(jnp.int32)
    small_pack = jnp.concatenate([wts[n].reshape(-1, dq) for n in SMALL_SHARDED], axis=0)[None]

    def place(key, after):
        n, layer = key
        if n == "small":
            return _place(small_pack, "col", chip_arr, mode="gather", name="place_small", out_dtype=F32, after=after)
        return _place(_as3(wts[n])[layer:layer + 1], kind_of[n], chip_arr, mode="gather",
                      name=f"place_{n}{layer}", out_dtype=BF16, after=after)

    gathers, where = [], {}
    for gi, group in enumerate(GATHER_GROUPS):
        prev = gathers[-1].token if gathers else None
        gathers.append(_Exchange("gather", [], [place(k, prev) for k in group], [kind_of[k[0]] for k in group],
                                 [0] * len(group), f"gather{gi}", after=prev))
        for k in group:
            where[k] = gi
    all_started = gathers[-1].token
    ready = {}

    def wget(name, layer, after):
        key = (name, layer)
        if key not in ready:
            gi = where[key]
            outs = gathers[gi].wait(all_started if gi == 0 else after)
            for k, arr in zip(GATHER_GROUPS[gi], outs):
                ready[k] = arr
        return ready[key]

    small_full = wget("small", 0, None)[0]
    ln_gain_f = small_full[0:6].reshape(DEPTH, 3, d)
    ln_bias_f = small_full[6:12].reshape(DEPTH, 3, d)
    alb_f = small_full[12:14]

    group_of = {k: gi for gi, group in enumerate(SCATTER_GROUPS) for k in group}
    grads_done, zones, scatters = {}, {}, []

    def grad_sink(name, layer, grad):
        grads_done[(name, layer)] = grad
        zones[name] = _place(grad, kind_of[name], chip_arr, mode="scatter", name=f"place_grad_{name}{layer}",
                             out_dtype=BF16, zone=zones.get(name), zone_shape=(N_CHIPS,) + _as3(wts[name]).shape,
                             layer=layer)
        gi = group_of[(name, layer)]
        group = SCATTER_GROUPS[gi]
        if not all(k in grads_done for k in group):
            return None
        ex = _Exchange("scatter", [grads_done[k] for k in group], [zones[k[0]] for k in group],
                       [kind_of[k[0]] for k in group], [k[1] for k in group], f"scatter{gi}")
        for k, zone in zip(group, ex.lands):
            zones[k[0]] = zone
        scatters.append((ex, group))
        return ex.token

    loss, grad_x, gs = _local_step(
        x[0], p[:, 0], loss_target[0], wget, grad_sink, ln_gain_f, ln_bias_f, alb_f, a_norm_gain, kv_b, b_b_q,
        b_sinks, b_b_out, ple_b_gate)

    res = {}

    def arrive(batch, after):
        for ex, group in batch:
            outs = ex.wait(after, lands=[zones[k[0]] for k in group])
            for k, zone in zip(group, outs):
                zones[k[0]] = zone

    def update(names, tag):
        partial = []
        for n in names:
            s2 = zones[n].reshape(N_CHIPS, -1, zones[n].shape[-1])
            partial.append(_rowwise(_sum_slots_fn, [s2], [], [(s2.shape[1:], BF16)], name=f"sum_{n}")[0])
        sibling = _sibling_swap(partial, tag)
        for n, own, sib in zip(names, partial, sibling):
            shp = wts[n].shape
            flat = lambda a: a.reshape(-1, shp[-1])
            out = _rowwise(_adam_fn, [flat(wts[n]), flat(mom[n]), flat(vel[n]), own, sib], [],
                           [(own.shape, F32)] * 4, name=f"adam_{n}")
            res[n] = [o.reshape(shp) for o in out]
        return res[names[-1]][1]

    last_names = [k[0] for k in SCATTER_GROUPS[-1]]
    arrive(scatters[:-1], grad_x)
    updated = update([n for n, _ in BIG if n not in last_names], "sibling_swap")
    arrive(scatters[-1:], updated)
    update(last_names, "sibling_swap_last")

    ln_g = jnp.concatenate([gs[f"ln_gain_{i}_{j}"] for i in range(DEPTH) for j in range(3)], axis=0)
    ln_b = jnp.concatenate([gs[f"ln_bias_{i}_{j}"] for i in range(DEPTH) for j in range(3)], axis=0)
    ple_bg = jnp.concatenate([gs[f"ple_b_{i}"] for i in range(DEPTH)], axis=0)
    small_list = [ln_g.reshape(1, -1), ln_b.reshape(1, -1), gs["alb"].reshape(1, -1), gs["norm_gain"],
                  gs["kv_b"], gs["b_q"], _pad_lanes(gs["sinks"]), gs["b_out"], ple_bg.reshape(1, -1), loss]
    small_vec = jnp.concatenate(small_list, axis=1)
    everyone = _gather_devices(small_vec)
    total, = _rowwise(_sum_rows_fn, [everyone], [], [(small_vec.shape, F32)], name="sum_small")
    offs, pos = [], 0
    for v in small_list:
        offs.append((pos, v.shape[1]))
        pos += v.shape[1]

    def seg(k):
        return total[0, offs[k][0]:offs[k][0] + offs[k][1]]

    def my_cols(full, rows):
        return lax.dynamic_slice_in_dim(full.reshape(rows, N_CHIPS, dq), chip, 1, axis=1).reshape(rows, dq)

    n_sink = b_sinks.shape[-1]
    small_grads = {
        "ln_gain": my_cols(seg(0), 6).reshape(ln_gain.shape), "ln_bias": my_cols(seg(1), 6).reshape(ln_bias.shape),
        "a_lower_bound": my_cols(seg(2), 2), "a_norm_gain": seg(3).reshape(a_norm_gain.shape),
        "kv_b": seg(4).reshape(kv_b.shape), "b_b_q": seg(5).reshape(b_b_q.shape),
        "b_sinks": seg(6)[:n_sink].reshape(b_sinks.shape), "b_b_out": seg(7).reshape(b_b_out.shape),
        "ple_b_gate": seg(8).reshape(ple_b_gate.shape)}
    names = SMALL_SHARDED + SMALL_REPLICATED
    pack = lambda dct: _pad_lanes(jnp.concatenate([dct[n].reshape(1, -1) for n in names], axis=1))
    g_pack = pack(small_grads)
    upd = _rowwise(_adam_small_fn, [pack(wts), pack(mom), pack(vel), g_pack], [], [(g_pack.shape, F32)] * 3,
                   name="adam_small")
    pos = 0
    for n in names:
        size = wts[n].size
        res[n] = [small_grads[n]] + [u[0, pos:pos + size].reshape(wts[n].shape) for u in upd]
        pos += size

    outs = [seg(9)[0], grad_x[None]]
    for k in range(4):
        outs += [res[n][k] for n in WEIGHT_ORDER]
    return tuple(outs)
```

```python
import functools

import jax
import jax.numpy as jnp
from jax import lax
from jax.experimental import pallas as pl
from jax.experimental.pallas import tpu as pltpu

F32 = jnp.float32
BF16 = jnp.bfloat16
MESH = pl.DeviceIdType.MESH

LANES = 128
HG_DK = 128
HG_CHUNK = 64
HG_SUB = 16
HG_ROWS = 512
ATT_HD = 64
ATT_G = 4
WINDOW = 128
DEPTH = 2
ALPHA = (2.0 * DEPTH) ** 0.25
LN_EPS = 1e-5
RMS_EPS = 1e-6
ADAM_LR, ADAM_B1, ADAM_B2, ADAM_EPS, ADAM_WD, ADAM_STEP = 0.001, 0.9, 0.999, 1e-08, 0.01, 10
N_CHIPS = 4
N_DEV = 8
VMEM_LIMIT = 56 * 1024 * 1024
NEG = -1e30


def _pick(n, cap):
    best = None
    for d in range(LANES, min(n, cap) + 1, LANES):
        if n % d == 0:
            best = d
    return n if best is None else best


def _pick_rows(m, cap):
    best = None
    for d in range(16, min(m, cap) + 1, 16):
        if m % d == 0:
            best = d
    return m if best is None else best


def _params(sem):
    return pltpu.CompilerParams(dimension_semantics=sem, vmem_limit_bytes=VMEM_LIMIT)


def _zeros_index(ndim):
    return lambda i, j, kk: (0,) * ndim


def _mm(a, b, *, name, la=None, lb=None, ta=False, tb=False, bias=None, add=None, out_dtype=F32,
        out_layers=None, out_layer=None, after=None, post=None, caps=(1024, 1536, 2048)):
    ar, ac = a.shape[-2:]
    br, bc = b.shape[-2:]
    m, k = (ac, ar) if ta else (ar, ac)
    k2, n = (bc, br) if tb else (br, bc)
    assert k == k2, (a.shape, b.shape, ta, tb)
    if post is not None:
        caps = (512, n, caps[2])
    tm, tn, tk = _pick(m, caps[0]), _pick(n, caps[1]), _pick(k, caps[2])
    assert post is None or tn == n
    nk = k // tk
    grid = (m // tm, n // tn, nk)

    def spec(block, idx, layer):
        if layer is None:
            return pl.BlockSpec(block, idx)
        return pl.BlockSpec((None,) + block, lambda i, j, kk: (layer,) + idx(i, j, kk))

    a_spec = spec((tk, tm), lambda i, j, kk: (kk, i), la) if ta else spec((tm, tk), lambda i, j, kk: (i, kk), la)
    b_spec = spec((tn, tk), lambda i, j, kk: (j, kk), lb) if tb else spec((tk, tn), lambda i, j, kk: (kk, j), lb)
    in_specs, operands = [a_spec, b_spec], [a, b]
    if bias is not None:
        in_specs.append(pl.BlockSpec((1, tn), lambda i, j, kk: (0, j)))
        operands.append(bias)
    if add is not None:
        in_specs.append(pl.BlockSpec((tm, tn), lambda i, j, kk: (i, j)))
        operands.append(add)
    if after is not None:
        in_specs.append(pl.BlockSpec(memory_space=pl.ANY))
        operands.append(after)
    dims = (((0 if ta else 1,), (1 if tb else 0,)), ((), ()))
    has_bias, has_add = bias is not None, add is not None
    if post is None:
        fn, rows, whole, outs, sums = None, [], [], [], []
        out_shape = jax.ShapeDtypeStruct((m, n) if out_layers is None else (out_layers, m, n), out_dtype)
        out_specs = spec((tm, tn), lambda i, j, kk: (i, j), out_layer)
    else:
        fn, rows, whole, outs, sums = post
        in_specs += [pl.BlockSpec((tm, r.shape[-1]), lambda i, j, kk: (i, 0)) for r in rows]
        in_specs += [pl.BlockSpec(tuple(w.shape), _zeros_index(w.ndim)) for w in whole]
        operands += list(rows) + list(whole)
        out_shape = [jax.ShapeDtypeStruct(sh, dt) for sh, dt in list(outs) + list(sums)]
        out_specs = ([pl.BlockSpec((tm, sh[-1]), lambda i, j, kk: (i, 0)) for sh, _ in outs]
                     + [pl.BlockSpec(tuple(sh), _zeros_index(len(sh))) for sh, _ in sums])
    n_in, n_extra, n_outs, n_sums = len(operands), len(rows) + len(whole), len(outs), len(sums)

    def body(*refs):
        a_ref, b_ref = refs[0], refs[1]
        pos = 2
        bias_ref = add_ref = None
        if has_bias:
            bias_ref = refs[pos]
            pos += 1
        if has_add:
            add_ref = refs[pos]
            pos += 1
        extra_refs = refs[n_in - n_extra:n_in]
        out_refs = refs[n_in:n_in + max(n_outs, 1)]
        sum_refs = refs[n_in + n_outs:n_in + n_outs + n_sums]
        acc_ref = refs[-1] if nk > 1 else None
        part = lax.dot_general(a_ref[...].astype(BF16), b_ref[...].astype(BF16), dims, preferred_element_type=F32)

        def finish(total):
            if has_bias:
                total = total + bias_ref[...]
            if has_add:
                total = total + add_ref[...]
            if fn is None:
                out_refs[0][...] = total.astype(out_refs[0].dtype)
                return
            res = fn(total, *[r[...] for r in extra_refs])
            for ref, val in zip(out_refs, res[:n_outs]):
                ref[...] = val.astype(ref.dtype)
            if n_sums:
                @pl.when(pl.program_id(0) == 0)
                def _():
                    for ref in sum_refs:
                        ref[...] = jnp.zeros(ref.shape, ref.dtype)

                for ref, val in zip(sum_refs, res[n_outs:]):
                    ref[...] += val

        if nk == 1:
            finish(part)
        else:
            kk = pl.program_id(2)

            @pl.when(kk == 0)
            def _():
                acc_ref[...] = part

            @pl.when(kk > 0)
            def _():
                acc_ref[...] += part

            @pl.when(kk == nk - 1)
            def _():
                finish(acc_ref[...])

    return pl.pallas_call(
        body, name=name, grid=grid, in_specs=in_specs, out_specs=out_specs, out_shape=out_shape,
        scratch_shapes=[pltpu.VMEM((tm, tn), F32)] if nk > 1 else [],
        compiler_params=_params(("arbitrary" if n_sums else "parallel", "parallel", "arbitrary")),
    )(*operands)


def _rowwise(fn, rows, whole, outs, sums=(), *, name, tm=256):
    m = rows[0].shape[-2]
    tm = _pick_rows(m, tm)
    n_rows, n_whole, n_outs, n_sums = len(rows), len(whole), len(outs), len(sums)

    def rspec(shape):
        lead = len(shape) - 2
        return pl.BlockSpec(tuple(shape[:-2]) + (tm, shape[-1]), lambda i: (0,) * lead + (i, 0))

    def wspec(shape):
        return pl.BlockSpec(tuple(shape), lambda i: (0,) * len(shape))

    def body(*refs):
        vals = [r[...] for r in refs[:n_rows + n_whole]]
        out_refs = refs[n_rows + n_whole:n_rows + n_whole + n_outs]
        sum_refs = refs[n_rows + n_whole + n_outs:]
        res = fn(*vals)
        for ref, val in zip(out_refs, res[:n_outs]):
            ref[...] = val.astype(ref.dtype)
        if n_sums:
            @pl.when(pl.program_id(0) == 0)
            def _():
                for ref in sum_refs:
                    ref[...] = jnp.zeros(ref.shape, ref.dtype)

            for ref, val in zip(sum_refs, res[n_outs:]):
                ref[...] += val

    result = pl.pallas_call(
        body, name=name, grid=(m // tm,),
        in_specs=[rspec(r.shape) for r in rows] + [wspec(w.shape) for w in whole],
        out_specs=[rspec(s) for s, _ in outs] + [wspec(s) for s, _ in sums],
        out_shape=[jax.ShapeDtypeStruct(s, d) for s, d in list(outs) + list(sums)],
        compiler_params=_params(("arbitrary",)),
    )(*rows, *whole)
    return result


def _sigmoid(v):
    return jax.nn.sigmoid(v)


def _col_sum(v):
    return jnp.sum(v, axis=0, keepdims=True)


def _ln_stats(z):
    mu = jnp.mean(z, axis=-1, keepdims=True)
    zc = z - mu
    var = jnp.mean(zc * zc, axis=-1, keepdims=True)
    rstd = lax.rsqrt(var + LN_EPS)
    return zc * rstd, rstd


def _ln_fwd_fn(xin, h, gain, bias):
    xhat, _ = _ln_stats(ALPHA * xin + h)
    y = xhat * gain + bias
    return y, y


def _ple_ln_fwd_fn(xin, pg, pu, gain, bias):
    xhat, _ = _ln_stats(ALPHA * xin + _sigmoid(pg) * pu)
    y = xhat * gain + bias
    return y, y


def _ln_dz(dy, z, gain):
    xhat, rstd = _ln_stats(z)
    dxhat = dy * gain
    dz = rstd * (dxhat - jnp.mean(dxhat, axis=-1, keepdims=True)
                 - xhat * jnp.mean(dxhat * xhat, axis=-1, keepdims=True))
    return dz, _col_sum(dy * xhat), _col_sum(dy)


def _ln_bwd_fn(dy, xin, h, gain):
    dz, dgain, dbias = _ln_dz(dy, ALPHA * xin + h, gain)
    return ALPHA * dz, dz, dgain, dbias, _col_sum(dz)


def _ple_ln_bwd_fn(dy, xin, pg, pu, gain):
    sg = _sigmoid(pg)
    dz, dgain, dbias = _ln_dz(dy, ALPHA * xin + sg * pu, gain)
    dpg = dz * pu * sg * (1.0 - sg)
    return ALPHA * dz, dpg, dz * sg, dgain, dbias, _col_sum(dpg)


def _swiglu_fwd_fn(gu):
    hid = gu.shape[-1] // 2
    gate, up = gu[:, :hid], gu[:, hid:]
    return (gate * _sigmoid(gate) * up,)


def _swiglu_bwd_fn(gu, dact):
    hid = gu.shape[-1] // 2
    gate, up = gu[:, :hid], gu[:, hid:]
    sg = _sigmoid(gate)
    dgate = dact * up * sg * (1.0 + gate * (1.0 - sg))
    dup = dact * gate * sg
    return (jnp.concatenate([dgate, dup], axis=-1),)


def _loss_fn(y, target):
    err = y - target
    inv = 1.0 / y.shape[-1]
    part = 0.5 * inv * jnp.sum(jnp.sum(err * err, axis=-1, keepdims=True), axis=0, keepdims=True)
    return err * inv, jnp.broadcast_to(part, (1, LANES))


def _adam_fn(w, mom, vel, p_own, p_sib):
    g = p_own.astype(F32) + p_sib.astype(F32)
    m_new = ADAM_B1 * mom + (1.0 - ADAM_B1) * g
    v_new = ADAM_B2 * vel + (1.0 - ADAM_B2) * (g * g)
    m_hat = m_new / (1.0 - ADAM_B1 ** ADAM_STEP)
    v_hat = v_new / (1.0 - ADAM_B2 ** ADAM_STEP)
    delta = -ADAM_LR * (m_hat / (jnp.sqrt(v_hat) + ADAM_EPS) + ADAM_WD * w)
    return g, delta, m_new, v_new


def _sum_slots_fn(slots):
    acc = slots[0].astype(F32)
    for s in range(1, slots.shape[0]):
        acc = acc + slots[s].astype(F32)
    return (acc,)


def _split2(x):
    hi = x.astype(BF16)
    return hi, (x - hi.astype(F32)).astype(BF16)


def _dot3(a, b, dims):
    a_hi, a_lo = _split2(a)
    b_hi, b_lo = _split2(b)
    dn = (dims, ((), ()))
    return (lax.dot_general(a_hi, b_hi, dn, preferred_element_type=F32)
            + (lax.dot_general(a_hi, b_lo, dn, preferred_element_type=F32)
               + lax.dot_general(a_lo, b_hi, dn, preferred_element_type=F32)))


def _tdot(mask01, b):
    m = mask01.astype(BF16)
    b_hi = b.astype(BF16)
    rest = b - b_hi.astype(F32)
    b_mid = rest.astype(BF16)
    b_lo = (rest - b_mid.astype(F32)).astype(BF16)
    dn = (((1,), (0,)), ((), ()))
    return (lax.dot_general(m, b_hi, dn, preferred_element_type=F32)
            + (lax.dot_general(m, b_mid, dn, preferred_element_type=F32)
               + lax.dot_general(m, b_lo, dn, preferred_element_type=F32)))


def _hdot(a, b):
    return _dot3(a, b, ((1,), (0,)))


def _hdot_nt(a, b):
    return _dot3(a, b, ((1,), (1,)))


def _hdot_tn(a, b):
    return _dot3(a, b, ((0,), (0,)))


def _dot(a, b):
    return lax.dot_general(a.astype(BF16), b.astype(BF16), (((1,), (0,)), ((), ())), preferred_element_type=F32)


def _dot_nt(a, b):
    return lax.dot_general(a.astype(BF16), b.astype(BF16), (((1,), (1,)), ((), ())), preferred_element_type=F32)


def _dot_tn(a, b):
    return lax.dot_general(a.astype(BF16), b.astype(BF16), (((0,), (0,)), ((), ())), preferred_element_type=F32)


def _hg_masks():
    c = HG_CHUNK
    row = lax.broadcasted_iota(jnp.int32, (c, c), 0)
    col = lax.broadcasted_iota(jnp.int32, (c, c), 1)
    base = row & (-HG_SUB)
    return row, col, base, col <= row, col < base


def _hg_gates(qr, fr, alb):
    lbound = _sigmoid(alb[0:1, :] - alb[1:2, :])
    sig = _sigmoid(fr)
    forget = lbound + (1.0 - lbound) * sig
    kk = (1.0 - lbound) * _sigmoid(-fr)
    qt = qr * _sigmoid(qr) * (HG_DK ** -0.5)
    return qt, kk, jnp.log(forget), lbound, sig, forget


def _hg_scores(qt, kk, g, scores=True):
    c, nsub = HG_CHUNK, HG_CHUNK // HG_SUB
    row, col, base, causal, below = _hg_masks()
    b = _tdot(causal, g)
    rr = _tdot(below, g)
    bq = b - rr
    qh = qt * jnp.exp(bq)
    edecs = [None]
    parts = [jnp.zeros((HG_SUB, c), F32)]
    for i in range(1, nsub):
        edec = jnp.exp(jnp.minimum(rr[i * HG_SUB:i * HG_SUB + 1, :] - b, 0.0))
        edecs.append(edec)
        if scores:
            parts.append(_dot_nt(qh[i * HG_SUB:(i + 1) * HG_SUB, :], kk * edec))
    b3 = b.reshape(nsub, HG_SUB, HG_DK)
    q3 = qt.reshape(nsub, HG_SUB, HG_DK)
    k3 = kk.reshape(nsub, HG_SUB, HG_DK)
    if not scores:
        return None, b, bq, qh, edecs, (b3, q3, k3)
    a = jnp.where(below, jnp.concatenate(parts, axis=0), 0.0)
    for j in range(HG_SUB):
        e = jnp.exp(jnp.minimum(b3 - b3[:, j:j + 1, :], 0.0))
        colv = jnp.sum(q3 * e * k3[:, j:j + 1, :], axis=-1, keepdims=True).reshape(c, 1)
        a = jnp.where(col == base + j, colv, a)
    a = jnp.where(causal, a, 0.0)
    return a, b, bq, qh, edecs, (b3, q3, k3)


def _hg_norm(o, gr, gain):
    r = lax.rsqrt(jnp.mean(o * o, axis=-1, keepdims=True) + RMS_EPS)
    sg = _sigmoid(gr)
    return o * r * gain, r, sg


def _hgrn2_fwd(proj, alb, gain, *, rb):
    m, d4 = proj.shape
    d = d4 // 4
    heads = d // HG_DK
    rb = min(rb, m)
    cpb = rb // HG_CHUNK
    nrb = m // rb

    def body(q_ref, f_ref, v_ref, g_ref, alb_ref, gain_ref, o_ref, og_ref, st_ref, a_ref, state):
        @pl.when(pl.program_id(1) == 0)
        def _():
            state[...] = jnp.zeros(state.shape, F32)

        def chunk(ci, carry):
            sl = pl.ds(pl.multiple_of(ci * HG_CHUNK, HG_CHUNK), HG_CHUNK)
            qt, kk, g, _, _, _ = _hg_gates(q_ref[sl, :], f_ref[sl, :], alb_ref[...])
            v = v_ref[sl, :]
            st = state[...]
            st_ref[ci] = st
            a, b, _, _, _, _ = _hg_scores(qt, kk, g)
            a_ref[ci] = a.astype(a_ref.dtype)
            o = _dot(a, v) + _dot_nt(qt * jnp.exp(b), st)
            b_last = b[HG_CHUNK - 1:HG_CHUNK, :]
            state[...] = st * jnp.exp(b_last) + _hdot_tn(v, kk * jnp.exp(b_last - b))
            o_ref[sl, :] = o
            n, _, sg = _hg_norm(o, g_ref[sl, :], gain_ref[...])
            og_ref[sl, :] = (n * g_ref[sl, :] * sg).astype(og_ref.dtype)
            return carry

        lax.fori_loop(0, cpb, chunk, 0, unroll=2)

    def col(cidx):
        return pl.BlockSpec((rb, HG_DK), lambda h, r: (r, cidx * heads + h))

    return pl.pallas_call(
        body, name="hgrn2_fwd", grid=(heads, nrb),
        in_specs=[col(0), col(1), col(2), col(3),
                  pl.BlockSpec((2, HG_DK), lambda h, r: (0, h)),
                  pl.BlockSpec((1, HG_DK), lambda h, r: (0, 0))],
        out_specs=[pl.BlockSpec((rb, HG_DK), lambda h, r: (r, h)),
                   pl.BlockSpec((rb, HG_DK), lambda h, r: (r, h)),
                   pl.BlockSpec((None, cpb, HG_DK, HG_DK), lambda h, r: (h, r, 0, 0)),
                   pl.BlockSpec((None, cpb, HG_CHUNK, HG_CHUNK), lambda h, r: (h, r, 0, 0))],
        out_shape=[jax.ShapeDtypeStruct((m, d), F32), jax.ShapeDtypeStruct((m, d), BF16),
                   jax.ShapeDtypeStruct((heads, m // HG_CHUNK, HG_DK, HG_DK), F32),
                   jax.ShapeDtypeStruct((heads, m // HG_CHUNK, HG_CHUNK, HG_CHUNK), BF16)],
        scratch_shapes=[pltpu.VMEM((HG_DK, HG_DK), F32)],
        compiler_params=_params(("parallel", "arbitrary")),
    )(proj, proj, proj, proj, alb, gain)


def _hgrn2_bwd(proj, o_pre, states, scores, dog, alb, gain, *, rb):
    m, d4 = proj.shape
    d = d4 // 4
    heads = d // HG_DK
    rb = min(rb, m)
    cpb = rb // HG_CHUNK
    nrb = m // rb
    c, nsub = HG_CHUNK, HG_CHUNK // HG_SUB

    def body(q_ref, f_ref, v_ref, g_ref, o_ref, st_ref, a_ref, dog_ref, alb_ref, gain_ref,
             dq_ref, df_ref, dv_ref, dg_ref, dalb_ref, dgain_ref, dstate, carry_ref):
        first = (pl.program_id(0) == 0) & (pl.program_id(1) == 0)

        @pl.when(first)
        def _():
            dgain_ref[...] = jnp.zeros(dgain_ref.shape, F32)

        @pl.when(pl.program_id(1) == 0)
        def _():
            dstate[...] = jnp.zeros(dstate.shape, F32)
            carry_ref[...] = jnp.zeros(carry_ref.shape, F32)
            dalb_ref[...] = jnp.zeros(dalb_ref.shape, F32)

        row, col, base, causal, below = _hg_masks()
        sub_iota = lax.broadcasted_iota(jnp.int32, (nsub, HG_SUB, HG_DK), 1)
        row_k = lax.broadcasted_iota(jnp.int32, (c, HG_DK), 0)
        upper = col >= row

        def chunk(step, carry):
            ci = cpb - 1 - step
            sl = pl.ds(pl.multiple_of(ci * HG_CHUNK, HG_CHUNK), HG_CHUNK)
            qr, fr, v, gr = q_ref[sl, :], f_ref[sl, :], v_ref[sl, :], g_ref[sl, :]
            qt, kk, g, lbound, sig, forget = _hg_gates(qr, fr, alb_ref[...])
            o = o_ref[sl, :]
            dogv = dog_ref[sl, :]
            gain_v = gain_ref[...]
            n, r, sg = _hg_norm(o, gr, gain_v)
            dgr = dogv * n * sg * (1.0 + gr * (1.0 - sg))
            dn = dogv * gr * sg
            dgain_ref[...] += _col_sum(dn * o * r)
            u = dn * gain_v
            d_o = r * u - o * (r * r * r) * jnp.mean(u * o, axis=-1, keepdims=True)
            st0 = st_ref[ci]
            dst = dstate[...]
            _, b, bq, qh, edecs, (b3, q3, k3) = _hg_scores(qt, kk, g, scores=False)
            a = a_ref[ci]
            eb = jnp.exp(b)
            b_last = b[c - 1:c, :]
            kdl_dec = jnp.exp(b_last - b)
            kdl = kk * kdl_dec
            d_a = jnp.where(causal, _dot_nt(d_o, v), 0.0)
            d_at = _dot_nt(v, d_o)
            dv = _dot_tn(a, d_o) + _dot_nt(kdl, dst)
            dq = eb * _hdot(d_o, st0)
            dk = _hdot(v, dst) * kdl_dec
            d_a_below = jnp.where(below, d_a, 0.0)
            dq_parts = [jnp.zeros((HG_SUB, HG_DK), F32)]
            for i in range(1, nsub):
                lo, hi = i * HG_SUB, (i + 1) * HG_SUB
                dq_parts.append(_hdot(d_a_below[lo:hi, :], kk * edecs[i]))
                gi = _hdot(d_at[:, lo:hi], qh[lo:hi, :])
                dk = dk + jnp.where(row_k < lo, edecs[i] * gi, 0.0)
            dq = dq + jnp.concatenate(dq_parts, axis=0) * jnp.exp(bq)
            dq3 = jnp.zeros((nsub, HG_SUB, HG_DK), F32)
            dk3 = jnp.zeros((nsub, HG_SUB, HG_DK), F32)
            for j in range(HG_SUB):
                e = jnp.exp(jnp.minimum(b3 - b3[:, j:j + 1, :], 0.0))
                dcol = jnp.sum(jnp.where(col == base + j, d_a, 0.0), axis=-1, keepdims=True)
                t1 = dcol.reshape(nsub, HG_SUB, 1) * e
                dq3 = dq3 + t1 * k3[:, j:j + 1, :]
                dk3 = jnp.where(sub_iota == j, jnp.sum(t1 * q3, axis=1, keepdims=True), dk3)
            dq = dq + dq3.reshape(c, HG_DK)
            dk = dk + dk3.reshape(c, HG_DK)
            dstate[...] = dst * jnp.exp(b_last) + _hdot_tn(d_o, qt * eb)
            dglog = _tdot(upper, qt * dq - kk * dk) + carry_ref[...]
            carry_ref[...] = dglog[0:1, :]
            dforget = dglog / forget
            one_m_lb = 1.0 - lbound
            dsig = (dforget - dk) * one_m_lb
            sneg = _sigmoid(-fr)
            dlb = _col_sum(dforget * (1.0 - sig) - dk * sneg)
            dalb0 = dlb * lbound * one_m_lb
            dalb_ref[...] += jnp.concatenate([dalb0, -dalb0], axis=0)
            sq = _sigmoid(qr)
            dq_ref[sl, :] = (dq * (HG_DK ** -0.5) * sq * (1.0 + qr * (1.0 - sq))).astype(dq_ref.dtype)
            df_ref[sl, :] = (dsig * sig * (1.0 - sig)).astype(df_ref.dtype)
            dv_ref[sl, :] = dv.astype(dv_ref.dtype)
            dg_ref[sl, :] = dgr.astype(dg_ref.dtype)
            return carry

        lax.fori_loop(0, cpb, chunk, 0, unroll=2)

    def rev(r):
        return nrb - 1 - r

    def col(cidx):
        return pl.BlockSpec((rb, HG_DK), lambda h, r: (rev(r), cidx * heads + h))

    def head_rows():
        return pl.BlockSpec((rb, HG_DK), lambda h, r: (rev(r), h))

    return pl.pallas_call(
        body, name="hgrn2_bwd", grid=(heads, nrb),
        in_specs=[col(0), col(1), col(2), col(3), head_rows(),
                  pl.BlockSpec((None, cpb, HG_DK, HG_DK), lambda h, r: (h, rev(r), 0, 0)),
                  pl.BlockSpec((None, cpb, HG_CHUNK, HG_CHUNK), lambda h, r: (h, rev(r), 0, 0)),
                  head_rows(),
                  pl.BlockSpec((2, HG_DK), lambda h, r: (0, h)),
                  pl.BlockSpec((1, HG_DK), lambda h, r: (0, 0))],
        out_specs=[head_rows(), head_rows(), head_rows(), head_rows(),
                   pl.BlockSpec((2, HG_DK), lambda h, r: (0, h)),
                   pl.BlockSpec((1, HG_DK), lambda h, r: (0, 0))],
        out_shape=[jax.ShapeDtypeStruct((m, d), BF16)] * 4
                  + [jax.ShapeDtypeStruct((2, d), F32), jax.ShapeDtypeStruct((1, HG_DK), F32)],
        scratch_shapes=[pltpu.VMEM((HG_DK, HG_DK), F32), pltpu.VMEM((1, HG_DK), F32)],
        compiler_params=_params(("arbitrary", "arbitrary")),
    )(proj, proj, proj, proj, o_pre, states, scores, dog, alb, gain)


def _swa_probs(qh, kp, kc, sink, slope, has_prev):
    rows = qh.shape[0]
    qi = lax.broadcasted_iota(jnp.int32, (rows, WINDOW), 0) & (WINDOW - 1)
    si = lax.broadcasted_iota(jnp.int32, (rows, WINDOW), 1)
    scale = ATT_HD ** -0.5
    dist_c = (qi - si).astype(F32)
    s_p = _dot_nt(qh, kp) * scale - slope * (dist_c + float(WINDOW))
    s_c = _dot_nt(qh, kc) * scale - slope * dist_c
    s_p = jnp.where((si > qi) & has_prev, s_p, NEG)
    s_c = jnp.where(si <= qi, s_c, NEG)
    mx = jnp.maximum(jnp.maximum(jnp.max(s_p, axis=-1, keepdims=True), jnp.max(s_c, axis=-1, keepdims=True)), sink)
    e_p, e_c, e_s = jnp.exp(s_p - mx), jnp.exp(s_c - mx), jnp.exp(sink - mx)
    inv = 1.0 / (jnp.sum(e_p, axis=-1, keepdims=True) + jnp.sum(e_c, axis=-1, keepdims=True) + e_s)
    return e_p * inv, e_c * inv, e_s * inv


def _slope(h, n_heads):
    return float(2.0 ** (-8.0 * (h + 1) / n_heads))


def _swa_group(ref_vals, sink_ref, kh, n_heads):
    heads = [kh * ATT_G + g for g in range(ATT_G)]
    stacked = [jnp.concatenate([v[:, h * ATT_HD:(h + 1) * ATT_HD] for h in heads], axis=0) for v in ref_vals]
    grp = lax.shift_right_logical(lax.broadcasted_iota(jnp.int32, (ATT_G * WINDOW, 1), 0), WINDOW.bit_length() - 1)
    slope = jnp.zeros((ATT_G * WINDOW, 1), F32)
    sink = jnp.zeros((ATT_G * WINDOW, 1), F32)
    for g, h in enumerate(heads):
        slope = jnp.where(grp == g, _slope(h, n_heads), slope)
        sink = jnp.where(grp == g, sink_ref[:, h:h + 1], sink)
    return stacked, slope, sink


def _swa_fwd(q, kv, sinks):
    m, d = q.shape
    n_heads = d // ATT_HD
    kvh = n_heads // ATT_G
    kd = kvh * ATT_HD
    nb = m // WINDOW

    def body(q_ref, kvp_ref, kvc_ref, sink_ref, o_ref):
        has_prev = pl.program_id(0) > 0
        qv, kvp, kvc = q_ref[...], kvp_ref[...], kvc_ref[...]
        outs = []
        for kh in range(kvh):
            ks = slice(kh * ATT_HD, (kh + 1) * ATT_HD)
            vs = slice(kd + kh * ATT_HD, kd + (kh + 1) * ATT_HD)
            (q4,), slope, sink = _swa_group([qv], sink_ref, kh, n_heads)
            p_p, p_c, _ = _swa_probs(q4, kvp[:, ks], kvc[:, ks], sink, slope, has_prev)
            o4 = _dot(p_p, kvp[:, vs]) + _dot(p_c, kvc[:, vs])
            outs += [o4[g * WINDOW:(g + 1) * WINDOW, :] for g in range(ATT_G)]
        o_ref[...] = jnp.concatenate(outs, axis=-1).astype(o_ref.dtype)

    return pl.pallas_call(
        body, name="swa_fwd", grid=(nb,),
        in_specs=[pl.BlockSpec((WINDOW, d), lambda n: (n, 0)),
                  pl.BlockSpec((WINDOW, 2 * kd), lambda n: (jnp.maximum(n - 1, 0), 0)),
                  pl.BlockSpec((WINDOW, 2 * kd), lambda n: (n, 0)),
                  pl.BlockSpec((1, n_heads), lambda n: (0, 0))],
        out_specs=pl.BlockSpec((WINDOW, d), lambda n: (n, 0)),
        out_shape=jax.ShapeDtypeStruct((m, d), BF16),
        compiler_params=_params(("arbitrary",)),
    )(q, kv, kv, sinks)


def _swa_bwd(q, kv, sinks, dao):
    m, d = q.shape
    n_heads = d // ATT_HD
    kvh = n_heads // ATT_G
    kd = kvh * ATT_HD
    nb = m // WINDOW
    scale = ATT_HD ** -0.5

    def body(q_ref, kvp_ref, kvc_ref, sink_ref, do_ref, dq_ref, dkvc_ref, dkvp_ref, dqsum_ref, dsink_ref):
        @pl.when(pl.program_id(0) == 0)
        def _():
            dqsum_ref[...] = jnp.zeros(dqsum_ref.shape, F32)
            dsink_ref[...] = jnp.zeros(dsink_ref.shape, F32)

        has_prev = pl.program_id(0) > 0
        qv, kvp, kvc, dov = q_ref[...], kvp_ref[...], kvc_ref[...], do_ref[...]
        lane_h = lax.broadcasted_iota(jnp.int32, (1, n_heads), 1)
        dsink = jnp.zeros((1, n_heads), F32)
        dq_parts, dk_p, dk_c, dv_p, dv_c = [], [], [], [], []
        for kh in range(kvh):
            ks = slice(kh * ATT_HD, (kh + 1) * ATT_HD)
            vs = slice(kd + kh * ATT_HD, kd + (kh + 1) * ATT_HD)
            kp, kc, vp, vc = kvp[:, ks], kvc[:, ks], kvp[:, vs], kvc[:, vs]
            (q4, do4), slope, sink = _swa_group([qv, dov], sink_ref, kh, n_heads)
            p_p, p_c, p_s = _swa_probs(q4, kp, kc, sink, slope, has_prev)
            dp_p, dp_c = _dot_nt(do4, vp), _dot_nt(do4, vc)
            delta = jnp.sum(p_p * dp_p, axis=-1, keepdims=True) + jnp.sum(p_c * dp_c, axis=-1, keepdims=True)
            ds_p, ds_c = p_p * (dp_p - delta), p_c * (dp_c - delta)
            sink_term = p_s * delta
            dq4 = (_dot(ds_p, kp) + _dot(ds_c, kc)) * scale
            for g in range(ATT_G):
                rows = slice(g * WINDOW, (g + 1) * WINDOW)
                dsink = dsink + jnp.where(lane_h == kh * ATT_G + g, -_col_sum(sink_term[rows, :]), 0.0)
                dq_parts.append(dq4[rows, :])
            dk_p.append(_dot_tn(ds_p, q4) * scale)
            dk_c.append(_dot_tn(ds_c, q4) * scale)
            dv_p.append(_dot_tn(p_p, do4))
            dv_c.append(_dot_tn(p_c, do4))
        dq = jnp.concatenate(dq_parts, axis=-1)
        dq_ref[...] = dq.astype(dq_ref.dtype)
        dqsum_ref[...] += _col_sum(dq)
        dsink_ref[...] += dsink
        dkvc_ref[...] = jnp.concatenate(dk_c + dv_c, axis=-1)
        dkvp_ref[...] = jnp.concatenate(dk_p + dv_p, axis=-1)

    return pl.pallas_call(
        body, name="swa_bwd", grid=(nb,),
        in_specs=[pl.BlockSpec((WINDOW, d), lambda n: (n, 0)),
                  pl.BlockSpec((WINDOW, 2 * kd), lambda n: (jnp.maximum(n - 1, 0), 0)),
                  pl.BlockSpec((WINDOW, 2 * kd), lambda n: (n, 0)),
                  pl.BlockSpec((1, n_heads), lambda n: (0, 0)),
                  pl.BlockSpec((WINDOW, d), lambda n: (n, 0))],
        out_specs=[pl.BlockSpec((WINDOW, d), lambda n: (n, 0)),
                   pl.BlockSpec((WINDOW, 2 * kd), lambda n: (n, 0)),
                   pl.BlockSpec((WINDOW, 2 * kd), lambda n: (n, 0)),
                   pl.BlockSpec((1, d), lambda n: (0, 0)),
                   pl.BlockSpec((1, n_heads), lambda n: (0, 0))],
        out_shape=[jax.ShapeDtypeStruct((m, d), BF16), jax.ShapeDtypeStruct((m, 2 * kd), F32),
                   jax.ShapeDtypeStruct((m, 2 * kd), F32), jax.ShapeDtypeStruct((1, d), F32),
                   jax.ShapeDtypeStruct((1, n_heads), F32)],
        compiler_params=_params(("arbitrary",)),
    )(q, kv, kv, sinks, dao)


def _kv_grad_combine(dkv_cur, dkv_prev):
    m, w = dkv_cur.shape
    nb = m // WINDOW

    def body(cur_ref, nxt_ref, o_ref, sum_ref):
        @pl.when(pl.program_id(0) == 0)
        def _():
            sum_ref[...] = jnp.zeros(sum_ref.shape, F32)

        total = cur_ref[...] + jnp.where(pl.program_id(0) < nb - 1, nxt_ref[...], 0.0)
        o_ref[...] = total.astype(o_ref.dtype)
        sum_ref[...] += _col_sum(total)

    return pl.pallas_call(
        body, name="kv_grad_combine", grid=(nb,),
        in_specs=[pl.BlockSpec((WINDOW, w), lambda n: (n, 0)),
                  pl.BlockSpec((WINDOW, w), lambda n: (jnp.minimum(n + 1, nb - 1), 0))],
        out_specs=[pl.BlockSpec((WINDOW, w), lambda n: (n, 0)), pl.BlockSpec((1, w), lambda n: (0, 0))],
        out_shape=[jax.ShapeDtypeStruct((m, w), BF16), jax.ShapeDtypeStruct((1, w), F32)],
        compiler_params=_params(("arbitrary",)),
    )(dkv_cur, dkv_prev)


def _row(v):
    return v.reshape(1, -1)


def _local_step(x, p, target, wget, grad_sink, ln_gain, ln_bias, alb, norm_gain, kv_b, b_q, sinks, b_out, ple_b):
    gs = {}
    gains = [[_row(ln_gain[i, j]) for j in range(3)] for i in range(DEPTH)]
    biases = [[_row(ln_bias[i, j]) for j in range(3)] for i in range(DEPTH)]
    sd = x.shape
    pending = [None]

    def mm(a, b, lb=0, **kw):
        after, pending[0] = pending[0], None
        return _mm(a, b, lb=lb, after=after, **kw)

    def mm_ln(a, wt, xin, i, j, nm, bias=None, pu=None):
        if pu is None:
            fn, rows = (lambda h, xv, g, bv: (h,) + _ln_fwd_fn(xv, h, g, bv)), [xin]
        else:
            fn, rows = (lambda h, xv, puv, g, bv: (h,) + _ple_ln_fwd_fn(xv, h, puv, g, bv)), [xin, pu]
        h, y, yb = _mm(a, wt, lb=0, bias=bias, name=nm,
                       post=(fn, rows, [gains[i][j], biases[i][j]], [(sd, F32), (sd, F32), (sd, BF16)], []))
        return h, (y, yb)

    def mm_ln_bwd(a, wt, add, xin, h, i, j, nm):
        dx_part, dh, dg, db, dhsum = mm(a, wt, tb=True, add=add, name=nm,
                                        post=(_ln_bwd_fn, [xin, h], [gains[i][j]], [(sd, F32), (sd, BF16)],
                                              [((1, sd[1]), F32)] * 3))
        gs[f"ln_gain_{i}_{j}"], gs[f"ln_bias_{i}_{j}"] = dg, db
        return dx_part, dh, dhsum

    def tail_fwd(xa, i):
        gu = _mm(xa[1], wget("ffn_w_gate_up", i, xa[1]), lb=0, name=f"ffn_up{i}")
        act, = _rowwise(_swiglu_fwd_fn, [gu], [], [((sd[0], gu.shape[1] // 2), BF16)], name=f"swiglu{i}", tm=128)
        f, xb = mm_ln(act, wget("ffn_w_down", i, act), xa[0], i, 1, f"ffn_down_ln{i}")
        pu = _mm(p[i], wget("ple_w_up", i, act), lb=0, name=f"ple_up{i}")
        pg, xc = mm_ln(xb[1], wget("ple_w_gate", i, act), xb[0], i, 2, f"ple_gate_ln{i}", bias=_row(ple_b[i]), pu=pu)
        return dict(xa=xa, gu=gu, act=act, f=f, xb=xb, pg=pg, pu=pu), xc

    def tail_bwd(head, sv, i, mix_in, mix_h):
        xa, xb = sv["xa"], sv["xb"]
        dxb_part, dpg, dpu, dg2, db2, dbg = head(
            _ple_ln_bwd_fn, [xb[0], sv["pg"], sv["pu"]], [gains[i][2]],
            [(sd, F32), (sd, BF16), (sd, BF16)], [((1, sd[1]), F32)] * 3)[:6]
        gs[f"ple_b_{i}"] = dbg
        gs[f"ln_gain_{i}_2"], gs[f"ln_bias_{i}_2"] = dg2, db2
        grad_of("ple_w_gate", i, xb[1], dpg)
        grad_of("ple_w_up", i, p[i], dpu)
        dxa_part, df, _ = mm_ln_bwd(dpg, wget("ple_w_gate", i, None), dxb_part, xa[0], sv["f"], i, 1,
                                    f"ple_gate_dx_ln{i}")
        grad_of("ffn_w_down", i, sv["act"], df)
        dact = mm(df, wget("ffn_w_down", i, None), tb=True, name=f"ffn_down_dx{i}")
        dgu, = _rowwise(_swiglu_bwd_fn, [sv["gu"], dact], [], [(sv["gu"].shape, BF16)], name=f"swiglu_bwd{i}", tm=128)
        grad_of("ffn_w_gate_up", i, xa[1], dgu)
        return mm_ln_bwd(dgu, wget("ffn_w_gate_up", i, None), dxa_part, mix_in, mix_h, i, 0, f"ffn_up_dx_ln{i}")

    def grad_of(nm, i, act, dout):
        grad = mm(act, dout, lb=None, ta=True, out_dtype=BF16, out_layers=1, out_layer=0, name=f"grad_{nm}{i}")
        token = grad_sink(nm, i, grad)
        if token is not None:
            pending[0] = token

    proj = _mm(x, wget("a_w_in", 0, None), lb=0, name="hg_proj")
    o_pre, og, states, scores = _hgrn2_fwd(proj, alb, norm_gain, rb=HG_ROWS)
    h0, x1 = mm_ln(og, wget("a_w_out", 0, og), x, 0, 0, "hg_out_ln")
    sv0, x3 = tail_fwd(x1, 0)
    kv = _mm(x3[1], wget("kv_w", 0, x3[1]), lb=0, bias=_row(kv_b), name="kv_proj")
    q = _mm(x3[1], wget("b_w_q", 0, x3[1]), lb=0, bias=b_q, name="q_proj")
    ao = _swa_fwd(q, kv, sinks)
    h1, x4 = mm_ln(ao, wget("b_w_out", 0, x3[1]), x3[0], 1, 0, "att_out_ln", bias=b_out)
    sv1, y = tail_fwd(x4, 1)

    loss_box = []

    def loss_head(fn, rows, whole, outs, sums):
        def with_loss(yv, tv, *rest):
            dy, part = _loss_fn(yv, tv)
            return fn(dy, *rest) + (part,)

        res = _rowwise(with_loss, [y[0], target] + rows, whole, outs, list(sums) + [((1, LANES), F32)],
                       name="loss_ln_ple_bwd1")
        loss_box.append(res[-1])
        return res

    dx3_part, dh1, dh1sum = tail_bwd(loss_head, sv1, 1, x3[0], h1)
    loss = loss_box[0]
    gs["b_out"] = dh1sum
    grad_of("b_w_out", 0, ao, dh1)
    dao = mm(dh1, wget("b_w_out", 0, None), tb=True, name="att_out_dx")
    dq, dkv_cur, dkv_prev, dqsum, dsinks = _swa_bwd(q, kv, sinks, dao)
    gs["b_q"], gs["sinks"] = dqsum, dsinks
    dkv, dkvsum = _kv_grad_combine(dkv_cur, dkv_prev)
    gs["kv_b"] = dkvsum
    grad_of("b_w_q", 0, x3[1], dq)
    grad_of("kv_w", 0, x3[1], dkv)
    dx3 = mm(dq, wget("b_w_q", 0, None), tb=True, add=dx3_part, name="q_proj_dx")

    def kv_head(*post):
        return mm(dkv, wget("kv_w", 0, None), tb=True, add=dx3, name="kv_proj_dx_ln_ple_bwd0", post=post)

    dx_part, dh0, _ = tail_bwd(kv_head, sv0, 0, x, h0)
    grad_of("a_w_out", 0, og, dh0)
    dog = mm(dh0, wget("a_w_out", 0, None), tb=True, name="hg_out_dx")
    dqr, dfr, dvr, dgr, dalb, dgain = _hgrn2_bwd(proj, o_pre, states, scores, dog, alb, norm_gain, rb=HG_ROWS)
    gs["alb"], gs["norm_gain"] = dalb, dgain
    dproj = jnp.concatenate([dqr, dfr, dvr, dgr], axis=1)
    grad_of("a_w_in", 0, x, dproj)
    grad_x = mm(dproj, wget("a_w_in", 0, None), tb=True, add=dx_part, name="hg_proj_dx")
    return loss, grad_x, gs


HBM_SPEC = pl.BlockSpec(memory_space=pl.ANY)
HBM_ONLY = pl.BlockSpec(memory_space=pltpu.HBM)
SEM_SPEC = pl.BlockSpec(memory_space=pltpu.SEMAPHORE)
SIDE_EFFECT = pltpu.SideEffectType.DATAFLOW_SIDE_EFFECTING


def _piece(ref, kind, j):
    _, r, c = ref.shape
    if kind == "row":
        return ref.at[:, pl.ds(j * (r // N_CHIPS), r // N_CHIPS), :]
    return ref.at[:, :, pl.ds(j * (c // N_CHIPS), c // N_CHIPS)]


def _chip_of(j, c):
    return (j // 2, j % 2, c)


def _in_hbm(a):
    return pltpu.with_memory_space_constraint(a, pltpu.HBM)


def _place(src, kind, chip, *, mode, name, out_dtype, zone=None, zone_shape=None, layer=0, after=None):
    if mode == "gather":
        _, r, c = src.shape
        out_shape = (1, r * N_CHIPS, c) if kind == "row" else (1, r, c * N_CHIPS)
    else:
        out_shape = tuple(zone.shape) if zone is not None else tuple(zone_shape)
        r, c = out_shape[-2:]
    tm = _pick_rows(r, 512)
    nb = r // tm

    def full_idx(i, chip_ref):
        return (0, chip_ref[0] * nb + i, 0) if kind == "row" else (0, i, chip_ref[0])

    if mode == "gather":
        in_spec = pl.BlockSpec((None, tm, c), lambda i, chip_ref: (0, i, 0))
        out_spec = pl.BlockSpec((None, tm, c), full_idx)
    else:
        in_spec = pl.BlockSpec((None, tm, c), full_idx)
        out_spec = pl.BlockSpec((None, None, tm, c), lambda i, chip_ref: (chip_ref[0], layer, i, 0))
    in_specs, operands, aliases = [in_spec], [src], {}
    if zone is not None:
        in_specs.append(HBM_SPEC)
        operands.append(zone)
        aliases = {2: 0}
    if after is not None:
        in_specs.append(HBM_SPEC)
        operands.append(after)

    def body(chip_ref, src_ref, *rest):
        rest[-1][...] = src_ref[...].astype(rest[-1].dtype)

    return pl.pallas_call(
        body, name=name,
        grid_spec=pltpu.PrefetchScalarGridSpec(num_scalar_prefetch=1, grid=(nb,), in_specs=in_specs,
                                               out_specs=out_spec),
        out_shape=jax.ShapeDtypeStruct(out_shape, out_dtype),
        input_output_aliases=aliases,
        compiler_params=_params(("arbitrary",)),
    )(chip, *operands)


class _Exchange:
    def __init__(self, mode, srcs, lands, kinds, layers, name, after=None):
        self.mode, self.kinds, self.layers, self.name, self.n = mode, kinds, layers, name, len(lands)
        n, ns = self.n, len(srcs)
        n_in = ns + n + (after is not None)
        sem_shape = pltpu.SemaphoreType.DMA((n * N_CHIPS,))

        def body(*refs):
            src_refs, land_refs = refs[:ns], refs[ns:ns + n]
            send_sems, recv_sems = refs[n_in], refs[n_in + 1]
            token = refs[-1]
            c = lax.axis_index("c")
            me = 2 * lax.axis_index("x") + lax.axis_index("y")
            for j in range(N_CHIPS):
                @pl.when(me == j)
                def _():
                    for a in range(n):
                        for t in range(N_CHIPS):
                            if t != j:
                                src, dst = self._ends(src_refs, land_refs, a, j, t)
                                pltpu.make_async_remote_copy(
                                    src_ref=src, dst_ref=dst, send_sem=send_sems.at[a * N_CHIPS + t],
                                    recv_sem=recv_sems.at[a * N_CHIPS + j],
                                    device_id=_chip_of(t, c), device_id_type=MESH).start()
            token[...] = jnp.zeros(token.shape, token.dtype)

        arrays = list(srcs) + list(lands)
        operands = [_in_hbm(a) for a in arrays]
        in_specs = [HBM_ONLY] * (ns + n)
        if after is not None:
            operands.append(after)
            in_specs.append(HBM_SPEC)
        outs = pl.pallas_call(
            body, name=name + "_start",
            in_specs=in_specs,
            out_specs=[SEM_SPEC, SEM_SPEC] + [HBM_ONLY] * (ns + n) + [pl.BlockSpec(memory_space=pltpu.VMEM)],
            out_shape=[sem_shape, sem_shape] + [pltpu.HBM(a.shape, a.dtype) for a in arrays]
                      + [jax.ShapeDtypeStruct((8, LANES), F32)],
            input_output_aliases={i: i + 2 for i in range(ns + n)},
            compiler_params=pltpu.CompilerParams(has_side_effects=SIDE_EFFECT),
        )(*operands)
        self.send_sems, self.recv_sems = outs[0], outs[1]
        self.srcs, self.lands = list(outs[2:2 + ns]), list(outs[2 + ns:2 + ns + n])
        self.token = outs[-1]

    def _ends(self, src_refs, land_refs, a, me_j, peer):
        if self.mode == "gather":
            mine = _piece(land_refs[a], self.kinds[a], me_j)
            return mine, mine
        return _piece(src_refs[a], self.kinds[a], peer), land_refs[a].at[me_j, pl.ds(self.layers[a], 1)]

    def wait(self, after, lands=None):
        n, ns = self.n, len(self.srcs)
        lands = self.lands if lands is None else lands

        def body(*refs):
            src_refs, land_refs = refs[:ns], refs[ns:ns + n]
            send_sems, recv_sems = refs[ns + n], refs[ns + n + 1]
            c = lax.axis_index("c")
            me = 2 * lax.axis_index("x") + lax.axis_index("y")
            for j in range(N_CHIPS):
                @pl.when(me != j)
                def _():
                    for a in range(n):
                        sent, _ = self._ends(src_refs, land_refs, a, 0, j)
                        _, landed = self._ends(src_refs, land_refs, a, j, 0)
                        cp = pltpu.make_async_remote_copy(
                            src_ref=sent, dst_ref=landed, send_sem=send_sems.at[a * N_CHIPS + j],
                            recv_sem=recv_sems.at[a * N_CHIPS + j],
                            device_id=_chip_of(j, c), device_id_type=MESH)
                        cp.wait_send()
                        cp.wait_recv()

        arrays = self.srcs + list(lands)
        operands = [_in_hbm(a) for a in arrays] + [self.send_sems, self.recv_sems]
        in_specs = [HBM_ONLY] * (ns + n) + [SEM_SPEC, SEM_SPEC]
        if after is not None:
            operands.append(after)
            in_specs.append(HBM_SPEC)
        outs = pl.pallas_call(
            body, name=self.name + "_wait",
            in_specs=in_specs, out_specs=[HBM_ONLY] * (ns + n),
            out_shape=[pltpu.HBM(a.shape, a.dtype) for a in arrays],
            input_output_aliases={i: i for i in range(ns + n)},
            compiler_params=pltpu.CompilerParams(has_side_effects=SIDE_EFFECT),
        )(*operands)
        return list(outs[ns:])


def _sibling_swap(arrays, name):
    n = len(arrays)

    def body(*refs):
        ins, outs = refs[:n], refs[n:2 * n]
        send_sems, recv_sems = refs[2 * n:]
        sibling = (lax.axis_index("x"), lax.axis_index("y"), 1 - lax.axis_index("c"))
        copies = [pltpu.make_async_remote_copy(src_ref=ins[a], dst_ref=outs[a], send_sem=send_sems.at[a],
                                               recv_sem=recv_sems.at[a], device_id=sibling, device_id_type=MESH)
                  for a in range(n)]
        for cp in copies:
            cp.start()
        for cp in copies:
            cp.wait()

    return pl.pallas_call(
        body, name=name,
        in_specs=[HBM_SPEC] * n, out_specs=[HBM_SPEC] * n,
        out_shape=[jax.ShapeDtypeStruct(a.shape, a.dtype) for a in arrays],
        scratch_shapes=[pltpu.SemaphoreType.DMA((n,)), pltpu.SemaphoreType.DMA((n,))],
    )(*arrays)


def _gather_devices(vec):
    def body(in_ref, out_ref, send_sems, recv_sems, local_sem):
        x, y, c = lax.axis_index("x"), lax.axis_index("y"), lax.axis_index("c")
        me = 4 * x + 2 * y + c
        mine = pltpu.make_async_copy(in_ref, out_ref.at[me], local_sem)
        mine.start()
        copies = []
        for rel in range(1, N_DEV):
            peer = (x ^ (rel >> 2), y ^ ((rel >> 1) & 1), c ^ (rel & 1))
            copies.append(pltpu.make_async_remote_copy(
                src_ref=in_ref, dst_ref=out_ref.at[me], send_sem=send_sems.at[rel], recv_sem=recv_sems.at[rel],
                device_id=peer, device_id_type=MESH))
        for cp in copies:
            cp.start()
        for cp in copies:
            cp.wait()
        mine.wait()

    return pl.pallas_call(
        body, name="gather_small",
        in_specs=[HBM_SPEC], out_specs=HBM_SPEC,
        out_shape=jax.ShapeDtypeStruct((N_DEV,) + vec.shape, vec.dtype),
        scratch_shapes=[pltpu.SemaphoreType.DMA((N_DEV,)), pltpu.SemaphoreType.DMA((N_DEV,)),
                        pltpu.SemaphoreType.DMA],
    )(vec)


BIG = [("a_w_in", "col"), ("a_w_out", "row"), ("kv_w", "row"), ("b_w_q", "row"), ("b_w_out", "row"),
       ("ffn_w_gate_up", "col"), ("ffn_w_down", "row"), ("ple_w_up", "col"), ("ple_w_gate", "row")]
GATHER_GROUPS = [[("a_w_in", 0), ("small", 0)], [("a_w_out", 0), ("ffn_w_gate_up", 0)],
                 [("ffn_w_down", 0), ("ple_w_gate", 0), ("ple_w_up", 0)], [("kv_w", 0), ("b_w_q", 0), ("b_w_out", 0)],
                 [("ffn_w_gate_up", 1)], [("ffn_w_down", 1), ("ple_w_gate", 1), ("ple_w_up", 1)]]
SCATTER_GROUPS = [[("ple_w_gate", 1), ("ple_w_up", 1), ("ffn_w_down", 1)], [("ffn_w_gate_up", 1)],
                  [("b_w_out", 0), ("b_w_q", 0), ("kv_w", 0)], [("ple_w_gate", 0), ("ple_w_up", 0), ("ffn_w_down", 0)],
                  [("ffn_w_gate_up", 0), ("a_w_out", 0)], [("a_w_in", 0)]]
SMALL_SHARDED = ["ln_gain", "ln_bias", "a_lower_bound"]
SMALL_REPLICATED = ["a_norm_gain", "kv_b", "b_b_q", "b_sinks", "b_b_out", "ple_b_gate"]
WEIGHT_ORDER = ["a_w_in", "a_lower_bound", "a_norm_gain", "a_w_out", "kv_w", "kv_b", "b_w_q", "b_b_q", "b_sinks",
                "b_w_out", "b_b_out", "ffn_w_gate_up", "ffn_w_down", "ple_w_up", "ple_w_gate", "ple_b_gate",
                "ln_gain", "ln_bias"]


def _as3(a):
    return a.reshape((-1,) + a.shape[-2:]) if a.ndim >= 3 else a.reshape((1,) + a.shape)


def _pad_lanes(v):
    n = v.shape[-1]
    return jnp.pad(v, ((0, 0), (0, (-n) % LANES)))


def _adam_small_fn(w, mom, vel, g):
    return _adam_fn(w, mom, vel, g, jnp.zeros_like(g))[1:]


def _sum_rows_fn(slots):
    acc = slots[0]
    for s in range(1, slots.shape[0]):
        acc = acc + slots[s]
    return (acc,)


def kernel(x, p, a_w_in, a_lower_bound, a_norm_gain, a_w_out, kv_w, kv_b, b_w_q, b_b_q, b_sinks, b_w_out, b_b_out, ffn_w_gate_up, ffn_w_down, ple_w_up, ple_w_gate, ple_b_gate, ln_gain, ln_bias, loss_target, m_a_w_in, m_a_lower_bound, m_a_norm_gain, m_a_w_out, m_kv_w, m_kv_b, m_b_w_q, m_b_b_q, m_b_sinks, m_b_w_out, m_b_b_out, m_ffn_w_gate_up, m_ffn_w_down, m_ple_w_up, m_ple_w_gate, m_ple_b_gate, m_ln_gain, m_ln_bias, v_a_w_in, v_a_lower_bound, v_a_norm_gain, v_a_w_out, v_kv_w, v_kv_b, v_b_w_q, v_b_b_q, v_b_sinks, v_b_w_out, v_b_b_out, v_ffn_w_gate_up, v_ffn_w_down, v_ple_w_up, v_ple_w_gate, v_ple_b_gate, v_ln_gain, v_ln_bias):
    args = dict(locals())
    wts = {n: args[n] for n in WEIGHT_ORDER}
    mom = {n: args["m_" + n] for n in WEIGHT_ORDER}
    vel = {n: args["v_" + n] for n in WEIGHT_ORDER}
    chip = 2 * lax.axis_index("x") + lax.axis_index("y")
    d = x.shape[-1]
    dq = d // N_CHIPS

    kind_of = dict(BIG)
    kind_of["small"] = "col"
    chip_arr = chip.reshape(1).astype(jnp.int32)
    small_pack = jnp.concatenate([wts[n].reshape(-1, dq) for n in SMALL_SHARDED], axis=0)[None]

    def place(key, after):
        n, layer = key
        if n == "small":
            return _place(small_pack, "col", chip_arr, mode="gather", name="place_small", out_dtype=F32, after=after)
        return _place(_as3(wts[n])[layer:layer + 1], kind_of[n], chip_arr, mode="gather",
                      name=f"place_{n}{layer}", out_dtype=BF16, after=after)

    gathers, where = [], {}
    for gi, group in enumerate(GATHER_GROUPS):
        prev = gathers[-1].token if gathers else None
        gathers.append(_Exchange("gather", [], [place(k, prev) for k in group], [kind_of[k[0]] for k in group],
                                 [0] * len(group), f"gather{gi}", after=prev))
        for k in group:
            where[k] = gi
    all_started = gathers[-1].token
    ready = {}

    def wget(name, layer, after):
        key = (name, layer)
        if key not in ready:
            gi = where[key]
            outs = gathers[gi].wait(all_started if gi == 0 else after)
            for k, arr in zip(GATHER_GROUPS[gi], outs):
                ready[k] = arr
        return ready[key]

    small_full = wget("small", 0, None)[0]
    ln_gain_f = small_full[0:6].reshape(DEPTH, 3, d)
    ln_bias_f = small_full[6:12].reshape(DEPTH, 3, d)
    alb_f = small_full[12:14]

    group_of = {k: gi for gi, group in enumerate(SCATTER_GROUPS) for k in group}
    grads_done, zones, scatters = {}, {}, []

    def grad_sink(name, layer, grad):
        grads_done[(name, layer)] = grad
        zones[name] = _place(grad, kind_of[name], chip_arr, mode="scatter", name=f"place_grad_{name}{layer}",
                             out_dtype=BF16, zone=zones.get(name), zone_shape=(N_CHIPS,) + _as3(wts[name]).shape,
                             layer=layer)
        gi = group_of[(name, layer)]
        group = SCATTER_GROUPS[gi]
        if not all(k in grads_done for k in group):
            return None
        ex = _Exchange("scatter", [grads_done[k] for k in group], [zones[k[0]] for k in group],
                       [kind_of[k[0]] for k in group], [k[1] for k in group], f"scatter{gi}")
        for k, zone in zip(group, ex.lands):
            zones[k[0]] = zone
        scatters.append((ex, group))
        return ex.token

    loss, grad_x, gs = _local_step(
        x[0], p[:, 0], loss_target[0], wget, grad_sink, ln_gain_f, ln_bias_f, alb_f, a_norm_gain, kv_b, b_b_q,
        b_sinks, b_b_out, ple_b_gate)

    res = {}

    def arrive(batch, after):
        for ex, group in batch:
            outs = ex.wait(after, lands=[zones[k[0]] for k in group])
            for k, zone in zip(group, outs):
                zones[k[0]] = zone

    def update(names, tag):
        partial = []
        for n in names:
            s2 = zones[n].reshape(N_CHIPS, -1, zones[n].shape[-1])
            partial.append(_rowwise(_sum_slots_fn, [s2], [], [(s2.shape[1:], BF16)], name=f"sum_{n}")[0])
        sibling = _sibling_swap(partial, tag)
        for n, own, sib in zip(names, partial, sibling):
            shp = wts[n].shape
            flat = lambda a: a.reshape(-1, shp[-1])
            out = _rowwise(_adam_fn, [flat(wts[n]), flat(mom[n]), flat(vel[n]), own, sib], [],
                           [(own.shape, F32)] * 4, name=f"adam_{n}")
            res[n] = [o.reshape(shp) for o in out]
        return res[names[-1]][1]

    last_names = [k[0] for k in SCATTER_GROUPS[-1]]
    arrive(scatters[:-1], grad_x)
    updated = update([n for n, _ in BIG if n not in last_names], "sibling_swap")
    arrive(scatters[-1:], updated)
    update(last_names, "sibling_swap_last")

    ln_g = jnp.concatenate([gs[f"ln_gain_{i}_{j}"] for i in range(DEPTH) for j in range(3)], axis=0)
    ln_b = jnp.concatenate([gs[f"ln_bias_{i}_{j}"] for i in range(DEPTH) for j in range(3)], axis=0)
    ple_bg = jnp.concatenate([gs[f"ple_b_{i}"] for i in range(DEPTH)], axis=0)
    small_list = [ln_g.reshape(1, -1), ln_b.reshape(1, -1), gs["alb"].reshape(1, -1), gs["norm_gain"],
                  gs["kv_b"], gs["b_q"], _pad_lanes(gs["sinks"]), gs["b_out"], ple_bg.reshape(1, -1), loss]
    small_vec = jnp.concatenate(small_list, axis=1)
    everyone = _gather_devices(small_vec)
    total, = _rowwise(_sum_rows_fn, [everyone], [], [(small_vec.shape, F32)], name="sum_small")
    offs, pos = [], 0
    for v in small_list:
        offs.append((pos, v.shape[1]))
        pos += v.shape[1]

    def seg(k):
        return total[0, offs[k][0]:offs[k][0] + offs[k][1]]

    def my_cols(full, rows):
        return lax.dynamic_slice_in_dim(full.reshape(rows, N_CHIPS, dq), chip, 1, axis=1).reshape(rows, dq)

    n_sink = b_sinks.shape[-1]
    small_grads = {
        "ln_gain": my_cols(seg(0), 6).reshape(ln_gain.shape), "ln_bias": my_cols(seg(1), 6).reshape(ln_bias.shape),
        "a_lower_bound": my_cols(seg(2), 2), "a_norm_gain": seg(3).reshape(a_norm_gain.shape),
        "kv_b": seg(4).reshape(kv_b.shape), "b_b_q": seg(5).reshape(b_b_q.shape),
        "b_sinks": seg(6)[:n_sink].reshape(b_sinks.shape), "b_b_out": seg(7).reshape(b_b_out.shape),
        "ple_b_gate": seg(8).reshape(ple_b_gate.shape)}
    names = SMALL_SHARDED + SMALL_REPLICATED
    pack = lambda dct: _pad_lanes(jnp.concatenate([dct[n].reshape(1, -1) for n in names], axis=1))
    g_pack = pack(small_grads)
    upd = _rowwise(_adam_small_fn, [pack(wts), pack(mom), pack(vel), g_pack], [], [(g_pack.shape, F32)] * 3,
                   name="adam_small")
    pos = 0
    for n in names:
        size = wts[n].size
        res[n] = [small_grads[n]] + [u[0, pos:pos + size].reshape(wts[n].shape) for u in upd]
        pos += size

    outs = [seg(9)[0], grad_x[None]]
    for k in range(4):
        outs += [res[n][k] for n in WEIGHT_ORDER]
    return tuple(outs)
```

```python
import functools

import jax
import jax.numpy as jnp
from jax import lax
from jax.experimental import pallas as pl
from jax.experimental.pallas import tpu as pltpu

F32 = jnp.float32
BF16 = jnp.bfloat16
MESH = pl.DeviceIdType.MESH

LANES = 128
HG_DK = 128
HG_CHUNK = 64
HG_SUB = 16
HG_ROWS = 512
ATT_HD = 64
ATT_G = 4
WINDOW = 128
DEPTH = 2
ALPHA = (2.0 * DEPTH) ** 0.25
LN_EPS = 1e-5
RMS_EPS = 1e-6
ADAM_LR, ADAM_B1, ADAM_B2, ADAM_EPS, ADAM_WD, ADAM_STEP = 0.001, 0.9, 0.999, 1e-08, 0.01, 10
N_CHIPS = 4
N_DEV = 8
VMEM_LIMIT = 56 * 1024 * 1024
NEG = -1e30


def _pick(n, cap):
    best = None
    for d in range(LANES, min(n, cap) + 1, LANES):
        if n % d == 0:
            best = d
    return n if best is None else best


def _pick_rows(m, cap):
    best = None
    for d in range(16, min(m, cap) + 1, 16):
        if m % d == 0:
            best = d
    return m if best is None else best


def _params(sem):
    return pltpu.CompilerParams(dimension_semantics=sem, vmem_limit_bytes=VMEM_LIMIT)


def _zeros_index(ndim):
    return lambda i, j, kk: (0,) * ndim


def _mm(a, b, *, name, la=None, lb=None, ta=False, tb=False, bias=None, add=None, out_dtype=F32,
        out_layers=None, out_layer=None, after=None, post=None, tile_cols=None, caps=(1024, 1536, 2048)):
    ar, ac = a.shape[-2:]
    br, bc = b.shape[-2:]
    m, k = (ac, ar) if ta else (ar, ac)
    k2, n = (bc, br) if tb else (br, bc)
    assert k == k2, (a.shape, b.shape, ta, tb)
    if post is not None:
        caps = (512, n if tile_cols is None else tile_cols, caps[2])
    tm, tn, tk = _pick(m, caps[0]), _pick(n, caps[1]), _pick(k, caps[2])
    assert post is None or tn == caps[1]
    nk = k // tk
    grid = (m // tm, n // tn, nk)

    def spec(block, idx, layer):
        if layer is None:
            return pl.BlockSpec(block, idx)
        return pl.BlockSpec((None,) + block, lambda i, j, kk: (layer,) + idx(i, j, kk))

    a_spec = spec((tk, tm), lambda i, j, kk: (kk, i), la) if ta else spec((tm, tk), lambda i, j, kk: (i, kk), la)
    b_spec = spec((tn, tk), lambda i, j, kk: (j, kk), lb) if tb else spec((tk, tn), lambda i, j, kk: (kk, j), lb)
    in_specs, operands = [a_spec, b_spec], [a, b]
    if bias is not None:
        in_specs.append(pl.BlockSpec((1, tn), lambda i, j, kk: (0, j)))
        operands.append(bias)
    if add is not None:
        in_specs.append(pl.BlockSpec((tm, tn), lambda i, j, kk: (i, j)))
        operands.append(add)
    if after is not None:
        in_specs.append(pl.BlockSpec(memory_space=pl.ANY))
        operands.append(after)
    dims = (((0 if ta else 1,), (1 if tb else 0,)), ((), ()))
    has_bias, has_add = bias is not None, add is not None
    if post is None:
        fn, rows, whole, outs, sums = None, [], [], [], []
        out_shape = jax.ShapeDtypeStruct((m, n) if out_layers is None else (out_layers, m, n), out_dtype)
        out_specs = spec((tm, tn), lambda i, j, kk: (i, j), out_layer)
    else:
        fn, rows, whole, outs, sums = post
        in_specs += [pl.BlockSpec((tm, r.shape[-1] // (n // tn)), lambda i, j, kk: (i, j)) for r in rows]
        in_specs += [pl.BlockSpec(tuple(w.shape), _zeros_index(w.ndim)) for w in whole]
        operands += list(rows) + list(whole)
        out_shape = [jax.ShapeDtypeStruct(sh, dt) for sh, dt in list(outs) + list(sums)]
        out_specs = ([pl.BlockSpec((tm, sh[-1] // (n // tn)), lambda i, j, kk: (i, j)) for sh, _ in outs]
                     + [pl.BlockSpec(tuple(sh), _zeros_index(len(sh))) for sh, _ in sums])
    n_in, n_extra, n_outs, n_sums = len(operands), len(rows) + len(whole), len(outs), len(sums)

    def body(*refs):
        a_ref, b_ref = refs[0], refs[1]
        pos = 2
        bias_ref = add_ref = None
        if has_bias:
            bias_ref = refs[pos]
            pos += 1
        if has_add:
            add_ref = refs[pos]
            pos += 1
        extra_refs = refs[n_in - n_extra:n_in]
        out_refs = refs[n_in:n_in + max(n_outs, 1)]
        sum_refs = refs[n_in + n_outs:n_in + n_outs + n_sums]
        acc_ref = refs[-1] if nk > 1 else None
        part = lax.dot_general(a_ref[...].astype(BF16), b_ref[...].astype(BF16), dims, preferred_element_type=F32)

        def finish(total):
            if has_bias:
                total = total + bias_ref[...]
            if has_add:
                total = total + add_ref[...]
            if fn is None:
                out_refs[0][...] = total.astype(out_refs[0].dtype)
                return
            res = fn(total, *[r[...] for r in extra_refs])
            for ref, val in zip(out_refs, res[:n_outs]):
                ref[...] = val.astype(ref.dtype)
            if n_sums:
                @pl.when(pl.program_id(0) == 0)
                def _():
                    for ref in sum_refs:
                        ref[...] = jnp.zeros(ref.shape, ref.dtype)

                for ref, val in zip(sum_refs, res[n_outs:]):
                    ref[...] += val

        if nk == 1:
            finish(part)
        else:
            kk = pl.program_id(2)

            @pl.when(kk == 0)
            def _():
                acc_ref[...] = part

            @pl.when(kk > 0)
            def _():
                acc_ref[...] += part

            @pl.when(kk == nk - 1)
            def _():
                finish(acc_ref[...])

    return pl.pallas_call(
        body, name=name, grid=grid, in_specs=in_specs, out_specs=out_specs, out_shape=out_shape,
        scratch_shapes=[pltpu.VMEM((tm, tn), F32)] if nk > 1 else [],
        compiler_params=_params(("arbitrary" if n_sums else "parallel", "parallel", "arbitrary")),
    )(*operands)


def _rowwise(fn, rows, whole, outs, sums=(), *, name, tm=256):
    m = rows[0].shape[-2]
    tm = _pick_rows(m, tm)
    n_rows, n_whole, n_outs, n_sums = len(rows), len(whole), len(outs), len(sums)

    def rspec(shape):
        lead = len(shape) - 2
        return pl.BlockSpec(tuple(shape[:-2]) + (tm, shape[-1]), lambda i: (0,) * lead + (i, 0))

    def wspec(shape):
        return pl.BlockSpec(tuple(shape), lambda i: (0,) * len(shape))

    def body(*refs):
        vals = [r[...] for r in refs[:n_rows + n_whole]]
        out_refs = refs[n_rows + n_whole:n_rows + n_whole + n_outs]
        sum_refs = refs[n_rows + n_whole + n_outs:]
        res = fn(*vals)
        for ref, val in zip(out_refs, res[:n_outs]):
            ref[...] = val.astype(ref.dtype)
        if n_sums:
            @pl.when(pl.program_id(0) == 0)
            def _():
                for ref in sum_refs:
                    ref[...] = jnp.zeros(ref.shape, ref.dtype)

            for ref, val in zip(sum_refs, res[n_outs:]):
                ref[...] += val

    result = pl.pallas_call(
        body, name=name, grid=(m // tm,),
        in_specs=[rspec(r.shape) for r in rows] + [wspec(w.shape) for w in whole],
        out_specs=[rspec(s) for s, _ in outs] + [wspec(s) for s, _ in sums],
        out_shape=[jax.ShapeDtypeStruct(s, d) for s, d in list(outs) + list(sums)],
        compiler_params=_params(("arbitrary",)),
    )(*rows, *whole)
    return result


def _sigmoid(v):
    return jax.nn.sigmoid(v)


def _col_sum(v):
    return jnp.sum(v, axis=0, keepdims=True)


def _ln_stats(z):
    mu = jnp.mean(z, axis=-1, keepdims=True)
    zc = z - mu
    var = jnp.mean(zc * zc, axis=-1, keepdims=True)
    rstd = lax.rsqrt(var + LN_EPS)
    return zc * rstd, rstd


def _ln_fwd_fn(xin, h, gain, bias):
    xhat, _ = _ln_stats(ALPHA * xin + h)
    y = xhat * gain + bias
    return y, y


def _ple_ln_fwd_fn(xin, pg, pu, gain, bias):
    xhat, _ = _ln_stats(ALPHA * xin + _sigmoid(pg) * pu)
    y = xhat * gain + bias
    return y, y


def _ln_dz(dy, z, gain):
    xhat, rstd = _ln_stats(z)
    dxhat = dy * gain
    dz = rstd * (dxhat - jnp.mean(dxhat, axis=-1, keepdims=True)
                 - xhat * jnp.mean(dxhat * xhat, axis=-1, keepdims=True))
    return dz, _col_sum(dy * xhat), _col_sum(dy)


def _ln_bwd_fn(dy, xin, h, gain):
    dz, dgain, dbias = _ln_dz(dy, ALPHA * xin + h, gain)
    return ALPHA * dz, dz, dgain, dbias, _col_sum(dz)


def _ple_ln_bwd_fn(dy, xin, pg, pu, gain):
    sg = _sigmoid(pg)
    dz, dgain, dbias = _ln_dz(dy, ALPHA * xin + sg * pu, gain)
    dpg = dz * pu * sg * (1.0 - sg)
    return ALPHA * dz, dpg, dz * sg, dgain, dbias, _col_sum(dpg)


def _swiglu_fwd_fn(gu):
    hid = gu.shape[-1] // 2
    gate, up = gu[:, :hid], gu[:, hid:]
    return gu, gate * _sigmoid(gate) * up


def _swiglu_bwd_fn(dact, gu):
    gu = gu.astype(F32)
    hid = gu.shape[-1] // 2
    gate, up = gu[:, :hid], gu[:, hid:]
    sg = _sigmoid(gate)
    dgate = dact * up * sg * (1.0 + gate * (1.0 - sg))
    dup = dact * gate * sg
    return (jnp.concatenate([dgate, dup], axis=-1),)


def _loss_fn(y, target):
    err = y - target
    inv = 1.0 / y.shape[-1]
    part = 0.5 * inv * jnp.sum(jnp.sum(err * err, axis=-1, keepdims=True), axis=0, keepdims=True)
    return err * inv, jnp.broadcast_to(part, (1, LANES))


def _adam_fn(w, mom, vel, p_own, p_sib):
    g = p_own.astype(F32) + p_sib.astype(F32)
    m_new = ADAM_B1 * mom + (1.0 - ADAM_B1) * g
    v_new = ADAM_B2 * vel + (1.0 - ADAM_B2) * (g * g)
    m_hat = m_new / (1.0 - ADAM_B1 ** ADAM_STEP)
    v_hat = v_new / (1.0 - ADAM_B2 ** ADAM_STEP)
    delta = -ADAM_LR * (m_hat / (jnp.sqrt(v_hat) + ADAM_EPS) + ADAM_WD * w)
    return g, delta, m_new, v_new


def _sum_slots_fn(slots):
    acc = slots[0].astype(F32)
    for s in range(1, slots.shape[0]):
        acc = acc + slots[s].astype(F32)
    return (acc,)


def _split2(x):
    hi = x.astype(BF16)
    return hi, (x - hi.astype(F32)).astype(BF16)


def _dot3(a, b, dims):
    a_hi, a_lo = _split2(a)
    b_hi, b_lo = _split2(b)
    dn = (dims, ((), ()))
    return (lax.dot_general(a_hi, b_hi, dn, preferred_element_type=F32)
            + (lax.dot_general(a_hi, b_lo, dn, preferred_element_type=F32)
               + lax.dot_general(a_lo, b_hi, dn, preferred_element_type=F32)))


def _tdot(mask01, b):
    m = mask01.astype(BF16)
    b_hi = b.astype(BF16)
    rest = b - b_hi.astype(F32)
    b_mid = rest.astype(BF16)
    b_lo = (rest - b_mid.astype(F32)).astype(BF16)
    dn = (((1,), (0,)), ((), ()))
    return (lax.dot_general(m, b_hi, dn, preferred_element_type=F32)
            + (lax.dot_general(m, b_mid, dn, preferred_element_type=F32)
               + lax.dot_general(m, b_lo, dn, preferred_element_type=F32)))


def _hdot(a, b):
    return _dot3(a, b, ((1,), (0,)))


def _hdot_nt(a, b):
    return _dot3(a, b, ((1,), (1,)))


def _hdot_tn(a, b):
    return _dot3(a, b, ((0,), (0,)))


def _dot(a, b):
    return lax.dot_general(a.astype(BF16), b.astype(BF16), (((1,), (0,)), ((), ())), preferred_element_type=F32)


def _dot_nt(a, b):
    return lax.dot_general(a.astype(BF16), b.astype(BF16), (((1,), (1,)), ((), ())), preferred_element_type=F32)


def _dot_tn(a, b):
    return lax.dot_general(a.astype(BF16), b.astype(BF16), (((0,), (0,)), ((), ())), preferred_element_type=F32)


def _hg_masks():
    c = HG_CHUNK
    row = lax.broadcasted_iota(jnp.int32, (c, c), 0)
    col = lax.broadcasted_iota(jnp.int32, (c, c), 1)
    base = row & (-HG_SUB)
    return row, col, base, col <= row, col < base


def _hg_gates(qr, fr, alb):
    lbound = _sigmoid(alb[0:1, :] - alb[1:2, :])
    sig = _sigmoid(fr)
    forget = lbound + (1.0 - lbound) * sig
    kk = (1.0 - lbound) * _sigmoid(-fr)
    qt = qr * _sigmoid(qr) * (HG_DK ** -0.5)
    return qt, kk, jnp.log(forget), lbound, sig, forget


def _hg_scores(qt, kk, g, scores=True):
    c, nsub = HG_CHUNK, HG_CHUNK // HG_SUB
    row, col, base, causal, below = _hg_masks()
    b = _tdot(causal, g)
    rr = _tdot(below, g)
    bq = b - rr
    qh = qt * jnp.exp(bq)
    edecs = [None]
    parts = [jnp.zeros((HG_SUB, c), F32)]
    for i in range(1, nsub):
        edec = jnp.exp(jnp.minimum(rr[i * HG_SUB:i * HG_SUB + 1, :] - b, 0.0))
        edecs.append(edec)
        if scores:
            parts.append(_dot_nt(qh[i * HG_SUB:(i + 1) * HG_SUB, :], kk * edec))
    b3 = b.reshape(nsub, HG_SUB, HG_DK)
    q3 = qt.reshape(nsub, HG_SUB, HG_DK)
    k3 = kk.reshape(nsub, HG_SUB, HG_DK)
    if not scores:
        return None, b, bq, qh, edecs, (b3, q3, k3)
    a = jnp.where(below, jnp.concatenate(parts, axis=0), 0.0)
    for j in range(HG_SUB):
        e = jnp.exp(jnp.minimum(b3 - b3[:, j:j + 1, :], 0.0))
        colv = jnp.sum(q3 * e * k3[:, j:j + 1, :], axis=-1, keepdims=True).reshape(c, 1)
        a = jnp.where(col == base + j, colv, a)
    a = jnp.where(causal, a, 0.0)
    return a, b, bq, qh, edecs, (b3, q3, k3)


def _hg_norm(o, gr, gain):
    r = lax.rsqrt(jnp.mean(o * o, axis=-1, keepdims=True) + RMS_EPS)
    sg = _sigmoid(gr)
    return o * r * gain, r, sg


def _hgrn2_fwd(proj, alb, gain, *, rb):
    m, d4 = proj.shape
    d = d4 // 4
    heads = d // HG_DK
    rb = min(rb, m)
    cpb = rb // HG_CHUNK
    nrb = m // rb

    def body(q_ref, f_ref, v_ref, g_ref, alb_ref, gain_ref, o_ref, og_ref, st_ref, a_ref, state):
        @pl.when(pl.program_id(1) == 0)
        def _():
            state[...] = jnp.zeros(state.shape, F32)

        def chunk(ci, carry):
            sl = pl.ds(pl.multiple_of(ci * HG_CHUNK, HG_CHUNK), HG_CHUNK)
            qt, kk, g, _, _, _ = _hg_gates(q_ref[sl, :], f_ref[sl, :], alb_ref[...])
            v = v_ref[sl, :]
            st = state[...]
            st_ref[ci] = st
            a, b, _, _, _, _ = _hg_scores(qt, kk, g)
            a_ref[ci] = a.astype(a_ref.dtype)
            o = _dot(a, v) + _dot_nt(qt * jnp.exp(b), st)
            b_last = b[HG_CHUNK - 1:HG_CHUNK, :]
            state[...] = st * jnp.exp(b_last) + _hdot_tn(v, kk * jnp.exp(b_last - b))
            o_ref[sl, :] = o
            n, _, sg = _hg_norm(o, g_ref[sl, :], gain_ref[...])
            og_ref[sl, :] = (n * g_ref[sl, :] * sg).astype(og_ref.dtype)
            return carry

        lax.fori_loop(0, cpb, chunk, 0, unroll=2)

    def col(cidx):
        return pl.BlockSpec((rb, HG_DK), lambda h, r: (r, cidx * heads + h))

    return pl.pallas_call(
        body, name="hgrn2_fwd", grid=(heads, nrb),
        in_specs=[col(0), col(1), col(2), col(3),
                  pl.BlockSpec((2, HG_DK), lambda h, r: (0, h)),
                  pl.BlockSpec((1, HG_DK), lambda h, r: (0, 0))],
        out_specs=[pl.BlockSpec((rb, HG_DK), lambda h, r: (r, h)),
                   pl.BlockSpec((rb, HG_DK), lambda h, r: (r, h)),
                   pl.BlockSpec((None, cpb, HG_DK, HG_DK), lambda h, r: (h, r, 0, 0)),
                   pl.BlockSpec((None, cpb, HG_CHUNK, HG_CHUNK), lambda h, r: (h, r, 0, 0))],
        out_shape=[jax.ShapeDtypeStruct((m, d), F32), jax.ShapeDtypeStruct((m, d), BF16),
                   jax.ShapeDtypeStruct((heads, m // HG_CHUNK, HG_DK, HG_DK), F32),
                   jax.ShapeDtypeStruct((heads, m // HG_CHUNK, HG_CHUNK, HG_CHUNK), BF16)],
        scratch_shapes=[pltpu.VMEM((HG_DK, HG_DK), F32)],
        compiler_params=_params(("parallel", "arbitrary")),
    )(proj, proj, proj, proj, alb, gain)


def _hgrn2_bwd(proj, o_pre, states, scores, dog, alb, gain, *, rb):
    m, d4 = proj.shape
    d = d4 // 4
    heads = d // HG_DK
    rb = min(rb, m)
    cpb = rb // HG_CHUNK
    nrb = m // rb
    c, nsub = HG_CHUNK, HG_CHUNK // HG_SUB

    def body(q_ref, f_ref, v_ref, g_ref, o_ref, st_ref, a_ref, dog_ref, alb_ref, gain_ref,
             dq_ref, df_ref, dv_ref, dg_ref, dalb_ref, dgain_ref, dstate, carry_ref):
        first = (pl.program_id(0) == 0) & (pl.program_id(1) == 0)

        @pl.when(first)
        def _():
            dgain_ref[...] = jnp.zeros(dgain_ref.shape, F32)

        @pl.when(pl.program_id(1) == 0)
        def _():
            dstate[...] = jnp.zeros(dstate.shape, F32)
            carry_ref[...] = jnp.zeros(carry_ref.shape, F32)
            dalb_ref[...] = jnp.zeros(dalb_ref.shape, F32)

        row, col, base, causal, below = _hg_masks()
        sub_iota = lax.broadcasted_iota(jnp.int32, (nsub, HG_SUB, HG_DK), 1)
        row_k = lax.broadcasted_iota(jnp.int32, (c, HG_DK), 0)
        upper = col >= row

        def chunk(step, carry):
            ci = cpb - 1 - step
            sl = pl.ds(pl.multiple_of(ci * HG_CHUNK, HG_CHUNK), HG_CHUNK)
            qr, fr, v, gr = q_ref[sl, :], f_ref[sl, :], v_ref[sl, :], g_ref[sl, :]
            qt, kk, g, lbound, sig, forget = _hg_gates(qr, fr, alb_ref[...])
            o = o_ref[sl, :]
            dogv = dog_ref[sl, :]
            gain_v = gain_ref[...]
            n, r, sg = _hg_norm(o, gr, gain_v)
            dgr = dogv * n * sg * (1.0 + gr * (1.0 - sg))
            dn = dogv * gr * sg
            dgain_ref[...] += _col_sum(dn * o * r)
            u = dn * gain_v
            d_o = r * u - o * (r * r * r) * jnp.mean(u * o, axis=-1, keepdims=True)
            st0 = st_ref[ci]
            dst = dstate[...]
            _, b, bq, qh, edecs, (b3, q3, k3) = _hg_scores(qt, kk, g, scores=False)
            a = a_ref[ci]
            eb = jnp.exp(b)
            b_last = b[c - 1:c, :]
            kdl_dec = jnp.exp(b_last - b)
            kdl = kk * kdl_dec
            d_a = jnp.where(causal, _dot_nt(d_o, v), 0.0)
            d_at = _dot_nt(v, d_o)
            dv = _dot_tn(a, d_o) + _dot_nt(kdl, dst)
            dq = eb * _hdot(d_o, st0)
            dk = _hdot(v, dst) * kdl_dec
            d_a_below = jnp.where(below, d_a, 0.0)
            dq_parts = [jnp.zeros((HG_SUB, HG_DK), F32)]
            for i in range(1, nsub):
                lo, hi = i * HG_SUB, (i + 1) * HG_SUB
                dq_parts.append(_hdot(d_a_below[lo:hi, :], kk * edecs[i]))
                gi = _hdot(d_at[:, lo:hi], qh[lo:hi, :])
                dk = dk + jnp.where(row_k < lo, edecs[i] * gi, 0.0)
            dq = dq + jnp.concatenate(dq_parts, axis=0) * jnp.exp(bq)
            dq3 = jnp.zeros((nsub, HG_SUB, HG_DK), F32)
            dk3 = jnp.zeros((nsub, HG_SUB, HG_DK), F32)
            for j in range(HG_SUB):
                e = jnp.exp(jnp.minimum(b3 - b3[:, j:j + 1, :], 0.0))
                dcol = jnp.sum(jnp.where(col == base + j, d_a, 0.0), axis=-1, keepdims=True)
                t1 = dcol.reshape(nsub, HG_SUB, 1) * e
                dq3 = dq3 + t1 * k3[:, j:j + 1, :]
                dk3 = jnp.where(sub_iota == j, jnp.sum(t1 * q3, axis=1, keepdims=True), dk3)
            dq = dq + dq3.reshape(c, HG_DK)
            dk = dk + dk3.reshape(c, HG_DK)
            dstate[...] = dst * jnp.exp(b_last) + _hdot_tn(d_o, qt * eb)
            dglog = _tdot(upper, qt * dq - kk * dk) + carry_ref[...]
            carry_ref[...] = dglog[0:1, :]
            dforget = dglog / forget
            one_m_lb = 1.0 - lbound
            dsig = (dforget - dk) * one_m_lb
            sneg = _sigmoid(-fr)
            dlb = _col_sum(dforget * (1.0 - sig) - dk * sneg)
            dalb0 = dlb * lbound * one_m_lb
            dalb_ref[...] += jnp.concatenate([dalb0, -dalb0], axis=0)
            sq = _sigmoid(qr)
            dq_ref[sl, :] = (dq * (HG_DK ** -0.5) * sq * (1.0 + qr * (1.0 - sq))).astype(dq_ref.dtype)
            df_ref[sl, :] = (dsig * sig * (1.0 - sig)).astype(df_ref.dtype)
            dv_ref[sl, :] = dv.astype(dv_ref.dtype)
            dg_ref[sl, :] = dgr.astype(dg_ref.dtype)
            return carry

        lax.fori_loop(0, cpb, chunk, 0, unroll=2)

    def rev(r):
        return nrb - 1 - r

    def col(cidx):
        return pl.BlockSpec((rb, HG_DK), lambda h, r: (rev(r), cidx * heads + h))

    def head_rows():
        return pl.BlockSpec((rb, HG_DK), lambda h, r: (rev(r), h))

    return pl.pallas_call(
        body, name="hgrn2_bwd", grid=(heads, nrb),
        in_specs=[col(0), col(1), col(2), col(3), head_rows(),
                  pl.BlockSpec((None, cpb, HG_DK, HG_DK), lambda h, r: (h, rev(r), 0, 0)),
                  pl.BlockSpec((None, cpb, HG_CHUNK, HG_CHUNK), lambda h, r: (h, rev(r), 0, 0)),
                  head_rows(),
                  pl.BlockSpec((2, HG_DK), lambda h, r: (0, h)),
                  pl.BlockSpec((1, HG_DK), lambda h, r: (0, 0))],
        out_specs=[head_rows(), head_rows(), head_rows(), head_rows(),
                   pl.BlockSpec((2, HG_DK), lambda h, r: (0, h)),
                   pl.BlockSpec((1, HG_DK), lambda h, r: (0, 0))],
        out_shape=[jax.ShapeDtypeStruct((m, d), BF16)] * 4
                  + [jax.ShapeDtypeStruct((2, d), F32), jax.ShapeDtypeStruct((1, HG_DK), F32)],
        scratch_shapes=[pltpu.VMEM((HG_DK, HG_DK), F32), pltpu.VMEM((1, HG_DK), F32)],
        compiler_params=_params(("arbitrary", "arbitrary")),
    )(proj, proj, proj, proj, o_pre, states, scores, dog, alb, gain)


def _swa_probs(qh, kp, kc, sink, slope, has_prev):
    rows = qh.shape[0]
    qi = lax.broadcasted_iota(jnp.int32, (rows, WINDOW), 0) & (WINDOW - 1)
    si = lax.broadcasted_iota(jnp.int32, (rows, WINDOW), 1)
    scale = ATT_HD ** -0.5
    dist_c = (qi - si).astype(F32)
    s_p = _dot_nt(qh, kp) * scale - slope * (dist_c + float(WINDOW))
    s_c = _dot_nt(qh, kc) * scale - slope * dist_c
    s_p = jnp.where((si > qi) & has_prev, s_p, NEG)
    s_c = jnp.where(si <= qi, s_c, NEG)
    mx = jnp.maximum(jnp.maximum(jnp.max(s_p, axis=-1, keepdims=True), jnp.max(s_c, axis=-1, keepdims=True)), sink)
    e_p, e_c, e_s = jnp.exp(s_p - mx), jnp.exp(s_c - mx), jnp.exp(sink - mx)
    inv = 1.0 / (jnp.sum(e_p, axis=-1, keepdims=True) + jnp.sum(e_c, axis=-1, keepdims=True) + e_s)
    return e_p * inv, e_c * inv, e_s * inv


def _slope(h, n_heads):
    return float(2.0 ** (-8.0 * (h + 1) / n_heads))


def _swa_group(ref_vals, sink_ref, kh, n_heads):
    heads = [kh * ATT_G + g for g in range(ATT_G)]
    stacked = [jnp.concatenate([v[:, h * ATT_HD:(h + 1) * ATT_HD] for h in heads], axis=0) for v in ref_vals]
    grp = lax.shift_right_logical(lax.broadcasted_iota(jnp.int32, (ATT_G * WINDOW, 1), 0), WINDOW.bit_length() - 1)
    slope = jnp.zeros((ATT_G * WINDOW, 1), F32)
    sink = jnp.zeros((ATT_G * WINDOW, 1), F32)
    for g, h in enumerate(heads):
        slope = jnp.where(grp == g, _slope(h, n_heads), slope)
        sink = jnp.where(grp == g, sink_ref[:, h:h + 1], sink)
    return stacked, slope, sink


def _swa_fwd(q, kv, sinks):
    m, d = q.shape
    n_heads = d // ATT_HD
    kvh = n_heads // ATT_G
    kd = kvh * ATT_HD
    nb = m // WINDOW

    def body(q_ref, kvp_ref, kvc_ref, sink_ref, o_ref):
        has_prev = pl.program_id(0) > 0
        qv, kvp, kvc = q_ref[...], kvp_ref[...], kvc_ref[...]
        outs = []
        for kh in range(kvh):
            ks = slice(kh * ATT_HD, (kh + 1) * ATT_HD)
            vs = slice(kd + kh * ATT_HD, kd + (kh + 1) * ATT_HD)
            (q4,), slope, sink = _swa_group([qv], sink_ref, kh, n_heads)
            p_p, p_c, _ = _swa_probs(q4, kvp[:, ks], kvc[:, ks], sink, slope, has_prev)
            o4 = _dot(p_p, kvp[:, vs]) + _dot(p_c, kvc[:, vs])
            outs += [o4[g * WINDOW:(g + 1) * WINDOW, :] for g in range(ATT_G)]
        o_ref[...] = jnp.concatenate(outs, axis=-1).astype(o_ref.dtype)

    return pl.pallas_call(
        body, name="swa_fwd", grid=(nb,),
        in_specs=[pl.BlockSpec((WINDOW, d), lambda n: (n, 0)),
                  pl.BlockSpec((WINDOW, 2 * kd), lambda n: (jnp.maximum(n - 1, 0), 0)),
                  pl.BlockSpec((WINDOW, 2 * kd), lambda n: (n, 0)),
                  pl.BlockSpec((1, n_heads), lambda n: (0, 0))],
        out_specs=pl.BlockSpec((WINDOW, d), lambda n: (n, 0)),
        out_shape=jax.ShapeDtypeStruct((m, d), BF16),
        compiler_params=_params(("arbitrary",)),
    )(q, kv, kv, sinks)


def _swa_bwd(q, kv, sinks, dao):
    m, d = q.shape
    n_heads = d // ATT_HD
    kvh = n_heads // ATT_G
    kd = kvh * ATT_HD
    nb = m // WINDOW
    scale = ATT_HD ** -0.5

    def body(q_ref, kvp_ref, kvc_ref, sink_ref, do_ref, dq_ref, dkvc_ref, dkvp_ref, dqsum_ref, dsink_ref):
        @pl.when(pl.program_id(0) == 0)
        def _():
            dqsum_ref[...] = jnp.zeros(dqsum_ref.shape, F32)
            dsink_ref[...] = jnp.zeros(dsink_ref.shape, F32)

        has_prev = pl.program_id(0) > 0
        qv, kvp, kvc, dov = q_ref[...], kvp_ref[...], kvc_ref[...], do_ref[...]
        lane_h = lax.broadcasted_iota(jnp.int32, (1, n_heads), 1)
        dsink = jnp.zeros((1, n_heads), F32)
        dq_parts, dk_p, dk_c, dv_p, dv_c = [], [], [], [], []
        for kh in range(kvh):
            ks = slice(kh * ATT_HD, (kh + 1) * ATT_HD)
            vs = slice(kd + kh * ATT_HD, kd + (kh + 1) * ATT_HD)
            kp, kc, vp, vc = kvp[:, ks], kvc[:, ks], kvp[:, vs], kvc[:, vs]
            (q4, do4), slope, sink = _swa_group([qv, dov], sink_ref, kh, n_heads)
            p_p, p_c, p_s = _swa_probs(q4, kp, kc, sink, slope, has_prev)
            dp_p, dp_c = _dot_nt(do4, vp), _dot_nt(do4, vc)
            delta = jnp.sum(p_p * dp_p, axis=-1, keepdims=True) + jnp.sum(p_c * dp_c, axis=-1, keepdims=True)
            ds_p, ds_c = p_p * (dp_p - delta), p_c * (dp_c - delta)
            sink_term = p_s * delta
            dq4 = (_dot(ds_p, kp) + _dot(ds_c, kc)) * scale
            for g in range(ATT_G):
                rows = slice(g * WINDOW, (g + 1) * WINDOW)
                dsink = dsink + jnp.where(lane_h == kh * ATT_G + g, -_col_sum(sink_term[rows, :]), 0.0)
                dq_parts.append(dq4[rows, :])
            dk_p.append(_dot_tn(ds_p, q4) * scale)
            dk_c.append(_dot_tn(ds_c, q4) * scale)
            dv_p.append(_dot_tn(p_p, do4))
            dv_c.append(_dot_tn(p_c, do4))
        dq = jnp.concatenate(dq_parts, axis=-1)
        dq_ref[...] = dq.astype(dq_ref.dtype)
        dqsum_ref[...] += _col_sum(dq)
        dsink_ref[...] += dsink
        dkvc_ref[...] = jnp.concatenate(dk_c + dv_c, axis=-1)
        dkvp_ref[...] = jnp.concatenate(dk_p + dv_p, axis=-1)

    return pl.pallas_call(
        body, name="swa_bwd", grid=(nb,),
        in_specs=[pl.BlockSpec((WINDOW, d), lambda n: (n, 0)),
                  pl.BlockSpec((WINDOW, 2 * kd), lambda n: (jnp.maximum(n - 1, 0), 0)),
                  pl.BlockSpec((WINDOW, 2 * kd), lambda n: (n, 0)),
                  pl.BlockSpec((1, n_heads), lambda n: (0, 0)),
                  pl.BlockSpec((WINDOW, d), lambda n: (n, 0))],
        out_specs=[pl.BlockSpec((WINDOW, d), lambda n: (n, 0)),
                   pl.BlockSpec((WINDOW, 2 * kd), lambda n: (n, 0)),
                   pl.BlockSpec((WINDOW, 2 * kd), lambda n: (n, 0)),
                   pl.BlockSpec((1, d), lambda n: (0, 0)),
                   pl.BlockSpec((1, n_heads), lambda n: (0, 0))],
        out_shape=[jax.ShapeDtypeStruct((m, d), BF16), jax.ShapeDtypeStruct((m, 2 * kd), F32),
                   jax.ShapeDtypeStruct((m, 2 * kd), F32), jax.ShapeDtypeStruct((1, d), F32),
                   jax.ShapeDtypeStruct((1, n_heads), F32)],
        compiler_params=_params(("arbitrary",)),
    )(q, kv, kv, sinks, dao)


def _kv_grad_combine(dkv_cur, dkv_prev):
    m, w = dkv_cur.shape
    nb = m // WINDOW

    def body(cur_ref, nxt_ref, o_ref, sum_ref):
        @pl.when(pl.program_id(0) == 0)
        def _():
            sum_ref[...] = jnp.zeros(sum_ref.shape, F32)

        total = cur_ref[...] + jnp.where(pl.program_id(0) < nb - 1, nxt_ref[...], 0.0)
        o_ref[...] = total.astype(o_ref.dtype)
        sum_ref[...] += _col_sum(total)

    return pl.pallas_call(
        body, name="kv_grad_combine", grid=(nb,),
        in_specs=[pl.BlockSpec((WINDOW, w), lambda n: (n, 0)),
                  pl.BlockSpec((WINDOW, w), lambda n: (jnp.minimum(n + 1, nb - 1), 0))],
        out_specs=[pl.BlockSpec((WINDOW, w), lambda n: (n, 0)), pl.BlockSpec((1, w), lambda n: (0, 0))],
        out_shape=[jax.ShapeDtypeStruct((m, w), BF16), jax.ShapeDtypeStruct((1, w), F32)],
        compiler_params=_params(("arbitrary",)),
    )(dkv_cur, dkv_prev)


def _row(v):
    return v.reshape(1, -1)


def _local_step(x, p, target, wget, grad_sink, ln_gain, ln_bias, alb, norm_gain, kv_b, b_q, sinks, b_out, ple_b):
    gs = {}
    gains = [[_row(ln_gain[i, j]) for j in range(3)] for i in range(DEPTH)]
    biases = [[_row(ln_bias[i, j]) for j in range(3)] for i in range(DEPTH)]
    sd = x.shape
    pending = [None]

    def mm(a, b, lb=0, **kw):
        after, pending[0] = pending[0], None
        return _mm(a, b, lb=lb, after=after, **kw)

    def mm_ln(a, wt, xin, i, j, nm, bias=None, pu=None):
        if pu is None:
            fn, rows = (lambda h, xv, g, bv: (h,) + _ln_fwd_fn(xv, h, g, bv)), [xin]
        else:
            fn, rows = (lambda h, xv, puv, g, bv: (h,) + _ple_ln_fwd_fn(xv, h, puv, g, bv)), [xin, pu]
        h, y, yb = _mm(a, wt, lb=0, bias=bias, name=nm,
                       post=(fn, rows, [gains[i][j], biases[i][j]], [(sd, F32), (sd, F32), (sd, BF16)], []))
        return h, (y, yb)

    def mm_ln_bwd(a, wt, add, xin, h, i, j, nm):
        dx_part, dh, dg, db, dhsum = mm(a, wt, tb=True, add=add, name=nm,
                                        post=(_ln_bwd_fn, [xin, h], [gains[i][j]], [(sd, F32), (sd, BF16)],
                                              [((1, sd[1]), F32)] * 3))
        gs[f"ln_gain_{i}_{j}"], gs[f"ln_bias_{i}_{j}"] = dg, db
        return dx_part, dh, dhsum

    def tail_fwd(xa, i):
        wgu = wget("ffn_w_gate_up", i, xa[1])
        hid2 = wgu.shape[-1]
        gu, act = _mm(xa[1], wgu, lb=0, name=f"ffn_up_swiglu{i}", tile_cols=hid2 // 2,
                      post=(_swiglu_fwd_fn, [], [], [((sd[0], hid2), BF16), ((sd[0], hid2 // 2), BF16)], []))
        f, xb = mm_ln(act, wget("ffn_w_down", i, act), xa[0], i, 1, f"ffn_down_ln{i}")
        pu = _mm(p[i], wget("ple_w_up", i, act), lb=0, name=f"ple_up{i}")
        pg, xc = mm_ln(xb[1], wget("ple_w_gate", i, act), xb[0], i, 2, f"ple_gate_ln{i}", bias=_row(ple_b[i]), pu=pu)
        return dict(xa=xa, gu=gu, act=act, f=f, xb=xb, pg=pg, pu=pu), xc

    def tail_bwd(head, sv, i, mix_in, mix_h):
        xa, xb = sv["xa"], sv["xb"]
        dxb_part, dpg, dpu, dg2, db2, dbg = head(
            _ple_ln_bwd_fn, [xb[0], sv["pg"], sv["pu"]], [gains[i][2]],
            [(sd, F32), (sd, BF16), (sd, BF16)], [((1, sd[1]), F32)] * 3)[:6]
        gs[f"ple_b_{i}"] = dbg
        gs[f"ln_gain_{i}_2"], gs[f"ln_bias_{i}_2"] = dg2, db2
        grad_of("ple_w_gate", i, xb[1], dpg)
        grad_of("ple_w_up", i, p[i], dpu)
        dxa_part, df, _ = mm_ln_bwd(dpg, wget("ple_w_gate", i, None), dxb_part, xa[0], sv["f"], i, 1,
                                    f"ple_gate_dx_ln{i}")
        grad_of("ffn_w_down", i, sv["act"], df)
        gu = sv["gu"]
        dgu, = mm(df, wget("ffn_w_down", i, None), tb=True, name=f"ffn_down_dx_swiglu{i}", tile_cols=gu.shape[1] // 4,
                  post=(_swiglu_bwd_fn, [gu], [], [(gu.shape, BF16)], []))
        grad_of("ffn_w_gate_up", i, xa[1], dgu)
        return mm_ln_bwd(dgu, wget("ffn_w_gate_up", i, None), dxa_part, mix_in, mix_h, i, 0, f"ffn_up_dx_ln{i}")

    def grad_of(nm, i, act, dout):
        grad = mm(act, dout, lb=None, ta=True, out_dtype=BF16, out_layers=1, out_layer=0, name=f"grad_{nm}{i}")
        token = grad_sink(nm, i, grad)
        if token is not None:
            pending[0] = token

    proj = _mm(x, wget("a_w_in", 0, None), lb=0, name="hg_proj")
    o_pre, og, states, scores = _hgrn2_fwd(proj, alb, norm_gain, rb=HG_ROWS)
    h0, x1 = mm_ln(og, wget("a_w_out", 0, og), x, 0, 0, "hg_out_ln")
    sv0, x3 = tail_fwd(x1, 0)
    kv = _mm(x3[1], wget("kv_w", 0, x3[1]), lb=0, bias=_row(kv_b), name="kv_proj")
    q = _mm(x3[1], wget("b_w_q", 0, x3[1]), lb=0, bias=b_q, name="q_proj")
    ao = _swa_fwd(q, kv, sinks)
    h1, x4 = mm_ln(ao, wget("b_w_out", 0, x3[1]), x3[0], 1, 0, "att_out_ln", bias=b_out)
    sv1, y = tail_fwd(x4, 1)

    loss_box = []

    def loss_head(fn, rows, whole, outs, sums):
        def with_loss(yv, tv, *rest):
            dy, part = _loss_fn(yv, tv)
            return fn(dy, *rest) + (part,)

        res = _rowwise(with_loss, [y[0], target] + rows, whole, outs, list(sums) + [((1, LANES), F32)],
                       name="loss_ln_ple_bwd1")
        loss_box.append(res[-1])
        return res

    dx3_part, dh1, dh1sum = tail_bwd(loss_head, sv1, 1, x3[0], h1)
    loss = loss_box[0]
    gs["b_out"] = dh1sum
    grad_of("b_w_out", 0, ao, dh1)
    dao = mm(dh1, wget("b_w_out", 0, None), tb=True, name="att_out_dx")
    dq, dkv_cur, dkv_prev, dqsum, dsinks = _swa_bwd(q, kv, sinks, dao)
    gs["b_q"], gs["sinks"] = dqsum, dsinks
    dkv, dkvsum = _kv_grad_combine(dkv_cur, dkv_prev)
    gs["kv_b"] = dkvsum
    grad_of("b_w_q", 0, x3[1], dq)
    grad_of("kv_w", 0, x3[1], dkv)
    dx3 = mm(dq, wget("b_w_q", 0, None), tb=True, add=dx3_part, name="q_proj_dx")

    def kv_head(*post):
        return mm(dkv, wget("kv_w", 0, None), tb=True, add=dx3, name="kv_proj_dx_ln_ple_bwd0", post=post)

    dx_part, dh0, _ = tail_bwd(kv_head, sv0, 0, x, h0)
    grad_of("a_w_out", 0, og, dh0)
    dog = mm(dh0, wget("a_w_out", 0, None), tb=True, name="hg_out_dx")
    dqr, dfr, dvr, dgr, dalb, dgain = _hgrn2_bwd(proj, o_pre, states, scores, dog, alb, norm_gain, rb=HG_ROWS)
    gs["alb"], gs["norm_gain"] = dalb, dgain
    dproj = jnp.concatenate([dqr, dfr, dvr, dgr], axis=1)
    grad_of("a_w_in", 0, x, dproj)
    grad_x = mm(dproj, wget("a_w_in", 0, None), tb=True, add=dx_part, name="hg_proj_dx")
    return loss, grad_x, gs


HBM_SPEC = pl.BlockSpec(memory_space=pl.ANY)
HBM_ONLY = pl.BlockSpec(memory_space=pltpu.HBM)
SEM_SPEC = pl.BlockSpec(memory_space=pltpu.SEMAPHORE)
SIDE_EFFECT = pltpu.SideEffectType.DATAFLOW_SIDE_EFFECTING


def _slot(kind, j):
    return (j % 2) * 2 + j // 2 if kind == "colp" else j


def _piece(ref, kind, j):
    _, r, c = ref.shape
    if kind == "row":
        return ref.at[:, pl.ds(j * (r // N_CHIPS), r // N_CHIPS), :]
    return ref.at[:, :, pl.ds(_slot(kind, j) * (c // N_CHIPS), c // N_CHIPS)]


def _chip_of(j, c):
    return (j // 2, j % 2, c)


def _in_hbm(a):
    return pltpu.with_memory_space_constraint(a, pltpu.HBM)


def _place(src, kind, chip, *, mode, name, out_dtype, zone=None, zone_shape=None, layer=0, after=None):
    if mode == "gather":
        _, r, c = src.shape
        out_shape = (1, r * N_CHIPS, c) if kind == "row" else (1, r, c * N_CHIPS)
    else:
        out_shape = tuple(zone.shape) if zone is not None else tuple(zone_shape)
        r, c = out_shape[-2:]
    tm = _pick_rows(r, 512)
    nb = r // tm

    def full_idx(i, chip_ref):
        return (0, chip_ref[0] * nb + i, 0) if kind == "row" else (0, i, _slot(kind, chip_ref[0]))

    if mode == "gather":
        in_spec = pl.BlockSpec((None, tm, c), lambda i, chip_ref: (0, i, 0))
        out_spec = pl.BlockSpec((None, tm, c), full_idx)
    else:
        in_spec = pl.BlockSpec((None, tm, c), full_idx)
        out_spec = pl.BlockSpec((None, None, tm, c), lambda i, chip_ref: (chip_ref[0], layer, i, 0))
    in_specs, operands, aliases = [in_spec], [src], {}
    if zone is not None:
        in_specs.append(HBM_SPEC)
        operands.append(zone)
        aliases = {2: 0}
    if after is not None:
        in_specs.append(HBM_SPEC)
        operands.append(after)

    def body(chip_ref, src_ref, *rest):
        rest[-1][...] = src_ref[...].astype(rest[-1].dtype)

    return pl.pallas_call(
        body, name=name,
        grid_spec=pltpu.PrefetchScalarGridSpec(num_scalar_prefetch=1, grid=(nb,), in_specs=in_specs,
                                               out_specs=out_spec),
        out_shape=jax.ShapeDtypeStruct(out_shape, out_dtype),
        input_output_aliases=aliases,
        compiler_params=_params(("arbitrary",)),
    )(chip, *operands)


class _Exchange:
    def __init__(self, mode, srcs, lands, kinds, layers, name, after=None):
        self.mode, self.kinds, self.layers, self.name, self.n = mode, kinds, layers, name, len(lands)
        n, ns = self.n, len(srcs)
        n_in = ns + n + (after is not None)
        sem_shape = pltpu.SemaphoreType.DMA((n * N_CHIPS,))

        def body(*refs):
            src_refs, land_refs = refs[:ns], refs[ns:ns + n]
            send_sems, recv_sems = refs[n_in], refs[n_in + 1]
            token = refs[-1]
            c = lax.axis_index("c")
            me = 2 * lax.axis_index("x") + lax.axis_index("y")
            for j in range(N_CHIPS):
                @pl.when(me == j)
                def _():
                    for a in range(n):
                        for t in range(N_CHIPS):
                            if t != j:
                                src, dst = self._ends(src_refs, land_refs, a, j, t)
                                pltpu.make_async_remote_copy(
                                    src_ref=src, dst_ref=dst, send_sem=send_sems.at[a * N_CHIPS + t],
                                    recv_sem=recv_sems.at[a * N_CHIPS + j],
                                    device_id=_chip_of(t, c), device_id_type=MESH).start()
            token[...] = jnp.zeros(token.shape, token.dtype)

        arrays = list(srcs) + list(lands)
        operands = [_in_hbm(a) for a in arrays]
        in_specs = [HBM_ONLY] * (ns + n)
        if after is not None:
            operands.append(after)
            in_specs.append(HBM_SPEC)
        outs = pl.pallas_call(
            body, name=name + "_start",
            in_specs=in_specs,
            out_specs=[SEM_SPEC, SEM_SPEC] + [HBM_ONLY] * (ns + n) + [pl.BlockSpec(memory_space=pltpu.VMEM)],
            out_shape=[sem_shape, sem_shape] + [pltpu.HBM(a.shape, a.dtype) for a in arrays]
                      + [jax.ShapeDtypeStruct((8, LANES), F32)],
            input_output_aliases={i: i + 2 for i in range(ns + n)},
            compiler_params=pltpu.CompilerParams(has_side_effects=SIDE_EFFECT),
        )(*operands)
        self.send_sems, self.recv_sems = outs[0], outs[1]
        self.srcs, self.lands = list(outs[2:2 + ns]), list(outs[2 + ns:2 + ns + n])
        self.token = outs[-1]

    def _ends(self, src_refs, land_refs, a, me_j, peer):
        if self.mode == "gather":
            mine = _piece(land_refs[a], self.kinds[a], me_j)
            return mine, mine
        return _piece(src_refs[a], self.kinds[a], peer), land_refs[a].at[me_j, pl.ds(self.layers[a], 1)]

    def wait(self, after, lands=None):
        n, ns = self.n, len(self.srcs)
        lands = self.lands if lands is None else lands

        def body(*refs):
            src_refs, land_refs = refs[:ns], refs[ns:ns + n]
            send_sems, recv_sems = refs[ns + n], refs[ns + n + 1]
            c = lax.axis_index("c")
            me = 2 * lax.axis_index("x") + lax.axis_index("y")
            for j in range(N_CHIPS):
                @pl.when(me != j)
                def _():
                    for a in range(n):
                        sent, _ = self._ends(src_refs, land_refs, a, 0, j)
                        _, landed = self._ends(src_refs, land_refs, a, j, 0)
                        cp = pltpu.make_async_remote_copy(
                            src_ref=sent, dst_ref=landed, send_sem=send_sems.at[a * N_CHIPS + j],
                            recv_sem=recv_sems.at[a * N_CHIPS + j],
                            device_id=_chip_of(j, c), device_id_type=MESH)
                        cp.wait_send()
                        cp.wait_recv()

        arrays = self.srcs + list(lands)
        operands = [_in_hbm(a) for a in arrays] + [self.send_sems, self.recv_sems]
        in_specs = [HBM_ONLY] * (ns + n) + [SEM_SPEC, SEM_SPEC]
        if after is not None:
            operands.append(after)
            in_specs.append(HBM_SPEC)
        outs = pl.pallas_call(
            body, name=self.name + "_wait",
            in_specs=in_specs, out_specs=[HBM_ONLY] * (ns + n),
            out_shape=[pltpu.HBM(a.shape, a.dtype) for a in arrays],
            input_output_aliases={i: i for i in range(ns + n)},
            compiler_params=pltpu.CompilerParams(has_side_effects=SIDE_EFFECT),
        )(*operands)
        return list(outs[ns:])


def _sibling_swap(arrays, name):
    n = len(arrays)

    def body(*refs):
        ins, outs = refs[:n], refs[n:2 * n]
        send_sems, recv_sems = refs[2 * n:]
        sibling = (lax.axis_index("x"), lax.axis_index("y"), 1 - lax.axis_index("c"))
        copies = [pltpu.make_async_remote_copy(src_ref=ins[a], dst_ref=outs[a], send_sem=send_sems.at[a],
                                               recv_sem=recv_sems.at[a], device_id=sibling, device_id_type=MESH)
                  for a in range(n)]
        for cp in copies:
            cp.start()
        for cp in copies:
            cp.wait()

    return pl.pallas_call(
        body, name=name,
        in_specs=[HBM_SPEC] * n, out_specs=[HBM_SPEC] * n,
        out_shape=[jax.ShapeDtypeStruct(a.shape, a.dtype) for a in arrays],
        scratch_shapes=[pltpu.SemaphoreType.DMA((n,)), pltpu.SemaphoreType.DMA((n,))],
    )(*arrays)


def _gather_devices(vec):
    def body(in_ref, out_ref, send_sems, recv_sems, local_sem):
        x, y, c = lax.axis_index("x"), lax.axis_index("y"), lax.axis_index("c")
        me = 4 * x + 2 * y + c
        mine = pltpu.make_async_copy(in_ref, out_ref.at[me], local_sem)
        mine.start()
        copies = []
        for rel in range(1, N_DEV):
            peer = (x ^ (rel >> 2), y ^ ((rel >> 1) & 1), c ^ (rel & 1))
            copies.append(pltpu.make_async_remote_copy(
                src_ref=in_ref, dst_ref=out_ref.at[me], send_sem=send_sems.at[rel], recv_sem=recv_sems.at[rel],
                device_id=peer, device_id_type=MESH))
        for cp in copies:
            cp.start()
        for cp in copies:
            cp.wait()
        mine.wait()

    return pl.pallas_call(
        body, name="gather_small",
        in_specs=[HBM_SPEC], out_specs=HBM_SPEC,
        out_shape=jax.ShapeDtypeStruct((N_DEV,) + vec.shape, vec.dtype),
        scratch_shapes=[pltpu.SemaphoreType.DMA((N_DEV,)), pltpu.SemaphoreType.DMA((N_DEV,)),
                        pltpu.SemaphoreType.DMA],
    )(vec)


BIG = [("a_w_in", "col"), ("a_w_out", "row"), ("kv_w", "row"), ("b_w_q", "row"), ("b_w_out", "row"),
       ("ffn_w_gate_up", "colp"), ("ffn_w_down", "row"), ("ple_w_up", "col"), ("ple_w_gate", "row")]
GATHER_GROUPS = [[("a_w_in", 0), ("small", 0)], [("a_w_out", 0), ("ffn_w_gate_up", 0)],
                 [("ffn_w_down", 0), ("ple_w_gate", 0), ("ple_w_up", 0)], [("kv_w", 0), ("b_w_q", 0), ("b_w_out", 0)],
                 [("ffn_w_gate_up", 1)], [("ffn_w_down", 1), ("ple_w_gate", 1), ("ple_w_up", 1)]]
SCATTER_GROUPS = [[("ple_w_gate", 1), ("ple_w_up", 1), ("ffn_w_down", 1)], [("ffn_w_gate_up", 1)],
                  [("b_w_out", 0), ("b_w_q", 0), ("kv_w", 0)], [("ple_w_gate", 0), ("ple_w_up", 0), ("ffn_w_down", 0)],
                  [("ffn_w_gate_up", 0), ("a_w_out", 0)], [("a_w_in", 0)]]
SMALL_SHARDED = ["ln_gain", "ln_bias", "a_lower_bound"]
SMALL_REPLICATED = ["a_norm_gain", "kv_b", "b_b_q", "b_sinks", "b_b_out", "ple_b_gate"]
WEIGHT_ORDER = ["a_w_in", "a_lower_bound", "a_norm_gain", "a_w_out", "kv_w", "kv_b", "b_w_q", "b_b_q", "b_sinks",
                "b_w_out", "b_b_out", "ffn_w_gate_up", "ffn_w_down", "ple_w_up", "ple_w_gate", "ple_b_gate",
                "ln_gain", "ln_bias"]


def _as3(a):
    return a.reshape((-1,) + a.shape[-2:]) if a.ndim >= 3 else a.reshape((1,) + a.shape)


def _pad_lanes(v):
    n = v.shape[-1]
    return jnp.pad(v, ((0, 0), (0, (-n) % LANES)))


def _adam_small_fn(w, mom, vel, g):
    return _adam_fn(w, mom, vel, g, jnp.zeros_like(g))[1:]


def _sum_rows_fn(slots):
    acc = slots[0]
    for s in range(1, slots.shape[0]):
        acc = acc + slots[s]
    return (acc,)


def kernel(x, p, a_w_in, a_lower_bound, a_norm_gain, a_w_out, kv_w, kv_b, b_w_q, b_b_q, b_sinks, b_w_out, b_b_out, ffn_w_gate_up, ffn_w_down, ple_w_up, ple_w_gate, ple_b_gate, ln_gain, ln_bias, loss_target, m_a_w_in, m_a_lower_bound, m_a_norm_gain, m_a_w_out, m_kv_w, m_kv_b, m_b_w_q, m_b_b_q, m_b_sinks, m_b_w_out, m_b_b_out, m_ffn_w_gate_up, m_ffn_w_down, m_ple_w_up, m_ple_w_gate, m_ple_b_gate, m_ln_gain, m_ln_bias, v_a_w_in, v_a_lower_bound, v_a_norm_gain, v_a_w_out, v_kv_w, v_kv_b, v_b_w_q, v_b_b_q, v_b_sinks, v_b_w_out, v_b_b_out, v_ffn_w_gate_up, v_ffn_w_down, v_ple_w_up, v_ple_w_gate, v_ple_b_gate, v_ln_gain, v_ln_bias):
    args = dict(locals())
    wts = {n: args[n] for n in WEIGHT_ORDER}
    mom = {n: args["m_" + n] for n in WEIGHT_ORDER}
    vel = {n: args["v_" + n] for n in WEIGHT_ORDER}
    chip = 2 * lax.axis_index("x") + lax.axis_index("y")
    d = x.shape[-1]
    dq = d // N_CHIPS

    kind_of = dict(BIG)
    kind_of["small"] = "col"
    chip_arr = chip.reshape(1).astype(jnp.int32)
    small_pack = jnp.concatenate([wts[n].reshape(-1, dq) for n in SMALL_SHARDED], axis=0)[None]

    def place(key, after):
        n, layer = key
        if n == "small":
            return _place(small_pack, "col", chip_arr, mode="gather", name="place_small", out_dtype=F32, after=after)
        return _place(_as3(wts[n])[layer:layer + 1], kind_of[n], chip_arr, mode="gather",
                      name=f"place_{n}{layer}", out_dtype=BF16, after=after)

    gathers, where = [], {}
    for gi, group in enumerate(GATHER_GROUPS):
        prev = gathers[-1].token if gathers else None
        gathers.append(_Exchange("gather", [], [place(k, prev) for k in group], [kind_of[k[0]] for k in group],
                                 [0] * len(group), f"gather{gi}", after=prev))
        for k in group:
            where[k] = gi
    all_started = gathers[-1].token
    ready = {}

    def wget(name, layer, after):
        key = (name, layer)
        if key not in ready:
            gi = where[key]
            outs = gathers[gi].wait(all_started if gi == 0 else after)
            for k, arr in zip(GATHER_GROUPS[gi], outs):
                ready[k] = arr
        return ready[key]

    small_full = wget("small", 0, None)[0]
    ln_gain_f = small_full[0:6].reshape(DEPTH, 3, d)
    ln_bias_f = small_full[6:12].reshape(DEPTH, 3, d)
    alb_f = small_full[12:14]

    group_of = {k: gi for gi, group in enumerate(SCATTER_GROUPS) for k in group}
    grads_done, zones, scatters = {}, {}, []

    def grad_sink(name, layer, grad):
        grads_done[(name, layer)] = grad
        zones[name] = _place(grad, kind_of[name], chip_arr, mode="scatter", name=f"place_grad_{name}{layer}",
                             out_dtype=BF16, zone=zones.get(name), zone_shape=(N_CHIPS,) + _as3(wts[name]).shape,
                             layer=layer)
        gi = group_of[(name, layer)]
        group = SCATTER_GROUPS[gi]
        if not all(k in grads_done for k in group):
            return None
        ex = _Exchange("scatter", [grads_done[k] for k in group], [zones[k[0]] for k in group],
                       [kind_of[k[0]] for k in group], [k[1] for k in group], f"scatter{gi}")
        for k, zone in zip(group, ex.lands):
            zones[k[0]] = zone
        scatters.append((ex, group))
        return ex.token

    loss, grad_x, gs = _local_step(
        x[0], p[:, 0], loss_target[0], wget, grad_sink, ln_gain_f, ln_bias_f, alb_f, a_norm_gain, kv_b, b_b_q,
        b_sinks, b_b_out, ple_b_gate)

    res = {}

    def arrive(batch, after):
        for ex, group in batch:
            outs = ex.wait(after, lands=[zones[k[0]] for k in group])
            for k, zone in zip(group, outs):
                zones[k[0]] = zone

    def update(names, tag):
        partial = []
        for n in names:
            s2 = zones[n].reshape(N_CHIPS, -1, zones[n].shape[-1])
            partial.append(_rowwise(_sum_slots_fn, [s2], [], [(s2.shape[1:], BF16)], name=f"sum_{n}")[0])
        sibling = _sibling_swap(partial, tag)
        for n, own, sib in zip(names, partial, sibling):
            shp = wts[n].shape
            flat = lambda a: a.reshape(-1, shp[-1])
            out = _rowwise(_adam_fn, [flat(wts[n]), flat(mom[n]), flat(vel[n]), own, sib], [],
                           [(own.shape, F32)] * 4, name=f"adam_{n}")
            res[n] = [o.reshape(shp) for o in out]
        return res[names[-1]][1]

    last_names = [k[0] for k in SCATTER_GROUPS[-1]]
    arrive(scatters[:-1], grad_x)
    updated = update([n for n, _ in BIG if n not in last_names], "sibling_swap")
    arrive(scatters[-1:], updated)
    update(last_names, "sibling_swap_last")

    ln_g = jnp.concatenate([gs[f"ln_gain_{i}_{j}"] for i in range(DEPTH) for j in range(3)], axis=0)
    ln_b = jnp.concatenate([gs[f"ln_bias_{i}_{j}"] for i in range(DEPTH) for j in range(3)], axis=0)
    ple_bg = jnp.concatenate([gs[f"ple_b_{i}"] for i in range(DEPTH)], axis=0)
    small_list = [ln_g.reshape(1, -1), ln_b.reshape(1, -1), gs["alb"].reshape(1, -1), gs["norm_gain"],
                  gs["kv_b"], gs["b_q"], _pad_lanes(gs["sinks"]), gs["b_out"], ple_bg.reshape(1, -1), loss]
    small_vec = jnp.concatenate(small_list, axis=1)
    everyone = _gather_devices(small_vec)
    total, = _rowwise(_sum_rows_fn, [everyone], [], [(small_vec.shape, F32)], name="sum_small")
    offs, pos = [], 0
    for v in small_list:
        offs.append((pos, v.shape[1]))
        pos += v.shape[1]

    def seg(k):
        return total[0, offs[k][0]:offs[k][0] + offs[k][1]]

    def my_cols(full, rows):
        return lax.dynamic_slice_in_dim(full.reshape(rows, N_CHIPS, dq), chip, 1, axis=1).reshape(rows, dq)

    n_sink = b_sinks.shape[-1]
    small_grads = {
        "ln_gain": my_cols(seg(0), 6).reshape(ln_gain.shape), "ln_bias": my_cols(seg(1), 6).reshape(ln_bias.shape),
        "a_lower_bound": my_cols(seg(2), 2), "a_norm_gain": seg(3).reshape(a_norm_gain.shape),
        "kv_b": seg(4).reshape(kv_b.shape), "b_b_q": seg(5).reshape(b_b_q.shape),
        "b_sinks": seg(6)[:n_sink].reshape(b_sinks.shape), "b_b_out": seg(7).reshape(b_b_out.shape),
        "ple_b_gate": seg(8).reshape(ple_b_gate.shape)}
    names = SMALL_SHARDED + SMALL_REPLICATED
    pack = lambda dct: _pad_lanes(jnp.concatenate([dct[n].reshape(1, -1) for n in names], axis=1))
    g_pack = pack(small_grads)
    upd = _rowwise(_adam_small_fn, [pack(wts), pack(mom), pack(vel), g_pack], [], [(g_pack.shape, F32)] * 3,
                   name="adam_small")
    pos = 0
    for n in names:
        size = wts[n].size
        res[n] = [small_grads[n]] + [u[0, pos:pos + size].reshape(wts[n].shape) for u in upd]
        pos += size

    outs = [seg(9)[0], grad_x[None]]
    for k in range(4):
        outs += [res[n][k] for n in WEIGHT_ORDER]
    return tuple(outs)
```

```python
import functools

import jax
import jax.numpy as jnp
from jax import lax
from jax.experimental import pallas as pl
from jax.experimental.pallas import tpu as pltpu

F32 = jnp.float32
BF16 = jnp.bfloat16
MESH = pl.DeviceIdType.MESH

LANES = 128
HG_DK = 128
HG_CHUNK = 64
HG_SUB = 16
HG_ROWS = 512
ATT_HD = 64
ATT_G = 4
WINDOW = 128
DEPTH = 2
ALPHA = (2.0 * DEPTH) ** 0.25
LN_EPS = 1e-5
RMS_EPS = 1e-6
ADAM_LR, ADAM_B1, ADAM_B2, ADAM_EPS, ADAM_WD, ADAM_STEP = 0.001, 0.9, 0.999, 1e-08, 0.01, 10
N_CHIPS = 4
N_DEV = 8
VMEM_LIMIT = 56 * 1024 * 1024
NEG = -1e30


def _pick(n, cap):
    best = None
    for d in range(LANES, min(n, cap) + 1, LANES):
        if n % d == 0:
            best = d
    return n if best is None else best


def _pick_rows(m, cap):
    best = None
    for d in range(16, min(m, cap) + 1, 16):
        if m % d == 0:
            best = d
    return m if best is None else best


def _params(sem):
    return pltpu.CompilerParams(dimension_semantics=sem, vmem_limit_bytes=VMEM_LIMIT)


def _zeros_index(ndim):
    return lambda i, j, kk: (0,) * ndim


def _mm(a, b, *, name, la=None, lb=None, ta=False, tb=False, bias=None, add=None, out_dtype=F32,
        out_layers=None, out_layer=None, after=None, post=None, tile_cols=None, caps=(1024, 1536, 2048)):
    ar, ac = a.shape[-2:]
    br, bc = b.shape[-2:]
    m, k = (ac, ar) if ta else (ar, ac)
    k2, n = (bc, br) if tb else (br, bc)
    assert k == k2, (a.shape, b.shape, ta, tb)
    if post is not None:
        caps = (512, n if tile_cols is None else tile_cols, caps[2])
    tm, tn, tk = _pick(m, caps[0]), _pick(n, caps[1]), _pick(k, caps[2])
    assert post is None or tn == caps[1]
    nk = k // tk
    grid = (m // tm, n // tn, nk)

    def spec(block, idx, layer):
        if layer is None:
            return pl.BlockSpec(block, idx)
        return pl.BlockSpec((None,) + block, lambda i, j, kk: (layer,) + idx(i, j, kk))

    a_spec = spec((tk, tm), lambda i, j, kk: (kk, i), la) if ta else spec((tm, tk), lambda i, j, kk: (i, kk), la)
    b_spec = spec((tn, tk), lambda i, j, kk: (j, kk), lb) if tb else spec((tk, tn), lambda i, j, kk: (kk, j), lb)
    in_specs, operands = [a_spec, b_spec], [a, b]
    if bias is not None:
        in_specs.append(pl.BlockSpec((1, tn), lambda i, j, kk: (0, j)))
        operands.append(bias)
    if add is not None:
        in_specs.append(pl.BlockSpec((tm, tn), lambda i, j, kk: (i, j)))
        operands.append(add)
    if after is not None:
        in_specs.append(pl.BlockSpec(memory_space=pl.ANY))
        operands.append(after)
    dims = (((0 if ta else 1,), (1 if tb else 0,)), ((), ()))
    has_bias, has_add = bias is not None, add is not None
    if post is None:
        fn, rows, whole, outs, sums = None, [], [], [], []
        out_shape = jax.ShapeDtypeStruct((m, n) if out_layers is None else (out_layers, m, n), out_dtype)
        out_specs = spec((tm, tn), lambda i, j, kk: (i, j), out_layer)
    else:
        fn, rows, whole, outs, sums = post
        in_specs += [pl.BlockSpec((tm, r.shape[-1] // (n // tn)), lambda i, j, kk: (i, j)) for r in rows]
        in_specs += [pl.BlockSpec(tuple(w.shape), _zeros_index(w.ndim)) for w in whole]
        operands += list(rows) + list(whole)
        out_shape = [jax.ShapeDtypeStruct(sh, dt) for sh, dt in list(outs) + list(sums)]
        out_specs = ([pl.BlockSpec((tm, sh[-1] // (n // tn)), lambda i, j, kk: (i, j)) for sh, _ in outs]
                     + [pl.BlockSpec(tuple(sh), _zeros_index(len(sh))) for sh, _ in sums])
    n_in, n_extra, n_outs, n_sums = len(operands), len(rows) + len(whole), len(outs), len(sums)

    def body(*refs):
        a_ref, b_ref = refs[0], refs[1]
        pos = 2
        bias_ref = add_ref = None
        if has_bias:
            bias_ref = refs[pos]
            pos += 1
        if has_add:
            add_ref = refs[pos]
            pos += 1
        extra_refs = refs[n_in - n_extra:n_in]
        out_refs = refs[n_in:n_in + max(n_outs, 1)]
        sum_refs = refs[n_in + n_outs:n_in + n_outs + n_sums]
        acc_ref = refs[-1] if nk > 1 else None
        part = lax.dot_general(a_ref[...].astype(BF16), b_ref[...].astype(BF16), dims, preferred_element_type=F32)

        def finish(total):
            if has_bias:
                total = total + bias_ref[...]
            if has_add:
                total = total + add_ref[...]
            if fn is None:
                out_refs[0][...] = total.astype(out_refs[0].dtype)
                return
            res = fn(total, *[r[...] for r in extra_refs])
            for ref, val in zip(out_refs, res[:n_outs]):
                ref[...] = val.astype(ref.dtype)
            if n_sums:
                @pl.when(pl.program_id(0) == 0)
                def _():
                    for ref in sum_refs:
                        ref[...] = jnp.zeros(ref.shape, ref.dtype)

                for ref, val in zip(sum_refs, res[n_outs:]):
                    ref[...] += val

        if nk == 1:
            finish(part)
        else:
            kk = pl.program_id(2)

            @pl.when(kk == 0)
            def _():
                acc_ref[...] = part

            @pl.when(kk > 0)
            def _():
                acc_ref[...] += part

            @pl.when(kk == nk - 1)
            def _():
                finish(acc_ref[...])

    return pl.pallas_call(
        body, name=name, grid=grid, in_specs=in_specs, out_specs=out_specs, out_shape=out_shape,
        scratch_shapes=[pltpu.VMEM((tm, tn), F32)] if nk > 1 else [],
        compiler_params=_params(("arbitrary" if n_sums else "parallel", "parallel", "arbitrary")),
    )(*operands)


def _rowwise(fn, rows, whole, outs, sums=(), *, name, tm=256):
    m = rows[0].shape[-2]
    tm = _pick_rows(m, tm)
    n_rows, n_whole, n_outs, n_sums = len(rows), len(whole), len(outs), len(sums)

    def rspec(shape):
        lead = len(shape) - 2
        return pl.BlockSpec(tuple(shape[:-2]) + (tm, shape[-1]), lambda i: (0,) * lead + (i, 0))

    def wspec(shape):
        return pl.BlockSpec(tuple(shape), lambda i: (0,) * len(shape))

    def body(*refs):
        vals = [r[...] for r in refs[:n_rows + n_whole]]
        out_refs = refs[n_rows + n_whole:n_rows + n_whole + n_outs]
        sum_refs = refs[n_rows + n_whole + n_outs:]
        res = fn(*vals)
        for ref, val in zip(out_refs, res[:n_outs]):
            ref[...] = val.astype(ref.dtype)
        if n_sums:
            @pl.when(pl.program_id(0) == 0)
            def _():
                for ref in sum_refs:
                    ref[...] = jnp.zeros(ref.shape, ref.dtype)

            for ref, val in zip(sum_refs, res[n_outs:]):
                ref[...] += val

    result = pl.pallas_call(
        body, name=name, grid=(m // tm,),
        in_specs=[rspec(r.shape) for r in rows] + [wspec(w.shape) for w in whole],
        out_specs=[rspec(s) for s, _ in outs] + [wspec(s) for s, _ in sums],
        out_shape=[jax.ShapeDtypeStruct(s, d) for s, d in list(outs) + list(sums)],
        compiler_params=_params(("arbitrary",)),
    )(*rows, *whole)
    return result


def _sigmoid(v):
    return jax.nn.sigmoid(v)


def _col_sum(v):
    return jnp.sum(v, axis=0, keepdims=True)


def _ln_stats(z):
    mu = jnp.mean(z, axis=-1, keepdims=True)
    zc = z - mu
    var = jnp.mean(zc * zc, axis=-1, keepdims=True)
    rstd = lax.rsqrt(var + LN_EPS)
    return zc * rstd, rstd


def _ln_fwd_fn(xin, h, gain, bias):
    xhat, _ = _ln_stats(ALPHA * xin + h)
    y = xhat * gain + bias
    return y, y


def _ple_ln_fwd_fn(xin, pg, pu, gain, bias):
    xhat, _ = _ln_stats(ALPHA * xin + _sigmoid(pg) * pu)
    y = xhat * gain + bias
    return y, y


def _ln_dz(dy, z, gain):
    xhat, rstd = _ln_stats(z)
    dxhat = dy * gain
    dz = rstd * (dxhat - jnp.mean(dxhat, axis=-1, keepdims=True)
                 - xhat * jnp.mean(dxhat * xhat, axis=-1, keepdims=True))
    return dz, _col_sum(dy * xhat), _col_sum(dy)


def _ln_bwd_fn(dy, xin, h, gain):
    dz, dgain, dbias = _ln_dz(dy, ALPHA * xin + h, gain)
    return ALPHA * dz, dz, dgain, dbias, _col_sum(dz)


def _ple_ln_bwd_fn(dy, xin, pg, pu, gain):
    sg = _sigmoid(pg)
    dz, dgain, dbias = _ln_dz(dy, ALPHA * xin + sg * pu, gain)
    dpg = dz * pu * sg * (1.0 - sg)
    return ALPHA * dz, dpg, dz * sg, dgain, dbias, _col_sum(dpg)


def _swiglu_fwd_fn(gu):
    hid = gu.shape[-1] // 2
    gate, up = gu[:, :hid], gu[:, hid:]
    return gu, gate * _sigmoid(gate) * up


def _swiglu_bwd_fn(dact, gu):
    gu = gu.astype(F32)
    hid = gu.shape[-1] // 2
    gate, up = gu[:, :hid], gu[:, hid:]
    sg = _sigmoid(gate)
    dgate = dact * up * sg * (1.0 + gate * (1.0 - sg))
    dup = dact * gate * sg
    return (jnp.concatenate([dgate, dup], axis=-1),)


def _loss_fn(y, target):
    err = y - target
    inv = 1.0 / y.shape[-1]
    part = 0.5 * inv * jnp.sum(jnp.sum(err * err, axis=-1, keepdims=True), axis=0, keepdims=True)
    return err * inv, jnp.broadcast_to(part, (1, LANES))


def _adam_fn(w, mom, vel, p_own, p_sib):
    g = p_own.astype(F32) + p_sib.astype(F32)
    m_new = ADAM_B1 * mom + (1.0 - ADAM_B1) * g
    v_new = ADAM_B2 * vel + (1.0 - ADAM_B2) * (g * g)
    m_hat = m_new / (1.0 - ADAM_B1 ** ADAM_STEP)
    v_hat = v_new / (1.0 - ADAM_B2 ** ADAM_STEP)
    delta = -ADAM_LR * (m_hat / (jnp.sqrt(v_hat) + ADAM_EPS) + ADAM_WD * w)
    return g, delta, m_new, v_new


def _sum_slots_fn(slots):
    acc = slots[0].astype(F32)
    for s in range(1, slots.shape[0]):
        acc = acc + slots[s].astype(F32)
    return (acc,)


def _split2(x):
    hi = x.astype(BF16)
    return hi, (x - hi.astype(F32)).astype(BF16)


def _dot3(a, b, dims):
    a_hi, a_lo = _split2(a)
    b_hi, b_lo = _split2(b)
    dn = (dims, ((), ()))
    return (lax.dot_general(a_hi, b_hi, dn, preferred_element_type=F32)
            + (lax.dot_general(a_hi, b_lo, dn, preferred_element_type=F32)
               + lax.dot_general(a_lo, b_hi, dn, preferred_element_type=F32)))


def _tdot(mask01, b):
    m = mask01.astype(BF16)
    b_hi = b.astype(BF16)
    rest = b - b_hi.astype(F32)
    b_mid = rest.astype(BF16)
    b_lo = (rest - b_mid.astype(F32)).astype(BF16)
    dn = (((1,), (0,)), ((), ()))
    return (lax.dot_general(m, b_hi, dn, preferred_element_type=F32)
            + (lax.dot_general(m, b_mid, dn, preferred_element_type=F32)
               + lax.dot_general(m, b_lo, dn, preferred_element_type=F32)))


def _hdot(a, b):
    return _dot3(a, b, ((1,), (0,)))


def _hdot_nt(a, b):
    return _dot3(a, b, ((1,), (1,)))


def _hdot_tn(a, b):
    return _dot3(a, b, ((0,), (0,)))


def _dot(a, b):
    return lax.dot_general(a.astype(BF16), b.astype(BF16), (((1,), (0,)), ((), ())), preferred_element_type=F32)


def _dot_nt(a, b):
    return lax.dot_general(a.astype(BF16), b.astype(BF16), (((1,), (1,)), ((), ())), preferred_element_type=F32)


def _dot_tn(a, b):
    return lax.dot_general(a.astype(BF16), b.astype(BF16), (((0,), (0,)), ((), ())), preferred_element_type=F32)


def _hg_masks():
    c = HG_CHUNK
    row = lax.broadcasted_iota(jnp.int32, (c, c), 0)
    col = lax.broadcasted_iota(jnp.int32, (c, c), 1)
    base = row & (-HG_SUB)
    return row, col, base, col <= row, col < base


def _hg_gates(qr, fr, alb):
    lbound = _sigmoid(alb[0:1, :] - alb[1:2, :])
    sig = _sigmoid(fr)
    forget = lbound + (1.0 - lbound) * sig
    kk = (1.0 - lbound) * _sigmoid(-fr)
    qt = qr * _sigmoid(qr) * (HG_DK ** -0.5)
    return qt, kk, jnp.log(forget), lbound, sig, forget


def _hg_scores(qt, kk, g, scores=True):
    c, nsub = HG_CHUNK, HG_CHUNK // HG_SUB
    row, col, base, causal, below = _hg_masks()
    b = _tdot(causal, g)
    rr = _tdot(below, g)
    bq = b - rr
    qh = qt * jnp.exp(bq)
    edecs = [None]
    parts = [jnp.zeros((HG_SUB, c), F32)]
    for i in range(1, nsub):
        edec = jnp.exp(jnp.minimum(rr[i * HG_SUB:i * HG_SUB + 1, :] - b, 0.0))
        edecs.append(edec)
        if scores:
            parts.append(_dot_nt(qh[i * HG_SUB:(i + 1) * HG_SUB, :], kk * edec))
    b3 = b.reshape(nsub, HG_SUB, HG_DK)
    q3 = qt.reshape(nsub, HG_SUB, HG_DK)
    k3 = kk.reshape(nsub, HG_SUB, HG_DK)
    if not scores:
        return None, b, bq, qh, edecs, (b3, q3, k3)
    a = jnp.where(below, jnp.concatenate(parts, axis=0), 0.0)
    for j in range(HG_SUB):
        e = jnp.exp(jnp.minimum(b3 - b3[:, j:j + 1, :], 0.0))
        colv = jnp.sum(q3 * e * k3[:, j:j + 1, :], axis=-1, keepdims=True).reshape(c, 1)
        a = jnp.where(col == base + j, colv, a)
    a = jnp.where(causal, a, 0.0)
    return a, b, bq, qh, edecs, (b3, q3, k3)


def _hg_norm(o, gr, gain):
    r = lax.rsqrt(jnp.mean(o * o, axis=-1, keepdims=True) + RMS_EPS)
    sg = _sigmoid(gr)
    return o * r * gain, r, sg


def _hgrn2_fwd(proj, alb, gain, *, rb):
    m, d4 = proj.shape
    d = d4 // 4
    heads = d // HG_DK
    rb = min(rb, m)
    cpb = rb // HG_CHUNK
    nrb = m // rb

    def body(q_ref, f_ref, v_ref, g_ref, alb_ref, gain_ref, o_ref, og_ref, st_ref, a_ref, state):
        @pl.when(pl.program_id(1) == 0)
        def _():
            state[...] = jnp.zeros(state.shape, F32)

        def chunk(ci, carry):
            sl = pl.ds(pl.multiple_of(ci * HG_CHUNK, HG_CHUNK), HG_CHUNK)
            qt, kk, g, _, _, _ = _hg_gates(q_ref[sl, :], f_ref[sl, :], alb_ref[...])
            v = v_ref[sl, :]
            st = state[...]
            st_ref[ci] = st
            a, b, _, _, _, _ = _hg_scores(qt, kk, g)
            a_ref[ci] = a.astype(a_ref.dtype)
            o = _dot(a, v) + _dot_nt(qt * jnp.exp(b), st)
            b_last = b[HG_CHUNK - 1:HG_CHUNK, :]
            state[...] = st * jnp.exp(b_last) + _hdot_tn(v, kk * jnp.exp(b_last - b))
            o_ref[sl, :] = o
            n, _, sg = _hg_norm(o, g_ref[sl, :], gain_ref[...])
            og_ref[sl, :] = (n * g_ref[sl, :] * sg).astype(og_ref.dtype)
            return carry

        lax.fori_loop(0, cpb, chunk, 0, unroll=2)

    def col(cidx):
        return pl.BlockSpec((rb, HG_DK), lambda h, r: (r, cidx * heads + h))

    return pl.pallas_call(
        body, name="hgrn2_fwd", grid=(heads, nrb),
        in_specs=[col(0), col(1), col(2), col(3),
                  pl.BlockSpec((2, HG_DK), lambda h, r: (0, h)),
                  pl.BlockSpec((1, HG_DK), lambda h, r: (0, 0))],
        out_specs=[pl.BlockSpec((rb, HG_DK), lambda h, r: (r, h)),
                   pl.BlockSpec((rb, HG_DK), lambda h, r: (r, h)),
                   pl.BlockSpec((None, cpb, HG_DK, HG_DK), lambda h, r: (h, r, 0, 0)),
                   pl.BlockSpec((None, cpb, HG_CHUNK, HG_CHUNK), lambda h, r: (h, r, 0, 0))],
        out_shape=[jax.ShapeDtypeStruct((m, d), F32), jax.ShapeDtypeStruct((m, d), BF16),
                   jax.ShapeDtypeStruct((heads, m // HG_CHUNK, HG_DK, HG_DK), F32),
                   jax.ShapeDtypeStruct((heads, m // HG_CHUNK, HG_CHUNK, HG_CHUNK), BF16)],
        scratch_shapes=[pltpu.VMEM((HG_DK, HG_DK), F32)],
        compiler_params=_params(("parallel", "arbitrary")),
    )(proj, proj, proj, proj, alb, gain)


def _hgrn2_bwd(proj, o_pre, states, scores, dog, alb, gain, *, rb):
    m, d4 = proj.shape
    d = d4 // 4
    heads = d // HG_DK
    rb = min(rb, m)
    cpb = rb // HG_CHUNK
    nrb = m // rb
    c, nsub = HG_CHUNK, HG_CHUNK // HG_SUB

    def body(q_ref, f_ref, v_ref, g_ref, o_ref, st_ref, a_ref, dog_ref, alb_ref, gain_ref,
             dq_ref, df_ref, dv_ref, dg_ref, dalb_ref, dgain_ref, dstate, carry_ref):
        first = (pl.program_id(0) == 0) & (pl.program_id(1) == 0)

        @pl.when(first)
        def _():
            dgain_ref[...] = jnp.zeros(dgain_ref.shape, F32)

        @pl.when(pl.program_id(1) == 0)
        def _():
            dstate[...] = jnp.zeros(dstate.shape, F32)
            carry_ref[...] = jnp.zeros(carry_ref.shape, F32)
            dalb_ref[...] = jnp.zeros(dalb_ref.shape, F32)

        row, col, base, causal, below = _hg_masks()
        sub_iota = lax.broadcasted_iota(jnp.int32, (nsub, HG_SUB, HG_DK), 1)
        row_k = lax.broadcasted_iota(jnp.int32, (c, HG_DK), 0)
        upper = col >= row

        def chunk(step, carry):
            ci = cpb - 1 - step
            sl = pl.ds(pl.multiple_of(ci * HG_CHUNK, HG_CHUNK), HG_CHUNK)
            qr, fr, v, gr = q_ref[sl, :], f_ref[sl, :], v_ref[sl, :], g_ref[sl, :]
            qt, kk, g, lbound, sig, forget = _hg_gates(qr, fr, alb_ref[...])
            o = o_ref[sl, :]
            dogv = dog_ref[sl, :]
            gain_v = gain_ref[...]
            n, r, sg = _hg_norm(o, gr, gain_v)
            dgr = dogv * n * sg * (1.0 + gr * (1.0 - sg))
            dn = dogv * gr * sg
            dgain_ref[...] += _col_sum(dn * o * r)
            u = dn * gain_v
            d_o = r * u - o * (r * r * r) * jnp.mean(u * o, axis=-1, keepdims=True)
            st0 = st_ref[ci]
            dst = dstate[...]
            _, b, bq, qh, edecs, (b3, q3, k3) = _hg_scores(qt, kk, g, scores=False)
            a = a_ref[ci]
            eb = jnp.exp(b)
            b_last = b[c - 1:c, :]
            kdl_dec = jnp.exp(b_last - b)
            kdl = kk * kdl_dec
            d_a = jnp.where(causal, _dot_nt(d_o, v), 0.0)
            d_at = _dot_nt(v, d_o)
            dv = _dot_tn(a, d_o) + _dot_nt(kdl, dst)
            dq = eb * _hdot(d_o, st0)
            dk = _hdot(v, dst) * kdl_dec
            d_a_below = jnp.where(below, d_a, 0.0)
            dq_parts = [jnp.zeros((HG_SUB, HG_DK), F32)]
            for i in range(1, nsub):
                lo, hi = i * HG_SUB, (i + 1) * HG_SUB
                dq_parts.append(_hdot(d_a_below[lo:hi, :], kk * edecs[i]))
                gi = _hdot(d_at[:, lo:hi], qh[lo:hi, :])
                dk = dk + jnp.where(row_k < lo, edecs[i] * gi, 0.0)
            dq = dq + jnp.concatenate(dq_parts, axis=0) * jnp.exp(bq)
            dq3 = jnp.zeros((nsub, HG_SUB, HG_DK), F32)
            dk3 = jnp.zeros((nsub, HG_SUB, HG_DK), F32)
            for j in range(HG_SUB):
                e = jnp.exp(jnp.minimum(b3 - b3[:, j:j + 1, :], 0.0))
                dcol = jnp.sum(jnp.where(col == base + j, d_a, 0.0), axis=-1, keepdims=True)
                t1 = dcol.reshape(nsub, HG_SUB, 1) * e
                dq3 = dq3 + t1 * k3[:, j:j + 1, :]
                dk3 = jnp.where(sub_iota == j, jnp.sum(t1 * q3, axis=1, keepdims=True), dk3)
            dq = dq + dq3.reshape(c, HG_DK)
            dk = dk + dk3.reshape(c, HG_DK)
            dstate[...] = dst * jnp.exp(b_last) + _hdot_tn(d_o, qt * eb)
            dglog = _tdot(upper, qt * dq - kk * dk) + carry_ref[...]
            carry_ref[...] = dglog[0:1, :]
            dforget = dglog / forget
            one_m_lb = 1.0 - lbound
            dsig = (dforget - dk) * one_m_lb
            sneg = _sigmoid(-fr)
            dlb = _col_sum(dforget * (1.0 - sig) - dk * sneg)
            dalb0 = dlb * lbound * one_m_lb
            dalb_ref[...] += jnp.concatenate([dalb0, -dalb0], axis=0)
            sq = _sigmoid(qr)
            dq_ref[sl, :] = (dq * (HG_DK ** -0.5) * sq * (1.0 + qr * (1.0 - sq))).astype(dq_ref.dtype)
            df_ref[sl, :] = (dsig * sig * (1.0 - sig)).astype(df_ref.dtype)
            dv_ref[sl, :] = dv.astype(dv_ref.dtype)
            dg_ref[sl, :] = dgr.astype(dg_ref.dtype)
            return carry

        lax.fori_loop(0, cpb, chunk, 0, unroll=2)

    def rev(r):
        return nrb - 1 - r

    def col(cidx):
        return pl.BlockSpec((rb, HG_DK), lambda h, r: (rev(r), cidx * heads + h))

    def head_rows():
        return pl.BlockSpec((rb, HG_DK), lambda h, r: (rev(r), h))

    return pl.pallas_call(
        body, name="hgrn2_bwd", grid=(heads, nrb),
        in_specs=[col(0), col(1), col(2), col(3), head_rows(),
                  pl.BlockSpec((None, cpb, HG_DK, HG_DK), lambda h, r: (h, rev(r), 0, 0)),
                  pl.BlockSpec((None, cpb, HG_CHUNK, HG_CHUNK), lambda h, r: (h, rev(r), 0, 0)),
                  head_rows(),
                  pl.BlockSpec((2, HG_DK), lambda h, r: (0, h)),
                  pl.BlockSpec((1, HG_DK), lambda h, r: (0, 0))],
        out_specs=[head_rows(), head_rows(), head_rows(), head_rows(),
                   pl.BlockSpec((2, HG_DK), lambda h, r: (0, h)),
                   pl.BlockSpec((1, HG_DK), lambda h, r: (0, 0))],
        out_shape=[jax.ShapeDtypeStruct((m, d), BF16)] * 4
                  + [jax.ShapeDtypeStruct((2, d), F32), jax.ShapeDtypeStruct((1, HG_DK), F32)],
        scratch_shapes=[pltpu.VMEM((HG_DK, HG_DK), F32), pltpu.VMEM((1, HG_DK), F32)],
        compiler_params=_params(("arbitrary", "arbitrary")),
    )(proj, proj, proj, proj, o_pre, states, scores, dog, alb, gain)


def _swa_probs(qh, kp, kc, sink, slope, has_prev):
    rows = qh.shape[0]
    qi = lax.broadcasted_iota(jnp.int32, (rows, WINDOW), 0) & (WINDOW - 1)
    si = lax.broadcasted_iota(jnp.int32, (rows, WINDOW), 1)
    scale = ATT_HD ** -0.5
    dist_c = (qi - si).astype(F32)
    s_p = _dot_nt(qh, kp) * scale - slope * (dist_c + float(WINDOW))
    s_c = _dot_nt(qh, kc) * scale - slope * dist_c
    s_p = jnp.where((si > qi) & has_prev, s_p, NEG)
    s_c = jnp.where(si <= qi, s_c, NEG)
    mx = jnp.maximum(jnp.maximum(jnp.max(s_p, axis=-1, keepdims=True), jnp.max(s_c, axis=-1, keepdims=True)), sink)
    e_p, e_c, e_s = jnp.exp(s_p - mx), jnp.exp(s_c - mx), jnp.exp(sink - mx)
    inv = 1.0 / (jnp.sum(e_p, axis=-1, keepdims=True) + jnp.sum(e_c, axis=-1, keepdims=True) + e_s)
    return e_p * inv, e_c * inv, e_s * inv


def _slope(h, n_heads):
    return float(2.0 ** (-8.0 * (h + 1) / n_heads))


def _swa_group(ref_vals, sink_ref, kh, n_heads):
    heads = [kh * ATT_G + g for g in range(ATT_G)]
    stacked = [jnp.concatenate([v[:, h * ATT_HD:(h + 1) * ATT_HD] for h in heads], axis=0) for v in ref_vals]
    grp = lax.shift_right_logical(lax.broadcasted_iota(jnp.int32, (ATT_G * WINDOW, 1), 0), WINDOW.bit_length() - 1)
    slope = jnp.zeros((ATT_G * WINDOW, 1), F32)
    sink = jnp.zeros((ATT_G * WINDOW, 1), F32)
    for g, h in enumerate(heads):
        slope = jnp.where(grp == g, _slope(h, n_heads), slope)
        sink = jnp.where(grp == g, sink_ref[:, h:h + 1], sink)
    return stacked, slope, sink


def _swa_fwd(q, kv, sinks):
    m, d = q.shape
    n_heads = d // ATT_HD
    kvh = n_heads // ATT_G
    kd = kvh * ATT_HD
    nb = m // WINDOW

    def body(q_ref, kvp_ref, kvc_ref, sink_ref, o_ref):
        has_prev = pl.program_id(0) > 0
        qv, kvp, kvc = q_ref[...], kvp_ref[...], kvc_ref[...]
        outs = []
        for kh in range(kvh):
            ks = slice(kh * ATT_HD, (kh + 1) * ATT_HD)
            vs = slice(kd + kh * ATT_HD, kd + (kh + 1) * ATT_HD)
            (q4,), slope, sink = _swa_group([qv], sink_ref, kh, n_heads)
            p_p, p_c, _ = _swa_probs(q4, kvp[:, ks], kvc[:, ks], sink, slope, has_prev)
            o4 = _dot(p_p, kvp[:, vs]) + _dot(p_c, kvc[:, vs])
            outs += [o4[g * WINDOW:(g + 1) * WINDOW, :] for g in range(ATT_G)]
        o_ref[...] = jnp.concatenate(outs, axis=-1).astype(o_ref.dtype)

    return pl.pallas_call(
        body, name="swa_fwd", grid=(nb,),
        in_specs=[pl.BlockSpec((WINDOW, d), lambda n: (n, 0)),
                  pl.BlockSpec((WINDOW, 2 * kd), lambda n: (jnp.maximum(n - 1, 0), 0)),
                  pl.BlockSpec((WINDOW, 2 * kd), lambda n: (n, 0)),
                  pl.BlockSpec((1, n_heads), lambda n: (0, 0))],
        out_specs=pl.BlockSpec((WINDOW, d), lambda n: (n, 0)),
        out_shape=jax.ShapeDtypeStruct((m, d), BF16),
        compiler_params=_params(("arbitrary",)),
    )(q, kv, kv, sinks)


def _swa_bwd(q, kv, sinks, dao):
    m, d = q.shape
    n_heads = d // ATT_HD
    kvh = n_heads // ATT_G
    kd = kvh * ATT_HD
    nb = m // WINDOW
    scale = ATT_HD ** -0.5

    def body(q_ref, kvp_ref, kvc_ref, sink_ref, do_ref, dq_ref, dkvc_ref, dkvp_ref, dqsum_ref, dsink_ref):
        @pl.when(pl.program_id(0) == 0)
        def _():
            dqsum_ref[...] = jnp.zeros(dqsum_ref.shape, F32)
            dsink_ref[...] = jnp.zeros(dsink_ref.shape, F32)

        has_prev = pl.program_id(0) > 0
        qv, kvp, kvc, dov = q_ref[...], kvp_ref[...], kvc_ref[...], do_ref[...]
        lane_h = lax.broadcasted_iota(jnp.int32, (1, n_heads), 1)
        dsink = jnp.zeros((1, n_heads), F32)
        dq_parts, dk_p, dk_c, dv_p, dv_c = [], [], [], [], []
        for kh in range(kvh):
            ks = slice(kh * ATT_HD, (kh + 1) * ATT_HD)
            vs = slice(kd + kh * ATT_HD, kd + (kh + 1) * ATT_HD)
            kp, kc, vp, vc = kvp[:, ks], kvc[:, ks], kvp[:, vs], kvc[:, vs]
            (q4, do4), slope, sink = _swa_group([qv, dov], sink_ref, kh, n_heads)
            p_p, p_c, p_s = _swa_probs(q4, kp, kc, sink, slope, has_prev)
            dp_p, dp_c = _dot_nt(do4, vp), _dot_nt(do4, vc)
            delta = jnp.sum(p_p * dp_p, axis=-1, keepdims=True) + jnp.sum(p_c * dp_c, axis=-1, keepdims=True)
            ds_p, ds_c = p_p * (dp_p - delta), p_c * (dp_c - delta)
            sink_term = p_s * delta
            dq4 = (_dot(ds_p, kp) + _dot(ds_c, kc)) * scale
            for g in range(ATT_G):
                rows = slice(g * WINDOW, (g + 1) * WINDOW)
                dsink = dsink + jnp.where(lane_h == kh * ATT_G + g, -_col_sum(sink_term[rows, :]), 0.0)
                dq_parts.append(dq4[rows, :])
            dk_p.append(_dot_tn(ds_p, q4) * scale)
            dk_c.append(_dot_tn(ds_c, q4) * scale)
            dv_p.append(_dot_tn(p_p, do4))
            dv_c.append(_dot_tn(p_c, do4))
        dq = jnp.concatenate(dq_parts, axis=-1)
        dq_ref[...] = dq.astype(dq_ref.dtype)
        dqsum_ref[...] += _col_sum(dq)
        dsink_ref[...] += dsink
        dkvc_ref[...] = jnp.concatenate(dk_c + dv_c, axis=-1)
        dkvp_ref[...] = jnp.concatenate(dk_p + dv_p, axis=-1)

    return pl.pallas_call(
        body, name="swa_bwd", grid=(nb,),
        in_specs=[pl.BlockSpec((WINDOW, d), lambda n: (n, 0)),
                  pl.BlockSpec((WINDOW, 2 * kd), lambda n: (jnp.maximum(n - 1, 0), 0)),
                  pl.BlockSpec((WINDOW, 2 * kd), lambda n: (n, 0)),
                  pl.BlockSpec((1, n_heads), lambda n: (0, 0)),
                  pl.BlockSpec((WINDOW, d), lambda n: (n, 0))],
        out_specs=[pl.BlockSpec((WINDOW, d), lambda n: (n, 0)),
                   pl.BlockSpec((WINDOW, 2 * kd), lambda n: (n, 0)),
                   pl.BlockSpec((WINDOW, 2 * kd), lambda n: (n, 0)),
                   pl.BlockSpec((1, d), lambda n: (0, 0)),
                   pl.BlockSpec((1, n_heads), lambda n: (0, 0))],
        out_shape=[jax.ShapeDtypeStruct((m, d), BF16), jax.ShapeDtypeStruct((m, 2 * kd), F32),
                   jax.ShapeDtypeStruct((m, 2 * kd), F32), jax.ShapeDtypeStruct((1, d), F32),
                   jax.ShapeDtypeStruct((1, n_heads), F32)],
        compiler_params=_params(("arbitrary",)),
    )(q, kv, kv, sinks, dao)


def _kv_grad_combine(dkv_cur, dkv_prev):
    m, w = dkv_cur.shape
    nb = m // WINDOW

    def body(cur_ref, nxt_ref, o_ref, sum_ref):
        @pl.when(pl.program_id(0) == 0)
        def _():
            sum_ref[...] = jnp.zeros(sum_ref.shape, F32)

        total = cur_ref[...] + jnp.where(pl.program_id(0) < nb - 1, nxt_ref[...], 0.0)
        o_ref[...] = total.astype(o_ref.dtype)
        sum_ref[...] += _col_sum(total)

    return pl.pallas_call(
        body, name="kv_grad_combine", grid=(nb,),
        in_specs=[pl.BlockSpec((WINDOW, w), lambda n: (n, 0)),
                  pl.BlockSpec((WINDOW, w), lambda n: (jnp.minimum(n + 1, nb - 1), 0))],
        out_specs=[pl.BlockSpec((WINDOW, w), lambda n: (n, 0)), pl.BlockSpec((1, w), lambda n: (0, 0))],
        out_shape=[jax.ShapeDtypeStruct((m, w), BF16), jax.ShapeDtypeStruct((1, w), F32)],
        compiler_params=_params(("arbitrary",)),
    )(dkv_cur, dkv_prev)


def _row(v):
    return v.reshape(1, -1)


def _local_step(x, p, target, wget, grad_sink, ln_gain, ln_bias, alb, norm_gain, kv_b, b_q, sinks, b_out, ple_b):
    gs = {}
    gains = [[_row(ln_gain[i, j]) for j in range(3)] for i in range(DEPTH)]
    biases = [[_row(ln_bias[i, j]) for j in range(3)] for i in range(DEPTH)]
    sd = x.shape
    pending = [None]

    def mm(a, b, lb=0, **kw):
        after, pending[0] = pending[0], None
        return _mm(a, b, lb=lb, after=after, **kw)

    def mm_ln(a, wt, xin, i, j, nm, bias=None, pu=None):
        if pu is None:
            fn, rows = (lambda h, xv, g, bv: (h,) + _ln_fwd_fn(xv, h, g, bv)), [xin]
        else:
            fn, rows = (lambda h, xv, puv, g, bv: (h,) + _ple_ln_fwd_fn(xv, h, puv, g, bv)), [xin, pu]
        h, y, yb = _mm(a, wt, lb=0, bias=bias, name=nm,
                       post=(fn, rows, [gains[i][j], biases[i][j]], [(sd, F32), (sd, F32), (sd, BF16)], []))
        return h, (y, yb)

    def mm_ln_bwd(a, wt, add, xin, h, i, j, nm):
        dx_part, dh, dg, db, dhsum = mm(a, wt, tb=True, add=add, name=nm,
                                        post=(_ln_bwd_fn, [xin, h], [gains[i][j]], [(sd, F32), (sd, BF16)],
                                              [((1, sd[1]), F32)] * 3))
        gs[f"ln_gain_{i}_{j}"], gs[f"ln_bias_{i}_{j}"] = dg, db
        return dx_part, dh, dhsum

    def tail_fwd(xa, i):
        wgu = wget("ffn_w_gate_up", i, xa[1])
        hid2 = wgu.shape[-1]
        gu, act = _mm(xa[1], wgu, lb=0, name=f"ffn_up_swiglu{i}", tile_cols=hid2 // 2,
                      post=(_swiglu_fwd_fn, [], [], [((sd[0], hid2), BF16), ((sd[0], hid2 // 2), BF16)], []))
        f, xb = mm_ln(act, wget("ffn_w_down", i, act), xa[0], i, 1, f"ffn_down_ln{i}")
        pu = _mm(p[i], wget("ple_w_up", i, act), lb=0, name=f"ple_up{i}")
        pg, xc = mm_ln(xb[1], wget("ple_w_gate", i, act), xb[0], i, 2, f"ple_gate_ln{i}", bias=_row(ple_b[i]), pu=pu)
        return dict(xa=xa, gu=gu, act=act, f=f, xb=xb, pg=pg, pu=pu), xc

    def tail_bwd(head, sv, i, mix_in, mix_h):
        xa, xb = sv["xa"], sv["xb"]
        dxb_part, dpg, dpu, dg2, db2, dbg = head(
            _ple_ln_bwd_fn, [xb[0], sv["pg"], sv["pu"]], [gains[i][2]],
            [(sd, F32), (sd, BF16), (sd, BF16)], [((1, sd[1]), F32)] * 3)[:6]
        gs[f"ple_b_{i}"] = dbg
        gs[f"ln_gain_{i}_2"], gs[f"ln_bias_{i}_2"] = dg2, db2
        grad_of("ple_w_gate", i, xb[1], dpg)
        grad_of("ple_w_up", i, p[i], dpu)
        dxa_part, df, _ = mm_ln_bwd(dpg, wget("ple_w_gate", i, None), dxb_part, xa[0], sv["f"], i, 1,
                                    f"ple_gate_dx_ln{i}")
        grad_of("ffn_w_down", i, sv["act"], df)
        gu = sv["gu"]
        dgu, = mm(df, wget("ffn_w_down", i, None), tb=True, name=f"ffn_down_dx_swiglu{i}", tile_cols=gu.shape[1] // 4,
                  post=(_swiglu_bwd_fn, [gu], [], [(gu.shape, BF16)], []))
        grad_of("ffn_w_gate_up", i, xa[1], dgu)
        return mm_ln_bwd(dgu, wget("ffn_w_gate_up", i, None), dxa_part, mix_in, mix_h, i, 0, f"ffn_up_dx_ln{i}")

    def grad_of(nm, i, act, dout):
        grad = mm(act, dout, lb=None, ta=True, out_dtype=BF16, out_layers=1, out_layer=0, name=f"grad_{nm}{i}")
        token = grad_sink(nm, i, grad)
        if token is not None:
            pending[0] = token

    proj = _mm(x, wget("a_w_in", 0, None), lb=0, name="hg_proj")
    o_pre, og, states, scores = _hgrn2_fwd(proj, alb, norm_gain, rb=HG_ROWS)
    h0, x1 = mm_ln(og, wget("a_w_out", 0, og), x, 0, 0, "hg_out_ln")
    sv0, x3 = tail_fwd(x1, 0)
    kv = _mm(x3[1], wget("kv_w", 0, x3[1]), lb=0, bias=_row(kv_b), name="kv_proj")
    q = _mm(x3[1], wget("b_w_q", 0, x3[1]), lb=0, bias=b_q, name="q_proj")
    ao = _swa_fwd(q, kv, sinks)
    h1, x4 = mm_ln(ao, wget("b_w_out", 0, x3[1]), x3[0], 1, 0, "att_out_ln", bias=b_out)
    sv1, y = tail_fwd(x4, 1)

    loss_box = []

    def loss_head(fn, rows, whole, outs, sums):
        def with_loss(yv, tv, *rest):
            dy, part = _loss_fn(yv, tv)
            return fn(dy, *rest) + (part,)

        res = _rowwise(with_loss, [y[0], target] + rows, whole, outs, list(sums) + [((1, LANES), F32)],
                       name="loss_ln_ple_bwd1")
        loss_box.append(res[-1])
        return res

    dx3_part, dh1, dh1sum = tail_bwd(loss_head, sv1, 1, x3[0], h1)
    loss = loss_box[0]
    gs["b_out"] = dh1sum
    grad_of("b_w_out", 0, ao, dh1)
    dao = mm(dh1, wget("b_w_out", 0, None), tb=True, name="att_out_dx")
    dq, dkv_cur, dkv_prev, dqsum, dsinks = _swa_bwd(q, kv, sinks, dao)
    gs["b_q"], gs["sinks"] = dqsum, dsinks
    dkv, dkvsum = _kv_grad_combine(dkv_cur, dkv_prev)
    gs["kv_b"] = dkvsum
    grad_of("b_w_q", 0, x3[1], dq)
    grad_of("kv_w", 0, x3[1], dkv)
    dx3 = mm(dq, wget("b_w_q", 0, None), tb=True, add=dx3_part, name="q_proj_dx")

    def kv_head(*post):
        return mm(dkv, wget("kv_w", 0, None), tb=True, add=dx3, name="kv_proj_dx_ln_ple_bwd0", post=post)

    dx_part, dh0, _ = tail_bwd(kv_head, sv0, 0, x, h0)
    grad_of("a_w_out", 0, og, dh0)
    dog = mm(dh0, wget("a_w_out", 0, None), tb=True, name="hg_out_dx")
    dqr, dfr, dvr, dgr, dalb, dgain = _hgrn2_bwd(proj, o_pre, states, scores, dog, alb, norm_gain, rb=HG_ROWS)
    gs["alb"], gs["norm_gain"] = dalb, dgain
    dproj = jnp.concatenate([dqr, dfr, dvr, dgr], axis=1)
    grad_of("a_w_in", 0, x, dproj)
    grad_x = mm(dproj, wget("a_w_in", 0, None), tb=True, add=dx_part, name="hg_proj_dx")
    return loss, grad_x, gs


HBM_SPEC = pl.BlockSpec(memory_space=pl.ANY)
HBM_ONLY = pl.BlockSpec(memory_space=pltpu.HBM)
SEM_SPEC = pl.BlockSpec(memory_space=pltpu.SEMAPHORE)
SIDE_EFFECT = pltpu.SideEffectType.DATAFLOW_SIDE_EFFECTING


def _slot(kind, j):
    return (j % 2) * 2 + j // 2 if kind == "colp" else j


def _piece(ref, kind, j):
    _, r, c = ref.shape
    if kind == "row":
        return ref.at[:, pl.ds(j * (r // N_CHIPS), r // N_CHIPS), :]
    return ref.at[:, :, pl.ds(_slot(kind, j) * (c // N_CHIPS), c // N_CHIPS)]


def _chip_of(j, c):
    return (j // 2, j % 2, c)


def _in_hbm(a):
    return pltpu.with_memory_space_constraint(a, pltpu.HBM)


def _place(src, kind, chip, *, mode, name, out_dtype, zone=None, zone_shape=None, layer=0, after=None):
    if mode == "gather":
        _, r, c = src.shape
        out_shape = (1, r * N_CHIPS, c) if kind == "row" else (1, r, c * N_CHIPS)
    else:
        out_shape = tuple(zone.shape) if zone is not None else tuple(zone_shape)
        r, c = out_shape[-2:]
    tm = _pick_rows(r, 512)
    nb = r // tm

    def full_idx(i, chip_ref):
        return (0, chip_ref[0] * nb + i, 0) if kind == "row" else (0, i, _slot(kind, chip_ref[0]))

    if mode == "gather":
        in_spec = pl.BlockSpec((None, tm, c), lambda i, chip_ref: (0, i, 0))
        out_spec = pl.BlockSpec((None, tm, c), full_idx)
    else:
        in_spec = pl.BlockSpec((None, tm, c), full_idx)
        out_spec = pl.BlockSpec((None, None, tm, c), lambda i, chip_ref: (chip_ref[0], layer, i, 0))
    in_specs, operands, aliases = [in_spec], [src], {}
    if zone is not None:
        in_specs.append(HBM_SPEC)
        operands.append(zone)
        aliases = {2: 0}
    if after is not None:
        in_specs.append(HBM_SPEC)
        operands.append(after)

    def body(chip_ref, src_ref, *rest):
        rest[-1][...] = src_ref[...].astype(rest[-1].dtype)

    return pl.pallas_call(
        body, name=name,
        grid_spec=pltpu.PrefetchScalarGridSpec(num_scalar_prefetch=1, grid=(nb,), in_specs=in_specs,
                                               out_specs=out_spec),
        out_shape=jax.ShapeDtypeStruct(out_shape, out_dtype),
        input_output_aliases=aliases,
        compiler_params=_params(("arbitrary",)),
    )(chip, *operands)


def _half(ref, c):
    h = ref.shape[1] // 2
    start = c * h if isinstance(c, int) else pl.multiple_of(c * h, 16)
    return ref.at[:, pl.ds(start, h), :]


def _sibling_fill(lands, kinds, name):
    n = len(lands)

    def body(*refs):
        land_refs = refs[n:2 * n]
        send_sems, recv_sems = refs[2 * n], refs[2 * n + 1]
        x, y, c = lax.axis_index("x"), lax.axis_index("y"), lax.axis_index("c")
        me = 2 * x + y
        for t in range(N_CHIPS):
            @pl.when(me != t)
            def _():
                copies = []
                for a in range(n):
                    got = _half(_piece(land_refs[a], kinds[a], t), c)
                    copies.append(pltpu.make_async_remote_copy(
                        src_ref=got, dst_ref=got, send_sem=send_sems.at[a * N_CHIPS + t],
                        recv_sem=recv_sems.at[a * N_CHIPS + t], device_id=(x, y, 1 - c), device_id_type=MESH))
                for cp in copies:
                    cp.start()
                for cp in copies:
                    cp.wait()

    return pl.pallas_call(
        body, name=name,
        in_specs=[HBM_SPEC] * n, out_specs=[HBM_SPEC] * n,
        out_shape=[jax.ShapeDtypeStruct(a.shape, a.dtype) for a in lands],
        input_output_aliases={i: i for i in range(n)},
        scratch_shapes=[pltpu.SemaphoreType.DMA((n * N_CHIPS,)), pltpu.SemaphoreType.DMA((n * N_CHIPS,))],
    )(*lands)


class _Exchange:
    def __init__(self, mode, srcs, lands, kinds, layers, name, after=None, halves=None):
        self.mode, self.kinds, self.layers, self.name, self.n = mode, kinds, layers, name, len(lands)
        self.halves = halves if halves is not None else [False] * len(lands)
        n, ns = self.n, len(srcs)
        n_in = ns + n + (after is not None)
        sem_shape = pltpu.SemaphoreType.DMA((n * N_CHIPS,))

        def body(*refs):
            src_refs, land_refs = refs[:ns], refs[ns:ns + n]
            send_sems, recv_sems = refs[n_in], refs[n_in + 1]
            token = refs[-1]
            c = lax.axis_index("c")
            me = 2 * lax.axis_index("x") + lax.axis_index("y")
            for j in range(N_CHIPS):
                @pl.when(me == j)
                def _():
                    for a in range(n):
                        for t in range(N_CHIPS):
                            if t != j:
                                src, dst = self._ends(src_refs, land_refs, a, j, t, c)
                                pltpu.make_async_remote_copy(
                                    src_ref=src, dst_ref=dst, send_sem=send_sems.at[a * N_CHIPS + t],
                                    recv_sem=recv_sems.at[a * N_CHIPS + j],
                                    device_id=_chip_of(t, c), device_id_type=MESH).start()
            token[...] = jnp.zeros(token.shape, token.dtype)

        arrays = list(srcs) + list(lands)
        operands = [_in_hbm(a) for a in arrays]
        in_specs = [HBM_ONLY] * (ns + n)
        if after is not None:
            operands.append(after)
            in_specs.append(HBM_SPEC)
        outs = pl.pallas_call(
            body, name=name + "_start",
            in_specs=in_specs,
            out_specs=[SEM_SPEC, SEM_SPEC] + [HBM_ONLY] * (ns + n) + [pl.BlockSpec(memory_space=pltpu.VMEM)],
            out_shape=[sem_shape, sem_shape] + [pltpu.HBM(a.shape, a.dtype) for a in arrays]
                      + [jax.ShapeDtypeStruct((8, LANES), F32)],
            input_output_aliases={i: i + 2 for i in range(ns + n)},
            compiler_params=pltpu.CompilerParams(has_side_effects=SIDE_EFFECT),
        )(*operands)
        self.send_sems, self.recv_sems = outs[0], outs[1]
        self.srcs, self.lands = list(outs[2:2 + ns]), list(outs[2 + ns:2 + ns + n])
        self.token = outs[-1]

    def _ends(self, src_refs, land_refs, a, me_j, peer, c):
        if self.mode == "gather":
            mine = _piece(land_refs[a], self.kinds[a], me_j)
            if self.halves[a]:
                mine = _half(mine, c)
            return mine, mine
        return _piece(src_refs[a], self.kinds[a], peer), land_refs[a].at[me_j, pl.ds(self.layers[a], 1)]

    def wait(self, after, lands=None):
        n, ns = self.n, len(self.srcs)
        lands = self.lands if lands is None else lands

        def body(*refs):
            src_refs, land_refs = refs[:ns], refs[ns:ns + n]
            send_sems, recv_sems = refs[ns + n], refs[ns + n + 1]
            c = lax.axis_index("c")
            me = 2 * lax.axis_index("x") + lax.axis_index("y")
            for j in range(N_CHIPS):
                @pl.when(me != j)
                def _():
                    for a in range(n):
                        sent, _ = self._ends(src_refs, land_refs, a, 0, j, c)
                        _, landed = self._ends(src_refs, land_refs, a, j, 0, c)
                        cp = pltpu.make_async_remote_copy(
                            src_ref=sent, dst_ref=landed, send_sem=send_sems.at[a * N_CHIPS + j],
                            recv_sem=recv_sems.at[a * N_CHIPS + j],
                            device_id=_chip_of(j, c), device_id_type=MESH)
                        cp.wait_send()
                        cp.wait_recv()

        arrays = self.srcs + list(lands)
        operands = [_in_hbm(a) for a in arrays] + [self.send_sems, self.recv_sems]
        in_specs = [HBM_ONLY] * (ns + n) + [SEM_SPEC, SEM_SPEC]
        if after is not None:
            operands.append(after)
            in_specs.append(HBM_SPEC)
        outs = pl.pallas_call(
            body, name=self.name + "_wait",
            in_specs=in_specs, out_specs=[HBM_ONLY] * (ns + n),
            out_shape=[pltpu.HBM(a.shape, a.dtype) for a in arrays],
            input_output_aliases={i: i for i in range(ns + n)},
            compiler_params=pltpu.CompilerParams(has_side_effects=SIDE_EFFECT),
        )(*operands)
        return list(outs[ns:])


def _sibling_swap(arrays, name):
    n = len(arrays)

    def body(*refs):
        ins, outs = refs[:n], refs[n:2 * n]
        send_sems, recv_sems = refs[2 * n:]
        sibling = (lax.axis_index("x"), lax.axis_index("y"), 1 - lax.axis_index("c"))
        copies = [pltpu.make_async_remote_copy(src_ref=ins[a], dst_ref=outs[a], send_sem=send_sems.at[a],
                                               recv_sem=recv_sems.at[a], device_id=sibling, device_id_type=MESH)
                  for a in range(n)]
        for cp in copies:
            cp.start()
        for cp in copies:
            cp.wait()

    return pl.pallas_call(
        body, name=name,
        in_specs=[HBM_SPEC] * n, out_specs=[HBM_SPEC] * n,
        out_shape=[jax.ShapeDtypeStruct(a.shape, a.dtype) for a in arrays],
        scratch_shapes=[pltpu.SemaphoreType.DMA((n,)), pltpu.SemaphoreType.DMA((n,))],
    )(*arrays)


def _gather_devices(vec):
    def body(in_ref, out_ref, send_sems, recv_sems, local_sem):
        x, y, c = lax.axis_index("x"), lax.axis_index("y"), lax.axis_index("c")
        me = 4 * x + 2 * y + c
        mine = pltpu.make_async_copy(in_ref, out_ref.at[me], local_sem)
        mine.start()
        copies = []
        for rel in range(1, N_DEV):
            peer = (x ^ (rel >> 2), y ^ ((rel >> 1) & 1), c ^ (rel & 1))
            copies.append(pltpu.make_async_remote_copy(
                src_ref=in_ref, dst_ref=out_ref.at[me], send_sem=send_sems.at[rel], recv_sem=recv_sems.at[rel],
                device_id=peer, device_id_type=MESH))
        for cp in copies:
            cp.start()
        for cp in copies:
            cp.wait()
        mine.wait()

    return pl.pallas_call(
        body, name="gather_small",
        in_specs=[HBM_SPEC], out_specs=HBM_SPEC,
        out_shape=jax.ShapeDtypeStruct((N_DEV,) + vec.shape, vec.dtype),
        scratch_shapes=[pltpu.SemaphoreType.DMA((N_DEV,)), pltpu.SemaphoreType.DMA((N_DEV,)),
                        pltpu.SemaphoreType.DMA],
    )(vec)


BIG = [("a_w_in", "col"), ("a_w_out", "row"), ("kv_w", "row"), ("b_w_q", "row"), ("b_w_out", "row"),
       ("ffn_w_gate_up", "colp"), ("ffn_w_down", "row"), ("ple_w_up", "col"), ("ple_w_gate", "row")]
GATHER_GROUPS = [[("a_w_in", 0), ("small", 0)], [("a_w_out", 0), ("ffn_w_gate_up", 0)],
                 [("ffn_w_down", 0), ("ple_w_gate", 0), ("ple_w_up", 0)], [("kv_w", 0), ("b_w_q", 0), ("b_w_out", 0)],
                 [("ffn_w_gate_up", 1)], [("ffn_w_down", 1), ("ple_w_gate", 1), ("ple_w_up", 1)]]
SCATTER_GROUPS = [[("ple_w_gate", 1), ("ple_w_up", 1), ("ffn_w_down", 1)], [("ffn_w_gate_up", 1)],
                  [("b_w_out", 0), ("b_w_q", 0), ("kv_w", 0)], [("ple_w_gate", 0), ("ple_w_up", 0), ("ffn_w_down", 0)],
                  [("ffn_w_gate_up", 0), ("a_w_out", 0)], [("a_w_in", 0)]]
SMALL_SHARDED = ["ln_gain", "ln_bias", "a_lower_bound"]
SMALL_REPLICATED = ["a_norm_gain", "kv_b", "b_b_q", "b_sinks", "b_b_out", "ple_b_gate"]
WEIGHT_ORDER = ["a_w_in", "a_lower_bound", "a_norm_gain", "a_w_out", "kv_w", "kv_b", "b_w_q", "b_b_q", "b_sinks",
                "b_w_out", "b_b_out", "ffn_w_gate_up", "ffn_w_down", "ple_w_up", "ple_w_gate", "ple_b_gate",
                "ln_gain", "ln_bias"]


def _as3(a):
    return a.reshape((-1,) + a.shape[-2:]) if a.ndim >= 3 else a.reshape((1,) + a.shape)


def _pad_lanes(v):
    n = v.shape[-1]
    return jnp.pad(v, ((0, 0), (0, (-n) % LANES)))


def _adam_small_fn(w, mom, vel, g):
    return _adam_fn(w, mom, vel, g, jnp.zeros_like(g))[1:]


def _sum_rows_fn(slots):
    acc = slots[0]
    for s in range(1, slots.shape[0]):
        acc = acc + slots[s]
    return (acc,)


def kernel(x, p, a_w_in, a_lower_bound, a_norm_gain, a_w_out, kv_w, kv_b, b_w_q, b_b_q, b_sinks, b_w_out, b_b_out, ffn_w_gate_up, ffn_w_down, ple_w_up, ple_w_gate, ple_b_gate, ln_gain, ln_bias, loss_target, m_a_w_in, m_a_lower_bound, m_a_norm_gain, m_a_w_out, m_kv_w, m_kv_b, m_b_w_q, m_b_b_q, m_b_sinks, m_b_w_out, m_b_b_out, m_ffn_w_gate_up, m_ffn_w_down, m_ple_w_up, m_ple_w_gate, m_ple_b_gate, m_ln_gain, m_ln_bias, v_a_w_in, v_a_lower_bound, v_a_norm_gain, v_a_w_out, v_kv_w, v_kv_b, v_b_w_q, v_b_b_q, v_b_sinks, v_b_w_out, v_b_b_out, v_ffn_w_gate_up, v_ffn_w_down, v_ple_w_up, v_ple_w_gate, v_ple_b_gate, v_ln_gain, v_ln_bias):
    args = dict(locals())
    wts = {n: args[n] for n in WEIGHT_ORDER}
    mom = {n: args["m_" + n] for n in WEIGHT_ORDER}
    vel = {n: args["v_" + n] for n in WEIGHT_ORDER}
    chip = 2 * lax.axis_index("x") + lax.axis_index("y")
    d = x.shape[-1]
    dq = d // N_CHIPS

    kind_of = dict(BIG)
    kind_of["small"] = "col"
    chip_arr = chip.reshape(1).astype(jnp.int32)
    small_pack = jnp.concatenate([wts[n].reshape(-1, dq) for n in SMALL_SHARDED], axis=0)[None]

    def place(key, after):
        n, layer = key
        if n == "small":
            return _place(small_pack, "col", chip_arr, mode="gather", name="place_small", out_dtype=F32, after=after)
        return _place(_as3(wts[n])[layer:layer + 1], kind_of[n], chip_arr, mode="gather",
                      name=f"place_{n}{layer}", out_dtype=BF16, after=after)

    gathers, where = [], {}
    for gi, group in enumerate(GATHER_GROUPS):
        prev = gathers[-1].token if gathers else None
        gathers.append(_Exchange("gather", [], [place(k, prev) for k in group], [kind_of[k[0]] for k in group],
                                 [0] * len(group), f"gather{gi}", after=prev,
                                 halves=[k[0] != "small" for k in group]))
        for k in group:
            where[k] = gi
    all_started = gathers[-1].token
    ready = {}

    def wget(name, layer, after):
        key = (name, layer)
        if key not in ready:
            gi = where[key]
            group = GATHER_GROUPS[gi]
            outs = gathers[gi].wait(all_started if gi == 0 else after)
            split = [i for i, k in enumerate(group) if k[0] != "small"]
            filled = _sibling_fill([outs[i] for i in split], [kind_of[group[i][0]] for i in split], f"fill{gi}")
            for i, arr in zip(split, filled):
                outs[i] = arr
            for k, arr in zip(group, outs):
                ready[k] = arr
        return ready[key]

    small_full = wget("small", 0, None)[0]
    ln_gain_f = small_full[0:6].reshape(DEPTH, 3, d)
    ln_bias_f = small_full[6:12].reshape(DEPTH, 3, d)
    alb_f = small_full[12:14]

    group_of = {k: gi for gi, group in enumerate(SCATTER_GROUPS) for k in group}
    grads_done, zones, scatters = {}, {}, []

    def grad_sink(name, layer, grad):
        grads_done[(name, layer)] = grad
        zones[name] = _place(grad, kind_of[name], chip_arr, mode="scatter", name=f"place_grad_{name}{layer}",
                             out_dtype=BF16, zone=zones.get(name), zone_shape=(N_CHIPS,) + _as3(wts[name]).shape,
                             layer=layer)
        gi = group_of[(name, layer)]
        group = SCATTER_GROUPS[gi]
        if not all(k in grads_done for k in group):
            return None
        ex = _Exchange("scatter", [grads_done[k] for k in group], [zones[k[0]] for k in group],
                       [kind_of[k[0]] for k in group], [k[1] for k in group], f"scatter{gi}")
        for k, zone in zip(group, ex.lands):
            zones[k[0]] = zone
        scatters.append((ex, group))
        return ex.token

    loss, grad_x, gs = _local_step(
        x[0], p[:, 0], loss_target[0], wget, grad_sink, ln_gain_f, ln_bias_f, alb_f, a_norm_gain, kv_b, b_b_q,
        b_sinks, b_b_out, ple_b_gate)

    res = {}

    def arrive(batch, after):
        for ex, group in batch:
            outs = ex.wait(after, lands=[zones[k[0]] for k in group])
            for k, zone in zip(group, outs):
                zones[k[0]] = zone

    def update(names, tag):
        partial = []
        for n in names:
            s2 = zones[n].reshape(N_CHIPS, -1, zones[n].shape[-1])
            partial.append(_rowwise(_sum_slots_fn, [s2], [], [(s2.shape[1:], BF16)], name=f"sum_{n}")[0])
        sibling = _sibling_swap(partial, tag)
        for n, own, sib in zip(names, partial, sibling):
            shp = wts[n].shape
            flat = lambda a: a.reshape(-1, shp[-1])
            out = _rowwise(_adam_fn, [flat(wts[n]), flat(mom[n]), flat(vel[n]), own, sib], [],
                           [(own.shape, F32)] * 4, name=f"adam_{n}")
            res[n] = [o.reshape(shp) for o in out]
        return res[names[-1]][1]

    last_names = [k[0] for k in SCATTER_GROUPS[-1]]
    arrive(scatters[:-1], grad_x)
    updated = update([n for n, _ in BIG if n not in last_names], "sibling_swap")
    arrive(scatters[-1:], updated)
    update(last_names, "sibling_swap_last")

    ln_g = jnp.concatenate([gs[f"ln_gain_{i}_{j}"] for i in range(DEPTH) for j in range(3)], axis=0)
    ln_b = jnp.concatenate([gs[f"ln_bias_{i}_{j}"] for i in range(DEPTH) for j in range(3)], axis=0)
    ple_bg = jnp.concatenate([gs[f"ple_b_{i}"] for i in range(DEPTH)], axis=0)
    small_list = [ln_g.reshape(1, -1), ln_b.reshape(1, -1), gs["alb"].reshape(1, -1), gs["norm_gain"],
                  gs["kv_b"], gs["b_q"], _pad_lanes(gs["sinks"]), gs["b_out"], ple_bg.reshape(1, -1), loss]
    small_vec = jnp.concatenate(small_list, axis=1)
    everyone = _gather_devices(small_vec)
    total, = _rowwise(_sum_rows_fn, [everyone], [], [(small_vec.shape, F32)], name="sum_small")
    offs, pos = [], 0
    for v in small_list:
        offs.append((pos, v.shape[1]))
        pos += v.shape[1]

    def seg(k):
        return total[0, offs[k][0]:offs[k][0] + offs[k][1]]

    def my_cols(full, rows):
        return lax.dynamic_slice_in_dim(full.reshape(rows, N_CHIPS, dq), chip, 1, axis=1).reshape(rows, dq)

    n_sink = b_sinks.shape[-1]
    small_grads = {
        "ln_gain": my_cols(seg(0), 6).reshape(ln_gain.shape), "ln_bias": my_cols(seg(1), 6).reshape(ln_bias.shape),
        "a_lower_bound": my_cols(seg(2), 2), "a_norm_gain": seg(3).reshape(a_norm_gain.shape),
        "kv_b": seg(4).reshape(kv_b.shape), "b_b_q": seg(5).reshape(b_b_q.shape),
        "b_sinks": seg(6)[:n_sink].reshape(b_sinks.shape), "b_b_out": seg(7).reshape(b_b_out.shape),
        "ple_b_gate": seg(8).reshape(ple_b_gate.shape)}
    names = SMALL_SHARDED + SMALL_REPLICATED
    pack = lambda dct: _pad_lanes(jnp.concatenate([dct[n].reshape(1, -1) for n in names], axis=1))
    g_pack = pack(small_grads)
    upd = _rowwise(_adam_small_fn, [pack(wts), pack(mom), pack(vel), g_pack], [], [(g_pack.shape, F32)] * 3,
                   name="adam_small")
    pos = 0
    for n in names:
        size = wts[n].size
        res[n] = [small_grads[n]] + [u[0, pos:pos + size].reshape(wts[n].shape) for u in upd]
        pos += size

    outs = [seg(9)[0], grad_x[None]]
    for k in range(4):
        outs += [res[n][k] for n in WEIGHT_ORDER]
    return tuple(outs)
```

```python
import functools

import jax
import jax.numpy as jnp
from jax import lax
from jax.experimental import pallas as pl
from jax.experimental.pallas import tpu as pltpu

F32 = jnp.float32
BF16 = jnp.bfloat16
MESH = pl.DeviceIdType.MESH

LANES = 128
HG_DK = 128
HG_CHUNK = 64
HG_SUB = 16
HG_ROWS = 512
HG_HEADS_PER_STEP = 2
ATT_HD = 64
ATT_G = 4
WINDOW = 128
DEPTH = 2
ALPHA = (2.0 * DEPTH) ** 0.25
LN_EPS = 1e-5
RMS_EPS = 1e-6
ADAM_LR, ADAM_B1, ADAM_B2, ADAM_EPS, ADAM_WD, ADAM_STEP = 0.001, 0.9, 0.999, 1e-08, 0.01, 10
N_CHIPS = 4
N_DEV = 8
VMEM_LIMIT = 56 * 1024 * 1024
NEG = -1e30


def _pick(n, cap):
    best = None
    for d in range(LANES, min(n, cap) + 1, LANES):
        if n % d == 0:
            best = d
    return n if best is None else best


def _pick_rows(m, cap):
    best = None
    for d in range(16, min(m, cap) + 1, 16):
        if m % d == 0:
            best = d
    return m if best is None else best


def _params(sem):
    return pltpu.CompilerParams(dimension_semantics=sem, vmem_limit_bytes=VMEM_LIMIT)


def _zeros_index(ndim):
    return lambda i, j, kk: (0,) * ndim


def _mm(a, b, *, name, la=None, lb=None, ta=False, tb=False, bias=None, add=None, out_dtype=F32,
        out_layers=None, out_layer=None, after=None, post=None, tile_cols=None, caps=(1024, 1536, 2048)):
    ar, ac = a.shape[-2:]
    br, bc = b.shape[-2:]
    m, k = (ac, ar) if ta else (ar, ac)
    k2, n = (bc, br) if tb else (br, bc)
    assert k == k2, (a.shape, b.shape, ta, tb)
    if post is not None:
        caps = (512, n if tile_cols is None else tile_cols, caps[2])
    tm, tn, tk = _pick(m, caps[0]), _pick(n, caps[1]), _pick(k, caps[2])
    assert post is None or tn == caps[1]
    nk = k // tk
    grid = (m // tm, n // tn, nk)

    def spec(block, idx, layer):
        if layer is None:
            return pl.BlockSpec(block, idx)
        return pl.BlockSpec((None,) + block, lambda i, j, kk: (layer,) + idx(i, j, kk))

    a_spec = spec((tk, tm), lambda i, j, kk: (kk, i), la) if ta else spec((tm, tk), lambda i, j, kk: (i, kk), la)
    b_spec = spec((tn, tk), lambda i, j, kk: (j, kk), lb) if tb else spec((tk, tn), lambda i, j, kk: (kk, j), lb)
    in_specs, operands = [a_spec, b_spec], [a, b]
    if bias is not None:
        in_specs.append(pl.BlockSpec((1, tn), lambda i, j, kk: (0, j)))
        operands.append(bias)
    if add is not None:
        in_specs.append(pl.BlockSpec((tm, tn), lambda i, j, kk: (i, j)))
        operands.append(add)
    if after is not None:
        in_specs.append(pl.BlockSpec(memory_space=pl.ANY))
        operands.append(after)
    dims = (((0 if ta else 1,), (1 if tb else 0,)), ((), ()))
    has_bias, has_add = bias is not None, add is not None
    if post is None:
        fn, rows, whole, outs, sums = None, [], [], [], []
        out_shape = jax.ShapeDtypeStruct((m, n) if out_layers is None else (out_layers, m, n), out_dtype)
        out_specs = spec((tm, tn), lambda i, j, kk: (i, j), out_layer)
    else:
        fn, rows, whole, outs, sums = post
        in_specs += [pl.BlockSpec((tm, r.shape[-1] // (n // tn)), lambda i, j, kk: (i, j)) for r in rows]
        in_specs += [pl.BlockSpec(tuple(w.shape), _zeros_index(w.ndim)) for w in whole]
        operands += list(rows) + list(whole)
        out_shape = [jax.ShapeDtypeStruct(sh, dt) for sh, dt in list(outs) + list(sums)]
        out_specs = ([pl.BlockSpec((tm, sh[-1] // (n // tn)), lambda i, j, kk: (i, j)) for sh, _ in outs]
                     + [pl.BlockSpec(tuple(sh), _zeros_index(len(sh))) for sh, _ in sums])
    n_in, n_extra, n_outs, n_sums = len(operands), len(rows) + len(whole), len(outs), len(sums)

    def body(*refs):
        a_ref, b_ref = refs[0], refs[1]
        pos = 2
        bias_ref = add_ref = None
        if has_bias:
            bias_ref = refs[pos]
            pos += 1
        if has_add:
            add_ref = refs[pos]
            pos += 1
        extra_refs = refs[n_in - n_extra:n_in]
        out_refs = refs[n_in:n_in + max(n_outs, 1)]
        sum_refs = refs[n_in + n_outs:n_in + n_outs + n_sums]
        acc_ref = refs[-1] if nk > 1 else None
        part = lax.dot_general(a_ref[...].astype(BF16), b_ref[...].astype(BF16), dims, preferred_element_type=F32)

        def finish(total):
            if has_bias:
                total = total + bias_ref[...]
            if has_add:
                total = total + add_ref[...]
            if fn is None:
                out_refs[0][...] = total.astype(out_refs[0].dtype)
                return
            res = fn(total, *[r[...] for r in extra_refs])
            for ref, val in zip(out_refs, res[:n_outs]):
                ref[...] = val.astype(ref.dtype)
            if n_sums:
                @pl.when(pl.program_id(0) == 0)
                def _():
                    for ref in sum_refs:
                        ref[...] = jnp.zeros(ref.shape, ref.dtype)

                for ref, val in zip(sum_refs, res[n_outs:]):
                    ref[...] += val

        if nk == 1:
            finish(part)
        else:
            kk = pl.program_id(2)

            @pl.when(kk == 0)
            def _():
                acc_ref[...] = part

            @pl.when(kk > 0)
            def _():
                acc_ref[...] += part

            @pl.when(kk == nk - 1)
            def _():
                finish(acc_ref[...])

    return pl.pallas_call(
        body, name=name, grid=grid, in_specs=in_specs, out_specs=out_specs, out_shape=out_shape,
        scratch_shapes=[pltpu.VMEM((tm, tn), F32)] if nk > 1 else [],
        compiler_params=_params(("arbitrary" if n_sums else "parallel", "parallel", "arbitrary")),
    )(*operands)


def _rowwise(fn, rows, whole, outs, sums=(), *, name, tm=256):
    m = rows[0].shape[-2]
    tm = _pick_rows(m, tm)
    n_rows, n_whole, n_outs, n_sums = len(rows), len(whole), len(outs), len(sums)

    def rspec(shape):
        lead = len(shape) - 2
        return pl.BlockSpec(tuple(shape[:-2]) + (tm, shape[-1]), lambda i: (0,) * lead + (i, 0))

    def wspec(shape):
        return pl.BlockSpec(tuple(shape), lambda i: (0,) * len(shape))

    def body(*refs):
        vals = [r[...] for r in refs[:n_rows + n_whole]]
        out_refs = refs[n_rows + n_whole:n_rows + n_whole + n_outs]
        sum_refs = refs[n_rows + n_whole + n_outs:]
        res = fn(*vals)
        for ref, val in zip(out_refs, res[:n_outs]):
            ref[...] = val.astype(ref.dtype)
        if n_sums:
            @pl.when(pl.program_id(0) == 0)
            def _():
                for ref in sum_refs:
                    ref[...] = jnp.zeros(ref.shape, ref.dtype)

            for ref, val in zip(sum_refs, res[n_outs:]):
                ref[...] += val

    result = pl.pallas_call(
        body, name=name, grid=(m // tm,),
        in_specs=[rspec(r.shape) for r in rows] + [wspec(w.shape) for w in whole],
        out_specs=[rspec(s) for s, _ in outs] + [wspec(s) for s, _ in sums],
        out_shape=[jax.ShapeDtypeStruct(s, d) for s, d in list(outs) + list(sums)],
        compiler_params=_params(("arbitrary",)),
    )(*rows, *whole)
    return result


def _sigmoid(v):
    return jax.nn.sigmoid(v)


def _col_sum(v):
    return jnp.sum(v, axis=0, keepdims=True)


def _ln_stats(z):
    mu = jnp.mean(z, axis=-1, keepdims=True)
    zc = z - mu
    var = jnp.mean(zc * zc, axis=-1, keepdims=True)
    rstd = lax.rsqrt(var + LN_EPS)
    return zc * rstd, rstd


def _ln_fwd_fn(xin, h, gain, bias):
    xhat, _ = _ln_stats(ALPHA * xin + h)
    y = xhat * gain + bias
    return y, y


def _ple_ln_fwd_fn(xin, pg, pu, gain, bias):
    xhat, _ = _ln_stats(ALPHA * xin + _sigmoid(pg) * pu)
    y = xhat * gain + bias
    return y, y


def _ln_dz(dy, z, gain):
    xhat, rstd = _ln_stats(z)
    dxhat = dy * gain
    dz = rstd * (dxhat - jnp.mean(dxhat, axis=-1, keepdims=True)
                 - xhat * jnp.mean(dxhat * xhat, axis=-1, keepdims=True))
    return dz, _col_sum(dy * xhat), _col_sum(dy)


def _ln_bwd_fn(dy, xin, h, gain):
    dz, dgain, dbias = _ln_dz(dy, ALPHA * xin + h, gain)
    return ALPHA * dz, dz, dgain, dbias, _col_sum(dz)


def _ple_ln_bwd_fn(dy, xin, pg, pu, gain):
    sg = _sigmoid(pg)
    dz, dgain, dbias = _ln_dz(dy, ALPHA * xin + sg * pu, gain)
    dpg = dz * pu * sg * (1.0 - sg)
    return ALPHA * dz, dpg, dz * sg, dgain, dbias, _col_sum(dpg)


def _swiglu_fwd_fn(gu):
    hid = gu.shape[-1] // 2
    gate, up = gu[:, :hid], gu[:, hid:]
    return gu, gate * _sigmoid(gate) * up


def _swiglu_bwd_fn(dact, gu):
    gu = gu.astype(F32)
    hid = gu.shape[-1] // 2
    gate, up = gu[:, :hid], gu[:, hid:]
    sg = _sigmoid(gate)
    dgate = dact * up * sg * (1.0 + gate * (1.0 - sg))
    dup = dact * gate * sg
    return (jnp.concatenate([dgate, dup], axis=-1),)


def _loss_fn(y, target):
    err = y - target
    inv = 1.0 / y.shape[-1]
    part = 0.5 * inv * jnp.sum(jnp.sum(err * err, axis=-1, keepdims=True), axis=0, keepdims=True)
    return err * inv, jnp.broadcast_to(part, (1, LANES))


def _adam_fn(w, mom, vel, p_own, p_sib):
    g = p_own.astype(F32) + p_sib.astype(F32)
    m_new = ADAM_B1 * mom + (1.0 - ADAM_B1) * g
    v_new = ADAM_B2 * vel + (1.0 - ADAM_B2) * (g * g)
    m_hat = m_new / (1.0 - ADAM_B1 ** ADAM_STEP)
    v_hat = v_new / (1.0 - ADAM_B2 ** ADAM_STEP)
    delta = -ADAM_LR * (m_hat / (jnp.sqrt(v_hat) + ADAM_EPS) + ADAM_WD * w)
    return g, delta, m_new, v_new


def _sum_slots_fn(slots):
    acc = slots[0].astype(F32)
    for s in range(1, slots.shape[0]):
        acc = acc + slots[s].astype(F32)
    return (acc,)


def _split2(x):
    hi = x.astype(BF16)
    return hi, (x - hi.astype(F32)).astype(BF16)


def _dot3(a, b, dims):
    a_hi, a_lo = _split2(a)
    b_hi, b_lo = _split2(b)
    dn = (dims, ((), ()))
    return (lax.dot_general(a_hi, b_hi, dn, preferred_element_type=F32)
            + (lax.dot_general(a_hi, b_lo, dn, preferred_element_type=F32)
               + lax.dot_general(a_lo, b_hi, dn, preferred_element_type=F32)))


def _tdot(mask01, b):
    m = mask01.astype(BF16)
    b_hi = b.astype(BF16)
    rest = b - b_hi.astype(F32)
    b_mid = rest.astype(BF16)
    b_lo = (rest - b_mid.astype(F32)).astype(BF16)
    dn = (((1,), (0,)), ((), ()))
    return (lax.dot_general(m, b_hi, dn, preferred_element_type=F32)
            + (lax.dot_general(m, b_mid, dn, preferred_element_type=F32)
               + lax.dot_general(m, b_lo, dn, preferred_element_type=F32)))


def _hdot(a, b):
    return _dot3(a, b, ((1,), (0,)))


def _hdot_nt(a, b):
    return _dot3(a, b, ((1,), (1,)))


def _hdot_tn(a, b):
    return _dot3(a, b, ((0,), (0,)))


def _dot(a, b):
    return lax.dot_general(a.astype(BF16), b.astype(BF16), (((1,), (0,)), ((), ())), preferred_element_type=F32)


def _dot_nt(a, b):
    return lax.dot_general(a.astype(BF16), b.astype(BF16), (((1,), (1,)), ((), ())), preferred_element_type=F32)


def _dot_tn(a, b):
    return lax.dot_general(a.astype(BF16), b.astype(BF16), (((0,), (0,)), ((), ())), preferred_element_type=F32)


def _hg_masks():
    c = HG_CHUNK
    row = lax.broadcasted_iota(jnp.int32, (c, c), 0)
    col = lax.broadcasted_iota(jnp.int32, (c, c), 1)
    base = row & (-HG_SUB)
    return row, col, base, col <= row, col < base


def _hg_gates(qr, fr, alb):
    lbound = _sigmoid(alb[0:1, :] - alb[1:2, :])
    sig = _sigmoid(fr)
    forget = lbound + (1.0 - lbound) * sig
    kk = (1.0 - lbound) * _sigmoid(-fr)
    qt = qr * _sigmoid(qr) * (HG_DK ** -0.5)
    return qt, kk, jnp.log(forget), lbound, sig, forget


def _hg_scores(qt, kk, g, scores=True):
    c, nsub = HG_CHUNK, HG_CHUNK // HG_SUB
    row, col, base, causal, below = _hg_masks()
    b = _tdot(causal, g)
    rr = _tdot(below, g)
    bq = b - rr
    qh = qt * jnp.exp(bq)
    edecs = [None]
    parts = [jnp.zeros((HG_SUB, c), F32)]
    for i in range(1, nsub):
        edec = jnp.exp(jnp.minimum(rr[i * HG_SUB:i * HG_SUB + 1, :] - b, 0.0))
        edecs.append(edec)
        if scores:
            parts.append(_dot_nt(qh[i * HG_SUB:(i + 1) * HG_SUB, :], kk * edec))
    b3 = b.reshape(nsub, HG_SUB, HG_DK)
    q3 = qt.reshape(nsub, HG_SUB, HG_DK)
    k3 = kk.reshape(nsub, HG_SUB, HG_DK)
    if not scores:
        return None, b, bq, qh, edecs, (b3, q3, k3)
    a = jnp.where(below, jnp.concatenate(parts, axis=0), 0.0)
    for j in range(HG_SUB):
        e = jnp.exp(jnp.minimum(b3 - b3[:, j:j + 1, :], 0.0))
        colv = jnp.sum(q3 * e * k3[:, j:j + 1, :], axis=-1, keepdims=True).reshape(c, 1)
        a = jnp.where(col == base + j, colv, a)
    a = jnp.where(causal, a, 0.0)
    return a, b, bq, qh, edecs, (b3, q3, k3)


def _hg_norm(o, gr, gain):
    r = lax.rsqrt(jnp.mean(o * o, axis=-1, keepdims=True) + RMS_EPS)
    sg = _sigmoid(gr)
    return o * r * gain, r, sg


def _hgrn2_fwd(proj, alb, gain, *, rb):
    m, d4 = proj.shape
    d = d4 // 4
    heads = d // HG_DK
    hp = HG_HEADS_PER_STEP
    rb = min(rb, m)
    cpb = rb // HG_CHUNK
    nrb = m // rb

    def body(q_ref, f_ref, v_ref, g_ref, alb_ref, gain_ref, o_ref, og_ref, st_ref, a_ref, state):
        @pl.when(pl.program_id(1) == 0)
        def _():
            state[...] = jnp.zeros(state.shape, F32)

        def chunk(ci, carry):
            sl = pl.ds(pl.multiple_of(ci * HG_CHUNK, HG_CHUNK), HG_CHUNK)
            for u in range(hp):
                ln = slice(u * HG_DK, (u + 1) * HG_DK)
                qt, kk, g, _, _, _ = _hg_gates(q_ref[sl, ln], f_ref[sl, ln], alb_ref[:, ln])
                v = v_ref[sl, ln]
                st = state[u]
                st_ref[u, ci] = st
                a, b, _, _, _, _ = _hg_scores(qt, kk, g)
                a_ref[u, ci] = a.astype(a_ref.dtype)
                o = _dot(a, v) + _dot_nt(qt * jnp.exp(b), st)
                b_last = b[HG_CHUNK - 1:HG_CHUNK, :]
                state[u] = st * jnp.exp(b_last) + _hdot_tn(v, kk * jnp.exp(b_last - b))
                o_ref[sl, ln] = o
                n, _, sg = _hg_norm(o, g_ref[sl, ln], gain_ref[...])
                og_ref[sl, ln] = (n * g_ref[sl, ln] * sg).astype(og_ref.dtype)
            return carry

        lax.fori_loop(0, cpb, chunk, 0)

    def col(cidx):
        return pl.BlockSpec((rb, hp * HG_DK), lambda h, r: (r, cidx * (heads // hp) + h))

    return pl.pallas_call(
        body, name="hgrn2_fwd", grid=(heads // hp, nrb),
        in_specs=[col(0), col(1), col(2), col(3),
                  pl.BlockSpec((2, hp * HG_DK), lambda h, r: (0, h)),
                  pl.BlockSpec((1, HG_DK), lambda h, r: (0, 0))],
        out_specs=[pl.BlockSpec((rb, hp * HG_DK), lambda h, r: (r, h)),
                   pl.BlockSpec((rb, hp * HG_DK), lambda h, r: (r, h)),
                   pl.BlockSpec((hp, cpb, HG_DK, HG_DK), lambda h, r: (h, r, 0, 0)),
                   pl.BlockSpec((hp, cpb, HG_CHUNK, HG_CHUNK), lambda h, r: (h, r, 0, 0))],
        out_shape=[jax.ShapeDtypeStruct((m, d), F32), jax.ShapeDtypeStruct((m, d), BF16),
                   jax.ShapeDtypeStruct((heads, m // HG_CHUNK, HG_DK, HG_DK), F32),
                   jax.ShapeDtypeStruct((heads, m // HG_CHUNK, HG_CHUNK, HG_CHUNK), BF16)],
        scratch_shapes=[pltpu.VMEM((hp, HG_DK, HG_DK), F32)],
        compiler_params=_params(("parallel", "arbitrary")),
    )(proj, proj, proj, proj, alb, gain)


def _hgrn2_bwd(proj, o_pre, states, scores, dog, alb, gain, *, rb):
    m, d4 = proj.shape
    d = d4 // 4
    heads = d // HG_DK
    rb = min(rb, m)
    cpb = rb // HG_CHUNK
    nrb = m // rb
    c, nsub = HG_CHUNK, HG_CHUNK // HG_SUB

    def body(q_ref, f_ref, v_ref, g_ref, o_ref, st_ref, a_ref, dog_ref, alb_ref, gain_ref,
             dq_ref, df_ref, dv_ref, dg_ref, dalb_ref, dgain_ref, dstate, carry_ref):
        first = (pl.program_id(0) == 0) & (pl.program_id(1) == 0)

        @pl.when(first)
        def _():
            dgain_ref[...] = jnp.zeros(dgain_ref.shape, F32)

        @pl.when(pl.program_id(1) == 0)
        def _():
            dstate[...] = jnp.zeros(dstate.shape, F32)
            carry_ref[...] = jnp.zeros(carry_ref.shape, F32)
            dalb_ref[...] = jnp.zeros(dalb_ref.shape, F32)

        row, col, base, causal, below = _hg_masks()
        sub_iota = lax.broadcasted_iota(jnp.int32, (nsub, HG_SUB, HG_DK), 1)
        row_k = lax.broadcasted_iota(jnp.int32, (c, HG_DK), 0)
        upper = col >= row

        def chunk(step, carry):
            ci = cpb - 1 - step
            sl = pl.ds(pl.multiple_of(ci * HG_CHUNK, HG_CHUNK), HG_CHUNK)
            qr, fr, v, gr = q_ref[sl, :], f_ref[sl, :], v_ref[sl, :], g_ref[sl, :]
            qt, kk, g, lbound, sig, forget = _hg_gates(qr, fr, alb_ref[...])
            o = o_ref[sl, :]
            dogv = dog_ref[sl, :]
            gain_v = gain_ref[...]
            n, r, sg = _hg_norm(o, gr, gain_v)
            dgr = dogv * n * sg * (1.0 + gr * (1.0 - sg))
            dn = dogv * gr * sg
            dgain_ref[...] += _col_sum(dn * o * r)
            u = dn * gain_v
            d_o = r * u - o * (r * r * r) * jnp.mean(u * o, axis=-1, keepdims=True)
            st0 = st_ref[ci]
            dst = dstate[...]
            _, b, bq, qh, edecs, (b3, q3, k3) = _hg_scores(qt, kk, g, scores=False)
            a = a_ref[ci]
            eb = jnp.exp(b)
            b_last = b[c - 1:c, :]
            kdl_dec = jnp.exp(b_last - b)
            kdl = kk * kdl_dec
            d_a = jnp.where(causal, _dot_nt(d_o, v), 0.0)
            d_at = _dot_nt(v, d_o)
            dv = _dot_tn(a, d_o) + _dot_nt(kdl, dst)
            dq = eb * _hdot(d_o, st0)
            dk = _hdot(v, dst) * kdl_dec
            d_a_below = jnp.where(below, d_a, 0.0)
            dq_parts = [jnp.zeros((HG_SUB, HG_DK), F32)]
            for i in range(1, nsub):
                lo, hi = i * HG_SUB, (i + 1) * HG_SUB
                dq_parts.append(_hdot(d_a_below[lo:hi, :], kk * edecs[i]))
                gi = _hdot(d_at[:, lo:hi], qh[lo:hi, :])
                dk = dk + jnp.where(row_k < lo, edecs[i] * gi, 0.0)
            dq = dq + jnp.concatenate(dq_parts, axis=0) * jnp.exp(bq)
            dq3 = jnp.zeros((nsub, HG_SUB, HG_DK), F32)
            dk3 = jnp.zeros((nsub, HG_SUB, HG_DK), F32)
            for j in range(HG_SUB):
                e = jnp.exp(jnp.minimum(b3 - b3[:, j:j + 1, :], 0.0))
                dcol = jnp.sum(jnp.where(col == base + j, d_a, 0.0), axis=-1, keepdims=True)
                t1 = dcol.reshape(nsub, HG_SUB, 1) * e
                dq3 = dq3 + t1 * k3[:, j:j + 1, :]
                dk3 = jnp.where(sub_iota == j, jnp.sum(t1 * q3, axis=1, keepdims=True), dk3)
            dq = dq + dq3.reshape(c, HG_DK)
            dk = dk + dk3.reshape(c, HG_DK)
            dstate[...] = dst * jnp.exp(b_last) + _hdot_tn(d_o, qt * eb)
            dglog = _tdot(upper, qt * dq - kk * dk) + carry_ref[...]
            carry_ref[...] = dglog[0:1, :]
            dforget = dglog / forget
            one_m_lb = 1.0 - lbound
            dsig = (dforget - dk) * one_m_lb
            sneg = _sigmoid(-fr)
            dlb = _col_sum(dforget * (1.0 - sig) - dk * sneg)
            dalb0 = dlb * lbound * one_m_lb
            dalb_ref[...] += jnp.concatenate([dalb0, -dalb0], axis=0)
            sq = _sigmoid(qr)
            dq_ref[sl, :] = (dq * (HG_DK ** -0.5) * sq * (1.0 + qr * (1.0 - sq))).astype(dq_ref.dtype)
            df_ref[sl, :] = (dsig * sig * (1.0 - sig)).astype(df_ref.dtype)
            dv_ref[sl, :] = dv.astype(dv_ref.dtype)
            dg_ref[sl, :] = dgr.astype(dg_ref.dtype)
            return carry

        lax.fori_loop(0, cpb, chunk, 0, unroll=2)

    def rev(r):
        return nrb - 1 - r

    def col(cidx):
        return pl.BlockSpec((rb, HG_DK), lambda h, r: (rev(r), cidx * heads + h))

    def head_rows():
        return pl.BlockSpec((rb, HG_DK), lambda h, r: (rev(r), h))

    return pl.pallas_call(
        body, name="hgrn2_bwd", grid=(heads, nrb),
        in_specs=[col(0), col(1), col(2), col(3), head_rows(),
                  pl.BlockSpec((None, cpb, HG_DK, HG_DK), lambda h, r: (h, rev(r), 0, 0)),
                  pl.BlockSpec((None, cpb, HG_CHUNK, HG_CHUNK), lambda h, r: (h, rev(r), 0, 0)),
                  head_rows(),
                  pl.BlockSpec((2, HG_DK), lambda h, r: (0, h)),
                  pl.BlockSpec((1, HG_DK), lambda h, r: (0, 0))],
        out_specs=[head_rows(), head_rows(), head_rows(), head_rows(),
                   pl.BlockSpec((2, HG_DK), lambda h, r: (0, h)),
                   pl.BlockSpec((1, HG_DK), lambda h, r: (0, 0))],
        out_shape=[jax.ShapeDtypeStruct((m, d), BF16)] * 4
                  + [jax.ShapeDtypeStruct((2, d), F32), jax.ShapeDtypeStruct((1, HG_DK), F32)],
        scratch_shapes=[pltpu.VMEM((HG_DK, HG_DK), F32), pltpu.VMEM((1, HG_DK), F32)],
        compiler_params=_params(("arbitrary", "arbitrary")),
    )(proj, proj, proj, proj, o_pre, states, scores, dog, alb, gain)


def _swa_probs(qh, kp, kc, sink, slope, has_prev):
    rows = qh.shape[0]
    qi = lax.broadcasted_iota(jnp.int32, (rows, WINDOW), 0) & (WINDOW - 1)
    si = lax.broadcasted_iota(jnp.int32, (rows, WINDOW), 1)
    scale = ATT_HD ** -0.5
    dist_c = (qi - si).astype(F32)
    s_p = _dot_nt(qh, kp) * scale - slope * (dist_c + float(WINDOW))
    s_c = _dot_nt(qh, kc) * scale - slope * dist_c
    s_p = jnp.where((si > qi) & has_prev, s_p, NEG)
    s_c = jnp.where(si <= qi, s_c, NEG)
    mx = jnp.maximum(jnp.maximum(jnp.max(s_p, axis=-1, keepdims=True), jnp.max(s_c, axis=-1, keepdims=True)), sink)
    e_p, e_c, e_s = jnp.exp(s_p - mx), jnp.exp(s_c - mx), jnp.exp(sink - mx)
    inv = 1.0 / (jnp.sum(e_p, axis=-1, keepdims=True) + jnp.sum(e_c, axis=-1, keepdims=True) + e_s)
    return e_p * inv, e_c * inv, e_s * inv


def _slope(h, n_heads):
    return float(2.0 ** (-8.0 * (h + 1) / n_heads))


def _swa_group(ref_vals, sink_ref, kh, n_heads):
    heads = [kh * ATT_G + g for g in range(ATT_G)]
    stacked = [jnp.concatenate([v[:, h * ATT_HD:(h + 1) * ATT_HD] for h in heads], axis=0) for v in ref_vals]
    grp = lax.shift_right_logical(lax.broadcasted_iota(jnp.int32, (ATT_G * WINDOW, 1), 0), WINDOW.bit_length() - 1)
    slope = jnp.zeros((ATT_G * WINDOW, 1), F32)
    sink = jnp.zeros((ATT_G * WINDOW, 1), F32)
    for g, h in enumerate(heads):
        slope = jnp.where(grp == g, _slope(h, n_heads), slope)
        sink = jnp.where(grp == g, sink_ref[:, h:h + 1], sink)
    return stacked, slope, sink


def _swa_fwd(q, kv, sinks):
    m, d = q.shape
    n_heads = d // ATT_HD
    kvh = n_heads // ATT_G
    kd = kvh * ATT_HD
    nb = m // WINDOW

    def body(q_ref, kvp_ref, kvc_ref, sink_ref, o_ref):
        has_prev = pl.program_id(0) > 0
        qv, kvp, kvc = q_ref[...], kvp_ref[...], kvc_ref[...]
        outs = []
        for kh in range(kvh):
            ks = slice(kh * ATT_HD, (kh + 1) * ATT_HD)
            vs = slice(kd + kh * ATT_HD, kd + (kh + 1) * ATT_HD)
            (q4,), slope, sink = _swa_group([qv], sink_ref, kh, n_heads)
            p_p, p_c, _ = _swa_probs(q4, kvp[:, ks], kvc[:, ks], sink, slope, has_prev)
            o4 = _dot(p_p, kvp[:, vs]) + _dot(p_c, kvc[:, vs])
            outs += [o4[g * WINDOW:(g + 1) * WINDOW, :] for g in range(ATT_G)]
        o_ref[...] = jnp.concatenate(outs, axis=-1).astype(o_ref.dtype)

    return pl.pallas_call(
        body, name="swa_fwd", grid=(nb,),
        in_specs=[pl.BlockSpec((WINDOW, d), lambda n: (n, 0)),
                  pl.BlockSpec((WINDOW, 2 * kd), lambda n: (jnp.maximum(n - 1, 0), 0)),
                  pl.BlockSpec((WINDOW, 2 * kd), lambda n: (n, 0)),
                  pl.BlockSpec((1, n_heads), lambda n: (0, 0))],
        out_specs=pl.BlockSpec((WINDOW, d), lambda n: (n, 0)),
        out_shape=jax.ShapeDtypeStruct((m, d), BF16),
        compiler_params=_params(("arbitrary",)),
    )(q, kv, kv, sinks)


def _swa_bwd(q, kv, sinks, dao):
    m, d = q.shape
    n_heads = d // ATT_HD
    kvh = n_heads // ATT_G
    kd = kvh * ATT_HD
    nb = m // WINDOW
    scale = ATT_HD ** -0.5

    def body(q_ref, kvp_ref, kvc_ref, sink_ref, do_ref, dq_ref, dkvc_ref, dkvp_ref, dqsum_ref, dsink_ref):
        @pl.when(pl.program_id(0) == 0)
        def _():
            dqsum_ref[...] = jnp.zeros(dqsum_ref.shape, F32)
            dsink_ref[...] = jnp.zeros(dsink_ref.shape, F32)

        has_prev = pl.program_id(0) > 0
        qv, kvp, kvc, dov = q_ref[...], kvp_ref[...], kvc_ref[...], do_ref[...]
        lane_h = lax.broadcasted_iota(jnp.int32, (1, n_heads), 1)
        dsink = jnp.zeros((1, n_heads), F32)
        dq_parts, dk_p, dk_c, dv_p, dv_c = [], [], [], [], []
        for kh in range(kvh):
            ks = slice(kh * ATT_HD, (kh + 1) * ATT_HD)
            vs = slice(kd + kh * ATT_HD, kd + (kh + 1) * ATT_HD)
            kp, kc, vp, vc = kvp[:, ks], kvc[:, ks], kvp[:, vs], kvc[:, vs]
            (q4, do4), slope, sink = _swa_group([qv, dov], sink_ref, kh, n_heads)
            p_p, p_c, p_s = _swa_probs(q4, kp, kc, sink, slope, has_prev)
            dp_p, dp_c = _dot_nt(do4, vp), _dot_nt(do4, vc)
            delta = jnp.sum(p_p * dp_p, axis=-1, keepdims=True) + jnp.sum(p_c * dp_c, axis=-1, keepdims=True)
            ds_p, ds_c = p_p * (dp_p - delta), p_c * (dp_c - delta)
            sink_term = p_s * delta
            dq4 = (_dot(ds_p, kp) + _dot(ds_c, kc)) * scale
            for g in range(ATT_G):
                rows = slice(g * WINDOW, (g + 1) * WINDOW)
                dsink = dsink + jnp.where(lane_h == kh * ATT_G + g, -_col_sum(sink_term[rows, :]), 0.0)
                dq_parts.append(dq4[rows, :])
            dk_p.append(_dot_tn(ds_p, q4) * scale)
            dk_c.append(_dot_tn(ds_c, q4) * scale)
            dv_p.append(_dot_tn(p_p, do4))
            dv_c.append(_dot_tn(p_c, do4))
        dq = jnp.concatenate(dq_parts, axis=-1)
        dq_ref[...] = dq.astype(dq_ref.dtype)
        dqsum_ref[...] += _col_sum(dq)
        dsink_ref[...] += dsink
        dkvc_ref[...] = jnp.concatenate(dk_c + dv_c, axis=-1)
        dkvp_ref[...] = jnp.concatenate(dk_p + dv_p, axis=-1)

    return pl.pallas_call(
        body, name="swa_bwd", grid=(nb,),
        in_specs=[pl.BlockSpec((WINDOW, d), lambda n: (n, 0)),
                  pl.BlockSpec((WINDOW, 2 * kd), lambda n: (jnp.maximum(n - 1, 0), 0)),
                  pl.BlockSpec((WINDOW, 2 * kd), lambda n: (n, 0)),
                  pl.BlockSpec((1, n_heads), lambda n: (0, 0)),
                  pl.BlockSpec((WINDOW, d), lambda n: (n, 0))],
        out_specs=[pl.BlockSpec((WINDOW, d), lambda n: (n, 0)),
                   pl.BlockSpec((WINDOW, 2 * kd), lambda n: (n, 0)),
                   pl.BlockSpec((WINDOW, 2 * kd), lambda n: (n, 0)),
                   pl.BlockSpec((1, d), lambda n: (0, 0)),
                   pl.BlockSpec((1, n_heads), lambda n: (0, 0))],
        out_shape=[jax.ShapeDtypeStruct((m, d), BF16), jax.ShapeDtypeStruct((m, 2 * kd), F32),
                   jax.ShapeDtypeStruct((m, 2 * kd), F32), jax.ShapeDtypeStruct((1, d), F32),
                   jax.ShapeDtypeStruct((1, n_heads), F32)],
        compiler_params=_params(("arbitrary",)),
    )(q, kv, kv, sinks, dao)


def _kv_grad_combine(dkv_cur, dkv_prev):
    m, w = dkv_cur.shape
    nb = m // WINDOW

    def body(cur_ref, nxt_ref, o_ref, sum_ref):
        @pl.when(pl.program_id(0) == 0)
        def _():
            sum_ref[...] = jnp.zeros(sum_ref.shape, F32)

        total = cur_ref[...] + jnp.where(pl.program_id(0) < nb - 1, nxt_ref[...], 0.0)
        o_ref[...] = total.astype(o_ref.dtype)
        sum_ref[...] += _col_sum(total)

    return pl.pallas_call(
        body, name="kv_grad_combine", grid=(nb,),
        in_specs=[pl.BlockSpec((WINDOW, w), lambda n: (n, 0)),
                  pl.BlockSpec((WINDOW, w), lambda n: (jnp.minimum(n + 1, nb - 1), 0))],
        out_specs=[pl.BlockSpec((WINDOW, w), lambda n: (n, 0)), pl.BlockSpec((1, w), lambda n: (0, 0))],
        out_shape=[jax.ShapeDtypeStruct((m, w), BF16), jax.ShapeDtypeStruct((1, w), F32)],
        compiler_params=_params(("arbitrary",)),
    )(dkv_cur, dkv_prev)


def _row(v):
    return v.reshape(1, -1)


def _local_step(x, p, target, wget, grad_sink, ln_gain, ln_bias, alb, norm_gain, kv_b, b_q, sinks, b_out, ple_b):
    gs = {}
    gains = [[_row(ln_gain[i, j]) for j in range(3)] for i in range(DEPTH)]
    biases = [[_row(ln_bias[i, j]) for j in range(3)] for i in range(DEPTH)]
    sd = x.shape
    pending = [None]

    def mm(a, b, lb=0, **kw):
        after, pending[0] = pending[0], None
        return _mm(a, b, lb=lb, after=after, **kw)

    def mm_ln(a, wt, xin, i, j, nm, bias=None, pu=None):
        if pu is None:
            fn, rows = (lambda h, xv, g, bv: (h,) + _ln_fwd_fn(xv, h, g, bv)), [xin]
        else:
            fn, rows = (lambda h, xv, puv, g, bv: (h,) + _ple_ln_fwd_fn(xv, h, puv, g, bv)), [xin, pu]
        h, y, yb = _mm(a, wt, lb=0, bias=bias, name=nm,
                       post=(fn, rows, [gains[i][j], biases[i][j]], [(sd, F32), (sd, F32), (sd, BF16)], []))
        return h, (y, yb)

    def mm_ln_bwd(a, wt, add, xin, h, i, j, nm):
        dx_part, dh, dg, db, dhsum = mm(a, wt, tb=True, add=add, name=nm,
                                        post=(_ln_bwd_fn, [xin, h], [gains[i][j]], [(sd, F32), (sd, BF16)],
                                              [((1, sd[1]), F32)] * 3))
        gs[f"ln_gain_{i}_{j}"], gs[f"ln_bias_{i}_{j}"] = dg, db
        return dx_part, dh, dhsum

    def tail_fwd(xa, i):
        wgu = wget("ffn_w_gate_up", i, xa[1])
        hid2 = wgu.shape[-1]
        gu, act = _mm(xa[1], wgu, lb=0, name=f"ffn_up_swiglu{i}", tile_cols=hid2 // 2,
                      post=(_swiglu_fwd_fn, [], [], [((sd[0], hid2), BF16), ((sd[0], hid2 // 2), BF16)], []))
        f, xb = mm_ln(act, wget("ffn_w_down", i, act), xa[0], i, 1, f"ffn_down_ln{i}")
        pu = _mm(p[i], wget("ple_w_up", i, act), lb=0, name=f"ple_up{i}")
        pg, xc = mm_ln(xb[1], wget("ple_w_gate", i, act), xb[0], i, 2, f"ple_gate_ln{i}", bias=_row(ple_b[i]), pu=pu)
        return dict(xa=xa, gu=gu, act=act, f=f, xb=xb, pg=pg, pu=pu), xc

    def tail_bwd(head, sv, i, mix_in, mix_h):
        xa, xb = sv["xa"], sv["xb"]
        dxb_part, dpg, dpu, dg2, db2, dbg = head(
            _ple_ln_bwd_fn, [xb[0], sv["pg"], sv["pu"]], [gains[i][2]],
            [(sd, F32), (sd, BF16), (sd, BF16)], [((1, sd[1]), F32)] * 3)[:6]
        gs[f"ple_b_{i}"] = dbg
        gs[f"ln_gain_{i}_2"], gs[f"ln_bias_{i}_2"] = dg2, db2
        grad_of("ple_w_gate", i, xb[1], dpg)
        grad_of("ple_w_up", i, p[i], dpu)
        dxa_part, df, _ = mm_ln_bwd(dpg, wget("ple_w_gate", i, None), dxb_part, xa[0], sv["f"], i, 1,
                                    f"ple_gate_dx_ln{i}")
        grad_of("ffn_w_down", i, sv["act"], df)
        gu = sv["gu"]
        dgu, = mm(df, wget("ffn_w_down", i, None), tb=True, name=f"ffn_down_dx_swiglu{i}", tile_cols=gu.shape[1] // 4,
                  post=(_swiglu_bwd_fn, [gu], [], [(gu.shape, BF16)], []))
        grad_of("ffn_w_gate_up", i, xa[1], dgu)
        return mm_ln_bwd(dgu, wget("ffn_w_gate_up", i, None), dxa_part, mix_in, mix_h, i, 0, f"ffn_up_dx_ln{i}")

    def grad_of(nm, i, act, dout):
        grad = mm(act, dout, lb=None, ta=True, out_dtype=BF16, out_layers=1, out_layer=0, name=f"grad_{nm}{i}")
        token = grad_sink(nm, i, grad)
        if token is not None:
            pending[0] = token

    proj = _mm(x, wget("a_w_in", 0, None), lb=0, name="hg_proj")
    o_pre, og, states, scores = _hgrn2_fwd(proj, alb, norm_gain, rb=HG_ROWS)
    h0, x1 = mm_ln(og, wget("a_w_out", 0, og), x, 0, 0, "hg_out_ln")
    sv0, x3 = tail_fwd(x1, 0)
    kv = _mm(x3[1], wget("kv_w", 0, x3[1]), lb=0, bias=_row(kv_b), name="kv_proj")
    q = _mm(x3[1], wget("b_w_q", 0, x3[1]), lb=0, bias=b_q, name="q_proj")
    ao = _swa_fwd(q, kv, sinks)
    h1, x4 = mm_ln(ao, wget("b_w_out", 0, x3[1]), x3[0], 1, 0, "att_out_ln", bias=b_out)
    sv1, y = tail_fwd(x4, 1)

    loss_box = []

    def loss_head(fn, rows, whole, outs, sums):
        def with_loss(yv, tv, *rest):
            dy, part = _loss_fn(yv, tv)
            return fn(dy, *rest) + (part,)

        res = _rowwise(with_loss, [y[0], target] + rows, whole, outs, list(sums) + [((1, LANES), F32)],
                       name="loss_ln_ple_bwd1")
        loss_box.append(res[-1])
        return res

    dx3_part, dh1, dh1sum = tail_bwd(loss_head, sv1, 1, x3[0], h1)
    loss = loss_box[0]
    gs["b_out"] = dh1sum
    grad_of("b_w_out", 0, ao, dh1)
    dao = mm(dh1, wget("b_w_out", 0, None), tb=True, name="att_out_dx")
    dq, dkv_cur, dkv_prev, dqsum, dsinks = _swa_bwd(q, kv, sinks, dao)
    gs["b_q"], gs["sinks"] = dqsum, dsinks
    dkv, dkvsum = _kv_grad_combine(dkv_cur, dkv_prev)
    gs["kv_b"] = dkvsum
    grad_of("b_w_q", 0, x3[1], dq)
    grad_of("kv_w", 0, x3[1], dkv)
    dx3 = mm(dq, wget("b_w_q", 0, None), tb=True, add=dx3_part, name="q_proj_dx")

    def kv_head(*post):
        return mm(dkv, wget("kv_w", 0, None), tb=True, add=dx3, name="kv_proj_dx_ln_ple_bwd0", post=post)

    dx_part, dh0, _ = tail_bwd(kv_head, sv0, 0, x, h0)
    grad_of("a_w_out", 0, og, dh0)
    dog = mm(dh0, wget("a_w_out", 0, None), tb=True, name="hg_out_dx")
    dqr, dfr, dvr, dgr, dalb, dgain = _hgrn2_bwd(proj, o_pre, states, scores, dog, alb, norm_gain, rb=HG_ROWS)
    gs["alb"], gs["norm_gain"] = dalb, dgain
    dproj = jnp.concatenate([dqr, dfr, dvr, dgr], axis=1)
    grad_of("a_w_in", 0, x, dproj)
    grad_x = mm(dproj, wget("a_w_in", 0, None), tb=True, add=dx_part, name="hg_proj_dx")
    return loss, grad_x, gs


HBM_SPEC = pl.BlockSpec(memory_space=pl.ANY)
HBM_ONLY = pl.BlockSpec(memory_space=pltpu.HBM)
SEM_SPEC = pl.BlockSpec(memory_space=pltpu.SEMAPHORE)
SIDE_EFFECT = pltpu.SideEffectType.DATAFLOW_SIDE_EFFECTING


def _slot(kind, j):
    return (j % 2) * 2 + j // 2 if kind == "colp" else j


def _piece(ref, kind, j):
    _, r, c = ref.shape
    if kind == "row":
        return ref.at[:, pl.ds(j * (r // N_CHIPS), r // N_CHIPS), :]
    return ref.at[:, :, pl.ds(_slot(kind, j) * (c // N_CHIPS), c // N_CHIPS)]


def _piece_dyn(ref, kind, j):
    _, r, c = ref.shape
    if kind == "row":
        return ref.at[:, pl.ds(pl.multiple_of(j * (r // N_CHIPS), 16), r // N_CHIPS), :]
    return ref.at[:, :, pl.ds(pl.multiple_of(_slot(kind, j) * (c // N_CHIPS), LANES), c // N_CHIPS)]


def _chip_of(j, c):
    return (j // 2, j % 2, c)


def _in_hbm(a):
    return pltpu.with_memory_space_constraint(a, pltpu.HBM)


def _place(src, kind, chip, *, mode, name, out_dtype, zone=None, zone_shape=None, layer=0, after=None):
    if mode == "gather":
        _, r, c = src.shape
        out_shape = (1, r * N_CHIPS, c) if kind == "row" else (1, r, c * N_CHIPS)
    else:
        out_shape = tuple(zone.shape) if zone is not None else tuple(zone_shape)
        r, c = out_shape[-2:]
    tm = _pick_rows(r, 512)
    nb = r // tm

    def full_idx(i, chip_ref):
        return (0, chip_ref[0] * nb + i, 0) if kind == "row" else (0, i, _slot(kind, chip_ref[0]))

    if mode == "gather":
        in_spec = pl.BlockSpec((None, tm, c), lambda i, chip_ref: (0, i, 0))
        out_spec = pl.BlockSpec((None, tm, c), full_idx)
    else:
        in_spec = pl.BlockSpec((None, tm, c), full_idx)
        out_spec = pl.BlockSpec((None, None, tm, c), lambda i, chip_ref: (chip_ref[0], layer, i, 0))
    in_specs, operands, aliases = [in_spec], [src], {}
    if zone is not None:
        in_specs.append(HBM_SPEC)
        operands.append(zone)
        aliases = {2: 0}
    if after is not None:
        in_specs.append(HBM_SPEC)
        operands.append(after)

    def body(chip_ref, src_ref, *rest):
        rest[-1][...] = src_ref[...].astype(rest[-1].dtype)

    return pl.pallas_call(
        body, name=name,
        grid_spec=pltpu.PrefetchScalarGridSpec(num_scalar_prefetch=1, grid=(nb,), in_specs=in_specs,
                                               out_specs=out_spec),
        out_shape=jax.ShapeDtypeStruct(out_shape, out_dtype),
        input_output_aliases=aliases,
        compiler_params=_params(("arbitrary",)),
    )(chip, *operands)


def _half(ref, c):
    h = ref.shape[1] // 2
    start = c * h if isinstance(c, int) else pl.multiple_of(c * h, 16)
    return ref.at[:, pl.ds(start, h), :]


class _SiblingFill:
    def __init__(self, lands, kinds, name):
        self.kinds, self.name, self.n = kinds, name, len(lands)
        n = self.n
        sem_shape = pltpu.SemaphoreType.DMA((n * N_CHIPS,))

        def body(*refs):
            land_refs, send_sems, recv_sems, token = refs[:n], refs[n], refs[n + 1], refs[-1]
            for cp in self._copies(land_refs, send_sems, recv_sems):
                cp.start()
            token[...] = jnp.zeros(token.shape, token.dtype)

        outs = pl.pallas_call(
            body, name=name + "_start",
            in_specs=[HBM_ONLY] * n,
            out_specs=[SEM_SPEC, SEM_SPEC] + [HBM_ONLY] * n + [pl.BlockSpec(memory_space=pltpu.VMEM)],
            out_shape=[sem_shape, sem_shape] + [pltpu.HBM(a.shape, a.dtype) for a in lands]
                      + [jax.ShapeDtypeStruct((8, LANES), F32)],
            input_output_aliases={i: i + 2 for i in range(n)},
            compiler_params=pltpu.CompilerParams(has_side_effects=SIDE_EFFECT),
        )(*[_in_hbm(a) for a in lands])
        self.send_sems, self.recv_sems, self.lands, self.token = outs[0], outs[1], list(outs[2:2 + n]), outs[-1]

    def _copies(self, land_refs, send_sems, recv_sems):
        x, y, c = lax.axis_index("x"), lax.axis_index("y"), lax.axis_index("c")
        me = 2 * x + y
        copies = []
        for a in range(self.n):
            for k in range(1, N_CHIPS):
                t = (me + k) % N_CHIPS
                slice_t = _piece_dyn(land_refs[a], self.kinds[a], t)
                got = _half(slice_t, c)
                copies.append(pltpu.make_async_remote_copy(
                    src_ref=got, dst_ref=got, send_sem=send_sems.at[a * N_CHIPS + k],
                    recv_sem=recv_sems.at[a * N_CHIPS + k], device_id=(x, y, 1 - c), device_id_type=MESH))
        return copies

    def wait(self, after):
        n = self.n

        def body(*refs):
            land_refs, send_sems, recv_sems = refs[:n], refs[n], refs[n + 1]
            for cp in self._copies(land_refs, send_sems, recv_sems):
                cp.wait_send()
                cp.wait_recv()

        operands = [_in_hbm(a) for a in self.lands] + [self.send_sems, self.recv_sems]
        in_specs = [HBM_ONLY] * n + [SEM_SPEC, SEM_SPEC]
        if after is not None:
            operands.append(after)
            in_specs.append(HBM_SPEC)
        outs = pl.pallas_call(
            body, name=self.name + "_wait",
            in_specs=in_specs, out_specs=[HBM_ONLY] * n,
            out_shape=[pltpu.HBM(a.shape, a.dtype) for a in self.lands],
            input_output_aliases={i: i for i in range(n)},
            compiler_params=pltpu.CompilerParams(has_side_effects=SIDE_EFFECT),
        )(*operands)
        return list(outs)


class _Exchange:
    def __init__(self, mode, srcs, lands, kinds, layers, name, after=None, halves=None):
        self.mode, self.kinds, self.layers, self.name, self.n = mode, kinds, layers, name, len(lands)
        self.halves = halves if halves is not None else [False] * len(lands)
        n, ns = self.n, len(srcs)
        n_in = ns + n + (after is not None)
        sem_shape = pltpu.SemaphoreType.DMA((n * N_CHIPS,))

        def body(*refs):
            src_refs, land_refs = refs[:ns], refs[ns:ns + n]
            send_sems, recv_sems = refs[n_in], refs[n_in + 1]
            token = refs[-1]
            c = lax.axis_index("c")
            me = 2 * lax.axis_index("x") + lax.axis_index("y")
            for j in range(N_CHIPS):
                @pl.when(me == j)
                def _():
                    for a in range(n):
                        for t in range(N_CHIPS):
                            if t != j:
                                src, dst = self._ends(src_refs, land_refs, a, j, t, c)
                                pltpu.make_async_remote_copy(
                                    src_ref=src, dst_ref=dst, send_sem=send_sems.at[a * N_CHIPS + t],
                                    recv_sem=recv_sems.at[a * N_CHIPS + j],
                                    device_id=_chip_of(t, c), device_id_type=MESH).start()
            token[...] = jnp.zeros(token.shape, token.dtype)

        arrays = list(srcs) + list(lands)
        operands = [_in_hbm(a) for a in arrays]
        in_specs = [HBM_ONLY] * (ns + n)
        if after is not None:
            operands.append(after)
            in_specs.append(HBM_SPEC)
        outs = pl.pallas_call(
            body, name=name + "_start",
            in_specs=in_specs,
            out_specs=[SEM_SPEC, SEM_SPEC] + [HBM_ONLY] * (ns + n) + [pl.BlockSpec(memory_space=pltpu.VMEM)],
            out_shape=[sem_shape, sem_shape] + [pltpu.HBM(a.shape, a.dtype) for a in arrays]
                      + [jax.ShapeDtypeStruct((8, LANES), F32)],
            input_output_aliases={i: i + 2 for i in range(ns + n)},
            compiler_params=pltpu.CompilerParams(has_side_effects=SIDE_EFFECT),
        )(*operands)
        self.send_sems, self.recv_sems = outs[0], outs[1]
        self.srcs, self.lands = list(outs[2:2 + ns]), list(outs[2 + ns:2 + ns + n])
        self.token = outs[-1]

    def _ends(self, src_refs, land_refs, a, me_j, peer, c):
        if self.mode == "gather":
            mine = _piece(land_refs[a], self.kinds[a], me_j)
            if self.halves[a]:
                mine = _half(mine, c)
            return mine, mine
        return _piece(src_refs[a], self.kinds[a], peer), land_refs[a].at[me_j, pl.ds(self.layers[a], 1)]

    def wait(self, after, lands=None):
        n, ns = self.n, len(self.srcs)
        lands = self.lands if lands is None else lands

        def body(*refs):
            src_refs, land_refs = refs[:ns], refs[ns:ns + n]
            send_sems, recv_sems = refs[ns + n], refs[ns + n + 1]
            c = lax.axis_index("c")
            me = 2 * lax.axis_index("x") + lax.axis_index("y")
            for j in range(N_CHIPS):
                @pl.when(me != j)
                def _():
                    for a in range(n):
                        sent, _ = self._ends(src_refs, land_refs, a, 0, j, c)
                        _, landed = self._ends(src_refs, land_refs, a, j, 0, c)
                        cp = pltpu.make_async_remote_copy(
                            src_ref=sent, dst_ref=landed, send_sem=send_sems.at[a * N_CHIPS + j],
                            recv_sem=recv_sems.at[a * N_CHIPS + j],
                            device_id=_chip_of(j, c), device_id_type=MESH)
                        cp.wait_send()
                        cp.wait_recv()

        arrays = self.srcs + list(lands)
        operands = [_in_hbm(a) for a in arrays] + [self.send_sems, self.recv_sems]
        in_specs = [HBM_ONLY] * (ns + n) + [SEM_SPEC, SEM_SPEC]
        if after is not None:
            operands.append(after)
            in_specs.append(HBM_SPEC)
        outs = pl.pallas_call(
            body, name=self.name + "_wait",
            in_specs=in_specs, out_specs=[HBM_ONLY] * (ns + n),
            out_shape=[pltpu.HBM(a.shape, a.dtype) for a in arrays],
            input_output_aliases={i: i for i in range(ns + n)},
            compiler_params=pltpu.CompilerParams(has_side_effects=SIDE_EFFECT),
        )(*operands)
        return list(outs[ns:])


def _sibling_swap(arrays, name):
    n = len(arrays)

    def body(*refs):
        ins, outs = refs[:n], refs[n:2 * n]
        send_sems, recv_sems = refs[2 * n:]
        sibling = (lax.axis_index("x"), lax.axis_index("y"), 1 - lax.axis_index("c"))
        copies = [pltpu.make_async_remote_copy(src_ref=ins[a], dst_ref=outs[a], send_sem=send_sems.at[a],
                                               recv_sem=recv_sems.at[a], device_id=sibling, device_id_type=MESH)
                  for a in range(n)]
        for cp in copies:
            cp.start()
        for cp in copies:
            cp.wait()

    return pl.pallas_call(
        body, name=name,
        in_specs=[HBM_SPEC] * n, out_specs=[HBM_SPEC] * n,
        out_shape=[jax.ShapeDtypeStruct(a.shape, a.dtype) for a in arrays],
        scratch_shapes=[pltpu.SemaphoreType.DMA((n,)), pltpu.SemaphoreType.DMA((n,))],
    )(*arrays)


def _gather_devices(vec):
    def body(in_ref, out_ref, send_sems, recv_sems, local_sem):
        x, y, c = lax.axis_index("x"), lax.axis_index("y"), lax.axis_index("c")
        me = 4 * x + 2 * y + c
        mine = pltpu.make_async_copy(in_ref, out_ref.at[me], local_sem)
        mine.start()
        copies = []
        for rel in range(1, N_DEV):
            peer = (x ^ (rel >> 2), y ^ ((rel >> 1) & 1), c ^ (rel & 1))
            copies.append(pltpu.make_async_remote_copy(
                src_ref=in_ref, dst_ref=out_ref.at[me], send_sem=send_sems.at[rel], recv_sem=recv_sems.at[rel],
                device_id=peer, device_id_type=MESH))
        for cp in copies:
            cp.start()
        for cp in copies:
            cp.wait()
        mine.wait()

    return pl.pallas_call(
        body, name="gather_small",
        in_specs=[HBM_SPEC], out_specs=HBM_SPEC,
        out_shape=jax.ShapeDtypeStruct((N_DEV,) + vec.shape, vec.dtype),
        scratch_shapes=[pltpu.SemaphoreType.DMA((N_DEV,)), pltpu.SemaphoreType.DMA((N_DEV,)),
                        pltpu.SemaphoreType.DMA],
    )(vec)


BIG = [("a_w_in", "col"), ("a_w_out", "row"), ("kv_w", "row"), ("b_w_q", "row"), ("b_w_out", "row"),
       ("ffn_w_gate_up", "colp"), ("ffn_w_down", "row"), ("ple_w_up", "col"), ("ple_w_gate", "row")]
GATHER_GROUPS = [[("a_w_in", 0), ("small", 0)], [("a_w_out", 0), ("ffn_w_gate_up", 0)],
                 [("ffn_w_down", 0), ("ple_w_gate", 0), ("ple_w_up", 0)], [("kv_w", 0), ("b_w_q", 0), ("b_w_out", 0)],
                 [("ffn_w_gate_up", 1)], [("ffn_w_down", 1), ("ple_w_gate", 1), ("ple_w_up", 1)]]
SCATTER_GROUPS = [[("ple_w_gate", 1), ("ple_w_up", 1), ("ffn_w_down", 1)], [("ffn_w_gate_up", 1)],
                  [("b_w_out", 0), ("b_w_q", 0), ("kv_w", 0)], [("ple_w_gate", 0), ("ple_w_up", 0), ("ffn_w_down", 0)],
                  [("ffn_w_gate_up", 0), ("a_w_out", 0)], [("a_w_in", 0)]]
SMALL_SHARDED = ["ln_gain", "ln_bias", "a_lower_bound"]
SMALL_REPLICATED = ["a_norm_gain", "kv_b", "b_b_q", "b_sinks", "b_b_out", "ple_b_gate"]
WEIGHT_ORDER = ["a_w_in", "a_lower_bound", "a_norm_gain", "a_w_out", "kv_w", "kv_b", "b_w_q", "b_b_q", "b_sinks",
                "b_w_out", "b_b_out", "ffn_w_gate_up", "ffn_w_down", "ple_w_up", "ple_w_gate", "ple_b_gate",
                "ln_gain", "ln_bias"]


def _as3(a):
    return a.reshape((-1,) + a.shape[-2:]) if a.ndim >= 3 else a.reshape((1,) + a.shape)


def _pad_lanes(v):
    n = v.shape[-1]
    return jnp.pad(v, ((0, 0), (0, (-n) % LANES)))


def _adam_small_fn(w, mom, vel, g):
    return _adam_fn(w, mom, vel, g, jnp.zeros_like(g))[1:]


def _sum_rows_fn(slots):
    acc = slots[0]
    for s in range(1, slots.shape[0]):
        acc = acc + slots[s]
    return (acc,)


def kernel(x, p, a_w_in, a_lower_bound, a_norm_gain, a_w_out, kv_w, kv_b, b_w_q, b_b_q, b_sinks, b_w_out, b_b_out, ffn_w_gate_up, ffn_w_down, ple_w_up, ple_w_gate, ple_b_gate, ln_gain, ln_bias, loss_target, m_a_w_in, m_a_lower_bound, m_a_norm_gain, m_a_w_out, m_kv_w, m_kv_b, m_b_w_q, m_b_b_q, m_b_sinks, m_b_w_out, m_b_b_out, m_ffn_w_gate_up, m_ffn_w_down, m_ple_w_up, m_ple_w_gate, m_ple_b_gate, m_ln_gain, m_ln_bias, v_a_w_in, v_a_lower_bound, v_a_norm_gain, v_a_w_out, v_kv_w, v_kv_b, v_b_w_q, v_b_b_q, v_b_sinks, v_b_w_out, v_b_b_out, v_ffn_w_gate_up, v_ffn_w_down, v_ple_w_up, v_ple_w_gate, v_ple_b_gate, v_ln_gain, v_ln_bias):
    args = dict(locals())
    wts = {n: args[n] for n in WEIGHT_ORDER}
    mom = {n: args["m_" + n] for n in WEIGHT_ORDER}
    vel = {n: args["v_" + n] for n in WEIGHT_ORDER}
    chip = 2 * lax.axis_index("x") + lax.axis_index("y")
    d = x.shape[-1]
    dq = d // N_CHIPS

    kind_of = dict(BIG)
    kind_of["small"] = "col"
    chip_arr = chip.reshape(1).astype(jnp.int32)
    small_pack = jnp.concatenate([wts[n].reshape(-1, dq) for n in SMALL_SHARDED], axis=0)[None]

    def place(key, after):
        n, layer = key
        if n == "small":
            return _place(small_pack, "col", chip_arr, mode="gather", name="place_small", out_dtype=F32, after=after)
        return _place(_as3(wts[n])[layer:layer + 1], kind_of[n], chip_arr, mode="gather",
                      name=f"place_{n}{layer}", out_dtype=BF16, after=after)

    gathers, where = [], {}
    for gi, group in enumerate(GATHER_GROUPS):
        prev = gathers[-1].token if gathers else None
        gathers.append(_Exchange("gather", [], [place(k, prev) for k in group], [kind_of[k[0]] for k in group],
                                 [0] * len(group), f"gather{gi}", after=prev,
                                 halves=[k[0] != "small" for k in group]))
        for k in group:
            where[k] = gi
    all_started = gathers[-1].token
    ready = {}

    fills = {}

    def pass_on(gi, after):
        if gi not in fills:
            group = GATHER_GROUPS[gi]
            outs = gathers[gi].wait(after)
            split = [i for i, k in enumerate(group) if k[0] != "small"]
            fills[gi] = (outs, split, _SiblingFill([outs[i] for i in split], [kind_of[group[i][0]] for i in split],
                                                   f"fill{gi}"))

    def wget(name, layer, after):
        key = (name, layer)
        if key not in ready:
            gi = where[key]
            after = all_started if gi == 0 else after
            pass_on(gi, after)
            if 1 <= gi < len(GATHER_GROUPS) - 1:
                pass_on(gi + 1, after)
            outs, split, fill = fills[gi]
            for i, arr in zip(split, fill.wait(after)):
                outs[i] = arr
            for k, arr in zip(GATHER_GROUPS[gi], outs):
                ready[k] = arr
        return ready[key]

    small_full = wget("small", 0, None)[0]
    ln_gain_f = small_full[0:6].reshape(DEPTH, 3, d)
    ln_bias_f = small_full[6:12].reshape(DEPTH, 3, d)
    alb_f = small_full[12:14]

    group_of = {k: gi for gi, group in enumerate(SCATTER_GROUPS) for k in group}
    grads_done, zones, scatters = {}, {}, []

    def grad_sink(name, layer, grad):
        grads_done[(name, layer)] = grad
        zones[name] = _place(grad, kind_of[name], chip_arr, mode="scatter", name=f"place_grad_{name}{layer}",
                             out_dtype=BF16, zone=zones.get(name), zone_shape=(N_CHIPS,) + _as3(wts[name]).shape,
                             layer=layer)
        gi = group_of[(name, layer)]
        group = SCATTER_GROUPS[gi]
        if not all(k in grads_done for k in group):
            return None
        ex = _Exchange("scatter", [grads_done[k] for k in group], [zones[k[0]] for k in group],
                       [kind_of[k[0]] for k in group], [k[1] for k in group], f"scatter{gi}")
        for k, zone in zip(group, ex.lands):
            zones[k[0]] = zone
        scatters.append((ex, group))
        return ex.token

    loss, grad_x, gs = _local_step(
        x[0], p[:, 0], loss_target[0], wget, grad_sink, ln_gain_f, ln_bias_f, alb_f, a_norm_gain, kv_b, b_b_q,
        b_sinks, b_b_out, ple_b_gate)

    res = {}

    def arrive(batch, after):
        for ex, group in batch:
            outs = ex.wait(after, lands=[zones[k[0]] for k in group])
            for k, zone in zip(group, outs):
                zones[k[0]] = zone

    def update(names, tag):
        partial = []
        for n in names:
            s2 = zones[n].reshape(N_CHIPS, -1, zones[n].shape[-1])
            partial.append(_rowwise(_sum_slots_fn, [s2], [], [(s2.shape[1:], BF16)], name=f"sum_{n}")[0])
        sibling = _sibling_swap(partial, tag)
        for n, own, sib in zip(names, partial, sibling):
            shp = wts[n].shape
            flat = lambda a: a.reshape(-1, shp[-1])
            out = _rowwise(_adam_fn, [flat(wts[n]), flat(mom[n]), flat(vel[n]), own, sib], [],
                           [(own.shape, F32)] * 4, name=f"adam_{n}")
            res[n] = [o.reshape(shp) for o in out]
        return res[names[-1]][1]

    last_names = [k[0] for k in SCATTER_GROUPS[-1]]
    arrive(scatters[:-1], grad_x)
    updated = update([n for n, _ in BIG if n not in last_names], "sibling_swap")
    arrive(scatters[-1:], updated)
    update(last_names, "sibling_swap_last")

    ln_g = jnp.concatenate([gs[f"ln_gain_{i}_{j}"] for i in range(DEPTH) for j in range(3)], axis=0)
    ln_b = jnp.concatenate([gs[f"ln_bias_{i}_{j}"] for i in range(DEPTH) for j in range(3)], axis=0)
    ple_bg = jnp.concatenate([gs[f"ple_b_{i}"] for i in range(DEPTH)], axis=0)
    small_list = [ln_g.reshape(1, -1), ln_b.reshape(1, -1), gs["alb"].reshape(1, -1), gs["norm_gain"],
                  gs["kv_b"], gs["b_q"], _pad_lanes(gs["sinks"]), gs["b_out"], ple_bg.reshape(1, -1), loss]
    small_vec = jnp.concatenate(small_list, axis=1)
    everyone = _gather_devices(small_vec)
    total, = _rowwise(_sum_rows_fn, [everyone], [], [(small_vec.shape, F32)], name="sum_small")
    offs, pos = [], 0
    for v in small_list:
        offs.append((pos, v.shape[1]))
        pos += v.shape[1]

    def seg(k):
        return total[0, offs[k][0]:offs[k][0] + offs[k][1]]

    def my_cols(full, rows):
        return lax.dynamic_slice_in_dim(full.reshape(rows, N_CHIPS, dq), chip, 1, axis=1).reshape(rows, dq)

    n_sink = b_sinks.shape[-1]
    small_grads = {
        "ln_gain": my_cols(seg(0), 6).reshape(ln_gain.shape), "ln_bias": my_cols(seg(1), 6).reshape(ln_bias.shape),
        "a_lower_bound": my_cols(seg(2), 2), "a_norm_gain": seg(3).reshape(a_norm_gain.shape),
        "kv_b": seg(4).reshape(kv_b.shape), "b_b_q": seg(5).reshape(b_b_q.shape),
        "b_sinks": seg(6)[:n_sink].reshape(b_sinks.shape), "b_b_out": seg(7).reshape(b_b_out.shape),
        "ple_b_gate": seg(8).reshape(ple_b_gate.shape)}
    names = SMALL_SHARDED + SMALL_REPLICATED
    pack = lambda dct: _pad_lanes(jnp.concatenate([dct[n].reshape(1, -1) for n in names], axis=1))
    g_pack = pack(small_grads)
    upd = _rowwise(_adam_small_fn, [pack(wts), pack(mom), pack(vel), g_pack], [], [(g_pack.shape, F32)] * 3,
                   name="adam_small")
    pos = 0
    for n in names:
        size = wts[n].size
        res[n] = [small_grads[n]] + [u[0, pos:pos + size].reshape(wts[n].shape) for u in upd]
        pos += size

    outs = [seg(9)[0], grad_x[None]]
    for k in range(4):
        outs += [res[n][k] for n in WEIGHT_ORDER]
    return tuple(outs)
```

```python
import functools

import jax
import jax.numpy as jnp
from jax import lax
from jax.experimental import pallas as pl
from jax.experimental.pallas import tpu as pltpu

F32 = jnp.float32
BF16 = jnp.bfloat16
MESH = pl.DeviceIdType.MESH

LANES = 128
HG_DK = 128
HG_CHUNK = 64
HG_SUB = 16
HG_ROWS = 512
HG_HEADS_PER_STEP = 2
ATT_HD = 64
ATT_G = 4
WINDOW = 128
DEPTH = 2
ALPHA = (2.0 * DEPTH) ** 0.25
LN_EPS = 1e-5
RMS_EPS = 1e-6
ADAM_LR, ADAM_B1, ADAM_B2, ADAM_EPS, ADAM_WD, ADAM_STEP = 0.001, 0.9, 0.999, 1e-08, 0.01, 10
N_CHIPS = 4
N_DEV = 8
VMEM_LIMIT = 56 * 1024 * 1024
NEG = -1e30


def _pick(n, cap):
    best = None
    for d in range(LANES, min(n, cap) + 1, LANES):
        if n % d == 0:
            best = d
    return n if best is None else best


def _pick_rows(m, cap):
    best = None
    for d in range(16, min(m, cap) + 1, 16):
        if m % d == 0:
            best = d
    return m if best is None else best


def _params(sem):
    return pltpu.CompilerParams(dimension_semantics=sem, vmem_limit_bytes=VMEM_LIMIT)


def _zeros_index(ndim, grid_rank=3):
    return (lambda i, j, kk: (0,) * ndim) if grid_rank == 3 else (lambda kk, i: (0,) * ndim)


def _mm(a, b, *, name, la=None, lb=None, ta=False, tb=False, bias=None, add=None, out_dtype=F32,
        out_layers=None, out_layer=None, after=None, post=None, tile_cols=None, caps=(1024, 1536, 2048)):
    ar, ac = a.shape[-2:]
    br, bc = b.shape[-2:]
    m, k = (ac, ar) if ta else (ar, ac)
    k2, n = (bc, br) if tb else (br, bc)
    assert k == k2, (a.shape, b.shape, ta, tb)
    if post is not None:
        caps = (512, n if tile_cols is None else tile_cols, caps[2])
    tm, tn, tk = _pick(m, caps[0]), _pick(n, caps[1]), _pick(k, caps[2])
    assert post is None or tn == caps[1]
    nk = k // tk
    gi, gj = m // tm, n // tn
    a_bytes, b_bytes = m * k * a.dtype.itemsize, k * n * b.dtype.itemsize
    rows_outer = (a_bytes + b_bytes * (gi if gj * nk > 1 else 1)) <= (b_bytes + a_bytes * (gj if gi * nk > 1 else 1))
    k_outer = post is not None and nk > 1 and gj == 1
    grid = (nk, gi) if k_outer else (gi, gj, nk) if rows_outer else (gj, gi, nk)

    def bs(block, idx, late=False):
        if k_outer:
            return pl.BlockSpec(block, lambda kk, i: idx(jnp.where(kk == nk - 1, i, 0) if late else i, 0, kk))
        return pl.BlockSpec(block, idx if rows_outer else (lambda q, p, kk: idx(p, q, kk)))

    def spec(block, idx, layer):
        if layer is None:
            return bs(block, idx)
        return bs((None,) + block, lambda i, j, kk: (layer,) + idx(i, j, kk))

    a_spec = spec((tk, tm), lambda i, j, kk: (kk, i), la) if ta else spec((tm, tk), lambda i, j, kk: (i, kk), la)
    b_spec = spec((tn, tk), lambda i, j, kk: (j, kk), lb) if tb else spec((tk, tn), lambda i, j, kk: (kk, j), lb)
    in_specs, operands = [a_spec, b_spec], [a, b]
    if bias is not None:
        in_specs.append(bs((1, tn), lambda i, j, kk: (0, j)))
        operands.append(bias)
    if add is not None:
        in_specs.append(bs((tm, tn), lambda i, j, kk: (i, j), late=True))
        operands.append(add)
    if after is not None:
        in_specs.append(pl.BlockSpec(memory_space=pl.ANY))
        operands.append(after)
    dims = (((0 if ta else 1,), (1 if tb else 0,)), ((), ()))
    has_bias, has_add = bias is not None, add is not None
    if post is None:
        fn, rows, whole, outs, sums = None, [], [], [], []
        out_shape = jax.ShapeDtypeStruct((m, n) if out_layers is None else (out_layers, m, n), out_dtype)
        out_specs = spec((tm, tn), lambda i, j, kk: (i, j), out_layer)
    else:
        fn, rows, whole, outs, sums = post
        in_specs += [bs((tm, r.shape[-1] // gj), lambda i, j, kk: (i, j), late=True) for r in rows]
        in_specs += [pl.BlockSpec(tuple(w.shape), _zeros_index(w.ndim, len(grid))) for w in whole]
        operands += list(rows) + list(whole)
        out_shape = [jax.ShapeDtypeStruct(sh, dt) for sh, dt in list(outs) + list(sums)]
        out_specs = ([bs((tm, sh[-1] // gj), lambda i, j, kk: (i, j), late=True) for sh, _ in outs]
                     + [pl.BlockSpec(tuple(sh), _zeros_index(len(sh), len(grid))) for sh, _ in sums])
    n_in, n_extra, n_outs, n_sums = len(operands), len(rows) + len(whole), len(outs), len(sums)

    def body(*refs):
        a_ref, b_ref = refs[0], refs[1]
        pos = 2
        bias_ref = add_ref = None
        if has_bias:
            bias_ref = refs[pos]
            pos += 1
        if has_add:
            add_ref = refs[pos]
            pos += 1
        extra_refs = refs[n_in - n_extra:n_in]
        out_refs = refs[n_in:n_in + max(n_outs, 1)]
        sum_refs = refs[n_in + n_outs:n_in + n_outs + n_sums]
        acc_ref = refs[-1] if nk > 1 else None
        part = lax.dot_general(a_ref[...].astype(BF16), b_ref[...].astype(BF16), dims, preferred_element_type=F32)

        def finish(total):
            if has_bias:
                total = total + bias_ref[...]
            if has_add:
                total = total + add_ref[...]
            if fn is None:
                out_refs[0][...] = total.astype(out_refs[0].dtype)
                return
            res = fn(total, *[r[...] for r in extra_refs])
            for ref, val in zip(out_refs, res[:n_outs]):
                ref[...] = val.astype(ref.dtype)
            if n_sums:
                @pl.when(pl.program_id(1 if k_outer or not rows_outer else 0) == 0)
                def _():
                    for ref in sum_refs:
                        ref[...] = jnp.zeros(ref.shape, ref.dtype)

                for ref, val in zip(sum_refs, res[n_outs:]):
                    ref[...] += val

        if nk == 1:
            finish(part)
        elif k_outer:
            kk = pl.program_id(0)
            rows_i = pl.ds(pl.multiple_of(pl.program_id(1) * tm, tm), tm)

            @pl.when(kk == 0)
            def _():
                acc_ref[rows_i, :] = part

            @pl.when(kk > 0)
            def _():
                acc_ref[rows_i, :] += part

            @pl.when(kk == nk - 1)
            def _():
                finish(acc_ref[rows_i, :])
        else:
            kk = pl.program_id(2)

            @pl.when(kk == 0)
            def _():
                acc_ref[...] = part

            @pl.when(kk > 0)
            def _():
                acc_ref[...] += part

            @pl.when(kk == nk - 1)
            def _():
                finish(acc_ref[...])

    return pl.pallas_call(
        body, name=name, grid=grid, in_specs=in_specs, out_specs=out_specs, out_shape=out_shape,
        scratch_shapes=[pltpu.VMEM((m, n) if k_outer else (tm, tn), F32)] if nk > 1 else [],
        compiler_params=_params(("arbitrary", "arbitrary") if k_outer
                                else ("arbitrary" if n_sums else "parallel", "parallel", "arbitrary") if rows_outer
                                else ("parallel", "arbitrary" if n_sums else "parallel", "arbitrary")),
    )(*operands)


def _rowwise(fn, rows, whole, outs, sums=(), *, name, tm=256):
    m = rows[0].shape[-2]
    tm = _pick_rows(m, tm)
    n_rows, n_whole, n_outs, n_sums = len(rows), len(whole), len(outs), len(sums)

    def rspec(shape):
        lead = len(shape) - 2
        return pl.BlockSpec(tuple(shape[:-2]) + (tm, shape[-1]), lambda i: (0,) * lead + (i, 0))

    def wspec(shape):
        return pl.BlockSpec(tuple(shape), lambda i: (0,) * len(shape))

    def body(*refs):
        vals = [r[...] for r in refs[:n_rows + n_whole]]
        out_refs = refs[n_rows + n_whole:n_rows + n_whole + n_outs]
        sum_refs = refs[n_rows + n_whole + n_outs:]
        res = fn(*vals)
        for ref, val in zip(out_refs, res[:n_outs]):
            ref[...] = val.astype(ref.dtype)
        if n_sums:
            @pl.when(pl.program_id(0) == 0)
            def _():
                for ref in sum_refs:
                    ref[...] = jnp.zeros(ref.shape, ref.dtype)

            for ref, val in zip(sum_refs, res[n_outs:]):
                ref[...] += val

    result = pl.pallas_call(
        body, name=name, grid=(m // tm,),
        in_specs=[rspec(r.shape) for r in rows] + [wspec(w.shape) for w in whole],
        out_specs=[rspec(s) for s, _ in outs] + [wspec(s) for s, _ in sums],
        out_shape=[jax.ShapeDtypeStruct(s, d) for s, d in list(outs) + list(sums)],
        compiler_params=_params(("arbitrary",)),
    )(*rows, *whole)
    return result


def _sigmoid(v):
    return jax.nn.sigmoid(v)


def _col_sum(v):
    return jnp.sum(v, axis=0, keepdims=True)


def _ln_stats(z):
    mu = jnp.mean(z, axis=-1, keepdims=True)
    zc = z - mu
    var = jnp.mean(zc * zc, axis=-1, keepdims=True)
    rstd = lax.rsqrt(var + LN_EPS)
    return zc * rstd, rstd


def _ln_fwd_fn(xin, h, gain, bias):
    xhat, _ = _ln_stats(ALPHA * xin + h)
    y = xhat * gain + bias
    return y, y


def _ple_ln_fwd_fn(xin, pg, pu, gain, bias):
    xhat, _ = _ln_stats(ALPHA * xin + _sigmoid(pg) * pu)
    y = xhat * gain + bias
    return y, y


def _ln_dz(dy, z, gain):
    xhat, rstd = _ln_stats(z)
    dxhat = dy * gain
    dz = rstd * (dxhat - jnp.mean(dxhat, axis=-1, keepdims=True)
                 - xhat * jnp.mean(dxhat * xhat, axis=-1, keepdims=True))
    return dz, _col_sum(dy * xhat), _col_sum(dy)


def _ln_bwd_fn(dy, xin, h, gain):
    dz, dgain, dbias = _ln_dz(dy, ALPHA * xin + h, gain)
    return ALPHA * dz, dz, dgain, dbias, _col_sum(dz)


def _ple_ln_bwd_fn(dy, xin, pg, pu, gain):
    sg = _sigmoid(pg)
    dz, dgain, dbias = _ln_dz(dy, ALPHA * xin + sg * pu, gain)
    dpg = dz * pu * sg * (1.0 - sg)
    return ALPHA * dz, dpg, dz * sg, dgain, dbias, _col_sum(dpg)


def _swiglu_fwd_fn(gu):
    hid = gu.shape[-1] // 2
    gate, up = gu[:, :hid], gu[:, hid:]
    return gu, gate * _sigmoid(gate) * up


def _swiglu_bwd_fn(dact, gu):
    gu = gu.astype(F32)
    hid = gu.shape[-1] // 2
    gate, up = gu[:, :hid], gu[:, hid:]
    sg = _sigmoid(gate)
    dgate = dact * up * sg * (1.0 + gate * (1.0 - sg))
    dup = dact * gate * sg
    return (jnp.concatenate([dgate, dup], axis=-1),)


def _loss_fn(y, target):
    err = y - target
    inv = 1.0 / y.shape[-1]
    part = 0.5 * inv * jnp.sum(jnp.sum(err * err, axis=-1, keepdims=True), axis=0, keepdims=True)
    return err * inv, jnp.broadcast_to(part, (1, LANES))


def _adam_fn(w, mom, vel, p_own, p_sib):
    g = p_own.astype(F32) + p_sib.astype(F32)
    m_new = ADAM_B1 * mom + (1.0 - ADAM_B1) * g
    v_new = ADAM_B2 * vel + (1.0 - ADAM_B2) * (g * g)
    m_hat = m_new / (1.0 - ADAM_B1 ** ADAM_STEP)
    v_hat = v_new / (1.0 - ADAM_B2 ** ADAM_STEP)
    delta = -ADAM_LR * (m_hat / (jnp.sqrt(v_hat) + ADAM_EPS) + ADAM_WD * w)
    return g, delta, m_new, v_new


def _sum_slots_fn(slots):
    acc = slots[0].astype(F32)
    for s in range(1, slots.shape[0]):
        acc = acc + slots[s].astype(F32)
    return (acc,)


def _split2(x):
    hi = x.astype(BF16)
    return hi, (x - hi.astype(F32)).astype(BF16)


def _dot3(a, b, dims):
    a_hi, a_lo = _split2(a)
    b_hi, b_lo = _split2(b)
    dn = (dims, ((), ()))
    return (lax.dot_general(a_hi, b_hi, dn, preferred_element_type=F32)
            + (lax.dot_general(a_hi, b_lo, dn, preferred_element_type=F32)
               + lax.dot_general(a_lo, b_hi, dn, preferred_element_type=F32)))


def _tdot(mask01, b):
    m = mask01.astype(BF16)
    b_hi = b.astype(BF16)
    rest = b - b_hi.astype(F32)
    b_mid = rest.astype(BF16)
    b_lo = (rest - b_mid.astype(F32)).astype(BF16)
    dn = (((1,), (0,)), ((), ()))
    return (lax.dot_general(m, b_hi, dn, preferred_element_type=F32)
            + (lax.dot_general(m, b_mid, dn, preferred_element_type=F32)
               + lax.dot_general(m, b_lo, dn, preferred_element_type=F32)))


def _hdot(a, b):
    return _dot3(a, b, ((1,), (0,)))


def _hdot_nt(a, b):
    return _dot3(a, b, ((1,), (1,)))


def _hdot_tn(a, b):
    return _dot3(a, b, ((0,), (0,)))


def _dot(a, b):
    return lax.dot_general(a.astype(BF16), b.astype(BF16), (((1,), (0,)), ((), ())), preferred_element_type=F32)


def _dot_nt(a, b):
    return lax.dot_general(a.astype(BF16), b.astype(BF16), (((1,), (1,)), ((), ())), preferred_element_type=F32)


def _dot_tn(a, b):
    return lax.dot_general(a.astype(BF16), b.astype(BF16), (((0,), (0,)), ((), ())), preferred_element_type=F32)


def _hg_masks():
    c = HG_CHUNK
    row = lax.broadcasted_iota(jnp.int32, (c, c), 0)
    col = lax.broadcasted_iota(jnp.int32, (c, c), 1)
    base = row & (-HG_SUB)
    return row, col, base, col <= row, col < base


def _hg_gates(qr, fr, alb):
    lbound = _sigmoid(alb[0:1, :] - alb[1:2, :])
    sig = _sigmoid(fr)
    forget = lbound + (1.0 - lbound) * sig
    kk = (1.0 - lbound) * _sigmoid(-fr)
    qt = qr * _sigmoid(qr) * (HG_DK ** -0.5)
    return qt, kk, jnp.log(forget), lbound, sig, forget


def _hg_scores(qt, kk, g, scores=True):
    c, nsub = HG_CHUNK, HG_CHUNK // HG_SUB
    row, col, base, causal, below = _hg_masks()
    b = _tdot(causal, g)
    rr = _tdot(below, g)
    bq = b - rr
    qh = qt * jnp.exp(bq)
    edecs = [None]
    parts = [jnp.zeros((HG_SUB, c), F32)]
    for i in range(1, nsub):
        edec = jnp.exp(jnp.minimum(rr[i * HG_SUB:i * HG_SUB + 1, :] - b, 0.0))
        edecs.append(edec)
        if scores:
            parts.append(_dot_nt(qh[i * HG_SUB:(i + 1) * HG_SUB, :], kk * edec))
    b3 = b.reshape(nsub, HG_SUB, HG_DK)
    q3 = qt.reshape(nsub, HG_SUB, HG_DK)
    k3 = kk.reshape(nsub, HG_SUB, HG_DK)
    if not scores:
        return None, b, bq, qh, edecs, (b3, q3, k3)
    a = jnp.where(below, jnp.concatenate(parts, axis=0), 0.0)
    for j in range(HG_SUB):
        e = jnp.exp(jnp.minimum(b3 - b3[:, j:j + 1, :], 0.0))
        colv = jnp.sum(q3 * e * k3[:, j:j + 1, :], axis=-1, keepdims=True).reshape(c, 1)
        a = jnp.where(col == base + j, colv, a)
    a = jnp.where(causal, a, 0.0)
    return a, b, bq, qh, edecs, (b3, q3, k3)


def _hg_norm(o, gr, gain):
    r = lax.rsqrt(jnp.mean(o * o, axis=-1, keepdims=True) + RMS_EPS)
    sg = _sigmoid(gr)
    return o * r * gain, r, sg


def _hgrn2_fwd(proj, alb, gain, *, rb):
    m, d4 = proj.shape
    d = d4 // 4
    heads = d // HG_DK
    hp = HG_HEADS_PER_STEP
    rb = min(rb, m)
    cpb = rb // HG_CHUNK
    nrb = m // rb

    def body(q_ref, f_ref, v_ref, g_ref, alb_ref, gain_ref, o_ref, og_ref, st_ref, a_ref, state):
        @pl.when(pl.program_id(1) == 0)
        def _():
            state[...] = jnp.zeros(state.shape, F32)

        def chunk(ci, carry):
            sl = pl.ds(pl.multiple_of(ci * HG_CHUNK, HG_CHUNK), HG_CHUNK)
            for u in range(hp):
                ln = slice(u * HG_DK, (u + 1) * HG_DK)
                qt, kk, g, _, _, _ = _hg_gates(q_ref[sl, ln], f_ref[sl, ln], alb_ref[:, ln])
                v = v_ref[sl, ln]
                st = state[u]
                st_ref[u, ci] = st
                a, b, _, _, _, _ = _hg_scores(qt, kk, g)
                a_ref[u, ci] = a.astype(a_ref.dtype)
                o = _dot(a, v) + _dot_nt(qt * jnp.exp(b), st)
                b_last = b[HG_CHUNK - 1:HG_CHUNK, :]
                state[u] = st * jnp.exp(b_last) + _hdot_tn(v, kk * jnp.exp(b_last - b))
                o_ref[sl, ln] = o
                n, _, sg = _hg_norm(o, g_ref[sl, ln], gain_ref[...])
                og_ref[sl, ln] = (n * g_ref[sl, ln] * sg).astype(og_ref.dtype)
            return carry

        lax.fori_loop(0, cpb, chunk, 0)

    def col(cidx):
        return pl.BlockSpec((rb, hp * HG_DK), lambda h, r: (r, cidx * (heads // hp) + h))

    return pl.pallas_call(
        body, name="hgrn2_fwd", grid=(heads // hp, nrb),
        in_specs=[col(0), col(1), col(2), col(3),
                  pl.BlockSpec((2, hp * HG_DK), lambda h, r: (0, h)),
                  pl.BlockSpec((1, HG_DK), lambda h, r: (0, 0))],
        out_specs=[pl.BlockSpec((rb, hp * HG_DK), lambda h, r: (r, h)),
                   pl.BlockSpec((rb, hp * HG_DK), lambda h, r: (r, h)),
                   pl.BlockSpec((hp, cpb, HG_DK, HG_DK), lambda h, r: (h, r, 0, 0)),
                   pl.BlockSpec((hp, cpb, HG_CHUNK, HG_CHUNK), lambda h, r: (h, r, 0, 0))],
        out_shape=[jax.ShapeDtypeStruct((m, d), F32), jax.ShapeDtypeStruct((m, d), BF16),
                   jax.ShapeDtypeStruct((heads, m // HG_CHUNK, HG_DK, HG_DK), F32),
                   jax.ShapeDtypeStruct((heads, m // HG_CHUNK, HG_CHUNK, HG_CHUNK), BF16)],
        scratch_shapes=[pltpu.VMEM((hp, HG_DK, HG_DK), F32)],
        compiler_params=_params(("parallel", "arbitrary")),
    )(proj, proj, proj, proj, alb, gain)


def _hgrn2_bwd(proj, o_pre, states, scores, dog, alb, gain, *, rb):
    m, d4 = proj.shape
    d = d4 // 4
    heads = d // HG_DK
    rb = min(rb, m)
    cpb = rb // HG_CHUNK
    nrb = m // rb
    c, nsub = HG_CHUNK, HG_CHUNK // HG_SUB

    def body(q_ref, f_ref, v_ref, g_ref, o_ref, st_ref, a_ref, dog_ref, alb_ref, gain_ref,
             dq_ref, df_ref, dv_ref, dg_ref, dalb_ref, dgain_ref, dstate, carry_ref):
        first = (pl.program_id(0) == 0) & (pl.program_id(1) == 0)

        @pl.when(first)
        def _():
            dgain_ref[...] = jnp.zeros(dgain_ref.shape, F32)

        @pl.when(pl.program_id(1) == 0)
        def _():
            dstate[...] = jnp.zeros(dstate.shape, F32)
            carry_ref[...] = jnp.zeros(carry_ref.shape, F32)
            dalb_ref[...] = jnp.zeros(dalb_ref.shape, F32)

        row, col, base, causal, below = _hg_masks()
        sub_iota = lax.broadcasted_iota(jnp.int32, (nsub, HG_SUB, HG_DK), 1)
        row_k = lax.broadcasted_iota(jnp.int32, (c, HG_DK), 0)
        upper = col >= row

        def chunk(step, carry):
            ci = cpb - 1 - step
            sl = pl.ds(pl.multiple_of(ci * HG_CHUNK, HG_CHUNK), HG_CHUNK)
            qr, fr, v, gr = q_ref[sl, :], f_ref[sl, :], v_ref[sl, :], g_ref[sl, :]
            qt, kk, g, lbound, sig, forget = _hg_gates(qr, fr, alb_ref[...])
            o = o_ref[sl, :]
            dogv = dog_ref[sl, :]
            gain_v = gain_ref[...]
            n, r, sg = _hg_norm(o, gr, gain_v)
            dgr = dogv * n * sg * (1.0 + gr * (1.0 - sg))
            dn = dogv * gr * sg
            dgain_ref[...] += _col_sum(dn * o * r)
            u = dn * gain_v
            d_o = r * u - o * (r * r * r) * jnp.mean(u * o, axis=-1, keepdims=True)
            st0 = st_ref[ci]
            dst = dstate[...]
            _, b, bq, qh, edecs, (b3, q3, k3) = _hg_scores(qt, kk, g, scores=False)
            a = a_ref[ci]
            eb = jnp.exp(b)
            b_last = b[c - 1:c, :]
            kdl_dec = jnp.exp(b_last - b)
            kdl = kk * kdl_dec
            d_a = jnp.where(causal, _dot_nt(d_o, v), 0.0)
            d_at = _dot_nt(v, d_o)
            dv = _dot_tn(a, d_o) + _dot_nt(kdl, dst)
            dq = eb * _hdot(d_o, st0)
            dk = _hdot(v, dst) * kdl_dec
            d_a_below = jnp.where(below, d_a, 0.0)
            dq_parts = [jnp.zeros((HG_SUB, HG_DK), F32)]
            for i in range(1, nsub):
                lo, hi = i * HG_SUB, (i + 1) * HG_SUB
                dq_parts.append(_hdot(d_a_below[lo:hi, :], kk * edecs[i]))
                gi = _hdot(d_at[:, lo:hi], qh[lo:hi, :])
                dk = dk + jnp.where(row_k < lo, edecs[i] * gi, 0.0)
            dq = dq + jnp.concatenate(dq_parts, axis=0) * jnp.exp(bq)
            dq3 = jnp.zeros((nsub, HG_SUB, HG_DK), F32)
            dk3 = jnp.zeros((nsub, HG_SUB, HG_DK), F32)
            for j in range(HG_SUB):
                e = jnp.exp(jnp.minimum(b3 - b3[:, j:j + 1, :], 0.0))
                dcol = jnp.sum(jnp.where(col == base + j, d_a, 0.0), axis=-1, keepdims=True)
                t1 = dcol.reshape(nsub, HG_SUB, 1) * e
                dq3 = dq3 + t1 * k3[:, j:j + 1, :]
                dk3 = jnp.where(sub_iota == j, jnp.sum(t1 * q3, axis=1, keepdims=True), dk3)
            dq = dq + dq3.reshape(c, HG_DK)
            dk = dk + dk3.reshape(c, HG_DK)
            dstate[...] = dst * jnp.exp(b_last) + _hdot_tn(d_o, qt * eb)
            dglog = _tdot(upper, qt * dq - kk * dk) + carry_ref[...]
            carry_ref[...] = dglog[0:1, :]
            dforget = dglog / forget
            one_m_lb = 1.0 - lbound
            dsig = (dforget - dk) * one_m_lb
            sneg = _sigmoid(-fr)
            dlb = _col_sum(dforget * (1.0 - sig) - dk * sneg)
            dalb0 = dlb * lbound * one_m_lb
            dalb_ref[...] += jnp.concatenate([dalb0, -dalb0], axis=0)
            sq = _sigmoid(qr)
            dq_ref[sl, :] = (dq * (HG_DK ** -0.5) * sq * (1.0 + qr * (1.0 - sq))).astype(dq_ref.dtype)
            df_ref[sl, :] = (dsig * sig * (1.0 - sig)).astype(df_ref.dtype)
            dv_ref[sl, :] = dv.astype(dv_ref.dtype)
            dg_ref[sl, :] = dgr.astype(dg_ref.dtype)
            return carry

        lax.fori_loop(0, cpb, chunk, 0, unroll=2)

    def rev(r):
        return nrb - 1 - r

    def col(cidx):
        return pl.BlockSpec((rb, HG_DK), lambda h, r: (rev(r), cidx * heads + h))

    def head_rows():
        return pl.BlockSpec((rb, HG_DK), lambda h, r: (rev(r), h))

    return pl.pallas_call(
        body, name="hgrn2_bwd", grid=(heads, nrb),
        in_specs=[col(0), col(1), col(2), col(3), head_rows(),
                  pl.BlockSpec((None, cpb, HG_DK, HG_DK), lambda h, r: (h, rev(r), 0, 0)),
                  pl.BlockSpec((None, cpb, HG_CHUNK, HG_CHUNK), lambda h, r: (h, rev(r), 0, 0)),
                  head_rows(),
                  pl.BlockSpec((2, HG_DK), lambda h, r: (0, h)),
                  pl.BlockSpec((1, HG_DK), lambda h, r: (0, 0))],
        out_specs=[head_rows(), head_rows(), head_rows(), head_rows(),
                   pl.BlockSpec((2, HG_DK), lambda h, r: (0, h)),
                   pl.BlockSpec((1, HG_DK), lambda h, r: (0, 0))],
        out_shape=[jax.ShapeDtypeStruct((m, d), BF16)] * 4
                  + [jax.ShapeDtypeStruct((2, d), F32), jax.ShapeDtypeStruct((1, HG_DK), F32)],
        scratch_shapes=[pltpu.VMEM((HG_DK, HG_DK), F32), pltpu.VMEM((1, HG_DK), F32)],
        compiler_params=_params(("arbitrary", "arbitrary")),
    )(proj, proj, proj, proj, o_pre, states, scores, dog, alb, gain)


def _swa_probs(qh, kp, kc, sink, slope, has_prev):
    rows = qh.shape[0]
    qi = lax.broadcasted_iota(jnp.int32, (rows, WINDOW), 0) & (WINDOW - 1)
    si = lax.broadcasted_iota(jnp.int32, (rows, WINDOW), 1)
    scale = ATT_HD ** -0.5
    dist_c = (qi - si).astype(F32)
    s_p = _dot_nt(qh, kp) * scale - slope * (dist_c + float(WINDOW))
    s_c = _dot_nt(qh, kc) * scale - slope * dist_c
    s_p = jnp.where((si > qi) & has_prev, s_p, NEG)
    s_c = jnp.where(si <= qi, s_c, NEG)
    mx = jnp.maximum(jnp.maximum(jnp.max(s_p, axis=-1, keepdims=True), jnp.max(s_c, axis=-1, keepdims=True)), sink)
    e_p, e_c, e_s = jnp.exp(s_p - mx), jnp.exp(s_c - mx), jnp.exp(sink - mx)
    inv = 1.0 / (jnp.sum(e_p, axis=-1, keepdims=True) + jnp.sum(e_c, axis=-1, keepdims=True) + e_s)
    return e_p * inv, e_c * inv, e_s * inv


def _slope(h, n_heads):
    return float(2.0 ** (-8.0 * (h + 1) / n_heads))


def _swa_group(ref_vals, sink_ref, kh, n_heads):
    heads = [kh * ATT_G + g for g in range(ATT_G)]
    stacked = [jnp.concatenate([v[:, h * ATT_HD:(h + 1) * ATT_HD] for h in heads], axis=0) for v in ref_vals]
    grp = lax.shift_right_logical(lax.broadcasted_iota(jnp.int32, (ATT_G * WINDOW, 1), 0), WINDOW.bit_length() - 1)
    slope = jnp.zeros((ATT_G * WINDOW, 1), F32)
    sink = jnp.zeros((ATT_G * WINDOW, 1), F32)
    for g, h in enumerate(heads):
        slope = jnp.where(grp == g, _slope(h, n_heads), slope)
        sink = jnp.where(grp == g, sink_ref[:, h:h + 1], sink)
    return stacked, slope, sink


def _swa_fwd(q, kv, sinks):
    m, d = q.shape
    n_heads = d // ATT_HD
    kvh = n_heads // ATT_G
    kd = kvh * ATT_HD
    nb = m // WINDOW

    def body(q_ref, kvp_ref, kvc_ref, sink_ref, o_ref):
        has_prev = pl.program_id(0) > 0
        qv, kvp, kvc = q_ref[...], kvp_ref[...], kvc_ref[...]
        outs = []
        for kh in range(kvh):
            ks = slice(kh * ATT_HD, (kh + 1) * ATT_HD)
            vs = slice(kd + kh * ATT_HD, kd + (kh + 1) * ATT_HD)
            (q4,), slope, sink = _swa_group([qv], sink_ref, kh, n_heads)
            p_p, p_c, _ = _swa_probs(q4, kvp[:, ks], kvc[:, ks], sink, slope, has_prev)
            o4 = _dot(p_p, kvp[:, vs]) + _dot(p_c, kvc[:, vs])
            outs += [o4[g * WINDOW:(g + 1) * WINDOW, :] for g in range(ATT_G)]
        o_ref[...] = jnp.concatenate(outs, axis=-1).astype(o_ref.dtype)

    return pl.pallas_call(
        body, name="swa_fwd", grid=(nb,),
        in_specs=[pl.BlockSpec((WINDOW, d), lambda n: (n, 0)),
                  pl.BlockSpec((WINDOW, 2 * kd), lambda n: (jnp.maximum(n - 1, 0), 0)),
                  pl.BlockSpec((WINDOW, 2 * kd), lambda n: (n, 0)),
                  pl.BlockSpec((1, n_heads), lambda n: (0, 0))],
        out_specs=pl.BlockSpec((WINDOW, d), lambda n: (n, 0)),
        out_shape=jax.ShapeDtypeStruct((m, d), BF16),
        compiler_params=_params(("arbitrary",)),
    )(q, kv, kv, sinks)


def _swa_bwd(q, kv, sinks, dao):
    m, d = q.shape
    n_heads = d // ATT_HD
    kvh = n_heads // ATT_G
    kd = kvh * ATT_HD
    nb = m // WINDOW
    scale = ATT_HD ** -0.5

    def body(q_ref, kvp_ref, kvc_ref, sink_ref, do_ref, dq_ref, dkvc_ref, dkvp_ref, dqsum_ref, dsink_ref):
        @pl.when(pl.program_id(0) == 0)
        def _():
            dqsum_ref[...] = jnp.zeros(dqsum_ref.shape, F32)
            dsink_ref[...] = jnp.zeros(dsink_ref.shape, F32)

        has_prev = pl.program_id(0) > 0
        qv, kvp, kvc, dov = q_ref[...], kvp_ref[...], kvc_ref[...], do_ref[...]
        lane_h = lax.broadcasted_iota(jnp.int32, (1, n_heads), 1)
        dsink = jnp.zeros((1, n_heads), F32)
        dq_parts, dk_p, dk_c, dv_p, dv_c = [], [], [], [], []
        for kh in range(kvh):
            ks = slice(kh * ATT_HD, (kh + 1) * ATT_HD)
            vs = slice(kd + kh * ATT_HD, kd + (kh + 1) * ATT_HD)
            kp, kc, vp, vc = kvp[:, ks], kvc[:, ks], kvp[:, vs], kvc[:, vs]
            (q4, do4), slope, sink = _swa_group([qv, dov], sink_ref, kh, n_heads)
            p_p, p_c, p_s = _swa_probs(q4, kp, kc, sink, slope, has_prev)
            dp_p, dp_c = _dot_nt(do4, vp), _dot_nt(do4, vc)
            delta = jnp.sum(p_p * dp_p, axis=-1, keepdims=True) + jnp.sum(p_c * dp_c, axis=-1, keepdims=True)
            ds_p, ds_c = p_p * (dp_p - delta), p_c * (dp_c - delta)
            sink_term = p_s * delta
            dq4 = (_dot(ds_p, kp) + _dot(ds_c, kc)) * scale
            for g in range(ATT_G):
                rows = slice(g * WINDOW, (g + 1) * WINDOW)
                dsink = dsink + jnp.where(lane_h == kh * ATT_G + g, -_col_sum(sink_term[rows, :]), 0.0)
                dq_parts.append(dq4[rows, :])
            dk_p.append(_dot_tn(ds_p, q4) * scale)
            dk_c.append(_dot_tn(ds_c, q4) * scale)
            dv_p.append(_dot_tn(p_p, do4))
            dv_c.append(_dot_tn(p_c, do4))
        dq = jnp.concatenate(dq_parts, axis=-1)
        dq_ref[...] = dq.astype(dq_ref.dtype)
        dqsum_ref[...] += _col_sum(dq)
        dsink_ref[...] += dsink
        dkvc_ref[...] = jnp.concatenate(dk_c + dv_c, axis=-1)
        dkvp_ref[...] = jnp.concatenate(dk_p + dv_p, axis=-1)

    return pl.pallas_call(
        body, name="swa_bwd", grid=(nb,),
        in_specs=[pl.BlockSpec((WINDOW, d), lambda n: (n, 0)),
                  pl.BlockSpec((WINDOW, 2 * kd), lambda n: (jnp.maximum(n - 1, 0), 0)),
                  pl.BlockSpec((WINDOW, 2 * kd), lambda n: (n, 0)),
                  pl.BlockSpec((1, n_heads), lambda n: (0, 0)),
                  pl.BlockSpec((WINDOW, d), lambda n: (n, 0))],
        out_specs=[pl.BlockSpec((WINDOW, d), lambda n: (n, 0)),
                   pl.BlockSpec((WINDOW, 2 * kd), lambda n: (n, 0)),
                   pl.BlockSpec((WINDOW, 2 * kd), lambda n: (n, 0)),
                   pl.BlockSpec((1, d), lambda n: (0, 0)),
                   pl.BlockSpec((1, n_heads), lambda n: (0, 0))],
        out_shape=[jax.ShapeDtypeStruct((m, d), BF16), jax.ShapeDtypeStruct((m, 2 * kd), F32),
                   jax.ShapeDtypeStruct((m, 2 * kd), F32), jax.ShapeDtypeStruct((1, d), F32),
                   jax.ShapeDtypeStruct((1, n_heads), F32)],
        compiler_params=_params(("arbitrary",)),
    )(q, kv, kv, sinks, dao)


def _kv_grad_combine(dkv_cur, dkv_prev):
    m, w = dkv_cur.shape
    nb = m // WINDOW

    def body(cur_ref, nxt_ref, o_ref, sum_ref):
        @pl.when(pl.program_id(0) == 0)
        def _():
            sum_ref[...] = jnp.zeros(sum_ref.shape, F32)

        total = cur_ref[...] + jnp.where(pl.program_id(0) < nb - 1, nxt_ref[...], 0.0)
        o_ref[...] = total.astype(o_ref.dtype)
        sum_ref[...] += _col_sum(total)

    return pl.pallas_call(
        body, name="kv_grad_combine", grid=(nb,),
        in_specs=[pl.BlockSpec((WINDOW, w), lambda n: (n, 0)),
                  pl.BlockSpec((WINDOW, w), lambda n: (jnp.minimum(n + 1, nb - 1), 0))],
        out_specs=[pl.BlockSpec((WINDOW, w), lambda n: (n, 0)), pl.BlockSpec((1, w), lambda n: (0, 0))],
        out_shape=[jax.ShapeDtypeStruct((m, w), BF16), jax.ShapeDtypeStruct((1, w), F32)],
        compiler_params=_params(("arbitrary",)),
    )(dkv_cur, dkv_prev)


def _row(v):
    return v.reshape(1, -1)


def _local_step(x, p, target, wget, grad_sink, ln_gain, ln_bias, alb, norm_gain, kv_b, b_q, sinks, b_out, ple_b):
    gs = {}
    gains = [[_row(ln_gain[i, j]) for j in range(3)] for i in range(DEPTH)]
    biases = [[_row(ln_bias[i, j]) for j in range(3)] for i in range(DEPTH)]
    sd = x.shape
    pending = [None]

    def mm(a, b, lb=0, **kw):
        after, pending[0] = pending[0], None
        return _mm(a, b, lb=lb, after=after, **kw)

    def mm_ln(a, wt, xin, i, j, nm, bias=None, pu=None):
        if pu is None:
            fn, rows = (lambda h, xv, g, bv: (h,) + _ln_fwd_fn(xv, h, g, bv)), [xin]
        else:
            fn, rows = (lambda h, xv, puv, g, bv: (h,) + _ple_ln_fwd_fn(xv, h, puv, g, bv)), [xin, pu]
        h, y, yb = _mm(a, wt, lb=0, bias=bias, name=nm,
                       post=(fn, rows, [gains[i][j], biases[i][j]], [(sd, F32), (sd, F32), (sd, BF16)], []))
        return h, (y, yb)

    def mm_ln_bwd(a, wt, add, xin, h, i, j, nm):
        dx_part, dh, dg, db, dhsum = mm(a, wt, tb=True, add=add, name=nm,
                                        post=(_ln_bwd_fn, [xin, h], [gains[i][j]], [(sd, F32), (sd, BF16)],
                                              [((1, sd[1]), F32)] * 3))
        gs[f"ln_gain_{i}_{j}"], gs[f"ln_bias_{i}_{j}"] = dg, db
        return dx_part, dh, dhsum

    def tail_fwd(xa, i):
        wgu = wget("ffn_w_gate_up", i, xa[1])
        hid2 = wgu.shape[-1]
        gu, act = _mm(xa[1], wgu, lb=0, name=f"ffn_up_swiglu{i}", tile_cols=hid2 // 2,
                      post=(_swiglu_fwd_fn, [], [], [((sd[0], hid2), BF16), ((sd[0], hid2 // 2), BF16)], []))
        f, xb = mm_ln(act, wget("ffn_w_down", i, act), xa[0], i, 1, f"ffn_down_ln{i}")
        pu = _mm(p[i], wget("ple_w_up", i, act), lb=0, name=f"ple_up{i}")
        pg, xc = mm_ln(xb[1], wget("ple_w_gate", i, act), xb[0], i, 2, f"ple_gate_ln{i}", bias=_row(ple_b[i]), pu=pu)
        return dict(xa=xa, gu=gu, act=act, f=f, xb=xb, pg=pg, pu=pu), xc

    def tail_bwd(head, sv, i, mix_in, mix_h):
        xa, xb = sv["xa"], sv["xb"]
        dxb_part, dpg, dpu, dg2, db2, dbg = head(
            _ple_ln_bwd_fn, [xb[0], sv["pg"], sv["pu"]], [gains[i][2]],
            [(sd, F32), (sd, BF16), (sd, BF16)], [((1, sd[1]), F32)] * 3)[:6]
        gs[f"ple_b_{i}"] = dbg
        gs[f"ln_gain_{i}_2"], gs[f"ln_bias_{i}_2"] = dg2, db2
        grad_of("ple_w_gate", i, xb[1], dpg)
        grad_of("ple_w_up", i, p[i], dpu)
        dxa_part, df, _ = mm_ln_bwd(dpg, wget("ple_w_gate", i, None), dxb_part, xa[0], sv["f"], i, 1,
                                    f"ple_gate_dx_ln{i}")
        grad_of("ffn_w_down", i, sv["act"], df)
        gu = sv["gu"]
        dgu, = mm(df, wget("ffn_w_down", i, None), tb=True, name=f"ffn_down_dx_swiglu{i}", tile_cols=gu.shape[1] // 4,
                  post=(_swiglu_bwd_fn, [gu], [], [(gu.shape, BF16)], []))
        grad_of("ffn_w_gate_up", i, xa[1], dgu)
        return mm_ln_bwd(dgu, wget("ffn_w_gate_up", i, None), dxa_part, mix_in, mix_h, i, 0, f"ffn_up_dx_ln{i}")

    def grad_of(nm, i, act, dout):
        grad = mm(act, dout, lb=None, ta=True, out_dtype=BF16, out_layers=1, out_layer=0, name=f"grad_{nm}{i}")
        token = grad_sink(nm, i, grad)
        if token is not None:
            pending[0] = token

    proj = _mm(x, wget("a_w_in", 0, None), lb=0, name="hg_proj")
    o_pre, og, states, scores = _hgrn2_fwd(proj, alb, norm_gain, rb=HG_ROWS)
    h0, x1 = mm_ln(og, wget("a_w_out", 0, og), x, 0, 0, "hg_out_ln")
    sv0, x3 = tail_fwd(x1, 0)
    kv = _mm(x3[1], wget("kv_w", 0, x3[1]), lb=0, bias=_row(kv_b), name="kv_proj")
    q = _mm(x3[1], wget("b_w_q", 0, x3[1]), lb=0, bias=b_q, name="q_proj")
    ao = _swa_fwd(q, kv, sinks)
    h1, x4 = mm_ln(ao, wget("b_w_out", 0, x3[1]), x3[0], 1, 0, "att_out_ln", bias=b_out)
    sv1, y = tail_fwd(x4, 1)

    loss_box = []

    def loss_head(fn, rows, whole, outs, sums):
        def with_loss(yv, tv, *rest):
            dy, part = _loss_fn(yv, tv)
            return fn(dy, *rest) + (part,)

        res = _rowwise(with_loss, [y[0], target] + rows, whole, outs, list(sums) + [((1, LANES), F32)],
                       name="loss_ln_ple_bwd1")
        loss_box.append(res[-1])
        return res

    dx3_part, dh1, dh1sum = tail_bwd(loss_head, sv1, 1, x3[0], h1)
    loss = loss_box[0]
    gs["b_out"] = dh1sum
    grad_of("b_w_out", 0, ao, dh1)
    dao = mm(dh1, wget("b_w_out", 0, None), tb=True, name="att_out_dx")
    dq, dkv_cur, dkv_prev, dqsum, dsinks = _swa_bwd(q, kv, sinks, dao)
    gs["b_q"], gs["sinks"] = dqsum, dsinks
    dkv, dkvsum = _kv_grad_combine(dkv_cur, dkv_prev)
    gs["kv_b"] = dkvsum
    grad_of("b_w_q", 0, x3[1], dq)
    grad_of("kv_w", 0, x3[1], dkv)
    dx3 = mm(dq, wget("b_w_q", 0, None), tb=True, add=dx3_part, name="q_proj_dx")

    def kv_head(*post):
        return mm(dkv, wget("kv_w", 0, None), tb=True, add=dx3, name="kv_proj_dx_ln_ple_bwd0", post=post)

    dx_part, dh0, _ = tail_bwd(kv_head, sv0, 0, x, h0)
    grad_of("a_w_out", 0, og, dh0)
    dog = mm(dh0, wget("a_w_out", 0, None), tb=True, name="hg_out_dx")
    dqr, dfr, dvr, dgr, dalb, dgain = _hgrn2_bwd(proj, o_pre, states, scores, dog, alb, norm_gain, rb=HG_ROWS)
    gs["alb"], gs["norm_gain"] = dalb, dgain
    dproj = jnp.concatenate([dqr, dfr, dvr, dgr], axis=1)
    grad_of("a_w_in", 0, x, dproj)
    grad_x = mm(dproj, wget("a_w_in", 0, None), tb=True, add=dx_part, name="hg_proj_dx")
    return loss, grad_x, gs


HBM_SPEC = pl.BlockSpec(memory_space=pl.ANY)
HBM_ONLY = pl.BlockSpec(memory_space=pltpu.HBM)
SEM_SPEC = pl.BlockSpec(memory_space=pltpu.SEMAPHORE)
SIDE_EFFECT = pltpu.SideEffectType.DATAFLOW_SIDE_EFFECTING


def _slot(kind, j):
    return (j % 2) * 2 + j // 2 if kind == "colp" else j


def _piece(ref, kind, j):
    _, r, c = ref.shape
    if kind == "row":
        return ref.at[:, pl.ds(j * (r // N_CHIPS), r // N_CHIPS), :]
    return ref.at[:, :, pl.ds(_slot(kind, j) * (c // N_CHIPS), c // N_CHIPS)]


def _piece_dyn(ref, kind, j):
    _, r, c = ref.shape
    if kind == "row":
        return ref.at[:, pl.ds(pl.multiple_of(j * (r // N_CHIPS), 16), r // N_CHIPS), :]
    return ref.at[:, :, pl.ds(pl.multiple_of(_slot(kind, j) * (c // N_CHIPS), LANES), c // N_CHIPS)]


def _chip_of(j, c):
    return (j // 2, j % 2, c)


def _in_hbm(a):
    return pltpu.with_memory_space_constraint(a, pltpu.HBM)


def _place(src, kind, chip, *, mode, name, out_dtype, zone=None, zone_shape=None, layer=0, after=None):
    if mode == "gather":
        _, r, c = src.shape
        out_shape = (1, r * N_CHIPS, c) if kind == "row" else (1, r, c * N_CHIPS)
    else:
        out_shape = tuple(zone.shape) if zone is not None else tuple(zone_shape)
        r, c = out_shape[-2:]
    tm = _pick_rows(r, 512)
    nb = r // tm

    def full_idx(i, chip_ref):
        return (0, chip_ref[0] * nb + i, 0) if kind == "row" else (0, i, _slot(kind, chip_ref[0]))

    if mode == "gather":
        in_spec = pl.BlockSpec((None, tm, c), lambda i, chip_ref: (0, i, 0))
        out_spec = pl.BlockSpec((None, tm, c), full_idx)
    else:
        in_spec = pl.BlockSpec((None, tm, c), full_idx)
        out_spec = pl.BlockSpec((None, None, tm, c), lambda i, chip_ref: (chip_ref[0], layer, i, 0))
    in_specs, operands, aliases = [in_spec], [src], {}
    if zone is not None:
        in_specs.append(HBM_SPEC)
        operands.append(zone)
        aliases = {2: 0}
    if after is not None:
        in_specs.append(HBM_SPEC)
        operands.append(after)

    def body(chip_ref, src_ref, *rest):
        rest[-1][...] = src_ref[...].astype(rest[-1].dtype)

    return pl.pallas_call(
        body, name=name,
        grid_spec=pltpu.PrefetchScalarGridSpec(num_scalar_prefetch=1, grid=(nb,), in_specs=in_specs,
                                               out_specs=out_spec),
        out_shape=jax.ShapeDtypeStruct(out_shape, out_dtype),
        input_output_aliases=aliases,
        compiler_params=_params(("arbitrary",)),
    )(chip, *operands)


def _half(ref, c):
    h = ref.shape[1] // 2
    start = c * h if isinstance(c, int) else pl.multiple_of(c * h, 16)
    return ref.at[:, pl.ds(start, h), :]


class _SiblingFill:
    def __init__(self, lands, kinds, name):
        self.kinds, self.name, self.n = kinds, name, len(lands)
        n = self.n
        sem_shape = pltpu.SemaphoreType.DMA((n * N_CHIPS,))

        def body(*refs):
            land_refs, send_sems, recv_sems, token = refs[:n], refs[n], refs[n + 1], refs[-1]
            for cp in self._copies(land_refs, send_sems, recv_sems):
                cp.start()
            token[...] = jnp.zeros(token.shape, token.dtype)

        outs = pl.pallas_call(
            body, name=name + "_start",
            in_specs=[HBM_ONLY] * n,
            out_specs=[SEM_SPEC, SEM_SPEC] + [HBM_ONLY] * n + [pl.BlockSpec(memory_space=pltpu.VMEM)],
            out_shape=[sem_shape, sem_shape] + [pltpu.HBM(a.shape, a.dtype) for a in lands]
                      + [jax.ShapeDtypeStruct((8, LANES), F32)],
            input_output_aliases={i: i + 2 for i in range(n)},
            compiler_params=pltpu.CompilerParams(has_side_effects=SIDE_EFFECT),
        )(*[_in_hbm(a) for a in lands])
        self.send_sems, self.recv_sems, self.lands, self.token = outs[0], outs[1], list(outs[2:2 + n]), outs[-1]

    def _copies(self, land_refs, send_sems, recv_sems):
        x, y, c = lax.axis_index("x"), lax.axis_index("y"), lax.axis_index("c")
        me = 2 * x + y
        copies = []
        for a in range(self.n):
            for k in range(1, N_CHIPS):
                t = (me + k) % N_CHIPS
                slice_t = _piece_dyn(land_refs[a], self.kinds[a], t)
                got = _half(slice_t, c)
                copies.append(pltpu.make_async_remote_copy(
                    src_ref=got, dst_ref=got, send_sem=send_sems.at[a * N_CHIPS + k],
                    recv_sem=recv_sems.at[a * N_CHIPS + k], device_id=(x, y, 1 - c), device_id_type=MESH))
        return copies

    def wait(self, after):
        n = self.n

        def body(*refs):
            land_refs, send_sems, recv_sems = refs[:n], refs[n], refs[n + 1]
            for cp in self._copies(land_refs, send_sems, recv_sems):
                cp.wait_send()
                cp.wait_recv()

        operands = [_in_hbm(a) for a in self.lands] + [self.send_sems, self.recv_sems]
        in_specs = [HBM_ONLY] * n + [SEM_SPEC, SEM_SPEC]
        if after is not None:
            operands.append(after)
            in_specs.append(HBM_SPEC)
        outs = pl.pallas_call(
            body, name=self.name + "_wait",
            in_specs=in_specs, out_specs=[HBM_ONLY] * n,
            out_shape=[pltpu.HBM(a.shape, a.dtype) for a in self.lands],
            input_output_aliases={i: i for i in range(n)},
            compiler_params=pltpu.CompilerParams(has_side_effects=SIDE_EFFECT),
        )(*operands)
        return list(outs)


class _Exchange:
    def __init__(self, mode, srcs, lands, kinds, layers, name, after=None, halves=None):
        self.mode, self.kinds, self.layers, self.name, self.n = mode, kinds, layers, name, len(lands)
        self.halves = halves if halves is not None else [False] * len(lands)
        n, ns = self.n, len(srcs)
        n_in = ns + n + (after is not None)
        sem_shape = pltpu.SemaphoreType.DMA((n * N_CHIPS,))

        def body(*refs):
            src_refs, land_refs = refs[:ns], refs[ns:ns + n]
            send_sems, recv_sems = refs[n_in], refs[n_in + 1]
            token = refs[-1]
            c = lax.axis_index("c")
            me = 2 * lax.axis_index("x") + lax.axis_index("y")
            for j in range(N_CHIPS):
                @pl.when(me == j)
                def _():
                    for a in range(n):
                        for t in range(N_CHIPS):
                            if t != j:
                                src, dst = self._ends(src_refs, land_refs, a, j, t, c)
                                pltpu.make_async_remote_copy(
                                    src_ref=src, dst_ref=dst, send_sem=send_sems.at[a * N_CHIPS + t],
                                    recv_sem=recv_sems.at[a * N_CHIPS + j],
                                    device_id=_chip_of(t, c), device_id_type=MESH).start()
            token[...] = jnp.zeros(token.shape, token.dtype)

        arrays = list(srcs) + list(lands)
        operands = [_in_hbm(a) for a in arrays]
        in_specs = [HBM_ONLY] * (ns + n)
        if after is not None:
            operands.append(after)
            in_specs.append(HBM_SPEC)
        outs = pl.pallas_call(
            body, name=name + "_start",
            in_specs=in_specs,
            out_specs=[SEM_SPEC, SEM_SPEC] + [HBM_ONLY] * (ns + n) + [pl.BlockSpec(memory_space=pltpu.VMEM)],
            out_shape=[sem_shape, sem_shape] + [pltpu.HBM(a.shape, a.dtype) for a in arrays]
                      + [jax.ShapeDtypeStruct((8, LANES), F32)],
            input_output_aliases={i: i + 2 for i in range(ns + n)},
            compiler_params=pltpu.CompilerParams(has_side_effects=SIDE_EFFECT),
        )(*operands)
        self.send_sems, self.recv_sems = outs[0], outs[1]
        self.srcs, self.lands = list(outs[2:2 + ns]), list(outs[2 + ns:2 + ns + n])
        self.token = outs[-1]

    def _ends(self, src_refs, land_refs, a, me_j, peer, c):
        if self.mode == "gather":
            mine = _piece(land_refs[a], self.kinds[a], me_j)
            if self.halves[a]:
                mine = _half(mine, c)
            return mine, mine
        return _piece(src_refs[a], self.kinds[a], peer), land_refs[a].at[me_j, pl.ds(self.layers[a], 1)]

    def wait(self, after, lands=None):
        n, ns = self.n, len(self.srcs)
        lands = self.lands if lands is None else lands

        def body(*refs):
            src_refs, land_refs = refs[:ns], refs[ns:ns + n]
            send_sems, recv_sems = refs[ns + n], refs[ns + n + 1]
            c = lax.axis_index("c")
            me = 2 * lax.axis_index("x") + lax.axis_index("y")
            for j in range(N_CHIPS):
                @pl.when(me != j)
                def _():
                    for a in range(n):
                        sent, _ = self._ends(src_refs, land_refs, a, 0, j, c)
                        _, landed = self._ends(src_refs, land_refs, a, j, 0, c)
                        cp = pltpu.make_async_remote_copy(
                            src_ref=sent, dst_ref=landed, send_sem=send_sems.at[a * N_CHIPS + j],
                            recv_sem=recv_sems.at[a * N_CHIPS + j],
                            device_id=_chip_of(j, c), device_id_type=MESH)
                        cp.wait_send()
                        cp.wait_recv()

        arrays = self.srcs + list(lands)
        operands = [_in_hbm(a) for a in arrays] + [self.send_sems, self.recv_sems]
        in_specs = [HBM_ONLY] * (ns + n) + [SEM_SPEC, SEM_SPEC]
        if after is not None:
            operands.append(after)
            in_specs.append(HBM_SPEC)
        outs = pl.pallas_call(
            body, name=self.name + "_wait",
            in_specs=in_specs, out_specs=[HBM_ONLY] * (ns + n),
            out_shape=[pltpu.HBM(a.shape, a.dtype) for a in arrays],
            input_output_aliases={i: i for i in range(ns + n)},
            compiler_params=pltpu.CompilerParams(has_side_effects=SIDE_EFFECT),
        )(*operands)
        return list(outs[ns:])


def _sibling_swap(arrays, name):
    n = len(arrays)

    def body(*refs):
        ins, outs = refs[:n], refs[n:2 * n]
        send_sems, recv_sems = refs[2 * n:]
        sibling = (lax.axis_index("x"), lax.axis_index("y"), 1 - lax.axis_index("c"))
        copies = [pltpu.make_async_remote_copy(src_ref=ins[a], dst_ref=outs[a], send_sem=send_sems.at[a],
                                               recv_sem=recv_sems.at[a], device_id=sibling, device_id_type=MESH)
                  for a in range(n)]
        for cp in copies:
            cp.start()
        for cp in copies:
            cp.wait()

    return pl.pallas_call(
        body, name=name,
        in_specs=[HBM_SPEC] * n, out_specs=[HBM_SPEC] * n,
        out_shape=[jax.ShapeDtypeStruct(a.shape, a.dtype) for a in arrays],
        scratch_shapes=[pltpu.SemaphoreType.DMA((n,)), pltpu.SemaphoreType.DMA((n,))],
    )(*arrays)


def _gather_devices(vec):
    def body(in_ref, out_ref, send_sems, recv_sems, local_sem):
        x, y, c = lax.axis_index("x"), lax.axis_index("y"), lax.axis_index("c")
        me = 4 * x + 2 * y + c
        mine = pltpu.make_async_copy(in_ref, out_ref.at[me], local_sem)
        mine.start()
        copies = []
        for rel in range(1, N_DEV):
            peer = (x ^ (rel >> 2), y ^ ((rel >> 1) & 1), c ^ (rel & 1))
            copies.append(pltpu.make_async_remote_copy(
                src_ref=in_ref, dst_ref=out_ref.at[me], send_sem=send_sems.at[rel], recv_sem=recv_sems.at[rel],
                device_id=peer, device_id_type=MESH))
        for cp in copies:
            cp.start()
        for cp in copies:
            cp.wait()
        mine.wait()

    return pl.pallas_call(
        body, name="gather_small",
        in_specs=[HBM_SPEC], out_specs=HBM_SPEC,
        out_shape=jax.ShapeDtypeStruct((N_DEV,) + vec.shape, vec.dtype),
        scratch_shapes=[pltpu.SemaphoreType.DMA((N_DEV,)), pltpu.SemaphoreType.DMA((N_DEV,)),
                        pltpu.SemaphoreType.DMA],
    )(vec)


BIG = [("a_w_in", "col"), ("a_w_out", "row"), ("kv_w", "row"), ("b_w_q", "row"), ("b_w_out", "row"),
       ("ffn_w_gate_up", "colp"), ("ffn_w_down", "row"), ("ple_w_up", "col"), ("ple_w_gate", "row")]
GATHER_GROUPS = [[("a_w_in", 0), ("small", 0)], [("a_w_out", 0), ("ffn_w_gate_up", 0)],
                 [("ffn_w_down", 0), ("ple_w_gate", 0), ("ple_w_up", 0)], [("kv_w", 0), ("b_w_q", 0), ("b_w_out", 0)],
                 [("ffn_w_gate_up", 1)], [("ffn_w_down", 1), ("ple_w_gate", 1), ("ple_w_up", 1)]]
SCATTER_GROUPS = [[("ple_w_gate", 1), ("ple_w_up", 1), ("ffn_w_down", 1)], [("ffn_w_gate_up", 1)],
                  [("b_w_out", 0), ("b_w_q", 0), ("kv_w", 0)], [("ple_w_gate", 0), ("ple_w_up", 0), ("ffn_w_down", 0)],
                  [("ffn_w_gate_up", 0), ("a_w_out", 0)], [("a_w_in", 0)]]
SMALL_SHARDED = ["ln_gain", "ln_bias", "a_lower_bound"]
SMALL_REPLICATED = ["a_norm_gain", "kv_b", "b_b_q", "b_sinks", "b_b_out", "ple_b_gate"]
WEIGHT_ORDER = ["a_w_in", "a_lower_bound", "a_norm_gain", "a_w_out", "kv_w", "kv_b", "b_w_q", "b_b_q", "b_sinks",
                "b_w_out", "b_b_out", "ffn_w_gate_up", "ffn_w_down", "ple_w_up", "ple_w_gate", "ple_b_gate",
                "ln_gain", "ln_bias"]


def _as3(a):
    return a.reshape((-1,) + a.shape[-2:]) if a.ndim >= 3 else a.reshape((1,) + a.shape)


def _pad_lanes(v):
    n = v.shape[-1]
    return jnp.pad(v, ((0, 0), (0, (-n) % LANES)))


def _adam_small_fn(w, mom, vel, g):
    return _adam_fn(w, mom, vel, g, jnp.zeros_like(g))[1:]


def _sum_rows_fn(slots):
    acc = slots[0]
    for s in range(1, slots.shape[0]):
        acc = acc + slots[s]
    return (acc,)


def kernel(x, p, a_w_in, a_lower_bound, a_norm_gain, a_w_out, kv_w, kv_b, b_w_q, b_b_q, b_sinks, b_w_out, b_b_out, ffn_w_gate_up, ffn_w_down, ple_w_up, ple_w_gate, ple_b_gate, ln_gain, ln_bias, loss_target, m_a_w_in, m_a_lower_bound, m_a_norm_gain, m_a_w_out, m_kv_w, m_kv_b, m_b_w_q, m_b_b_q, m_b_sinks, m_b_w_out, m_b_b_out, m_ffn_w_gate_up, m_ffn_w_down, m_ple_w_up, m_ple_w_gate, m_ple_b_gate, m_ln_gain, m_ln_bias, v_a_w_in, v_a_lower_bound, v_a_norm_gain, v_a_w_out, v_kv_w, v_kv_b, v_b_w_q, v_b_b_q, v_b_sinks, v_b_w_out, v_b_b_out, v_ffn_w_gate_up, v_ffn_w_down, v_ple_w_up, v_ple_w_gate, v_ple_b_gate, v_ln_gain, v_ln_bias):
    args = dict(locals())
    wts = {n: args[n] for n in WEIGHT_ORDER}
    mom = {n: args["m_" + n] for n in WEIGHT_ORDER}
    vel = {n: args["v_" + n] for n in WEIGHT_ORDER}
    chip = 2 * lax.axis_index("x") + lax.axis_index("y")
    d = x.shape[-1]
    dq = d // N_CHIPS

    kind_of = dict(BIG)
    kind_of["small"] = "col"
    chip_arr = chip.reshape(1).astype(jnp.int32)
    small_pack = jnp.concatenate([wts[n].reshape(-1, dq) for n in SMALL_SHARDED], axis=0)[None]

    def place(key, after):
        n, layer = key
        if n == "small":
            return _place(small_pack, "col", chip_arr, mode="gather", name="place_small", out_dtype=F32, after=after)
        return _place(_as3(wts[n])[layer:layer + 1], kind_of[n], chip_arr, mode="gather",
                      name=f"place_{n}{layer}", out_dtype=BF16, after=after)

    gathers, where = [], {}
    for gi, group in enumerate(GATHER_GROUPS):
        prev = gathers[-1].token if gathers else None
        gathers.append(_Exchange("gather", [], [place(k, prev) for k in group], [kind_of[k[0]] for k in group],
                                 [0] * len(group), f"gather{gi}", after=prev,
                                 halves=[k[0] != "small" for k in group]))
        for k in group:
            where[k] = gi
    all_started = gathers[-1].token
    ready = {}

    fills = {}

    def pass_on(gi, after):
        if gi not in fills:
            group = GATHER_GROUPS[gi]
            outs = gathers[gi].wait(after)
            split = [i for i, k in enumerate(group) if k[0] != "small"]
            fills[gi] = (outs, split, _SiblingFill([outs[i] for i in split], [kind_of[group[i][0]] for i in split],
                                                   f"fill{gi}"))

    def wget(name, layer, after):
        key = (name, layer)
        if key not in ready:
            gi = where[key]
            after = all_started if gi == 0 else after
            pass_on(gi, after)
            if 1 <= gi < len(GATHER_GROUPS) - 1:
                pass_on(gi + 1, after)
            outs, split, fill = fills[gi]
            for i, arr in zip(split, fill.wait(after)):
                outs[i] = arr
            for k, arr in zip(GATHER_GROUPS[gi], outs):
                ready[k] = arr
        return ready[key]

    small_full = wget("small", 0, None)[0]
    ln_gain_f = small_full[0:6].reshape(DEPTH, 3, d)
    ln_bias_f = small_full[6:12].reshape(DEPTH, 3, d)
    alb_f = small_full[12:14]

    group_of = {k: gi for gi, group in enumerate(SCATTER_GROUPS) for k in group}
    grads_done, zones, scatters = {}, {}, []

    def grad_sink(name, layer, grad):
        grads_done[(name, layer)] = grad
        zones[name] = _place(grad, kind_of[name], chip_arr, mode="scatter", name=f"place_grad_{name}{layer}",
                             out_dtype=BF16, zone=zones.get(name), zone_shape=(N_CHIPS,) + _as3(wts[name]).shape,
                             layer=layer)
        gi = group_of[(name, layer)]
        group = SCATTER_GROUPS[gi]
        if not all(k in grads_done for k in group):
            return None
        ex = _Exchange("scatter", [grads_done[k] for k in group], [zones[k[0]] for k in group],
                       [kind_of[k[0]] for k in group], [k[1] for k in group], f"scatter{gi}")
        for k, zone in zip(group, ex.lands):
            zones[k[0]] = zone
        scatters.append((ex, group))
        return ex.token

    loss, grad_x, gs = _local_step(
        x[0], p[:, 0], loss_target[0], wget, grad_sink, ln_gain_f, ln_bias_f, alb_f, a_norm_gain, kv_b, b_b_q,
        b_sinks, b_b_out, ple_b_gate)

    res = {}

    def arrive(batch, after):
        for ex, group in batch:
            outs = ex.wait(after, lands=[zones[k[0]] for k in group])
            for k, zone in zip(group, outs):
                zones[k[0]] = zone

    def update(names, tag):
        partial = []
        for n in names:
            s2 = zones[n].reshape(N_CHIPS, -1, zones[n].shape[-1])
            partial.append(_rowwise(_sum_slots_fn, [s2], [], [(s2.shape[1:], BF16)], name=f"sum_{n}")[0])
        sibling = _sibling_swap(partial, tag)
        for n, own, sib in zip(names, partial, sibling):
            shp = wts[n].shape
            flat = lambda a: a.reshape(-1, shp[-1])
            out = _rowwise(_adam_fn, [flat(wts[n]), flat(mom[n]), flat(vel[n]), own, sib], [],
                           [(own.shape, F32)] * 4, name=f"adam_{n}")
            res[n] = [o.reshape(shp) for o in out]
        return res[names[-1]][1]

    last_names = [k[0] for k in SCATTER_GROUPS[-1]]
    arrive(scatters[:-1], grad_x)
    updated = update([n for n, _ in BIG if n not in last_names], "sibling_swap")
    arrive(scatters[-1:], updated)
    update(last_names, "sibling_swap_last")

    ln_g = jnp.concatenate([gs[f"ln_gain_{i}_{j}"] for i in range(DEPTH) for j in range(3)], axis=0)
    ln_b = jnp.concatenate([gs[f"ln_bias_{i}_{j}"] for i in range(DEPTH) for j in range(3)], axis=0)
    ple_bg = jnp.concatenate([gs[f"ple_b_{i}"] for i in range(DEPTH)], axis=0)
    small_list = [ln_g.reshape(1, -1), ln_b.reshape(1, -1), gs["alb"].reshape(1, -1), gs["norm_gain"],
                  gs["kv_b"], gs["b_q"], _pad_lanes(gs["sinks"]), gs["b_out"], ple_bg.reshape(1, -1), loss]
    small_vec = jnp.concatenate(small_list, axis=1)
    everyone = _gather_devices(small_vec)
    total, = _rowwise(_sum_rows_fn, [everyone], [], [(small_vec.shape, F32)], name="sum_small")
    offs, pos = [], 0
    for v in small_list:
        offs.append((pos, v.shape[1]))
        pos += v.shape[1]

    def seg(k):
        return total[0, offs[k][0]:offs[k][0] + offs[k][1]]

    def my_cols(full, rows):
        return lax.dynamic_slice_in_dim(full.reshape(rows, N_CHIPS, dq), chip, 1, axis=1).reshape(rows, dq)

    n_sink = b_sinks.shape[-1]
    small_grads = {
        "ln_gain": my_cols(seg(0), 6).reshape(ln_gain.shape), "ln_bias": my_cols(seg(1), 6).reshape(ln_bias.shape),
        "a_lower_bound": my_cols(seg(2), 2), "a_norm_gain": seg(3).reshape(a_norm_gain.shape),
        "kv_b": seg(4).reshape(kv_b.shape), "b_b_q": seg(5).reshape(b_b_q.shape),
        "b_sinks": seg(6)[:n_sink].reshape(b_sinks.shape), "b_b_out": seg(7).reshape(b_b_out.shape),
        "ple_b_gate": seg(8).reshape(ple_b_gate.shape)}
    names = SMALL_SHARDED + SMALL_REPLICATED
    pack = lambda dct: _pad_lanes(jnp.concatenate([dct[n].reshape(1, -1) for n in names], axis=1))
    g_pack = pack(small_grads)
    upd = _rowwise(_adam_small_fn, [pack(wts), pack(mom), pack(vel), g_pack], [], [(g_pack.shape, F32)] * 3,
                   name="adam_small")
    pos = 0
    for n in names:
        size = wts[n].size
        res[n] = [small_grads[n]] + [u[0, pos:pos + size].reshape(wts[n].shape) for u in upd]
        pos += size

    outs = [seg(9)[0], grad_x[None]]
    for k in range(4):
        outs += [res[n][k] for n in WEIGHT_ORDER]
    return tuple(outs)
```

```python
import functools

import jax
import jax.numpy as jnp
from jax import lax
from jax.experimental import pallas as pl
from jax.experimental.pallas import tpu as pltpu

F32 = jnp.float32
BF16 = jnp.bfloat16
MESH = pl.DeviceIdType.MESH

LANES = 128
HG_DK = 128
HG_CHUNK = 64
HG_SUB = 16
HG_ROWS = 512
HG_HEADS_PER_STEP = 2
ATT_HD = 64
ATT_G = 4
WINDOW = 128
DEPTH = 2
ALPHA = (2.0 * DEPTH) ** 0.25
LN_EPS = 1e-5
RMS_EPS = 1e-6
ADAM_LR, ADAM_B1, ADAM_B2, ADAM_EPS, ADAM_WD, ADAM_STEP = 0.001, 0.9, 0.999, 1e-08, 0.01, 10
N_CHIPS = 4
N_DEV = 8
VMEM_LIMIT = 56 * 1024 * 1024
NEG = -1e30


def _pick(n, cap):
    best = None
    for d in range(LANES, min(n, cap) + 1, LANES):
        if n % d == 0:
            best = d
    return n if best is None else best


def _pick_rows(m, cap):
    best = None
    for d in range(16, min(m, cap) + 1, 16):
        if m % d == 0:
            best = d
    return m if best is None else best


def _params(sem):
    return pltpu.CompilerParams(dimension_semantics=sem, vmem_limit_bytes=VMEM_LIMIT)


def _zeros_index(ndim, grid_rank=3):
    return (lambda i, j, kk: (0,) * ndim) if grid_rank == 3 else (lambda kk, i: (0,) * ndim)


def _mm(a, b, *, name, la=None, lb=None, ta=False, tb=False, bias=None, add=None, out_dtype=F32,
        out_layers=None, out_layer=None, after=None, post=None, tile_cols=None, caps=(1024, 1536, 2048)):
    ar, ac = a.shape[-2:]
    br, bc = b.shape[-2:]
    m, k = (ac, ar) if ta else (ar, ac)
    k2, n = (bc, br) if tb else (br, bc)
    assert k == k2, (a.shape, b.shape, ta, tb)
    if post is not None:
        caps = (512, n if tile_cols is None else tile_cols, caps[2])
    tm, tn, tk = _pick(m, caps[0]), _pick(n, caps[1]), _pick(k, caps[2])
    assert post is None or tn == caps[1]
    nk = k // tk
    gi, gj = m // tm, n // tn
    a_bytes, b_bytes = m * k * a.dtype.itemsize, k * n * b.dtype.itemsize
    rows_outer = (a_bytes + b_bytes * (gi if gj * nk > 1 else 1)) <= (b_bytes + a_bytes * (gj if gi * nk > 1 else 1))
    k_outer = post is not None and nk > 1 and gj == 1
    grid = (nk, gi) if k_outer else (gi, gj, nk) if rows_outer else (gj, gi, nk)

    def bs(block, idx, late=False):
        if k_outer:
            return pl.BlockSpec(block, lambda kk, i: idx(jnp.where(kk == nk - 1, i, 0) if late else i, 0, kk))
        return pl.BlockSpec(block, idx if rows_outer else (lambda q, p, kk: idx(p, q, kk)))

    def spec(block, idx, layer):
        if layer is None:
            return bs(block, idx)
        return bs((None,) + block, lambda i, j, kk: (layer,) + idx(i, j, kk))

    a_spec = spec((tk, tm), lambda i, j, kk: (kk, i), la) if ta else spec((tm, tk), lambda i, j, kk: (i, kk), la)
    b_spec = spec((tn, tk), lambda i, j, kk: (j, kk), lb) if tb else spec((tk, tn), lambda i, j, kk: (kk, j), lb)
    in_specs, operands = [a_spec, b_spec], [a, b]
    if bias is not None:
        in_specs.append(bs((1, tn), lambda i, j, kk: (0, j)))
        operands.append(bias)
    if add is not None:
        in_specs.append(bs((tm, tn), lambda i, j, kk: (i, j), late=True))
        operands.append(add)
    if after is not None:
        in_specs.append(pl.BlockSpec(memory_space=pl.ANY))
        operands.append(after)
    dims = (((0 if ta else 1,), (1 if tb else 0,)), ((), ()))
    has_bias, has_add = bias is not None, add is not None
    if post is None:
        fn, rows, whole, outs, sums = None, [], [], [], []
        out_shape = jax.ShapeDtypeStruct((m, n) if out_layers is None else (out_layers, m, n), out_dtype)
        out_specs = spec((tm, tn), lambda i, j, kk: (i, j), out_layer)
    else:
        fn, rows, whole, outs, sums = post
        in_specs += [bs((tm, r.shape[-1] // gj), lambda i, j, kk: (i, j), late=True) for r in rows]
        in_specs += [pl.BlockSpec(tuple(w.shape), _zeros_index(w.ndim, len(grid))) for w in whole]
        operands += list(rows) + list(whole)
        out_shape = [jax.ShapeDtypeStruct(sh, dt) for sh, dt in list(outs) + list(sums)]
        out_specs = ([bs((tm, sh[-1] // gj), lambda i, j, kk: (i, j), late=True) for sh, _ in outs]
                     + [pl.BlockSpec(tuple(sh), _zeros_index(len(sh), len(grid))) for sh, _ in sums])
    n_in, n_extra, n_outs, n_sums = len(operands), len(rows) + len(whole), len(outs), len(sums)

    def body(*refs):
        a_ref, b_ref = refs[0], refs[1]
        pos = 2
        bias_ref = add_ref = None
        if has_bias:
            bias_ref = refs[pos]
            pos += 1
        if has_add:
            add_ref = refs[pos]
            pos += 1
        extra_refs = refs[n_in - n_extra:n_in]
        out_refs = refs[n_in:n_in + max(n_outs, 1)]
        sum_refs = refs[n_in + n_outs:n_in + n_outs + n_sums]
        acc_ref = refs[-1] if nk > 1 else None
        part = lax.dot_general(a_ref[...].astype(BF16), b_ref[...].astype(BF16), dims, preferred_element_type=F32)

        def finish(total):
            if has_bias:
                total = total + bias_ref[...]
            if has_add:
                total = total + add_ref[...]
            if fn is None:
                out_refs[0][...] = total.astype(out_refs[0].dtype)
                return
            res = fn(total, *[r[...] for r in extra_refs])
            for ref, val in zip(out_refs, res[:n_outs]):
                ref[...] = val.astype(ref.dtype)
            if n_sums:
                @pl.when(pl.program_id(1 if k_outer or not rows_outer else 0) == 0)
                def _():
                    for ref in sum_refs:
                        ref[...] = jnp.zeros(ref.shape, ref.dtype)

                for ref, val in zip(sum_refs, res[n_outs:]):
                    ref[...] += val

        if nk == 1:
            finish(part)
        elif k_outer:
            kk = pl.program_id(0)
            rows_i = pl.ds(pl.multiple_of(pl.program_id(1) * tm, tm), tm)

            @pl.when(kk == 0)
            def _():
                acc_ref[rows_i, :] = part

            @pl.when(kk > 0)
            def _():
                acc_ref[rows_i, :] += part

            @pl.when(kk == nk - 1)
            def _():
                finish(acc_ref[rows_i, :])
        else:
            kk = pl.program_id(2)

            @pl.when(kk == 0)
            def _():
                acc_ref[...] = part

            @pl.when(kk > 0)
            def _():
                acc_ref[...] += part

            @pl.when(kk == nk - 1)
            def _():
                finish(acc_ref[...])

    return pl.pallas_call(
        body, name=name, grid=grid, in_specs=in_specs, out_specs=out_specs, out_shape=out_shape,
        scratch_shapes=[pltpu.VMEM((m, n) if k_outer else (tm, tn), F32)] if nk > 1 else [],
        compiler_params=_params(("arbitrary", "arbitrary") if k_outer
                                else ("arbitrary" if n_sums else "parallel", "parallel", "arbitrary") if rows_outer
                                else ("parallel", "arbitrary" if n_sums else "parallel", "arbitrary")),
    )(*operands)


def _rowwise(fn, rows, whole, outs, sums=(), *, name, tm=256):
    m = rows[0].shape[-2]
    tm = _pick_rows(m, tm)
    n_rows, n_whole, n_outs, n_sums = len(rows), len(whole), len(outs), len(sums)

    def rspec(shape):
        lead = len(shape) - 2
        return pl.BlockSpec(tuple(shape[:-2]) + (tm, shape[-1]), lambda i: (0,) * lead + (i, 0))

    def wspec(shape):
        return pl.BlockSpec(tuple(shape), lambda i: (0,) * len(shape))

    def body(*refs):
        vals = [r[...] for r in refs[:n_rows + n_whole]]
        out_refs = refs[n_rows + n_whole:n_rows + n_whole + n_outs]
        sum_refs = refs[n_rows + n_whole + n_outs:]
        res = fn(*vals)
        for ref, val in zip(out_refs, res[:n_outs]):
            ref[...] = val.astype(ref.dtype)
        if n_sums:
            @pl.when(pl.program_id(0) == 0)
            def _():
                for ref in sum_refs:
                    ref[...] = jnp.zeros(ref.shape, ref.dtype)

            for ref, val in zip(sum_refs, res[n_outs:]):
                ref[...] += val

    result = pl.pallas_call(
        body, name=name, grid=(m // tm,),
        in_specs=[rspec(r.shape) for r in rows] + [wspec(w.shape) for w in whole],
        out_specs=[rspec(s) for s, _ in outs] + [wspec(s) for s, _ in sums],
        out_shape=[jax.ShapeDtypeStruct(s, d) for s, d in list(outs) + list(sums)],
        compiler_params=_params(("arbitrary",)),
    )(*rows, *whole)
    return result


def _sigmoid(v):
    return jax.nn.sigmoid(v)


def _col_sum(v):
    return jnp.sum(v, axis=0, keepdims=True)


def _ln_stats(z):
    mu = jnp.mean(z, axis=-1, keepdims=True)
    zc = z - mu
    var = jnp.mean(zc * zc, axis=-1, keepdims=True)
    rstd = lax.rsqrt(var + LN_EPS)
    return zc * rstd, rstd


def _ln_fwd_fn(xin, h, gain, bias):
    xhat, _ = _ln_stats(ALPHA * xin + h)
    y = xhat * gain + bias
    return y, y


def _ple_ln_fwd_fn(xin, pg, pu, gain, bias):
    xhat, _ = _ln_stats(ALPHA * xin + _sigmoid(pg) * pu)
    y = xhat * gain + bias
    return y, y


def _ln_dz(dy, z, gain):
    xhat, rstd = _ln_stats(z)
    dxhat = dy * gain
    dz = rstd * (dxhat - jnp.mean(dxhat, axis=-1, keepdims=True)
                 - xhat * jnp.mean(dxhat * xhat, axis=-1, keepdims=True))
    return dz, _col_sum(dy * xhat), _col_sum(dy)


def _ln_bwd_fn(dy, xin, h, gain):
    dz, dgain, dbias = _ln_dz(dy, ALPHA * xin + h, gain)
    return ALPHA * dz, dz, dgain, dbias, _col_sum(dz)


def _ple_ln_bwd_fn(dy, xin, pg, pu, gain):
    sg = _sigmoid(pg)
    dz, dgain, dbias = _ln_dz(dy, ALPHA * xin + sg * pu, gain)
    dpg = dz * pu * sg * (1.0 - sg)
    return ALPHA * dz, dpg, dz * sg, dgain, dbias, _col_sum(dpg)


def _swiglu_fwd_fn(gu):
    hid = gu.shape[-1] // 2
    gate, up = gu[:, :hid], gu[:, hid:]
    return gu, gate * _sigmoid(gate) * up


def _swiglu_bwd_fn(dact, gu):
    gu = gu.astype(F32)
    hid = gu.shape[-1] // 2
    gate, up = gu[:, :hid], gu[:, hid:]
    sg = _sigmoid(gate)
    dgate = dact * up * sg * (1.0 + gate * (1.0 - sg))
    dup = dact * gate * sg
    return (jnp.concatenate([dgate, dup], axis=-1),)


def _loss_fn(y, target):
    err = y - target
    inv = 1.0 / y.shape[-1]
    part = 0.5 * inv * jnp.sum(jnp.sum(err * err, axis=-1, keepdims=True), axis=0, keepdims=True)
    return err * inv, jnp.broadcast_to(part, (1, LANES))


def _adam_fn(w, mom, vel, p_own, p_sib):
    g = p_own.astype(F32) + p_sib.astype(F32)
    m_new = ADAM_B1 * mom + (1.0 - ADAM_B1) * g
    v_new = ADAM_B2 * vel + (1.0 - ADAM_B2) * (g * g)
    m_hat = m_new / (1.0 - ADAM_B1 ** ADAM_STEP)
    v_hat = v_new / (1.0 - ADAM_B2 ** ADAM_STEP)
    delta = -ADAM_LR * (m_hat / (jnp.sqrt(v_hat) + ADAM_EPS) + ADAM_WD * w)
    return g, delta, m_new, v_new


def _split2(x):
    hi = x.astype(BF16)
    return hi, (x - hi.astype(F32)).astype(BF16)


def _dot3(a, b, dims):
    a_hi, a_lo = _split2(a)
    b_hi, b_lo = _split2(b)
    dn = (dims, ((), ()))
    return (lax.dot_general(a_hi, b_hi, dn, preferred_element_type=F32)
            + (lax.dot_general(a_hi, b_lo, dn, preferred_element_type=F32)
               + lax.dot_general(a_lo, b_hi, dn, preferred_element_type=F32)))


def _tdot(mask01, b):
    m = mask01.astype(BF16)
    b_hi = b.astype(BF16)
    rest = b - b_hi.astype(F32)
    b_mid = rest.astype(BF16)
    b_lo = (rest - b_mid.astype(F32)).astype(BF16)
    dn = (((1,), (0,)), ((), ()))
    return (lax.dot_general(m, b_hi, dn, preferred_element_type=F32)
            + (lax.dot_general(m, b_mid, dn, preferred_element_type=F32)
               + lax.dot_general(m, b_lo, dn, preferred_element_type=F32)))


def _hdot(a, b):
    return _dot3(a, b, ((1,), (0,)))


def _hdot_nt(a, b):
    return _dot3(a, b, ((1,), (1,)))


def _hdot_tn(a, b):
    return _dot3(a, b, ((0,), (0,)))


def _dot(a, b):
    return lax.dot_general(a.astype(BF16), b.astype(BF16), (((1,), (0,)), ((), ())), preferred_element_type=F32)


def _dot_nt(a, b):
    return lax.dot_general(a.astype(BF16), b.astype(BF16), (((1,), (1,)), ((), ())), preferred_element_type=F32)


def _dot_tn(a, b):
    return lax.dot_general(a.astype(BF16), b.astype(BF16), (((0,), (0,)), ((), ())), preferred_element_type=F32)


def _hg_masks():
    c = HG_CHUNK
    row = lax.broadcasted_iota(jnp.int32, (c, c), 0)
    col = lax.broadcasted_iota(jnp.int32, (c, c), 1)
    base = row & (-HG_SUB)
    return row, col, base, col <= row, col < base


def _hg_gates(qr, fr, alb):
    lbound = _sigmoid(alb[0:1, :] - alb[1:2, :])
    sig = _sigmoid(fr)
    forget = lbound + (1.0 - lbound) * sig
    kk = (1.0 - lbound) * _sigmoid(-fr)
    qt = qr * _sigmoid(qr) * (HG_DK ** -0.5)
    return qt, kk, jnp.log(forget), lbound, sig, forget


def _hg_scores(qt, kk, g, scores=True):
    c, nsub = HG_CHUNK, HG_CHUNK // HG_SUB
    row, col, base, causal, below = _hg_masks()
    b = _tdot(causal, g)
    rr = _tdot(below, g)
    bq = b - rr
    qh = qt * jnp.exp(bq)
    edecs = [None]
    parts = [jnp.zeros((HG_SUB, c), F32)]
    for i in range(1, nsub):
        edec = jnp.exp(jnp.minimum(rr[i * HG_SUB:i * HG_SUB + 1, :] - b, 0.0))
        edecs.append(edec)
        if scores:
            parts.append(_dot_nt(qh[i * HG_SUB:(i + 1) * HG_SUB, :], kk * edec))
    b3 = b.reshape(nsub, HG_SUB, HG_DK)
    q3 = qt.reshape(nsub, HG_SUB, HG_DK)
    k3 = kk.reshape(nsub, HG_SUB, HG_DK)
    if not scores:
        return None, b, bq, qh, edecs, (b3, q3, k3)
    a = jnp.where(below, jnp.concatenate(parts, axis=0), 0.0)
    for j in range(HG_SUB):
        e = jnp.exp(jnp.minimum(b3 - b3[:, j:j + 1, :], 0.0))
        colv = jnp.sum(q3 * e * k3[:, j:j + 1, :], axis=-1, keepdims=True).reshape(c, 1)
        a = jnp.where(col == base + j, colv, a)
    a = jnp.where(causal, a, 0.0)
    return a, b, bq, qh, edecs, (b3, q3, k3)


def _hg_norm(o, gr, gain):
    r = lax.rsqrt(jnp.mean(o * o, axis=-1, keepdims=True) + RMS_EPS)
    sg = _sigmoid(gr)
    return o * r * gain, r, sg


def _hgrn2_fwd(proj, alb, gain, *, rb):
    m, d4 = proj.shape
    d = d4 // 4
    heads = d // HG_DK
    hp = HG_HEADS_PER_STEP
    rb = min(rb, m)
    cpb = rb // HG_CHUNK
    nrb = m // rb

    def body(q_ref, f_ref, v_ref, g_ref, alb_ref, gain_ref, o_ref, og_ref, st_ref, a_ref, state):
        @pl.when(pl.program_id(1) == 0)
        def _():
            state[...] = jnp.zeros(state.shape, F32)

        def chunk(ci, carry):
            sl = pl.ds(pl.multiple_of(ci * HG_CHUNK, HG_CHUNK), HG_CHUNK)
            for u in range(hp):
                ln = slice(u * HG_DK, (u + 1) * HG_DK)
                qt, kk, g, _, _, _ = _hg_gates(q_ref[sl, ln], f_ref[sl, ln], alb_ref[:, ln])
                v = v_ref[sl, ln]
                st = state[u]
                st_ref[u, ci] = st
                a, b, _, _, _, _ = _hg_scores(qt, kk, g)
                a_ref[u, ci] = a.astype(a_ref.dtype)
                o = _dot(a, v) + _dot_nt(qt * jnp.exp(b), st)
                b_last = b[HG_CHUNK - 1:HG_CHUNK, :]
                state[u] = st * jnp.exp(b_last) + _hdot_tn(v, kk * jnp.exp(b_last - b))
                o_ref[sl, ln] = o
                n, _, sg = _hg_norm(o, g_ref[sl, ln], gain_ref[...])
                og_ref[sl, ln] = (n * g_ref[sl, ln] * sg).astype(og_ref.dtype)
            return carry

        lax.fori_loop(0, cpb, chunk, 0)

    def col(cidx):
        return pl.BlockSpec((rb, hp * HG_DK), lambda h, r: (r, cidx * (heads // hp) + h))

    return pl.pallas_call(
        body, name="hgrn2_fwd", grid=(heads // hp, nrb),
        in_specs=[col(0), col(1), col(2), col(3),
                  pl.BlockSpec((2, hp * HG_DK), lambda h, r: (0, h)),
                  pl.BlockSpec((1, HG_DK), lambda h, r: (0, 0))],
        out_specs=[pl.BlockSpec((rb, hp * HG_DK), lambda h, r: (r, h)),
                   pl.BlockSpec((rb, hp * HG_DK), lambda h, r: (r, h)),
                   pl.BlockSpec((hp, cpb, HG_DK, HG_DK), lambda h, r: (h, r, 0, 0)),
                   pl.BlockSpec((hp, cpb, HG_CHUNK, HG_CHUNK), lambda h, r: (h, r, 0, 0))],
        out_shape=[jax.ShapeDtypeStruct((m, d), F32), jax.ShapeDtypeStruct((m, d), BF16),
                   jax.ShapeDtypeStruct((heads, m // HG_CHUNK, HG_DK, HG_DK), F32),
                   jax.ShapeDtypeStruct((heads, m // HG_CHUNK, HG_CHUNK, HG_CHUNK), BF16)],
        scratch_shapes=[pltpu.VMEM((hp, HG_DK, HG_DK), F32)],
        compiler_params=_params(("parallel", "arbitrary")),
    )(proj, proj, proj, proj, alb, gain)


def _hgrn2_bwd(proj, o_pre, states, scores, dog, alb, gain, *, rb):
    m, d4 = proj.shape
    d = d4 // 4
    heads = d // HG_DK
    rb = min(rb, m)
    cpb = rb // HG_CHUNK
    nrb = m // rb
    c, nsub = HG_CHUNK, HG_CHUNK // HG_SUB

    def body(q_ref, f_ref, v_ref, g_ref, o_ref, st_ref, a_ref, dog_ref, alb_ref, gain_ref,
             dq_ref, df_ref, dv_ref, dg_ref, dalb_ref, dgain_ref, dstate, carry_ref):
        first = (pl.program_id(0) == 0) & (pl.program_id(1) == 0)

        @pl.when(first)
        def _():
            dgain_ref[...] = jnp.zeros(dgain_ref.shape, F32)

        @pl.when(pl.program_id(1) == 0)
        def _():
            dstate[...] = jnp.zeros(dstate.shape, F32)
            carry_ref[...] = jnp.zeros(carry_ref.shape, F32)
            dalb_ref[...] = jnp.zeros(dalb_ref.shape, F32)

        row, col, base, causal, below = _hg_masks()
        sub_iota = lax.broadcasted_iota(jnp.int32, (nsub, HG_SUB, HG_DK), 1)
        row_k = lax.broadcasted_iota(jnp.int32, (c, HG_DK), 0)
        upper = col >= row

        def chunk(step, carry):
            ci = cpb - 1 - step
            sl = pl.ds(pl.multiple_of(ci * HG_CHUNK, HG_CHUNK), HG_CHUNK)
            qr, fr, v, gr = q_ref[sl, :], f_ref[sl, :], v_ref[sl, :], g_ref[sl, :]
            qt, kk, g, lbound, sig, forget = _hg_gates(qr, fr, alb_ref[...])
            o = o_ref[sl, :]
            dogv = dog_ref[sl, :]
            gain_v = gain_ref[...]
            n, r, sg = _hg_norm(o, gr, gain_v)
            dgr = dogv * n * sg * (1.0 + gr * (1.0 - sg))
            dn = dogv * gr * sg
            dgain_ref[...] += _col_sum(dn * o * r)
            u = dn * gain_v
            d_o = r * u - o * (r * r * r) * jnp.mean(u * o, axis=-1, keepdims=True)
            st0 = st_ref[ci]
            dst = dstate[...]
            _, b, bq, qh, edecs, (b3, q3, k3) = _hg_scores(qt, kk, g, scores=False)
            a = a_ref[ci]
            eb = jnp.exp(b)
            b_last = b[c - 1:c, :]
            kdl_dec = jnp.exp(b_last - b)
            kdl = kk * kdl_dec
            d_a = jnp.where(causal, _dot_nt(d_o, v), 0.0)
            d_at = _dot_nt(v, d_o)
            dv = _dot_tn(a, d_o) + _dot_nt(kdl, dst)
            dq = eb * _hdot(d_o, st0)
            dk = _hdot(v, dst) * kdl_dec
            d_a_below = jnp.where(below, d_a, 0.0)
            dq_parts = [jnp.zeros((HG_SUB, HG_DK), F32)]
            for i in range(1, nsub):
                lo, hi = i * HG_SUB, (i + 1) * HG_SUB
                dq_parts.append(_hdot(d_a_below[lo:hi, :], kk * edecs[i]))
                gi = _hdot(d_at[:, lo:hi], qh[lo:hi, :])
                dk = dk + jnp.where(row_k < lo, edecs[i] * gi, 0.0)
            dq = dq + jnp.concatenate(dq_parts, axis=0) * jnp.exp(bq)
            dq3 = jnp.zeros((nsub, HG_SUB, HG_DK), F32)
            dk3 = jnp.zeros((nsub, HG_SUB, HG_DK), F32)
            for j in range(HG_SUB):
                e = jnp.exp(jnp.minimum(b3 - b3[:, j:j + 1, :], 0.0))
                dcol = jnp.sum(jnp.where(col == base + j, d_a, 0.0), axis=-1, keepdims=True)
                t1 = dcol.reshape(nsub, HG_SUB, 1) * e
                dq3 = dq3 + t1 * k3[:, j:j + 1, :]
                dk3 = jnp.where(sub_iota == j, jnp.sum(t1 * q3, axis=1, keepdims=True), dk3)
            dq = dq + dq3.reshape(c, HG_DK)
            dk = dk + dk3.reshape(c, HG_DK)
            dstate[...] = dst * jnp.exp(b_last) + _hdot_tn(d_o, qt * eb)
            dglog = _tdot(upper, qt * dq - kk * dk) + carry_ref[...]
            carry_ref[...] = dglog[0:1, :]
            dforget = dglog / forget
            one_m_lb = 1.0 - lbound
            dsig = (dforget - dk) * one_m_lb
            sneg = _sigmoid(-fr)
            dlb = _col_sum(dforget * (1.0 - sig) - dk * sneg)
            dalb0 = dlb * lbound * one_m_lb
            dalb_ref[...] += jnp.concatenate([dalb0, -dalb0], axis=0)
            sq = _sigmoid(qr)
            dq_ref[sl, :] = (dq * (HG_DK ** -0.5) * sq * (1.0 + qr * (1.0 - sq))).astype(dq_ref.dtype)
            df_ref[sl, :] = (dsig * sig * (1.0 - sig)).astype(df_ref.dtype)
            dv_ref[sl, :] = dv.astype(dv_ref.dtype)
            dg_ref[sl, :] = dgr.astype(dg_ref.dtype)
            return carry

        lax.fori_loop(0, cpb, chunk, 0, unroll=2)

    def rev(r):
        return nrb - 1 - r

    def col(cidx):
        return pl.BlockSpec((rb, HG_DK), lambda h, r: (rev(r), cidx * heads + h))

    def head_rows():
        return pl.BlockSpec((rb, HG_DK), lambda h, r: (rev(r), h))

    return pl.pallas_call(
        body, name="hgrn2_bwd", grid=(heads, nrb),
        in_specs=[col(0), col(1), col(2), col(3), head_rows(),
                  pl.BlockSpec((None, cpb, HG_DK, HG_DK), lambda h, r: (h, rev(r), 0, 0)),
                  pl.BlockSpec((None, cpb, HG_CHUNK, HG_CHUNK), lambda h, r: (h, rev(r), 0, 0)),
                  head_rows(),
                  pl.BlockSpec((2, HG_DK), lambda h, r: (0, h)),
                  pl.BlockSpec((1, HG_DK), lambda h, r: (0, 0))],
        out_specs=[head_rows(), head_rows(), head_rows(), head_rows(),
                   pl.BlockSpec((2, HG_DK), lambda h, r: (0, h)),
                   pl.BlockSpec((1, HG_DK), lambda h, r: (0, 0))],
        out_shape=[jax.ShapeDtypeStruct((m, d), BF16)] * 4
                  + [jax.ShapeDtypeStruct((2, d), F32), jax.ShapeDtypeStruct((1, HG_DK), F32)],
        scratch_shapes=[pltpu.VMEM((HG_DK, HG_DK), F32), pltpu.VMEM((1, HG_DK), F32)],
        compiler_params=_params(("arbitrary", "arbitrary")),
    )(proj, proj, proj, proj, o_pre, states, scores, dog, alb, gain)


def _swa_probs(qh, kp, kc, sink, slope, has_prev):
    rows = qh.shape[0]
    qi = lax.broadcasted_iota(jnp.int32, (rows, WINDOW), 0) & (WINDOW - 1)
    si = lax.broadcasted_iota(jnp.int32, (rows, WINDOW), 1)
    scale = ATT_HD ** -0.5
    dist_c = (qi - si).astype(F32)
    s_p = _dot_nt(qh, kp) * scale - slope * (dist_c + float(WINDOW))
    s_c = _dot_nt(qh, kc) * scale - slope * dist_c
    s_p = jnp.where((si > qi) & has_prev, s_p, NEG)
    s_c = jnp.where(si <= qi, s_c, NEG)
    mx = jnp.maximum(jnp.maximum(jnp.max(s_p, axis=-1, keepdims=True), jnp.max(s_c, axis=-1, keepdims=True)), sink)
    e_p, e_c, e_s = jnp.exp(s_p - mx), jnp.exp(s_c - mx), jnp.exp(sink - mx)
    inv = 1.0 / (jnp.sum(e_p, axis=-1, keepdims=True) + jnp.sum(e_c, axis=-1, keepdims=True) + e_s)
    return e_p * inv, e_c * inv, e_s * inv


def _slope(h, n_heads):
    return float(2.0 ** (-8.0 * (h + 1) / n_heads))


def _swa_group(ref_vals, sink_ref, kh, n_heads):
    heads = [kh * ATT_G + g for g in range(ATT_G)]
    stacked = [jnp.concatenate([v[:, h * ATT_HD:(h + 1) * ATT_HD] for h in heads], axis=0) for v in ref_vals]
    grp = lax.shift_right_logical(lax.broadcasted_iota(jnp.int32, (ATT_G * WINDOW, 1), 0), WINDOW.bit_length() - 1)
    slope = jnp.zeros((ATT_G * WINDOW, 1), F32)
    sink = jnp.zeros((ATT_G * WINDOW, 1), F32)
    for g, h in enumerate(heads):
        slope = jnp.where(grp == g, _slope(h, n_heads), slope)
        sink = jnp.where(grp == g, sink_ref[:, h:h + 1], sink)
    return stacked, slope, sink


def _swa_fwd(q, kv, sinks):
    m, d = q.shape
    n_heads = d // ATT_HD
    kvh = n_heads // ATT_G
    kd = kvh * ATT_HD
    nb = m // WINDOW

    def body(q_ref, kvp_ref, kvc_ref, sink_ref, o_ref):
        has_prev = pl.program_id(0) > 0
        qv, kvp, kvc = q_ref[...], kvp_ref[...], kvc_ref[...]
        outs = []
        for kh in range(kvh):
            ks = slice(kh * ATT_HD, (kh + 1) * ATT_HD)
            vs = slice(kd + kh * ATT_HD, kd + (kh + 1) * ATT_HD)
            (q4,), slope, sink = _swa_group([qv], sink_ref, kh, n_heads)
            p_p, p_c, _ = _swa_probs(q4, kvp[:, ks], kvc[:, ks], sink, slope, has_prev)
            o4 = _dot(p_p, kvp[:, vs]) + _dot(p_c, kvc[:, vs])
            outs += [o4[g * WINDOW:(g + 1) * WINDOW, :] for g in range(ATT_G)]
        o_ref[...] = jnp.concatenate(outs, axis=-1).astype(o_ref.dtype)

    return pl.pallas_call(
        body, name="swa_fwd", grid=(nb,),
        in_specs=[pl.BlockSpec((WINDOW, d), lambda n: (n, 0)),
                  pl.BlockSpec((WINDOW, 2 * kd), lambda n: (jnp.maximum(n - 1, 0), 0)),
                  pl.BlockSpec((WINDOW, 2 * kd), lambda n: (n, 0)),
                  pl.BlockSpec((1, n_heads), lambda n: (0, 0))],
        out_specs=pl.BlockSpec((WINDOW, d), lambda n: (n, 0)),
        out_shape=jax.ShapeDtypeStruct((m, d), BF16),
        compiler_params=_params(("arbitrary",)),
    )(q, kv, kv, sinks)


def _swa_bwd(q, kv, sinks, dao):
    m, d = q.shape
    n_heads = d // ATT_HD
    kvh = n_heads // ATT_G
    kd = kvh * ATT_HD
    nb = m // WINDOW
    scale = ATT_HD ** -0.5

    def body(q_ref, kvp_ref, kvc_ref, sink_ref, do_ref, dq_ref, dkvc_ref, dkvp_ref, dqsum_ref, dsink_ref):
        @pl.when(pl.program_id(0) == 0)
        def _():
            dqsum_ref[...] = jnp.zeros(dqsum_ref.shape, F32)
            dsink_ref[...] = jnp.zeros(dsink_ref.shape, F32)

        has_prev = pl.program_id(0) > 0
        qv, kvp, kvc, dov = q_ref[...], kvp_ref[...], kvc_ref[...], do_ref[...]
        lane_h = lax.broadcasted_iota(jnp.int32, (1, n_heads), 1)
        dsink = jnp.zeros((1, n_heads), F32)
        dq_parts, dk_p, dk_c, dv_p, dv_c = [], [], [], [], []
        for kh in range(kvh):
            ks = slice(kh * ATT_HD, (kh + 1) * ATT_HD)
            vs = slice(kd + kh * ATT_HD, kd + (kh + 1) * ATT_HD)
            kp, kc, vp, vc = kvp[:, ks], kvc[:, ks], kvp[:, vs], kvc[:, vs]
            (q4, do4), slope, sink = _swa_group([qv, dov], sink_ref, kh, n_heads)
            p_p, p_c, p_s = _swa_probs(q4, kp, kc, sink, slope, has_prev)
            dp_p, dp_c = _dot_nt(do4, vp), _dot_nt(do4, vc)
            delta = jnp.sum(p_p * dp_p, axis=-1, keepdims=True) + jnp.sum(p_c * dp_c, axis=-1, keepdims=True)
            ds_p, ds_c = p_p * (dp_p - delta), p_c * (dp_c - delta)
            sink_term = p_s * delta
            dq4 = (_dot(ds_p, kp) + _dot(ds_c, kc)) * scale
            for g in range(ATT_G):
                rows = slice(g * WINDOW, (g + 1) * WINDOW)
                dsink = dsink + jnp.where(lane_h == kh * ATT_G + g, -_col_sum(sink_term[rows, :]), 0.0)
                dq_parts.append(dq4[rows, :])
            dk_p.append(_dot_tn(ds_p, q4) * scale)
            dk_c.append(_dot_tn(ds_c, q4) * scale)
            dv_p.append(_dot_tn(p_p, do4))
            dv_c.append(_dot_tn(p_c, do4))
        dq = jnp.concatenate(dq_parts, axis=-1)
        dq_ref[...] = dq.astype(dq_ref.dtype)
        dqsum_ref[...] += _col_sum(dq)
        dsink_ref[...] += dsink
        dkvc_ref[...] = jnp.concatenate(dk_c + dv_c, axis=-1)
        dkvp_ref[...] = jnp.concatenate(dk_p + dv_p, axis=-1)

    return pl.pallas_call(
        body, name="swa_bwd", grid=(nb,),
        in_specs=[pl.BlockSpec((WINDOW, d), lambda n: (n, 0)),
                  pl.BlockSpec((WINDOW, 2 * kd), lambda n: (jnp.maximum(n - 1, 0), 0)),
                  pl.BlockSpec((WINDOW, 2 * kd), lambda n: (n, 0)),
                  pl.BlockSpec((1, n_heads), lambda n: (0, 0)),
                  pl.BlockSpec((WINDOW, d), lambda n: (n, 0))],
        out_specs=[pl.BlockSpec((WINDOW, d), lambda n: (n, 0)),
                   pl.BlockSpec((WINDOW, 2 * kd), lambda n: (n, 0)),
                   pl.BlockSpec((WINDOW, 2 * kd), lambda n: (n, 0)),
                   pl.BlockSpec((1, d), lambda n: (0, 0)),
                   pl.BlockSpec((1, n_heads), lambda n: (0, 0))],
        out_shape=[jax.ShapeDtypeStruct((m, d), BF16), jax.ShapeDtypeStruct((m, 2 * kd), F32),
                   jax.ShapeDtypeStruct((m, 2 * kd), F32), jax.ShapeDtypeStruct((1, d), F32),
                   jax.ShapeDtypeStruct((1, n_heads), F32)],
        compiler_params=_params(("arbitrary",)),
    )(q, kv, kv, sinks, dao)


def _kv_grad_combine(dkv_cur, dkv_prev):
    m, w = dkv_cur.shape
    nb = m // WINDOW

    def body(cur_ref, nxt_ref, o_ref, sum_ref):
        @pl.when(pl.program_id(0) == 0)
        def _():
            sum_ref[...] = jnp.zeros(sum_ref.shape, F32)

        total = cur_ref[...] + jnp.where(pl.program_id(0) < nb - 1, nxt_ref[...], 0.0)
        o_ref[...] = total.astype(o_ref.dtype)
        sum_ref[...] += _col_sum(total)

    return pl.pallas_call(
        body, name="kv_grad_combine", grid=(nb,),
        in_specs=[pl.BlockSpec((WINDOW, w), lambda n: (n, 0)),
                  pl.BlockSpec((WINDOW, w), lambda n: (jnp.minimum(n + 1, nb - 1), 0))],
        out_specs=[pl.BlockSpec((WINDOW, w), lambda n: (n, 0)), pl.BlockSpec((1, w), lambda n: (0, 0))],
        out_shape=[jax.ShapeDtypeStruct((m, w), BF16), jax.ShapeDtypeStruct((1, w), F32)],
        compiler_params=_params(("arbitrary",)),
    )(dkv_cur, dkv_prev)


def _row(v):
    return v.reshape(1, -1)


def _local_step(x, p, target, wget, grad_sink, ln_gain, ln_bias, alb, norm_gain, kv_b, b_q, sinks, b_out, ple_b):
    gs = {}
    gains = ln_gain.reshape(DEPTH * 3, -1)
    biases = ln_bias.reshape(DEPTH * 3, -1)
    sd = x.shape
    pending = [None]

    def mm(a, b, lb=0, **kw):
        after, pending[0] = pending[0], None
        return _mm(a, b, lb=lb, after=after, **kw)

    def mm_ln(a, wt, xin, i, j, nm, bias=None, pu=None):
        r = 3 * i + j
        if pu is None:
            fn, rows = (lambda h, xv, g, bv: (h,) + _ln_fwd_fn(xv, h, g[r:r + 1], bv[r:r + 1])), [xin]
        else:
            fn = lambda h, xv, puv, g, bv: (h,) + _ple_ln_fwd_fn(xv, h, puv, g[r:r + 1], bv[r:r + 1])
            rows = [xin, pu]
        h, y, yb = _mm(a, wt, lb=0, bias=bias, name=nm,
                       post=(fn, rows, [gains, biases], [(sd, F32), (sd, F32), (sd, BF16)], []))
        return h, (y, yb)

    def mm_ln_bwd(a, wt, add, xin, h, i, j, nm):
        r = 3 * i + j
        dx_part, dh, dg, db, dhsum = mm(a, wt, tb=True, add=add, name=nm,
                                        post=(lambda dy, xv, hv, g: _ln_bwd_fn(dy, xv, hv, g[r:r + 1]), [xin, h],
                                              [gains], [(sd, F32), (sd, BF16)], [((1, sd[1]), F32)] * 3))
        gs[f"ln_gain_{i}_{j}"], gs[f"ln_bias_{i}_{j}"] = dg, db
        return dx_part, dh, dhsum

    def tail_fwd(xa, i):
        wgu = wget("ffn_w_gate_up", i, xa[1])
        hid2 = wgu.shape[-1]
        gu, act = _mm(xa[1], wgu, lb=0, name=f"ffn_up_swiglu{i}", tile_cols=hid2 // 2,
                      post=(_swiglu_fwd_fn, [], [], [((sd[0], hid2), BF16), ((sd[0], hid2 // 2), BF16)], []))
        f, xb = mm_ln(act, wget("ffn_w_down", i, act), xa[0], i, 1, f"ffn_down_ln{i}")
        pu = _mm(p[i], wget("ple_w_up", i, act), lb=0, name=f"ple_up{i}")
        pg, xc = mm_ln(xb[1], wget("ple_w_gate", i, act), xb[0], i, 2, f"ple_gate_ln{i}", bias=_row(ple_b[i]), pu=pu)
        return dict(xa=xa, gu=gu, act=act, f=f, xb=xb, pg=pg, pu=pu), xc

    def tail_bwd(head, sv, i, mix_in, mix_h):
        xa, xb = sv["xa"], sv["xb"]
        r = 3 * i + 2
        dxb_part, dpg, dpu, dg2, db2, dbg = head(
            lambda dy, xv, pgv, puv, g: _ple_ln_bwd_fn(dy, xv, pgv, puv, g[r:r + 1]), [xb[0], sv["pg"], sv["pu"]],
            [gains], [(sd, F32), (sd, BF16), (sd, BF16)], [((1, sd[1]), F32)] * 3)[:6]
        gs[f"ple_b_{i}"] = dbg
        gs[f"ln_gain_{i}_2"], gs[f"ln_bias_{i}_2"] = dg2, db2
        grad_of("ple_w_gate", i, xb[1], dpg)
        grad_of("ple_w_up", i, p[i], dpu)
        dxa_part, df, _ = mm_ln_bwd(dpg, wget("ple_w_gate", i, None), dxb_part, xa[0], sv["f"], i, 1,
                                    f"ple_gate_dx_ln{i}")
        grad_of("ffn_w_down", i, sv["act"], df)
        gu = sv["gu"]
        dgu, = mm(df, wget("ffn_w_down", i, None), tb=True, name=f"ffn_down_dx_swiglu{i}", tile_cols=gu.shape[1] // 4,
                  post=(_swiglu_bwd_fn, [gu], [], [(gu.shape, BF16)], []))
        grad_of("ffn_w_gate_up", i, xa[1], dgu)
        return mm_ln_bwd(dgu, wget("ffn_w_gate_up", i, None), dxa_part, mix_in, mix_h, i, 0, f"ffn_up_dx_ln{i}")

    def grad_of(nm, i, act, dout):
        grad = mm(act, dout, lb=None, ta=True, out_dtype=BF16, out_layers=1, out_layer=0, name=f"grad_{nm}{i}")
        token = grad_sink(nm, i, grad)
        if token is not None:
            pending[0] = token

    proj = _mm(x, wget("a_w_in", 0, None), lb=0, name="hg_proj")
    o_pre, og, states, scores = _hgrn2_fwd(proj, alb, norm_gain, rb=HG_ROWS)
    h0, x1 = mm_ln(og, wget("a_w_out", 0, og), x, 0, 0, "hg_out_ln")
    sv0, x3 = tail_fwd(x1, 0)
    kv = _mm(x3[1], wget("kv_w", 0, x3[1]), lb=0, bias=_row(kv_b), out_dtype=BF16, name="kv_proj")
    q = _mm(x3[1], wget("b_w_q", 0, x3[1]), lb=0, bias=b_q, out_dtype=BF16, name="q_proj")
    ao = _swa_fwd(q, kv, sinks)
    h1, x4 = mm_ln(ao, wget("b_w_out", 0, x3[1]), x3[0], 1, 0, "att_out_ln", bias=b_out)
    sv1, y = tail_fwd(x4, 1)

    loss_box = []

    def loss_head(fn, rows, whole, outs, sums):
        def with_loss(yv, tv, *rest):
            dy, part = _loss_fn(yv, tv)
            return fn(dy, *rest) + (part,)

        res = _rowwise(with_loss, [y[0], target] + rows, whole, outs, list(sums) + [((1, LANES), F32)],
                       name="loss_ln_ple_bwd1")
        loss_box.append(res[-1])
        return res

    dx3_part, dh1, dh1sum = tail_bwd(loss_head, sv1, 1, x3[0], h1)
    loss = loss_box[0]
    gs["b_out"] = dh1sum
    grad_of("b_w_out", 0, ao, dh1)
    dao = mm(dh1, wget("b_w_out", 0, None), tb=True, out_dtype=BF16, name="att_out_dx")
    dq, dkv_cur, dkv_prev, dqsum, dsinks = _swa_bwd(q, kv, sinks, dao)
    gs["b_q"], gs["sinks"] = dqsum, dsinks
    dkv, dkvsum = _kv_grad_combine(dkv_cur, dkv_prev)
    gs["kv_b"] = dkvsum
    grad_of("b_w_q", 0, x3[1], dq)
    grad_of("kv_w", 0, x3[1], dkv)
    dx3 = mm(dq, wget("b_w_q", 0, None), tb=True, add=dx3_part, name="q_proj_dx")

    def kv_head(*post):
        return mm(dkv, wget("kv_w", 0, None), tb=True, add=dx3, name="kv_proj_dx_ln_ple_bwd0", post=post)

    dx_part, dh0, _ = tail_bwd(kv_head, sv0, 0, x, h0)
    grad_of("a_w_out", 0, og, dh0)
    dog = mm(dh0, wget("a_w_out", 0, None), tb=True, name="hg_out_dx")
    dqr, dfr, dvr, dgr, dalb, dgain = _hgrn2_bwd(proj, o_pre, states, scores, dog, alb, norm_gain, rb=HG_ROWS)
    gs["alb"], gs["norm_gain"] = dalb, dgain
    dproj = jnp.concatenate([dqr, dfr, dvr, dgr], axis=1)
    grad_of("a_w_in", 0, x, dproj)
    grad_x = mm(dproj, wget("a_w_in", 0, None), tb=True, add=dx_part, name="hg_proj_dx")
    return loss, grad_x, gs


HBM_SPEC = pl.BlockSpec(memory_space=pl.ANY)
HBM_ONLY = pl.BlockSpec(memory_space=pltpu.HBM)
SEM_SPEC = pl.BlockSpec(memory_space=pltpu.SEMAPHORE)
SIDE_EFFECT = pltpu.SideEffectType.DATAFLOW_SIDE_EFFECTING


def _slot(kind, j):
    return (j % 2) * 2 + j // 2 if kind == "colp" else j


def _piece(ref, kind, j):
    _, r, c = ref.shape
    if kind == "row":
        return ref.at[:, pl.ds(j * (r // N_CHIPS), r // N_CHIPS), :]
    return ref.at[:, :, pl.ds(_slot(kind, j) * (c // N_CHIPS), c // N_CHIPS)]


def _piece_dyn(ref, kind, j):
    _, r, c = ref.shape
    if kind == "row":
        return ref.at[:, pl.ds(pl.multiple_of(j * (r // N_CHIPS), 16), r // N_CHIPS), :]
    return ref.at[:, :, pl.ds(pl.multiple_of(_slot(kind, j) * (c // N_CHIPS), LANES), c // N_CHIPS)]


def _chip_of(j, c):
    return (j // 2, j % 2, c)


def _in_hbm(a):
    return pltpu.with_memory_space_constraint(a, pltpu.HBM)


def _place(src, kind, chip, *, mode, name, out_dtype, zone=None, zone_shape=None, layer=0, after=None):
    if mode == "gather":
        _, r, c = src.shape
        out_shape = (1, r * N_CHIPS, c) if kind == "row" else (1, r, c * N_CHIPS)
    else:
        out_shape = tuple(zone.shape) if zone is not None else tuple(zone_shape)
        r, c = out_shape[-2:]
    tm = _pick_rows(r, 512)
    nb = r // tm

    def full_idx(i, chip_ref):
        return (0, chip_ref[0] * nb + i, 0) if kind == "row" else (0, i, _slot(kind, chip_ref[0]))

    if mode == "gather":
        in_spec = pl.BlockSpec((None, tm, c), lambda i, chip_ref: (0, i, 0))
        out_spec = pl.BlockSpec((None, tm, c), full_idx)
    else:
        in_spec = pl.BlockSpec((None, tm, c), full_idx)
        out_spec = pl.BlockSpec((None, None, tm, c), lambda i, chip_ref: (chip_ref[0], layer, i, 0))
    in_specs, operands, aliases = [in_spec], [src], {}
    if zone is not None:
        in_specs.append(HBM_SPEC)
        operands.append(zone)
        aliases = {2: 0}
    if after is not None:
        in_specs.append(HBM_SPEC)
        operands.append(after)

    def body(chip_ref, src_ref, *rest):
        rest[-1][...] = src_ref[...].astype(rest[-1].dtype)

    return pl.pallas_call(
        body, name=name,
        grid_spec=pltpu.PrefetchScalarGridSpec(num_scalar_prefetch=1, grid=(nb,), in_specs=in_specs,
                                               out_specs=out_spec),
        out_shape=jax.ShapeDtypeStruct(out_shape, out_dtype),
        input_output_aliases=aliases,
        compiler_params=_params(("arbitrary",)),
    )(chip, *operands)


def _half(ref, c):
    h = ref.shape[1] // 2
    start = c * h if isinstance(c, int) else pl.multiple_of(c * h, 16)
    return ref.at[:, pl.ds(start, h), :]


class _SiblingFill:
    def __init__(self, lands, kinds, name):
        self.kinds, self.name, self.n = kinds, name, len(lands)
        n = self.n
        sem_shape = pltpu.SemaphoreType.DMA((n * N_CHIPS,))

        def body(*refs):
            land_refs, send_sems, recv_sems, token = refs[:n], refs[n], refs[n + 1], refs[-1]
            for cp in self._copies(land_refs, send_sems, recv_sems):
                cp.start()
            token[...] = jnp.zeros(token.shape, token.dtype)

        outs = pl.pallas_call(
            body, name=name + "_start",
            in_specs=[HBM_ONLY] * n,
            out_specs=[SEM_SPEC, SEM_SPEC] + [HBM_ONLY] * n + [pl.BlockSpec(memory_space=pltpu.VMEM)],
            out_shape=[sem_shape, sem_shape] + [pltpu.HBM(a.shape, a.dtype) for a in lands]
                      + [jax.ShapeDtypeStruct((8, LANES), F32)],
            input_output_aliases={i: i + 2 for i in range(n)},
            compiler_params=pltpu.CompilerParams(has_side_effects=SIDE_EFFECT),
        )(*[_in_hbm(a) for a in lands])
        self.send_sems, self.recv_sems, self.lands, self.token = outs[0], outs[1], list(outs[2:2 + n]), outs[-1]

    def _copies(self, land_refs, send_sems, recv_sems):
        x, y, c = lax.axis_index("x"), lax.axis_index("y"), lax.axis_index("c")
        me = 2 * x + y
        copies = []
        for a in range(self.n):
            for k in range(1, N_CHIPS):
                t = (me + k) % N_CHIPS
                slice_t = _piece_dyn(land_refs[a], self.kinds[a], t)
                got = _half(slice_t, c)
                copies.append(pltpu.make_async_remote_copy(
                    src_ref=got, dst_ref=got, send_sem=send_sems.at[a * N_CHIPS + k],
                    recv_sem=recv_sems.at[a * N_CHIPS + k], device_id=(x, y, 1 - c), device_id_type=MESH))
        return copies

    def wait(self, after):
        n = self.n

        def body(*refs):
            land_refs, send_sems, recv_sems = refs[:n], refs[n], refs[n + 1]
            for cp in self._copies(land_refs, send_sems, recv_sems):
                cp.wait_send()
                cp.wait_recv()

        operands = [_in_hbm(a) for a in self.lands] + [self.send_sems, self.recv_sems]
        in_specs = [HBM_ONLY] * n + [SEM_SPEC, SEM_SPEC]
        if after is not None:
            operands.append(after)
            in_specs.append(HBM_SPEC)
        outs = pl.pallas_call(
            body, name=self.name + "_wait",
            in_specs=in_specs, out_specs=[HBM_ONLY] * n,
            out_shape=[pltpu.HBM(a.shape, a.dtype) for a in self.lands],
            input_output_aliases={i: i for i in range(n)},
            compiler_params=pltpu.CompilerParams(has_side_effects=SIDE_EFFECT),
        )(*operands)
        return list(outs)


class _Exchange:
    def __init__(self, mode, srcs, lands, kinds, layers, name, after=None, halves=None):
        self.mode, self.kinds, self.layers, self.name, self.n = mode, kinds, layers, name, len(lands)
        self.halves = halves if halves is not None else [False] * len(lands)
        n, ns = self.n, len(srcs)
        n_in = ns + n + (after is not None)
        sem_shape = pltpu.SemaphoreType.DMA((n * N_CHIPS,))

        def body(*refs):
            src_refs, land_refs = refs[:ns], refs[ns:ns + n]
            send_sems, recv_sems = refs[n_in], refs[n_in + 1]
            token = refs[-1]
            c = lax.axis_index("c")
            me = 2 * lax.axis_index("x") + lax.axis_index("y")
            for j in range(N_CHIPS):
                @pl.when(me == j)
                def _():
                    for a in range(n):
                        for t in range(N_CHIPS):
                            if t != j:
                                src, dst = self._ends(src_refs, land_refs, a, j, t, c)
                                pltpu.make_async_remote_copy(
                                    src_ref=src, dst_ref=dst, send_sem=send_sems.at[a * N_CHIPS + t],
                                    recv_sem=recv_sems.at[a * N_CHIPS + j],
                                    device_id=_chip_of(t, c), device_id_type=MESH).start()
            token[...] = jnp.zeros(token.shape, token.dtype)

        arrays = list(srcs) + list(lands)
        operands = [_in_hbm(a) for a in arrays]
        in_specs = [HBM_ONLY] * (ns + n)
        if after is not None:
            operands.append(after)
            in_specs.append(HBM_SPEC)
        outs = pl.pallas_call(
            body, name=name + "_start",
            in_specs=in_specs,
            out_specs=[SEM_SPEC, SEM_SPEC] + [HBM_ONLY] * (ns + n) + [pl.BlockSpec(memory_space=pltpu.VMEM)],
            out_shape=[sem_shape, sem_shape] + [pltpu.HBM(a.shape, a.dtype) for a in arrays]
                      + [jax.ShapeDtypeStruct((8, LANES), F32)],
            input_output_aliases={i: i + 2 for i in range(ns + n)},
            compiler_params=pltpu.CompilerParams(has_side_effects=SIDE_EFFECT),
        )(*operands)
        self.send_sems, self.recv_sems = outs[0], outs[1]
        self.srcs, self.lands = list(outs[2:2 + ns]), list(outs[2 + ns:2 + ns + n])
        self.token = outs[-1]

    def _ends(self, src_refs, land_refs, a, me_j, peer, c):
        if self.mode == "gather":
            mine = _piece(land_refs[a], self.kinds[a], me_j)
            if self.halves[a]:
                mine = _half(mine, c)
            return mine, mine
        return _piece(src_refs[a], self.kinds[a], peer), land_refs[a].at[me_j, pl.ds(self.layers[a], 1)]

    def wait(self, after, lands=None):
        n, ns = self.n, len(self.srcs)
        lands = self.lands if lands is None else lands

        def body(*refs):
            src_refs, land_refs = refs[:ns], refs[ns:ns + n]
            send_sems, recv_sems = refs[ns + n], refs[ns + n + 1]
            c = lax.axis_index("c")
            me = 2 * lax.axis_index("x") + lax.axis_index("y")
            for j in range(N_CHIPS):
                @pl.when(me != j)
                def _():
                    for a in range(n):
                        sent, _ = self._ends(src_refs, land_refs, a, 0, j, c)
                        _, landed = self._ends(src_refs, land_refs, a, j, 0, c)
                        cp = pltpu.make_async_remote_copy(
                            src_ref=sent, dst_ref=landed, send_sem=send_sems.at[a * N_CHIPS + j],
                            recv_sem=recv_sems.at[a * N_CHIPS + j],
                            device_id=_chip_of(j, c), device_id_type=MESH)
                        cp.wait_send()
                        cp.wait_recv()

        arrays = self.srcs + list(lands)
        operands = [_in_hbm(a) for a in arrays] + [self.send_sems, self.recv_sems]
        in_specs = [HBM_ONLY] * (ns + n) + [SEM_SPEC, SEM_SPEC]
        if after is not None:
            operands.append(after)
            in_specs.append(HBM_SPEC)
        outs = pl.pallas_call(
            body, name=self.name + "_wait",
            in_specs=in_specs, out_specs=[HBM_ONLY] * (ns + n),
            out_shape=[pltpu.HBM(a.shape, a.dtype) for a in arrays],
            input_output_aliases={i: i for i in range(ns + n)},
            compiler_params=pltpu.CompilerParams(has_side_effects=SIDE_EFFECT),
        )(*operands)
        return list(outs[:ns]), list(outs[ns:])


def _sum_arrivals(zone, own_grads, kind, chip, name):
    _, layers, r, c = zone.shape
    tm = _pick_rows(r, 256)
    nb = r // tm

    def own_idx(l, i, chip_ref):
        return (0, chip_ref[0] * nb + i, 0) if kind == "row" else (0, i, _slot(kind, chip_ref[0]))

    def slot_idx(k):
        return lambda l, i, chip_ref: (jnp.where(chip_ref[0] == k, (k + 1) % N_CHIPS, k), l, i, 0)

    in_specs = [pl.BlockSpec((None, None, tm, c), slot_idx(k)) for k in range(N_CHIPS)]
    in_specs += [pl.BlockSpec((None, tm, c), own_idx) for _ in own_grads]

    def body(chip_ref, *refs):
        slot_refs, own_refs, o_ref = refs[:N_CHIPS], refs[N_CHIPS:N_CHIPS + layers], refs[-1]
        own = own_refs[0][...]
        for u in range(1, layers):
            own = jnp.where(pl.program_id(0) == u, own_refs[u][...], own)
        acc = None
        for k in range(N_CHIPS):
            term = jnp.where(chip_ref[0] == k, own, slot_refs[k][...]).astype(F32)
            acc = term if acc is None else acc + term
        o_ref[...] = acc.astype(o_ref.dtype)

    return pl.pallas_call(
        body, name=name,
        grid_spec=pltpu.PrefetchScalarGridSpec(
            num_scalar_prefetch=1, grid=(layers, nb), in_specs=in_specs,
            out_specs=pl.BlockSpec((tm, c), lambda l, i, chip_ref: (l * nb + i, 0))),
        out_shape=jax.ShapeDtypeStruct((layers * r, c), BF16),
        compiler_params=_params(("arbitrary", "arbitrary")),
    )(chip, zone, zone, zone, zone, *own_grads)


def _sibling_swap(arrays, name):
    n = len(arrays)

    def body(*refs):
        ins, outs = refs[:n], refs[n:2 * n]
        send_sems, recv_sems = refs[2 * n:]
        sibling = (lax.axis_index("x"), lax.axis_index("y"), 1 - lax.axis_index("c"))
        copies = [pltpu.make_async_remote_copy(src_ref=ins[a], dst_ref=outs[a], send_sem=send_sems.at[a],
                                               recv_sem=recv_sems.at[a], device_id=sibling, device_id_type=MESH)
                  for a in range(n)]
        for cp in copies:
            cp.start()
        for cp in copies:
            cp.wait()

    return pl.pallas_call(
        body, name=name,
        in_specs=[HBM_SPEC] * n, out_specs=[HBM_SPEC] * n,
        out_shape=[jax.ShapeDtypeStruct(a.shape, a.dtype) for a in arrays],
        scratch_shapes=[pltpu.SemaphoreType.DMA((n,)), pltpu.SemaphoreType.DMA((n,))],
    )(*arrays)


def _gather_devices(vec):
    def body(in_ref, out_ref, send_sems, recv_sems, local_sem):
        x, y, c = lax.axis_index("x"), lax.axis_index("y"), lax.axis_index("c")
        me = 4 * x + 2 * y + c
        mine = pltpu.make_async_copy(in_ref, out_ref.at[me], local_sem)
        mine.start()
        copies = []
        for rel in range(1, N_DEV):
            peer = (x ^ (rel >> 2), y ^ ((rel >> 1) & 1), c ^ (rel & 1))
            copies.append(pltpu.make_async_remote_copy(
                src_ref=in_ref, dst_ref=out_ref.at[me], send_sem=send_sems.at[rel], recv_sem=recv_sems.at[rel],
                device_id=peer, device_id_type=MESH))
        for cp in copies:
            cp.start()
        for cp in copies:
            cp.wait()
        mine.wait()

    return pl.pallas_call(
        body, name="gather_small",
        in_specs=[HBM_SPEC], out_specs=HBM_SPEC,
        out_shape=jax.ShapeDtypeStruct((N_DEV,) + vec.shape, vec.dtype),
        scratch_shapes=[pltpu.SemaphoreType.DMA((N_DEV,)), pltpu.SemaphoreType.DMA((N_DEV,)),
                        pltpu.SemaphoreType.DMA],
    )(vec)


BIG = [("a_w_in", "col"), ("a_w_out", "row"), ("kv_w", "row"), ("b_w_q", "row"), ("b_w_out", "row"),
       ("ffn_w_gate_up", "colp"), ("ffn_w_down", "row"), ("ple_w_up", "col"), ("ple_w_gate", "row")]
GATHER_GROUPS = [[("a_w_in", 0), ("small", 0)], [("a_w_out", 0), ("ffn_w_gate_up", 0)],
                 [("ffn_w_down", 0), ("ple_w_gate", 0), ("ple_w_up", 0)], [("kv_w", 0), ("b_w_q", 0), ("b_w_out", 0)],
                 [("ffn_w_gate_up", 1)], [("ffn_w_down", 1), ("ple_w_gate", 1), ("ple_w_up", 1)]]
SCATTER_GROUPS = [[("ple_w_gate", 1), ("ple_w_up", 1), ("ffn_w_down", 1)], [("ffn_w_gate_up", 1)],
                  [("b_w_out", 0), ("b_w_q", 0), ("kv_w", 0)], [("ple_w_gate", 0), ("ple_w_up", 0), ("ffn_w_down", 0)],
                  [("ffn_w_gate_up", 0), ("a_w_out", 0)], [("a_w_in", 0)]]
SMALL_SHARDED = ["ln_gain", "ln_bias", "a_lower_bound"]
SMALL_REPLICATED = ["a_norm_gain", "kv_b", "b_b_q", "b_sinks", "b_b_out", "ple_b_gate"]
WEIGHT_ORDER = ["a_w_in", "a_lower_bound", "a_norm_gain", "a_w_out", "kv_w", "kv_b", "b_w_q", "b_b_q", "b_sinks",
                "b_w_out", "b_b_out", "ffn_w_gate_up", "ffn_w_down", "ple_w_up", "ple_w_gate", "ple_b_gate",
                "ln_gain", "ln_bias"]


def _as3(a):
    return a.reshape((-1,) + a.shape[-2:]) if a.ndim >= 3 else a.reshape((1,) + a.shape)


def _pad_lanes(v):
    n = v.shape[-1]
    return jnp.pad(v, ((0, 0), (0, (-n) % LANES)))


def _adam_small_fn(w, mom, vel, g):
    return _adam_fn(w, mom, vel, g, jnp.zeros_like(g))[1:]


def _sum_rows_fn(slots):
    acc = slots[0]
    for s in range(1, slots.shape[0]):
        acc = acc + slots[s]
    return (acc,)


def kernel(x, p, a_w_in, a_lower_bound, a_norm_gain, a_w_out, kv_w, kv_b, b_w_q, b_b_q, b_sinks, b_w_out, b_b_out, ffn_w_gate_up, ffn_w_down, ple_w_up, ple_w_gate, ple_b_gate, ln_gain, ln_bias, loss_target, m_a_w_in, m_a_lower_bound, m_a_norm_gain, m_a_w_out, m_kv_w, m_kv_b, m_b_w_q, m_b_b_q, m_b_sinks, m_b_w_out, m_b_b_out, m_ffn_w_gate_up, m_ffn_w_down, m_ple_w_up, m_ple_w_gate, m_ple_b_gate, m_ln_gain, m_ln_bias, v_a_w_in, v_a_lower_bound, v_a_norm_gain, v_a_w_out, v_kv_w, v_kv_b, v_b_w_q, v_b_b_q, v_b_sinks, v_b_w_out, v_b_b_out, v_ffn_w_gate_up, v_ffn_w_down, v_ple_w_up, v_ple_w_gate, v_ple_b_gate, v_ln_gain, v_ln_bias):
    args = dict(locals())
    wts = {n: args[n] for n in WEIGHT_ORDER}
    mom = {n: args["m_" + n] for n in WEIGHT_ORDER}
    vel = {n: args["v_" + n] for n in WEIGHT_ORDER}
    chip = 2 * lax.axis_index("x") + lax.axis_index("y")
    d = x.shape[-1]
    dq = d // N_CHIPS

    kind_of = dict(BIG)
    kind_of["small"] = "col"
    chip_arr = chip.reshape(1).astype(jnp.int32)
    small_pack = jnp.concatenate([wts[n].reshape(-1, dq) for n in SMALL_SHARDED], axis=0)[None]

    def place(key, after):
        n, layer = key
        if n == "small":
            return _place(small_pack, "col", chip_arr, mode="gather", name="place_small", out_dtype=F32, after=after)
        return _place(_as3(wts[n])[layer:layer + 1], kind_of[n], chip_arr, mode="gather",
                      name=f"place_{n}{layer}", out_dtype=BF16, after=after)

    gathers, where = [], {}
    for gi, group in enumerate(GATHER_GROUPS):
        prev = gathers[-1].token if gathers else None
        gathers.append(_Exchange("gather", [], [place(k, prev) for k in group], [kind_of[k[0]] for k in group],
                                 [0] * len(group), f"gather{gi}", after=prev,
                                 halves=[k[0] != "small" for k in group]))
        for k in group:
            where[k] = gi
    all_started = gathers[-1].token
    ready = {}

    fills = {}

    def pass_on(gi, after):
        if gi not in fills:
            group = GATHER_GROUPS[gi]
            outs = gathers[gi].wait(after)[1]
            split = [i for i, k in enumerate(group) if k[0] != "small"]
            fills[gi] = (outs, split, _SiblingFill([outs[i] for i in split], [kind_of[group[i][0]] for i in split],
                                                   f"fill{gi}"))

    def wget(name, layer, after):
        key = (name, layer)
        if key not in ready:
            gi = where[key]
            after = all_started if gi == 0 else after
            pass_on(gi, after)
            if 1 <= gi < len(GATHER_GROUPS) - 1:
                pass_on(gi + 1, after)
            outs, split, fill = fills[gi]
            for i, arr in zip(split, fill.wait(after)):
                outs[i] = arr
            for k, arr in zip(GATHER_GROUPS[gi], outs):
                ready[k] = arr
        return ready[key]

    small_full = wget("small", 0, None)[0]
    ln_gain_f = small_full[0:6].reshape(DEPTH, 3, d)
    ln_bias_f = small_full[6:12].reshape(DEPTH, 3, d)
    alb_f = small_full[12:14]

    group_of = {k: gi for gi, group in enumerate(SCATTER_GROUPS) for k in group}
    grads_done, zones, scatters = {}, {}, []

    def grad_sink(name, layer, grad):
        grads_done[(name, layer)] = grad
        if name not in zones:
            zones[name] = lax.empty((N_CHIPS,) + _as3(wts[name]).shape, BF16)
        gi = group_of[(name, layer)]
        group = SCATTER_GROUPS[gi]
        if not all(k in grads_done for k in group):
            return None
        ex = _Exchange("scatter", [grads_done[k] for k in group], [zones[k[0]] for k in group],
                       [kind_of[k[0]] for k in group], [k[1] for k in group], f"scatter{gi}")
        for k, zone in zip(group, ex.lands):
            zones[k[0]] = zone
        scatters.append((ex, group))
        return ex.token

    loss, grad_x, gs = _local_step(
        x[0], p[:, 0], loss_target[0], wget, grad_sink, ln_gain_f, ln_bias_f, alb_f, a_norm_gain, kv_b, b_b_q,
        b_sinks, b_b_out, ple_b_gate)

    res = {}

    def arrive(batch, after):
        for ex, group in batch:
            srcs, outs = ex.wait(after, lands=[zones[k[0]] for k in group])
            for k, grad, zone in zip(group, srcs, outs):
                grads_done[k], zones[k[0]] = grad, zone

    def update(names, tag):
        partial = []
        for n in names:
            own = [grads_done[(n, layer)] for layer in range(zones[n].shape[1])]
            partial.append(_sum_arrivals(zones[n], own, kind_of[n], chip_arr, f"sum_{n}"))
        sibling = _sibling_swap(partial, tag)
        for n, own, sib in zip(names, partial, sibling):
            shp = wts[n].shape
            flat = lambda a: a.reshape(-1, shp[-1])
            out = _rowwise(_adam_fn, [flat(wts[n]), flat(mom[n]), flat(vel[n]), own, sib], [],
                           [(own.shape, F32)] * 4, name=f"adam_{n}")
            res[n] = [o.reshape(shp) for o in out]
        return res[names[-1]][1]

    last_names = [k[0] for k in SCATTER_GROUPS[-1]]
    arrive(scatters[:-1], grad_x)
    updated = update([n for n, _ in BIG if n not in last_names], "sibling_swap")
    arrive(scatters[-1:], updated)
    update(last_names, "sibling_swap_last")

    ln_g = jnp.concatenate([gs[f"ln_gain_{i}_{j}"] for i in range(DEPTH) for j in range(3)], axis=0)
    ln_b = jnp.concatenate([gs[f"ln_bias_{i}_{j}"] for i in range(DEPTH) for j in range(3)], axis=0)
    ple_bg = jnp.concatenate([gs[f"ple_b_{i}"] for i in range(DEPTH)], axis=0)
    small_list = [ln_g.reshape(1, -1), ln_b.reshape(1, -1), gs["alb"].reshape(1, -1), gs["norm_gain"],
                  gs["kv_b"], gs["b_q"], _pad_lanes(gs["sinks"]), gs["b_out"], ple_bg.reshape(1, -1), loss]
    small_vec = jnp.concatenate(small_list, axis=1)
    everyone = _gather_devices(small_vec)
    total, = _rowwise(_sum_rows_fn, [everyone], [], [(small_vec.shape, F32)], name="sum_small")
    offs, pos = [], 0
    for v in small_list:
        offs.append((pos, v.shape[1]))
        pos += v.shape[1]

    def seg(k):
        return total[0, offs[k][0]:offs[k][0] + offs[k][1]]

    def my_cols(full, rows):
        return lax.dynamic_slice_in_dim(full.reshape(rows, N_CHIPS, dq), chip, 1, axis=1).reshape(rows, dq)

    n_sink = b_sinks.shape[-1]
    small_grads = {
        "ln_gain": my_cols(seg(0), 6).reshape(ln_gain.shape), "ln_bias": my_cols(seg(1), 6).reshape(ln_bias.shape),
        "a_lower_bound": my_cols(seg(2), 2), "a_norm_gain": seg(3).reshape(a_norm_gain.shape),
        "kv_b": seg(4).reshape(kv_b.shape), "b_b_q": seg(5).reshape(b_b_q.shape),
        "b_sinks": seg(6)[:n_sink].reshape(b_sinks.shape), "b_b_out": seg(7).reshape(b_b_out.shape),
        "ple_b_gate": seg(8).reshape(ple_b_gate.shape)}
    names = SMALL_SHARDED + SMALL_REPLICATED
    pack = lambda dct: _pad_lanes(jnp.concatenate([dct[n].reshape(1, -1) for n in names], axis=1))
    g_pack = pack(small_grads)
    upd = _rowwise(_adam_small_fn, [pack(wts), pack(mom), pack(vel), g_pack], [], [(g_pack.shape, F32)] * 3,
                   name="adam_small")
    pos = 0
    for n in names:
        size = wts[n].size
        res[n] = [small_grads[n]] + [u[0, pos:pos + size].reshape(wts[n].shape) for u in upd]
        pos += size

    outs = [seg(9)[0], grad_x[None]]
    for k in range(4):
        outs += [res[n][k] for n in WEIGHT_ORDER]
    return tuple(outs)
```

```python
import functools

import jax
import jax.numpy as jnp
from jax import lax
from jax.experimental import pallas as pl
from jax.experimental.pallas import tpu as pltpu

F32 = jnp.float32
BF16 = jnp.bfloat16
MESH = pl.DeviceIdType.MESH

LANES = 128
HG_DK = 128
HG_CHUNK = 64
HG_SUB = 16
HG_ROWS = 512
HG_HEADS_PER_STEP = 2
ATT_HD = 64
ATT_G = 4
WINDOW = 128
DEPTH = 2
ALPHA = (2.0 * DEPTH) ** 0.25
LN_EPS = 1e-5
RMS_EPS = 1e-6
ADAM_LR, ADAM_B1, ADAM_B2, ADAM_EPS, ADAM_WD, ADAM_STEP = 0.001, 0.9, 0.999, 1e-08, 0.01, 10
N_CHIPS = 4
N_DEV = 8
VMEM_LIMIT = 56 * 1024 * 1024
NEG = -1e30


def _pick(n, cap):
    best = None
    for d in range(LANES, min(n, cap) + 1, LANES):
        if n % d == 0:
            best = d
    return n if best is None else best


def _pick_rows(m, cap):
    best = None
    for d in range(16, min(m, cap) + 1, 16):
        if m % d == 0:
            best = d
    return m if best is None else best


def _params(sem):
    return pltpu.CompilerParams(dimension_semantics=sem, vmem_limit_bytes=VMEM_LIMIT)


def _zeros_index(ndim, grid_rank=3):
    return (lambda i, j, kk: (0,) * ndim) if grid_rank == 3 else (lambda kk, i: (0,) * ndim)


def _mm(a, b, *, name, la=None, lb=None, ta=False, tb=False, bias=None, add=None, out_dtype=F32,
        out_layers=None, out_layer=None, after=None, post=None, tile_cols=None, caps=(1024, 1536, 2048)):
    ar, ac = a.shape[-2:]
    br, bc = b.shape[-2:]
    m, k = (ac, ar) if ta else (ar, ac)
    k2, n = (bc, br) if tb else (br, bc)
    assert k == k2, (a.shape, b.shape, ta, tb)
    if post is not None:
        caps = (512, n if tile_cols is None else tile_cols, caps[2])
    tm, tn, tk = _pick(m, caps[0]), _pick(n, caps[1]), _pick(k, caps[2])
    assert post is None or tn == caps[1]
    nk = k // tk
    gi, gj = m // tm, n // tn
    a_bytes, b_bytes = m * k * a.dtype.itemsize, k * n * b.dtype.itemsize
    rows_outer = (a_bytes + b_bytes * (gi if gj * nk > 1 else 1)) <= (b_bytes + a_bytes * (gj if gi * nk > 1 else 1))
    k_outer = post is not None and nk > 1 and gj == 1
    grid = (nk, gi) if k_outer else (gi, gj, nk) if rows_outer else (gj, gi, nk)

    def bs(block, idx, late=False):
        if k_outer:
            return pl.BlockSpec(block, lambda kk, i: idx(jnp.where(kk == nk - 1, i, 0) if late else i, 0, kk))
        return pl.BlockSpec(block, idx if rows_outer else (lambda q, p, kk: idx(p, q, kk)))

    def spec(block, idx, layer):
        if layer is None:
            return bs(block, idx)
        return bs((None,) + block, lambda i, j, kk: (layer,) + idx(i, j, kk))

    a_spec = spec((tk, tm), lambda i, j, kk: (kk, i), la) if ta else spec((tm, tk), lambda i, j, kk: (i, kk), la)
    b_spec = spec((tn, tk), lambda i, j, kk: (j, kk), lb) if tb else spec((tk, tn), lambda i, j, kk: (kk, j), lb)
    in_specs, operands = [a_spec, b_spec], [a, b]
    if bias is not None:
        in_specs.append(bs((1, tn), lambda i, j, kk: (0, j)))
        operands.append(bias)
    if add is not None:
        in_specs.append(bs((tm, tn), lambda i, j, kk: (i, j), late=True))
        operands.append(add)
    if after is not None:
        in_specs.append(pl.BlockSpec(memory_space=pl.ANY))
        operands.append(after)
    dims = (((0 if ta else 1,), (1 if tb else 0,)), ((), ()))
    has_bias, has_add = bias is not None, add is not None
    if post is None:
        fn, rows, whole, outs, sums = None, [], [], [], []
        out_shape = jax.ShapeDtypeStruct((m, n) if out_layers is None else (out_layers, m, n), out_dtype)
        out_specs = spec((tm, tn), lambda i, j, kk: (i, j), out_layer)
    else:
        fn, rows, whole, outs, sums = post
        in_specs += [bs((tm, r.shape[-1] // gj), lambda i, j, kk: (i, j), late=True) for r in rows]
        in_specs += [pl.BlockSpec(tuple(w.shape), _zeros_index(w.ndim, len(grid))) for w in whole]
        operands += list(rows) + list(whole)
        out_shape = [jax.ShapeDtypeStruct(sh, dt) for sh, dt in list(outs) + list(sums)]
        out_specs = ([bs((tm, sh[-1] // gj), lambda i, j, kk: (i, j), late=True) for sh, _ in outs]
                     + [pl.BlockSpec(tuple(sh), _zeros_index(len(sh), len(grid))) for sh, _ in sums])
    n_in, n_extra, n_outs, n_sums = len(operands), len(rows) + len(whole), len(outs), len(sums)

    def body(*refs):
        a_ref, b_ref = refs[0], refs[1]
        pos = 2
        bias_ref = add_ref = None
        if has_bias:
            bias_ref = refs[pos]
            pos += 1
        if has_add:
            add_ref = refs[pos]
            pos += 1
        extra_refs = refs[n_in - n_extra:n_in]
        out_refs = refs[n_in:n_in + max(n_outs, 1)]
        sum_refs = refs[n_in + n_outs:n_in + n_outs + n_sums]
        acc_ref = refs[-1] if nk > 1 else None
        part = lax.dot_general(a_ref[...].astype(BF16), b_ref[...].astype(BF16), dims, preferred_element_type=F32)

        def finish(total):
            if has_bias:
                total = total + bias_ref[...]
            if has_add:
                total = total + add_ref[...]
            if fn is None:
                out_refs[0][...] = total.astype(out_refs[0].dtype)
                return
            res = fn(total, *[r[...] for r in extra_refs])
            for ref, val in zip(out_refs, res[:n_outs]):
                ref[...] = val.astype(ref.dtype)
            if n_sums:
                @pl.when(pl.program_id(1 if k_outer or not rows_outer else 0) == 0)
                def _():
                    for ref in sum_refs:
                        ref[...] = jnp.zeros(ref.shape, ref.dtype)

                for ref, val in zip(sum_refs, res[n_outs:]):
                    ref[...] += val

        if nk == 1:
            finish(part)
        elif k_outer:
            kk = pl.program_id(0)
            rows_i = pl.ds(pl.multiple_of(pl.program_id(1) * tm, tm), tm)

            @pl.when(kk == 0)
            def _():
                acc_ref[rows_i, :] = part

            @pl.when(kk > 0)
            def _():
                acc_ref[rows_i, :] += part

            @pl.when(kk == nk - 1)
            def _():
                finish(acc_ref[rows_i, :])
        else:
            kk = pl.program_id(2)

            @pl.when(kk == 0)
            def _():
                acc_ref[...] = part

            @pl.when(kk > 0)
            def _():
                acc_ref[...] += part

            @pl.when(kk == nk - 1)
            def _():
                finish(acc_ref[...])

    return pl.pallas_call(
        body, name=name, grid=grid, in_specs=in_specs, out_specs=out_specs, out_shape=out_shape,
        scratch_shapes=[pltpu.VMEM((m, n) if k_outer else (tm, tn), F32)] if nk > 1 else [],
        compiler_params=_params(("arbitrary", "arbitrary") if k_outer
                                else ("arbitrary" if n_sums else "parallel", "parallel", "arbitrary") if rows_outer
                                else ("parallel", "arbitrary" if n_sums else "parallel", "arbitrary")),
    )(*operands)


def _rowwise(fn, rows, whole, outs, sums=(), *, name, tm=256):
    m = rows[0].shape[-2]
    tm = _pick_rows(m, tm)
    n_rows, n_whole, n_outs, n_sums = len(rows), len(whole), len(outs), len(sums)

    def rspec(shape):
        lead = len(shape) - 2
        return pl.BlockSpec(tuple(shape[:-2]) + (tm, shape[-1]), lambda i: (0,) * lead + (i, 0))

    def wspec(shape):
        return pl.BlockSpec(tuple(shape), lambda i: (0,) * len(shape))

    def body(*refs):
        vals = [r[...] for r in refs[:n_rows + n_whole]]
        out_refs = refs[n_rows + n_whole:n_rows + n_whole + n_outs]
        sum_refs = refs[n_rows + n_whole + n_outs:]
        res = fn(*vals)
        for ref, val in zip(out_refs, res[:n_outs]):
            ref[...] = val.astype(ref.dtype)
        if n_sums:
            @pl.when(pl.program_id(0) == 0)
            def _():
                for ref in sum_refs:
                    ref[...] = jnp.zeros(ref.shape, ref.dtype)

            for ref, val in zip(sum_refs, res[n_outs:]):
                ref[...] += val

    result = pl.pallas_call(
        body, name=name, grid=(m // tm,),
        in_specs=[rspec(r.shape) for r in rows] + [wspec(w.shape) for w in whole],
        out_specs=[rspec(s) for s, _ in outs] + [wspec(s) for s, _ in sums],
        out_shape=[jax.ShapeDtypeStruct(s, d) for s, d in list(outs) + list(sums)],
        compiler_params=_params(("arbitrary",)),
    )(*rows, *whole)
    return result


def _sigmoid(v):
    return jax.nn.sigmoid(v)


def _col_sum(v):
    return jnp.sum(v, axis=0, keepdims=True)


def _ln_stats(z):
    mu = jnp.mean(z, axis=-1, keepdims=True)
    zc = z - mu
    var = jnp.mean(zc * zc, axis=-1, keepdims=True)
    rstd = lax.rsqrt(var + LN_EPS)
    return zc * rstd, rstd


def _ln_fwd_fn(xin, h, gain, bias):
    xhat, _ = _ln_stats(ALPHA * xin + h)
    y = xhat * gain + bias
    return y, y


def _ple_ln_fwd_fn(xin, pg, pu, gain, bias):
    xhat, _ = _ln_stats(ALPHA * xin + _sigmoid(pg) * pu)
    y = xhat * gain + bias
    return y, y


def _ln_dz(dy, z, gain):
    xhat, rstd = _ln_stats(z)
    dxhat = dy * gain
    dz = rstd * (dxhat - jnp.mean(dxhat, axis=-1, keepdims=True)
                 - xhat * jnp.mean(dxhat * xhat, axis=-1, keepdims=True))
    return dz, _col_sum(dy * xhat), _col_sum(dy)


def _ln_bwd_fn(dy, xin, h, gain):
    dz, dgain, dbias = _ln_dz(dy, ALPHA * xin + h, gain)
    return ALPHA * dz, dz, dgain, dbias, _col_sum(dz)


def _ple_ln_bwd_fn(dy, xin, pg, pu, gain):
    sg = _sigmoid(pg)
    dz, dgain, dbias = _ln_dz(dy, ALPHA * xin + sg * pu, gain)
    dpg = dz * pu * sg * (1.0 - sg)
    return ALPHA * dz, dpg, dz * sg, dgain, dbias, _col_sum(dpg)


def _swiglu_fwd_fn(gu):
    hid = gu.shape[-1] // 2
    gate, up = gu[:, :hid], gu[:, hid:]
    return gu, gate * _sigmoid(gate) * up


def _swiglu_bwd_fn(dact, gu):
    gu = gu.astype(F32)
    hid = gu.shape[-1] // 2
    gate, up = gu[:, :hid], gu[:, hid:]
    sg = _sigmoid(gate)
    dgate = dact * up * sg * (1.0 + gate * (1.0 - sg))
    dup = dact * gate * sg
    return (jnp.concatenate([dgate, dup], axis=-1),)


def _loss_fn(y, target):
    err = y - target
    inv = 1.0 / y.shape[-1]
    part = 0.5 * inv * jnp.sum(jnp.sum(err * err, axis=-1, keepdims=True), axis=0, keepdims=True)
    return err * inv, jnp.broadcast_to(part, (1, LANES))


def _adam_fn(w, mom, vel, p_own, p_sib):
    g = p_own.astype(F32) + p_sib.astype(F32)
    m_new = ADAM_B1 * mom + (1.0 - ADAM_B1) * g
    v_new = ADAM_B2 * vel + (1.0 - ADAM_B2) * (g * g)
    m_hat = m_new / (1.0 - ADAM_B1 ** ADAM_STEP)
    v_hat = v_new / (1.0 - ADAM_B2 ** ADAM_STEP)
    delta = -ADAM_LR * (m_hat / (jnp.sqrt(v_hat) + ADAM_EPS) + ADAM_WD * w)
    return g, delta, m_new, v_new


def _split2(x):
    hi = x.astype(BF16)
    return hi, (x - hi.astype(F32)).astype(BF16)


def _dot3(a, b, dims):
    a_hi, a_lo = _split2(a)
    b_hi, b_lo = _split2(b)
    dn = (dims, ((), ()))
    return (lax.dot_general(a_hi, b_hi, dn, preferred_element_type=F32)
            + (lax.dot_general(a_hi, b_lo, dn, preferred_element_type=F32)
               + lax.dot_general(a_lo, b_hi, dn, preferred_element_type=F32)))


def _tdot(mask01, b):
    m = mask01.astype(BF16)
    b_hi = b.astype(BF16)
    rest = b - b_hi.astype(F32)
    b_mid = rest.astype(BF16)
    b_lo = (rest - b_mid.astype(F32)).astype(BF16)
    dn = (((1,), (0,)), ((), ()))
    return (lax.dot_general(m, b_hi, dn, preferred_element_type=F32)
            + (lax.dot_general(m, b_mid, dn, preferred_element_type=F32)
               + lax.dot_general(m, b_lo, dn, preferred_element_type=F32)))


def _hdot(a, b):
    return _dot3(a, b, ((1,), (0,)))


def _hdot_nt(a, b):
    return _dot3(a, b, ((1,), (1,)))


def _hdot_tn(a, b):
    return _dot3(a, b, ((0,), (0,)))


def _dot(a, b):
    return lax.dot_general(a.astype(BF16), b.astype(BF16), (((1,), (0,)), ((), ())), preferred_element_type=F32)


def _dot_nt(a, b):
    return lax.dot_general(a.astype(BF16), b.astype(BF16), (((1,), (1,)), ((), ())), preferred_element_type=F32)


def _dot_tn(a, b):
    return lax.dot_general(a.astype(BF16), b.astype(BF16), (((0,), (0,)), ((), ())), preferred_element_type=F32)


def _hg_masks():
    c = HG_CHUNK
    row = lax.broadcasted_iota(jnp.int32, (c, c), 0)
    col = lax.broadcasted_iota(jnp.int32, (c, c), 1)
    base = row & (-HG_SUB)
    return row, col, base, col <= row, col < base


def _hg_gates(qr, fr, alb):
    lbound = _sigmoid(alb[0:1, :] - alb[1:2, :])
    sig = _sigmoid(fr)
    forget = lbound + (1.0 - lbound) * sig
    kk = (1.0 - lbound) * _sigmoid(-fr)
    qt = qr * _sigmoid(qr) * (HG_DK ** -0.5)
    return qt, kk, jnp.log(forget), lbound, sig, forget


def _hg_scores(qt, kk, g, scores=True):
    c, nsub = HG_CHUNK, HG_CHUNK // HG_SUB
    row, col, base, causal, below = _hg_masks()
    b = _tdot(causal, g)
    rr = _tdot(below, g)
    bq = b - rr
    qh = qt * jnp.exp(bq)
    edecs = [None]
    parts = [jnp.zeros((HG_SUB, c), F32)]
    for i in range(1, nsub):
        edec = jnp.exp(jnp.minimum(rr[i * HG_SUB:i * HG_SUB + 1, :] - b, 0.0))
        edecs.append(edec)
        if scores:
            parts.append(_dot_nt(qh[i * HG_SUB:(i + 1) * HG_SUB, :], kk * edec))
    b3 = b.reshape(nsub, HG_SUB, HG_DK)
    q3 = qt.reshape(nsub, HG_SUB, HG_DK)
    k3 = kk.reshape(nsub, HG_SUB, HG_DK)
    if not scores:
        return None, b, bq, qh, edecs, (b3, q3, k3)
    a = jnp.where(below, jnp.concatenate(parts, axis=0), 0.0)
    for j in range(HG_SUB):
        e = jnp.exp(b3 - b3[:, j:j + 1, :])
        colv = jnp.sum(q3 * e * k3[:, j:j + 1, :], axis=-1, keepdims=True).reshape(c, 1)
        a = jnp.where(col == base + j, colv, a)
    a = jnp.where(causal, a, 0.0)
    return a, b, bq, qh, edecs, (b3, q3, k3)


def _hg_norm(o, gr, gain):
    r = lax.rsqrt(jnp.mean(o * o, axis=-1, keepdims=True) + RMS_EPS)
    sg = _sigmoid(gr)
    return o * r * gain, r, sg


def _hgrn2_fwd(proj, alb, gain, *, rb):
    m, d4 = proj.shape
    d = d4 // 4
    heads = d // HG_DK
    hp = HG_HEADS_PER_STEP
    rb = min(rb, m)
    cpb = rb // HG_CHUNK
    nrb = m // rb

    def body(q_ref, f_ref, v_ref, g_ref, alb_ref, gain_ref, o_ref, og_ref, st_ref, a_ref, state):
        @pl.when(pl.program_id(1) == 0)
        def _():
            state[...] = jnp.zeros(state.shape, F32)

        def chunk(ci, carry):
            sl = pl.ds(pl.multiple_of(ci * HG_CHUNK, HG_CHUNK), HG_CHUNK)
            for u in range(hp):
                ln = slice(u * HG_DK, (u + 1) * HG_DK)
                qt, kk, g, _, _, _ = _hg_gates(q_ref[sl, ln], f_ref[sl, ln], alb_ref[:, ln])
                v = v_ref[sl, ln]
                st = state[u]
                st_ref[u, ci] = st
                a, b, _, _, _, _ = _hg_scores(qt, kk, g)
                a_ref[u, ci] = a.astype(a_ref.dtype)
                o = _dot(a, v) + _dot_nt(qt * jnp.exp(b), st)
                b_last = b[HG_CHUNK - 1:HG_CHUNK, :]
                state[u] = st * jnp.exp(b_last) + _hdot_tn(v, kk * jnp.exp(b_last - b))
                o_ref[sl, ln] = o
                n, _, sg = _hg_norm(o, g_ref[sl, ln], gain_ref[...])
                og_ref[sl, ln] = (n * g_ref[sl, ln] * sg).astype(og_ref.dtype)
            return carry

        lax.fori_loop(0, cpb, chunk, 0)

    def col(cidx):
        return pl.BlockSpec((rb, hp * HG_DK), lambda h, r: (r, cidx * (heads // hp) + h))

    return pl.pallas_call(
        body, name="hgrn2_fwd", grid=(heads // hp, nrb),
        in_specs=[col(0), col(1), col(2), col(3),
                  pl.BlockSpec((2, hp * HG_DK), lambda h, r: (0, h)),
                  pl.BlockSpec((1, HG_DK), lambda h, r: (0, 0))],
        out_specs=[pl.BlockSpec((rb, hp * HG_DK), lambda h, r: (r, h)),
                   pl.BlockSpec((rb, hp * HG_DK), lambda h, r: (r, h)),
                   pl.BlockSpec((hp, cpb, HG_DK, HG_DK), lambda h, r: (h, r, 0, 0)),
                   pl.BlockSpec((hp, cpb, HG_CHUNK, HG_CHUNK), lambda h, r: (h, r, 0, 0))],
        out_shape=[jax.ShapeDtypeStruct((m, d), F32), jax.ShapeDtypeStruct((m, d), BF16),
                   jax.ShapeDtypeStruct((heads, m // HG_CHUNK, HG_DK, HG_DK), F32),
                   jax.ShapeDtypeStruct((heads, m // HG_CHUNK, HG_CHUNK, HG_CHUNK), BF16)],
        scratch_shapes=[pltpu.VMEM((hp, HG_DK, HG_DK), F32)],
        compiler_params=_params(("parallel", "arbitrary")),
    )(proj, proj, proj, proj, alb, gain)


def _hgrn2_bwd(proj, o_pre, states, scores, dog, alb, gain, *, rb):
    m, d4 = proj.shape
    d = d4 // 4
    heads = d // HG_DK
    rb = min(rb, m)
    cpb = rb // HG_CHUNK
    nrb = m // rb
    c, nsub = HG_CHUNK, HG_CHUNK // HG_SUB

    def body(q_ref, f_ref, v_ref, g_ref, o_ref, st_ref, a_ref, dog_ref, alb_ref, gain_ref,
             dq_ref, df_ref, dv_ref, dg_ref, dalb_ref, dgain_ref, dstate, carry_ref):
        first = (pl.program_id(0) == 0) & (pl.program_id(1) == 0)

        @pl.when(first)
        def _():
            dgain_ref[...] = jnp.zeros(dgain_ref.shape, F32)

        @pl.when(pl.program_id(1) == 0)
        def _():
            dstate[...] = jnp.zeros(dstate.shape, F32)
            carry_ref[...] = jnp.zeros(carry_ref.shape, F32)
            dalb_ref[...] = jnp.zeros(dalb_ref.shape, F32)

        row, col, base, causal, below = _hg_masks()
        sub_iota = lax.broadcasted_iota(jnp.int32, (nsub, HG_SUB, HG_DK), 1)
        row_k = lax.broadcasted_iota(jnp.int32, (c, HG_DK), 0)
        upper = col >= row

        def chunk(step, carry):
            ci = cpb - 1 - step
            sl = pl.ds(pl.multiple_of(ci * HG_CHUNK, HG_CHUNK), HG_CHUNK)
            qr, fr, v, gr = q_ref[sl, :], f_ref[sl, :], v_ref[sl, :], g_ref[sl, :]
            qt, kk, g, lbound, sig, forget = _hg_gates(qr, fr, alb_ref[...])
            o = o_ref[sl, :]
            dogv = dog_ref[sl, :]
            gain_v = gain_ref[...]
            n, r, sg = _hg_norm(o, gr, gain_v)
            dgr = dogv * n * sg * (1.0 + gr * (1.0 - sg))
            dn = dogv * gr * sg
            dgain_ref[...] += _col_sum(dn * o * r)
            u = dn * gain_v
            d_o = r * u - o * (r * r * r) * jnp.mean(u * o, axis=-1, keepdims=True)
            st0 = st_ref[ci]
            dst = dstate[...]
            _, b, bq, qh, edecs, (b3, q3, k3) = _hg_scores(qt, kk, g, scores=False)
            a = a_ref[ci]
            eb = jnp.exp(b)
            b_last = b[c - 1:c, :]
            kdl_dec = jnp.exp(b_last - b)
            kdl = kk * kdl_dec
            d_a = jnp.where(causal, _dot_nt(d_o, v), 0.0)
            d_at = _dot_nt(v, d_o)
            dv = _dot_tn(a, d_o) + _dot_nt(kdl, dst)
            dq = eb * _hdot(d_o, st0)
            dk = _hdot(v, dst) * kdl_dec
            d_a_below = jnp.where(below, d_a, 0.0)
            dq_parts = [jnp.zeros((HG_SUB, HG_DK), F32)]
            for i in range(1, nsub):
                lo, hi = i * HG_SUB, (i + 1) * HG_SUB
                dq_parts.append(_hdot(d_a_below[lo:hi, :], kk * edecs[i]))
                gi = _hdot(d_at[:, lo:hi], qh[lo:hi, :])
                dk = dk + jnp.where(row_k < lo, edecs[i] * gi, 0.0)
            dq = dq + jnp.concatenate(dq_parts, axis=0) * jnp.exp(bq)
            dq3 = jnp.zeros((nsub, HG_SUB, HG_DK), F32)
            dk3 = jnp.zeros((nsub, HG_SUB, HG_DK), F32)
            for j in range(HG_SUB):
                e = jnp.exp(jnp.minimum(b3 - b3[:, j:j + 1, :], 0.0))
                dcol = jnp.sum(jnp.where(col == base + j, d_a, 0.0), axis=-1, keepdims=True)
                t1 = dcol.reshape(nsub, HG_SUB, 1) * e
                dq3 = dq3 + t1 * k3[:, j:j + 1, :]
                dk3 = jnp.where(sub_iota == j, jnp.sum(t1 * q3, axis=1, keepdims=True), dk3)
            dq = dq + dq3.reshape(c, HG_DK)
            dk = dk + dk3.reshape(c, HG_DK)
            dstate[...] = dst * jnp.exp(b_last) + _hdot_tn(d_o, qt * eb)
            dglog = _tdot(upper, qt * dq - kk * dk) + carry_ref[...]
            carry_ref[...] = dglog[0:1, :]
            dforget = dglog / forget
            one_m_lb = 1.0 - lbound
            dsig = (dforget - dk) * one_m_lb
            sneg = _sigmoid(-fr)
            dlb = _col_sum(dforget * (1.0 - sig) - dk * sneg)
            dalb0 = dlb * lbound * one_m_lb
            dalb_ref[...] += jnp.concatenate([dalb0, -dalb0], axis=0)
            sq = _sigmoid(qr)
            dq_ref[sl, :] = (dq * (HG_DK ** -0.5) * sq * (1.0 + qr * (1.0 - sq))).astype(dq_ref.dtype)
            df_ref[sl, :] = (dsig * sig * (1.0 - sig)).astype(df_ref.dtype)
            dv_ref[sl, :] = dv.astype(dv_ref.dtype)
            dg_ref[sl, :] = dgr.astype(dg_ref.dtype)
            return carry

        lax.fori_loop(0, cpb, chunk, 0, unroll=2)

    def rev(r):
        return nrb - 1 - r

    def col(cidx):
        return pl.BlockSpec((rb, HG_DK), lambda h, r: (rev(r), cidx * heads + h))

    def head_rows():
        return pl.BlockSpec((rb, HG_DK), lambda h, r: (rev(r), h))

    return pl.pallas_call(
        body, name="hgrn2_bwd", grid=(heads, nrb),
        in_specs=[col(0), col(1), col(2), col(3), head_rows(),
                  pl.BlockSpec((None, cpb, HG_DK, HG_DK), lambda h, r: (h, rev(r), 0, 0)),
                  pl.BlockSpec((None, cpb, HG_CHUNK, HG_CHUNK), lambda h, r: (h, rev(r), 0, 0)),
                  head_rows(),
                  pl.BlockSpec((2, HG_DK), lambda h, r: (0, h)),
                  pl.BlockSpec((1, HG_DK), lambda h, r: (0, 0))],
        out_specs=[head_rows(), head_rows(), head_rows(), head_rows(),
                   pl.BlockSpec((2, HG_DK), lambda h, r: (0, h)),
                   pl.BlockSpec((1, HG_DK), lambda h, r: (0, 0))],
        out_shape=[jax.ShapeDtypeStruct((m, d), BF16)] * 4
                  + [jax.ShapeDtypeStruct((2, d), F32), jax.ShapeDtypeStruct((1, HG_DK), F32)],
        scratch_shapes=[pltpu.VMEM((HG_DK, HG_DK), F32), pltpu.VMEM((1, HG_DK), F32)],
        compiler_params=_params(("arbitrary", "arbitrary")),
    )(proj, proj, proj, proj, o_pre, states, scores, dog, alb, gain)


def _swa_probs(qh, kp, kc, sink, slope, has_prev):
    rows = qh.shape[0]
    qi = lax.broadcasted_iota(jnp.int32, (rows, WINDOW), 0) & (WINDOW - 1)
    si = lax.broadcasted_iota(jnp.int32, (rows, WINDOW), 1)
    scale = ATT_HD ** -0.5
    dist_c = (qi - si).astype(F32)
    s_p = _dot_nt(qh, kp) * scale - slope * (dist_c + float(WINDOW))
    s_c = _dot_nt(qh, kc) * scale - slope * dist_c
    s_p = jnp.where((si > qi) & has_prev, s_p, NEG)
    s_c = jnp.where(si <= qi, s_c, NEG)
    mx = jnp.maximum(jnp.maximum(jnp.max(s_p, axis=-1, keepdims=True), jnp.max(s_c, axis=-1, keepdims=True)), sink)
    e_p, e_c, e_s = jnp.exp(s_p - mx), jnp.exp(s_c - mx), jnp.exp(sink - mx)
    inv = 1.0 / (jnp.sum(e_p, axis=-1, keepdims=True) + jnp.sum(e_c, axis=-1, keepdims=True) + e_s)
    return e_p * inv, e_c * inv, e_s * inv


def _slope(h, n_heads):
    return float(2.0 ** (-8.0 * (h + 1) / n_heads))


def _swa_group(ref_vals, sink_ref, kh, n_heads):
    heads = [kh * ATT_G + g for g in range(ATT_G)]
    stacked = [jnp.concatenate([v[:, h * ATT_HD:(h + 1) * ATT_HD] for h in heads], axis=0) for v in ref_vals]
    grp = lax.shift_right_logical(lax.broadcasted_iota(jnp.int32, (ATT_G * WINDOW, 1), 0), WINDOW.bit_length() - 1)
    slope = jnp.zeros((ATT_G * WINDOW, 1), F32)
    sink = jnp.zeros((ATT_G * WINDOW, 1), F32)
    for g, h in enumerate(heads):
        slope = jnp.where(grp == g, _slope(h, n_heads), slope)
        sink = jnp.where(grp == g, sink_ref[:, h:h + 1], sink)
    return stacked, slope, sink


def _swa_fwd(q, kv, sinks):
    m, d = q.shape
    n_heads = d // ATT_HD
    kvh = n_heads // ATT_G
    kd = kvh * ATT_HD
    nb = m // WINDOW

    def body(q_ref, kvp_ref, kvc_ref, sink_ref, o_ref):
        has_prev = pl.program_id(0) > 0
        qv, kvp, kvc = q_ref[...], kvp_ref[...], kvc_ref[...]
        outs = []
        for kh in range(kvh):
            ks = slice(kh * ATT_HD, (kh + 1) * ATT_HD)
            vs = slice(kd + kh * ATT_HD, kd + (kh + 1) * ATT_HD)
            (q4,), slope, sink = _swa_group([qv], sink_ref, kh, n_heads)
            p_p, p_c, _ = _swa_probs(q4, kvp[:, ks], kvc[:, ks], sink, slope, has_prev)
            o4 = _dot(p_p, kvp[:, vs]) + _dot(p_c, kvc[:, vs])
            outs += [o4[g * WINDOW:(g + 1) * WINDOW, :] for g in range(ATT_G)]
        o_ref[...] = jnp.concatenate(outs, axis=-1).astype(o_ref.dtype)

    return pl.pallas_call(
        body, name="swa_fwd", grid=(nb,),
        in_specs=[pl.BlockSpec((WINDOW, d), lambda n: (n, 0)),
                  pl.BlockSpec((WINDOW, 2 * kd), lambda n: (jnp.maximum(n - 1, 0), 0)),
                  pl.BlockSpec((WINDOW, 2 * kd), lambda n: (n, 0)),
                  pl.BlockSpec((1, n_heads), lambda n: (0, 0))],
        out_specs=pl.BlockSpec((WINDOW, d), lambda n: (n, 0)),
        out_shape=jax.ShapeDtypeStruct((m, d), BF16),
        compiler_params=_params(("arbitrary",)),
    )(q, kv, kv, sinks)


def _swa_bwd(q, kv, sinks, dao):
    m, d = q.shape
    n_heads = d // ATT_HD
    kvh = n_heads // ATT_G
    kd = kvh * ATT_HD
    nb = m // WINDOW
    scale = ATT_HD ** -0.5

    def body(q_ref, kvp_ref, kvc_ref, sink_ref, do_ref, dq_ref, dkvc_ref, dkvp_ref, dqsum_ref, dsink_ref):
        @pl.when(pl.program_id(0) == 0)
        def _():
            dqsum_ref[...] = jnp.zeros(dqsum_ref.shape, F32)
            dsink_ref[...] = jnp.zeros(dsink_ref.shape, F32)

        has_prev = pl.program_id(0) > 0
        qv, kvp, kvc, dov = q_ref[...], kvp_ref[...], kvc_ref[...], do_ref[...]
        lane_h = lax.broadcasted_iota(jnp.int32, (1, n_heads), 1)
        dsink = jnp.zeros((1, n_heads), F32)
        dq_parts, dk_p, dk_c, dv_p, dv_c = [], [], [], [], []
        for kh in range(kvh):
            ks = slice(kh * ATT_HD, (kh + 1) * ATT_HD)
            vs = slice(kd + kh * ATT_HD, kd + (kh + 1) * ATT_HD)
            kp, kc, vp, vc = kvp[:, ks], kvc[:, ks], kvp[:, vs], kvc[:, vs]
            (q4, do4), slope, sink = _swa_group([qv, dov], sink_ref, kh, n_heads)
            p_p, p_c, p_s = _swa_probs(q4, kp, kc, sink, slope, has_prev)
            dp_p, dp_c = _dot_nt(do4, vp), _dot_nt(do4, vc)
            delta = jnp.sum(p_p * dp_p, axis=-1, keepdims=True) + jnp.sum(p_c * dp_c, axis=-1, keepdims=True)
            ds_p, ds_c = p_p * (dp_p - delta), p_c * (dp_c - delta)
            sink_term = p_s * delta
            dq4 = (_dot(ds_p, kp) + _dot(ds_c, kc)) * scale
            for g in range(ATT_G):
                rows = slice(g * WINDOW, (g + 1) * WINDOW)
                dsink = dsink + jnp.where(lane_h == kh * ATT_G + g, -_col_sum(sink_term[rows, :]), 0.0)
                dq_parts.append(dq4[rows, :])
            dk_p.append(_dot_tn(ds_p, q4) * scale)
            dk_c.append(_dot_tn(ds_c, q4) * scale)
            dv_p.append(_dot_tn(p_p, do4))
            dv_c.append(_dot_tn(p_c, do4))
        dq = jnp.concatenate(dq_parts, axis=-1)
        dq_ref[...] = dq.astype(dq_ref.dtype)
        dqsum_ref[...] += _col_sum(dq)
        dsink_ref[...] += dsink
        dkvc_ref[...] = jnp.concatenate(dk_c + dv_c, axis=-1)
        dkvp_ref[...] = jnp.concatenate(dk_p + dv_p, axis=-1)

    return pl.pallas_call(
        body, name="swa_bwd", grid=(nb,),
        in_specs=[pl.BlockSpec((WINDOW, d), lambda n: (n, 0)),
                  pl.BlockSpec((WINDOW, 2 * kd), lambda n: (jnp.maximum(n - 1, 0), 0)),
                  pl.BlockSpec((WINDOW, 2 * kd), lambda n: (n, 0)),
                  pl.BlockSpec((1, n_heads), lambda n: (0, 0)),
                  pl.BlockSpec((WINDOW, d), lambda n: (n, 0))],
        out_specs=[pl.BlockSpec((WINDOW, d), lambda n: (n, 0)),
                   pl.BlockSpec((WINDOW, 2 * kd), lambda n: (n, 0)),
                   pl.BlockSpec((WINDOW, 2 * kd), lambda n: (n, 0)),
                   pl.BlockSpec((1, d), lambda n: (0, 0)),
                   pl.BlockSpec((1, n_heads), lambda n: (0, 0))],
        out_shape=[jax.ShapeDtypeStruct((m, d), BF16), jax.ShapeDtypeStruct((m, 2 * kd), F32),
                   jax.ShapeDtypeStruct((m, 2 * kd), F32), jax.ShapeDtypeStruct((1, d), F32),
                   jax.ShapeDtypeStruct((1, n_heads), F32)],
        compiler_params=_params(("arbitrary",)),
    )(q, kv, kv, sinks, dao)


def _kv_grad_combine(dkv_cur, dkv_prev):
    m, w = dkv_cur.shape
    nb = m // WINDOW

    def body(cur_ref, nxt_ref, o_ref, sum_ref):
        @pl.when(pl.program_id(0) == 0)
        def _():
            sum_ref[...] = jnp.zeros(sum_ref.shape, F32)

        total = cur_ref[...] + jnp.where(pl.program_id(0) < nb - 1, nxt_ref[...], 0.0)
        o_ref[...] = total.astype(o_ref.dtype)
        sum_ref[...] += _col_sum(total)

    return pl.pallas_call(
        body, name="kv_grad_combine", grid=(nb,),
        in_specs=[pl.BlockSpec((WINDOW, w), lambda n: (n, 0)),
                  pl.BlockSpec((WINDOW, w), lambda n: (jnp.minimum(n + 1, nb - 1), 0))],
        out_specs=[pl.BlockSpec((WINDOW, w), lambda n: (n, 0)), pl.BlockSpec((1, w), lambda n: (0, 0))],
        out_shape=[jax.ShapeDtypeStruct((m, w), BF16), jax.ShapeDtypeStruct((1, w), F32)],
        compiler_params=_params(("arbitrary",)),
    )(dkv_cur, dkv_prev)


def _row(v):
    return v.reshape(1, -1)


def _local_step(x, p, target, wget, grad_sink, ln_gain, ln_bias, alb, norm_gain, kv_b, b_q, sinks, b_out, ple_b):
    gs = {}
    gains = ln_gain.reshape(DEPTH * 3, -1)
    biases = ln_bias.reshape(DEPTH * 3, -1)
    sd = x.shape
    pending = [None]

    def mm(a, b, lb=0, **kw):
        after, pending[0] = pending[0], None
        return _mm(a, b, lb=lb, after=after, **kw)

    def mm_ln(a, wt, xin, i, j, nm, bias=None, pu=None):
        r = 3 * i + j
        if pu is None:
            fn, rows = (lambda h, xv, g, bv: (h,) + _ln_fwd_fn(xv, h, g[r:r + 1], bv[r:r + 1])), [xin]
        else:
            fn = lambda h, xv, puv, g, bv: (h,) + _ple_ln_fwd_fn(xv, h, puv, g[r:r + 1], bv[r:r + 1])
            rows = [xin, pu]
        h, y, yb = _mm(a, wt, lb=0, bias=bias, name=nm,
                       post=(fn, rows, [gains, biases], [(sd, F32), (sd, F32), (sd, BF16)], []))
        return h, (y, yb)

    def mm_ln_bwd(a, wt, add, xin, h, i, j, nm):
        r = 3 * i + j
        dx_part, dh, dg, db, dhsum = mm(a, wt, tb=True, add=add, name=nm,
                                        post=(lambda dy, xv, hv, g: _ln_bwd_fn(dy, xv, hv, g[r:r + 1]), [xin, h],
                                              [gains], [(sd, F32), (sd, BF16)], [((1, sd[1]), F32)] * 3))
        gs[f"ln_gain_{i}_{j}"], gs[f"ln_bias_{i}_{j}"] = dg, db
        return dx_part, dh, dhsum

    def tail_fwd(xa, i):
        wgu = wget("ffn_w_gate_up", i, xa[1])
        hid2 = wgu.shape[-1]
        gu, act = _mm(xa[1], wgu, lb=0, name=f"ffn_up_swiglu{i}", tile_cols=hid2 // 2,
                      post=(_swiglu_fwd_fn, [], [], [((sd[0], hid2), BF16), ((sd[0], hid2 // 2), BF16)], []))
        f, xb = mm_ln(act, wget("ffn_w_down", i, act), xa[0], i, 1, f"ffn_down_ln{i}")
        pu = _mm(p, wget("ple_w_up", i, act), la=i, lb=0, name=f"ple_up{i}")
        pg, xc = mm_ln(xb[1], wget("ple_w_gate", i, act), xb[0], i, 2, f"ple_gate_ln{i}", bias=_row(ple_b[i]), pu=pu)
        return dict(xa=xa, gu=gu, act=act, f=f, xb=xb, pg=pg, pu=pu), xc

    def tail_bwd(head, sv, i, mix_in, mix_h):
        xa, xb = sv["xa"], sv["xb"]
        r = 3 * i + 2
        dxb_part, dpg, dpu, dg2, db2, dbg = head(
            lambda dy, xv, pgv, puv, g: _ple_ln_bwd_fn(dy, xv, pgv, puv, g[r:r + 1]), [xb[0], sv["pg"], sv["pu"]],
            [gains], [(sd, F32), (sd, BF16), (sd, BF16)], [((1, sd[1]), F32)] * 3)[:6]
        gs[f"ple_b_{i}"] = dbg
        gs[f"ln_gain_{i}_2"], gs[f"ln_bias_{i}_2"] = dg2, db2
        grad_of("ple_w_gate", i, xb[1], dpg)
        grad_of("ple_w_up", i, p, dpu, la=i)
        dxa_part, df, _ = mm_ln_bwd(dpg, wget("ple_w_gate", i, None), dxb_part, xa[0], sv["f"], i, 1,
                                    f"ple_gate_dx_ln{i}")
        grad_of("ffn_w_down", i, sv["act"], df)
        gu = sv["gu"]
        dgu, = mm(df, wget("ffn_w_down", i, None), tb=True, name=f"ffn_down_dx_swiglu{i}", tile_cols=gu.shape[1] // 4,
                  post=(_swiglu_bwd_fn, [gu], [], [(gu.shape, BF16)], []))
        grad_of("ffn_w_gate_up", i, xa[1], dgu)
        return mm_ln_bwd(dgu, wget("ffn_w_gate_up", i, None), dxa_part, mix_in, mix_h, i, 0, f"ffn_up_dx_ln{i}")

    def grad_of(nm, i, act, dout, la=None):
        grad = mm(act, dout, la=la, lb=None, ta=True, out_dtype=BF16, out_layers=1, out_layer=0,
                  name=f"grad_{nm}{i}")
        token = grad_sink(nm, i, grad)
        if token is not None:
            pending[0] = token

    proj = _mm(x, wget("a_w_in", 0, None), lb=0, name="hg_proj")
    o_pre, og, states, scores = _hgrn2_fwd(proj, alb, norm_gain, rb=HG_ROWS)
    h0, x1 = mm_ln(og, wget("a_w_out", 0, og), x, 0, 0, "hg_out_ln")
    sv0, x3 = tail_fwd(x1, 0)
    kv = _mm(x3[1], wget("kv_w", 0, x3[1]), lb=0, bias=_row(kv_b), out_dtype=BF16, name="kv_proj")
    q = _mm(x3[1], wget("b_w_q", 0, x3[1]), lb=0, bias=b_q, out_dtype=BF16, name="q_proj")
    ao = _swa_fwd(q, kv, sinks)
    h1, x4 = mm_ln(ao, wget("b_w_out", 0, x3[1]), x3[0], 1, 0, "att_out_ln", bias=b_out)
    sv1, y = tail_fwd(x4, 1)

    loss_box = []

    def loss_head(fn, rows, whole, outs, sums):
        def with_loss(yv, tv, *rest):
            dy, part = _loss_fn(yv, tv)
            return fn(dy, *rest) + (part,)

        res = _rowwise(with_loss, [y[0], target] + rows, whole, outs, list(sums) + [((1, LANES), F32)],
                       name="loss_ln_ple_bwd1")
        loss_box.append(res[-1])
        return res

    dx3_part, dh1, dh1sum = tail_bwd(loss_head, sv1, 1, x3[0], h1)
    loss = loss_box[0]
    gs["b_out"] = dh1sum
    grad_of("b_w_out", 0, ao, dh1)
    dao = mm(dh1, wget("b_w_out", 0, None), tb=True, out_dtype=BF16, name="att_out_dx")
    dq, dkv_cur, dkv_prev, dqsum, dsinks = _swa_bwd(q, kv, sinks, dao)
    gs["b_q"], gs["sinks"] = dqsum, dsinks
    dkv, dkvsum = _kv_grad_combine(dkv_cur, dkv_prev)
    gs["kv_b"] = dkvsum
    grad_of("b_w_q", 0, x3[1], dq)
    grad_of("kv_w", 0, x3[1], dkv)
    dx3 = mm(dq, wget("b_w_q", 0, None), tb=True, add=dx3_part, name="q_proj_dx")

    def kv_head(*post):
        return mm(dkv, wget("kv_w", 0, None), tb=True, add=dx3, name="kv_proj_dx_ln_ple_bwd0", post=post)

    dx_part, dh0, _ = tail_bwd(kv_head, sv0, 0, x, h0)
    grad_of("a_w_out", 0, og, dh0)
    dog = mm(dh0, wget("a_w_out", 0, None), tb=True, name="hg_out_dx")
    dqr, dfr, dvr, dgr, dalb, dgain = _hgrn2_bwd(proj, o_pre, states, scores, dog, alb, norm_gain, rb=HG_ROWS)
    gs["alb"], gs["norm_gain"] = dalb, dgain
    dproj = jnp.concatenate([dqr, dfr, dvr, dgr], axis=1)
    grad_of("a_w_in", 0, x, dproj)
    grad_x = mm(dproj, wget("a_w_in", 0, None), tb=True, add=dx_part, name="hg_proj_dx")
    return loss, grad_x, gs


HBM_SPEC = pl.BlockSpec(memory_space=pl.ANY)
HBM_ONLY = pl.BlockSpec(memory_space=pltpu.HBM)
SEM_SPEC = pl.BlockSpec(memory_space=pltpu.SEMAPHORE)
SIDE_EFFECT = pltpu.SideEffectType.DATAFLOW_SIDE_EFFECTING


def _slot(kind, j):
    return (j % 2) * 2 + j // 2 if kind == "colp" else j


def _piece(ref, kind, j):
    _, r, c = ref.shape
    if kind == "row":
        return ref.at[:, pl.ds(j * (r // N_CHIPS), r // N_CHIPS), :]
    return ref.at[:, :, pl.ds(_slot(kind, j) * (c // N_CHIPS), c // N_CHIPS)]


def _piece_dyn(ref, kind, j):
    _, r, c = ref.shape
    if kind == "row":
        return ref.at[:, pl.ds(pl.multiple_of(j * (r // N_CHIPS), 16), r // N_CHIPS), :]
    return ref.at[:, :, pl.ds(pl.multiple_of(_slot(kind, j) * (c // N_CHIPS), LANES), c // N_CHIPS)]


def _chip_of(j, c):
    return (j // 2, j % 2, c)


def _in_hbm(a):
    return pltpu.with_memory_space_constraint(a, pltpu.HBM)


def _place(src, layer, kind, chip, *, name, out_dtype, after=None):
    _, r, c = src.shape
    out_shape = (1, r * N_CHIPS, c) if kind == "row" else (1, r, c * N_CHIPS)
    tm = _pick_rows(r, 512)
    nb = r // tm

    def full_idx(i, chip_ref):
        return (0, chip_ref[0] * nb + i, 0) if kind == "row" else (0, i, _slot(kind, chip_ref[0]))

    in_specs, operands = [pl.BlockSpec((None, tm, c), lambda i, chip_ref: (layer, i, 0))], [src]
    if after is not None:
        in_specs.append(HBM_SPEC)
        operands.append(after)

    def body(chip_ref, src_ref, *rest):
        rest[-1][...] = src_ref[...].astype(rest[-1].dtype)

    return pl.pallas_call(
        body, name=name,
        grid_spec=pltpu.PrefetchScalarGridSpec(num_scalar_prefetch=1, grid=(nb,), in_specs=in_specs,
                                               out_specs=pl.BlockSpec((None, tm, c), full_idx)),
        out_shape=jax.ShapeDtypeStruct(out_shape, out_dtype),
        compiler_params=_params(("arbitrary",)),
    )(chip, *operands)


def _half(ref, c):
    h = ref.shape[1] // 2
    start = c * h if isinstance(c, int) else pl.multiple_of(c * h, 16)
    return ref.at[:, pl.ds(start, h), :]


class _SiblingFill:
    def __init__(self, lands, kinds, name):
        self.kinds, self.name, self.n = kinds, name, len(lands)
        n = self.n
        sem_shape = pltpu.SemaphoreType.DMA((n * N_CHIPS,))

        def body(*refs):
            land_refs, send_sems, recv_sems, token = refs[:n], refs[n], refs[n + 1], refs[-1]
            for cp in self._copies(land_refs, send_sems, recv_sems):
                cp.start()
            token[...] = jnp.zeros(token.shape, token.dtype)

        outs = pl.pallas_call(
            body, name=name + "_start",
            in_specs=[HBM_ONLY] * n,
            out_specs=[SEM_SPEC, SEM_SPEC] + [HBM_ONLY] * n + [pl.BlockSpec(memory_space=pltpu.VMEM)],
            out_shape=[sem_shape, sem_shape] + [pltpu.HBM(a.shape, a.dtype) for a in lands]
                      + [jax.ShapeDtypeStruct((8, LANES), F32)],
            input_output_aliases={i: i + 2 for i in range(n)},
            compiler_params=pltpu.CompilerParams(has_side_effects=SIDE_EFFECT),
        )(*[_in_hbm(a) for a in lands])
        self.send_sems, self.recv_sems, self.lands, self.token = outs[0], outs[1], list(outs[2:2 + n]), outs[-1]

    def _copies(self, land_refs, send_sems, recv_sems):
        x, y, c = lax.axis_index("x"), lax.axis_index("y"), lax.axis_index("c")
        me = 2 * x + y
        copies = []
        for a in range(self.n):
            for k in range(1, N_CHIPS):
                t = (me + k) % N_CHIPS
                slice_t = _piece_dyn(land_refs[a], self.kinds[a], t)
                got = _half(slice_t, c)
                copies.append(pltpu.make_async_remote_copy(
                    src_ref=got, dst_ref=got, send_sem=send_sems.at[a * N_CHIPS + k],
                    recv_sem=recv_sems.at[a * N_CHIPS + k], device_id=(x, y, 1 - c), device_id_type=MESH))
        return copies

    def wait(self, after):
        n = self.n

        def body(*refs):
            land_refs, send_sems, recv_sems = refs[:n], refs[n], refs[n + 1]
            for cp in self._copies(land_refs, send_sems, recv_sems):
                cp.wait_send()
                cp.wait_recv()

        operands = [_in_hbm(a) for a in self.lands] + [self.send_sems, self.recv_sems]
        in_specs = [HBM_ONLY] * n + [SEM_SPEC, SEM_SPEC]
        if after is not None:
            operands.append(after)
            in_specs.append(HBM_SPEC)
        outs = pl.pallas_call(
            body, name=self.name + "_wait",
            in_specs=in_specs, out_specs=[HBM_ONLY] * n,
            out_shape=[pltpu.HBM(a.shape, a.dtype) for a in self.lands],
            input_output_aliases={i: i for i in range(n)},
            compiler_params=pltpu.CompilerParams(has_side_effects=SIDE_EFFECT),
        )(*operands)
        return list(outs)


class _Exchange:
    def __init__(self, mode, srcs, lands, kinds, layers, name, after=None, halves=None):
        self.mode, self.kinds, self.layers, self.name, self.n = mode, kinds, layers, name, len(lands)
        self.halves = halves if halves is not None else [False] * len(lands)
        n, ns = self.n, len(srcs)
        n_in = ns + n + (after is not None)
        sem_shape = pltpu.SemaphoreType.DMA((n * N_CHIPS,))

        def body(*refs):
            src_refs, land_refs = refs[:ns], refs[ns:ns + n]
            send_sems, recv_sems = refs[n_in], refs[n_in + 1]
            token = refs[-1]
            c = lax.axis_index("c")
            me = 2 * lax.axis_index("x") + lax.axis_index("y")
            for j in range(N_CHIPS):
                @pl.when(me == j)
                def _():
                    for a in range(n):
                        for t in range(N_CHIPS):
                            if t != j:
                                src, dst = self._ends(src_refs, land_refs, a, j, t, c)
                                pltpu.make_async_remote_copy(
                                    src_ref=src, dst_ref=dst, send_sem=send_sems.at[a * N_CHIPS + t],
                                    recv_sem=recv_sems.at[a * N_CHIPS + j],
                                    device_id=_chip_of(t, c), device_id_type=MESH).start()
            token[...] = jnp.zeros(token.shape, token.dtype)

        arrays = list(srcs) + list(lands)
        operands = [_in_hbm(a) for a in arrays]
        in_specs = [HBM_ONLY] * (ns + n)
        if after is not None:
            operands.append(after)
            in_specs.append(HBM_SPEC)
        outs = pl.pallas_call(
            body, name=name + "_start",
            in_specs=in_specs,
            out_specs=[SEM_SPEC, SEM_SPEC] + [HBM_ONLY] * (ns + n) + [pl.BlockSpec(memory_space=pltpu.VMEM)],
            out_shape=[sem_shape, sem_shape] + [pltpu.HBM(a.shape, a.dtype) for a in arrays]
                      + [jax.ShapeDtypeStruct((8, LANES), F32)],
            input_output_aliases={i: i + 2 for i in range(ns + n)},
            compiler_params=pltpu.CompilerParams(has_side_effects=SIDE_EFFECT),
        )(*operands)
        self.send_sems, self.recv_sems = outs[0], outs[1]
        self.srcs, self.lands = list(outs[2:2 + ns]), list(outs[2 + ns:2 + ns + n])
        self.token = outs[-1]

    def _ends(self, src_refs, land_refs, a, me_j, peer, c):
        if self.mode == "gather":
            mine = _piece(land_refs[a], self.kinds[a], me_j)
            if self.halves[a]:
                mine = _half(mine, c)
            return mine, mine
        return _piece(src_refs[a], self.kinds[a], peer), land_refs[a].at[me_j, pl.ds(self.layers[a], 1)]

    def wait(self, after, lands=None):
        n, ns = self.n, len(self.srcs)
        lands = self.lands if lands is None else lands

        def body(*refs):
            src_refs, land_refs = refs[:ns], refs[ns:ns + n]
            send_sems, recv_sems = refs[ns + n], refs[ns + n + 1]
            c = lax.axis_index("c")
            me = 2 * lax.axis_index("x") + lax.axis_index("y")
            for j in range(N_CHIPS):
                @pl.when(me != j)
                def _():
                    for a in range(n):
                        sent, _ = self._ends(src_refs, land_refs, a, 0, j, c)
                        _, landed = self._ends(src_refs, land_refs, a, j, 0, c)
                        cp = pltpu.make_async_remote_copy(
                            src_ref=sent, dst_ref=landed, send_sem=send_sems.at[a * N_CHIPS + j],
                            recv_sem=recv_sems.at[a * N_CHIPS + j],
                            device_id=_chip_of(j, c), device_id_type=MESH)
                        cp.wait_send()
                        cp.wait_recv()

        arrays = self.srcs + list(lands)
        operands = [_in_hbm(a) for a in arrays] + [self.send_sems, self.recv_sems]
        in_specs = [HBM_ONLY] * (ns + n) + [SEM_SPEC, SEM_SPEC]
        if after is not None:
            operands.append(after)
            in_specs.append(HBM_SPEC)
        outs = pl.pallas_call(
            body, name=self.name + "_wait",
            in_specs=in_specs, out_specs=[HBM_ONLY] * (ns + n),
            out_shape=[pltpu.HBM(a.shape, a.dtype) for a in arrays],
            input_output_aliases={i: i for i in range(ns + n)},
            compiler_params=pltpu.CompilerParams(has_side_effects=SIDE_EFFECT),
        )(*operands)
        return list(outs[:ns]), list(outs[ns:])


def _sum_arrivals(zone, own_grads, kind, chip, name):
    _, layers, r, c = zone.shape
    tm = _pick_rows(r, 256)
    nb = r // tm

    def own_idx(l, i, chip_ref):
        return (0, chip_ref[0] * nb + i, 0) if kind == "row" else (0, i, _slot(kind, chip_ref[0]))

    def slot_idx(k):
        return lambda l, i, chip_ref: (jnp.where(chip_ref[0] == k, (k + 1) % N_CHIPS, k), l, i, 0)

    in_specs = [pl.BlockSpec((None, None, tm, c), slot_idx(k)) for k in range(N_CHIPS)]
    in_specs += [pl.BlockSpec((None, tm, c), own_idx) for _ in own_grads]

    def body(chip_ref, *refs):
        slot_refs, own_refs, o_ref = refs[:N_CHIPS], refs[N_CHIPS:N_CHIPS + layers], refs[-1]
        own = own_refs[0][...]
        for u in range(1, layers):
            own = jnp.where(pl.program_id(0) == u, own_refs[u][...], own)
        acc = None
        for k in range(N_CHIPS):
            term = jnp.where(chip_ref[0] == k, own, slot_refs[k][...]).astype(F32)
            acc = term if acc is None else acc + term
        o_ref[...] = acc.astype(o_ref.dtype)

    return pl.pallas_call(
        body, name=name,
        grid_spec=pltpu.PrefetchScalarGridSpec(
            num_scalar_prefetch=1, grid=(layers, nb), in_specs=in_specs,
            out_specs=pl.BlockSpec((tm, c), lambda l, i, chip_ref: (l * nb + i, 0))),
        out_shape=jax.ShapeDtypeStruct((layers * r, c), BF16),
        compiler_params=_params(("arbitrary", "arbitrary")),
    )(chip, zone, zone, zone, zone, *own_grads)


def _sibling_swap(arrays, name):
    n = len(arrays)

    def body(*refs):
        ins, outs = refs[:n], refs[n:2 * n]
        send_sems, recv_sems = refs[2 * n:]
        sibling = (lax.axis_index("x"), lax.axis_index("y"), 1 - lax.axis_index("c"))
        copies = [pltpu.make_async_remote_copy(src_ref=ins[a], dst_ref=outs[a], send_sem=send_sems.at[a],
                                               recv_sem=recv_sems.at[a], device_id=sibling, device_id_type=MESH)
                  for a in range(n)]
        for cp in copies:
            cp.start()
        for cp in copies:
            cp.wait()

    return pl.pallas_call(
        body, name=name,
        in_specs=[HBM_SPEC] * n, out_specs=[HBM_SPEC] * n,
        out_shape=[jax.ShapeDtypeStruct(a.shape, a.dtype) for a in arrays],
        scratch_shapes=[pltpu.SemaphoreType.DMA((n,)), pltpu.SemaphoreType.DMA((n,))],
    )(*arrays)


def _gather_devices(vec):
    def body(in_ref, out_ref, send_sems, recv_sems, local_sem):
        x, y, c = lax.axis_index("x"), lax.axis_index("y"), lax.axis_index("c")
        me = 4 * x + 2 * y + c
        mine = pltpu.make_async_copy(in_ref, out_ref.at[me], local_sem)
        mine.start()
        copies = []
        for rel in range(1, N_DEV):
            peer = (x ^ (rel >> 2), y ^ ((rel >> 1) & 1), c ^ (rel & 1))
            copies.append(pltpu.make_async_remote_copy(
                src_ref=in_ref, dst_ref=out_ref.at[me], send_sem=send_sems.at[rel], recv_sem=recv_sems.at[rel],
                device_id=peer, device_id_type=MESH))
        for cp in copies:
            cp.start()
        for cp in copies:
            cp.wait()
        mine.wait()

    return pl.pallas_call(
        body, name="gather_small",
        in_specs=[HBM_SPEC], out_specs=HBM_SPEC,
        out_shape=jax.ShapeDtypeStruct((N_DEV,) + vec.shape, vec.dtype),
        scratch_shapes=[pltpu.SemaphoreType.DMA((N_DEV,)), pltpu.SemaphoreType.DMA((N_DEV,)),
                        pltpu.SemaphoreType.DMA],
    )(vec)


BIG = [("a_w_in", "col"), ("a_w_out", "row"), ("kv_w", "row"), ("b_w_q", "row"), ("b_w_out", "row"),
       ("ffn_w_gate_up", "colp"), ("ffn_w_down", "row"), ("ple_w_up", "col"), ("ple_w_gate", "row")]
GATHER_GROUPS = [[("a_w_in", 0), ("small", 0)], [("a_w_out", 0), ("ffn_w_gate_up", 0)],
                 [("ffn_w_down", 0), ("ple_w_gate", 0), ("ple_w_up", 0)], [("kv_w", 0), ("b_w_q", 0), ("b_w_out", 0)],
                 [("ffn_w_gate_up", 1)], [("ffn_w_down", 1), ("ple_w_gate", 1), ("ple_w_up", 1)]]
SCATTER_GROUPS = [[("ple_w_gate", 1), ("ple_w_up", 1), ("ffn_w_down", 1)], [("ffn_w_gate_up", 1)],
                  [("b_w_out", 0), ("b_w_q", 0), ("kv_w", 0)], [("ple_w_gate", 0), ("ple_w_up", 0), ("ffn_w_down", 0)],
                  [("ffn_w_gate_up", 0), ("a_w_out", 0)], [("a_w_in", 0)]]
SMALL_SHARDED = ["ln_gain", "ln_bias", "a_lower_bound"]
SMALL_REPLICATED = ["a_norm_gain", "kv_b", "b_b_q", "b_sinks", "b_b_out", "ple_b_gate"]
WEIGHT_ORDER = ["a_w_in", "a_lower_bound", "a_norm_gain", "a_w_out", "kv_w", "kv_b", "b_w_q", "b_b_q", "b_sinks",
                "b_w_out", "b_b_out", "ffn_w_gate_up", "ffn_w_down", "ple_w_up", "ple_w_gate", "ple_b_gate",
                "ln_gain", "ln_bias"]


def _as3(a):
    return a.reshape((-1,) + a.shape[-2:]) if a.ndim >= 3 else a.reshape((1,) + a.shape)


def _pad_lanes(v):
    n = v.shape[-1]
    return jnp.pad(v, ((0, 0), (0, (-n) % LANES)))


def _adam_small_fn(w, mom, vel, g):
    return _adam_fn(w, mom, vel, g, jnp.zeros_like(g))[1:]


def _sum_rows_fn(slots):
    acc = slots[0]
    for s in range(1, slots.shape[0]):
        acc = acc + slots[s]
    return (acc,)


def kernel(x, p, a_w_in, a_lower_bound, a_norm_gain, a_w_out, kv_w, kv_b, b_w_q, b_b_q, b_sinks, b_w_out, b_b_out, ffn_w_gate_up, ffn_w_down, ple_w_up, ple_w_gate, ple_b_gate, ln_gain, ln_bias, loss_target, m_a_w_in, m_a_lower_bound, m_a_norm_gain, m_a_w_out, m_kv_w, m_kv_b, m_b_w_q, m_b_b_q, m_b_sinks, m_b_w_out, m_b_b_out, m_ffn_w_gate_up, m_ffn_w_down, m_ple_w_up, m_ple_w_gate, m_ple_b_gate, m_ln_gain, m_ln_bias, v_a_w_in, v_a_lower_bound, v_a_norm_gain, v_a_w_out, v_kv_w, v_kv_b, v_b_w_q, v_b_b_q, v_b_sinks, v_b_w_out, v_b_b_out, v_ffn_w_gate_up, v_ffn_w_down, v_ple_w_up, v_ple_w_gate, v_ple_b_gate, v_ln_gain, v_ln_bias):
    args = dict(locals())
    wts = {n: args[n] for n in WEIGHT_ORDER}
    mom = {n: args["m_" + n] for n in WEIGHT_ORDER}
    vel = {n: args["v_" + n] for n in WEIGHT_ORDER}
    chip = 2 * lax.axis_index("x") + lax.axis_index("y")
    d = x.shape[-1]
    dq = d // N_CHIPS

    kind_of = dict(BIG)
    kind_of["small"] = "col"
    chip_arr = chip.reshape(1).astype(jnp.int32)
    small_pack = jnp.concatenate([wts[n].reshape(-1, dq) for n in SMALL_SHARDED], axis=0)[None]

    def place(key, after):
        n, layer = key
        if n == "small":
            return _place(small_pack, 0, "col", chip_arr, name="place_small", out_dtype=F32, after=after)
        return _place(_as3(wts[n]), layer, kind_of[n], chip_arr, name=f"place_{n}{layer}", out_dtype=BF16,
                      after=after)

    gathers, where = [], {}
    for gi, group in enumerate(GATHER_GROUPS):
        prev = gathers[-1].token if gathers else None
        gathers.append(_Exchange("gather", [], [place(k, prev) for k in group], [kind_of[k[0]] for k in group],
                                 [0] * len(group), f"gather{gi}", after=prev,
                                 halves=[k[0] != "small" for k in group]))
        for k in group:
            where[k] = gi
    all_started = gathers[-1].token
    ready = {}

    fills = {}

    def pass_on(gi, after):
        if gi not in fills:
            group = GATHER_GROUPS[gi]
            outs = gathers[gi].wait(after)[1]
            split = [i for i, k in enumerate(group) if k[0] != "small"]
            fills[gi] = (outs, split, _SiblingFill([outs[i] for i in split], [kind_of[group[i][0]] for i in split],
                                                   f"fill{gi}"))

    def wget(name, layer, after):
        key = (name, layer)
        if key not in ready:
            gi = where[key]
            after = all_started if gi == 0 else after
            pass_on(gi, after)
            if 1 <= gi < len(GATHER_GROUPS) - 1:
                pass_on(gi + 1, after)
            outs, split, fill = fills[gi]
            for i, arr in zip(split, fill.wait(after)):
                outs[i] = arr
            for k, arr in zip(GATHER_GROUPS[gi], outs):
                ready[k] = arr
        return ready[key]

    small_full = wget("small", 0, None)[0]
    ln_gain_f = small_full[0:6].reshape(DEPTH, 3, d)
    ln_bias_f = small_full[6:12].reshape(DEPTH, 3, d)
    alb_f = small_full[12:14]

    group_of = {k: gi for gi, group in enumerate(SCATTER_GROUPS) for k in group}
    grads_done, zones, scatters = {}, {}, []

    def grad_sink(name, layer, grad):
        grads_done[(name, layer)] = grad
        if name not in zones:
            zones[name] = lax.empty((N_CHIPS,) + _as3(wts[name]).shape, BF16)
        gi = group_of[(name, layer)]
        group = SCATTER_GROUPS[gi]
        if not all(k in grads_done for k in group):
            return None
        ex = _Exchange("scatter", [grads_done[k] for k in group], [zones[k[0]] for k in group],
                       [kind_of[k[0]] for k in group], [k[1] for k in group], f"scatter{gi}")
        for k, zone in zip(group, ex.lands):
            zones[k[0]] = zone
        scatters.append((ex, group))
        return ex.token

    loss, grad_x, gs = _local_step(
        x[0], p.reshape((p.shape[0],) + p.shape[2:]), loss_target[0], wget, grad_sink, ln_gain_f, ln_bias_f, alb_f, a_norm_gain, kv_b, b_b_q,
        b_sinks, b_b_out, ple_b_gate)

    res = {}

    def arrive(batch, after):
        for ex, group in batch:
            srcs, outs = ex.wait(after, lands=[zones[k[0]] for k in group])
            for k, grad, zone in zip(group, srcs, outs):
                grads_done[k], zones[k[0]] = grad, zone

    def update(names, tag):
        partial = []
        for n in names:
            own = [grads_done[(n, layer)] for layer in range(zones[n].shape[1])]
            partial.append(_sum_arrivals(zones[n], own, kind_of[n], chip_arr, f"sum_{n}"))
        sibling = _sibling_swap(partial, tag)
        for n, own, sib in zip(names, partial, sibling):
            shp = wts[n].shape
            flat = lambda a: a.reshape(-1, shp[-1])
            out = _rowwise(_adam_fn, [flat(wts[n]), flat(mom[n]), flat(vel[n]), own, sib], [],
                           [(own.shape, F32)] * 4, name=f"adam_{n}")
            res[n] = [o.reshape(shp) for o in out]
        return res[names[-1]][1]

    last_names = [k[0] for k in SCATTER_GROUPS[-1]]
    arrive(scatters[:-1], grad_x)
    updated = update([n for n, _ in BIG if n not in last_names], "sibling_swap")
    arrive(scatters[-1:], updated)
    update(last_names, "sibling_swap_last")

    ln_g = jnp.concatenate([gs[f"ln_gain_{i}_{j}"] for i in range(DEPTH) for j in range(3)], axis=0)
    ln_b = jnp.concatenate([gs[f"ln_bias_{i}_{j}"] for i in range(DEPTH) for j in range(3)], axis=0)
    ple_bg = jnp.concatenate([gs[f"ple_b_{i}"] for i in range(DEPTH)], axis=0)
    small_list = [ln_g.reshape(1, -1), ln_b.reshape(1, -1), gs["alb"].reshape(1, -1), gs["norm_gain"],
                  gs["kv_b"], gs["b_q"], _pad_lanes(gs["sinks"]), gs["b_out"], ple_bg.reshape(1, -1), loss]
    small_vec = jnp.concatenate(small_list, axis=1)
    everyone = _gather_devices(small_vec)
    total, = _rowwise(_sum_rows_fn, [everyone], [], [(small_vec.shape, F32)], name="sum_small")
    offs, pos = [], 0
    for v in small_list:
        offs.append((pos, v.shape[1]))
        pos += v.shape[1]

    def seg(k):
        return total[0, offs[k][0]:offs[k][0] + offs[k][1]]

    def my_cols(full, rows):
        return lax.dynamic_slice_in_dim(full.reshape(rows, N_CHIPS, dq), chip, 1, axis=1).reshape(rows, dq)

    n_sink = b_sinks.shape[-1]
    small_grads = {
        "ln_gain": my_cols(seg(0), 6).reshape(ln_gain.shape), "ln_bias": my_cols(seg(1), 6).reshape(ln_bias.shape),
        "a_lower_bound": my_cols(seg(2), 2), "a_norm_gain": seg(3).reshape(a_norm_gain.shape),
        "kv_b": seg(4).reshape(kv_b.shape), "b_b_q": seg(5).reshape(b_b_q.shape),
        "b_sinks": seg(6)[:n_sink].reshape(b_sinks.shape), "b_b_out": seg(7).reshape(b_b_out.shape),
        "ple_b_gate": seg(8).reshape(ple_b_gate.shape)}
    names = SMALL_SHARDED + SMALL_REPLICATED
    pack = lambda dct: _pad_lanes(jnp.concatenate([dct[n].reshape(1, -1) for n in names], axis=1))
    g_pack = pack(small_grads)
    upd = _rowwise(_adam_small_fn, [pack(wts), pack(mom), pack(vel), g_pack], [], [(g_pack.shape, F32)] * 3,
                   name="adam_small")
    pos = 0
    for n in names:
        size = wts[n].size
        res[n] = [small_grads[n]] + [u[0, pos:pos + size].reshape(wts[n].shape) for u in upd]
        pos += size

    outs = [seg(9)[0], grad_x[None]]
    for k in range(4):
        outs += [res[n][k] for n in WEIGHT_ORDER]
    return tuple(outs)
```

```python
import functools

import jax
import jax.numpy as jnp
from jax import lax
from jax.experimental import pallas as pl
from jax.experimental.pallas import tpu as pltpu

F32 = jnp.float32
BF16 = jnp.bfloat16
MESH = pl.DeviceIdType.MESH

LANES = 128
HG_DK = 128
HG_CHUNK = 64
HG_SUB = 16
HG_ROWS = 512
HG_HEADS_PER_STEP = 2
ATT_HD = 64
ATT_G = 4
WINDOW = 128
DEPTH = 2
ALPHA = (2.0 * DEPTH) ** 0.25
LN_EPS = 1e-5
RMS_EPS = 1e-6
ADAM_LR, ADAM_B1, ADAM_B2, ADAM_EPS, ADAM_WD, ADAM_STEP = 0.001, 0.9, 0.999, 1e-08, 0.01, 10
N_CHIPS = 4
N_DEV = 8
VMEM_LIMIT = 56 * 1024 * 1024
NEG = -1e30


def _pick(n, cap):
    best = None
    for d in range(LANES, min(n, cap) + 1, LANES):
        if n % d == 0:
            best = d
    return n if best is None else best


def _pick_rows(m, cap):
    best = None
    for d in range(16, min(m, cap) + 1, 16):
        if m % d == 0:
            best = d
    return m if best is None else best


def _params(sem):
    return pltpu.CompilerParams(dimension_semantics=sem, vmem_limit_bytes=VMEM_LIMIT)


def _zeros_index(ndim, grid_rank=3):
    return (lambda i, j, kk: (0,) * ndim) if grid_rank == 3 else (lambda kk, i: (0,) * ndim)


def _mm(a, b, *, name, la=None, lb=None, ta=False, tb=False, bias=None, add=None, out_dtype=F32,
        out_layers=None, out_layer=None, after=None, post=None, tile_cols=None, caps=(1024, 1536, 2048)):
    ar, ac = a.shape[-2:]
    br, bc = b.shape[-2:]
    m, k = (ac, ar) if ta else (ar, ac)
    k2, n = (bc, br) if tb else (br, bc)
    assert k == k2, (a.shape, b.shape, ta, tb)
    if post is not None:
        caps = (512, n if tile_cols is None else tile_cols, caps[2])
    tm, tn, tk = _pick(m, caps[0]), _pick(n, caps[1]), _pick(k, caps[2])
    assert post is None or tn == caps[1]
    nk = k // tk
    gi, gj = m // tm, n // tn
    a_bytes, b_bytes = m * k * a.dtype.itemsize, k * n * b.dtype.itemsize
    rows_outer = (a_bytes + b_bytes * (gi if gj * nk > 1 else 1)) <= (b_bytes + a_bytes * (gj if gi * nk > 1 else 1))
    k_outer = post is not None and nk > 1 and gj == 1
    grid = (nk, gi) if k_outer else (gi, gj, nk) if rows_outer else (gj, gi, nk)
    keep_at = ta and nk == 1 and gj > 1 and rows_outer

    def bs(block, idx, late=False):
        if k_outer:
            return pl.BlockSpec(block, lambda kk, i: idx(jnp.where(kk == nk - 1, i, 0) if late else i, 0, kk))
        return pl.BlockSpec(block, idx if rows_outer else (lambda q, p, kk: idx(p, q, kk)))

    def spec(block, idx, layer):
        if layer is None:
            return bs(block, idx)
        return bs((None,) + block, lambda i, j, kk: (layer,) + idx(i, j, kk))

    a_spec = spec((tk, tm), lambda i, j, kk: (kk, i), la) if ta else spec((tm, tk), lambda i, j, kk: (i, kk), la)
    b_spec = spec((tn, tk), lambda i, j, kk: (j, kk), lb) if tb else spec((tk, tn), lambda i, j, kk: (kk, j), lb)
    in_specs, operands = [a_spec, b_spec], [a, b]
    if bias is not None:
        in_specs.append(bs((1, tn), lambda i, j, kk: (0, j)))
        operands.append(bias)
    if add is not None:
        in_specs.append(bs((tm, tn), lambda i, j, kk: (i, j), late=True))
        operands.append(add)
    if after is not None:
        in_specs.append(pl.BlockSpec(memory_space=pl.ANY))
        operands.append(after)
    dims = (((0 if ta else 1,), (1 if tb else 0,)), ((), ()))
    has_bias, has_add = bias is not None, add is not None
    if post is None:
        fn, rows, whole, outs, sums = None, [], [], [], []
        out_shape = jax.ShapeDtypeStruct((m, n) if out_layers is None else (out_layers, m, n), out_dtype)
        out_specs = spec((tm, tn), lambda i, j, kk: (i, j), out_layer)
    else:
        fn, rows, whole, outs, sums = post
        in_specs += [bs((tm, r.shape[-1] // gj), lambda i, j, kk: (i, j), late=True) for r in rows]
        in_specs += [pl.BlockSpec(tuple(w.shape), _zeros_index(w.ndim, len(grid))) for w in whole]
        operands += list(rows) + list(whole)
        out_shape = [jax.ShapeDtypeStruct(sh, dt) for sh, dt in list(outs) + list(sums)]
        out_specs = ([bs((tm, sh[-1] // gj), lambda i, j, kk: (i, j), late=True) for sh, _ in outs]
                     + [pl.BlockSpec(tuple(sh), _zeros_index(len(sh), len(grid))) for sh, _ in sums])
    n_in, n_extra, n_outs, n_sums = len(operands), len(rows) + len(whole), len(outs), len(sums)

    def body(*refs):
        a_ref, b_ref = refs[0], refs[1]
        pos = 2
        bias_ref = add_ref = None
        if has_bias:
            bias_ref = refs[pos]
            pos += 1
        if has_add:
            add_ref = refs[pos]
            pos += 1
        extra_refs = refs[n_in - n_extra:n_in]
        out_refs = refs[n_in:n_in + max(n_outs, 1)]
        sum_refs = refs[n_in + n_outs:n_in + n_outs + n_sums]
        acc_ref = refs[-1] if nk > 1 else None
        if keep_at:
            at_ref = refs[-1]

            @pl.when(pl.program_id(1) == 0)
            def _():
                at_ref[...] = a_ref[...].astype(BF16).T

            part = lax.dot_general(at_ref[...], b_ref[...].astype(BF16), (((1,), (1 if tb else 0,)), ((), ())),
                                   preferred_element_type=F32)
        else:
            part = lax.dot_general(a_ref[...].astype(BF16), b_ref[...].astype(BF16), dims,
                                   preferred_element_type=F32)

        def finish(total):
            if has_bias:
                total = total + bias_ref[...]
            if has_add:
                total = total + add_ref[...]
            if fn is None:
                out_refs[0][...] = total.astype(out_refs[0].dtype)
                return
            res = fn(total, *[r[...] for r in extra_refs])
            for ref, val in zip(out_refs, res[:n_outs]):
                ref[...] = val.astype(ref.dtype)
            if n_sums:
                @pl.when(pl.program_id(1 if k_outer or not rows_outer else 0) == 0)
                def _():
                    for ref in sum_refs:
                        ref[...] = jnp.zeros(ref.shape, ref.dtype)

                for ref, val in zip(sum_refs, res[n_outs:]):
                    ref[...] += val

        if nk == 1:
            finish(part)
        elif k_outer:
            kk = pl.program_id(0)
            rows_i = pl.ds(pl.multiple_of(pl.program_id(1) * tm, tm), tm)

            @pl.when(kk == 0)
            def _():
                acc_ref[rows_i, :] = part

            @pl.when(kk > 0)
            def _():
                acc_ref[rows_i, :] += part

            @pl.when(kk == nk - 1)
            def _():
                finish(acc_ref[rows_i, :])
        else:
            kk = pl.program_id(2)

            @pl.when(kk == 0)
            def _():
                acc_ref[...] = part

            @pl.when(kk > 0)
            def _():
                acc_ref[...] += part

            @pl.when(kk == nk - 1)
            def _():
                finish(acc_ref[...])

    return pl.pallas_call(
        body, name=name, grid=grid, in_specs=in_specs, out_specs=out_specs, out_shape=out_shape,
        scratch_shapes=([pltpu.VMEM((m, n) if k_outer else (tm, tn), F32)] if nk > 1
                        else [pltpu.VMEM((tm, tk), BF16)] if keep_at else []),
        compiler_params=_params(("arbitrary", "arbitrary") if k_outer
                                else ("arbitrary" if n_sums else "parallel", "arbitrary" if keep_at else "parallel",
                                      "arbitrary") if rows_outer
                                else ("parallel", "arbitrary" if n_sums else "parallel", "arbitrary")),
    )(*operands)


def _rowwise(fn, rows, whole, outs, sums=(), *, name, tm=256):
    m = rows[0].shape[-2]
    tm = _pick_rows(m, tm)
    n_rows, n_whole, n_outs, n_sums = len(rows), len(whole), len(outs), len(sums)

    def rspec(shape):
        lead = len(shape) - 2
        return pl.BlockSpec(tuple(shape[:-2]) + (tm, shape[-1]), lambda i: (0,) * lead + (i, 0))

    def wspec(shape):
        return pl.BlockSpec(tuple(shape), lambda i: (0,) * len(shape))

    def body(*refs):
        vals = [r[...] for r in refs[:n_rows + n_whole]]
        out_refs = refs[n_rows + n_whole:n_rows + n_whole + n_outs]
        sum_refs = refs[n_rows + n_whole + n_outs:]
        res = fn(*vals)
        for ref, val in zip(out_refs, res[:n_outs]):
            ref[...] = val.astype(ref.dtype)
        if n_sums:
            @pl.when(pl.program_id(0) == 0)
            def _():
                for ref in sum_refs:
                    ref[...] = jnp.zeros(ref.shape, ref.dtype)

            for ref, val in zip(sum_refs, res[n_outs:]):
                ref[...] += val

    result = pl.pallas_call(
        body, name=name, grid=(m // tm,),
        in_specs=[rspec(r.shape) for r in rows] + [wspec(w.shape) for w in whole],
        out_specs=[rspec(s) for s, _ in outs] + [wspec(s) for s, _ in sums],
        out_shape=[jax.ShapeDtypeStruct(s, d) for s, d in list(outs) + list(sums)],
        compiler_params=_params(("arbitrary",)),
    )(*rows, *whole)
    return result


def _sigmoid(v):
    return jax.nn.sigmoid(v)


def _col_sum(v):
    return jnp.sum(v, axis=0, keepdims=True)


def _ln_stats(z):
    mu = jnp.mean(z, axis=-1, keepdims=True)
    zc = z - mu
    var = jnp.mean(zc * zc, axis=-1, keepdims=True)
    rstd = lax.rsqrt(var + LN_EPS)
    return zc * rstd, rstd


def _ln_fwd_fn(xin, h, gain, bias):
    xhat, _ = _ln_stats(ALPHA * xin + h)
    y = xhat * gain + bias
    return y, y


def _ple_ln_fwd_fn(xin, pg, pu, gain, bias):
    xhat, _ = _ln_stats(ALPHA * xin + _sigmoid(pg) * pu)
    y = xhat * gain + bias
    return y, y


def _ln_dz(dy, z, gain):
    xhat, rstd = _ln_stats(z)
    dxhat = dy * gain
    dz = rstd * (dxhat - jnp.mean(dxhat, axis=-1, keepdims=True)
                 - xhat * jnp.mean(dxhat * xhat, axis=-1, keepdims=True))
    return dz, _col_sum(dy * xhat), _col_sum(dy)


def _ln_bwd_fn(dy, xin, h, gain):
    dz, dgain, dbias = _ln_dz(dy, ALPHA * xin + h, gain)
    return ALPHA * dz, dz, dgain, dbias, _col_sum(dz)


def _ple_ln_bwd_fn(dy, xin, pg, pu, gain):
    sg = _sigmoid(pg)
    dz, dgain, dbias = _ln_dz(dy, ALPHA * xin + sg * pu, gain)
    dpg = dz * pu * sg * (1.0 - sg)
    return ALPHA * dz, dpg, dz * sg, dgain, dbias, _col_sum(dpg)


def _swiglu_fwd_fn(gu):
    hid = gu.shape[-1] // 2
    gate, up = gu[:, :hid], gu[:, hid:]
    return gu, gate * _sigmoid(gate) * up


def _swiglu_bwd_fn(dact, gu):
    gu = gu.astype(F32)
    hid = gu.shape[-1] // 2
    gate, up = gu[:, :hid], gu[:, hid:]
    sg = _sigmoid(gate)
    dgate = dact * up * sg * (1.0 + gate * (1.0 - sg))
    dup = dact * gate * sg
    return (jnp.concatenate([dgate, dup], axis=-1),)


def _loss_fn(y, target):
    err = y - target
    inv = 1.0 / y.shape[-1]
    part = 0.5 * inv * jnp.sum(jnp.sum(err * err, axis=-1, keepdims=True), axis=0, keepdims=True)
    return err * inv, jnp.broadcast_to(part, (1, LANES))


def _adam_fn(w, mom, vel, p_own, p_sib):
    g = p_own.astype(F32) + p_sib.astype(F32)
    m_new = ADAM_B1 * mom + (1.0 - ADAM_B1) * g
    v_new = ADAM_B2 * vel + (1.0 - ADAM_B2) * (g * g)
    m_hat = m_new / (1.0 - ADAM_B1 ** ADAM_STEP)
    v_hat = v_new / (1.0 - ADAM_B2 ** ADAM_STEP)
    delta = -ADAM_LR * (m_hat / (jnp.sqrt(v_hat) + ADAM_EPS) + ADAM_WD * w)
    return g, delta, m_new, v_new


def _split2(x):
    hi = x.astype(BF16)
    return hi, (x - hi.astype(F32)).astype(BF16)


def _dot3(a, b, dims):
    a_hi, a_lo = _split2(a)
    b_hi, b_lo = _split2(b)
    dn = (dims, ((), ()))
    return (lax.dot_general(a_hi, b_hi, dn, preferred_element_type=F32)
            + (lax.dot_general(a_hi, b_lo, dn, preferred_element_type=F32)
               + lax.dot_general(a_lo, b_hi, dn, preferred_element_type=F32)))


def _tdot(mask01, b):
    m = mask01.astype(BF16)
    b_hi = b.astype(BF16)
    rest = b - b_hi.astype(F32)
    b_mid = rest.astype(BF16)
    b_lo = (rest - b_mid.astype(F32)).astype(BF16)
    dn = (((1,), (0,)), ((), ()))
    return (lax.dot_general(m, b_hi, dn, preferred_element_type=F32)
            + (lax.dot_general(m, b_mid, dn, preferred_element_type=F32)
               + lax.dot_general(m, b_lo, dn, preferred_element_type=F32)))


def _hdot(a, b):
    return _dot3(a, b, ((1,), (0,)))


def _hdot_nt(a, b):
    return _dot3(a, b, ((1,), (1,)))


def _hdot_tn(a, b):
    return _dot3(a, b, ((0,), (0,)))


def _dot(a, b):
    return lax.dot_general(a.astype(BF16), b.astype(BF16), (((1,), (0,)), ((), ())), preferred_element_type=F32)


def _dot_nt(a, b):
    return lax.dot_general(a.astype(BF16), b.astype(BF16), (((1,), (1,)), ((), ())), preferred_element_type=F32)


def _dot_tn(a, b):
    return lax.dot_general(a.astype(BF16), b.astype(BF16), (((0,), (0,)), ((), ())), preferred_element_type=F32)


def _hg_masks():
    c = HG_CHUNK
    row = lax.broadcasted_iota(jnp.int32, (c, c), 0)
    col = lax.broadcasted_iota(jnp.int32, (c, c), 1)
    base = row & (-HG_SUB)
    return row, col, base, col <= row, col < base


def _hg_gates(qr, fr, alb):
    lbound = _sigmoid(alb[0:1, :] - alb[1:2, :])
    sig = _sigmoid(fr)
    forget = lbound + (1.0 - lbound) * sig
    kk = (1.0 - lbound) * _sigmoid(-fr)
    qt = qr * _sigmoid(qr) * (HG_DK ** -0.5)
    return qt, kk, jnp.log(forget), lbound, sig, forget


def _hg_scores(qt, kk, g, scores=True):
    c, nsub = HG_CHUNK, HG_CHUNK // HG_SUB
    row, col, base, causal, below = _hg_masks()
    b = _tdot(causal, g)
    rr = _tdot(below, g)
    bq = b - rr
    qh = qt * jnp.exp(bq)
    edecs = [None]
    parts = [jnp.zeros((HG_SUB, c), F32)]
    for i in range(1, nsub):
        edec = jnp.exp(jnp.minimum(rr[i * HG_SUB:i * HG_SUB + 1, :] - b, 0.0))
        edecs.append(edec)
        if scores:
            parts.append(_dot_nt(qh[i * HG_SUB:(i + 1) * HG_SUB, :], kk * edec))
    b3 = b.reshape(nsub, HG_SUB, HG_DK)
    q3 = qt.reshape(nsub, HG_SUB, HG_DK)
    k3 = kk.reshape(nsub, HG_SUB, HG_DK)
    if not scores:
        return None, b, bq, qh, edecs, (b3, q3, k3)
    a = jnp.where(below, jnp.concatenate(parts, axis=0), 0.0)
    for j in range(HG_SUB):
        e = jnp.exp(b3 - b3[:, j:j + 1, :])
        colv = jnp.sum(q3 * e * k3[:, j:j + 1, :], axis=-1, keepdims=True).reshape(c, 1)
        a = jnp.where(col == base + j, colv, a)
    a = jnp.where(causal, a, 0.0)
    return a, b, bq, qh, edecs, (b3, q3, k3)


def _hg_norm(o, gr, gain):
    r = lax.rsqrt(jnp.mean(o * o, axis=-1, keepdims=True) + RMS_EPS)
    sg = _sigmoid(gr)
    return o * r * gain, r, sg


def _hgrn2_fwd(proj, alb, gain, *, rb):
    m, d4 = proj.shape
    d = d4 // 4
    heads = d // HG_DK
    hp = HG_HEADS_PER_STEP
    rb = min(rb, m)
    cpb = rb // HG_CHUNK
    nrb = m // rb

    def body(q_ref, f_ref, v_ref, g_ref, alb_ref, gain_ref, o_ref, og_ref, st_ref, a_ref, state):
        @pl.when(pl.program_id(1) == 0)
        def _():
            state[...] = jnp.zeros(state.shape, F32)

        def chunk(ci, carry):
            sl = pl.ds(pl.multiple_of(ci * HG_CHUNK, HG_CHUNK), HG_CHUNK)
            for u in range(hp):
                ln = slice(u * HG_DK, (u + 1) * HG_DK)
                qt, kk, g, _, _, _ = _hg_gates(q_ref[sl, ln], f_ref[sl, ln], alb_ref[:, ln])
                v = v_ref[sl, ln]
                st = state[u]
                st_ref[u, ci] = st
                a, b, _, _, _, _ = _hg_scores(qt, kk, g)
                a_ref[u, ci] = a.astype(a_ref.dtype)
                o = _dot(a, v) + _dot_nt(qt * jnp.exp(b), st)
                b_last = b[HG_CHUNK - 1:HG_CHUNK, :]
                state[u] = st * jnp.exp(b_last) + _hdot_tn(v, kk * jnp.exp(b_last - b))
                o_ref[sl, ln] = o
                n, _, sg = _hg_norm(o, g_ref[sl, ln], gain_ref[...])
                og_ref[sl, ln] = (n * g_ref[sl, ln] * sg).astype(og_ref.dtype)
            return carry

        lax.fori_loop(0, cpb, chunk, 0)

    def col(cidx):
        return pl.BlockSpec((rb, hp * HG_DK), lambda h, r: (r, cidx * (heads // hp) + h))

    return pl.pallas_call(
        body, name="hgrn2_fwd", grid=(heads // hp, nrb),
        in_specs=[col(0), col(1), col(2), col(3),
                  pl.BlockSpec((2, hp * HG_DK), lambda h, r: (0, h)),
                  pl.BlockSpec((1, HG_DK), lambda h, r: (0, 0))],
        out_specs=[pl.BlockSpec((rb, hp * HG_DK), lambda h, r: (r, h)),
                   pl.BlockSpec((rb, hp * HG_DK), lambda h, r: (r, h)),
                   pl.BlockSpec((hp, cpb, HG_DK, HG_DK), lambda h, r: (h, r, 0, 0)),
                   pl.BlockSpec((hp, cpb, HG_CHUNK, HG_CHUNK), lambda h, r: (h, r, 0, 0))],
        out_shape=[jax.ShapeDtypeStruct((m, d), F32), jax.ShapeDtypeStruct((m, d), BF16),
                   jax.ShapeDtypeStruct((heads, m // HG_CHUNK, HG_DK, HG_DK), F32),
                   jax.ShapeDtypeStruct((heads, m // HG_CHUNK, HG_CHUNK, HG_CHUNK), BF16)],
        scratch_shapes=[pltpu.VMEM((hp, HG_DK, HG_DK), F32)],
        compiler_params=_params(("parallel", "arbitrary")),
    )(proj, proj, proj, proj, alb, gain)


def _hgrn2_bwd(proj, o_pre, states, scores, dog, alb, gain, *, rb):
    m, d4 = proj.shape
    d = d4 // 4
    heads = d // HG_DK
    rb = min(rb, m)
    cpb = rb // HG_CHUNK
    nrb = m // rb
    c, nsub = HG_CHUNK, HG_CHUNK // HG_SUB

    def body(q_ref, f_ref, v_ref, g_ref, o_ref, st_ref, a_ref, dog_ref, alb_ref, gain_ref,
             dq_ref, df_ref, dv_ref, dg_ref, dalb_ref, dgain_ref, dstate, carry_ref):
        first = (pl.program_id(0) == 0) & (pl.program_id(1) == 0)

        @pl.when(first)
        def _():
            dgain_ref[...] = jnp.zeros(dgain_ref.shape, F32)

        @pl.when(pl.program_id(1) == 0)
        def _():
            dstate[...] = jnp.zeros(dstate.shape, F32)
            carry_ref[...] = jnp.zeros(carry_ref.shape, F32)
            dalb_ref[...] = jnp.zeros(dalb_ref.shape, F32)

        row, col, base, causal, below = _hg_masks()
        sub_iota = lax.broadcasted_iota(jnp.int32, (nsub, HG_SUB, HG_DK), 1)
        row_k = lax.broadcasted_iota(jnp.int32, (c, HG_DK), 0)
        upper = col >= row

        def chunk(step, carry):
            ci = cpb - 1 - step
            sl = pl.ds(pl.multiple_of(ci * HG_CHUNK, HG_CHUNK), HG_CHUNK)
            qr, fr, v, gr = q_ref[sl, :], f_ref[sl, :], v_ref[sl, :], g_ref[sl, :]
            qt, kk, g, lbound, sig, forget = _hg_gates(qr, fr, alb_ref[...])
            o = o_ref[sl, :]
            dogv = dog_ref[sl, :]
            gain_v = gain_ref[...]
            n, r, sg = _hg_norm(o, gr, gain_v)
            dgr = dogv * n * sg * (1.0 + gr * (1.0 - sg))
            dn = dogv * gr * sg
            dgain_ref[...] += _col_sum(dn * o * r)
            u = dn * gain_v
            d_o = r * u - o * (r * r * r) * jnp.mean(u * o, axis=-1, keepdims=True)
            st0 = st_ref[ci]
            dst = dstate[...]
            _, b, bq, qh, edecs, (b3, q3, k3) = _hg_scores(qt, kk, g, scores=False)
            a = a_ref[ci]
            eb = jnp.exp(b)
            b_last = b[c - 1:c, :]
            kdl_dec = jnp.exp(b_last - b)
            kdl = kk * kdl_dec
            d_a = jnp.where(causal, _dot_nt(d_o, v), 0.0)
            d_at = _dot_nt(v, d_o)
            dv = _dot_tn(a, d_o) + _dot_nt(kdl, dst)
            dq = eb * _hdot(d_o, st0)
            dk = _hdot(v, dst) * kdl_dec
            d_a_below = jnp.where(below, d_a, 0.0)
            dq_parts = [jnp.zeros((HG_SUB, HG_DK), F32)]
            for i in range(1, nsub):
                lo, hi = i * HG_SUB, (i + 1) * HG_SUB
                dq_parts.append(_hdot(d_a_below[lo:hi, :], kk * edecs[i]))
                gi = _hdot(d_at[:, lo:hi], qh[lo:hi, :])
                dk = dk + jnp.where(row_k < lo, edecs[i] * gi, 0.0)
            dq = dq + jnp.concatenate(dq_parts, axis=0) * jnp.exp(bq)
            dq3 = jnp.zeros((nsub, HG_SUB, HG_DK), F32)
            dk3 = jnp.zeros((nsub, HG_SUB, HG_DK), F32)
            d_diag = jnp.concatenate([d_a[i * HG_SUB:(i + 1) * HG_SUB, i * HG_SUB:(i + 1) * HG_SUB]
                                      for i in range(nsub)], axis=0).reshape(nsub, HG_SUB, HG_SUB)
            for j in range(HG_SUB):
                e = jnp.exp(jnp.minimum(b3 - b3[:, j:j + 1, :], 0.0))
                t1 = d_diag[:, :, j:j + 1] * e
                dq3 = dq3 + t1 * k3[:, j:j + 1, :]
                dk3 = jnp.where(sub_iota == j, jnp.sum(t1 * q3, axis=1, keepdims=True), dk3)
            dq = dq + dq3.reshape(c, HG_DK)
            dk = dk + dk3.reshape(c, HG_DK)
            dstate[...] = dst * jnp.exp(b_last) + _hdot_tn(d_o, qt * eb)
            dglog = _tdot(upper, qt * dq - kk * dk) + carry_ref[...]
            carry_ref[...] = dglog[0:1, :]
            dforget = dglog / forget
            one_m_lb = 1.0 - lbound
            dsig = (dforget - dk) * one_m_lb
            sneg = _sigmoid(-fr)
            dlb = _col_sum(dforget * (1.0 - sig) - dk * sneg)
            dalb0 = dlb * lbound * one_m_lb
            dalb_ref[...] += jnp.concatenate([dalb0, -dalb0], axis=0)
            sq = _sigmoid(qr)
            dq_ref[sl, :] = (dq * (HG_DK ** -0.5) * sq * (1.0 + qr * (1.0 - sq))).astype(dq_ref.dtype)
            df_ref[sl, :] = (dsig * sig * (1.0 - sig)).astype(df_ref.dtype)
            dv_ref[sl, :] = dv.astype(dv_ref.dtype)
            dg_ref[sl, :] = dgr.astype(dg_ref.dtype)
            return carry

        lax.fori_loop(0, cpb, chunk, 0, unroll=2)

    def rev(r):
        return nrb - 1 - r

    def col(cidx):
        return pl.BlockSpec((rb, HG_DK), lambda h, r: (rev(r), cidx * heads + h))

    def head_rows():
        return pl.BlockSpec((rb, HG_DK), lambda h, r: (rev(r), h))

    return pl.pallas_call(
        body, name="hgrn2_bwd", grid=(heads, nrb),
        in_specs=[col(0), col(1), col(2), col(3), head_rows(),
                  pl.BlockSpec((None, cpb, HG_DK, HG_DK), lambda h, r: (h, rev(r), 0, 0)),
                  pl.BlockSpec((None, cpb, HG_CHUNK, HG_CHUNK), lambda h, r: (h, rev(r), 0, 0)),
                  head_rows(),
                  pl.BlockSpec((2, HG_DK), lambda h, r: (0, h)),
                  pl.BlockSpec((1, HG_DK), lambda h, r: (0, 0))],
        out_specs=[head_rows(), head_rows(), head_rows(), head_rows(),
                   pl.BlockSpec((2, HG_DK), lambda h, r: (0, h)),
                   pl.BlockSpec((1, HG_DK), lambda h, r: (0, 0))],
        out_shape=[jax.ShapeDtypeStruct((m, d), BF16)] * 4
                  + [jax.ShapeDtypeStruct((2, d), F32), jax.ShapeDtypeStruct((1, HG_DK), F32)],
        scratch_shapes=[pltpu.VMEM((HG_DK, HG_DK), F32), pltpu.VMEM((1, HG_DK), F32)],
        compiler_params=_params(("arbitrary", "arbitrary")),
    )(proj, proj, proj, proj, o_pre, states, scores, dog, alb, gain)


def _swa_probs(qh, kp, kc, sink, slope, has_prev):
    rows = qh.shape[0]
    qi = lax.broadcasted_iota(jnp.int32, (rows, WINDOW), 0) & (WINDOW - 1)
    si = lax.broadcasted_iota(jnp.int32, (rows, WINDOW), 1)
    scale = ATT_HD ** -0.5
    dist_c = (qi - si).astype(F32)
    s_p = _dot_nt(qh, kp) * scale - slope * (dist_c + float(WINDOW))
    s_c = _dot_nt(qh, kc) * scale - slope * dist_c
    s_p = jnp.where((si > qi) & has_prev, s_p, NEG)
    s_c = jnp.where(si <= qi, s_c, NEG)
    mx = jnp.maximum(jnp.maximum(jnp.max(s_p, axis=-1, keepdims=True), jnp.max(s_c, axis=-1, keepdims=True)), sink)
    e_p, e_c, e_s = jnp.exp(s_p - mx), jnp.exp(s_c - mx), jnp.exp(sink - mx)
    inv = 1.0 / (jnp.sum(e_p, axis=-1, keepdims=True) + jnp.sum(e_c, axis=-1, keepdims=True) + e_s)
    return e_p * inv, e_c * inv, e_s * inv


def _slope(h, n_heads):
    return float(2.0 ** (-8.0 * (h + 1) / n_heads))


def _swa_group(ref_vals, sink_ref, kh, n_heads):
    heads = [kh * ATT_G + g for g in range(ATT_G)]
    stacked = [jnp.concatenate([v[:, h * ATT_HD:(h + 1) * ATT_HD] for h in heads], axis=0) for v in ref_vals]
    grp = lax.shift_right_logical(lax.broadcasted_iota(jnp.int32, (ATT_G * WINDOW, 1), 0), WINDOW.bit_length() - 1)
    slope = jnp.zeros((ATT_G * WINDOW, 1), F32)
    sink = jnp.zeros((ATT_G * WINDOW, 1), F32)
    for g, h in enumerate(heads):
        slope = jnp.where(grp == g, _slope(h, n_heads), slope)
        sink = jnp.where(grp == g, sink_ref[:, h:h + 1], sink)
    return stacked, slope, sink


def _swa_fwd(q, kv, sinks):
    m, d = q.shape
    n_heads = d // ATT_HD
    kvh = n_heads // ATT_G
    kd = kvh * ATT_HD
    nb = m // WINDOW

    def body(q_ref, kvp_ref, kvc_ref, sink_ref, o_ref):
        has_prev = pl.program_id(0) > 0
        qv, kvp, kvc = q_ref[...], kvp_ref[...], kvc_ref[...]
        outs = []
        for kh in range(kvh):
            ks = slice(kh * ATT_HD, (kh + 1) * ATT_HD)
            vs = slice(kd + kh * ATT_HD, kd + (kh + 1) * ATT_HD)
            (q4,), slope, sink = _swa_group([qv], sink_ref, kh, n_heads)
            p_p, p_c, _ = _swa_probs(q4, kvp[:, ks], kvc[:, ks], sink, slope, has_prev)
            o4 = _dot(p_p, kvp[:, vs]) + _dot(p_c, kvc[:, vs])
            outs += [o4[g * WINDOW:(g + 1) * WINDOW, :] for g in range(ATT_G)]
        o_ref[...] = jnp.concatenate(outs, axis=-1).astype(o_ref.dtype)

    return pl.pallas_call(
        body, name="swa_fwd", grid=(nb,),
        in_specs=[pl.BlockSpec((WINDOW, d), lambda n: (n, 0)),
                  pl.BlockSpec((WINDOW, 2 * kd), lambda n: (jnp.maximum(n - 1, 0), 0)),
                  pl.BlockSpec((WINDOW, 2 * kd), lambda n: (n, 0)),
                  pl.BlockSpec((1, n_heads), lambda n: (0, 0))],
        out_specs=pl.BlockSpec((WINDOW, d), lambda n: (n, 0)),
        out_shape=jax.ShapeDtypeStruct((m, d), BF16),
        compiler_params=_params(("arbitrary",)),
    )(q, kv, kv, sinks)


def _swa_bwd(q, kv, sinks, dao):
    m, d = q.shape
    n_heads = d // ATT_HD
    kvh = n_heads // ATT_G
    kd = kvh * ATT_HD
    nb = m // WINDOW
    scale = ATT_HD ** -0.5

    def body(q_ref, kvp_ref, kvc_ref, sink_ref, do_ref, dq_ref, dkvc_ref, dkvp_ref, dqsum_ref, dsink_ref):
        @pl.when(pl.program_id(0) == 0)
        def _():
            dqsum_ref[...] = jnp.zeros(dqsum_ref.shape, F32)
            dsink_ref[...] = jnp.zeros(dsink_ref.shape, F32)

        has_prev = pl.program_id(0) > 0
        qv, kvp, kvc, dov = q_ref[...], kvp_ref[...], kvc_ref[...], do_ref[...]
        lane_h = lax.broadcasted_iota(jnp.int32, (1, n_heads), 1)
        dsink = jnp.zeros((1, n_heads), F32)
        dq_parts, dk_p, dk_c, dv_p, dv_c = [], [], [], [], []
        for kh in range(kvh):
            ks = slice(kh * ATT_HD, (kh + 1) * ATT_HD)
            vs = slice(kd + kh * ATT_HD, kd + (kh + 1) * ATT_HD)
            kp, kc, vp, vc = kvp[:, ks], kvc[:, ks], kvp[:, vs], kvc[:, vs]
            (q4, do4), slope, sink = _swa_group([qv, dov], sink_ref, kh, n_heads)
            p_p, p_c, p_s = _swa_probs(q4, kp, kc, sink, slope, has_prev)
            dp_p, dp_c = _dot_nt(do4, vp), _dot_nt(do4, vc)
            delta = jnp.sum(p_p * dp_p, axis=-1, keepdims=True) + jnp.sum(p_c * dp_c, axis=-1, keepdims=True)
            ds_p, ds_c = p_p * (dp_p - delta), p_c * (dp_c - delta)
            sink_term = p_s * delta
            dq4 = (_dot(ds_p, kp) + _dot(ds_c, kc)) * scale
            for g in range(ATT_G):
                rows = slice(g * WINDOW, (g + 1) * WINDOW)
                dsink = dsink + jnp.where(lane_h == kh * ATT_G + g, -_col_sum(sink_term[rows, :]), 0.0)
                dq_parts.append(dq4[rows, :])
            dk_p.append(_dot_tn(ds_p, q4) * scale)
            dk_c.append(_dot_tn(ds_c, q4) * scale)
            dv_p.append(_dot_tn(p_p, do4))
            dv_c.append(_dot_tn(p_c, do4))
        dq = jnp.concatenate(dq_parts, axis=-1)
        dq_ref[...] = dq.astype(dq_ref.dtype)
        dqsum_ref[...] += _col_sum(dq)
        dsink_ref[...] += dsink
        dkvc_ref[...] = jnp.concatenate(dk_c + dv_c, axis=-1)
        dkvp_ref[...] = jnp.concatenate(dk_p + dv_p, axis=-1)

    return pl.pallas_call(
        body, name="swa_bwd", grid=(nb,),
        in_specs=[pl.BlockSpec((WINDOW, d), lambda n: (n, 0)),
                  pl.BlockSpec((WINDOW, 2 * kd), lambda n: (jnp.maximum(n - 1, 0), 0)),
                  pl.BlockSpec((WINDOW, 2 * kd), lambda n: (n, 0)),
                  pl.BlockSpec((1, n_heads), lambda n: (0, 0)),
                  pl.BlockSpec((WINDOW, d), lambda n: (n, 0))],
        out_specs=[pl.BlockSpec((WINDOW, d), lambda n: (n, 0)),
                   pl.BlockSpec((WINDOW, 2 * kd), lambda n: (n, 0)),
                   pl.BlockSpec((WINDOW, 2 * kd), lambda n: (n, 0)),
                   pl.BlockSpec((1, d), lambda n: (0, 0)),
                   pl.BlockSpec((1, n_heads), lambda n: (0, 0))],
        out_shape=[jax.ShapeDtypeStruct((m, d), BF16), jax.ShapeDtypeStruct((m, 2 * kd), F32),
                   jax.ShapeDtypeStruct((m, 2 * kd), F32), jax.ShapeDtypeStruct((1, d), F32),
                   jax.ShapeDtypeStruct((1, n_heads), F32)],
        compiler_params=_params(("arbitrary",)),
    )(q, kv, kv, sinks, dao)


def _kv_grad_combine(dkv_cur, dkv_prev):
    m, w = dkv_cur.shape
    nb = m // WINDOW

    def body(cur_ref, nxt_ref, o_ref, sum_ref):
        @pl.when(pl.program_id(0) == 0)
        def _():
            sum_ref[...] = jnp.zeros(sum_ref.shape, F32)

        total = cur_ref[...] + jnp.where(pl.program_id(0) < nb - 1, nxt_ref[...], 0.0)
        o_ref[...] = total.astype(o_ref.dtype)
        sum_ref[...] += _col_sum(total)

    return pl.pallas_call(
        body, name="kv_grad_combine", grid=(nb,),
        in_specs=[pl.BlockSpec((WINDOW, w), lambda n: (n, 0)),
                  pl.BlockSpec((WINDOW, w), lambda n: (jnp.minimum(n + 1, nb - 1), 0))],
        out_specs=[pl.BlockSpec((WINDOW, w), lambda n: (n, 0)), pl.BlockSpec((1, w), lambda n: (0, 0))],
        out_shape=[jax.ShapeDtypeStruct((m, w), BF16), jax.ShapeDtypeStruct((1, w), F32)],
        compiler_params=_params(("arbitrary",)),
    )(dkv_cur, dkv_prev)


def _row(v):
    return v.reshape(1, -1)


def _local_step(x, p, target, wget, grad_sink, ln_gain, ln_bias, alb, norm_gain, kv_b, b_q, sinks, b_out, ple_b):
    gs = {}
    gains = ln_gain.reshape(DEPTH * 3, -1)
    biases = ln_bias.reshape(DEPTH * 3, -1)
    sd = x.shape
    pending = [None]

    def mm(a, b, lb=0, **kw):
        after, pending[0] = pending[0], None
        return _mm(a, b, lb=lb, after=after, **kw)

    def mm_ln(a, wt, xin, i, j, nm, bias=None, pu=None):
        r = 3 * i + j
        if pu is None:
            fn, rows = (lambda h, xv, g, bv: (h,) + _ln_fwd_fn(xv, h, g[r:r + 1], bv[r:r + 1])), [xin]
        else:
            fn = lambda h, xv, puv, g, bv: (h,) + _ple_ln_fwd_fn(xv, h, puv, g[r:r + 1], bv[r:r + 1])
            rows = [xin, pu]
        h, y, yb = _mm(a, wt, lb=0, bias=bias, name=nm,
                       post=(fn, rows, [gains, biases], [(sd, F32), (sd, F32), (sd, BF16)], []))
        return h, (y, yb)

    def mm_ln_bwd(a, wt, add, xin, h, i, j, nm):
        r = 3 * i + j
        dx_part, dh, dg, db, dhsum = mm(a, wt, tb=True, add=add, name=nm,
                                        post=(lambda dy, xv, hv, g: _ln_bwd_fn(dy, xv, hv, g[r:r + 1]), [xin, h],
                                              [gains], [(sd, F32), (sd, BF16)], [((1, sd[1]), F32)] * 3))
        gs[f"ln_gain_{i}_{j}"], gs[f"ln_bias_{i}_{j}"] = dg, db
        return dx_part, dh, dhsum

    def tail_fwd(xa, i):
        wgu = wget("ffn_w_gate_up", i, xa[1])
        hid2 = wgu.shape[-1]
        gu, act = _mm(xa[1], wgu, lb=0, name=f"ffn_up_swiglu{i}", tile_cols=hid2 // 2,
                      post=(_swiglu_fwd_fn, [], [], [((sd[0], hid2), BF16), ((sd[0], hid2 // 2), BF16)], []))
        f, xb = mm_ln(act, wget("ffn_w_down", i, act), xa[0], i, 1, f"ffn_down_ln{i}")
        pu = _mm(p, wget("ple_w_up", i, act), la=i, lb=0, name=f"ple_up{i}")
        pg, xc = mm_ln(xb[1], wget("ple_w_gate", i, act), xb[0], i, 2, f"ple_gate_ln{i}", bias=_row(ple_b[i]), pu=pu)
        return dict(xa=xa, gu=gu, act=act, f=f, xb=xb, pg=pg, pu=pu), xc

    def tail_bwd(head, sv, i, mix_in, mix_h):
        xa, xb = sv["xa"], sv["xb"]
        r = 3 * i + 2
        dxb_part, dpg, dpu, dg2, db2, dbg = head(
            lambda dy, xv, pgv, puv, g: _ple_ln_bwd_fn(dy, xv, pgv, puv, g[r:r + 1]), [xb[0], sv["pg"], sv["pu"]],
            [gains], [(sd, F32), (sd, BF16), (sd, BF16)], [((1, sd[1]), F32)] * 3)[:6]
        gs[f"ple_b_{i}"] = dbg
        gs[f"ln_gain_{i}_2"], gs[f"ln_bias_{i}_2"] = dg2, db2
        grad_of("ple_w_gate", i, xb[1], dpg)
        grad_of("ple_w_up", i, p, dpu, la=i)
        dxa_part, df, _ = mm_ln_bwd(dpg, wget("ple_w_gate", i, None), dxb_part, xa[0], sv["f"], i, 1,
                                    f"ple_gate_dx_ln{i}")
        grad_of("ffn_w_down", i, sv["act"], df)
        gu = sv["gu"]
        dgu, = mm(df, wget("ffn_w_down", i, None), tb=True, name=f"ffn_down_dx_swiglu{i}", tile_cols=gu.shape[1] // 4,
                  post=(_swiglu_bwd_fn, [gu], [], [(gu.shape, BF16)], []))
        grad_of("ffn_w_gate_up", i, xa[1], dgu)
        return mm_ln_bwd(dgu, wget("ffn_w_gate_up", i, None), dxa_part, mix_in, mix_h, i, 0, f"ffn_up_dx_ln{i}")

    def grad_of(nm, i, act, dout, la=None):
        grad = mm(act, dout, la=la, lb=None, ta=True, out_dtype=BF16, out_layers=1, out_layer=0,
                  name=f"grad_{nm}{i}")
        token = grad_sink(nm, i, grad)
        if token is not None:
            pending[0] = token

    proj = _mm(x, wget("a_w_in", 0, None), lb=0, name="hg_proj")
    o_pre, og, states, scores = _hgrn2_fwd(proj, alb, norm_gain, rb=HG_ROWS)
    h0, x1 = mm_ln(og, wget("a_w_out", 0, og), x, 0, 0, "hg_out_ln")
    sv0, x3 = tail_fwd(x1, 0)
    kv = _mm(x3[1], wget("kv_w", 0, x3[1]), lb=0, bias=_row(kv_b), out_dtype=BF16, name="kv_proj")
    q = _mm(x3[1], wget("b_w_q", 0, x3[1]), lb=0, bias=b_q, out_dtype=BF16, name="q_proj")
    ao = _swa_fwd(q, kv, sinks)
    h1, x4 = mm_ln(ao, wget("b_w_out", 0, x3[1]), x3[0], 1, 0, "att_out_ln", bias=b_out)
    sv1, y = tail_fwd(x4, 1)

    loss_box = []

    def loss_head(fn, rows, whole, outs, sums):
        def with_loss(yv, tv, *rest):
            dy, part = _loss_fn(yv, tv)
            return fn(dy, *rest) + (part,)

        res = _rowwise(with_loss, [y[0], target] + rows, whole, outs, list(sums) + [((1, LANES), F32)],
                       name="loss_ln_ple_bwd1")
        loss_box.append(res[-1])
        return res

    dx3_part, dh1, dh1sum = tail_bwd(loss_head, sv1, 1, x3[0], h1)
    loss = loss_box[0]
    gs["b_out"] = dh1sum
    grad_of("b_w_out", 0, ao, dh1)
    dao = mm(dh1, wget("b_w_out", 0, None), tb=True, out_dtype=BF16, name="att_out_dx")
    dq, dkv_cur, dkv_prev, dqsum, dsinks = _swa_bwd(q, kv, sinks, dao)
    gs["b_q"], gs["sinks"] = dqsum, dsinks
    dkv, dkvsum = _kv_grad_combine(dkv_cur, dkv_prev)
    gs["kv_b"] = dkvsum
    grad_of("b_w_q", 0, x3[1], dq)
    grad_of("kv_w", 0, x3[1], dkv)
    dx3 = mm(dq, wget("b_w_q", 0, None), tb=True, add=dx3_part, name="q_proj_dx")

    def kv_head(*post):
        return mm(dkv, wget("kv_w", 0, None), tb=True, add=dx3, name="kv_proj_dx_ln_ple_bwd0", post=post)

    dx_part, dh0, _ = tail_bwd(kv_head, sv0, 0, x, h0)
    grad_of("a_w_out", 0, og, dh0)
    dog = mm(dh0, wget("a_w_out", 0, None), tb=True, name="hg_out_dx")
    dqr, dfr, dvr, dgr, dalb, dgain = _hgrn2_bwd(proj, o_pre, states, scores, dog, alb, norm_gain, rb=HG_ROWS)
    gs["alb"], gs["norm_gain"] = dalb, dgain
    dproj = jnp.concatenate([dqr, dfr, dvr, dgr], axis=1)
    grad_of("a_w_in", 0, x, dproj)
    grad_x = mm(dproj, wget("a_w_in", 0, None), tb=True, add=dx_part, name="hg_proj_dx")
    return loss, grad_x, gs


HBM_SPEC = pl.BlockSpec(memory_space=pl.ANY)
HBM_ONLY = pl.BlockSpec(memory_space=pltpu.HBM)
SEM_SPEC = pl.BlockSpec(memory_space=pltpu.SEMAPHORE)
SIDE_EFFECT = pltpu.SideEffectType.DATAFLOW_SIDE_EFFECTING


def _slot(kind, j):
    return (j % 2) * 2 + j // 2 if kind == "colp" else j


def _piece(ref, kind, j):
    _, r, c = ref.shape
    if kind == "row":
        return ref.at[:, pl.ds(j * (r // N_CHIPS), r // N_CHIPS), :]
    return ref.at[:, :, pl.ds(_slot(kind, j) * (c // N_CHIPS), c // N_CHIPS)]


def _piece_dyn(ref, kind, j):
    _, r, c = ref.shape
    if kind == "row":
        return ref.at[:, pl.ds(pl.multiple_of(j * (r // N_CHIPS), 16), r // N_CHIPS), :]
    return ref.at[:, :, pl.ds(pl.multiple_of(_slot(kind, j) * (c // N_CHIPS), LANES), c // N_CHIPS)]


def _chip_of(j, c):
    return (j // 2, j % 2, c)


def _in_hbm(a):
    return pltpu.with_memory_space_constraint(a, pltpu.HBM)


def _place(src, layer, kind, chip, *, name, out_dtype, after=None):
    _, r, c = src.shape
    out_shape = (1, r * N_CHIPS, c) if kind == "row" else (1, r, c * N_CHIPS)
    tm = _pick_rows(r, 512)
    nb = r // tm

    def full_idx(i, chip_ref):
        return (0, chip_ref[0] * nb + i, 0) if kind == "row" else (0, i, _slot(kind, chip_ref[0]))

    in_specs, operands = [pl.BlockSpec((None, tm, c), lambda i, chip_ref: (layer, i, 0))], [src]
    if after is not None:
        in_specs.append(HBM_SPEC)
        operands.append(after)

    def body(chip_ref, src_ref, *rest):
        rest[-1][...] = src_ref[...].astype(rest[-1].dtype)

    return pl.pallas_call(
        body, name=name,
        grid_spec=pltpu.PrefetchScalarGridSpec(num_scalar_prefetch=1, grid=(nb,), in_specs=in_specs,
                                               out_specs=pl.BlockSpec((None, tm, c), full_idx)),
        out_shape=jax.ShapeDtypeStruct(out_shape, out_dtype),
        compiler_params=_params(("arbitrary",)),
    )(chip, *operands)


def _half(ref, c):
    h = ref.shape[1] // 2
    start = c * h if isinstance(c, int) else pl.multiple_of(c * h, 16)
    return ref.at[:, pl.ds(start, h), :]


class _SiblingFill:
    def __init__(self, lands, kinds, name):
        self.kinds, self.name, self.n = kinds, name, len(lands)
        n = self.n
        sem_shape = pltpu.SemaphoreType.DMA((n * N_CHIPS,))

        def body(*refs):
            land_refs, send_sems, recv_sems, token = refs[:n], refs[n], refs[n + 1], refs[-1]
            for cp in self._copies(land_refs, send_sems, recv_sems):
                cp.start()
            token[...] = jnp.zeros(token.shape, token.dtype)

        outs = pl.pallas_call(
            body, name=name + "_start",
            in_specs=[HBM_ONLY] * n,
            out_specs=[SEM_SPEC, SEM_SPEC] + [HBM_ONLY] * n + [pl.BlockSpec(memory_space=pltpu.VMEM)],
            out_shape=[sem_shape, sem_shape] + [pltpu.HBM(a.shape, a.dtype) for a in lands]
                      + [jax.ShapeDtypeStruct((8, LANES), F32)],
            input_output_aliases={i: i + 2 for i in range(n)},
            compiler_params=pltpu.CompilerParams(has_side_effects=SIDE_EFFECT),
        )(*[_in_hbm(a) for a in lands])
        self.send_sems, self.recv_sems, self.lands, self.token = outs[0], outs[1], list(outs[2:2 + n]), outs[-1]

    def _copies(self, land_refs, send_sems, recv_sems):
        x, y, c = lax.axis_index("x"), lax.axis_index("y"), lax.axis_index("c")
        me = 2 * x + y
        copies = []
        for a in range(self.n):
            for k in range(1, N_CHIPS):
                t = (me + k) % N_CHIPS
                slice_t = _piece_dyn(land_refs[a], self.kinds[a], t)
                got = _half(slice_t, c)
                copies.append(pltpu.make_async_remote_copy(
                    src_ref=got, dst_ref=got, send_sem=send_sems.at[a * N_CHIPS + k],
                    recv_sem=recv_sems.at[a * N_CHIPS + k], device_id=(x, y, 1 - c), device_id_type=MESH))
        return copies

    def wait(self, after):
        n = self.n

        def body(*refs):
            land_refs, send_sems, recv_sems = refs[:n], refs[n], refs[n + 1]
            for cp in self._copies(land_refs, send_sems, recv_sems):
                cp.wait_send()
                cp.wait_recv()

        operands = [_in_hbm(a) for a in self.lands] + [self.send_sems, self.recv_sems]
        in_specs = [HBM_ONLY] * n + [SEM_SPEC, SEM_SPEC]
        if after is not None:
            operands.append(after)
            in_specs.append(HBM_SPEC)
        outs = pl.pallas_call(
            body, name=self.name + "_wait",
            in_specs=in_specs, out_specs=[HBM_ONLY] * n,
            out_shape=[pltpu.HBM(a.shape, a.dtype) for a in self.lands],
            input_output_aliases={i: i for i in range(n)},
            compiler_params=pltpu.CompilerParams(has_side_effects=SIDE_EFFECT),
        )(*operands)
        return list(outs)


class _Exchange:
    def __init__(self, mode, srcs, lands, kinds, layers, name, after=None, halves=None):
        self.mode, self.kinds, self.layers, self.name, self.n = mode, kinds, layers, name, len(lands)
        self.halves = halves if halves is not None else [False] * len(lands)
        n, ns = self.n, len(srcs)
        n_in = ns + n + (after is not None)
        sem_shape = pltpu.SemaphoreType.DMA((n * N_CHIPS,))

        def body(*refs):
            src_refs, land_refs = refs[:ns], refs[ns:ns + n]
            send_sems, recv_sems = refs[n_in], refs[n_in + 1]
            token = refs[-1]
            c = lax.axis_index("c")
            me = 2 * lax.axis_index("x") + lax.axis_index("y")
            for j in range(N_CHIPS):
                @pl.when(me == j)
                def _():
                    for a in range(n):
                        for t in range(N_CHIPS):
                            if t != j:
                                src, dst = self._ends(src_refs, land_refs, a, j, t, c)
                                pltpu.make_async_remote_copy(
                                    src_ref=src, dst_ref=dst, send_sem=send_sems.at[a * N_CHIPS + t],
                                    recv_sem=recv_sems.at[a * N_CHIPS + j],
                                    device_id=_chip_of(t, c), device_id_type=MESH).start()
            token[...] = jnp.zeros(token.shape, token.dtype)

        arrays = list(srcs) + list(lands)
        operands = [_in_hbm(a) for a in arrays]
        in_specs = [HBM_ONLY] * (ns + n)
        if after is not None:
            operands.append(after)
            in_specs.append(HBM_SPEC)
        outs = pl.pallas_call(
            body, name=name + "_start",
            in_specs=in_specs,
            out_specs=[SEM_SPEC, SEM_SPEC] + [HBM_ONLY] * (ns + n) + [pl.BlockSpec(memory_space=pltpu.VMEM)],
            out_shape=[sem_shape, sem_shape] + [pltpu.HBM(a.shape, a.dtype) for a in arrays]
                      + [jax.ShapeDtypeStruct((8, LANES), F32)],
            input_output_aliases={i: i + 2 for i in range(ns + n)},
            compiler_params=pltpu.CompilerParams(has_side_effects=SIDE_EFFECT),
        )(*operands)
        self.send_sems, self.recv_sems = outs[0], outs[1]
        self.srcs, self.lands = list(outs[2:2 + ns]), list(outs[2 + ns:2 + ns + n])
        self.token = outs[-1]

    def _ends(self, src_refs, land_refs, a, me_j, peer, c):
        if self.mode == "gather":
            mine = _piece(land_refs[a], self.kinds[a], me_j)
            if self.halves[a]:
                mine = _half(mine, c)
            return mine, mine
        return _piece(src_refs[a], self.kinds[a], peer), land_refs[a].at[me_j, pl.ds(self.layers[a], 1)]

    def wait(self, after, lands=None):
        n, ns = self.n, len(self.srcs)
        lands = self.lands if lands is None else lands

        def body(*refs):
            src_refs, land_refs = refs[:ns], refs[ns:ns + n]
            send_sems, recv_sems = refs[ns + n], refs[ns + n + 1]
            c = lax.axis_index("c")
            me = 2 * lax.axis_index("x") + lax.axis_index("y")
            for j in range(N_CHIPS):
                @pl.when(me != j)
                def _():
                    for a in range(n):
                        sent, _ = self._ends(src_refs, land_refs, a, 0, j, c)
                        _, landed = self._ends(src_refs, land_refs, a, j, 0, c)
                        cp = pltpu.make_async_remote_copy(
                            src_ref=sent, dst_ref=landed, send_sem=send_sems.at[a * N_CHIPS + j],
                            recv_sem=recv_sems.at[a * N_CHIPS + j],
                            device_id=_chip_of(j, c), device_id_type=MESH)
                        cp.wait_send()
                        cp.wait_recv()

        arrays = self.srcs + list(lands)
        operands = [_in_hbm(a) for a in arrays] + [self.send_sems, self.recv_sems]
        in_specs = [HBM_ONLY] * (ns + n) + [SEM_SPEC, SEM_SPEC]
        if after is not None:
            operands.append(after)
            in_specs.append(HBM_SPEC)
        outs = pl.pallas_call(
            body, name=self.name + "_wait",
            in_specs=in_specs, out_specs=[HBM_ONLY] * (ns + n),
            out_shape=[pltpu.HBM(a.shape, a.dtype) for a in arrays],
            input_output_aliases={i: i for i in range(ns + n)},
            compiler_params=pltpu.CompilerParams(has_side_effects=SIDE_EFFECT),
        )(*operands)
        return list(outs[:ns]), list(outs[ns:])


def _sum_arrivals(zone, own_grads, kind, chip, name):
    _, layers, r, c = zone.shape
    tm = _pick_rows(r, 256)
    nb = r // tm

    def own_idx(l, i, chip_ref):
        return (0, chip_ref[0] * nb + i, 0) if kind == "row" else (0, i, _slot(kind, chip_ref[0]))

    def slot_idx(k):
        return lambda l, i, chip_ref: (jnp.where(chip_ref[0] == k, (k + 1) % N_CHIPS, k), l, i, 0)

    in_specs = [pl.BlockSpec((None, None, tm, c), slot_idx(k)) for k in range(N_CHIPS)]
    in_specs += [pl.BlockSpec((None, tm, c), own_idx) for _ in own_grads]

    def body(chip_ref, *refs):
        slot_refs, own_refs, o_ref = refs[:N_CHIPS], refs[N_CHIPS:N_CHIPS + layers], refs[-1]
        own = own_refs[0][...]
        for u in range(1, layers):
            own = jnp.where(pl.program_id(0) == u, own_refs[u][...], own)
        acc = None
        for k in range(N_CHIPS):
            term = jnp.where(chip_ref[0] == k, own, slot_refs[k][...]).astype(F32)
            acc = term if acc is None else acc + term
        o_ref[...] = acc.astype(o_ref.dtype)

    return pl.pallas_call(
        body, name=name,
        grid_spec=pltpu.PrefetchScalarGridSpec(
            num_scalar_prefetch=1, grid=(layers, nb), in_specs=in_specs,
            out_specs=pl.BlockSpec((tm, c), lambda l, i, chip_ref: (l * nb + i, 0))),
        out_shape=jax.ShapeDtypeStruct((layers * r, c), BF16),
        compiler_params=_params(("arbitrary", "arbitrary")),
    )(chip, zone, zone, zone, zone, *own_grads)


def _sibling_swap(arrays, name):
    n = len(arrays)

    def body(*refs):
        ins, outs = refs[:n], refs[n:2 * n]
        send_sems, recv_sems = refs[2 * n:]
        sibling = (lax.axis_index("x"), lax.axis_index("y"), 1 - lax.axis_index("c"))
        copies = [pltpu.make_async_remote_copy(src_ref=ins[a], dst_ref=outs[a], send_sem=send_sems.at[a],
                                               recv_sem=recv_sems.at[a], device_id=sibling, device_id_type=MESH)
                  for a in range(n)]
        for cp in copies:
            cp.start()
        for cp in copies:
            cp.wait()

    return pl.pallas_call(
        body, name=name,
        in_specs=[HBM_SPEC] * n, out_specs=[HBM_SPEC] * n,
        out_shape=[jax.ShapeDtypeStruct(a.shape, a.dtype) for a in arrays],
        scratch_shapes=[pltpu.SemaphoreType.DMA((n,)), pltpu.SemaphoreType.DMA((n,))],
    )(*arrays)


def _gather_devices(vec):
    def body(in_ref, out_ref, send_sems, recv_sems, local_sem):
        x, y, c = lax.axis_index("x"), lax.axis_index("y"), lax.axis_index("c")
        me = 4 * x + 2 * y + c
        mine = pltpu.make_async_copy(in_ref, out_ref.at[me], local_sem)
        mine.start()
        copies = []
        for rel in range(1, N_DEV):
            peer = (x ^ (rel >> 2), y ^ ((rel >> 1) & 1), c ^ (rel & 1))
            copies.append(pltpu.make_async_remote_copy(
                src_ref=in_ref, dst_ref=out_ref.at[me], send_sem=send_sems.at[rel], recv_sem=recv_sems.at[rel],
                device_id=peer, device_id_type=MESH))
        for cp in copies:
            cp.start()
        for cp in copies:
            cp.wait()
        mine.wait()

    return pl.pallas_call(
        body, name="gather_small",
        in_specs=[HBM_SPEC], out_specs=HBM_SPEC,
        out_shape=jax.ShapeDtypeStruct((N_DEV,) + vec.shape, vec.dtype),
        scratch_shapes=[pltpu.SemaphoreType.DMA((N_DEV,)), pltpu.SemaphoreType.DMA((N_DEV,)),
                        pltpu.SemaphoreType.DMA],
    )(vec)


BIG = [("a_w_in", "col"), ("a_w_out", "row"), ("kv_w", "row"), ("b_w_q", "row"), ("b_w_out", "row"),
       ("ffn_w_gate_up", "colp"), ("ffn_w_down", "row"), ("ple_w_up", "col"), ("ple_w_gate", "row")]
GATHER_GROUPS = [[("a_w_in", 0), ("small", 0)], [("a_w_out", 0), ("ffn_w_gate_up", 0)],
                 [("ffn_w_down", 0), ("ple_w_gate", 0), ("ple_w_up", 0)], [("kv_w", 0), ("b_w_q", 0), ("b_w_out", 0)],
                 [("ffn_w_gate_up", 1)], [("ffn_w_down", 1), ("ple_w_gate", 1), ("ple_w_up", 1)]]
SCATTER_GROUPS = [[("ple_w_gate", 1), ("ple_w_up", 1), ("ffn_w_down", 1)], [("ffn_w_gate_up", 1)],
                  [("b_w_out", 0), ("b_w_q", 0), ("kv_w", 0)], [("ple_w_gate", 0), ("ple_w_up", 0), ("ffn_w_down", 0)],
                  [("ffn_w_gate_up", 0), ("a_w_out", 0)], [("a_w_in", 0)]]
SMALL_SHARDED = ["ln_gain", "ln_bias", "a_lower_bound"]
SMALL_REPLICATED = ["a_norm_gain", "kv_b", "b_b_q", "b_sinks", "b_b_out", "ple_b_gate"]
WEIGHT_ORDER = ["a_w_in", "a_lower_bound", "a_norm_gain", "a_w_out", "kv_w", "kv_b", "b_w_q", "b_b_q", "b_sinks",
                "b_w_out", "b_b_out", "ffn_w_gate_up", "ffn_w_down", "ple_w_up", "ple_w_gate", "ple_b_gate",
                "ln_gain", "ln_bias"]


def _as3(a):
    return a.reshape((-1,) + a.shape[-2:]) if a.ndim >= 3 else a.reshape((1,) + a.shape)


def _pad_lanes(v):
    n = v.shape[-1]
    return jnp.pad(v, ((0, 0), (0, (-n) % LANES)))


def _adam_small_fn(w, mom, vel, g):
    return _adam_fn(w, mom, vel, g, jnp.zeros_like(g))[1:]


def _sum_rows_fn(slots):
    acc = slots[0]
    for s in range(1, slots.shape[0]):
        acc = acc + slots[s]
    return (acc,)


def kernel(x, p, a_w_in, a_lower_bound, a_norm_gain, a_w_out, kv_w, kv_b, b_w_q, b_b_q, b_sinks, b_w_out, b_b_out, ffn_w_gate_up, ffn_w_down, ple_w_up, ple_w_gate, ple_b_gate, ln_gain, ln_bias, loss_target, m_a_w_in, m_a_lower_bound, m_a_norm_gain, m_a_w_out, m_kv_w, m_kv_b, m_b_w_q, m_b_b_q, m_b_sinks, m_b_w_out, m_b_b_out, m_ffn_w_gate_up, m_ffn_w_down, m_ple_w_up, m_ple_w_gate, m_ple_b_gate, m_ln_gain, m_ln_bias, v_a_w_in, v_a_lower_bound, v_a_norm_gain, v_a_w_out, v_kv_w, v_kv_b, v_b_w_q, v_b_b_q, v_b_sinks, v_b_w_out, v_b_b_out, v_ffn_w_gate_up, v_ffn_w_down, v_ple_w_up, v_ple_w_gate, v_ple_b_gate, v_ln_gain, v_ln_bias):
    args = dict(locals())
    wts = {n: args[n] for n in WEIGHT_ORDER}
    mom = {n: args["m_" + n] for n in WEIGHT_ORDER}
    vel = {n: args["v_" + n] for n in WEIGHT_ORDER}
    chip = 2 * lax.axis_index("x") + lax.axis_index("y")
    d = x.shape[-1]
    dq = d // N_CHIPS

    kind_of = dict(BIG)
    kind_of["small"] = "col"
    chip_arr = chip.reshape(1).astype(jnp.int32)
    small_pack = jnp.concatenate([wts[n].reshape(-1, dq) for n in SMALL_SHARDED], axis=0)[None]

    def place(key, after):
        n, layer = key
        if n == "small":
            return _place(small_pack, 0, "col", chip_arr, name="place_small", out_dtype=F32, after=after)
        return _place(_as3(wts[n]), layer, kind_of[n], chip_arr, name=f"place_{n}{layer}", out_dtype=BF16,
                      after=after)

    gathers, where = [], {}
    for gi, group in enumerate(GATHER_GROUPS):
        prev = gathers[-1].token if gathers else None
        gathers.append(_Exchange("gather", [], [place(k, prev) for k in group], [kind_of[k[0]] for k in group],
                                 [0] * len(group), f"gather{gi}", after=prev,
                                 halves=[k[0] != "small" for k in group]))
        for k in group:
            where[k] = gi
    all_started = gathers[-1].token
    ready = {}

    fills = {}

    def pass_on(gi, after):
        if gi not in fills:
            group = GATHER_GROUPS[gi]
            outs = gathers[gi].wait(after)[1]
            split = [i for i, k in enumerate(group) if k[0] != "small"]
            fills[gi] = (outs, split, _SiblingFill([outs[i] for i in split], [kind_of[group[i][0]] for i in split],
                                                   f"fill{gi}"))

    def wget(name, layer, after):
        key = (name, layer)
        if key not in ready:
            gi = where[key]
            after = all_started if gi == 0 else after
            pass_on(gi, after)
            if 1 <= gi < len(GATHER_GROUPS) - 1:
                pass_on(gi + 1, after)
            outs, split, fill = fills[gi]
            for i, arr in zip(split, fill.wait(after)):
                outs[i] = arr
            for k, arr in zip(GATHER_GROUPS[gi], outs):
                ready[k] = arr
        return ready[key]

    small_full = wget("small", 0, None)[0]
    ln_gain_f = small_full[0:6].reshape(DEPTH, 3, d)
    ln_bias_f = small_full[6:12].reshape(DEPTH, 3, d)
    alb_f = small_full[12:14]

    group_of = {k: gi for gi, group in enumerate(SCATTER_GROUPS) for k in group}
    grads_done, zones, scatters = {}, {}, []

    def grad_sink(name, layer, grad):
        grads_done[(name, layer)] = grad
        if name not in zones:
            zones[name] = lax.empty((N_CHIPS,) + _as3(wts[name]).shape, BF16)
        gi = group_of[(name, layer)]
        group = SCATTER_GROUPS[gi]
        if not all(k in grads_done for k in group):
            return None
        ex = _Exchange("scatter", [grads_done[k] for k in group], [zones[k[0]] for k in group],
                       [kind_of[k[0]] for k in group], [k[1] for k in group], f"scatter{gi}")
        for k, zone in zip(group, ex.lands):
            zones[k[0]] = zone
        scatters.append((ex, group))
        return ex.token

    loss, grad_x, gs = _local_step(
        x[0], p.reshape((p.shape[0],) + p.shape[2:]), loss_target[0], wget, grad_sink, ln_gain_f, ln_bias_f, alb_f, a_norm_gain, kv_b, b_b_q,
        b_sinks, b_b_out, ple_b_gate)

    res = {}

    def arrive(batch, after):
        for ex, group in batch:
            srcs, outs = ex.wait(after, lands=[zones[k[0]] for k in group])
            for k, grad, zone in zip(group, srcs, outs):
                grads_done[k], zones[k[0]] = grad, zone

    def update(names, tag):
        partial = []
        for n in names:
            own = [grads_done[(n, layer)] for layer in range(zones[n].shape[1])]
            partial.append(_sum_arrivals(zones[n], own, kind_of[n], chip_arr, f"sum_{n}"))
        sibling = _sibling_swap(partial, tag)
        for n, own, sib in zip(names, partial, sibling):
            shp = wts[n].shape
            flat = lambda a: a.reshape(-1, shp[-1])
            out = _rowwise(_adam_fn, [flat(wts[n]), flat(mom[n]), flat(vel[n]), own, sib], [],
                           [(own.shape, F32)] * 4, name=f"adam_{n}")
            res[n] = [o.reshape(shp) for o in out]
        return res[names[-1]][1]

    last_names = [k[0] for k in SCATTER_GROUPS[-1]]
    arrive(scatters[:-1], grad_x)
    updated = update([n for n, _ in BIG if n not in last_names], "sibling_swap")
    arrive(scatters[-1:], updated)
    update(last_names, "sibling_swap_last")

    ln_g = jnp.concatenate([gs[f"ln_gain_{i}_{j}"] for i in range(DEPTH) for j in range(3)], axis=0)
    ln_b = jnp.concatenate([gs[f"ln_bias_{i}_{j}"] for i in range(DEPTH) for j in range(3)], axis=0)
    ple_bg = jnp.concatenate([gs[f"ple_b_{i}"] for i in range(DEPTH)], axis=0)
    small_list = [ln_g.reshape(1, -1), ln_b.reshape(1, -1), gs["alb"].reshape(1, -1), gs["norm_gain"],
                  gs["kv_b"], gs["b_q"], _pad_lanes(gs["sinks"]), gs["b_out"], ple_bg.reshape(1, -1), loss]
    small_vec = jnp.concatenate(small_list, axis=1)
    everyone = _gather_devices(small_vec)
    total, = _rowwise(_sum_rows_fn, [everyone], [], [(small_vec.shape, F32)], name="sum_small")
    offs, pos = [], 0
    for v in small_list:
        offs.append((pos, v.shape[1]))
        pos += v.shape[1]

    def seg(k):
        return total[0, offs[k][0]:offs[k][0] + offs[k][1]]

    def my_cols(full, rows):
        return lax.dynamic_slice_in_dim(full.reshape(rows, N_CHIPS, dq), chip, 1, axis=1).reshape(rows, dq)

    n_sink = b_sinks.shape[-1]
    small_grads = {
        "ln_gain": my_cols(seg(0), 6).reshape(ln_gain.shape), "ln_bias": my_cols(seg(1), 6).reshape(ln_bias.shape),
        "a_lower_bound": my_cols(seg(2), 2), "a_norm_gain": seg(3).reshape(a_norm_gain.shape),
        "kv_b": seg(4).reshape(kv_b.shape), "b_b_q": seg(5).reshape(b_b_q.shape),
        "b_sinks": seg(6)[:n_sink].reshape(b_sinks.shape), "b_b_out": seg(7).reshape(b_b_out.shape),
        "ple_b_gate": seg(8).reshape(ple_b_gate.shape)}
    names = SMALL_SHARDED + SMALL_REPLICATED
    pack = lambda dct: _pad_lanes(jnp.concatenate([dct[n].reshape(1, -1) for n in names], axis=1))
    g_pack = pack(small_grads)
    upd = _rowwise(_adam_small_fn, [pack(wts), pack(mom), pack(vel), g_pack], [], [(g_pack.shape, F32)] * 3,
                   name="adam_small")
    pos = 0
    for n in names:
        size = wts[n].size
        res[n] = [small_grads[n]] + [u[0, pos:pos + size].reshape(wts[n].shape) for u in upd]
        pos += size

    outs = [seg(9)[0], grad_x[None]]
    for k in range(4):
        outs += [res[n][k] for n in WEIGHT_ORDER]
    return tuple(outs)
```

```python
import functools

import jax
import jax.numpy as jnp
from jax import lax
from jax.experimental import pallas as pl
from jax.experimental.pallas import tpu as pltpu

F32 = jnp.float32
BF16 = jnp.bfloat16
MESH = pl.DeviceIdType.MESH

LANES = 128
HG_DK = 128
HG_CHUNK = 64
HG_SUB = 16
HG_ROWS = 512
HG_HEADS_PER_STEP = 2
ATT_HD = 64
ATT_G = 4
WINDOW = 128
DEPTH = 2
ALPHA = (2.0 * DEPTH) ** 0.25
LN_EPS = 1e-5
RMS_EPS = 1e-6
ADAM_LR, ADAM_B1, ADAM_B2, ADAM_EPS, ADAM_WD, ADAM_STEP = 0.001, 0.9, 0.999, 1e-08, 0.01, 10
N_CHIPS = 4
N_DEV = 8
VMEM_LIMIT = 56 * 1024 * 1024
NEG = -1e30


def _pick(n, cap):
    best = None
    for d in range(LANES, min(n, cap) + 1, LANES):
        if n % d == 0:
            best = d
    return n if best is None else best


def _pick_rows(m, cap):
    best = None
    for d in range(16, min(m, cap) + 1, 16):
        if m % d == 0:
            best = d
    return m if best is None else best


def _params(sem):
    return pltpu.CompilerParams(dimension_semantics=sem, vmem_limit_bytes=VMEM_LIMIT)


def _zeros_index(ndim, grid_rank=3):
    return (lambda i, j, kk: (0,) * ndim) if grid_rank == 3 else (lambda kk, i: (0,) * ndim)


def _mm(a, b, *, name, la=None, lb=None, ta=False, tb=False, bias=None, add=None, out_dtype=F32,
        out_layers=None, out_layer=None, after=None, post=None, tile_cols=None, caps=(1024, 1536, 2048)):
    ar, ac = a.shape[-2:]
    br, bc = b.shape[-2:]
    m, k = (ac, ar) if ta else (ar, ac)
    k2, n = (bc, br) if tb else (br, bc)
    assert k == k2, (a.shape, b.shape, ta, tb)
    if post is not None:
        caps = (512, n if tile_cols is None else tile_cols, caps[2])
    tm, tn, tk = _pick(m, caps[0]), _pick(n, caps[1]), _pick(k, caps[2])
    assert post is None or tn == caps[1]
    nk = k // tk
    gi, gj = m // tm, n // tn
    a_bytes, b_bytes = m * k * a.dtype.itemsize, k * n * b.dtype.itemsize
    rows_outer = (a_bytes + b_bytes * (gi if gj * nk > 1 else 1)) <= (b_bytes + a_bytes * (gj if gi * nk > 1 else 1))
    k_outer = post is not None and nk > 1 and gj == 1
    grid = (nk, gi) if k_outer else (gi, gj, nk) if rows_outer else (gj, gi, nk)
    keep_at = ta and nk == 1 and gj > 1 and rows_outer

    def bs(block, idx, late=False):
        if k_outer:
            return pl.BlockSpec(block, lambda kk, i: idx(jnp.where(kk == nk - 1, i, 0) if late else i, 0, kk))
        return pl.BlockSpec(block, idx if rows_outer else (lambda q, p, kk: idx(p, q, kk)))

    def spec(block, idx, layer):
        if layer is None:
            return bs(block, idx)
        return bs((None,) + block, lambda i, j, kk: (layer,) + idx(i, j, kk))

    a_spec = spec((tk, tm), lambda i, j, kk: (kk, i), la) if ta else spec((tm, tk), lambda i, j, kk: (i, kk), la)
    b_spec = spec((tn, tk), lambda i, j, kk: (j, kk), lb) if tb else spec((tk, tn), lambda i, j, kk: (kk, j), lb)
    in_specs, operands = [a_spec, b_spec], [a, b]
    if bias is not None:
        in_specs.append(bs((1, tn), lambda i, j, kk: (0, j)))
        operands.append(bias)
    if add is not None:
        in_specs.append(bs((tm, tn), lambda i, j, kk: (i, j), late=True))
        operands.append(add)
    if after is not None:
        in_specs.append(pl.BlockSpec(memory_space=pl.ANY))
        operands.append(after)
    dims = (((0 if ta else 1,), (1 if tb else 0,)), ((), ()))
    has_bias, has_add = bias is not None, add is not None
    if post is None:
        fn, rows, whole, outs, sums = None, [], [], [], []
        out_shape = jax.ShapeDtypeStruct((m, n) if out_layers is None else (out_layers, m, n), out_dtype)
        out_specs = spec((tm, tn), lambda i, j, kk: (i, j), out_layer)
    else:
        fn, rows, whole, outs, sums = post
        in_specs += [bs((tm, r.shape[-1] // gj), lambda i, j, kk: (i, j), late=True) for r in rows]
        in_specs += [pl.BlockSpec(tuple(w.shape), _zeros_index(w.ndim, len(grid))) for w in whole]
        operands += list(rows) + list(whole)
        out_shape = [jax.ShapeDtypeStruct(sh, dt) for sh, dt in list(outs) + list(sums)]
        out_specs = ([bs((tm, sh[-1] // gj), lambda i, j, kk: (i, j), late=True) for sh, _ in outs]
                     + [pl.BlockSpec(tuple(sh), _zeros_index(len(sh), len(grid))) for sh, _ in sums])
    n_in, n_extra, n_outs, n_sums = len(operands), len(rows) + len(whole), len(outs), len(sums)

    def body(*refs):
        a_ref, b_ref = refs[0], refs[1]
        pos = 2
        bias_ref = add_ref = None
        if has_bias:
            bias_ref = refs[pos]
            pos += 1
        if has_add:
            add_ref = refs[pos]
            pos += 1
        extra_refs = refs[n_in - n_extra:n_in]
        out_refs = refs[n_in:n_in + max(n_outs, 1)]
        sum_refs = refs[n_in + n_outs:n_in + n_outs + n_sums]
        acc_ref = refs[-1] if nk > 1 else None
        if keep_at:
            at_ref = refs[-1]

            @pl.when(pl.program_id(1) == 0)
            def _():
                at_ref[...] = a_ref[...].astype(BF16).T

            part = lax.dot_general(at_ref[...], b_ref[...].astype(BF16), (((1,), (1 if tb else 0,)), ((), ())),
                                   preferred_element_type=F32)
        else:
            part = lax.dot_general(a_ref[...].astype(BF16), b_ref[...].astype(BF16), dims,
                                   preferred_element_type=F32)

        def finish(total):
            if has_bias:
                total = total + bias_ref[...]
            if has_add:
                total = total + add_ref[...]
            if fn is None:
                out_refs[0][...] = total.astype(out_refs[0].dtype)
                return
            res = fn(total, *[r[...] for r in extra_refs])
            for ref, val in zip(out_refs, res[:n_outs]):
                ref[...] = val.astype(ref.dtype)
            if n_sums:
                @pl.when(pl.program_id(1 if k_outer or not rows_outer else 0) == 0)
                def _():
                    for ref in sum_refs:
                        ref[...] = jnp.zeros(ref.shape, ref.dtype)

                for ref, val in zip(sum_refs, res[n_outs:]):
                    ref[...] += val

        if nk == 1:
            finish(part)
        elif k_outer:
            kk = pl.program_id(0)
            rows_i = pl.ds(pl.multiple_of(pl.program_id(1) * tm, tm), tm)

            @pl.when(kk == 0)
            def _():
                acc_ref[rows_i, :] = part

            @pl.when(kk > 0)
            def _():
                acc_ref[rows_i, :] += part

            @pl.when(kk == nk - 1)
            def _():
                finish(acc_ref[rows_i, :])
        else:
            kk = pl.program_id(2)

            @pl.when(kk == 0)
            def _():
                acc_ref[...] = part

            @pl.when(kk > 0)
            def _():
                acc_ref[...] += part

            @pl.when(kk == nk - 1)
            def _():
                finish(acc_ref[...])

    return pl.pallas_call(
        body, name=name, grid=grid, in_specs=in_specs, out_specs=out_specs, out_shape=out_shape,
        scratch_shapes=([pltpu.VMEM((m, n) if k_outer else (tm, tn), F32)] if nk > 1
                        else [pltpu.VMEM((tm, tk), BF16)] if keep_at else []),
        compiler_params=_params(("arbitrary", "arbitrary") if k_outer
                                else ("arbitrary" if n_sums else "parallel", "arbitrary" if keep_at else "parallel",
                                      "arbitrary") if rows_outer
                                else ("parallel", "arbitrary" if n_sums else "parallel", "arbitrary")),
    )(*operands)


def _rowwise(fn, rows, whole, outs, sums=(), *, name, tm=256):
    m = rows[0].shape[-2]
    tm = _pick_rows(m, tm)
    n_rows, n_whole, n_outs, n_sums = len(rows), len(whole), len(outs), len(sums)

    def rspec(shape):
        lead = len(shape) - 2
        return pl.BlockSpec(tuple(shape[:-2]) + (tm, shape[-1]), lambda i: (0,) * lead + (i, 0))

    def wspec(shape):
        return pl.BlockSpec(tuple(shape), lambda i: (0,) * len(shape))

    def body(*refs):
        vals = [r[...] for r in refs[:n_rows + n_whole]]
        out_refs = refs[n_rows + n_whole:n_rows + n_whole + n_outs]
        sum_refs = refs[n_rows + n_whole + n_outs:]
        res = fn(*vals)
        for ref, val in zip(out_refs, res[:n_outs]):
            ref[...] = val.astype(ref.dtype)
        if n_sums:
            @pl.when(pl.program_id(0) == 0)
            def _():
                for ref in sum_refs:
                    ref[...] = jnp.zeros(ref.shape, ref.dtype)

            for ref, val in zip(sum_refs, res[n_outs:]):
                ref[...] += val

    result = pl.pallas_call(
        body, name=name, grid=(m // tm,),
        in_specs=[rspec(r.shape) for r in rows] + [wspec(w.shape) for w in whole],
        out_specs=[rspec(s) for s, _ in outs] + [wspec(s) for s, _ in sums],
        out_shape=[jax.ShapeDtypeStruct(s, d) for s, d in list(outs) + list(sums)],
        compiler_params=_params(("arbitrary",)),
    )(*rows, *whole)
    return result


def _sigmoid(v):
    return jax.nn.sigmoid(v)


def _col_sum(v):
    return jnp.sum(v, axis=0, keepdims=True)


def _ln_stats(z):
    mu = jnp.mean(z, axis=-1, keepdims=True)
    zc = z - mu
    var = jnp.mean(zc * zc, axis=-1, keepdims=True)
    rstd = lax.rsqrt(var + LN_EPS)
    return zc * rstd, rstd


def _ln_fwd_fn(xin, h, gain, bias):
    xhat, _ = _ln_stats(ALPHA * xin + h)
    y = xhat * gain + bias
    return y, y


def _ple_ln_fwd_fn(xin, pg, pu, gain, bias):
    xhat, _ = _ln_stats(ALPHA * xin + _sigmoid(pg) * pu)
    y = xhat * gain + bias
    return y, y


def _ln_dz(dy, z, gain):
    xhat, rstd = _ln_stats(z)
    dxhat = dy * gain
    dz = rstd * (dxhat - jnp.mean(dxhat, axis=-1, keepdims=True)
                 - xhat * jnp.mean(dxhat * xhat, axis=-1, keepdims=True))
    return dz, _col_sum(dy * xhat), _col_sum(dy)


def _ln_bwd_fn(dy, xin, h, gain):
    dz, dgain, dbias = _ln_dz(dy, ALPHA * xin + h, gain)
    return ALPHA * dz, dz, dgain, dbias, _col_sum(dz)


def _ple_ln_bwd_fn(dy, xin, pg, pu, gain):
    sg = _sigmoid(pg)
    dz, dgain, dbias = _ln_dz(dy, ALPHA * xin + sg * pu, gain)
    dpg = dz * pu * sg * (1.0 - sg)
    return ALPHA * dz, dpg, dz * sg, dgain, dbias, _col_sum(dpg)


def _swiglu_fwd_fn(gu):
    hid = gu.shape[-1] // 2
    gate, up = gu[:, :hid], gu[:, hid:]
    return gu, gate * _sigmoid(gate) * up


def _swiglu_bwd_fn(dact, gu):
    gu = gu.astype(F32)
    hid = gu.shape[-1] // 2
    gate, up = gu[:, :hid], gu[:, hid:]
    sg = _sigmoid(gate)
    dgate = dact * up * sg * (1.0 + gate * (1.0 - sg))
    dup = dact * gate * sg
    return (jnp.concatenate([dgate, dup], axis=-1),)


def _loss_fn(y, target):
    err = y - target
    inv = 1.0 / y.shape[-1]
    part = 0.5 * inv * jnp.sum(jnp.sum(err * err, axis=-1, keepdims=True), axis=0, keepdims=True)
    return err * inv, jnp.broadcast_to(part, (1, LANES))


def _adam_fn(w, mom, vel, p_own, p_sib):
    g = p_own.astype(F32) + p_sib.astype(F32)
    m_new = ADAM_B1 * mom + (1.0 - ADAM_B1) * g
    v_new = ADAM_B2 * vel + (1.0 - ADAM_B2) * (g * g)
    m_hat = m_new / (1.0 - ADAM_B1 ** ADAM_STEP)
    v_hat = v_new / (1.0 - ADAM_B2 ** ADAM_STEP)
    delta = -ADAM_LR * (m_hat / (jnp.sqrt(v_hat) + ADAM_EPS) + ADAM_WD * w)
    return g, delta, m_new, v_new


def _split2(x):
    hi = x.astype(BF16)
    return hi, (x - hi.astype(F32)).astype(BF16)


def _dot3(a, b, dims):
    a_hi, a_lo = _split2(a)
    b_hi, b_lo = _split2(b)
    dn = (dims, ((), ()))
    return (lax.dot_general(a_hi, b_hi, dn, preferred_element_type=F32)
            + (lax.dot_general(a_hi, b_lo, dn, preferred_element_type=F32)
               + lax.dot_general(a_lo, b_hi, dn, preferred_element_type=F32)))


def _tdot(mask01, b):
    m = mask01.astype(BF16)
    b_hi = b.astype(BF16)
    rest = b - b_hi.astype(F32)
    b_mid = rest.astype(BF16)
    b_lo = (rest - b_mid.astype(F32)).astype(BF16)
    dn = (((1,), (0,)), ((), ()))
    return (lax.dot_general(m, b_hi, dn, preferred_element_type=F32)
            + (lax.dot_general(m, b_mid, dn, preferred_element_type=F32)
               + lax.dot_general(m, b_lo, dn, preferred_element_type=F32)))


def _hdot(a, b):
    return _dot3(a, b, ((1,), (0,)))


def _hdot_nt(a, b):
    return _dot3(a, b, ((1,), (1,)))


def _hdot_tn(a, b):
    return _dot3(a, b, ((0,), (0,)))


def _dot(a, b):
    return lax.dot_general(a.astype(BF16), b.astype(BF16), (((1,), (0,)), ((), ())), preferred_element_type=F32)


def _dot_nt(a, b):
    return lax.dot_general(a.astype(BF16), b.astype(BF16), (((1,), (1,)), ((), ())), preferred_element_type=F32)


def _dot_tn(a, b):
    return lax.dot_general(a.astype(BF16), b.astype(BF16), (((0,), (0,)), ((), ())), preferred_element_type=F32)


def _hg_masks():
    c = HG_CHUNK
    row = lax.broadcasted_iota(jnp.int32, (c, c), 0)
    col = lax.broadcasted_iota(jnp.int32, (c, c), 1)
    base = row & (-HG_SUB)
    return row, col, base, col <= row, col < base


def _hg_gates(qr, fr, alb):
    lbound = _sigmoid(alb[0:1, :] - alb[1:2, :])
    sig = _sigmoid(fr)
    forget = lbound + (1.0 - lbound) * sig
    kk = (1.0 - lbound) * _sigmoid(-fr)
    qt = qr * _sigmoid(qr) * (HG_DK ** -0.5)
    return qt, kk, jnp.log(forget), lbound, sig, forget


def _hg_scores(qt, kk, g, scores=True):
    c, nsub = HG_CHUNK, HG_CHUNK // HG_SUB
    row, col, base, causal, below = _hg_masks()
    b = _tdot(causal, g)
    rr = _tdot(below, g)
    bq = b - rr
    qh = qt * jnp.exp(bq)
    edecs = [None]
    parts = [jnp.zeros((HG_SUB, c), F32)]
    for i in range(1, nsub):
        edec = jnp.exp(jnp.minimum(rr[i * HG_SUB:i * HG_SUB + 1, :] - b, 0.0))
        edecs.append(edec)
        if scores:
            parts.append(_dot_nt(qh[i * HG_SUB:(i + 1) * HG_SUB, :], kk * edec))
    b3 = b.reshape(nsub, HG_SUB, HG_DK)
    q3 = qt.reshape(nsub, HG_SUB, HG_DK)
    k3 = kk.reshape(nsub, HG_SUB, HG_DK)
    if not scores:
        return None, b, bq, qh, edecs, (b3, q3, k3)
    a = jnp.where(below, jnp.concatenate(parts, axis=0), 0.0)
    for j in range(HG_SUB):
        e = jnp.exp(b3 - b3[:, j:j + 1, :])
        colv = jnp.sum(q3 * e * k3[:, j:j + 1, :], axis=-1, keepdims=True).reshape(c, 1)
        a = jnp.where(col == base + j, colv, a)
    a = jnp.where(causal, a, 0.0)
    return a, b, bq, qh, edecs, (b3, q3, k3)


def _hg_norm(o, gr, gain):
    r = lax.rsqrt(jnp.mean(o * o, axis=-1, keepdims=True) + RMS_EPS)
    sg = _sigmoid(gr)
    return o * r * gain, r, sg


def _hgrn2_fwd(proj, alb, gain, *, rb):
    m, d4 = proj.shape
    d = d4 // 4
    heads = d // HG_DK
    hp = HG_HEADS_PER_STEP
    rb = min(rb, m)
    cpb = rb // HG_CHUNK
    nrb = m // rb

    def body(q_ref, f_ref, v_ref, g_ref, alb_ref, gain_ref, o_ref, og_ref, st_ref, a_ref, state):
        @pl.when(pl.program_id(1) == 0)
        def _():
            state[...] = jnp.zeros(state.shape, F32)

        def chunk(ci, carry):
            sl = pl.ds(pl.multiple_of(ci * HG_CHUNK, HG_CHUNK), HG_CHUNK)
            for u in range(hp):
                ln = slice(u * HG_DK, (u + 1) * HG_DK)
                qt, kk, g, _, _, _ = _hg_gates(q_ref[sl, ln], f_ref[sl, ln], alb_ref[:, ln])
                v = v_ref[sl, ln]
                st = state[u]
                st_ref[u, ci] = st
                a, b, _, _, _, _ = _hg_scores(qt, kk, g)
                a_ref[u, ci] = a.astype(a_ref.dtype)
                o = _dot(a, v) + _dot_nt(qt * jnp.exp(b), st)
                b_last = b[HG_CHUNK - 1:HG_CHUNK, :]
                state[u] = st * jnp.exp(b_last) + _hdot_tn(v, kk * jnp.exp(b_last - b))
                o_ref[sl, ln] = o
                n, _, sg = _hg_norm(o, g_ref[sl, ln], gain_ref[...])
                og_ref[sl, ln] = (n * g_ref[sl, ln] * sg).astype(og_ref.dtype)
            return carry

        lax.fori_loop(0, cpb, chunk, 0)

    def col(cidx):
        return pl.BlockSpec((rb, hp * HG_DK), lambda h, r: (r, cidx * (heads // hp) + h))

    return pl.pallas_call(
        body, name="hgrn2_fwd", grid=(heads // hp, nrb),
        in_specs=[col(0), col(1), col(2), col(3),
                  pl.BlockSpec((2, hp * HG_DK), lambda h, r: (0, h)),
                  pl.BlockSpec((1, HG_DK), lambda h, r: (0, 0))],
        out_specs=[pl.BlockSpec((rb, hp * HG_DK), lambda h, r: (r, h)),
                   pl.BlockSpec((rb, hp * HG_DK), lambda h, r: (r, h)),
                   pl.BlockSpec((hp, cpb, HG_DK, HG_DK), lambda h, r: (h, r, 0, 0)),
                   pl.BlockSpec((hp, cpb, HG_CHUNK, HG_CHUNK), lambda h, r: (h, r, 0, 0))],
        out_shape=[jax.ShapeDtypeStruct((m, d), F32), jax.ShapeDtypeStruct((m, d), BF16),
                   jax.ShapeDtypeStruct((heads, m // HG_CHUNK, HG_DK, HG_DK), F32),
                   jax.ShapeDtypeStruct((heads, m // HG_CHUNK, HG_CHUNK, HG_CHUNK), BF16)],
        scratch_shapes=[pltpu.VMEM((hp, HG_DK, HG_DK), F32)],
        compiler_params=_params(("parallel", "arbitrary")),
    )(proj, proj, proj, proj, alb, gain)


def _hgrn2_bwd(proj, o_pre, states, scores, dog, alb, gain, *, rb):
    m, d4 = proj.shape
    d = d4 // 4
    heads = d // HG_DK
    rb = min(rb, m)
    cpb = rb // HG_CHUNK
    nrb = m // rb
    c, nsub = HG_CHUNK, HG_CHUNK // HG_SUB

    def body(q_ref, f_ref, v_ref, g_ref, o_ref, st_ref, a_ref, dog_ref, alb_ref, gain_ref,
             dq_ref, df_ref, dv_ref, dg_ref, dalb_ref, dgain_ref, dstate, carry_ref):
        first = (pl.program_id(0) == 0) & (pl.program_id(1) == 0)

        @pl.when(first)
        def _():
            dgain_ref[...] = jnp.zeros(dgain_ref.shape, F32)

        @pl.when(pl.program_id(1) == 0)
        def _():
            dstate[...] = jnp.zeros(dstate.shape, F32)
            carry_ref[...] = jnp.zeros(carry_ref.shape, F32)
            dalb_ref[...] = jnp.zeros(dalb_ref.shape, F32)

        row, col, base, causal, below = _hg_masks()
        sub_iota = lax.broadcasted_iota(jnp.int32, (nsub, HG_SUB, HG_DK), 1)
        row_k = lax.broadcasted_iota(jnp.int32, (c, HG_DK), 0)
        upper = col >= row

        def chunk(step, carry):
            ci = cpb - 1 - step
            sl = pl.ds(pl.multiple_of(ci * HG_CHUNK, HG_CHUNK), HG_CHUNK)
            qr, fr, v, gr = q_ref[sl, :], f_ref[sl, :], v_ref[sl, :], g_ref[sl, :]
            qt, kk, g, lbound, sig, forget = _hg_gates(qr, fr, alb_ref[...])
            o = o_ref[sl, :]
            dogv = dog_ref[sl, :]
            gain_v = gain_ref[...]
            n, r, sg = _hg_norm(o, gr, gain_v)
            dgr = dogv * n * sg * (1.0 + gr * (1.0 - sg))
            dn = dogv * gr * sg
            dgain_ref[...] += _col_sum(dn * o * r)
            u = dn * gain_v
            d_o = r * u - o * (r * r * r) * jnp.mean(u * o, axis=-1, keepdims=True)
            st0 = st_ref[ci]
            dst = dstate[...]
            _, b, bq, qh, edecs, (b3, q3, k3) = _hg_scores(qt, kk, g, scores=False)
            a = a_ref[ci]
            eb = jnp.exp(b)
            b_last = b[c - 1:c, :]
            kdl_dec = jnp.exp(b_last - b)
            kdl = kk * kdl_dec
            d_a = jnp.where(causal, _dot_nt(d_o, v), 0.0)
            d_at = _dot_nt(v, d_o)
            dv = _dot_tn(a, d_o) + _dot_nt(kdl, dst)
            dq = eb * _hdot(d_o, st0)
            dk = _hdot(v, dst) * kdl_dec
            d_a_below = jnp.where(below, d_a, 0.0)
            dq_parts = [jnp.zeros((HG_SUB, HG_DK), F32)]
            for i in range(1, nsub):
                lo, hi = i * HG_SUB, (i + 1) * HG_SUB
                dq_parts.append(_hdot(d_a_below[lo:hi, :], kk * edecs[i]))
                gi = _hdot(d_at[:, lo:hi], qh[lo:hi, :])
                dk = dk + jnp.where(row_k < lo, edecs[i] * gi, 0.0)
            dq = dq + jnp.concatenate(dq_parts, axis=0) * jnp.exp(bq)
            dq3 = jnp.zeros((nsub, HG_SUB, HG_DK), F32)
            dk3 = jnp.zeros((nsub, HG_SUB, HG_DK), F32)
            d_diag = jnp.concatenate([d_a[i * HG_SUB:(i + 1) * HG_SUB, i * HG_SUB:(i + 1) * HG_SUB]
                                      for i in range(nsub)], axis=0).reshape(nsub, HG_SUB, HG_SUB)
            for j in range(HG_SUB):
                e = jnp.exp(jnp.minimum(b3 - b3[:, j:j + 1, :], 0.0))
                t1 = d_diag[:, :, j:j + 1] * e
                dq3 = dq3 + t1 * k3[:, j:j + 1, :]
                dk3 = jnp.where(sub_iota == j, jnp.sum(t1 * q3, axis=1, keepdims=True), dk3)
            dq = dq + dq3.reshape(c, HG_DK)
            dk = dk + dk3.reshape(c, HG_DK)
            dstate[...] = dst * jnp.exp(b_last) + _hdot_tn(d_o, qt * eb)
            dglog = _tdot(upper, qt * dq - kk * dk) + carry_ref[...]
            carry_ref[...] = dglog[0:1, :]
            dforget = dglog / forget
            one_m_lb = 1.0 - lbound
            dsig = (dforget - dk) * one_m_lb
            sneg = _sigmoid(-fr)
            dlb = _col_sum(dforget * (1.0 - sig) - dk * sneg)
            dalb0 = dlb * lbound * one_m_lb
            dalb_ref[...] += jnp.concatenate([dalb0, -dalb0], axis=0)
            sq = _sigmoid(qr)
            dq_ref[sl, :] = (dq * (HG_DK ** -0.5) * sq * (1.0 + qr * (1.0 - sq))).astype(dq_ref.dtype)
            df_ref[sl, :] = (dsig * sig * (1.0 - sig)).astype(df_ref.dtype)
            dv_ref[sl, :] = dv.astype(dv_ref.dtype)
            dg_ref[sl, :] = dgr.astype(dg_ref.dtype)
            return carry

        lax.fori_loop(0, cpb, chunk, 0, unroll=2)

    def rev(r):
        return nrb - 1 - r

    def col(cidx):
        return pl.BlockSpec((rb, HG_DK), lambda h, r: (rev(r), cidx * heads + h))

    def head_rows():
        return pl.BlockSpec((rb, HG_DK), lambda h, r: (rev(r), h))

    return pl.pallas_call(
        body, name="hgrn2_bwd", grid=(heads, nrb),
        in_specs=[col(0), col(1), col(2), col(3), head_rows(),
                  pl.BlockSpec((None, cpb, HG_DK, HG_DK), lambda h, r: (h, rev(r), 0, 0)),
                  pl.BlockSpec((None, cpb, HG_CHUNK, HG_CHUNK), lambda h, r: (h, rev(r), 0, 0)),
                  head_rows(),
                  pl.BlockSpec((2, HG_DK), lambda h, r: (0, h)),
                  pl.BlockSpec((1, HG_DK), lambda h, r: (0, 0))],
        out_specs=[head_rows(), head_rows(), head_rows(), head_rows(),
                   pl.BlockSpec((2, HG_DK), lambda h, r: (0, h)),
                   pl.BlockSpec((1, HG_DK), lambda h, r: (0, 0))],
        out_shape=[jax.ShapeDtypeStruct((m, d), BF16)] * 4
                  + [jax.ShapeDtypeStruct((2, d), F32), jax.ShapeDtypeStruct((1, HG_DK), F32)],
        scratch_shapes=[pltpu.VMEM((HG_DK, HG_DK), F32), pltpu.VMEM((1, HG_DK), F32)],
        compiler_params=_params(("arbitrary", "arbitrary")),
    )(proj, proj, proj, proj, o_pre, states, scores, dog, alb, gain)


def _swa_probs(qh, kp, kc, sink, slope, has_prev, lse=None):
    rows = qh.shape[0]
    qi = lax.broadcasted_iota(jnp.int32, (rows, WINDOW), 0) & (WINDOW - 1)
    si = lax.broadcasted_iota(jnp.int32, (rows, WINDOW), 1)
    scale = ATT_HD ** -0.5
    dist_c = (qi - si).astype(F32)
    s_p = _dot_nt(qh, kp) * scale - slope * (dist_c + float(WINDOW))
    s_c = _dot_nt(qh, kc) * scale - slope * dist_c
    s_p = jnp.where((si > qi) & has_prev, s_p, NEG)
    s_c = jnp.where(si <= qi, s_c, NEG)
    if lse is not None:
        return jnp.exp(s_p - lse), jnp.exp(s_c - lse), jnp.exp(sink - lse), lse
    mx = jnp.maximum(jnp.maximum(jnp.max(s_p, axis=-1, keepdims=True), jnp.max(s_c, axis=-1, keepdims=True)), sink)
    e_p, e_c, e_s = jnp.exp(s_p - mx), jnp.exp(s_c - mx), jnp.exp(sink - mx)
    total = jnp.sum(e_p, axis=-1, keepdims=True) + jnp.sum(e_c, axis=-1, keepdims=True) + e_s
    inv = 1.0 / total
    return e_p * inv, e_c * inv, e_s * inv, mx + jnp.log(total)


def _slope(h, n_heads):
    return float(2.0 ** (-8.0 * (h + 1) / n_heads))


def _swa_group(ref_vals, sink_ref, kh, n_heads):
    heads = [kh * ATT_G + g for g in range(ATT_G)]
    stacked = [jnp.concatenate([v[:, h * ATT_HD:(h + 1) * ATT_HD] for h in heads], axis=0) for v in ref_vals]
    grp = lax.shift_right_logical(lax.broadcasted_iota(jnp.int32, (ATT_G * WINDOW, 1), 0), WINDOW.bit_length() - 1)
    slope = jnp.zeros((ATT_G * WINDOW, 1), F32)
    sink = jnp.zeros((ATT_G * WINDOW, 1), F32)
    for g, h in enumerate(heads):
        slope = jnp.where(grp == g, _slope(h, n_heads), slope)
        sink = jnp.where(grp == g, sink_ref[:, h:h + 1], sink)
    return stacked, slope, sink


def _swa_fwd(q, kv, sinks):
    m, d = q.shape
    n_heads = d // ATT_HD
    kvh = n_heads // ATT_G
    kd = kvh * ATT_HD
    nb = m // WINDOW

    def body(q_ref, kvp_ref, kvc_ref, sink_ref, o_ref, lse_ref):
        has_prev = pl.program_id(0) > 0
        qv, kvp, kvc = q_ref[...], kvp_ref[...], kvc_ref[...]
        lane_h = lax.broadcasted_iota(jnp.int32, (WINDOW, n_heads), 1)
        outs, lse_all = [], jnp.zeros((WINDOW, n_heads), F32)
        for kh in range(kvh):
            ks = slice(kh * ATT_HD, (kh + 1) * ATT_HD)
            vs = slice(kd + kh * ATT_HD, kd + (kh + 1) * ATT_HD)
            (q4,), slope, sink = _swa_group([qv], sink_ref, kh, n_heads)
            p_p, p_c, _, lse = _swa_probs(q4, kvp[:, ks], kvc[:, ks], sink, slope, has_prev)
            o4 = _dot(p_p, kvp[:, vs]) + _dot(p_c, kvc[:, vs])
            for g in range(ATT_G):
                rows = slice(g * WINDOW, (g + 1) * WINDOW)
                outs.append(o4[rows, :])
                lse_all = jnp.where(lane_h == kh * ATT_G + g, lse[rows, :], lse_all)
        o_ref[...] = jnp.concatenate(outs, axis=-1).astype(o_ref.dtype)
        lse_ref[...] = lse_all

    return pl.pallas_call(
        body, name="swa_fwd", grid=(nb,),
        in_specs=[pl.BlockSpec((WINDOW, d), lambda n: (n, 0)),
                  pl.BlockSpec((WINDOW, 2 * kd), lambda n: (jnp.maximum(n - 1, 0), 0)),
                  pl.BlockSpec((WINDOW, 2 * kd), lambda n: (n, 0)),
                  pl.BlockSpec((1, n_heads), lambda n: (0, 0))],
        out_specs=[pl.BlockSpec((WINDOW, d), lambda n: (n, 0)), pl.BlockSpec((WINDOW, n_heads), lambda n: (n, 0))],
        out_shape=[jax.ShapeDtypeStruct((m, d), BF16), jax.ShapeDtypeStruct((m, n_heads), F32)],
        compiler_params=_params(("arbitrary",)),
    )(q, kv, kv, sinks)


def _swa_bwd(q, kv, sinks, lse, dao):
    m, d = q.shape
    n_heads = d // ATT_HD
    kvh = n_heads // ATT_G
    kd = kvh * ATT_HD
    nb = m // WINDOW
    scale = ATT_HD ** -0.5

    def body(q_ref, kvp_ref, kvc_ref, sink_ref, lse_ref, do_ref, dq_ref, dkvc_ref, dkvp_ref, dqsum_ref, dsink_ref):
        @pl.when(pl.program_id(0) == 0)
        def _():
            dqsum_ref[...] = jnp.zeros(dqsum_ref.shape, F32)
            dsink_ref[...] = jnp.zeros(dsink_ref.shape, F32)

        has_prev = pl.program_id(0) > 0
        qv, kvp, kvc, dov = q_ref[...], kvp_ref[...], kvc_ref[...], do_ref[...]
        lane_h = lax.broadcasted_iota(jnp.int32, (1, n_heads), 1)
        dsink = jnp.zeros((1, n_heads), F32)
        dq_parts, dk_p, dk_c, dv_p, dv_c = [], [], [], [], []
        for kh in range(kvh):
            ks = slice(kh * ATT_HD, (kh + 1) * ATT_HD)
            vs = slice(kd + kh * ATT_HD, kd + (kh + 1) * ATT_HD)
            kp, kc, vp, vc = kvp[:, ks], kvc[:, ks], kvp[:, vs], kvc[:, vs]
            (q4, do4), slope, sink = _swa_group([qv, dov], sink_ref, kh, n_heads)
            lse4 = jnp.concatenate([lse_ref[:, kh * ATT_G + g:kh * ATT_G + g + 1] for g in range(ATT_G)], axis=0)
            p_p, p_c, p_s, _ = _swa_probs(q4, kp, kc, sink, slope, has_prev, lse=lse4)
            dp_p, dp_c = _dot_nt(do4, vp), _dot_nt(do4, vc)
            delta = jnp.sum(p_p * dp_p, axis=-1, keepdims=True) + jnp.sum(p_c * dp_c, axis=-1, keepdims=True)
            ds_p, ds_c = p_p * (dp_p - delta), p_c * (dp_c - delta)
            sink_term = p_s * delta
            dq4 = (_dot(ds_p, kp) + _dot(ds_c, kc)) * scale
            for g in range(ATT_G):
                rows = slice(g * WINDOW, (g + 1) * WINDOW)
                dsink = dsink + jnp.where(lane_h == kh * ATT_G + g, -_col_sum(sink_term[rows, :]), 0.0)
                dq_parts.append(dq4[rows, :])
            dk_p.append(_dot_tn(ds_p, q4) * scale)
            dk_c.append(_dot_tn(ds_c, q4) * scale)
            dv_p.append(_dot_tn(p_p, do4))
            dv_c.append(_dot_tn(p_c, do4))
        dq = jnp.concatenate(dq_parts, axis=-1)
        dq_ref[...] = dq.astype(dq_ref.dtype)
        dqsum_ref[...] += _col_sum(dq)
        dsink_ref[...] += dsink
        dkvc_ref[...] = jnp.concatenate(dk_c + dv_c, axis=-1)
        dkvp_ref[...] = jnp.concatenate(dk_p + dv_p, axis=-1)

    return pl.pallas_call(
        body, name="swa_bwd", grid=(nb,),
        in_specs=[pl.BlockSpec((WINDOW, d), lambda n: (n, 0)),
                  pl.BlockSpec((WINDOW, 2 * kd), lambda n: (jnp.maximum(n - 1, 0), 0)),
                  pl.BlockSpec((WINDOW, 2 * kd), lambda n: (n, 0)),
                  pl.BlockSpec((1, n_heads), lambda n: (0, 0)),
                  pl.BlockSpec((WINDOW, n_heads), lambda n: (n, 0)),
                  pl.BlockSpec((WINDOW, d), lambda n: (n, 0))],
        out_specs=[pl.BlockSpec((WINDOW, d), lambda n: (n, 0)),
                   pl.BlockSpec((WINDOW, 2 * kd), lambda n: (n, 0)),
                   pl.BlockSpec((WINDOW, 2 * kd), lambda n: (n, 0)),
                   pl.BlockSpec((1, d), lambda n: (0, 0)),
                   pl.BlockSpec((1, n_heads), lambda n: (0, 0))],
        out_shape=[jax.ShapeDtypeStruct((m, d), BF16), jax.ShapeDtypeStruct((m, 2 * kd), F32),
                   jax.ShapeDtypeStruct((m, 2 * kd), F32), jax.ShapeDtypeStruct((1, d), F32),
                   jax.ShapeDtypeStruct((1, n_heads), F32)],
        compiler_params=_params(("arbitrary",)),
    )(q, kv, kv, sinks, lse, dao)


def _kv_grad_combine(dkv_cur, dkv_prev):
    m, w = dkv_cur.shape
    nb = m // WINDOW

    def body(cur_ref, nxt_ref, o_ref, sum_ref):
        @pl.when(pl.program_id(0) == 0)
        def _():
            sum_ref[...] = jnp.zeros(sum_ref.shape, F32)

        total = cur_ref[...] + jnp.where(pl.program_id(0) < nb - 1, nxt_ref[...], 0.0)
        o_ref[...] = total.astype(o_ref.dtype)
        sum_ref[...] += _col_sum(total)

    return pl.pallas_call(
        body, name="kv_grad_combine", grid=(nb,),
        in_specs=[pl.BlockSpec((WINDOW, w), lambda n: (n, 0)),
                  pl.BlockSpec((WINDOW, w), lambda n: (jnp.minimum(n + 1, nb - 1), 0))],
        out_specs=[pl.BlockSpec((WINDOW, w), lambda n: (n, 0)), pl.BlockSpec((1, w), lambda n: (0, 0))],
        out_shape=[jax.ShapeDtypeStruct((m, w), BF16), jax.ShapeDtypeStruct((1, w), F32)],
        compiler_params=_params(("arbitrary",)),
    )(dkv_cur, dkv_prev)


def _row(v):
    return v.reshape(1, -1)


def _local_step(x, p, target, wget, grad_sink, ln_gain, ln_bias, alb, norm_gain, kv_b, b_q, sinks, b_out, ple_b):
    gs = {}
    gains = ln_gain.reshape(DEPTH * 3, -1)
    biases = ln_bias.reshape(DEPTH * 3, -1)
    sd = x.shape
    pending = [None]

    def mm(a, b, lb=0, **kw):
        after, pending[0] = pending[0], None
        return _mm(a, b, lb=lb, after=after, **kw)

    def mm_ln(a, wt, xin, i, j, nm, bias=None, pu=None):
        r = 3 * i + j
        if pu is None:
            fn, rows = (lambda h, xv, g, bv: (h,) + _ln_fwd_fn(xv, h, g[r:r + 1], bv[r:r + 1])), [xin]
        else:
            fn = lambda h, xv, puv, g, bv: (h,) + _ple_ln_fwd_fn(xv, h, puv, g[r:r + 1], bv[r:r + 1])
            rows = [xin, pu]
        h, y, yb = _mm(a, wt, lb=0, bias=bias, name=nm,
                       post=(fn, rows, [gains, biases], [(sd, F32), (sd, F32), (sd, BF16)], []))
        return h, (y, yb)

    def mm_ln_bwd(a, wt, add, xin, h, i, j, nm):
        r = 3 * i + j
        dx_part, dh, dg, db, dhsum = mm(a, wt, tb=True, add=add, name=nm,
                                        post=(lambda dy, xv, hv, g: _ln_bwd_fn(dy, xv, hv, g[r:r + 1]), [xin, h],
                                              [gains], [(sd, F32), (sd, BF16)], [((1, sd[1]), F32)] * 3))
        gs[f"ln_gain_{i}_{j}"], gs[f"ln_bias_{i}_{j}"] = dg, db
        return dx_part, dh, dhsum

    def tail_fwd(xa, i):
        wgu = wget("ffn_w_gate_up", i, xa[1])
        hid2 = wgu.shape[-1]
        gu, act = _mm(xa[1], wgu, lb=0, name=f"ffn_up_swiglu{i}", tile_cols=hid2 // 2,
                      post=(_swiglu_fwd_fn, [], [], [((sd[0], hid2), BF16), ((sd[0], hid2 // 2), BF16)], []))
        f, xb = mm_ln(act, wget("ffn_w_down", i, act), xa[0], i, 1, f"ffn_down_ln{i}")
        pu = _mm(p, wget("ple_w_up", i, act), la=i, lb=0, name=f"ple_up{i}")
        pg, xc = mm_ln(xb[1], wget("ple_w_gate", i, act), xb[0], i, 2, f"ple_gate_ln{i}", bias=_row(ple_b[i]), pu=pu)
        return dict(xa=xa, gu=gu, act=act, f=f, xb=xb, pg=pg, pu=pu), xc

    def tail_bwd(head, sv, i, mix_in, mix_h):
        xa, xb = sv["xa"], sv["xb"]
        r = 3 * i + 2
        dxb_part, dpg, dpu, dg2, db2, dbg = head(
            lambda dy, xv, pgv, puv, g: _ple_ln_bwd_fn(dy, xv, pgv, puv, g[r:r + 1]), [xb[0], sv["pg"], sv["pu"]],
            [gains], [(sd, F32), (sd, BF16), (sd, BF16)], [((1, sd[1]), F32)] * 3)[:6]
        gs[f"ple_b_{i}"] = dbg
        gs[f"ln_gain_{i}_2"], gs[f"ln_bias_{i}_2"] = dg2, db2
        grad_of("ple_w_gate", i, xb[1], dpg)
        grad_of("ple_w_up", i, p, dpu, la=i)
        dxa_part, df, _ = mm_ln_bwd(dpg, wget("ple_w_gate", i, None), dxb_part, xa[0], sv["f"], i, 1,
                                    f"ple_gate_dx_ln{i}")
        grad_of("ffn_w_down", i, sv["act"], df)
        gu = sv["gu"]
        dgu, = mm(df, wget("ffn_w_down", i, None), tb=True, name=f"ffn_down_dx_swiglu{i}", tile_cols=gu.shape[1] // 4,
                  post=(_swiglu_bwd_fn, [gu], [], [(gu.shape, BF16)], []))
        grad_of("ffn_w_gate_up", i, xa[1], dgu)
        return mm_ln_bwd(dgu, wget("ffn_w_gate_up", i, None), dxa_part, mix_in, mix_h, i, 0, f"ffn_up_dx_ln{i}")

    def grad_of(nm, i, act, dout, la=None):
        grad = mm(act, dout, la=la, lb=None, ta=True, out_dtype=BF16, out_layers=1, out_layer=0,
                  name=f"grad_{nm}{i}")
        token = grad_sink(nm, i, grad)
        if token is not None:
            pending[0] = token

    proj = _mm(x, wget("a_w_in", 0, None), lb=0, name="hg_proj")
    o_pre, og, states, scores = _hgrn2_fwd(proj, alb, norm_gain, rb=HG_ROWS)
    h0, x1 = mm_ln(og, wget("a_w_out", 0, og), x, 0, 0, "hg_out_ln")
    sv0, x3 = tail_fwd(x1, 0)
    kv = _mm(x3[1], wget("kv_w", 0, x3[1]), lb=0, bias=_row(kv_b), out_dtype=BF16, name="kv_proj")
    q = _mm(x3[1], wget("b_w_q", 0, x3[1]), lb=0, bias=b_q, out_dtype=BF16, name="q_proj")
    ao, lse = _swa_fwd(q, kv, sinks)
    h1, x4 = mm_ln(ao, wget("b_w_out", 0, x3[1]), x3[0], 1, 0, "att_out_ln", bias=b_out)
    sv1, y = tail_fwd(x4, 1)

    loss_box = []

    def loss_head(fn, rows, whole, outs, sums):
        def with_loss(yv, tv, *rest):
            dy, part = _loss_fn(yv, tv)
            return fn(dy, *rest) + (part,)

        res = _rowwise(with_loss, [y[0], target] + rows, whole, outs, list(sums) + [((1, LANES), F32)],
                       name="loss_ln_ple_bwd1")
        loss_box.append(res[-1])
        return res

    dx3_part, dh1, dh1sum = tail_bwd(loss_head, sv1, 1, x3[0], h1)
    loss = loss_box[0]
    gs["b_out"] = dh1sum
    grad_of("b_w_out", 0, ao, dh1)
    dao = mm(dh1, wget("b_w_out", 0, None), tb=True, out_dtype=BF16, name="att_out_dx")
    dq, dkv_cur, dkv_prev, dqsum, dsinks = _swa_bwd(q, kv, sinks, lse, dao)
    gs["b_q"], gs["sinks"] = dqsum, dsinks
    dkv, dkvsum = _kv_grad_combine(dkv_cur, dkv_prev)
    gs["kv_b"] = dkvsum
    grad_of("b_w_q", 0, x3[1], dq)
    grad_of("kv_w", 0, x3[1], dkv)
    dx3 = mm(dq, wget("b_w_q", 0, None), tb=True, add=dx3_part, name="q_proj_dx")

    def kv_head(*post):
        return mm(dkv, wget("kv_w", 0, None), tb=True, add=dx3, name="kv_proj_dx_ln_ple_bwd0", post=post)

    dx_part, dh0, _ = tail_bwd(kv_head, sv0, 0, x, h0)
    grad_of("a_w_out", 0, og, dh0)
    dog = mm(dh0, wget("a_w_out", 0, None), tb=True, name="hg_out_dx")
    dqr, dfr, dvr, dgr, dalb, dgain = _hgrn2_bwd(proj, o_pre, states, scores, dog, alb, norm_gain, rb=HG_ROWS)
    gs["alb"], gs["norm_gain"] = dalb, dgain
    dproj = jnp.concatenate([dqr, dfr, dvr, dgr], axis=1)
    grad_of("a_w_in", 0, x, dproj)
    grad_x = mm(dproj, wget("a_w_in", 0, None), tb=True, add=dx_part, name="hg_proj_dx")
    return loss, grad_x, gs


HBM_SPEC = pl.BlockSpec(memory_space=pl.ANY)
HBM_ONLY = pl.BlockSpec(memory_space=pltpu.HBM)
SEM_SPEC = pl.BlockSpec(memory_space=pltpu.SEMAPHORE)
SIDE_EFFECT = pltpu.SideEffectType.DATAFLOW_SIDE_EFFECTING


def _slot(kind, j):
    return (j % 2) * 2 + j // 2 if kind == "colp" else j


def _piece(ref, kind, j):
    _, r, c = ref.shape
    if kind == "row":
        return ref.at[:, pl.ds(j * (r // N_CHIPS), r // N_CHIPS), :]
    return ref.at[:, :, pl.ds(_slot(kind, j) * (c // N_CHIPS), c // N_CHIPS)]


def _piece_dyn(ref, kind, j):
    _, r, c = ref.shape
    if kind == "row":
        return ref.at[:, pl.ds(pl.multiple_of(j * (r // N_CHIPS), 16), r // N_CHIPS), :]
    return ref.at[:, :, pl.ds(pl.multiple_of(_slot(kind, j) * (c // N_CHIPS), LANES), c // N_CHIPS)]


def _chip_of(j, c):
    return (j // 2, j % 2, c)


def _in_hbm(a):
    return pltpu.with_memory_space_constraint(a, pltpu.HBM)


def _place(src, layer, kind, chip, *, name, out_dtype, after=None):
    _, r, c = src.shape
    out_shape = (1, r * N_CHIPS, c) if kind == "row" else (1, r, c * N_CHIPS)
    tm = _pick_rows(r, 512)
    nb = r // tm

    def full_idx(i, chip_ref):
        return (0, chip_ref[0] * nb + i, 0) if kind == "row" else (0, i, _slot(kind, chip_ref[0]))

    in_specs, operands = [pl.BlockSpec((None, tm, c), lambda i, chip_ref: (layer, i, 0))], [src]
    if after is not None:
        in_specs.append(HBM_SPEC)
        operands.append(after)

    def body(chip_ref, src_ref, *rest):
        rest[-1][...] = src_ref[...].astype(rest[-1].dtype)

    return pl.pallas_call(
        body, name=name,
        grid_spec=pltpu.PrefetchScalarGridSpec(num_scalar_prefetch=1, grid=(nb,), in_specs=in_specs,
                                               out_specs=pl.BlockSpec((None, tm, c), full_idx)),
        out_shape=jax.ShapeDtypeStruct(out_shape, out_dtype),
        compiler_params=_params(("arbitrary",)),
    )(chip, *operands)


def _half(ref, c):
    h = ref.shape[1] // 2
    start = c * h if isinstance(c, int) else pl.multiple_of(c * h, 16)
    return ref.at[:, pl.ds(start, h), :]


class _SiblingFill:
    def __init__(self, lands, kinds, name):
        self.kinds, self.name, self.n = kinds, name, len(lands)
        n = self.n
        sem_shape = pltpu.SemaphoreType.DMA((n * N_CHIPS,))

        def body(*refs):
            land_refs, send_sems, recv_sems, token = refs[:n], refs[n], refs[n + 1], refs[-1]
            for cp in self._copies(land_refs, send_sems, recv_sems):
                cp.start()
            token[...] = jnp.zeros(token.shape, token.dtype)

        outs = pl.pallas_call(
            body, name=name + "_start",
            in_specs=[HBM_ONLY] * n,
            out_specs=[SEM_SPEC, SEM_SPEC] + [HBM_ONLY] * n + [pl.BlockSpec(memory_space=pltpu.VMEM)],
            out_shape=[sem_shape, sem_shape] + [pltpu.HBM(a.shape, a.dtype) for a in lands]
                      + [jax.ShapeDtypeStruct((8, LANES), F32)],
            input_output_aliases={i: i + 2 for i in range(n)},
            compiler_params=pltpu.CompilerParams(has_side_effects=SIDE_EFFECT),
        )(*[_in_hbm(a) for a in lands])
        self.send_sems, self.recv_sems, self.lands, self.token = outs[0], outs[1], list(outs[2:2 + n]), outs[-1]

    def _copies(self, land_refs, send_sems, recv_sems):
        x, y, c = lax.axis_index("x"), lax.axis_index("y"), lax.axis_index("c")
        me = 2 * x + y
        copies = []
        for a in range(self.n):
            for k in range(1, N_CHIPS):
                t = (me + k) % N_CHIPS
                slice_t = _piece_dyn(land_refs[a], self.kinds[a], t)
                got = _half(slice_t, c)
                copies.append(pltpu.make_async_remote_copy(
                    src_ref=got, dst_ref=got, send_sem=send_sems.at[a * N_CHIPS + k],
                    recv_sem=recv_sems.at[a * N_CHIPS + k], device_id=(x, y, 1 - c), device_id_type=MESH))
        return copies

    def wait(self, after):
        n = self.n

        def body(*refs):
            land_refs, send_sems, recv_sems = refs[:n], refs[n], refs[n + 1]
            for cp in self._copies(land_refs, send_sems, recv_sems):
                cp.wait_send()
                cp.wait_recv()

        operands = [_in_hbm(a) for a in self.lands] + [self.send_sems, self.recv_sems]
        in_specs = [HBM_ONLY] * n + [SEM_SPEC, SEM_SPEC]
        if after is not None:
            operands.append(after)
            in_specs.append(HBM_SPEC)
        outs = pl.pallas_call(
            body, name=self.name + "_wait",
            in_specs=in_specs, out_specs=[HBM_ONLY] * n,
            out_shape=[pltpu.HBM(a.shape, a.dtype) for a in self.lands],
            input_output_aliases={i: i for i in range(n)},
            compiler_params=pltpu.CompilerParams(has_side_effects=SIDE_EFFECT),
        )(*operands)
        return list(outs)


class _Exchange:
    def __init__(self, mode, srcs, lands, kinds, layers, name, after=None, halves=None):
        self.mode, self.kinds, self.layers, self.name, self.n = mode, kinds, layers, name, len(lands)
        self.halves = halves if halves is not None else [False] * len(lands)
        n, ns = self.n, len(srcs)
        n_in = ns + n + (after is not None)
        sem_shape = pltpu.SemaphoreType.DMA((n * N_CHIPS,))

        def body(*refs):
            src_refs, land_refs = refs[:ns], refs[ns:ns + n]
            send_sems, recv_sems = refs[n_in], refs[n_in + 1]
            token = refs[-1]
            c = lax.axis_index("c")
            me = 2 * lax.axis_index("x") + lax.axis_index("y")
            for j in range(N_CHIPS):
                @pl.when(me == j)
                def _():
                    for a in range(n):
                        for t in range(N_CHIPS):
                            if t != j:
                                src, dst = self._ends(src_refs, land_refs, a, j, t, c)
                                pltpu.make_async_remote_copy(
                                    src_ref=src, dst_ref=dst, send_sem=send_sems.at[a * N_CHIPS + t],
                                    recv_sem=recv_sems.at[a * N_CHIPS + j],
                                    device_id=_chip_of(t, c), device_id_type=MESH).start()
            token[...] = jnp.zeros(token.shape, token.dtype)

        arrays = list(srcs) + list(lands)
        operands = [_in_hbm(a) for a in arrays]
        in_specs = [HBM_ONLY] * (ns + n)
        if after is not None:
            operands.append(after)
            in_specs.append(HBM_SPEC)
        outs = pl.pallas_call(
            body, name=name + "_start",
            in_specs=in_specs,
            out_specs=[SEM_SPEC, SEM_SPEC] + [HBM_ONLY] * (ns + n) + [pl.BlockSpec(memory_space=pltpu.VMEM)],
            out_shape=[sem_shape, sem_shape] + [pltpu.HBM(a.shape, a.dtype) for a in arrays]
                      + [jax.ShapeDtypeStruct((8, LANES), F32)],
            input_output_aliases={i: i + 2 for i in range(ns + n)},
            compiler_params=pltpu.CompilerParams(has_side_effects=SIDE_EFFECT),
        )(*operands)
        self.send_sems, self.recv_sems = outs[0], outs[1]
        self.srcs, self.lands = list(outs[2:2 + ns]), list(outs[2 + ns:2 + ns + n])
        self.token = outs[-1]

    def _ends(self, src_refs, land_refs, a, me_j, peer, c):
        if self.mode == "gather":
            mine = _piece(land_refs[a], self.kinds[a], me_j)
            if self.halves[a]:
                mine = _half(mine, c)
            return mine, mine
        return _piece(src_refs[a], self.kinds[a], peer), land_refs[a].at[me_j, pl.ds(self.layers[a], 1)]

    def wait(self, after, lands=None):
        n, ns = self.n, len(self.srcs)
        lands = self.lands if lands is None else lands

        def body(*refs):
            src_refs, land_refs = refs[:ns], refs[ns:ns + n]
            send_sems, recv_sems = refs[ns + n], refs[ns + n + 1]
            c = lax.axis_index("c")
            me = 2 * lax.axis_index("x") + lax.axis_index("y")
            for j in range(N_CHIPS):
                @pl.when(me != j)
                def _():
                    for a in range(n):
                        sent, _ = self._ends(src_refs, land_refs, a, 0, j, c)
                        _, landed = self._ends(src_refs, land_refs, a, j, 0, c)
                        cp = pltpu.make_async_remote_copy(
                            src_ref=sent, dst_ref=landed, send_sem=send_sems.at[a * N_CHIPS + j],
                            recv_sem=recv_sems.at[a * N_CHIPS + j],
                            device_id=_chip_of(j, c), device_id_type=MESH)
                        cp.wait_send()
                        cp.wait_recv()

        arrays = self.srcs + list(lands)
        operands = [_in_hbm(a) for a in arrays] + [self.send_sems, self.recv_sems]
        in_specs = [HBM_ONLY] * (ns + n) + [SEM_SPEC, SEM_SPEC]
        if after is not None:
            operands.append(after)
            in_specs.append(HBM_SPEC)
        outs = pl.pallas_call(
            body, name=self.name + "_wait",
            in_specs=in_specs, out_specs=[HBM_ONLY] * (ns + n),
            out_shape=[pltpu.HBM(a.shape, a.dtype) for a in arrays],
            input_output_aliases={i: i for i in range(ns + n)},
            compiler_params=pltpu.CompilerParams(has_side_effects=SIDE_EFFECT),
        )(*operands)
        return list(outs[:ns]), list(outs[ns:])


def _sum_arrivals(zone, own_grads, kind, chip, name):
    _, layers, r, c = zone.shape
    tm = _pick_rows(r, 256)
    nb = r // tm

    def own_idx(l, i, chip_ref):
        return (0, chip_ref[0] * nb + i, 0) if kind == "row" else (0, i, _slot(kind, chip_ref[0]))

    def slot_idx(k):
        return lambda l, i, chip_ref: (jnp.where(chip_ref[0] == k, (k + 1) % N_CHIPS, k), l, i, 0)

    in_specs = [pl.BlockSpec((None, None, tm, c), slot_idx(k)) for k in range(N_CHIPS)]
    in_specs += [pl.BlockSpec((None, tm, c), own_idx) for _ in own_grads]

    def body(chip_ref, *refs):
        slot_refs, own_refs, o_ref = refs[:N_CHIPS], refs[N_CHIPS:N_CHIPS + layers], refs[-1]
        own = own_refs[0][...]
        for u in range(1, layers):
            own = jnp.where(pl.program_id(0) == u, own_refs[u][...], own)
        acc = None
        for k in range(N_CHIPS):
            term = jnp.where(chip_ref[0] == k, own, slot_refs[k][...]).astype(F32)
            acc = term if acc is None else acc + term
        o_ref[...] = acc.astype(o_ref.dtype)

    return pl.pallas_call(
        body, name=name,
        grid_spec=pltpu.PrefetchScalarGridSpec(
            num_scalar_prefetch=1, grid=(layers, nb), in_specs=in_specs,
            out_specs=pl.BlockSpec((tm, c), lambda l, i, chip_ref: (l * nb + i, 0))),
        out_shape=jax.ShapeDtypeStruct((layers * r, c), BF16),
        compiler_params=_params(("arbitrary", "arbitrary")),
    )(chip, zone, zone, zone, zone, *own_grads)


def _sibling_swap(arrays, name):
    n = len(arrays)

    def body(*refs):
        ins, outs = refs[:n], refs[n:2 * n]
        send_sems, recv_sems = refs[2 * n:]
        sibling = (lax.axis_index("x"), lax.axis_index("y"), 1 - lax.axis_index("c"))
        copies = [pltpu.make_async_remote_copy(src_ref=ins[a], dst_ref=outs[a], send_sem=send_sems.at[a],
                                               recv_sem=recv_sems.at[a], device_id=sibling, device_id_type=MESH)
                  for a in range(n)]
        for cp in copies:
            cp.start()
        for cp in copies:
            cp.wait()

    return pl.pallas_call(
        body, name=name,
        in_specs=[HBM_SPEC] * n, out_specs=[HBM_SPEC] * n,
        out_shape=[jax.ShapeDtypeStruct(a.shape, a.dtype) for a in arrays],
        scratch_shapes=[pltpu.SemaphoreType.DMA((n,)), pltpu.SemaphoreType.DMA((n,))],
    )(*arrays)


def _gather_devices(vec):
    def body(in_ref, out_ref, send_sems, recv_sems, local_sem):
        x, y, c = lax.axis_index("x"), lax.axis_index("y"), lax.axis_index("c")
        me = 4 * x + 2 * y + c
        mine = pltpu.make_async_copy(in_ref, out_ref.at[me], local_sem)
        mine.start()
        copies = []
        for rel in range(1, N_DEV):
            peer = (x ^ (rel >> 2), y ^ ((rel >> 1) & 1), c ^ (rel & 1))
            copies.append(pltpu.make_async_remote_copy(
                src_ref=in_ref, dst_ref=out_ref.at[me], send_sem=send_sems.at[rel], recv_sem=recv_sems.at[rel],
                device_id=peer, device_id_type=MESH))
        for cp in copies:
            cp.start()
        for cp in copies:
            cp.wait()
        mine.wait()

    return pl.pallas_call(
        body, name="gather_small",
        in_specs=[HBM_SPEC], out_specs=HBM_SPEC,
        out_shape=jax.ShapeDtypeStruct((N_DEV,) + vec.shape, vec.dtype),
        scratch_shapes=[pltpu.SemaphoreType.DMA((N_DEV,)), pltpu.SemaphoreType.DMA((N_DEV,)),
                        pltpu.SemaphoreType.DMA],
    )(vec)


BIG = [("a_w_in", "col"), ("a_w_out", "row"), ("kv_w", "row"), ("b_w_q", "row"), ("b_w_out", "row"),
       ("ffn_w_gate_up", "colp"), ("ffn_w_down", "row"), ("ple_w_up", "col"), ("ple_w_gate", "row")]
GATHER_GROUPS = [[("a_w_in", 0), ("small", 0)], [("a_w_out", 0), ("ffn_w_gate_up", 0)],
                 [("ffn_w_down", 0), ("ple_w_gate", 0), ("ple_w_up", 0)], [("kv_w", 0), ("b_w_q", 0), ("b_w_out", 0)],
                 [("ffn_w_gate_up", 1)], [("ffn_w_down", 1), ("ple_w_gate", 1), ("ple_w_up", 1)]]
SCATTER_GROUPS = [[("ple_w_gate", 1), ("ple_w_up", 1), ("ffn_w_down", 1)], [("ffn_w_gate_up", 1)],
                  [("b_w_out", 0), ("b_w_q", 0), ("kv_w", 0)], [("ple_w_gate", 0), ("ple_w_up", 0), ("ffn_w_down", 0)],
                  [("ffn_w_gate_up", 0), ("a_w_out", 0)], [("a_w_in", 0)]]
SMALL_SHARDED = ["ln_gain", "ln_bias", "a_lower_bound"]
SMALL_REPLICATED = ["a_norm_gain", "kv_b", "b_b_q", "b_sinks", "b_b_out", "ple_b_gate"]
WEIGHT_ORDER = ["a_w_in", "a_lower_bound", "a_norm_gain", "a_w_out", "kv_w", "kv_b", "b_w_q", "b_b_q", "b_sinks",
                "b_w_out", "b_b_out", "ffn_w_gate_up", "ffn_w_down", "ple_w_up", "ple_w_gate", "ple_b_gate",
                "ln_gain", "ln_bias"]


def _as3(a):
    return a.reshape((-1,) + a.shape[-2:]) if a.ndim >= 3 else a.reshape((1,) + a.shape)


def _pad_lanes(v):
    n = v.shape[-1]
    return jnp.pad(v, ((0, 0), (0, (-n) % LANES)))


def _adam_small_fn(w, mom, vel, g):
    return _adam_fn(w, mom, vel, g, jnp.zeros_like(g))[1:]


def _sum_rows_fn(slots):
    acc = slots[0]
    for s in range(1, slots.shape[0]):
        acc = acc + slots[s]
    return (acc,)


def kernel(x, p, a_w_in, a_lower_bound, a_norm_gain, a_w_out, kv_w, kv_b, b_w_q, b_b_q, b_sinks, b_w_out, b_b_out, ffn_w_gate_up, ffn_w_down, ple_w_up, ple_w_gate, ple_b_gate, ln_gain, ln_bias, loss_target, m_a_w_in, m_a_lower_bound, m_a_norm_gain, m_a_w_out, m_kv_w, m_kv_b, m_b_w_q, m_b_b_q, m_b_sinks, m_b_w_out, m_b_b_out, m_ffn_w_gate_up, m_ffn_w_down, m_ple_w_up, m_ple_w_gate, m_ple_b_gate, m_ln_gain, m_ln_bias, v_a_w_in, v_a_lower_bound, v_a_norm_gain, v_a_w_out, v_kv_w, v_kv_b, v_b_w_q, v_b_b_q, v_b_sinks, v_b_w_out, v_b_b_out, v_ffn_w_gate_up, v_ffn_w_down, v_ple_w_up, v_ple_w_gate, v_ple_b_gate, v_ln_gain, v_ln_bias):
    args = dict(locals())
    wts = {n: args[n] for n in WEIGHT_ORDER}
    mom = {n: args["m_" + n] for n in WEIGHT_ORDER}
    vel = {n: args["v_" + n] for n in WEIGHT_ORDER}
    chip = 2 * lax.axis_index("x") + lax.axis_index("y")
    d = x.shape[-1]
    dq = d // N_CHIPS

    kind_of = dict(BIG)
    kind_of["small"] = "col"
    chip_arr = chip.reshape(1).astype(jnp.int32)
    small_pack = jnp.concatenate([wts[n].reshape(-1, dq) for n in SMALL_SHARDED], axis=0)[None]

    def place(key, after):
        n, layer = key
        if n == "small":
            return _place(small_pack, 0, "col", chip_arr, name="place_small", out_dtype=F32, after=after)
        return _place(_as3(wts[n]), layer, kind_of[n], chip_arr, name=f"place_{n}{layer}", out_dtype=BF16,
                      after=after)

    gathers, where = [], {}
    for gi, group in enumerate(GATHER_GROUPS):
        prev = gathers[-1].token if gathers else None
        gathers.append(_Exchange("gather", [], [place(k, prev) for k in group], [kind_of[k[0]] for k in group],
                                 [0] * len(group), f"gather{gi}", after=prev,
                                 halves=[k[0] != "small" for k in group]))
        for k in group:
            where[k] = gi
    all_started = gathers[-1].token
    ready = {}

    fills = {}

    def pass_on(gi, after):
        if gi not in fills:
            group = GATHER_GROUPS[gi]
            outs = gathers[gi].wait(after)[1]
            split = [i for i, k in enumerate(group) if k[0] != "small"]
            fills[gi] = (outs, split, _SiblingFill([outs[i] for i in split], [kind_of[group[i][0]] for i in split],
                                                   f"fill{gi}"))

    def wget(name, layer, after):
        key = (name, layer)
        if key not in ready:
            gi = where[key]
            after = all_started if gi == 0 else after
            pass_on(gi, after)
            if 1 <= gi < len(GATHER_GROUPS) - 1:
                pass_on(gi + 1, after)
            outs, split, fill = fills[gi]
            for i, arr in zip(split, fill.wait(after)):
                outs[i] = arr
            for k, arr in zip(GATHER_GROUPS[gi], outs):
                ready[k] = arr
        return ready[key]

    small_full = wget("small", 0, None)[0]
    ln_gain_f = small_full[0:6].reshape(DEPTH, 3, d)
    ln_bias_f = small_full[6:12].reshape(DEPTH, 3, d)
    alb_f = small_full[12:14]

    group_of = {k: gi for gi, group in enumerate(SCATTER_GROUPS) for k in group}
    grads_done, zones, scatters = {}, {}, []

    def grad_sink(name, layer, grad):
        grads_done[(name, layer)] = grad
        if name not in zones:
            zones[name] = lax.empty((N_CHIPS,) + _as3(wts[name]).shape, BF16)
        gi = group_of[(name, layer)]
        group = SCATTER_GROUPS[gi]
        if not all(k in grads_done for k in group):
            return None
        ex = _Exchange("scatter", [grads_done[k] for k in group], [zones[k[0]] for k in group],
                       [kind_of[k[0]] for k in group], [k[1] for k in group], f"scatter{gi}")
        for k, zone in zip(group, ex.lands):
            zones[k[0]] = zone
        scatters.append((ex, group))
        return ex.token

    loss, grad_x, gs = _local_step(
        x[0], p.reshape((p.shape[0],) + p.shape[2:]), loss_target[0], wget, grad_sink, ln_gain_f, ln_bias_f, alb_f, a_norm_gain, kv_b, b_b_q,
        b_sinks, b_b_out, ple_b_gate)

    res = {}

    def arrive(batch, after):
        for ex, group in batch:
            srcs, outs = ex.wait(after, lands=[zones[k[0]] for k in group])
            for k, grad, zone in zip(group, srcs, outs):
                grads_done[k], zones[k[0]] = grad, zone

    def update(names, tag):
        partial = []
        for n in names:
            own = [grads_done[(n, layer)] for layer in range(zones[n].shape[1])]
            partial.append(_sum_arrivals(zones[n], own, kind_of[n], chip_arr, f"sum_{n}"))
        sibling = _sibling_swap(partial, tag)
        for n, own, sib in zip(names, partial, sibling):
            shp = wts[n].shape
            flat = lambda a: a.reshape(-1, shp[-1])
            out = _rowwise(_adam_fn, [flat(wts[n]), flat(mom[n]), flat(vel[n]), own, sib], [],
                           [(own.shape, F32)] * 4, name=f"adam_{n}")
            res[n] = [o.reshape(shp) for o in out]
        return res[names[-1]][1]

    last_names = [k[0] for k in SCATTER_GROUPS[-1]]
    arrive(scatters[:-1], grad_x)
    updated = update([n for n, _ in BIG if n not in last_names], "sibling_swap")
    arrive(scatters[-1:], updated)
    update(last_names, "sibling_swap_last")

    ln_g = jnp.concatenate([gs[f"ln_gain_{i}_{j}"] for i in range(DEPTH) for j in range(3)], axis=0)
    ln_b = jnp.concatenate([gs[f"ln_bias_{i}_{j}"] for i in range(DEPTH) for j in range(3)], axis=0)
    ple_bg = jnp.concatenate([gs[f"ple_b_{i}"] for i in range(DEPTH)], axis=0)
    small_list = [ln_g.reshape(1, -1), ln_b.reshape(1, -1), gs["alb"].reshape(1, -1), gs["norm_gain"],
                  gs["kv_b"], gs["b_q"], _pad_lanes(gs["sinks"]), gs["b_out"], ple_bg.reshape(1, -1), loss]
    small_vec = jnp.concatenate(small_list, axis=1)
    everyone = _gather_devices(small_vec)
    total, = _rowwise(_sum_rows_fn, [everyone], [], [(small_vec.shape, F32)], name="sum_small")
    offs, pos = [], 0
    for v in small_list:
        offs.append((pos, v.shape[1]))
        pos += v.shape[1]

    def seg(k):
        return total[0, offs[k][0]:offs[k][0] + offs[k][1]]

    def my_cols(full, rows):
        return lax.dynamic_slice_in_dim(full.reshape(rows, N_CHIPS, dq), chip, 1, axis=1).reshape(rows, dq)

    n_sink = b_sinks.shape[-1]
    small_grads = {
        "ln_gain": my_cols(seg(0), 6).reshape(ln_gain.shape), "ln_bias": my_cols(seg(1), 6).reshape(ln_bias.shape),
        "a_lower_bound": my_cols(seg(2), 2), "a_norm_gain": seg(3).reshape(a_norm_gain.shape),
        "kv_b": seg(4).reshape(kv_b.shape), "b_b_q": seg(5).reshape(b_b_q.shape),
        "b_sinks": seg(6)[:n_sink].reshape(b_sinks.shape), "b_b_out": seg(7).reshape(b_b_out.shape),
        "ple_b_gate": seg(8).reshape(ple_b_gate.shape)}
    names = SMALL_SHARDED + SMALL_REPLICATED
    pack = lambda dct: _pad_lanes(jnp.concatenate([dct[n].reshape(1, -1) for n in names], axis=1))
    g_pack = pack(small_grads)
    upd = _rowwise(_adam_small_fn, [pack(wts), pack(mom), pack(vel), g_pack], [], [(g_pack.shape, F32)] * 3,
                   name="adam_small")
    pos = 0
    for n in names:
        size = wts[n].size
        res[n] = [small_grads[n]] + [u[0, pos:pos + size].reshape(wts[n].shape) for u in upd]
        pos += size

    outs = [seg(9)[0], grad_x[None]]
    for k in range(4):
        outs += [res[n][k] for n in WEIGHT_ORDER]
    return tuple(outs)
```

```python
import functools

import jax
import jax.numpy as jnp
from jax import lax
from jax.experimental import pallas as pl
from jax.experimental.pallas import tpu as pltpu

F32 = jnp.float32
BF16 = jnp.bfloat16
MESH = pl.DeviceIdType.MESH

LANES = 128
HG_DK = 128
HG_CHUNK = 64
HG_SUB = 16
HG_ROWS = 512
HG_HEADS_PER_STEP = 2
ATT_HD = 64
ATT_G = 4
WINDOW = 128
DEPTH = 2
ALPHA = (2.0 * DEPTH) ** 0.25
LN_EPS = 1e-5
RMS_EPS = 1e-6
ADAM_LR, ADAM_B1, ADAM_B2, ADAM_EPS, ADAM_WD, ADAM_STEP = 0.001, 0.9, 0.999, 1e-08, 0.01, 10
N_CHIPS = 4
N_DEV = 8
VMEM_LIMIT = 56 * 1024 * 1024
NEG = -1e30


def _pick(n, cap):
    best = None
    for d in range(LANES, min(n, cap) + 1, LANES):
        if n % d == 0:
            best = d
    return n if best is None else best


def _pick_rows(m, cap):
    best = None
    for d in range(16, min(m, cap) + 1, 16):
        if m % d == 0:
            best = d
    return m if best is None else best


def _params(sem):
    return pltpu.CompilerParams(dimension_semantics=sem, vmem_limit_bytes=VMEM_LIMIT)


def _zeros_index(ndim, grid_rank=3):
    return (lambda i, j, kk: (0,) * ndim) if grid_rank == 3 else (lambda kk, i: (0,) * ndim)


def _mm(a, b, *, name, la=None, lb=None, ta=False, tb=False, bias=None, add=None, out_dtype=F32,
        out_layers=None, out_layer=None, after=None, post=None, tile_cols=None, caps=(1024, 1536, 2048)):
    ar, ac = a.shape[-2:]
    br, bc = b.shape[-2:]
    m, k = (ac, ar) if ta else (ar, ac)
    k2, n = (bc, br) if tb else (br, bc)
    assert k == k2, (a.shape, b.shape, ta, tb)
    if post is not None:
        caps = (512, n if tile_cols is None else tile_cols, caps[2])
    tm, tn, tk = _pick(m, caps[0]), _pick(n, caps[1]), _pick(k, caps[2])
    assert post is None or tn == caps[1]
    nk = k // tk
    gi, gj = m // tm, n // tn
    a_bytes, b_bytes = m * k * a.dtype.itemsize, k * n * b.dtype.itemsize
    rows_outer = (a_bytes + b_bytes * (gi if gj * nk > 1 else 1)) <= (b_bytes + a_bytes * (gj if gi * nk > 1 else 1))
    k_outer = post is not None and nk > 1 and gj == 1
    grid = (nk, gi) if k_outer else (gi, gj, nk) if rows_outer else (gj, gi, nk)
    keep_at = ta and nk == 1 and gj > 1 and rows_outer

    def bs(block, idx, late=False):
        if k_outer:
            return pl.BlockSpec(block, lambda kk, i: idx(jnp.where(kk == nk - 1, i, 0) if late else i, 0, kk))
        return pl.BlockSpec(block, idx if rows_outer else (lambda q, p, kk: idx(p, q, kk)))

    def spec(block, idx, layer):
        if layer is None:
            return bs(block, idx)
        return bs((None,) + block, lambda i, j, kk: (layer,) + idx(i, j, kk))

    a_spec = spec((tk, tm), lambda i, j, kk: (kk, i), la) if ta else spec((tm, tk), lambda i, j, kk: (i, kk), la)
    b_spec = spec((tn, tk), lambda i, j, kk: (j, kk), lb) if tb else spec((tk, tn), lambda i, j, kk: (kk, j), lb)
    in_specs, operands = [a_spec, b_spec], [a, b]
    if bias is not None:
        in_specs.append(bs((1, tn), lambda i, j, kk: (0, j)))
        operands.append(bias)
    if add is not None:
        in_specs.append(bs((tm, tn), lambda i, j, kk: (i, j), late=True))
        operands.append(add)
    if after is not None:
        in_specs.append(pl.BlockSpec(memory_space=pl.ANY))
        operands.append(after)
    dims = (((0 if ta else 1,), (1 if tb else 0,)), ((), ()))
    has_bias, has_add = bias is not None, add is not None
    if post is None:
        fn, rows, whole, outs, sums = None, [], [], [], []
        out_shape = jax.ShapeDtypeStruct((m, n) if out_layers is None else (out_layers, m, n), out_dtype)
        out_specs = spec((tm, tn), lambda i, j, kk: (i, j), out_layer)
    else:
        fn, rows, whole, outs, sums = post
        in_specs += [bs((tm, r.shape[-1] // gj), lambda i, j, kk: (i, j), late=True) for r in rows]
        in_specs += [pl.BlockSpec(tuple(w.shape), _zeros_index(w.ndim, len(grid))) for w in whole]
        operands += list(rows) + list(whole)
        out_shape = [jax.ShapeDtypeStruct(sh, dt) for sh, dt in list(outs) + list(sums)]
        out_specs = ([bs((tm, sh[-1] // gj), lambda i, j, kk: (i, j), late=True) for sh, _ in outs]
                     + [pl.BlockSpec(tuple(sh), _zeros_index(len(sh), len(grid))) for sh, _ in sums])
    n_in, n_extra, n_outs, n_sums = len(operands), len(rows) + len(whole), len(outs), len(sums)

    def body(*refs):
        a_ref, b_ref = refs[0], refs[1]
        pos = 2
        bias_ref = add_ref = None
        if has_bias:
            bias_ref = refs[pos]
            pos += 1
        if has_add:
            add_ref = refs[pos]
            pos += 1
        extra_refs = refs[n_in - n_extra:n_in]
        out_refs = refs[n_in:n_in + max(n_outs, 1)]
        sum_refs = refs[n_in + n_outs:n_in + n_outs + n_sums]
        acc_ref = refs[-1] if nk > 1 else None
        if keep_at:
            at_ref = refs[-1]

            @pl.when(pl.program_id(1) == 0)
            def _():
                at_ref[...] = a_ref[...].astype(BF16).T

            part = lax.dot_general(at_ref[...], b_ref[...].astype(BF16), (((1,), (1 if tb else 0,)), ((), ())),
                                   preferred_element_type=F32)
        else:
            part = lax.dot_general(a_ref[...].astype(BF16), b_ref[...].astype(BF16), dims,
                                   preferred_element_type=F32)

        def finish(total):
            if has_bias:
                total = total + bias_ref[...]
            if has_add:
                total = total + add_ref[...]
            if fn is None:
                out_refs[0][...] = total.astype(out_refs[0].dtype)
                return
            res = fn(total, *[r[...] for r in extra_refs])
            for ref, val in zip(out_refs, res[:n_outs]):
                ref[...] = val.astype(ref.dtype)
            if n_sums:
                @pl.when(pl.program_id(1 if k_outer or not rows_outer else 0) == 0)
                def _():
                    for ref in sum_refs:
                        ref[...] = jnp.zeros(ref.shape, ref.dtype)

                for ref, val in zip(sum_refs, res[n_outs:]):
                    ref[...] += val

        if nk == 1:
            finish(part)
        elif k_outer:
            kk = pl.program_id(0)
            rows_i = pl.ds(pl.multiple_of(pl.program_id(1) * tm, tm), tm)

            @pl.when(kk == 0)
            def _():
                acc_ref[rows_i, :] = part

            @pl.when(kk > 0)
            def _():
                acc_ref[rows_i, :] += part

            @pl.when(kk == nk - 1)
            def _():
                finish(acc_ref[rows_i, :])
        else:
            kk = pl.program_id(2)

            @pl.when(kk == 0)
            def _():
                acc_ref[...] = part

            @pl.when(kk > 0)
            def _():
                acc_ref[...] += part

            @pl.when(kk == nk - 1)
            def _():
                finish(acc_ref[...])

    return pl.pallas_call(
        body, name=name, grid=grid, in_specs=in_specs, out_specs=out_specs, out_shape=out_shape,
        scratch_shapes=([pltpu.VMEM((m, n) if k_outer else (tm, tn), F32)] if nk > 1
                        else [pltpu.VMEM((tm, tk), BF16)] if keep_at else []),
        compiler_params=_params(("arbitrary", "arbitrary") if k_outer
                                else ("arbitrary" if n_sums else "parallel", "arbitrary" if keep_at else "parallel",
                                      "arbitrary") if rows_outer
                                else ("parallel", "arbitrary" if n_sums else "parallel", "arbitrary")),
    )(*operands)


def _rowwise(fn, rows, whole, outs, sums=(), *, name, tm=256):
    m = rows[0].shape[-2]
    tm = _pick_rows(m, tm)
    n_rows, n_whole, n_outs, n_sums = len(rows), len(whole), len(outs), len(sums)

    def rspec(shape):
        lead = len(shape) - 2
        return pl.BlockSpec(tuple(shape[:-2]) + (tm, shape[-1]), lambda i: (0,) * lead + (i, 0))

    def wspec(shape):
        return pl.BlockSpec(tuple(shape), lambda i: (0,) * len(shape))

    def body(*refs):
        vals = [r[...] for r in refs[:n_rows + n_whole]]
        out_refs = refs[n_rows + n_whole:n_rows + n_whole + n_outs]
        sum_refs = refs[n_rows + n_whole + n_outs:]
        res = fn(*vals)
        for ref, val in zip(out_refs, res[:n_outs]):
            ref[...] = val.astype(ref.dtype)
        if n_sums:
            @pl.when(pl.program_id(0) == 0)
            def _():
                for ref in sum_refs:
                    ref[...] = jnp.zeros(ref.shape, ref.dtype)

            for ref, val in zip(sum_refs, res[n_outs:]):
                ref[...] += val

    result = pl.pallas_call(
        body, name=name, grid=(m // tm,),
        in_specs=[rspec(r.shape) for r in rows] + [wspec(w.shape) for w in whole],
        out_specs=[rspec(s) for s, _ in outs] + [wspec(s) for s, _ in sums],
        out_shape=[jax.ShapeDtypeStruct(s, d) for s, d in list(outs) + list(sums)],
        compiler_params=_params(("arbitrary",)),
    )(*rows, *whole)
    return result


def _sigmoid(v):
    return jax.nn.sigmoid(v)


def _col_sum(v):
    return jnp.sum(v, axis=0, keepdims=True)


def _ln_stats(z):
    mu = jnp.mean(z, axis=-1, keepdims=True)
    zc = z - mu
    var = jnp.mean(zc * zc, axis=-1, keepdims=True)
    rstd = lax.rsqrt(var + LN_EPS)
    return zc * rstd, rstd


def _ln_fwd_fn(xin, h, gain, bias):
    xhat, _ = _ln_stats(ALPHA * xin + h)
    y = xhat * gain + bias
    return y, y


def _ple_ln_fwd_fn(xin, pg, pu, gain, bias):
    xhat, _ = _ln_stats(ALPHA * xin + _sigmoid(pg) * pu)
    y = xhat * gain + bias
    return y, y


def _ln_dz(dy, z, gain):
    xhat, rstd = _ln_stats(z)
    dxhat = dy * gain
    dz = rstd * (dxhat - jnp.mean(dxhat, axis=-1, keepdims=True)
                 - xhat * jnp.mean(dxhat * xhat, axis=-1, keepdims=True))
    return dz, _col_sum(dy * xhat), _col_sum(dy)


def _ln_bwd_fn(dy, xin, h, gain):
    dz, dgain, dbias = _ln_dz(dy, ALPHA * xin + h, gain)
    return ALPHA * dz, dz, dgain, dbias, _col_sum(dz)


def _ple_ln_bwd_fn(dy, xin, pg, pu, gain):
    sg = _sigmoid(pg)
    dz, dgain, dbias = _ln_dz(dy, ALPHA * xin + sg * pu, gain)
    dpg = dz * pu * sg * (1.0 - sg)
    return ALPHA * dz, dpg, dz * sg, dgain, dbias, _col_sum(dpg)


def _swiglu_fwd_fn(gu):
    hid = gu.shape[-1] // 2
    gate, up = gu[:, :hid], gu[:, hid:]
    return gu, gate * _sigmoid(gate) * up


def _swiglu_bwd_fn(dact, gu):
    gu = gu.astype(F32)
    hid = gu.shape[-1] // 2
    gate, up = gu[:, :hid], gu[:, hid:]
    sg = _sigmoid(gate)
    dgate = dact * up * sg * (1.0 + gate * (1.0 - sg))
    dup = dact * gate * sg
    return (jnp.concatenate([dgate, dup], axis=-1),)


def _loss_fn(y, target):
    err = y - target
    inv = 1.0 / y.shape[-1]
    part = 0.5 * inv * jnp.sum(jnp.sum(err * err, axis=-1, keepdims=True), axis=0, keepdims=True)
    return err * inv, jnp.broadcast_to(part, (1, LANES))


def _adam_fn(w, mom, vel, p_own, p_sib):
    g = p_own.astype(F32) + p_sib.astype(F32)
    m_new = ADAM_B1 * mom + (1.0 - ADAM_B1) * g
    v_new = ADAM_B2 * vel + (1.0 - ADAM_B2) * (g * g)
    m_hat = m_new / (1.0 - ADAM_B1 ** ADAM_STEP)
    v_hat = v_new / (1.0 - ADAM_B2 ** ADAM_STEP)
    delta = -ADAM_LR * (m_hat / (jnp.sqrt(v_hat) + ADAM_EPS) + ADAM_WD * w)
    return g, delta, m_new, v_new


def _split2(x):
    hi = x.astype(BF16)
    return hi, (x - hi.astype(F32)).astype(BF16)


def _dot3(a, b, dims):
    a_hi, a_lo = _split2(a)
    b_hi, b_lo = _split2(b)
    dn = (dims, ((), ()))
    return (lax.dot_general(a_hi, b_hi, dn, preferred_element_type=F32)
            + (lax.dot_general(a_hi, b_lo, dn, preferred_element_type=F32)
               + lax.dot_general(a_lo, b_hi, dn, preferred_element_type=F32)))


def _tdot(mask01, b):
    m = mask01.astype(BF16)
    b_hi = b.astype(BF16)
    rest = b - b_hi.astype(F32)
    b_mid = rest.astype(BF16)
    b_lo = (rest - b_mid.astype(F32)).astype(BF16)
    dn = (((1,), (0,)), ((), ()))
    return (lax.dot_general(m, b_hi, dn, preferred_element_type=F32)
            + (lax.dot_general(m, b_mid, dn, preferred_element_type=F32)
               + lax.dot_general(m, b_lo, dn, preferred_element_type=F32)))


def _hdot(a, b):
    return _dot3(a, b, ((1,), (0,)))


def _hdot_nt(a, b):
    return _dot3(a, b, ((1,), (1,)))


def _hdot_tn(a, b):
    return _dot3(a, b, ((0,), (0,)))


def _dot(a, b):
    return lax.dot_general(a.astype(BF16), b.astype(BF16), (((1,), (0,)), ((), ())), preferred_element_type=F32)


def _dot_nt(a, b):
    return lax.dot_general(a.astype(BF16), b.astype(BF16), (((1,), (1,)), ((), ())), preferred_element_type=F32)


def _dot_tn(a, b):
    return lax.dot_general(a.astype(BF16), b.astype(BF16), (((0,), (0,)), ((), ())), preferred_element_type=F32)


def _hg_masks():
    c = HG_CHUNK
    row = lax.broadcasted_iota(jnp.int32, (c, c), 0)
    col = lax.broadcasted_iota(jnp.int32, (c, c), 1)
    base = row & (-HG_SUB)
    return row, col, base, col <= row, col < base


def _hg_gates(qr, fr, alb):
    lbound = _sigmoid(alb[0:1, :] - alb[1:2, :])
    sig = _sigmoid(fr)
    forget = lbound + (1.0 - lbound) * sig
    kk = (1.0 - lbound) * _sigmoid(-fr)
    qt = qr * _sigmoid(qr) * (HG_DK ** -0.5)
    return qt, kk, jnp.log(forget), lbound, sig, forget


def _hg_scores(qt, kk, g, scores=True):
    c, nsub = HG_CHUNK, HG_CHUNK // HG_SUB
    row, col, base, causal, below = _hg_masks()
    b = _tdot(causal, g)
    rr = _tdot(below, g)
    bq = b - rr
    qh = qt * jnp.exp(bq)
    edecs = [None]
    parts = [jnp.zeros((HG_SUB, c), F32)]
    for i in range(1, nsub):
        edec = jnp.exp(jnp.minimum(rr[i * HG_SUB:i * HG_SUB + 1, :] - b, 0.0))
        edecs.append(edec)
        if scores:
            parts.append(_dot_nt(qh[i * HG_SUB:(i + 1) * HG_SUB, :], kk * edec))
    b3 = b.reshape(nsub, HG_SUB, HG_DK)
    q3 = qt.reshape(nsub, HG_SUB, HG_DK)
    k3 = kk.reshape(nsub, HG_SUB, HG_DK)
    if not scores:
        return None, b, bq, qh, edecs, (b3, q3, k3)
    a = jnp.where(below, jnp.concatenate(parts, axis=0), 0.0)
    for j in range(HG_SUB):
        e = jnp.exp(b3 - b3[:, j:j + 1, :])
        colv = jnp.sum(q3 * e * k3[:, j:j + 1, :], axis=-1, keepdims=True).reshape(c, 1)
        a = jnp.where(col == base + j, colv, a)
    a = jnp.where(causal, a, 0.0)
    return a, b, bq, qh, edecs, (b3, q3, k3)


def _hg_norm(o, gr, gain):
    r = lax.rsqrt(jnp.mean(o * o, axis=-1, keepdims=True) + RMS_EPS)
    sg = _sigmoid(gr)
    return o * r * gain, r, sg


def _hgrn2_fwd(proj, alb, gain, *, rb):
    m, d4 = proj.shape
    d = d4 // 4
    heads = d // HG_DK
    hp = HG_HEADS_PER_STEP
    rb = min(rb, m)
    cpb = rb // HG_CHUNK
    nrb = m // rb

    def body(q_ref, f_ref, v_ref, g_ref, alb_ref, gain_ref, o_ref, og_ref, st_ref, a_ref, state):
        @pl.when(pl.program_id(1) == 0)
        def _():
            state[...] = jnp.zeros(state.shape, F32)

        def chunk(ci, carry):
            sl = pl.ds(pl.multiple_of(ci * HG_CHUNK, HG_CHUNK), HG_CHUNK)
            for u in range(hp):
                ln = slice(u * HG_DK, (u + 1) * HG_DK)
                qt, kk, g, _, _, _ = _hg_gates(q_ref[sl, ln], f_ref[sl, ln], alb_ref[:, ln])
                v = v_ref[sl, ln]
                st = state[u]
                st_ref[u, ci] = st
                a, b, _, _, _, _ = _hg_scores(qt, kk, g)
                a_ref[u, ci] = a.astype(a_ref.dtype)
                o = _dot(a, v) + _dot_nt(qt * jnp.exp(b), st)
                b_last = b[HG_CHUNK - 1:HG_CHUNK, :]
                state[u] = st * jnp.exp(b_last) + _hdot_tn(v, kk * jnp.exp(b_last - b))
                o_ref[sl, ln] = o
                n, _, sg = _hg_norm(o, g_ref[sl, ln], gain_ref[...])
                og_ref[sl, ln] = (n * g_ref[sl, ln] * sg).astype(og_ref.dtype)
            return carry

        lax.fori_loop(0, cpb, chunk, 0)

    def col(cidx):
        return pl.BlockSpec((rb, hp * HG_DK), lambda h, r: (r, cidx * (heads // hp) + h))

    return pl.pallas_call(
        body, name="hgrn2_fwd", grid=(heads // hp, nrb),
        in_specs=[col(0), col(1), col(2), col(3),
                  pl.BlockSpec((2, hp * HG_DK), lambda h, r: (0, h)),
                  pl.BlockSpec((1, HG_DK), lambda h, r: (0, 0))],
        out_specs=[pl.BlockSpec((rb, hp * HG_DK), lambda h, r: (r, h)),
                   pl.BlockSpec((rb, hp * HG_DK), lambda h, r: (r, h)),
                   pl.BlockSpec((hp, cpb, HG_DK, HG_DK), lambda h, r: (h, r, 0, 0)),
                   pl.BlockSpec((hp, cpb, HG_CHUNK, HG_CHUNK), lambda h, r: (h, r, 0, 0))],
        out_shape=[jax.ShapeDtypeStruct((m, d), F32), jax.ShapeDtypeStruct((m, d), BF16),
                   jax.ShapeDtypeStruct((heads, m // HG_CHUNK, HG_DK, HG_DK), F32),
                   jax.ShapeDtypeStruct((heads, m // HG_CHUNK, HG_CHUNK, HG_CHUNK), BF16)],
        scratch_shapes=[pltpu.VMEM((hp, HG_DK, HG_DK), F32)],
        compiler_params=_params(("parallel", "arbitrary")),
    )(proj, proj, proj, proj, alb, gain)


def _hgrn2_bwd(proj, o_pre, states, scores, dog, alb, gain, *, rb):
    m, d4 = proj.shape
    d = d4 // 4
    heads = d // HG_DK
    rb = min(rb, m)
    cpb = rb // HG_CHUNK
    nrb = m // rb
    c, nsub = HG_CHUNK, HG_CHUNK // HG_SUB

    def body(q_ref, f_ref, v_ref, g_ref, o_ref, st_ref, a_ref, dog_ref, alb_ref, gain_ref,
             dq_ref, df_ref, dv_ref, dg_ref, dalb_ref, dgain_ref, dstate, carry_ref):
        first = (pl.program_id(0) == 0) & (pl.program_id(1) == 0)

        @pl.when(first)
        def _():
            dgain_ref[...] = jnp.zeros(dgain_ref.shape, F32)

        @pl.when(pl.program_id(1) == 0)
        def _():
            dstate[...] = jnp.zeros(dstate.shape, F32)
            carry_ref[...] = jnp.zeros(carry_ref.shape, F32)
            dalb_ref[...] = jnp.zeros(dalb_ref.shape, F32)

        row, col, base, causal, below = _hg_masks()
        sub_iota = lax.broadcasted_iota(jnp.int32, (nsub, HG_SUB, HG_DK), 1)
        row_k = lax.broadcasted_iota(jnp.int32, (c, HG_DK), 0)
        upper = col >= row

        def chunk(step, carry):
            ci = cpb - 1 - step
            sl = pl.ds(pl.multiple_of(ci * HG_CHUNK, HG_CHUNK), HG_CHUNK)
            qr, fr, v, gr = q_ref[sl, :], f_ref[sl, :], v_ref[sl, :], g_ref[sl, :]
            qt, kk, g, lbound, sig, forget = _hg_gates(qr, fr, alb_ref[...])
            o = o_ref[sl, :]
            dogv = dog_ref[sl, :]
            gain_v = gain_ref[...]
            n, r, sg = _hg_norm(o, gr, gain_v)
            dgr = dogv * n * sg * (1.0 + gr * (1.0 - sg))
            dn = dogv * gr * sg
            dgain_ref[...] += _col_sum(dn * o * r)
            u = dn * gain_v
            d_o = r * u - o * (r * r * r) * jnp.mean(u * o, axis=-1, keepdims=True)
            st0 = st_ref[ci]
            dst = dstate[...]
            _, b, bq, qh, edecs, (b3, q3, k3) = _hg_scores(qt, kk, g, scores=False)
            a = a_ref[ci]
            eb = jnp.exp(b)
            b_last = b[c - 1:c, :]
            kdl_dec = jnp.exp(b_last - b)
            kdl = kk * kdl_dec
            d_a = jnp.where(causal, _dot_nt(d_o, v), 0.0)
            d_at = _dot_nt(v, d_o)
            dv = _dot_tn(a, d_o) + _dot_nt(kdl, dst)
            dq = eb * _hdot(d_o, st0)
            dk = _hdot(v, dst) * kdl_dec
            d_a_below = jnp.where(below, d_a, 0.0)
            dq_parts = [jnp.zeros((HG_SUB, HG_DK), F32)]
            for i in range(1, nsub):
                lo, hi = i * HG_SUB, (i + 1) * HG_SUB
                dq_parts.append(_hdot(d_a_below[lo:hi, :], kk * edecs[i]))
                gi = _hdot(d_at[:, lo:hi], qh[lo:hi, :])
                dk = dk + jnp.where(row_k < lo, edecs[i] * gi, 0.0)
            dq = dq + jnp.concatenate(dq_parts, axis=0) * jnp.exp(bq)
            dq3 = jnp.zeros((nsub, HG_SUB, HG_DK), F32)
            dk3 = jnp.zeros((nsub, HG_SUB, HG_DK), F32)
            d_diag = jnp.concatenate([d_a[i * HG_SUB:(i + 1) * HG_SUB, i * HG_SUB:(i + 1) * HG_SUB]
                                      for i in range(nsub)], axis=0).reshape(nsub, HG_SUB, HG_SUB)
            for j in range(HG_SUB):
                e = jnp.exp(jnp.minimum(b3 - b3[:, j:j + 1, :], 0.0))
                t1 = d_diag[:, :, j:j + 1] * e
                dq3 = dq3 + t1 * k3[:, j:j + 1, :]
                dk3 = jnp.where(sub_iota == j, jnp.sum(t1 * q3, axis=1, keepdims=True), dk3)
            dq = dq + dq3.reshape(c, HG_DK)
            dk = dk + dk3.reshape(c, HG_DK)
            dstate[...] = dst * jnp.exp(b_last) + _hdot_tn(d_o, qt * eb)
            dglog = _tdot(upper, qt * dq - kk * dk) + carry_ref[...]
            carry_ref[...] = dglog[0:1, :]
            dforget = dglog / forget
            one_m_lb = 1.0 - lbound
            dsig = (dforget - dk) * one_m_lb
            sneg = _sigmoid(-fr)
            dlb = _col_sum(dforget * (1.0 - sig) - dk * sneg)
            dalb0 = dlb * lbound * one_m_lb
            dalb_ref[...] += jnp.concatenate([dalb0, -dalb0], axis=0)
            sq = _sigmoid(qr)
            dq_ref[sl, :] = (dq * (HG_DK ** -0.5) * sq * (1.0 + qr * (1.0 - sq))).astype(dq_ref.dtype)
            df_ref[sl, :] = (dsig * sig * (1.0 - sig)).astype(df_ref.dtype)
            dv_ref[sl, :] = dv.astype(dv_ref.dtype)
            dg_ref[sl, :] = dgr.astype(dg_ref.dtype)
            return carry

        lax.fori_loop(0, cpb, chunk, 0, unroll=2)

    def rev(r):
        return nrb - 1 - r

    def col(cidx):
        return pl.BlockSpec((rb, HG_DK), lambda h, r: (rev(r), cidx * heads + h))

    def head_rows():
        return pl.BlockSpec((rb, HG_DK), lambda h, r: (rev(r), h))

    return pl.pallas_call(
        body, name="hgrn2_bwd", grid=(heads, nrb),
        in_specs=[col(0), col(1), col(2), col(3), head_rows(),
                  pl.BlockSpec((None, cpb, HG_DK, HG_DK), lambda h, r: (h, rev(r), 0, 0)),
                  pl.BlockSpec((None, cpb, HG_CHUNK, HG_CHUNK), lambda h, r: (h, rev(r), 0, 0)),
                  head_rows(),
                  pl.BlockSpec((2, HG_DK), lambda h, r: (0, h)),
                  pl.BlockSpec((1, HG_DK), lambda h, r: (0, 0))],
        out_specs=[head_rows(), head_rows(), head_rows(), head_rows(),
                   pl.BlockSpec((2, HG_DK), lambda h, r: (0, h)),
                   pl.BlockSpec((1, HG_DK), lambda h, r: (0, 0))],
        out_shape=[jax.ShapeDtypeStruct((m, d), BF16)] * 4
                  + [jax.ShapeDtypeStruct((2, d), F32), jax.ShapeDtypeStruct((1, HG_DK), F32)],
        scratch_shapes=[pltpu.VMEM((HG_DK, HG_DK), F32), pltpu.VMEM((1, HG_DK), F32)],
        compiler_params=_params(("arbitrary", "arbitrary")),
    )(proj, proj, proj, proj, o_pre, states, scores, dog, alb, gain)


def _swa_probs(qh, kp, kc, sink, slope, has_prev, lse=None):
    rows = qh.shape[0]
    qi = lax.broadcasted_iota(jnp.int32, (rows, WINDOW), 0) & (WINDOW - 1)
    si = lax.broadcasted_iota(jnp.int32, (rows, WINDOW), 1)
    scale = ATT_HD ** -0.5
    dist_c = (qi - si).astype(F32)
    s_p = _dot_nt(qh, kp) * scale - slope * (dist_c + float(WINDOW))
    s_c = _dot_nt(qh, kc) * scale - slope * dist_c
    s_p = jnp.where((si > qi) & has_prev, s_p, NEG)
    s_c = jnp.where(si <= qi, s_c, NEG)
    if lse is not None:
        return jnp.exp(s_p - lse), jnp.exp(s_c - lse), jnp.exp(sink - lse), lse
    mx = jnp.maximum(jnp.maximum(jnp.max(s_p, axis=-1, keepdims=True), jnp.max(s_c, axis=-1, keepdims=True)), sink)
    e_p, e_c, e_s = jnp.exp(s_p - mx), jnp.exp(s_c - mx), jnp.exp(sink - mx)
    total = jnp.sum(e_p, axis=-1, keepdims=True) + jnp.sum(e_c, axis=-1, keepdims=True) + e_s
    inv = 1.0 / total
    return e_p * inv, e_c * inv, e_s * inv, mx + jnp.log(total)


def _slope(h, n_heads):
    return float(2.0 ** (-8.0 * (h + 1) / n_heads))


def _swa_group(ref_vals, sink_ref, kh, n_heads):
    heads = [kh * ATT_G + g for g in range(ATT_G)]
    stacked = [jnp.concatenate([v[:, h * ATT_HD:(h + 1) * ATT_HD] for h in heads], axis=0) for v in ref_vals]
    grp = lax.shift_right_logical(lax.broadcasted_iota(jnp.int32, (ATT_G * WINDOW, 1), 0), WINDOW.bit_length() - 1)
    slope = jnp.zeros((ATT_G * WINDOW, 1), F32)
    sink = jnp.zeros((ATT_G * WINDOW, 1), F32)
    for g, h in enumerate(heads):
        slope = jnp.where(grp == g, _slope(h, n_heads), slope)
        sink = jnp.where(grp == g, sink_ref[:, h:h + 1], sink)
    return stacked, slope, sink


def _swa_fwd(q, kv, sinks):
    m, d = q.shape
    n_heads = d // ATT_HD
    kvh = n_heads // ATT_G
    kd = kvh * ATT_HD
    nb = m // WINDOW

    def body(q_ref, kvp_ref, kvc_ref, sink_ref, o_ref, lse_ref):
        has_prev = pl.program_id(0) > 0
        qv, kvp, kvc = q_ref[...], kvp_ref[...], kvc_ref[...]
        lane_h = lax.broadcasted_iota(jnp.int32, (WINDOW, n_heads), 1)
        outs, lse_all = [], jnp.zeros((WINDOW, n_heads), F32)
        for kh in range(kvh):
            ks = slice(kh * ATT_HD, (kh + 1) * ATT_HD)
            vs = slice(kd + kh * ATT_HD, kd + (kh + 1) * ATT_HD)
            (q4,), slope, sink = _swa_group([qv], sink_ref, kh, n_heads)
            p_p, p_c, _, lse = _swa_probs(q4, kvp[:, ks], kvc[:, ks], sink, slope, has_prev)
            o4 = _dot(p_p, kvp[:, vs]) + _dot(p_c, kvc[:, vs])
            for g in range(ATT_G):
                rows = slice(g * WINDOW, (g + 1) * WINDOW)
                outs.append(o4[rows, :])
                lse_all = jnp.where(lane_h == kh * ATT_G + g, lse[rows, :], lse_all)
        o_ref[...] = jnp.concatenate(outs, axis=-1).astype(o_ref.dtype)
        lse_ref[...] = lse_all

    return pl.pallas_call(
        body, name="swa_fwd", grid=(nb,),
        in_specs=[pl.BlockSpec((WINDOW, d), lambda n: (n, 0)),
                  pl.BlockSpec((WINDOW, 2 * kd), lambda n: (jnp.maximum(n - 1, 0), 0)),
                  pl.BlockSpec((WINDOW, 2 * kd), lambda n: (n, 0)),
                  pl.BlockSpec((1, n_heads), lambda n: (0, 0))],
        out_specs=[pl.BlockSpec((WINDOW, d), lambda n: (n, 0)), pl.BlockSpec((WINDOW, n_heads), lambda n: (n, 0))],
        out_shape=[jax.ShapeDtypeStruct((m, d), BF16), jax.ShapeDtypeStruct((m, n_heads), F32)],
        compiler_params=_params(("arbitrary",)),
    )(q, kv, kv, sinks)


def _swa_bwd(q, kv, sinks, lse, dao):
    m, d = q.shape
    n_heads = d // ATT_HD
    kvh = n_heads // ATT_G
    kd = kvh * ATT_HD
    nb = m // WINDOW
    scale = ATT_HD ** -0.5

    def body(q_ref, kvp_ref, kvc_ref, sink_ref, lse_ref, do_ref, dq_ref, dkvc_ref, dkvp_ref, dqsum_ref, dsink_ref):
        @pl.when(pl.program_id(0) == 0)
        def _():
            dqsum_ref[...] = jnp.zeros(dqsum_ref.shape, F32)
            dsink_ref[...] = jnp.zeros(dsink_ref.shape, F32)

        has_prev = pl.program_id(0) > 0
        qv, kvp, kvc, dov = q_ref[...], kvp_ref[...], kvc_ref[...], do_ref[...]
        lane_h = lax.broadcasted_iota(jnp.int32, (1, n_heads), 1)
        dsink = jnp.zeros((1, n_heads), F32)
        dq_parts, dk_p, dk_c, dv_p, dv_c = [], [], [], [], []
        for kh in range(kvh):
            ks = slice(kh * ATT_HD, (kh + 1) * ATT_HD)
            vs = slice(kd + kh * ATT_HD, kd + (kh + 1) * ATT_HD)
            kp, kc, vp, vc = kvp[:, ks], kvc[:, ks], kvp[:, vs], kvc[:, vs]
            (q4, do4), slope, sink = _swa_group([qv, dov], sink_ref, kh, n_heads)
            lse4 = jnp.concatenate([lse_ref[:, kh * ATT_G + g:kh * ATT_G + g + 1] for g in range(ATT_G)], axis=0)
            p_p, p_c, p_s, _ = _swa_probs(q4, kp, kc, sink, slope, has_prev, lse=lse4)
            dp_p, dp_c = _dot_nt(do4, vp), _dot_nt(do4, vc)
            delta = jnp.sum(p_p * dp_p, axis=-1, keepdims=True) + jnp.sum(p_c * dp_c, axis=-1, keepdims=True)
            ds_p, ds_c = p_p * (dp_p - delta), p_c * (dp_c - delta)
            sink_term = p_s * delta
            dq4 = (_dot(ds_p, kp) + _dot(ds_c, kc)) * scale
            for g in range(ATT_G):
                rows = slice(g * WINDOW, (g + 1) * WINDOW)
                dsink = dsink + jnp.where(lane_h == kh * ATT_G + g, -_col_sum(sink_term[rows, :]), 0.0)
                dq_parts.append(dq4[rows, :])
            dk_p.append(_dot_tn(ds_p, q4) * scale)
            dk_c.append(_dot_tn(ds_c, q4) * scale)
            dv_p.append(_dot_tn(p_p, do4))
            dv_c.append(_dot_tn(p_c, do4))
        dq = jnp.concatenate(dq_parts, axis=-1)
        dq_ref[...] = dq.astype(dq_ref.dtype)
        dqsum_ref[...] += _col_sum(dq)
        dsink_ref[...] += dsink
        dkvc_ref[...] = jnp.concatenate(dk_c + dv_c, axis=-1)
        dkvp_ref[...] = jnp.concatenate(dk_p + dv_p, axis=-1)

    return pl.pallas_call(
        body, name="swa_bwd", grid=(nb,),
        in_specs=[pl.BlockSpec((WINDOW, d), lambda n: (n, 0)),
                  pl.BlockSpec((WINDOW, 2 * kd), lambda n: (jnp.maximum(n - 1, 0), 0)),
                  pl.BlockSpec((WINDOW, 2 * kd), lambda n: (n, 0)),
                  pl.BlockSpec((1, n_heads), lambda n: (0, 0)),
                  pl.BlockSpec((WINDOW, n_heads), lambda n: (n, 0)),
                  pl.BlockSpec((WINDOW, d), lambda n: (n, 0))],
        out_specs=[pl.BlockSpec((WINDOW, d), lambda n: (n, 0)),
                   pl.BlockSpec((WINDOW, 2 * kd), lambda n: (n, 0)),
                   pl.BlockSpec((WINDOW, 2 * kd), lambda n: (n, 0)),
                   pl.BlockSpec((1, d), lambda n: (0, 0)),
                   pl.BlockSpec((1, n_heads), lambda n: (0, 0))],
        out_shape=[jax.ShapeDtypeStruct((m, d), BF16), jax.ShapeDtypeStruct((m, 2 * kd), F32),
                   jax.ShapeDtypeStruct((m, 2 * kd), F32), jax.ShapeDtypeStruct((1, d), F32),
                   jax.ShapeDtypeStruct((1, n_heads), F32)],
        compiler_params=_params(("arbitrary",)),
    )(q, kv, kv, sinks, lse, dao)


def _kv_grad_combine(dkv_cur, dkv_prev):
    m, w = dkv_cur.shape
    nb = m // WINDOW

    def body(cur_ref, nxt_ref, o_ref, sum_ref):
        @pl.when(pl.program_id(0) == 0)
        def _():
            sum_ref[...] = jnp.zeros(sum_ref.shape, F32)

        total = cur_ref[...] + jnp.where(pl.program_id(0) < nb - 1, nxt_ref[...], 0.0)
        o_ref[...] = total.astype(o_ref.dtype)
        sum_ref[...] += _col_sum(total)

    return pl.pallas_call(
        body, name="kv_grad_combine", grid=(nb,),
        in_specs=[pl.BlockSpec((WINDOW, w), lambda n: (n, 0)),
                  pl.BlockSpec((WINDOW, w), lambda n: (jnp.minimum(n + 1, nb - 1), 0))],
        out_specs=[pl.BlockSpec((WINDOW, w), lambda n: (n, 0)), pl.BlockSpec((1, w), lambda n: (0, 0))],
        out_shape=[jax.ShapeDtypeStruct((m, w), BF16), jax.ShapeDtypeStruct((1, w), F32)],
        compiler_params=_params(("arbitrary",)),
    )(dkv_cur, dkv_prev)


def _row(v):
    return v.reshape(1, -1)


def _local_step(x, p, target, wget, grad_sink, ln_gain, ln_bias, alb, norm_gain, kv_b, b_q, sinks, b_out, ple_b,
                small_sink=None):
    gs = {}
    gains = ln_gain.reshape(DEPTH * 3, -1)
    biases = ln_bias.reshape(DEPTH * 3, -1)
    sd = x.shape
    pending = [None]

    def mm(a, b, lb=0, **kw):
        after, pending[0] = pending[0], None
        return _mm(a, b, lb=lb, after=after, **kw)

    def mm_ln(a, wt, xin, i, j, nm, bias=None, pu=None):
        r = 3 * i + j
        if pu is None:
            fn, rows = (lambda h, xv, g, bv: (h,) + _ln_fwd_fn(xv, h, g[r:r + 1], bv[r:r + 1])), [xin]
        else:
            fn = lambda h, xv, puv, g, bv: (h,) + _ple_ln_fwd_fn(xv, h, puv, g[r:r + 1], bv[r:r + 1])
            rows = [xin, pu]
        h, y, yb = _mm(a, wt, lb=0, bias=bias, name=nm,
                       post=(fn, rows, [gains, biases], [(sd, F32), (sd, F32), (sd, BF16)], []))
        return h, (y, yb)

    def mm_ln_bwd(a, wt, add, xin, h, i, j, nm):
        r = 3 * i + j
        dx_part, dh, dg, db, dhsum = mm(a, wt, tb=True, add=add, name=nm,
                                        post=(lambda dy, xv, hv, g: _ln_bwd_fn(dy, xv, hv, g[r:r + 1]), [xin, h],
                                              [gains], [(sd, F32), (sd, BF16)], [((1, sd[1]), F32)] * 3))
        gs[f"ln_gain_{i}_{j}"], gs[f"ln_bias_{i}_{j}"] = dg, db
        return dx_part, dh, dhsum

    def tail_fwd(xa, i):
        wgu = wget("ffn_w_gate_up", i, xa[1])
        hid2 = wgu.shape[-1]
        gu, act = _mm(xa[1], wgu, lb=0, name=f"ffn_up_swiglu{i}", tile_cols=hid2 // 2,
                      post=(_swiglu_fwd_fn, [], [], [((sd[0], hid2), BF16), ((sd[0], hid2 // 2), BF16)], []))
        f, xb = mm_ln(act, wget("ffn_w_down", i, act), xa[0], i, 1, f"ffn_down_ln{i}")
        pu = _mm(p, wget("ple_w_up", i, act), la=i, lb=0, name=f"ple_up{i}")
        pg, xc = mm_ln(xb[1], wget("ple_w_gate", i, act), xb[0], i, 2, f"ple_gate_ln{i}", bias=_row(ple_b[i]), pu=pu)
        return dict(xa=xa, gu=gu, act=act, f=f, xb=xb, pg=pg, pu=pu), xc

    def tail_bwd(head, sv, i, mix_in, mix_h):
        xa, xb = sv["xa"], sv["xb"]
        r = 3 * i + 2
        dxb_part, dpg, dpu, dg2, db2, dbg = head(
            lambda dy, xv, pgv, puv, g: _ple_ln_bwd_fn(dy, xv, pgv, puv, g[r:r + 1]), [xb[0], sv["pg"], sv["pu"]],
            [gains], [(sd, F32), (sd, BF16), (sd, BF16)], [((1, sd[1]), F32)] * 3)[:6]
        gs[f"ple_b_{i}"] = dbg
        gs[f"ln_gain_{i}_2"], gs[f"ln_bias_{i}_2"] = dg2, db2
        grad_of("ple_w_gate", i, xb[1], dpg)
        grad_of("ple_w_up", i, p, dpu, la=i)
        dxa_part, df, _ = mm_ln_bwd(dpg, wget("ple_w_gate", i, None), dxb_part, xa[0], sv["f"], i, 1,
                                    f"ple_gate_dx_ln{i}")
        grad_of("ffn_w_down", i, sv["act"], df)
        gu = sv["gu"]
        dgu, = mm(df, wget("ffn_w_down", i, None), tb=True, name=f"ffn_down_dx_swiglu{i}", tile_cols=gu.shape[1] // 4,
                  post=(_swiglu_bwd_fn, [gu], [], [(gu.shape, BF16)], []))
        grad_of("ffn_w_gate_up", i, xa[1], dgu)
        return mm_ln_bwd(dgu, wget("ffn_w_gate_up", i, None), dxa_part, mix_in, mix_h, i, 0, f"ffn_up_dx_ln{i}")

    def grad_of(nm, i, act, dout, la=None):
        grad = mm(act, dout, la=la, lb=None, ta=True, out_dtype=BF16, out_layers=1, out_layer=0,
                  name=f"grad_{nm}{i}")
        token = grad_sink(nm, i, grad)
        if token is not None:
            pending[0] = token

    proj = _mm(x, wget("a_w_in", 0, None), lb=0, name="hg_proj")
    o_pre, og, states, scores = _hgrn2_fwd(proj, alb, norm_gain, rb=HG_ROWS)
    h0, x1 = mm_ln(og, wget("a_w_out", 0, og), x, 0, 0, "hg_out_ln")
    sv0, x3 = tail_fwd(x1, 0)
    kv = _mm(x3[1], wget("kv_w", 0, x3[1]), lb=0, bias=_row(kv_b), out_dtype=BF16, name="kv_proj")
    q = _mm(x3[1], wget("b_w_q", 0, x3[1]), lb=0, bias=b_q, out_dtype=BF16, name="q_proj")
    ao, lse = _swa_fwd(q, kv, sinks)
    h1, x4 = mm_ln(ao, wget("b_w_out", 0, x3[1]), x3[0], 1, 0, "att_out_ln", bias=b_out)
    sv1, y = tail_fwd(x4, 1)

    loss_box = []

    def loss_head(fn, rows, whole, outs, sums):
        def with_loss(yv, tv, *rest):
            dy, part = _loss_fn(yv, tv)
            return fn(dy, *rest) + (part,)

        res = _rowwise(with_loss, [y[0], target] + rows, whole, outs, list(sums) + [((1, LANES), F32)],
                       name="loss_ln_ple_bwd1")
        loss_box.append(res[-1])
        return res

    dx3_part, dh1, dh1sum = tail_bwd(loss_head, sv1, 1, x3[0], h1)
    loss = loss_box[0]
    gs["b_out"] = dh1sum
    grad_of("b_w_out", 0, ao, dh1)
    dao = mm(dh1, wget("b_w_out", 0, None), tb=True, out_dtype=BF16, name="att_out_dx")
    dq, dkv_cur, dkv_prev, dqsum, dsinks = _swa_bwd(q, kv, sinks, lse, dao)
    gs["b_q"], gs["sinks"] = dqsum, dsinks
    dkv, dkvsum = _kv_grad_combine(dkv_cur, dkv_prev)
    gs["kv_b"] = dkvsum
    grad_of("b_w_q", 0, x3[1], dq)
    grad_of("kv_w", 0, x3[1], dkv)
    dx3 = mm(dq, wget("b_w_q", 0, None), tb=True, add=dx3_part, name="q_proj_dx")

    def kv_head(*post):
        return mm(dkv, wget("kv_w", 0, None), tb=True, add=dx3, name="kv_proj_dx_ln_ple_bwd0", post=post)

    dx_part, dh0, _ = tail_bwd(kv_head, sv0, 0, x, h0)
    grad_of("a_w_out", 0, og, dh0)
    dog = mm(dh0, wget("a_w_out", 0, None), tb=True, name="hg_out_dx")
    dqr, dfr, dvr, dgr, dalb, dgain = _hgrn2_bwd(proj, o_pre, states, scores, dog, alb, norm_gain, rb=HG_ROWS)
    gs["alb"], gs["norm_gain"] = dalb, dgain
    if small_sink is not None:
        pending[0] = small_sink(loss, gs)
    dproj = jnp.concatenate([dqr, dfr, dvr, dgr], axis=1)
    grad_of("a_w_in", 0, x, dproj)
    grad_x = mm(dproj, wget("a_w_in", 0, None), tb=True, add=dx_part, name="hg_proj_dx")
    return loss, grad_x, gs


HBM_SPEC = pl.BlockSpec(memory_space=pl.ANY)
HBM_ONLY = pl.BlockSpec(memory_space=pltpu.HBM)
SEM_SPEC = pl.BlockSpec(memory_space=pltpu.SEMAPHORE)
SIDE_EFFECT = pltpu.SideEffectType.DATAFLOW_SIDE_EFFECTING


def _slot(kind, j):
    return (j % 2) * 2 + j // 2 if kind == "colp" else j


def _piece(ref, kind, j):
    _, r, c = ref.shape
    if kind == "row":
        return ref.at[:, pl.ds(j * (r // N_CHIPS), r // N_CHIPS), :]
    return ref.at[:, :, pl.ds(_slot(kind, j) * (c // N_CHIPS), c // N_CHIPS)]


def _piece_dyn(ref, kind, j):
    _, r, c = ref.shape
    if kind == "row":
        return ref.at[:, pl.ds(pl.multiple_of(j * (r // N_CHIPS), 16), r // N_CHIPS), :]
    return ref.at[:, :, pl.ds(pl.multiple_of(_slot(kind, j) * (c // N_CHIPS), LANES), c // N_CHIPS)]


def _chip_of(j, c):
    return (j // 2, j % 2, c)


def _in_hbm(a):
    return pltpu.with_memory_space_constraint(a, pltpu.HBM)


def _place(src, layer, kind, chip, *, name, out_dtype, after=None):
    _, r, c = src.shape
    out_shape = (1, r * N_CHIPS, c) if kind == "row" else (1, r, c * N_CHIPS)
    tm = _pick_rows(r, 512)
    nb = r // tm

    def full_idx(i, chip_ref):
        return (0, chip_ref[0] * nb + i, 0) if kind == "row" else (0, i, _slot(kind, chip_ref[0]))

    in_specs, operands = [pl.BlockSpec((None, tm, c), lambda i, chip_ref: (layer, i, 0))], [src]
    if after is not None:
        in_specs.append(HBM_SPEC)
        operands.append(after)

    def body(chip_ref, src_ref, *rest):
        rest[-1][...] = src_ref[...].astype(rest[-1].dtype)

    return pl.pallas_call(
        body, name=name,
        grid_spec=pltpu.PrefetchScalarGridSpec(num_scalar_prefetch=1, grid=(nb,), in_specs=in_specs,
                                               out_specs=pl.BlockSpec((None, tm, c), full_idx)),
        out_shape=jax.ShapeDtypeStruct(out_shape, out_dtype),
        compiler_params=_params(("arbitrary",)),
    )(chip, *operands)


def _half(ref, c):
    h = ref.shape[1] // 2
    start = c * h if isinstance(c, int) else pl.multiple_of(c * h, 16)
    return ref.at[:, pl.ds(start, h), :]


class _SiblingFill:
    def __init__(self, lands, kinds, name):
        self.kinds, self.name, self.n = kinds, name, len(lands)
        n = self.n
        sem_shape = pltpu.SemaphoreType.DMA((n * N_CHIPS,))

        def body(*refs):
            land_refs, send_sems, recv_sems, token = refs[:n], refs[n], refs[n + 1], refs[-1]
            for cp in self._copies(land_refs, send_sems, recv_sems):
                cp.start()
            token[...] = jnp.zeros(token.shape, token.dtype)

        outs = pl.pallas_call(
            body, name=name + "_start",
            in_specs=[HBM_ONLY] * n,
            out_specs=[SEM_SPEC, SEM_SPEC] + [HBM_ONLY] * n + [pl.BlockSpec(memory_space=pltpu.VMEM)],
            out_shape=[sem_shape, sem_shape] + [pltpu.HBM(a.shape, a.dtype) for a in lands]
                      + [jax.ShapeDtypeStruct((8, LANES), F32)],
            input_output_aliases={i: i + 2 for i in range(n)},
            compiler_params=pltpu.CompilerParams(has_side_effects=SIDE_EFFECT),
        )(*[_in_hbm(a) for a in lands])
        self.send_sems, self.recv_sems, self.lands, self.token = outs[0], outs[1], list(outs[2:2 + n]), outs[-1]

    def _copies(self, land_refs, send_sems, recv_sems):
        x, y, c = lax.axis_index("x"), lax.axis_index("y"), lax.axis_index("c")
        me = 2 * x + y
        copies = []
        for a in range(self.n):
            for k in range(1, N_CHIPS):
                t = (me + k) % N_CHIPS
                slice_t = _piece_dyn(land_refs[a], self.kinds[a], t)
                got = _half(slice_t, c)
                copies.append(pltpu.make_async_remote_copy(
                    src_ref=got, dst_ref=got, send_sem=send_sems.at[a * N_CHIPS + k],
                    recv_sem=recv_sems.at[a * N_CHIPS + k], device_id=(x, y, 1 - c), device_id_type=MESH))
        return copies

    def wait(self, after):
        n = self.n

        def body(*refs):
            land_refs, send_sems, recv_sems = refs[:n], refs[n], refs[n + 1]
            for cp in self._copies(land_refs, send_sems, recv_sems):
                cp.wait_send()
                cp.wait_recv()

        operands = [_in_hbm(a) for a in self.lands] + [self.send_sems, self.recv_sems]
        in_specs = [HBM_ONLY] * n + [SEM_SPEC, SEM_SPEC]
        if after is not None:
            operands.append(after)
            in_specs.append(HBM_SPEC)
        outs = pl.pallas_call(
            body, name=self.name + "_wait",
            in_specs=in_specs, out_specs=[HBM_ONLY] * n,
            out_shape=[pltpu.HBM(a.shape, a.dtype) for a in self.lands],
            input_output_aliases={i: i for i in range(n)},
            compiler_params=pltpu.CompilerParams(has_side_effects=SIDE_EFFECT),
        )(*operands)
        return list(outs)


class _Exchange:
    def __init__(self, mode, srcs, lands, kinds, layers, name, after=None, halves=None):
        self.mode, self.kinds, self.layers, self.name, self.n = mode, kinds, layers, name, len(lands)
        self.halves = halves if halves is not None else [False] * len(lands)
        n, ns = self.n, len(srcs)
        n_in = ns + n + (after is not None)
        sem_shape = pltpu.SemaphoreType.DMA((n * N_CHIPS,))

        def body(*refs):
            src_refs, land_refs = refs[:ns], refs[ns:ns + n]
            send_sems, recv_sems = refs[n_in], refs[n_in + 1]
            token = refs[-1]
            c = lax.axis_index("c")
            me = 2 * lax.axis_index("x") + lax.axis_index("y")
            for j in range(N_CHIPS):
                @pl.when(me == j)
                def _():
                    for a in range(n):
                        for t in range(N_CHIPS):
                            if t != j:
                                src, dst = self._ends(src_refs, land_refs, a, j, t, c)
                                pltpu.make_async_remote_copy(
                                    src_ref=src, dst_ref=dst, send_sem=send_sems.at[a * N_CHIPS + t],
                                    recv_sem=recv_sems.at[a * N_CHIPS + j],
                                    device_id=_chip_of(t, c), device_id_type=MESH).start()
            token[...] = jnp.zeros(token.shape, token.dtype)

        arrays = list(srcs) + list(lands)
        operands = [_in_hbm(a) for a in arrays]
        in_specs = [HBM_ONLY] * (ns + n)
        if after is not None:
            operands.append(after)
            in_specs.append(HBM_SPEC)
        outs = pl.pallas_call(
            body, name=name + "_start",
            in_specs=in_specs,
            out_specs=[SEM_SPEC, SEM_SPEC] + [HBM_ONLY] * (ns + n) + [pl.BlockSpec(memory_space=pltpu.VMEM)],
            out_shape=[sem_shape, sem_shape] + [pltpu.HBM(a.shape, a.dtype) for a in arrays]
                      + [jax.ShapeDtypeStruct((8, LANES), F32)],
            input_output_aliases={i: i + 2 for i in range(ns + n)},
            compiler_params=pltpu.CompilerParams(has_side_effects=SIDE_EFFECT),
        )(*operands)
        self.send_sems, self.recv_sems = outs[0], outs[1]
        self.srcs, self.lands = list(outs[2:2 + ns]), list(outs[2 + ns:2 + ns + n])
        self.token = outs[-1]

    def _ends(self, src_refs, land_refs, a, me_j, peer, c):
        if self.mode == "gather":
            mine = _piece(land_refs[a], self.kinds[a], me_j)
            if self.halves[a]:
                mine = _half(mine, c)
            return mine, mine
        return _piece(src_refs[a], self.kinds[a], peer), land_refs[a].at[me_j, pl.ds(self.layers[a], 1)]

    def wait(self, after, lands=None):
        n, ns = self.n, len(self.srcs)
        lands = self.lands if lands is None else lands

        def body(*refs):
            src_refs, land_refs = refs[:ns], refs[ns:ns + n]
            send_sems, recv_sems = refs[ns + n], refs[ns + n + 1]
            c = lax.axis_index("c")
            me = 2 * lax.axis_index("x") + lax.axis_index("y")
            for j in range(N_CHIPS):
                @pl.when(me != j)
                def _():
                    for a in range(n):
                        sent, _ = self._ends(src_refs, land_refs, a, 0, j, c)
                        _, landed = self._ends(src_refs, land_refs, a, j, 0, c)
                        cp = pltpu.make_async_remote_copy(
                            src_ref=sent, dst_ref=landed, send_sem=send_sems.at[a * N_CHIPS + j],
                            recv_sem=recv_sems.at[a * N_CHIPS + j],
                            device_id=_chip_of(j, c), device_id_type=MESH)
                        cp.wait_send()
                        cp.wait_recv()

        arrays = self.srcs + list(lands)
        operands = [_in_hbm(a) for a in arrays] + [self.send_sems, self.recv_sems]
        in_specs = [HBM_ONLY] * (ns + n) + [SEM_SPEC, SEM_SPEC]
        if after is not None:
            operands.append(after)
            in_specs.append(HBM_SPEC)
        outs = pl.pallas_call(
            body, name=self.name + "_wait",
            in_specs=in_specs, out_specs=[HBM_ONLY] * (ns + n),
            out_shape=[pltpu.HBM(a.shape, a.dtype) for a in arrays],
            input_output_aliases={i: i for i in range(ns + n)},
            compiler_params=pltpu.CompilerParams(has_side_effects=SIDE_EFFECT),
        )(*operands)
        return list(outs[:ns]), list(outs[ns:])


def _sum_arrivals(zone, own_grads, kind, chip, name, after=None):
    _, layers, r, c = zone.shape
    tm = _pick_rows(r, 256)
    nb = r // tm

    def own_idx(l, i, chip_ref):
        return (0, chip_ref[0] * nb + i, 0) if kind == "row" else (0, i, _slot(kind, chip_ref[0]))

    def slot_idx(k):
        return lambda l, i, chip_ref: (jnp.where(chip_ref[0] == k, (k + 1) % N_CHIPS, k), l, i, 0)

    in_specs = [pl.BlockSpec((None, None, tm, c), slot_idx(k)) for k in range(N_CHIPS)]
    in_specs += [pl.BlockSpec((None, tm, c), own_idx) for _ in own_grads]
    operands = [zone] * N_CHIPS + list(own_grads)
    if after is not None:
        in_specs.append(HBM_SPEC)
        operands.append(after)

    def body(chip_ref, *refs):
        slot_refs, own_refs, o_ref = refs[:N_CHIPS], refs[N_CHIPS:N_CHIPS + layers], refs[-1]
        own = own_refs[0][...]
        for u in range(1, layers):
            own = jnp.where(pl.program_id(0) == u, own_refs[u][...], own)
        acc = None
        for k in range(N_CHIPS):
            term = jnp.where(chip_ref[0] == k, own, slot_refs[k][...]).astype(F32)
            acc = term if acc is None else acc + term
        o_ref[...] = acc.astype(o_ref.dtype)

    return pl.pallas_call(
        body, name=name,
        grid_spec=pltpu.PrefetchScalarGridSpec(
            num_scalar_prefetch=1, grid=(layers, nb), in_specs=in_specs,
            out_specs=pl.BlockSpec((tm, c), lambda l, i, chip_ref: (l * nb + i, 0))),
        out_shape=jax.ShapeDtypeStruct((layers * r, c), BF16),
        compiler_params=_params(("arbitrary", "arbitrary")),
    )(chip, *operands)


class _SiblingSwap:
    def __init__(self, arrays, name, after=None):
        self.name, self.n = name, len(arrays)
        n = self.n
        n_in = n + (after is not None)
        sem_shape = pltpu.SemaphoreType.DMA((n,))

        def body(*refs):
            ins, send_sems, recv_sems = refs[:n], refs[n_in], refs[n_in + 1]
            theirs, token = refs[n_in + 2 + n:n_in + 2 + 2 * n], refs[-1]
            for cp in self._copies(ins, theirs, send_sems, recv_sems):
                cp.start()
            token[...] = jnp.zeros(token.shape, token.dtype)

        operands, in_specs = [_in_hbm(a) for a in arrays], [HBM_ONLY] * n
        if after is not None:
            operands.append(after)
            in_specs.append(HBM_SPEC)
        outs = pl.pallas_call(
            body, name=name + "_start",
            in_specs=in_specs,
            out_specs=[SEM_SPEC, SEM_SPEC] + [HBM_ONLY] * (2 * n) + [pl.BlockSpec(memory_space=pltpu.VMEM)],
            out_shape=[sem_shape, sem_shape] + [pltpu.HBM(a.shape, a.dtype) for a in arrays] * 2
                      + [jax.ShapeDtypeStruct((8, LANES), F32)],
            input_output_aliases={i: i + 2 for i in range(n)},
            compiler_params=pltpu.CompilerParams(has_side_effects=SIDE_EFFECT),
        )(*operands)
        self.send_sems, self.recv_sems = outs[0], outs[1]
        self.mine, self.theirs, self.token = list(outs[2:2 + n]), list(outs[2 + n:2 + 2 * n]), outs[-1]

    def _copies(self, mine, theirs, send_sems, recv_sems):
        sibling = (lax.axis_index("x"), lax.axis_index("y"), 1 - lax.axis_index("c"))
        return [pltpu.make_async_remote_copy(src_ref=mine[a], dst_ref=theirs[a], send_sem=send_sems.at[a],
                                             recv_sem=recv_sems.at[a], device_id=sibling, device_id_type=MESH)
                for a in range(self.n)]

    def wait(self, after):
        n = self.n

        def body(*refs):
            for cp in self._copies(refs[:n], refs[n:2 * n], refs[2 * n], refs[2 * n + 1]):
                cp.wait_send()
                cp.wait_recv()

        arrays = self.mine + self.theirs
        outs = pl.pallas_call(
            body, name=self.name + "_wait",
            in_specs=[HBM_ONLY] * (2 * n) + [SEM_SPEC, SEM_SPEC, HBM_SPEC], out_specs=[HBM_ONLY] * (2 * n),
            out_shape=[pltpu.HBM(a.shape, a.dtype) for a in arrays],
            input_output_aliases={i: i for i in range(2 * n)},
            compiler_params=pltpu.CompilerParams(has_side_effects=SIDE_EFFECT),
        )(*[_in_hbm(a) for a in arrays], self.send_sems, self.recv_sems, after)
        return list(outs[:n]), list(outs[n:])


class _GatherDevices:
    def __init__(self, vec):
        sem_shape = pltpu.SemaphoreType.DMA((N_DEV,))

        def body(in_ref, send_sems, recv_sems, vec_ref, out_ref, token):
            for cp in self._copies(in_ref, out_ref, send_sems, recv_sems):
                cp.start()
            token[...] = jnp.zeros(token.shape, token.dtype)

        outs = pl.pallas_call(
            body, name="gather_small_start",
            in_specs=[HBM_ONLY],
            out_specs=[SEM_SPEC, SEM_SPEC, HBM_ONLY, HBM_ONLY, pl.BlockSpec(memory_space=pltpu.VMEM)],
            out_shape=[sem_shape, sem_shape, pltpu.HBM(vec.shape, vec.dtype),
                       pltpu.HBM((N_DEV,) + vec.shape, vec.dtype), jax.ShapeDtypeStruct((8, LANES), F32)],
            input_output_aliases={0: 2},
            compiler_params=pltpu.CompilerParams(has_side_effects=SIDE_EFFECT),
        )(_in_hbm(vec))
        self.send_sems, self.recv_sems, self.vec, self.rows, self.token = outs

    def _copies(self, in_ref, out_ref, send_sems, recv_sems):
        x, y, c = lax.axis_index("x"), lax.axis_index("y"), lax.axis_index("c")
        me = 4 * x + 2 * y + c
        copies = [pltpu.make_async_copy(in_ref, out_ref.at[me], recv_sems.at[0])]
        for rel in range(1, N_DEV):
            peer = (x ^ (rel >> 2), y ^ ((rel >> 1) & 1), c ^ (rel & 1))
            copies.append(pltpu.make_async_remote_copy(
                src_ref=in_ref, dst_ref=out_ref.at[me], send_sem=send_sems.at[rel], recv_sem=recv_sems.at[rel],
                device_id=peer, device_id_type=MESH))
        return copies

    def wait(self, after):
        def body(vec_ref, rows_ref, send_sems, recv_sems, after_ref, vec_out, rows_out):
            copies = self._copies(vec_ref, rows_ref, send_sems, recv_sems)
            copies[0].wait()
            for cp in copies[1:]:
                cp.wait_send()
                cp.wait_recv()

        outs = pl.pallas_call(
            body, name="gather_small_wait",
            in_specs=[HBM_ONLY, HBM_ONLY, SEM_SPEC, SEM_SPEC, HBM_SPEC], out_specs=[HBM_ONLY, HBM_ONLY],
            out_shape=[pltpu.HBM(self.vec.shape, self.vec.dtype), pltpu.HBM(self.rows.shape, self.rows.dtype)],
            input_output_aliases={0: 0, 1: 1},
            compiler_params=pltpu.CompilerParams(has_side_effects=SIDE_EFFECT),
        )(_in_hbm(self.vec), _in_hbm(self.rows), self.send_sems, self.recv_sems, after)
        return outs[1]


BIG = [("a_w_in", "col"), ("a_w_out", "row"), ("kv_w", "row"), ("b_w_q", "row"), ("b_w_out", "row"),
       ("ffn_w_gate_up", "colp"), ("ffn_w_down", "row"), ("ple_w_up", "col"), ("ple_w_gate", "row")]
GATHER_GROUPS = [[("a_w_in", 0), ("small", 0)], [("a_w_out", 0), ("ffn_w_gate_up", 0)],
                 [("ffn_w_down", 0), ("ple_w_gate", 0), ("ple_w_up", 0)], [("kv_w", 0), ("b_w_q", 0), ("b_w_out", 0)],
                 [("ffn_w_gate_up", 1)], [("ffn_w_down", 1), ("ple_w_gate", 1), ("ple_w_up", 1)]]
SCATTER_GROUPS = [[("ple_w_gate", 1), ("ple_w_up", 1), ("ffn_w_down", 1)], [("ffn_w_gate_up", 1)],
                  [("b_w_out", 0), ("b_w_q", 0), ("kv_w", 0)], [("ple_w_gate", 0), ("ple_w_up", 0), ("ffn_w_down", 0)],
                  [("ffn_w_gate_up", 0), ("a_w_out", 0)], [("a_w_in", 0)]]
SMALL_SHARDED = ["ln_gain", "ln_bias", "a_lower_bound"]
SMALL_REPLICATED = ["a_norm_gain", "kv_b", "b_b_q", "b_sinks", "b_b_out", "ple_b_gate"]
WEIGHT_ORDER = ["a_w_in", "a_lower_bound", "a_norm_gain", "a_w_out", "kv_w", "kv_b", "b_w_q", "b_b_q", "b_sinks",
                "b_w_out", "b_b_out", "ffn_w_gate_up", "ffn_w_down", "ple_w_up", "ple_w_gate", "ple_b_gate",
                "ln_gain", "ln_bias"]


def _as3(a):
    return a.reshape((-1,) + a.shape[-2:]) if a.ndim >= 3 else a.reshape((1,) + a.shape)


def _pad_lanes(v):
    n = v.shape[-1]
    return jnp.pad(v, ((0, 0), (0, (-n) % LANES)))


def _adam_small_fn(w, mom, vel, g):
    return _adam_fn(w, mom, vel, g, jnp.zeros_like(g))[1:]


def _sum_rows_fn(slots):
    acc = slots[0]
    for s in range(1, slots.shape[0]):
        acc = acc + slots[s]
    return (acc,)


def kernel(x, p, a_w_in, a_lower_bound, a_norm_gain, a_w_out, kv_w, kv_b, b_w_q, b_b_q, b_sinks, b_w_out, b_b_out, ffn_w_gate_up, ffn_w_down, ple_w_up, ple_w_gate, ple_b_gate, ln_gain, ln_bias, loss_target, m_a_w_in, m_a_lower_bound, m_a_norm_gain, m_a_w_out, m_kv_w, m_kv_b, m_b_w_q, m_b_b_q, m_b_sinks, m_b_w_out, m_b_b_out, m_ffn_w_gate_up, m_ffn_w_down, m_ple_w_up, m_ple_w_gate, m_ple_b_gate, m_ln_gain, m_ln_bias, v_a_w_in, v_a_lower_bound, v_a_norm_gain, v_a_w_out, v_kv_w, v_kv_b, v_b_w_q, v_b_b_q, v_b_sinks, v_b_w_out, v_b_b_out, v_ffn_w_gate_up, v_ffn_w_down, v_ple_w_up, v_ple_w_gate, v_ple_b_gate, v_ln_gain, v_ln_bias):
    args = dict(locals())
    wts = {n: args[n] for n in WEIGHT_ORDER}
    mom = {n: args["m_" + n] for n in WEIGHT_ORDER}
    vel = {n: args["v_" + n] for n in WEIGHT_ORDER}
    chip = 2 * lax.axis_index("x") + lax.axis_index("y")
    d = x.shape[-1]
    dq = d // N_CHIPS

    kind_of = dict(BIG)
    kind_of["small"] = "col"
    chip_arr = chip.reshape(1).astype(jnp.int32)
    small_pack = jnp.concatenate([wts[n].reshape(-1, dq) for n in SMALL_SHARDED], axis=0)[None]

    def place(key, after):
        n, layer = key
        if n == "small":
            return _place(small_pack, 0, "col", chip_arr, name="place_small", out_dtype=F32, after=after)
        return _place(_as3(wts[n]), layer, kind_of[n], chip_arr, name=f"place_{n}{layer}", out_dtype=BF16,
                      after=after)

    gathers, where = [], {}
    for gi, group in enumerate(GATHER_GROUPS):
        prev = gathers[-1].token if gathers else None
        gathers.append(_Exchange("gather", [], [place(k, prev) for k in group], [kind_of[k[0]] for k in group],
                                 [0] * len(group), f"gather{gi}", after=prev,
                                 halves=[k[0] != "small" for k in group]))
        for k in group:
            where[k] = gi
    all_started = gathers[-1].token
    ready = {}

    fills = {}

    def pass_on(gi, after):
        if gi not in fills:
            group = GATHER_GROUPS[gi]
            outs = gathers[gi].wait(after)[1]
            split = [i for i, k in enumerate(group) if k[0] != "small"]
            fills[gi] = (outs, split, _SiblingFill([outs[i] for i in split], [kind_of[group[i][0]] for i in split],
                                                   f"fill{gi}"))

    def wget(name, layer, after):
        key = (name, layer)
        if key not in ready:
            gi = where[key]
            after = all_started if gi == 0 else after
            pass_on(gi, after)
            if 1 <= gi < len(GATHER_GROUPS) - 1:
                pass_on(gi + 1, after)
                after = fills[gi + 1][2].token
            outs, split, fill = fills[gi]
            for i, arr in zip(split, fill.wait(after)):
                outs[i] = arr
            for k, arr in zip(GATHER_GROUPS[gi], outs):
                ready[k] = arr
        return ready[key]

    small_full = wget("small", 0, None)[0]
    ln_gain_f = small_full[0:6].reshape(DEPTH, 3, d)
    ln_bias_f = small_full[6:12].reshape(DEPTH, 3, d)
    alb_f = small_full[12:14]

    group_of = {k: gi for gi, group in enumerate(SCATTER_GROUPS) for k in group}
    grads_done, zones, scatters = {}, {}, []

    def grad_sink(name, layer, grad):
        grads_done[(name, layer)] = grad
        if name not in zones:
            zones[name] = lax.empty((N_CHIPS,) + _as3(wts[name]).shape, BF16)
        gi = group_of[(name, layer)]
        group = SCATTER_GROUPS[gi]
        if not all(k in grads_done for k in group):
            return None
        ex = _Exchange("scatter", [grads_done[k] for k in group], [zones[k[0]] for k in group],
                       [kind_of[k[0]] for k in group], [k[1] for k in group], f"scatter{gi}")
        for k, zone in zip(group, ex.lands):
            zones[k[0]] = zone
        scatters.append((ex, group))
        return ex.token

    small = {}

    def small_sink(loss, gs):
        ln_g = jnp.concatenate([gs[f"ln_gain_{i}_{j}"] for i in range(DEPTH) for j in range(3)], axis=0)
        ln_b = jnp.concatenate([gs[f"ln_bias_{i}_{j}"] for i in range(DEPTH) for j in range(3)], axis=0)
        ple_bg = jnp.concatenate([gs[f"ple_b_{i}"] for i in range(DEPTH)], axis=0)
        small["list"] = [ln_g.reshape(1, -1), ln_b.reshape(1, -1), gs["alb"].reshape(1, -1), gs["norm_gain"],
                         gs["kv_b"], gs["b_q"], _pad_lanes(gs["sinks"]), gs["b_out"], ple_bg.reshape(1, -1), loss]
        small["gather"] = _GatherDevices(jnp.concatenate(small["list"], axis=1))
        return small["gather"].token

    loss, grad_x, gs = _local_step(
        x[0], p.reshape((p.shape[0],) + p.shape[2:]), loss_target[0], wget, grad_sink, ln_gain_f, ln_bias_f, alb_f, a_norm_gain, kv_b, b_b_q,
        b_sinks, b_b_out, ple_b_gate, small_sink)

    res = {}

    def arrive(batch, after):
        for ex, group in batch:
            srcs, outs = ex.wait(after, lands=[zones[k[0]] for k in group])
            for k, grad, zone in zip(group, srcs, outs):
                grads_done[k], zones[k[0]] = grad, zone

    def half_sums(names, tag, after):
        partial = []
        for n in names:
            own = [grads_done[(n, layer)] for layer in range(zones[n].shape[1])]
            partial.append(_sum_arrivals(zones[n], own, kind_of[n], chip_arr, f"sum_{n}", after=after))
        return _SiblingSwap(partial, tag, after=after)

    def update(names, swap, after):
        for n, own, sib in zip(names, *swap.wait(after)):
            shp = wts[n].shape
            flat = lambda a: a.reshape(-1, shp[-1])
            out = _rowwise(_adam_fn, [flat(wts[n]), flat(mom[n]), flat(vel[n]), own, sib], [],
                           [(own.shape, F32)] * 4, name=f"adam_{n}")
            res[n] = [o.reshape(shp) for o in out]
        return res[names[-1]][1]

    last_names = [k[0] for k in SCATTER_GROUPS[-1]]
    batches = [["ffn_w_gate_up"], [n for n, _ in BIG if n != "ffn_w_gate_up" and n not in last_names], last_names]
    arrive(scatters[:-1], grad_x)
    swap0 = half_sums(batches[0], "swap0", None)
    swap1 = half_sums(batches[1], "swap1", swap0.token)
    updated = update(batches[0], swap0, swap1.token)
    arrive(scatters[-1:], updated)
    swap2 = half_sums(batches[2], "swap2", swap1.token)
    updated = update(batches[1], swap1, swap2.token)
    update(batches[2], swap2, updated)

    small_list = small["list"]
    everyone = small["gather"].wait(grad_x)
    total, = _rowwise(_sum_rows_fn, [everyone], [], [(everyone.shape[1:], F32)], name="sum_small")
    offs, pos = [], 0
    for v in small_list:
        offs.append((pos, v.shape[1]))
        pos += v.shape[1]

    def seg(k):
        return total[0, offs[k][0]:offs[k][0] + offs[k][1]]

    def my_cols(full, rows):
        return lax.dynamic_slice_in_dim(full.reshape(rows, N_CHIPS, dq), chip, 1, axis=1).reshape(rows, dq)

    n_sink = b_sinks.shape[-1]
    small_grads = {
        "ln_gain": my_cols(seg(0), 6).reshape(ln_gain.shape), "ln_bias": my_cols(seg(1), 6).reshape(ln_bias.shape),
        "a_lower_bound": my_cols(seg(2), 2), "a_norm_gain": seg(3).reshape(a_norm_gain.shape),
        "kv_b": seg(4).reshape(kv_b.shape), "b_b_q": seg(5).reshape(b_b_q.shape),
        "b_sinks": seg(6)[:n_sink].reshape(b_sinks.shape), "b_b_out": seg(7).reshape(b_b_out.shape),
        "ple_b_gate": seg(8).reshape(ple_b_gate.shape)}
    names = SMALL_SHARDED + SMALL_REPLICATED
    pack = lambda dct: _pad_lanes(jnp.concatenate([dct[n].reshape(1, -1) for n in names], axis=1))
    g_pack = pack(small_grads)
    upd = _rowwise(_adam_small_fn, [pack(wts), pack(mom), pack(vel), g_pack], [], [(g_pack.shape, F32)] * 3,
                   name="adam_small")
    pos = 0
    for n in names:
        size = wts[n].size
        res[n] = [small_grads[n]] + [u[0, pos:pos + size].reshape(wts[n].shape) for u in upd]
        pos += size

    outs = [seg(9)[0], grad_x[None]]
    for k in range(4):
        outs += [res[n][k] for n in WEIGHT_ORDER]
    return tuple(outs)
```

```python
import functools

import jax
import jax.numpy as jnp
from jax import lax
from jax.experimental import pallas as pl
from jax.experimental.pallas import tpu as pltpu

F32 = jnp.float32
BF16 = jnp.bfloat16
MESH = pl.DeviceIdType.MESH

LANES = 128
HG_DK = 128
HG_CHUNK = 64
HG_SUB = 16
HG_ROWS = 512
HG_HEADS_PER_STEP = 2
LOG2_E = 1.4426950408889634
ATT_HD = 64
ATT_G = 4
WINDOW = 128
DEPTH = 2
ALPHA = (2.0 * DEPTH) ** 0.25
LN_EPS = 1e-5
RMS_EPS = 1e-6
ADAM_LR, ADAM_B1, ADAM_B2, ADAM_EPS, ADAM_WD, ADAM_STEP = 0.001, 0.9, 0.999, 1e-08, 0.01, 10
N_CHIPS = 4
N_DEV = 8
VMEM_LIMIT = 56 * 1024 * 1024
NEG = -1e30


def _pick(n, cap):
    best = None
    for d in range(LANES, min(n, cap) + 1, LANES):
        if n % d == 0:
            best = d
    return n if best is None else best


def _pick_rows(m, cap):
    best = None
    for d in range(16, min(m, cap) + 1, 16):
        if m % d == 0:
            best = d
    return m if best is None else best


def _params(sem):
    return pltpu.CompilerParams(dimension_semantics=sem, vmem_limit_bytes=VMEM_LIMIT)


def _zeros_index(ndim, grid_rank=3):
    return (lambda i, j, kk: (0,) * ndim) if grid_rank == 3 else (lambda kk, i: (0,) * ndim)


def _mm(a, b, *, name, la=None, lb=None, ta=False, tb=False, bias=None, add=None, out_dtype=F32,
        out_layers=None, out_layer=None, after=None, post=None, tile_cols=None, caps=(1024, 1536, 2048)):
    ar, ac = a.shape[-2:]
    br, bc = b.shape[-2:]
    m, k = (ac, ar) if ta else (ar, ac)
    k2, n = (bc, br) if tb else (br, bc)
    assert k == k2, (a.shape, b.shape, ta, tb)
    if post is not None:
        caps = (512, n if tile_cols is None else tile_cols, caps[2])
    tm, tn, tk = _pick(m, caps[0]), _pick(n, caps[1]), _pick(k, caps[2])
    assert post is None or tn == caps[1]
    nk = k // tk
    gi, gj = m // tm, n // tn
    a_bytes, b_bytes = m * k * a.dtype.itemsize, k * n * b.dtype.itemsize
    rows_outer = (a_bytes + b_bytes * (gi if gj * nk > 1 else 1)) <= (b_bytes + a_bytes * (gj if gi * nk > 1 else 1))
    k_outer = post is not None and nk > 1 and gj == 1
    grid = (nk, gi) if k_outer else (gi, gj, nk) if rows_outer else (gj, gi, nk)
    keep_at = ta and nk == 1 and gj > 1 and rows_outer

    def bs(block, idx, late=False):
        if k_outer:
            return pl.BlockSpec(block, lambda kk, i: idx(jnp.where(kk == nk - 1, i, 0) if late else i, 0, kk))
        return pl.BlockSpec(block, idx if rows_outer else (lambda q, p, kk: idx(p, q, kk)))

    def spec(block, idx, layer):
        if layer is None:
            return bs(block, idx)
        return bs((None,) + block, lambda i, j, kk: (layer,) + idx(i, j, kk))

    a_spec = spec((tk, tm), lambda i, j, kk: (kk, i), la) if ta else spec((tm, tk), lambda i, j, kk: (i, kk), la)
    b_spec = spec((tn, tk), lambda i, j, kk: (j, kk), lb) if tb else spec((tk, tn), lambda i, j, kk: (kk, j), lb)
    in_specs, operands = [a_spec, b_spec], [a, b]
    if bias is not None:
        in_specs.append(bs((1, tn), lambda i, j, kk: (0, j)))
        operands.append(bias)
    if add is not None:
        in_specs.append(bs((tm, tn), lambda i, j, kk: (i, j), late=True))
        operands.append(add)
    if after is not None:
        in_specs.append(pl.BlockSpec(memory_space=pl.ANY))
        operands.append(after)
    dims = (((0 if ta else 1,), (1 if tb else 0,)), ((), ()))
    has_bias, has_add = bias is not None, add is not None
    if post is None:
        fn, rows, whole, outs, sums = None, [], [], [], []
        out_shape = jax.ShapeDtypeStruct((m, n) if out_layers is None else (out_layers, m, n), out_dtype)
        out_specs = spec((tm, tn), lambda i, j, kk: (i, j), out_layer)
    else:
        fn, rows, whole, outs, sums = post
        in_specs += [bs((tm, r.shape[-1] // gj), lambda i, j, kk: (i, j), late=True) for r in rows]
        in_specs += [pl.BlockSpec(tuple(w.shape), _zeros_index(w.ndim, len(grid))) for w in whole]
        operands += list(rows) + list(whole)
        out_shape = [jax.ShapeDtypeStruct(sh, dt) for sh, dt in list(outs) + list(sums)]
        out_specs = ([bs((tm, sh[-1] // gj), lambda i, j, kk: (i, j), late=True) for sh, _ in outs]
                     + [pl.BlockSpec(tuple(sh), _zeros_index(len(sh), len(grid))) for sh, _ in sums])
    n_in, n_extra, n_outs, n_sums = len(operands), len(rows) + len(whole), len(outs), len(sums)

    def body(*refs):
        a_ref, b_ref = refs[0], refs[1]
        pos = 2
        bias_ref = add_ref = None
        if has_bias:
            bias_ref = refs[pos]
            pos += 1
        if has_add:
            add_ref = refs[pos]
            pos += 1
        extra_refs = refs[n_in - n_extra:n_in]
        out_refs = refs[n_in:n_in + max(n_outs, 1)]
        sum_refs = refs[n_in + n_outs:n_in + n_outs + n_sums]
        acc_ref = refs[-1] if nk > 1 else None
        if keep_at:
            at_ref = refs[-1]

            @pl.when(pl.program_id(1) == 0)
            def _():
                at_ref[...] = a_ref[...].astype(BF16).T

            part = lax.dot_general(at_ref[...], b_ref[...].astype(BF16), (((1,), (1 if tb else 0,)), ((), ())),
                                   preferred_element_type=F32)
        else:
            part = lax.dot_general(a_ref[...].astype(BF16), b_ref[...].astype(BF16), dims,
                                   preferred_element_type=F32)

        def finish(total):
            if has_bias:
                total = total + bias_ref[...]
            if has_add:
                total = total + add_ref[...]
            if fn is None:
                out_refs[0][...] = total.astype(out_refs[0].dtype)
                return
            res = fn(total, *[r[...] for r in extra_refs])
            for ref, val in zip(out_refs, res[:n_outs]):
                ref[...] = val.astype(ref.dtype)
            if n_sums:
                @pl.when(pl.program_id(1 if k_outer or not rows_outer else 0) == 0)
                def _():
                    for ref in sum_refs:
                        ref[...] = jnp.zeros(ref.shape, ref.dtype)

                for ref, val in zip(sum_refs, res[n_outs:]):
                    ref[...] += val

        if nk == 1:
            finish(part)
        elif k_outer:
            kk = pl.program_id(0)
            rows_i = pl.ds(pl.multiple_of(pl.program_id(1) * tm, tm), tm)

            @pl.when(kk == 0)
            def _():
                acc_ref[rows_i, :] = part

            @pl.when(kk > 0)
            def _():
                acc_ref[rows_i, :] += part

            @pl.when(kk == nk - 1)
            def _():
                finish(acc_ref[rows_i, :])
        else:
            kk = pl.program_id(2)

            @pl.when(kk == 0)
            def _():
                acc_ref[...] = part

            @pl.when(kk > 0)
            def _():
                acc_ref[...] += part

            @pl.when(kk == nk - 1)
            def _():
                finish(acc_ref[...])

    return pl.pallas_call(
        body, name=name, grid=grid, in_specs=in_specs, out_specs=out_specs, out_shape=out_shape,
        scratch_shapes=([pltpu.VMEM((m, n) if k_outer else (tm, tn), F32)] if nk > 1
                        else [pltpu.VMEM((tm, tk), BF16)] if keep_at else []),
        compiler_params=_params(("arbitrary", "arbitrary") if k_outer
                                else ("arbitrary" if n_sums else "parallel", "arbitrary" if keep_at else "parallel",
                                      "arbitrary") if rows_outer
                                else ("parallel", "arbitrary" if n_sums else "parallel", "arbitrary")),
    )(*operands)


def _rowwise(fn, rows, whole, outs, sums=(), *, name, tm=256):
    m = rows[0].shape[-2]
    tm = _pick_rows(m, tm)
    n_rows, n_whole, n_outs, n_sums = len(rows), len(whole), len(outs), len(sums)

    def rspec(shape):
        lead = len(shape) - 2
        return pl.BlockSpec(tuple(shape[:-2]) + (tm, shape[-1]), lambda i: (0,) * lead + (i, 0))

    def wspec(shape):
        return pl.BlockSpec(tuple(shape), lambda i: (0,) * len(shape))

    def body(*refs):
        vals = [r[...] for r in refs[:n_rows + n_whole]]
        out_refs = refs[n_rows + n_whole:n_rows + n_whole + n_outs]
        sum_refs = refs[n_rows + n_whole + n_outs:]
        res = fn(*vals)
        for ref, val in zip(out_refs, res[:n_outs]):
            ref[...] = val.astype(ref.dtype)
        if n_sums:
            @pl.when(pl.program_id(0) == 0)
            def _():
                for ref in sum_refs:
                    ref[...] = jnp.zeros(ref.shape, ref.dtype)

            for ref, val in zip(sum_refs, res[n_outs:]):
                ref[...] += val

    result = pl.pallas_call(
        body, name=name, grid=(m // tm,),
        in_specs=[rspec(r.shape) for r in rows] + [wspec(w.shape) for w in whole],
        out_specs=[rspec(s) for s, _ in outs] + [wspec(s) for s, _ in sums],
        out_shape=[jax.ShapeDtypeStruct(s, d) for s, d in list(outs) + list(sums)],
        compiler_params=_params(("arbitrary",)),
    )(*rows, *whole)
    return result


def _sigmoid(v):
    return jax.nn.sigmoid(v)


def _col_sum(v):
    return jnp.sum(v, axis=0, keepdims=True)


def _ln_stats(z):
    mu = jnp.mean(z, axis=-1, keepdims=True)
    zc = z - mu
    var = jnp.mean(zc * zc, axis=-1, keepdims=True)
    rstd = lax.rsqrt(var + LN_EPS)
    return zc * rstd, rstd


def _ln_fwd_fn(xin, h, gain, bias):
    xhat, _ = _ln_stats(ALPHA * xin + h)
    y = xhat * gain + bias
    return y, y


def _ple_ln_fwd_fn(xin, pg, pu, gain, bias):
    xhat, _ = _ln_stats(ALPHA * xin + _sigmoid(pg) * pu)
    y = xhat * gain + bias
    return y, y


def _ln_dz(dy, z, gain):
    xhat, rstd = _ln_stats(z)
    dxhat = dy * gain
    dz = rstd * (dxhat - jnp.mean(dxhat, axis=-1, keepdims=True)
                 - xhat * jnp.mean(dxhat * xhat, axis=-1, keepdims=True))
    return dz, _col_sum(dy * xhat), _col_sum(dy)


def _ln_bwd_fn(dy, xin, h, gain):
    dz, dgain, dbias = _ln_dz(dy, ALPHA * xin + h, gain)
    return ALPHA * dz, dz, dgain, dbias, _col_sum(dz)


def _ple_ln_bwd_fn(dy, xin, pg, pu, gain):
    sg = _sigmoid(pg)
    dz, dgain, dbias = _ln_dz(dy, ALPHA * xin + sg * pu, gain)
    dpg = dz * pu * sg * (1.0 - sg)
    return ALPHA * dz, dpg, dz * sg, dgain, dbias, _col_sum(dpg)


def _swiglu_fwd_fn(gu):
    hid = gu.shape[-1] // 2
    gate, up = gu[:, :hid], gu[:, hid:]
    return gu, gate * _sigmoid(gate) * up


def _swiglu_bwd_fn(dact, gu):
    gu = gu.astype(F32)
    hid = gu.shape[-1] // 2
    gate, up = gu[:, :hid], gu[:, hid:]
    sg = _sigmoid(gate)
    dgate = dact * up * sg * (1.0 + gate * (1.0 - sg))
    dup = dact * gate * sg
    return (jnp.concatenate([dgate, dup], axis=-1),)


def _loss_fn(y, target):
    err = y - target
    inv = 1.0 / y.shape[-1]
    part = 0.5 * inv * jnp.sum(jnp.sum(err * err, axis=-1, keepdims=True), axis=0, keepdims=True)
    return err * inv, jnp.broadcast_to(part, (1, LANES))


def _adam_fn(w, mom, vel, p_own, p_sib):
    g = p_own.astype(F32) + p_sib.astype(F32)
    m_new = ADAM_B1 * mom + (1.0 - ADAM_B1) * g
    v_new = ADAM_B2 * vel + (1.0 - ADAM_B2) * (g * g)
    m_hat = m_new / (1.0 - ADAM_B1 ** ADAM_STEP)
    v_hat = v_new / (1.0 - ADAM_B2 ** ADAM_STEP)
    delta = -ADAM_LR * (m_hat / (jnp.sqrt(v_hat) + ADAM_EPS) + ADAM_WD * w)
    return g, delta, m_new, v_new


def _split2(x):
    hi = x.astype(BF16)
    return hi, (x - hi.astype(F32)).astype(BF16)


def _dot3(a, b, dims):
    a_hi, a_lo = _split2(a)
    b_hi, b_lo = _split2(b)
    dn = (dims, ((), ()))
    return (lax.dot_general(a_hi, b_hi, dn, preferred_element_type=F32)
            + (lax.dot_general(a_hi, b_lo, dn, preferred_element_type=F32)
               + lax.dot_general(a_lo, b_hi, dn, preferred_element_type=F32)))


def _tdot(mask01, b):
    m = mask01.astype(BF16)
    b_hi = b.astype(BF16)
    rest = b - b_hi.astype(F32)
    b_mid = rest.astype(BF16)
    b_lo = (rest - b_mid.astype(F32)).astype(BF16)
    dn = (((1,), (0,)), ((), ()))
    return (lax.dot_general(m, b_hi, dn, preferred_element_type=F32)
            + (lax.dot_general(m, b_mid, dn, preferred_element_type=F32)
               + lax.dot_general(m, b_lo, dn, preferred_element_type=F32)))


def _hdot(a, b):
    return _dot3(a, b, ((1,), (0,)))


def _hdot_nt(a, b):
    return _dot3(a, b, ((1,), (1,)))


def _hdot_tn(a, b):
    return _dot3(a, b, ((0,), (0,)))


def _dot(a, b):
    return lax.dot_general(a.astype(BF16), b.astype(BF16), (((1,), (0,)), ((), ())), preferred_element_type=F32)


def _dot_nt(a, b):
    return lax.dot_general(a.astype(BF16), b.astype(BF16), (((1,), (1,)), ((), ())), preferred_element_type=F32)


def _dot_tn(a, b):
    return lax.dot_general(a.astype(BF16), b.astype(BF16), (((0,), (0,)), ((), ())), preferred_element_type=F32)


def _hg_masks():
    c = HG_CHUNK
    row = lax.broadcasted_iota(jnp.int32, (c, c), 0)
    col = lax.broadcasted_iota(jnp.int32, (c, c), 1)
    base = row & (-HG_SUB)
    return row, col, base, col <= row, col < base


def _hg_gates(qr, fr, alb):
    lbound = _sigmoid(alb[0:1, :] - alb[1:2, :])
    sig = _sigmoid(fr)
    forget = lbound + (1.0 - lbound) * sig
    kk = (1.0 - lbound) * _sigmoid(-fr)
    qt = qr * _sigmoid(qr) * (HG_DK ** -0.5)
    return qt, kk, jnp.log(forget), lbound, sig, forget


def _hg_scores(qt, kk, g, scores=True):
    c, nsub = HG_CHUNK, HG_CHUNK // HG_SUB
    row, col, base, causal, below = _hg_masks()
    b = _tdot(causal, g)
    rr = _tdot(below, g)
    bq = b - rr
    qh = qt * jnp.exp(bq)
    edecs = [None]
    parts = [jnp.zeros((HG_SUB, c), F32)]
    for i in range(1, nsub):
        edec = jnp.exp(jnp.minimum(rr[i * HG_SUB:i * HG_SUB + 1, :] - b, 0.0))
        edecs.append(edec)
        if scores:
            parts.append(_dot_nt(qh[i * HG_SUB:(i + 1) * HG_SUB, :], kk * edec))
    q3 = qt.reshape(nsub, HG_SUB, HG_DK)
    if not scores:
        return None, b, bq, qh, edecs, (b.reshape(nsub, HG_SUB, HG_DK), q3, kk.reshape(nsub, HG_SUB, HG_DK))
    a = jnp.where(below, jnp.concatenate(parts, axis=0), 0.0)
    b2 = b * LOG2_E
    b3 = b2.reshape(nsub, HG_SUB, HG_DK)
    c3 = (b2 - jnp.log2(kk)).reshape(nsub, HG_SUB, HG_DK)
    for j in range(HG_SUB):
        ek = jnp.exp2(b3 - c3[:, j:j + 1, :])
        colv = jnp.sum(q3 * ek, axis=-1, keepdims=True).reshape(c, 1)
        a = jnp.where(col == base + j, colv, a)
    a = jnp.where(causal, a, 0.0)
    return a, b, bq, qh, edecs, None


def _hg_norm(o, gr, gain):
    r = lax.rsqrt(jnp.mean(o * o, axis=-1, keepdims=True) + RMS_EPS)
    sg = _sigmoid(gr)
    return o * r * gain, r, sg


def _hgrn2_fwd(proj, alb, gain, *, rb):
    m, d4 = proj.shape
    d = d4 // 4
    heads = d // HG_DK
    hp = HG_HEADS_PER_STEP
    rb = min(rb, m)
    cpb = rb // HG_CHUNK
    nrb = m // rb

    def body(q_ref, f_ref, v_ref, g_ref, alb_ref, gain_ref, o_ref, og_ref, st_ref, a_ref, state):
        @pl.when(pl.program_id(1) == 0)
        def _():
            state[...] = jnp.zeros(state.shape, F32)

        def chunk(ci, carry):
            sl = pl.ds(pl.multiple_of(ci * HG_CHUNK, HG_CHUNK), HG_CHUNK)
            for u in range(hp):
                ln = slice(u * HG_DK, (u + 1) * HG_DK)
                qt, kk, g, _, _, _ = _hg_gates(q_ref[sl, ln], f_ref[sl, ln], alb_ref[:, ln])
                v = v_ref[sl, ln]
                st = state[u]
                st_ref[u, ci] = st
                a, b, _, _, _, _ = _hg_scores(qt, kk, g)
                a_ref[u, ci] = a.astype(a_ref.dtype)
                o = _dot(a, v) + _dot_nt(qt * jnp.exp(b), st)
                b_last = b[HG_CHUNK - 1:HG_CHUNK, :]
                state[u] = st * jnp.exp(b_last) + _hdot_tn(v, kk * jnp.exp(b_last - b))
                o_ref[sl, ln] = o
                n, _, sg = _hg_norm(o, g_ref[sl, ln], gain_ref[...])
                og_ref[sl, ln] = (n * g_ref[sl, ln] * sg).astype(og_ref.dtype)
            return carry

        lax.fori_loop(0, cpb, chunk, 0)

    def col(cidx):
        return pl.BlockSpec((rb, hp * HG_DK), lambda h, r: (r, cidx * (heads // hp) + h))

    return pl.pallas_call(
        body, name="hgrn2_fwd", grid=(heads // hp, nrb),
        in_specs=[col(0), col(1), col(2), col(3),
                  pl.BlockSpec((2, hp * HG_DK), lambda h, r: (0, h)),
                  pl.BlockSpec((1, HG_DK), lambda h, r: (0, 0))],
        out_specs=[pl.BlockSpec((rb, hp * HG_DK), lambda h, r: (r, h)),
                   pl.BlockSpec((rb, hp * HG_DK), lambda h, r: (r, h)),
                   pl.BlockSpec((hp, cpb, HG_DK, HG_DK), lambda h, r: (h, r, 0, 0)),
                   pl.BlockSpec((hp, cpb, HG_CHUNK, HG_CHUNK), lambda h, r: (h, r, 0, 0))],
        out_shape=[jax.ShapeDtypeStruct((m, d), F32), jax.ShapeDtypeStruct((m, d), BF16),
                   jax.ShapeDtypeStruct((heads, m // HG_CHUNK, HG_DK, HG_DK), F32),
                   jax.ShapeDtypeStruct((heads, m // HG_CHUNK, HG_CHUNK, HG_CHUNK), BF16)],
        scratch_shapes=[pltpu.VMEM((hp, HG_DK, HG_DK), F32)],
        compiler_params=_params(("parallel", "arbitrary")),
    )(proj, proj, proj, proj, alb, gain)


def _hgrn2_bwd(proj, o_pre, states, scores, dog, alb, gain, *, rb):
    m, d4 = proj.shape
    d = d4 // 4
    heads = d // HG_DK
    rb = min(rb, m)
    cpb = rb // HG_CHUNK
    nrb = m // rb
    c, nsub = HG_CHUNK, HG_CHUNK // HG_SUB

    def body(q_ref, f_ref, v_ref, g_ref, o_ref, st_ref, a_ref, dog_ref, alb_ref, gain_ref,
             dq_ref, df_ref, dv_ref, dg_ref, dalb_ref, dgain_ref, dstate, carry_ref):
        first = (pl.program_id(0) == 0) & (pl.program_id(1) == 0)

        @pl.when(first)
        def _():
            dgain_ref[...] = jnp.zeros(dgain_ref.shape, F32)

        @pl.when(pl.program_id(1) == 0)
        def _():
            dstate[...] = jnp.zeros(dstate.shape, F32)
            carry_ref[...] = jnp.zeros(carry_ref.shape, F32)
            dalb_ref[...] = jnp.zeros(dalb_ref.shape, F32)

        row, col, base, causal, below = _hg_masks()
        sub_iota = lax.broadcasted_iota(jnp.int32, (nsub, HG_SUB, HG_DK), 1)
        row_k = lax.broadcasted_iota(jnp.int32, (c, HG_DK), 0)
        upper = col >= row

        def chunk(step, carry):
            ci = cpb - 1 - step
            sl = pl.ds(pl.multiple_of(ci * HG_CHUNK, HG_CHUNK), HG_CHUNK)
            qr, fr, v, gr = q_ref[sl, :], f_ref[sl, :], v_ref[sl, :], g_ref[sl, :]
            qt, kk, g, lbound, sig, forget = _hg_gates(qr, fr, alb_ref[...])
            o = o_ref[sl, :]
            dogv = dog_ref[sl, :]
            gain_v = gain_ref[...]
            n, r, sg = _hg_norm(o, gr, gain_v)
            dgr = dogv * n * sg * (1.0 + gr * (1.0 - sg))
            dn = dogv * gr * sg
            dgain_ref[...] += _col_sum(dn * o * r)
            u = dn * gain_v
            d_o = r * u - o * (r * r * r) * jnp.mean(u * o, axis=-1, keepdims=True)
            st0 = st_ref[ci]
            dst = dstate[...]
            _, b, bq, qh, edecs, (b3, q3, k3) = _hg_scores(qt, kk, g, scores=False)
            a = a_ref[ci]
            eb = jnp.exp(b)
            b_last = b[c - 1:c, :]
            kdl_dec = jnp.exp(b_last - b)
            kdl = kk * kdl_dec
            d_a = jnp.where(causal, _dot_nt(d_o, v), 0.0)
            d_at = _dot_nt(v, d_o)
            dv = _dot_tn(a, d_o) + _dot_nt(kdl, dst)
            dq = eb * _hdot(d_o, st0)
            dk = _hdot(v, dst) * kdl_dec
            d_a_below = jnp.where(below, d_a, 0.0)
            dq_parts = [jnp.zeros((HG_SUB, HG_DK), F32)]
            for i in range(1, nsub):
                lo, hi = i * HG_SUB, (i + 1) * HG_SUB
                dq_parts.append(_hdot(d_a_below[lo:hi, :], kk * edecs[i]))
                gi = _hdot(d_at[:, lo:hi], qh[lo:hi, :])
                dk = dk + jnp.where(row_k < lo, edecs[i] * gi, 0.0)
            dq = dq + jnp.concatenate(dq_parts, axis=0) * jnp.exp(bq)
            dq3 = jnp.zeros((nsub, HG_SUB, HG_DK), F32)
            dk3 = jnp.zeros((nsub, HG_SUB, HG_DK), F32)
            d_diag = jnp.concatenate([d_a[i * HG_SUB:(i + 1) * HG_SUB, i * HG_SUB:(i + 1) * HG_SUB]
                                      for i in range(nsub)], axis=0).reshape(nsub, HG_SUB, HG_SUB)
            for j in range(HG_SUB):
                e = jnp.exp(jnp.minimum(b3 - b3[:, j:j + 1, :], 0.0))
                t1 = d_diag[:, :, j:j + 1] * e
                dq3 = dq3 + t1 * k3[:, j:j + 1, :]
                dk3 = jnp.where(sub_iota == j, jnp.sum(t1 * q3, axis=1, keepdims=True), dk3)
            dq = dq + dq3.reshape(c, HG_DK)
            dk = dk + dk3.reshape(c, HG_DK)
            dstate[...] = dst * jnp.exp(b_last) + _hdot_tn(d_o, qt * eb)
            dglog = _tdot(upper, qt * dq - kk * dk) + carry_ref[...]
            carry_ref[...] = dglog[0:1, :]
            dforget = dglog / forget
            one_m_lb = 1.0 - lbound
            dsig = (dforget - dk) * one_m_lb
            sneg = _sigmoid(-fr)
            dlb = _col_sum(dforget * (1.0 - sig) - dk * sneg)
            dalb0 = dlb * lbound * one_m_lb
            dalb_ref[...] += jnp.concatenate([dalb0, -dalb0], axis=0)
            sq = _sigmoid(qr)
            dq_ref[sl, :] = (dq * (HG_DK ** -0.5) * sq * (1.0 + qr * (1.0 - sq))).astype(dq_ref.dtype)
            df_ref[sl, :] = (dsig * sig * (1.0 - sig)).astype(df_ref.dtype)
            dv_ref[sl, :] = dv.astype(dv_ref.dtype)
            dg_ref[sl, :] = dgr.astype(dg_ref.dtype)
            return carry

        lax.fori_loop(0, cpb, chunk, 0, unroll=2)

    def rev(r):
        return nrb - 1 - r

    def col(cidx):
        return pl.BlockSpec((rb, HG_DK), lambda h, r: (rev(r), cidx * heads + h))

    def head_rows():
        return pl.BlockSpec((rb, HG_DK), lambda h, r: (rev(r), h))

    return pl.pallas_call(
        body, name="hgrn2_bwd", grid=(heads, nrb),
        in_specs=[col(0), col(1), col(2), col(3), head_rows(),
                  pl.BlockSpec((None, cpb, HG_DK, HG_DK), lambda h, r: (h, rev(r), 0, 0)),
                  pl.BlockSpec((None, cpb, HG_CHUNK, HG_CHUNK), lambda h, r: (h, rev(r), 0, 0)),
                  head_rows(),
                  pl.BlockSpec((2, HG_DK), lambda h, r: (0, h)),
                  pl.BlockSpec((1, HG_DK), lambda h, r: (0, 0))],
        out_specs=[head_rows(), head_rows(), head_rows(), head_rows(),
                   pl.BlockSpec((2, HG_DK), lambda h, r: (0, h)),
                   pl.BlockSpec((1, HG_DK), lambda h, r: (0, 0))],
        out_shape=[jax.ShapeDtypeStruct((m, d), BF16)] * 4
                  + [jax.ShapeDtypeStruct((2, d), F32), jax.ShapeDtypeStruct((1, HG_DK), F32)],
        scratch_shapes=[pltpu.VMEM((HG_DK, HG_DK), F32), pltpu.VMEM((1, HG_DK), F32)],
        compiler_params=_params(("arbitrary", "arbitrary")),
    )(proj, proj, proj, proj, o_pre, states, scores, dog, alb, gain)


def _swa_probs(qh, kp, kc, sink, slope, has_prev, lse=None):
    rows = qh.shape[0]
    qi = lax.broadcasted_iota(jnp.int32, (rows, WINDOW), 0) & (WINDOW - 1)
    si = lax.broadcasted_iota(jnp.int32, (rows, WINDOW), 1)
    scale = ATT_HD ** -0.5
    dist_c = (qi - si).astype(F32)
    s_p = _dot_nt(qh, kp) * scale - slope * (dist_c + float(WINDOW))
    s_c = _dot_nt(qh, kc) * scale - slope * dist_c
    s_p = jnp.where((si > qi) & has_prev, s_p, NEG)
    s_c = jnp.where(si <= qi, s_c, NEG)
    if lse is not None:
        return jnp.exp(s_p - lse), jnp.exp(s_c - lse), jnp.exp(sink - lse), lse
    mx = jnp.maximum(jnp.maximum(jnp.max(s_p, axis=-1, keepdims=True), jnp.max(s_c, axis=-1, keepdims=True)), sink)
    e_p, e_c, e_s = jnp.exp(s_p - mx), jnp.exp(s_c - mx), jnp.exp(sink - mx)
    total = jnp.sum(e_p, axis=-1, keepdims=True) + jnp.sum(e_c, axis=-1, keepdims=True) + e_s
    inv = 1.0 / total
    return e_p * inv, e_c * inv, e_s * inv, mx + jnp.log(total)


def _slope(h, n_heads):
    return float(2.0 ** (-8.0 * (h + 1) / n_heads))


def _swa_group(ref_vals, sink_ref, kh, n_heads):
    heads = [kh * ATT_G + g for g in range(ATT_G)]
    stacked = [jnp.concatenate([v[:, h * ATT_HD:(h + 1) * ATT_HD] for h in heads], axis=0) for v in ref_vals]
    grp = lax.shift_right_logical(lax.broadcasted_iota(jnp.int32, (ATT_G * WINDOW, 1), 0), WINDOW.bit_length() - 1)
    slope = jnp.zeros((ATT_G * WINDOW, 1), F32)
    sink = jnp.zeros((ATT_G * WINDOW, 1), F32)
    for g, h in enumerate(heads):
        slope = jnp.where(grp == g, _slope(h, n_heads), slope)
        sink = jnp.where(grp == g, sink_ref[:, h:h + 1], sink)
    return stacked, slope, sink


def _swa_fwd(q, kv, sinks):
    m, d = q.shape
    n_heads = d // ATT_HD
    kvh = n_heads // ATT_G
    kd = kvh * ATT_HD
    nb = m // WINDOW

    def body(q_ref, kvp_ref, kvc_ref, sink_ref, o_ref, lse_ref):
        has_prev = pl.program_id(0) > 0
        qv, kvp, kvc = q_ref[...], kvp_ref[...], kvc_ref[...]
        lane_h = lax.broadcasted_iota(jnp.int32, (WINDOW, n_heads), 1)
        outs, lse_all = [], jnp.zeros((WINDOW, n_heads), F32)
        for kh in range(kvh):
            ks = slice(kh * ATT_HD, (kh + 1) * ATT_HD)
            vs = slice(kd + kh * ATT_HD, kd + (kh + 1) * ATT_HD)
            (q4,), slope, sink = _swa_group([qv], sink_ref, kh, n_heads)
            p_p, p_c, _, lse = _swa_probs(q4, kvp[:, ks], kvc[:, ks], sink, slope, has_prev)
            o4 = _dot(p_p, kvp[:, vs]) + _dot(p_c, kvc[:, vs])
            for g in range(ATT_G):
                rows = slice(g * WINDOW, (g + 1) * WINDOW)
                outs.append(o4[rows, :])
                lse_all = jnp.where(lane_h == kh * ATT_G + g, lse[rows, :], lse_all)
        o_ref[...] = jnp.concatenate(outs, axis=-1).astype(o_ref.dtype)
        lse_ref[...] = lse_all

    return pl.pallas_call(
        body, name="swa_fwd", grid=(nb,),
        in_specs=[pl.BlockSpec((WINDOW, d), lambda n: (n, 0)),
                  pl.BlockSpec((WINDOW, 2 * kd), lambda n: (jnp.maximum(n - 1, 0), 0)),
                  pl.BlockSpec((WINDOW, 2 * kd), lambda n: (n, 0)),
                  pl.BlockSpec((1, n_heads), lambda n: (0, 0))],
        out_specs=[pl.BlockSpec((WINDOW, d), lambda n: (n, 0)), pl.BlockSpec((WINDOW, n_heads), lambda n: (n, 0))],
        out_shape=[jax.ShapeDtypeStruct((m, d), BF16), jax.ShapeDtypeStruct((m, n_heads), F32)],
        compiler_params=_params(("arbitrary",)),
    )(q, kv, kv, sinks)


def _swa_bwd(q, kv, sinks, lse, dao):
    m, d = q.shape
    n_heads = d // ATT_HD
    kvh = n_heads // ATT_G
    kd = kvh * ATT_HD
    nb = m // WINDOW
    scale = ATT_HD ** -0.5

    def body(q_ref, kvp_ref, kvc_ref, sink_ref, lse_ref, do_ref, dq_ref, dkvc_ref, dkvp_ref, dqsum_ref, dsink_ref):
        @pl.when(pl.program_id(0) == 0)
        def _():
            dqsum_ref[...] = jnp.zeros(dqsum_ref.shape, F32)
            dsink_ref[...] = jnp.zeros(dsink_ref.shape, F32)

        has_prev = pl.program_id(0) > 0
        qv, kvp, kvc, dov = q_ref[...], kvp_ref[...], kvc_ref[...], do_ref[...]
        lane_h = lax.broadcasted_iota(jnp.int32, (1, n_heads), 1)
        dsink = jnp.zeros((1, n_heads), F32)
        dq_parts, dk_p, dk_c, dv_p, dv_c = [], [], [], [], []
        for kh in range(kvh):
            ks = slice(kh * ATT_HD, (kh + 1) * ATT_HD)
            vs = slice(kd + kh * ATT_HD, kd + (kh + 1) * ATT_HD)
            kp, kc, vp, vc = kvp[:, ks], kvc[:, ks], kvp[:, vs], kvc[:, vs]
            (q4, do4), slope, sink = _swa_group([qv, dov], sink_ref, kh, n_heads)
            lse4 = jnp.concatenate([lse_ref[:, kh * ATT_G + g:kh * ATT_G + g + 1] for g in range(ATT_G)], axis=0)
            p_p, p_c, p_s, _ = _swa_probs(q4, kp, kc, sink, slope, has_prev, lse=lse4)
            dp_p, dp_c = _dot_nt(do4, vp), _dot_nt(do4, vc)
            delta = jnp.sum(p_p * dp_p, axis=-1, keepdims=True) + jnp.sum(p_c * dp_c, axis=-1, keepdims=True)
            ds_p, ds_c = p_p * (dp_p - delta), p_c * (dp_c - delta)
            sink_term = p_s * delta
            dq4 = (_dot(ds_p, kp) + _dot(ds_c, kc)) * scale
            for g in range(ATT_G):
                rows = slice(g * WINDOW, (g + 1) * WINDOW)
                dsink = dsink + jnp.where(lane_h == kh * ATT_G + g, -_col_sum(sink_term[rows, :]), 0.0)
                dq_parts.append(dq4[rows, :])
            dk_p.append(_dot_tn(ds_p, q4) * scale)
            dk_c.append(_dot_tn(ds_c, q4) * scale)
            dv_p.append(_dot_tn(p_p, do4))
            dv_c.append(_dot_tn(p_c, do4))
        dq = jnp.concatenate(dq_parts, axis=-1)
        dq_ref[...] = dq.astype(dq_ref.dtype)
        dqsum_ref[...] += _col_sum(dq)
        dsink_ref[...] += dsink
        dkvc_ref[...] = jnp.concatenate(dk_c + dv_c, axis=-1)
        dkvp_ref[...] = jnp.concatenate(dk_p + dv_p, axis=-1)

    return pl.pallas_call(
        body, name="swa_bwd", grid=(nb,),
        in_specs=[pl.BlockSpec((WINDOW, d), lambda n: (n, 0)),
                  pl.BlockSpec((WINDOW, 2 * kd), lambda n: (jnp.maximum(n - 1, 0), 0)),
                  pl.BlockSpec((WINDOW, 2 * kd), lambda n: (n, 0)),
                  pl.BlockSpec((1, n_heads), lambda n: (0, 0)),
                  pl.BlockSpec((WINDOW, n_heads), lambda n: (n, 0)),
                  pl.BlockSpec((WINDOW, d), lambda n: (n, 0))],
        out_specs=[pl.BlockSpec((WINDOW, d), lambda n: (n, 0)),
                   pl.BlockSpec((WINDOW, 2 * kd), lambda n: (n, 0)),
                   pl.BlockSpec((WINDOW, 2 * kd), lambda n: (n, 0)),
                   pl.BlockSpec((1, d), lambda n: (0, 0)),
                   pl.BlockSpec((1, n_heads), lambda n: (0, 0))],
        out_shape=[jax.ShapeDtypeStruct((m, d), BF16), jax.ShapeDtypeStruct((m, 2 * kd), F32),
                   jax.ShapeDtypeStruct((m, 2 * kd), F32), jax.ShapeDtypeStruct((1, d), F32),
                   jax.ShapeDtypeStruct((1, n_heads), F32)],
        compiler_params=_params(("arbitrary",)),
    )(q, kv, kv, sinks, lse, dao)


def _kv_grad_combine(dkv_cur, dkv_prev):
    m, w = dkv_cur.shape
    nb = m // WINDOW

    def body(cur_ref, nxt_ref, o_ref, sum_ref):
        @pl.when(pl.program_id(0) == 0)
        def _():
            sum_ref[...] = jnp.zeros(sum_ref.shape, F32)

        total = cur_ref[...] + jnp.where(pl.program_id(0) < nb - 1, nxt_ref[...], 0.0)
        o_ref[...] = total.astype(o_ref.dtype)
        sum_ref[...] += _col_sum(total)

    return pl.pallas_call(
        body, name="kv_grad_combine", grid=(nb,),
        in_specs=[pl.BlockSpec((WINDOW, w), lambda n: (n, 0)),
                  pl.BlockSpec((WINDOW, w), lambda n: (jnp.minimum(n + 1, nb - 1), 0))],
        out_specs=[pl.BlockSpec((WINDOW, w), lambda n: (n, 0)), pl.BlockSpec((1, w), lambda n: (0, 0))],
        out_shape=[jax.ShapeDtypeStruct((m, w), BF16), jax.ShapeDtypeStruct((1, w), F32)],
        compiler_params=_params(("arbitrary",)),
    )(dkv_cur, dkv_prev)


def _row(v):
    return v.reshape(1, -1)


def _local_step(x, p, target, wget, grad_sink, ln_gain, ln_bias, alb, norm_gain, kv_b, b_q, sinks, b_out, ple_b,
                small_sink=None):
    gs = {}
    gains = ln_gain.reshape(DEPTH * 3, -1)
    biases = ln_bias.reshape(DEPTH * 3, -1)
    sd = x.shape
    pending = [None]

    def mm(a, b, lb=0, **kw):
        after, pending[0] = pending[0], None
        return _mm(a, b, lb=lb, after=after, **kw)

    def mm_ln(a, wt, xin, i, j, nm, bias=None, pu=None):
        r = 3 * i + j
        if pu is None:
            fn, rows = (lambda h, xv, g, bv: (h,) + _ln_fwd_fn(xv, h, g[r:r + 1], bv[r:r + 1])), [xin]
        else:
            fn = lambda h, xv, puv, g, bv: (h,) + _ple_ln_fwd_fn(xv, h, puv, g[r:r + 1], bv[r:r + 1])
            rows = [xin, pu]
        h, y, yb = _mm(a, wt, lb=0, bias=bias, name=nm,
                       post=(fn, rows, [gains, biases], [(sd, F32), (sd, F32), (sd, BF16)], []))
        return h, (y, yb)

    def mm_ln_bwd(a, wt, add, xin, h, i, j, nm):
        r = 3 * i + j
        dx_part, dh, dg, db, dhsum = mm(a, wt, tb=True, add=add, name=nm,
                                        post=(lambda dy, xv, hv, g: _ln_bwd_fn(dy, xv, hv, g[r:r + 1]), [xin, h],
                                              [gains], [(sd, F32), (sd, BF16)], [((1, sd[1]), F32)] * 3))
        gs[f"ln_gain_{i}_{j}"], gs[f"ln_bias_{i}_{j}"] = dg, db
        return dx_part, dh, dhsum

    def tail_fwd(xa, i):
        wgu = wget("ffn_w_gate_up", i, xa[1])
        hid2 = wgu.shape[-1]
        gu, act = _mm(xa[1], wgu, lb=0, name=f"ffn_up_swiglu{i}", tile_cols=hid2 // 2,
                      post=(_swiglu_fwd_fn, [], [], [((sd[0], hid2), BF16), ((sd[0], hid2 // 2), BF16)], []))
        f, xb = mm_ln(act, wget("ffn_w_down", i, act), xa[0], i, 1, f"ffn_down_ln{i}")
        pu = _mm(p, wget("ple_w_up", i, act), la=i, lb=0, name=f"ple_up{i}")
        pg, xc = mm_ln(xb[1], wget("ple_w_gate", i, act), xb[0], i, 2, f"ple_gate_ln{i}", bias=_row(ple_b[i]), pu=pu)
        return dict(xa=xa, gu=gu, act=act, f=f, xb=xb, pg=pg, pu=pu), xc

    def tail_bwd(head, sv, i, mix_in, mix_h):
        xa, xb = sv["xa"], sv["xb"]
        r = 3 * i + 2
        dxb_part, dpg, dpu, dg2, db2, dbg = head(
            lambda dy, xv, pgv, puv, g: _ple_ln_bwd_fn(dy, xv, pgv, puv, g[r:r + 1]), [xb[0], sv["pg"], sv["pu"]],
            [gains], [(sd, F32), (sd, BF16), (sd, BF16)], [((1, sd[1]), F32)] * 3)[:6]
        gs[f"ple_b_{i}"] = dbg
        gs[f"ln_gain_{i}_2"], gs[f"ln_bias_{i}_2"] = dg2, db2
        grad_of("ple_w_gate", i, xb[1], dpg)
        grad_of("ple_w_up", i, p, dpu, la=i)
        dxa_part, df, _ = mm_ln_bwd(dpg, wget("ple_w_gate", i, None), dxb_part, xa[0], sv["f"], i, 1,
                                    f"ple_gate_dx_ln{i}")
        grad_of("ffn_w_down", i, sv["act"], df)
        gu = sv["gu"]
        dgu, = mm(df, wget("ffn_w_down", i, None), tb=True, name=f"ffn_down_dx_swiglu{i}", tile_cols=gu.shape[1] // 4,
                  post=(_swiglu_bwd_fn, [gu], [], [(gu.shape, BF16)], []))
        grad_of("ffn_w_gate_up", i, xa[1], dgu)
        return mm_ln_bwd(dgu, wget("ffn_w_gate_up", i, None), dxa_part, mix_in, mix_h, i, 0, f"ffn_up_dx_ln{i}")

    def grad_of(nm, i, act, dout, la=None):
        grad = mm(act, dout, la=la, lb=None, ta=True, out_dtype=BF16, out_layers=1, out_layer=0,
                  name=f"grad_{nm}{i}")
        token = grad_sink(nm, i, grad)
        if token is not None:
            pending[0] = token

    proj = _mm(x, wget("a_w_in", 0, None), lb=0, name="hg_proj")
    o_pre, og, states, scores = _hgrn2_fwd(proj, alb, norm_gain, rb=HG_ROWS)
    h0, x1 = mm_ln(og, wget("a_w_out", 0, og), x, 0, 0, "hg_out_ln")
    sv0, x3 = tail_fwd(x1, 0)
    kv = _mm(x3[1], wget("kv_w", 0, x3[1]), lb=0, bias=_row(kv_b), out_dtype=BF16, name="kv_proj")
    q = _mm(x3[1], wget("b_w_q", 0, x3[1]), lb=0, bias=b_q, out_dtype=BF16, name="q_proj")
    ao, lse = _swa_fwd(q, kv, sinks)
    h1, x4 = mm_ln(ao, wget("b_w_out", 0, x3[1]), x3[0], 1, 0, "att_out_ln", bias=b_out)
    sv1, y = tail_fwd(x4, 1)

    loss_box = []

    def loss_head(fn, rows, whole, outs, sums):
        def with_loss(yv, tv, *rest):
            dy, part = _loss_fn(yv, tv)
            return fn(dy, *rest) + (part,)

        res = _rowwise(with_loss, [y[0], target] + rows, whole, outs, list(sums) + [((1, LANES), F32)],
                       name="loss_ln_ple_bwd1")
        loss_box.append(res[-1])
        return res

    dx3_part, dh1, dh1sum = tail_bwd(loss_head, sv1, 1, x3[0], h1)
    loss = loss_box[0]
    gs["b_out"] = dh1sum
    grad_of("b_w_out", 0, ao, dh1)
    dao = mm(dh1, wget("b_w_out", 0, None), tb=True, out_dtype=BF16, name="att_out_dx")
    dq, dkv_cur, dkv_prev, dqsum, dsinks = _swa_bwd(q, kv, sinks, lse, dao)
    gs["b_q"], gs["sinks"] = dqsum, dsinks
    dkv, dkvsum = _kv_grad_combine(dkv_cur, dkv_prev)
    gs["kv_b"] = dkvsum
    grad_of("b_w_q", 0, x3[1], dq)
    grad_of("kv_w", 0, x3[1], dkv)
    dx3 = mm(dq, wget("b_w_q", 0, None), tb=True, add=dx3_part, name="q_proj_dx")

    def kv_head(*post):
        return mm(dkv, wget("kv_w", 0, None), tb=True, add=dx3, name="kv_proj_dx_ln_ple_bwd0", post=post)

    dx_part, dh0, _ = tail_bwd(kv_head, sv0, 0, x, h0)
    grad_of("a_w_out", 0, og, dh0)
    dog = mm(dh0, wget("a_w_out", 0, None), tb=True, name="hg_out_dx")
    dqr, dfr, dvr, dgr, dalb, dgain = _hgrn2_bwd(proj, o_pre, states, scores, dog, alb, norm_gain, rb=HG_ROWS)
    gs["alb"], gs["norm_gain"] = dalb, dgain
    if small_sink is not None:
        pending[0] = small_sink(loss, gs)
    dproj = jnp.concatenate([dqr, dfr, dvr, dgr], axis=1)
    grad_of("a_w_in", 0, x, dproj)
    grad_x = mm(dproj, wget("a_w_in", 0, None), tb=True, add=dx_part, name="hg_proj_dx")
    return loss, grad_x, gs


HBM_SPEC = pl.BlockSpec(memory_space=pl.ANY)
HBM_ONLY = pl.BlockSpec(memory_space=pltpu.HBM)
SEM_SPEC = pl.BlockSpec(memory_space=pltpu.SEMAPHORE)
SIDE_EFFECT = pltpu.SideEffectType.DATAFLOW_SIDE_EFFECTING


def _slot(kind, j):
    return (j % 2) * 2 + j // 2 if kind == "colp" else j


def _piece(ref, kind, j):
    _, r, c = ref.shape
    if kind == "row":
        return ref.at[:, pl.ds(j * (r // N_CHIPS), r // N_CHIPS), :]
    return ref.at[:, :, pl.ds(_slot(kind, j) * (c // N_CHIPS), c // N_CHIPS)]


def _piece_dyn(ref, kind, j):
    _, r, c = ref.shape
    if kind == "row":
        return ref.at[:, pl.ds(pl.multiple_of(j * (r // N_CHIPS), 16), r // N_CHIPS), :]
    return ref.at[:, :, pl.ds(pl.multiple_of(_slot(kind, j) * (c // N_CHIPS), LANES), c // N_CHIPS)]


def _chip_of(j, c):
    return (j // 2, j % 2, c)


def _in_hbm(a):
    return pltpu.with_memory_space_constraint(a, pltpu.HBM)


PLACE_STEPS = 4


def _place(items, chip, *, name, after=None):
    n = len(items)
    in_specs, out_specs, out_shapes, blocks = [], [], [], []
    for src, layer, kind, out_dtype in items:
        _, r, c = src.shape
        nb = max(k for k in (1, 2, PLACE_STEPS) if r % (16 * k) == 0 or k == 1)
        blocks.append(nb)

        def src_idx(i, chip_ref, layer=layer, nb=nb):
            return (layer, jnp.minimum(i, nb - 1), 0)

        def full_idx(i, chip_ref, kind=kind, nb=nb):
            ib = jnp.minimum(i, nb - 1)
            return (0, chip_ref[0] * nb + ib, 0) if kind == "row" else (0, ib, _slot(kind, chip_ref[0]))

        in_specs.append(pl.BlockSpec((None, r // nb, c), src_idx))
        out_specs.append(pl.BlockSpec((None, r // nb, c), full_idx))
        out_shapes.append(jax.ShapeDtypeStruct((1, r * N_CHIPS, c) if kind == "row" else (1, r, c * N_CHIPS),
                                               out_dtype))
    operands = [it[0] for it in items]
    if after is not None:
        in_specs.append(HBM_SPEC)
        operands.append(after)

    def body(chip_ref, *refs):
        for a in range(n):
            refs[len(refs) - n + a][...] = refs[a][...].astype(refs[len(refs) - n + a].dtype)

    return pl.pallas_call(
        body, name=name,
        grid_spec=pltpu.PrefetchScalarGridSpec(num_scalar_prefetch=1, grid=(PLACE_STEPS,), in_specs=in_specs,
                                               out_specs=out_specs),
        out_shape=out_shapes,
        compiler_params=_params(("arbitrary",)),
    )(chip, *operands)


def _half(ref, c):
    h = ref.shape[1] // 2
    start = c * h if isinstance(c, int) else pl.multiple_of(c * h, 16)
    return ref.at[:, pl.ds(start, h), :]


class _SiblingFill:
    def __init__(self, lands, kinds, name):
        self.kinds, self.name, self.n = kinds, name, len(lands)
        n = self.n
        sem_shape = pltpu.SemaphoreType.DMA((n * N_CHIPS,))

        def body(*refs):
            land_refs, send_sems, recv_sems, token = refs[:n], refs[n], refs[n + 1], refs[-1]
            for cp in self._copies(land_refs, send_sems, recv_sems):
                cp.start()
            token[...] = jnp.zeros(token.shape, token.dtype)

        outs = pl.pallas_call(
            body, name=name + "_start",
            in_specs=[HBM_ONLY] * n,
            out_specs=[SEM_SPEC, SEM_SPEC] + [HBM_ONLY] * n + [pl.BlockSpec(memory_space=pltpu.VMEM)],
            out_shape=[sem_shape, sem_shape] + [pltpu.HBM(a.shape, a.dtype) for a in lands]
                      + [jax.ShapeDtypeStruct((8, LANES), F32)],
            input_output_aliases={i: i + 2 for i in range(n)},
            compiler_params=pltpu.CompilerParams(has_side_effects=SIDE_EFFECT),
        )(*[_in_hbm(a) for a in lands])
        self.send_sems, self.recv_sems, self.lands, self.token = outs[0], outs[1], list(outs[2:2 + n]), outs[-1]

    def _copies(self, land_refs, send_sems, recv_sems):
        x, y, c = lax.axis_index("x"), lax.axis_index("y"), lax.axis_index("c")
        me = 2 * x + y
        copies = []
        for a in range(self.n):
            for k in range(1, N_CHIPS):
                t = (me + k) % N_CHIPS
                slice_t = _piece_dyn(land_refs[a], self.kinds[a], t)
                got = _half(slice_t, c)
                copies.append(pltpu.make_async_remote_copy(
                    src_ref=got, dst_ref=got, send_sem=send_sems.at[a * N_CHIPS + k],
                    recv_sem=recv_sems.at[a * N_CHIPS + k], device_id=(x, y, 1 - c), device_id_type=MESH))
        return copies

    def wait(self, after):
        n = self.n

        def body(*refs):
            land_refs, send_sems, recv_sems = refs[:n], refs[n], refs[n + 1]
            for cp in self._copies(land_refs, send_sems, recv_sems):
                cp.wait_send()
                cp.wait_recv()

        operands = [_in_hbm(a) for a in self.lands] + [self.send_sems, self.recv_sems]
        in_specs = [HBM_ONLY] * n + [SEM_SPEC, SEM_SPEC]
        if after is not None:
            operands.append(after)
            in_specs.append(HBM_SPEC)
        outs = pl.pallas_call(
            body, name=self.name + "_wait",
            in_specs=in_specs, out_specs=[HBM_ONLY] * n,
            out_shape=[pltpu.HBM(a.shape, a.dtype) for a in self.lands],
            input_output_aliases={i: i for i in range(n)},
            compiler_params=pltpu.CompilerParams(has_side_effects=SIDE_EFFECT),
        )(*operands)
        return list(outs)


class _Exchange:
    def __init__(self, mode, srcs, lands, kinds, layers, name, after=None, halves=None):
        self.mode, self.kinds, self.layers, self.name, self.n = mode, kinds, layers, name, len(lands)
        self.halves = halves if halves is not None else [False] * len(lands)
        n, ns = self.n, len(srcs)
        n_in = ns + n + (after is not None)
        sem_shape = pltpu.SemaphoreType.DMA((n * N_CHIPS,))

        def body(*refs):
            src_refs, land_refs = refs[:ns], refs[ns:ns + n]
            send_sems, recv_sems = refs[n_in], refs[n_in + 1]
            token = refs[-1]
            c = lax.axis_index("c")
            me = 2 * lax.axis_index("x") + lax.axis_index("y")
            for j in range(N_CHIPS):
                @pl.when(me == j)
                def _():
                    for a in range(n):
                        for t in range(N_CHIPS):
                            if t != j:
                                src, dst = self._ends(src_refs, land_refs, a, j, t, c)
                                pltpu.make_async_remote_copy(
                                    src_ref=src, dst_ref=dst, send_sem=send_sems.at[a * N_CHIPS + t],
                                    recv_sem=recv_sems.at[a * N_CHIPS + j],
                                    device_id=_chip_of(t, c), device_id_type=MESH).start()
            token[...] = jnp.zeros(token.shape, token.dtype)

        arrays = list(srcs) + list(lands)
        operands = [_in_hbm(a) for a in arrays]
        in_specs = [HBM_ONLY] * (ns + n)
        if after is not None:
            operands.append(after)
            in_specs.append(HBM_SPEC)
        outs = pl.pallas_call(
            body, name=name + "_start",
            in_specs=in_specs,
            out_specs=[SEM_SPEC, SEM_SPEC] + [HBM_ONLY] * (ns + n) + [pl.BlockSpec(memory_space=pltpu.VMEM)],
            out_shape=[sem_shape, sem_shape] + [pltpu.HBM(a.shape, a.dtype) for a in arrays]
                      + [jax.ShapeDtypeStruct((8, LANES), F32)],
            input_output_aliases={i: i + 2 for i in range(ns + n)},
            compiler_params=pltpu.CompilerParams(has_side_effects=SIDE_EFFECT),
        )(*operands)
        self.send_sems, self.recv_sems = outs[0], outs[1]
        self.srcs, self.lands = list(outs[2:2 + ns]), list(outs[2 + ns:2 + ns + n])
        self.token = outs[-1]

    def _ends(self, src_refs, land_refs, a, me_j, peer, c):
        if self.mode == "gather":
            mine = _piece(land_refs[a], self.kinds[a], me_j)
            if self.halves[a]:
                mine = _half(mine, c)
            return mine, mine
        return _piece(src_refs[a], self.kinds[a], peer), land_refs[a].at[me_j, pl.ds(self.layers[a], 1)]

    def wait(self, after, lands=None):
        n, ns = self.n, len(self.srcs)
        lands = self.lands if lands is None else lands

        def body(*refs):
            src_refs, land_refs = refs[:ns], refs[ns:ns + n]
            send_sems, recv_sems = refs[ns + n], refs[ns + n + 1]
            c = lax.axis_index("c")
            me = 2 * lax.axis_index("x") + lax.axis_index("y")
            for j in range(N_CHIPS):
                @pl.when(me != j)
                def _():
                    for a in range(n):
                        sent, _ = self._ends(src_refs, land_refs, a, 0, j, c)
                        _, landed = self._ends(src_refs, land_refs, a, j, 0, c)
                        cp = pltpu.make_async_remote_copy(
                            src_ref=sent, dst_ref=landed, send_sem=send_sems.at[a * N_CHIPS + j],
                            recv_sem=recv_sems.at[a * N_CHIPS + j],
                            device_id=_chip_of(j, c), device_id_type=MESH)
                        cp.wait_send()
                        cp.wait_recv()

        arrays = self.srcs + list(lands)
        operands = [_in_hbm(a) for a in arrays] + [self.send_sems, self.recv_sems]
        in_specs = [HBM_ONLY] * (ns + n) + [SEM_SPEC, SEM_SPEC]
        if after is not None:
            operands.append(after)
            in_specs.append(HBM_SPEC)
        outs = pl.pallas_call(
            body, name=self.name + "_wait",
            in_specs=in_specs, out_specs=[HBM_ONLY] * (ns + n),
            out_shape=[pltpu.HBM(a.shape, a.dtype) for a in arrays],
            input_output_aliases={i: i for i in range(ns + n)},
            compiler_params=pltpu.CompilerParams(has_side_effects=SIDE_EFFECT),
        )(*operands)
        return list(outs[:ns]), list(outs[ns:])


def _sum_arrivals(zone, own_grads, kind, chip, name, after=None):
    _, layers, r, c = zone.shape
    tm = _pick_rows(r, 256)
    nb = r // tm

    def own_idx(l, i, chip_ref):
        return (0, chip_ref[0] * nb + i, 0) if kind == "row" else (0, i, _slot(kind, chip_ref[0]))

    def slot_idx(k):
        return lambda l, i, chip_ref: (jnp.where(chip_ref[0] == k, (k + 1) % N_CHIPS, k), l, i, 0)

    in_specs = [pl.BlockSpec((None, None, tm, c), slot_idx(k)) for k in range(N_CHIPS)]
    in_specs += [pl.BlockSpec((None, tm, c), own_idx) for _ in own_grads]
    operands = [zone] * N_CHIPS + list(own_grads)
    if after is not None:
        in_specs.append(HBM_SPEC)
        operands.append(after)

    def body(chip_ref, *refs):
        slot_refs, own_refs, o_ref = refs[:N_CHIPS], refs[N_CHIPS:N_CHIPS + layers], refs[-1]
        own = own_refs[0][...]
        for u in range(1, layers):
            own = jnp.where(pl.program_id(0) == u, own_refs[u][...], own)
        acc = None
        for k in range(N_CHIPS):
            term = jnp.where(chip_ref[0] == k, own, slot_refs[k][...]).astype(F32)
            acc = term if acc is None else acc + term
        o_ref[...] = acc.astype(o_ref.dtype)

    return pl.pallas_call(
        body, name=name,
        grid_spec=pltpu.PrefetchScalarGridSpec(
            num_scalar_prefetch=1, grid=(layers, nb), in_specs=in_specs,
            out_specs=pl.BlockSpec((tm, c), lambda l, i, chip_ref: (l * nb + i, 0))),
        out_shape=jax.ShapeDtypeStruct((layers * r, c), BF16),
        compiler_params=_params(("arbitrary", "arbitrary")),
    )(chip, *operands)


class _SiblingSwap:
    def __init__(self, arrays, name, after=None):
        self.name, self.n = name, len(arrays)
        n = self.n
        n_in = n + (after is not None)
        sem_shape = pltpu.SemaphoreType.DMA((n,))

        def body(*refs):
            ins, send_sems, recv_sems = refs[:n], refs[n_in], refs[n_in + 1]
            theirs, token = refs[n_in + 2 + n:n_in + 2 + 2 * n], refs[-1]
            for cp in self._copies(ins, theirs, send_sems, recv_sems):
                cp.start()
            token[...] = jnp.zeros(token.shape, token.dtype)

        operands, in_specs = [_in_hbm(a) for a in arrays], [HBM_ONLY] * n
        if after is not None:
            operands.append(after)
            in_specs.append(HBM_SPEC)
        outs = pl.pallas_call(
            body, name=name + "_start",
            in_specs=in_specs,
            out_specs=[SEM_SPEC, SEM_SPEC] + [HBM_ONLY] * (2 * n) + [pl.BlockSpec(memory_space=pltpu.VMEM)],
            out_shape=[sem_shape, sem_shape] + [pltpu.HBM(a.shape, a.dtype) for a in arrays] * 2
                      + [jax.ShapeDtypeStruct((8, LANES), F32)],
            input_output_aliases={i: i + 2 for i in range(n)},
            compiler_params=pltpu.CompilerParams(has_side_effects=SIDE_EFFECT),
        )(*operands)
        self.send_sems, self.recv_sems = outs[0], outs[1]
        self.mine, self.theirs, self.token = list(outs[2:2 + n]), list(outs[2 + n:2 + 2 * n]), outs[-1]

    def _copies(self, mine, theirs, send_sems, recv_sems):
        sibling = (lax.axis_index("x"), lax.axis_index("y"), 1 - lax.axis_index("c"))
        return [pltpu.make_async_remote_copy(src_ref=mine[a], dst_ref=theirs[a], send_sem=send_sems.at[a],
                                             recv_sem=recv_sems.at[a], device_id=sibling, device_id_type=MESH)
                for a in range(self.n)]

    def wait(self, after):
        n = self.n

        def body(*refs):
            for cp in self._copies(refs[:n], refs[n:2 * n], refs[2 * n], refs[2 * n + 1]):
                cp.wait_send()
                cp.wait_recv()

        arrays = self.mine + self.theirs
        outs = pl.pallas_call(
            body, name=self.name + "_wait",
            in_specs=[HBM_ONLY] * (2 * n) + [SEM_SPEC, SEM_SPEC, HBM_SPEC], out_specs=[HBM_ONLY] * (2 * n),
            out_shape=[pltpu.HBM(a.shape, a.dtype) for a in arrays],
            input_output_aliases={i: i for i in range(2 * n)},
            compiler_params=pltpu.CompilerParams(has_side_effects=SIDE_EFFECT),
        )(*[_in_hbm(a) for a in arrays], self.send_sems, self.recv_sems, after)
        return list(outs[:n]), list(outs[n:])


class _GatherDevices:
    def __init__(self, vec):
        sem_shape = pltpu.SemaphoreType.DMA((N_DEV,))

        def body(in_ref, send_sems, recv_sems, vec_ref, out_ref, token):
            for cp in self._copies(in_ref, out_ref, send_sems, recv_sems):
                cp.start()
            token[...] = jnp.zeros(token.shape, token.dtype)

        outs = pl.pallas_call(
            body, name="gather_small_start",
            in_specs=[HBM_ONLY],
            out_specs=[SEM_SPEC, SEM_SPEC, HBM_ONLY, HBM_ONLY, pl.BlockSpec(memory_space=pltpu.VMEM)],
            out_shape=[sem_shape, sem_shape, pltpu.HBM(vec.shape, vec.dtype),
                       pltpu.HBM((N_DEV,) + vec.shape, vec.dtype), jax.ShapeDtypeStruct((8, LANES), F32)],
            input_output_aliases={0: 2},
            compiler_params=pltpu.CompilerParams(has_side_effects=SIDE_EFFECT),
        )(_in_hbm(vec))
        self.send_sems, self.recv_sems, self.vec, self.rows, self.token = outs

    def _copies(self, in_ref, out_ref, send_sems, recv_sems):
        x, y, c = lax.axis_index("x"), lax.axis_index("y"), lax.axis_index("c")
        me = 4 * x + 2 * y + c
        copies = [pltpu.make_async_copy(in_ref, out_ref.at[me], recv_sems.at[0])]
        for rel in range(1, N_DEV):
            peer = (x ^ (rel >> 2), y ^ ((rel >> 1) & 1), c ^ (rel & 1))
            copies.append(pltpu.make_async_remote_copy(
                src_ref=in_ref, dst_ref=out_ref.at[me], send_sem=send_sems.at[rel], recv_sem=recv_sems.at[rel],
                device_id=peer, device_id_type=MESH))
        return copies

    def wait(self, after):
        def body(vec_ref, rows_ref, send_sems, recv_sems, after_ref, vec_out, rows_out):
            copies = self._copies(vec_ref, rows_ref, send_sems, recv_sems)
            copies[0].wait()
            for cp in copies[1:]:
                cp.wait_send()
                cp.wait_recv()

        outs = pl.pallas_call(
            body, name="gather_small_wait",
            in_specs=[HBM_ONLY, HBM_ONLY, SEM_SPEC, SEM_SPEC, HBM_SPEC], out_specs=[HBM_ONLY, HBM_ONLY],
            out_shape=[pltpu.HBM(self.vec.shape, self.vec.dtype), pltpu.HBM(self.rows.shape, self.rows.dtype)],
            input_output_aliases={0: 0, 1: 1},
            compiler_params=pltpu.CompilerParams(has_side_effects=SIDE_EFFECT),
        )(_in_hbm(self.vec), _in_hbm(self.rows), self.send_sems, self.recv_sems, after)
        return outs[1]


BIG = [("a_w_in", "col"), ("a_w_out", "row"), ("kv_w", "row"), ("b_w_q", "row"), ("b_w_out", "row"),
       ("ffn_w_gate_up", "colp"), ("ffn_w_down", "row"), ("ple_w_up", "col"), ("ple_w_gate", "row")]
GATHER_GROUPS = [[("a_w_in", 0), ("small", 0)], [("a_w_out", 0), ("ffn_w_gate_up", 0)],
                 [("ffn_w_down", 0), ("ple_w_gate", 0), ("ple_w_up", 0)], [("kv_w", 0), ("b_w_q", 0), ("b_w_out", 0)],
                 [("ffn_w_gate_up", 1)], [("ffn_w_down", 1), ("ple_w_gate", 1), ("ple_w_up", 1)]]
SCATTER_GROUPS = [[("ple_w_gate", 1), ("ple_w_up", 1), ("ffn_w_down", 1)], [("ffn_w_gate_up", 1)],
                  [("b_w_out", 0), ("b_w_q", 0), ("kv_w", 0)], [("ple_w_gate", 0), ("ple_w_up", 0), ("ffn_w_down", 0)],
                  [("ffn_w_gate_up", 0), ("a_w_out", 0)], [("a_w_in", 0)]]
SMALL_SHARDED = ["ln_gain", "ln_bias", "a_lower_bound"]
SMALL_REPLICATED = ["a_norm_gain", "kv_b", "b_b_q", "b_sinks", "b_b_out", "ple_b_gate"]
WEIGHT_ORDER = ["a_w_in", "a_lower_bound", "a_norm_gain", "a_w_out", "kv_w", "kv_b", "b_w_q", "b_b_q", "b_sinks",
                "b_w_out", "b_b_out", "ffn_w_gate_up", "ffn_w_down", "ple_w_up", "ple_w_gate", "ple_b_gate",
                "ln_gain", "ln_bias"]


def _as3(a):
    return a.reshape((-1,) + a.shape[-2:]) if a.ndim >= 3 else a.reshape((1,) + a.shape)


def _pad_lanes(v):
    n = v.shape[-1]
    return jnp.pad(v, ((0, 0), (0, (-n) % LANES)))


def _adam_small_fn(w, mom, vel, g):
    return _adam_fn(w, mom, vel, g, jnp.zeros_like(g))[1:]


def _sum_rows_fn(slots):
    acc = slots[0]
    for s in range(1, slots.shape[0]):
        acc = acc + slots[s]
    return (acc,)


def kernel(x, p, a_w_in, a_lower_bound, a_norm_gain, a_w_out, kv_w, kv_b, b_w_q, b_b_q, b_sinks, b_w_out, b_b_out, ffn_w_gate_up, ffn_w_down, ple_w_up, ple_w_gate, ple_b_gate, ln_gain, ln_bias, loss_target, m_a_w_in, m_a_lower_bound, m_a_norm_gain, m_a_w_out, m_kv_w, m_kv_b, m_b_w_q, m_b_b_q, m_b_sinks, m_b_w_out, m_b_b_out, m_ffn_w_gate_up, m_ffn_w_down, m_ple_w_up, m_ple_w_gate, m_ple_b_gate, m_ln_gain, m_ln_bias, v_a_w_in, v_a_lower_bound, v_a_norm_gain, v_a_w_out, v_kv_w, v_kv_b, v_b_w_q, v_b_b_q, v_b_sinks, v_b_w_out, v_b_b_out, v_ffn_w_gate_up, v_ffn_w_down, v_ple_w_up, v_ple_w_gate, v_ple_b_gate, v_ln_gain, v_ln_bias):
    args = dict(locals())
    wts = {n: args[n] for n in WEIGHT_ORDER}
    mom = {n: args["m_" + n] for n in WEIGHT_ORDER}
    vel = {n: args["v_" + n] for n in WEIGHT_ORDER}
    chip = 2 * lax.axis_index("x") + lax.axis_index("y")
    d = x.shape[-1]
    dq = d // N_CHIPS

    kind_of = dict(BIG)
    kind_of["small"] = "col"
    chip_arr = chip.reshape(1).astype(jnp.int32)
    small_pack = jnp.concatenate([wts[n].reshape(-1, dq) for n in SMALL_SHARDED], axis=0)[None]

    def place_item(key):
        n, layer = key
        if n == "small":
            return small_pack, 0, "col", F32
        return _as3(wts[n]), layer, kind_of[n], BF16

    gathers, where = [], {}
    for gi, group in enumerate(GATHER_GROUPS):
        prev = gathers[-1].token if gathers else None
        placed = _place([place_item(k) for k in group], chip_arr, name=f"place{gi}", after=prev)
        gathers.append(_Exchange("gather", [], placed, [kind_of[k[0]] for k in group],
                                 [0] * len(group), f"gather{gi}", after=prev,
                                 halves=[k[0] != "small" for k in group]))
        for k in group:
            where[k] = gi
    all_started = gathers[-1].token
    ready = {}

    fills = {}

    def pass_on(gi, after):
        if gi not in fills:
            group = GATHER_GROUPS[gi]
            outs = gathers[gi].wait(after)[1]
            split = [i for i, k in enumerate(group) if k[0] != "small"]
            fills[gi] = (outs, split, _SiblingFill([outs[i] for i in split], [kind_of[group[i][0]] for i in split],
                                                   f"fill{gi}"))

    def wget(name, layer, after):
        key = (name, layer)
        if key not in ready:
            gi = where[key]
            after = all_started if gi == 0 else after
            pass_on(gi, after)
            if 1 <= gi < len(GATHER_GROUPS) - 1:
                pass_on(gi + 1, after)
                after = fills[gi + 1][2].token
            outs, split, fill = fills[gi]
            for i, arr in zip(split, fill.wait(after)):
                outs[i] = arr
            for k, arr in zip(GATHER_GROUPS[gi], outs):
                ready[k] = arr
        return ready[key]

    small_full = wget("small", 0, None)[0]
    ln_gain_f = small_full[0:6].reshape(DEPTH, 3, d)
    ln_bias_f = small_full[6:12].reshape(DEPTH, 3, d)
    alb_f = small_full[12:14]

    group_of = {k: gi for gi, group in enumerate(SCATTER_GROUPS) for k in group}
    grads_done, zones, scatters = {}, {}, []

    def grad_sink(name, layer, grad):
        grads_done[(name, layer)] = grad
        if name not in zones:
            zones[name] = lax.empty((N_CHIPS,) + _as3(wts[name]).shape, BF16)
        gi = group_of[(name, layer)]
        group = SCATTER_GROUPS[gi]
        if not all(k in grads_done for k in group):
            return None
        ex = _Exchange("scatter", [grads_done[k] for k in group], [zones[k[0]] for k in group],
                       [kind_of[k[0]] for k in group], [k[1] for k in group], f"scatter{gi}")
        for k, zone in zip(group, ex.lands):
            zones[k[0]] = zone
        scatters.append((ex, group))
        return ex.token

    small = {}

    def small_sink(loss, gs):
        ln_g = jnp.concatenate([gs[f"ln_gain_{i}_{j}"] for i in range(DEPTH) for j in range(3)], axis=0)
        ln_b = jnp.concatenate([gs[f"ln_bias_{i}_{j}"] for i in range(DEPTH) for j in range(3)], axis=0)
        ple_bg = jnp.concatenate([gs[f"ple_b_{i}"] for i in range(DEPTH)], axis=0)
        small["list"] = [ln_g.reshape(1, -1), ln_b.reshape(1, -1), gs["alb"].reshape(1, -1), gs["norm_gain"],
                         gs["kv_b"], gs["b_q"], _pad_lanes(gs["sinks"]), gs["b_out"], ple_bg.reshape(1, -1), loss]
        small["gather"] = _GatherDevices(jnp.concatenate(small["list"], axis=1))
        return small["gather"].token

    loss, grad_x, gs = _local_step(
        x[0], p.reshape((p.shape[0],) + p.shape[2:]), loss_target[0], wget, grad_sink, ln_gain_f, ln_bias_f, alb_f, a_norm_gain, kv_b, b_b_q,
        b_sinks, b_b_out, ple_b_gate, small_sink)

    res = {}

    def arrive(batch, after):
        for ex, group in batch:
            srcs, outs = ex.wait(after, lands=[zones[k[0]] for k in group])
            for k, grad, zone in zip(group, srcs, outs):
                grads_done[k], zones[k[0]] = grad, zone

    def half_sums(names, tag, after):
        partial = []
        for n in names:
            own = [grads_done[(n, layer)] for layer in range(zones[n].shape[1])]
            partial.append(_sum_arrivals(zones[n], own, kind_of[n], chip_arr, f"sum_{n}", after=after))
        return _SiblingSwap(partial, tag, after=after)

    def update(names, swap, after):
        for n, own, sib in zip(names, *swap.wait(after)):
            shp = wts[n].shape
            flat = lambda a: a.reshape(-1, shp[-1])
            out = _rowwise(_adam_fn, [flat(wts[n]), flat(mom[n]), flat(vel[n]), own, sib], [],
                           [(own.shape, F32)] * 4, name=f"adam_{n}")
            res[n] = [o.reshape(shp) for o in out]
        return res[names[-1]][1]

    last_names = [k[0] for k in SCATTER_GROUPS[-1]]
    batches = [["ffn_w_gate_up"], [n for n, _ in BIG if n != "ffn_w_gate_up" and n not in last_names], last_names]
    arrive(scatters[:-1], grad_x)
    swap0 = half_sums(batches[0], "swap0", None)
    swap1 = half_sums(batches[1], "swap1", swap0.token)
    updated = update(batches[0], swap0, swap1.token)
    arrive(scatters[-1:], updated)
    swap2 = half_sums(batches[2], "swap2", swap1.token)
    updated = update(batches[1], swap1, swap2.token)
    update(batches[2], swap2, updated)

    small_list = small["list"]
    everyone = small["gather"].wait(grad_x)
    total, = _rowwise(_sum_rows_fn, [everyone], [], [(everyone.shape[1:], F32)], name="sum_small")
    offs, pos = [], 0
    for v in small_list:
        offs.append((pos, v.shape[1]))
        pos += v.shape[1]

    def seg(k):
        return total[0, offs[k][0]:offs[k][0] + offs[k][1]]

    def my_cols(full, rows):
        return lax.dynamic_slice_in_dim(full.reshape(rows, N_CHIPS, dq), chip, 1, axis=1).reshape(rows, dq)

    n_sink = b_sinks.shape[-1]
    small_grads = {
        "ln_gain": my_cols(seg(0), 6).reshape(ln_gain.shape), "ln_bias": my_cols(seg(1), 6).reshape(ln_bias.shape),
        "a_lower_bound": my_cols(seg(2), 2), "a_norm_gain": seg(3).reshape(a_norm_gain.shape),
        "kv_b": seg(4).reshape(kv_b.shape), "b_b_q": seg(5).reshape(b_b_q.shape),
        "b_sinks": seg(6)[:n_sink].reshape(b_sinks.shape), "b_b_out": seg(7).reshape(b_b_out.shape),
        "ple_b_gate": seg(8).reshape(ple_b_gate.shape)}
    names = SMALL_SHARDED + SMALL_REPLICATED
    pack = lambda dct: _pad_lanes(jnp.concatenate([dct[n].reshape(1, -1) for n in names], axis=1))
    g_pack = pack(small_grads)
    upd = _rowwise(_adam_small_fn, [pack(wts), pack(mom), pack(vel), g_pack], [], [(g_pack.shape, F32)] * 3,
                   name="adam_small")
    pos = 0
    for n in names:
        size = wts[n].size
        res[n] = [small_grads[n]] + [u[0, pos:pos + size].reshape(wts[n].shape) for u in upd]
        pos += size

    outs = [seg(9)[0], grad_x[None]]
    for k in range(4):
        outs += [res[n][k] for n in WEIGHT_ORDER]
    return tuple(outs)
```

```python
import functools

import jax
import jax.numpy as jnp
from jax import lax
from jax.experimental import pallas as pl
from jax.experimental.pallas import tpu as pltpu

F32 = jnp.float32
BF16 = jnp.bfloat16
MESH = pl.DeviceIdType.MESH

LANES = 128
HG_DK = 128
HG_CHUNK = 64
HG_SUB = 16
HG_ROWS = 512
HG_HEADS_PER_STEP = 2
LOG2_E = 1.4426950408889634
ATT_HD = 64
ATT_G = 4
WINDOW = 128
DEPTH = 2
ALPHA = (2.0 * DEPTH) ** 0.25
LN_EPS = 1e-5
RMS_EPS = 1e-6
ADAM_LR, ADAM_B1, ADAM_B2, ADAM_EPS, ADAM_WD, ADAM_STEP = 0.001, 0.9, 0.999, 1e-08, 0.01, 10
N_CHIPS = 4
N_DEV = 8
VMEM_LIMIT = 56 * 1024 * 1024
NEG = -1e30


def _pick(n, cap):
    best = None
    for d in range(LANES, min(n, cap) + 1, LANES):
        if n % d == 0:
            best = d
    return n if best is None else best


def _pick_rows(m, cap):
    best = None
    for d in range(16, min(m, cap) + 1, 16):
        if m % d == 0:
            best = d
    return m if best is None else best


def _params(sem):
    return pltpu.CompilerParams(dimension_semantics=sem, vmem_limit_bytes=VMEM_LIMIT)


def _zeros_index(ndim, grid_rank=3):
    return (lambda i, j, kk: (0,) * ndim) if grid_rank == 3 else (lambda kk, i: (0,) * ndim)


def _mm(a, b, *, name, la=None, lb=None, ta=False, tb=False, bias=None, add=None, out_dtype=F32,
        out_layers=None, out_layer=None, after=None, post=None, tile_cols=None, caps=(1024, 1536, 2048),
        a_parts=None, b_parts=None):
    ar, ac = a.shape[-2:]
    br, bc = b.shape[-2:]
    assert a_parts is None or (not ta and la is None and a.shape[0] == a_parts)
    assert b_parts is None or (not tb and lb is None and b.shape[0] == b_parts)
    m, k = (ac, ar) if ta else (ar, ac * (a_parts or 1))
    k2, n = (bc, br) if tb else (br, bc * (b_parts or 1))
    assert k == k2, (a.shape, b.shape, ta, tb)
    if post is not None:
        caps = (512, n if tile_cols is None else tile_cols, caps[2])
    tm, tn, tk = _pick(m, caps[0]), _pick(bc if b_parts else n, caps[1]), _pick(ac if a_parts else k, caps[2])
    assert post is None or tn == caps[1]
    nk = k // tk
    gi, gj = m // tm, n // tn
    a_bytes, b_bytes = m * k * a.dtype.itemsize, k * n * b.dtype.itemsize
    rows_outer = (a_bytes + b_bytes * (gi if gj * nk > 1 else 1)) <= (b_bytes + a_bytes * (gj if gi * nk > 1 else 1))
    k_outer = post is not None and nk > 1 and gj == 1
    grid = (nk, gi) if k_outer else (gi, gj, nk) if rows_outer else (gj, gi, nk)
    keep_at = ta and nk == 1 and gj > 1 and rows_outer

    def bs(block, idx, late=False):
        if k_outer:
            return pl.BlockSpec(block, lambda kk, i: idx(jnp.where(kk == nk - 1, i, 0) if late else i, 0, kk))
        return pl.BlockSpec(block, idx if rows_outer else (lambda q, p, kk: idx(p, q, kk)))

    def spec(block, idx, layer):
        if layer is None:
            return bs(block, idx)
        return bs((None,) + block, lambda i, j, kk: (layer,) + idx(i, j, kk))

    a_spec = spec((tk, tm), lambda i, j, kk: (kk, i), la) if ta else spec((tm, tk), lambda i, j, kk: (i, kk), la)
    b_spec = spec((tn, tk), lambda i, j, kk: (j, kk), lb) if tb else spec((tk, tn), lambda i, j, kk: (kk, j), lb)
    if a_parts:
        a_spec = bs((None, tm, tk), lambda i, j, kk: (kk // (ac // tk), i, kk % (ac // tk)))
    if b_parts:
        b_spec = bs((None, tk, tn), lambda i, j, kk: (j // (bc // tn), kk, j % (bc // tn)))
    in_specs, operands = [a_spec, b_spec], [a, b]
    if bias is not None:
        in_specs.append(bs((1, tn), lambda i, j, kk: (0, j)))
        operands.append(bias)
    if add is not None:
        in_specs.append(bs((tm, tn), lambda i, j, kk: (i, j), late=True))
        operands.append(add)
    if after is not None:
        in_specs.append(pl.BlockSpec(memory_space=pl.ANY))
        operands.append(after)
    dims = (((0 if ta else 1,), (1 if tb else 0,)), ((), ()))
    has_bias, has_add = bias is not None, add is not None
    if post is None:
        fn, rows, whole, outs, sums = None, [], [], [], []
        out_shape = jax.ShapeDtypeStruct((m, n) if out_layers is None else (out_layers, m, n), out_dtype)
        out_specs = spec((tm, tn), lambda i, j, kk: (i, j), out_layer)
    else:
        fn, rows, whole, outs, sums = post
        in_specs += [bs((tm, r.shape[-1] // gj), lambda i, j, kk: (i, j), late=True) for r in rows]
        in_specs += [pl.BlockSpec(tuple(w.shape), _zeros_index(w.ndim, len(grid))) for w in whole]
        operands += list(rows) + list(whole)
        out_shape = [jax.ShapeDtypeStruct(sh, dt) for sh, dt in list(outs) + list(sums)]
        out_specs = ([bs((tm, sh[-1] // gj), lambda i, j, kk: (i, j), late=True) for sh, _ in outs]
                     + [pl.BlockSpec(tuple(sh), _zeros_index(len(sh), len(grid))) for sh, _ in sums])
    n_in, n_extra, n_outs, n_sums = len(operands), len(rows) + len(whole), len(outs), len(sums)

    def body(*refs):
        a_ref, b_ref = refs[0], refs[1]
        pos = 2
        bias_ref = add_ref = None
        if has_bias:
            bias_ref = refs[pos]
            pos += 1
        if has_add:
            add_ref = refs[pos]
            pos += 1
        extra_refs = refs[n_in - n_extra:n_in]
        out_refs = refs[n_in:n_in + max(n_outs, 1)]
        sum_refs = refs[n_in + n_outs:n_in + n_outs + n_sums]
        acc_ref = refs[-1] if nk > 1 else None
        if keep_at:
            at_ref = refs[-1]

            @pl.when(pl.program_id(1) == 0)
            def _():
                at_ref[...] = a_ref[...].astype(BF16).T

            part = lax.dot_general(at_ref[...], b_ref[...].astype(BF16), (((1,), (1 if tb else 0,)), ((), ())),
                                   preferred_element_type=F32)
        else:
            part = lax.dot_general(a_ref[...].astype(BF16), b_ref[...].astype(BF16), dims,
                                   preferred_element_type=F32)

        def finish(total):
            if has_bias:
                total = total + bias_ref[...]
            if has_add:
                total = total + add_ref[...]
            if fn is None:
                out_refs[0][...] = total.astype(out_refs[0].dtype)
                return
            res = fn(total, *[r[...] for r in extra_refs])
            for ref, val in zip(out_refs, res[:n_outs]):
                ref[...] = val.astype(ref.dtype)
            if n_sums:
                @pl.when(pl.program_id(1 if k_outer or not rows_outer else 0) == 0)
                def _():
                    for ref in sum_refs:
                        ref[...] = jnp.zeros(ref.shape, ref.dtype)

                for ref, val in zip(sum_refs, res[n_outs:]):
                    ref[...] += val

        if nk == 1:
            finish(part)
        elif k_outer:
            kk = pl.program_id(0)
            rows_i = pl.ds(pl.multiple_of(pl.program_id(1) * tm, tm), tm)

            @pl.when(kk == 0)
            def _():
                acc_ref[rows_i, :] = part

            @pl.when(kk > 0)
            def _():
                acc_ref[rows_i, :] += part

            @pl.when(kk == nk - 1)
            def _():
                finish(acc_ref[rows_i, :])
        else:
            kk = pl.program_id(2)

            @pl.when(kk == 0)
            def _():
                acc_ref[...] = part

            @pl.when(kk > 0)
            def _():
                acc_ref[...] += part

            @pl.when(kk == nk - 1)
            def _():
                finish(acc_ref[...])

    return pl.pallas_call(
        body, name=name, grid=grid, in_specs=in_specs, out_specs=out_specs, out_shape=out_shape,
        scratch_shapes=([pltpu.VMEM((m, n) if k_outer else (tm, tn), F32)] if nk > 1
                        else [pltpu.VMEM((tm, tk), BF16)] if keep_at else []),
        compiler_params=_params(("arbitrary", "arbitrary") if k_outer
                                else ("arbitrary" if n_sums else "parallel", "arbitrary" if keep_at else "parallel",
                                      "arbitrary") if rows_outer
                                else ("parallel", "arbitrary" if n_sums else "parallel", "arbitrary")),
    )(*operands)


def _rowwise(fn, rows, whole, outs, sums=(), *, name, tm=256):
    m = rows[0].shape[-2]
    tm = _pick_rows(m, tm)
    n_rows, n_whole, n_outs, n_sums = len(rows), len(whole), len(outs), len(sums)

    def rspec(shape):
        lead = len(shape) - 2
        return pl.BlockSpec(tuple(shape[:-2]) + (tm, shape[-1]), lambda i: (0,) * lead + (i, 0))

    def wspec(shape):
        return pl.BlockSpec(tuple(shape), lambda i: (0,) * len(shape))

    def body(*refs):
        vals = [r[...] for r in refs[:n_rows + n_whole]]
        out_refs = refs[n_rows + n_whole:n_rows + n_whole + n_outs]
        sum_refs = refs[n_rows + n_whole + n_outs:]
        res = fn(*vals)
        for ref, val in zip(out_refs, res[:n_outs]):
            ref[...] = val.astype(ref.dtype)
        if n_sums:
            @pl.when(pl.program_id(0) == 0)
            def _():
                for ref in sum_refs:
                    ref[...] = jnp.zeros(ref.shape, ref.dtype)

            for ref, val in zip(sum_refs, res[n_outs:]):
                ref[...] += val

    result = pl.pallas_call(
        body, name=name, grid=(m // tm,),
        in_specs=[rspec(r.shape) for r in rows] + [wspec(w.shape) for w in whole],
        out_specs=[rspec(s) for s, _ in outs] + [wspec(s) for s, _ in sums],
        out_shape=[jax.ShapeDtypeStruct(s, d) for s, d in list(outs) + list(sums)],
        compiler_params=_params(("arbitrary",)),
    )(*rows, *whole)
    return result


def _sigmoid(v):
    return jax.nn.sigmoid(v)


def _col_sum(v):
    return jnp.sum(v, axis=0, keepdims=True)


def _ln_stats(z):
    mu = jnp.mean(z, axis=-1, keepdims=True)
    zc = z - mu
    var = jnp.mean(zc * zc, axis=-1, keepdims=True)
    rstd = lax.rsqrt(var + LN_EPS)
    return zc * rstd, rstd


def _ln_fwd_fn(xin, h, gain, bias):
    xhat, _ = _ln_stats(ALPHA * xin + h)
    y = xhat * gain + bias
    return y, y


def _ple_ln_fwd_fn(xin, pg, pu, gain, bias):
    xhat, _ = _ln_stats(ALPHA * xin + _sigmoid(pg) * pu)
    y = xhat * gain + bias
    return y, y


def _ln_dz(dy, z, gain):
    xhat, rstd = _ln_stats(z)
    dxhat = dy * gain
    dz = rstd * (dxhat - jnp.mean(dxhat, axis=-1, keepdims=True)
                 - xhat * jnp.mean(dxhat * xhat, axis=-1, keepdims=True))
    return dz, _col_sum(dy * xhat), _col_sum(dy)


def _ln_bwd_fn(dy, xin, h, gain):
    dz, dgain, dbias = _ln_dz(dy, ALPHA * xin + h, gain)
    return ALPHA * dz, dz, dgain, dbias, _col_sum(dz)


def _ple_ln_bwd_fn(dy, xin, pg, pu, gain):
    sg = _sigmoid(pg)
    dz, dgain, dbias = _ln_dz(dy, ALPHA * xin + sg * pu, gain)
    dpg = dz * pu * sg * (1.0 - sg)
    return ALPHA * dz, dpg, dz * sg, dgain, dbias, _col_sum(dpg)


def _swiglu_fwd_fn(gu):
    hid = gu.shape[-1] // 2
    gate, up = gu[:, :hid], gu[:, hid:]
    return gu, gate * _sigmoid(gate) * up


def _swiglu_bwd_fn(dact, gu):
    gu = gu.astype(F32)
    hid = gu.shape[-1] // 2
    gate, up = gu[:, :hid], gu[:, hid:]
    sg = _sigmoid(gate)
    dgate = dact * up * sg * (1.0 + gate * (1.0 - sg))
    dup = dact * gate * sg
    return (jnp.concatenate([dgate, dup], axis=-1),)


def _loss_fn(y, target):
    err = y - target
    inv = 1.0 / y.shape[-1]
    part = 0.5 * inv * jnp.sum(jnp.sum(err * err, axis=-1, keepdims=True), axis=0, keepdims=True)
    return err * inv, jnp.broadcast_to(part, (1, LANES))


def _adam_fn(w, mom, vel, p_own, p_sib):
    g = p_own.astype(F32) + p_sib.astype(F32)
    m_new = ADAM_B1 * mom + (1.0 - ADAM_B1) * g
    v_new = ADAM_B2 * vel + (1.0 - ADAM_B2) * (g * g)
    m_hat = m_new / (1.0 - ADAM_B1 ** ADAM_STEP)
    v_hat = v_new / (1.0 - ADAM_B2 ** ADAM_STEP)
    delta = -ADAM_LR * (m_hat / (jnp.sqrt(v_hat) + ADAM_EPS) + ADAM_WD * w)
    return g, delta, m_new, v_new


def _split2(x):
    hi = x.astype(BF16)
    return hi, (x - hi.astype(F32)).astype(BF16)


def _dot3(a, b, dims):
    a_hi, a_lo = _split2(a)
    b_hi, b_lo = _split2(b)
    dn = (dims, ((), ()))
    return (lax.dot_general(a_hi, b_hi, dn, preferred_element_type=F32)
            + (lax.dot_general(a_hi, b_lo, dn, preferred_element_type=F32)
               + lax.dot_general(a_lo, b_hi, dn, preferred_element_type=F32)))


def _tdot(mask01, b):
    m = mask01.astype(BF16)
    b_hi = b.astype(BF16)
    rest = b - b_hi.astype(F32)
    b_mid = rest.astype(BF16)
    b_lo = (rest - b_mid.astype(F32)).astype(BF16)
    dn = (((1,), (0,)), ((), ()))
    return (lax.dot_general(m, b_hi, dn, preferred_element_type=F32)
            + (lax.dot_general(m, b_mid, dn, preferred_element_type=F32)
               + lax.dot_general(m, b_lo, dn, preferred_element_type=F32)))


def _hdot(a, b):
    return _dot3(a, b, ((1,), (0,)))


def _hdot_nt(a, b):
    return _dot3(a, b, ((1,), (1,)))


def _hdot_tn(a, b):
    return _dot3(a, b, ((0,), (0,)))


def _dot(a, b):
    return lax.dot_general(a.astype(BF16), b.astype(BF16), (((1,), (0,)), ((), ())), preferred_element_type=F32)


def _dot_nt(a, b):
    return lax.dot_general(a.astype(BF16), b.astype(BF16), (((1,), (1,)), ((), ())), preferred_element_type=F32)


def _dot_tn(a, b):
    return lax.dot_general(a.astype(BF16), b.astype(BF16), (((0,), (0,)), ((), ())), preferred_element_type=F32)


def _hg_masks():
    c = HG_CHUNK
    row = lax.broadcasted_iota(jnp.int32, (c, c), 0)
    col = lax.broadcasted_iota(jnp.int32, (c, c), 1)
    base = row & (-HG_SUB)
    return row, col, base, col <= row, col < base


def _hg_gates(qr, fr, alb):
    lbound = _sigmoid(alb[0:1, :] - alb[1:2, :])
    sig = _sigmoid(fr)
    forget = lbound + (1.0 - lbound) * sig
    kk = (1.0 - lbound) * _sigmoid(-fr)
    qt = qr * _sigmoid(qr) * (HG_DK ** -0.5)
    return qt, kk, jnp.log(forget), lbound, sig, forget


def _hg_scores(qt, kk, g, scores=True):
    c, nsub = HG_CHUNK, HG_CHUNK // HG_SUB
    row, col, base, causal, below = _hg_masks()
    b = _tdot(causal, g)
    rr = _tdot(below, g)
    bq = b - rr
    qh = qt * jnp.exp(bq)
    edecs = [None]
    parts = [jnp.zeros((HG_SUB, c), F32)]
    for i in range(1, nsub):
        edec = jnp.exp(jnp.minimum(rr[i * HG_SUB:i * HG_SUB + 1, :] - b, 0.0))
        edecs.append(edec)
        if scores:
            parts.append(_dot_nt(qh[i * HG_SUB:(i + 1) * HG_SUB, :], kk * edec))
    q3 = qt.reshape(nsub, HG_SUB, HG_DK)
    if not scores:
        return None, b, bq, qh, edecs, (b.reshape(nsub, HG_SUB, HG_DK), q3, kk.reshape(nsub, HG_SUB, HG_DK))
    a = jnp.where(below, jnp.concatenate(parts, axis=0), 0.0)
    b2 = b * LOG2_E
    b3 = b2.reshape(nsub, HG_SUB, HG_DK)
    c3 = (b2 - jnp.log2(kk)).reshape(nsub, HG_SUB, HG_DK)
    for j in range(HG_SUB):
        ek = jnp.exp2(b3 - c3[:, j:j + 1, :])
        colv = jnp.sum(q3 * ek, axis=-1, keepdims=True).reshape(c, 1)
        a = jnp.where(col == base + j, colv, a)
    a = jnp.where(causal, a, 0.0)
    return a, b, bq, qh, edecs, None


def _hg_norm(o, gr, gain):
    r = lax.rsqrt(jnp.mean(o * o, axis=-1, keepdims=True) + RMS_EPS)
    sg = _sigmoid(gr)
    return o * r * gain, r, sg


def _hgrn2_fwd(proj, alb, gain, *, rb):
    m, d4 = proj.shape
    d = d4 // 4
    heads = d // HG_DK
    hp = HG_HEADS_PER_STEP
    rb = min(rb, m)
    cpb = rb // HG_CHUNK
    nrb = m // rb

    def body(q_ref, f_ref, v_ref, g_ref, alb_ref, gain_ref, o_ref, og_ref, st_ref, a_ref, state):
        @pl.when(pl.program_id(1) == 0)
        def _():
            state[...] = jnp.zeros(state.shape, F32)

        def chunk(ci, carry):
            sl = pl.ds(pl.multiple_of(ci * HG_CHUNK, HG_CHUNK), HG_CHUNK)
            for u in range(hp):
                ln = slice(u * HG_DK, (u + 1) * HG_DK)
                qt, kk, g, _, _, _ = _hg_gates(q_ref[sl, ln], f_ref[sl, ln], alb_ref[:, ln])
                v = v_ref[sl, ln]
                st = state[u]
                st_ref[u, ci] = st
                a, b, _, _, _, _ = _hg_scores(qt, kk, g)
                a_ref[u, ci] = a.astype(a_ref.dtype)
                o = _dot(a, v) + _dot_nt(qt * jnp.exp(b), st)
                b_last = b[HG_CHUNK - 1:HG_CHUNK, :]
                state[u] = st * jnp.exp(b_last) + _hdot_tn(v, kk * jnp.exp(b_last - b))
                o_ref[sl, ln] = o
                n, _, sg = _hg_norm(o, g_ref[sl, ln], gain_ref[...])
                og_ref[sl, ln] = (n * g_ref[sl, ln] * sg).astype(og_ref.dtype)
            return carry

        lax.fori_loop(0, cpb, chunk, 0)

    def col(cidx):
        return pl.BlockSpec((rb, hp * HG_DK), lambda h, r: (r, cidx * (heads // hp) + h))

    return pl.pallas_call(
        body, name="hgrn2_fwd", grid=(heads // hp, nrb),
        in_specs=[col(0), col(1), col(2), col(3),
                  pl.BlockSpec((2, hp * HG_DK), lambda h, r: (0, h)),
                  pl.BlockSpec((1, HG_DK), lambda h, r: (0, 0))],
        out_specs=[pl.BlockSpec((rb, hp * HG_DK), lambda h, r: (r, h)),
                   pl.BlockSpec((rb, hp * HG_DK), lambda h, r: (r, h)),
                   pl.BlockSpec((hp, cpb, HG_DK, HG_DK), lambda h, r: (h, r, 0, 0)),
                   pl.BlockSpec((hp, cpb, HG_CHUNK, HG_CHUNK), lambda h, r: (h, r, 0, 0))],
        out_shape=[jax.ShapeDtypeStruct((m, d), F32), jax.ShapeDtypeStruct((m, d), BF16),
                   jax.ShapeDtypeStruct((heads, m // HG_CHUNK, HG_DK, HG_DK), F32),
                   jax.ShapeDtypeStruct((heads, m // HG_CHUNK, HG_CHUNK, HG_CHUNK), BF16)],
        scratch_shapes=[pltpu.VMEM((hp, HG_DK, HG_DK), F32)],
        compiler_params=_params(("parallel", "arbitrary")),
    )(proj, proj, proj, proj, alb, gain)


def _hgrn2_bwd(proj, o_pre, states, scores, dog, alb, gain, *, rb):
    m, d4 = proj.shape
    d = d4 // 4
    heads = d // HG_DK
    rb = min(rb, m)
    cpb = rb // HG_CHUNK
    nrb = m // rb
    c, nsub = HG_CHUNK, HG_CHUNK // HG_SUB

    def body(q_ref, f_ref, v_ref, g_ref, o_ref, st_ref, a_ref, dog_ref, alb_ref, gain_ref,
             dp_ref, dalb_ref, dgain_ref, dstate, carry_ref):
        first = (pl.program_id(0) == 0) & (pl.program_id(1) == 0)

        @pl.when(first)
        def _():
            dgain_ref[...] = jnp.zeros(dgain_ref.shape, F32)

        @pl.when(pl.program_id(1) == 0)
        def _():
            dstate[...] = jnp.zeros(dstate.shape, F32)
            carry_ref[...] = jnp.zeros(carry_ref.shape, F32)
            dalb_ref[...] = jnp.zeros(dalb_ref.shape, F32)

        row, col, base, causal, below = _hg_masks()
        sub_iota = lax.broadcasted_iota(jnp.int32, (nsub, HG_SUB, HG_DK), 1)
        row_k = lax.broadcasted_iota(jnp.int32, (c, HG_DK), 0)
        upper = col >= row

        def chunk(step, carry):
            ci = cpb - 1 - step
            sl = pl.ds(pl.multiple_of(ci * HG_CHUNK, HG_CHUNK), HG_CHUNK)
            qr, fr, v, gr = q_ref[sl, :], f_ref[sl, :], v_ref[sl, :], g_ref[sl, :]
            qt, kk, g, lbound, sig, forget = _hg_gates(qr, fr, alb_ref[...])
            o = o_ref[sl, :]
            dogv = dog_ref[sl, :]
            gain_v = gain_ref[...]
            n, r, sg = _hg_norm(o, gr, gain_v)
            dgr = dogv * n * sg * (1.0 + gr * (1.0 - sg))
            dn = dogv * gr * sg
            dgain_ref[...] += _col_sum(dn * o * r)
            u = dn * gain_v
            d_o = r * u - o * (r * r * r) * jnp.mean(u * o, axis=-1, keepdims=True)
            st0 = st_ref[ci]
            dst = dstate[...]
            _, b, bq, qh, edecs, (b3, q3, k3) = _hg_scores(qt, kk, g, scores=False)
            a = a_ref[ci]
            eb = jnp.exp(b)
            b_last = b[c - 1:c, :]
            kdl_dec = jnp.exp(b_last - b)
            kdl = kk * kdl_dec
            d_a = jnp.where(causal, _dot_nt(d_o, v), 0.0)
            d_at = _dot_nt(v, d_o)
            dv = _dot_tn(a, d_o) + _dot_nt(kdl, dst)
            dq = eb * _hdot(d_o, st0)
            dk = _hdot(v, dst) * kdl_dec
            d_a_below = jnp.where(below, d_a, 0.0)
            dq_parts = [jnp.zeros((HG_SUB, HG_DK), F32)]
            for i in range(1, nsub):
                lo, hi = i * HG_SUB, (i + 1) * HG_SUB
                dq_parts.append(_hdot(d_a_below[lo:hi, :], kk * edecs[i]))
                gi = _hdot(d_at[:, lo:hi], qh[lo:hi, :])
                dk = dk + jnp.where(row_k < lo, edecs[i] * gi, 0.0)
            dq = dq + jnp.concatenate(dq_parts, axis=0) * jnp.exp(bq)
            dq3 = jnp.zeros((nsub, HG_SUB, HG_DK), F32)
            dk3 = jnp.zeros((nsub, HG_SUB, HG_DK), F32)
            d_diag = jnp.concatenate([d_a[i * HG_SUB:(i + 1) * HG_SUB, i * HG_SUB:(i + 1) * HG_SUB]
                                      for i in range(nsub)], axis=0).reshape(nsub, HG_SUB, HG_SUB)
            for j in range(HG_SUB):
                e = jnp.exp(jnp.minimum(b3 - b3[:, j:j + 1, :], 0.0))
                t1 = d_diag[:, :, j:j + 1] * e
                dq3 = dq3 + t1 * k3[:, j:j + 1, :]
                dk3 = jnp.where(sub_iota == j, jnp.sum(t1 * q3, axis=1, keepdims=True), dk3)
            dq = dq + dq3.reshape(c, HG_DK)
            dk = dk + dk3.reshape(c, HG_DK)
            dstate[...] = dst * jnp.exp(b_last) + _hdot_tn(d_o, qt * eb)
            dglog = _tdot(upper, qt * dq - kk * dk) + carry_ref[...]
            carry_ref[...] = dglog[0:1, :]
            dforget = dglog / forget
            one_m_lb = 1.0 - lbound
            dsig = (dforget - dk) * one_m_lb
            sneg = _sigmoid(-fr)
            dlb = _col_sum(dforget * (1.0 - sig) - dk * sneg)
            dalb0 = dlb * lbound * one_m_lb
            dalb_ref[...] += jnp.concatenate([dalb0, -dalb0], axis=0)
            sq = _sigmoid(qr)
            dp_ref[0, sl, :] = (dq * (HG_DK ** -0.5) * sq * (1.0 + qr * (1.0 - sq))).astype(dp_ref.dtype)
            dp_ref[1, sl, :] = (dsig * sig * (1.0 - sig)).astype(dp_ref.dtype)
            dp_ref[2, sl, :] = dv.astype(dp_ref.dtype)
            dp_ref[3, sl, :] = dgr.astype(dp_ref.dtype)
            return carry

        lax.fori_loop(0, cpb, chunk, 0, unroll=2)

    def rev(r):
        return nrb - 1 - r

    def col(cidx):
        return pl.BlockSpec((rb, HG_DK), lambda h, r: (rev(r), cidx * heads + h))

    def head_rows():
        return pl.BlockSpec((rb, HG_DK), lambda h, r: (rev(r), h))

    return pl.pallas_call(
        body, name="hgrn2_bwd", grid=(heads, nrb),
        in_specs=[col(0), col(1), col(2), col(3), head_rows(),
                  pl.BlockSpec((None, cpb, HG_DK, HG_DK), lambda h, r: (h, rev(r), 0, 0)),
                  pl.BlockSpec((None, cpb, HG_CHUNK, HG_CHUNK), lambda h, r: (h, rev(r), 0, 0)),
                  head_rows(),
                  pl.BlockSpec((2, HG_DK), lambda h, r: (0, h)),
                  pl.BlockSpec((1, HG_DK), lambda h, r: (0, 0))],
        out_specs=[pl.BlockSpec((4, rb, HG_DK), lambda h, r: (0, rev(r), h)),
                   pl.BlockSpec((2, HG_DK), lambda h, r: (0, h)),
                   pl.BlockSpec((1, HG_DK), lambda h, r: (0, 0))],
        out_shape=[jax.ShapeDtypeStruct((4, m, d), BF16), jax.ShapeDtypeStruct((2, d), F32),
                   jax.ShapeDtypeStruct((1, HG_DK), F32)],
        scratch_shapes=[pltpu.VMEM((HG_DK, HG_DK), F32), pltpu.VMEM((1, HG_DK), F32)],
        compiler_params=_params(("arbitrary", "arbitrary")),
    )(proj, proj, proj, proj, o_pre, states, scores, dog, alb, gain)


def _swa_probs(qh, kp, kc, sink, slope, has_prev, lse=None):
    rows = qh.shape[0]
    qi = lax.broadcasted_iota(jnp.int32, (rows, WINDOW), 0) & (WINDOW - 1)
    si = lax.broadcasted_iota(jnp.int32, (rows, WINDOW), 1)
    scale = ATT_HD ** -0.5
    dist_c = (qi - si).astype(F32)
    s_p = _dot_nt(qh, kp) * scale - slope * (dist_c + float(WINDOW))
    s_c = _dot_nt(qh, kc) * scale - slope * dist_c
    s_p = jnp.where((si > qi) & has_prev, s_p, NEG)
    s_c = jnp.where(si <= qi, s_c, NEG)
    if lse is not None:
        return jnp.exp(s_p - lse), jnp.exp(s_c - lse), jnp.exp(sink - lse), lse
    mx = jnp.maximum(jnp.maximum(jnp.max(s_p, axis=-1, keepdims=True), jnp.max(s_c, axis=-1, keepdims=True)), sink)
    e_p, e_c, e_s = jnp.exp(s_p - mx), jnp.exp(s_c - mx), jnp.exp(sink - mx)
    total = jnp.sum(e_p, axis=-1, keepdims=True) + jnp.sum(e_c, axis=-1, keepdims=True) + e_s
    inv = 1.0 / total
    return e_p * inv, e_c * inv, e_s * inv, mx + jnp.log(total)


def _slope(h, n_heads):
    return float(2.0 ** (-8.0 * (h + 1) / n_heads))


def _swa_group(ref_vals, sink_ref, kh, n_heads):
    heads = [kh * ATT_G + g for g in range(ATT_G)]
    stacked = [jnp.concatenate([v[:, h * ATT_HD:(h + 1) * ATT_HD] for h in heads], axis=0) for v in ref_vals]
    grp = lax.shift_right_logical(lax.broadcasted_iota(jnp.int32, (ATT_G * WINDOW, 1), 0), WINDOW.bit_length() - 1)
    slope = jnp.zeros((ATT_G * WINDOW, 1), F32)
    sink = jnp.zeros((ATT_G * WINDOW, 1), F32)
    for g, h in enumerate(heads):
        slope = jnp.where(grp == g, _slope(h, n_heads), slope)
        sink = jnp.where(grp == g, sink_ref[:, h:h + 1], sink)
    return stacked, slope, sink


def _swa_fwd(q, kv, sinks):
    m, d = q.shape
    n_heads = d // ATT_HD
    kvh = n_heads // ATT_G
    kd = kvh * ATT_HD
    nb = m // WINDOW

    def body(q_ref, kvp_ref, kvc_ref, sink_ref, o_ref, lse_ref):
        has_prev = pl.program_id(0) > 0
        qv, kvp, kvc = q_ref[...], kvp_ref[...], kvc_ref[...]
        lane_h = lax.broadcasted_iota(jnp.int32, (WINDOW, n_heads), 1)
        outs, lse_all = [], jnp.zeros((WINDOW, n_heads), F32)
        for kh in range(kvh):
            ks = slice(kh * ATT_HD, (kh + 1) * ATT_HD)
            vs = slice(kd + kh * ATT_HD, kd + (kh + 1) * ATT_HD)
            (q4,), slope, sink = _swa_group([qv], sink_ref, kh, n_heads)
            p_p, p_c, _, lse = _swa_probs(q4, kvp[:, ks], kvc[:, ks], sink, slope, has_prev)
            o4 = _dot(p_p, kvp[:, vs]) + _dot(p_c, kvc[:, vs])
            for g in range(ATT_G):
                rows = slice(g * WINDOW, (g + 1) * WINDOW)
                outs.append(o4[rows, :])
                lse_all = jnp.where(lane_h == kh * ATT_G + g, lse[rows, :], lse_all)
        o_ref[...] = jnp.concatenate(outs, axis=-1).astype(o_ref.dtype)
        lse_ref[...] = lse_all

    return pl.pallas_call(
        body, name="swa_fwd", grid=(nb,),
        in_specs=[pl.BlockSpec((WINDOW, d), lambda n: (n, 0)),
                  pl.BlockSpec((WINDOW, 2 * kd), lambda n: (jnp.maximum(n - 1, 0), 0)),
                  pl.BlockSpec((WINDOW, 2 * kd), lambda n: (n, 0)),
                  pl.BlockSpec((1, n_heads), lambda n: (0, 0))],
        out_specs=[pl.BlockSpec((WINDOW, d), lambda n: (n, 0)), pl.BlockSpec((WINDOW, n_heads), lambda n: (n, 0))],
        out_shape=[jax.ShapeDtypeStruct((m, d), BF16), jax.ShapeDtypeStruct((m, n_heads), F32)],
        compiler_params=_params(("arbitrary",)),
    )(q, kv, kv, sinks)


def _swa_bwd(q, kv, sinks, lse, dao):
    m, d = q.shape
    n_heads = d // ATT_HD
    kvh = n_heads // ATT_G
    kd = kvh * ATT_HD
    nb = m // WINDOW
    scale = ATT_HD ** -0.5

    def body(q_ref, kvp_ref, kvc_ref, sink_ref, lse_ref, do_ref, dq_ref, dkvc_ref, dkvp_ref, dqsum_ref, dsink_ref):
        @pl.when(pl.program_id(0) == 0)
        def _():
            dqsum_ref[...] = jnp.zeros(dqsum_ref.shape, F32)
            dsink_ref[...] = jnp.zeros(dsink_ref.shape, F32)

        has_prev = pl.program_id(0) > 0
        qv, kvp, kvc, dov = q_ref[...], kvp_ref[...], kvc_ref[...], do_ref[...]
        lane_h = lax.broadcasted_iota(jnp.int32, (1, n_heads), 1)
        dsink = jnp.zeros((1, n_heads), F32)
        dq_parts, dk_p, dk_c, dv_p, dv_c = [], [], [], [], []
        for kh in range(kvh):
            ks = slice(kh * ATT_HD, (kh + 1) * ATT_HD)
            vs = slice(kd + kh * ATT_HD, kd + (kh + 1) * ATT_HD)
            kp, kc, vp, vc = kvp[:, ks], kvc[:, ks], kvp[:, vs], kvc[:, vs]
            (q4, do4), slope, sink = _swa_group([qv, dov], sink_ref, kh, n_heads)
            lse4 = jnp.concatenate([lse_ref[:, kh * ATT_G + g:kh * ATT_G + g + 1] for g in range(ATT_G)], axis=0)
            p_p, p_c, p_s, _ = _swa_probs(q4, kp, kc, sink, slope, has_prev, lse=lse4)
            dp_p, dp_c = _dot_nt(do4, vp), _dot_nt(do4, vc)
            delta = jnp.sum(p_p * dp_p, axis=-1, keepdims=True) + jnp.sum(p_c * dp_c, axis=-1, keepdims=True)
            ds_p, ds_c = p_p * (dp_p - delta), p_c * (dp_c - delta)
            sink_term = p_s * delta
            dq4 = (_dot(ds_p, kp) + _dot(ds_c, kc)) * scale
            for g in range(ATT_G):
                rows = slice(g * WINDOW, (g + 1) * WINDOW)
                dsink = dsink + jnp.where(lane_h == kh * ATT_G + g, -_col_sum(sink_term[rows, :]), 0.0)
                dq_parts.append(dq4[rows, :])
            dk_p.append(_dot_tn(ds_p, q4) * scale)
            dk_c.append(_dot_tn(ds_c, q4) * scale)
            dv_p.append(_dot_tn(p_p, do4))
            dv_c.append(_dot_tn(p_c, do4))
        dq = jnp.concatenate(dq_parts, axis=-1)
        dq_ref[...] = dq.astype(dq_ref.dtype)
        dqsum_ref[...] += _col_sum(dq)
        dsink_ref[...] += dsink
        dkvc_ref[...] = jnp.concatenate(dk_c + dv_c, axis=-1)
        dkvp_ref[...] = jnp.concatenate(dk_p + dv_p, axis=-1)

    return pl.pallas_call(
        body, name="swa_bwd", grid=(nb,),
        in_specs=[pl.BlockSpec((WINDOW, d), lambda n: (n, 0)),
                  pl.BlockSpec((WINDOW, 2 * kd), lambda n: (jnp.maximum(n - 1, 0), 0)),
                  pl.BlockSpec((WINDOW, 2 * kd), lambda n: (n, 0)),
                  pl.BlockSpec((1, n_heads), lambda n: (0, 0)),
                  pl.BlockSpec((WINDOW, n_heads), lambda n: (n, 0)),
                  pl.BlockSpec((WINDOW, d), lambda n: (n, 0))],
        out_specs=[pl.BlockSpec((WINDOW, d), lambda n: (n, 0)),
                   pl.BlockSpec((WINDOW, 2 * kd), lambda n: (n, 0)),
                   pl.BlockSpec((WINDOW, 2 * kd), lambda n: (n, 0)),
                   pl.BlockSpec((1, d), lambda n: (0, 0)),
                   pl.BlockSpec((1, n_heads), lambda n: (0, 0))],
        out_shape=[jax.ShapeDtypeStruct((m, d), BF16), jax.ShapeDtypeStruct((m, 2 * kd), F32),
                   jax.ShapeDtypeStruct((m, 2 * kd), F32), jax.ShapeDtypeStruct((1, d), F32),
                   jax.ShapeDtypeStruct((1, n_heads), F32)],
        compiler_params=_params(("arbitrary",)),
    )(q, kv, kv, sinks, lse, dao)


def _kv_grad_combine(dkv_cur, dkv_prev):
    m, w = dkv_cur.shape
    nb = m // WINDOW

    def body(cur_ref, nxt_ref, o_ref, sum_ref):
        @pl.when(pl.program_id(0) == 0)
        def _():
            sum_ref[...] = jnp.zeros(sum_ref.shape, F32)

        total = cur_ref[...] + jnp.where(pl.program_id(0) < nb - 1, nxt_ref[...], 0.0)
        o_ref[...] = total.astype(o_ref.dtype)
        sum_ref[...] += _col_sum(total)

    return pl.pallas_call(
        body, name="kv_grad_combine", grid=(nb,),
        in_specs=[pl.BlockSpec((WINDOW, w), lambda n: (n, 0)),
                  pl.BlockSpec((WINDOW, w), lambda n: (jnp.minimum(n + 1, nb - 1), 0))],
        out_specs=[pl.BlockSpec((WINDOW, w), lambda n: (n, 0)), pl.BlockSpec((1, w), lambda n: (0, 0))],
        out_shape=[jax.ShapeDtypeStruct((m, w), BF16), jax.ShapeDtypeStruct((1, w), F32)],
        compiler_params=_params(("arbitrary",)),
    )(dkv_cur, dkv_prev)


def _row(v):
    return v.reshape(1, -1)


def _local_step(x, p, target, wget, grad_sink, ln_gain, ln_bias, alb, norm_gain, kv_b, b_q, sinks, b_out, ple_b,
                small_sink=None):
    gs = {}
    gains = ln_gain.reshape(DEPTH * 3, -1)
    biases = ln_bias.reshape(DEPTH * 3, -1)
    sd = x.shape
    pending = [None]

    def mm(a, b, lb=0, **kw):
        after, pending[0] = pending[0], None
        return _mm(a, b, lb=lb, after=after, **kw)

    def mm_ln(a, wt, xin, i, j, nm, bias=None, pu=None):
        r = 3 * i + j
        if pu is None:
            fn, rows = (lambda h, xv, g, bv: (h,) + _ln_fwd_fn(xv, h, g[r:r + 1], bv[r:r + 1])), [xin]
        else:
            fn = lambda h, xv, puv, g, bv: (h,) + _ple_ln_fwd_fn(xv, h, puv, g[r:r + 1], bv[r:r + 1])
            rows = [xin, pu]
        h, y, yb = _mm(a, wt, lb=0, bias=bias, name=nm,
                       post=(fn, rows, [gains, biases], [(sd, F32), (sd, F32), (sd, BF16)], []))
        return h, (y, yb)

    def mm_ln_bwd(a, wt, add, xin, h, i, j, nm):
        r = 3 * i + j
        dx_part, dh, dg, db, dhsum = mm(a, wt, tb=True, add=add, name=nm,
                                        post=(lambda dy, xv, hv, g: _ln_bwd_fn(dy, xv, hv, g[r:r + 1]), [xin, h],
                                              [gains], [(sd, F32), (sd, BF16)], [((1, sd[1]), F32)] * 3))
        gs[f"ln_gain_{i}_{j}"], gs[f"ln_bias_{i}_{j}"] = dg, db
        return dx_part, dh, dhsum

    def tail_fwd(xa, i):
        wgu = wget("ffn_w_gate_up", i, xa[1])
        hid2 = wgu.shape[-1]
        gu, act = _mm(xa[1], wgu, lb=0, name=f"ffn_up_swiglu{i}", tile_cols=hid2 // 2,
                      post=(_swiglu_fwd_fn, [], [], [((sd[0], hid2), BF16), ((sd[0], hid2 // 2), BF16)], []))
        f, xb = mm_ln(act, wget("ffn_w_down", i, act), xa[0], i, 1, f"ffn_down_ln{i}")
        pu = _mm(p, wget("ple_w_up", i, act), la=i, lb=0, name=f"ple_up{i}")
        pg, xc = mm_ln(xb[1], wget("ple_w_gate", i, act), xb[0], i, 2, f"ple_gate_ln{i}", bias=_row(ple_b[i]), pu=pu)
        return dict(xa=xa, gu=gu, act=act, f=f, xb=xb, pg=pg, pu=pu), xc

    def tail_bwd(head, sv, i, mix_in, mix_h):
        xa, xb = sv["xa"], sv["xb"]
        r = 3 * i + 2
        dxb_part, dpg, dpu, dg2, db2, dbg = head(
            lambda dy, xv, pgv, puv, g: _ple_ln_bwd_fn(dy, xv, pgv, puv, g[r:r + 1]), [xb[0], sv["pg"], sv["pu"]],
            [gains], [(sd, F32), (sd, BF16), (sd, BF16)], [((1, sd[1]), F32)] * 3)[:6]
        gs[f"ple_b_{i}"] = dbg
        gs[f"ln_gain_{i}_2"], gs[f"ln_bias_{i}_2"] = dg2, db2
        grad_of("ple_w_gate", i, xb[1], dpg)
        grad_of("ple_w_up", i, p, dpu, la=i)
        dxa_part, df, _ = mm_ln_bwd(dpg, wget("ple_w_gate", i, None), dxb_part, xa[0], sv["f"], i, 1,
                                    f"ple_gate_dx_ln{i}")
        grad_of("ffn_w_down", i, sv["act"], df)
        gu = sv["gu"]
        dgu, = mm(df, wget("ffn_w_down", i, None), tb=True, name=f"ffn_down_dx_swiglu{i}", tile_cols=gu.shape[1] // 4,
                  post=(_swiglu_bwd_fn, [gu], [], [(gu.shape, BF16)], []))
        grad_of("ffn_w_gate_up", i, xa[1], dgu)
        return mm_ln_bwd(dgu, wget("ffn_w_gate_up", i, None), dxa_part, mix_in, mix_h, i, 0, f"ffn_up_dx_ln{i}")

    def grad_of(nm, i, act, dout, la=None, b_parts=None):
        grad = mm(act, dout, la=la, lb=None, ta=True, out_dtype=BF16, out_layers=1, out_layer=0,
                  name=f"grad_{nm}{i}", b_parts=b_parts)
        token = grad_sink(nm, i, grad)
        if token is not None:
            pending[0] = token

    proj = _mm(x, wget("a_w_in", 0, None), lb=0, name="hg_proj")
    o_pre, og, states, scores = _hgrn2_fwd(proj, alb, norm_gain, rb=HG_ROWS)
    h0, x1 = mm_ln(og, wget("a_w_out", 0, og), x, 0, 0, "hg_out_ln")
    sv0, x3 = tail_fwd(x1, 0)
    kv = _mm(x3[1], wget("kv_w", 0, x3[1]), lb=0, bias=_row(kv_b), out_dtype=BF16, name="kv_proj")
    q = _mm(x3[1], wget("b_w_q", 0, x3[1]), lb=0, bias=b_q, out_dtype=BF16, name="q_proj")
    ao, lse = _swa_fwd(q, kv, sinks)
    h1, x4 = mm_ln(ao, wget("b_w_out", 0, x3[1]), x3[0], 1, 0, "att_out_ln", bias=b_out)
    sv1, y = tail_fwd(x4, 1)

    loss_box = []

    def loss_head(fn, rows, whole, outs, sums):
        def with_loss(yv, tv, *rest):
            dy, part = _loss_fn(yv, tv)
            return fn(dy, *rest) + (part,)

        res = _rowwise(with_loss, [y[0], target] + rows, whole, outs, list(sums) + [((1, LANES), F32)],
                       name="loss_ln_ple_bwd1")
        loss_box.append(res[-1])
        return res

    dx3_part, dh1, dh1sum = tail_bwd(loss_head, sv1, 1, x3[0], h1)
    loss = loss_box[0]
    gs["b_out"] = dh1sum
    grad_of("b_w_out", 0, ao, dh1)
    dao = mm(dh1, wget("b_w_out", 0, None), tb=True, out_dtype=BF16, name="att_out_dx")
    dq, dkv_cur, dkv_prev, dqsum, dsinks = _swa_bwd(q, kv, sinks, lse, dao)
    gs["b_q"], gs["sinks"] = dqsum, dsinks
    dkv, dkvsum = _kv_grad_combine(dkv_cur, dkv_prev)
    gs["kv_b"] = dkvsum
    grad_of("b_w_q", 0, x3[1], dq)
    grad_of("kv_w", 0, x3[1], dkv)
    dx3 = mm(dq, wget("b_w_q", 0, None), tb=True, add=dx3_part, name="q_proj_dx")

    def kv_head(*post):
        return mm(dkv, wget("kv_w", 0, None), tb=True, add=dx3, name="kv_proj_dx_ln_ple_bwd0", post=post)

    dx_part, dh0, _ = tail_bwd(kv_head, sv0, 0, x, h0)
    grad_of("a_w_out", 0, og, dh0)
    dog = mm(dh0, wget("a_w_out", 0, None), tb=True, name="hg_out_dx")
    dproj, dalb, dgain = _hgrn2_bwd(proj, o_pre, states, scores, dog, alb, norm_gain, rb=HG_ROWS)
    gs["alb"], gs["norm_gain"] = dalb, dgain
    if small_sink is not None:
        pending[0] = small_sink(loss, gs)
    grad_of("a_w_in", 0, x, dproj, b_parts=4)
    grad_x = mm(dproj, wget("a_w_in", 0, None), tb=True, add=dx_part, name="hg_proj_dx", a_parts=4)
    return loss, grad_x, gs


HBM_SPEC = pl.BlockSpec(memory_space=pl.ANY)
HBM_ONLY = pl.BlockSpec(memory_space=pltpu.HBM)
SEM_SPEC = pl.BlockSpec(memory_space=pltpu.SEMAPHORE)
SIDE_EFFECT = pltpu.SideEffectType.DATAFLOW_SIDE_EFFECTING


def _slot(kind, j):
    return (j % 2) * 2 + j // 2 if kind == "colp" else j


def _piece(ref, kind, j):
    _, r, c = ref.shape
    if kind == "row":
        return ref.at[:, pl.ds(j * (r // N_CHIPS), r // N_CHIPS), :]
    return ref.at[:, :, pl.ds(_slot(kind, j) * (c // N_CHIPS), c // N_CHIPS)]


def _piece_dyn(ref, kind, j):
    _, r, c = ref.shape
    if kind == "row":
        return ref.at[:, pl.ds(pl.multiple_of(j * (r // N_CHIPS), 16), r // N_CHIPS), :]
    return ref.at[:, :, pl.ds(pl.multiple_of(_slot(kind, j) * (c // N_CHIPS), LANES), c // N_CHIPS)]


def _chip_of(j, c):
    return (j // 2, j % 2, c)


def _in_hbm(a):
    return pltpu.with_memory_space_constraint(a, pltpu.HBM)


PLACE_STEPS = 4


def _place(items, chip, *, name, after=None):
    n = len(items)
    in_specs, out_specs, out_shapes, blocks = [], [], [], []
    for src, layer, kind, out_dtype in items:
        _, r, c = src.shape
        nb = max(k for k in (1, 2, PLACE_STEPS) if r % (16 * k) == 0 or k == 1)
        blocks.append(nb)

        def src_idx(i, chip_ref, layer=layer, nb=nb):
            return (layer, jnp.minimum(i, nb - 1), 0)

        def full_idx(i, chip_ref, kind=kind, nb=nb):
            ib = jnp.minimum(i, nb - 1)
            return (0, chip_ref[0] * nb + ib, 0) if kind == "row" else (0, ib, _slot(kind, chip_ref[0]))

        in_specs.append(pl.BlockSpec((None, r // nb, c), src_idx))
        out_specs.append(pl.BlockSpec((None, r // nb, c), full_idx))
        out_shapes.append(jax.ShapeDtypeStruct((1, r * N_CHIPS, c) if kind == "row" else (1, r, c * N_CHIPS),
                                               out_dtype))
    operands = [it[0] for it in items]
    if after is not None:
        in_specs.append(HBM_SPEC)
        operands.append(after)

    def body(chip_ref, *refs):
        for a in range(n):
            refs[len(refs) - n + a][...] = refs[a][...].astype(refs[len(refs) - n + a].dtype)

    return pl.pallas_call(
        body, name=name,
        grid_spec=pltpu.PrefetchScalarGridSpec(num_scalar_prefetch=1, grid=(PLACE_STEPS,), in_specs=in_specs,
                                               out_specs=out_specs),
        out_shape=out_shapes,
        compiler_params=_params(("arbitrary",)),
    )(chip, *operands)


def _half(ref, c):
    h = ref.shape[1] // 2
    start = c * h if isinstance(c, int) else pl.multiple_of(c * h, 16)
    return ref.at[:, pl.ds(start, h), :]


class _SiblingFill:
    def __init__(self, lands, kinds, name):
        self.kinds, self.name, self.n = kinds, name, len(lands)
        n = self.n
        sem_shape = pltpu.SemaphoreType.DMA((n * N_CHIPS,))

        def body(*refs):
            land_refs, send_sems, recv_sems, token = refs[:n], refs[n], refs[n + 1], refs[-1]
            for cp in self._copies(land_refs, send_sems, recv_sems):
                cp.start()
            token[...] = jnp.zeros(token.shape, token.dtype)

        outs = pl.pallas_call(
            body, name=name + "_start",
            in_specs=[HBM_ONLY] * n,
            out_specs=[SEM_SPEC, SEM_SPEC] + [HBM_ONLY] * n + [pl.BlockSpec(memory_space=pltpu.VMEM)],
            out_shape=[sem_shape, sem_shape] + [pltpu.HBM(a.shape, a.dtype) for a in lands]
                      + [jax.ShapeDtypeStruct((8, LANES), F32)],
            input_output_aliases={i: i + 2 for i in range(n)},
            compiler_params=pltpu.CompilerParams(has_side_effects=SIDE_EFFECT),
        )(*[_in_hbm(a) for a in lands])
        self.send_sems, self.recv_sems, self.lands, self.token = outs[0], outs[1], list(outs[2:2 + n]), outs[-1]

    def _copies(self, land_refs, send_sems, recv_sems):
        x, y, c = lax.axis_index("x"), lax.axis_index("y"), lax.axis_index("c")
        me = 2 * x + y
        copies = []
        for a in range(self.n):
            for k in range(1, N_CHIPS):
                t = (me + k) % N_CHIPS
                slice_t = _piece_dyn(land_refs[a], self.kinds[a], t)
                got = _half(slice_t, c)
                copies.append(pltpu.make_async_remote_copy(
                    src_ref=got, dst_ref=got, send_sem=send_sems.at[a * N_CHIPS + k],
                    recv_sem=recv_sems.at[a * N_CHIPS + k], device_id=(x, y, 1 - c), device_id_type=MESH))
        return copies

    def wait(self, after):
        n = self.n

        def body(*refs):
            land_refs, send_sems, recv_sems = refs[:n], refs[n], refs[n + 1]
            for cp in self._copies(land_refs, send_sems, recv_sems):
                cp.wait_send()
                cp.wait_recv()

        operands = [_in_hbm(a) for a in self.lands] + [self.send_sems, self.recv_sems]
        in_specs = [HBM_ONLY] * n + [SEM_SPEC, SEM_SPEC]
        if after is not None:
            operands.append(after)
            in_specs.append(HBM_SPEC)
        outs = pl.pallas_call(
            body, name=self.name + "_wait",
            in_specs=in_specs, out_specs=[HBM_ONLY] * n,
            out_shape=[pltpu.HBM(a.shape, a.dtype) for a in self.lands],
            input_output_aliases={i: i for i in range(n)},
            compiler_params=pltpu.CompilerParams(has_side_effects=SIDE_EFFECT),
        )(*operands)
        return list(outs)


class _Exchange:
    def __init__(self, mode, srcs, lands, kinds, layers, name, after=None, halves=None):
        self.mode, self.kinds, self.layers, self.name, self.n = mode, kinds, layers, name, len(lands)
        self.halves = halves if halves is not None else [False] * len(lands)
        n, ns = self.n, len(srcs)
        n_in = ns + n + (after is not None)
        sem_shape = pltpu.SemaphoreType.DMA((n * N_CHIPS,))

        def body(*refs):
            src_refs, land_refs = refs[:ns], refs[ns:ns + n]
            send_sems, recv_sems = refs[n_in], refs[n_in + 1]
            token = refs[-1]
            c = lax.axis_index("c")
            me = 2 * lax.axis_index("x") + lax.axis_index("y")
            for j in range(N_CHIPS):
                @pl.when(me == j)
                def _():
                    for a in range(n):
                        for t in range(N_CHIPS):
                            if t != j:
                                src, dst = self._ends(src_refs, land_refs, a, j, t, c)
                                pltpu.make_async_remote_copy(
                                    src_ref=src, dst_ref=dst, send_sem=send_sems.at[a * N_CHIPS + t],
                                    recv_sem=recv_sems.at[a * N_CHIPS + j],
                                    device_id=_chip_of(t, c), device_id_type=MESH).start()
            token[...] = jnp.zeros(token.shape, token.dtype)

        arrays = list(srcs) + list(lands)
        operands = [_in_hbm(a) for a in arrays]
        in_specs = [HBM_ONLY] * (ns + n)
        if after is not None:
            operands.append(after)
            in_specs.append(HBM_SPEC)
        outs = pl.pallas_call(
            body, name=name + "_start",
            in_specs=in_specs,
            out_specs=[SEM_SPEC, SEM_SPEC] + [HBM_ONLY] * (ns + n) + [pl.BlockSpec(memory_space=pltpu.VMEM)],
            out_shape=[sem_shape, sem_shape] + [pltpu.HBM(a.shape, a.dtype) for a in arrays]
                      + [jax.ShapeDtypeStruct((8, LANES), F32)],
            input_output_aliases={i: i + 2 for i in range(ns + n)},
            compiler_params=pltpu.CompilerParams(has_side_effects=SIDE_EFFECT),
        )(*operands)
        self.send_sems, self.recv_sems = outs[0], outs[1]
        self.srcs, self.lands = list(outs[2:2 + ns]), list(outs[2 + ns:2 + ns + n])
        self.token = outs[-1]

    def _ends(self, src_refs, land_refs, a, me_j, peer, c):
        if self.mode == "gather":
            mine = _piece(land_refs[a], self.kinds[a], me_j)
            if self.halves[a]:
                mine = _half(mine, c)
            return mine, mine
        return _piece(src_refs[a], self.kinds[a], peer), land_refs[a].at[me_j, pl.ds(self.layers[a], 1)]

    def wait(self, after, lands=None):
        n, ns = self.n, len(self.srcs)
        lands = self.lands if lands is None else lands

        def body(*refs):
            src_refs, land_refs = refs[:ns], refs[ns:ns + n]
            send_sems, recv_sems = refs[ns + n], refs[ns + n + 1]
            c = lax.axis_index("c")
            me = 2 * lax.axis_index("x") + lax.axis_index("y")
            for j in range(N_CHIPS):
                @pl.when(me != j)
                def _():
                    for a in range(n):
                        sent, _ = self._ends(src_refs, land_refs, a, 0, j, c)
                        _, landed = self._ends(src_refs, land_refs, a, j, 0, c)
                        cp = pltpu.make_async_remote_copy(
                            src_ref=sent, dst_ref=landed, send_sem=send_sems.at[a * N_CHIPS + j],
                            recv_sem=recv_sems.at[a * N_CHIPS + j],
                            device_id=_chip_of(j, c), device_id_type=MESH)
                        cp.wait_send()
                        cp.wait_recv()

        arrays = self.srcs + list(lands)
        operands = [_in_hbm(a) for a in arrays] + [self.send_sems, self.recv_sems]
        in_specs = [HBM_ONLY] * (ns + n) + [SEM_SPEC, SEM_SPEC]
        if after is not None:
            operands.append(after)
            in_specs.append(HBM_SPEC)
        outs = pl.pallas_call(
            body, name=self.name + "_wait",
            in_specs=in_specs, out_specs=[HBM_ONLY] * (ns + n),
            out_shape=[pltpu.HBM(a.shape, a.dtype) for a in arrays],
            input_output_aliases={i: i for i in range(ns + n)},
            compiler_params=pltpu.CompilerParams(has_side_effects=SIDE_EFFECT),
        )(*operands)
        return list(outs[:ns]), list(outs[ns:])


def _sum_arrivals(zone, own_grads, kind, chip, name, after=None):
    _, layers, r, c = zone.shape
    tm = _pick_rows(r, 256)
    nb = r // tm

    def own_idx(l, i, chip_ref):
        return (0, chip_ref[0] * nb + i, 0) if kind == "row" else (0, i, _slot(kind, chip_ref[0]))

    def slot_idx(k):
        return lambda l, i, chip_ref: (jnp.where(chip_ref[0] == k, (k + 1) % N_CHIPS, k), l, i, 0)

    in_specs = [pl.BlockSpec((None, None, tm, c), slot_idx(k)) for k in range(N_CHIPS)]
    in_specs += [pl.BlockSpec((None, tm, c), own_idx) for _ in own_grads]
    operands = [zone] * N_CHIPS + list(own_grads)
    if after is not None:
        in_specs.append(HBM_SPEC)
        operands.append(after)

    def body(chip_ref, *refs):
        slot_refs, own_refs, o_ref = refs[:N_CHIPS], refs[N_CHIPS:N_CHIPS + layers], refs[-1]
        own = own_refs[0][...]
        for u in range(1, layers):
            own = jnp.where(pl.program_id(0) == u, own_refs[u][...], own)
        acc = None
        for k in range(N_CHIPS):
            term = jnp.where(chip_ref[0] == k, own, slot_refs[k][...]).astype(F32)
            acc = term if acc is None else acc + term
        o_ref[...] = acc.astype(o_ref.dtype)

    return pl.pallas_call(
        body, name=name,
        grid_spec=pltpu.PrefetchScalarGridSpec(
            num_scalar_prefetch=1, grid=(layers, nb), in_specs=in_specs,
            out_specs=pl.BlockSpec((tm, c), lambda l, i, chip_ref: (l * nb + i, 0))),
        out_shape=jax.ShapeDtypeStruct((layers * r, c), BF16),
        compiler_params=_params(("arbitrary", "arbitrary")),
    )(chip, *operands)


class _SiblingSwap:
    def __init__(self, arrays, name, after=None):
        self.name, self.n = name, len(arrays)
        n = self.n
        n_in = n + (after is not None)
        sem_shape = pltpu.SemaphoreType.DMA((n,))

        def body(*refs):
            ins, send_sems, recv_sems = refs[:n], refs[n_in], refs[n_in + 1]
            theirs, token = refs[n_in + 2 + n:n_in + 2 + 2 * n], refs[-1]
            for cp in self._copies(ins, theirs, send_sems, recv_sems):
                cp.start()
            token[...] = jnp.zeros(token.shape, token.dtype)

        operands, in_specs = [_in_hbm(a) for a in arrays], [HBM_ONLY] * n
        if after is not None:
            operands.append(after)
            in_specs.append(HBM_SPEC)
        outs = pl.pallas_call(
            body, name=name + "_start",
            in_specs=in_specs,
            out_specs=[SEM_SPEC, SEM_SPEC] + [HBM_ONLY] * (2 * n) + [pl.BlockSpec(memory_space=pltpu.VMEM)],
            out_shape=[sem_shape, sem_shape] + [pltpu.HBM(a.shape, a.dtype) for a in arrays] * 2
                      + [jax.ShapeDtypeStruct((8, LANES), F32)],
            input_output_aliases={i: i + 2 for i in range(n)},
            compiler_params=pltpu.CompilerParams(has_side_effects=SIDE_EFFECT),
        )(*operands)
        self.send_sems, self.recv_sems = outs[0], outs[1]
        self.mine, self.theirs, self.token = list(outs[2:2 + n]), list(outs[2 + n:2 + 2 * n]), outs[-1]

    def _copies(self, mine, theirs, send_sems, recv_sems):
        sibling = (lax.axis_index("x"), lax.axis_index("y"), 1 - lax.axis_index("c"))
        return [pltpu.make_async_remote_copy(src_ref=mine[a], dst_ref=theirs[a], send_sem=send_sems.at[a],
                                             recv_sem=recv_sems.at[a], device_id=sibling, device_id_type=MESH)
                for a in range(self.n)]

    def wait(self, after):
        n = self.n

        def body(*refs):
            for cp in self._copies(refs[:n], refs[n:2 * n], refs[2 * n], refs[2 * n + 1]):
                cp.wait_send()
                cp.wait_recv()

        arrays = self.mine + self.theirs
        outs = pl.pallas_call(
            body, name=self.name + "_wait",
            in_specs=[HBM_ONLY] * (2 * n) + [SEM_SPEC, SEM_SPEC, HBM_SPEC], out_specs=[HBM_ONLY] * (2 * n),
            out_shape=[pltpu.HBM(a.shape, a.dtype) for a in arrays],
            input_output_aliases={i: i for i in range(2 * n)},
            compiler_params=pltpu.CompilerParams(has_side_effects=SIDE_EFFECT),
        )(*[_in_hbm(a) for a in arrays], self.send_sems, self.recv_sems, after)
        return list(outs[:n]), list(outs[n:])


class _GatherDevices:
    def __init__(self, vec):
        sem_shape = pltpu.SemaphoreType.DMA((N_DEV,))

        def body(in_ref, send_sems, recv_sems, vec_ref, out_ref, token):
            for cp in self._copies(in_ref, out_ref, send_sems, recv_sems):
                cp.start()
            token[...] = jnp.zeros(token.shape, token.dtype)

        outs = pl.pallas_call(
            body, name="gather_small_start",
            in_specs=[HBM_ONLY],
            out_specs=[SEM_SPEC, SEM_SPEC, HBM_ONLY, HBM_ONLY, pl.BlockSpec(memory_space=pltpu.VMEM)],
            out_shape=[sem_shape, sem_shape, pltpu.HBM(vec.shape, vec.dtype),
                       pltpu.HBM((N_DEV,) + vec.shape, vec.dtype), jax.ShapeDtypeStruct((8, LANES), F32)],
            input_output_aliases={0: 2},
            compiler_params=pltpu.CompilerParams(has_side_effects=SIDE_EFFECT),
        )(_in_hbm(vec))
        self.send_sems, self.recv_sems, self.vec, self.rows, self.token = outs

    def _copies(self, in_ref, out_ref, send_sems, recv_sems):
        x, y, c = lax.axis_index("x"), lax.axis_index("y"), lax.axis_index("c")
        me = 4 * x + 2 * y + c
        copies = [pltpu.make_async_copy(in_ref, out_ref.at[me], recv_sems.at[0])]
        for rel in range(1, N_DEV):
            peer = (x ^ (rel >> 2), y ^ ((rel >> 1) & 1), c ^ (rel & 1))
            copies.append(pltpu.make_async_remote_copy(
                src_ref=in_ref, dst_ref=out_ref.at[me], send_sem=send_sems.at[rel], recv_sem=recv_sems.at[rel],
                device_id=peer, device_id_type=MESH))
        return copies

    def wait(self, after):
        def body(vec_ref, rows_ref, send_sems, recv_sems, after_ref, vec_out, rows_out):
            copies = self._copies(vec_ref, rows_ref, send_sems, recv_sems)
            copies[0].wait()
            for cp in copies[1:]:
                cp.wait_send()
                cp.wait_recv()

        outs = pl.pallas_call(
            body, name="gather_small_wait",
            in_specs=[HBM_ONLY, HBM_ONLY, SEM_SPEC, SEM_SPEC, HBM_SPEC], out_specs=[HBM_ONLY, HBM_ONLY],
            out_shape=[pltpu.HBM(self.vec.shape, self.vec.dtype), pltpu.HBM(self.rows.shape, self.rows.dtype)],
            input_output_aliases={0: 0, 1: 1},
            compiler_params=pltpu.CompilerParams(has_side_effects=SIDE_EFFECT),
        )(_in_hbm(self.vec), _in_hbm(self.rows), self.send_sems, self.recv_sems, after)
        return outs[1]


BIG = [("a_w_in", "col"), ("a_w_out", "row"), ("kv_w", "row"), ("b_w_q", "row"), ("b_w_out", "row"),
       ("ffn_w_gate_up", "colp"), ("ffn_w_down", "row"), ("ple_w_up", "col"), ("ple_w_gate", "row")]
GATHER_GROUPS = [[("a_w_in", 0), ("small", 0)], [("a_w_out", 0), ("ffn_w_gate_up", 0)],
                 [("ffn_w_down", 0), ("ple_w_gate", 0), ("ple_w_up", 0)], [("kv_w", 0), ("b_w_q", 0), ("b_w_out", 0)],
                 [("ffn_w_gate_up", 1)], [("ffn_w_down", 1), ("ple_w_gate", 1), ("ple_w_up", 1)]]
SCATTER_GROUPS = [[("ple_w_gate", 1), ("ple_w_up", 1), ("ffn_w_down", 1)], [("ffn_w_gate_up", 1)],
                  [("b_w_out", 0), ("b_w_q", 0), ("kv_w", 0)], [("ple_w_gate", 0), ("ple_w_up", 0), ("ffn_w_down", 0)],
                  [("ffn_w_gate_up", 0), ("a_w_out", 0)], [("a_w_in", 0)]]
SMALL_SHARDED = ["ln_gain", "ln_bias", "a_lower_bound"]
SMALL_REPLICATED = ["a_norm_gain", "kv_b", "b_b_q", "b_sinks", "b_b_out", "ple_b_gate"]
WEIGHT_ORDER = ["a_w_in", "a_lower_bound", "a_norm_gain", "a_w_out", "kv_w", "kv_b", "b_w_q", "b_b_q", "b_sinks",
                "b_w_out", "b_b_out", "ffn_w_gate_up", "ffn_w_down", "ple_w_up", "ple_w_gate", "ple_b_gate",
                "ln_gain", "ln_bias"]


def _as3(a):
    return a.reshape((-1,) + a.shape[-2:]) if a.ndim >= 3 else a.reshape((1,) + a.shape)


def _pad_lanes(v):
    n = v.shape[-1]
    return jnp.pad(v, ((0, 0), (0, (-n) % LANES)))


def _adam_small_fn(w, mom, vel, g):
    return _adam_fn(w, mom, vel, g, jnp.zeros_like(g))[1:]


def _sum_rows_fn(slots):
    acc = slots[0]
    for s in range(1, slots.shape[0]):
        acc = acc + slots[s]
    return (acc,)


def kernel(x, p, a_w_in, a_lower_bound, a_norm_gain, a_w_out, kv_w, kv_b, b_w_q, b_b_q, b_sinks, b_w_out, b_b_out, ffn_w_gate_up, ffn_w_down, ple_w_up, ple_w_gate, ple_b_gate, ln_gain, ln_bias, loss_target, m_a_w_in, m_a_lower_bound, m_a_norm_gain, m_a_w_out, m_kv_w, m_kv_b, m_b_w_q, m_b_b_q, m_b_sinks, m_b_w_out, m_b_b_out, m_ffn_w_gate_up, m_ffn_w_down, m_ple_w_up, m_ple_w_gate, m_ple_b_gate, m_ln_gain, m_ln_bias, v_a_w_in, v_a_lower_bound, v_a_norm_gain, v_a_w_out, v_kv_w, v_kv_b, v_b_w_q, v_b_b_q, v_b_sinks, v_b_w_out, v_b_b_out, v_ffn_w_gate_up, v_ffn_w_down, v_ple_w_up, v_ple_w_gate, v_ple_b_gate, v_ln_gain, v_ln_bias):
    args = dict(locals())
    wts = {n: args[n] for n in WEIGHT_ORDER}
    mom = {n: args["m_" + n] for n in WEIGHT_ORDER}
    vel = {n: args["v_" + n] for n in WEIGHT_ORDER}
    chip = 2 * lax.axis_index("x") + lax.axis_index("y")
    d = x.shape[-1]
    dq = d // N_CHIPS

    kind_of = dict(BIG)
    kind_of["small"] = "col"
    chip_arr = chip.reshape(1).astype(jnp.int32)
    small_pack = jnp.concatenate([wts[n].reshape(-1, dq) for n in SMALL_SHARDED], axis=0)[None]

    def place_item(key):
        n, layer = key
        if n == "small":
            return small_pack, 0, "col", F32
        return _as3(wts[n]), layer, kind_of[n], BF16

    gathers, where = [], {}
    for gi, group in enumerate(GATHER_GROUPS):
        prev = gathers[-1].token if gathers else None
        placed = _place([place_item(k) for k in group], chip_arr, name=f"place{gi}", after=prev)
        gathers.append(_Exchange("gather", [], placed, [kind_of[k[0]] for k in group],
                                 [0] * len(group), f"gather{gi}", after=prev,
                                 halves=[k[0] != "small" for k in group]))
        for k in group:
            where[k] = gi
    all_started = gathers[-1].token
    ready = {}

    fills = {}

    def pass_on(gi, after):
        if gi not in fills:
            group = GATHER_GROUPS[gi]
            outs = gathers[gi].wait(after)[1]
            split = [i for i, k in enumerate(group) if k[0] != "small"]
            fills[gi] = (outs, split, _SiblingFill([outs[i] for i in split], [kind_of[group[i][0]] for i in split],
                                                   f"fill{gi}"))

    def wget(name, layer, after):
        key = (name, layer)
        if key not in ready:
            gi = where[key]
            after = all_started if gi == 0 else after
            pass_on(gi, after)
            if 1 <= gi < len(GATHER_GROUPS) - 1:
                pass_on(gi + 1, after)
                after = fills[gi + 1][2].token
            outs, split, fill = fills[gi]
            for i, arr in zip(split, fill.wait(after)):
                outs[i] = arr
            for k, arr in zip(GATHER_GROUPS[gi], outs):
                ready[k] = arr
        return ready[key]

    small_full = wget("small", 0, None)[0]
    ln_gain_f = small_full[0:6].reshape(DEPTH, 3, d)
    ln_bias_f = small_full[6:12].reshape(DEPTH, 3, d)
    alb_f = small_full[12:14]

    group_of = {k: gi for gi, group in enumerate(SCATTER_GROUPS) for k in group}
    grads_done, zones, scatters = {}, {}, []

    def grad_sink(name, layer, grad):
        grads_done[(name, layer)] = grad
        if name not in zones:
            zones[name] = lax.empty((N_CHIPS,) + _as3(wts[name]).shape, BF16)
        gi = group_of[(name, layer)]
        group = SCATTER_GROUPS[gi]
        if not all(k in grads_done for k in group):
            return None
        ex = _Exchange("scatter", [grads_done[k] for k in group], [zones[k[0]] for k in group],
                       [kind_of[k[0]] for k in group], [k[1] for k in group], f"scatter{gi}")
        for k, zone in zip(group, ex.lands):
            zones[k[0]] = zone
        scatters.append((ex, group))
        return ex.token

    small = {}

    def small_sink(loss, gs):
        ln_g = jnp.concatenate([gs[f"ln_gain_{i}_{j}"] for i in range(DEPTH) for j in range(3)], axis=0)
        ln_b = jnp.concatenate([gs[f"ln_bias_{i}_{j}"] for i in range(DEPTH) for j in range(3)], axis=0)
        ple_bg = jnp.concatenate([gs[f"ple_b_{i}"] for i in range(DEPTH)], axis=0)
        small["list"] = [ln_g.reshape(1, -1), ln_b.reshape(1, -1), gs["alb"].reshape(1, -1), gs["norm_gain"],
                         gs["kv_b"], gs["b_q"], _pad_lanes(gs["sinks"]), gs["b_out"], ple_bg.reshape(1, -1), loss]
        small["gather"] = _GatherDevices(jnp.concatenate(small["list"], axis=1))
        return small["gather"].token

    loss, grad_x, gs = _local_step(
        x[0], p.reshape((p.shape[0],) + p.shape[2:]), loss_target[0], wget, grad_sink, ln_gain_f, ln_bias_f, alb_f, a_norm_gain, kv_b, b_b_q,
        b_sinks, b_b_out, ple_b_gate, small_sink)

    res = {}

    def arrive(batch, after):
        for ex, group in batch:
            srcs, outs = ex.wait(after, lands=[zones[k[0]] for k in group])
            for k, grad, zone in zip(group, srcs, outs):
                grads_done[k], zones[k[0]] = grad, zone

    def half_sums(names, tag, after):
        partial = []
        for n in names:
            own = [grads_done[(n, layer)] for layer in range(zones[n].shape[1])]
            partial.append(_sum_arrivals(zones[n], own, kind_of[n], chip_arr, f"sum_{n}", after=after))
        return _SiblingSwap(partial, tag, after=after)

    def update(names, swap, after):
        for n, own, sib in zip(names, *swap.wait(after)):
            shp = wts[n].shape
            flat = lambda a: a.reshape(-1, shp[-1])
            out = _rowwise(_adam_fn, [flat(wts[n]), flat(mom[n]), flat(vel[n]), own, sib], [],
                           [(own.shape, F32)] * 4, name=f"adam_{n}")
            res[n] = [o.reshape(shp) for o in out]
        return res[names[-1]][1]

    last_names = [k[0] for k in SCATTER_GROUPS[-1]]
    batches = [["ffn_w_gate_up"], [n for n, _ in BIG if n != "ffn_w_gate_up" and n not in last_names], last_names]
    arrive(scatters[:-1], grad_x)
    swap0 = half_sums(batches[0], "swap0", None)
    swap1 = half_sums(batches[1], "swap1", swap0.token)
    updated = update(batches[0], swap0, swap1.token)
    arrive(scatters[-1:], updated)
    swap2 = half_sums(batches[2], "swap2", swap1.token)
    updated = update(batches[1], swap1, swap2.token)
    update(batches[2], swap2, updated)

    small_list = small["list"]
    everyone = small["gather"].wait(grad_x)
    total, = _rowwise(_sum_rows_fn, [everyone], [], [(everyone.shape[1:], F32)], name="sum_small")
    offs, pos = [], 0
    for v in small_list:
        offs.append((pos, v.shape[1]))
        pos += v.shape[1]

    def seg(k):
        return total[0, offs[k][0]:offs[k][0] + offs[k][1]]

    def my_cols(full, rows):
        return lax.dynamic_slice_in_dim(full.reshape(rows, N_CHIPS, dq), chip, 1, axis=1).reshape(rows, dq)

    n_sink = b_sinks.shape[-1]
    small_grads = {
        "ln_gain": my_cols(seg(0), 6).reshape(ln_gain.shape), "ln_bias": my_cols(seg(1), 6).reshape(ln_bias.shape),
        "a_lower_bound": my_cols(seg(2), 2), "a_norm_gain": seg(3).reshape(a_norm_gain.shape),
        "kv_b": seg(4).reshape(kv_b.shape), "b_b_q": seg(5).reshape(b_b_q.shape),
        "b_sinks": seg(6)[:n_sink].reshape(b_sinks.shape), "b_b_out": seg(7).reshape(b_b_out.shape),
        "ple_b_gate": seg(8).reshape(ple_b_gate.shape)}
    names = SMALL_SHARDED + SMALL_REPLICATED
    pack = lambda dct: _pad_lanes(jnp.concatenate([dct[n].reshape(1, -1) for n in names], axis=1))
    g_pack = pack(small_grads)
    upd = _rowwise(_adam_small_fn, [pack(wts), pack(mom), pack(vel), g_pack], [], [(g_pack.shape, F32)] * 3,
                   name="adam_small")
    pos = 0
    for n in names:
        size = wts[n].size
        res[n] = [small_grads[n]] + [u[0, pos:pos + size].reshape(wts[n].shape) for u in upd]
        pos += size

    outs = [seg(9)[0], grad_x[None]]
    for k in range(4):
        outs += [res[n][k] for n in WEIGHT_ORDER]
    return tuple(outs)
```

```python
import functools

import jax
import jax.numpy as jnp
from jax import lax
from jax.experimental import pallas as pl
from jax.experimental.pallas import tpu as pltpu

F32 = jnp.float32
BF16 = jnp.bfloat16
MESH = pl.DeviceIdType.MESH

LANES = 128
HG_DK = 128
HG_CHUNK = 64
HG_SUB = 16
HG_ROWS = 512
HG_HEADS_PER_STEP = 2
LOG2_E = 1.4426950408889634
ATT_HD = 64
ATT_G = 4
WINDOW = 128
DEPTH = 2
ALPHA = (2.0 * DEPTH) ** 0.25
LN_EPS = 1e-5
RMS_EPS = 1e-6
ADAM_LR, ADAM_B1, ADAM_B2, ADAM_EPS, ADAM_WD, ADAM_STEP = 0.001, 0.9, 0.999, 1e-08, 0.01, 10
N_CHIPS = 4
N_DEV = 8
VMEM_LIMIT = 56 * 1024 * 1024
NEG = -1e30


def _pick(n, cap):
    best = None
    for d in range(LANES, min(n, cap) + 1, LANES):
        if n % d == 0:
            best = d
    return n if best is None else best


def _pick_rows(m, cap):
    best = None
    for d in range(16, min(m, cap) + 1, 16):
        if m % d == 0:
            best = d
    return m if best is None else best


def _params(sem):
    return pltpu.CompilerParams(dimension_semantics=sem, vmem_limit_bytes=VMEM_LIMIT)


def _zeros_index(ndim, grid_rank=3):
    return (lambda i, j, kk: (0,) * ndim) if grid_rank == 3 else (lambda kk, i: (0,) * ndim)


def _mm(a, b, *, name, la=None, lb=None, ta=False, tb=False, bias=None, add=None, out_dtype=F32,
        out_layers=None, out_layer=None, after=None, post=None, tile_cols=None, caps=(1024, 1536, 2048),
        a_parts=None, b_parts=None):
    ar, ac = a.shape[-2:]
    br, bc = b.shape[-2:]
    assert a_parts is None or (not ta and la is None and a.shape[0] == a_parts)
    assert b_parts is None or (not tb and lb is None and b.shape[0] == b_parts)
    m, k = (ac, ar) if ta else (ar, ac * (a_parts or 1))
    k2, n = (bc, br) if tb else (br, bc * (b_parts or 1))
    assert k == k2, (a.shape, b.shape, ta, tb)
    if post is not None:
        caps = (512, n if tile_cols is None else tile_cols, caps[2])
    tm, tn, tk = _pick(m, caps[0]), _pick(bc if b_parts else n, caps[1]), _pick(ac if a_parts else k, caps[2])
    assert post is None or tn == caps[1]
    nk = k // tk
    gi, gj = m // tm, n // tn
    a_bytes, b_bytes = m * k * a.dtype.itemsize, k * n * b.dtype.itemsize
    rows_outer = (a_bytes + b_bytes * (gi if gj * nk > 1 else 1)) <= (b_bytes + a_bytes * (gj if gi * nk > 1 else 1))
    k_outer = post is not None and nk > 1 and gj == 1
    grid = (nk, gi) if k_outer else (gi, gj, nk) if rows_outer else (gj, gi, nk)
    keep_at = ta and nk == 1 and gj > 1 and rows_outer

    def bs(block, idx, late=False):
        if k_outer:
            return pl.BlockSpec(block, lambda kk, i: idx(jnp.where(kk == nk - 1, i, 0) if late else i, 0, kk))
        return pl.BlockSpec(block, idx if rows_outer else (lambda q, p, kk: idx(p, q, kk)))

    def spec(block, idx, layer):
        if layer is None:
            return bs(block, idx)
        return bs((None,) + block, lambda i, j, kk: (layer,) + idx(i, j, kk))

    a_spec = spec((tk, tm), lambda i, j, kk: (kk, i), la) if ta else spec((tm, tk), lambda i, j, kk: (i, kk), la)
    b_spec = spec((tn, tk), lambda i, j, kk: (j, kk), lb) if tb else spec((tk, tn), lambda i, j, kk: (kk, j), lb)
    if a_parts:
        a_spec = bs((None, tm, tk), lambda i, j, kk: (kk // (ac // tk), i, kk % (ac // tk)))
    if b_parts:
        b_spec = bs((None, tk, tn), lambda i, j, kk: (j // (bc // tn), kk, j % (bc // tn)))
    in_specs, operands = [a_spec, b_spec], [a, b]
    if bias is not None:
        in_specs.append(bs((1, tn), lambda i, j, kk: (0, j)))
        operands.append(bias)
    if add is not None:
        in_specs.append(bs((tm, tn), lambda i, j, kk: (i, j), late=True))
        operands.append(add)
    if after is not None:
        in_specs.append(pl.BlockSpec(memory_space=pl.ANY))
        operands.append(after)
    dims = (((0 if ta else 1,), (1 if tb else 0,)), ((), ()))
    has_bias, has_add = bias is not None, add is not None
    if post is None:
        fn, rows, whole, outs, sums = None, [], [], [], []
        out_shape = jax.ShapeDtypeStruct((m, n) if out_layers is None else (out_layers, m, n), out_dtype)
        out_specs = spec((tm, tn), lambda i, j, kk: (i, j), out_layer)
    else:
        fn, rows, whole, outs, sums = post
        in_specs += [bs((tm, r.shape[-1] // gj), lambda i, j, kk: (i, j), late=True) for r in rows]
        in_specs += [pl.BlockSpec(tuple(w.shape), _zeros_index(w.ndim, len(grid))) for w in whole]
        operands += list(rows) + list(whole)
        out_shape = [jax.ShapeDtypeStruct(sh, dt) for sh, dt in list(outs) + list(sums)]
        out_specs = ([bs((tm, sh[-1] // gj), lambda i, j, kk: (i, j), late=True) for sh, _ in outs]
                     + [pl.BlockSpec(tuple(sh), _zeros_index(len(sh), len(grid))) for sh, _ in sums])
    n_in, n_extra, n_outs, n_sums = len(operands), len(rows) + len(whole), len(outs), len(sums)

    def body(*refs):
        a_ref, b_ref = refs[0], refs[1]
        pos = 2
        bias_ref = add_ref = None
        if has_bias:
            bias_ref = refs[pos]
            pos += 1
        if has_add:
            add_ref = refs[pos]
            pos += 1
        extra_refs = refs[n_in - n_extra:n_in]
        out_refs = refs[n_in:n_in + max(n_outs, 1)]
        sum_refs = refs[n_in + n_outs:n_in + n_outs + n_sums]
        acc_ref = refs[-1] if nk > 1 else None
        if keep_at:
            at_ref = refs[-1]

            @pl.when(pl.program_id(1) == 0)
            def _():
                at_ref[...] = a_ref[...].astype(BF16).T

            part = lax.dot_general(at_ref[...], b_ref[...].astype(BF16), (((1,), (1 if tb else 0,)), ((), ())),
                                   preferred_element_type=F32)
        else:
            part = lax.dot_general(a_ref[...].astype(BF16), b_ref[...].astype(BF16), dims,
                                   preferred_element_type=F32)

        def finish(total):
            if has_bias:
                total = total + bias_ref[...]
            if has_add:
                total = total + add_ref[...]
            if fn is None:
                out_refs[0][...] = total.astype(out_refs[0].dtype)
                return
            res = fn(total, *[r[...] for r in extra_refs])
            for ref, val in zip(out_refs, res[:n_outs]):
                ref[...] = val.astype(ref.dtype)
            if n_sums:
                @pl.when(pl.program_id(1 if k_outer or not rows_outer else 0) == 0)
                def _():
                    for ref in sum_refs:
                        ref[...] = jnp.zeros(ref.shape, ref.dtype)

                for ref, val in zip(sum_refs, res[n_outs:]):
                    ref[...] += val

        if nk == 1:
            finish(part)
        elif k_outer:
            kk = pl.program_id(0)
            rows_i = pl.ds(pl.multiple_of(pl.program_id(1) * tm, tm), tm)

            @pl.when(kk == 0)
            def _():
                acc_ref[rows_i, :] = part

            @pl.when(kk > 0)
            def _():
                acc_ref[rows_i, :] += part

            @pl.when(kk == nk - 1)
            def _():
                finish(acc_ref[rows_i, :])
        else:
            kk = pl.program_id(2)

            @pl.when(kk == 0)
            def _():
                acc_ref[...] = part

            @pl.when(kk > 0)
            def _():
                acc_ref[...] += part

            @pl.when(kk == nk - 1)
            def _():
                finish(acc_ref[...])

    return pl.pallas_call(
        body, name=name, grid=grid, in_specs=in_specs, out_specs=out_specs, out_shape=out_shape,
        scratch_shapes=([pltpu.VMEM((m, n) if k_outer else (tm, tn), F32)] if nk > 1
                        else [pltpu.VMEM((tm, tk), BF16)] if keep_at else []),
        compiler_params=_params(("arbitrary", "arbitrary") if k_outer
                                else ("arbitrary" if n_sums else "parallel", "arbitrary" if keep_at else "parallel",
                                      "arbitrary") if rows_outer
                                else ("parallel", "arbitrary" if n_sums else "parallel", "arbitrary")),
    )(*operands)


def _rowwise(fn, rows, whole, outs, sums=(), *, name, tm=256):
    m = rows[0].shape[-2]
    tm = _pick_rows(m, tm)
    n_rows, n_whole, n_outs, n_sums = len(rows), len(whole), len(outs), len(sums)

    def rspec(shape):
        lead = len(shape) - 2
        return pl.BlockSpec(tuple(shape[:-2]) + (tm, shape[-1]), lambda i: (0,) * lead + (i, 0))

    def wspec(shape):
        return pl.BlockSpec(tuple(shape), lambda i: (0,) * len(shape))

    def body(*refs):
        vals = [r[...] for r in refs[:n_rows + n_whole]]
        out_refs = refs[n_rows + n_whole:n_rows + n_whole + n_outs]
        sum_refs = refs[n_rows + n_whole + n_outs:]
        res = fn(*vals)
        for ref, val in zip(out_refs, res[:n_outs]):
            ref[...] = val.astype(ref.dtype)
        if n_sums:
            @pl.when(pl.program_id(0) == 0)
            def _():
                for ref in sum_refs:
                    ref[...] = jnp.zeros(ref.shape, ref.dtype)

            for ref, val in zip(sum_refs, res[n_outs:]):
                ref[...] += val

    result = pl.pallas_call(
        body, name=name, grid=(m // tm,),
        in_specs=[rspec(r.shape) for r in rows] + [wspec(w.shape) for w in whole],
        out_specs=[rspec(s) for s, _ in outs] + [wspec(s) for s, _ in sums],
        out_shape=[jax.ShapeDtypeStruct(s, d) for s, d in list(outs) + list(sums)],
        compiler_params=_params(("arbitrary",)),
    )(*rows, *whole)
    return result


def _sigmoid(v):
    return jax.nn.sigmoid(v)


def _col_sum(v):
    return jnp.sum(v, axis=0, keepdims=True)


def _ln_stats(z):
    mu = jnp.mean(z, axis=-1, keepdims=True)
    zc = z - mu
    var = jnp.mean(zc * zc, axis=-1, keepdims=True)
    rstd = lax.rsqrt(var + LN_EPS)
    return zc * rstd, rstd


def _ln_fwd_fn(xin, h, gain, bias):
    xhat, _ = _ln_stats(ALPHA * xin + h)
    y = xhat * gain + bias
    return y, y


def _ple_ln_fwd_fn(xin, pg, pu, gain, bias):
    xhat, _ = _ln_stats(ALPHA * xin + _sigmoid(pg) * pu)
    y = xhat * gain + bias
    return y, y


def _ln_dz(dy, z, gain):
    xhat, rstd = _ln_stats(z)
    dxhat = dy * gain
    dz = rstd * (dxhat - jnp.mean(dxhat, axis=-1, keepdims=True)
                 - xhat * jnp.mean(dxhat * xhat, axis=-1, keepdims=True))
    return dz, _col_sum(dy * xhat), _col_sum(dy)


def _ln_bwd_fn(dy, xin, h, gain):
    dz, dgain, dbias = _ln_dz(dy, ALPHA * xin + h, gain)
    return ALPHA * dz, dz, dgain, dbias, _col_sum(dz)


def _ple_ln_bwd_fn(dy, xin, pg, pu, gain):
    sg = _sigmoid(pg)
    dz, dgain, dbias = _ln_dz(dy, ALPHA * xin + sg * pu, gain)
    dpg = dz * pu * sg * (1.0 - sg)
    return ALPHA * dz, dpg, dz * sg, dgain, dbias, _col_sum(dpg)


def _swiglu_fwd_fn(gu):
    hid = gu.shape[-1] // 2
    gate, up = gu[:, :hid], gu[:, hid:]
    return gu, gate * _sigmoid(gate) * up


def _swiglu_bwd_fn(dact, gu):
    gu = gu.astype(F32)
    hid = gu.shape[-1] // 2
    gate, up = gu[:, :hid], gu[:, hid:]
    sg = _sigmoid(gate)
    dgate = dact * up * sg * (1.0 + gate * (1.0 - sg))
    dup = dact * gate * sg
    return (jnp.concatenate([dgate, dup], axis=-1),)


def _loss_fn(y, target):
    err = y - target
    inv = 1.0 / y.shape[-1]
    part = 0.5 * inv * jnp.sum(jnp.sum(err * err, axis=-1, keepdims=True), axis=0, keepdims=True)
    return err * inv, jnp.broadcast_to(part, (1, LANES))


def _adam_fn(w, mom, vel, p_own, p_sib):
    g = p_own.astype(F32) + p_sib.astype(F32)
    m_new = ADAM_B1 * mom + (1.0 - ADAM_B1) * g
    v_new = ADAM_B2 * vel + (1.0 - ADAM_B2) * (g * g)
    m_hat = m_new / (1.0 - ADAM_B1 ** ADAM_STEP)
    v_hat = v_new / (1.0 - ADAM_B2 ** ADAM_STEP)
    delta = -ADAM_LR * (m_hat / (jnp.sqrt(v_hat) + ADAM_EPS) + ADAM_WD * w)
    return g, delta, m_new, v_new


def _split2(x):
    hi = x.astype(BF16)
    return hi, (x - hi.astype(F32)).astype(BF16)


def _dot3(a, b, dims):
    a_hi, a_lo = _split2(a)
    b_hi, b_lo = _split2(b)
    dn = (dims, ((), ()))
    return (lax.dot_general(a_hi, b_hi, dn, preferred_element_type=F32)
            + (lax.dot_general(a_hi, b_lo, dn, preferred_element_type=F32)
               + lax.dot_general(a_lo, b_hi, dn, preferred_element_type=F32)))


def _tdot(mask01, b):
    m = mask01.astype(BF16)
    b_hi = b.astype(BF16)
    rest = b - b_hi.astype(F32)
    b_mid = rest.astype(BF16)
    b_lo = (rest - b_mid.astype(F32)).astype(BF16)
    dn = (((1,), (0,)), ((), ()))
    return (lax.dot_general(m, b_hi, dn, preferred_element_type=F32)
            + (lax.dot_general(m, b_mid, dn, preferred_element_type=F32)
               + lax.dot_general(m, b_lo, dn, preferred_element_type=F32)))


def _hdot(a, b):
    return _dot3(a, b, ((1,), (0,)))


def _hdot_nt(a, b):
    return _dot3(a, b, ((1,), (1,)))


def _hdot_tn(a, b):
    return _dot3(a, b, ((0,), (0,)))


def _dot(a, b):
    return lax.dot_general(a.astype(BF16), b.astype(BF16), (((1,), (0,)), ((), ())), preferred_element_type=F32)


def _dot_nt(a, b):
    return lax.dot_general(a.astype(BF16), b.astype(BF16), (((1,), (1,)), ((), ())), preferred_element_type=F32)


def _dot_tn(a, b):
    return lax.dot_general(a.astype(BF16), b.astype(BF16), (((0,), (0,)), ((), ())), preferred_element_type=F32)


def _hg_masks():
    c = HG_CHUNK
    row = lax.broadcasted_iota(jnp.int32, (c, c), 0)
    col = lax.broadcasted_iota(jnp.int32, (c, c), 1)
    base = row & (-HG_SUB)
    return row, col, base, col <= row, col < base


def _hg_gates(qr, fr, alb):
    lbound = _sigmoid(alb[0:1, :] - alb[1:2, :])
    sig = _sigmoid(fr)
    forget = lbound + (1.0 - lbound) * sig
    kk = (1.0 - lbound) * _sigmoid(-fr)
    qt = qr * _sigmoid(qr) * (HG_DK ** -0.5)
    return qt, kk, jnp.log(forget), lbound, sig, forget


def _hg_scores(qt, kk, g, scores=True):
    c, nsub = HG_CHUNK, HG_CHUNK // HG_SUB
    row, col, base, causal, below = _hg_masks()
    b = _tdot(causal, g)
    rr = _tdot(below, g)
    bq = b - rr
    qh = qt * jnp.exp(bq)
    edecs = [None]
    parts = [jnp.zeros((HG_SUB, c), F32)]
    for i in range(1, nsub):
        edec = jnp.exp(jnp.minimum(rr[i * HG_SUB:i * HG_SUB + 1, :] - b, 0.0))
        edecs.append(edec)
        if scores:
            parts.append(_dot_nt(qh[i * HG_SUB:(i + 1) * HG_SUB, :], kk * edec))
    q3 = qt.reshape(nsub, HG_SUB, HG_DK)
    if not scores:
        return None, b, bq, qh, edecs, (b.reshape(nsub, HG_SUB, HG_DK), q3, kk.reshape(nsub, HG_SUB, HG_DK))
    a = jnp.where(below, jnp.concatenate(parts, axis=0), 0.0)
    b2 = b * LOG2_E
    b3 = b2.reshape(nsub, HG_SUB, HG_DK)
    c3 = (b2 - jnp.log2(kk)).reshape(nsub, HG_SUB, HG_DK)
    for j in range(HG_SUB):
        ek = jnp.exp2(b3 - c3[:, j:j + 1, :])
        colv = jnp.sum(q3 * ek, axis=-1, keepdims=True).reshape(c, 1)
        a = jnp.where(col == base + j, colv, a)
    a = jnp.where(causal, a, 0.0)
    return a, b, bq, qh, edecs, None


def _hg_norm(o, gr, gain):
    r = lax.rsqrt(jnp.mean(o * o, axis=-1, keepdims=True) + RMS_EPS)
    sg = _sigmoid(gr)
    return o * r * gain, r, sg


def _hgrn2_fwd(proj, alb, gain, *, rb):
    m, d4 = proj.shape
    d = d4 // 4
    heads = d // HG_DK
    hp = HG_HEADS_PER_STEP
    rb = min(rb, m)
    cpb = rb // HG_CHUNK
    nrb = m // rb

    def body(q_ref, f_ref, v_ref, g_ref, alb_ref, gain_ref, o_ref, og_ref, st_ref, a_ref, state):
        @pl.when(pl.program_id(1) == 0)
        def _():
            state[...] = jnp.zeros(state.shape, F32)

        def chunk(ci, carry):
            sl = pl.ds(pl.multiple_of(ci * HG_CHUNK, HG_CHUNK), HG_CHUNK)
            for u in range(hp):
                ln = slice(u * HG_DK, (u + 1) * HG_DK)
                qt, kk, g, _, _, _ = _hg_gates(q_ref[sl, ln], f_ref[sl, ln], alb_ref[:, ln])
                v = v_ref[sl, ln]
                st = state[u]
                st_ref[u, ci] = st
                a, b, _, _, _, _ = _hg_scores(qt, kk, g)
                a_ref[u, ci] = a.astype(a_ref.dtype)
                o = _dot(a, v) + _dot_nt(qt * jnp.exp(b), st)
                b_last = b[HG_CHUNK - 1:HG_CHUNK, :]
                state[u] = st * jnp.exp(b_last) + _hdot_tn(v, kk * jnp.exp(b_last - b))
                o_ref[sl, ln] = o
                n, _, sg = _hg_norm(o, g_ref[sl, ln], gain_ref[...])
                og_ref[sl, ln] = (n * g_ref[sl, ln] * sg).astype(og_ref.dtype)
            return carry

        lax.fori_loop(0, cpb, chunk, 0)

    def col(cidx):
        return pl.BlockSpec((rb, hp * HG_DK), lambda h, r: (r, cidx * (heads // hp) + h))

    return pl.pallas_call(
        body, name="hgrn2_fwd", grid=(heads // hp, nrb),
        in_specs=[col(0), col(1), col(2), col(3),
                  pl.BlockSpec((2, hp * HG_DK), lambda h, r: (0, h)),
                  pl.BlockSpec((1, HG_DK), lambda h, r: (0, 0))],
        out_specs=[pl.BlockSpec((rb, hp * HG_DK), lambda h, r: (r, h)),
                   pl.BlockSpec((rb, hp * HG_DK), lambda h, r: (r, h)),
                   pl.BlockSpec((hp, cpb, HG_DK, HG_DK), lambda h, r: (h, r, 0, 0)),
                   pl.BlockSpec((hp, cpb, HG_CHUNK, HG_CHUNK), lambda h, r: (h, r, 0, 0))],
        out_shape=[jax.ShapeDtypeStruct((m, d), F32), jax.ShapeDtypeStruct((m, d), BF16),
                   jax.ShapeDtypeStruct((heads, m // HG_CHUNK, HG_DK, HG_DK), F32),
                   jax.ShapeDtypeStruct((heads, m // HG_CHUNK, HG_CHUNK, HG_CHUNK), BF16)],
        scratch_shapes=[pltpu.VMEM((hp, HG_DK, HG_DK), F32)],
        compiler_params=_params(("parallel", "arbitrary")),
    )(proj, proj, proj, proj, alb, gain)


def _hgrn2_bwd(proj, o_pre, states, scores, dog, alb, gain, *, rb):
    m, d4 = proj.shape
    d = d4 // 4
    heads = d // HG_DK
    rb = min(rb, m)
    cpb = rb // HG_CHUNK
    nrb = m // rb
    c, nsub = HG_CHUNK, HG_CHUNK // HG_SUB

    def body(q_ref, f_ref, v_ref, g_ref, o_ref, st_ref, a_ref, dog_ref, alb_ref, gain_ref,
             dp_ref, dalb_ref, dgain_ref, dstate, carry_ref):
        first = (pl.program_id(0) == 0) & (pl.program_id(1) == 0)

        @pl.when(first)
        def _():
            dgain_ref[...] = jnp.zeros(dgain_ref.shape, F32)

        @pl.when(pl.program_id(1) == 0)
        def _():
            dstate[...] = jnp.zeros(dstate.shape, F32)
            carry_ref[...] = jnp.zeros(carry_ref.shape, F32)
            dalb_ref[...] = jnp.zeros(dalb_ref.shape, F32)

        row, col, base, causal, below = _hg_masks()
        sub_iota = lax.broadcasted_iota(jnp.int32, (nsub, HG_SUB, HG_DK), 1)
        row_k = lax.broadcasted_iota(jnp.int32, (c, HG_DK), 0)
        upper = col >= row

        def chunk(step, carry):
            ci = cpb - 1 - step
            sl = pl.ds(pl.multiple_of(ci * HG_CHUNK, HG_CHUNK), HG_CHUNK)
            qr, fr, v, gr = q_ref[sl, :], f_ref[sl, :], v_ref[sl, :], g_ref[sl, :]
            qt, kk, g, lbound, sig, forget = _hg_gates(qr, fr, alb_ref[...])
            o = o_ref[sl, :]
            dogv = dog_ref[sl, :]
            gain_v = gain_ref[...]
            n, r, sg = _hg_norm(o, gr, gain_v)
            dgr = dogv * n * sg * (1.0 + gr * (1.0 - sg))
            dn = dogv * gr * sg
            dgain_ref[...] += _col_sum(dn * o * r)
            u = dn * gain_v
            d_o = r * u - o * (r * r * r) * jnp.mean(u * o, axis=-1, keepdims=True)
            st0 = st_ref[ci]
            dst = dstate[...]
            _, b, bq, qh, edecs, (b3, q3, k3) = _hg_scores(qt, kk, g, scores=False)
            a = a_ref[ci]
            eb = jnp.exp(b)
            b_last = b[c - 1:c, :]
            kdl_dec = jnp.exp(b_last - b)
            kdl = kk * kdl_dec
            d_a = jnp.where(causal, _dot_nt(d_o, v), 0.0)
            d_at = _dot_nt(v, d_o)
            dv = _dot_tn(a, d_o) + _dot_nt(kdl, dst)
            dq = eb * _hdot(d_o, st0)
            dk = _hdot(v, dst) * kdl_dec
            d_a_below = jnp.where(below, d_a, 0.0)
            dq_parts = [jnp.zeros((HG_SUB, HG_DK), F32)]
            for i in range(1, nsub):
                lo, hi = i * HG_SUB, (i + 1) * HG_SUB
                dq_parts.append(_hdot(d_a_below[lo:hi, :], kk * edecs[i]))
                gi = _hdot(d_at[:, lo:hi], qh[lo:hi, :])
                dk = dk + jnp.where(row_k < lo, edecs[i] * gi, 0.0)
            dq = dq + jnp.concatenate(dq_parts, axis=0) * jnp.exp(bq)
            dq3 = jnp.zeros((nsub, HG_SUB, HG_DK), F32)
            dk3 = jnp.zeros((nsub, HG_SUB, HG_DK), F32)
            d_diag = jnp.concatenate([d_a[i * HG_SUB:(i + 1) * HG_SUB, i * HG_SUB:(i + 1) * HG_SUB]
                                      for i in range(nsub)], axis=0).reshape(nsub, HG_SUB, HG_SUB)
            for j in range(HG_SUB):
                e = jnp.exp(jnp.minimum(b3 - b3[:, j:j + 1, :], 0.0))
                t1 = d_diag[:, :, j:j + 1] * e
                dq3 = dq3 + t1 * k3[:, j:j + 1, :]
                dk3 = jnp.where(sub_iota == j, jnp.sum(t1 * q3, axis=1, keepdims=True), dk3)
            dq = dq + dq3.reshape(c, HG_DK)
            dk = dk + dk3.reshape(c, HG_DK)
            dstate[...] = dst * jnp.exp(b_last) + _hdot_tn(d_o, qt * eb)
            dglog = _tdot(upper, qt * dq - kk * dk) + carry_ref[...]
            carry_ref[...] = dglog[0:1, :]
            dforget = dglog / forget
            one_m_lb = 1.0 - lbound
            dsig = (dforget - dk) * one_m_lb
            sneg = _sigmoid(-fr)
            dlb = _col_sum(dforget * (1.0 - sig) - dk * sneg)
            dalb0 = dlb * lbound * one_m_lb
            dalb_ref[...] += jnp.concatenate([dalb0, -dalb0], axis=0)
            sq = _sigmoid(qr)
            dp_ref[0, sl, :] = (dq * (HG_DK ** -0.5) * sq * (1.0 + qr * (1.0 - sq))).astype(dp_ref.dtype)
            dp_ref[1, sl, :] = (dsig * sig * (1.0 - sig)).astype(dp_ref.dtype)
            dp_ref[2, sl, :] = dv.astype(dp_ref.dtype)
            dp_ref[3, sl, :] = dgr.astype(dp_ref.dtype)
            return carry

        lax.fori_loop(0, cpb, chunk, 0, unroll=2)

    def rev(r):
        return nrb - 1 - r

    def col(cidx):
        return pl.BlockSpec((rb, HG_DK), lambda h, r: (rev(r), cidx * heads + h))

    def head_rows():
        return pl.BlockSpec((rb, HG_DK), lambda h, r: (rev(r), h))

    return pl.pallas_call(
        body, name="hgrn2_bwd", grid=(heads, nrb),
        in_specs=[col(0), col(1), col(2), col(3), head_rows(),
                  pl.BlockSpec((None, cpb, HG_DK, HG_DK), lambda h, r: (h, rev(r), 0, 0)),
                  pl.BlockSpec((None, cpb, HG_CHUNK, HG_CHUNK), lambda h, r: (h, rev(r), 0, 0)),
                  head_rows(),
                  pl.BlockSpec((2, HG_DK), lambda h, r: (0, h)),
                  pl.BlockSpec((1, HG_DK), lambda h, r: (0, 0))],
        out_specs=[pl.BlockSpec((4, rb, HG_DK), lambda h, r: (0, rev(r), h)),
                   pl.BlockSpec((2, HG_DK), lambda h, r: (0, h)),
                   pl.BlockSpec((1, HG_DK), lambda h, r: (0, 0))],
        out_shape=[jax.ShapeDtypeStruct((4, m, d), BF16), jax.ShapeDtypeStruct((2, d), F32),
                   jax.ShapeDtypeStruct((1, HG_DK), F32)],
        scratch_shapes=[pltpu.VMEM((HG_DK, HG_DK), F32), pltpu.VMEM((1, HG_DK), F32)],
        compiler_params=_params(("arbitrary", "arbitrary")),
    )(proj, proj, proj, proj, o_pre, states, scores, dog, alb, gain)


def _swa_probs(qh, kp, kc, sink, slope, has_prev, lse=None):
    rows = qh.shape[0]
    qi = lax.broadcasted_iota(jnp.int32, (rows, WINDOW), 0) & (WINDOW - 1)
    si = lax.broadcasted_iota(jnp.int32, (rows, WINDOW), 1)
    scale = ATT_HD ** -0.5
    dist_c = (qi - si).astype(F32)
    s_p = _dot_nt(qh, kp) * scale - slope * (dist_c + float(WINDOW))
    s_c = _dot_nt(qh, kc) * scale - slope * dist_c
    s_p = jnp.where((si > qi) & has_prev, s_p, NEG)
    s_c = jnp.where(si <= qi, s_c, NEG)
    if lse is not None:
        return jnp.exp(s_p - lse), jnp.exp(s_c - lse), jnp.exp(sink - lse), lse
    mx = jnp.maximum(jnp.maximum(jnp.max(s_p, axis=-1, keepdims=True), jnp.max(s_c, axis=-1, keepdims=True)), sink)
    e_p, e_c, e_s = jnp.exp(s_p - mx), jnp.exp(s_c - mx), jnp.exp(sink - mx)
    total = jnp.sum(e_p, axis=-1, keepdims=True) + jnp.sum(e_c, axis=-1, keepdims=True) + e_s
    inv = 1.0 / total
    return e_p * inv, e_c * inv, e_s * inv, mx + jnp.log(total)


def _slope(h, n_heads):
    return float(2.0 ** (-8.0 * (h + 1) / n_heads))


def _swa_group(ref_vals, sink_ref, kh, n_heads):
    heads = [kh * ATT_G + g for g in range(ATT_G)]
    stacked = [jnp.concatenate([v[:, h * ATT_HD:(h + 1) * ATT_HD] for h in heads], axis=0) for v in ref_vals]
    grp = lax.shift_right_logical(lax.broadcasted_iota(jnp.int32, (ATT_G * WINDOW, 1), 0), WINDOW.bit_length() - 1)
    slope = jnp.zeros((ATT_G * WINDOW, 1), F32)
    sink = jnp.zeros((ATT_G * WINDOW, 1), F32)
    for g, h in enumerate(heads):
        slope = jnp.where(grp == g, _slope(h, n_heads), slope)
        sink = jnp.where(grp == g, sink_ref[:, h:h + 1], sink)
    return stacked, slope, sink


def _swa_fwd(q, kv, sinks):
    m, d = q.shape
    n_heads = d // ATT_HD
    kvh = n_heads // ATT_G
    kd = kvh * ATT_HD
    nb = m // WINDOW

    def body(q_ref, kvp_ref, kvc_ref, sink_ref, o_ref, lse_ref):
        has_prev = pl.program_id(0) > 0
        qv, kvp, kvc = q_ref[...], kvp_ref[...], kvc_ref[...]
        lane_h = lax.broadcasted_iota(jnp.int32, (WINDOW, n_heads), 1)
        outs, lse_all = [], jnp.zeros((WINDOW, n_heads), F32)
        for kh in range(kvh):
            ks = slice(kh * ATT_HD, (kh + 1) * ATT_HD)
            vs = slice(kd + kh * ATT_HD, kd + (kh + 1) * ATT_HD)
            (q4,), slope, sink = _swa_group([qv], sink_ref, kh, n_heads)
            p_p, p_c, _, lse = _swa_probs(q4, kvp[:, ks], kvc[:, ks], sink, slope, has_prev)
            o4 = _dot(p_p, kvp[:, vs]) + _dot(p_c, kvc[:, vs])
            for g in range(ATT_G):
                rows = slice(g * WINDOW, (g + 1) * WINDOW)
                outs.append(o4[rows, :])
                lse_all = jnp.where(lane_h == kh * ATT_G + g, lse[rows, :], lse_all)
        o_ref[...] = jnp.concatenate(outs, axis=-1).astype(o_ref.dtype)
        lse_ref[...] = lse_all

    return pl.pallas_call(
        body, name="swa_fwd", grid=(nb,),
        in_specs=[pl.BlockSpec((WINDOW, d), lambda n: (n, 0)),
                  pl.BlockSpec((WINDOW, 2 * kd), lambda n: (jnp.maximum(n - 1, 0), 0)),
                  pl.BlockSpec((WINDOW, 2 * kd), lambda n: (n, 0)),
                  pl.BlockSpec((1, n_heads), lambda n: (0, 0))],
        out_specs=[pl.BlockSpec((WINDOW, d), lambda n: (n, 0)), pl.BlockSpec((WINDOW, n_heads), lambda n: (n, 0))],
        out_shape=[jax.ShapeDtypeStruct((m, d), BF16), jax.ShapeDtypeStruct((m, n_heads), F32)],
        compiler_params=_params(("arbitrary",)),
    )(q, kv, kv, sinks)


def _swa_bwd(q, kv, sinks, lse, dao):
    m, d = q.shape
    n_heads = d // ATT_HD
    kvh = n_heads // ATT_G
    kd = kvh * ATT_HD
    nb = m // WINDOW
    scale = ATT_HD ** -0.5

    def body(q_ref, kvp_ref, kvc_ref, sink_ref, lse_ref, do_ref, dq_ref, dkvc_ref, dkvp_ref, dqsum_ref, dsink_ref):
        @pl.when(pl.program_id(0) == 0)
        def _():
            dqsum_ref[...] = jnp.zeros(dqsum_ref.shape, F32)
            dsink_ref[...] = jnp.zeros(dsink_ref.shape, F32)

        has_prev = pl.program_id(0) > 0
        qv, kvp, kvc, dov = q_ref[...], kvp_ref[...], kvc_ref[...], do_ref[...]
        lane_h = lax.broadcasted_iota(jnp.int32, (1, n_heads), 1)
        dsink = jnp.zeros((1, n_heads), F32)
        dq_parts, dk_p, dk_c, dv_p, dv_c = [], [], [], [], []
        for kh in range(kvh):
            ks = slice(kh * ATT_HD, (kh + 1) * ATT_HD)
            vs = slice(kd + kh * ATT_HD, kd + (kh + 1) * ATT_HD)
            kp, kc, vp, vc = kvp[:, ks], kvc[:, ks], kvp[:, vs], kvc[:, vs]
            (q4, do4), slope, sink = _swa_group([qv, dov], sink_ref, kh, n_heads)
            lse4 = jnp.concatenate([lse_ref[:, kh * ATT_G + g:kh * ATT_G + g + 1] for g in range(ATT_G)], axis=0)
            p_p, p_c, p_s, _ = _swa_probs(q4, kp, kc, sink, slope, has_prev, lse=lse4)
            dp_p, dp_c = _dot_nt(do4, vp), _dot_nt(do4, vc)
            delta = jnp.sum(p_p * dp_p, axis=-1, keepdims=True) + jnp.sum(p_c * dp_c, axis=-1, keepdims=True)
            ds_p, ds_c = p_p * (dp_p - delta), p_c * (dp_c - delta)
            sink_term = p_s * delta
            dq4 = (_dot(ds_p, kp) + _dot(ds_c, kc)) * scale
            for g in range(ATT_G):
                rows = slice(g * WINDOW, (g + 1) * WINDOW)
                dsink = dsink + jnp.where(lane_h == kh * ATT_G + g, -_col_sum(sink_term[rows, :]), 0.0)
                dq_parts.append(dq4[rows, :])
            dk_p.append(_dot_tn(ds_p, q4) * scale)
            dk_c.append(_dot_tn(ds_c, q4) * scale)
            dv_p.append(_dot_tn(p_p, do4))
            dv_c.append(_dot_tn(p_c, do4))
        dq = jnp.concatenate(dq_parts, axis=-1)
        dq_ref[...] = dq.astype(dq_ref.dtype)
        dqsum_ref[...] += _col_sum(dq)
        dsink_ref[...] += dsink
        dkvc_ref[...] = jnp.concatenate(dk_c + dv_c, axis=-1)
        dkvp_ref[...] = jnp.concatenate(dk_p + dv_p, axis=-1)

    return pl.pallas_call(
        body, name="swa_bwd", grid=(nb,),
        in_specs=[pl.BlockSpec((WINDOW, d), lambda n: (n, 0)),
                  pl.BlockSpec((WINDOW, 2 * kd), lambda n: (jnp.maximum(n - 1, 0), 0)),
                  pl.BlockSpec((WINDOW, 2 * kd), lambda n: (n, 0)),
                  pl.BlockSpec((1, n_heads), lambda n: (0, 0)),
                  pl.BlockSpec((WINDOW, n_heads), lambda n: (n, 0)),
                  pl.BlockSpec((WINDOW, d), lambda n: (n, 0))],
        out_specs=[pl.BlockSpec((WINDOW, d), lambda n: (n, 0)),
                   pl.BlockSpec((WINDOW, 2 * kd), lambda n: (n, 0)),
                   pl.BlockSpec((WINDOW, 2 * kd), lambda n: (n, 0)),
                   pl.BlockSpec((1, d), lambda n: (0, 0)),
                   pl.BlockSpec((1, n_heads), lambda n: (0, 0))],
        out_shape=[jax.ShapeDtypeStruct((m, d), BF16), jax.ShapeDtypeStruct((m, 2 * kd), F32),
                   jax.ShapeDtypeStruct((m, 2 * kd), F32), jax.ShapeDtypeStruct((1, d), F32),
                   jax.ShapeDtypeStruct((1, n_heads), F32)],
        compiler_params=_params(("arbitrary",)),
    )(q, kv, kv, sinks, lse, dao)


def _kv_grad_combine(dkv_cur, dkv_prev):
    m, w = dkv_cur.shape
    nb = m // WINDOW

    def body(cur_ref, nxt_ref, o_ref, sum_ref):
        @pl.when(pl.program_id(0) == 0)
        def _():
            sum_ref[...] = jnp.zeros(sum_ref.shape, F32)

        total = cur_ref[...] + jnp.where(pl.program_id(0) < nb - 1, nxt_ref[...], 0.0)
        o_ref[...] = total.astype(o_ref.dtype)
        sum_ref[...] += _col_sum(total)

    return pl.pallas_call(
        body, name="kv_grad_combine", grid=(nb,),
        in_specs=[pl.BlockSpec((WINDOW, w), lambda n: (n, 0)),
                  pl.BlockSpec((WINDOW, w), lambda n: (jnp.minimum(n + 1, nb - 1), 0))],
        out_specs=[pl.BlockSpec((WINDOW, w), lambda n: (n, 0)), pl.BlockSpec((1, w), lambda n: (0, 0))],
        out_shape=[jax.ShapeDtypeStruct((m, w), BF16), jax.ShapeDtypeStruct((1, w), F32)],
        compiler_params=_params(("arbitrary",)),
    )(dkv_cur, dkv_prev)


def _row(v):
    return v.reshape(1, -1)


def _local_step(x, p, target, wget, grad_sink, ln_gain, ln_bias, alb, norm_gain, kv_b, b_q, sinks, b_out, ple_b,
                small_sink=None):
    gs = {}
    gains = ln_gain.reshape(DEPTH * 3, -1)
    biases = ln_bias.reshape(DEPTH * 3, -1)
    sd = x.shape
    pending = [None]

    def mm(a, b, lb=0, **kw):
        after, pending[0] = pending[0], None
        return _mm(a, b, lb=lb, after=after, **kw)

    def mm_ln(a, wt, xin, i, j, nm, bias=None, pu=None):
        r = 3 * i + j
        if pu is None:
            fn, rows = (lambda h, xv, g, bv: (h,) + _ln_fwd_fn(xv, h, g[r:r + 1], bv[r:r + 1])), [xin]
        else:
            fn = lambda h, xv, puv, g, bv: (h,) + _ple_ln_fwd_fn(xv, h, puv, g[r:r + 1], bv[r:r + 1])
            rows = [xin, pu]
        h, y, yb = _mm(a, wt, lb=0, bias=bias, name=nm,
                       post=(fn, rows, [gains, biases], [(sd, F32), (sd, F32), (sd, BF16)], []))
        return h, (y, yb)

    def mm_ln_bwd(a, wt, add, xin, h, i, j, nm):
        r = 3 * i + j
        dx_part, dh, dg, db, dhsum = mm(a, wt, tb=True, add=add, name=nm,
                                        post=(lambda dy, xv, hv, g: _ln_bwd_fn(dy, xv, hv, g[r:r + 1]), [xin, h],
                                              [gains], [(sd, F32), (sd, BF16)], [((1, sd[1]), F32)] * 3))
        gs[f"ln_gain_{i}_{j}"], gs[f"ln_bias_{i}_{j}"] = dg, db
        return dx_part, dh, dhsum

    def tail_fwd(xa, i):
        wgu = wget("ffn_w_gate_up", i, xa[1])
        hid2 = wgu.shape[-1]
        gu, act = _mm(xa[1], wgu, lb=0, name=f"ffn_up_swiglu{i}", tile_cols=hid2 // 2,
                      post=(_swiglu_fwd_fn, [], [], [((sd[0], hid2), BF16), ((sd[0], hid2 // 2), BF16)], []))
        f, xb = mm_ln(act, wget("ffn_w_down", i, act), xa[0], i, 1, f"ffn_down_ln{i}")
        pu = _mm(p, wget("ple_w_up", i, act), la=i, lb=0, name=f"ple_up{i}")
        pg, xc = mm_ln(xb[1], wget("ple_w_gate", i, act), xb[0], i, 2, f"ple_gate_ln{i}", bias=_row(ple_b[i]), pu=pu)
        return dict(xa=xa, gu=gu, act=act, f=f, xb=xb, pg=pg, pu=pu), xc

    def tail_bwd(head, sv, i, mix_in, mix_h):
        xa, xb = sv["xa"], sv["xb"]
        r = 3 * i + 2
        dxb_part, dpg, dpu, dg2, db2, dbg = head(
            lambda dy, xv, pgv, puv, g: _ple_ln_bwd_fn(dy, xv, pgv, puv, g[r:r + 1]), [xb[0], sv["pg"], sv["pu"]],
            [gains], [(sd, F32), (sd, BF16), (sd, BF16)], [((1, sd[1]), F32)] * 3)[:6]
        gs[f"ple_b_{i}"] = dbg
        gs[f"ln_gain_{i}_2"], gs[f"ln_bias_{i}_2"] = dg2, db2
        grad_of("ple_w_gate", i, xb[1], dpg)
        grad_of("ple_w_up", i, p, dpu, la=i)
        dxa_part, df, _ = mm_ln_bwd(dpg, wget("ple_w_gate", i, None), dxb_part, xa[0], sv["f"], i, 1,
                                    f"ple_gate_dx_ln{i}")
        grad_of("ffn_w_down", i, sv["act"], df)
        gu = sv["gu"]
        dgu, = mm(df, wget("ffn_w_down", i, None), tb=True, name=f"ffn_down_dx_swiglu{i}", tile_cols=gu.shape[1] // 4,
                  post=(_swiglu_bwd_fn, [gu], [], [(gu.shape, BF16)], []))
        grad_of("ffn_w_gate_up", i, xa[1], dgu)
        return mm_ln_bwd(dgu, wget("ffn_w_gate_up", i, None), dxa_part, mix_in, mix_h, i, 0, f"ffn_up_dx_ln{i}")

    def grad_of(nm, i, act, dout, la=None, b_parts=None):
        grad = mm(act, dout, la=la, lb=None, ta=True, out_dtype=BF16, out_layers=1, out_layer=0,
                  name=f"grad_{nm}{i}", b_parts=b_parts)
        token = grad_sink(nm, i, grad)
        if token is not None:
            pending[0] = token

    proj = _mm(x, wget("a_w_in", 0, None), lb=0, name="hg_proj")
    o_pre, og, states, scores = _hgrn2_fwd(proj, alb, norm_gain, rb=HG_ROWS)
    h0, x1 = mm_ln(og, wget("a_w_out", 0, og), x, 0, 0, "hg_out_ln")
    sv0, x3 = tail_fwd(x1, 0)
    kv = _mm(x3[1], wget("kv_w", 0, x3[1]), lb=0, bias=_row(kv_b), out_dtype=BF16, name="kv_proj")
    q = _mm(x3[1], wget("b_w_q", 0, x3[1]), lb=0, bias=b_q, out_dtype=BF16, name="q_proj")
    ao, lse = _swa_fwd(q, kv, sinks)
    h1, x4 = mm_ln(ao, wget("b_w_out", 0, x3[1]), x3[0], 1, 0, "att_out_ln", bias=b_out)
    sv1, y = tail_fwd(x4, 1)

    loss_box = []

    def loss_head(fn, rows, whole, outs, sums):
        def with_loss(yv, tv, *rest):
            dy, part = _loss_fn(yv, tv)
            return fn(dy, *rest) + (part,)

        res = _rowwise(with_loss, [y[0], target] + rows, whole, outs, list(sums) + [((1, LANES), F32)],
                       name="loss_ln_ple_bwd1")
        loss_box.append(res[-1])
        return res

    dx3_part, dh1, dh1sum = tail_bwd(loss_head, sv1, 1, x3[0], h1)
    loss = loss_box[0]
    gs["b_out"] = dh1sum
    grad_of("b_w_out", 0, ao, dh1)
    dao = mm(dh1, wget("b_w_out", 0, None), tb=True, out_dtype=BF16, name="att_out_dx")
    dq, dkv_cur, dkv_prev, dqsum, dsinks = _swa_bwd(q, kv, sinks, lse, dao)
    gs["b_q"], gs["sinks"] = dqsum, dsinks
    dkv, dkvsum = _kv_grad_combine(dkv_cur, dkv_prev)
    gs["kv_b"] = dkvsum
    grad_of("b_w_q", 0, x3[1], dq)
    grad_of("kv_w", 0, x3[1], dkv)
    dx3 = mm(dq, wget("b_w_q", 0, None), tb=True, add=dx3_part, name="q_proj_dx")

    def kv_head(*post):
        return mm(dkv, wget("kv_w", 0, None), tb=True, add=dx3, name="kv_proj_dx_ln_ple_bwd0", post=post)

    dx_part, dh0, _ = tail_bwd(kv_head, sv0, 0, x, h0)
    grad_of("a_w_out", 0, og, dh0)
    dog = mm(dh0, wget("a_w_out", 0, None), tb=True, name="hg_out_dx")
    dproj, dalb, dgain = _hgrn2_bwd(proj, o_pre, states, scores, dog, alb, norm_gain, rb=HG_ROWS)
    gs["alb"], gs["norm_gain"] = dalb, dgain
    if small_sink is not None:
        pending[0] = small_sink(loss, gs)
    grad_of("a_w_in", 0, x, dproj, b_parts=4)
    grad_x = mm(dproj, wget("a_w_in", 0, None), tb=True, add=dx_part, name="hg_proj_dx", a_parts=4)
    return loss, grad_x, gs


HBM_SPEC = pl.BlockSpec(memory_space=pl.ANY)
HBM_ONLY = pl.BlockSpec(memory_space=pltpu.HBM)
SEM_SPEC = pl.BlockSpec(memory_space=pltpu.SEMAPHORE)
SIDE_EFFECT = pltpu.SideEffectType.DATAFLOW_SIDE_EFFECTING


def _slot(kind, j):
    return (j % 2) * 2 + j // 2 if kind == "colp" else j


def _piece(ref, kind, j):
    _, r, c = ref.shape
    if kind == "row":
        return ref.at[:, pl.ds(j * (r // N_CHIPS), r // N_CHIPS), :]
    return ref.at[:, :, pl.ds(_slot(kind, j) * (c // N_CHIPS), c // N_CHIPS)]


def _piece_dyn(ref, kind, j):
    _, r, c = ref.shape
    if kind == "row":
        return ref.at[:, pl.ds(pl.multiple_of(j * (r // N_CHIPS), 16), r // N_CHIPS), :]
    return ref.at[:, :, pl.ds(pl.multiple_of(_slot(kind, j) * (c // N_CHIPS), LANES), c // N_CHIPS)]


def _chip_of(j, c):
    return (j // 2, j % 2, c)


def _in_hbm(a):
    return pltpu.with_memory_space_constraint(a, pltpu.HBM)


PLACE_STEPS = 4


def _place(items, chip, *, name, after=None):
    n = len(items)
    in_specs, out_specs, out_shapes, blocks = [], [], [], []
    for src, layer, kind, out_dtype in items:
        _, r, c = src.shape
        nb = max(k for k in (1, 2, PLACE_STEPS) if r % (16 * k) == 0 or k == 1)
        blocks.append(nb)

        def src_idx(i, chip_ref, layer=layer, nb=nb):
            return (layer, jnp.minimum(i, nb - 1), 0)

        def full_idx(i, chip_ref, kind=kind, nb=nb):
            ib = jnp.minimum(i, nb - 1)
            return (0, chip_ref[0] * nb + ib, 0) if kind == "row" else (0, ib, _slot(kind, chip_ref[0]))

        in_specs.append(pl.BlockSpec((None, r // nb, c), src_idx))
        out_specs.append(pl.BlockSpec((None, r // nb, c), full_idx))
        out_shapes.append(jax.ShapeDtypeStruct((1, r * N_CHIPS, c) if kind == "row" else (1, r, c * N_CHIPS),
                                               out_dtype))
    operands = [it[0] for it in items]
    if after is not None:
        in_specs.append(HBM_SPEC)
        operands.append(after)

    def body(chip_ref, *refs):
        for a in range(n):
            refs[len(refs) - n + a][...] = refs[a][...].astype(refs[len(refs) - n + a].dtype)

    return pl.pallas_call(
        body, name=name,
        grid_spec=pltpu.PrefetchScalarGridSpec(num_scalar_prefetch=1, grid=(PLACE_STEPS,), in_specs=in_specs,
                                               out_specs=out_specs),
        out_shape=out_shapes,
        compiler_params=_params(("arbitrary",)),
    )(chip, *operands)


def _half(ref, c):
    h = ref.shape[1] // 2
    start = c * h if isinstance(c, int) else pl.multiple_of(c * h, 16)
    return ref.at[:, pl.ds(start, h), :]


def _sibling_handshake():
    barrier = pltpu.get_barrier_semaphore()
    sibling = (lax.axis_index("x"), lax.axis_index("y"), 1 - lax.axis_index("c"))
    pl.semaphore_signal(barrier, inc=1, device_id=sibling, device_id_type=MESH)
    pl.semaphore_wait(barrier, 1)


class _SiblingFill:
    def __init__(self, lands, kinds, name, collective_id):
        self.kinds, self.name, self.n = kinds, name, len(lands)
        n = self.n
        sem_shape = pltpu.SemaphoreType.DMA((n * N_CHIPS,))

        def body(*refs):
            land_refs, send_sems, recv_sems, token = refs[:n], refs[n], refs[n + 1], refs[-1]
            _sibling_handshake()
            for cp in self._copies(land_refs, send_sems, recv_sems):
                cp.start()
            token[...] = jnp.zeros(token.shape, token.dtype)

        outs = pl.pallas_call(
            body, name=name + "_start",
            in_specs=[HBM_ONLY] * n,
            out_specs=[SEM_SPEC, SEM_SPEC] + [HBM_ONLY] * n + [pl.BlockSpec(memory_space=pltpu.VMEM)],
            out_shape=[sem_shape, sem_shape] + [pltpu.HBM(a.shape, a.dtype) for a in lands]
                      + [jax.ShapeDtypeStruct((8, LANES), F32)],
            input_output_aliases={i: i + 2 for i in range(n)},
            compiler_params=pltpu.CompilerParams(has_side_effects=SIDE_EFFECT, collective_id=collective_id),
        )(*[_in_hbm(a) for a in lands])
        self.send_sems, self.recv_sems, self.lands, self.token = outs[0], outs[1], list(outs[2:2 + n]), outs[-1]

    def _copies(self, land_refs, send_sems, recv_sems):
        x, y, c = lax.axis_index("x"), lax.axis_index("y"), lax.axis_index("c")
        me = 2 * x + y
        copies = []
        for a in range(self.n):
            for k in range(1, N_CHIPS):
                t = (me + k) % N_CHIPS
                slice_t = _piece_dyn(land_refs[a], self.kinds[a], t)
                got = _half(slice_t, c)
                copies.append(pltpu.make_async_remote_copy(
                    src_ref=got, dst_ref=got, send_sem=send_sems.at[a * N_CHIPS + k],
                    recv_sem=recv_sems.at[a * N_CHIPS + k], device_id=(x, y, 1 - c), device_id_type=MESH))
        return copies

    def wait(self, after):
        n = self.n

        def body(*refs):
            land_refs, send_sems, recv_sems = refs[:n], refs[n], refs[n + 1]
            for cp in self._copies(land_refs, send_sems, recv_sems):
                cp.wait_send()
                cp.wait_recv()

        operands = [_in_hbm(a) for a in self.lands] + [self.send_sems, self.recv_sems]
        in_specs = [HBM_ONLY] * n + [SEM_SPEC, SEM_SPEC]
        if after is not None:
            operands.append(after)
            in_specs.append(HBM_SPEC)
        outs = pl.pallas_call(
            body, name=self.name + "_wait",
            in_specs=in_specs, out_specs=[HBM_ONLY] * n,
            out_shape=[pltpu.HBM(a.shape, a.dtype) for a in self.lands],
            input_output_aliases={i: i for i in range(n)},
            compiler_params=pltpu.CompilerParams(has_side_effects=SIDE_EFFECT),
        )(*operands)
        return list(outs)


class _Exchange:
    def __init__(self, mode, srcs, lands, kinds, layers, name, after=None, halves=None):
        self.mode, self.kinds, self.layers, self.name, self.n = mode, kinds, layers, name, len(lands)
        self.halves = halves if halves is not None else [False] * len(lands)
        n, ns = self.n, len(srcs)
        n_in = ns + n + (after is not None)
        sem_shape = pltpu.SemaphoreType.DMA((n * N_CHIPS,))

        def body(*refs):
            src_refs, land_refs = refs[:ns], refs[ns:ns + n]
            send_sems, recv_sems = refs[n_in], refs[n_in + 1]
            token = refs[-1]
            c = lax.axis_index("c")
            me = 2 * lax.axis_index("x") + lax.axis_index("y")
            for j in range(N_CHIPS):
                @pl.when(me == j)
                def _():
                    for a in range(n):
                        for t in range(N_CHIPS):
                            if t != j:
                                src, dst = self._ends(src_refs, land_refs, a, j, t, c)
                                pltpu.make_async_remote_copy(
                                    src_ref=src, dst_ref=dst, send_sem=send_sems.at[a * N_CHIPS + t],
                                    recv_sem=recv_sems.at[a * N_CHIPS + j],
                                    device_id=_chip_of(t, c), device_id_type=MESH).start()
            token[...] = jnp.zeros(token.shape, token.dtype)

        arrays = list(srcs) + list(lands)
        operands = [_in_hbm(a) for a in arrays]
        in_specs = [HBM_ONLY] * (ns + n)
        if after is not None:
            operands.append(after)
            in_specs.append(HBM_SPEC)
        outs = pl.pallas_call(
            body, name=name + "_start",
            in_specs=in_specs,
            out_specs=[SEM_SPEC, SEM_SPEC] + [HBM_ONLY] * (ns + n) + [pl.BlockSpec(memory_space=pltpu.VMEM)],
            out_shape=[sem_shape, sem_shape] + [pltpu.HBM(a.shape, a.dtype) for a in arrays]
                      + [jax.ShapeDtypeStruct((8, LANES), F32)],
            input_output_aliases={i: i + 2 for i in range(ns + n)},
            compiler_params=pltpu.CompilerParams(has_side_effects=SIDE_EFFECT),
        )(*operands)
        self.send_sems, self.recv_sems = outs[0], outs[1]
        self.srcs, self.lands = list(outs[2:2 + ns]), list(outs[2 + ns:2 + ns + n])
        self.token = outs[-1]

    def _ends(self, src_refs, land_refs, a, me_j, peer, c):
        if self.mode == "gather":
            mine = _piece(land_refs[a], self.kinds[a], me_j)
            if self.halves[a]:
                mine = _half(mine, c)
            return mine, mine
        return _piece(src_refs[a], self.kinds[a], peer), land_refs[a].at[me_j, pl.ds(self.layers[a], 1)]

    def wait(self, after, lands=None):
        n, ns = self.n, len(self.srcs)
        lands = self.lands if lands is None else lands

        def body(*refs):
            src_refs, land_refs = refs[:ns], refs[ns:ns + n]
            send_sems, recv_sems = refs[ns + n], refs[ns + n + 1]
            c = lax.axis_index("c")
            me = 2 * lax.axis_index("x") + lax.axis_index("y")
            for j in range(N_CHIPS):
                @pl.when(me != j)
                def _():
                    for a in range(n):
                        sent, _ = self._ends(src_refs, land_refs, a, 0, j, c)
                        _, landed = self._ends(src_refs, land_refs, a, j, 0, c)
                        cp = pltpu.make_async_remote_copy(
                            src_ref=sent, dst_ref=landed, send_sem=send_sems.at[a * N_CHIPS + j],
                            recv_sem=recv_sems.at[a * N_CHIPS + j],
                            device_id=_chip_of(j, c), device_id_type=MESH)
                        cp.wait_send()
                        cp.wait_recv()

        arrays = self.srcs + list(lands)
        operands = [_in_hbm(a) for a in arrays] + [self.send_sems, self.recv_sems]
        in_specs = [HBM_ONLY] * (ns + n) + [SEM_SPEC, SEM_SPEC]
        if after is not None:
            operands.append(after)
            in_specs.append(HBM_SPEC)
        outs = pl.pallas_call(
            body, name=self.name + "_wait",
            in_specs=in_specs, out_specs=[HBM_ONLY] * (ns + n),
            out_shape=[pltpu.HBM(a.shape, a.dtype) for a in arrays],
            input_output_aliases={i: i for i in range(ns + n)},
            compiler_params=pltpu.CompilerParams(has_side_effects=SIDE_EFFECT),
        )(*operands)
        return list(outs[:ns]), list(outs[ns:])


def _sum_arrivals(zone, own_grads, kind, chip, name, after=None):
    _, layers, r, c = zone.shape
    tm = _pick_rows(r, 256)
    nb = r // tm

    def own_idx(l, i, chip_ref):
        return (0, chip_ref[0] * nb + i, 0) if kind == "row" else (0, i, _slot(kind, chip_ref[0]))

    def slot_idx(k):
        return lambda l, i, chip_ref: (jnp.where(chip_ref[0] == k, (k + 1) % N_CHIPS, k), l, i, 0)

    in_specs = [pl.BlockSpec((None, None, tm, c), slot_idx(k)) for k in range(N_CHIPS)]
    in_specs += [pl.BlockSpec((None, tm, c), own_idx) for _ in own_grads]
    operands = [zone] * N_CHIPS + list(own_grads)
    if after is not None:
        in_specs.append(HBM_SPEC)
        operands.append(after)

    def body(chip_ref, *refs):
        slot_refs, own_refs, o_ref = refs[:N_CHIPS], refs[N_CHIPS:N_CHIPS + layers], refs[-1]
        own = own_refs[0][...]
        for u in range(1, layers):
            own = jnp.where(pl.program_id(0) == u, own_refs[u][...], own)
        acc = None
        for k in range(N_CHIPS):
            term = jnp.where(chip_ref[0] == k, own, slot_refs[k][...]).astype(F32)
            acc = term if acc is None else acc + term
        o_ref[...] = acc.astype(o_ref.dtype)

    return pl.pallas_call(
        body, name=name,
        grid_spec=pltpu.PrefetchScalarGridSpec(
            num_scalar_prefetch=1, grid=(layers, nb), in_specs=in_specs,
            out_specs=pl.BlockSpec((tm, c), lambda l, i, chip_ref: (l * nb + i, 0))),
        out_shape=jax.ShapeDtypeStruct((layers * r, c), BF16),
        compiler_params=_params(("arbitrary", "arbitrary")),
    )(chip, *operands)


class _SiblingSwap:
    def __init__(self, arrays, name, collective_id, after=None):
        self.name, self.n = name, len(arrays)
        n = self.n
        n_in = n + (after is not None)
        sem_shape = pltpu.SemaphoreType.DMA((n,))

        def body(*refs):
            ins, send_sems, recv_sems = refs[:n], refs[n_in], refs[n_in + 1]
            theirs, token = refs[n_in + 2 + n:n_in + 2 + 2 * n], refs[-1]
            _sibling_handshake()
            for cp in self._copies(ins, theirs, send_sems, recv_sems):
                cp.start()
            token[...] = jnp.zeros(token.shape, token.dtype)

        operands, in_specs = [_in_hbm(a) for a in arrays], [HBM_ONLY] * n
        if after is not None:
            operands.append(after)
            in_specs.append(HBM_SPEC)
        outs = pl.pallas_call(
            body, name=name + "_start",
            in_specs=in_specs,
            out_specs=[SEM_SPEC, SEM_SPEC] + [HBM_ONLY] * (2 * n) + [pl.BlockSpec(memory_space=pltpu.VMEM)],
            out_shape=[sem_shape, sem_shape] + [pltpu.HBM(a.shape, a.dtype) for a in arrays] * 2
                      + [jax.ShapeDtypeStruct((8, LANES), F32)],
            input_output_aliases={i: i + 2 for i in range(n)},
            compiler_params=pltpu.CompilerParams(has_side_effects=SIDE_EFFECT, collective_id=collective_id),
        )(*operands)
        self.send_sems, self.recv_sems = outs[0], outs[1]
        self.mine, self.theirs, self.token = list(outs[2:2 + n]), list(outs[2 + n:2 + 2 * n]), outs[-1]

    def _copies(self, mine, theirs, send_sems, recv_sems):
        sibling = (lax.axis_index("x"), lax.axis_index("y"), 1 - lax.axis_index("c"))
        return [pltpu.make_async_remote_copy(src_ref=mine[a], dst_ref=theirs[a], send_sem=send_sems.at[a],
                                             recv_sem=recv_sems.at[a], device_id=sibling, device_id_type=MESH)
                for a in range(self.n)]

    def wait(self, after):
        n = self.n

        def body(*refs):
            for cp in self._copies(refs[:n], refs[n:2 * n], refs[2 * n], refs[2 * n + 1]):
                cp.wait_send()
                cp.wait_recv()

        arrays = self.mine + self.theirs
        outs = pl.pallas_call(
            body, name=self.name + "_wait",
            in_specs=[HBM_ONLY] * (2 * n) + [SEM_SPEC, SEM_SPEC, HBM_SPEC], out_specs=[HBM_ONLY] * (2 * n),
            out_shape=[pltpu.HBM(a.shape, a.dtype) for a in arrays],
            input_output_aliases={i: i for i in range(2 * n)},
            compiler_params=pltpu.CompilerParams(has_side_effects=SIDE_EFFECT),
        )(*[_in_hbm(a) for a in arrays], self.send_sems, self.recv_sems, after)
        return list(outs[:n]), list(outs[n:])


class _GatherDevices:
    def __init__(self, vec):
        sem_shape = pltpu.SemaphoreType.DMA((N_DEV,))

        def body(in_ref, send_sems, recv_sems, vec_ref, out_ref, token):
            for cp in self._copies(in_ref, out_ref, send_sems, recv_sems):
                cp.start()
            token[...] = jnp.zeros(token.shape, token.dtype)

        outs = pl.pallas_call(
            body, name="gather_small_start",
            in_specs=[HBM_ONLY],
            out_specs=[SEM_SPEC, SEM_SPEC, HBM_ONLY, HBM_ONLY, pl.BlockSpec(memory_space=pltpu.VMEM)],
            out_shape=[sem_shape, sem_shape, pltpu.HBM(vec.shape, vec.dtype),
                       pltpu.HBM((N_DEV,) + vec.shape, vec.dtype), jax.ShapeDtypeStruct((8, LANES), F32)],
            input_output_aliases={0: 2},
            compiler_params=pltpu.CompilerParams(has_side_effects=SIDE_EFFECT),
        )(_in_hbm(vec))
        self.send_sems, self.recv_sems, self.vec, self.rows, self.token = outs

    def _copies(self, in_ref, out_ref, send_sems, recv_sems):
        x, y, c = lax.axis_index("x"), lax.axis_index("y"), lax.axis_index("c")
        me = 4 * x + 2 * y + c
        copies = [pltpu.make_async_copy(in_ref, out_ref.at[me], recv_sems.at[0])]
        for rel in range(1, N_DEV):
            peer = (x ^ (rel >> 2), y ^ ((rel >> 1) & 1), c ^ (rel & 1))
            copies.append(pltpu.make_async_remote_copy(
                src_ref=in_ref, dst_ref=out_ref.at[me], send_sem=send_sems.at[rel], recv_sem=recv_sems.at[rel],
                device_id=peer, device_id_type=MESH))
        return copies

    def wait(self, after):
        def body(vec_ref, rows_ref, send_sems, recv_sems, after_ref, vec_out, rows_out):
            copies = self._copies(vec_ref, rows_ref, send_sems, recv_sems)
            copies[0].wait()
            for cp in copies[1:]:
                cp.wait_send()
                cp.wait_recv()

        outs = pl.pallas_call(
            body, name="gather_small_wait",
            in_specs=[HBM_ONLY, HBM_ONLY, SEM_SPEC, SEM_SPEC, HBM_SPEC], out_specs=[HBM_ONLY, HBM_ONLY],
            out_shape=[pltpu.HBM(self.vec.shape, self.vec.dtype), pltpu.HBM(self.rows.shape, self.rows.dtype)],
            input_output_aliases={0: 0, 1: 1},
            compiler_params=pltpu.CompilerParams(has_side_effects=SIDE_EFFECT),
        )(_in_hbm(self.vec), _in_hbm(self.rows), self.send_sems, self.recv_sems, after)
        return outs[1]


BIG = [("a_w_in", "col"), ("a_w_out", "row"), ("kv_w", "row"), ("b_w_q", "row"), ("b_w_out", "row"),
       ("ffn_w_gate_up", "colp"), ("ffn_w_down", "row"), ("ple_w_up", "col"), ("ple_w_gate", "row")]
GATHER_GROUPS = [[("a_w_in", 0), ("small", 0)], [("a_w_out", 0), ("ffn_w_gate_up", 0)],
                 [("ffn_w_down", 0), ("ple_w_gate", 0), ("ple_w_up", 0)], [("kv_w", 0), ("b_w_q", 0), ("b_w_out", 0)],
                 [("ffn_w_gate_up", 1)], [("ffn_w_down", 1), ("ple_w_gate", 1), ("ple_w_up", 1)]]
SCATTER_GROUPS = [[("ple_w_gate", 1), ("ple_w_up", 1), ("ffn_w_down", 1)], [("ffn_w_gate_up", 1)],
                  [("b_w_out", 0), ("b_w_q", 0), ("kv_w", 0)], [("ple_w_gate", 0), ("ple_w_up", 0), ("ffn_w_down", 0)],
                  [("ffn_w_gate_up", 0), ("a_w_out", 0)], [("a_w_in", 0)]]
SMALL_SHARDED = ["ln_gain", "ln_bias", "a_lower_bound"]
SMALL_REPLICATED = ["a_norm_gain", "kv_b", "b_b_q", "b_sinks", "b_b_out", "ple_b_gate"]
WEIGHT_ORDER = ["a_w_in", "a_lower_bound", "a_norm_gain", "a_w_out", "kv_w", "kv_b", "b_w_q", "b_b_q", "b_sinks",
                "b_w_out", "b_b_out", "ffn_w_gate_up", "ffn_w_down", "ple_w_up", "ple_w_gate", "ple_b_gate",
                "ln_gain", "ln_bias"]


def _as3(a):
    return a.reshape((-1,) + a.shape[-2:]) if a.ndim >= 3 else a.reshape((1,) + a.shape)


def _pad_lanes(v):
    n = v.shape[-1]
    return jnp.pad(v, ((0, 0), (0, (-n) % LANES)))


def _adam_small_fn(w, mom, vel, g):
    return _adam_fn(w, mom, vel, g, jnp.zeros_like(g))[1:]


def _sum_rows_fn(slots):
    acc = slots[0]
    for s in range(1, slots.shape[0]):
        acc = acc + slots[s]
    return (acc,)


def kernel(x, p, a_w_in, a_lower_bound, a_norm_gain, a_w_out, kv_w, kv_b, b_w_q, b_b_q, b_sinks, b_w_out, b_b_out, ffn_w_gate_up, ffn_w_down, ple_w_up, ple_w_gate, ple_b_gate, ln_gain, ln_bias, loss_target, m_a_w_in, m_a_lower_bound, m_a_norm_gain, m_a_w_out, m_kv_w, m_kv_b, m_b_w_q, m_b_b_q, m_b_sinks, m_b_w_out, m_b_b_out, m_ffn_w_gate_up, m_ffn_w_down, m_ple_w_up, m_ple_w_gate, m_ple_b_gate, m_ln_gain, m_ln_bias, v_a_w_in, v_a_lower_bound, v_a_norm_gain, v_a_w_out, v_kv_w, v_kv_b, v_b_w_q, v_b_b_q, v_b_sinks, v_b_w_out, v_b_b_out, v_ffn_w_gate_up, v_ffn_w_down, v_ple_w_up, v_ple_w_gate, v_ple_b_gate, v_ln_gain, v_ln_bias):
    args = dict(locals())
    wts = {n: args[n] for n in WEIGHT_ORDER}
    mom = {n: args["m_" + n] for n in WEIGHT_ORDER}
    vel = {n: args["v_" + n] for n in WEIGHT_ORDER}
    chip = 2 * lax.axis_index("x") + lax.axis_index("y")
    d = x.shape[-1]
    dq = d // N_CHIPS

    kind_of = dict(BIG)
    kind_of["small"] = "col"
    chip_arr = chip.reshape(1).astype(jnp.int32)
    small_pack = jnp.concatenate([wts[n].reshape(-1, dq) for n in SMALL_SHARDED], axis=0)[None]

    def place_item(key):
        n, layer = key
        if n == "small":
            return small_pack, 0, "col", F32
        return _as3(wts[n]), layer, kind_of[n], BF16

    gathers, where = [], {}
    for gi, group in enumerate(GATHER_GROUPS):
        prev = gathers[-1].token if gathers else None
        placed = _place([place_item(k) for k in group], chip_arr, name=f"place{gi}", after=prev)
        gathers.append(_Exchange("gather", [], placed, [kind_of[k[0]] for k in group],
                                 [0] * len(group), f"gather{gi}", after=prev,
                                 halves=[k[0] != "small" for k in group]))
        for k in group:
            where[k] = gi
    all_started = gathers[-1].token
    ready = {}

    fills = {}

    def pass_on(gi, after):
        if gi not in fills:
            group = GATHER_GROUPS[gi]
            outs = gathers[gi].wait(after)[1]
            split = [i for i, k in enumerate(group) if k[0] != "small"]
            fills[gi] = (outs, split, _SiblingFill([outs[i] for i in split], [kind_of[group[i][0]] for i in split],
                                                   f"fill{gi}", collective_id=gi))

    def wget(name, layer, after):
        key = (name, layer)
        if key not in ready:
            gi = where[key]
            after = all_started if gi == 0 else after
            pass_on(gi, after)
            if 1 <= gi < len(GATHER_GROUPS) - 1:
                pass_on(gi + 1, after)
                after = fills[gi + 1][2].token
            outs, split, fill = fills[gi]
            for i, arr in zip(split, fill.wait(after)):
                outs[i] = arr
            for k, arr in zip(GATHER_GROUPS[gi], outs):
                ready[k] = arr
        return ready[key]

    small_full = wget("small", 0, None)[0]
    ln_gain_f = small_full[0:6].reshape(DEPTH, 3, d)
    ln_bias_f = small_full[6:12].reshape(DEPTH, 3, d)
    alb_f = small_full[12:14]

    group_of = {k: gi for gi, group in enumerate(SCATTER_GROUPS) for k in group}
    grads_done, zones, scatters = {}, {}, []

    def grad_sink(name, layer, grad):
        grads_done[(name, layer)] = grad
        if name not in zones:
            zones[name] = lax.empty((N_CHIPS,) + _as3(wts[name]).shape, BF16)
        gi = group_of[(name, layer)]
        group = SCATTER_GROUPS[gi]
        if not all(k in grads_done for k in group):
            return None
        ex = _Exchange("scatter", [grads_done[k] for k in group], [zones[k[0]] for k in group],
                       [kind_of[k[0]] for k in group], [k[1] for k in group], f"scatter{gi}")
        for k, zone in zip(group, ex.lands):
            zones[k[0]] = zone
        scatters.append((ex, group))
        return ex.token

    small = {}

    def small_sink(loss, gs):
        ln_g = jnp.concatenate([gs[f"ln_gain_{i}_{j}"] for i in range(DEPTH) for j in range(3)], axis=0)
        ln_b = jnp.concatenate([gs[f"ln_bias_{i}_{j}"] for i in range(DEPTH) for j in range(3)], axis=0)
        ple_bg = jnp.concatenate([gs[f"ple_b_{i}"] for i in range(DEPTH)], axis=0)
        small["list"] = [ln_g.reshape(1, -1), ln_b.reshape(1, -1), gs["alb"].reshape(1, -1), gs["norm_gain"],
                         gs["kv_b"], gs["b_q"], _pad_lanes(gs["sinks"]), gs["b_out"], ple_bg.reshape(1, -1), loss]
        small["gather"] = _GatherDevices(jnp.concatenate(small["list"], axis=1))
        return small["gather"].token

    loss, grad_x, gs = _local_step(
        x[0], p.reshape((p.shape[0],) + p.shape[2:]), loss_target[0], wget, grad_sink, ln_gain_f, ln_bias_f, alb_f, a_norm_gain, kv_b, b_b_q,
        b_sinks, b_b_out, ple_b_gate, small_sink)

    res = {}

    def arrive(batch, after):
        for ex, group in batch:
            srcs, outs = ex.wait(after, lands=[zones[k[0]] for k in group])
            for k, grad, zone in zip(group, srcs, outs):
                grads_done[k], zones[k[0]] = grad, zone

    def half_sums(names, batch, after):
        partial = []
        for n in names:
            own = [grads_done[(n, layer)] for layer in range(zones[n].shape[1])]
            partial.append(_sum_arrivals(zones[n], own, kind_of[n], chip_arr, f"sum_{n}", after=after))
        return _SiblingSwap(partial, f"swap{batch}", collective_id=len(GATHER_GROUPS) + batch, after=after)

    def update(names, swap, after):
        for n, own, sib in zip(names, *swap.wait(after)):
            shp = wts[n].shape
            flat = lambda a: a.reshape(-1, shp[-1])
            out = _rowwise(_adam_fn, [flat(wts[n]), flat(mom[n]), flat(vel[n]), own, sib], [],
                           [(own.shape, F32)] * 4, name=f"adam_{n}")
            res[n] = [o.reshape(shp) for o in out]
        return res[names[-1]][1]

    last_names = [k[0] for k in SCATTER_GROUPS[-1]]
    batches = [["ffn_w_gate_up"], [n for n, _ in BIG if n != "ffn_w_gate_up" and n not in last_names], last_names]
    arrive(scatters[:-1], grad_x)
    swap0 = half_sums(batches[0], 0, None)
    swap1 = half_sums(batches[1], 1, swap0.token)
    updated = update(batches[0], swap0, swap1.token)
    arrive(scatters[-1:], updated)
    swap2 = half_sums(batches[2], 2, swap1.token)
    updated = update(batches[1], swap1, swap2.token)
    update(batches[2], swap2, updated)

    small_list = small["list"]
    everyone = small["gather"].wait(grad_x)
    total, = _rowwise(_sum_rows_fn, [everyone], [], [(everyone.shape[1:], F32)], name="sum_small")
    offs, pos = [], 0
    for v in small_list:
        offs.append((pos, v.shape[1]))
        pos += v.shape[1]

    def seg(k):
        return total[0, offs[k][0]:offs[k][0] + offs[k][1]]

    def my_cols(full, rows):
        return lax.dynamic_slice_in_dim(full.reshape(rows, N_CHIPS, dq), chip, 1, axis=1).reshape(rows, dq)

    n_sink = b_sinks.shape[-1]
    small_grads = {
        "ln_gain": my_cols(seg(0), 6).reshape(ln_gain.shape), "ln_bias": my_cols(seg(1), 6).reshape(ln_bias.shape),
        "a_lower_bound": my_cols(seg(2), 2), "a_norm_gain": seg(3).reshape(a_norm_gain.shape),
        "kv_b": seg(4).reshape(kv_b.shape), "b_b_q": seg(5).reshape(b_b_q.shape),
        "b_sinks": seg(6)[:n_sink].reshape(b_sinks.shape), "b_b_out": seg(7).reshape(b_b_out.shape),
        "ple_b_gate": seg(8).reshape(ple_b_gate.shape)}
    names = SMALL_SHARDED + SMALL_REPLICATED
    pack = lambda dct: _pad_lanes(jnp.concatenate([dct[n].reshape(1, -1) for n in names], axis=1))
    g_pack = pack(small_grads)
    upd = _rowwise(_adam_small_fn, [pack(wts), pack(mom), pack(vel), g_pack], [], [(g_pack.shape, F32)] * 3,
                   name="adam_small")
    pos = 0
    for n in names:
        size = wts[n].size
        res[n] = [small_grads[n]] + [u[0, pos:pos + size].reshape(wts[n].shape) for u in upd]
        pos += size

    outs = [seg(9)[0], grad_x[None]]
    for k in range(4):
        outs += [res[n][k] for n in WEIGHT_ORDER]
    return tuple(outs)
```

```python
import functools

import jax
import jax.numpy as jnp
from jax import lax
from jax.experimental import pallas as pl
from jax.experimental.pallas import tpu as pltpu

F32 = jnp.float32
BF16 = jnp.bfloat16
MESH = pl.DeviceIdType.MESH

LANES = 128
HG_DK = 128
HG_CHUNK = 64
HG_SUB = 16
HG_ROWS = 512
HG_HEADS_PER_STEP = 2
LOG2_E = 1.4426950408889634
ATT_HD = 64
ATT_G = 4
WINDOW = 128
DEPTH = 2
ALPHA = (2.0 * DEPTH) ** 0.25
LN_EPS = 1e-5
RMS_EPS = 1e-6
ADAM_LR, ADAM_B1, ADAM_B2, ADAM_EPS, ADAM_WD, ADAM_STEP = 0.001, 0.9, 0.999, 1e-08, 0.01, 10
N_CHIPS = 4
N_DEV = 8
VMEM_LIMIT = 56 * 1024 * 1024
NEG = -1e30


def _pick(n, cap):
    best = None
    for d in range(LANES, min(n, cap) + 1, LANES):
        if n % d == 0:
            best = d
    return n if best is None else best


def _pick_rows(m, cap):
    best = None
    for d in range(16, min(m, cap) + 1, 16):
        if m % d == 0:
            best = d
    return m if best is None else best


def _params(sem):
    return pltpu.CompilerParams(dimension_semantics=sem, vmem_limit_bytes=VMEM_LIMIT)


def _zeros_index(ndim, grid_rank=3):
    return (lambda i, j, kk: (0,) * ndim) if grid_rank == 3 else (lambda kk, i: (0,) * ndim)


def _mm(a, b, *, name, la=None, lb=None, ta=False, tb=False, bias=None, add=None, out_dtype=F32,
        out_layers=None, out_layer=None, after=None, post=None, tile_cols=None, caps=(1024, 1536, 2048),
        a_parts=None, b_parts=None):
    ar, ac = a.shape[-2:]
    br, bc = b.shape[-2:]
    assert a_parts is None or (not ta and la is None and a.shape[0] == a_parts)
    assert b_parts is None or (not tb and lb is None and b.shape[0] == b_parts)
    m, k = (ac, ar) if ta else (ar, ac * (a_parts or 1))
    k2, n = (bc, br) if tb else (br, bc * (b_parts or 1))
    assert k == k2, (a.shape, b.shape, ta, tb)
    if post is not None:
        caps = (512, n if tile_cols is None else tile_cols, caps[2])
    tm, tn, tk = _pick(m, caps[0]), _pick(bc if b_parts else n, caps[1]), _pick(ac if a_parts else k, caps[2])
    assert post is None or tn == caps[1]
    nk = k // tk
    gi, gj = m // tm, n // tn
    a_bytes, b_bytes = m * k * a.dtype.itemsize, k * n * b.dtype.itemsize
    rows_outer = (a_bytes + b_bytes * (gi if gj * nk > 1 else 1)) <= (b_bytes + a_bytes * (gj if gi * nk > 1 else 1))
    k_outer = post is not None and nk > 1 and gj == 1
    grid = (nk, gi) if k_outer else (gi, gj, nk) if rows_outer else (gj, gi, nk)
    keep_at = ta and nk == 1 and gj > 1 and rows_outer

    def bs(block, idx, late=False):
        if k_outer:
            return pl.BlockSpec(block, lambda kk, i: idx(jnp.where(kk == nk - 1, i, 0) if late else i, 0, kk))
        return pl.BlockSpec(block, idx if rows_outer else (lambda q, p, kk: idx(p, q, kk)))

    def spec(block, idx, layer):
        if layer is None:
            return bs(block, idx)
        return bs((None,) + block, lambda i, j, kk: (layer,) + idx(i, j, kk))

    a_spec = spec((tk, tm), lambda i, j, kk: (kk, i), la) if ta else spec((tm, tk), lambda i, j, kk: (i, kk), la)
    b_spec = spec((tn, tk), lambda i, j, kk: (j, kk), lb) if tb else spec((tk, tn), lambda i, j, kk: (kk, j), lb)
    if a_parts:
        a_spec = bs((None, tm, tk), lambda i, j, kk: (kk // (ac // tk), i, kk % (ac // tk)))
    if b_parts:
        b_spec = bs((None, tk, tn), lambda i, j, kk: (j // (bc // tn), kk, j % (bc // tn)))
    in_specs, operands = [a_spec, b_spec], [a, b]
    if bias is not None:
        in_specs.append(bs((1, tn), lambda i, j, kk: (0, j)))
        operands.append(bias)
    if add is not None:
        in_specs.append(bs((tm, tn), lambda i, j, kk: (i, j), late=True))
        operands.append(add)
    if after is not None:
        in_specs.append(pl.BlockSpec(memory_space=pl.ANY))
        operands.append(after)
    dims = (((0 if ta else 1,), (1 if tb else 0,)), ((), ()))
    has_bias, has_add = bias is not None, add is not None
    if post is None:
        fn, rows, whole, outs, sums = None, [], [], [], []
        out_shape = jax.ShapeDtypeStruct((m, n) if out_layers is None else (out_layers, m, n), out_dtype)
        out_specs = spec((tm, tn), lambda i, j, kk: (i, j), out_layer)
    else:
        fn, rows, whole, outs, sums = post
        in_specs += [bs((tm, r.shape[-1] // gj), lambda i, j, kk: (i, j), late=True) for r in rows]
        in_specs += [pl.BlockSpec(tuple(w.shape), _zeros_index(w.ndim, len(grid))) for w in whole]
        operands += list(rows) + list(whole)
        out_shape = [jax.ShapeDtypeStruct(sh, dt) for sh, dt in list(outs) + list(sums)]
        out_specs = ([bs((tm, sh[-1] // gj), lambda i, j, kk: (i, j), late=True) for sh, _ in outs]
                     + [pl.BlockSpec(tuple(sh), _zeros_index(len(sh), len(grid))) for sh, _ in sums])
    n_in, n_extra, n_outs, n_sums = len(operands), len(rows) + len(whole), len(outs), len(sums)

    def body(*refs):
        a_ref, b_ref = refs[0], refs[1]
        pos = 2
        bias_ref = add_ref = None
        if has_bias:
            bias_ref = refs[pos]
            pos += 1
        if has_add:
            add_ref = refs[pos]
            pos += 1
        extra_refs = refs[n_in - n_extra:n_in]
        out_refs = refs[n_in:n_in + max(n_outs, 1)]
        sum_refs = refs[n_in + n_outs:n_in + n_outs + n_sums]
        acc_ref = refs[-1] if nk > 1 else None
        if keep_at:
            at_ref = refs[-1]

            @pl.when(pl.program_id(1) == 0)
            def _():
                at_ref[...] = a_ref[...].astype(BF16).T

            part = lax.dot_general(at_ref[...], b_ref[...].astype(BF16), (((1,), (1 if tb else 0,)), ((), ())),
                                   preferred_element_type=F32)
        else:
            part = lax.dot_general(a_ref[...].astype(BF16), b_ref[...].astype(BF16), dims,
                                   preferred_element_type=F32)

        def finish(total):
            if has_bias:
                total = total + bias_ref[...]
            if has_add:
                total = total + add_ref[...]
            if fn is None:
                out_refs[0][...] = total.astype(out_refs[0].dtype)
                return
            res = fn(total, *[r[...] for r in extra_refs])
            for ref, val in zip(out_refs, res[:n_outs]):
                ref[...] = val.astype(ref.dtype)
            if n_sums:
                @pl.when(pl.program_id(1 if k_outer or not rows_outer else 0) == 0)
                def _():
                    for ref in sum_refs:
                        ref[...] = jnp.zeros(ref.shape, ref.dtype)

                for ref, val in zip(sum_refs, res[n_outs:]):
                    ref[...] += val

        if nk == 1:
            finish(part)
        elif k_outer:
            kk = pl.program_id(0)
            rows_i = pl.ds(pl.multiple_of(pl.program_id(1) * tm, tm), tm)

            @pl.when(kk == 0)
            def _():
                acc_ref[rows_i, :] = part

            @pl.when(kk > 0)
            def _():
                acc_ref[rows_i, :] += part

            @pl.when(kk == nk - 1)
            def _():
                finish(acc_ref[rows_i, :])
        else:
            kk = pl.program_id(2)

            @pl.when(kk == 0)
            def _():
                acc_ref[...] = part

            @pl.when(kk > 0)
            def _():
                acc_ref[...] += part

            @pl.when(kk == nk - 1)
            def _():
                finish(acc_ref[...])

    return pl.pallas_call(
        body, name=name, grid=grid, in_specs=in_specs, out_specs=out_specs, out_shape=out_shape,
        scratch_shapes=([pltpu.VMEM((m, n) if k_outer else (tm, tn), F32)] if nk > 1
                        else [pltpu.VMEM((tm, tk), BF16)] if keep_at else []),
        compiler_params=_params(("arbitrary", "arbitrary") if k_outer
                                else ("arbitrary" if n_sums else "parallel", "arbitrary" if keep_at else "parallel",
                                      "arbitrary") if rows_outer
                                else ("parallel", "arbitrary" if n_sums else "parallel", "arbitrary")),
    )(*operands)


def _rowwise(fn, rows, whole, outs, sums=(), *, name, tm=256):
    m = rows[0].shape[-2]
    tm = _pick_rows(m, tm)
    n_rows, n_whole, n_outs, n_sums = len(rows), len(whole), len(outs), len(sums)

    def rspec(shape):
        lead = len(shape) - 2
        return pl.BlockSpec(tuple(shape[:-2]) + (tm, shape[-1]), lambda i: (0,) * lead + (i, 0))

    def wspec(shape):
        return pl.BlockSpec(tuple(shape), lambda i: (0,) * len(shape))

    def body(*refs):
        vals = [r[...] for r in refs[:n_rows + n_whole]]
        out_refs = refs[n_rows + n_whole:n_rows + n_whole + n_outs]
        sum_refs = refs[n_rows + n_whole + n_outs:]
        res = fn(*vals)
        for ref, val in zip(out_refs, res[:n_outs]):
            ref[...] = val.astype(ref.dtype)
        if n_sums:
            @pl.when(pl.program_id(0) == 0)
            def _():
                for ref in sum_refs:
                    ref[...] = jnp.zeros(ref.shape, ref.dtype)

            for ref, val in zip(sum_refs, res[n_outs:]):
                ref[...] += val

    result = pl.pallas_call(
        body, name=name, grid=(m // tm,),
        in_specs=[rspec(r.shape) for r in rows] + [wspec(w.shape) for w in whole],
        out_specs=[rspec(s) for s, _ in outs] + [wspec(s) for s, _ in sums],
        out_shape=[jax.ShapeDtypeStruct(s, d) for s, d in list(outs) + list(sums)],
        compiler_params=_params(("arbitrary",)),
    )(*rows, *whole)
    return result


def _sigmoid(v):
    return jax.nn.sigmoid(v)


def _col_sum(v):
    return jnp.sum(v, axis=0, keepdims=True)


def _ln_stats(z):
    mu = jnp.mean(z, axis=-1, keepdims=True)
    zc = z - mu
    var = jnp.mean(zc * zc, axis=-1, keepdims=True)
    rstd = lax.rsqrt(var + LN_EPS)
    return zc * rstd, rstd


def _ln_fwd_fn(xin, h, gain, bias):
    xhat, _ = _ln_stats(ALPHA * xin + h)
    y = xhat * gain + bias
    return y, y


def _ple_ln_fwd_fn(xin, pg, pu, gain, bias):
    xhat, _ = _ln_stats(ALPHA * xin + _sigmoid(pg) * pu)
    y = xhat * gain + bias
    return y, y


def _ln_dz(dy, z, gain):
    xhat, rstd = _ln_stats(z)
    dxhat = dy * gain
    dz = rstd * (dxhat - jnp.mean(dxhat, axis=-1, keepdims=True)
                 - xhat * jnp.mean(dxhat * xhat, axis=-1, keepdims=True))
    return dz, _col_sum(dy * xhat), _col_sum(dy)


def _ln_bwd_fn(dy, xin, h, gain):
    dz, dgain, dbias = _ln_dz(dy, ALPHA * xin + h, gain)
    return ALPHA * dz, dz, dgain, dbias, _col_sum(dz)


def _ple_ln_bwd_fn(dy, xin, pg, pu, gain):
    sg = _sigmoid(pg)
    dz, dgain, dbias = _ln_dz(dy, ALPHA * xin + sg * pu, gain)
    dpg = dz * pu * sg * (1.0 - sg)
    return ALPHA * dz, dpg, dz * sg, dgain, dbias, _col_sum(dpg)


def _swiglu_fwd_fn(gu):
    hid = gu.shape[-1] // 2
    gate, up = gu[:, :hid], gu[:, hid:]
    return gu, gate * _sigmoid(gate) * up


def _swiglu_bwd_fn(dact, gu):
    gu = gu.astype(F32)
    hid = gu.shape[-1] // 2
    gate, up = gu[:, :hid], gu[:, hid:]
    sg = _sigmoid(gate)
    dgate = dact * up * sg * (1.0 + gate * (1.0 - sg))
    dup = dact * gate * sg
    return (jnp.concatenate([dgate, dup], axis=-1),)


def _loss_fn(y, target):
    err = y - target
    inv = 1.0 / y.shape[-1]
    part = 0.5 * inv * jnp.sum(jnp.sum(err * err, axis=-1, keepdims=True), axis=0, keepdims=True)
    return err * inv, jnp.broadcast_to(part, (1, LANES))


def _adam_fn(w, mom, vel, p_own, p_sib):
    g = p_own.astype(F32) + p_sib.astype(F32)
    m_new = ADAM_B1 * mom + (1.0 - ADAM_B1) * g
    v_new = ADAM_B2 * vel + (1.0 - ADAM_B2) * (g * g)
    m_hat = m_new / (1.0 - ADAM_B1 ** ADAM_STEP)
    v_hat = v_new / (1.0 - ADAM_B2 ** ADAM_STEP)
    delta = -ADAM_LR * (m_hat / (jnp.sqrt(v_hat) + ADAM_EPS) + ADAM_WD * w)
    return g, delta, m_new, v_new


def _split2(x):
    hi = x.astype(BF16)
    return hi, (x - hi.astype(F32)).astype(BF16)


def _dot3(a, b, dims):
    a_hi, a_lo = _split2(a)
    b_hi, b_lo = _split2(b)
    dn = (dims, ((), ()))
    return (lax.dot_general(a_hi, b_hi, dn, preferred_element_type=F32)
            + (lax.dot_general(a_hi, b_lo, dn, preferred_element_type=F32)
               + lax.dot_general(a_lo, b_hi, dn, preferred_element_type=F32)))


def _tdot(mask01, b):
    m = mask01.astype(BF16)
    b_hi = b.astype(BF16)
    rest = b - b_hi.astype(F32)
    b_mid = rest.astype(BF16)
    b_lo = (rest - b_mid.astype(F32)).astype(BF16)
    dn = (((1,), (0,)), ((), ()))
    return (lax.dot_general(m, b_hi, dn, preferred_element_type=F32)
            + (lax.dot_general(m, b_mid, dn, preferred_element_type=F32)
               + lax.dot_general(m, b_lo, dn, preferred_element_type=F32)))


def _hdot(a, b):
    return _dot3(a, b, ((1,), (0,)))


def _hdot_nt(a, b):
    return _dot3(a, b, ((1,), (1,)))


def _hdot_tn(a, b):
    return _dot3(a, b, ((0,), (0,)))


def _dot(a, b):
    return lax.dot_general(a.astype(BF16), b.astype(BF16), (((1,), (0,)), ((), ())), preferred_element_type=F32)


def _dot_nt(a, b):
    return lax.dot_general(a.astype(BF16), b.astype(BF16), (((1,), (1,)), ((), ())), preferred_element_type=F32)


def _dot_tn(a, b):
    return lax.dot_general(a.astype(BF16), b.astype(BF16), (((0,), (0,)), ((), ())), preferred_element_type=F32)


def _hg_masks():
    c = HG_CHUNK
    row = lax.broadcasted_iota(jnp.int32, (c, c), 0)
    col = lax.broadcasted_iota(jnp.int32, (c, c), 1)
    base = row & (-HG_SUB)
    return row, col, base, col <= row, col < base


def _hg_gates(qr, fr, alb):
    lbound = _sigmoid(alb[0:1, :] - alb[1:2, :])
    sig = _sigmoid(fr)
    forget = lbound + (1.0 - lbound) * sig
    kk = (1.0 - lbound) * _sigmoid(-fr)
    qt = qr * _sigmoid(qr) * (HG_DK ** -0.5)
    return qt, kk, jnp.log(forget), lbound, sig, forget


def _hg_scores(qt, kk, g, scores=True):
    c, nsub = HG_CHUNK, HG_CHUNK // HG_SUB
    row, col, base, causal, below = _hg_masks()
    b = _tdot(causal, g)
    rr = _tdot(below, g)
    bq = b - rr
    qh = qt * jnp.exp(bq)
    edecs = [None]
    parts = [jnp.zeros((HG_SUB, c), F32)]
    for i in range(1, nsub):
        edec = jnp.exp(jnp.minimum(rr[i * HG_SUB:i * HG_SUB + 1, :] - b, 0.0))
        edecs.append(edec)
        if scores:
            parts.append(_dot_nt(qh[i * HG_SUB:(i + 1) * HG_SUB, :], kk * edec))
    q3 = qt.reshape(nsub, HG_SUB, HG_DK)
    if not scores:
        return None, b, bq, qh, edecs, (b.reshape(nsub, HG_SUB, HG_DK), q3, kk.reshape(nsub, HG_SUB, HG_DK))
    a = jnp.where(below, jnp.concatenate(parts, axis=0), 0.0)
    b2 = b * LOG2_E
    b3 = b2.reshape(nsub, HG_SUB, HG_DK)
    c3 = (b2 - jnp.log2(kk)).reshape(nsub, HG_SUB, HG_DK)
    for j in range(HG_SUB):
        ek = jnp.exp2(b3 - c3[:, j:j + 1, :])
        colv = jnp.sum(q3 * ek, axis=-1, keepdims=True).reshape(c, 1)
        a = jnp.where(col == base + j, colv, a)
    a = jnp.where(causal, a, 0.0)
    return a, b, bq, qh, edecs, None


def _hg_norm(o, gr, gain):
    r = lax.rsqrt(jnp.mean(o * o, axis=-1, keepdims=True) + RMS_EPS)
    sg = _sigmoid(gr)
    return o * r * gain, r, sg


def _hgrn2_fwd(proj, alb, gain, *, rb):
    m, d4 = proj.shape
    d = d4 // 4
    heads = d // HG_DK
    hp = HG_HEADS_PER_STEP
    rb = min(rb, m)
    cpb = rb // HG_CHUNK
    nrb = m // rb

    def body(q_ref, f_ref, v_ref, g_ref, alb_ref, gain_ref, o_ref, og_ref, st_ref, a_ref, state):
        @pl.when(pl.program_id(1) == 0)
        def _():
            state[...] = jnp.zeros(state.shape, F32)

        def chunk(ci, carry):
            sl = pl.ds(pl.multiple_of(ci * HG_CHUNK, HG_CHUNK), HG_CHUNK)
            for u in range(hp):
                ln = slice(u * HG_DK, (u + 1) * HG_DK)
                qt, kk, g, _, _, _ = _hg_gates(q_ref[sl, ln], f_ref[sl, ln], alb_ref[:, ln])
                v = v_ref[sl, ln]
                st = state[u]
                st_ref[u, ci] = st
                a, b, _, _, _, _ = _hg_scores(qt, kk, g)
                a_ref[u, ci] = a.astype(a_ref.dtype)
                o = _dot(a, v) + _dot_nt(qt * jnp.exp(b), st)
                b_last = b[HG_CHUNK - 1:HG_CHUNK, :]
                state[u] = st * jnp.exp(b_last) + _hdot_tn(v, kk * jnp.exp(b_last - b))
                o_ref[sl, ln] = o
                n, _, sg = _hg_norm(o, g_ref[sl, ln], gain_ref[...])
                og_ref[sl, ln] = (n * g_ref[sl, ln] * sg).astype(og_ref.dtype)
            return carry

        lax.fori_loop(0, cpb, chunk, 0)

    def col(cidx):
        return pl.BlockSpec((rb, hp * HG_DK), lambda h, r: (r, cidx * (heads // hp) + h))

    return pl.pallas_call(
        body, name="hgrn2_fwd", grid=(heads // hp, nrb),
        in_specs=[col(0), col(1), col(2), col(3),
                  pl.BlockSpec((2, hp * HG_DK), lambda h, r: (0, h)),
                  pl.BlockSpec((1, HG_DK), lambda h, r: (0, 0))],
        out_specs=[pl.BlockSpec((rb, hp * HG_DK), lambda h, r: (r, h)),
                   pl.BlockSpec((rb, hp * HG_DK), lambda h, r: (r, h)),
                   pl.BlockSpec((hp, cpb, HG_DK, HG_DK), lambda h, r: (h, r, 0, 0)),
                   pl.BlockSpec((hp, cpb, HG_CHUNK, HG_CHUNK), lambda h, r: (h, r, 0, 0))],
        out_shape=[jax.ShapeDtypeStruct((m, d), F32), jax.ShapeDtypeStruct((m, d), BF16),
                   jax.ShapeDtypeStruct((heads, m // HG_CHUNK, HG_DK, HG_DK), F32),
                   jax.ShapeDtypeStruct((heads, m // HG_CHUNK, HG_CHUNK, HG_CHUNK), BF16)],
        scratch_shapes=[pltpu.VMEM((hp, HG_DK, HG_DK), F32)],
        compiler_params=_params(("parallel", "arbitrary")),
    )(proj, proj, proj, proj, alb, gain)


def _hgrn2_bwd(proj, o_pre, states, scores, dog, alb, gain, *, rb):
    m, d4 = proj.shape
    d = d4 // 4
    heads = d // HG_DK
    rb = min(rb, m)
    cpb = rb // HG_CHUNK
    nrb = m // rb
    c, nsub = HG_CHUNK, HG_CHUNK // HG_SUB

    def body(q_ref, f_ref, v_ref, g_ref, o_ref, st_ref, a_ref, dog_ref, alb_ref, gain_ref,
             dp_ref, dalb_ref, dgain_ref, dstate, carry_ref):
        first = (pl.program_id(0) == 0) & (pl.program_id(1) == 0)

        @pl.when(first)
        def _():
            dgain_ref[...] = jnp.zeros(dgain_ref.shape, F32)

        @pl.when(pl.program_id(1) == 0)
        def _():
            dstate[...] = jnp.zeros(dstate.shape, F32)
            carry_ref[...] = jnp.zeros(carry_ref.shape, F32)
            dalb_ref[...] = jnp.zeros(dalb_ref.shape, F32)

        row, col, base, causal, below = _hg_masks()
        sub_iota = lax.broadcasted_iota(jnp.int32, (nsub, HG_SUB, HG_DK), 1)
        row_k = lax.broadcasted_iota(jnp.int32, (c, HG_DK), 0)
        upper = col >= row

        def chunk(step, carry):
            ci = cpb - 1 - step
            sl = pl.ds(pl.multiple_of(ci * HG_CHUNK, HG_CHUNK), HG_CHUNK)
            qr, fr, v, gr = q_ref[sl, :], f_ref[sl, :], v_ref[sl, :], g_ref[sl, :]
            qt, kk, g, lbound, sig, forget = _hg_gates(qr, fr, alb_ref[...])
            o = o_ref[sl, :]
            dogv = dog_ref[sl, :]
            gain_v = gain_ref[...]
            n, r, sg = _hg_norm(o, gr, gain_v)
            dgr = dogv * n * sg * (1.0 + gr * (1.0 - sg))
            dn = dogv * gr * sg
            dgain_ref[...] += _col_sum(dn * o * r)
            u = dn * gain_v
            d_o = r * u - o * (r * r * r) * jnp.mean(u * o, axis=-1, keepdims=True)
            st0 = st_ref[ci]
            dst = dstate[...]
            _, b, bq, qh, edecs, (b3, q3, k3) = _hg_scores(qt, kk, g, scores=False)
            a = a_ref[ci]
            eb = jnp.exp(b)
            b_last = b[c - 1:c, :]
            kdl_dec = jnp.exp(b_last - b)
            kdl = kk * kdl_dec
            d_a = jnp.where(causal, _dot_nt(d_o, v), 0.0)
            d_at = _dot_nt(v, d_o)
            dv = _dot_tn(a, d_o) + _dot_nt(kdl, dst)
            dq = eb * _hdot(d_o, st0)
            dk = _hdot(v, dst) * kdl_dec
            d_a_below = jnp.where(below, d_a, 0.0)
            dq_parts = [jnp.zeros((HG_SUB, HG_DK), F32)]
            for i in range(1, nsub):
                lo, hi = i * HG_SUB, (i + 1) * HG_SUB
                dq_parts.append(_hdot(d_a_below[lo:hi, :], kk * edecs[i]))
                gi = _hdot(d_at[:, lo:hi], qh[lo:hi, :])
                dk = dk + jnp.where(row_k < lo, edecs[i] * gi, 0.0)
            dq = dq + jnp.concatenate(dq_parts, axis=0) * jnp.exp(bq)
            dq3 = jnp.zeros((nsub, HG_SUB, HG_DK), F32)
            dk3 = jnp.zeros((nsub, HG_SUB, HG_DK), F32)
            d_diag = jnp.concatenate([d_a[i * HG_SUB:(i + 1) * HG_SUB, i * HG_SUB:(i + 1) * HG_SUB]
                                      for i in range(nsub)], axis=0).reshape(nsub, HG_SUB, HG_SUB)
            for j in range(HG_SUB):
                e = jnp.exp(jnp.minimum(b3 - b3[:, j:j + 1, :], 0.0))
                t1 = d_diag[:, :, j:j + 1] * e
                dq3 = dq3 + t1 * k3[:, j:j + 1, :]
                dk3 = jnp.where(sub_iota == j, jnp.sum(t1 * q3, axis=1, keepdims=True), dk3)
            dq = dq + dq3.reshape(c, HG_DK)
            dk = dk + dk3.reshape(c, HG_DK)
            dstate[...] = dst * jnp.exp(b_last) + _hdot_tn(d_o, qt * eb)
            dglog = _tdot(upper, qt * dq - kk * dk) + carry_ref[...]
            carry_ref[...] = dglog[0:1, :]
            dforget = dglog / forget
            one_m_lb = 1.0 - lbound
            dsig = (dforget - dk) * one_m_lb
            sneg = _sigmoid(-fr)
            dlb = _col_sum(dforget * (1.0 - sig) - dk * sneg)
            dalb0 = dlb * lbound * one_m_lb
            dalb_ref[...] += jnp.concatenate([dalb0, -dalb0], axis=0)
            sq = _sigmoid(qr)
            dp_ref[0, sl, :] = (dq * (HG_DK ** -0.5) * sq * (1.0 + qr * (1.0 - sq))).astype(dp_ref.dtype)
            dp_ref[1, sl, :] = (dsig * sig * (1.0 - sig)).astype(dp_ref.dtype)
            dp_ref[2, sl, :] = dv.astype(dp_ref.dtype)
            dp_ref[3, sl, :] = dgr.astype(dp_ref.dtype)
            return carry

        lax.fori_loop(0, cpb, chunk, 0, unroll=2)

    def rev(r):
        return nrb - 1 - r

    def col(cidx):
        return pl.BlockSpec((rb, HG_DK), lambda h, r: (rev(r), cidx * heads + h))

    def head_rows():
        return pl.BlockSpec((rb, HG_DK), lambda h, r: (rev(r), h))

    return pl.pallas_call(
        body, name="hgrn2_bwd", grid=(heads, nrb),
        in_specs=[col(0), col(1), col(2), col(3), head_rows(),
                  pl.BlockSpec((None, cpb, HG_DK, HG_DK), lambda h, r: (h, rev(r), 0, 0)),
                  pl.BlockSpec((None, cpb, HG_CHUNK, HG_CHUNK), lambda h, r: (h, rev(r), 0, 0)),
                  head_rows(),
                  pl.BlockSpec((2, HG_DK), lambda h, r: (0, h)),
                  pl.BlockSpec((1, HG_DK), lambda h, r: (0, 0))],
        out_specs=[pl.BlockSpec((4, rb, HG_DK), lambda h, r: (0, rev(r), h)),
                   pl.BlockSpec((2, HG_DK), lambda h, r: (0, h)),
                   pl.BlockSpec((1, HG_DK), lambda h, r: (0, 0))],
        out_shape=[jax.ShapeDtypeStruct((4, m, d), BF16), jax.ShapeDtypeStruct((2, d), F32),
                   jax.ShapeDtypeStruct((1, HG_DK), F32)],
        scratch_shapes=[pltpu.VMEM((HG_DK, HG_DK), F32), pltpu.VMEM((1, HG_DK), F32)],
        compiler_params=_params(("arbitrary", "arbitrary")),
    )(proj, proj, proj, proj, o_pre, states, scores, dog, alb, gain)


def _swa_probs(qh, kp, kc, sink, slope, has_prev, lse=None):
    rows = qh.shape[0]
    qi = lax.broadcasted_iota(jnp.int32, (rows, WINDOW), 0) & (WINDOW - 1)
    si = lax.broadcasted_iota(jnp.int32, (rows, WINDOW), 1)
    scale = ATT_HD ** -0.5
    dist_c = (qi - si).astype(F32)
    s_p = _dot_nt(qh, kp) * scale - slope * (dist_c + float(WINDOW))
    s_c = _dot_nt(qh, kc) * scale - slope * dist_c
    s_p = jnp.where((si > qi) & has_prev, s_p, NEG)
    s_c = jnp.where(si <= qi, s_c, NEG)
    if lse is not None:
        return jnp.exp(s_p - lse), jnp.exp(s_c - lse), jnp.exp(sink - lse), lse
    mx = jnp.maximum(jnp.maximum(jnp.max(s_p, axis=-1, keepdims=True), jnp.max(s_c, axis=-1, keepdims=True)), sink)
    e_p, e_c, e_s = jnp.exp(s_p - mx), jnp.exp(s_c - mx), jnp.exp(sink - mx)
    total = jnp.sum(e_p, axis=-1, keepdims=True) + jnp.sum(e_c, axis=-1, keepdims=True) + e_s
    inv = 1.0 / total
    return e_p * inv, e_c * inv, e_s * inv, mx + jnp.log(total)


def _slope(h, n_heads):
    return float(2.0 ** (-8.0 * (h + 1) / n_heads))


def _swa_group(ref_vals, sink_ref, kh, n_heads):
    heads = [kh * ATT_G + g for g in range(ATT_G)]
    stacked = [jnp.concatenate([v[:, h * ATT_HD:(h + 1) * ATT_HD] for h in heads], axis=0) for v in ref_vals]
    grp = lax.shift_right_logical(lax.broadcasted_iota(jnp.int32, (ATT_G * WINDOW, 1), 0), WINDOW.bit_length() - 1)
    slope = jnp.zeros((ATT_G * WINDOW, 1), F32)
    sink = jnp.zeros((ATT_G * WINDOW, 1), F32)
    for g, h in enumerate(heads):
        slope = jnp.where(grp == g, _slope(h, n_heads), slope)
        sink = jnp.where(grp == g, sink_ref[:, h:h + 1], sink)
    return stacked, slope, sink


def _swa_fwd(q, kv, sinks):
    m, d = q.shape
    n_heads = d // ATT_HD
    kvh = n_heads // ATT_G
    kd = kvh * ATT_HD
    nb = m // WINDOW

    def body(q_ref, kvp_ref, kvc_ref, sink_ref, o_ref, lse_ref):
        has_prev = pl.program_id(0) > 0
        qv, kvp, kvc = q_ref[...], kvp_ref[...], kvc_ref[...]
        lane_h = lax.broadcasted_iota(jnp.int32, (WINDOW, n_heads), 1)
        outs, lse_all = [], jnp.zeros((WINDOW, n_heads), F32)
        for kh in range(kvh):
            ks = slice(kh * ATT_HD, (kh + 1) * ATT_HD)
            vs = slice(kd + kh * ATT_HD, kd + (kh + 1) * ATT_HD)
            (q4,), slope, sink = _swa_group([qv], sink_ref, kh, n_heads)
            p_p, p_c, _, lse = _swa_probs(q4, kvp[:, ks], kvc[:, ks], sink, slope, has_prev)
            o4 = _dot(p_p, kvp[:, vs]) + _dot(p_c, kvc[:, vs])
            for g in range(ATT_G):
                rows = slice(g * WINDOW, (g + 1) * WINDOW)
                outs.append(o4[rows, :])
                lse_all = jnp.where(lane_h == kh * ATT_G + g, lse[rows, :], lse_all)
        o_ref[...] = jnp.concatenate(outs, axis=-1).astype(o_ref.dtype)
        lse_ref[...] = lse_all

    return pl.pallas_call(
        body, name="swa_fwd", grid=(nb,),
        in_specs=[pl.BlockSpec((WINDOW, d), lambda n: (n, 0)),
                  pl.BlockSpec((WINDOW, 2 * kd), lambda n: (jnp.maximum(n - 1, 0), 0)),
                  pl.BlockSpec((WINDOW, 2 * kd), lambda n: (n, 0)),
                  pl.BlockSpec((1, n_heads), lambda n: (0, 0))],
        out_specs=[pl.BlockSpec((WINDOW, d), lambda n: (n, 0)), pl.BlockSpec((WINDOW, n_heads), lambda n: (n, 0))],
        out_shape=[jax.ShapeDtypeStruct((m, d), BF16), jax.ShapeDtypeStruct((m, n_heads), F32)],
        compiler_params=_params(("arbitrary",)),
    )(q, kv, kv, sinks)


def _swa_bwd(q, kv, sinks, lse, dao):
    m, d = q.shape
    n_heads = d // ATT_HD
    kvh = n_heads // ATT_G
    kd = kvh * ATT_HD
    nb = m // WINDOW
    scale = ATT_HD ** -0.5

    def body(q_ref, kvp_ref, kvc_ref, sink_ref, lse_ref, do_ref, dq_ref, dkvc_ref, dkvp_ref, dqsum_ref, dsink_ref):
        @pl.when(pl.program_id(0) == 0)
        def _():
            dqsum_ref[...] = jnp.zeros(dqsum_ref.shape, F32)
            dsink_ref[...] = jnp.zeros(dsink_ref.shape, F32)

        has_prev = pl.program_id(0) > 0
        qv, kvp, kvc, dov = q_ref[...], kvp_ref[...], kvc_ref[...], do_ref[...]
        lane_h = lax.broadcasted_iota(jnp.int32, (1, n_heads), 1)
        dsink = jnp.zeros((1, n_heads), F32)
        dq_parts, dk_p, dk_c, dv_p, dv_c = [], [], [], [], []
        for kh in range(kvh):
            ks = slice(kh * ATT_HD, (kh + 1) * ATT_HD)
            vs = slice(kd + kh * ATT_HD, kd + (kh + 1) * ATT_HD)
            kp, kc, vp, vc = kvp[:, ks], kvc[:, ks], kvp[:, vs], kvc[:, vs]
            (q4, do4), slope, sink = _swa_group([qv, dov], sink_ref, kh, n_heads)
            lse4 = jnp.concatenate([lse_ref[:, kh * ATT_G + g:kh * ATT_G + g + 1] for g in range(ATT_G)], axis=0)
            p_p, p_c, p_s, _ = _swa_probs(q4, kp, kc, sink, slope, has_prev, lse=lse4)
            dp_p, dp_c = _dot_nt(do4, vp), _dot_nt(do4, vc)
            delta = jnp.sum(p_p * dp_p, axis=-1, keepdims=True) + jnp.sum(p_c * dp_c, axis=-1, keepdims=True)
            ds_p, ds_c = p_p * (dp_p - delta), p_c * (dp_c - delta)
            sink_term = p_s * delta
            dq4 = (_dot(ds_p, kp) + _dot(ds_c, kc)) * scale
            for g in range(ATT_G):
                rows = slice(g * WINDOW, (g + 1) * WINDOW)
                dsink = dsink + jnp.where(lane_h == kh * ATT_G + g, -_col_sum(sink_term[rows, :]), 0.0)
                dq_parts.append(dq4[rows, :])
            dk_p.append(_dot_tn(ds_p, q4) * scale)
            dk_c.append(_dot_tn(ds_c, q4) * scale)
            dv_p.append(_dot_tn(p_p, do4))
            dv_c.append(_dot_tn(p_c, do4))
        dq = jnp.concatenate(dq_parts, axis=-1)
        dq_ref[...] = dq.astype(dq_ref.dtype)
        dqsum_ref[...] += _col_sum(dq)
        dsink_ref[...] += dsink
        dkvc_ref[...] = jnp.concatenate(dk_c + dv_c, axis=-1)
        dkvp_ref[...] = jnp.concatenate(dk_p + dv_p, axis=-1)

    return pl.pallas_call(
        body, name="swa_bwd", grid=(nb,),
        in_specs=[pl.BlockSpec((WINDOW, d), lambda n: (n, 0)),
                  pl.BlockSpec((WINDOW, 2 * kd), lambda n: (jnp.maximum(n - 1, 0), 0)),
                  pl.BlockSpec((WINDOW, 2 * kd), lambda n: (n, 0)),
                  pl.BlockSpec((1, n_heads), lambda n: (0, 0)),
                  pl.BlockSpec((WINDOW, n_heads), lambda n: (n, 0)),
                  pl.BlockSpec((WINDOW, d), lambda n: (n, 0))],
        out_specs=[pl.BlockSpec((WINDOW, d), lambda n: (n, 0)),
                   pl.BlockSpec((WINDOW, 2 * kd), lambda n: (n, 0)),
                   pl.BlockSpec((WINDOW, 2 * kd), lambda n: (n, 0)),
                   pl.BlockSpec((1, d), lambda n: (0, 0)),
                   pl.BlockSpec((1, n_heads), lambda n: (0, 0))],
        out_shape=[jax.ShapeDtypeStruct((m, d), BF16), jax.ShapeDtypeStruct((m, 2 * kd), F32),
                   jax.ShapeDtypeStruct((m, 2 * kd), F32), jax.ShapeDtypeStruct((1, d), F32),
                   jax.ShapeDtypeStruct((1, n_heads), F32)],
        compiler_params=_params(("arbitrary",)),
    )(q, kv, kv, sinks, lse, dao)


def _kv_grad_combine(dkv_cur, dkv_prev):
    m, w = dkv_cur.shape
    nb = m // WINDOW

    def body(cur_ref, nxt_ref, o_ref, sum_ref):
        @pl.when(pl.program_id(0) == 0)
        def _():
            sum_ref[...] = jnp.zeros(sum_ref.shape, F32)

        total = cur_ref[...] + jnp.where(pl.program_id(0) < nb - 1, nxt_ref[...], 0.0)
        o_ref[...] = total.astype(o_ref.dtype)
        sum_ref[...] += _col_sum(total)

    return pl.pallas_call(
        body, name="kv_grad_combine", grid=(nb,),
        in_specs=[pl.BlockSpec((WINDOW, w), lambda n: (n, 0)),
                  pl.BlockSpec((WINDOW, w), lambda n: (jnp.minimum(n + 1, nb - 1), 0))],
        out_specs=[pl.BlockSpec((WINDOW, w), lambda n: (n, 0)), pl.BlockSpec((1, w), lambda n: (0, 0))],
        out_shape=[jax.ShapeDtypeStruct((m, w), BF16), jax.ShapeDtypeStruct((1, w), F32)],
        compiler_params=_params(("arbitrary",)),
    )(dkv_cur, dkv_prev)


def _row(v):
    return v.reshape(1, -1)


def _local_step(x, p, target, wget, grad_sink, ln_gain, ln_bias, alb, norm_gain, kv_b, b_q, sinks, b_out, ple_b,
                small_sink=None):
    gs = {}
    gains = ln_gain.reshape(DEPTH * 3, -1)
    biases = ln_bias.reshape(DEPTH * 3, -1)
    sd = x.shape
    pending = [None]

    def mm(a, b, lb=0, **kw):
        after, pending[0] = pending[0], None
        return _mm(a, b, lb=lb, after=after, **kw)

    def mm_ln(a, wt, xin, i, j, nm, bias=None, pu=None):
        r = 3 * i + j
        if pu is None:
            fn, rows = (lambda h, xv, g, bv: (h,) + _ln_fwd_fn(xv, h, g[r:r + 1], bv[r:r + 1])), [xin]
        else:
            fn = lambda h, xv, puv, g, bv: (h,) + _ple_ln_fwd_fn(xv, h, puv, g[r:r + 1], bv[r:r + 1])
            rows = [xin, pu]
        h, y, yb = _mm(a, wt, lb=0, bias=bias, name=nm,
                       post=(fn, rows, [gains, biases], [(sd, F32), (sd, F32), (sd, BF16)], []))
        return h, (y, yb)

    def mm_ln_bwd(a, wt, add, xin, h, i, j, nm):
        r = 3 * i + j
        dx_part, dh, dg, db, dhsum = mm(a, wt, tb=True, add=add, name=nm,
                                        post=(lambda dy, xv, hv, g: _ln_bwd_fn(dy, xv, hv, g[r:r + 1]), [xin, h],
                                              [gains], [(sd, F32), (sd, BF16)], [((1, sd[1]), F32)] * 3))
        gs[f"ln_gain_{i}_{j}"], gs[f"ln_bias_{i}_{j}"] = dg, db
        return dx_part, dh, dhsum

    def tail_fwd(xa, i):
        wgu = wget("ffn_w_gate_up", i, xa[1])
        hid2 = wgu.shape[-1]
        gu, act = _mm(xa[1], wgu, lb=0, name=f"ffn_up_swiglu{i}", tile_cols=hid2 // 2,
                      post=(_swiglu_fwd_fn, [], [], [((sd[0], hid2), BF16), ((sd[0], hid2 // 2), BF16)], []))
        f, xb = mm_ln(act, wget("ffn_w_down", i, act), xa[0], i, 1, f"ffn_down_ln{i}")
        pu = _mm(p, wget("ple_w_up", i, act), la=i, lb=0, name=f"ple_up{i}")
        pg, xc = mm_ln(xb[1], wget("ple_w_gate", i, act), xb[0], i, 2, f"ple_gate_ln{i}", bias=_row(ple_b[i]), pu=pu)
        return dict(xa=xa, gu=gu, act=act, f=f, xb=xb, pg=pg, pu=pu), xc

    def tail_bwd(head, sv, i, mix_in, mix_h):
        xa, xb = sv["xa"], sv["xb"]
        r = 3 * i + 2
        dxb_part, dpg, dpu, dg2, db2, dbg = head(
            lambda dy, xv, pgv, puv, g: _ple_ln_bwd_fn(dy, xv, pgv, puv, g[r:r + 1]), [xb[0], sv["pg"], sv["pu"]],
            [gains], [(sd, F32), (sd, BF16), (sd, BF16)], [((1, sd[1]), F32)] * 3)[:6]
        gs[f"ple_b_{i}"] = dbg
        gs[f"ln_gain_{i}_2"], gs[f"ln_bias_{i}_2"] = dg2, db2
        grad_of("ple_w_gate", i, xb[1], dpg)
        grad_of("ple_w_up", i, p, dpu, la=i)
        dxa_part, df, _ = mm_ln_bwd(dpg, wget("ple_w_gate", i, None), dxb_part, xa[0], sv["f"], i, 1,
                                    f"ple_gate_dx_ln{i}")
        grad_of("ffn_w_down", i, sv["act"], df)
        gu = sv["gu"]
        dgu, = mm(df, wget("ffn_w_down", i, None), tb=True, name=f"ffn_down_dx_swiglu{i}", tile_cols=gu.shape[1] // 4,
                  post=(_swiglu_bwd_fn, [gu], [], [(gu.shape, BF16)], []))
        grad_of("ffn_w_gate_up", i, xa[1], dgu)
        return mm_ln_bwd(dgu, wget("ffn_w_gate_up", i, None), dxa_part, mix_in, mix_h, i, 0, f"ffn_up_dx_ln{i}")

    def grad_of(nm, i, act, dout, la=None, b_parts=None):
        grad = mm(act, dout, la=la, lb=None, ta=True, out_dtype=BF16, out_layers=1, out_layer=0,
                  name=f"grad_{nm}{i}", b_parts=b_parts)
        token = grad_sink(nm, i, grad)
        if token is not None:
            pending[0] = token

    proj = _mm(x, wget("a_w_in", 0, None), lb=0, name="hg_proj")
    o_pre, og, states, scores = _hgrn2_fwd(proj, alb, norm_gain, rb=HG_ROWS)
    h0, x1 = mm_ln(og, wget("a_w_out", 0, og), x, 0, 0, "hg_out_ln")
    sv0, x3 = tail_fwd(x1, 0)
    kv = _mm(x3[1], wget("kv_w", 0, x3[1]), lb=0, bias=_row(kv_b), out_dtype=BF16, name="kv_proj")
    q = _mm(x3[1], wget("b_w_q", 0, x3[1]), lb=0, bias=b_q, out_dtype=BF16, name="q_proj")
    ao, lse = _swa_fwd(q, kv, sinks)
    h1, x4 = mm_ln(ao, wget("b_w_out", 0, x3[1]), x3[0], 1, 0, "att_out_ln", bias=b_out)
    sv1, y = tail_fwd(x4, 1)

    loss_box = []

    def loss_head(fn, rows, whole, outs, sums):
        def with_loss(yv, tv, *rest):
            dy, part = _loss_fn(yv, tv)
            return fn(dy, *rest) + (part,)

        res = _rowwise(with_loss, [y[0], target] + rows, whole, outs, list(sums) + [((1, LANES), F32)],
                       name="loss_ln_ple_bwd1")
        loss_box.append(res[-1])
        return res

    dx3_part, dh1, dh1sum = tail_bwd(loss_head, sv1, 1, x3[0], h1)
    loss = loss_box[0]
    gs["b_out"] = dh1sum
    grad_of("b_w_out", 0, ao, dh1)
    dao = mm(dh1, wget("b_w_out", 0, None), tb=True, out_dtype=BF16, name="att_out_dx")
    dq, dkv_cur, dkv_prev, dqsum, dsinks = _swa_bwd(q, kv, sinks, lse, dao)
    gs["b_q"], gs["sinks"] = dqsum, dsinks
    dkv, dkvsum = _kv_grad_combine(dkv_cur, dkv_prev)
    gs["kv_b"] = dkvsum
    grad_of("b_w_q", 0, x3[1], dq)
    grad_of("kv_w", 0, x3[1], dkv)
    dx3 = mm(dq, wget("b_w_q", 0, None), tb=True, add=dx3_part, name="q_proj_dx")

    def kv_head(*post):
        return mm(dkv, wget("kv_w", 0, None), tb=True, add=dx3, name="kv_proj_dx_ln_ple_bwd0", post=post)

    dx_part, dh0, _ = tail_bwd(kv_head, sv0, 0, x, h0)
    grad_of("a_w_out", 0, og, dh0)
    dog = mm(dh0, wget("a_w_out", 0, None), tb=True, name="hg_out_dx")
    dproj, dalb, dgain = _hgrn2_bwd(proj, o_pre, states, scores, dog, alb, norm_gain, rb=HG_ROWS)
    gs["alb"], gs["norm_gain"] = dalb, dgain
    if small_sink is not None:
        pending[0] = small_sink(loss, gs)
    grad_of("a_w_in", 0, x, dproj, b_parts=4)
    grad_x = mm(dproj, wget("a_w_in", 0, None), tb=True, add=dx_part, name="hg_proj_dx", a_parts=4)
    return loss, grad_x, gs


HBM_SPEC = pl.BlockSpec(memory_space=pl.ANY)
HBM_ONLY = pl.BlockSpec(memory_space=pltpu.HBM)
SEM_SPEC = pl.BlockSpec(memory_space=pltpu.SEMAPHORE)
SIDE_EFFECT = pltpu.SideEffectType.DATAFLOW_SIDE_EFFECTING


def _slot(kind, j):
    return (j % 2) * 2 + j // 2 if kind == "colp" else j


def _piece(ref, kind, j):
    _, r, c = ref.shape
    if kind == "row":
        return ref.at[:, pl.ds(j * (r // N_CHIPS), r // N_CHIPS), :]
    return ref.at[:, :, pl.ds(_slot(kind, j) * (c // N_CHIPS), c // N_CHIPS)]


def _piece_dyn(ref, kind, j):
    _, r, c = ref.shape
    if kind == "row":
        return ref.at[:, pl.ds(pl.multiple_of(j * (r // N_CHIPS), 16), r // N_CHIPS), :]
    return ref.at[:, :, pl.ds(pl.multiple_of(_slot(kind, j) * (c // N_CHIPS), LANES), c // N_CHIPS)]


def _chip_of(j, c):
    return (j // 2, j % 2, c)


def _in_hbm(a):
    return pltpu.with_memory_space_constraint(a, pltpu.HBM)


PLACE_STEPS = 4


def _place(items, chip, *, name, after=None):
    n = len(items)
    in_specs, out_specs, out_shapes, blocks = [], [], [], []
    for src, layer, kind, out_dtype in items:
        _, r, c = src.shape
        nb = max(k for k in (1, 2, PLACE_STEPS) if r % (16 * k) == 0 or k == 1)
        blocks.append(nb)

        def src_idx(i, chip_ref, layer=layer, nb=nb):
            return (layer, jnp.minimum(i, nb - 1), 0)

        def full_idx(i, chip_ref, kind=kind, nb=nb):
            ib = jnp.minimum(i, nb - 1)
            return (0, chip_ref[0] * nb + ib, 0) if kind == "row" else (0, ib, _slot(kind, chip_ref[0]))

        in_specs.append(pl.BlockSpec((None, r // nb, c), src_idx))
        out_specs.append(pl.BlockSpec((None, r // nb, c), full_idx))
        out_shapes.append(jax.ShapeDtypeStruct((1, r * N_CHIPS, c) if kind == "row" else (1, r, c * N_CHIPS),
                                               out_dtype))
    operands = [it[0] for it in items]
    if after is not None:
        in_specs.append(HBM_SPEC)
        operands.append(after)

    def body(chip_ref, *refs):
        for a in range(n):
            refs[len(refs) - n + a][...] = refs[a][...].astype(refs[len(refs) - n + a].dtype)

    return pl.pallas_call(
        body, name=name,
        grid_spec=pltpu.PrefetchScalarGridSpec(num_scalar_prefetch=1, grid=(PLACE_STEPS,), in_specs=in_specs,
                                               out_specs=out_specs),
        out_shape=out_shapes,
        compiler_params=_params(("arbitrary",)),
    )(chip, *operands)


def _half(ref, c):
    h = ref.shape[1] // 2
    start = c * h if isinstance(c, int) else pl.multiple_of(c * h, 16)
    return ref.at[:, pl.ds(start, h), :]


def _sibling_handshake():
    barrier = pltpu.get_barrier_semaphore()
    sibling = (lax.axis_index("x"), lax.axis_index("y"), 1 - lax.axis_index("c"))
    pl.semaphore_signal(barrier, inc=1, device_id=sibling, device_id_type=MESH)
    pl.semaphore_wait(barrier, 1)


class _SiblingFill:
    def __init__(self, lands, kinds, name, collective_id):
        self.kinds, self.name, self.n = kinds, name, len(lands)
        n = self.n
        sem_shape = pltpu.SemaphoreType.DMA((n * N_CHIPS,))

        def body(*refs):
            land_refs, send_sems, recv_sems, token = refs[:n], refs[n], refs[n + 1], refs[-1]
            _sibling_handshake()
            for cp in self._copies(land_refs, send_sems, recv_sems):
                cp.start()
            token[...] = jnp.zeros(token.shape, token.dtype)

        outs = pl.pallas_call(
            body, name=name + "_start",
            in_specs=[HBM_ONLY] * n,
            out_specs=[SEM_SPEC, SEM_SPEC] + [HBM_ONLY] * n + [pl.BlockSpec(memory_space=pltpu.VMEM)],
            out_shape=[sem_shape, sem_shape] + [pltpu.HBM(a.shape, a.dtype) for a in lands]
                      + [jax.ShapeDtypeStruct((8, LANES), F32)],
            input_output_aliases={i: i + 2 for i in range(n)},
            compiler_params=pltpu.CompilerParams(has_side_effects=SIDE_EFFECT, collective_id=collective_id),
        )(*[_in_hbm(a) for a in lands])
        self.send_sems, self.recv_sems, self.lands, self.token = outs[0], outs[1], list(outs[2:2 + n]), outs[-1]

    def _copies(self, land_refs, send_sems, recv_sems):
        x, y, c = lax.axis_index("x"), lax.axis_index("y"), lax.axis_index("c")
        me = 2 * x + y
        copies = []
        for a in range(self.n):
            for k in range(1, N_CHIPS):
                t = (me + k) % N_CHIPS
                slice_t = _piece_dyn(land_refs[a], self.kinds[a], t)
                got = _half(slice_t, c)
                copies.append(pltpu.make_async_remote_copy(
                    src_ref=got, dst_ref=got, send_sem=send_sems.at[a * N_CHIPS + k],
                    recv_sem=recv_sems.at[a * N_CHIPS + k], device_id=(x, y, 1 - c), device_id_type=MESH))
        return copies

    def wait(self, after):
        n = self.n

        def body(*refs):
            land_refs, send_sems, recv_sems = refs[:n], refs[n], refs[n + 1]
            for cp in self._copies(land_refs, send_sems, recv_sems):
                cp.wait_send()
                cp.wait_recv()

        operands = [_in_hbm(a) for a in self.lands] + [self.send_sems, self.recv_sems]
        in_specs = [HBM_ONLY] * n + [SEM_SPEC, SEM_SPEC]
        if after is not None:
            operands.append(after)
            in_specs.append(HBM_SPEC)
        outs = pl.pallas_call(
            body, name=self.name + "_wait",
            in_specs=in_specs, out_specs=[HBM_ONLY] * n,
            out_shape=[pltpu.HBM(a.shape, a.dtype) for a in self.lands],
            input_output_aliases={i: i for i in range(n)},
            compiler_params=pltpu.CompilerParams(has_side_effects=SIDE_EFFECT),
        )(*operands)
        return list(outs)


class _Exchange:
    def __init__(self, mode, srcs, lands, kinds, layers, name, collective_id, after=None, halves=None):
        self.mode, self.kinds, self.layers, self.name, self.n = mode, kinds, layers, name, len(lands)
        self.halves = halves if halves is not None else [False] * len(lands)
        n, ns = self.n, len(srcs)
        n_in = ns + n + (after is not None)
        sem_shape = pltpu.SemaphoreType.DMA((n * N_CHIPS,))

        def body(*refs):
            src_refs, land_refs = refs[:ns], refs[ns:ns + n]
            send_sems, recv_sems = refs[n_in], refs[n_in + 1]
            token = refs[-1]
            c = lax.axis_index("c")
            me = 2 * lax.axis_index("x") + lax.axis_index("y")
            barrier = pltpu.get_barrier_semaphore()
            for k in range(1, N_CHIPS):
                t = (me + k) % N_CHIPS
                pl.semaphore_signal(barrier, inc=1, device_id=(t // 2, t % 2, c), device_id_type=MESH)
            pl.semaphore_wait(barrier, N_CHIPS - 1)
            for j in range(N_CHIPS):
                @pl.when(me == j)
                def _():
                    for a in range(n):
                        for t in range(N_CHIPS):
                            if t != j:
                                src, dst = self._ends(src_refs, land_refs, a, j, t, c)
                                pltpu.make_async_remote_copy(
                                    src_ref=src, dst_ref=dst, send_sem=send_sems.at[a * N_CHIPS + t],
                                    recv_sem=recv_sems.at[a * N_CHIPS + j],
                                    device_id=_chip_of(t, c), device_id_type=MESH).start()
            token[...] = jnp.zeros(token.shape, token.dtype)

        arrays = list(srcs) + list(lands)
        operands = [_in_hbm(a) for a in arrays]
        in_specs = [HBM_ONLY] * (ns + n)
        if after is not None:
            operands.append(after)
            in_specs.append(HBM_SPEC)
        outs = pl.pallas_call(
            body, name=name + "_start",
            in_specs=in_specs,
            out_specs=[SEM_SPEC, SEM_SPEC] + [HBM_ONLY] * (ns + n) + [pl.BlockSpec(memory_space=pltpu.VMEM)],
            out_shape=[sem_shape, sem_shape] + [pltpu.HBM(a.shape, a.dtype) for a in arrays]
                      + [jax.ShapeDtypeStruct((8, LANES), F32)],
            input_output_aliases={i: i + 2 for i in range(ns + n)},
            compiler_params=pltpu.CompilerParams(has_side_effects=SIDE_EFFECT, collective_id=collective_id),
        )(*operands)
        self.send_sems, self.recv_sems = outs[0], outs[1]
        self.srcs, self.lands = list(outs[2:2 + ns]), list(outs[2 + ns:2 + ns + n])
        self.token = outs[-1]

    def _ends(self, src_refs, land_refs, a, me_j, peer, c):
        if self.mode == "gather":
            mine = _piece(land_refs[a], self.kinds[a], me_j)
            if self.halves[a]:
                mine = _half(mine, c)
            return mine, mine
        return _piece(src_refs[a], self.kinds[a], peer), land_refs[a].at[me_j, pl.ds(self.layers[a], 1)]

    def wait(self, after, lands=None):
        n, ns = self.n, len(self.srcs)
        lands = self.lands if lands is None else lands

        def body(*refs):
            src_refs, land_refs = refs[:ns], refs[ns:ns + n]
            send_sems, recv_sems = refs[ns + n], refs[ns + n + 1]
            c = lax.axis_index("c")
            me = 2 * lax.axis_index("x") + lax.axis_index("y")
            for j in range(N_CHIPS):
                @pl.when(me != j)
                def _():
                    for a in range(n):
                        sent, _ = self._ends(src_refs, land_refs, a, 0, j, c)
                        _, landed = self._ends(src_refs, land_refs, a, j, 0, c)
                        cp = pltpu.make_async_remote_copy(
                            src_ref=sent, dst_ref=landed, send_sem=send_sems.at[a * N_CHIPS + j],
                            recv_sem=recv_sems.at[a * N_CHIPS + j],
                            device_id=_chip_of(j, c), device_id_type=MESH)
                        cp.wait_send()
                        cp.wait_recv()

        arrays = self.srcs + list(lands)
        operands = [_in_hbm(a) for a in arrays] + [self.send_sems, self.recv_sems]
        in_specs = [HBM_ONLY] * (ns + n) + [SEM_SPEC, SEM_SPEC]
        if after is not None:
            operands.append(after)
            in_specs.append(HBM_SPEC)
        outs = pl.pallas_call(
            body, name=self.name + "_wait",
            in_specs=in_specs, out_specs=[HBM_ONLY] * (ns + n),
            out_shape=[pltpu.HBM(a.shape, a.dtype) for a in arrays],
            input_output_aliases={i: i for i in range(ns + n)},
            compiler_params=pltpu.CompilerParams(has_side_effects=SIDE_EFFECT),
        )(*operands)
        return list(outs[:ns]), list(outs[ns:])


def _sum_arrivals(zone, own_grads, kind, chip, name, after=None):
    _, layers, r, c = zone.shape
    tm = _pick_rows(r, 256)
    nb = r // tm

    def own_idx(l, i, chip_ref):
        return (0, chip_ref[0] * nb + i, 0) if kind == "row" else (0, i, _slot(kind, chip_ref[0]))

    def slot_idx(k):
        return lambda l, i, chip_ref: (jnp.where(chip_ref[0] == k, (k + 1) % N_CHIPS, k), l, i, 0)

    in_specs = [pl.BlockSpec((None, None, tm, c), slot_idx(k)) for k in range(N_CHIPS)]
    in_specs += [pl.BlockSpec((None, tm, c), own_idx) for _ in own_grads]
    operands = [zone] * N_CHIPS + list(own_grads)
    if after is not None:
        in_specs.append(HBM_SPEC)
        operands.append(after)

    def body(chip_ref, *refs):
        slot_refs, own_refs, o_ref = refs[:N_CHIPS], refs[N_CHIPS:N_CHIPS + layers], refs[-1]
        own = own_refs[0][...]
        for u in range(1, layers):
            own = jnp.where(pl.program_id(0) == u, own_refs[u][...], own)
        acc = None
        for k in range(N_CHIPS):
            term = jnp.where(chip_ref[0] == k, own, slot_refs[k][...]).astype(F32)
            acc = term if acc is None else acc + term
        o_ref[...] = acc.astype(o_ref.dtype)

    return pl.pallas_call(
        body, name=name,
        grid_spec=pltpu.PrefetchScalarGridSpec(
            num_scalar_prefetch=1, grid=(layers, nb), in_specs=in_specs,
            out_specs=pl.BlockSpec((tm, c), lambda l, i, chip_ref: (l * nb + i, 0))),
        out_shape=jax.ShapeDtypeStruct((layers * r, c), BF16),
        compiler_params=_params(("arbitrary", "arbitrary")),
    )(chip, *operands)


class _SiblingSwap:
    def __init__(self, arrays, name, collective_id, after=None):
        self.name, self.n = name, len(arrays)
        n = self.n
        n_in = n + (after is not None)
        sem_shape = pltpu.SemaphoreType.DMA((n,))

        def body(*refs):
            ins, send_sems, recv_sems = refs[:n], refs[n_in], refs[n_in + 1]
            theirs, token = refs[n_in + 2 + n:n_in + 2 + 2 * n], refs[-1]
            _sibling_handshake()
            for cp in self._copies(ins, theirs, send_sems, recv_sems):
                cp.start()
            token[...] = jnp.zeros(token.shape, token.dtype)

        operands, in_specs = [_in_hbm(a) for a in arrays], [HBM_ONLY] * n
        if after is not None:
            operands.append(after)
            in_specs.append(HBM_SPEC)
        outs = pl.pallas_call(
            body, name=name + "_start",
            in_specs=in_specs,
            out_specs=[SEM_SPEC, SEM_SPEC] + [HBM_ONLY] * (2 * n) + [pl.BlockSpec(memory_space=pltpu.VMEM)],
            out_shape=[sem_shape, sem_shape] + [pltpu.HBM(a.shape, a.dtype) for a in arrays] * 2
                      + [jax.ShapeDtypeStruct((8, LANES), F32)],
            input_output_aliases={i: i + 2 for i in range(n)},
            compiler_params=pltpu.CompilerParams(has_side_effects=SIDE_EFFECT, collective_id=collective_id),
        )(*operands)
        self.send_sems, self.recv_sems = outs[0], outs[1]
        self.mine, self.theirs, self.token = list(outs[2:2 + n]), list(outs[2 + n:2 + 2 * n]), outs[-1]

    def _copies(self, mine, theirs, send_sems, recv_sems):
        sibling = (lax.axis_index("x"), lax.axis_index("y"), 1 - lax.axis_index("c"))
        return [pltpu.make_async_remote_copy(src_ref=mine[a], dst_ref=theirs[a], send_sem=send_sems.at[a],
                                             recv_sem=recv_sems.at[a], device_id=sibling, device_id_type=MESH)
                for a in range(self.n)]

    def wait(self, after):
        n = self.n

        def body(*refs):
            for cp in self._copies(refs[:n], refs[n:2 * n], refs[2 * n], refs[2 * n + 1]):
                cp.wait_send()
                cp.wait_recv()

        arrays = self.mine + self.theirs
        outs = pl.pallas_call(
            body, name=self.name + "_wait",
            in_specs=[HBM_ONLY] * (2 * n) + [SEM_SPEC, SEM_SPEC, HBM_SPEC], out_specs=[HBM_ONLY] * (2 * n),
            out_shape=[pltpu.HBM(a.shape, a.dtype) for a in arrays],
            input_output_aliases={i: i for i in range(2 * n)},
            compiler_params=pltpu.CompilerParams(has_side_effects=SIDE_EFFECT),
        )(*[_in_hbm(a) for a in arrays], self.send_sems, self.recv_sems, after)
        return list(outs[:n]), list(outs[n:])


class _GatherDevices:
    def __init__(self, vec):
        sem_shape = pltpu.SemaphoreType.DMA((N_DEV,))

        def body(in_ref, send_sems, recv_sems, vec_ref, out_ref, token):
            for cp in self._copies(in_ref, out_ref, send_sems, recv_sems):
                cp.start()
            token[...] = jnp.zeros(token.shape, token.dtype)

        outs = pl.pallas_call(
            body, name="gather_small_start",
            in_specs=[HBM_ONLY],
            out_specs=[SEM_SPEC, SEM_SPEC, HBM_ONLY, HBM_ONLY, pl.BlockSpec(memory_space=pltpu.VMEM)],
            out_shape=[sem_shape, sem_shape, pltpu.HBM(vec.shape, vec.dtype),
                       pltpu.HBM((N_DEV,) + vec.shape, vec.dtype), jax.ShapeDtypeStruct((8, LANES), F32)],
            input_output_aliases={0: 2},
            compiler_params=pltpu.CompilerParams(has_side_effects=SIDE_EFFECT),
        )(_in_hbm(vec))
        self.send_sems, self.recv_sems, self.vec, self.rows, self.token = outs

    def _copies(self, in_ref, out_ref, send_sems, recv_sems):
        x, y, c = lax.axis_index("x"), lax.axis_index("y"), lax.axis_index("c")
        me = 4 * x + 2 * y + c
        copies = [pltpu.make_async_copy(in_ref, out_ref.at[me], recv_sems.at[0])]
        for rel in range(1, N_DEV):
            peer = (x ^ (rel >> 2), y ^ ((rel >> 1) & 1), c ^ (rel & 1))
            copies.append(pltpu.make_async_remote_copy(
                src_ref=in_ref, dst_ref=out_ref.at[me], send_sem=send_sems.at[rel], recv_sem=recv_sems.at[rel],
                device_id=peer, device_id_type=MESH))
        return copies

    def wait(self, after):
        def body(vec_ref, rows_ref, send_sems, recv_sems, after_ref, vec_out, rows_out):
            copies = self._copies(vec_ref, rows_ref, send_sems, recv_sems)
            copies[0].wait()
            for cp in copies[1:]:
                cp.wait_send()
                cp.wait_recv()

        outs = pl.pallas_call(
            body, name="gather_small_wait",
            in_specs=[HBM_ONLY, HBM_ONLY, SEM_SPEC, SEM_SPEC, HBM_SPEC], out_specs=[HBM_ONLY, HBM_ONLY],
            out_shape=[pltpu.HBM(self.vec.shape, self.vec.dtype), pltpu.HBM(self.rows.shape, self.rows.dtype)],
            input_output_aliases={0: 0, 1: 1},
            compiler_params=pltpu.CompilerParams(has_side_effects=SIDE_EFFECT),
        )(_in_hbm(self.vec), _in_hbm(self.rows), self.send_sems, self.recv_sems, after)
        return outs[1]


BIG = [("a_w_in", "col"), ("a_w_out", "row"), ("kv_w", "row"), ("b_w_q", "row"), ("b_w_out", "row"),
       ("ffn_w_gate_up", "colp"), ("ffn_w_down", "row"), ("ple_w_up", "col"), ("ple_w_gate", "row")]
GATHER_GROUPS = [[("a_w_in", 0), ("small", 0)], [("a_w_out", 0), ("ffn_w_gate_up", 0)],
                 [("ffn_w_down", 0), ("ple_w_gate", 0), ("ple_w_up", 0)], [("kv_w", 0), ("b_w_q", 0), ("b_w_out", 0)],
                 [("ffn_w_gate_up", 1)], [("ffn_w_down", 1), ("ple_w_gate", 1), ("ple_w_up", 1)]]
SCATTER_GROUPS = [[("ple_w_gate", 1), ("ple_w_up", 1), ("ffn_w_down", 1)], [("ffn_w_gate_up", 1)],
                  [("b_w_out", 0), ("b_w_q", 0), ("kv_w", 0)], [("ple_w_gate", 0), ("ple_w_up", 0), ("ffn_w_down", 0)],
                  [("ffn_w_gate_up", 0), ("a_w_out", 0)], [("a_w_in", 0)]]
COLLECTIVE_IDS = {"fill": 0, "swap": 6, "gather": 9, "scatter": 15}
SMALL_SHARDED = ["ln_gain", "ln_bias", "a_lower_bound"]
SMALL_REPLICATED = ["a_norm_gain", "kv_b", "b_b_q", "b_sinks", "b_b_out", "ple_b_gate"]
WEIGHT_ORDER = ["a_w_in", "a_lower_bound", "a_norm_gain", "a_w_out", "kv_w", "kv_b", "b_w_q", "b_b_q", "b_sinks",
                "b_w_out", "b_b_out", "ffn_w_gate_up", "ffn_w_down", "ple_w_up", "ple_w_gate", "ple_b_gate",
                "ln_gain", "ln_bias"]


def _as3(a):
    return a.reshape((-1,) + a.shape[-2:]) if a.ndim >= 3 else a.reshape((1,) + a.shape)


def _pad_lanes(v):
    n = v.shape[-1]
    return jnp.pad(v, ((0, 0), (0, (-n) % LANES)))


def _adam_small_fn(w, mom, vel, g):
    return _adam_fn(w, mom, vel, g, jnp.zeros_like(g))[1:]


def _sum_rows_fn(slots):
    acc = slots[0]
    for s in range(1, slots.shape[0]):
        acc = acc + slots[s]
    return (acc,)


def kernel(x, p, a_w_in, a_lower_bound, a_norm_gain, a_w_out, kv_w, kv_b, b_w_q, b_b_q, b_sinks, b_w_out, b_b_out, ffn_w_gate_up, ffn_w_down, ple_w_up, ple_w_gate, ple_b_gate, ln_gain, ln_bias, loss_target, m_a_w_in, m_a_lower_bound, m_a_norm_gain, m_a_w_out, m_kv_w, m_kv_b, m_b_w_q, m_b_b_q, m_b_sinks, m_b_w_out, m_b_b_out, m_ffn_w_gate_up, m_ffn_w_down, m_ple_w_up, m_ple_w_gate, m_ple_b_gate, m_ln_gain, m_ln_bias, v_a_w_in, v_a_lower_bound, v_a_norm_gain, v_a_w_out, v_kv_w, v_kv_b, v_b_w_q, v_b_b_q, v_b_sinks, v_b_w_out, v_b_b_out, v_ffn_w_gate_up, v_ffn_w_down, v_ple_w_up, v_ple_w_gate, v_ple_b_gate, v_ln_gain, v_ln_bias):
    args = dict(locals())
    wts = {n: args[n] for n in WEIGHT_ORDER}
    mom = {n: args["m_" + n] for n in WEIGHT_ORDER}
    vel = {n: args["v_" + n] for n in WEIGHT_ORDER}
    chip = 2 * lax.axis_index("x") + lax.axis_index("y")
    d = x.shape[-1]
    dq = d // N_CHIPS

    kind_of = dict(BIG)
    kind_of["small"] = "col"
    chip_arr = chip.reshape(1).astype(jnp.int32)
    small_pack = jnp.concatenate([wts[n].reshape(-1, dq) for n in SMALL_SHARDED], axis=0)[None]

    def place_item(key):
        n, layer = key
        if n == "small":
            return small_pack, 0, "col", F32
        return _as3(wts[n]), layer, kind_of[n], BF16

    gathers, where = [], {}
    for gi, group in enumerate(GATHER_GROUPS):
        prev = gathers[-1].token if gathers else None
        placed = _place([place_item(k) for k in group], chip_arr, name=f"place{gi}", after=prev)
        gathers.append(_Exchange("gather", [], placed, [kind_of[k[0]] for k in group],
                                 [0] * len(group), f"gather{gi}", COLLECTIVE_IDS["gather"] + gi, after=prev,
                                 halves=[k[0] != "small" for k in group]))
        for k in group:
            where[k] = gi
    all_started = gathers[-1].token
    ready = {}

    fills = {}

    def pass_on(gi, after):
        if gi not in fills:
            group = GATHER_GROUPS[gi]
            outs = gathers[gi].wait(after)[1]
            split = [i for i, k in enumerate(group) if k[0] != "small"]
            fills[gi] = (outs, split, _SiblingFill([outs[i] for i in split], [kind_of[group[i][0]] for i in split],
                                                   f"fill{gi}", COLLECTIVE_IDS["fill"] + gi))

    def wget(name, layer, after):
        key = (name, layer)
        if key not in ready:
            gi = where[key]
            after = all_started if gi == 0 else after
            pass_on(gi, after)
            if 1 <= gi < len(GATHER_GROUPS) - 1:
                pass_on(gi + 1, after)
                after = fills[gi + 1][2].token
            outs, split, fill = fills[gi]
            for i, arr in zip(split, fill.wait(after)):
                outs[i] = arr
            for k, arr in zip(GATHER_GROUPS[gi], outs):
                ready[k] = arr
        return ready[key]

    small_full = wget("small", 0, None)[0]
    ln_gain_f = small_full[0:6].reshape(DEPTH, 3, d)
    ln_bias_f = small_full[6:12].reshape(DEPTH, 3, d)
    alb_f = small_full[12:14]

    group_of = {k: gi for gi, group in enumerate(SCATTER_GROUPS) for k in group}
    grads_done, zones, scatters = {}, {}, []

    def grad_sink(name, layer, grad):
        grads_done[(name, layer)] = grad
        if name not in zones:
            zones[name] = lax.empty((N_CHIPS,) + _as3(wts[name]).shape, BF16)
        gi = group_of[(name, layer)]
        group = SCATTER_GROUPS[gi]
        if not all(k in grads_done for k in group):
            return None
        ex = _Exchange("scatter", [grads_done[k] for k in group], [zones[k[0]] for k in group],
                       [kind_of[k[0]] for k in group], [k[1] for k in group], f"scatter{gi}",
                       COLLECTIVE_IDS["scatter"] + gi)
        for k, zone in zip(group, ex.lands):
            zones[k[0]] = zone
        scatters.append((ex, group))
        return ex.token

    small = {}

    def small_sink(loss, gs):
        ln_g = jnp.concatenate([gs[f"ln_gain_{i}_{j}"] for i in range(DEPTH) for j in range(3)], axis=0)
        ln_b = jnp.concatenate([gs[f"ln_bias_{i}_{j}"] for i in range(DEPTH) for j in range(3)], axis=0)
        ple_bg = jnp.concatenate([gs[f"ple_b_{i}"] for i in range(DEPTH)], axis=0)
        small["list"] = [ln_g.reshape(1, -1), ln_b.reshape(1, -1), gs["alb"].reshape(1, -1), gs["norm_gain"],
                         gs["kv_b"], gs["b_q"], _pad_lanes(gs["sinks"]), gs["b_out"], ple_bg.reshape(1, -1), loss]
        small["gather"] = _GatherDevices(jnp.concatenate(small["list"], axis=1))
        return small["gather"].token

    loss, grad_x, gs = _local_step(
        x[0], p.reshape((p.shape[0],) + p.shape[2:]), loss_target[0], wget, grad_sink, ln_gain_f, ln_bias_f, alb_f, a_norm_gain, kv_b, b_b_q,
        b_sinks, b_b_out, ple_b_gate, small_sink)

    res = {}

    def arrive(batch, after):
        for ex, group in batch:
            srcs, outs = ex.wait(after, lands=[zones[k[0]] for k in group])
            for k, grad, zone in zip(group, srcs, outs):
                grads_done[k], zones[k[0]] = grad, zone

    def half_sums(names, batch, after):
        partial = []
        for n in names:
            own = [grads_done[(n, layer)] for layer in range(zones[n].shape[1])]
            partial.append(_sum_arrivals(zones[n], own, kind_of[n], chip_arr, f"sum_{n}", after=after))
        return _SiblingSwap(partial, f"swap{batch}", COLLECTIVE_IDS["swap"] + batch, after=after)

    def update(names, swap, after):
        for n, own, sib in zip(names, *swap.wait(after)):
            shp = wts[n].shape
            flat = lambda a: a.reshape(-1, shp[-1])
            out = _rowwise(_adam_fn, [flat(wts[n]), flat(mom[n]), flat(vel[n]), own, sib], [],
                           [(own.shape, F32)] * 4, name=f"adam_{n}")
            res[n] = [o.reshape(shp) for o in out]
        return res[names[-1]][1]

    last_names = [k[0] for k in SCATTER_GROUPS[-1]]
    batches = [["ffn_w_gate_up"], [n for n, _ in BIG if n != "ffn_w_gate_up" and n not in last_names], last_names]
    arrive(scatters[:-1], grad_x)
    swap0 = half_sums(batches[0], 0, None)
    swap1 = half_sums(batches[1], 1, swap0.token)
    updated = update(batches[0], swap0, swap1.token)
    arrive(scatters[-1:], updated)
    swap2 = half_sums(batches[2], 2, swap1.token)
    updated = update(batches[1], swap1, swap2.token)
    update(batches[2], swap2, updated)

    small_list = small["list"]
    everyone = small["gather"].wait(grad_x)
    total, = _rowwise(_sum_rows_fn, [everyone], [], [(everyone.shape[1:], F32)], name="sum_small")
    offs, pos = [], 0
    for v in small_list:
        offs.append((pos, v.shape[1]))
        pos += v.shape[1]

    def seg(k):
        return total[0, offs[k][0]:offs[k][0] + offs[k][1]]

    def my_cols(full, rows):
        return lax.dynamic_slice_in_dim(full.reshape(rows, N_CHIPS, dq), chip, 1, axis=1).reshape(rows, dq)

    n_sink = b_sinks.shape[-1]
    small_grads = {
        "ln_gain": my_cols(seg(0), 6).reshape(ln_gain.shape), "ln_bias": my_cols(seg(1), 6).reshape(ln_bias.shape),
        "a_lower_bound": my_cols(seg(2), 2), "a_norm_gain": seg(3).reshape(a_norm_gain.shape),
        "kv_b": seg(4).reshape(kv_b.shape), "b_b_q": seg(5).reshape(b_b_q.shape),
        "b_sinks": seg(6)[:n_sink].reshape(b_sinks.shape), "b_b_out": seg(7).reshape(b_b_out.shape),
        "ple_b_gate": seg(8).reshape(ple_b_gate.shape)}
    names = SMALL_SHARDED + SMALL_REPLICATED
    pack = lambda dct: _pad_lanes(jnp.concatenate([dct[n].reshape(1, -1) for n in names], axis=1))
    g_pack = pack(small_grads)
    upd = _rowwise(_adam_small_fn, [pack(wts), pack(mom), pack(vel), g_pack], [], [(g_pack.shape, F32)] * 3,
                   name="adam_small")
    pos = 0
    for n in names:
        size = wts[n].size
        res[n] = [small_grads[n]] + [u[0, pos:pos + size].reshape(wts[n].shape) for u in upd]
        pos += size

    outs = [seg(9)[0], grad_x[None]]
    for k in range(4):
        outs += [res[n][k] for n in WEIGHT_ORDER]
    return tuple(outs)
```

```python
import functools

import jax
import jax.numpy as jnp
from jax import lax
from jax.experimental import pallas as pl
from jax.experimental.pallas import tpu as pltpu

F32 = jnp.float32
BF16 = jnp.bfloat16
MESH = pl.DeviceIdType.MESH

LANES = 128
HG_DK = 128
HG_CHUNK = 64
HG_SUB = 16
HG_ROWS = 512
HG_HEADS_PER_STEP = 2
LOG2_E = 1.4426950408889634
ATT_HD = 64
ATT_G = 4
WINDOW = 128
DEPTH = 2
ALPHA = (2.0 * DEPTH) ** 0.25
LN_EPS = 1e-5
RMS_EPS = 1e-6
ADAM_LR, ADAM_B1, ADAM_B2, ADAM_EPS, ADAM_WD, ADAM_STEP = 0.001, 0.9, 0.999, 1e-08, 0.01, 10
N_CHIPS = 4
N_DEV = 8
VMEM_LIMIT = 56 * 1024 * 1024
NEG = -1e30


def _pick(n, cap):
    best = None
    for d in range(LANES, min(n, cap) + 1, LANES):
        if n % d == 0:
            best = d
    return n if best is None else best


def _pick_rows(m, cap):
    best = None
    for d in range(16, min(m, cap) + 1, 16):
        if m % d == 0:
            best = d
    return m if best is None else best


def _params(sem):
    return pltpu.CompilerParams(dimension_semantics=sem, vmem_limit_bytes=VMEM_LIMIT)


def _zeros_index(ndim, grid_rank=3):
    return (lambda i, j, kk: (0,) * ndim) if grid_rank == 3 else (lambda kk, i: (0,) * ndim)


def _mm(a, b, *, name, la=None, lb=None, ta=False, tb=False, bias=None, add=None, out_dtype=F32,
        out_layers=None, out_layer=None, after=None, post=None, tile_cols=None, caps=(1024, 1536, 2048),
        a_parts=None, b_parts=None):
    ar, ac = a.shape[-2:]
    br, bc = b.shape[-2:]
    assert a_parts is None or (not ta and la is None and a.shape[0] == a_parts)
    assert b_parts is None or (not tb and lb is None and b.shape[0] == b_parts)
    m, k = (ac, ar) if ta else (ar, ac * (a_parts or 1))
    k2, n = (bc, br) if tb else (br, bc * (b_parts or 1))
    assert k == k2, (a.shape, b.shape, ta, tb)
    if post is not None:
        caps = (512, n if tile_cols is None else tile_cols, caps[2])
    tm, tn, tk = _pick(m, caps[0]), _pick(bc if b_parts else n, caps[1]), _pick(ac if a_parts else k, caps[2])
    assert post is None or tn == caps[1]
    nk = k // tk
    gi, gj = m // tm, n // tn
    a_bytes, b_bytes = m * k * a.dtype.itemsize, k * n * b.dtype.itemsize
    rows_outer = (a_bytes + b_bytes * (gi if gj * nk > 1 else 1)) <= (b_bytes + a_bytes * (gj if gi * nk > 1 else 1))
    k_outer = post is not None and nk > 1 and gj == 1
    grid = (nk, gi) if k_outer else (gi, gj, nk) if rows_outer else (gj, gi, nk)
    keep_at = ta and nk == 1 and gj > 1 and rows_outer

    def bs(block, idx, late=False):
        if k_outer:
            return pl.BlockSpec(block, lambda kk, i: idx(jnp.where(kk == nk - 1, i, 0) if late else i, 0, kk))
        return pl.BlockSpec(block, idx if rows_outer else (lambda q, p, kk: idx(p, q, kk)))

    def spec(block, idx, layer):
        if layer is None:
            return bs(block, idx)
        return bs((None,) + block, lambda i, j, kk: (layer,) + idx(i, j, kk))

    a_spec = spec((tk, tm), lambda i, j, kk: (kk, i), la) if ta else spec((tm, tk), lambda i, j, kk: (i, kk), la)
    b_spec = spec((tn, tk), lambda i, j, kk: (j, kk), lb) if tb else spec((tk, tn), lambda i, j, kk: (kk, j), lb)
    if a_parts:
        a_spec = bs((None, tm, tk), lambda i, j, kk: (kk // (ac // tk), i, kk % (ac // tk)))
    if b_parts:
        b_spec = bs((None, tk, tn), lambda i, j, kk: (j // (bc // tn), kk, j % (bc // tn)))
    in_specs, operands = [a_spec, b_spec], [a, b]
    if bias is not None:
        in_specs.append(bs((1, tn), lambda i, j, kk: (0, j)))
        operands.append(bias)
    if add is not None:
        in_specs.append(bs((tm, tn), lambda i, j, kk: (i, j), late=True))
        operands.append(add)
    if after is not None:
        in_specs.append(pl.BlockSpec(memory_space=pl.ANY))
        operands.append(after)
    dims = (((0 if ta else 1,), (1 if tb else 0,)), ((), ()))
    has_bias, has_add = bias is not None, add is not None
    if post is None:
        fn, rows, whole, outs, sums = None, [], [], [], []
        out_shape = jax.ShapeDtypeStruct((m, n) if out_layers is None else (out_layers, m, n), out_dtype)
        out_specs = spec((tm, tn), lambda i, j, kk: (i, j), out_layer)
    else:
        fn, rows, whole, outs, sums = post
        in_specs += [bs((tm, r.shape[-1] // gj), lambda i, j, kk: (i, j), late=True) for r in rows]
        in_specs += [pl.BlockSpec(tuple(w.shape), _zeros_index(w.ndim, len(grid))) for w in whole]
        operands += list(rows) + list(whole)
        out_shape = [jax.ShapeDtypeStruct(sh, dt) for sh, dt in list(outs) + list(sums)]
        out_specs = ([bs((tm, sh[-1] // gj), lambda i, j, kk: (i, j), late=True) for sh, _ in outs]
                     + [pl.BlockSpec(tuple(sh), _zeros_index(len(sh), len(grid))) for sh, _ in sums])
    n_in, n_extra, n_outs, n_sums = len(operands), len(rows) + len(whole), len(outs), len(sums)

    def body(*refs):
        a_ref, b_ref = refs[0], refs[1]
        pos = 2
        bias_ref = add_ref = None
        if has_bias:
            bias_ref = refs[pos]
            pos += 1
        if has_add:
            add_ref = refs[pos]
            pos += 1
        extra_refs = refs[n_in - n_extra:n_in]
        out_refs = refs[n_in:n_in + max(n_outs, 1)]
        sum_refs = refs[n_in + n_outs:n_in + n_outs + n_sums]
        acc_ref = refs[-1] if nk > 1 else None
        if keep_at:
            at_ref = refs[-1]

            @pl.when(pl.program_id(1) == 0)
            def _():
                at_ref[...] = a_ref[...].astype(BF16).T

            part = lax.dot_general(at_ref[...], b_ref[...].astype(BF16), (((1,), (1 if tb else 0,)), ((), ())),
                                   preferred_element_type=F32)
        else:
            part = lax.dot_general(a_ref[...].astype(BF16), b_ref[...].astype(BF16), dims,
                                   preferred_element_type=F32)

        def finish(total):
            if has_bias:
                total = total + bias_ref[...]
            if has_add:
                total = total + add_ref[...]
            if fn is None:
                out_refs[0][...] = total.astype(out_refs[0].dtype)
                return
            res = fn(total, *[r[...] for r in extra_refs])
            for ref, val in zip(out_refs, res[:n_outs]):
                ref[...] = val.astype(ref.dtype)
            if n_sums:
                @pl.when(pl.program_id(1 if k_outer or not rows_outer else 0) == 0)
                def _():
                    for ref in sum_refs:
                        ref[...] = jnp.zeros(ref.shape, ref.dtype)

                for ref, val in zip(sum_refs, res[n_outs:]):
                    ref[...] += val

        if nk == 1:
            finish(part)
        elif k_outer:
            kk = pl.program_id(0)
            rows_i = pl.ds(pl.multiple_of(pl.program_id(1) * tm, tm), tm)

            @pl.when(kk == 0)
            def _():
                acc_ref[rows_i, :] = part

            @pl.when(kk > 0)
            def _():
                acc_ref[rows_i, :] += part

            @pl.when(kk == nk - 1)
            def _():
                finish(acc_ref[rows_i, :])
        else:
            kk = pl.program_id(2)

            @pl.when(kk == 0)
            def _():
                acc_ref[...] = part

            @pl.when(kk > 0)
            def _():
                acc_ref[...] += part

            @pl.when(kk == nk - 1)
            def _():
                finish(acc_ref[...])

    return pl.pallas_call(
        body, name=name, grid=grid, in_specs=in_specs, out_specs=out_specs, out_shape=out_shape,
        scratch_shapes=([pltpu.VMEM((m, n) if k_outer else (tm, tn), F32)] if nk > 1
                        else [pltpu.VMEM((tm, tk), BF16)] if keep_at else []),
        compiler_params=_params(("arbitrary", "arbitrary") if k_outer
                                else ("arbitrary" if n_sums else "parallel", "arbitrary" if keep_at else "parallel",
                                      "arbitrary") if rows_outer
                                else ("parallel", "arbitrary" if n_sums else "parallel", "arbitrary")),
    )(*operands)


def _rowwise(fn, rows, whole, outs, sums=(), *, name, tm=256):
    m = rows[0].shape[-2]
    tm = _pick_rows(m, tm)
    n_rows, n_whole, n_outs, n_sums = len(rows), len(whole), len(outs), len(sums)

    def rspec(shape):
        lead = len(shape) - 2
        return pl.BlockSpec(tuple(shape[:-2]) + (tm, shape[-1]), lambda i: (0,) * lead + (i, 0))

    def wspec(shape):
        return pl.BlockSpec(tuple(shape), lambda i: (0,) * len(shape))

    def body(*refs):
        vals = [r[...] for r in refs[:n_rows + n_whole]]
        out_refs = refs[n_rows + n_whole:n_rows + n_whole + n_outs]
        sum_refs = refs[n_rows + n_whole + n_outs:]
        res = fn(*vals)
        for ref, val in zip(out_refs, res[:n_outs]):
            ref[...] = val.astype(ref.dtype)
        if n_sums:
            @pl.when(pl.program_id(0) == 0)
            def _():
                for ref in sum_refs:
                    ref[...] = jnp.zeros(ref.shape, ref.dtype)

            for ref, val in zip(sum_refs, res[n_outs:]):
                ref[...] += val

    result = pl.pallas_call(
        body, name=name, grid=(m // tm,),
        in_specs=[rspec(r.shape) for r in rows] + [wspec(w.shape) for w in whole],
        out_specs=[rspec(s) for s, _ in outs] + [wspec(s) for s, _ in sums],
        out_shape=[jax.ShapeDtypeStruct(s, d) for s, d in list(outs) + list(sums)],
        compiler_params=_params(("arbitrary",)),
    )(*rows, *whole)
    return result


def _sigmoid(v):
    return jax.nn.sigmoid(v)


def _col_sum(v):
    return jnp.sum(v, axis=0, keepdims=True)


def _ln_stats(z):
    mu = jnp.mean(z, axis=-1, keepdims=True)
    zc = z - mu
    var = jnp.mean(zc * zc, axis=-1, keepdims=True)
    rstd = lax.rsqrt(var + LN_EPS)
    return zc * rstd, rstd


def _ln_fwd_fn(xin, h, gain, bias):
    xhat, _ = _ln_stats(ALPHA * xin + h)
    y = xhat * gain + bias
    return y, y


def _ple_ln_fwd_fn(xin, pg, pu, gain, bias):
    xhat, _ = _ln_stats(ALPHA * xin + _sigmoid(pg) * pu)
    y = xhat * gain + bias
    return y, y


def _ln_dz(dy, z, gain):
    xhat, rstd = _ln_stats(z)
    dxhat = dy * gain
    dz = rstd * (dxhat - jnp.mean(dxhat, axis=-1, keepdims=True)
                 - xhat * jnp.mean(dxhat * xhat, axis=-1, keepdims=True))
    return dz, _col_sum(dy * xhat), _col_sum(dy)


def _ln_bwd_fn(dy, xin, h, gain):
    dz, dgain, dbias = _ln_dz(dy, ALPHA * xin + h, gain)
    return ALPHA * dz, dz, dgain, dbias, _col_sum(dz)


def _ple_ln_bwd_fn(dy, xin, pg, pu, gain):
    sg = _sigmoid(pg)
    dz, dgain, dbias = _ln_dz(dy, ALPHA * xin + sg * pu, gain)
    dpg = dz * pu * sg * (1.0 - sg)
    return ALPHA * dz, dpg, dz * sg, dgain, dbias, _col_sum(dpg)


def _swiglu_fwd_fn(gu):
    hid = gu.shape[-1] // 2
    gate, up = gu[:, :hid], gu[:, hid:]
    return gu, gate * _sigmoid(gate) * up


def _swiglu_bwd_fn(dact, gu):
    gu = gu.astype(F32)
    hid = gu.shape[-1] // 2
    gate, up = gu[:, :hid], gu[:, hid:]
    sg = _sigmoid(gate)
    dgate = dact * up * sg * (1.0 + gate * (1.0 - sg))
    dup = dact * gate * sg
    return (jnp.concatenate([dgate, dup], axis=-1),)


def _loss_fn(y, target):
    err = y - target
    inv = 1.0 / y.shape[-1]
    part = 0.5 * inv * jnp.sum(jnp.sum(err * err, axis=-1, keepdims=True), axis=0, keepdims=True)
    return err * inv, jnp.broadcast_to(part, (1, LANES))


def _adam_fn(w, mom, vel, p_own, p_sib):
    g = p_own.astype(F32) + p_sib.astype(F32)
    m_new = ADAM_B1 * mom + (1.0 - ADAM_B1) * g
    v_new = ADAM_B2 * vel + (1.0 - ADAM_B2) * (g * g)
    m_hat = m_new / (1.0 - ADAM_B1 ** ADAM_STEP)
    v_hat = v_new / (1.0 - ADAM_B2 ** ADAM_STEP)
    delta = -ADAM_LR * (m_hat / (jnp.sqrt(v_hat) + ADAM_EPS) + ADAM_WD * w)
    return g, delta, m_new, v_new


def _split2(x):
    hi = x.astype(BF16)
    return hi, (x - hi.astype(F32)).astype(BF16)


def _dot3(a, b, dims):
    a_hi, a_lo = _split2(a)
    b_hi, b_lo = _split2(b)
    dn = (dims, ((), ()))
    return (lax.dot_general(a_hi, b_hi, dn, preferred_element_type=F32)
            + (lax.dot_general(a_hi, b_lo, dn, preferred_element_type=F32)
               + lax.dot_general(a_lo, b_hi, dn, preferred_element_type=F32)))


def _tdot(mask01, b):
    m = mask01.astype(BF16)
    b_hi = b.astype(BF16)
    rest = b - b_hi.astype(F32)
    b_mid = rest.astype(BF16)
    b_lo = (rest - b_mid.astype(F32)).astype(BF16)
    dn = (((1,), (0,)), ((), ()))
    return (lax.dot_general(m, b_hi, dn, preferred_element_type=F32)
            + (lax.dot_general(m, b_mid, dn, preferred_element_type=F32)
               + lax.dot_general(m, b_lo, dn, preferred_element_type=F32)))


def _hdot(a, b):
    return _dot3(a, b, ((1,), (0,)))


def _hdot_nt(a, b):
    return _dot3(a, b, ((1,), (1,)))


def _hdot_tn(a, b):
    return _dot3(a, b, ((0,), (0,)))


def _dot(a, b):
    return lax.dot_general(a.astype(BF16), b.astype(BF16), (((1,), (0,)), ((), ())), preferred_element_type=F32)


def _dot_nt(a, b):
    return lax.dot_general(a.astype(BF16), b.astype(BF16), (((1,), (1,)), ((), ())), preferred_element_type=F32)


def _dot_tn(a, b):
    return lax.dot_general(a.astype(BF16), b.astype(BF16), (((0,), (0,)), ((), ())), preferred_element_type=F32)


def _hg_masks():
    c = HG_CHUNK
    row = lax.broadcasted_iota(jnp.int32, (c, c), 0)
    col = lax.broadcasted_iota(jnp.int32, (c, c), 1)
    base = row & (-HG_SUB)
    return row, col, base, col <= row, col < base


def _hg_gates(qr, fr, alb):
    lbound = _sigmoid(alb[0:1, :] - alb[1:2, :])
    sig = _sigmoid(fr)
    forget = lbound + (1.0 - lbound) * sig
    kk = (1.0 - lbound) * _sigmoid(-fr)
    qt = qr * _sigmoid(qr) * (HG_DK ** -0.5)
    return qt, kk, jnp.log(forget), lbound, sig, forget


def _hg_scores(qt, kk, g, scores=True):
    c, nsub = HG_CHUNK, HG_CHUNK // HG_SUB
    row, col, base, causal, below = _hg_masks()
    b = _tdot(causal, g)
    rr = _tdot(below, g)
    bq = b - rr
    qh = qt * jnp.exp(bq)
    edecs = [None]
    parts = [jnp.zeros((HG_SUB, c), F32)]
    for i in range(1, nsub):
        edec = jnp.exp(jnp.minimum(rr[i * HG_SUB:i * HG_SUB + 1, :] - b, 0.0))
        edecs.append(edec)
        if scores:
            parts.append(_dot_nt(qh[i * HG_SUB:(i + 1) * HG_SUB, :], kk * edec))
    q3 = qt.reshape(nsub, HG_SUB, HG_DK)
    if not scores:
        return None, b, bq, qh, edecs, (b.reshape(nsub, HG_SUB, HG_DK), q3, kk.reshape(nsub, HG_SUB, HG_DK))
    a = jnp.where(below, jnp.concatenate(parts, axis=0), 0.0)
    b2 = b * LOG2_E
    b3 = b2.reshape(nsub, HG_SUB, HG_DK)
    c3 = (b2 - jnp.log2(kk)).reshape(nsub, HG_SUB, HG_DK)
    for j in range(HG_SUB):
        ek = jnp.exp2(b3 - c3[:, j:j + 1, :])
        colv = jnp.sum(q3 * ek, axis=-1, keepdims=True).reshape(c, 1)
        a = jnp.where(col == base + j, colv, a)
    a = jnp.where(causal, a, 0.0)
    return a, b, bq, qh, edecs, None


def _hg_norm(o, gr, gain):
    r = lax.rsqrt(jnp.mean(o * o, axis=-1, keepdims=True) + RMS_EPS)
    sg = _sigmoid(gr)
    return o * r * gain, r, sg


def _hgrn2_fwd(proj, alb, gain, *, rb):
    m, d4 = proj.shape
    d = d4 // 4
    heads = d // HG_DK
    hp = HG_HEADS_PER_STEP
    rb = min(rb, m)
    cpb = rb // HG_CHUNK
    nrb = m // rb

    def body(q_ref, f_ref, v_ref, g_ref, alb_ref, gain_ref, o_ref, og_ref, st_ref, a_ref, state):
        @pl.when(pl.program_id(1) == 0)
        def _():
            state[...] = jnp.zeros(state.shape, F32)

        def chunk(ci, carry):
            sl = pl.ds(pl.multiple_of(ci * HG_CHUNK, HG_CHUNK), HG_CHUNK)
            for u in range(hp):
                ln = slice(u * HG_DK, (u + 1) * HG_DK)
                qt, kk, g, _, _, _ = _hg_gates(q_ref[sl, ln], f_ref[sl, ln], alb_ref[:, ln])
                v = v_ref[sl, ln]
                st = state[u]
                st_ref[u, ci] = st
                a, b, _, _, _, _ = _hg_scores(qt, kk, g)
                a_ref[u, ci] = a.astype(a_ref.dtype)
                o = _dot(a, v) + _dot_nt(qt * jnp.exp(b), st)
                b_last = b[HG_CHUNK - 1:HG_CHUNK, :]
                state[u] = st * jnp.exp(b_last) + _hdot_tn(v, kk * jnp.exp(b_last - b))
                o_ref[sl, ln] = o
                n, _, sg = _hg_norm(o, g_ref[sl, ln], gain_ref[...])
                og_ref[sl, ln] = (n * g_ref[sl, ln] * sg).astype(og_ref.dtype)
            return carry

        lax.fori_loop(0, cpb, chunk, 0)

    def col(cidx):
        return pl.BlockSpec((rb, hp * HG_DK), lambda h, r: (r, cidx * (heads // hp) + h))

    return pl.pallas_call(
        body, name="hgrn2_fwd", grid=(heads // hp, nrb),
        in_specs=[col(0), col(1), col(2), col(3),
                  pl.BlockSpec((2, hp * HG_DK), lambda h, r: (0, h)),
                  pl.BlockSpec((1, HG_DK), lambda h, r: (0, 0))],
        out_specs=[pl.BlockSpec((rb, hp * HG_DK), lambda h, r: (r, h)),
                   pl.BlockSpec((rb, hp * HG_DK), lambda h, r: (r, h)),
                   pl.BlockSpec((hp, cpb, HG_DK, HG_DK), lambda h, r: (h, r, 0, 0)),
                   pl.BlockSpec((hp, cpb, HG_CHUNK, HG_CHUNK), lambda h, r: (h, r, 0, 0))],
        out_shape=[jax.ShapeDtypeStruct((m, d), F32), jax.ShapeDtypeStruct((m, d), BF16),
                   jax.ShapeDtypeStruct((heads, m // HG_CHUNK, HG_DK, HG_DK), F32),
                   jax.ShapeDtypeStruct((heads, m // HG_CHUNK, HG_CHUNK, HG_CHUNK), BF16)],
        scratch_shapes=[pltpu.VMEM((hp, HG_DK, HG_DK), F32)],
        compiler_params=_params(("parallel", "arbitrary")),
    )(proj, proj, proj, proj, alb, gain)


def _hgrn2_bwd(proj, o_pre, states, scores, dog, alb, gain, *, rb):
    m, d4 = proj.shape
    d = d4 // 4
    heads = d // HG_DK
    rb = min(rb, m)
    cpb = rb // HG_CHUNK
    nrb = m // rb
    c, nsub = HG_CHUNK, HG_CHUNK // HG_SUB

    def body(q_ref, f_ref, v_ref, g_ref, o_ref, st_ref, a_ref, dog_ref, alb_ref, gain_ref,
             dp_ref, dalb_ref, dgain_ref, dstate, carry_ref):
        first = (pl.program_id(0) == 0) & (pl.program_id(1) == 0)

        @pl.when(first)
        def _():
            dgain_ref[...] = jnp.zeros(dgain_ref.shape, F32)

        @pl.when(pl.program_id(1) == 0)
        def _():
            dstate[...] = jnp.zeros(dstate.shape, F32)
            carry_ref[...] = jnp.zeros(carry_ref.shape, F32)
            dalb_ref[...] = jnp.zeros(dalb_ref.shape, F32)

        row, col, base, causal, below = _hg_masks()
        sub_iota = lax.broadcasted_iota(jnp.int32, (nsub, HG_SUB, HG_DK), 1)
        row_k = lax.broadcasted_iota(jnp.int32, (c, HG_DK), 0)
        upper = col >= row

        def chunk(step, carry):
            ci = cpb - 1 - step
            sl = pl.ds(pl.multiple_of(ci * HG_CHUNK, HG_CHUNK), HG_CHUNK)
            qr, fr, v, gr = q_ref[sl, :], f_ref[sl, :], v_ref[sl, :], g_ref[sl, :]
            qt, kk, g, lbound, sig, forget = _hg_gates(qr, fr, alb_ref[...])
            o = o_ref[sl, :]
            dogv = dog_ref[sl, :]
            gain_v = gain_ref[...]
            n, r, sg = _hg_norm(o, gr, gain_v)
            dgr = dogv * n * sg * (1.0 + gr * (1.0 - sg))
            dn = dogv * gr * sg
            dgain_ref[...] += _col_sum(dn * o * r)
            u = dn * gain_v
            d_o = r * u - o * (r * r * r) * jnp.mean(u * o, axis=-1, keepdims=True)
            st0 = st_ref[ci]
            dst = dstate[...]
            _, b, bq, qh, edecs, (b3, q3, k3) = _hg_scores(qt, kk, g, scores=False)
            a = a_ref[ci]
            eb = jnp.exp(b)
            b_last = b[c - 1:c, :]
            kdl_dec = jnp.exp(b_last - b)
            kdl = kk * kdl_dec
            d_a = jnp.where(causal, _dot_nt(d_o, v), 0.0)
            d_at = _dot_nt(v, d_o)
            dv = _dot_tn(a, d_o) + _dot_nt(kdl, dst)
            dq = eb * _hdot(d_o, st0)
            dk = _hdot(v, dst) * kdl_dec
            d_a_below = jnp.where(below, d_a, 0.0)
            dq_parts = [jnp.zeros((HG_SUB, HG_DK), F32)]
            for i in range(1, nsub):
                lo, hi = i * HG_SUB, (i + 1) * HG_SUB
                dq_parts.append(_hdot(d_a_below[lo:hi, :], kk * edecs[i]))
                gi = _hdot(d_at[:, lo:hi], qh[lo:hi, :])
                dk = dk + jnp.where(row_k < lo, edecs[i] * gi, 0.0)
            dq = dq + jnp.concatenate(dq_parts, axis=0) * jnp.exp(bq)
            dq3 = jnp.zeros((nsub, HG_SUB, HG_DK), F32)
            dk3 = jnp.zeros((nsub, HG_SUB, HG_DK), F32)
            d_diag = jnp.concatenate([d_a[i * HG_SUB:(i + 1) * HG_SUB, i * HG_SUB:(i + 1) * HG_SUB]
                                      for i in range(nsub)], axis=0).reshape(nsub, HG_SUB, HG_SUB)
            for j in range(HG_SUB):
                e = jnp.exp(jnp.minimum(b3 - b3[:, j:j + 1, :], 0.0))
                t1 = d_diag[:, :, j:j + 1] * e
                dq3 = dq3 + t1 * k3[:, j:j + 1, :]
                dk3 = jnp.where(sub_iota == j, jnp.sum(t1 * q3, axis=1, keepdims=True), dk3)
            dq = dq + dq3.reshape(c, HG_DK)
            dk = dk + dk3.reshape(c, HG_DK)
            dstate[...] = dst * jnp.exp(b_last) + _hdot_tn(d_o, qt * eb)
            dglog = _tdot(upper, qt * dq - kk * dk) + carry_ref[...]
            carry_ref[...] = dglog[0:1, :]
            dforget = dglog / forget
            one_m_lb = 1.0 - lbound
            dsig = (dforget - dk) * one_m_lb
            sneg = _sigmoid(-fr)
            dlb = _col_sum(dforget * (1.0 - sig) - dk * sneg)
            dalb0 = dlb * lbound * one_m_lb
            dalb_ref[...] += jnp.concatenate([dalb0, -dalb0], axis=0)
            sq = _sigmoid(qr)
            dp_ref[0, sl, :] = (dq * (HG_DK ** -0.5) * sq * (1.0 + qr * (1.0 - sq))).astype(dp_ref.dtype)
            dp_ref[1, sl, :] = (dsig * sig * (1.0 - sig)).astype(dp_ref.dtype)
            dp_ref[2, sl, :] = dv.astype(dp_ref.dtype)
            dp_ref[3, sl, :] = dgr.astype(dp_ref.dtype)
            return carry

        lax.fori_loop(0, cpb, chunk, 0, unroll=2)

    def rev(r):
        return nrb - 1 - r

    def col(cidx):
        return pl.BlockSpec((rb, HG_DK), lambda h, r: (rev(r), cidx * heads + h))

    def head_rows():
        return pl.BlockSpec((rb, HG_DK), lambda h, r: (rev(r), h))

    return pl.pallas_call(
        body, name="hgrn2_bwd", grid=(heads, nrb),
        in_specs=[col(0), col(1), col(2), col(3), head_rows(),
                  pl.BlockSpec((None, cpb, HG_DK, HG_DK), lambda h, r: (h, rev(r), 0, 0)),
                  pl.BlockSpec((None, cpb, HG_CHUNK, HG_CHUNK), lambda h, r: (h, rev(r), 0, 0)),
                  head_rows(),
                  pl.BlockSpec((2, HG_DK), lambda h, r: (0, h)),
                  pl.BlockSpec((1, HG_DK), lambda h, r: (0, 0))],
        out_specs=[pl.BlockSpec((4, rb, HG_DK), lambda h, r: (0, rev(r), h)),
                   pl.BlockSpec((2, HG_DK), lambda h, r: (0, h)),
                   pl.BlockSpec((1, HG_DK), lambda h, r: (0, 0))],
        out_shape=[jax.ShapeDtypeStruct((4, m, d), BF16), jax.ShapeDtypeStruct((2, d), F32),
                   jax.ShapeDtypeStruct((1, HG_DK), F32)],
        scratch_shapes=[pltpu.VMEM((HG_DK, HG_DK), F32), pltpu.VMEM((1, HG_DK), F32)],
        compiler_params=_params(("arbitrary", "arbitrary")),
    )(proj, proj, proj, proj, o_pre, states, scores, dog, alb, gain)


def _swa_probs(qh, kp, kc, sink, slope, has_prev, lse=None):
    rows = qh.shape[0]
    qi = lax.broadcasted_iota(jnp.int32, (rows, WINDOW), 0) & (WINDOW - 1)
    si = lax.broadcasted_iota(jnp.int32, (rows, WINDOW), 1)
    scale = ATT_HD ** -0.5
    dist_c = (qi - si).astype(F32)
    s_p = _dot_nt(qh, kp) * scale - slope * (dist_c + float(WINDOW))
    s_c = _dot_nt(qh, kc) * scale - slope * dist_c
    s_p = jnp.where((si > qi) & has_prev, s_p, NEG)
    s_c = jnp.where(si <= qi, s_c, NEG)
    if lse is not None:
        return jnp.exp(s_p - lse), jnp.exp(s_c - lse), jnp.exp(sink - lse), lse
    mx = jnp.maximum(jnp.maximum(jnp.max(s_p, axis=-1, keepdims=True), jnp.max(s_c, axis=-1, keepdims=True)), sink)
    e_p, e_c, e_s = jnp.exp(s_p - mx), jnp.exp(s_c - mx), jnp.exp(sink - mx)
    total = jnp.sum(e_p, axis=-1, keepdims=True) + jnp.sum(e_c, axis=-1, keepdims=True) + e_s
    inv = 1.0 / total
    return e_p * inv, e_c * inv, e_s * inv, mx + jnp.log(total)


def _slope(h, n_heads):
    return float(2.0 ** (-8.0 * (h + 1) / n_heads))


def _swa_group(ref_vals, sink_ref, kh, n_heads):
    heads = [kh * ATT_G + g for g in range(ATT_G)]
    stacked = [jnp.concatenate([v[:, h * ATT_HD:(h + 1) * ATT_HD] for h in heads], axis=0) for v in ref_vals]
    grp = lax.shift_right_logical(lax.broadcasted_iota(jnp.int32, (ATT_G * WINDOW, 1), 0), WINDOW.bit_length() - 1)
    slope = jnp.zeros((ATT_G * WINDOW, 1), F32)
    sink = jnp.zeros((ATT_G * WINDOW, 1), F32)
    for g, h in enumerate(heads):
        slope = jnp.where(grp == g, _slope(h, n_heads), slope)
        sink = jnp.where(grp == g, sink_ref[:, h:h + 1], sink)
    return stacked, slope, sink


def _swa_fwd(q, kv, sinks):
    m, d = q.shape
    n_heads = d // ATT_HD
    kvh = n_heads // ATT_G
    kd = kvh * ATT_HD
    nb = m // WINDOW

    def body(q_ref, kvp_ref, kvc_ref, sink_ref, o_ref, lse_ref):
        has_prev = pl.program_id(0) > 0
        qv, kvp, kvc = q_ref[...], kvp_ref[...], kvc_ref[...]
        lane_h = lax.broadcasted_iota(jnp.int32, (WINDOW, n_heads), 1)
        outs, lse_all = [], jnp.zeros((WINDOW, n_heads), F32)
        for kh in range(kvh):
            ks = slice(kh * ATT_HD, (kh + 1) * ATT_HD)
            vs = slice(kd + kh * ATT_HD, kd + (kh + 1) * ATT_HD)
            (q4,), slope, sink = _swa_group([qv], sink_ref, kh, n_heads)
            p_p, p_c, _, lse = _swa_probs(q4, kvp[:, ks], kvc[:, ks], sink, slope, has_prev)
            o4 = _dot(p_p, kvp[:, vs]) + _dot(p_c, kvc[:, vs])
            for g in range(ATT_G):
                rows = slice(g * WINDOW, (g + 1) * WINDOW)
                outs.append(o4[rows, :])
                lse_all = jnp.where(lane_h == kh * ATT_G + g, lse[rows, :], lse_all)
        o_ref[...] = jnp.concatenate(outs, axis=-1).astype(o_ref.dtype)
        lse_ref[...] = lse_all

    return pl.pallas_call(
        body, name="swa_fwd", grid=(nb,),
        in_specs=[pl.BlockSpec((WINDOW, d), lambda n: (n, 0)),
                  pl.BlockSpec((WINDOW, 2 * kd), lambda n: (jnp.maximum(n - 1, 0), 0)),
                  pl.BlockSpec((WINDOW, 2 * kd), lambda n: (n, 0)),
                  pl.BlockSpec((1, n_heads), lambda n: (0, 0))],
        out_specs=[pl.BlockSpec((WINDOW, d), lambda n: (n, 0)), pl.BlockSpec((WINDOW, n_heads), lambda n: (n, 0))],
        out_shape=[jax.ShapeDtypeStruct((m, d), BF16), jax.ShapeDtypeStruct((m, n_heads), F32)],
        compiler_params=_params(("arbitrary",)),
    )(q, kv, kv, sinks)


def _swa_bwd(q, kv, sinks, lse, dao):
    m, d = q.shape
    n_heads = d // ATT_HD
    kvh = n_heads // ATT_G
    kd = kvh * ATT_HD
    nb = m // WINDOW
    scale = ATT_HD ** -0.5

    def body(q_ref, kvp_ref, kvc_ref, sink_ref, lse_ref, do_ref, dq_ref, dkvc_ref, dkvp_ref, dqsum_ref, dsink_ref):
        @pl.when(pl.program_id(0) == 0)
        def _():
            dqsum_ref[...] = jnp.zeros(dqsum_ref.shape, F32)
            dsink_ref[...] = jnp.zeros(dsink_ref.shape, F32)

        has_prev = pl.program_id(0) > 0
        qv, kvp, kvc, dov = q_ref[...], kvp_ref[...], kvc_ref[...], do_ref[...]
        lane_h = lax.broadcasted_iota(jnp.int32, (1, n_heads), 1)
        dsink = jnp.zeros((1, n_heads), F32)
        dq_parts, dk_p, dk_c, dv_p, dv_c = [], [], [], [], []
        for kh in range(kvh):
            ks = slice(kh * ATT_HD, (kh + 1) * ATT_HD)
            vs = slice(kd + kh * ATT_HD, kd + (kh + 1) * ATT_HD)
            kp, kc, vp, vc = kvp[:, ks], kvc[:, ks], kvp[:, vs], kvc[:, vs]
            (q4, do4), slope, sink = _swa_group([qv, dov], sink_ref, kh, n_heads)
            lse4 = jnp.concatenate([lse_ref[:, kh * ATT_G + g:kh * ATT_G + g + 1] for g in range(ATT_G)], axis=0)
            p_p, p_c, p_s, _ = _swa_probs(q4, kp, kc, sink, slope, has_prev, lse=lse4)
            dp_p, dp_c = _dot_nt(do4, vp), _dot_nt(do4, vc)
            delta = jnp.sum(p_p * dp_p, axis=-1, keepdims=True) + jnp.sum(p_c * dp_c, axis=-1, keepdims=True)
            ds_p, ds_c = p_p * (dp_p - delta), p_c * (dp_c - delta)
            sink_term = p_s * delta
            dq4 = (_dot(ds_p, kp) + _dot(ds_c, kc)) * scale
            for g in range(ATT_G):
                rows = slice(g * WINDOW, (g + 1) * WINDOW)
                dsink = dsink + jnp.where(lane_h == kh * ATT_G + g, -_col_sum(sink_term[rows, :]), 0.0)
                dq_parts.append(dq4[rows, :])
            dk_p.append(_dot_tn(ds_p, q4) * scale)
            dk_c.append(_dot_tn(ds_c, q4) * scale)
            dv_p.append(_dot_tn(p_p, do4))
            dv_c.append(_dot_tn(p_c, do4))
        dq = jnp.concatenate(dq_parts, axis=-1)
        dq_ref[...] = dq.astype(dq_ref.dtype)
        dqsum_ref[...] += _col_sum(dq)
        dsink_ref[...] += dsink
        dkvc_ref[...] = jnp.concatenate(dk_c + dv_c, axis=-1)
        dkvp_ref[...] = jnp.concatenate(dk_p + dv_p, axis=-1)

    return pl.pallas_call(
        body, name="swa_bwd", grid=(nb,),
        in_specs=[pl.BlockSpec((WINDOW, d), lambda n: (n, 0)),
                  pl.BlockSpec((WINDOW, 2 * kd), lambda n: (jnp.maximum(n - 1, 0), 0)),
                  pl.BlockSpec((WINDOW, 2 * kd), lambda n: (n, 0)),
                  pl.BlockSpec((1, n_heads), lambda n: (0, 0)),
                  pl.BlockSpec((WINDOW, n_heads), lambda n: (n, 0)),
                  pl.BlockSpec((WINDOW, d), lambda n: (n, 0))],
        out_specs=[pl.BlockSpec((WINDOW, d), lambda n: (n, 0)),
                   pl.BlockSpec((WINDOW, 2 * kd), lambda n: (n, 0)),
                   pl.BlockSpec((WINDOW, 2 * kd), lambda n: (n, 0)),
                   pl.BlockSpec((1, d), lambda n: (0, 0)),
                   pl.BlockSpec((1, n_heads), lambda n: (0, 0))],
        out_shape=[jax.ShapeDtypeStruct((m, d), BF16), jax.ShapeDtypeStruct((m, 2 * kd), F32),
                   jax.ShapeDtypeStruct((m, 2 * kd), F32), jax.ShapeDtypeStruct((1, d), F32),
                   jax.ShapeDtypeStruct((1, n_heads), F32)],
        compiler_params=_params(("arbitrary",)),
    )(q, kv, kv, sinks, lse, dao)


def _kv_grad_combine(dkv_cur, dkv_prev):
    m, w = dkv_cur.shape
    nb = m // WINDOW

    def body(cur_ref, nxt_ref, o_ref, sum_ref):
        @pl.when(pl.program_id(0) == 0)
        def _():
            sum_ref[...] = jnp.zeros(sum_ref.shape, F32)

        total = cur_ref[...] + jnp.where(pl.program_id(0) < nb - 1, nxt_ref[...], 0.0)
        o_ref[...] = total.astype(o_ref.dtype)
        sum_ref[...] += _col_sum(total)

    return pl.pallas_call(
        body, name="kv_grad_combine", grid=(nb,),
        in_specs=[pl.BlockSpec((WINDOW, w), lambda n: (n, 0)),
                  pl.BlockSpec((WINDOW, w), lambda n: (jnp.minimum(n + 1, nb - 1), 0))],
        out_specs=[pl.BlockSpec((WINDOW, w), lambda n: (n, 0)), pl.BlockSpec((1, w), lambda n: (0, 0))],
        out_shape=[jax.ShapeDtypeStruct((m, w), BF16), jax.ShapeDtypeStruct((1, w), F32)],
        compiler_params=_params(("arbitrary",)),
    )(dkv_cur, dkv_prev)


def _row(v):
    return v.reshape(1, -1)


def _local_step(x, p, target, wget, grad_sink, ln_gain, ln_bias, alb, norm_gain, kv_b, b_q, sinks, b_out, ple_b,
                small_sink=None):
    gs = {}
    gains = ln_gain.reshape(DEPTH * 3, -1)
    biases = ln_bias.reshape(DEPTH * 3, -1)
    sd = x.shape
    pending = [None]

    def mm(a, b, lb=0, **kw):
        after, pending[0] = pending[0], None
        return _mm(a, b, lb=lb, after=after, **kw)

    def mm_ln(a, wt, xin, i, j, nm, bias=None, pu=None):
        r = 3 * i + j
        if pu is None:
            fn, rows = (lambda h, xv, g, bv: (h,) + _ln_fwd_fn(xv, h, g[r:r + 1], bv[r:r + 1])), [xin]
        else:
            fn = lambda h, xv, puv, g, bv: (h,) + _ple_ln_fwd_fn(xv, h, puv, g[r:r + 1], bv[r:r + 1])
            rows = [xin, pu]
        h, y, yb = _mm(a, wt, lb=0, bias=bias, name=nm,
                       post=(fn, rows, [gains, biases], [(sd, F32), (sd, F32), (sd, BF16)], []))
        return h, (y, yb)

    def mm_ln_bwd(a, wt, add, xin, h, i, j, nm):
        r = 3 * i + j
        dx_part, dh, dg, db, dhsum = mm(a, wt, tb=True, add=add, name=nm,
                                        post=(lambda dy, xv, hv, g: _ln_bwd_fn(dy, xv, hv, g[r:r + 1]), [xin, h],
                                              [gains], [(sd, F32), (sd, BF16)], [((1, sd[1]), F32)] * 3))
        gs[f"ln_gain_{i}_{j}"], gs[f"ln_bias_{i}_{j}"] = dg, db
        return dx_part, dh, dhsum

    def tail_fwd(xa, i):
        wgu = wget("ffn_w_gate_up", i, xa[1])
        hid2 = wgu.shape[-1]
        gu, act = _mm(xa[1], wgu, lb=0, name=f"ffn_up_swiglu{i}", tile_cols=hid2 // 2,
                      post=(_swiglu_fwd_fn, [], [], [((sd[0], hid2), BF16), ((sd[0], hid2 // 2), BF16)], []))
        f, xb = mm_ln(act, wget("ffn_w_down", i, act), xa[0], i, 1, f"ffn_down_ln{i}")
        pu = _mm(p, wget("ple_w_up", i, act), la=i, lb=0, name=f"ple_up{i}")
        pg, xc = mm_ln(xb[1], wget("ple_w_gate", i, act), xb[0], i, 2, f"ple_gate_ln{i}", bias=_row(ple_b[i]), pu=pu)
        return dict(xa=xa, gu=gu, act=act, f=f, xb=xb, pg=pg, pu=pu), xc

    def tail_bwd(head, sv, i, mix_in, mix_h):
        xa, xb = sv["xa"], sv["xb"]
        r = 3 * i + 2
        dxb_part, dpg, dpu, dg2, db2, dbg = head(
            lambda dy, xv, pgv, puv, g: _ple_ln_bwd_fn(dy, xv, pgv, puv, g[r:r + 1]), [xb[0], sv["pg"], sv["pu"]],
            [gains], [(sd, F32), (sd, BF16), (sd, BF16)], [((1, sd[1]), F32)] * 3)[:6]
        gs[f"ple_b_{i}"] = dbg
        gs[f"ln_gain_{i}_2"], gs[f"ln_bias_{i}_2"] = dg2, db2
        grad_of("ple_w_gate", i, xb[1], dpg)
        grad_of("ple_w_up", i, p, dpu, la=i)
        dxa_part, df, _ = mm_ln_bwd(dpg, wget("ple_w_gate", i, None), dxb_part, xa[0], sv["f"], i, 1,
                                    f"ple_gate_dx_ln{i}")
        grad_of("ffn_w_down", i, sv["act"], df)
        gu = sv["gu"]
        dgu, = mm(df, wget("ffn_w_down", i, None), tb=True, name=f"ffn_down_dx_swiglu{i}", tile_cols=gu.shape[1] // 4,
                  post=(_swiglu_bwd_fn, [gu], [], [(gu.shape, BF16)], []))
        grad_of("ffn_w_gate_up", i, xa[1], dgu)
        return mm_ln_bwd(dgu, wget("ffn_w_gate_up", i, None), dxa_part, mix_in, mix_h, i, 0, f"ffn_up_dx_ln{i}")

    def grad_of(nm, i, act, dout, la=None, b_parts=None):
        grad = mm(act, dout, la=la, lb=None, ta=True, out_dtype=BF16, out_layers=1, out_layer=0,
                  name=f"grad_{nm}{i}", b_parts=b_parts)
        token = grad_sink(nm, i, grad)
        if token is not None:
            pending[0] = token

    proj = _mm(x, wget("a_w_in", 0, None), lb=0, name="hg_proj")
    o_pre, og, states, scores = _hgrn2_fwd(proj, alb, norm_gain, rb=HG_ROWS)
    h0, x1 = mm_ln(og, wget("a_w_out", 0, og), x, 0, 0, "hg_out_ln")
    sv0, x3 = tail_fwd(x1, 0)
    kv = _mm(x3[1], wget("kv_w", 0, x3[1]), lb=0, bias=_row(kv_b), out_dtype=BF16, name="kv_proj")
    q = _mm(x3[1], wget("b_w_q", 0, x3[1]), lb=0, bias=b_q, out_dtype=BF16, name="q_proj")
    ao, lse = _swa_fwd(q, kv, sinks)
    h1, x4 = mm_ln(ao, wget("b_w_out", 0, x3[1]), x3[0], 1, 0, "att_out_ln", bias=b_out)
    sv1, y = tail_fwd(x4, 1)

    loss_box = []

    def loss_head(fn, rows, whole, outs, sums):
        def with_loss(yv, tv, *rest):
            dy, part = _loss_fn(yv, tv)
            return fn(dy, *rest) + (part,)

        res = _rowwise(with_loss, [y[0], target] + rows, whole, outs, list(sums) + [((1, LANES), F32)],
                       name="loss_ln_ple_bwd1")
        loss_box.append(res[-1])
        return res

    dx3_part, dh1, dh1sum = tail_bwd(loss_head, sv1, 1, x3[0], h1)
    loss = loss_box[0]
    gs["b_out"] = dh1sum
    grad_of("b_w_out", 0, ao, dh1)
    dao = mm(dh1, wget("b_w_out", 0, None), tb=True, out_dtype=BF16, name="att_out_dx")
    dq, dkv_cur, dkv_prev, dqsum, dsinks = _swa_bwd(q, kv, sinks, lse, dao)
    gs["b_q"], gs["sinks"] = dqsum, dsinks
    dkv, dkvsum = _kv_grad_combine(dkv_cur, dkv_prev)
    gs["kv_b"] = dkvsum
    grad_of("b_w_q", 0, x3[1], dq)
    grad_of("kv_w", 0, x3[1], dkv)
    dx3 = mm(dq, wget("b_w_q", 0, None), tb=True, add=dx3_part, name="q_proj_dx")

    def kv_head(*post):
        return mm(dkv, wget("kv_w", 0, None), tb=True, add=dx3, name="kv_proj_dx_ln_ple_bwd0", post=post)

    dx_part, dh0, _ = tail_bwd(kv_head, sv0, 0, x, h0)
    grad_of("a_w_out", 0, og, dh0)
    dog = mm(dh0, wget("a_w_out", 0, None), tb=True, name="hg_out_dx")
    dproj, dalb, dgain = _hgrn2_bwd(proj, o_pre, states, scores, dog, alb, norm_gain, rb=HG_ROWS)
    gs["alb"], gs["norm_gain"] = dalb, dgain
    if small_sink is not None:
        pending[0] = small_sink(loss, gs)
    grad_of("a_w_in", 0, x, dproj, b_parts=4)
    grad_x = mm(dproj, wget("a_w_in", 0, None), tb=True, add=dx_part, name="hg_proj_dx", a_parts=4)
    return loss, grad_x, gs


HBM_SPEC = pl.BlockSpec(memory_space=pl.ANY)
HBM_ONLY = pl.BlockSpec(memory_space=pltpu.HBM)
SEM_SPEC = pl.BlockSpec(memory_space=pltpu.SEMAPHORE)
SIDE_EFFECT = pltpu.SideEffectType.DATAFLOW_SIDE_EFFECTING


def _slot(kind, j):
    return (j % 2) * 2 + j // 2 if kind == "colp" else j


def _piece(ref, kind, j):
    _, r, c = ref.shape
    if kind == "row":
        return ref.at[:, pl.ds(j * (r // N_CHIPS), r // N_CHIPS), :]
    return ref.at[:, :, pl.ds(_slot(kind, j) * (c // N_CHIPS), c // N_CHIPS)]


def _piece_dyn(ref, kind, j):
    _, r, c = ref.shape
    if kind == "row":
        return ref.at[:, pl.ds(pl.multiple_of(j * (r // N_CHIPS), 16), r // N_CHIPS), :]
    return ref.at[:, :, pl.ds(pl.multiple_of(_slot(kind, j) * (c // N_CHIPS), LANES), c // N_CHIPS)]


def _chip_of(j, c):
    return (j // 2, j % 2, c)


def _in_hbm(a):
    return pltpu.with_memory_space_constraint(a, pltpu.HBM)


PLACE_STEPS = 4


def _place(items, chip, *, name, after=None):
    n = len(items)
    in_specs, out_specs, out_shapes, blocks = [], [], [], []
    for src, layer, kind, out_dtype in items:
        _, r, c = src.shape
        nb = max(k for k in (1, 2, PLACE_STEPS) if r % (16 * k) == 0 or k == 1)
        blocks.append(nb)

        def src_idx(i, chip_ref, layer=layer, nb=nb):
            return (layer, jnp.minimum(i, nb - 1), 0)

        def full_idx(i, chip_ref, kind=kind, nb=nb):
            ib = jnp.minimum(i, nb - 1)
            return (0, chip_ref[0] * nb + ib, 0) if kind == "row" else (0, ib, _slot(kind, chip_ref[0]))

        in_specs.append(pl.BlockSpec((None, r // nb, c), src_idx))
        out_specs.append(pl.BlockSpec((None, r // nb, c), full_idx))
        out_shapes.append(jax.ShapeDtypeStruct((1, r * N_CHIPS, c) if kind == "row" else (1, r, c * N_CHIPS),
                                               out_dtype))
    operands = [it[0] for it in items]
    if after is not None:
        in_specs.append(HBM_SPEC)
        operands.append(after)

    def body(chip_ref, *refs):
        for a in range(n):
            refs[len(refs) - n + a][...] = refs[a][...].astype(refs[len(refs) - n + a].dtype)

    return pl.pallas_call(
        body, name=name,
        grid_spec=pltpu.PrefetchScalarGridSpec(num_scalar_prefetch=1, grid=(PLACE_STEPS,), in_specs=in_specs,
                                               out_specs=out_specs),
        out_shape=out_shapes,
        compiler_params=_params(("arbitrary",)),
    )(chip, *operands)


def _half(ref, c):
    h = ref.shape[1] // 2
    start = c * h if isinstance(c, int) else pl.multiple_of(c * h, 16)
    return ref.at[:, pl.ds(start, h), :]


def _sibling_handshake():
    barrier = pltpu.get_barrier_semaphore()
    sibling = (lax.axis_index("x"), lax.axis_index("y"), 1 - lax.axis_index("c"))
    pl.semaphore_signal(barrier, inc=1, device_id=sibling, device_id_type=MESH)
    pl.semaphore_wait(barrier, 1)


class _SiblingFill:
    def __init__(self, lands, kinds, name, collective_id):
        self.kinds, self.name, self.n = kinds, name, len(lands)
        n = self.n
        sem_shape = pltpu.SemaphoreType.DMA((n * N_CHIPS,))

        def body(*refs):
            land_refs, send_sems, recv_sems, token = refs[:n], refs[n], refs[n + 1], refs[-1]
            _sibling_handshake()
            for cp in self._copies(land_refs, send_sems, recv_sems):
                cp.start()
            token[...] = jnp.zeros(token.shape, token.dtype)

        outs = pl.pallas_call(
            body, name=name + "_start",
            in_specs=[HBM_ONLY] * n,
            out_specs=[SEM_SPEC, SEM_SPEC] + [HBM_ONLY] * n + [pl.BlockSpec(memory_space=pltpu.VMEM)],
            out_shape=[sem_shape, sem_shape] + [pltpu.HBM(a.shape, a.dtype) for a in lands]
                      + [jax.ShapeDtypeStruct((8, LANES), F32)],
            input_output_aliases={i: i + 2 for i in range(n)},
            compiler_params=pltpu.CompilerParams(has_side_effects=SIDE_EFFECT, collective_id=collective_id),
        )(*[_in_hbm(a) for a in lands])
        self.send_sems, self.recv_sems, self.lands, self.token = outs[0], outs[1], list(outs[2:2 + n]), outs[-1]

    def _copies(self, land_refs, send_sems, recv_sems):
        x, y, c = lax.axis_index("x"), lax.axis_index("y"), lax.axis_index("c")
        me = 2 * x + y
        copies = []
        for a in range(self.n):
            for k in range(1, N_CHIPS):
                t = (me + k) % N_CHIPS
                slice_t = _piece_dyn(land_refs[a], self.kinds[a], t)
                got = _half(slice_t, c)
                copies.append(pltpu.make_async_remote_copy(
                    src_ref=got, dst_ref=got, send_sem=send_sems.at[a * N_CHIPS + k],
                    recv_sem=recv_sems.at[a * N_CHIPS + k], device_id=(x, y, 1 - c), device_id_type=MESH))
        return copies

    def wait(self, after):
        n = self.n

        def body(*refs):
            land_refs, send_sems, recv_sems = refs[:n], refs[n], refs[n + 1]
            for cp in self._copies(land_refs, send_sems, recv_sems):
                cp.wait_send()
                cp.wait_recv()

        operands = [_in_hbm(a) for a in self.lands] + [self.send_sems, self.recv_sems]
        in_specs = [HBM_ONLY] * n + [SEM_SPEC, SEM_SPEC]
        if after is not None:
            operands.append(after)
            in_specs.append(HBM_SPEC)
        outs = pl.pallas_call(
            body, name=self.name + "_wait",
            in_specs=in_specs, out_specs=[HBM_ONLY] * n,
            out_shape=[pltpu.HBM(a.shape, a.dtype) for a in self.lands],
            input_output_aliases={i: i for i in range(n)},
            compiler_params=pltpu.CompilerParams(has_side_effects=SIDE_EFFECT),
        )(*operands)
        return list(outs)


class _Exchange:
    def __init__(self, mode, srcs, lands, kinds, layers, name, collective_id, after=None, halves=None):
        self.mode, self.kinds, self.layers, self.name, self.n = mode, kinds, layers, name, len(lands)
        self.halves = halves if halves is not None else [False] * len(lands)
        n, ns = self.n, len(srcs)
        n_in = ns + n + (after is not None)
        sem_shape = pltpu.SemaphoreType.DMA((n * N_CHIPS,))

        def body(*refs):
            src_refs, land_refs = refs[:ns], refs[ns:ns + n]
            send_sems, recv_sems = refs[n_in], refs[n_in + 1]
            token = refs[-1]
            c = lax.axis_index("c")
            me = 2 * lax.axis_index("x") + lax.axis_index("y")
            barrier = pltpu.get_barrier_semaphore()
            for k in range(1, N_CHIPS):
                t = (me + k) % N_CHIPS
                pl.semaphore_signal(barrier, inc=1, device_id=(t // 2, t % 2, c), device_id_type=MESH)
            pl.semaphore_wait(barrier, N_CHIPS - 1)
            for j in range(N_CHIPS):
                @pl.when(me == j)
                def _():
                    for a in range(n):
                        for t in range(N_CHIPS):
                            if t != j:
                                src, dst = self._ends(src_refs, land_refs, a, j, t, c)
                                pltpu.make_async_remote_copy(
                                    src_ref=src, dst_ref=dst, send_sem=send_sems.at[a * N_CHIPS + t],
                                    recv_sem=recv_sems.at[a * N_CHIPS + j],
                                    device_id=_chip_of(t, c), device_id_type=MESH).start()
            token[...] = jnp.zeros(token.shape, token.dtype)

        arrays = list(srcs) + list(lands)
        operands = [_in_hbm(a) for a in arrays]
        in_specs = [HBM_ONLY] * (ns + n)
        if after is not None:
            operands.append(after)
            in_specs.append(HBM_SPEC)
        outs = pl.pallas_call(
            body, name=name + "_start",
            in_specs=in_specs,
            out_specs=[SEM_SPEC, SEM_SPEC] + [HBM_ONLY] * (ns + n) + [pl.BlockSpec(memory_space=pltpu.VMEM)],
            out_shape=[sem_shape, sem_shape] + [pltpu.HBM(a.shape, a.dtype) for a in arrays]
                      + [jax.ShapeDtypeStruct((8, LANES), F32)],
            input_output_aliases={i: i + 2 for i in range(ns + n)},
            compiler_params=pltpu.CompilerParams(has_side_effects=SIDE_EFFECT, collective_id=collective_id),
        )(*operands)
        self.send_sems, self.recv_sems = outs[0], outs[1]
        self.srcs, self.lands = list(outs[2:2 + ns]), list(outs[2 + ns:2 + ns + n])
        self.token = outs[-1]

    def _ends(self, src_refs, land_refs, a, me_j, peer, c):
        if self.mode == "gather":
            mine = _piece(land_refs[a], self.kinds[a], me_j)
            if self.halves[a]:
                mine = _half(mine, c)
            return mine, mine
        return _piece(src_refs[a], self.kinds[a], peer), land_refs[a].at[me_j, pl.ds(self.layers[a], 1)]

    def wait(self, after, lands=None):
        n, ns = self.n, len(self.srcs)
        lands = self.lands if lands is None else lands

        def body(*refs):
            src_refs, land_refs = refs[:ns], refs[ns:ns + n]
            send_sems, recv_sems = refs[ns + n], refs[ns + n + 1]
            c = lax.axis_index("c")
            me = 2 * lax.axis_index("x") + lax.axis_index("y")
            for j in range(N_CHIPS):
                @pl.when(me != j)
                def _():
                    for a in range(n):
                        sent, _ = self._ends(src_refs, land_refs, a, 0, j, c)
                        _, landed = self._ends(src_refs, land_refs, a, j, 0, c)
                        cp = pltpu.make_async_remote_copy(
                            src_ref=sent, dst_ref=landed, send_sem=send_sems.at[a * N_CHIPS + j],
                            recv_sem=recv_sems.at[a * N_CHIPS + j],
                            device_id=_chip_of(j, c), device_id_type=MESH)
                        cp.wait_send()
                        cp.wait_recv()

        arrays = self.srcs + list(lands)
        operands = [_in_hbm(a) for a in arrays] + [self.send_sems, self.recv_sems]
        in_specs = [HBM_ONLY] * (ns + n) + [SEM_SPEC, SEM_SPEC]
        if after is not None:
            operands.append(after)
            in_specs.append(HBM_SPEC)
        outs = pl.pallas_call(
            body, name=self.name + "_wait",
            in_specs=in_specs, out_specs=[HBM_ONLY] * (ns + n),
            out_shape=[pltpu.HBM(a.shape, a.dtype) for a in arrays],
            input_output_aliases={i: i for i in range(ns + n)},
            compiler_params=pltpu.CompilerParams(has_side_effects=SIDE_EFFECT),
        )(*operands)
        return list(outs[:ns]), list(outs[ns:])


def _sum_arrivals(zone, own_grads, kind, chip, name, after=None):
    _, layers, r, c = zone.shape
    tm = _pick_rows(r, 256)
    nb = r // tm

    def own_idx(l, i, chip_ref):
        return (0, chip_ref[0] * nb + i, 0) if kind == "row" else (0, i, _slot(kind, chip_ref[0]))

    def slot_idx(k):
        return lambda l, i, chip_ref: (jnp.where(chip_ref[0] == k, (k + 1) % N_CHIPS, k), l, i, 0)

    in_specs = [pl.BlockSpec((None, None, tm, c), slot_idx(k)) for k in range(N_CHIPS)]
    in_specs += [pl.BlockSpec((None, tm, c), own_idx) for _ in own_grads]
    operands = [zone] * N_CHIPS + list(own_grads)
    if after is not None:
        in_specs.append(HBM_SPEC)
        operands.append(after)

    def body(chip_ref, *refs):
        slot_refs, own_refs, o_ref = refs[:N_CHIPS], refs[N_CHIPS:N_CHIPS + layers], refs[-1]
        own = own_refs[0][...]
        for u in range(1, layers):
            own = jnp.where(pl.program_id(0) == u, own_refs[u][...], own)
        acc = None
        for k in range(N_CHIPS):
            term = jnp.where(chip_ref[0] == k, own, slot_refs[k][...]).astype(F32)
            acc = term if acc is None else acc + term
        o_ref[...] = acc.astype(o_ref.dtype)

    return pl.pallas_call(
        body, name=name,
        grid_spec=pltpu.PrefetchScalarGridSpec(
            num_scalar_prefetch=1, grid=(layers, nb), in_specs=in_specs,
            out_specs=pl.BlockSpec((tm, c), lambda l, i, chip_ref: (l * nb + i, 0))),
        out_shape=jax.ShapeDtypeStruct((layers * r, c), BF16),
        compiler_params=_params(("arbitrary", "arbitrary")),
    )(chip, *operands)


class _SiblingSwap:
    def __init__(self, arrays, name, collective_id, after=None):
        self.name, self.n = name, len(arrays)
        n = self.n
        n_in = n + (after is not None)
        sem_shape = pltpu.SemaphoreType.DMA((n,))

        def body(*refs):
            ins, send_sems, recv_sems = refs[:n], refs[n_in], refs[n_in + 1]
            theirs, token = refs[n_in + 2 + n:n_in + 2 + 2 * n], refs[-1]
            _sibling_handshake()
            for cp in self._copies(ins, theirs, send_sems, recv_sems):
                cp.start()
            token[...] = jnp.zeros(token.shape, token.dtype)

        operands, in_specs = [_in_hbm(a) for a in arrays], [HBM_ONLY] * n
        if after is not None:
            operands.append(after)
            in_specs.append(HBM_SPEC)
        outs = pl.pallas_call(
            body, name=name + "_start",
            in_specs=in_specs,
            out_specs=[SEM_SPEC, SEM_SPEC] + [HBM_ONLY] * (2 * n) + [pl.BlockSpec(memory_space=pltpu.VMEM)],
            out_shape=[sem_shape, sem_shape] + [pltpu.HBM(a.shape, a.dtype) for a in arrays] * 2
                      + [jax.ShapeDtypeStruct((8, LANES), F32)],
            input_output_aliases={i: i + 2 for i in range(n)},
            compiler_params=pltpu.CompilerParams(has_side_effects=SIDE_EFFECT, collective_id=collective_id),
        )(*operands)
        self.send_sems, self.recv_sems = outs[0], outs[1]
        self.mine, self.theirs, self.token = list(outs[2:2 + n]), list(outs[2 + n:2 + 2 * n]), outs[-1]

    def _copies(self, mine, theirs, send_sems, recv_sems):
        sibling = (lax.axis_index("x"), lax.axis_index("y"), 1 - lax.axis_index("c"))
        return [pltpu.make_async_remote_copy(src_ref=mine[a], dst_ref=theirs[a], send_sem=send_sems.at[a],
                                             recv_sem=recv_sems.at[a], device_id=sibling, device_id_type=MESH)
                for a in range(self.n)]

    def wait(self, after):
        n = self.n

        def body(*refs):
            for cp in self._copies(refs[:n], refs[n:2 * n], refs[2 * n], refs[2 * n + 1]):
                cp.wait_send()
                cp.wait_recv()

        arrays = self.mine + self.theirs
        outs = pl.pallas_call(
            body, name=self.name + "_wait",
            in_specs=[HBM_ONLY] * (2 * n) + [SEM_SPEC, SEM_SPEC, HBM_SPEC], out_specs=[HBM_ONLY] * (2 * n),
            out_shape=[pltpu.HBM(a.shape, a.dtype) for a in arrays],
            input_output_aliases={i: i for i in range(2 * n)},
            compiler_params=pltpu.CompilerParams(has_side_effects=SIDE_EFFECT),
        )(*[_in_hbm(a) for a in arrays], self.send_sems, self.recv_sems, after)
        return list(outs[:n]), list(outs[n:])


class _GatherDevices:
    def __init__(self, vec):
        sem_shape = pltpu.SemaphoreType.DMA((N_DEV,))

        def body(in_ref, send_sems, recv_sems, vec_ref, out_ref, token):
            for cp in self._copies(in_ref, out_ref, send_sems, recv_sems):
                cp.start()
            token[...] = jnp.zeros(token.shape, token.dtype)

        outs = pl.pallas_call(
            body, name="gather_small_start",
            in_specs=[HBM_ONLY],
            out_specs=[SEM_SPEC, SEM_SPEC, HBM_ONLY, HBM_ONLY, pl.BlockSpec(memory_space=pltpu.VMEM)],
            out_shape=[sem_shape, sem_shape, pltpu.HBM(vec.shape, vec.dtype),
                       pltpu.HBM((N_DEV,) + vec.shape, vec.dtype), jax.ShapeDtypeStruct((8, LANES), F32)],
            input_output_aliases={0: 2},
            compiler_params=pltpu.CompilerParams(has_side_effects=SIDE_EFFECT),
        )(_in_hbm(vec))
        self.send_sems, self.recv_sems, self.vec, self.rows, self.token = outs

    def _copies(self, in_ref, out_ref, send_sems, recv_sems):
        x, y, c = lax.axis_index("x"), lax.axis_index("y"), lax.axis_index("c")
        me = 4 * x + 2 * y + c
        copies = [pltpu.make_async_copy(in_ref, out_ref.at[me], recv_sems.at[0])]
        for rel in range(1, N_DEV):
            peer = (x ^ (rel >> 2), y ^ ((rel >> 1) & 1), c ^ (rel & 1))
            copies.append(pltpu.make_async_remote_copy(
                src_ref=in_ref, dst_ref=out_ref.at[me], send_sem=send_sems.at[rel], recv_sem=recv_sems.at[rel],
                device_id=peer, device_id_type=MESH))
        return copies

    def wait(self, after):
        def body(vec_ref, rows_ref, send_sems, recv_sems, after_ref, vec_out, rows_out):
            copies = self._copies(vec_ref, rows_ref, send_sems, recv_sems)
            copies[0].wait()
            for cp in copies[1:]:
                cp.wait_send()
                cp.wait_recv()

        outs = pl.pallas_call(
            body, name="gather_small_wait",
            in_specs=[HBM_ONLY, HBM_ONLY, SEM_SPEC, SEM_SPEC, HBM_SPEC], out_specs=[HBM_ONLY, HBM_ONLY],
            out_shape=[pltpu.HBM(self.vec.shape, self.vec.dtype), pltpu.HBM(self.rows.shape, self.rows.dtype)],
            input_output_aliases={0: 0, 1: 1},
            compiler_params=pltpu.CompilerParams(has_side_effects=SIDE_EFFECT),
        )(_in_hbm(self.vec), _in_hbm(self.rows), self.send_sems, self.recv_sems, after)
        return outs[1]


BIG = [("a_w_in", "col"), ("a_w_out", "row"), ("kv_w", "row"), ("b_w_q", "row"), ("b_w_out", "row"),
       ("ffn_w_gate_up", "colp"), ("ffn_w_down", "row"), ("ple_w_up", "col"), ("ple_w_gate", "row")]
GATHER_GROUPS = [[("a_w_in", 0), ("small", 0)], [("a_w_out", 0)],
                 [("ffn_w_gate_up", 0), ("ffn_w_down", 0), ("ple_w_gate", 0), ("ple_w_up", 0)],
                 [("kv_w", 0), ("b_w_q", 0), ("b_w_out", 0)],
                 [("ffn_w_gate_up", 1)], [("ffn_w_down", 1), ("ple_w_gate", 1), ("ple_w_up", 1)]]
SCATTER_GROUPS = [[("ple_w_gate", 1), ("ple_w_up", 1), ("ffn_w_down", 1)], [("ffn_w_gate_up", 1)],
                  [("b_w_out", 0), ("b_w_q", 0), ("kv_w", 0)], [("ple_w_gate", 0), ("ple_w_up", 0), ("ffn_w_down", 0)],
                  [("ffn_w_gate_up", 0), ("a_w_out", 0)], [("a_w_in", 0)]]
COLLECTIVE_IDS = {"fill": 0, "swap": 6, "gather": 9, "scatter": 15}
SMALL_SHARDED = ["ln_gain", "ln_bias", "a_lower_bound"]
SMALL_REPLICATED = ["a_norm_gain", "kv_b", "b_b_q", "b_sinks", "b_b_out", "ple_b_gate"]
WEIGHT_ORDER = ["a_w_in", "a_lower_bound", "a_norm_gain", "a_w_out", "kv_w", "kv_b", "b_w_q", "b_b_q", "b_sinks",
                "b_w_out", "b_b_out", "ffn_w_gate_up", "ffn_w_down", "ple_w_up", "ple_w_gate", "ple_b_gate",
                "ln_gain", "ln_bias"]


def _as3(a):
    return a.reshape((-1,) + a.shape[-2:]) if a.ndim >= 3 else a.reshape((1,) + a.shape)


def _pad_lanes(v):
    n = v.shape[-1]
    return jnp.pad(v, ((0, 0), (0, (-n) % LANES)))


def _adam_small_fn(w, mom, vel, g):
    return _adam_fn(w, mom, vel, g, jnp.zeros_like(g))[1:]


def _sum_rows_fn(slots):
    acc = slots[0]
    for s in range(1, slots.shape[0]):
        acc = acc + slots[s]
    return (acc,)


def kernel(x, p, a_w_in, a_lower_bound, a_norm_gain, a_w_out, kv_w, kv_b, b_w_q, b_b_q, b_sinks, b_w_out, b_b_out, ffn_w_gate_up, ffn_w_down, ple_w_up, ple_w_gate, ple_b_gate, ln_gain, ln_bias, loss_target, m_a_w_in, m_a_lower_bound, m_a_norm_gain, m_a_w_out, m_kv_w, m_kv_b, m_b_w_q, m_b_b_q, m_b_sinks, m_b_w_out, m_b_b_out, m_ffn_w_gate_up, m_ffn_w_down, m_ple_w_up, m_ple_w_gate, m_ple_b_gate, m_ln_gain, m_ln_bias, v_a_w_in, v_a_lower_bound, v_a_norm_gain, v_a_w_out, v_kv_w, v_kv_b, v_b_w_q, v_b_b_q, v_b_sinks, v_b_w_out, v_b_b_out, v_ffn_w_gate_up, v_ffn_w_down, v_ple_w_up, v_ple_w_gate, v_ple_b_gate, v_ln_gain, v_ln_bias):
    args = dict(locals())
    wts = {n: args[n] for n in WEIGHT_ORDER}
    mom = {n: args["m_" + n] for n in WEIGHT_ORDER}
    vel = {n: args["v_" + n] for n in WEIGHT_ORDER}
    chip = 2 * lax.axis_index("x") + lax.axis_index("y")
    d = x.shape[-1]
    dq = d // N_CHIPS

    kind_of = dict(BIG)
    kind_of["small"] = "col"
    chip_arr = chip.reshape(1).astype(jnp.int32)
    small_pack = jnp.concatenate([wts[n].reshape(-1, dq) for n in SMALL_SHARDED], axis=0)[None]

    def place_item(key):
        n, layer = key
        if n == "small":
            return small_pack, 0, "col", F32
        return _as3(wts[n]), layer, kind_of[n], BF16

    gathers, where = [], {}
    for gi, group in enumerate(GATHER_GROUPS):
        prev = gathers[-1].token if gathers else None
        placed = _place([place_item(k) for k in group], chip_arr, name=f"place{gi}", after=prev)
        gathers.append(_Exchange("gather", [], placed, [kind_of[k[0]] for k in group],
                                 [0] * len(group), f"gather{gi}", COLLECTIVE_IDS["gather"] + gi, after=prev,
                                 halves=[k[0] != "small" for k in group]))
        for k in group:
            where[k] = gi
    all_started = gathers[-1].token
    ready = {}

    fills = {}

    def pass_on(gi, after):
        if gi not in fills:
            group = GATHER_GROUPS[gi]
            outs = gathers[gi].wait(after)[1]
            split = [i for i, k in enumerate(group) if k[0] != "small"]
            fills[gi] = (outs, split, _SiblingFill([outs[i] for i in split], [kind_of[group[i][0]] for i in split],
                                                   f"fill{gi}", COLLECTIVE_IDS["fill"] + gi))

    def wget(name, layer, after):
        key = (name, layer)
        if key not in ready:
            gi = where[key]
            after = all_started if gi == 0 else after
            pass_on(gi, after)
            if 1 <= gi < len(GATHER_GROUPS) - 1:
                pass_on(gi + 1, after)
                after = fills[gi + 1][2].token
            outs, split, fill = fills[gi]
            for i, arr in zip(split, fill.wait(after)):
                outs[i] = arr
            for k, arr in zip(GATHER_GROUPS[gi], outs):
                ready[k] = arr
        return ready[key]

    small_full = wget("small", 0, None)[0]
    ln_gain_f = small_full[0:6].reshape(DEPTH, 3, d)
    ln_bias_f = small_full[6:12].reshape(DEPTH, 3, d)
    alb_f = small_full[12:14]

    group_of = {k: gi for gi, group in enumerate(SCATTER_GROUPS) for k in group}
    grads_done, zones, scatters = {}, {}, []

    def grad_sink(name, layer, grad):
        grads_done[(name, layer)] = grad
        if name not in zones:
            zones[name] = lax.empty((N_CHIPS,) + _as3(wts[name]).shape, BF16)
        gi = group_of[(name, layer)]
        group = SCATTER_GROUPS[gi]
        if not all(k in grads_done for k in group):
            return None
        ex = _Exchange("scatter", [grads_done[k] for k in group], [zones[k[0]] for k in group],
                       [kind_of[k[0]] for k in group], [k[1] for k in group], f"scatter{gi}",
                       COLLECTIVE_IDS["scatter"] + gi)
        for k, zone in zip(group, ex.lands):
            zones[k[0]] = zone
        scatters.append((ex, group))
        return ex.token

    small = {}

    def small_sink(loss, gs):
        ln_g = jnp.concatenate([gs[f"ln_gain_{i}_{j}"] for i in range(DEPTH) for j in range(3)], axis=0)
        ln_b = jnp.concatenate([gs[f"ln_bias_{i}_{j}"] for i in range(DEPTH) for j in range(3)], axis=0)
        ple_bg = jnp.concatenate([gs[f"ple_b_{i}"] for i in range(DEPTH)], axis=0)
        small["list"] = [ln_g.reshape(1, -1), ln_b.reshape(1, -1), gs["alb"].reshape(1, -1), gs["norm_gain"],
                         gs["kv_b"], gs["b_q"], _pad_lanes(gs["sinks"]), gs["b_out"], ple_bg.reshape(1, -1), loss]
        small["gather"] = _GatherDevices(jnp.concatenate(small["list"], axis=1))
        return small["gather"].token

    loss, grad_x, gs = _local_step(
        x[0], p.reshape((p.shape[0],) + p.shape[2:]), loss_target[0], wget, grad_sink, ln_gain_f, ln_bias_f, alb_f, a_norm_gain, kv_b, b_b_q,
        b_sinks, b_b_out, ple_b_gate, small_sink)

    res = {}

    def arrive(batch, after):
        for ex, group in batch:
            srcs, outs = ex.wait(after, lands=[zones[k[0]] for k in group])
            for k, grad, zone in zip(group, srcs, outs):
                grads_done[k], zones[k[0]] = grad, zone

    def half_sums(names, batch, after):
        partial = []
        for n in names:
            own = [grads_done[(n, layer)] for layer in range(zones[n].shape[1])]
            partial.append(_sum_arrivals(zones[n], own, kind_of[n], chip_arr, f"sum_{n}", after=after))
        return _SiblingSwap(partial, f"swap{batch}", COLLECTIVE_IDS["swap"] + batch, after=after)

    def update(names, swap, after):
        for n, own, sib in zip(names, *swap.wait(after)):
            shp = wts[n].shape
            flat = lambda a: a.reshape(-1, shp[-1])
            out = _rowwise(_adam_fn, [flat(wts[n]), flat(mom[n]), flat(vel[n]), own, sib], [],
                           [(own.shape, F32)] * 4, name=f"adam_{n}")
            res[n] = [o.reshape(shp) for o in out]
        return res[names[-1]][1]

    last_names = [k[0] for k in SCATTER_GROUPS[-1]]
    batches = [["ffn_w_gate_up"], [n for n, _ in BIG if n != "ffn_w_gate_up" and n not in last_names], last_names]
    arrive(scatters[:-1], grad_x)
    swap0 = half_sums(batches[0], 0, None)
    swap1 = half_sums(batches[1], 1, swap0.token)
    updated = update(batches[0], swap0, swap1.token)
    arrive(scatters[-1:], updated)
    swap2 = half_sums(batches[2], 2, swap1.token)
    updated = update(batches[1], swap1, swap2.token)
    update(batches[2], swap2, updated)

    small_list = small["list"]
    everyone = small["gather"].wait(grad_x)
    total, = _rowwise(_sum_rows_fn, [everyone], [], [(everyone.shape[1:], F32)], name="sum_small")
    offs, pos = [], 0
    for v in small_list:
        offs.append((pos, v.shape[1]))
        pos += v.shape[1]

    def seg(k):
        return total[0, offs[k][0]:offs[k][0] + offs[k][1]]

    def my_cols(full, rows):
        return lax.dynamic_slice_in_dim(full.reshape(rows, N_CHIPS, dq), chip, 1, axis=1).reshape(rows, dq)

    n_sink = b_sinks.shape[-1]
    small_grads = {
        "ln_gain": my_cols(seg(0), 6).reshape(ln_gain.shape), "ln_bias": my_cols(seg(1), 6).reshape(ln_bias.shape),
        "a_lower_bound": my_cols(seg(2), 2), "a_norm_gain": seg(3).reshape(a_norm_gain.shape),
        "kv_b": seg(4).reshape(kv_b.shape), "b_b_q": seg(5).reshape(b_b_q.shape),
        "b_sinks": seg(6)[:n_sink].reshape(b_sinks.shape), "b_b_out": seg(7).reshape(b_b_out.shape),
        "ple_b_gate": seg(8).reshape(ple_b_gate.shape)}
    names = SMALL_SHARDED + SMALL_REPLICATED
    pack = lambda dct: _pad_lanes(jnp.concatenate([dct[n].reshape(1, -1) for n in names], axis=1))
    g_pack = pack(small_grads)
    upd = _rowwise(_adam_small_fn, [pack(wts), pack(mom), pack(vel), g_pack], [], [(g_pack.shape, F32)] * 3,
                   name="adam_small")
    pos = 0
    for n in names:
        size = wts[n].size
        res[n] = [small_grads[n]] + [u[0, pos:pos + size].reshape(wts[n].shape) for u in upd]
        pos += size

    outs = [seg(9)[0], grad_x[None]]
    for k in range(4):
        outs += [res[n][k] for n in WEIGHT_ORDER]
    return tuple(outs)
```

```python
import functools

import jax
import jax.numpy as jnp
from jax import lax
from jax.experimental import pallas as pl
from jax.experimental.pallas import tpu as pltpu

F32 = jnp.float32
BF16 = jnp.bfloat16
MESH = pl.DeviceIdType.MESH

LANES = 128
HG_DK = 128
HG_CHUNK = 64
HG_SUB = 16
HG_ROWS = 512
HG_HEADS_PER_STEP = 2
LOG2_E = 1.4426950408889634
ATT_HD = 64
ATT_G = 4
WINDOW = 128
DEPTH = 2
ALPHA = (2.0 * DEPTH) ** 0.25
LN_EPS = 1e-5
RMS_EPS = 1e-6
ADAM_LR, ADAM_B1, ADAM_B2, ADAM_EPS, ADAM_WD, ADAM_STEP = 0.001, 0.9, 0.999, 1e-08, 0.01, 10
N_CHIPS = 4
N_DEV = 8
VMEM_LIMIT = 56 * 1024 * 1024
NEG = -1e30


def _pick(n, cap):
    best = None
    for d in range(LANES, min(n, cap) + 1, LANES):
        if n % d == 0:
            best = d
    return n if best is None else best


def _pick_rows(m, cap):
    best = None
    for d in range(16, min(m, cap) + 1, 16):
        if m % d == 0:
            best = d
    return m if best is None else best


def _params(sem):
    return pltpu.CompilerParams(dimension_semantics=sem, vmem_limit_bytes=VMEM_LIMIT)


def _zeros_index(ndim, grid_rank=3):
    return (lambda i, j, kk: (0,) * ndim) if grid_rank == 3 else (lambda kk, i: (0,) * ndim)


def _mm(a, b, *, name, la=None, lb=None, ta=False, tb=False, bias=None, add=None, out_dtype=F32,
        out_layers=None, out_layer=None, after=None, post=None, tile_cols=None, caps=(1024, 1536, 2048),
        a_parts=None, b_parts=None):
    ar, ac = a.shape[-2:]
    br, bc = b.shape[-2:]
    assert a_parts is None or (not ta and la is None and a.shape[0] == a_parts)
    assert b_parts is None or (not tb and lb is None and b.shape[0] == b_parts)
    m, k = (ac, ar) if ta else (ar, ac * (a_parts or 1))
    k2, n = (bc, br) if tb else (br, bc * (b_parts or 1))
    assert k == k2, (a.shape, b.shape, ta, tb)
    if post is not None:
        caps = (512, n if tile_cols is None else tile_cols, caps[2])
    tm, tn, tk = _pick(m, caps[0]), _pick(bc if b_parts else n, caps[1]), _pick(ac if a_parts else k, caps[2])
    assert post is None or tn == caps[1]
    nk = k // tk
    gi, gj = m // tm, n // tn
    a_bytes, b_bytes = m * k * a.dtype.itemsize, k * n * b.dtype.itemsize
    rows_outer = (a_bytes + b_bytes * (gi if gj * nk > 1 else 1)) <= (b_bytes + a_bytes * (gj if gi * nk > 1 else 1))
    k_outer = post is not None and nk > 1 and gj == 1
    grid = (nk, gi) if k_outer else (gi, gj, nk) if rows_outer else (gj, gi, nk)
    keep_at = ta and nk == 1 and gj > 1 and rows_outer

    def bs(block, idx, late=False):
        if k_outer:
            return pl.BlockSpec(block, lambda kk, i: idx(jnp.where(kk == nk - 1, i, 0) if late else i, 0, kk))
        return pl.BlockSpec(block, idx if rows_outer else (lambda q, p, kk: idx(p, q, kk)))

    def spec(block, idx, layer):
        if layer is None:
            return bs(block, idx)
        return bs((None,) + block, lambda i, j, kk: (layer,) + idx(i, j, kk))

    a_spec = spec((tk, tm), lambda i, j, kk: (kk, i), la) if ta else spec((tm, tk), lambda i, j, kk: (i, kk), la)
    b_spec = spec((tn, tk), lambda i, j, kk: (j, kk), lb) if tb else spec((tk, tn), lambda i, j, kk: (kk, j), lb)
    if a_parts:
        a_spec = bs((None, tm, tk), lambda i, j, kk: (kk // (ac // tk), i, kk % (ac // tk)))
    if b_parts:
        b_spec = bs((None, tk, tn), lambda i, j, kk: (j // (bc // tn), kk, j % (bc // tn)))
    in_specs, operands = [a_spec, b_spec], [a, b]
    if bias is not None:
        in_specs.append(bs((1, tn), lambda i, j, kk: (0, j)))
        operands.append(bias)
    if add is not None:
        in_specs.append(bs((tm, tn), lambda i, j, kk: (i, j), late=True))
        operands.append(add)
    if after is not None:
        in_specs.append(pl.BlockSpec(memory_space=pl.ANY))
        operands.append(after)
    dims = (((0 if ta else 1,), (1 if tb else 0,)), ((), ()))
    has_bias, has_add = bias is not None, add is not None
    if post is None:
        fn, rows, whole, outs, sums = None, [], [], [], []
        out_shape = jax.ShapeDtypeStruct((m, n) if out_layers is None else (out_layers, m, n), out_dtype)
        out_specs = spec((tm, tn), lambda i, j, kk: (i, j), out_layer)
    else:
        fn, rows, whole, outs, sums = post
        in_specs += [bs((tm, r.shape[-1] // gj), lambda i, j, kk: (i, j), late=True) for r in rows]
        in_specs += [pl.BlockSpec(tuple(w.shape), _zeros_index(w.ndim, len(grid))) for w in whole]
        operands += list(rows) + list(whole)
        out_shape = [jax.ShapeDtypeStruct(sh, dt) for sh, dt in list(outs) + list(sums)]
        out_specs = ([bs((tm, sh[-1] // gj), lambda i, j, kk: (i, j), late=True) for sh, _ in outs]
                     + [pl.BlockSpec(tuple(sh), _zeros_index(len(sh), len(grid))) for sh, _ in sums])
    n_in, n_extra, n_outs, n_sums = len(operands), len(rows) + len(whole), len(outs), len(sums)

    def body(*refs):
        a_ref, b_ref = refs[0], refs[1]
        pos = 2
        bias_ref = add_ref = None
        if has_bias:
            bias_ref = refs[pos]
            pos += 1
        if has_add:
            add_ref = refs[pos]
            pos += 1
        extra_refs = refs[n_in - n_extra:n_in]
        out_refs = refs[n_in:n_in + max(n_outs, 1)]
        sum_refs = refs[n_in + n_outs:n_in + n_outs + n_sums]
        acc_ref = refs[-1] if nk > 1 else None
        if keep_at:
            at_ref = refs[-1]

            @pl.when(pl.program_id(1) == 0)
            def _():
                at_ref[...] = a_ref[...].astype(BF16).T

            part = lax.dot_general(at_ref[...], b_ref[...].astype(BF16), (((1,), (1 if tb else 0,)), ((), ())),
                                   preferred_element_type=F32)
        else:
            part = lax.dot_general(a_ref[...].astype(BF16), b_ref[...].astype(BF16), dims,
                                   preferred_element_type=F32)

        def finish(total):
            if has_bias:
                total = total + bias_ref[...]
            if has_add:
                total = total + add_ref[...]
            if fn is None:
                out_refs[0][...] = total.astype(out_refs[0].dtype)
                return
            res = fn(total, *[r[...] for r in extra_refs])
            for ref, val in zip(out_refs, res[:n_outs]):
                ref[...] = val.astype(ref.dtype)
            if n_sums:
                @pl.when(pl.program_id(1 if k_outer or not rows_outer else 0) == 0)
                def _():
                    for ref in sum_refs:
                        ref[...] = jnp.zeros(ref.shape, ref.dtype)

                for ref, val in zip(sum_refs, res[n_outs:]):
                    ref[...] += val

        if nk == 1:
            finish(part)
        elif k_outer:
            kk = pl.program_id(0)
            rows_i = pl.ds(pl.multiple_of(pl.program_id(1) * tm, tm), tm)

            @pl.when(kk == 0)
            def _():
                acc_ref[rows_i, :] = part

            @pl.when(kk > 0)
            def _():
                acc_ref[rows_i, :] += part

            @pl.when(kk == nk - 1)
            def _():
                finish(acc_ref[rows_i, :])
        else:
            kk = pl.program_id(2)

            @pl.when(kk == 0)
            def _():
                acc_ref[...] = part

            @pl.when(kk > 0)
            def _():
                acc_ref[...] += part

            @pl.when(kk == nk - 1)
            def _():
                finish(acc_ref[...])

    return pl.pallas_call(
        body, name=name, grid=grid, in_specs=in_specs, out_specs=out_specs, out_shape=out_shape,
        scratch_shapes=([pltpu.VMEM((m, n) if k_outer else (tm, tn), F32)] if nk > 1
                        else [pltpu.VMEM((tm, tk), BF16)] if keep_at else []),
        compiler_params=_params(("arbitrary", "arbitrary") if k_outer
                                else ("arbitrary" if n_sums else "parallel", "arbitrary" if keep_at else "parallel",
                                      "arbitrary") if rows_outer
                                else ("parallel", "arbitrary" if n_sums else "parallel", "arbitrary")),
    )(*operands)


def _rowwise(fn, rows, whole, outs, sums=(), *, name, tm=256):
    m = rows[0].shape[-2]
    tm = _pick_rows(m, tm)
    n_rows, n_whole, n_outs, n_sums = len(rows), len(whole), len(outs), len(sums)

    def rspec(shape):
        lead = len(shape) - 2
        return pl.BlockSpec(tuple(shape[:-2]) + (tm, shape[-1]), lambda i: (0,) * lead + (i, 0))

    def wspec(shape):
        return pl.BlockSpec(tuple(shape), lambda i: (0,) * len(shape))

    def body(*refs):
        vals = [r[...] for r in refs[:n_rows + n_whole]]
        out_refs = refs[n_rows + n_whole:n_rows + n_whole + n_outs]
        sum_refs = refs[n_rows + n_whole + n_outs:]
        res = fn(*vals)
        for ref, val in zip(out_refs, res[:n_outs]):
            ref[...] = val.astype(ref.dtype)
        if n_sums:
            @pl.when(pl.program_id(0) == 0)
            def _():
                for ref in sum_refs:
                    ref[...] = jnp.zeros(ref.shape, ref.dtype)

            for ref, val in zip(sum_refs, res[n_outs:]):
                ref[...] += val

    result = pl.pallas_call(
        body, name=name, grid=(m // tm,),
        in_specs=[rspec(r.shape) for r in rows] + [wspec(w.shape) for w in whole],
        out_specs=[rspec(s) for s, _ in outs] + [wspec(s) for s, _ in sums],
        out_shape=[jax.ShapeDtypeStruct(s, d) for s, d in list(outs) + list(sums)],
        compiler_params=_params(("arbitrary",)),
    )(*rows, *whole)
    return result


def _sigmoid(v):
    return jax.nn.sigmoid(v)


def _col_sum(v):
    return jnp.sum(v, axis=0, keepdims=True)


def _ln_stats(z):
    mu = jnp.mean(z, axis=-1, keepdims=True)
    zc = z - mu
    var = jnp.mean(zc * zc, axis=-1, keepdims=True)
    rstd = lax.rsqrt(var + LN_EPS)
    return zc * rstd, rstd


def _ln_fwd_fn(xin, h, gain, bias):
    xhat, _ = _ln_stats(ALPHA * xin + h)
    y = xhat * gain + bias
    return y, y


def _ple_ln_fwd_fn(xin, pg, pu, gain, bias):
    xhat, _ = _ln_stats(ALPHA * xin + _sigmoid(pg) * pu)
    y = xhat * gain + bias
    return y, y


def _ln_dz(dy, z, gain):
    xhat, rstd = _ln_stats(z)
    dxhat = dy * gain
    dz = rstd * (dxhat - jnp.mean(dxhat, axis=-1, keepdims=True)
                 - xhat * jnp.mean(dxhat * xhat, axis=-1, keepdims=True))
    return dz, _col_sum(dy * xhat), _col_sum(dy)


def _ln_bwd_fn(dy, xin, h, gain):
    dz, dgain, dbias = _ln_dz(dy, ALPHA * xin + h, gain)
    return ALPHA * dz, dz, dgain, dbias, _col_sum(dz)


def _ple_ln_bwd_fn(dy, xin, pg, pu, gain):
    sg = _sigmoid(pg)
    dz, dgain, dbias = _ln_dz(dy, ALPHA * xin + sg * pu, gain)
    dpg = dz * pu * sg * (1.0 - sg)
    return ALPHA * dz, dpg, dz * sg, dgain, dbias, _col_sum(dpg)


def _swiglu_fwd_fn(gu):
    hid = gu.shape[-1] // 2
    gate, up = gu[:, :hid], gu[:, hid:]
    return gu, gate * _sigmoid(gate) * up


def _swiglu_bwd_fn(dact, gu):
    gu = gu.astype(F32)
    hid = gu.shape[-1] // 2
    gate, up = gu[:, :hid], gu[:, hid:]
    sg = _sigmoid(gate)
    dgate = dact * up * sg * (1.0 + gate * (1.0 - sg))
    dup = dact * gate * sg
    return (jnp.concatenate([dgate, dup], axis=-1),)


def _loss_fn(y, target):
    err = y - target
    inv = 1.0 / y.shape[-1]
    part = 0.5 * inv * jnp.sum(jnp.sum(err * err, axis=-1, keepdims=True), axis=0, keepdims=True)
    return err * inv, jnp.broadcast_to(part, (1, LANES))


def _adam_fn(w, mom, vel, p_own, p_sib):
    g = p_own.astype(F32) + p_sib.astype(F32)
    m_new = ADAM_B1 * mom + (1.0 - ADAM_B1) * g
    v_new = ADAM_B2 * vel + (1.0 - ADAM_B2) * (g * g)
    m_hat = m_new / (1.0 - ADAM_B1 ** ADAM_STEP)
    v_hat = v_new / (1.0 - ADAM_B2 ** ADAM_STEP)
    delta = -ADAM_LR * (m_hat / (jnp.sqrt(v_hat) + ADAM_EPS) + ADAM_WD * w)
    return g, delta, m_new, v_new


def _split2(x):
    hi = x.astype(BF16)
    return hi, (x - hi.astype(F32)).astype(BF16)


def _dot3(a, b, dims):
    a_hi, a_lo = _split2(a)
    b_hi, b_lo = _split2(b)
    dn = (dims, ((), ()))
    return (lax.dot_general(a_hi, b_hi, dn, preferred_element_type=F32)
            + (lax.dot_general(a_hi, b_lo, dn, preferred_element_type=F32)
               + lax.dot_general(a_lo, b_hi, dn, preferred_element_type=F32)))


def _tdot(mask01, b):
    m = mask01.astype(BF16)
    b_hi = b.astype(BF16)
    rest = b - b_hi.astype(F32)
    b_mid = rest.astype(BF16)
    b_lo = (rest - b_mid.astype(F32)).astype(BF16)
    dn = (((1,), (0,)), ((), ()))
    return (lax.dot_general(m, b_hi, dn, preferred_element_type=F32)
            + (lax.dot_general(m, b_mid, dn, preferred_element_type=F32)
               + lax.dot_general(m, b_lo, dn, preferred_element_type=F32)))


def _hdot(a, b):
    return _dot3(a, b, ((1,), (0,)))


def _hdot_nt(a, b):
    return _dot3(a, b, ((1,), (1,)))


def _hdot_tn(a, b):
    return _dot3(a, b, ((0,), (0,)))


def _dot(a, b):
    return lax.dot_general(a.astype(BF16), b.astype(BF16), (((1,), (0,)), ((), ())), preferred_element_type=F32)


def _dot_nt(a, b):
    return lax.dot_general(a.astype(BF16), b.astype(BF16), (((1,), (1,)), ((), ())), preferred_element_type=F32)


def _dot_tn(a, b):
    return lax.dot_general(a.astype(BF16), b.astype(BF16), (((0,), (0,)), ((), ())), preferred_element_type=F32)


def _hg_masks():
    c = HG_CHUNK
    row = lax.broadcasted_iota(jnp.int32, (c, c), 0)
    col = lax.broadcasted_iota(jnp.int32, (c, c), 1)
    base = row & (-HG_SUB)
    return row, col, base, col <= row, col < base


def _hg_gates(qr, fr, alb):
    lbound = _sigmoid(alb[0:1, :] - alb[1:2, :])
    sig = _sigmoid(fr)
    forget = lbound + (1.0 - lbound) * sig
    kk = (1.0 - lbound) * _sigmoid(-fr)
    qt = qr * _sigmoid(qr) * (HG_DK ** -0.5)
    return qt, kk, jnp.log(forget), lbound, sig, forget


def _hg_scores(qt, kk, g, scores=True):
    c, nsub = HG_CHUNK, HG_CHUNK // HG_SUB
    row, col, base, causal, below = _hg_masks()
    b = _tdot(causal, g)
    rr = _tdot(below, g)
    bq = b - rr
    qh = qt * jnp.exp(bq)
    edecs = [None]
    parts = [jnp.zeros((HG_SUB, c), F32)]
    for i in range(1, nsub):
        edec = jnp.exp(jnp.minimum(rr[i * HG_SUB:i * HG_SUB + 1, :] - b, 0.0))
        edecs.append(edec)
        if scores:
            parts.append(_dot_nt(qh[i * HG_SUB:(i + 1) * HG_SUB, :], kk * edec))
    q3 = qt.reshape(nsub, HG_SUB, HG_DK)
    if not scores:
        return None, b, bq, qh, edecs, (b.reshape(nsub, HG_SUB, HG_DK), q3, kk.reshape(nsub, HG_SUB, HG_DK))
    a = jnp.where(below, jnp.concatenate(parts, axis=0), 0.0)
    b2 = b * LOG2_E
    b3 = b2.reshape(nsub, HG_SUB, HG_DK)
    c3 = (b2 - jnp.log2(kk)).reshape(nsub, HG_SUB, HG_DK)
    for j in range(HG_SUB):
        ek = jnp.exp2(b3 - c3[:, j:j + 1, :])
        colv = jnp.sum(q3 * ek, axis=-1, keepdims=True).reshape(c, 1)
        a = jnp.where(col == base + j, colv, a)
    a = jnp.where(causal, a, 0.0)
    return a, b, bq, qh, edecs, None


def _hg_norm(o, gr, gain):
    r = lax.rsqrt(jnp.mean(o * o, axis=-1, keepdims=True) + RMS_EPS)
    sg = _sigmoid(gr)
    return o * r * gain, r, sg


def _hgrn2_fwd(proj, alb, gain, *, rb):
    m, d4 = proj.shape
    d = d4 // 4
    heads = d // HG_DK
    hp = HG_HEADS_PER_STEP
    rb = min(rb, m)
    cpb = rb // HG_CHUNK
    nrb = m // rb

    def body(q_ref, f_ref, v_ref, g_ref, alb_ref, gain_ref, o_ref, og_ref, st_ref, a_ref, state):
        @pl.when(pl.program_id(1) == 0)
        def _():
            state[...] = jnp.zeros(state.shape, F32)

        def chunk(ci, carry):
            sl = pl.ds(pl.multiple_of(ci * HG_CHUNK, HG_CHUNK), HG_CHUNK)
            for u in range(hp):
                ln = slice(u * HG_DK, (u + 1) * HG_DK)
                qt, kk, g, _, _, _ = _hg_gates(q_ref[sl, ln], f_ref[sl, ln], alb_ref[:, ln])
                v = v_ref[sl, ln]
                st = state[u]
                st_ref[u, ci] = st
                a, b, _, _, _, _ = _hg_scores(qt, kk, g)
                a_ref[u, ci] = a.astype(a_ref.dtype)
                o = _dot(a, v) + _dot_nt(qt * jnp.exp(b), st)
                b_last = b[HG_CHUNK - 1:HG_CHUNK, :]
                state[u] = st * jnp.exp(b_last) + _hdot_tn(v, kk * jnp.exp(b_last - b))
                o_ref[sl, ln] = o
                n, _, sg = _hg_norm(o, g_ref[sl, ln], gain_ref[...])
                og_ref[sl, ln] = (n * g_ref[sl, ln] * sg).astype(og_ref.dtype)
            return carry

        lax.fori_loop(0, cpb, chunk, 0)

    def col(cidx):
        return pl.BlockSpec((rb, hp * HG_DK), lambda h, r: (r, cidx * (heads // hp) + h))

    return pl.pallas_call(
        body, name="hgrn2_fwd", grid=(heads // hp, nrb),
        in_specs=[col(0), col(1), col(2), col(3),
                  pl.BlockSpec((2, hp * HG_DK), lambda h, r: (0, h)),
                  pl.BlockSpec((1, HG_DK), lambda h, r: (0, 0))],
        out_specs=[pl.BlockSpec((rb, hp * HG_DK), lambda h, r: (r, h)),
                   pl.BlockSpec((rb, hp * HG_DK), lambda h, r: (r, h)),
                   pl.BlockSpec((hp, cpb, HG_DK, HG_DK), lambda h, r: (h, r, 0, 0)),
                   pl.BlockSpec((hp, cpb, HG_CHUNK, HG_CHUNK), lambda h, r: (h, r, 0, 0))],
        out_shape=[jax.ShapeDtypeStruct((m, d), F32), jax.ShapeDtypeStruct((m, d), BF16),
                   jax.ShapeDtypeStruct((heads, m // HG_CHUNK, HG_DK, HG_DK), F32),
                   jax.ShapeDtypeStruct((heads, m // HG_CHUNK, HG_CHUNK, HG_CHUNK), BF16)],
        scratch_shapes=[pltpu.VMEM((hp, HG_DK, HG_DK), F32)],
        compiler_params=_params(("parallel", "arbitrary")),
    )(proj, proj, proj, proj, alb, gain)


def _hgrn2_bwd(proj, o_pre, states, scores, dog, alb, gain, *, rb):
    m, d4 = proj.shape
    d = d4 // 4
    heads = d // HG_DK
    rb = min(rb, m)
    cpb = rb // HG_CHUNK
    nrb = m // rb
    c, nsub = HG_CHUNK, HG_CHUNK // HG_SUB

    def body(q_ref, f_ref, v_ref, g_ref, o_ref, st_ref, a_ref, dog_ref, alb_ref, gain_ref,
             dp_ref, dalb_ref, dgain_ref, dstate, carry_ref):
        first = (pl.program_id(0) == 0) & (pl.program_id(1) == 0)

        @pl.when(first)
        def _():
            dgain_ref[...] = jnp.zeros(dgain_ref.shape, F32)

        @pl.when(pl.program_id(1) == 0)
        def _():
            dstate[...] = jnp.zeros(dstate.shape, F32)
            carry_ref[...] = jnp.zeros(carry_ref.shape, F32)
            dalb_ref[...] = jnp.zeros(dalb_ref.shape, F32)

        row, col, base, causal, below = _hg_masks()
        sub_iota = lax.broadcasted_iota(jnp.int32, (nsub, HG_SUB, HG_DK), 1)
        row_k = lax.broadcasted_iota(jnp.int32, (c, HG_DK), 0)
        upper = col >= row

        def chunk(step, carry):
            ci = cpb - 1 - step
            sl = pl.ds(pl.multiple_of(ci * HG_CHUNK, HG_CHUNK), HG_CHUNK)
            qr, fr, v, gr = q_ref[sl, :], f_ref[sl, :], v_ref[sl, :], g_ref[sl, :]
            qt, kk, g, lbound, sig, forget = _hg_gates(qr, fr, alb_ref[...])
            o = o_ref[sl, :]
            dogv = dog_ref[sl, :]
            gain_v = gain_ref[...]
            n, r, sg = _hg_norm(o, gr, gain_v)
            dgr = dogv * n * sg * (1.0 + gr * (1.0 - sg))
            dn = dogv * gr * sg
            dgain_ref[...] += _col_sum(dn * o * r)
            u = dn * gain_v
            d_o = r * u - o * (r * r * r) * jnp.mean(u * o, axis=-1, keepdims=True)
            st0 = st_ref[ci]
            dst = dstate[...]
            _, b, bq, qh, edecs, (b3, q3, k3) = _hg_scores(qt, kk, g, scores=False)
            a = a_ref[ci]
            eb = jnp.exp(b)
            b_last = b[c - 1:c, :]
            kdl_dec = jnp.exp(b_last - b)
            kdl = kk * kdl_dec
            d_a = jnp.where(causal, _dot_nt(d_o, v), 0.0)
            d_at = _dot_nt(v, d_o)
            dv = _dot_tn(a, d_o) + _dot_nt(kdl, dst)
            dq = eb * _hdot(d_o, st0)
            dk = _hdot(v, dst) * kdl_dec
            d_a_below = jnp.where(below, d_a, 0.0)
            dq_parts = [jnp.zeros((HG_SUB, HG_DK), F32)]
            for i in range(1, nsub):
                lo, hi = i * HG_SUB, (i + 1) * HG_SUB
                dq_parts.append(_hdot(d_a_below[lo:hi, :], kk * edecs[i]))
                gi = _hdot(d_at[:, lo:hi], qh[lo:hi, :])
                dk = dk + jnp.where(row_k < lo, edecs[i] * gi, 0.0)
            dq = dq + jnp.concatenate(dq_parts, axis=0) * jnp.exp(bq)
            dq3 = jnp.zeros((nsub, HG_SUB, HG_DK), F32)
            dk3 = jnp.zeros((nsub, HG_SUB, HG_DK), F32)
            d_diag = jnp.concatenate([d_a[i * HG_SUB:(i + 1) * HG_SUB, i * HG_SUB:(i + 1) * HG_SUB]
                                      for i in range(nsub)], axis=0).reshape(nsub, HG_SUB, HG_SUB)
            for j in range(HG_SUB):
                e = jnp.exp(jnp.minimum(b3 - b3[:, j:j + 1, :], 0.0))
                t1 = d_diag[:, :, j:j + 1] * e
                dq3 = dq3 + t1 * k3[:, j:j + 1, :]
                dk3 = jnp.where(sub_iota == j, jnp.sum(t1 * q3, axis=1, keepdims=True), dk3)
            dq = dq + dq3.reshape(c, HG_DK)
            dk = dk + dk3.reshape(c, HG_DK)
            dstate[...] = dst * jnp.exp(b_last) + _hdot_tn(d_o, qt * eb)
            dglog = _tdot(upper, qt * dq - kk * dk) + carry_ref[...]
            carry_ref[...] = dglog[0:1, :]
            dforget = dglog / forget
            one_m_lb = 1.0 - lbound
            dsig = (dforget - dk) * one_m_lb
            sneg = _sigmoid(-fr)
            dlb = _col_sum(dforget * (1.0 - sig) - dk * sneg)
            dalb0 = dlb * lbound * one_m_lb
            dalb_ref[...] += jnp.concatenate([dalb0, -dalb0], axis=0)
            sq = _sigmoid(qr)
            dp_ref[0, sl, :] = (dq * (HG_DK ** -0.5) * sq * (1.0 + qr * (1.0 - sq))).astype(dp_ref.dtype)
            dp_ref[1, sl, :] = (dsig * sig * (1.0 - sig)).astype(dp_ref.dtype)
            dp_ref[2, sl, :] = dv.astype(dp_ref.dtype)
            dp_ref[3, sl, :] = dgr.astype(dp_ref.dtype)
            return carry

        lax.fori_loop(0, cpb, chunk, 0, unroll=2)

    def rev(r):
        return nrb - 1 - r

    def col(cidx):
        return pl.BlockSpec((rb, HG_DK), lambda h, r: (rev(r), cidx * heads + h))

    def head_rows():
        return pl.BlockSpec((rb, HG_DK), lambda h, r: (rev(r), h))

    return pl.pallas_call(
        body, name="hgrn2_bwd", grid=(heads, nrb),
        in_specs=[col(0), col(1), col(2), col(3), head_rows(),
                  pl.BlockSpec((None, cpb, HG_DK, HG_DK), lambda h, r: (h, rev(r), 0, 0)),
                  pl.BlockSpec((None, cpb, HG_CHUNK, HG_CHUNK), lambda h, r: (h, rev(r), 0, 0)),
                  head_rows(),
                  pl.BlockSpec((2, HG_DK), lambda h, r: (0, h)),
                  pl.BlockSpec((1, HG_DK), lambda h, r: (0, 0))],
        out_specs=[pl.BlockSpec((4, rb, HG_DK), lambda h, r: (0, rev(r), h)),
                   pl.BlockSpec((2, HG_DK), lambda h, r: (0, h)),
                   pl.BlockSpec((1, HG_DK), lambda h, r: (0, 0))],
        out_shape=[jax.ShapeDtypeStruct((4, m, d), BF16), jax.ShapeDtypeStruct((2, d), F32),
                   jax.ShapeDtypeStruct((1, HG_DK), F32)],
        scratch_shapes=[pltpu.VMEM((HG_DK, HG_DK), F32), pltpu.VMEM((1, HG_DK), F32)],
        compiler_params=_params(("arbitrary", "arbitrary")),
    )(proj, proj, proj, proj, o_pre, states, scores, dog, alb, gain)


def _swa_probs(qh, kp, kc, sink, slope, has_prev, lse=None):
    rows = qh.shape[0]
    qi = lax.broadcasted_iota(jnp.int32, (rows, WINDOW), 0) & (WINDOW - 1)
    si = lax.broadcasted_iota(jnp.int32, (rows, WINDOW), 1)
    scale = ATT_HD ** -0.5
    dist_c = (qi - si).astype(F32)
    s_p = _dot_nt(qh, kp) * scale - slope * (dist_c + float(WINDOW))
    s_c = _dot_nt(qh, kc) * scale - slope * dist_c
    s_p = jnp.where((si > qi) & has_prev, s_p, NEG)
    s_c = jnp.where(si <= qi, s_c, NEG)
    if lse is not None:
        return jnp.exp(s_p - lse), jnp.exp(s_c - lse), jnp.exp(sink - lse), lse
    mx = jnp.maximum(jnp.maximum(jnp.max(s_p, axis=-1, keepdims=True), jnp.max(s_c, axis=-1, keepdims=True)), sink)
    e_p, e_c, e_s = jnp.exp(s_p - mx), jnp.exp(s_c - mx), jnp.exp(sink - mx)
    total = jnp.sum(e_p, axis=-1, keepdims=True) + jnp.sum(e_c, axis=-1, keepdims=True) + e_s
    inv = 1.0 / total
    return e_p * inv, e_c * inv, e_s * inv, mx + jnp.log(total)


def _slope(h, n_heads):
    return float(2.0 ** (-8.0 * (h + 1) / n_heads))


def _swa_group(ref_vals, sink_ref, kh, n_heads):
    heads = [kh * ATT_G + g for g in range(ATT_G)]
    stacked = [jnp.concatenate([v[:, h * ATT_HD:(h + 1) * ATT_HD] for h in heads], axis=0) for v in ref_vals]
    grp = lax.shift_right_logical(lax.broadcasted_iota(jnp.int32, (ATT_G * WINDOW, 1), 0), WINDOW.bit_length() - 1)
    slope = jnp.zeros((ATT_G * WINDOW, 1), F32)
    sink = jnp.zeros((ATT_G * WINDOW, 1), F32)
    for g, h in enumerate(heads):
        slope = jnp.where(grp == g, _slope(h, n_heads), slope)
        sink = jnp.where(grp == g, sink_ref[:, h:h + 1], sink)
    return stacked, slope, sink


def _swa_fwd(q, kv, sinks):
    m, d = q.shape
    n_heads = d // ATT_HD
    kvh = n_heads // ATT_G
    kd = kvh * ATT_HD
    nb = m // WINDOW

    def body(q_ref, kvp_ref, kvc_ref, sink_ref, o_ref, lse_ref):
        has_prev = pl.program_id(0) > 0
        qv, kvp, kvc = q_ref[...], kvp_ref[...], kvc_ref[...]
        lane_h = lax.broadcasted_iota(jnp.int32, (WINDOW, n_heads), 1)
        outs, lse_all = [], jnp.zeros((WINDOW, n_heads), F32)
        for kh in range(kvh):
            ks = slice(kh * ATT_HD, (kh + 1) * ATT_HD)
            vs = slice(kd + kh * ATT_HD, kd + (kh + 1) * ATT_HD)
            (q4,), slope, sink = _swa_group([qv], sink_ref, kh, n_heads)
            p_p, p_c, _, lse = _swa_probs(q4, kvp[:, ks], kvc[:, ks], sink, slope, has_prev)
            o4 = _dot(p_p, kvp[:, vs]) + _dot(p_c, kvc[:, vs])
            for g in range(ATT_G):
                rows = slice(g * WINDOW, (g + 1) * WINDOW)
                outs.append(o4[rows, :])
                lse_all = jnp.where(lane_h == kh * ATT_G + g, lse[rows, :], lse_all)
        o_ref[...] = jnp.concatenate(outs, axis=-1).astype(o_ref.dtype)
        lse_ref[...] = lse_all

    return pl.pallas_call(
        body, name="swa_fwd", grid=(nb,),
        in_specs=[pl.BlockSpec((WINDOW, d), lambda n: (n, 0)),
                  pl.BlockSpec((WINDOW, 2 * kd), lambda n: (jnp.maximum(n - 1, 0), 0)),
                  pl.BlockSpec((WINDOW, 2 * kd), lambda n: (n, 0)),
                  pl.BlockSpec((1, n_heads), lambda n: (0, 0))],
        out_specs=[pl.BlockSpec((WINDOW, d), lambda n: (n, 0)), pl.BlockSpec((WINDOW, n_heads), lambda n: (n, 0))],
        out_shape=[jax.ShapeDtypeStruct((m, d), BF16), jax.ShapeDtypeStruct((m, n_heads), F32)],
        compiler_params=_params(("arbitrary",)),
    )(q, kv, kv, sinks)


def _swa_bwd(q, kv, sinks, lse, dao):
    m, d = q.shape
    n_heads = d // ATT_HD
    kvh = n_heads // ATT_G
    kd = kvh * ATT_HD
    nb = m // WINDOW
    scale = ATT_HD ** -0.5

    def body(q_ref, kvp_ref, kvc_ref, sink_ref, lse_ref, do_ref, dq_ref, dkvc_ref, dkvp_ref, dqsum_ref, dsink_ref):
        @pl.when(pl.program_id(0) == 0)
        def _():
            dqsum_ref[...] = jnp.zeros(dqsum_ref.shape, F32)
            dsink_ref[...] = jnp.zeros(dsink_ref.shape, F32)

        has_prev = pl.program_id(0) > 0
        qv, kvp, kvc, dov = q_ref[...], kvp_ref[...], kvc_ref[...], do_ref[...]
        lane_h = lax.broadcasted_iota(jnp.int32, (1, n_heads), 1)
        dsink = jnp.zeros((1, n_heads), F32)
        dq_parts, dk_p, dk_c, dv_p, dv_c = [], [], [], [], []
        for kh in range(kvh):
            ks = slice(kh * ATT_HD, (kh + 1) * ATT_HD)
            vs = slice(kd + kh * ATT_HD, kd + (kh + 1) * ATT_HD)
            kp, kc, vp, vc = kvp[:, ks], kvc[:, ks], kvp[:, vs], kvc[:, vs]
            (q4, do4), slope, sink = _swa_group([qv, dov], sink_ref, kh, n_heads)
            lse4 = jnp.concatenate([lse_ref[:, kh * ATT_G + g:kh * ATT_G + g + 1] for g in range(ATT_G)], axis=0)
            p_p, p_c, p_s, _ = _swa_probs(q4, kp, kc, sink, slope, has_prev, lse=lse4)
            dp_p, dp_c = _dot_nt(do4, vp), _dot_nt(do4, vc)
            delta = jnp.sum(p_p * dp_p, axis=-1, keepdims=True) + jnp.sum(p_c * dp_c, axis=-1, keepdims=True)
            ds_p, ds_c = p_p * (dp_p - delta), p_c * (dp_c - delta)
            sink_term = p_s * delta
            dq4 = (_dot(ds_p, kp) + _dot(ds_c, kc)) * scale
            for g in range(ATT_G):
                rows = slice(g * WINDOW, (g + 1) * WINDOW)
                dsink = dsink + jnp.where(lane_h == kh * ATT_G + g, -_col_sum(sink_term[rows, :]), 0.0)
                dq_parts.append(dq4[rows, :])
            dk_p.append(_dot_tn(ds_p, q4) * scale)
            dk_c.append(_dot_tn(ds_c, q4) * scale)
            dv_p.append(_dot_tn(p_p, do4))
            dv_c.append(_dot_tn(p_c, do4))
        dq = jnp.concatenate(dq_parts, axis=-1)
        dq_ref[...] = dq.astype(dq_ref.dtype)
        dqsum_ref[...] += _col_sum(dq)
        dsink_ref[...] += dsink
        dkvc_ref[...] = jnp.concatenate(dk_c + dv_c, axis=-1)
        dkvp_ref[...] = jnp.concatenate(dk_p + dv_p, axis=-1)

    return pl.pallas_call(
        body, name="swa_bwd", grid=(nb,),
        in_specs=[pl.BlockSpec((WINDOW, d), lambda n: (n, 0)),
                  pl.BlockSpec((WINDOW, 2 * kd), lambda n: (jnp.maximum(n - 1, 0), 0)),
                  pl.BlockSpec((WINDOW, 2 * kd), lambda n: (n, 0)),
                  pl.BlockSpec((1, n_heads), lambda n: (0, 0)),
                  pl.BlockSpec((WINDOW, n_heads), lambda n: (n, 0)),
                  pl.BlockSpec((WINDOW, d), lambda n: (n, 0))],
        out_specs=[pl.BlockSpec((WINDOW, d), lambda n: (n, 0)),
                   pl.BlockSpec((WINDOW, 2 * kd), lambda n: (n, 0)),
                   pl.BlockSpec((WINDOW, 2 * kd), lambda n: (n, 0)),
                   pl.BlockSpec((1, d), lambda n: (0, 0)),
                   pl.BlockSpec((1, n_heads), lambda n: (0, 0))],
        out_shape=[jax.ShapeDtypeStruct((m, d), BF16), jax.ShapeDtypeStruct((m, 2 * kd), F32),
                   jax.ShapeDtypeStruct((m, 2 * kd), F32), jax.ShapeDtypeStruct((1, d), F32),
                   jax.ShapeDtypeStruct((1, n_heads), F32)],
        compiler_params=_params(("arbitrary",)),
    )(q, kv, kv, sinks, lse, dao)


def _kv_grad_combine(dkv_cur, dkv_prev):
    m, w = dkv_cur.shape
    nb = m // WINDOW

    def body(cur_ref, nxt_ref, o_ref, sum_ref):
        @pl.when(pl.program_id(0) == 0)
        def _():
            sum_ref[...] = jnp.zeros(sum_ref.shape, F32)

        total = cur_ref[...] + jnp.where(pl.program_id(0) < nb - 1, nxt_ref[...], 0.0)
        o_ref[...] = total.astype(o_ref.dtype)
        sum_ref[...] += _col_sum(total)

    return pl.pallas_call(
        body, name="kv_grad_combine", grid=(nb,),
        in_specs=[pl.BlockSpec((WINDOW, w), lambda n: (n, 0)),
                  pl.BlockSpec((WINDOW, w), lambda n: (jnp.minimum(n + 1, nb - 1), 0))],
        out_specs=[pl.BlockSpec((WINDOW, w), lambda n: (n, 0)), pl.BlockSpec((1, w), lambda n: (0, 0))],
        out_shape=[jax.ShapeDtypeStruct((m, w), BF16), jax.ShapeDtypeStruct((1, w), F32)],
        compiler_params=_params(("arbitrary",)),
    )(dkv_cur, dkv_prev)


def _row(v):
    return v.reshape(1, -1)


def _local_step(x, p, target, wget, grad_sink, ln_gain, ln_bias, alb, norm_gain, kv_b, b_q, sinks, b_out, ple_b,
                small_sink=None):
    gs = {}
    gains = ln_gain.reshape(DEPTH * 3, -1)
    biases = ln_bias.reshape(DEPTH * 3, -1)
    sd = x.shape
    pending = [None]

    def mm(a, b, lb=0, **kw):
        after, pending[0] = pending[0], None
        return _mm(a, b, lb=lb, after=after, **kw)

    def mm_ln(a, wt, xin, i, j, nm, bias=None, pu=None):
        r = 3 * i + j
        if pu is None:
            fn, rows = (lambda h, xv, g, bv: (h,) + _ln_fwd_fn(xv, h, g[r:r + 1], bv[r:r + 1])), [xin]
        else:
            fn = lambda h, xv, puv, g, bv: (h,) + _ple_ln_fwd_fn(xv, h, puv, g[r:r + 1], bv[r:r + 1])
            rows = [xin, pu]
        h, y, yb = _mm(a, wt, lb=0, bias=bias, name=nm,
                       post=(fn, rows, [gains, biases], [(sd, F32), (sd, F32), (sd, BF16)], []))
        return h, (y, yb)

    def mm_ln_bwd(a, wt, add, xin, h, i, j, nm):
        r = 3 * i + j
        dx_part, dh, dg, db, dhsum = mm(a, wt, tb=True, add=add, name=nm,
                                        post=(lambda dy, xv, hv, g: _ln_bwd_fn(dy, xv, hv, g[r:r + 1]), [xin, h],
                                              [gains], [(sd, F32), (sd, BF16)], [((1, sd[1]), F32)] * 3))
        gs[f"ln_gain_{i}_{j}"], gs[f"ln_bias_{i}_{j}"] = dg, db
        return dx_part, dh, dhsum

    def tail_fwd(xa, i):
        wgu = wget("ffn_w_gate_up", i, xa[1])
        hid2 = wgu.shape[-1]
        gu, act = _mm(xa[1], wgu, lb=0, name=f"ffn_up_swiglu{i}", tile_cols=hid2 // 2,
                      post=(_swiglu_fwd_fn, [], [], [((sd[0], hid2), BF16), ((sd[0], hid2 // 2), BF16)], []))
        f, xb = mm_ln(act, wget("ffn_w_down", i, act), xa[0], i, 1, f"ffn_down_ln{i}")
        pu = _mm(p, wget("ple_w_up", i, act), la=i, lb=0, name=f"ple_up{i}")
        pg, xc = mm_ln(xb[1], wget("ple_w_gate", i, act), xb[0], i, 2, f"ple_gate_ln{i}", bias=_row(ple_b[i]), pu=pu)
        return dict(xa=xa, gu=gu, act=act, f=f, xb=xb, pg=pg, pu=pu), xc

    def tail_bwd(head, sv, i, mix_in, mix_h):
        xa, xb = sv["xa"], sv["xb"]
        r = 3 * i + 2
        dxb_part, dpg, dpu, dg2, db2, dbg = head(
            lambda dy, xv, pgv, puv, g: _ple_ln_bwd_fn(dy, xv, pgv, puv, g[r:r + 1]), [xb[0], sv["pg"], sv["pu"]],
            [gains], [(sd, F32), (sd, BF16), (sd, BF16)], [((1, sd[1]), F32)] * 3)[:6]
        gs[f"ple_b_{i}"] = dbg
        gs[f"ln_gain_{i}_2"], gs[f"ln_bias_{i}_2"] = dg2, db2
        grad_of("ple_w_gate", i, xb[1], dpg)
        grad_of("ple_w_up", i, p, dpu, la=i)
        dxa_part, df, _ = mm_ln_bwd(dpg, wget("ple_w_gate", i, None), dxb_part, xa[0], sv["f"], i, 1,
                                    f"ple_gate_dx_ln{i}")
        grad_of("ffn_w_down", i, sv["act"], df)
        gu = sv["gu"]
        dgu, = mm(df, wget("ffn_w_down", i, None), tb=True, name=f"ffn_down_dx_swiglu{i}", tile_cols=gu.shape[1] // 4,
                  post=(_swiglu_bwd_fn, [gu], [], [(gu.shape, BF16)], []))
        grad_of("ffn_w_gate_up", i, xa[1], dgu)
        return mm_ln_bwd(dgu, wget("ffn_w_gate_up", i, None), dxa_part, mix_in, mix_h, i, 0, f"ffn_up_dx_ln{i}")

    def grad_of(nm, i, act, dout, la=None, b_parts=None):
        grad = mm(act, dout, la=la, lb=None, ta=True, out_dtype=BF16, out_layers=1, out_layer=0,
                  name=f"grad_{nm}{i}", b_parts=b_parts)
        token = grad_sink(nm, i, grad)
        if token is not None:
            pending[0] = token

    proj = _mm(x, wget("a_w_in", 0, None), lb=0, name="hg_proj")
    o_pre, og, states, scores = _hgrn2_fwd(proj, alb, norm_gain, rb=HG_ROWS)
    h0, x1 = mm_ln(og, wget("a_w_out", 0, og), x, 0, 0, "hg_out_ln")
    sv0, x3 = tail_fwd(x1, 0)
    kv = _mm(x3[1], wget("kv_w", 0, x3[1]), lb=0, bias=_row(kv_b), out_dtype=BF16, name="kv_proj")
    q = _mm(x3[1], wget("b_w_q", 0, x3[1]), lb=0, bias=b_q, out_dtype=BF16, name="q_proj")
    ao, lse = _swa_fwd(q, kv, sinks)
    h1, x4 = mm_ln(ao, wget("b_w_out", 0, x3[1]), x3[0], 1, 0, "att_out_ln", bias=b_out)
    sv1, y = tail_fwd(x4, 1)

    loss_box = []

    def loss_head(fn, rows, whole, outs, sums):
        def with_loss(yv, tv, *rest):
            dy, part = _loss_fn(yv, tv)
            return fn(dy, *rest) + (part,)

        res = _rowwise(with_loss, [y[0], target] + rows, whole, outs, list(sums) + [((1, LANES), F32)],
                       name="loss_ln_ple_bwd1")
        loss_box.append(res[-1])
        return res

    dx3_part, dh1, dh1sum = tail_bwd(loss_head, sv1, 1, x3[0], h1)
    loss = loss_box[0]
    gs["b_out"] = dh1sum
    grad_of("b_w_out", 0, ao, dh1)
    dao = mm(dh1, wget("b_w_out", 0, None), tb=True, out_dtype=BF16, name="att_out_dx")
    dq, dkv_cur, dkv_prev, dqsum, dsinks = _swa_bwd(q, kv, sinks, lse, dao)
    gs["b_q"], gs["sinks"] = dqsum, dsinks
    dkv, dkvsum = _kv_grad_combine(dkv_cur, dkv_prev)
    gs["kv_b"] = dkvsum
    grad_of("b_w_q", 0, x3[1], dq)
    grad_of("kv_w", 0, x3[1], dkv)
    dx3 = mm(dq, wget("b_w_q", 0, None), tb=True, add=dx3_part, name="q_proj_dx")

    def kv_head(*post):
        return mm(dkv, wget("kv_w", 0, None), tb=True, add=dx3, name="kv_proj_dx_ln_ple_bwd0", post=post)

    dx_part, dh0, _ = tail_bwd(kv_head, sv0, 0, x, h0)
    grad_of("a_w_out", 0, og, dh0)
    dog = mm(dh0, wget("a_w_out", 0, None), tb=True, name="hg_out_dx")
    dproj, dalb, dgain = _hgrn2_bwd(proj, o_pre, states, scores, dog, alb, norm_gain, rb=HG_ROWS)
    gs["alb"], gs["norm_gain"] = dalb, dgain
    if small_sink is not None:
        pending[0] = small_sink(loss, gs)
    grad_of("a_w_in", 0, x, dproj, b_parts=4)
    grad_x = mm(dproj, wget("a_w_in", 0, None), tb=True, add=dx_part, name="hg_proj_dx", a_parts=4)
    return loss, grad_x, gs


HBM_SPEC = pl.BlockSpec(memory_space=pl.ANY)
HBM_ONLY = pl.BlockSpec(memory_space=pltpu.HBM)
SEM_SPEC = pl.BlockSpec(memory_space=pltpu.SEMAPHORE)
SIDE_EFFECT = pltpu.SideEffectType.DATAFLOW_SIDE_EFFECTING


def _slot(kind, j):
    return (j % 2) * 2 + j // 2 if kind == "colp" else j


def _piece(ref, kind, j):
    _, r, c = ref.shape
    if kind == "row":
        return ref.at[:, pl.ds(j * (r // N_CHIPS), r // N_CHIPS), :]
    return ref.at[:, :, pl.ds(_slot(kind, j) * (c // N_CHIPS), c // N_CHIPS)]


def _piece_dyn(ref, kind, j):
    _, r, c = ref.shape
    if kind == "row":
        return ref.at[:, pl.ds(pl.multiple_of(j * (r // N_CHIPS), 16), r // N_CHIPS), :]
    return ref.at[:, :, pl.ds(pl.multiple_of(_slot(kind, j) * (c // N_CHIPS), LANES), c // N_CHIPS)]


def _chip_of(j, c):
    return (j // 2, j % 2, c)


def _in_hbm(a):
    return pltpu.with_memory_space_constraint(a, pltpu.HBM)


PLACE_STEPS = 4


def _place(items, chip, *, name, after=None):
    n = len(items)
    in_specs, out_specs, out_shapes, blocks = [], [], [], []
    for src, layer, kind, out_dtype in items:
        _, r, c = src.shape
        nb = max(k for k in (1, 2, PLACE_STEPS) if r % (16 * k) == 0 or k == 1)
        blocks.append(nb)

        def src_idx(i, chip_ref, layer=layer, nb=nb):
            return (layer, jnp.minimum(i, nb - 1), 0)

        def full_idx(i, chip_ref, kind=kind, nb=nb):
            ib = jnp.minimum(i, nb - 1)
            return (0, chip_ref[0] * nb + ib, 0) if kind == "row" else (0, ib, _slot(kind, chip_ref[0]))

        in_specs.append(pl.BlockSpec((None, r // nb, c), src_idx))
        out_specs.append(pl.BlockSpec((None, r // nb, c), full_idx))
        out_shapes.append(jax.ShapeDtypeStruct((1, r * N_CHIPS, c) if kind == "row" else (1, r, c * N_CHIPS),
                                               out_dtype))
    operands = [it[0] for it in items]
    if after is not None:
        in_specs.append(HBM_SPEC)
        operands.append(after)

    def body(chip_ref, *refs):
        for a in range(n):
            refs[len(refs) - n + a][...] = refs[a][...].astype(refs[len(refs) - n + a].dtype)

    return pl.pallas_call(
        body, name=name,
        grid_spec=pltpu.PrefetchScalarGridSpec(num_scalar_prefetch=1, grid=(PLACE_STEPS,), in_specs=in_specs,
                                               out_specs=out_specs),
        out_shape=out_shapes,
        compiler_params=_params(("arbitrary",)),
    )(chip, *operands)


def _half(ref, c):
    h = ref.shape[1] // 2
    start = c * h if isinstance(c, int) else pl.multiple_of(c * h, 16)
    return ref.at[:, pl.ds(start, h), :]


def _sibling_handshake():
    barrier = pltpu.get_barrier_semaphore()
    sibling = (lax.axis_index("x"), lax.axis_index("y"), 1 - lax.axis_index("c"))
    pl.semaphore_signal(barrier, inc=1, device_id=sibling, device_id_type=MESH)
    pl.semaphore_wait(barrier, 1)


class _SiblingFill:
    def __init__(self, lands, kinds, name, collective_id):
        self.kinds, self.name, self.n = kinds, name, len(lands)
        n = self.n
        sem_shape = pltpu.SemaphoreType.DMA((n * N_CHIPS,))

        def body(*refs):
            land_refs, send_sems, recv_sems, token = refs[:n], refs[n], refs[n + 1], refs[-1]
            _sibling_handshake()
            for cp in self._copies(land_refs, send_sems, recv_sems):
                cp.start()
            token[...] = jnp.zeros(token.shape, token.dtype)

        outs = pl.pallas_call(
            body, name=name + "_start",
            in_specs=[HBM_ONLY] * n,
            out_specs=[SEM_SPEC, SEM_SPEC] + [HBM_ONLY] * n + [pl.BlockSpec(memory_space=pltpu.VMEM)],
            out_shape=[sem_shape, sem_shape] + [pltpu.HBM(a.shape, a.dtype) for a in lands]
                      + [jax.ShapeDtypeStruct((8, LANES), F32)],
            input_output_aliases={i: i + 2 for i in range(n)},
            compiler_params=pltpu.CompilerParams(has_side_effects=SIDE_EFFECT, collective_id=collective_id),
        )(*[_in_hbm(a) for a in lands])
        self.send_sems, self.recv_sems, self.lands, self.token = outs[0], outs[1], list(outs[2:2 + n]), outs[-1]

    def _copies(self, land_refs, send_sems, recv_sems):
        x, y, c = lax.axis_index("x"), lax.axis_index("y"), lax.axis_index("c")
        me = 2 * x + y
        copies = []
        for a in range(self.n):
            for k in range(1, N_CHIPS):
                t = (me + k) % N_CHIPS
                slice_t = _piece_dyn(land_refs[a], self.kinds[a], t)
                got = _half(slice_t, c)
                copies.append(pltpu.make_async_remote_copy(
                    src_ref=got, dst_ref=got, send_sem=send_sems.at[a * N_CHIPS + k],
                    recv_sem=recv_sems.at[a * N_CHIPS + k], device_id=(x, y, 1 - c), device_id_type=MESH))
        return copies

    def wait(self, after):
        n = self.n

        def body(*refs):
            land_refs, send_sems, recv_sems = refs[:n], refs[n], refs[n + 1]
            for cp in self._copies(land_refs, send_sems, recv_sems):
                cp.wait_send()
                cp.wait_recv()

        operands = [_in_hbm(a) for a in self.lands] + [self.send_sems, self.recv_sems]
        in_specs = [HBM_ONLY] * n + [SEM_SPEC, SEM_SPEC]
        if after is not None:
            operands.append(after)
            in_specs.append(HBM_SPEC)
        outs = pl.pallas_call(
            body, name=self.name + "_wait",
            in_specs=in_specs, out_specs=[HBM_ONLY] * n,
            out_shape=[pltpu.HBM(a.shape, a.dtype) for a in self.lands],
            input_output_aliases={i: i for i in range(n)},
            compiler_params=pltpu.CompilerParams(has_side_effects=SIDE_EFFECT),
        )(*operands)
        return list(outs)


class _Exchange:
    def __init__(self, mode, srcs, lands, kinds, layers, name, collective_id, after=None, halves=None):
        self.mode, self.kinds, self.layers, self.name, self.n = mode, kinds, layers, name, len(lands)
        self.halves = halves if halves is not None else [False] * len(lands)
        n, ns = self.n, len(srcs)
        n_in = ns + n + (after is not None)
        sem_shape = pltpu.SemaphoreType.DMA((n * N_CHIPS,))

        def body(*refs):
            src_refs, land_refs = refs[:ns], refs[ns:ns + n]
            send_sems, recv_sems = refs[n_in], refs[n_in + 1]
            token = refs[-1]
            c = lax.axis_index("c")
            me = 2 * lax.axis_index("x") + lax.axis_index("y")
            barrier = pltpu.get_barrier_semaphore()
            for k in range(1, N_CHIPS):
                t = (me + k) % N_CHIPS
                pl.semaphore_signal(barrier, inc=1, device_id=(t // 2, t % 2, c), device_id_type=MESH)
            pl.semaphore_wait(barrier, N_CHIPS - 1)
            for j in range(N_CHIPS):
                @pl.when(me == j)
                def _():
                    for a in range(n):
                        for t in range(N_CHIPS):
                            if t != j:
                                src, dst = self._ends(src_refs, land_refs, a, j, t, c)
                                pltpu.make_async_remote_copy(
                                    src_ref=src, dst_ref=dst, send_sem=send_sems.at[a * N_CHIPS + t],
                                    recv_sem=recv_sems.at[a * N_CHIPS + j],
                                    device_id=_chip_of(t, c), device_id_type=MESH).start()
            token[...] = jnp.zeros(token.shape, token.dtype)

        arrays = list(srcs) + list(lands)
        operands = [_in_hbm(a) for a in arrays]
        in_specs = [HBM_ONLY] * (ns + n)
        if after is not None:
            operands.append(after)
            in_specs.append(HBM_SPEC)
        outs = pl.pallas_call(
            body, name=name + "_start",
            in_specs=in_specs,
            out_specs=[SEM_SPEC, SEM_SPEC] + [HBM_ONLY] * (ns + n) + [pl.BlockSpec(memory_space=pltpu.VMEM)],
            out_shape=[sem_shape, sem_shape] + [pltpu.HBM(a.shape, a.dtype) for a in arrays]
                      + [jax.ShapeDtypeStruct((8, LANES), F32)],
            input_output_aliases={i: i + 2 for i in range(ns + n)},
            compiler_params=pltpu.CompilerParams(has_side_effects=SIDE_EFFECT, collective_id=collective_id),
        )(*operands)
        self.send_sems, self.recv_sems = outs[0], outs[1]
        self.srcs, self.lands = list(outs[2:2 + ns]), list(outs[2 + ns:2 + ns + n])
        self.token = outs[-1]

    def _ends(self, src_refs, land_refs, a, me_j, peer, c):
        if self.mode == "gather":
            mine = _piece(land_refs[a], self.kinds[a], me_j)
            if self.halves[a]:
                mine = _half(mine, c)
            return mine, mine
        return _piece(src_refs[a], self.kinds[a], peer), land_refs[a].at[me_j, pl.ds(self.layers[a], 1)]

    def wait(self, after, lands=None):
        n, ns = self.n, len(self.srcs)
        lands = self.lands if lands is None else lands

        def body(*refs):
            src_refs, land_refs = refs[:ns], refs[ns:ns + n]
            send_sems, recv_sems = refs[ns + n], refs[ns + n + 1]
            c = lax.axis_index("c")
            me = 2 * lax.axis_index("x") + lax.axis_index("y")
            for j in range(N_CHIPS):
                @pl.when(me != j)
                def _():
                    for a in range(n):
                        sent, _ = self._ends(src_refs, land_refs, a, 0, j, c)
                        _, landed = self._ends(src_refs, land_refs, a, j, 0, c)
                        cp = pltpu.make_async_remote_copy(
                            src_ref=sent, dst_ref=landed, send_sem=send_sems.at[a * N_CHIPS + j],
                            recv_sem=recv_sems.at[a * N_CHIPS + j],
                            device_id=_chip_of(j, c), device_id_type=MESH)
                        cp.wait_send()
                        cp.wait_recv()

        arrays = self.srcs + list(lands)
        operands = [_in_hbm(a) for a in arrays] + [self.send_sems, self.recv_sems]
        in_specs = [HBM_ONLY] * (ns + n) + [SEM_SPEC, SEM_SPEC]
        if after is not None:
            operands.append(after)
            in_specs.append(HBM_SPEC)
        outs = pl.pallas_call(
            body, name=self.name + "_wait",
            in_specs=in_specs, out_specs=[HBM_ONLY] * (ns + n),
            out_shape=[pltpu.HBM(a.shape, a.dtype) for a in arrays],
            input_output_aliases={i: i for i in range(ns + n)},
            compiler_params=pltpu.CompilerParams(has_side_effects=SIDE_EFFECT),
        )(*operands)
        return list(outs[:ns]), list(outs[ns:])


def _sum_arrivals(zone, own_grads, kind, chip, name, after=None):
    _, layers, r, c = zone.shape
    tm = _pick_rows(r, 256)
    nb = r // tm

    def own_idx(l, i, chip_ref):
        return (0, chip_ref[0] * nb + i, 0) if kind == "row" else (0, i, _slot(kind, chip_ref[0]))

    def slot_idx(k):
        return lambda l, i, chip_ref: (jnp.where(chip_ref[0] == k, (k + 1) % N_CHIPS, k), l, i, 0)

    in_specs = [pl.BlockSpec((None, None, tm, c), slot_idx(k)) for k in range(N_CHIPS)]
    in_specs += [pl.BlockSpec((None, tm, c), own_idx) for _ in own_grads]
    operands = [zone] * N_CHIPS + list(own_grads)
    if after is not None:
        in_specs.append(HBM_SPEC)
        operands.append(after)

    def body(chip_ref, *refs):
        slot_refs, own_refs, o_ref = refs[:N_CHIPS], refs[N_CHIPS:N_CHIPS + layers], refs[-1]
        own = own_refs[0][...]
        for u in range(1, layers):
            own = jnp.where(pl.program_id(0) == u, own_refs[u][...], own)
        acc = None
        for k in range(N_CHIPS):
            term = jnp.where(chip_ref[0] == k, own, slot_refs[k][...]).astype(F32)
            acc = term if acc is None else acc + term
        o_ref[...] = acc.astype(o_ref.dtype)

    return pl.pallas_call(
        body, name=name,
        grid_spec=pltpu.PrefetchScalarGridSpec(
            num_scalar_prefetch=1, grid=(layers, nb), in_specs=in_specs,
            out_specs=pl.BlockSpec((tm, c), lambda l, i, chip_ref: (l * nb + i, 0))),
        out_shape=jax.ShapeDtypeStruct((layers * r, c), BF16),
        compiler_params=_params(("arbitrary", "arbitrary")),
    )(chip, *operands)


class _SiblingSwap:
    def __init__(self, arrays, name, collective_id, after=None):
        self.name, self.n = name, len(arrays)
        n = self.n
        n_in = n + (after is not None)
        sem_shape = pltpu.SemaphoreType.DMA((n,))

        def body(*refs):
            ins, send_sems, recv_sems = refs[:n], refs[n_in], refs[n_in + 1]
            theirs, token = refs[n_in + 2 + n:n_in + 2 + 2 * n], refs[-1]
            _sibling_handshake()
            for cp in self._copies(ins, theirs, send_sems, recv_sems):
                cp.start()
            token[...] = jnp.zeros(token.shape, token.dtype)

        operands, in_specs = [_in_hbm(a) for a in arrays], [HBM_ONLY] * n
        if after is not None:
            operands.append(after)
            in_specs.append(HBM_SPEC)
        outs = pl.pallas_call(
            body, name=name + "_start",
            in_specs=in_specs,
            out_specs=[SEM_SPEC, SEM_SPEC] + [HBM_ONLY] * (2 * n) + [pl.BlockSpec(memory_space=pltpu.VMEM)],
            out_shape=[sem_shape, sem_shape] + [pltpu.HBM(a.shape, a.dtype) for a in arrays] * 2
                      + [jax.ShapeDtypeStruct((8, LANES), F32)],
            input_output_aliases={i: i + 2 for i in range(n)},
            compiler_params=pltpu.CompilerParams(has_side_effects=SIDE_EFFECT, collective_id=collective_id),
        )(*operands)
        self.send_sems, self.recv_sems = outs[0], outs[1]
        self.mine, self.theirs, self.token = list(outs[2:2 + n]), list(outs[2 + n:2 + 2 * n]), outs[-1]

    def _copies(self, mine, theirs, send_sems, recv_sems):
        sibling = (lax.axis_index("x"), lax.axis_index("y"), 1 - lax.axis_index("c"))
        return [pltpu.make_async_remote_copy(src_ref=mine[a], dst_ref=theirs[a], send_sem=send_sems.at[a],
                                             recv_sem=recv_sems.at[a], device_id=sibling, device_id_type=MESH)
                for a in range(self.n)]

    def wait(self, after):
        n = self.n

        def body(*refs):
            for cp in self._copies(refs[:n], refs[n:2 * n], refs[2 * n], refs[2 * n + 1]):
                cp.wait_send()
                cp.wait_recv()

        arrays = self.mine + self.theirs
        outs = pl.pallas_call(
            body, name=self.name + "_wait",
            in_specs=[HBM_ONLY] * (2 * n) + [SEM_SPEC, SEM_SPEC, HBM_SPEC], out_specs=[HBM_ONLY] * (2 * n),
            out_shape=[pltpu.HBM(a.shape, a.dtype) for a in arrays],
            input_output_aliases={i: i for i in range(2 * n)},
            compiler_params=pltpu.CompilerParams(has_side_effects=SIDE_EFFECT),
        )(*[_in_hbm(a) for a in arrays], self.send_sems, self.recv_sems, after)
        return list(outs[:n]), list(outs[n:])


class _GatherDevices:
    def __init__(self, vec):
        sem_shape = pltpu.SemaphoreType.DMA((N_DEV,))

        def body(in_ref, send_sems, recv_sems, vec_ref, out_ref, token):
            for cp in self._copies(in_ref, out_ref, send_sems, recv_sems):
                cp.start()
            token[...] = jnp.zeros(token.shape, token.dtype)

        outs = pl.pallas_call(
            body, name="gather_small_start",
            in_specs=[HBM_ONLY],
            out_specs=[SEM_SPEC, SEM_SPEC, HBM_ONLY, HBM_ONLY, pl.BlockSpec(memory_space=pltpu.VMEM)],
            out_shape=[sem_shape, sem_shape, pltpu.HBM(vec.shape, vec.dtype),
                       pltpu.HBM((N_DEV,) + vec.shape, vec.dtype), jax.ShapeDtypeStruct((8, LANES), F32)],
            input_output_aliases={0: 2},
            compiler_params=pltpu.CompilerParams(has_side_effects=SIDE_EFFECT),
        )(_in_hbm(vec))
        self.send_sems, self.recv_sems, self.vec, self.rows, self.token = outs

    def _copies(self, in_ref, out_ref, send_sems, recv_sems):
        x, y, c = lax.axis_index("x"), lax.axis_index("y"), lax.axis_index("c")
        me = 4 * x + 2 * y + c
        copies = [pltpu.make_async_copy(in_ref, out_ref.at[me], recv_sems.at[0])]
        for rel in range(1, N_DEV):
            peer = (x ^ (rel >> 2), y ^ ((rel >> 1) & 1), c ^ (rel & 1))
            copies.append(pltpu.make_async_remote_copy(
                src_ref=in_ref, dst_ref=out_ref.at[me], send_sem=send_sems.at[rel], recv_sem=recv_sems.at[rel],
                device_id=peer, device_id_type=MESH))
        return copies

    def wait(self, after):
        def body(vec_ref, rows_ref, send_sems, recv_sems, after_ref, vec_out, rows_out):
            copies = self._copies(vec_ref, rows_ref, send_sems, recv_sems)
            copies[0].wait()
            for cp in copies[1:]:
                cp.wait_send()
                cp.wait_recv()

        outs = pl.pallas_call(
            body, name="gather_small_wait",
            in_specs=[HBM_ONLY, HBM_ONLY, SEM_SPEC, SEM_SPEC, HBM_SPEC], out_specs=[HBM_ONLY, HBM_ONLY],
            out_shape=[pltpu.HBM(self.vec.shape, self.vec.dtype), pltpu.HBM(self.rows.shape, self.rows.dtype)],
            input_output_aliases={0: 0, 1: 1},
            compiler_params=pltpu.CompilerParams(has_side_effects=SIDE_EFFECT),
        )(_in_hbm(self.vec), _in_hbm(self.rows), self.send_sems, self.recv_sems, after)
        return outs[1]


BIG = [("a_w_in", "col"), ("a_w_out", "row"), ("kv_w", "row"), ("b_w_q", "row"), ("b_w_out", "row"),
       ("ffn_w_gate_up", "colp"), ("ffn_w_down", "row"), ("ple_w_up", "col"), ("ple_w_gate", "row")]
GATHER_GROUPS = [[("a_w_in", 0), ("small", 0)], [("a_w_out", 0)],
                 [("ffn_w_gate_up", 0), ("ffn_w_down", 0), ("ple_w_gate", 0), ("ple_w_up", 0)],
                 [("kv_w", 0), ("b_w_q", 0), ("b_w_out", 0)],
                 [("ffn_w_gate_up", 1)], [("ffn_w_down", 1), ("ple_w_gate", 1), ("ple_w_up", 1)]]
SCATTER_GROUPS = [[("ple_w_gate", 1), ("ple_w_up", 1), ("ffn_w_down", 1)], [("ffn_w_gate_up", 1)],
                  [("b_w_out", 0), ("b_w_q", 0), ("kv_w", 0)], [("ple_w_gate", 0), ("ple_w_up", 0), ("ffn_w_down", 0)],
                  [("ffn_w_gate_up", 0), ("a_w_out", 0)], [("a_w_in", 0)]]
COLLECTIVE_IDS = {"fill": 0, "swap": 6, "gather": 9, "scatter": 15}
SMALL_SHARDED = ["ln_gain", "ln_bias", "a_lower_bound"]
SMALL_REPLICATED = ["a_norm_gain", "kv_b", "b_b_q", "b_sinks", "b_b_out", "ple_b_gate"]
WEIGHT_ORDER = ["a_w_in", "a_lower_bound", "a_norm_gain", "a_w_out", "kv_w", "kv_b", "b_w_q", "b_b_q", "b_sinks",
                "b_w_out", "b_b_out", "ffn_w_gate_up", "ffn_w_down", "ple_w_up", "ple_w_gate", "ple_b_gate",
                "ln_gain", "ln_bias"]


def _as3(a):
    return a.reshape((-1,) + a.shape[-2:]) if a.ndim >= 3 else a.reshape((1,) + a.shape)


def _pad_lanes(v):
    n = v.shape[-1]
    return jnp.pad(v, ((0, 0), (0, (-n) % LANES)))


def _adam_small(everyone, chip, items, loss_off):
    n_items = len(items)

    def body(chip_ref, every_ref, *refs):
        ins, outs = refs[:3 * n_items], refs[3 * n_items:]

        def total(off, width):
            acc = every_ref[0, :, off:off + width]
            for s in range(1, N_DEV):
                acc = acc + every_ref[s, :, off:off + width]
            return acc

        for a, (w, _, _, off, sharded) in enumerate(items):
            cols = w.shape[-1]
            for r in range(w.size // cols):
                at = (slice(r, r + 1),) if w.ndim == 2 else (r // w.shape[1], slice(r % w.shape[1], r % w.shape[1] + 1))
                if sharded:
                    full = total(off + r * N_CHIPS * cols, N_CHIPS * cols)
                    g = full[:, 0:cols]
                    for c in range(1, N_CHIPS):
                        g = jnp.where(chip_ref[0] == c, full[:, c * cols:(c + 1) * cols], g)
                else:
                    g = total(off + r * cols, cols)
                w_ref, m_ref, v_ref = ins[3 * a:3 * a + 3]
                res = _adam_fn(w_ref[at], m_ref[at], v_ref[at], g, jnp.zeros_like(g))
                for out_ref, val in zip(outs[4 * a:4 * a + 4], res):
                    out_ref[at] = val
        outs[-1][...] = total(loss_off, LANES)

    def whole(shape):
        return pl.BlockSpec(tuple(shape), lambda i, chip_ref: (0,) * len(shape))

    arrays = [arr for it in items for arr in it[:3]]
    out_shapes = [jax.ShapeDtypeStruct(it[0].shape, F32) for it in items for _ in range(4)]
    out_shapes.append(jax.ShapeDtypeStruct((1, LANES), F32))
    result = pl.pallas_call(
        body, name="adam_small",
        grid_spec=pltpu.PrefetchScalarGridSpec(
            num_scalar_prefetch=1, grid=(1,),
            in_specs=[whole(everyone.shape)] + [whole(arr.shape) for arr in arrays],
            out_specs=[whole(s.shape) for s in out_shapes]),
        out_shape=out_shapes,
        compiler_params=_params(("arbitrary",)),
    )(chip, everyone, *arrays)
    return [result[4 * a:4 * a + 4] for a in range(n_items)], result[-1]


def kernel(x, p, a_w_in, a_lower_bound, a_norm_gain, a_w_out, kv_w, kv_b, b_w_q, b_b_q, b_sinks, b_w_out, b_b_out, ffn_w_gate_up, ffn_w_down, ple_w_up, ple_w_gate, ple_b_gate, ln_gain, ln_bias, loss_target, m_a_w_in, m_a_lower_bound, m_a_norm_gain, m_a_w_out, m_kv_w, m_kv_b, m_b_w_q, m_b_b_q, m_b_sinks, m_b_w_out, m_b_b_out, m_ffn_w_gate_up, m_ffn_w_down, m_ple_w_up, m_ple_w_gate, m_ple_b_gate, m_ln_gain, m_ln_bias, v_a_w_in, v_a_lower_bound, v_a_norm_gain, v_a_w_out, v_kv_w, v_kv_b, v_b_w_q, v_b_b_q, v_b_sinks, v_b_w_out, v_b_b_out, v_ffn_w_gate_up, v_ffn_w_down, v_ple_w_up, v_ple_w_gate, v_ple_b_gate, v_ln_gain, v_ln_bias):
    args = dict(locals())
    wts = {n: args[n] for n in WEIGHT_ORDER}
    mom = {n: args["m_" + n] for n in WEIGHT_ORDER}
    vel = {n: args["v_" + n] for n in WEIGHT_ORDER}
    chip = 2 * lax.axis_index("x") + lax.axis_index("y")
    d = x.shape[-1]
    dq = d // N_CHIPS

    kind_of = dict(BIG)
    kind_of["small"] = "col"
    chip_arr = chip.reshape(1).astype(jnp.int32)
    small_pack = jnp.concatenate([wts[n].reshape(-1, dq) for n in SMALL_SHARDED], axis=0)[None]

    def place_item(key):
        n, layer = key
        if n == "small":
            return small_pack, 0, "col", F32
        return _as3(wts[n]), layer, kind_of[n], BF16

    gathers, where = [], {}
    for gi, group in enumerate(GATHER_GROUPS):
        prev = gathers[-1].token if gathers else None
        placed = _place([place_item(k) for k in group], chip_arr, name=f"place{gi}", after=prev)
        gathers.append(_Exchange("gather", [], placed, [kind_of[k[0]] for k in group],
                                 [0] * len(group), f"gather{gi}", COLLECTIVE_IDS["gather"] + gi, after=prev,
                                 halves=[k[0] != "small" for k in group]))
        for k in group:
            where[k] = gi
    all_started = gathers[-1].token
    ready = {}

    fills = {}

    def pass_on(gi, after):
        if gi not in fills:
            group = GATHER_GROUPS[gi]
            outs = gathers[gi].wait(after)[1]
            split = [i for i, k in enumerate(group) if k[0] != "small"]
            fills[gi] = (outs, split, _SiblingFill([outs[i] for i in split], [kind_of[group[i][0]] for i in split],
                                                   f"fill{gi}", COLLECTIVE_IDS["fill"] + gi))

    def wget(name, layer, after):
        key = (name, layer)
        if key not in ready:
            gi = where[key]
            after = all_started if gi == 0 else after
            pass_on(gi, after)
            if 1 <= gi < len(GATHER_GROUPS) - 1:
                pass_on(gi + 1, after)
                after = fills[gi + 1][2].token
            outs, split, fill = fills[gi]
            for i, arr in zip(split, fill.wait(after)):
                outs[i] = arr
            for k, arr in zip(GATHER_GROUPS[gi], outs):
                ready[k] = arr
        return ready[key]

    small_full = wget("small", 0, None)[0]
    ln_gain_f = small_full[0:6].reshape(DEPTH, 3, d)
    ln_bias_f = small_full[6:12].reshape(DEPTH, 3, d)
    alb_f = small_full[12:14]

    group_of = {k: gi for gi, group in enumerate(SCATTER_GROUPS) for k in group}
    grads_done, zones, scatters = {}, {}, []

    def grad_sink(name, layer, grad):
        grads_done[(name, layer)] = grad
        if name not in zones:
            zones[name] = lax.empty((N_CHIPS,) + _as3(wts[name]).shape, BF16)
        gi = group_of[(name, layer)]
        group = SCATTER_GROUPS[gi]
        if not all(k in grads_done for k in group):
            return None
        ex = _Exchange("scatter", [grads_done[k] for k in group], [zones[k[0]] for k in group],
                       [kind_of[k[0]] for k in group], [k[1] for k in group], f"scatter{gi}",
                       COLLECTIVE_IDS["scatter"] + gi)
        for k, zone in zip(group, ex.lands):
            zones[k[0]] = zone
        scatters.append((ex, group))
        return ex.token

    small = {}

    def small_sink(loss, gs):
        ln_g = jnp.concatenate([gs[f"ln_gain_{i}_{j}"] for i in range(DEPTH) for j in range(3)], axis=0)
        ln_b = jnp.concatenate([gs[f"ln_bias_{i}_{j}"] for i in range(DEPTH) for j in range(3)], axis=0)
        ple_bg = jnp.concatenate([gs[f"ple_b_{i}"] for i in range(DEPTH)], axis=0)
        small["list"] = [ln_g.reshape(1, -1), ln_b.reshape(1, -1), gs["alb"].reshape(1, -1), gs["norm_gain"],
                         gs["kv_b"], gs["b_q"], _pad_lanes(gs["sinks"]), gs["b_out"], ple_bg.reshape(1, -1), loss]
        small["gather"] = _GatherDevices(jnp.concatenate(small["list"], axis=1))
        return small["gather"].token

    loss, grad_x, gs = _local_step(
        x[0], p.reshape((p.shape[0],) + p.shape[2:]), loss_target[0], wget, grad_sink, ln_gain_f, ln_bias_f, alb_f, a_norm_gain, kv_b, b_b_q,
        b_sinks, b_b_out, ple_b_gate, small_sink)

    res = {}

    def arrive(batch, after):
        for ex, group in batch:
            srcs, outs = ex.wait(after, lands=[zones[k[0]] for k in group])
            for k, grad, zone in zip(group, srcs, outs):
                grads_done[k], zones[k[0]] = grad, zone

    def half_sums(names, batch, after):
        partial = []
        for n in names:
            own = [grads_done[(n, layer)] for layer in range(zones[n].shape[1])]
            partial.append(_sum_arrivals(zones[n], own, kind_of[n], chip_arr, f"sum_{n}", after=after))
        return _SiblingSwap(partial, f"swap{batch}", COLLECTIVE_IDS["swap"] + batch, after=after)

    def update(names, swap, after):
        for n, own, sib in zip(names, *swap.wait(after)):
            shp = wts[n].shape
            flat = lambda a: a.reshape(-1, shp[-1])
            out = _rowwise(_adam_fn, [flat(wts[n]), flat(mom[n]), flat(vel[n]), own, sib], [],
                           [(own.shape, F32)] * 4, name=f"adam_{n}")
            res[n] = [o.reshape(shp) for o in out]
        return res[names[-1]][1]

    last_names = [k[0] for k in SCATTER_GROUPS[-1]]
    batches = [["ffn_w_gate_up"], [n for n, _ in BIG if n != "ffn_w_gate_up" and n not in last_names], last_names]
    arrive(scatters[:-1], grad_x)
    swap0 = half_sums(batches[0], 0, None)
    swap1 = half_sums(batches[1], 1, swap0.token)
    updated = update(batches[0], swap0, swap1.token)
    arrive(scatters[-1:], updated)
    swap2 = half_sums(batches[2], 2, swap1.token)
    updated = update(batches[1], swap1, swap2.token)
    update(batches[2], swap2, updated)

    everyone = small["gather"].wait(grad_x)
    offs, pos = [], 0
    for v in small["list"]:
        offs.append(pos)
        pos += v.shape[1]
    names = ["ln_gain", "ln_bias", "a_lower_bound", "a_norm_gain", "kv_b", "b_b_q", "b_sinks", "b_b_out", "ple_b_gate"]
    as_rows = lambda a: a.reshape(1, -1) if a.ndim == 1 else a
    items = [(as_rows(wts[n]), as_rows(mom[n]), as_rows(vel[n]), off, n in SMALL_SHARDED)
             for n, off in zip(names, offs)]
    updates, loss_row = _adam_small(everyone, chip_arr, items, offs[len(names)])
    for n, upd in zip(names, updates):
        res[n] = [u.reshape(wts[n].shape) for u in upd]

    outs = [loss_row[0, 0], grad_x[None]]
    for k in range(4):
        outs += [res[n][k] for n in WEIGHT_ORDER]
    return tuple(outs)
```

```python
import functools

import jax
import jax.numpy as jnp
from jax import lax
from jax.experimental import pallas as pl
from jax.experimental.pallas import tpu as pltpu

F32 = jnp.float32
BF16 = jnp.bfloat16
MESH = pl.DeviceIdType.MESH

LANES = 128
HG_DK = 128
HG_CHUNK = 64
HG_SUB = 16
HG_ROWS = 512
HG_HEADS_PER_STEP = 2
LOG2_E = 1.4426950408889634
ATT_HD = 64
ATT_G = 4
WINDOW = 128
DEPTH = 2
ALPHA = (2.0 * DEPTH) ** 0.25
LN_EPS = 1e-5
RMS_EPS = 1e-6
ADAM_LR, ADAM_B1, ADAM_B2, ADAM_EPS, ADAM_WD, ADAM_STEP = 0.001, 0.9, 0.999, 1e-08, 0.01, 10
N_CHIPS = 4
N_DEV = 8
VMEM_LIMIT = 56 * 1024 * 1024
NEG = -1e30


def _pick(n, cap):
    best = None
    for d in range(LANES, min(n, cap) + 1, LANES):
        if n % d == 0:
            best = d
    return n if best is None else best


def _pick_rows(m, cap):
    best = None
    for d in range(16, min(m, cap) + 1, 16):
        if m % d == 0:
            best = d
    return m if best is None else best


def _params(sem):
    return pltpu.CompilerParams(dimension_semantics=sem, vmem_limit_bytes=VMEM_LIMIT)


def _zeros_index(ndim, grid_rank=3):
    return (lambda i, j, kk: (0,) * ndim) if grid_rank == 3 else (lambda kk, i: (0,) * ndim)


def _mm(a, b, *, name, la=None, lb=None, ta=False, tb=False, bias=None, add=None, out_dtype=F32,
        out_layers=None, out_layer=None, after=None, post=None, tile_cols=None, caps=(1024, 1536, 2048),
        a_parts=None, b_parts=None):
    ar, ac = a.shape[-2:]
    br, bc = b.shape[-2:]
    assert a_parts is None or (not ta and la is None and a.shape[0] == a_parts)
    assert b_parts is None or (not tb and lb is None and b.shape[0] == b_parts)
    m, k = (ac, ar) if ta else (ar, ac * (a_parts or 1))
    k2, n = (bc, br) if tb else (br, bc * (b_parts or 1))
    assert k == k2, (a.shape, b.shape, ta, tb)
    if post is not None:
        caps = (512, n if tile_cols is None else tile_cols, caps[2])
    tm, tn, tk = _pick(m, caps[0]), _pick(bc if b_parts else n, caps[1]), _pick(ac if a_parts else k, caps[2])
    assert post is None or tn == caps[1]
    nk = k // tk
    gi, gj = m // tm, n // tn
    a_bytes, b_bytes = m * k * a.dtype.itemsize, k * n * b.dtype.itemsize
    rows_outer = (a_bytes + b_bytes * (gi if gj * nk > 1 else 1)) <= (b_bytes + a_bytes * (gj if gi * nk > 1 else 1))
    k_outer = post is not None and nk > 1 and gj == 1
    grid = (nk, gi) if k_outer else (gi, gj, nk) if rows_outer else (gj, gi, nk)
    keep_at = ta and nk == 1 and gj > 1 and rows_outer

    def bs(block, idx, late=False):
        if k_outer:
            return pl.BlockSpec(block, lambda kk, i: idx(jnp.where(kk == nk - 1, i, 0) if late else i, 0, kk))
        return pl.BlockSpec(block, idx if rows_outer else (lambda q, p, kk: idx(p, q, kk)))

    def spec(block, idx, layer):
        if layer is None:
            return bs(block, idx)
        return bs((None,) + block, lambda i, j, kk: (layer,) + idx(i, j, kk))

    a_spec = spec((tk, tm), lambda i, j, kk: (kk, i), la) if ta else spec((tm, tk), lambda i, j, kk: (i, kk), la)
    b_spec = spec((tn, tk), lambda i, j, kk: (j, kk), lb) if tb else spec((tk, tn), lambda i, j, kk: (kk, j), lb)
    if a_parts:
        a_spec = bs((None, tm, tk), lambda i, j, kk: (kk // (ac // tk), i, kk % (ac // tk)))
    if b_parts:
        b_spec = bs((None, tk, tn), lambda i, j, kk: (j // (bc // tn), kk, j % (bc // tn)))
    in_specs, operands = [a_spec, b_spec], [a, b]
    if bias is not None:
        in_specs.append(bs((1, tn), lambda i, j, kk: (0, j)))
        operands.append(bias)
    if add is not None:
        in_specs.append(bs((tm, tn), lambda i, j, kk: (i, j), late=True))
        operands.append(add)
    if after is not None:
        in_specs.append(pl.BlockSpec(memory_space=pl.ANY))
        operands.append(after)
    dims = (((0 if ta else 1,), (1 if tb else 0,)), ((), ()))
    has_bias, has_add = bias is not None, add is not None
    if post is None:
        fn, rows, whole, outs, sums = None, [], [], [], []
        out_shape = jax.ShapeDtypeStruct((m, n) if out_layers is None else (out_layers, m, n), out_dtype)
        out_specs = spec((tm, tn), lambda i, j, kk: (i, j), out_layer)
    else:
        fn, rows, whole, outs, sums = post
        in_specs += [bs((tm, r.shape[-1] // gj), lambda i, j, kk: (i, j), late=True) for r in rows]
        in_specs += [pl.BlockSpec(tuple(w.shape), _zeros_index(w.ndim, len(grid))) for w in whole]
        operands += list(rows) + list(whole)
        out_shape = [jax.ShapeDtypeStruct(sh, dt) for sh, dt in list(outs) + list(sums)]
        out_specs = ([bs((tm, sh[-1] // gj), lambda i, j, kk: (i, j), late=True) for sh, _ in outs]
                     + [pl.BlockSpec(tuple(sh), _zeros_index(len(sh), len(grid))) for sh, _ in sums])
    n_in, n_extra, n_outs, n_sums = len(operands), len(rows) + len(whole), len(outs), len(sums)

    def body(*refs):
        a_ref, b_ref = refs[0], refs[1]
        pos = 2
        bias_ref = add_ref = None
        if has_bias:
            bias_ref = refs[pos]
            pos += 1
        if has_add:
            add_ref = refs[pos]
            pos += 1
        extra_refs = refs[n_in - n_extra:n_in]
        out_refs = refs[n_in:n_in + max(n_outs, 1)]
        sum_refs = refs[n_in + n_outs:n_in + n_outs + n_sums]
        acc_ref = refs[-1] if nk > 1 else None
        if keep_at:
            at_ref = refs[-1]

            @pl.when(pl.program_id(1) == 0)
            def _():
                at_ref[...] = a_ref[...].astype(BF16).T

            part = lax.dot_general(at_ref[...], b_ref[...].astype(BF16), (((1,), (1 if tb else 0,)), ((), ())),
                                   preferred_element_type=F32)
        else:
            part = lax.dot_general(a_ref[...].astype(BF16), b_ref[...].astype(BF16), dims,
                                   preferred_element_type=F32)

        def finish(total):
            if has_bias:
                total = total + bias_ref[...]
            if has_add:
                total = total + add_ref[...]
            if fn is None:
                out_refs[0][...] = total.astype(out_refs[0].dtype)
                return
            res = fn(total, *[r[...] for r in extra_refs])
            for ref, val in zip(out_refs, res[:n_outs]):
                ref[...] = val.astype(ref.dtype)
            if n_sums:
                @pl.when(pl.program_id(1 if k_outer or not rows_outer else 0) == 0)
                def _():
                    for ref in sum_refs:
                        ref[...] = jnp.zeros(ref.shape, ref.dtype)

                for ref, val in zip(sum_refs, res[n_outs:]):
                    ref[...] += val

        if nk == 1:
            finish(part)
        elif k_outer:
            kk = pl.program_id(0)
            rows_i = pl.ds(pl.multiple_of(pl.program_id(1) * tm, tm), tm)

            @pl.when(kk == 0)
            def _():
                acc_ref[rows_i, :] = part

            @pl.when(kk > 0)
            def _():
                acc_ref[rows_i, :] += part

            @pl.when(kk == nk - 1)
            def _():
                finish(acc_ref[rows_i, :])
        else:
            kk = pl.program_id(2)

            @pl.when(kk == 0)
            def _():
                acc_ref[...] = part

            @pl.when(kk > 0)
            def _():
                acc_ref[...] += part

            @pl.when(kk == nk - 1)
            def _():
                finish(acc_ref[...])

    return pl.pallas_call(
        body, name=name, grid=grid, in_specs=in_specs, out_specs=out_specs, out_shape=out_shape,
        scratch_shapes=([pltpu.VMEM((m, n) if k_outer else (tm, tn), F32)] if nk > 1
                        else [pltpu.VMEM((tm, tk), BF16)] if keep_at else []),
        compiler_params=_params(("arbitrary", "arbitrary") if k_outer
                                else ("arbitrary" if n_sums else "parallel", "arbitrary" if keep_at else "parallel",
                                      "arbitrary") if rows_outer
                                else ("parallel", "arbitrary" if n_sums else "parallel", "arbitrary")),
    )(*operands)


def _rowwise(fn, rows, whole, outs, sums=(), *, name, tm=256):
    m = rows[0].shape[-2]
    tm = _pick_rows(m, tm)
    n_rows, n_whole, n_outs, n_sums = len(rows), len(whole), len(outs), len(sums)

    def rspec(shape):
        lead = len(shape) - 2
        return pl.BlockSpec(tuple(shape[:-2]) + (tm, shape[-1]), lambda i: (0,) * lead + (i, 0))

    def wspec(shape):
        return pl.BlockSpec(tuple(shape), lambda i: (0,) * len(shape))

    def body(*refs):
        vals = [r[...] for r in refs[:n_rows + n_whole]]
        out_refs = refs[n_rows + n_whole:n_rows + n_whole + n_outs]
        sum_refs = refs[n_rows + n_whole + n_outs:]
        res = fn(*vals)
        for ref, val in zip(out_refs, res[:n_outs]):
            ref[...] = val.astype(ref.dtype)
        if n_sums:
            @pl.when(pl.program_id(0) == 0)
            def _():
                for ref in sum_refs:
                    ref[...] = jnp.zeros(ref.shape, ref.dtype)

            for ref, val in zip(sum_refs, res[n_outs:]):
                ref[...] += val

    result = pl.pallas_call(
        body, name=name, grid=(m // tm,),
        in_specs=[rspec(r.shape) for r in rows] + [wspec(w.shape) for w in whole],
        out_specs=[rspec(s) for s, _ in outs] + [wspec(s) for s, _ in sums],
        out_shape=[jax.ShapeDtypeStruct(s, d) for s, d in list(outs) + list(sums)],
        compiler_params=_params(("arbitrary",)),
    )(*rows, *whole)
    return result


def _sigmoid(v):
    return jax.nn.sigmoid(v)


def _col_sum(v):
    return jnp.sum(v, axis=0, keepdims=True)


def _ln_stats(z):
    mu = jnp.mean(z, axis=-1, keepdims=True)
    zc = z - mu
    var = jnp.mean(zc * zc, axis=-1, keepdims=True)
    rstd = lax.rsqrt(var + LN_EPS)
    return zc * rstd, rstd


def _ln_fwd_fn(xin, h, gain, bias):
    xhat, _ = _ln_stats(ALPHA * xin + h)
    y = xhat * gain + bias
    return y, y


def _ple_ln_fwd_fn(xin, pg, pu, gain, bias):
    xhat, _ = _ln_stats(ALPHA * xin + _sigmoid(pg) * pu)
    y = xhat * gain + bias
    return y, y


def _ln_dz(dy, z, gain):
    xhat, rstd = _ln_stats(z)
    dxhat = dy * gain
    dz = rstd * (dxhat - jnp.mean(dxhat, axis=-1, keepdims=True)
                 - xhat * jnp.mean(dxhat * xhat, axis=-1, keepdims=True))
    return dz, _col_sum(dy * xhat), _col_sum(dy)


def _ln_bwd_fn(dy, xin, h, gain):
    dz, dgain, dbias = _ln_dz(dy, ALPHA * xin + h, gain)
    return ALPHA * dz, dz, dgain, dbias, _col_sum(dz)


def _ple_ln_bwd_fn(dy, xin, pg, pu, gain):
    sg = _sigmoid(pg)
    dz, dgain, dbias = _ln_dz(dy, ALPHA * xin + sg * pu, gain)
    dpg = dz * pu * sg * (1.0 - sg)
    return ALPHA * dz, dpg, dz * sg, dgain, dbias, _col_sum(dpg)


def _swiglu_fwd_fn(gu):
    hid = gu.shape[-1] // 2
    gate, up = gu[:, :hid], gu[:, hid:]
    return gu, gate * _sigmoid(gate) * up


def _swiglu_bwd_fn(dact, gu):
    gu = gu.astype(F32)
    hid = gu.shape[-1] // 2
    gate, up = gu[:, :hid], gu[:, hid:]
    sg = _sigmoid(gate)
    dgate = dact * up * sg * (1.0 + gate * (1.0 - sg))
    dup = dact * gate * sg
    return (jnp.concatenate([dgate, dup], axis=-1),)


def _loss_fn(y, target):
    err = y - target
    inv = 1.0 / y.shape[-1]
    part = 0.5 * inv * jnp.sum(jnp.sum(err * err, axis=-1, keepdims=True), axis=0, keepdims=True)
    return err * inv, jnp.broadcast_to(part, (1, LANES))


def _adam_fn(w, mom, vel, p_own, p_sib):
    g = p_own.astype(F32) + p_sib.astype(F32)
    m_new = ADAM_B1 * mom + (1.0 - ADAM_B1) * g
    v_new = ADAM_B2 * vel + (1.0 - ADAM_B2) * (g * g)
    m_hat = m_new / (1.0 - ADAM_B1 ** ADAM_STEP)
    v_hat = v_new / (1.0 - ADAM_B2 ** ADAM_STEP)
    delta = -ADAM_LR * (m_hat / (jnp.sqrt(v_hat) + ADAM_EPS) + ADAM_WD * w)
    return g, delta, m_new, v_new


def _split2(x):
    hi = x.astype(BF16)
    return hi, (x - hi.astype(F32)).astype(BF16)


def _dot3(a, b, dims):
    a_hi, a_lo = _split2(a)
    b_hi, b_lo = _split2(b)
    dn = (dims, ((), ()))
    return (lax.dot_general(a_hi, b_hi, dn, preferred_element_type=F32)
            + (lax.dot_general(a_hi, b_lo, dn, preferred_element_type=F32)
               + lax.dot_general(a_lo, b_hi, dn, preferred_element_type=F32)))


def _tdot(mask01, b):
    m = mask01.astype(BF16)
    b_hi = b.astype(BF16)
    rest = b - b_hi.astype(F32)
    b_mid = rest.astype(BF16)
    b_lo = (rest - b_mid.astype(F32)).astype(BF16)
    dn = (((1,), (0,)), ((), ()))
    return (lax.dot_general(m, b_hi, dn, preferred_element_type=F32)
            + (lax.dot_general(m, b_mid, dn, preferred_element_type=F32)
               + lax.dot_general(m, b_lo, dn, preferred_element_type=F32)))


def _hdot(a, b):
    return _dot3(a, b, ((1,), (0,)))


def _hdot_nt(a, b):
    return _dot3(a, b, ((1,), (1,)))


def _hdot_tn(a, b):
    return _dot3(a, b, ((0,), (0,)))


def _dot(a, b):
    return lax.dot_general(a.astype(BF16), b.astype(BF16), (((1,), (0,)), ((), ())), preferred_element_type=F32)


def _dot_nt(a, b):
    return lax.dot_general(a.astype(BF16), b.astype(BF16), (((1,), (1,)), ((), ())), preferred_element_type=F32)


def _dot_tn(a, b):
    return lax.dot_general(a.astype(BF16), b.astype(BF16), (((0,), (0,)), ((), ())), preferred_element_type=F32)


def _hg_masks():
    c = HG_CHUNK
    row = lax.broadcasted_iota(jnp.int32, (c, c), 0)
    col = lax.broadcasted_iota(jnp.int32, (c, c), 1)
    base = row & (-HG_SUB)
    return row, col, base, col <= row, col < base


def _hg_gates(qr, fr, alb):
    lbound = _sigmoid(alb[0:1, :] - alb[1:2, :])
    sig = _sigmoid(fr)
    forget = lbound + (1.0 - lbound) * sig
    kk = (1.0 - lbound) * _sigmoid(-fr)
    qt = qr * _sigmoid(qr) * (HG_DK ** -0.5)
    return qt, kk, jnp.log(forget), lbound, sig, forget


def _hg_scores(qt, kk, g, scores=True):
    c, nsub = HG_CHUNK, HG_CHUNK // HG_SUB
    row, col, base, causal, below = _hg_masks()
    b = _tdot(causal, g)
    rr = _tdot(below, g)
    bq = b - rr
    qh = qt * jnp.exp(bq)
    edecs = [None]
    parts = [jnp.zeros((HG_SUB, c), F32)]
    for i in range(1, nsub):
        edec = jnp.exp(jnp.minimum(rr[i * HG_SUB:i * HG_SUB + 1, :] - b, 0.0))
        edecs.append(edec)
        if scores:
            parts.append(_dot_nt(qh[i * HG_SUB:(i + 1) * HG_SUB, :], kk * edec))
    q3 = qt.reshape(nsub, HG_SUB, HG_DK)
    if not scores:
        return None, b, bq, qh, edecs, (b.reshape(nsub, HG_SUB, HG_DK), q3, kk.reshape(nsub, HG_SUB, HG_DK))
    a = jnp.where(below, jnp.concatenate(parts, axis=0), 0.0)
    b2 = b * LOG2_E
    b3 = b2.reshape(nsub, HG_SUB, HG_DK)
    c3 = (b2 - jnp.log2(kk)).reshape(nsub, HG_SUB, HG_DK)
    for j in range(HG_SUB):
        ek = jnp.exp2(b3 - c3[:, j:j + 1, :])
        colv = jnp.sum(q3 * ek, axis=-1, keepdims=True).reshape(c, 1)
        a = jnp.where(col == base + j, colv, a)
    a = jnp.where(causal, a, 0.0)
    return a, b, bq, qh, edecs, None


def _hg_norm(o, gr, gain):
    r = lax.rsqrt(jnp.mean(o * o, axis=-1, keepdims=True) + RMS_EPS)
    sg = _sigmoid(gr)
    return o * r * gain, r, sg


def _hgrn2_fwd(proj, alb, gain, *, rb):
    m, d4 = proj.shape
    d = d4 // 4
    heads = d // HG_DK
    hp = HG_HEADS_PER_STEP
    rb = min(rb, m)
    cpb = rb // HG_CHUNK
    nrb = m // rb

    def body(q_ref, f_ref, v_ref, g_ref, alb_ref, gain_ref, o_ref, og_ref, st_ref, a_ref, state):
        @pl.when(pl.program_id(1) == 0)
        def _():
            state[...] = jnp.zeros(state.shape, F32)

        def chunk(ci, carry):
            sl = pl.ds(pl.multiple_of(ci * HG_CHUNK, HG_CHUNK), HG_CHUNK)
            for u in range(hp):
                ln = slice(u * HG_DK, (u + 1) * HG_DK)
                qt, kk, g, _, _, _ = _hg_gates(q_ref[sl, ln], f_ref[sl, ln], alb_ref[:, ln])
                v = v_ref[sl, ln]
                st = state[u]
                st_ref[u, ci] = st
                a, b, _, _, _, _ = _hg_scores(qt, kk, g)
                a_ref[u, ci] = a.astype(a_ref.dtype)
                o = _dot(a, v) + _dot_nt(qt * jnp.exp(b), st)
                b_last = b[HG_CHUNK - 1:HG_CHUNK, :]
                state[u] = st * jnp.exp(b_last) + _hdot_tn(v, kk * jnp.exp(b_last - b))
                o_ref[sl, ln] = o
                n, _, sg = _hg_norm(o, g_ref[sl, ln], gain_ref[...])
                og_ref[sl, ln] = (n * g_ref[sl, ln] * sg).astype(og_ref.dtype)
            return carry

        lax.fori_loop(0, cpb, chunk, 0)

    def col(cidx):
        return pl.BlockSpec((rb, hp * HG_DK), lambda h, r: (r, cidx * (heads // hp) + h))

    return pl.pallas_call(
        body, name="hgrn2_fwd", grid=(heads // hp, nrb),
        in_specs=[col(0), col(1), col(2), col(3),
                  pl.BlockSpec((2, hp * HG_DK), lambda h, r: (0, h)),
                  pl.BlockSpec((1, HG_DK), lambda h, r: (0, 0))],
        out_specs=[pl.BlockSpec((rb, hp * HG_DK), lambda h, r: (r, h)),
                   pl.BlockSpec((rb, hp * HG_DK), lambda h, r: (r, h)),
                   pl.BlockSpec((hp, cpb, HG_DK, HG_DK), lambda h, r: (h, r, 0, 0)),
                   pl.BlockSpec((hp, cpb, HG_CHUNK, HG_CHUNK), lambda h, r: (h, r, 0, 0))],
        out_shape=[jax.ShapeDtypeStruct((m, d), F32), jax.ShapeDtypeStruct((m, d), BF16),
                   jax.ShapeDtypeStruct((heads, m // HG_CHUNK, HG_DK, HG_DK), F32),
                   jax.ShapeDtypeStruct((heads, m // HG_CHUNK, HG_CHUNK, HG_CHUNK), BF16)],
        scratch_shapes=[pltpu.VMEM((hp, HG_DK, HG_DK), F32)],
        compiler_params=_params(("parallel", "arbitrary")),
    )(proj, proj, proj, proj, alb, gain)


def _hgrn2_bwd(proj, o_pre, states, scores, dog, alb, gain, *, rb):
    m, d4 = proj.shape
    d = d4 // 4
    heads = d // HG_DK
    rb = min(rb, m)
    cpb = rb // HG_CHUNK
    nrb = m // rb
    c, nsub = HG_CHUNK, HG_CHUNK // HG_SUB

    def body(q_ref, f_ref, v_ref, g_ref, o_ref, st_ref, a_ref, dog_ref, alb_ref, gain_ref,
             dp_ref, dalb_ref, dgain_ref, dstate, carry_ref):
        first = (pl.program_id(0) == 0) & (pl.program_id(1) == 0)

        @pl.when(first)
        def _():
            dgain_ref[...] = jnp.zeros(dgain_ref.shape, F32)

        @pl.when(pl.program_id(1) == 0)
        def _():
            dstate[...] = jnp.zeros(dstate.shape, F32)
            carry_ref[...] = jnp.zeros(carry_ref.shape, F32)
            dalb_ref[...] = jnp.zeros(dalb_ref.shape, F32)

        row, col, base, causal, below = _hg_masks()
        sub_iota = lax.broadcasted_iota(jnp.int32, (nsub, HG_SUB, HG_DK), 1)
        row_k = lax.broadcasted_iota(jnp.int32, (c, HG_DK), 0)
        upper = col >= row

        def chunk(step, carry):
            ci = cpb - 1 - step
            sl = pl.ds(pl.multiple_of(ci * HG_CHUNK, HG_CHUNK), HG_CHUNK)
            qr, fr, v, gr = q_ref[sl, :], f_ref[sl, :], v_ref[sl, :], g_ref[sl, :]
            qt, kk, g, lbound, sig, forget = _hg_gates(qr, fr, alb_ref[...])
            o = o_ref[sl, :]
            dogv = dog_ref[sl, :]
            gain_v = gain_ref[...]
            n, r, sg = _hg_norm(o, gr, gain_v)
            dgr = dogv * n * sg * (1.0 + gr * (1.0 - sg))
            dn = dogv * gr * sg
            dgain_ref[...] += _col_sum(dn * o * r)
            u = dn * gain_v
            d_o = r * u - o * (r * r * r) * jnp.mean(u * o, axis=-1, keepdims=True)
            st0 = st_ref[ci]
            dst = dstate[...]
            _, b, bq, qh, edecs, (b3, q3, k3) = _hg_scores(qt, kk, g, scores=False)
            a = a_ref[ci]
            eb = jnp.exp(b)
            b_last = b[c - 1:c, :]
            kdl_dec = jnp.exp(b_last - b)
            kdl = kk * kdl_dec
            d_a = jnp.where(causal, _dot_nt(d_o, v), 0.0)
            d_at = _dot_nt(v, d_o)
            dv = _dot_tn(a, d_o) + _dot_nt(kdl, dst)
            dq = eb * _hdot(d_o, st0)
            dk = _hdot(v, dst) * kdl_dec
            d_a_below = jnp.where(below, d_a, 0.0)
            dq_parts = [jnp.zeros((HG_SUB, HG_DK), F32)]
            for i in range(1, nsub):
                lo, hi = i * HG_SUB, (i + 1) * HG_SUB
                dq_parts.append(_hdot(d_a_below[lo:hi, :], kk * edecs[i]))
                gi = _hdot(d_at[:, lo:hi], qh[lo:hi, :])
                dk = dk + jnp.where(row_k < lo, edecs[i] * gi, 0.0)
            dq = dq + jnp.concatenate(dq_parts, axis=0) * jnp.exp(bq)
            dq3 = jnp.zeros((nsub, HG_SUB, HG_DK), F32)
            dk3 = jnp.zeros((nsub, HG_SUB, HG_DK), F32)
            d_diag = jnp.concatenate([d_a[i * HG_SUB:(i + 1) * HG_SUB, i * HG_SUB:(i + 1) * HG_SUB]
                                      for i in range(nsub)], axis=0).reshape(nsub, HG_SUB, HG_SUB)
            for j in range(HG_SUB):
                e = jnp.exp(jnp.minimum(b3 - b3[:, j:j + 1, :], 0.0))
                t1 = d_diag[:, :, j:j + 1] * e
                dq3 = dq3 + t1 * k3[:, j:j + 1, :]
                dk3 = jnp.where(sub_iota == j, jnp.sum(t1 * q3, axis=1, keepdims=True), dk3)
            dq = dq + dq3.reshape(c, HG_DK)
            dk = dk + dk3.reshape(c, HG_DK)
            dstate[...] = dst * jnp.exp(b_last) + _hdot_tn(d_o, qt * eb)
            dglog = _tdot(upper, qt * dq - kk * dk) + carry_ref[...]
            carry_ref[...] = dglog[0:1, :]
            dforget = dglog / forget
            one_m_lb = 1.0 - lbound
            dsig = (dforget - dk) * one_m_lb
            sneg = _sigmoid(-fr)
            dlb = _col_sum(dforget * (1.0 - sig) - dk * sneg)
            dalb0 = dlb * lbound * one_m_lb
            dalb_ref[...] += jnp.concatenate([dalb0, -dalb0], axis=0)
            sq = _sigmoid(qr)
            dp_ref[0, sl, :] = (dq * (HG_DK ** -0.5) * sq * (1.0 + qr * (1.0 - sq))).astype(dp_ref.dtype)
            dp_ref[1, sl, :] = (dsig * sig * (1.0 - sig)).astype(dp_ref.dtype)
            dp_ref[2, sl, :] = dv.astype(dp_ref.dtype)
            dp_ref[3, sl, :] = dgr.astype(dp_ref.dtype)
            return carry

        lax.fori_loop(0, cpb, chunk, 0, unroll=2)

    def rev(r):
        return nrb - 1 - r

    def col(cidx):
        return pl.BlockSpec((rb, HG_DK), lambda h, r: (rev(r), cidx * heads + h))

    def head_rows():
        return pl.BlockSpec((rb, HG_DK), lambda h, r: (rev(r), h))

    return pl.pallas_call(
        body, name="hgrn2_bwd", grid=(heads, nrb),
        in_specs=[col(0), col(1), col(2), col(3), head_rows(),
                  pl.BlockSpec((None, cpb, HG_DK, HG_DK), lambda h, r: (h, rev(r), 0, 0)),
                  pl.BlockSpec((None, cpb, HG_CHUNK, HG_CHUNK), lambda h, r: (h, rev(r), 0, 0)),
                  head_rows(),
                  pl.BlockSpec((2, HG_DK), lambda h, r: (0, h)),
                  pl.BlockSpec((1, HG_DK), lambda h, r: (0, 0))],
        out_specs=[pl.BlockSpec((4, rb, HG_DK), lambda h, r: (0, rev(r), h)),
                   pl.BlockSpec((2, HG_DK), lambda h, r: (0, h)),
                   pl.BlockSpec((1, HG_DK), lambda h, r: (0, 0))],
        out_shape=[jax.ShapeDtypeStruct((4, m, d), BF16), jax.ShapeDtypeStruct((2, d), F32),
                   jax.ShapeDtypeStruct((1, HG_DK), F32)],
        scratch_shapes=[pltpu.VMEM((HG_DK, HG_DK), F32), pltpu.VMEM((1, HG_DK), F32)],
        compiler_params=_params(("arbitrary", "arbitrary")),
    )(proj, proj, proj, proj, o_pre, states, scores, dog, alb, gain)


def _swa_probs(qh, kp, kc, sink, slope, has_prev, lse=None):
    rows = qh.shape[0]
    qi = lax.broadcasted_iota(jnp.int32, (rows, WINDOW), 0) & (WINDOW - 1)
    si = lax.broadcasted_iota(jnp.int32, (rows, WINDOW), 1)
    scale = ATT_HD ** -0.5
    dist_c = (qi - si).astype(F32)
    s_p = _dot_nt(qh, kp) * scale - slope * (dist_c + float(WINDOW))
    s_c = _dot_nt(qh, kc) * scale - slope * dist_c
    s_p = jnp.where((si > qi) & has_prev, s_p, NEG)
    s_c = jnp.where(si <= qi, s_c, NEG)
    if lse is not None:
        return jnp.exp(s_p - lse), jnp.exp(s_c - lse), jnp.exp(sink - lse), lse
    mx = jnp.maximum(jnp.maximum(jnp.max(s_p, axis=-1, keepdims=True), jnp.max(s_c, axis=-1, keepdims=True)), sink)
    e_p, e_c, e_s = jnp.exp(s_p - mx), jnp.exp(s_c - mx), jnp.exp(sink - mx)
    total = jnp.sum(e_p, axis=-1, keepdims=True) + jnp.sum(e_c, axis=-1, keepdims=True) + e_s
    inv = 1.0 / total
    return e_p * inv, e_c * inv, e_s * inv, mx + jnp.log(total)


def _slope(h, n_heads):
    return float(2.0 ** (-8.0 * (h + 1) / n_heads))


def _swa_group(ref_vals, sink_ref, kh, n_heads):
    heads = [kh * ATT_G + g for g in range(ATT_G)]
    stacked = [jnp.concatenate([v[:, h * ATT_HD:(h + 1) * ATT_HD] for h in heads], axis=0) for v in ref_vals]
    grp = lax.shift_right_logical(lax.broadcasted_iota(jnp.int32, (ATT_G * WINDOW, 1), 0), WINDOW.bit_length() - 1)
    slope = jnp.zeros((ATT_G * WINDOW, 1), F32)
    sink = jnp.zeros((ATT_G * WINDOW, 1), F32)
    for g, h in enumerate(heads):
        slope = jnp.where(grp == g, _slope(h, n_heads), slope)
        sink = jnp.where(grp == g, sink_ref[:, h:h + 1], sink)
    return stacked, slope, sink


def _swa_fwd(q, kv, sinks):
    m, d = q.shape
    n_heads = d // ATT_HD
    kvh = n_heads // ATT_G
    kd = kvh * ATT_HD
    nb = m // WINDOW

    def body(q_ref, kvp_ref, kvc_ref, sink_ref, o_ref, lse_ref):
        has_prev = pl.program_id(0) > 0
        qv, kvp, kvc = q_ref[...], kvp_ref[...], kvc_ref[...]
        lane_h = lax.broadcasted_iota(jnp.int32, (WINDOW, n_heads), 1)
        outs, lse_all = [], jnp.zeros((WINDOW, n_heads), F32)
        for kh in range(kvh):
            ks = slice(kh * ATT_HD, (kh + 1) * ATT_HD)
            vs = slice(kd + kh * ATT_HD, kd + (kh + 1) * ATT_HD)
            (q4,), slope, sink = _swa_group([qv], sink_ref, kh, n_heads)
            p_p, p_c, _, lse = _swa_probs(q4, kvp[:, ks], kvc[:, ks], sink, slope, has_prev)
            o4 = _dot(p_p, kvp[:, vs]) + _dot(p_c, kvc[:, vs])
            for g in range(ATT_G):
                rows = slice(g * WINDOW, (g + 1) * WINDOW)
                outs.append(o4[rows, :])
                lse_all = jnp.where(lane_h == kh * ATT_G + g, lse[rows, :], lse_all)
        o_ref[...] = jnp.concatenate(outs, axis=-1).astype(o_ref.dtype)
        lse_ref[...] = lse_all

    return pl.pallas_call(
        body, name="swa_fwd", grid=(nb,),
        in_specs=[pl.BlockSpec((WINDOW, d), lambda n: (n, 0)),
                  pl.BlockSpec((WINDOW, 2 * kd), lambda n: (jnp.maximum(n - 1, 0), 0)),
                  pl.BlockSpec((WINDOW, 2 * kd), lambda n: (n, 0)),
                  pl.BlockSpec((1, n_heads), lambda n: (0, 0))],
        out_specs=[pl.BlockSpec((WINDOW, d), lambda n: (n, 0)), pl.BlockSpec((WINDOW, n_heads), lambda n: (n, 0))],
        out_shape=[jax.ShapeDtypeStruct((m, d), BF16), jax.ShapeDtypeStruct((m, n_heads), F32)],
        compiler_params=_params(("arbitrary",)),
    )(q, kv, kv, sinks)


def _swa_bwd(q, kv, sinks, lse, dao):
    m, d = q.shape
    n_heads = d // ATT_HD
    kvh = n_heads // ATT_G
    kd = kvh * ATT_HD
    nb = m // WINDOW
    scale = ATT_HD ** -0.5

    def body(q_ref, kvp_ref, kvc_ref, sink_ref, lse_ref, do_ref, dq_ref, dkvc_ref, dkvp_ref, dqsum_ref, dsink_ref):
        @pl.when(pl.program_id(0) == 0)
        def _():
            dqsum_ref[...] = jnp.zeros(dqsum_ref.shape, F32)
            dsink_ref[...] = jnp.zeros(dsink_ref.shape, F32)

        has_prev = pl.program_id(0) > 0
        qv, kvp, kvc, dov = q_ref[...], kvp_ref[...], kvc_ref[...], do_ref[...]
        lane_h = lax.broadcasted_iota(jnp.int32, (1, n_heads), 1)
        dsink = jnp.zeros((1, n_heads), F32)
        dq_parts, dk_p, dk_c, dv_p, dv_c = [], [], [], [], []
        for kh in range(kvh):
            ks = slice(kh * ATT_HD, (kh + 1) * ATT_HD)
            vs = slice(kd + kh * ATT_HD, kd + (kh + 1) * ATT_HD)
            kp, kc, vp, vc = kvp[:, ks], kvc[:, ks], kvp[:, vs], kvc[:, vs]
            (q4, do4), slope, sink = _swa_group([qv, dov], sink_ref, kh, n_heads)
            lse4 = jnp.concatenate([lse_ref[:, kh * ATT_G + g:kh * ATT_G + g + 1] for g in range(ATT_G)], axis=0)
            p_p, p_c, p_s, _ = _swa_probs(q4, kp, kc, sink, slope, has_prev, lse=lse4)
            dp_p, dp_c = _dot_nt(do4, vp), _dot_nt(do4, vc)
            delta = jnp.sum(p_p * dp_p, axis=-1, keepdims=True) + jnp.sum(p_c * dp_c, axis=-1, keepdims=True)
            ds_p, ds_c = p_p * (dp_p - delta), p_c * (dp_c - delta)
            sink_term = p_s * delta
            dq4 = (_dot(ds_p, kp) + _dot(ds_c, kc)) * scale
            for g in range(ATT_G):
                rows = slice(g * WINDOW, (g + 1) * WINDOW)
                dsink = dsink + jnp.where(lane_h == kh * ATT_G + g, -_col_sum(sink_term[rows, :]), 0.0)
                dq_parts.append(dq4[rows, :])
            dk_p.append(_dot_tn(ds_p, q4) * scale)
            dk_c.append(_dot_tn(ds_c, q4) * scale)
            dv_p.append(_dot_tn(p_p, do4))
            dv_c.append(_dot_tn(p_c, do4))
        dq = jnp.concatenate(dq_parts, axis=-1)
        dq_ref[...] = dq.astype(dq_ref.dtype)
        dqsum_ref[...] += _col_sum(dq)
        dsink_ref[...] += dsink
        dkvc_ref[...] = jnp.concatenate(dk_c + dv_c, axis=-1)
        dkvp_ref[...] = jnp.concatenate(dk_p + dv_p, axis=-1)

    return pl.pallas_call(
        body, name="swa_bwd", grid=(nb,),
        in_specs=[pl.BlockSpec((WINDOW, d), lambda n: (n, 0)),
                  pl.BlockSpec((WINDOW, 2 * kd), lambda n: (jnp.maximum(n - 1, 0), 0)),
                  pl.BlockSpec((WINDOW, 2 * kd), lambda n: (n, 0)),
                  pl.BlockSpec((1, n_heads), lambda n: (0, 0)),
                  pl.BlockSpec((WINDOW, n_heads), lambda n: (n, 0)),
                  pl.BlockSpec((WINDOW, d), lambda n: (n, 0))],
        out_specs=[pl.BlockSpec((WINDOW, d), lambda n: (n, 0)),
                   pl.BlockSpec((WINDOW, 2 * kd), lambda n: (n, 0)),
                   pl.BlockSpec((WINDOW, 2 * kd), lambda n: (n, 0)),
                   pl.BlockSpec((1, d), lambda n: (0, 0)),
                   pl.BlockSpec((1, n_heads), lambda n: (0, 0))],
        out_shape=[jax.ShapeDtypeStruct((m, d), BF16), jax.ShapeDtypeStruct((m, 2 * kd), F32),
                   jax.ShapeDtypeStruct((m, 2 * kd), F32), jax.ShapeDtypeStruct((1, d), F32),
                   jax.ShapeDtypeStruct((1, n_heads), F32)],
        compiler_params=_params(("arbitrary",)),
    )(q, kv, kv, sinks, lse, dao)


def _kv_grad_combine(dkv_cur, dkv_prev):
    m, w = dkv_cur.shape
    nb = m // WINDOW

    def body(cur_ref, nxt_ref, o_ref, sum_ref):
        @pl.when(pl.program_id(0) == 0)
        def _():
            sum_ref[...] = jnp.zeros(sum_ref.shape, F32)

        total = cur_ref[...] + jnp.where(pl.program_id(0) < nb - 1, nxt_ref[...], 0.0)
        o_ref[...] = total.astype(o_ref.dtype)
        sum_ref[...] += _col_sum(total)

    return pl.pallas_call(
        body, name="kv_grad_combine", grid=(nb,),
        in_specs=[pl.BlockSpec((WINDOW, w), lambda n: (n, 0)),
                  pl.BlockSpec((WINDOW, w), lambda n: (jnp.minimum(n + 1, nb - 1), 0))],
        out_specs=[pl.BlockSpec((WINDOW, w), lambda n: (n, 0)), pl.BlockSpec((1, w), lambda n: (0, 0))],
        out_shape=[jax.ShapeDtypeStruct((m, w), BF16), jax.ShapeDtypeStruct((1, w), F32)],
        compiler_params=_params(("arbitrary",)),
    )(dkv_cur, dkv_prev)


def _row(v):
    return v.reshape(1, -1)


def _local_step(x, p, target, wget, grad_sink, ln_gain, ln_bias, alb, norm_gain, kv_b, b_q, sinks, b_out, ple_b,
                small_sink=None):
    gs = {}
    gains = ln_gain.reshape(DEPTH * 3, -1)
    biases = ln_bias.reshape(DEPTH * 3, -1)
    sd = x.shape
    pending = [None]

    def mm(a, b, lb=0, **kw):
        after, pending[0] = pending[0], None
        return _mm(a, b, lb=lb, after=after, **kw)

    def mm_ln(a, wt, xin, i, j, nm, bias=None, pu=None):
        r = 3 * i + j
        if pu is None:
            fn, rows = (lambda h, xv, g, bv: (h,) + _ln_fwd_fn(xv, h, g[r:r + 1], bv[r:r + 1])), [xin]
        else:
            fn = lambda h, xv, puv, g, bv: (h,) + _ple_ln_fwd_fn(xv, h, puv, g[r:r + 1], bv[r:r + 1])
            rows = [xin, pu]
        h, y, yb = _mm(a, wt, lb=0, bias=bias, name=nm,
                       post=(fn, rows, [gains, biases], [(sd, F32), (sd, F32), (sd, BF16)], []))
        return h, (y, yb)

    def mm_ln_bwd(a, wt, add, xin, h, i, j, nm):
        r = 3 * i + j
        dx_part, dh, dg, db, dhsum = mm(a, wt, tb=True, add=add, name=nm,
                                        post=(lambda dy, xv, hv, g: _ln_bwd_fn(dy, xv, hv, g[r:r + 1]), [xin, h],
                                              [gains], [(sd, F32), (sd, BF16)], [((1, sd[1]), F32)] * 3))
        gs[f"ln_gain_{i}_{j}"], gs[f"ln_bias_{i}_{j}"] = dg, db
        return dx_part, dh, dhsum

    def tail_fwd(xa, i):
        wgu = wget("ffn_w_gate_up", i, xa[1])
        hid2 = wgu.shape[-1]
        gu, act = _mm(xa[1], wgu, lb=0, name=f"ffn_up_swiglu{i}", tile_cols=hid2 // 2,
                      post=(_swiglu_fwd_fn, [], [], [((sd[0], hid2), BF16), ((sd[0], hid2 // 2), BF16)], []))
        f, xb = mm_ln(act, wget("ffn_w_down", i, act), xa[0], i, 1, f"ffn_down_ln{i}")
        pu = _mm(p, wget("ple_w_up", i, act), la=i, lb=0, name=f"ple_up{i}")
        pg, xc = mm_ln(xb[1], wget("ple_w_gate", i, act), xb[0], i, 2, f"ple_gate_ln{i}", bias=_row(ple_b[i]), pu=pu)
        return dict(xa=xa, gu=gu, act=act, f=f, xb=xb, pg=pg, pu=pu), xc

    def tail_bwd(head, sv, i, mix_in, mix_h):
        xa, xb = sv["xa"], sv["xb"]
        r = 3 * i + 2
        dxb_part, dpg, dpu, dg2, db2, dbg = head(
            lambda dy, xv, pgv, puv, g: _ple_ln_bwd_fn(dy, xv, pgv, puv, g[r:r + 1]), [xb[0], sv["pg"], sv["pu"]],
            [gains], [(sd, F32), (sd, BF16), (sd, BF16)], [((1, sd[1]), F32)] * 3)[:6]
        gs[f"ple_b_{i}"] = dbg
        gs[f"ln_gain_{i}_2"], gs[f"ln_bias_{i}_2"] = dg2, db2
        grad_of("ple_w_gate", i, xb[1], dpg)
        grad_of("ple_w_up", i, p, dpu, la=i)
        dxa_part, df, _ = mm_ln_bwd(dpg, wget("ple_w_gate", i, None), dxb_part, xa[0], sv["f"], i, 1,
                                    f"ple_gate_dx_ln{i}")
        grad_of("ffn_w_down", i, sv["act"], df)
        gu = sv["gu"]
        dgu, = mm(df, wget("ffn_w_down", i, None), tb=True, name=f"ffn_down_dx_swiglu{i}", tile_cols=gu.shape[1] // 4,
                  post=(_swiglu_bwd_fn, [gu], [], [(gu.shape, BF16)], []))
        grad_of("ffn_w_gate_up", i, xa[1], dgu)
        return mm_ln_bwd(dgu, wget("ffn_w_gate_up", i, None), dxa_part, mix_in, mix_h, i, 0, f"ffn_up_dx_ln{i}")

    def grad_of(nm, i, act, dout, la=None, b_parts=None):
        grad = mm(act, dout, la=la, lb=None, ta=True, out_dtype=BF16, out_layers=1, out_layer=0,
                  name=f"grad_{nm}{i}", b_parts=b_parts)
        token = grad_sink(nm, i, grad)
        if token is not None:
            pending[0] = token

    proj = _mm(x, wget("a_w_in", 0, None), lb=0, name="hg_proj")
    o_pre, og, states, scores = _hgrn2_fwd(proj, alb, norm_gain, rb=HG_ROWS)
    h0, x1 = mm_ln(og, wget("a_w_out", 0, og), x, 0, 0, "hg_out_ln")
    sv0, x3 = tail_fwd(x1, 0)
    kv = _mm(x3[1], wget("kv_w", 0, x3[1]), lb=0, bias=_row(kv_b), out_dtype=BF16, name="kv_proj")
    q = _mm(x3[1], wget("b_w_q", 0, x3[1]), lb=0, bias=b_q, out_dtype=BF16, name="q_proj")
    ao, lse = _swa_fwd(q, kv, sinks)
    h1, x4 = mm_ln(ao, wget("b_w_out", 0, x3[1]), x3[0], 1, 0, "att_out_ln", bias=b_out)
    sv1, y = tail_fwd(x4, 1)

    loss_box = []

    def loss_head(fn, rows, whole, outs, sums):
        def with_loss(yv, tv, *rest):
            dy, part = _loss_fn(yv, tv)
            return fn(dy, *rest) + (part,)

        res = _rowwise(with_loss, [y[0], target] + rows, whole, outs, list(sums) + [((1, LANES), F32)],
                       name="loss_ln_ple_bwd1")
        loss_box.append(res[-1])
        return res

    dx3_part, dh1, dh1sum = tail_bwd(loss_head, sv1, 1, x3[0], h1)
    loss = loss_box[0]
    gs["b_out"] = dh1sum
    grad_of("b_w_out", 0, ao, dh1)
    dao = mm(dh1, wget("b_w_out", 0, None), tb=True, out_dtype=BF16, name="att_out_dx")
    dq, dkv_cur, dkv_prev, dqsum, dsinks = _swa_bwd(q, kv, sinks, lse, dao)
    gs["b_q"], gs["sinks"] = dqsum, dsinks
    dkv, dkvsum = _kv_grad_combine(dkv_cur, dkv_prev)
    gs["kv_b"] = dkvsum
    grad_of("b_w_q", 0, x3[1], dq)
    grad_of("kv_w", 0, x3[1], dkv)
    dx3 = mm(dq, wget("b_w_q", 0, None), tb=True, add=dx3_part, name="q_proj_dx")

    def kv_head(*post):
        return mm(dkv, wget("kv_w", 0, None), tb=True, add=dx3, name="kv_proj_dx_ln_ple_bwd0", post=post)

    dx_part, dh0, _ = tail_bwd(kv_head, sv0, 0, x, h0)
    grad_of("a_w_out", 0, og, dh0)
    dog = mm(dh0, wget("a_w_out", 0, None), tb=True, name="hg_out_dx")
    dproj, dalb, dgain = _hgrn2_bwd(proj, o_pre, states, scores, dog, alb, norm_gain, rb=HG_ROWS)
    gs["alb"], gs["norm_gain"] = dalb, dgain
    if small_sink is not None:
        pending[0] = small_sink(loss, gs)
    grad_of("a_w_in", 0, x, dproj, b_parts=4)
    grad_x = mm(dproj, wget("a_w_in", 0, None), tb=True, add=dx_part, name="hg_proj_dx", a_parts=4)
    return loss, grad_x, gs


HBM_SPEC = pl.BlockSpec(memory_space=pl.ANY)
HBM_ONLY = pl.BlockSpec(memory_space=pltpu.HBM)
SEM_SPEC = pl.BlockSpec(memory_space=pltpu.SEMAPHORE)
SIDE_EFFECT = pltpu.SideEffectType.DATAFLOW_SIDE_EFFECTING


def _slot(kind, j):
    return (j % 2) * 2 + j // 2 if kind == "colp" else j


def _piece(ref, kind, j):
    _, r, c = ref.shape
    if kind == "row":
        return ref.at[:, pl.ds(j * (r // N_CHIPS), r // N_CHIPS), :]
    return ref.at[:, :, pl.ds(_slot(kind, j) * (c // N_CHIPS), c // N_CHIPS)]


def _piece_dyn(ref, kind, j):
    _, r, c = ref.shape
    if kind == "row":
        return ref.at[:, pl.ds(pl.multiple_of(j * (r // N_CHIPS), 16), r // N_CHIPS), :]
    return ref.at[:, :, pl.ds(pl.multiple_of(_slot(kind, j) * (c // N_CHIPS), LANES), c // N_CHIPS)]


def _chip_of(j, c):
    return (j // 2, j % 2, c)


def _in_hbm(a):
    return pltpu.with_memory_space_constraint(a, pltpu.HBM)


PLACE_STEPS = 4


def _place(items, chip, *, name, after=None):
    n = len(items)
    in_specs, out_specs, out_shapes, blocks = [], [], [], []
    for src, layer, kind, out_dtype in items:
        _, r, c = src.shape
        nb = max(k for k in (1, 2, PLACE_STEPS) if r % (16 * k) == 0 or k == 1)
        blocks.append(nb)

        def src_idx(i, chip_ref, layer=layer, nb=nb):
            return (layer, jnp.minimum(i, nb - 1), 0)

        def full_idx(i, chip_ref, kind=kind, nb=nb):
            ib = jnp.minimum(i, nb - 1)
            return (0, chip_ref[0] * nb + ib, 0) if kind == "row" else (0, ib, _slot(kind, chip_ref[0]))

        in_specs.append(pl.BlockSpec((None, r // nb, c), src_idx))
        out_specs.append(pl.BlockSpec((None, r // nb, c), full_idx))
        out_shapes.append(jax.ShapeDtypeStruct((1, r * N_CHIPS, c) if kind == "row" else (1, r, c * N_CHIPS),
                                               out_dtype))
    operands = [it[0] for it in items]
    if after is not None:
        in_specs.append(HBM_SPEC)
        operands.append(after)

    def body(chip_ref, *refs):
        for a in range(n):
            refs[len(refs) - n + a][...] = refs[a][...].astype(refs[len(refs) - n + a].dtype)

    return pl.pallas_call(
        body, name=name,
        grid_spec=pltpu.PrefetchScalarGridSpec(num_scalar_prefetch=1, grid=(PLACE_STEPS,), in_specs=in_specs,
                                               out_specs=out_specs),
        out_shape=out_shapes,
        compiler_params=_params(("arbitrary",)),
    )(chip, *operands)


def _half(ref, c):
    h = ref.shape[1] // 2
    start = c * h if isinstance(c, int) else pl.multiple_of(c * h, 16)
    return ref.at[:, pl.ds(start, h), :]


def _sibling_handshake():
    barrier = pltpu.get_barrier_semaphore()
    sibling = (lax.axis_index("x"), lax.axis_index("y"), 1 - lax.axis_index("c"))
    pl.semaphore_signal(barrier, inc=1, device_id=sibling, device_id_type=MESH)
    pl.semaphore_wait(barrier, 1)


class _SiblingFill:
    def __init__(self, lands, kinds, name, collective_id):
        self.kinds, self.name, self.n = kinds, name, len(lands)
        n = self.n
        sem_shape = pltpu.SemaphoreType.DMA((n * N_CHIPS,))

        def body(*refs):
            land_refs, send_sems, recv_sems, token = refs[:n], refs[n], refs[n + 1], refs[-1]
            _sibling_handshake()
            for cp in self._copies(land_refs, send_sems, recv_sems):
                cp.start()
            token[...] = jnp.zeros(token.shape, token.dtype)

        outs = pl.pallas_call(
            body, name=name + "_start",
            in_specs=[HBM_ONLY] * n,
            out_specs=[SEM_SPEC, SEM_SPEC] + [HBM_ONLY] * n + [pl.BlockSpec(memory_space=pltpu.VMEM)],
            out_shape=[sem_shape, sem_shape] + [pltpu.HBM(a.shape, a.dtype) for a in lands]
                      + [jax.ShapeDtypeStruct((8, LANES), F32)],
            input_output_aliases={i: i + 2 for i in range(n)},
            compiler_params=pltpu.CompilerParams(has_side_effects=SIDE_EFFECT, collective_id=collective_id),
        )(*[_in_hbm(a) for a in lands])
        self.send_sems, self.recv_sems, self.lands, self.token = outs[0], outs[1], list(outs[2:2 + n]), outs[-1]

    def _copies(self, land_refs, send_sems, recv_sems):
        x, y, c = lax.axis_index("x"), lax.axis_index("y"), lax.axis_index("c")
        me = 2 * x + y
        copies = []
        for a in range(self.n):
            for k in range(1, N_CHIPS):
                t = (me + k) % N_CHIPS
                slice_t = _piece_dyn(land_refs[a], self.kinds[a], t)
                got = _half(slice_t, c)
                copies.append(pltpu.make_async_remote_copy(
                    src_ref=got, dst_ref=got, send_sem=send_sems.at[a * N_CHIPS + k],
                    recv_sem=recv_sems.at[a * N_CHIPS + k], device_id=(x, y, 1 - c), device_id_type=MESH))
        return copies

    def wait(self, after):
        n = self.n

        def body(*refs):
            land_refs, send_sems, recv_sems = refs[:n], refs[n], refs[n + 1]
            for cp in self._copies(land_refs, send_sems, recv_sems):
                cp.wait_send()
                cp.wait_recv()

        operands = [_in_hbm(a) for a in self.lands] + [self.send_sems, self.recv_sems]
        in_specs = [HBM_ONLY] * n + [SEM_SPEC, SEM_SPEC]
        if after is not None:
            operands.append(after)
            in_specs.append(HBM_SPEC)
        outs = pl.pallas_call(
            body, name=self.name + "_wait",
            in_specs=in_specs, out_specs=[HBM_ONLY] * n,
            out_shape=[pltpu.HBM(a.shape, a.dtype) for a in self.lands],
            input_output_aliases={i: i for i in range(n)},
            compiler_params=pltpu.CompilerParams(has_side_effects=SIDE_EFFECT),
        )(*operands)
        return list(outs)


class _Exchange:
    def __init__(self, mode, srcs, lands, kinds, layers, name, collective_id, after=None, halves=None):
        self.mode, self.kinds, self.layers, self.name, self.n = mode, kinds, layers, name, len(lands)
        self.halves = halves if halves is not None else [False] * len(lands)
        n, ns = self.n, len(srcs)
        n_in = ns + n + (after is not None)
        sem_shape = pltpu.SemaphoreType.DMA((n * N_CHIPS,))

        def body(*refs):
            src_refs, land_refs = refs[:ns], refs[ns:ns + n]
            send_sems, recv_sems = refs[n_in], refs[n_in + 1]
            token = refs[-1]
            c = lax.axis_index("c")
            me = 2 * lax.axis_index("x") + lax.axis_index("y")
            barrier = pltpu.get_barrier_semaphore()
            for k in range(1, N_CHIPS):
                t = (me + k) % N_CHIPS
                pl.semaphore_signal(barrier, inc=1, device_id=(t // 2, t % 2, c), device_id_type=MESH)
            pl.semaphore_wait(barrier, N_CHIPS - 1)
            for j in range(N_CHIPS):
                @pl.when(me == j)
                def _():
                    for a in range(n):
                        for t in range(N_CHIPS):
                            if t != j:
                                src, dst = self._ends(src_refs, land_refs, a, j, t, c)
                                pltpu.make_async_remote_copy(
                                    src_ref=src, dst_ref=dst, send_sem=send_sems.at[a * N_CHIPS + t],
                                    recv_sem=recv_sems.at[a * N_CHIPS + j],
                                    device_id=_chip_of(t, c), device_id_type=MESH).start()
            token[...] = jnp.zeros(token.shape, token.dtype)

        arrays = list(srcs) + list(lands)
        operands = [_in_hbm(a) for a in arrays]
        in_specs = [HBM_ONLY] * (ns + n)
        if after is not None:
            operands.append(after)
            in_specs.append(HBM_SPEC)
        outs = pl.pallas_call(
            body, name=name + "_start",
            in_specs=in_specs,
            out_specs=[SEM_SPEC, SEM_SPEC] + [HBM_ONLY] * (ns + n) + [pl.BlockSpec(memory_space=pltpu.VMEM)],
            out_shape=[sem_shape, sem_shape] + [pltpu.HBM(a.shape, a.dtype) for a in arrays]
                      + [jax.ShapeDtypeStruct((8, LANES), F32)],
            input_output_aliases={i: i + 2 for i in range(ns + n)},
            compiler_params=pltpu.CompilerParams(has_side_effects=SIDE_EFFECT, collective_id=collective_id),
        )(*operands)
        self.send_sems, self.recv_sems = outs[0], outs[1]
        self.srcs, self.lands = list(outs[2:2 + ns]), list(outs[2 + ns:2 + ns + n])
        self.token = outs[-1]

    def _ends(self, src_refs, land_refs, a, me_j, peer, c):
        if self.mode == "gather":
            mine = _piece(land_refs[a], self.kinds[a], me_j)
            if self.halves[a]:
                mine = _half(mine, c)
            return mine, mine
        return _piece(src_refs[a], self.kinds[a], peer), land_refs[a].at[me_j, pl.ds(self.layers[a], 1)]

    def wait(self, after, lands=None):
        n, ns = self.n, len(self.srcs)
        lands = self.lands if lands is None else lands

        def body(*refs):
            src_refs, land_refs = refs[:ns], refs[ns:ns + n]
            send_sems, recv_sems = refs[ns + n], refs[ns + n + 1]
            c = lax.axis_index("c")
            me = 2 * lax.axis_index("x") + lax.axis_index("y")
            for j in range(N_CHIPS):
                @pl.when(me != j)
                def _():
                    for a in range(n):
                        sent, _ = self._ends(src_refs, land_refs, a, 0, j, c)
                        _, landed = self._ends(src_refs, land_refs, a, j, 0, c)
                        cp = pltpu.make_async_remote_copy(
                            src_ref=sent, dst_ref=landed, send_sem=send_sems.at[a * N_CHIPS + j],
                            recv_sem=recv_sems.at[a * N_CHIPS + j],
                            device_id=_chip_of(j, c), device_id_type=MESH)
                        cp.wait_send()
                        cp.wait_recv()

        arrays = self.srcs + list(lands)
        operands = [_in_hbm(a) for a in arrays] + [self.send_sems, self.recv_sems]
        in_specs = [HBM_ONLY] * (ns + n) + [SEM_SPEC, SEM_SPEC]
        if after is not None:
            operands.append(after)
            in_specs.append(HBM_SPEC)
        outs = pl.pallas_call(
            body, name=self.name + "_wait",
            in_specs=in_specs, out_specs=[HBM_ONLY] * (ns + n),
            out_shape=[pltpu.HBM(a.shape, a.dtype) for a in arrays],
            input_output_aliases={i: i for i in range(ns + n)},
            compiler_params=pltpu.CompilerParams(has_side_effects=SIDE_EFFECT),
        )(*operands)
        return list(outs[:ns]), list(outs[ns:])


def _sum_arrivals(zone, own_grads, kind, chip, name, after=None):
    _, layers, r, c = zone.shape
    tm = _pick_rows(r, 256)
    nb = r // tm

    def own_idx(l, i, chip_ref):
        return (0, chip_ref[0] * nb + i, 0) if kind == "row" else (0, i, _slot(kind, chip_ref[0]))

    def slot_idx(k):
        return lambda l, i, chip_ref: (jnp.where(chip_ref[0] == k, (k + 1) % N_CHIPS, k), l, i, 0)

    in_specs = [pl.BlockSpec((None, None, tm, c), slot_idx(k)) for k in range(N_CHIPS)]
    in_specs += [pl.BlockSpec((None, tm, c), own_idx) for _ in own_grads]
    operands = [zone] * N_CHIPS + list(own_grads)
    if after is not None:
        in_specs.append(HBM_SPEC)
        operands.append(after)

    def body(chip_ref, *refs):
        slot_refs, own_refs, o_ref = refs[:N_CHIPS], refs[N_CHIPS:N_CHIPS + layers], refs[-1]
        own = own_refs[0][...]
        for u in range(1, layers):
            own = jnp.where(pl.program_id(0) == u, own_refs[u][...], own)
        acc = None
        for k in range(N_CHIPS):
            term = jnp.where(chip_ref[0] == k, own, slot_refs[k][...]).astype(F32)
            acc = term if acc is None else acc + term
        o_ref[...] = acc.astype(o_ref.dtype)

    return pl.pallas_call(
        body, name=name,
        grid_spec=pltpu.PrefetchScalarGridSpec(
            num_scalar_prefetch=1, grid=(layers, nb), in_specs=in_specs,
            out_specs=pl.BlockSpec((tm, c), lambda l, i, chip_ref: (l * nb + i, 0))),
        out_shape=jax.ShapeDtypeStruct((layers * r, c), BF16),
        compiler_params=_params(("arbitrary", "arbitrary")),
    )(chip, *operands)


class _SiblingSwap:
    def __init__(self, arrays, name, collective_id, after=None):
        self.name, self.n = name, len(arrays)
        n = self.n
        n_in = n + (after is not None)
        sem_shape = pltpu.SemaphoreType.DMA((n,))

        def body(*refs):
            ins, send_sems, recv_sems = refs[:n], refs[n_in], refs[n_in + 1]
            theirs, token = refs[n_in + 2 + n:n_in + 2 + 2 * n], refs[-1]
            _sibling_handshake()
            for cp in self._copies(ins, theirs, send_sems, recv_sems):
                cp.start()
            token[...] = jnp.zeros(token.shape, token.dtype)

        operands, in_specs = [_in_hbm(a) for a in arrays], [HBM_ONLY] * n
        if after is not None:
            operands.append(after)
            in_specs.append(HBM_SPEC)
        outs = pl.pallas_call(
            body, name=name + "_start",
            in_specs=in_specs,
            out_specs=[SEM_SPEC, SEM_SPEC] + [HBM_ONLY] * (2 * n) + [pl.BlockSpec(memory_space=pltpu.VMEM)],
            out_shape=[sem_shape, sem_shape] + [pltpu.HBM(a.shape, a.dtype) for a in arrays] * 2
                      + [jax.ShapeDtypeStruct((8, LANES), F32)],
            input_output_aliases={i: i + 2 for i in range(n)},
            compiler_params=pltpu.CompilerParams(has_side_effects=SIDE_EFFECT, collective_id=collective_id),
        )(*operands)
        self.send_sems, self.recv_sems = outs[0], outs[1]
        self.mine, self.theirs, self.token = list(outs[2:2 + n]), list(outs[2 + n:2 + 2 * n]), outs[-1]

    def _copies(self, mine, theirs, send_sems, recv_sems):
        sibling = (lax.axis_index("x"), lax.axis_index("y"), 1 - lax.axis_index("c"))
        return [pltpu.make_async_remote_copy(src_ref=mine[a], dst_ref=theirs[a], send_sem=send_sems.at[a],
                                             recv_sem=recv_sems.at[a], device_id=sibling, device_id_type=MESH)
                for a in range(self.n)]

    def wait(self, after):
        n = self.n

        def body(*refs):
            for cp in self._copies(refs[:n], refs[n:2 * n], refs[2 * n], refs[2 * n + 1]):
                cp.wait_send()
                cp.wait_recv()

        arrays = self.mine + self.theirs
        outs = pl.pallas_call(
            body, name=self.name + "_wait",
            in_specs=[HBM_ONLY] * (2 * n) + [SEM_SPEC, SEM_SPEC, HBM_SPEC], out_specs=[HBM_ONLY] * (2 * n),
            out_shape=[pltpu.HBM(a.shape, a.dtype) for a in arrays],
            input_output_aliases={i: i for i in range(2 * n)},
            compiler_params=pltpu.CompilerParams(has_side_effects=SIDE_EFFECT),
        )(*[_in_hbm(a) for a in arrays], self.send_sems, self.recv_sems, after)
        return list(outs[:n]), list(outs[n:])


class _GatherDevices:
    def __init__(self, vec):
        sem_shape = pltpu.SemaphoreType.DMA((N_DEV,))

        def body(in_ref, send_sems, recv_sems, vec_ref, out_ref, token):
            for cp in self._copies(in_ref, out_ref, send_sems, recv_sems):
                cp.start()
            token[...] = jnp.zeros(token.shape, token.dtype)

        outs = pl.pallas_call(
            body, name="gather_small_start",
            in_specs=[HBM_ONLY],
            out_specs=[SEM_SPEC, SEM_SPEC, HBM_ONLY, HBM_ONLY, pl.BlockSpec(memory_space=pltpu.VMEM)],
            out_shape=[sem_shape, sem_shape, pltpu.HBM(vec.shape, vec.dtype),
                       pltpu.HBM((N_DEV,) + vec.shape, vec.dtype), jax.ShapeDtypeStruct((8, LANES), F32)],
            input_output_aliases={0: 2},
            compiler_params=pltpu.CompilerParams(has_side_effects=SIDE_EFFECT),
        )(_in_hbm(vec))
        self.send_sems, self.recv_sems, self.vec, self.rows, self.token = outs

    def _copies(self, in_ref, out_ref, send_sems, recv_sems):
        x, y, c = lax.axis_index("x"), lax.axis_index("y"), lax.axis_index("c")
        me = 4 * x + 2 * y + c
        copies = [pltpu.make_async_copy(in_ref, out_ref.at[me], recv_sems.at[0])]
        for rel in range(1, N_DEV):
            peer = (x ^ (rel >> 2), y ^ ((rel >> 1) & 1), c ^ (rel & 1))
            copies.append(pltpu.make_async_remote_copy(
                src_ref=in_ref, dst_ref=out_ref.at[me], send_sem=send_sems.at[rel], recv_sem=recv_sems.at[rel],
                device_id=peer, device_id_type=MESH))
        return copies

    def wait(self, after):
        def body(vec_ref, rows_ref, send_sems, recv_sems, after_ref, vec_out, rows_out):
            copies = self._copies(vec_ref, rows_ref, send_sems, recv_sems)
            copies[0].wait()
            for cp in copies[1:]:
                cp.wait_send()
                cp.wait_recv()

        outs = pl.pallas_call(
            body, name="gather_small_wait",
            in_specs=[HBM_ONLY, HBM_ONLY, SEM_SPEC, SEM_SPEC, HBM_SPEC], out_specs=[HBM_ONLY, HBM_ONLY],
            out_shape=[pltpu.HBM(self.vec.shape, self.vec.dtype), pltpu.HBM(self.rows.shape, self.rows.dtype)],
            input_output_aliases={0: 0, 1: 1},
            compiler_params=pltpu.CompilerParams(has_side_effects=SIDE_EFFECT),
        )(_in_hbm(self.vec), _in_hbm(self.rows), self.send_sems, self.recv_sems, after)
        return outs[1]


BIG = [("a_w_in", "col"), ("a_w_out", "row"), ("kv_w", "row"), ("b_w_q", "row"), ("b_w_out", "row"),
       ("ffn_w_gate_up", "colp"), ("ffn_w_down", "row"), ("ple_w_up", "col"), ("ple_w_gate", "row")]
GATHER_GROUPS = [[("a_w_in", 0), ("small", 0)], [("a_w_out", 0)],
                 [("ffn_w_gate_up", 0), ("ffn_w_down", 0), ("ple_w_gate", 0), ("ple_w_up", 0)],
                 [("kv_w", 0), ("b_w_q", 0), ("b_w_out", 0)],
                 [("ffn_w_gate_up", 1)], [("ffn_w_down", 1), ("ple_w_gate", 1), ("ple_w_up", 1)]]
SCATTER_GROUPS = [[("ple_w_gate", 1), ("ple_w_up", 1), ("ffn_w_down", 1)], [("ffn_w_gate_up", 1)],
                  [("b_w_out", 0), ("b_w_q", 0), ("kv_w", 0)], [("ple_w_gate", 0), ("ple_w_up", 0), ("ffn_w_down", 0)],
                  [("ffn_w_gate_up", 0), ("a_w_out", 0)], [("a_w_in", 0)]]
COLLECTIVE_IDS = {"fill": 0, "swap": 6, "gather": 9, "scatter": 15}
SMALL_SHARDED = ["ln_gain", "ln_bias", "a_lower_bound"]
SMALL_REPLICATED = ["a_norm_gain", "kv_b", "b_b_q", "b_sinks", "b_b_out", "ple_b_gate"]
WEIGHT_ORDER = ["a_w_in", "a_lower_bound", "a_norm_gain", "a_w_out", "kv_w", "kv_b", "b_w_q", "b_b_q", "b_sinks",
                "b_w_out", "b_b_out", "ffn_w_gate_up", "ffn_w_down", "ple_w_up", "ple_w_gate", "ple_b_gate",
                "ln_gain", "ln_bias"]


def _as3(a):
    return a.reshape((-1,) + a.shape[-2:]) if a.ndim >= 3 else a.reshape((1,) + a.shape)


def _pad_lanes(v):
    n = v.shape[-1]
    return jnp.pad(v, ((0, 0), (0, (-n) % LANES)))


ADAM_MANY_STEPS = 4
ADAM_MANY_MAX = 1 << 19


def _adam_many(groups, name):
    in_specs, out_specs, out_shapes, arrays = [], [], [], []
    for group in groups:
        r, c = group[0].shape
        block = pl.BlockSpec((r // ADAM_MANY_STEPS, c), lambda i: (i, 0))
        in_specs += [block] * len(group)
        arrays += list(group)
        out_specs += [block] * 4
        out_shapes += [jax.ShapeDtypeStruct((r, c), F32)] * 4

    def body(*refs):
        ins, outs = refs[:len(arrays)], refs[len(arrays):]
        for k in range(len(groups)):
            res = _adam_fn(*[ref[...] for ref in ins[5 * k:5 * k + 5]])
            for out_ref, val in zip(outs[4 * k:4 * k + 4], res):
                out_ref[...] = val

    result = pl.pallas_call(
        body, name=name, grid=(ADAM_MANY_STEPS,), in_specs=in_specs, out_specs=out_specs, out_shape=out_shapes,
        compiler_params=_params(("parallel",)),
    )(*arrays)
    return [result[4 * k:4 * k + 4] for k in range(len(groups))]


def _adam_small(everyone, chip, items, loss_off):
    n_items = len(items)

    def body(chip_ref, every_ref, *refs):
        ins, outs = refs[:3 * n_items], refs[3 * n_items:]

        def total(off, width):
            acc = every_ref[0, :, off:off + width]
            for s in range(1, N_DEV):
                acc = acc + every_ref[s, :, off:off + width]
            return acc

        for a, (w, _, _, off, sharded) in enumerate(items):
            cols = w.shape[-1]
            for r in range(w.size // cols):
                at = (slice(r, r + 1),) if w.ndim == 2 else (r // w.shape[1], slice(r % w.shape[1], r % w.shape[1] + 1))
                if sharded:
                    full = total(off + r * N_CHIPS * cols, N_CHIPS * cols)
                    g = full[:, 0:cols]
                    for c in range(1, N_CHIPS):
                        g = jnp.where(chip_ref[0] == c, full[:, c * cols:(c + 1) * cols], g)
                else:
                    g = total(off + r * cols, cols)
                w_ref, m_ref, v_ref = ins[3 * a:3 * a + 3]
                res = _adam_fn(w_ref[at], m_ref[at], v_ref[at], g, jnp.zeros_like(g))
                for out_ref, val in zip(outs[4 * a:4 * a + 4], res):
                    out_ref[at] = val
        outs[-1][...] = total(loss_off, LANES)

    def whole(shape):
        return pl.BlockSpec(tuple(shape), lambda i, chip_ref: (0,) * len(shape))

    arrays = [arr for it in items for arr in it[:3]]
    out_shapes = [jax.ShapeDtypeStruct(it[0].shape, F32) for it in items for _ in range(4)]
    out_shapes.append(jax.ShapeDtypeStruct((1, LANES), F32))
    result = pl.pallas_call(
        body, name="adam_small",
        grid_spec=pltpu.PrefetchScalarGridSpec(
            num_scalar_prefetch=1, grid=(1,),
            in_specs=[whole(everyone.shape)] + [whole(arr.shape) for arr in arrays],
            out_specs=[whole(s.shape) for s in out_shapes]),
        out_shape=out_shapes,
        compiler_params=_params(("arbitrary",)),
    )(chip, everyone, *arrays)
    return [result[4 * a:4 * a + 4] for a in range(n_items)], result[-1]


def kernel(x, p, a_w_in, a_lower_bound, a_norm_gain, a_w_out, kv_w, kv_b, b_w_q, b_b_q, b_sinks, b_w_out, b_b_out, ffn_w_gate_up, ffn_w_down, ple_w_up, ple_w_gate, ple_b_gate, ln_gain, ln_bias, loss_target, m_a_w_in, m_a_lower_bound, m_a_norm_gain, m_a_w_out, m_kv_w, m_kv_b, m_b_w_q, m_b_b_q, m_b_sinks, m_b_w_out, m_b_b_out, m_ffn_w_gate_up, m_ffn_w_down, m_ple_w_up, m_ple_w_gate, m_ple_b_gate, m_ln_gain, m_ln_bias, v_a_w_in, v_a_lower_bound, v_a_norm_gain, v_a_w_out, v_kv_w, v_kv_b, v_b_w_q, v_b_b_q, v_b_sinks, v_b_w_out, v_b_b_out, v_ffn_w_gate_up, v_ffn_w_down, v_ple_w_up, v_ple_w_gate, v_ple_b_gate, v_ln_gain, v_ln_bias):
    args = dict(locals())
    wts = {n: args[n] for n in WEIGHT_ORDER}
    mom = {n: args["m_" + n] for n in WEIGHT_ORDER}
    vel = {n: args["v_" + n] for n in WEIGHT_ORDER}
    chip = 2 * lax.axis_index("x") + lax.axis_index("y")
    d = x.shape[-1]
    dq = d // N_CHIPS

    kind_of = dict(BIG)
    kind_of["small"] = "col"
    chip_arr = chip.reshape(1).astype(jnp.int32)
    small_pack = jnp.concatenate([wts[n].reshape(-1, dq) for n in SMALL_SHARDED], axis=0)[None]

    def place_item(key):
        n, layer = key
        if n == "small":
            return small_pack, 0, "col", F32
        return _as3(wts[n]), layer, kind_of[n], BF16

    gathers, where = [], {}
    for gi, group in enumerate(GATHER_GROUPS):
        prev = gathers[-1].token if gathers else None
        placed = _place([place_item(k) for k in group], chip_arr, name=f"place{gi}", after=prev)
        gathers.append(_Exchange("gather", [], placed, [kind_of[k[0]] for k in group],
                                 [0] * len(group), f"gather{gi}", COLLECTIVE_IDS["gather"] + gi, after=prev,
                                 halves=[k[0] != "small" for k in group]))
        for k in group:
            where[k] = gi
    all_started = gathers[-1].token
    ready = {}

    fills = {}

    def pass_on(gi, after):
        if gi not in fills:
            group = GATHER_GROUPS[gi]
            outs = gathers[gi].wait(after)[1]
            split = [i for i, k in enumerate(group) if k[0] != "small"]
            fills[gi] = (outs, split, _SiblingFill([outs[i] for i in split], [kind_of[group[i][0]] for i in split],
                                                   f"fill{gi}", COLLECTIVE_IDS["fill"] + gi))

    def wget(name, layer, after):
        key = (name, layer)
        if key not in ready:
            gi = where[key]
            after = all_started if gi == 0 else after
            pass_on(gi, after)
            if 1 <= gi < len(GATHER_GROUPS) - 1:
                pass_on(gi + 1, after)
                after = fills[gi + 1][2].token
            outs, split, fill = fills[gi]
            for i, arr in zip(split, fill.wait(after)):
                outs[i] = arr
            for k, arr in zip(GATHER_GROUPS[gi], outs):
                ready[k] = arr
        return ready[key]

    small_full = wget("small", 0, None)[0]
    ln_gain_f = small_full[0:6].reshape(DEPTH, 3, d)
    ln_bias_f = small_full[6:12].reshape(DEPTH, 3, d)
    alb_f = small_full[12:14]

    group_of = {k: gi for gi, group in enumerate(SCATTER_GROUPS) for k in group}
    grads_done, zones, scatters = {}, {}, []

    def grad_sink(name, layer, grad):
        grads_done[(name, layer)] = grad
        if name not in zones:
            zones[name] = lax.empty((N_CHIPS,) + _as3(wts[name]).shape, BF16)
        gi = group_of[(name, layer)]
        group = SCATTER_GROUPS[gi]
        if not all(k in grads_done for k in group):
            return None
        ex = _Exchange("scatter", [grads_done[k] for k in group], [zones[k[0]] for k in group],
                       [kind_of[k[0]] for k in group], [k[1] for k in group], f"scatter{gi}",
                       COLLECTIVE_IDS["scatter"] + gi)
        for k, zone in zip(group, ex.lands):
            zones[k[0]] = zone
        scatters.append((ex, group))
        return ex.token

    small = {}

    def small_sink(loss, gs):
        ln_g = jnp.concatenate([gs[f"ln_gain_{i}_{j}"] for i in range(DEPTH) for j in range(3)], axis=0)
        ln_b = jnp.concatenate([gs[f"ln_bias_{i}_{j}"] for i in range(DEPTH) for j in range(3)], axis=0)
        ple_bg = jnp.concatenate([gs[f"ple_b_{i}"] for i in range(DEPTH)], axis=0)
        small["list"] = [ln_g.reshape(1, -1), ln_b.reshape(1, -1), gs["alb"].reshape(1, -1), gs["norm_gain"],
                         gs["kv_b"], gs["b_q"], _pad_lanes(gs["sinks"]), gs["b_out"], ple_bg.reshape(1, -1), loss]
        small["gather"] = _GatherDevices(jnp.concatenate(small["list"], axis=1))
        return small["gather"].token

    loss, grad_x, gs = _local_step(
        x[0], p.reshape((p.shape[0],) + p.shape[2:]), loss_target[0], wget, grad_sink, ln_gain_f, ln_bias_f, alb_f, a_norm_gain, kv_b, b_b_q,
        b_sinks, b_b_out, ple_b_gate, small_sink)

    res = {}

    def arrive(batch, after):
        for ex, group in batch:
            srcs, outs = ex.wait(after, lands=[zones[k[0]] for k in group])
            for k, grad, zone in zip(group, srcs, outs):
                grads_done[k], zones[k[0]] = grad, zone

    def half_sums(names, batch, after):
        partial = []
        for n in names:
            own = [grads_done[(n, layer)] for layer in range(zones[n].shape[1])]
            partial.append(_sum_arrivals(zones[n], own, kind_of[n], chip_arr, f"sum_{n}", after=after))
        return _SiblingSwap(partial, f"swap{batch}", COLLECTIVE_IDS["swap"] + batch, after=after)

    def update(names, swap, after):
        flat = lambda a: a.reshape(-1, a.shape[-1])
        work = [(n, [flat(wts[n]), flat(mom[n]), flat(vel[n]), own, sib]) for n, own, sib in zip(names, *swap.wait(after))]
        many = [(n, ops) for n, ops in work if wts[n].size <= ADAM_MANY_MAX]
        if len(many) > 1:
            for (n, _), out in zip(many, _adam_many([ops for _, ops in many], f"adam_from_{many[0][0]}")):
                res[n] = [o.reshape(wts[n].shape) for o in out]
        for n, ops in work:
            if n not in res:
                out = _rowwise(_adam_fn, ops, [], [(ops[3].shape, F32)] * 4, name=f"adam_{n}")
                res[n] = [o.reshape(wts[n].shape) for o in out]
        return res[names[-1]][1]

    last_names = [k[0] for k in SCATTER_GROUPS[-1]]
    batches = [["ffn_w_gate_up"], [n for n, _ in BIG if n != "ffn_w_gate_up" and n not in last_names], last_names]
    arrive(scatters[:-1], grad_x)
    swap0 = half_sums(batches[0], 0, None)
    swap1 = half_sums(batches[1], 1, swap0.token)
    updated = update(batches[0], swap0, swap1.token)
    arrive(scatters[-1:], updated)
    swap2 = half_sums(batches[2], 2, swap1.token)
    updated = update(batches[1], swap1, swap2.token)
    update(batches[2], swap2, updated)

    everyone = small["gather"].wait(grad_x)
    offs, pos = [], 0
    for v in small["list"]:
        offs.append(pos)
        pos += v.shape[1]
    names = ["ln_gain", "ln_bias", "a_lower_bound", "a_norm_gain", "kv_b", "b_b_q", "b_sinks", "b_b_out", "ple_b_gate"]
    as_rows = lambda a: a.reshape(1, -1) if a.ndim == 1 else a
    items = [(as_rows(wts[n]), as_rows(mom[n]), as_rows(vel[n]), off, n in SMALL_SHARDED)
             for n, off in zip(names, offs)]
    updates, loss_row = _adam_small(everyone, chip_arr, items, offs[len(names)])
    for n, upd in zip(names, updates):
        res[n] = [u.reshape(wts[n].shape) for u in upd]

    outs = [loss_row[0, 0], grad_x[None]]
    for k in range(4):
        outs += [res[n][k] for n in WEIGHT_ORDER]
    return tuple(outs)
```

```python
import functools

import jax
import jax.numpy as jnp
from jax import lax
from jax.experimental import pallas as pl
from jax.experimental.pallas import tpu as pltpu

F32 = jnp.float32
BF16 = jnp.bfloat16
MESH = pl.DeviceIdType.MESH

LANES = 128
HG_DK = 128
HG_CHUNK = 64
HG_SUB = 16
HG_ROWS = 512
HG_HEADS_PER_STEP = 2
LOG2_E = 1.4426950408889634
ATT_HD = 64
ATT_G = 4
WINDOW = 128
DEPTH = 2
ALPHA = (2.0 * DEPTH) ** 0.25
LN_EPS = 1e-5
RMS_EPS = 1e-6
ADAM_LR, ADAM_B1, ADAM_B2, ADAM_EPS, ADAM_WD, ADAM_STEP = 0.001, 0.9, 0.999, 1e-08, 0.01, 10
N_CHIPS = 4
N_DEV = 8
VMEM_LIMIT = 56 * 1024 * 1024
NEG = -1e30


def _pick(n, cap):
    best = None
    for d in range(LANES, min(n, cap) + 1, LANES):
        if n % d == 0:
            best = d
    return n if best is None else best


def _pick_rows(m, cap):
    best = None
    for d in range(16, min(m, cap) + 1, 16):
        if m % d == 0:
            best = d
    return m if best is None else best


def _params(sem):
    return pltpu.CompilerParams(dimension_semantics=sem, vmem_limit_bytes=VMEM_LIMIT)


def _zeros_index(ndim, grid_rank=3):
    return (lambda i, j, kk: (0,) * ndim) if grid_rank == 3 else (lambda kk, i: (0,) * ndim)


def _mm(a, b, *, name, la=None, lb=None, ta=False, tb=False, bias=None, add=None, out_dtype=F32,
        out_layers=None, out_layer=None, after=None, post=None, tile_cols=None, caps=(1024, 1536, 2048),
        a_parts=None, b_parts=None):
    ar, ac = a.shape[-2:]
    br, bc = b.shape[-2:]
    assert a_parts is None or (not ta and la is None and a.shape[0] == a_parts)
    assert b_parts is None or (not tb and lb is None and b.shape[0] == b_parts)
    m, k = (ac, ar) if ta else (ar, ac * (a_parts or 1))
    k2, n = (bc, br) if tb else (br, bc * (b_parts or 1))
    assert k == k2, (a.shape, b.shape, ta, tb)
    if post is not None:
        caps = (512, n if tile_cols is None else tile_cols, caps[2])
    tm, tn, tk = _pick(m, caps[0]), _pick(bc if b_parts else n, caps[1]), _pick(ac if a_parts else k, caps[2])
    assert post is None or tn == caps[1]
    nk = k // tk
    gi, gj = m // tm, n // tn
    a_bytes, b_bytes = m * k * a.dtype.itemsize, k * n * b.dtype.itemsize
    rows_outer = (a_bytes + b_bytes * (gi if gj * nk > 1 else 1)) <= (b_bytes + a_bytes * (gj if gi * nk > 1 else 1))
    k_outer = post is not None and nk > 1 and gj == 1
    grid = (nk, gi) if k_outer else (gi, gj, nk) if rows_outer else (gj, gi, nk)
    keep_at = ta and nk == 1 and gj > 1 and rows_outer

    def bs(block, idx, late=False):
        if k_outer:
            return pl.BlockSpec(block, lambda kk, i: idx(jnp.where(kk == nk - 1, i, 0) if late else i, 0, kk))
        return pl.BlockSpec(block, idx if rows_outer else (lambda q, p, kk: idx(p, q, kk)))

    def spec(block, idx, layer):
        if layer is None:
            return bs(block, idx)
        return bs((None,) + block, lambda i, j, kk: (layer,) + idx(i, j, kk))

    a_spec = spec((tk, tm), lambda i, j, kk: (kk, i), la) if ta else spec((tm, tk), lambda i, j, kk: (i, kk), la)
    b_spec = spec((tn, tk), lambda i, j, kk: (j, kk), lb) if tb else spec((tk, tn), lambda i, j, kk: (kk, j), lb)
    if a_parts:
        a_spec = bs((None, tm, tk), lambda i, j, kk: (kk // (ac // tk), i, kk % (ac // tk)))
    if b_parts:
        b_spec = bs((None, tk, tn), lambda i, j, kk: (j // (bc // tn), kk, j % (bc // tn)))
    in_specs, operands = [a_spec, b_spec], [a, b]
    if bias is not None:
        in_specs.append(bs((1, tn), lambda i, j, kk: (0, j)))
        operands.append(bias)
    if add is not None:
        in_specs.append(bs((tm, tn), lambda i, j, kk: (i, j), late=True))
        operands.append(add)
    if after is not None:
        in_specs.append(pl.BlockSpec(memory_space=pl.ANY))
        operands.append(after)
    dims = (((0 if ta else 1,), (1 if tb else 0,)), ((), ()))
    has_bias, has_add = bias is not None, add is not None
    if post is None:
        fn, rows, whole, outs, sums = None, [], [], [], []
        out_shape = jax.ShapeDtypeStruct((m, n) if out_layers is None else (out_layers, m, n), out_dtype)
        out_specs = spec((tm, tn), lambda i, j, kk: (i, j), out_layer)
    else:
        fn, rows, whole, outs, sums = post
        in_specs += [bs((tm, r.shape[-1] // gj), lambda i, j, kk: (i, j), late=True) for r in rows]
        in_specs += [pl.BlockSpec(tuple(w.shape), _zeros_index(w.ndim, len(grid))) for w in whole]
        operands += list(rows) + list(whole)
        out_shape = [jax.ShapeDtypeStruct(sh, dt) for sh, dt in list(outs) + list(sums)]
        out_specs = ([bs((tm, sh[-1] // gj), lambda i, j, kk: (i, j), late=True) for sh, _ in outs]
                     + [pl.BlockSpec(tuple(sh), _zeros_index(len(sh), len(grid))) for sh, _ in sums])
    n_in, n_extra, n_outs, n_sums = len(operands), len(rows) + len(whole), len(outs), len(sums)

    def body(*refs):
        a_ref, b_ref = refs[0], refs[1]
        pos = 2
        bias_ref = add_ref = None
        if has_bias:
            bias_ref = refs[pos]
            pos += 1
        if has_add:
            add_ref = refs[pos]
            pos += 1
        extra_refs = refs[n_in - n_extra:n_in]
        out_refs = refs[n_in:n_in + max(n_outs, 1)]
        sum_refs = refs[n_in + n_outs:n_in + n_outs + n_sums]
        acc_ref = refs[-1] if nk > 1 else None
        if keep_at:
            at_ref = refs[-1]

            @pl.when(pl.program_id(1) == 0)
            def _():
                at_ref[...] = a_ref[...].astype(BF16).T

            part = lax.dot_general(at_ref[...], b_ref[...].astype(BF16), (((1,), (1 if tb else 0,)), ((), ())),
                                   preferred_element_type=F32)
        else:
            part = lax.dot_general(a_ref[...].astype(BF16), b_ref[...].astype(BF16), dims,
                                   preferred_element_type=F32)

        def finish(total):
            if has_bias:
                total = total + bias_ref[...]
            if has_add:
                total = total + add_ref[...]
            if fn is None:
                out_refs[0][...] = total.astype(out_refs[0].dtype)
                return
            res = fn(total, *[r[...] for r in extra_refs])
            for ref, val in zip(out_refs, res[:n_outs]):
                ref[...] = val.astype(ref.dtype)
            if n_sums:
                @pl.when(pl.program_id(1 if k_outer or not rows_outer else 0) == 0)
                def _():
                    for ref in sum_refs:
                        ref[...] = jnp.zeros(ref.shape, ref.dtype)

                for ref, val in zip(sum_refs, res[n_outs:]):
                    ref[...] += val

        if nk == 1:
            finish(part)
        elif k_outer:
            kk = pl.program_id(0)
            rows_i = pl.ds(pl.multiple_of(pl.program_id(1) * tm, tm), tm)

            @pl.when(kk == 0)
            def _():
                acc_ref[rows_i, :] = part

            @pl.when(kk > 0)
            def _():
                acc_ref[rows_i, :] += part

            @pl.when(kk == nk - 1)
            def _():
                finish(acc_ref[rows_i, :])
        else:
            kk = pl.program_id(2)

            @pl.when(kk == 0)
            def _():
                acc_ref[...] = part

            @pl.when(kk > 0)
            def _():
                acc_ref[...] += part

            @pl.when(kk == nk - 1)
            def _():
                finish(acc_ref[...])

    return pl.pallas_call(
        body, name=name, grid=grid, in_specs=in_specs, out_specs=out_specs, out_shape=out_shape,
        scratch_shapes=([pltpu.VMEM((m, n) if k_outer else (tm, tn), F32)] if nk > 1
                        else [pltpu.VMEM((tm, tk), BF16)] if keep_at else []),
        compiler_params=_params(("arbitrary", "arbitrary") if k_outer
                                else ("arbitrary" if n_sums else "parallel", "arbitrary" if keep_at else "parallel",
                                      "arbitrary") if rows_outer
                                else ("parallel", "arbitrary" if n_sums else "parallel", "arbitrary")),
    )(*operands)


def _rowwise(fn, rows, whole, outs, sums=(), *, name, tm=256):
    m = rows[0].shape[-2]
    tm = _pick_rows(m, tm)
    n_rows, n_whole, n_outs, n_sums = len(rows), len(whole), len(outs), len(sums)

    def rspec(shape):
        lead = len(shape) - 2
        return pl.BlockSpec(tuple(shape[:-2]) + (tm, shape[-1]), lambda i: (0,) * lead + (i, 0))

    def wspec(shape):
        return pl.BlockSpec(tuple(shape), lambda i: (0,) * len(shape))

    def body(*refs):
        vals = [r[...] for r in refs[:n_rows + n_whole]]
        out_refs = refs[n_rows + n_whole:n_rows + n_whole + n_outs]
        sum_refs = refs[n_rows + n_whole + n_outs:]
        res = fn(*vals)
        for ref, val in zip(out_refs, res[:n_outs]):
            ref[...] = val.astype(ref.dtype)
        if n_sums:
            @pl.when(pl.program_id(0) == 0)
            def _():
                for ref in sum_refs:
                    ref[...] = jnp.zeros(ref.shape, ref.dtype)

            for ref, val in zip(sum_refs, res[n_outs:]):
                ref[...] += val

    result = pl.pallas_call(
        body, name=name, grid=(m // tm,),
        in_specs=[rspec(r.shape) for r in rows] + [wspec(w.shape) for w in whole],
        out_specs=[rspec(s) for s, _ in outs] + [wspec(s) for s, _ in sums],
        out_shape=[jax.ShapeDtypeStruct(s, d) for s, d in list(outs) + list(sums)],
        compiler_params=_params(("arbitrary",)),
    )(*rows, *whole)
    return result


def _sigmoid(v):
    return jax.nn.sigmoid(v)


def _col_sum(v):
    return jnp.sum(v, axis=0, keepdims=True)


def _ln_stats(z):
    mu = jnp.mean(z, axis=-1, keepdims=True)
    zc = z - mu
    var = jnp.mean(zc * zc, axis=-1, keepdims=True)
    rstd = lax.rsqrt(var + LN_EPS)
    return zc * rstd, rstd


def _ln_fwd_fn(xin, h, gain, bias):
    xhat, _ = _ln_stats(ALPHA * xin + h)
    y = xhat * gain + bias
    return y, y


def _ple_ln_fwd_fn(xin, pg, pu, gain, bias):
    xhat, _ = _ln_stats(ALPHA * xin + _sigmoid(pg) * pu)
    y = xhat * gain + bias
    return y, y


def _ln_dz(dy, z, gain):
    xhat, rstd = _ln_stats(z)
    dxhat = dy * gain
    dz = rstd * (dxhat - jnp.mean(dxhat, axis=-1, keepdims=True)
                 - xhat * jnp.mean(dxhat * xhat, axis=-1, keepdims=True))
    return dz, _col_sum(dy * xhat), _col_sum(dy)


def _ln_bwd_fn(dy, xin, h, gain):
    dz, dgain, dbias = _ln_dz(dy, ALPHA * xin + h, gain)
    return ALPHA * dz, dz, dgain, dbias, _col_sum(dz)


def _ple_ln_bwd_fn(dy, xin, pg, pu, gain):
    sg = _sigmoid(pg)
    dz, dgain, dbias = _ln_dz(dy, ALPHA * xin + sg * pu, gain)
    dpg = dz * pu * sg * (1.0 - sg)
    return ALPHA * dz, dpg, dz * sg, dgain, dbias, _col_sum(dpg)


def _swiglu_fwd_fn(gu):
    hid = gu.shape[-1] // 2
    gate, up = gu[:, :hid], gu[:, hid:]
    return gu, gate * _sigmoid(gate) * up


def _swiglu_bwd_fn(dact, gu):
    gu = gu.astype(F32)
    hid = gu.shape[-1] // 2
    gate, up = gu[:, :hid], gu[:, hid:]
    sg = _sigmoid(gate)
    dgate = dact * up * sg * (1.0 + gate * (1.0 - sg))
    dup = dact * gate * sg
    return (jnp.concatenate([dgate, dup], axis=-1),)


def _loss_fn(y, target):
    err = y - target
    inv = 1.0 / y.shape[-1]
    part = 0.5 * inv * jnp.sum(jnp.sum(err * err, axis=-1, keepdims=True), axis=0, keepdims=True)
    return err * inv, jnp.broadcast_to(part, (1, LANES))


def _adam_fn(w, mom, vel, p_own, p_sib):
    g = p_own.astype(F32) + p_sib.astype(F32)
    m_new = ADAM_B1 * mom + (1.0 - ADAM_B1) * g
    v_new = ADAM_B2 * vel + (1.0 - ADAM_B2) * (g * g)
    m_hat = m_new / (1.0 - ADAM_B1 ** ADAM_STEP)
    v_hat = v_new / (1.0 - ADAM_B2 ** ADAM_STEP)
    delta = -ADAM_LR * (m_hat / (jnp.sqrt(v_hat) + ADAM_EPS) + ADAM_WD * w)
    return g, delta, m_new, v_new


def _split2(x):
    hi = x.astype(BF16)
    return hi, (x - hi.astype(F32)).astype(BF16)


def _dot3(a, b, dims):
    a_hi, a_lo = _split2(a)
    b_hi, b_lo = _split2(b)
    dn = (dims, ((), ()))
    return (lax.dot_general(a_hi, b_hi, dn, preferred_element_type=F32)
            + (lax.dot_general(a_hi, b_lo, dn, preferred_element_type=F32)
               + lax.dot_general(a_lo, b_hi, dn, preferred_element_type=F32)))


def _tdot(mask01, b):
    m = mask01.astype(BF16)
    b_hi = b.astype(BF16)
    rest = b - b_hi.astype(F32)
    b_mid = rest.astype(BF16)
    b_lo = (rest - b_mid.astype(F32)).astype(BF16)
    dn = (((1,), (0,)), ((), ()))
    return (lax.dot_general(m, b_hi, dn, preferred_element_type=F32)
            + (lax.dot_general(m, b_mid, dn, preferred_element_type=F32)
               + lax.dot_general(m, b_lo, dn, preferred_element_type=F32)))


def _hdot(a, b):
    return _dot3(a, b, ((1,), (0,)))


def _hdot_nt(a, b):
    return _dot3(a, b, ((1,), (1,)))


def _hdot_tn(a, b):
    return _dot3(a, b, ((0,), (0,)))


def _dot(a, b):
    return lax.dot_general(a.astype(BF16), b.astype(BF16), (((1,), (0,)), ((), ())), preferred_element_type=F32)


def _dot_nt(a, b):
    return lax.dot_general(a.astype(BF16), b.astype(BF16), (((1,), (1,)), ((), ())), preferred_element_type=F32)


def _dot_tn(a, b):
    return lax.dot_general(a.astype(BF16), b.astype(BF16), (((0,), (0,)), ((), ())), preferred_element_type=F32)


def _hg_masks():
    c = HG_CHUNK
    row = lax.broadcasted_iota(jnp.int32, (c, c), 0)
    col = lax.broadcasted_iota(jnp.int32, (c, c), 1)
    base = row & (-HG_SUB)
    return row, col, base, col <= row, col < base


def _hg_gates(qr, fr, alb):
    lbound = _sigmoid(alb[0:1, :] - alb[1:2, :])
    sig = _sigmoid(fr)
    forget = lbound + (1.0 - lbound) * sig
    kk = (1.0 - lbound) * _sigmoid(-fr)
    qt = qr * _sigmoid(qr) * (HG_DK ** -0.5)
    return qt, kk, jnp.log(forget), lbound, sig, forget


def _hg_scores(qt, kk, g, scores=True):
    c, nsub = HG_CHUNK, HG_CHUNK // HG_SUB
    row, col, base, causal, below = _hg_masks()
    b = _tdot(causal, g)
    rr = _tdot(below, g)
    bq = b - rr
    qh = qt * jnp.exp(bq)
    edecs = [None]
    parts = [jnp.zeros((HG_SUB, c), F32)]
    for i in range(1, nsub):
        edec = jnp.exp(jnp.minimum(rr[i * HG_SUB:i * HG_SUB + 1, :] - b, 0.0))
        edecs.append(edec)
        if scores:
            parts.append(_dot_nt(qh[i * HG_SUB:(i + 1) * HG_SUB, :], kk * edec))
    q3 = qt.reshape(nsub, HG_SUB, HG_DK)
    if not scores:
        return None, b, bq, qh, edecs, (b.reshape(nsub, HG_SUB, HG_DK), q3, kk.reshape(nsub, HG_SUB, HG_DK))
    a = jnp.where(below, jnp.concatenate(parts, axis=0), 0.0)
    b2 = b * LOG2_E
    b3 = b2.reshape(nsub, HG_SUB, HG_DK)
    c3 = (b2 - jnp.log2(kk)).reshape(nsub, HG_SUB, HG_DK)
    for j in range(HG_SUB):
        ek = jnp.exp2(b3 - c3[:, j:j + 1, :])
        colv = jnp.sum(q3 * ek, axis=-1, keepdims=True).reshape(c, 1)
        a = jnp.where(col == base + j, colv, a)
    a = jnp.where(causal, a, 0.0)
    return a, b, bq, qh, edecs, None


def _hg_norm(o, gr, gain):
    r = lax.rsqrt(jnp.mean(o * o, axis=-1, keepdims=True) + RMS_EPS)
    sg = _sigmoid(gr)
    return o * r * gain, r, sg


def _hgrn2_fwd(proj, alb, gain, *, rb):
    m, d4 = proj.shape
    d = d4 // 4
    heads = d // HG_DK
    hp = HG_HEADS_PER_STEP
    rb = min(rb, m)
    cpb = rb // HG_CHUNK
    nrb = m // rb

    def body(q_ref, f_ref, v_ref, g_ref, alb_ref, gain_ref, o_ref, og_ref, st_ref, a_ref, state):
        @pl.when(pl.program_id(1) == 0)
        def _():
            state[...] = jnp.zeros(state.shape, F32)

        def chunk(ci, carry):
            sl = pl.ds(pl.multiple_of(ci * HG_CHUNK, HG_CHUNK), HG_CHUNK)
            for u in range(hp):
                ln = slice(u * HG_DK, (u + 1) * HG_DK)
                qt, kk, g, _, _, _ = _hg_gates(q_ref[sl, ln], f_ref[sl, ln], alb_ref[:, ln])
                v = v_ref[sl, ln]
                st = state[u]
                st_ref[u, ci] = st
                a, b, _, _, _, _ = _hg_scores(qt, kk, g)
                a_ref[u, ci] = a.astype(a_ref.dtype)
                o = _dot(a, v) + _dot_nt(qt * jnp.exp(b), st)
                b_last = b[HG_CHUNK - 1:HG_CHUNK, :]
                state[u] = st * jnp.exp(b_last) + _hdot_tn(v, kk * jnp.exp(b_last - b))
                o_ref[sl, ln] = o
                n, _, sg = _hg_norm(o, g_ref[sl, ln], gain_ref[...])
                og_ref[sl, ln] = (n * g_ref[sl, ln] * sg).astype(og_ref.dtype)
            return carry

        lax.fori_loop(0, cpb, chunk, 0)

    def col(cidx):
        return pl.BlockSpec((rb, hp * HG_DK), lambda h, r: (r, cidx * (heads // hp) + h))

    return pl.pallas_call(
        body, name="hgrn2_fwd", grid=(heads // hp, nrb),
        in_specs=[col(0), col(1), col(2), col(3),
                  pl.BlockSpec((2, hp * HG_DK), lambda h, r: (0, h)),
                  pl.BlockSpec((1, HG_DK), lambda h, r: (0, 0))],
        out_specs=[pl.BlockSpec((rb, hp * HG_DK), lambda h, r: (r, h)),
                   pl.BlockSpec((rb, hp * HG_DK), lambda h, r: (r, h)),
                   pl.BlockSpec((hp, cpb, HG_DK, HG_DK), lambda h, r: (h, r, 0, 0)),
                   pl.BlockSpec((hp, cpb, HG_CHUNK, HG_CHUNK), lambda h, r: (h, r, 0, 0))],
        out_shape=[jax.ShapeDtypeStruct((m, d), F32), jax.ShapeDtypeStruct((m, d), BF16),
                   jax.ShapeDtypeStruct((heads, m // HG_CHUNK, HG_DK, HG_DK), F32),
                   jax.ShapeDtypeStruct((heads, m // HG_CHUNK, HG_CHUNK, HG_CHUNK), BF16)],
        scratch_shapes=[pltpu.VMEM((hp, HG_DK, HG_DK), F32)],
        compiler_params=_params(("parallel", "arbitrary")),
    )(proj, proj, proj, proj, alb, gain)


def _hgrn2_bwd(proj, o_pre, states, scores, dog, alb, gain, *, rb):
    m, d4 = proj.shape
    d = d4 // 4
    heads = d // HG_DK
    rb = min(rb, m)
    cpb = rb // HG_CHUNK
    nrb = m // rb
    c, nsub = HG_CHUNK, HG_CHUNK // HG_SUB

    def body(q_ref, f_ref, v_ref, g_ref, o_ref, st_ref, a_ref, dog_ref, alb_ref, gain_ref,
             dp_ref, dalb_ref, dgain_ref, dstate, carry_ref):
        first = (pl.program_id(0) == 0) & (pl.program_id(1) == 0)

        @pl.when(first)
        def _():
            dgain_ref[...] = jnp.zeros(dgain_ref.shape, F32)

        @pl.when(pl.program_id(1) == 0)
        def _():
            dstate[...] = jnp.zeros(dstate.shape, F32)
            carry_ref[...] = jnp.zeros(carry_ref.shape, F32)
            dalb_ref[...] = jnp.zeros(dalb_ref.shape, F32)

        row, col, base, causal, below = _hg_masks()
        sub_iota = lax.broadcasted_iota(jnp.int32, (nsub, HG_SUB, HG_DK), 1)
        row_k = lax.broadcasted_iota(jnp.int32, (c, HG_DK), 0)
        upper = col >= row

        def chunk(step, carry):
            ci = cpb - 1 - step
            sl = pl.ds(pl.multiple_of(ci * HG_CHUNK, HG_CHUNK), HG_CHUNK)
            qr, fr, v, gr = q_ref[sl, :], f_ref[sl, :], v_ref[sl, :], g_ref[sl, :]
            qt, kk, g, lbound, sig, forget = _hg_gates(qr, fr, alb_ref[...])
            o = o_ref[sl, :]
            dogv = dog_ref[sl, :]
            gain_v = gain_ref[...]
            n, r, sg = _hg_norm(o, gr, gain_v)
            dgr = dogv * n * sg * (1.0 + gr * (1.0 - sg))
            dn = dogv * gr * sg
            dgain_ref[...] += _col_sum(dn * o * r)
            u = dn * gain_v
            d_o = r * u - o * (r * r * r) * jnp.mean(u * o, axis=-1, keepdims=True)
            st0 = st_ref[ci]
            dst = dstate[...]
            _, b, bq, qh, edecs, (b3, q3, k3) = _hg_scores(qt, kk, g, scores=False)
            a = a_ref[ci]
            eb = jnp.exp(b)
            b_last = b[c - 1:c, :]
            kdl_dec = jnp.exp(b_last - b)
            kdl = kk * kdl_dec
            d_a = jnp.where(causal, _dot_nt(d_o, v), 0.0)
            d_at = _dot_nt(v, d_o)
            dv = _dot_tn(a, d_o) + _dot_nt(kdl, dst)
            dq = eb * _hdot(d_o, st0)
            dk = _hdot(v, dst) * kdl_dec
            d_a_below = jnp.where(below, d_a, 0.0)
            dq_parts = [jnp.zeros((HG_SUB, HG_DK), F32)]
            for i in range(1, nsub):
                lo, hi = i * HG_SUB, (i + 1) * HG_SUB
                dq_parts.append(_hdot(d_a_below[lo:hi, :], kk * edecs[i]))
                gi = _hdot(d_at[:, lo:hi], qh[lo:hi, :])
                dk = dk + jnp.where(row_k < lo, edecs[i] * gi, 0.0)
            dq = dq + jnp.concatenate(dq_parts, axis=0) * jnp.exp(bq)
            dq3 = jnp.zeros((nsub, HG_SUB, HG_DK), F32)
            dk3 = jnp.zeros((nsub, HG_SUB, HG_DK), F32)
            d_diag = jnp.concatenate([d_a[i * HG_SUB:(i + 1) * HG_SUB, i * HG_SUB:(i + 1) * HG_SUB]
                                      for i in range(nsub)], axis=0).reshape(nsub, HG_SUB, HG_SUB)
            for j in range(HG_SUB):
                e = jnp.exp(jnp.minimum(b3 - b3[:, j:j + 1, :], 0.0))
                t1 = d_diag[:, :, j:j + 1] * e
                dq3 = dq3 + t1 * k3[:, j:j + 1, :]
                dk3 = jnp.where(sub_iota == j, jnp.sum(t1 * q3, axis=1, keepdims=True), dk3)
            dq = dq + dq3.reshape(c, HG_DK)
            dk = dk + dk3.reshape(c, HG_DK)
            dstate[...] = dst * jnp.exp(b_last) + _hdot_tn(d_o, qt * eb)
            dglog = _tdot(upper, qt * dq - kk * dk) + carry_ref[...]
            carry_ref[...] = dglog[0:1, :]
            dforget = dglog / forget
            one_m_lb = 1.0 - lbound
            dsig = (dforget - dk) * one_m_lb
            sneg = _sigmoid(-fr)
            dlb = _col_sum(dforget * (1.0 - sig) - dk * sneg)
            dalb0 = dlb * lbound * one_m_lb
            dalb_ref[...] += jnp.concatenate([dalb0, -dalb0], axis=0)
            sq = _sigmoid(qr)
            dp_ref[0, sl, :] = (dq * (HG_DK ** -0.5) * sq * (1.0 + qr * (1.0 - sq))).astype(dp_ref.dtype)
            dp_ref[1, sl, :] = (dsig * sig * (1.0 - sig)).astype(dp_ref.dtype)
            dp_ref[2, sl, :] = dv.astype(dp_ref.dtype)
            dp_ref[3, sl, :] = dgr.astype(dp_ref.dtype)
            return carry

        lax.fori_loop(0, cpb, chunk, 0, unroll=2)

    def rev(r):
        return nrb - 1 - r

    def col(cidx):
        return pl.BlockSpec((rb, HG_DK), lambda h, r: (rev(r), cidx * heads + h))

    def head_rows():
        return pl.BlockSpec((rb, HG_DK), lambda h, r: (rev(r), h))

    return pl.pallas_call(
        body, name="hgrn2_bwd", grid=(heads, nrb),
        in_specs=[col(0), col(1), col(2), col(3), head_rows(),
                  pl.BlockSpec((None, cpb, HG_DK, HG_DK), lambda h, r: (h, rev(r), 0, 0)),
                  pl.BlockSpec((None, cpb, HG_CHUNK, HG_CHUNK), lambda h, r: (h, rev(r), 0, 0)),
                  head_rows(),
                  pl.BlockSpec((2, HG_DK), lambda h, r: (0, h)),
                  pl.BlockSpec((1, HG_DK), lambda h, r: (0, 0))],
        out_specs=[pl.BlockSpec((4, rb, HG_DK), lambda h, r: (0, rev(r), h)),
                   pl.BlockSpec((2, HG_DK), lambda h, r: (0, h)),
                   pl.BlockSpec((1, HG_DK), lambda h, r: (0, 0))],
        out_shape=[jax.ShapeDtypeStruct((4, m, d), BF16), jax.ShapeDtypeStruct((2, d), F32),
                   jax.ShapeDtypeStruct((1, HG_DK), F32)],
        scratch_shapes=[pltpu.VMEM((HG_DK, HG_DK), F32), pltpu.VMEM((1, HG_DK), F32)],
        compiler_params=_params(("arbitrary", "arbitrary")),
    )(proj, proj, proj, proj, o_pre, states, scores, dog, alb, gain)


def _swa_probs(qh, kp, kc, sink, slope, has_prev, lse=None):
    rows = qh.shape[0]
    qi = lax.broadcasted_iota(jnp.int32, (rows, WINDOW), 0) & (WINDOW - 1)
    si = lax.broadcasted_iota(jnp.int32, (rows, WINDOW), 1)
    scale = ATT_HD ** -0.5
    dist_c = (qi - si).astype(F32)
    s_p = _dot_nt(qh, kp) * scale - slope * (dist_c + float(WINDOW))
    s_c = _dot_nt(qh, kc) * scale - slope * dist_c
    s_p = jnp.where((si > qi) & has_prev, s_p, NEG)
    s_c = jnp.where(si <= qi, s_c, NEG)
    if lse is not None:
        return jnp.exp(s_p - lse), jnp.exp(s_c - lse), jnp.exp(sink - lse), lse
    mx = jnp.maximum(jnp.maximum(jnp.max(s_p, axis=-1, keepdims=True), jnp.max(s_c, axis=-1, keepdims=True)), sink)
    e_p, e_c, e_s = jnp.exp(s_p - mx), jnp.exp(s_c - mx), jnp.exp(sink - mx)
    total = jnp.sum(e_p, axis=-1, keepdims=True) + jnp.sum(e_c, axis=-1, keepdims=True) + e_s
    inv = 1.0 / total
    return e_p * inv, e_c * inv, e_s * inv, mx + jnp.log(total)


def _slope(h, n_heads):
    return float(2.0 ** (-8.0 * (h + 1) / n_heads))


def _swa_group(ref_vals, sink_ref, kh, n_heads):
    heads = [kh * ATT_G + g for g in range(ATT_G)]
    stacked = [jnp.concatenate([v[:, h * ATT_HD:(h + 1) * ATT_HD] for h in heads], axis=0) for v in ref_vals]
    grp = lax.shift_right_logical(lax.broadcasted_iota(jnp.int32, (ATT_G * WINDOW, 1), 0), WINDOW.bit_length() - 1)
    slope = jnp.zeros((ATT_G * WINDOW, 1), F32)
    sink = jnp.zeros((ATT_G * WINDOW, 1), F32)
    for g, h in enumerate(heads):
        slope = jnp.where(grp == g, _slope(h, n_heads), slope)
        sink = jnp.where(grp == g, sink_ref[:, h:h + 1], sink)
    return stacked, slope, sink


def _swa_fwd(q, kv, sinks):
    m, d = q.shape
    n_heads = d // ATT_HD
    kvh = n_heads // ATT_G
    kd = kvh * ATT_HD
    nb = m // WINDOW

    def body(q_ref, kvp_ref, kvc_ref, sink_ref, o_ref, lse_ref):
        has_prev = pl.program_id(0) > 0
        qv, kvp, kvc = q_ref[...], kvp_ref[...], kvc_ref[...]
        lane_h = lax.broadcasted_iota(jnp.int32, (WINDOW, n_heads), 1)
        outs, lse_all = [], jnp.zeros((WINDOW, n_heads), F32)
        for kh in range(kvh):
            ks = slice(kh * ATT_HD, (kh + 1) * ATT_HD)
            vs = slice(kd + kh * ATT_HD, kd + (kh + 1) * ATT_HD)
            (q4,), slope, sink = _swa_group([qv], sink_ref, kh, n_heads)
            p_p, p_c, _, lse = _swa_probs(q4, kvp[:, ks], kvc[:, ks], sink, slope, has_prev)
            o4 = _dot(p_p, kvp[:, vs]) + _dot(p_c, kvc[:, vs])
            for g in range(ATT_G):
                rows = slice(g * WINDOW, (g + 1) * WINDOW)
                outs.append(o4[rows, :])
                lse_all = jnp.where(lane_h == kh * ATT_G + g, lse[rows, :], lse_all)
        o_ref[...] = jnp.concatenate(outs, axis=-1).astype(o_ref.dtype)
        lse_ref[...] = lse_all

    return pl.pallas_call(
        body, name="swa_fwd", grid=(nb,),
        in_specs=[pl.BlockSpec((WINDOW, d), lambda n: (n, 0)),
                  pl.BlockSpec((WINDOW, 2 * kd), lambda n: (jnp.maximum(n - 1, 0), 0)),
                  pl.BlockSpec((WINDOW, 2 * kd), lambda n: (n, 0)),
                  pl.BlockSpec((1, n_heads), lambda n: (0, 0))],
        out_specs=[pl.BlockSpec((WINDOW, d), lambda n: (n, 0)), pl.BlockSpec((WINDOW, n_heads), lambda n: (n, 0))],
        out_shape=[jax.ShapeDtypeStruct((m, d), BF16), jax.ShapeDtypeStruct((m, n_heads), F32)],
        compiler_params=_params(("arbitrary",)),
    )(q, kv, kv, sinks)


def _swa_bwd(q, kv, sinks, lse, dao):
    m, d = q.shape
    n_heads = d // ATT_HD
    kvh = n_heads // ATT_G
    kd = kvh * ATT_HD
    nb = m // WINDOW
    scale = ATT_HD ** -0.5

    def body(q_ref, kvp_ref, kvc_ref, sink_ref, lse_ref, do_ref, dq_ref, dkvc_ref, dkvp_ref, dqsum_ref, dsink_ref):
        @pl.when(pl.program_id(0) == 0)
        def _():
            dqsum_ref[...] = jnp.zeros(dqsum_ref.shape, F32)
            dsink_ref[...] = jnp.zeros(dsink_ref.shape, F32)

        has_prev = pl.program_id(0) > 0
        qv, kvp, kvc, dov = q_ref[...], kvp_ref[...], kvc_ref[...], do_ref[...]
        lane_h = lax.broadcasted_iota(jnp.int32, (1, n_heads), 1)
        dsink = jnp.zeros((1, n_heads), F32)
        dq_parts, dk_p, dk_c, dv_p, dv_c = [], [], [], [], []
        for kh in range(kvh):
            ks = slice(kh * ATT_HD, (kh + 1) * ATT_HD)
            vs = slice(kd + kh * ATT_HD, kd + (kh + 1) * ATT_HD)
            kp, kc, vp, vc = kvp[:, ks], kvc[:, ks], kvp[:, vs], kvc[:, vs]
            (q4, do4), slope, sink = _swa_group([qv, dov], sink_ref, kh, n_heads)
            lse4 = jnp.concatenate([lse_ref[:, kh * ATT_G + g:kh * ATT_G + g + 1] for g in range(ATT_G)], axis=0)
            p_p, p_c, p_s, _ = _swa_probs(q4, kp, kc, sink, slope, has_prev, lse=lse4)
            dp_p, dp_c = _dot_nt(do4, vp), _dot_nt(do4, vc)
            delta = jnp.sum(p_p * dp_p, axis=-1, keepdims=True) + jnp.sum(p_c * dp_c, axis=-1, keepdims=True)
            ds_p, ds_c = p_p * (dp_p - delta), p_c * (dp_c - delta)
            sink_term = p_s * delta
            dq4 = (_dot(ds_p, kp) + _dot(ds_c, kc)) * scale
            for g in range(ATT_G):
                rows = slice(g * WINDOW, (g + 1) * WINDOW)
                dsink = dsink + jnp.where(lane_h == kh * ATT_G + g, -_col_sum(sink_term[rows, :]), 0.0)
                dq_parts.append(dq4[rows, :])
            dk_p.append(_dot_tn(ds_p, q4) * scale)
            dk_c.append(_dot_tn(ds_c, q4) * scale)
            dv_p.append(_dot_tn(p_p, do4))
            dv_c.append(_dot_tn(p_c, do4))
        dq = jnp.concatenate(dq_parts, axis=-1)
        dq_ref[...] = dq.astype(dq_ref.dtype)
        dqsum_ref[...] += _col_sum(dq)
        dsink_ref[...] += dsink
        dkvc_ref[...] = jnp.concatenate(dk_c + dv_c, axis=-1)
        dkvp_ref[...] = jnp.concatenate(dk_p + dv_p, axis=-1)

    return pl.pallas_call(
        body, name="swa_bwd", grid=(nb,),
        in_specs=[pl.BlockSpec((WINDOW, d), lambda n: (n, 0)),
                  pl.BlockSpec((WINDOW, 2 * kd), lambda n: (jnp.maximum(n - 1, 0), 0)),
                  pl.BlockSpec((WINDOW, 2 * kd), lambda n: (n, 0)),
                  pl.BlockSpec((1, n_heads), lambda n: (0, 0)),
                  pl.BlockSpec((WINDOW, n_heads), lambda n: (n, 0)),
                  pl.BlockSpec((WINDOW, d), lambda n: (n, 0))],
        out_specs=[pl.BlockSpec((WINDOW, d), lambda n: (n, 0)),
                   pl.BlockSpec((WINDOW, 2 * kd), lambda n: (n, 0)),
                   pl.BlockSpec((WINDOW, 2 * kd), lambda n: (n, 0)),
                   pl.BlockSpec((1, d), lambda n: (0, 0)),
                   pl.BlockSpec((1, n_heads), lambda n: (0, 0))],
        out_shape=[jax.ShapeDtypeStruct((m, d), BF16), jax.ShapeDtypeStruct((m, 2 * kd), F32),
                   jax.ShapeDtypeStruct((m, 2 * kd), F32), jax.ShapeDtypeStruct((1, d), F32),
                   jax.ShapeDtypeStruct((1, n_heads), F32)],
        compiler_params=_params(("arbitrary",)),
    )(q, kv, kv, sinks, lse, dao)


def _kv_grad_combine(dkv_cur, dkv_prev):
    m, w = dkv_cur.shape
    nb = m // WINDOW
    per = max(g for g in (1, 2, 4) if nb % g == 0)
    rows, steps = per * WINDOW, nb // per

    def body(cur_ref, same_ref, next_ref, o_ref, sum_ref):
        @pl.when(pl.program_id(0) == 0)
        def _():
            sum_ref[...] = jnp.zeros(sum_ref.shape, F32)

        after = jnp.where(pl.program_id(0) < steps - 1, next_ref[...], 0.0)
        total = cur_ref[...] + (jnp.concatenate([same_ref[WINDOW:, :], after], axis=0) if per > 1 else after)
        o_ref[...] = total.astype(o_ref.dtype)
        sum_ref[...] += _col_sum(total)

    return pl.pallas_call(
        body, name="kv_grad_combine", grid=(steps,),
        in_specs=[pl.BlockSpec((rows, w), lambda n: (n, 0)), pl.BlockSpec((rows, w), lambda n: (n, 0)),
                  pl.BlockSpec((WINDOW, w), lambda n: (jnp.minimum((n + 1) * per, nb - 1), 0))],
        out_specs=[pl.BlockSpec((rows, w), lambda n: (n, 0)), pl.BlockSpec((1, w), lambda n: (0, 0))],
        out_shape=[jax.ShapeDtypeStruct((m, w), BF16), jax.ShapeDtypeStruct((1, w), F32)],
        compiler_params=_params(("arbitrary",)),
    )(dkv_cur, dkv_prev, dkv_prev)


def _row(v):
    return v.reshape(1, -1)


def _local_step(x, p, target, wget, grad_sink, ln_gain, ln_bias, alb, norm_gain, kv_b, b_q, sinks, b_out, ple_b,
                small_sink=None):
    gs = {}
    gains = ln_gain.reshape(DEPTH * 3, -1)
    biases = ln_bias.reshape(DEPTH * 3, -1)
    sd = x.shape
    pending = [None]

    def mm(a, b, lb=0, **kw):
        after, pending[0] = pending[0], None
        return _mm(a, b, lb=lb, after=after, **kw)

    def mm_ln(a, wt, xin, i, j, nm, bias=None, pu=None):
        r = 3 * i + j
        if pu is None:
            fn, rows = (lambda h, xv, g, bv: (h,) + _ln_fwd_fn(xv, h, g[r:r + 1], bv[r:r + 1])), [xin]
        else:
            fn = lambda h, xv, puv, g, bv: (h,) + _ple_ln_fwd_fn(xv, h, puv, g[r:r + 1], bv[r:r + 1])
            rows = [xin, pu]
        h, y, yb = _mm(a, wt, lb=0, bias=bias, name=nm,
                       post=(fn, rows, [gains, biases], [(sd, F32), (sd, F32), (sd, BF16)], []))
        return h, (y, yb)

    def mm_ln_bwd(a, wt, add, xin, h, i, j, nm):
        r = 3 * i + j
        dx_part, dh, dg, db, dhsum = mm(a, wt, tb=True, add=add, name=nm,
                                        post=(lambda dy, xv, hv, g: _ln_bwd_fn(dy, xv, hv, g[r:r + 1]), [xin, h],
                                              [gains], [(sd, F32), (sd, BF16)], [((1, sd[1]), F32)] * 3))
        gs[f"ln_gain_{i}_{j}"], gs[f"ln_bias_{i}_{j}"] = dg, db
        return dx_part, dh, dhsum

    def tail_fwd(xa, i):
        wgu = wget("ffn_w_gate_up", i, xa[1])
        hid2 = wgu.shape[-1]
        gu, act = _mm(xa[1], wgu, lb=0, name=f"ffn_up_swiglu{i}", tile_cols=hid2 // 2,
                      post=(_swiglu_fwd_fn, [], [], [((sd[0], hid2), BF16), ((sd[0], hid2 // 2), BF16)], []))
        f, xb = mm_ln(act, wget("ffn_w_down", i, act), xa[0], i, 1, f"ffn_down_ln{i}")
        pu = _mm(p, wget("ple_w_up", i, act), la=i, lb=0, name=f"ple_up{i}")
        pg, xc = mm_ln(xb[1], wget("ple_w_gate", i, act), xb[0], i, 2, f"ple_gate_ln{i}", bias=_row(ple_b[i]), pu=pu)
        return dict(xa=xa, gu=gu, act=act, f=f, xb=xb, pg=pg, pu=pu), xc

    def tail_bwd(head, sv, i, mix_in, mix_h):
        xa, xb = sv["xa"], sv["xb"]
        r = 3 * i + 2
        dxb_part, dpg, dpu, dg2, db2, dbg = head(
            lambda dy, xv, pgv, puv, g: _ple_ln_bwd_fn(dy, xv, pgv, puv, g[r:r + 1]), [xb[0], sv["pg"], sv["pu"]],
            [gains], [(sd, F32), (sd, BF16), (sd, BF16)], [((1, sd[1]), F32)] * 3)[:6]
        gs[f"ple_b_{i}"] = dbg
        gs[f"ln_gain_{i}_2"], gs[f"ln_bias_{i}_2"] = dg2, db2
        grad_of("ple_w_gate", i, xb[1], dpg)
        grad_of("ple_w_up", i, p, dpu, la=i)
        dxa_part, df, _ = mm_ln_bwd(dpg, wget("ple_w_gate", i, None), dxb_part, xa[0], sv["f"], i, 1,
                                    f"ple_gate_dx_ln{i}")
        grad_of("ffn_w_down", i, sv["act"], df)
        gu = sv["gu"]
        dgu, = mm(df, wget("ffn_w_down", i, None), tb=True, name=f"ffn_down_dx_swiglu{i}", tile_cols=gu.shape[1] // 4,
                  post=(_swiglu_bwd_fn, [gu], [], [(gu.shape, BF16)], []))
        grad_of("ffn_w_gate_up", i, xa[1], dgu)
        return mm_ln_bwd(dgu, wget("ffn_w_gate_up", i, None), dxa_part, mix_in, mix_h, i, 0, f"ffn_up_dx_ln{i}")

    def grad_of(nm, i, act, dout, la=None, b_parts=None):
        grad = mm(act, dout, la=la, lb=None, ta=True, out_dtype=BF16, out_layers=1, out_layer=0,
                  name=f"grad_{nm}{i}", b_parts=b_parts)
        token = grad_sink(nm, i, grad)
        if token is not None:
            pending[0] = token

    proj = _mm(x, wget("a_w_in", 0, None), lb=0, name="hg_proj")
    o_pre, og, states, scores = _hgrn2_fwd(proj, alb, norm_gain, rb=HG_ROWS)
    h0, x1 = mm_ln(og, wget("a_w_out", 0, og), x, 0, 0, "hg_out_ln")
    sv0, x3 = tail_fwd(x1, 0)
    kv = _mm(x3[1], wget("kv_w", 0, x3[1]), lb=0, bias=_row(kv_b), out_dtype=BF16, name="kv_proj")
    q = _mm(x3[1], wget("b_w_q", 0, x3[1]), lb=0, bias=b_q, out_dtype=BF16, name="q_proj")
    ao, lse = _swa_fwd(q, kv, sinks)
    h1, x4 = mm_ln(ao, wget("b_w_out", 0, x3[1]), x3[0], 1, 0, "att_out_ln", bias=b_out)
    sv1, y = tail_fwd(x4, 1)

    loss_box = []

    def loss_head(fn, rows, whole, outs, sums):
        def with_loss(yv, tv, *rest):
            dy, part = _loss_fn(yv, tv)
            return fn(dy, *rest) + (part,)

        res = _rowwise(with_loss, [y[0], target] + rows, whole, outs, list(sums) + [((1, LANES), F32)],
                       name="loss_ln_ple_bwd1")
        loss_box.append(res[-1])
        return res

    dx3_part, dh1, dh1sum = tail_bwd(loss_head, sv1, 1, x3[0], h1)
    loss = loss_box[0]
    gs["b_out"] = dh1sum
    grad_of("b_w_out", 0, ao, dh1)
    dao = mm(dh1, wget("b_w_out", 0, None), tb=True, out_dtype=BF16, name="att_out_dx")
    dq, dkv_cur, dkv_prev, dqsum, dsinks = _swa_bwd(q, kv, sinks, lse, dao)
    gs["b_q"], gs["sinks"] = dqsum, dsinks
    dkv, dkvsum = _kv_grad_combine(dkv_cur, dkv_prev)
    gs["kv_b"] = dkvsum
    grad_of("b_w_q", 0, x3[1], dq)
    grad_of("kv_w", 0, x3[1], dkv)
    dx3 = mm(dq, wget("b_w_q", 0, None), tb=True, add=dx3_part, name="q_proj_dx")

    def kv_head(*post):
        return mm(dkv, wget("kv_w", 0, None), tb=True, add=dx3, name="kv_proj_dx_ln_ple_bwd0", post=post)

    dx_part, dh0, _ = tail_bwd(kv_head, sv0, 0, x, h0)
    grad_of("a_w_out", 0, og, dh0)
    dog = mm(dh0, wget("a_w_out", 0, None), tb=True, name="hg_out_dx")
    dproj, dalb, dgain = _hgrn2_bwd(proj, o_pre, states, scores, dog, alb, norm_gain, rb=HG_ROWS)
    gs["alb"], gs["norm_gain"] = dalb, dgain
    if small_sink is not None:
        pending[0] = small_sink(loss, gs)
    grad_of("a_w_in", 0, x, dproj, b_parts=4)
    grad_x = mm(dproj, wget("a_w_in", 0, None), tb=True, add=dx_part, name="hg_proj_dx", a_parts=4)
    return loss, grad_x, gs


HBM_SPEC = pl.BlockSpec(memory_space=pl.ANY)
HBM_ONLY = pl.BlockSpec(memory_space=pltpu.HBM)
SEM_SPEC = pl.BlockSpec(memory_space=pltpu.SEMAPHORE)
SIDE_EFFECT = pltpu.SideEffectType.DATAFLOW_SIDE_EFFECTING


def _slot(kind, j):
    return (j % 2) * 2 + j // 2 if kind == "colp" else j


def _piece(ref, kind, j):
    _, r, c = ref.shape
    if kind == "row":
        return ref.at[:, pl.ds(j * (r // N_CHIPS), r // N_CHIPS), :]
    return ref.at[:, :, pl.ds(_slot(kind, j) * (c // N_CHIPS), c // N_CHIPS)]


def _piece_dyn(ref, kind, j):
    _, r, c = ref.shape
    if kind == "row":
        return ref.at[:, pl.ds(pl.multiple_of(j * (r // N_CHIPS), 16), r // N_CHIPS), :]
    return ref.at[:, :, pl.ds(pl.multiple_of(_slot(kind, j) * (c // N_CHIPS), LANES), c // N_CHIPS)]


def _chip_of(j, c):
    return (j // 2, j % 2, c)


def _in_hbm(a):
    return pltpu.with_memory_space_constraint(a, pltpu.HBM)


PLACE_STEPS = 4


def _place(items, chip, *, name, after=None):
    n = len(items)
    in_specs, out_specs, out_shapes, blocks = [], [], [], []
    for src, layer, kind, out_dtype in items:
        _, r, c = src.shape
        nb = max(k for k in (1, 2, PLACE_STEPS) if r % (16 * k) == 0 or k == 1)
        blocks.append(nb)

        def src_idx(i, chip_ref, layer=layer, nb=nb):
            return (layer, jnp.minimum(i, nb - 1), 0)

        def full_idx(i, chip_ref, kind=kind, nb=nb):
            ib = jnp.minimum(i, nb - 1)
            return (0, chip_ref[0] * nb + ib, 0) if kind == "row" else (0, ib, _slot(kind, chip_ref[0]))

        in_specs.append(pl.BlockSpec((None, r // nb, c), src_idx))
        out_specs.append(pl.BlockSpec((None, r // nb, c), full_idx))
        out_shapes.append(jax.ShapeDtypeStruct((1, r * N_CHIPS, c) if kind == "row" else (1, r, c * N_CHIPS),
                                               out_dtype))
    operands = [it[0] for it in items]
    if after is not None:
        in_specs.append(HBM_SPEC)
        operands.append(after)

    def body(chip_ref, *refs):
        for a in range(n):
            refs[len(refs) - n + a][...] = refs[a][...].astype(refs[len(refs) - n + a].dtype)

    return pl.pallas_call(
        body, name=name,
        grid_spec=pltpu.PrefetchScalarGridSpec(num_scalar_prefetch=1, grid=(PLACE_STEPS,), in_specs=in_specs,
                                               out_specs=out_specs),
        out_shape=out_shapes,
        compiler_params=_params(("arbitrary",)),
    )(chip, *operands)


def _half(ref, c):
    h = ref.shape[1] // 2
    start = c * h if isinstance(c, int) else pl.multiple_of(c * h, 16)
    return ref.at[:, pl.ds(start, h), :]


def _sibling_handshake():
    barrier = pltpu.get_barrier_semaphore()
    sibling = (lax.axis_index("x"), lax.axis_index("y"), 1 - lax.axis_index("c"))
    pl.semaphore_signal(barrier, inc=1, device_id=sibling, device_id_type=MESH)
    pl.semaphore_wait(barrier, 1)


class _SiblingFill:
    def __init__(self, lands, kinds, name, collective_id):
        self.kinds, self.name, self.n = kinds, name, len(lands)
        n = self.n
        sem_shape = pltpu.SemaphoreType.DMA((n * N_CHIPS,))

        def body(*refs):
            land_refs, send_sems, recv_sems, token = refs[:n], refs[n], refs[n + 1], refs[-1]
            _sibling_handshake()
            for cp in self._copies(land_refs, send_sems, recv_sems):
                cp.start()
            token[...] = jnp.zeros(token.shape, token.dtype)

        outs = pl.pallas_call(
            body, name=name + "_start",
            in_specs=[HBM_ONLY] * n,
            out_specs=[SEM_SPEC, SEM_SPEC] + [HBM_ONLY] * n + [pl.BlockSpec(memory_space=pltpu.VMEM)],
            out_shape=[sem_shape, sem_shape] + [pltpu.HBM(a.shape, a.dtype) for a in lands]
                      + [jax.ShapeDtypeStruct((8, LANES), F32)],
            input_output_aliases={i: i + 2 for i in range(n)},
            compiler_params=pltpu.CompilerParams(has_side_effects=SIDE_EFFECT, collective_id=collective_id),
        )(*[_in_hbm(a) for a in lands])
        self.send_sems, self.recv_sems, self.lands, self.token = outs[0], outs[1], list(outs[2:2 + n]), outs[-1]

    def _copies(self, land_refs, send_sems, recv_sems):
        x, y, c = lax.axis_index("x"), lax.axis_index("y"), lax.axis_index("c")
        me = 2 * x + y
        copies = []
        for a in range(self.n):
            for k in range(1, N_CHIPS):
                t = (me + k) % N_CHIPS
                slice_t = _piece_dyn(land_refs[a], self.kinds[a], t)
                got = _half(slice_t, c)
                copies.append(pltpu.make_async_remote_copy(
                    src_ref=got, dst_ref=got, send_sem=send_sems.at[a * N_CHIPS + k],
                    recv_sem=recv_sems.at[a * N_CHIPS + k], device_id=(x, y, 1 - c), device_id_type=MESH))
        return copies

    def wait(self, after):
        n = self.n

        def body(*refs):
            land_refs, send_sems, recv_sems = refs[:n], refs[n], refs[n + 1]
            for cp in self._copies(land_refs, send_sems, recv_sems):
                cp.wait_send()
                cp.wait_recv()

        operands = [_in_hbm(a) for a in self.lands] + [self.send_sems, self.recv_sems]
        in_specs = [HBM_ONLY] * n + [SEM_SPEC, SEM_SPEC]
        if after is not None:
            operands.append(after)
            in_specs.append(HBM_SPEC)
        outs = pl.pallas_call(
            body, name=self.name + "_wait",
            in_specs=in_specs, out_specs=[HBM_ONLY] * n,
            out_shape=[pltpu.HBM(a.shape, a.dtype) for a in self.lands],
            input_output_aliases={i: i for i in range(n)},
            compiler_params=pltpu.CompilerParams(has_side_effects=SIDE_EFFECT),
        )(*operands)
        return list(outs)


class _Exchange:
    def __init__(self, mode, srcs, lands, kinds, layers, name, collective_id, after=None, halves=None):
        self.mode, self.kinds, self.layers, self.name, self.n = mode, kinds, layers, name, len(lands)
        self.halves = halves if halves is not None else [False] * len(lands)
        n, ns = self.n, len(srcs)
        n_in = ns + n + (after is not None)
        sem_shape = pltpu.SemaphoreType.DMA((n * N_CHIPS,))

        def body(*refs):
            src_refs, land_refs = refs[:ns], refs[ns:ns + n]
            send_sems, recv_sems = refs[n_in], refs[n_in + 1]
            token = refs[-1]
            c = lax.axis_index("c")
            me = 2 * lax.axis_index("x") + lax.axis_index("y")
            barrier = pltpu.get_barrier_semaphore()
            for k in range(1, N_CHIPS):
                t = (me + k) % N_CHIPS
                pl.semaphore_signal(barrier, inc=1, device_id=(t // 2, t % 2, c), device_id_type=MESH)
            pl.semaphore_wait(barrier, N_CHIPS - 1)
            for j in range(N_CHIPS):
                @pl.when(me == j)
                def _():
                    for a in range(n):
                        for t in range(N_CHIPS):
                            if t != j:
                                src, dst = self._ends(src_refs, land_refs, a, j, t, c)
                                pltpu.make_async_remote_copy(
                                    src_ref=src, dst_ref=dst, send_sem=send_sems.at[a * N_CHIPS + t],
                                    recv_sem=recv_sems.at[a * N_CHIPS + j],
                                    device_id=_chip_of(t, c), device_id_type=MESH).start()
            token[...] = jnp.zeros(token.shape, token.dtype)

        arrays = list(srcs) + list(lands)
        operands = [_in_hbm(a) for a in arrays]
        in_specs = [HBM_ONLY] * (ns + n)
        if after is not None:
            operands.append(after)
            in_specs.append(HBM_SPEC)
        outs = pl.pallas_call(
            body, name=name + "_start",
            in_specs=in_specs,
            out_specs=[SEM_SPEC, SEM_SPEC] + [HBM_ONLY] * (ns + n) + [pl.BlockSpec(memory_space=pltpu.VMEM)],
            out_shape=[sem_shape, sem_shape] + [pltpu.HBM(a.shape, a.dtype) for a in arrays]
                      + [jax.ShapeDtypeStruct((8, LANES), F32)],
            input_output_aliases={i: i + 2 for i in range(ns + n)},
            compiler_params=pltpu.CompilerParams(has_side_effects=SIDE_EFFECT, collective_id=collective_id),
        )(*operands)
        self.send_sems, self.recv_sems = outs[0], outs[1]
        self.srcs, self.lands = list(outs[2:2 + ns]), list(outs[2 + ns:2 + ns + n])
        self.token = outs[-1]

    def _ends(self, src_refs, land_refs, a, me_j, peer, c):
        if self.mode == "gather":
            mine = _piece(land_refs[a], self.kinds[a], me_j)
            if self.halves[a]:
                mine = _half(mine, c)
            return mine, mine
        return _piece(src_refs[a], self.kinds[a], peer), land_refs[a].at[me_j, pl.ds(self.layers[a], 1)]

    def wait(self, after, lands=None):
        n, ns = self.n, len(self.srcs)
        lands = self.lands if lands is None else lands

        def body(*refs):
            src_refs, land_refs = refs[:ns], refs[ns:ns + n]
            send_sems, recv_sems = refs[ns + n], refs[ns + n + 1]
            c = lax.axis_index("c")
            me = 2 * lax.axis_index("x") + lax.axis_index("y")
            for j in range(N_CHIPS):
                @pl.when(me != j)
                def _():
                    for a in range(n):
                        sent, _ = self._ends(src_refs, land_refs, a, 0, j, c)
                        _, landed = self._ends(src_refs, land_refs, a, j, 0, c)
                        cp = pltpu.make_async_remote_copy(
                            src_ref=sent, dst_ref=landed, send_sem=send_sems.at[a * N_CHIPS + j],
                            recv_sem=recv_sems.at[a * N_CHIPS + j],
                            device_id=_chip_of(j, c), device_id_type=MESH)
                        cp.wait_send()
                        cp.wait_recv()

        arrays = self.srcs + list(lands)
        operands = [_in_hbm(a) for a in arrays] + [self.send_sems, self.recv_sems]
        in_specs = [HBM_ONLY] * (ns + n) + [SEM_SPEC, SEM_SPEC]
        if after is not None:
            operands.append(after)
            in_specs.append(HBM_SPEC)
        outs = pl.pallas_call(
            body, name=self.name + "_wait",
            in_specs=in_specs, out_specs=[HBM_ONLY] * (ns + n),
            out_shape=[pltpu.HBM(a.shape, a.dtype) for a in arrays],
            input_output_aliases={i: i for i in range(ns + n)},
            compiler_params=pltpu.CompilerParams(has_side_effects=SIDE_EFFECT),
        )(*operands)
        return list(outs[:ns]), list(outs[ns:])


def _sum_arrivals(zone, own_grads, kind, chip, name, after=None):
    _, layers, r, c = zone.shape
    tm = _pick_rows(r, 256)
    if layers * (r // tm) == 1 and r % 32 == 0:
        tm = r // 2
    nb = r // tm

    def own_idx(l, i, chip_ref):
        return (0, chip_ref[0] * nb + i, 0) if kind == "row" else (0, i, _slot(kind, chip_ref[0]))

    def slot_idx(k):
        return lambda l, i, chip_ref: (jnp.where(chip_ref[0] == k, (k + 1) % N_CHIPS, k), l, i, 0)

    in_specs = [pl.BlockSpec((None, None, tm, c), slot_idx(k)) for k in range(N_CHIPS)]
    in_specs += [pl.BlockSpec((None, tm, c), own_idx) for _ in own_grads]
    operands = [zone] * N_CHIPS + list(own_grads)
    if after is not None:
        in_specs.append(HBM_SPEC)
        operands.append(after)

    def body(chip_ref, *refs):
        slot_refs, own_refs, o_ref = refs[:N_CHIPS], refs[N_CHIPS:N_CHIPS + layers], refs[-1]
        own = own_refs[0][...]
        for u in range(1, layers):
            own = jnp.where(pl.program_id(0) == u, own_refs[u][...], own)
        acc = None
        for k in range(N_CHIPS):
            term = jnp.where(chip_ref[0] == k, own, slot_refs[k][...]).astype(F32)
            acc = term if acc is None else acc + term
        o_ref[...] = acc.astype(o_ref.dtype)

    return pl.pallas_call(
        body, name=name,
        grid_spec=pltpu.PrefetchScalarGridSpec(
            num_scalar_prefetch=1, grid=(layers, nb), in_specs=in_specs,
            out_specs=pl.BlockSpec((tm, c), lambda l, i, chip_ref: (l * nb + i, 0))),
        out_shape=jax.ShapeDtypeStruct((layers * r, c), BF16),
        compiler_params=_params(("arbitrary", "arbitrary")),
    )(chip, *operands)


class _SiblingSwap:
    def __init__(self, arrays, name, collective_id, after=None):
        self.name, self.n = name, len(arrays)
        n = self.n
        n_in = n + (after is not None)
        sem_shape = pltpu.SemaphoreType.DMA((n,))

        def body(*refs):
            ins, send_sems, recv_sems = refs[:n], refs[n_in], refs[n_in + 1]
            theirs, token = refs[n_in + 2 + n:n_in + 2 + 2 * n], refs[-1]
            _sibling_handshake()
            for cp in self._copies(ins, theirs, send_sems, recv_sems):
                cp.start()
            token[...] = jnp.zeros(token.shape, token.dtype)

        operands, in_specs = [_in_hbm(a) for a in arrays], [HBM_ONLY] * n
        if after is not None:
            operands.append(after)
            in_specs.append(HBM_SPEC)
        outs = pl.pallas_call(
            body, name=name + "_start",
            in_specs=in_specs,
            out_specs=[SEM_SPEC, SEM_SPEC] + [HBM_ONLY] * (2 * n) + [pl.BlockSpec(memory_space=pltpu.VMEM)],
            out_shape=[sem_shape, sem_shape] + [pltpu.HBM(a.shape, a.dtype) for a in arrays] * 2
                      + [jax.ShapeDtypeStruct((8, LANES), F32)],
            input_output_aliases={i: i + 2 for i in range(n)},
            compiler_params=pltpu.CompilerParams(has_side_effects=SIDE_EFFECT, collective_id=collective_id),
        )(*operands)
        self.send_sems, self.recv_sems = outs[0], outs[1]
        self.mine, self.theirs, self.token = list(outs[2:2 + n]), list(outs[2 + n:2 + 2 * n]), outs[-1]

    def _copies(self, mine, theirs, send_sems, recv_sems):
        sibling = (lax.axis_index("x"), lax.axis_index("y"), 1 - lax.axis_index("c"))
        return [pltpu.make_async_remote_copy(src_ref=mine[a], dst_ref=theirs[a], send_sem=send_sems.at[a],
                                             recv_sem=recv_sems.at[a], device_id=sibling, device_id_type=MESH)
                for a in range(self.n)]

    def wait(self, after):
        n = self.n

        def body(*refs):
            for cp in self._copies(refs[:n], refs[n:2 * n], refs[2 * n], refs[2 * n + 1]):
                cp.wait_send()
                cp.wait_recv()

        arrays = self.mine + self.theirs
        outs = pl.pallas_call(
            body, name=self.name + "_wait",
            in_specs=[HBM_ONLY] * (2 * n) + [SEM_SPEC, SEM_SPEC, HBM_SPEC], out_specs=[HBM_ONLY] * (2 * n),
            out_shape=[pltpu.HBM(a.shape, a.dtype) for a in arrays],
            input_output_aliases={i: i for i in range(2 * n)},
            compiler_params=pltpu.CompilerParams(has_side_effects=SIDE_EFFECT),
        )(*[_in_hbm(a) for a in arrays], self.send_sems, self.recv_sems, after)
        return list(outs[:n]), list(outs[n:])


class _GatherDevices:
    def __init__(self, vec):
        sem_shape = pltpu.SemaphoreType.DMA((N_DEV,))

        def body(in_ref, send_sems, recv_sems, vec_ref, out_ref, token):
            for cp in self._copies(in_ref, out_ref, send_sems, recv_sems):
                cp.start()
            token[...] = jnp.zeros(token.shape, token.dtype)

        outs = pl.pallas_call(
            body, name="gather_small_start",
            in_specs=[HBM_ONLY],
            out_specs=[SEM_SPEC, SEM_SPEC, HBM_ONLY, HBM_ONLY, pl.BlockSpec(memory_space=pltpu.VMEM)],
            out_shape=[sem_shape, sem_shape, pltpu.HBM(vec.shape, vec.dtype),
                       pltpu.HBM((N_DEV,) + vec.shape, vec.dtype), jax.ShapeDtypeStruct((8, LANES), F32)],
            input_output_aliases={0: 2},
            compiler_params=pltpu.CompilerParams(has_side_effects=SIDE_EFFECT),
        )(_in_hbm(vec))
        self.send_sems, self.recv_sems, self.vec, self.rows, self.token = outs

    def _copies(self, in_ref, out_ref, send_sems, recv_sems):
        x, y, c = lax.axis_index("x"), lax.axis_index("y"), lax.axis_index("c")
        me = 4 * x + 2 * y + c
        copies = [pltpu.make_async_copy(in_ref, out_ref.at[me], recv_sems.at[0])]
        for rel in range(1, N_DEV):
            peer = (x ^ (rel >> 2), y ^ ((rel >> 1) & 1), c ^ (rel & 1))
            copies.append(pltpu.make_async_remote_copy(
                src_ref=in_ref, dst_ref=out_ref.at[me], send_sem=send_sems.at[rel], recv_sem=recv_sems.at[rel],
                device_id=peer, device_id_type=MESH))
        return copies

    def wait(self, after):
        def body(vec_ref, rows_ref, send_sems, recv_sems, after_ref, vec_out, rows_out):
            copies = self._copies(vec_ref, rows_ref, send_sems, recv_sems)
            copies[0].wait()
            for cp in copies[1:]:
                cp.wait_send()
                cp.wait_recv()

        outs = pl.pallas_call(
            body, name="gather_small_wait",
            in_specs=[HBM_ONLY, HBM_ONLY, SEM_SPEC, SEM_SPEC, HBM_SPEC], out_specs=[HBM_ONLY, HBM_ONLY],
            out_shape=[pltpu.HBM(self.vec.shape, self.vec.dtype), pltpu.HBM(self.rows.shape, self.rows.dtype)],
            input_output_aliases={0: 0, 1: 1},
            compiler_params=pltpu.CompilerParams(has_side_effects=SIDE_EFFECT),
        )(_in_hbm(self.vec), _in_hbm(self.rows), self.send_sems, self.recv_sems, after)
        return outs[1]


BIG = [("a_w_in", "col"), ("a_w_out", "row"), ("kv_w", "row"), ("b_w_q", "row"), ("b_w_out", "row"),
       ("ffn_w_gate_up", "colp"), ("ffn_w_down", "row"), ("ple_w_up", "col"), ("ple_w_gate", "row")]
GATHER_GROUPS = [[("a_w_in", 0), ("small", 0)], [("a_w_out", 0)],
                 [("ffn_w_gate_up", 0), ("ffn_w_down", 0), ("ple_w_gate", 0), ("ple_w_up", 0)],
                 [("kv_w", 0), ("b_w_q", 0), ("b_w_out", 0)],
                 [("ffn_w_gate_up", 1)], [("ffn_w_down", 1), ("ple_w_gate", 1), ("ple_w_up", 1)]]
SCATTER_GROUPS = [[("ple_w_gate", 1), ("ple_w_up", 1), ("ffn_w_down", 1)], [("ffn_w_gate_up", 1)],
                  [("b_w_out", 0), ("b_w_q", 0), ("kv_w", 0)], [("ple_w_gate", 0), ("ple_w_up", 0), ("ffn_w_down", 0)],
                  [("ffn_w_gate_up", 0), ("a_w_out", 0)], [("a_w_in", 0)]]
COLLECTIVE_IDS = {"fill": 0, "swap": 6, "gather": 9, "scatter": 15}
SMALL_SHARDED = ["ln_gain", "ln_bias", "a_lower_bound"]
SMALL_REPLICATED = ["a_norm_gain", "kv_b", "b_b_q", "b_sinks", "b_b_out", "ple_b_gate"]
WEIGHT_ORDER = ["a_w_in", "a_lower_bound", "a_norm_gain", "a_w_out", "kv_w", "kv_b", "b_w_q", "b_b_q", "b_sinks",
                "b_w_out", "b_b_out", "ffn_w_gate_up", "ffn_w_down", "ple_w_up", "ple_w_gate", "ple_b_gate",
                "ln_gain", "ln_bias"]


def _as3(a):
    return a.reshape((-1,) + a.shape[-2:]) if a.ndim >= 3 else a.reshape((1,) + a.shape)


def _pad_lanes(v):
    n = v.shape[-1]
    return jnp.pad(v, ((0, 0), (0, (-n) % LANES)))


ADAM_MANY_STEPS = 4
ADAM_MANY_MAX = 1 << 19


def _adam_many(groups, name):
    in_specs, out_specs, out_shapes, arrays = [], [], [], []
    for group in groups:
        r, c = group[0].shape
        block = pl.BlockSpec((r // ADAM_MANY_STEPS, c), lambda i: (i, 0))
        in_specs += [block] * len(group)
        arrays += list(group)
        out_specs += [block] * 4
        out_shapes += [jax.ShapeDtypeStruct((r, c), F32)] * 4

    def body(*refs):
        ins, outs = refs[:len(arrays)], refs[len(arrays):]
        for k in range(len(groups)):
            res = _adam_fn(*[ref[...] for ref in ins[5 * k:5 * k + 5]])
            for out_ref, val in zip(outs[4 * k:4 * k + 4], res):
                out_ref[...] = val

    result = pl.pallas_call(
        body, name=name, grid=(ADAM_MANY_STEPS,), in_specs=in_specs, out_specs=out_specs, out_shape=out_shapes,
        compiler_params=_params(("parallel",)),
    )(*arrays)
    return [result[4 * k:4 * k + 4] for k in range(len(groups))]


def _adam_small(everyone, chip, items, loss_off):
    n_items = len(items)

    def body(chip_ref, every_ref, *refs):
        ins, outs = refs[:3 * n_items], refs[3 * n_items:]

        def total(off, width):
            acc = every_ref[0, :, off:off + width]
            for s in range(1, N_DEV):
                acc = acc + every_ref[s, :, off:off + width]
            return acc

        for a, (w, _, _, off, sharded) in enumerate(items):
            cols = w.shape[-1]
            for r in range(w.size // cols):
                at = (slice(r, r + 1),) if w.ndim == 2 else (r // w.shape[1], slice(r % w.shape[1], r % w.shape[1] + 1))
                if sharded:
                    full = total(off + r * N_CHIPS * cols, N_CHIPS * cols)
                    g = full[:, 0:cols]
                    for c in range(1, N_CHIPS):
                        g = jnp.where(chip_ref[0] == c, full[:, c * cols:(c + 1) * cols], g)
                else:
                    g = total(off + r * cols, cols)
                w_ref, m_ref, v_ref = ins[3 * a:3 * a + 3]
                res = _adam_fn(w_ref[at], m_ref[at], v_ref[at], g, jnp.zeros_like(g))
                for out_ref, val in zip(outs[4 * a:4 * a + 4], res):
                    out_ref[at] = val
        outs[-1][...] = total(loss_off, LANES)

    def whole(shape):
        return pl.BlockSpec(tuple(shape), lambda i, chip_ref: (0,) * len(shape))

    arrays = [arr for it in items for arr in it[:3]]
    out_shapes = [jax.ShapeDtypeStruct(it[0].shape, F32) for it in items for _ in range(4)]
    out_shapes.append(jax.ShapeDtypeStruct((1, LANES), F32))
    result = pl.pallas_call(
        body, name="adam_small",
        grid_spec=pltpu.PrefetchScalarGridSpec(
            num_scalar_prefetch=1, grid=(1,),
            in_specs=[whole(everyone.shape)] + [whole(arr.shape) for arr in arrays],
            out_specs=[whole(s.shape) for s in out_shapes]),
        out_shape=out_shapes,
        compiler_params=_params(("arbitrary",)),
    )(chip, everyone, *arrays)
    return [result[4 * a:4 * a + 4] for a in range(n_items)], result[-1]


def kernel(x, p, a_w_in, a_lower_bound, a_norm_gain, a_w_out, kv_w, kv_b, b_w_q, b_b_q, b_sinks, b_w_out, b_b_out, ffn_w_gate_up, ffn_w_down, ple_w_up, ple_w_gate, ple_b_gate, ln_gain, ln_bias, loss_target, m_a_w_in, m_a_lower_bound, m_a_norm_gain, m_a_w_out, m_kv_w, m_kv_b, m_b_w_q, m_b_b_q, m_b_sinks, m_b_w_out, m_b_b_out, m_ffn_w_gate_up, m_ffn_w_down, m_ple_w_up, m_ple_w_gate, m_ple_b_gate, m_ln_gain, m_ln_bias, v_a_w_in, v_a_lower_bound, v_a_norm_gain, v_a_w_out, v_kv_w, v_kv_b, v_b_w_q, v_b_b_q, v_b_sinks, v_b_w_out, v_b_b_out, v_ffn_w_gate_up, v_ffn_w_down, v_ple_w_up, v_ple_w_gate, v_ple_b_gate, v_ln_gain, v_ln_bias):
    args = dict(locals())
    wts = {n: args[n] for n in WEIGHT_ORDER}
    mom = {n: args["m_" + n] for n in WEIGHT_ORDER}
    vel = {n: args["v_" + n] for n in WEIGHT_ORDER}
    chip = 2 * lax.axis_index("x") + lax.axis_index("y")
    d = x.shape[-1]
    dq = d // N_CHIPS

    kind_of = dict(BIG)
    kind_of["small"] = "col"
    chip_arr = chip.reshape(1).astype(jnp.int32)
    small_pack = jnp.concatenate([wts[n].reshape(-1, dq) for n in SMALL_SHARDED], axis=0)[None]

    def place_item(key):
        n, layer = key
        if n == "small":
            return small_pack, 0, "col", F32
        return _as3(wts[n]), layer, kind_of[n], BF16

    gathers, where = [], {}
    for gi, group in enumerate(GATHER_GROUPS):
        prev = gathers[-1].token if gathers else None
        placed = _place([place_item(k) for k in group], chip_arr, name=f"place{gi}", after=prev)
        gathers.append(_Exchange("gather", [], placed, [kind_of[k[0]] for k in group],
                                 [0] * len(group), f"gather{gi}", COLLECTIVE_IDS["gather"] + gi, after=prev,
                                 halves=[k[0] != "small" for k in group]))
        for k in group:
            where[k] = gi
    all_started = gathers[-1].token
    ready = {}

    fills = {}

    def pass_on(gi, after):
        if gi not in fills:
            group = GATHER_GROUPS[gi]
            outs = gathers[gi].wait(after)[1]
            split = [i for i, k in enumerate(group) if k[0] != "small"]
            fills[gi] = (outs, split, _SiblingFill([outs[i] for i in split], [kind_of[group[i][0]] for i in split],
                                                   f"fill{gi}", COLLECTIVE_IDS["fill"] + gi))

    def wget(name, layer, after):
        key = (name, layer)
        if key not in ready:
            gi = where[key]
            after = all_started if gi == 0 else after
            pass_on(gi, after)
            if 1 <= gi < len(GATHER_GROUPS) - 1:
                pass_on(gi + 1, after)
                after = fills[gi + 1][2].token
            outs, split, fill = fills[gi]
            for i, arr in zip(split, fill.wait(after)):
                outs[i] = arr
            for k, arr in zip(GATHER_GROUPS[gi], outs):
                ready[k] = arr
        return ready[key]

    small_full = wget("small", 0, None)[0]
    ln_gain_f = small_full[0:6].reshape(DEPTH, 3, d)
    ln_bias_f = small_full[6:12].reshape(DEPTH, 3, d)
    alb_f = small_full[12:14]

    group_of = {k: gi for gi, group in enumerate(SCATTER_GROUPS) for k in group}
    grads_done, zones, scatters = {}, {}, []

    def grad_sink(name, layer, grad):
        grads_done[(name, layer)] = grad
        if name not in zones:
            zones[name] = lax.empty((N_CHIPS,) + _as3(wts[name]).shape, BF16)
        gi = group_of[(name, layer)]
        group = SCATTER_GROUPS[gi]
        if not all(k in grads_done for k in group):
            return None
        ex = _Exchange("scatter", [grads_done[k] for k in group], [zones[k[0]] for k in group],
                       [kind_of[k[0]] for k in group], [k[1] for k in group], f"scatter{gi}",
                       COLLECTIVE_IDS["scatter"] + gi)
        for k, zone in zip(group, ex.lands):
            zones[k[0]] = zone
        scatters.append((ex, group))
        return ex.token

    small = {}

    def small_sink(loss, gs):
        ln_g = jnp.concatenate([gs[f"ln_gain_{i}_{j}"] for i in range(DEPTH) for j in range(3)], axis=0)
        ln_b = jnp.concatenate([gs[f"ln_bias_{i}_{j}"] for i in range(DEPTH) for j in range(3)], axis=0)
        ple_bg = jnp.concatenate([gs[f"ple_b_{i}"] for i in range(DEPTH)], axis=0)
        small["list"] = [ln_g.reshape(1, -1), ln_b.reshape(1, -1), gs["alb"].reshape(1, -1), gs["norm_gain"],
                         gs["kv_b"], gs["b_q"], _pad_lanes(gs["sinks"]), gs["b_out"], ple_bg.reshape(1, -1), loss]
        small["gather"] = _GatherDevices(jnp.concatenate(small["list"], axis=1))
        return small["gather"].token

    loss, grad_x, gs = _local_step(
        x[0], p.reshape((p.shape[0],) + p.shape[2:]), loss_target[0], wget, grad_sink, ln_gain_f, ln_bias_f, alb_f, a_norm_gain, kv_b, b_b_q,
        b_sinks, b_b_out, ple_b_gate, small_sink)

    res = {}

    def arrive(batch, after):
        for ex, group in batch:
            srcs, outs = ex.wait(after, lands=[zones[k[0]] for k in group])
            for k, grad, zone in zip(group, srcs, outs):
                grads_done[k], zones[k[0]] = grad, zone

    def half_sums(names, batch, after):
        partial = []
        for n in names:
            own = [grads_done[(n, layer)] for layer in range(zones[n].shape[1])]
            partial.append(_sum_arrivals(zones[n], own, kind_of[n], chip_arr, f"sum_{n}", after=after))
        return _SiblingSwap(partial, f"swap{batch}", COLLECTIVE_IDS["swap"] + batch, after=after)

    def update(names, swap, after):
        flat = lambda a: a.reshape(-1, a.shape[-1])
        work = [(n, [flat(wts[n]), flat(mom[n]), flat(vel[n]), own, sib]) for n, own, sib in zip(names, *swap.wait(after))]
        many = [(n, ops) for n, ops in work if wts[n].size <= ADAM_MANY_MAX]
        if len(many) > 1:
            for (n, _), out in zip(many, _adam_many([ops for _, ops in many], f"adam_from_{many[0][0]}")):
                res[n] = [o.reshape(wts[n].shape) for o in out]
        for n, ops in work:
            if n not in res:
                out = _rowwise(_adam_fn, ops, [], [(ops[3].shape, F32)] * 4, name=f"adam_{n}")
                res[n] = [o.reshape(wts[n].shape) for o in out]
        return res[names[-1]][1]

    last_names = [k[0] for k in SCATTER_GROUPS[-1]]
    batches = [["ffn_w_gate_up"], [n for n, _ in BIG if n != "ffn_w_gate_up" and n not in last_names], last_names]
    arrive(scatters[:-1], grad_x)
    swap0 = half_sums(batches[0], 0, None)
    swap1 = half_sums(batches[1], 1, swap0.token)
    updated = update(batches[0], swap0, swap1.token)
    arrive(scatters[-1:], updated)
    swap2 = half_sums(batches[2], 2, swap1.token)
    updated = update(batches[1], swap1, swap2.token)
    update(batches[2], swap2, updated)

    everyone = small["gather"].wait(grad_x)
    offs, pos = [], 0
    for v in small["list"]:
        offs.append(pos)
        pos += v.shape[1]
    names = ["ln_gain", "ln_bias", "a_lower_bound", "a_norm_gain", "kv_b", "b_b_q", "b_sinks", "b_b_out", "ple_b_gate"]
    as_rows = lambda a: a.reshape(1, -1) if a.ndim == 1 else a
    items = [(as_rows(wts[n]), as_rows(mom[n]), as_rows(vel[n]), off, n in SMALL_SHARDED)
             for n, off in zip(names, offs)]
    updates, loss_row = _adam_small(everyone, chip_arr, items, offs[len(names)])
    for n, upd in zip(names, updates):
        res[n] = [u.reshape(wts[n].shape) for u in upd]

    outs = [loss_row[0, 0], grad_x[None]]
    for k in range(4):
        outs += [res[n][k] for n in WEIGHT_ORDER]
    return tuple(outs)
```

```python
import functools

import jax
import jax.numpy as jnp
from jax import lax
from jax.experimental import pallas as pl
from jax.experimental.pallas import tpu as pltpu

F32 = jnp.float32
BF16 = jnp.bfloat16
MESH = pl.DeviceIdType.MESH

LANES = 128
HG_DK = 128
HG_CHUNK = 64
HG_SUB = 16
HG_ROWS = 512
HG_HEADS_PER_STEP = 2
LOG2_E = 1.4426950408889634
ATT_HD = 64
ATT_G = 4
WINDOW = 128
DEPTH = 2
ALPHA = (2.0 * DEPTH) ** 0.25
LN_EPS = 1e-5
RMS_EPS = 1e-6
ADAM_LR, ADAM_B1, ADAM_B2, ADAM_EPS, ADAM_WD, ADAM_STEP = 0.001, 0.9, 0.999, 1e-08, 0.01, 10
N_CHIPS = 4
N_DEV = 8
VMEM_LIMIT = 56 * 1024 * 1024
NEG = -1e30


def _pick(n, cap):
    best = None
    for d in range(LANES, min(n, cap) + 1, LANES):
        if n % d == 0:
            best = d
    return n if best is None else best


def _pick_rows(m, cap):
    best = None
    for d in range(16, min(m, cap) + 1, 16):
        if m % d == 0:
            best = d
    return m if best is None else best


def _params(sem):
    return pltpu.CompilerParams(dimension_semantics=sem, vmem_limit_bytes=VMEM_LIMIT)


def _zeros_index(ndim, grid_rank=3):
    return (lambda i, j, kk: (0,) * ndim) if grid_rank == 3 else (lambda kk, i: (0,) * ndim)


def _mm(a, b, *, name, la=None, lb=None, ta=False, tb=False, bias=None, add=None, out_dtype=F32,
        out_layers=None, out_layer=None, after=None, post=None, tile_cols=None, caps=(1024, 1536, 2048),
        a_parts=None, b_parts=None):
    ar, ac = a.shape[-2:]
    br, bc = b.shape[-2:]
    assert a_parts is None or (not ta and la is None and a.shape[0] == a_parts)
    assert b_parts is None or (not tb and lb is None and b.shape[0] == b_parts)
    m, k = (ac, ar) if ta else (ar, ac * (a_parts or 1))
    k2, n = (bc, br) if tb else (br, bc * (b_parts or 1))
    assert k == k2, (a.shape, b.shape, ta, tb)
    if post is not None:
        caps = (512, n if tile_cols is None else tile_cols, caps[2])
    tm, tn, tk = _pick(m, caps[0]), _pick(bc if b_parts else n, caps[1]), _pick(ac if a_parts else k, caps[2])
    assert post is None or tn == caps[1]
    nk = k // tk
    gi, gj = m // tm, n // tn
    a_bytes, b_bytes = m * k * a.dtype.itemsize, k * n * b.dtype.itemsize
    rows_outer = (a_bytes + b_bytes * (gi if gj * nk > 1 else 1)) <= (b_bytes + a_bytes * (gj if gi * nk > 1 else 1))
    k_outer = post is not None and nk > 1 and gj == 1
    grid = (nk, gi) if k_outer else (gi, gj, nk) if rows_outer else (gj, gi, nk)
    keep_at = ta and nk == 1 and gj > 1 and rows_outer

    def bs(block, idx, late=False):
        if k_outer:
            return pl.BlockSpec(block, lambda kk, i: idx(jnp.where(kk == nk - 1, i, 0) if late else i, 0, kk))
        return pl.BlockSpec(block, idx if rows_outer else (lambda q, p, kk: idx(p, q, kk)))

    def spec(block, idx, layer):
        if layer is None:
            return bs(block, idx)
        return bs((None,) + block, lambda i, j, kk: (layer,) + idx(i, j, kk))

    a_spec = spec((tk, tm), lambda i, j, kk: (kk, i), la) if ta else spec((tm, tk), lambda i, j, kk: (i, kk), la)
    b_spec = spec((tn, tk), lambda i, j, kk: (j, kk), lb) if tb else spec((tk, tn), lambda i, j, kk: (kk, j), lb)
    if a_parts:
        a_spec = bs((None, tm, tk), lambda i, j, kk: (kk // (ac // tk), i, kk % (ac // tk)))
    if b_parts:
        b_spec = bs((None, tk, tn), lambda i, j, kk: (j // (bc // tn), kk, j % (bc // tn)))
    in_specs, operands = [a_spec, b_spec], [a, b]
    if bias is not None:
        in_specs.append(bs((1, tn), lambda i, j, kk: (0, j)))
        operands.append(bias)
    if add is not None:
        in_specs.append(bs((tm, tn), lambda i, j, kk: (i, j), late=True))
        operands.append(add)
    if after is not None:
        in_specs.append(pl.BlockSpec(memory_space=pl.ANY))
        operands.append(after)
    dims = (((0 if ta else 1,), (1 if tb else 0,)), ((), ()))
    has_bias, has_add = bias is not None, add is not None
    if post is None:
        fn, rows, whole, outs, sums = None, [], [], [], []
        out_shape = jax.ShapeDtypeStruct((m, n) if out_layers is None else (out_layers, m, n), out_dtype)
        out_specs = spec((tm, tn), lambda i, j, kk: (i, j), out_layer)
    else:
        fn, rows, whole, outs, sums = post
        in_specs += [bs((tm, r.shape[-1] // gj), lambda i, j, kk: (i, j), late=True) for r in rows]
        in_specs += [pl.BlockSpec(tuple(w.shape), _zeros_index(w.ndim, len(grid))) for w in whole]
        operands += list(rows) + list(whole)
        out_shape = [jax.ShapeDtypeStruct(sh, dt) for sh, dt in list(outs) + list(sums)]
        out_specs = ([bs((tm, sh[-1] // gj), lambda i, j, kk: (i, j), late=True) for sh, _ in outs]
                     + [pl.BlockSpec(tuple(sh), _zeros_index(len(sh), len(grid))) for sh, _ in sums])
    n_in, n_extra, n_outs, n_sums = len(operands), len(rows) + len(whole), len(outs), len(sums)

    def body(*refs):
        a_ref, b_ref = refs[0], refs[1]
        pos = 2
        bias_ref = add_ref = None
        if has_bias:
            bias_ref = refs[pos]
            pos += 1
        if has_add:
            add_ref = refs[pos]
            pos += 1
        extra_refs = refs[n_in - n_extra:n_in]
        out_refs = refs[n_in:n_in + max(n_outs, 1)]
        sum_refs = refs[n_in + n_outs:n_in + n_outs + n_sums]
        acc_ref = refs[-1] if nk > 1 else None
        if keep_at:
            at_ref = refs[-1]

            @pl.when(pl.program_id(1) == 0)
            def _():
                at_ref[...] = a_ref[...].astype(BF16).T

            part = lax.dot_general(at_ref[...], b_ref[...].astype(BF16), (((1,), (1 if tb else 0,)), ((), ())),
                                   preferred_element_type=F32)
        else:
            part = lax.dot_general(a_ref[...].astype(BF16), b_ref[...].astype(BF16), dims,
                                   preferred_element_type=F32)

        def finish(total):
            if has_bias:
                total = total + bias_ref[...]
            if has_add:
                total = total + add_ref[...]
            if fn is None:
                out_refs[0][...] = total.astype(out_refs[0].dtype)
                return
            res = fn(total, *[r[...] for r in extra_refs])
            for ref, val in zip(out_refs, res[:n_outs]):
                ref[...] = val.astype(ref.dtype)
            if n_sums:
                @pl.when(pl.program_id(1 if k_outer or not rows_outer else 0) == 0)
                def _():
                    for ref in sum_refs:
                        ref[...] = jnp.zeros(ref.shape, ref.dtype)

                for ref, val in zip(sum_refs, res[n_outs:]):
                    ref[...] += val

        if nk == 1:
            finish(part)
        elif k_outer:
            kk = pl.program_id(0)
            rows_i = pl.ds(pl.multiple_of(pl.program_id(1) * tm, tm), tm)

            @pl.when(kk == 0)
            def _():
                acc_ref[rows_i, :] = part

            @pl.when(kk > 0)
            def _():
                acc_ref[rows_i, :] += part

            @pl.when(kk == nk - 1)
            def _():
                finish(acc_ref[rows_i, :])
        else:
            kk = pl.program_id(2)

            @pl.when(kk == 0)
            def _():
                acc_ref[...] = part

            @pl.when(kk > 0)
            def _():
                acc_ref[...] += part

            @pl.when(kk == nk - 1)
            def _():
                finish(acc_ref[...])

    return pl.pallas_call(
        body, name=name, grid=grid, in_specs=in_specs, out_specs=out_specs, out_shape=out_shape,
        scratch_shapes=([pltpu.VMEM((m, n) if k_outer else (tm, tn), F32)] if nk > 1
                        else [pltpu.VMEM((tm, tk), BF16)] if keep_at else []),
        compiler_params=_params(("arbitrary", "arbitrary") if k_outer
                                else ("arbitrary" if n_sums else "parallel", "arbitrary" if keep_at else "parallel",
                                      "arbitrary") if rows_outer
                                else ("parallel", "arbitrary" if n_sums else "parallel", "arbitrary")),
    )(*operands)


def _rowwise(fn, rows, whole, outs, sums=(), *, name, tm=256):
    m = rows[0].shape[-2]
    tm = _pick_rows(m, tm)
    n_rows, n_whole, n_outs, n_sums = len(rows), len(whole), len(outs), len(sums)

    def rspec(shape):
        lead = len(shape) - 2
        return pl.BlockSpec(tuple(shape[:-2]) + (tm, shape[-1]), lambda i: (0,) * lead + (i, 0))

    def wspec(shape):
        return pl.BlockSpec(tuple(shape), lambda i: (0,) * len(shape))

    def body(*refs):
        vals = [r[...] for r in refs[:n_rows + n_whole]]
        out_refs = refs[n_rows + n_whole:n_rows + n_whole + n_outs]
        sum_refs = refs[n_rows + n_whole + n_outs:]
        res = fn(*vals)
        for ref, val in zip(out_refs, res[:n_outs]):
            ref[...] = val.astype(ref.dtype)
        if n_sums:
            @pl.when(pl.program_id(0) == 0)
            def _():
                for ref in sum_refs:
                    ref[...] = jnp.zeros(ref.shape, ref.dtype)

            for ref, val in zip(sum_refs, res[n_outs:]):
                ref[...] += val

    result = pl.pallas_call(
        body, name=name, grid=(m // tm,),
        in_specs=[rspec(r.shape) for r in rows] + [wspec(w.shape) for w in whole],
        out_specs=[rspec(s) for s, _ in outs] + [wspec(s) for s, _ in sums],
        out_shape=[jax.ShapeDtypeStruct(s, d) for s, d in list(outs) + list(sums)],
        compiler_params=_params(("arbitrary",)),
    )(*rows, *whole)
    return result


def _sigmoid(v):
    return jax.nn.sigmoid(v)


def _col_sum(v):
    return jnp.sum(v, axis=0, keepdims=True)


def _ln_stats(z):
    mu = jnp.mean(z, axis=-1, keepdims=True)
    zc = z - mu
    var = jnp.mean(zc * zc, axis=-1, keepdims=True)
    rstd = lax.rsqrt(var + LN_EPS)
    return zc * rstd, rstd


def _ln_fwd_fn(xin, h, gain, bias):
    xhat, _ = _ln_stats(ALPHA * xin + h)
    y = xhat * gain + bias
    return y, y


def _ple_ln_fwd_fn(xin, pg, pu, gain, bias):
    xhat, _ = _ln_stats(ALPHA * xin + _sigmoid(pg) * pu)
    y = xhat * gain + bias
    return y, y


def _ln_dz(dy, z, gain):
    xhat, rstd = _ln_stats(z)
    dxhat = dy * gain
    dz = rstd * (dxhat - jnp.mean(dxhat, axis=-1, keepdims=True)
                 - xhat * jnp.mean(dxhat * xhat, axis=-1, keepdims=True))
    return dz, _col_sum(dy * xhat), _col_sum(dy)


def _ln_bwd_fn(dy, xin, h, gain):
    dz, dgain, dbias = _ln_dz(dy, ALPHA * xin + h, gain)
    return ALPHA * dz, dz, dgain, dbias, _col_sum(dz)


def _ple_ln_bwd_fn(dy, xin, pg, pu, gain):
    sg = _sigmoid(pg)
    dz, dgain, dbias = _ln_dz(dy, ALPHA * xin + sg * pu, gain)
    dpg = dz * pu * sg * (1.0 - sg)
    return ALPHA * dz, dpg, dz * sg, dgain, dbias, _col_sum(dpg)


def _swiglu_fwd_fn(gu):
    hid = gu.shape[-1] // 2
    gate, up = gu[:, :hid], gu[:, hid:]
    return gu, gate * _sigmoid(gate) * up


def _swiglu_bwd_fn(dact, gu):
    gu = gu.astype(F32)
    hid = gu.shape[-1] // 2
    gate, up = gu[:, :hid], gu[:, hid:]
    sg = _sigmoid(gate)
    dgate = dact * up * sg * (1.0 + gate * (1.0 - sg))
    dup = dact * gate * sg
    return (jnp.concatenate([dgate, dup], axis=-1),)


def _loss_fn(y, target):
    err = y - target
    inv = 1.0 / y.shape[-1]
    part = 0.5 * inv * jnp.sum(jnp.sum(err * err, axis=-1, keepdims=True), axis=0, keepdims=True)
    return err * inv, jnp.broadcast_to(part, (1, LANES))


def _adam_fn(w, mom, vel, p_own, p_sib):
    g = p_own.astype(F32) + p_sib.astype(F32)
    m_new = ADAM_B1 * mom + (1.0 - ADAM_B1) * g
    v_new = ADAM_B2 * vel + (1.0 - ADAM_B2) * (g * g)
    m_hat = m_new / (1.0 - ADAM_B1 ** ADAM_STEP)
    v_hat = v_new / (1.0 - ADAM_B2 ** ADAM_STEP)
    delta = -ADAM_LR * (m_hat / (jnp.sqrt(v_hat) + ADAM_EPS) + ADAM_WD * w)
    return g, delta, m_new, v_new


def _split2(x):
    hi = x.astype(BF16)
    return hi, (x - hi.astype(F32)).astype(BF16)


def _dot3(a, b, dims):
    a_hi, a_lo = _split2(a)
    b_hi, b_lo = _split2(b)
    dn = (dims, ((), ()))
    return (lax.dot_general(a_hi, b_hi, dn, preferred_element_type=F32)
            + (lax.dot_general(a_hi, b_lo, dn, preferred_element_type=F32)
               + lax.dot_general(a_lo, b_hi, dn, preferred_element_type=F32)))


def _tdot(mask01, b):
    m = mask01.astype(BF16)
    b_hi = b.astype(BF16)
    rest = b - b_hi.astype(F32)
    b_mid = rest.astype(BF16)
    b_lo = (rest - b_mid.astype(F32)).astype(BF16)
    dn = (((1,), (0,)), ((), ()))
    return (lax.dot_general(m, b_hi, dn, preferred_element_type=F32)
            + (lax.dot_general(m, b_mid, dn, preferred_element_type=F32)
               + lax.dot_general(m, b_lo, dn, preferred_element_type=F32)))


def _hdot(a, b):
    return _dot3(a, b, ((1,), (0,)))


def _hdot_nt(a, b):
    return _dot3(a, b, ((1,), (1,)))


def _hdot_tn(a, b):
    return _dot3(a, b, ((0,), (0,)))


def _dot(a, b):
    return lax.dot_general(a.astype(BF16), b.astype(BF16), (((1,), (0,)), ((), ())), preferred_element_type=F32)


def _dot_nt(a, b):
    return lax.dot_general(a.astype(BF16), b.astype(BF16), (((1,), (1,)), ((), ())), preferred_element_type=F32)


def _dot_tn(a, b):
    return lax.dot_general(a.astype(BF16), b.astype(BF16), (((0,), (0,)), ((), ())), preferred_element_type=F32)


def _hg_masks():
    c = HG_CHUNK
    row = lax.broadcasted_iota(jnp.int32, (c, c), 0)
    col = lax.broadcasted_iota(jnp.int32, (c, c), 1)
    base = row & (-HG_SUB)
    return row, col, base, col <= row, col < base


def _hg_gates(qr, fr, alb):
    lbound = _sigmoid(alb[0:1, :] - alb[1:2, :])
    sig = _sigmoid(fr)
    forget = lbound + (1.0 - lbound) * sig
    kk = (1.0 - lbound) * _sigmoid(-fr)
    qt = qr * _sigmoid(qr) * (HG_DK ** -0.5)
    return qt, kk, jnp.log(forget), lbound, sig, forget


def _hg_scores(qt, kk, g, scores=True):
    c, nsub = HG_CHUNK, HG_CHUNK // HG_SUB
    row, col, base, causal, below = _hg_masks()
    b = _tdot(causal, g)
    rr = _tdot(below, g)
    bq = b - rr
    qh = qt * jnp.exp(bq)
    edecs = [None]
    parts = [jnp.zeros((HG_SUB, c), F32)]
    for i in range(1, nsub):
        edec = jnp.exp(jnp.minimum(rr[i * HG_SUB:i * HG_SUB + 1, :] - b, 0.0))
        edecs.append(edec)
        if scores:
            parts.append(_dot_nt(qh[i * HG_SUB:(i + 1) * HG_SUB, :], kk * edec))
    q3 = qt.reshape(nsub, HG_SUB, HG_DK)
    if not scores:
        return None, b, bq, qh, edecs, (b.reshape(nsub, HG_SUB, HG_DK), q3, kk.reshape(nsub, HG_SUB, HG_DK))
    a = jnp.where(below, jnp.concatenate(parts, axis=0), 0.0)
    b2 = b * LOG2_E
    b3 = b2.reshape(nsub, HG_SUB, HG_DK)
    c3 = (b2 - jnp.log2(kk)).reshape(nsub, HG_SUB, HG_DK)
    for j in range(HG_SUB):
        ek = jnp.exp2(b3 - c3[:, j:j + 1, :])
        colv = jnp.sum(q3 * ek, axis=-1, keepdims=True).reshape(c, 1)
        a = jnp.where(col == base + j, colv, a)
    a = jnp.where(causal, a, 0.0)
    return a, b, bq, qh, edecs, None


def _hg_norm(o, gr, gain):
    r = lax.rsqrt(jnp.mean(o * o, axis=-1, keepdims=True) + RMS_EPS)
    sg = _sigmoid(gr)
    return o * r * gain, r, sg


def _hgrn2_fwd(proj, alb, gain, *, rb):
    m, d4 = proj.shape
    d = d4 // 4
    heads = d // HG_DK
    hp = HG_HEADS_PER_STEP
    rb = min(rb, m)
    cpb = rb // HG_CHUNK
    nrb = m // rb

    def body(q_ref, f_ref, v_ref, g_ref, alb_ref, gain_ref, o_ref, og_ref, st_ref, a_ref, state):
        @pl.when(pl.program_id(1) == 0)
        def _():
            state[...] = jnp.zeros(state.shape, F32)

        def chunk(ci, carry):
            sl = pl.ds(pl.multiple_of(ci * HG_CHUNK, HG_CHUNK), HG_CHUNK)
            for u in range(hp):
                ln = slice(u * HG_DK, (u + 1) * HG_DK)
                qt, kk, g, _, _, _ = _hg_gates(q_ref[sl, ln], f_ref[sl, ln], alb_ref[:, ln])
                v = v_ref[sl, ln]
                st = state[u]
                st_ref[u, ci] = st
                a, b, _, _, _, _ = _hg_scores(qt, kk, g)
                a_ref[u, ci] = a.astype(a_ref.dtype)
                o = _dot(a, v) + _dot_nt(qt * jnp.exp(b), st)
                b_last = b[HG_CHUNK - 1:HG_CHUNK, :]
                state[u] = st * jnp.exp(b_last) + _hdot_tn(v, kk * jnp.exp(b_last - b))
                o_ref[sl, ln] = o
                n, _, sg = _hg_norm(o, g_ref[sl, ln], gain_ref[...])
                og_ref[sl, ln] = (n * g_ref[sl, ln] * sg).astype(og_ref.dtype)
            return carry

        lax.fori_loop(0, cpb, chunk, 0)

    def col(cidx):
        return pl.BlockSpec((rb, hp * HG_DK), lambda h, r: (r, cidx * (heads // hp) + h))

    return pl.pallas_call(
        body, name="hgrn2_fwd", grid=(heads // hp, nrb),
        in_specs=[col(0), col(1), col(2), col(3),
                  pl.BlockSpec((2, hp * HG_DK), lambda h, r: (0, h)),
                  pl.BlockSpec((1, HG_DK), lambda h, r: (0, 0))],
        out_specs=[pl.BlockSpec((rb, hp * HG_DK), lambda h, r: (r, h)),
                   pl.BlockSpec((rb, hp * HG_DK), lambda h, r: (r, h)),
                   pl.BlockSpec((hp, cpb, HG_DK, HG_DK), lambda h, r: (h, r, 0, 0)),
                   pl.BlockSpec((hp, cpb, HG_CHUNK, HG_CHUNK), lambda h, r: (h, r, 0, 0))],
        out_shape=[jax.ShapeDtypeStruct((m, d), F32), jax.ShapeDtypeStruct((m, d), BF16),
                   jax.ShapeDtypeStruct((heads, m // HG_CHUNK, HG_DK, HG_DK), F32),
                   jax.ShapeDtypeStruct((heads, m // HG_CHUNK, HG_CHUNK, HG_CHUNK), BF16)],
        scratch_shapes=[pltpu.VMEM((hp, HG_DK, HG_DK), F32)],
        compiler_params=_params(("parallel", "arbitrary")),
    )(proj, proj, proj, proj, alb, gain)


def _hgrn2_bwd(proj, o_pre, states, scores, dog, alb, gain, *, rb):
    m, d4 = proj.shape
    d = d4 // 4
    heads = d // HG_DK
    rb = min(rb, m)
    cpb = rb // HG_CHUNK
    nrb = m // rb
    c, nsub = HG_CHUNK, HG_CHUNK // HG_SUB

    def body(q_ref, f_ref, v_ref, g_ref, o_ref, st_ref, a_ref, dog_ref, alb_ref, gain_ref,
             dp_ref, dalb_ref, dgain_ref, dstate, carry_ref):
        first = (pl.program_id(0) == 0) & (pl.program_id(1) == 0)

        @pl.when(first)
        def _():
            dgain_ref[...] = jnp.zeros(dgain_ref.shape, F32)

        @pl.when(pl.program_id(1) == 0)
        def _():
            dstate[...] = jnp.zeros(dstate.shape, F32)
            carry_ref[...] = jnp.zeros(carry_ref.shape, F32)
            dalb_ref[...] = jnp.zeros(dalb_ref.shape, F32)

        row, col, base, causal, below = _hg_masks()
        sub_iota = lax.broadcasted_iota(jnp.int32, (nsub, HG_SUB, HG_DK), 1)
        row_k = lax.broadcasted_iota(jnp.int32, (c, HG_DK), 0)
        upper = col >= row

        def chunk(step, carry):
            ci = cpb - 1 - step
            sl = pl.ds(pl.multiple_of(ci * HG_CHUNK, HG_CHUNK), HG_CHUNK)
            qr, fr, v, gr = q_ref[sl, :], f_ref[sl, :], v_ref[sl, :], g_ref[sl, :]
            qt, kk, g, lbound, sig, forget = _hg_gates(qr, fr, alb_ref[...])
            o = o_ref[sl, :]
            dogv = dog_ref[sl, :]
            gain_v = gain_ref[...]
            n, r, sg = _hg_norm(o, gr, gain_v)
            dgr = dogv * n * sg * (1.0 + gr * (1.0 - sg))
            dn = dogv * gr * sg
            dgain_ref[...] += _col_sum(dn * o * r)
            u = dn * gain_v
            d_o = r * u - o * (r * r * r) * jnp.mean(u * o, axis=-1, keepdims=True)
            st0 = st_ref[ci]
            dst = dstate[...]
            _, b, bq, qh, edecs, (b3, q3, k3) = _hg_scores(qt, kk, g, scores=False)
            a = a_ref[ci]
            eb = jnp.exp(b)
            b_last = b[c - 1:c, :]
            kdl_dec = jnp.exp(b_last - b)
            kdl = kk * kdl_dec
            d_a = jnp.where(causal, _dot_nt(d_o, v), 0.0)
            d_at = _dot_nt(v, d_o)
            dv = _dot_tn(a, d_o) + _dot_nt(kdl, dst)
            dq = eb * _hdot(d_o, st0)
            dk = _hdot(v, dst) * kdl_dec
            d_a_below = jnp.where(below, d_a, 0.0)
            dq_parts = [jnp.zeros((HG_SUB, HG_DK), F32)]
            for i in range(1, nsub):
                lo, hi = i * HG_SUB, (i + 1) * HG_SUB
                dq_parts.append(_hdot(d_a_below[lo:hi, :], kk * edecs[i]))
                gi = _hdot(d_at[:, lo:hi], qh[lo:hi, :])
                dk = dk + jnp.where(row_k < lo, edecs[i] * gi, 0.0)
            dq = dq + jnp.concatenate(dq_parts, axis=0) * jnp.exp(bq)
            dq3 = jnp.zeros((nsub, HG_SUB, HG_DK), F32)
            dk3 = jnp.zeros((nsub, HG_SUB, HG_DK), F32)
            d_diag = jnp.concatenate([d_a[i * HG_SUB:(i + 1) * HG_SUB, i * HG_SUB:(i + 1) * HG_SUB]
                                      for i in range(nsub)], axis=0).reshape(nsub, HG_SUB, HG_SUB)
            for j in range(HG_SUB):
                e = jnp.exp(jnp.minimum(b3 - b3[:, j:j + 1, :], 0.0))
                t1 = d_diag[:, :, j:j + 1] * e
                dq3 = dq3 + t1 * k3[:, j:j + 1, :]
                dk3 = jnp.where(sub_iota == j, jnp.sum(t1 * q3, axis=1, keepdims=True), dk3)
            dq = dq + dq3.reshape(c, HG_DK)
            dk = dk + dk3.reshape(c, HG_DK)
            dstate[...] = dst * jnp.exp(b_last) + _hdot_tn(d_o, qt * eb)
            dglog = _tdot(upper, qt * dq - kk * dk) + carry_ref[...]
            carry_ref[...] = dglog[0:1, :]
            dforget = dglog / forget
            one_m_lb = 1.0 - lbound
            dsig = (dforget - dk) * one_m_lb
            sneg = _sigmoid(-fr)
            dlb = _col_sum(dforget * (1.0 - sig) - dk * sneg)
            dalb0 = dlb * lbound * one_m_lb
            dalb_ref[...] += jnp.concatenate([dalb0, -dalb0], axis=0)
            sq = _sigmoid(qr)
            dp_ref[0, sl, :] = (dq * (HG_DK ** -0.5) * sq * (1.0 + qr * (1.0 - sq))).astype(dp_ref.dtype)
            dp_ref[1, sl, :] = (dsig * sig * (1.0 - sig)).astype(dp_ref.dtype)
            dp_ref[2, sl, :] = dv.astype(dp_ref.dtype)
            dp_ref[3, sl, :] = dgr.astype(dp_ref.dtype)
            return carry

        lax.fori_loop(0, cpb, chunk, 0, unroll=2)

    def rev(r):
        return nrb - 1 - r

    def col(cidx):
        return pl.BlockSpec((rb, HG_DK), lambda h, r: (rev(r), cidx * heads + h))

    def head_rows():
        return pl.BlockSpec((rb, HG_DK), lambda h, r: (rev(r), h))

    return pl.pallas_call(
        body, name="hgrn2_bwd", grid=(heads, nrb),
        in_specs=[col(0), col(1), col(2), col(3), head_rows(),
                  pl.BlockSpec((None, cpb, HG_DK, HG_DK), lambda h, r: (h, rev(r), 0, 0)),
                  pl.BlockSpec((None, cpb, HG_CHUNK, HG_CHUNK), lambda h, r: (h, rev(r), 0, 0)),
                  head_rows(),
                  pl.BlockSpec((2, HG_DK), lambda h, r: (0, h)),
                  pl.BlockSpec((1, HG_DK), lambda h, r: (0, 0))],
        out_specs=[pl.BlockSpec((4, rb, HG_DK), lambda h, r: (0, rev(r), h)),
                   pl.BlockSpec((2, HG_DK), lambda h, r: (0, h)),
                   pl.BlockSpec((1, HG_DK), lambda h, r: (0, 0))],
        out_shape=[jax.ShapeDtypeStruct((4, m, d), BF16), jax.ShapeDtypeStruct((2, d), F32),
                   jax.ShapeDtypeStruct((1, HG_DK), F32)],
        scratch_shapes=[pltpu.VMEM((HG_DK, HG_DK), F32), pltpu.VMEM((1, HG_DK), F32)],
        compiler_params=_params(("arbitrary", "arbitrary")),
    )(proj, proj, proj, proj, o_pre, states, scores, dog, alb, gain)


def _swa_probs(qh, kp, kc, sink, slope, has_prev, lse=None):
    rows = qh.shape[0]
    qi = lax.broadcasted_iota(jnp.int32, (rows, WINDOW), 0) & (WINDOW - 1)
    si = lax.broadcasted_iota(jnp.int32, (rows, WINDOW), 1)
    scale = ATT_HD ** -0.5
    dist_c = (qi - si).astype(F32)
    s_p = _dot_nt(qh, kp) * scale - slope * (dist_c + float(WINDOW))
    s_c = _dot_nt(qh, kc) * scale - slope * dist_c
    s_p = jnp.where((si > qi) & has_prev, s_p, NEG)
    s_c = jnp.where(si <= qi, s_c, NEG)
    if lse is not None:
        return jnp.exp(s_p - lse), jnp.exp(s_c - lse), jnp.exp(sink - lse), lse
    mx = jnp.maximum(jnp.maximum(jnp.max(s_p, axis=-1, keepdims=True), jnp.max(s_c, axis=-1, keepdims=True)), sink)
    e_p, e_c, e_s = jnp.exp(s_p - mx), jnp.exp(s_c - mx), jnp.exp(sink - mx)
    total = jnp.sum(e_p, axis=-1, keepdims=True) + jnp.sum(e_c, axis=-1, keepdims=True) + e_s
    inv = 1.0 / total
    return e_p * inv, e_c * inv, e_s * inv, mx + jnp.log(total)


def _slope(h, n_heads):
    return float(2.0 ** (-8.0 * (h + 1) / n_heads))


def _swa_group(ref_vals, sink_ref, kh, n_heads):
    heads = [kh * ATT_G + g for g in range(ATT_G)]
    stacked = [jnp.concatenate([v[:, h * ATT_HD:(h + 1) * ATT_HD] for h in heads], axis=0) for v in ref_vals]
    grp = lax.shift_right_logical(lax.broadcasted_iota(jnp.int32, (ATT_G * WINDOW, 1), 0), WINDOW.bit_length() - 1)
    slope = jnp.zeros((ATT_G * WINDOW, 1), F32)
    sink = jnp.zeros((ATT_G * WINDOW, 1), F32)
    for g, h in enumerate(heads):
        slope = jnp.where(grp == g, _slope(h, n_heads), slope)
        sink = jnp.where(grp == g, sink_ref[:, h:h + 1], sink)
    return stacked, slope, sink


def _swa_fwd(q, kv, sinks):
    m, d = q.shape
    n_heads = d // ATT_HD
    kvh = n_heads // ATT_G
    kd = kvh * ATT_HD
    nb = m // WINDOW

    def body(q_ref, kvp_ref, kvc_ref, sink_ref, o_ref, lse_ref):
        has_prev = pl.program_id(0) > 0
        qv, kvp, kvc = q_ref[...], kvp_ref[...], kvc_ref[...]
        lane_h = lax.broadcasted_iota(jnp.int32, (WINDOW, n_heads), 1)
        outs, lse_all = [], jnp.zeros((WINDOW, n_heads), F32)
        for kh in range(kvh):
            ks = slice(kh * ATT_HD, (kh + 1) * ATT_HD)
            vs = slice(kd + kh * ATT_HD, kd + (kh + 1) * ATT_HD)
            (q4,), slope, sink = _swa_group([qv], sink_ref, kh, n_heads)
            p_p, p_c, _, lse = _swa_probs(q4, kvp[:, ks], kvc[:, ks], sink, slope, has_prev)
            o4 = _dot(p_p, kvp[:, vs]) + _dot(p_c, kvc[:, vs])
            for g in range(ATT_G):
                rows = slice(g * WINDOW, (g + 1) * WINDOW)
                outs.append(o4[rows, :])
                lse_all = jnp.where(lane_h == kh * ATT_G + g, lse[rows, :], lse_all)
        o_ref[...] = jnp.concatenate(outs, axis=-1).astype(o_ref.dtype)
        lse_ref[...] = lse_all

    return pl.pallas_call(
        body, name="swa_fwd", grid=(nb,),
        in_specs=[pl.BlockSpec((WINDOW, d), lambda n: (n, 0)),
                  pl.BlockSpec((WINDOW, 2 * kd), lambda n: (jnp.maximum(n - 1, 0), 0)),
                  pl.BlockSpec((WINDOW, 2 * kd), lambda n: (n, 0)),
                  pl.BlockSpec((1, n_heads), lambda n: (0, 0))],
        out_specs=[pl.BlockSpec((WINDOW, d), lambda n: (n, 0)), pl.BlockSpec((WINDOW, n_heads), lambda n: (n, 0))],
        out_shape=[jax.ShapeDtypeStruct((m, d), BF16), jax.ShapeDtypeStruct((m, n_heads), F32)],
        compiler_params=_params(("arbitrary",)),
    )(q, kv, kv, sinks)


def _swa_bwd(q, kv, sinks, lse, dao):
    m, d = q.shape
    n_heads = d // ATT_HD
    kvh = n_heads // ATT_G
    kd = kvh * ATT_HD
    nb = m // WINDOW
    scale = ATT_HD ** -0.5

    def body(q_ref, kvp_ref, kvc_ref, sink_ref, lse_ref, do_ref, dq_ref, dkvc_ref, dkvp_ref, dqsum_ref, dsink_ref):
        @pl.when(pl.program_id(0) == 0)
        def _():
            dqsum_ref[...] = jnp.zeros(dqsum_ref.shape, F32)
            dsink_ref[...] = jnp.zeros(dsink_ref.shape, F32)

        has_prev = pl.program_id(0) > 0
        qv, kvp, kvc, dov = q_ref[...], kvp_ref[...], kvc_ref[...], do_ref[...]
        lane_h = lax.broadcasted_iota(jnp.int32, (1, n_heads), 1)
        dsink = jnp.zeros((1, n_heads), F32)
        dq_parts, dk_p, dk_c, dv_p, dv_c = [], [], [], [], []
        for kh in range(kvh):
            ks = slice(kh * ATT_HD, (kh + 1) * ATT_HD)
            vs = slice(kd + kh * ATT_HD, kd + (kh + 1) * ATT_HD)
            kp, kc, vp, vc = kvp[:, ks], kvc[:, ks], kvp[:, vs], kvc[:, vs]
            (q4, do4), slope, sink = _swa_group([qv, dov], sink_ref, kh, n_heads)
            lse4 = jnp.concatenate([lse_ref[:, kh * ATT_G + g:kh * ATT_G + g + 1] for g in range(ATT_G)], axis=0)
            p_p, p_c, p_s, _ = _swa_probs(q4, kp, kc, sink, slope, has_prev, lse=lse4)
            dp_p, dp_c = _dot_nt(do4, vp), _dot_nt(do4, vc)
            delta = jnp.sum(p_p * dp_p, axis=-1, keepdims=True) + jnp.sum(p_c * dp_c, axis=-1, keepdims=True)
            ds_p, ds_c = p_p * (dp_p - delta), p_c * (dp_c - delta)
            sink_term = p_s * delta
            dq4 = (_dot(ds_p, kp) + _dot(ds_c, kc)) * scale
            for g in range(ATT_G):
                rows = slice(g * WINDOW, (g + 1) * WINDOW)
                dsink = dsink + jnp.where(lane_h == kh * ATT_G + g, -_col_sum(sink_term[rows, :]), 0.0)
                dq_parts.append(dq4[rows, :])
            dk_p.append(_dot_tn(ds_p, q4) * scale)
            dk_c.append(_dot_tn(ds_c, q4) * scale)
            dv_p.append(_dot_tn(p_p, do4))
            dv_c.append(_dot_tn(p_c, do4))
        dq = jnp.concatenate(dq_parts, axis=-1)
        dq_ref[...] = dq.astype(dq_ref.dtype)
        dqsum_ref[...] += _col_sum(dq)
        dsink_ref[...] += dsink
        dkvc_ref[...] = jnp.concatenate(dk_c + dv_c, axis=-1)
        dkvp_ref[...] = jnp.concatenate(dk_p + dv_p, axis=-1)

    return pl.pallas_call(
        body, name="swa_bwd", grid=(nb,),
        in_specs=[pl.BlockSpec((WINDOW, d), lambda n: (n, 0)),
                  pl.BlockSpec((WINDOW, 2 * kd), lambda n: (jnp.maximum(n - 1, 0), 0)),
                  pl.BlockSpec((WINDOW, 2 * kd), lambda n: (n, 0)),
                  pl.BlockSpec((1, n_heads), lambda n: (0, 0)),
                  pl.BlockSpec((WINDOW, n_heads), lambda n: (n, 0)),
                  pl.BlockSpec((WINDOW, d), lambda n: (n, 0))],
        out_specs=[pl.BlockSpec((WINDOW, d), lambda n: (n, 0)),
                   pl.BlockSpec((WINDOW, 2 * kd), lambda n: (n, 0)),
                   pl.BlockSpec((WINDOW, 2 * kd), lambda n: (n, 0)),
                   pl.BlockSpec((1, d), lambda n: (0, 0)),
                   pl.BlockSpec((1, n_heads), lambda n: (0, 0))],
        out_shape=[jax.ShapeDtypeStruct((m, d), BF16), jax.ShapeDtypeStruct((m, 2 * kd), F32),
                   jax.ShapeDtypeStruct((m, 2 * kd), F32), jax.ShapeDtypeStruct((1, d), F32),
                   jax.ShapeDtypeStruct((1, n_heads), F32)],
        compiler_params=_params(("arbitrary",)),
    )(q, kv, kv, sinks, lse, dao)


def _kv_grad_combine(dkv_cur, dkv_prev):
    m, w = dkv_cur.shape
    nb = m // WINDOW
    per = max(g for g in (1, 2, 4) if nb % g == 0)
    rows, steps = per * WINDOW, nb // per

    def body(cur_ref, same_ref, next_ref, o_ref, sum_ref):
        @pl.when(pl.program_id(0) == 0)
        def _():
            sum_ref[...] = jnp.zeros(sum_ref.shape, F32)

        after = jnp.where(pl.program_id(0) < steps - 1, next_ref[...], 0.0)
        total = cur_ref[...] + (jnp.concatenate([same_ref[WINDOW:, :], after], axis=0) if per > 1 else after)
        o_ref[...] = total.astype(o_ref.dtype)
        sum_ref[...] += _col_sum(total)

    return pl.pallas_call(
        body, name="kv_grad_combine", grid=(steps,),
        in_specs=[pl.BlockSpec((rows, w), lambda n: (n, 0)), pl.BlockSpec((rows, w), lambda n: (n, 0)),
                  pl.BlockSpec((WINDOW, w), lambda n: (jnp.minimum((n + 1) * per, nb - 1), 0))],
        out_specs=[pl.BlockSpec((rows, w), lambda n: (n, 0)), pl.BlockSpec((1, w), lambda n: (0, 0))],
        out_shape=[jax.ShapeDtypeStruct((m, w), BF16), jax.ShapeDtypeStruct((1, w), F32)],
        compiler_params=_params(("arbitrary",)),
    )(dkv_cur, dkv_prev, dkv_prev)


def _row(v):
    return v.reshape(1, -1)


def _local_step(x, p, target, wget, grad_sink, ln_gain, ln_bias, alb, norm_gain, kv_b, b_q, sinks, b_out, ple_b,
                small_sink=None):
    gs = {}
    gains = ln_gain.reshape(DEPTH * 3, -1)
    biases = ln_bias.reshape(DEPTH * 3, -1)
    sd = x.shape
    pending = [None]

    def mm(a, b, lb=0, **kw):
        after, pending[0] = pending[0], None
        return _mm(a, b, lb=lb, after=after, **kw)

    def mm_ln(a, wt, xin, i, j, nm, bias=None, pu=None):
        r = 3 * i + j
        if pu is None:
            fn, rows = (lambda h, xv, g, bv: (h,) + _ln_fwd_fn(xv, h, g[r:r + 1], bv[r:r + 1])), [xin]
        else:
            fn = lambda h, xv, puv, g, bv: (h,) + _ple_ln_fwd_fn(xv, h, puv, g[r:r + 1], bv[r:r + 1])
            rows = [xin, pu]
        h, y, yb = _mm(a, wt, lb=0, bias=bias, name=nm,
                       post=(fn, rows, [gains, biases], [(sd, F32), (sd, F32), (sd, BF16)], []))
        return h, (y, yb)

    def mm_ln_bwd(a, wt, add, xin, h, i, j, nm):
        r = 3 * i + j
        dx_part, dh, dg, db, dhsum = mm(a, wt, tb=True, add=add, name=nm,
                                        post=(lambda dy, xv, hv, g: _ln_bwd_fn(dy, xv, hv, g[r:r + 1]), [xin, h],
                                              [gains], [(sd, F32), (sd, BF16)], [((1, sd[1]), F32)] * 3))
        gs[f"ln_gain_{i}_{j}"], gs[f"ln_bias_{i}_{j}"] = dg, db
        return dx_part, dh, dhsum

    def tail_fwd(xa, i):
        wgu = wget("ffn_w_gate_up", i, xa[1])
        hid2 = wgu.shape[-1]
        gu, act = _mm(xa[1], wgu, lb=0, name=f"ffn_up_swiglu{i}", tile_cols=hid2 // 2,
                      post=(_swiglu_fwd_fn, [], [], [((sd[0], hid2), BF16), ((sd[0], hid2 // 2), BF16)], []))
        f, xb = mm_ln(act, wget("ffn_w_down", i, act), xa[0], i, 1, f"ffn_down_ln{i}")
        pu = _mm(p, wget("ple_w_up", i, act), la=i, lb=0, name=f"ple_up{i}")
        pg, xc = mm_ln(xb[1], wget("ple_w_gate", i, act), xb[0], i, 2, f"ple_gate_ln{i}", bias=_row(ple_b[i]), pu=pu)
        return dict(xa=xa, gu=gu, act=act, f=f, xb=xb, pg=pg, pu=pu), xc

    def tail_bwd(head, sv, i, mix_in, mix_h):
        xa, xb = sv["xa"], sv["xb"]
        r = 3 * i + 2
        dxb_part, dpg, dpu, dg2, db2, dbg = head(
            lambda dy, xv, pgv, puv, g: _ple_ln_bwd_fn(dy, xv, pgv, puv, g[r:r + 1]), [xb[0], sv["pg"], sv["pu"]],
            [gains], [(sd, F32), (sd, BF16), (sd, BF16)], [((1, sd[1]), F32)] * 3)[:6]
        gs[f"ple_b_{i}"] = dbg
        gs[f"ln_gain_{i}_2"], gs[f"ln_bias_{i}_2"] = dg2, db2
        grad_of("ple_w_gate", i, xb[1], dpg)
        grad_of("ple_w_up", i, p, dpu, la=i)
        dxa_part, df, _ = mm_ln_bwd(dpg, wget("ple_w_gate", i, None), dxb_part, xa[0], sv["f"], i, 1,
                                    f"ple_gate_dx_ln{i}")
        grad_of("ffn_w_down", i, sv["act"], df)
        gu = sv["gu"]
        dgu, = mm(df, wget("ffn_w_down", i, None), tb=True, name=f"ffn_down_dx_swiglu{i}", tile_cols=gu.shape[1] // 4,
                  post=(_swiglu_bwd_fn, [gu], [], [(gu.shape, BF16)], []))
        grad_of("ffn_w_gate_up", i, xa[1], dgu)
        return mm_ln_bwd(dgu, wget("ffn_w_gate_up", i, None), dxa_part, mix_in, mix_h, i, 0, f"ffn_up_dx_ln{i}")

    def grad_of(nm, i, act, dout, la=None, b_parts=None):
        grad = mm(act, dout, la=la, lb=None, ta=True, out_dtype=BF16, out_layers=1, out_layer=0,
                  name=f"grad_{nm}{i}", b_parts=b_parts)
        token = grad_sink(nm, i, grad)
        if token is not None:
            pending[0] = token

    proj = _mm(x, wget("a_w_in", 0, None), lb=0, name="hg_proj")
    o_pre, og, states, scores = _hgrn2_fwd(proj, alb, norm_gain, rb=HG_ROWS)
    h0, x1 = mm_ln(og, wget("a_w_out", 0, og), x, 0, 0, "hg_out_ln")
    sv0, x3 = tail_fwd(x1, 0)
    kv = _mm(x3[1], wget("kv_w", 0, x3[1]), lb=0, bias=_row(kv_b), out_dtype=BF16, name="kv_proj")
    q = _mm(x3[1], wget("b_w_q", 0, x3[1]), lb=0, bias=b_q, out_dtype=BF16, name="q_proj")
    ao, lse = _swa_fwd(q, kv, sinks)
    h1, x4 = mm_ln(ao, wget("b_w_out", 0, x3[1]), x3[0], 1, 0, "att_out_ln", bias=b_out)
    sv1, y = tail_fwd(x4, 1)

    loss_box = []

    def loss_head(fn, rows, whole, outs, sums):
        def with_loss(yv, tv, *rest):
            dy, part = _loss_fn(yv, tv)
            return fn(dy, *rest) + (part,)

        res = _rowwise(with_loss, [y[0], target] + rows, whole, outs, list(sums) + [((1, LANES), F32)],
                       name="loss_ln_ple_bwd1")
        loss_box.append(res[-1])
        return res

    dx3_part, dh1, dh1sum = tail_bwd(loss_head, sv1, 1, x3[0], h1)
    loss = loss_box[0]
    gs["b_out"] = dh1sum
    grad_of("b_w_out", 0, ao, dh1)
    dao = mm(dh1, wget("b_w_out", 0, None), tb=True, out_dtype=BF16, name="att_out_dx")
    dq, dkv_cur, dkv_prev, dqsum, dsinks = _swa_bwd(q, kv, sinks, lse, dao)
    gs["b_q"], gs["sinks"] = dqsum, dsinks
    dkv, dkvsum = _kv_grad_combine(dkv_cur, dkv_prev)
    gs["kv_b"] = dkvsum
    grad_of("b_w_q", 0, x3[1], dq)
    grad_of("kv_w", 0, x3[1], dkv)
    dx3 = mm(dq, wget("b_w_q", 0, None), tb=True, add=dx3_part, name="q_proj_dx")

    def kv_head(*post):
        return mm(dkv, wget("kv_w", 0, None), tb=True, add=dx3, name="kv_proj_dx_ln_ple_bwd0", post=post)

    dx_part, dh0, _ = tail_bwd(kv_head, sv0, 0, x, h0)
    grad_of("a_w_out", 0, og, dh0)
    dog = mm(dh0, wget("a_w_out", 0, None), tb=True, name="hg_out_dx")
    dproj, dalb, dgain = _hgrn2_bwd(proj, o_pre, states, scores, dog, alb, norm_gain, rb=HG_ROWS)
    gs["alb"], gs["norm_gain"] = dalb, dgain
    if small_sink is not None:
        pending[0] = small_sink(loss, gs)
    grad_of("a_w_in", 0, x, dproj, b_parts=4)
    grad_x = mm(dproj, wget("a_w_in", 0, None), tb=True, add=dx_part, name="hg_proj_dx", a_parts=4)
    return loss, grad_x, gs


HBM_SPEC = pl.BlockSpec(memory_space=pl.ANY)
HBM_ONLY = pl.BlockSpec(memory_space=pltpu.HBM)
SEM_SPEC = pl.BlockSpec(memory_space=pltpu.SEMAPHORE)
SIDE_EFFECT = pltpu.SideEffectType.DATAFLOW_SIDE_EFFECTING


def _slot(kind, j):
    return (j % 2) * 2 + j // 2 if kind == "colp" else j


def _piece(ref, kind, j):
    _, r, c = ref.shape
    if kind == "row":
        return ref.at[:, pl.ds(j * (r // N_CHIPS), r // N_CHIPS), :]
    return ref.at[:, :, pl.ds(_slot(kind, j) * (c // N_CHIPS), c // N_CHIPS)]


def _piece_dyn(ref, kind, j):
    _, r, c = ref.shape
    if kind == "row":
        return ref.at[:, pl.ds(pl.multiple_of(j * (r // N_CHIPS), 16), r // N_CHIPS), :]
    return ref.at[:, :, pl.ds(pl.multiple_of(_slot(kind, j) * (c // N_CHIPS), LANES), c // N_CHIPS)]


def _chip_of(j, c):
    return (j // 2, j % 2, c)


def _in_hbm(a):
    return pltpu.with_memory_space_constraint(a, pltpu.HBM)


PLACE_STEPS = 4


def _place(items, chip, *, name, after=None):
    n = len(items)
    in_specs, out_specs, out_shapes, blocks = [], [], [], []
    for src, layer, kind, out_dtype in items:
        _, r, c = src.shape
        nb = max(k for k in (1, 2, PLACE_STEPS) if r % (16 * k) == 0 or k == 1)
        blocks.append(nb)

        def src_idx(i, chip_ref, layer=layer, nb=nb):
            return (layer, jnp.minimum(i, nb - 1), 0)

        def full_idx(i, chip_ref, kind=kind, nb=nb):
            ib = jnp.minimum(i, nb - 1)
            return (0, chip_ref[0] * nb + ib, 0) if kind == "row" else (0, ib, _slot(kind, chip_ref[0]))

        in_specs.append(pl.BlockSpec((None, r // nb, c), src_idx))
        out_specs.append(pl.BlockSpec((None, r // nb, c), full_idx))
        out_shapes.append(jax.ShapeDtypeStruct((1, r * N_CHIPS, c) if kind == "row" else (1, r, c * N_CHIPS),
                                               out_dtype))
    operands = [it[0] for it in items]
    if after is not None:
        in_specs.append(HBM_SPEC)
        operands.append(after)

    def body(chip_ref, *refs):
        for a in range(n):
            refs[len(refs) - n + a][...] = refs[a][...].astype(refs[len(refs) - n + a].dtype)

    return pl.pallas_call(
        body, name=name,
        grid_spec=pltpu.PrefetchScalarGridSpec(num_scalar_prefetch=1, grid=(PLACE_STEPS,), in_specs=in_specs,
                                               out_specs=out_specs),
        out_shape=out_shapes,
        compiler_params=_params(("arbitrary",)),
    )(chip, *operands)


def _half(ref, c):
    h = ref.shape[1] // 2
    start = c * h if isinstance(c, int) else pl.multiple_of(c * h, 16)
    return ref.at[:, pl.ds(start, h), :]


def _sibling_handshake():
    barrier = pltpu.get_barrier_semaphore()
    sibling = (lax.axis_index("x"), lax.axis_index("y"), 1 - lax.axis_index("c"))
    pl.semaphore_signal(barrier, inc=1, device_id=sibling, device_id_type=MESH)
    pl.semaphore_wait(barrier, 1)


class _SiblingFill:
    def __init__(self, lands, kinds, name, collective_id):
        self.kinds, self.name, self.n = kinds, name, len(lands)
        n = self.n
        sem_shape = pltpu.SemaphoreType.DMA((n * N_CHIPS,))

        def body(*refs):
            land_refs, send_sems, recv_sems, token = refs[:n], refs[n], refs[n + 1], refs[-1]
            _sibling_handshake()
            for cp in self._copies(land_refs, send_sems, recv_sems):
                cp.start()
            token[...] = jnp.zeros(token.shape, token.dtype)

        outs = pl.pallas_call(
            body, name=name + "_start",
            in_specs=[HBM_ONLY] * n,
            out_specs=[SEM_SPEC, SEM_SPEC] + [HBM_ONLY] * n + [pl.BlockSpec(memory_space=pltpu.VMEM)],
            out_shape=[sem_shape, sem_shape] + [pltpu.HBM(a.shape, a.dtype) for a in lands]
                      + [jax.ShapeDtypeStruct((8, LANES), F32)],
            input_output_aliases={i: i + 2 for i in range(n)},
            compiler_params=pltpu.CompilerParams(has_side_effects=SIDE_EFFECT, collective_id=collective_id),
        )(*[_in_hbm(a) for a in lands])
        self.send_sems, self.recv_sems, self.lands, self.token = outs[0], outs[1], list(outs[2:2 + n]), outs[-1]

    def _copies(self, land_refs, send_sems, recv_sems):
        x, y, c = lax.axis_index("x"), lax.axis_index("y"), lax.axis_index("c")
        me = 2 * x + y
        copies = []
        for a in range(self.n):
            for k in range(1, N_CHIPS):
                t = (me + k) % N_CHIPS
                slice_t = _piece_dyn(land_refs[a], self.kinds[a], t)
                got = _half(slice_t, c)
                copies.append(pltpu.make_async_remote_copy(
                    src_ref=got, dst_ref=got, send_sem=send_sems.at[a * N_CHIPS + k],
                    recv_sem=recv_sems.at[a * N_CHIPS + k], device_id=(x, y, 1 - c), device_id_type=MESH))
        return copies

    def wait(self, after):
        n = self.n

        def body(*refs):
            land_refs, send_sems, recv_sems = refs[:n], refs[n], refs[n + 1]
            for cp in self._copies(land_refs, send_sems, recv_sems):
                cp.wait_send()
                cp.wait_recv()

        operands = [_in_hbm(a) for a in self.lands] + [self.send_sems, self.recv_sems]
        in_specs = [HBM_ONLY] * n + [SEM_SPEC, SEM_SPEC]
        if after is not None:
            operands.append(after)
            in_specs.append(HBM_SPEC)
        outs = pl.pallas_call(
            body, name=self.name + "_wait",
            in_specs=in_specs, out_specs=[HBM_ONLY] * n,
            out_shape=[pltpu.HBM(a.shape, a.dtype) for a in self.lands],
            input_output_aliases={i: i for i in range(n)},
            compiler_params=pltpu.CompilerParams(has_side_effects=SIDE_EFFECT),
        )(*operands)
        return list(outs)


class _Exchange:
    def __init__(self, mode, srcs, lands, kinds, layers, name, collective_id, after=None, halves=None):
        self.mode, self.kinds, self.layers, self.name, self.n = mode, kinds, layers, name, len(lands)
        self.halves = halves if halves is not None else [False] * len(lands)
        n, ns = self.n, len(srcs)
        n_in = ns + n + (after is not None)
        sem_shape = pltpu.SemaphoreType.DMA((n * N_CHIPS,))

        def body(*refs):
            src_refs, land_refs = refs[:ns], refs[ns:ns + n]
            send_sems, recv_sems = refs[n_in], refs[n_in + 1]
            token = refs[-1]
            c = lax.axis_index("c")
            me = 2 * lax.axis_index("x") + lax.axis_index("y")
            barrier = pltpu.get_barrier_semaphore()
            for k in range(1, N_CHIPS):
                t = (me + k) % N_CHIPS
                pl.semaphore_signal(barrier, inc=1, device_id=(t // 2, t % 2, c), device_id_type=MESH)
            pl.semaphore_wait(barrier, N_CHIPS - 1)
            for j in range(N_CHIPS):
                @pl.when(me == j)
                def _():
                    for a in range(n):
                        for t in range(N_CHIPS):
                            if t != j:
                                src, dst = self._ends(src_refs, land_refs, a, j, t, c)
                                pltpu.make_async_remote_copy(
                                    src_ref=src, dst_ref=dst, send_sem=send_sems.at[a * N_CHIPS + t],
                                    recv_sem=recv_sems.at[a * N_CHIPS + j],
                                    device_id=_chip_of(t, c), device_id_type=MESH).start()
            token[...] = jnp.zeros(token.shape, token.dtype)

        arrays = list(srcs) + list(lands)
        operands = [_in_hbm(a) for a in arrays]
        in_specs = [HBM_ONLY] * (ns + n)
        if after is not None:
            operands.append(after)
            in_specs.append(HBM_SPEC)
        outs = pl.pallas_call(
            body, name=name + "_start",
            in_specs=in_specs,
            out_specs=[SEM_SPEC, SEM_SPEC] + [HBM_ONLY] * (ns + n) + [pl.BlockSpec(memory_space=pltpu.VMEM)],
            out_shape=[sem_shape, sem_shape] + [pltpu.HBM(a.shape, a.dtype) for a in arrays]
                      + [jax.ShapeDtypeStruct((8, LANES), F32)],
            input_output_aliases={i: i + 2 for i in range(ns + n)},
            compiler_params=pltpu.CompilerParams(has_side_effects=SIDE_EFFECT, collective_id=collective_id),
        )(*operands)
        self.send_sems, self.recv_sems = outs[0], outs[1]
        self.srcs, self.lands = list(outs[2:2 + ns]), list(outs[2 + ns:2 + ns + n])
        self.token = outs[-1]

    def _ends(self, src_refs, land_refs, a, me_j, peer, c):
        if self.mode == "gather":
            mine = _piece(land_refs[a], self.kinds[a], me_j)
            if self.halves[a]:
                mine = _half(mine, c)
            return mine, mine
        return _piece(src_refs[a], self.kinds[a], peer), land_refs[a].at[me_j, pl.ds(self.layers[a], 1)]

    def wait(self, after, lands=None):
        n, ns = self.n, len(self.srcs)
        lands = self.lands if lands is None else lands

        def body(*refs):
            src_refs, land_refs = refs[:ns], refs[ns:ns + n]
            send_sems, recv_sems = refs[ns + n], refs[ns + n + 1]
            c = lax.axis_index("c")
            me = 2 * lax.axis_index("x") + lax.axis_index("y")
            for j in range(N_CHIPS):
                @pl.when(me != j)
                def _():
                    for a in range(n):
                        sent, _ = self._ends(src_refs, land_refs, a, 0, j, c)
                        _, landed = self._ends(src_refs, land_refs, a, j, 0, c)
                        cp = pltpu.make_async_remote_copy(
                            src_ref=sent, dst_ref=landed, send_sem=send_sems.at[a * N_CHIPS + j],
                            recv_sem=recv_sems.at[a * N_CHIPS + j],
                            device_id=_chip_of(j, c), device_id_type=MESH)
                        cp.wait_send()
                        cp.wait_recv()

        arrays = self.srcs + list(lands)
        operands = [_in_hbm(a) for a in arrays] + [self.send_sems, self.recv_sems]
        in_specs = [HBM_ONLY] * (ns + n) + [SEM_SPEC, SEM_SPEC]
        if after is not None:
            operands.append(after)
            in_specs.append(HBM_SPEC)
        outs = pl.pallas_call(
            body, name=self.name + "_wait",
            in_specs=in_specs, out_specs=[HBM_ONLY] * (ns + n),
            out_shape=[pltpu.HBM(a.shape, a.dtype) for a in arrays],
            input_output_aliases={i: i for i in range(ns + n)},
            compiler_params=pltpu.CompilerParams(has_side_effects=SIDE_EFFECT),
        )(*operands)
        return list(outs[:ns]), list(outs[ns:])


def _sum_arrivals(zone, own_grads, kind, chip, name, after=None):
    _, layers, r, c = zone.shape
    tm = _pick_rows(r, 256)
    if layers * (r // tm) == 1 and r % 32 == 0:
        tm = r // 2
    nb = r // tm

    def own_idx(u):
        def idx(l, i, chip_ref):
            ib = jnp.where(l == u, i, 0)
            return (0, chip_ref[0] * nb + ib, 0) if kind == "row" else (0, ib, _slot(kind, chip_ref[0]))
        return idx

    def slot_idx(k):
        return lambda l, i, chip_ref: ((chip_ref[0] + k) % N_CHIPS, l, i, 0)

    in_specs = [pl.BlockSpec((None, None, tm, c), slot_idx(k)) for k in range(1, N_CHIPS)]
    in_specs += [pl.BlockSpec((None, tm, c), own_idx(u)) for u in range(len(own_grads))]
    operands = [zone] * (N_CHIPS - 1) + list(own_grads)
    if after is not None:
        in_specs.append(HBM_SPEC)
        operands.append(after)

    def body(chip_ref, *refs):
        slot_refs, own_refs, o_ref = refs[:N_CHIPS - 1], refs[N_CHIPS - 1:N_CHIPS - 1 + layers], refs[-1]
        own = own_refs[0][...]
        for u in range(1, layers):
            own = jnp.where(pl.program_id(0) == u, own_refs[u][...], own)
        acc = own.astype(F32)
        for ref in slot_refs:
            acc = acc + ref[...].astype(F32)
        o_ref[...] = acc.astype(o_ref.dtype)

    return pl.pallas_call(
        body, name=name,
        grid_spec=pltpu.PrefetchScalarGridSpec(
            num_scalar_prefetch=1, grid=(layers, nb), in_specs=in_specs,
            out_specs=pl.BlockSpec((tm, c), lambda l, i, chip_ref: (l * nb + i, 0))),
        out_shape=jax.ShapeDtypeStruct((layers * r, c), BF16),
        compiler_params=_params(("arbitrary", "arbitrary")),
    )(chip, *operands)


class _SiblingSwap:
    def __init__(self, arrays, name, collective_id, after=None):
        self.name, self.n = name, len(arrays)
        n = self.n
        n_in = n + (after is not None)
        sem_shape = pltpu.SemaphoreType.DMA((n,))

        def body(*refs):
            ins, send_sems, recv_sems = refs[:n], refs[n_in], refs[n_in + 1]
            theirs, token = refs[n_in + 2 + n:n_in + 2 + 2 * n], refs[-1]
            _sibling_handshake()
            for cp in self._copies(ins, theirs, send_sems, recv_sems):
                cp.start()
            token[...] = jnp.zeros(token.shape, token.dtype)

        operands, in_specs = [_in_hbm(a) for a in arrays], [HBM_ONLY] * n
        if after is not None:
            operands.append(after)
            in_specs.append(HBM_SPEC)
        outs = pl.pallas_call(
            body, name=name + "_start",
            in_specs=in_specs,
            out_specs=[SEM_SPEC, SEM_SPEC] + [HBM_ONLY] * (2 * n) + [pl.BlockSpec(memory_space=pltpu.VMEM)],
            out_shape=[sem_shape, sem_shape] + [pltpu.HBM(a.shape, a.dtype) for a in arrays] * 2
                      + [jax.ShapeDtypeStruct((8, LANES), F32)],
            input_output_aliases={i: i + 2 for i in range(n)},
            compiler_params=pltpu.CompilerParams(has_side_effects=SIDE_EFFECT, collective_id=collective_id),
        )(*operands)
        self.send_sems, self.recv_sems = outs[0], outs[1]
        self.mine, self.theirs, self.token = list(outs[2:2 + n]), list(outs[2 + n:2 + 2 * n]), outs[-1]

    def _copies(self, mine, theirs, send_sems, recv_sems):
        sibling = (lax.axis_index("x"), lax.axis_index("y"), 1 - lax.axis_index("c"))
        return [pltpu.make_async_remote_copy(src_ref=mine[a], dst_ref=theirs[a], send_sem=send_sems.at[a],
                                             recv_sem=recv_sems.at[a], device_id=sibling, device_id_type=MESH)
                for a in range(self.n)]

    def wait(self, after):
        n = self.n

        def body(*refs):
            for cp in self._copies(refs[:n], refs[n:2 * n], refs[2 * n], refs[2 * n + 1]):
                cp.wait_send()
                cp.wait_recv()

        arrays = self.mine + self.theirs
        outs = pl.pallas_call(
            body, name=self.name + "_wait",
            in_specs=[HBM_ONLY] * (2 * n) + [SEM_SPEC, SEM_SPEC, HBM_SPEC], out_specs=[HBM_ONLY] * (2 * n),
            out_shape=[pltpu.HBM(a.shape, a.dtype) for a in arrays],
            input_output_aliases={i: i for i in range(2 * n)},
            compiler_params=pltpu.CompilerParams(has_side_effects=SIDE_EFFECT),
        )(*[_in_hbm(a) for a in arrays], self.send_sems, self.recv_sems, after)
        return list(outs[:n]), list(outs[n:])


class _GatherDevices:
    def __init__(self, vec):
        sem_shape = pltpu.SemaphoreType.DMA((N_DEV,))

        def body(in_ref, send_sems, recv_sems, vec_ref, out_ref, token):
            for cp in self._copies(in_ref, out_ref, send_sems, recv_sems):
                cp.start()
            token[...] = jnp.zeros(token.shape, token.dtype)

        outs = pl.pallas_call(
            body, name="gather_small_start",
            in_specs=[HBM_ONLY],
            out_specs=[SEM_SPEC, SEM_SPEC, HBM_ONLY, HBM_ONLY, pl.BlockSpec(memory_space=pltpu.VMEM)],
            out_shape=[sem_shape, sem_shape, pltpu.HBM(vec.shape, vec.dtype),
                       pltpu.HBM((N_DEV,) + vec.shape, vec.dtype), jax.ShapeDtypeStruct((8, LANES), F32)],
            input_output_aliases={0: 2},
            compiler_params=pltpu.CompilerParams(has_side_effects=SIDE_EFFECT),
        )(_in_hbm(vec))
        self.send_sems, self.recv_sems, self.vec, self.rows, self.token = outs

    def _copies(self, in_ref, out_ref, send_sems, recv_sems):
        x, y, c = lax.axis_index("x"), lax.axis_index("y"), lax.axis_index("c")
        me = 4 * x + 2 * y + c
        copies = [pltpu.make_async_copy(in_ref, out_ref.at[me], recv_sems.at[0])]
        for rel in range(1, N_DEV):
            peer = (x ^ (rel >> 2), y ^ ((rel >> 1) & 1), c ^ (rel & 1))
            copies.append(pltpu.make_async_remote_copy(
                src_ref=in_ref, dst_ref=out_ref.at[me], send_sem=send_sems.at[rel], recv_sem=recv_sems.at[rel],
                device_id=peer, device_id_type=MESH))
        return copies

    def wait(self, after):
        def body(vec_ref, rows_ref, send_sems, recv_sems, after_ref, vec_out, rows_out):
            copies = self._copies(vec_ref, rows_ref, send_sems, recv_sems)
            copies[0].wait()
            for cp in copies[1:]:
                cp.wait_send()
                cp.wait_recv()

        outs = pl.pallas_call(
            body, name="gather_small_wait",
            in_specs=[HBM_ONLY, HBM_ONLY, SEM_SPEC, SEM_SPEC, HBM_SPEC], out_specs=[HBM_ONLY, HBM_ONLY],
            out_shape=[pltpu.HBM(self.vec.shape, self.vec.dtype), pltpu.HBM(self.rows.shape, self.rows.dtype)],
            input_output_aliases={0: 0, 1: 1},
            compiler_params=pltpu.CompilerParams(has_side_effects=SIDE_EFFECT),
        )(_in_hbm(self.vec), _in_hbm(self.rows), self.send_sems, self.recv_sems, after)
        return outs[1]


BIG = [("a_w_in", "col"), ("a_w_out", "row"), ("kv_w", "row"), ("b_w_q", "row"), ("b_w_out", "row"),
       ("ffn_w_gate_up", "colp"), ("ffn_w_down", "row"), ("ple_w_up", "col"), ("ple_w_gate", "row")]
GATHER_GROUPS = [[("a_w_in", 0), ("small", 0)], [("a_w_out", 0)],
                 [("ffn_w_gate_up", 0), ("ffn_w_down", 0), ("ple_w_gate", 0), ("ple_w_up", 0)],
                 [("kv_w", 0), ("b_w_q", 0), ("b_w_out", 0)],
                 [("ffn_w_gate_up", 1)], [("ffn_w_down", 1), ("ple_w_gate", 1), ("ple_w_up", 1)]]
SCATTER_GROUPS = [[("ple_w_gate", 1), ("ple_w_up", 1), ("ffn_w_down", 1)], [("ffn_w_gate_up", 1)],
                  [("b_w_out", 0), ("b_w_q", 0), ("kv_w", 0)], [("ple_w_gate", 0), ("ple_w_up", 0), ("ffn_w_down", 0)],
                  [("ffn_w_gate_up", 0), ("a_w_out", 0)], [("a_w_in", 0)]]
COLLECTIVE_IDS = {"fill": 0, "swap": 6, "gather": 9, "scatter": 15}
SMALL_SHARDED = ["ln_gain", "ln_bias", "a_lower_bound"]
SMALL_REPLICATED = ["a_norm_gain", "kv_b", "b_b_q", "b_sinks", "b_b_out", "ple_b_gate"]
WEIGHT_ORDER = ["a_w_in", "a_lower_bound", "a_norm_gain", "a_w_out", "kv_w", "kv_b", "b_w_q", "b_b_q", "b_sinks",
                "b_w_out", "b_b_out", "ffn_w_gate_up", "ffn_w_down", "ple_w_up", "ple_w_gate", "ple_b_gate",
                "ln_gain", "ln_bias"]


def _as3(a):
    return a.reshape((-1,) + a.shape[-2:]) if a.ndim >= 3 else a.reshape((1,) + a.shape)


def _pad_lanes(v):
    n = v.shape[-1]
    return jnp.pad(v, ((0, 0), (0, (-n) % LANES)))


ADAM_MANY_STEPS = 4
ADAM_MANY_MAX = 1 << 19


def _adam_many(groups, name):
    in_specs, out_specs, out_shapes, arrays = [], [], [], []
    for group in groups:
        r, c = group[0].shape
        block = pl.BlockSpec((r // ADAM_MANY_STEPS, c), lambda i: (i, 0))
        in_specs += [block] * len(group)
        arrays += list(group)
        out_specs += [block] * 4
        out_shapes += [jax.ShapeDtypeStruct((r, c), F32)] * 4

    def body(*refs):
        ins, outs = refs[:len(arrays)], refs[len(arrays):]
        for k in range(len(groups)):
            res = _adam_fn(*[ref[...] for ref in ins[5 * k:5 * k + 5]])
            for out_ref, val in zip(outs[4 * k:4 * k + 4], res):
                out_ref[...] = val

    result = pl.pallas_call(
        body, name=name, grid=(ADAM_MANY_STEPS,), in_specs=in_specs, out_specs=out_specs, out_shape=out_shapes,
        compiler_params=_params(("parallel",)),
    )(*arrays)
    return [result[4 * k:4 * k + 4] for k in range(len(groups))]


def _adam_small(everyone, chip, items, loss_off):
    n_items = len(items)

    def body(chip_ref, every_ref, *refs):
        ins, outs = refs[:3 * n_items], refs[3 * n_items:]

        def total(off, width):
            acc = every_ref[0, :, off:off + width]
            for s in range(1, N_DEV):
                acc = acc + every_ref[s, :, off:off + width]
            return acc

        for a, (w, _, _, off, sharded) in enumerate(items):
            cols = w.shape[-1]
            for r in range(w.size // cols):
                at = (slice(r, r + 1),) if w.ndim == 2 else (r // w.shape[1], slice(r % w.shape[1], r % w.shape[1] + 1))
                if sharded:
                    full = total(off + r * N_CHIPS * cols, N_CHIPS * cols)
                    g = full[:, 0:cols]
                    for c in range(1, N_CHIPS):
                        g = jnp.where(chip_ref[0] == c, full[:, c * cols:(c + 1) * cols], g)
                else:
                    g = total(off + r * cols, cols)
                w_ref, m_ref, v_ref = ins[3 * a:3 * a + 3]
                res = _adam_fn(w_ref[at], m_ref[at], v_ref[at], g, jnp.zeros_like(g))
                for out_ref, val in zip(outs[4 * a:4 * a + 4], res):
                    out_ref[at] = val
        outs[-1][...] = total(loss_off, LANES)

    def whole(shape):
        return pl.BlockSpec(tuple(shape), lambda i, chip_ref: (0,) * len(shape))

    arrays = [arr for it in items for arr in it[:3]]
    out_shapes = [jax.ShapeDtypeStruct(it[0].shape, F32) for it in items for _ in range(4)]
    out_shapes.append(jax.ShapeDtypeStruct((1, LANES), F32))
    result = pl.pallas_call(
        body, name="adam_small",
        grid_spec=pltpu.PrefetchScalarGridSpec(
            num_scalar_prefetch=1, grid=(1,),
            in_specs=[whole(everyone.shape)] + [whole(arr.shape) for arr in arrays],
            out_specs=[whole(s.shape) for s in out_shapes]),
        out_shape=out_shapes,
        compiler_params=_params(("arbitrary",)),
    )(chip, everyone, *arrays)
    return [result[4 * a:4 * a + 4] for a in range(n_items)], result[-1]


def kernel(x, p, a_w_in, a_lower_bound, a_norm_gain, a_w_out, kv_w, kv_b, b_w_q, b_b_q, b_sinks, b_w_out, b_b_out, ffn_w_gate_up, ffn_w_down, ple_w_up, ple_w_gate, ple_b_gate, ln_gain, ln_bias, loss_target, m_a_w_in, m_a_lower_bound, m_a_norm_gain, m_a_w_out, m_kv_w, m_kv_b, m_b_w_q, m_b_b_q, m_b_sinks, m_b_w_out, m_b_b_out, m_ffn_w_gate_up, m_ffn_w_down, m_ple_w_up, m_ple_w_gate, m_ple_b_gate, m_ln_gain, m_ln_bias, v_a_w_in, v_a_lower_bound, v_a_norm_gain, v_a_w_out, v_kv_w, v_kv_b, v_b_w_q, v_b_b_q, v_b_sinks, v_b_w_out, v_b_b_out, v_ffn_w_gate_up, v_ffn_w_down, v_ple_w_up, v_ple_w_gate, v_ple_b_gate, v_ln_gain, v_ln_bias):
    args = dict(locals())
    wts = {n: args[n] for n in WEIGHT_ORDER}
    mom = {n: args["m_" + n] for n in WEIGHT_ORDER}
    vel = {n: args["v_" + n] for n in WEIGHT_ORDER}
    chip = 2 * lax.axis_index("x") + lax.axis_index("y")
    d = x.shape[-1]
    dq = d // N_CHIPS

    kind_of = dict(BIG)
    kind_of["small"] = "col"
    chip_arr = chip.reshape(1).astype(jnp.int32)
    small_pack = jnp.concatenate([wts[n].reshape(-1, dq) for n in SMALL_SHARDED], axis=0)[None]

    def place_item(key):
        n, layer = key
        if n == "small":
            return small_pack, 0, "col", F32
        return _as3(wts[n]), layer, kind_of[n], BF16

    gathers, where = [], {}
    for gi, group in enumerate(GATHER_GROUPS):
        prev = gathers[-1].token if gathers else None
        placed = _place([place_item(k) for k in group], chip_arr, name=f"place{gi}", after=prev)
        gathers.append(_Exchange("gather", [], placed, [kind_of[k[0]] for k in group],
                                 [0] * len(group), f"gather{gi}", COLLECTIVE_IDS["gather"] + gi, after=prev,
                                 halves=[k[0] != "small" for k in group]))
        for k in group:
            where[k] = gi
    all_started = gathers[-1].token
    ready = {}

    fills = {}

    def pass_on(gi, after):
        if gi not in fills:
            group = GATHER_GROUPS[gi]
            outs = gathers[gi].wait(after)[1]
            split = [i for i, k in enumerate(group) if k[0] != "small"]
            fills[gi] = (outs, split, _SiblingFill([outs[i] for i in split], [kind_of[group[i][0]] for i in split],
                                                   f"fill{gi}", COLLECTIVE_IDS["fill"] + gi))

    def wget(name, layer, after):
        key = (name, layer)
        if key not in ready:
            gi = where[key]
            after = all_started if gi == 0 else after
            pass_on(gi, after)
            if 1 <= gi < len(GATHER_GROUPS) - 1:
                pass_on(gi + 1, after)
                after = fills[gi + 1][2].token
            outs, split, fill = fills[gi]
            for i, arr in zip(split, fill.wait(after)):
                outs[i] = arr
            for k, arr in zip(GATHER_GROUPS[gi], outs):
                ready[k] = arr
        return ready[key]

    small_full = wget("small", 0, None)[0]
    ln_gain_f = small_full[0:6].reshape(DEPTH, 3, d)
    ln_bias_f = small_full[6:12].reshape(DEPTH, 3, d)
    alb_f = small_full[12:14]

    group_of = {k: gi for gi, group in enumerate(SCATTER_GROUPS) for k in group}
    grads_done, zones, scatters = {}, {}, []

    def grad_sink(name, layer, grad):
        grads_done[(name, layer)] = grad
        if name not in zones:
            zones[name] = lax.empty((N_CHIPS,) + _as3(wts[name]).shape, BF16)
        gi = group_of[(name, layer)]
        group = SCATTER_GROUPS[gi]
        if not all(k in grads_done for k in group):
            return None
        ex = _Exchange("scatter", [grads_done[k] for k in group], [zones[k[0]] for k in group],
                       [kind_of[k[0]] for k in group], [k[1] for k in group], f"scatter{gi}",
                       COLLECTIVE_IDS["scatter"] + gi)
        for k, zone in zip(group, ex.lands):
            zones[k[0]] = zone
        scatters.append((ex, group))
        return ex.token

    small = {}

    def small_sink(loss, gs):
        ln_g = jnp.concatenate([gs[f"ln_gain_{i}_{j}"] for i in range(DEPTH) for j in range(3)], axis=0)
        ln_b = jnp.concatenate([gs[f"ln_bias_{i}_{j}"] for i in range(DEPTH) for j in range(3)], axis=0)
        ple_bg = jnp.concatenate([gs[f"ple_b_{i}"] for i in range(DEPTH)], axis=0)
        small["list"] = [ln_g.reshape(1, -1), ln_b.reshape(1, -1), gs["alb"].reshape(1, -1), gs["norm_gain"],
                         gs["kv_b"], gs["b_q"], _pad_lanes(gs["sinks"]), gs["b_out"], ple_bg.reshape(1, -1), loss]
        small["gather"] = _GatherDevices(jnp.concatenate(small["list"], axis=1))
        return small["gather"].token

    loss, grad_x, gs = _local_step(
        x[0], p.reshape((p.shape[0],) + p.shape[2:]), loss_target[0], wget, grad_sink, ln_gain_f, ln_bias_f, alb_f, a_norm_gain, kv_b, b_b_q,
        b_sinks, b_b_out, ple_b_gate, small_sink)

    res = {}

    def arrive(batch, after):
        for ex, group in batch:
            srcs, outs = ex.wait(after, lands=[zones[k[0]] for k in group])
            for k, grad, zone in zip(group, srcs, outs):
                grads_done[k], zones[k[0]] = grad, zone

    def half_sums(names, batch, after):
        partial = []
        for n in names:
            own = [grads_done[(n, layer)] for layer in range(zones[n].shape[1])]
            partial.append(_sum_arrivals(zones[n], own, kind_of[n], chip_arr, f"sum_{n}", after=after))
        return _SiblingSwap(partial, f"swap{batch}", COLLECTIVE_IDS["swap"] + batch, after=after)

    def update(names, swap, after):
        flat = lambda a: a.reshape(-1, a.shape[-1])
        work = [(n, [flat(wts[n]), flat(mom[n]), flat(vel[n]), own, sib]) for n, own, sib in zip(names, *swap.wait(after))]
        many = [(n, ops) for n, ops in work if wts[n].size <= ADAM_MANY_MAX]
        if len(many) > 1:
            for (n, _), out in zip(many, _adam_many([ops for _, ops in many], f"adam_from_{many[0][0]}")):
                res[n] = [o.reshape(wts[n].shape) for o in out]
        for n, ops in work:
            if n not in res:
                out = _rowwise(_adam_fn, ops, [], [(ops[3].shape, F32)] * 4, name=f"adam_{n}")
                res[n] = [o.reshape(wts[n].shape) for o in out]
        return res[names[-1]][1]

    last_names = [k[0] for k in SCATTER_GROUPS[-1]]
    batches = [["ffn_w_gate_up"], [n for n, _ in BIG if n != "ffn_w_gate_up" and n not in last_names], last_names]
    arrive(scatters[:-1], grad_x)
    swap0 = half_sums(batches[0], 0, None)
    swap1 = half_sums(batches[1], 1, swap0.token)
    updated = update(batches[0], swap0, swap1.token)
    arrive(scatters[-1:], updated)
    swap2 = half_sums(batches[2], 2, swap1.token)
    updated = update(batches[1], swap1, swap2.token)
    update(batches[2], swap2, updated)

    everyone = small["gather"].wait(grad_x)
    offs, pos = [], 0
    for v in small["list"]:
        offs.append(pos)
        pos += v.shape[1]
    names = ["ln_gain", "ln_bias", "a_lower_bound", "a_norm_gain", "kv_b", "b_b_q", "b_sinks", "b_b_out", "ple_b_gate"]
    as_rows = lambda a: a.reshape(1, -1) if a.ndim == 1 else a
    items = [(as_rows(wts[n]), as_rows(mom[n]), as_rows(vel[n]), off, n in SMALL_SHARDED)
             for n, off in zip(names, offs)]
    updates, loss_row = _adam_small(everyone, chip_arr, items, offs[len(names)])
    for n, upd in zip(names, updates):
        res[n] = [u.reshape(wts[n].shape) for u in upd]

    outs = [loss_row[0, 0], grad_x[None]]
    for k in range(4):
        outs += [res[n][k] for n in WEIGHT_ORDER]
    return tuple(outs)
```

```python
import functools

import jax
import jax.numpy as jnp
from jax import lax
from jax.experimental import pallas as pl
from jax.experimental.pallas import tpu as pltpu

F32 = jnp.float32
BF16 = jnp.bfloat16
MESH = pl.DeviceIdType.MESH

LANES = 128
HG_DK = 128
HG_CHUNK = 64
HG_SUB = 16
HG_ROWS = 512
HG_HEADS_PER_STEP = 2
LOG2_E = 1.4426950408889634
ATT_HD = 64
ATT_G = 4
WINDOW = 128
DEPTH = 2
ALPHA = (2.0 * DEPTH) ** 0.25
LN_EPS = 1e-5
RMS_EPS = 1e-6
ADAM_LR, ADAM_B1, ADAM_B2, ADAM_EPS, ADAM_WD, ADAM_STEP = 0.001, 0.9, 0.999, 1e-08, 0.01, 10
N_CHIPS = 4
N_DEV = 8
VMEM_LIMIT = 56 * 1024 * 1024
NEG = -1e30


def _pick(n, cap):
    best = None
    for d in range(LANES, min(n, cap) + 1, LANES):
        if n % d == 0:
            best = d
    return n if best is None else best


def _pick_rows(m, cap):
    best = None
    for d in range(16, min(m, cap) + 1, 16):
        if m % d == 0:
            best = d
    return m if best is None else best


def _params(sem):
    return pltpu.CompilerParams(dimension_semantics=sem, vmem_limit_bytes=VMEM_LIMIT)


def _zeros_index(ndim, grid_rank=3):
    return (lambda i, j, kk: (0,) * ndim) if grid_rank == 3 else (lambda kk, i: (0,) * ndim)


def _mm(a, b, *, name, la=None, lb=None, ta=False, tb=False, bias=None, add=None, out_dtype=F32,
        out_layers=None, out_layer=None, after=None, post=None, tile_cols=None, caps=(1024, 1536, 2048),
        a_parts=None, b_parts=None):
    ar, ac = a.shape[-2:]
    br, bc = b.shape[-2:]
    assert a_parts is None or (not ta and la is None and a.shape[0] == a_parts)
    assert b_parts is None or (not tb and lb is None and b.shape[0] == b_parts)
    m, k = (ac, ar) if ta else (ar, ac * (a_parts or 1))
    k2, n = (bc, br) if tb else (br, bc * (b_parts or 1))
    assert k == k2, (a.shape, b.shape, ta, tb)
    if post is not None:
        caps = (512, n if tile_cols is None else tile_cols, caps[2])
    tm, tn, tk = _pick(m, caps[0]), _pick(bc if b_parts else n, caps[1]), _pick(ac if a_parts else k, caps[2])
    assert post is None or tn == caps[1]
    nk = k // tk
    gi, gj = m // tm, n // tn
    a_bytes, b_bytes = m * k * a.dtype.itemsize, k * n * b.dtype.itemsize
    rows_outer = (a_bytes + b_bytes * (gi if gj * nk > 1 else 1)) <= (b_bytes + a_bytes * (gj if gi * nk > 1 else 1))
    k_outer = post is not None and nk > 1 and gj == 1
    grid = (nk, gi) if k_outer else (gi, gj, nk) if rows_outer else (gj, gi, nk)
    keep_at = ta and nk == 1 and gj > 1 and rows_outer

    def bs(block, idx, late=False):
        if k_outer:
            return pl.BlockSpec(block, lambda kk, i: idx(jnp.where(kk == nk - 1, i, 0) if late else i, 0, kk))
        return pl.BlockSpec(block, idx if rows_outer else (lambda q, p, kk: idx(p, q, kk)))

    def spec(block, idx, layer):
        if layer is None:
            return bs(block, idx)
        return bs((None,) + block, lambda i, j, kk: (layer,) + idx(i, j, kk))

    a_spec = spec((tk, tm), lambda i, j, kk: (kk, i), la) if ta else spec((tm, tk), lambda i, j, kk: (i, kk), la)
    b_spec = spec((tn, tk), lambda i, j, kk: (j, kk), lb) if tb else spec((tk, tn), lambda i, j, kk: (kk, j), lb)
    if a_parts:
        a_spec = bs((None, tm, tk), lambda i, j, kk: (kk // (ac // tk), i, kk % (ac // tk)))
    if b_parts:
        b_spec = bs((None, tk, tn), lambda i, j, kk: (j // (bc // tn), kk, j % (bc // tn)))
    in_specs, operands = [a_spec, b_spec], [a, b]
    if bias is not None:
        in_specs.append(bs((1, tn), lambda i, j, kk: (0, j)))
        operands.append(bias)
    if add is not None:
        in_specs.append(bs((tm, tn), lambda i, j, kk: (i, j), late=True))
        operands.append(add)
    if after is not None:
        in_specs.append(pl.BlockSpec(memory_space=pl.ANY))
        operands.append(after)
    dims = (((0 if ta else 1,), (1 if tb else 0,)), ((), ()))
    has_bias, has_add = bias is not None, add is not None
    if post is None:
        fn, rows, whole, outs, sums = None, [], [], [], []
        out_shape = jax.ShapeDtypeStruct((m, n) if out_layers is None else (out_layers, m, n), out_dtype)
        out_specs = spec((tm, tn), lambda i, j, kk: (i, j), out_layer)
    else:
        fn, rows, whole, outs, sums = post
        in_specs += [bs((tm, r.shape[-1] // gj), lambda i, j, kk: (i, j), late=True) for r in rows]
        in_specs += [pl.BlockSpec(tuple(w.shape), _zeros_index(w.ndim, len(grid))) for w in whole]
        operands += list(rows) + list(whole)
        out_shape = [jax.ShapeDtypeStruct(sh, dt) for sh, dt in list(outs) + list(sums)]
        out_specs = ([bs((tm, sh[-1] // gj), lambda i, j, kk: (i, j), late=True) for sh, _ in outs]
                     + [pl.BlockSpec(tuple(sh), _zeros_index(len(sh), len(grid))) for sh, _ in sums])
    n_in, n_extra, n_outs, n_sums = len(operands), len(rows) + len(whole), len(outs), len(sums)

    def body(*refs):
        a_ref, b_ref = refs[0], refs[1]
        pos = 2
        bias_ref = add_ref = None
        if has_bias:
            bias_ref = refs[pos]
            pos += 1
        if has_add:
            add_ref = refs[pos]
            pos += 1
        extra_refs = refs[n_in - n_extra:n_in]
        out_refs = refs[n_in:n_in + max(n_outs, 1)]
        sum_refs = refs[n_in + n_outs:n_in + n_outs + n_sums]
        acc_ref = refs[-1] if nk > 1 else None
        if keep_at:
            at_ref = refs[-1]

            @pl.when(pl.program_id(1) == 0)
            def _():
                at_ref[...] = a_ref[...].astype(BF16).T

            part = lax.dot_general(at_ref[...], b_ref[...].astype(BF16), (((1,), (1 if tb else 0,)), ((), ())),
                                   preferred_element_type=F32)
        else:
            part = lax.dot_general(a_ref[...].astype(BF16), b_ref[...].astype(BF16), dims,
                                   preferred_element_type=F32)

        def finish(total):
            if has_bias:
                total = total + bias_ref[...]
            if has_add:
                total = total + add_ref[...]
            if fn is None:
                out_refs[0][...] = total.astype(out_refs[0].dtype)
                return
            res = fn(total, *[r[...] for r in extra_refs])
            for ref, val in zip(out_refs, res[:n_outs]):
                ref[...] = val.astype(ref.dtype)
            if n_sums:
                @pl.when(pl.program_id(1 if k_outer or not rows_outer else 0) == 0)
                def _():
                    for ref in sum_refs:
                        ref[...] = jnp.zeros(ref.shape, ref.dtype)

                for ref, val in zip(sum_refs, res[n_outs:]):
                    ref[...] += val

        if nk == 1:
            finish(part)
        elif k_outer:
            kk = pl.program_id(0)
            rows_i = pl.ds(pl.multiple_of(pl.program_id(1) * tm, tm), tm)

            @pl.when(kk == 0)
            def _():
                acc_ref[rows_i, :] = part

            @pl.when(kk > 0)
            def _():
                acc_ref[rows_i, :] += part

            @pl.when(kk == nk - 1)
            def _():
                finish(acc_ref[rows_i, :])
        else:
            kk = pl.program_id(2)

            @pl.when(kk == 0)
            def _():
                acc_ref[...] = part

            @pl.when(kk > 0)
            def _():
                acc_ref[...] += part

            @pl.when(kk == nk - 1)
            def _():
                finish(acc_ref[...])

    return pl.pallas_call(
        body, name=name, grid=grid, in_specs=in_specs, out_specs=out_specs, out_shape=out_shape,
        scratch_shapes=([pltpu.VMEM((m, n) if k_outer else (tm, tn), F32)] if nk > 1
                        else [pltpu.VMEM((tm, tk), BF16)] if keep_at else []),
        compiler_params=_params(("arbitrary", "arbitrary") if k_outer
                                else ("arbitrary" if n_sums else "parallel", "arbitrary" if keep_at else "parallel",
                                      "arbitrary") if rows_outer
                                else ("parallel", "arbitrary" if n_sums else "parallel", "arbitrary")),
    )(*operands)


def _rowwise(fn, rows, whole, outs, sums=(), *, name, tm=256):
    m = rows[0].shape[-2]
    tm = _pick_rows(m, tm)
    n_rows, n_whole, n_outs, n_sums = len(rows), len(whole), len(outs), len(sums)

    def rspec(shape):
        lead = len(shape) - 2
        return pl.BlockSpec(tuple(shape[:-2]) + (tm, shape[-1]), lambda i: (0,) * lead + (i, 0))

    def wspec(shape):
        return pl.BlockSpec(tuple(shape), lambda i: (0,) * len(shape))

    def body(*refs):
        vals = [r[...] for r in refs[:n_rows + n_whole]]
        out_refs = refs[n_rows + n_whole:n_rows + n_whole + n_outs]
        sum_refs = refs[n_rows + n_whole + n_outs:]
        res = fn(*vals)
        for ref, val in zip(out_refs, res[:n_outs]):
            ref[...] = val.astype(ref.dtype)
        if n_sums:
            @pl.when(pl.program_id(0) == 0)
            def _():
                for ref in sum_refs:
                    ref[...] = jnp.zeros(ref.shape, ref.dtype)

            for ref, val in zip(sum_refs, res[n_outs:]):
                ref[...] += val

    result = pl.pallas_call(
        body, name=name, grid=(m // tm,),
        in_specs=[rspec(r.shape) for r in rows] + [wspec(w.shape) for w in whole],
        out_specs=[rspec(s) for s, _ in outs] + [wspec(s) for s, _ in sums],
        out_shape=[jax.ShapeDtypeStruct(s, d) for s, d in list(outs) + list(sums)],
        compiler_params=_params(("arbitrary",)),
    )(*rows, *whole)
    return result


def _sigmoid(v):
    return jax.nn.sigmoid(v)


def _col_sum(v):
    return jnp.sum(v, axis=0, keepdims=True)


def _ln_stats(z):
    mu = jnp.mean(z, axis=-1, keepdims=True)
    zc = z - mu
    var = jnp.mean(zc * zc, axis=-1, keepdims=True)
    rstd = lax.rsqrt(var + LN_EPS)
    return zc * rstd, rstd


def _ln_fwd_fn(xin, h, gain, bias):
    xhat, _ = _ln_stats(ALPHA * xin + h)
    y = xhat * gain + bias
    return y, y


def _ple_ln_fwd_fn(xin, pg, pu, gain, bias):
    xhat, _ = _ln_stats(ALPHA * xin + _sigmoid(pg) * pu)
    y = xhat * gain + bias
    return y, y


def _ln_dz(dy, z, gain):
    xhat, rstd = _ln_stats(z)
    dxhat = dy * gain
    dz = rstd * (dxhat - jnp.mean(dxhat, axis=-1, keepdims=True)
                 - xhat * jnp.mean(dxhat * xhat, axis=-1, keepdims=True))
    return dz, _col_sum(dy * xhat), _col_sum(dy)


def _ln_bwd_fn(dy, xin, h, gain):
    dz, dgain, dbias = _ln_dz(dy, ALPHA * xin + h, gain)
    return ALPHA * dz, dz, dgain, dbias, _col_sum(dz)


def _ple_ln_bwd_fn(dy, xin, pg, pu, gain):
    sg = _sigmoid(pg)
    dz, dgain, dbias = _ln_dz(dy, ALPHA * xin + sg * pu, gain)
    dpg = dz * pu * sg * (1.0 - sg)
    return ALPHA * dz, dpg, dz * sg, dgain, dbias, _col_sum(dpg)


def _swiglu_fwd_fn(gu):
    hid = gu.shape[-1] // 2
    gate, up = gu[:, :hid], gu[:, hid:]
    return gu, gate * _sigmoid(gate) * up


def _swiglu_bwd_fn(dact, gu):
    gu = gu.astype(F32)
    hid = gu.shape[-1] // 2
    gate, up = gu[:, :hid], gu[:, hid:]
    sg = _sigmoid(gate)
    dgate = dact * up * sg * (1.0 + gate * (1.0 - sg))
    dup = dact * gate * sg
    return (jnp.concatenate([dgate, dup], axis=-1),)


def _loss_fn(y, target):
    err = y - target
    inv = 1.0 / y.shape[-1]
    part = 0.5 * inv * jnp.sum(jnp.sum(err * err, axis=-1, keepdims=True), axis=0, keepdims=True)
    return err * inv, jnp.broadcast_to(part, (1, LANES))


def _adam_fn(w, mom, vel, p_own, p_sib):
    g = p_own.astype(F32) + p_sib.astype(F32)
    m_new = ADAM_B1 * mom + (1.0 - ADAM_B1) * g
    v_new = ADAM_B2 * vel + (1.0 - ADAM_B2) * (g * g)
    m_hat = m_new / (1.0 - ADAM_B1 ** ADAM_STEP)
    v_hat = v_new / (1.0 - ADAM_B2 ** ADAM_STEP)
    delta = -ADAM_LR * (m_hat / (jnp.sqrt(v_hat) + ADAM_EPS) + ADAM_WD * w)
    return g, delta, m_new, v_new


def _split2(x):
    hi = x.astype(BF16)
    return hi, (x - hi.astype(F32)).astype(BF16)


def _dot3(a, b, dims):
    a_hi, a_lo = _split2(a)
    b_hi, b_lo = _split2(b)
    dn = (dims, ((), ()))
    return (lax.dot_general(a_hi, b_hi, dn, preferred_element_type=F32)
            + (lax.dot_general(a_hi, b_lo, dn, preferred_element_type=F32)
               + lax.dot_general(a_lo, b_hi, dn, preferred_element_type=F32)))


def _tdot(mask01, b):
    m = mask01.astype(BF16)
    b_hi = b.astype(BF16)
    rest = b - b_hi.astype(F32)
    b_mid = rest.astype(BF16)
    b_lo = (rest - b_mid.astype(F32)).astype(BF16)
    dn = (((1,), (0,)), ((), ()))
    return (lax.dot_general(m, b_hi, dn, preferred_element_type=F32)
            + (lax.dot_general(m, b_mid, dn, preferred_element_type=F32)
               + lax.dot_general(m, b_lo, dn, preferred_element_type=F32)))


def _hdot(a, b):
    return _dot3(a, b, ((1,), (0,)))


def _hdot_nt(a, b):
    return _dot3(a, b, ((1,), (1,)))


def _hdot_tn(a, b):
    return _dot3(a, b, ((0,), (0,)))


def _dot(a, b):
    return lax.dot_general(a.astype(BF16), b.astype(BF16), (((1,), (0,)), ((), ())), preferred_element_type=F32)


def _dot_nt(a, b):
    return lax.dot_general(a.astype(BF16), b.astype(BF16), (((1,), (1,)), ((), ())), preferred_element_type=F32)


def _dot_tn(a, b):
    return lax.dot_general(a.astype(BF16), b.astype(BF16), (((0,), (0,)), ((), ())), preferred_element_type=F32)


def _hg_masks():
    c = HG_CHUNK
    row = lax.broadcasted_iota(jnp.int32, (c, c), 0)
    col = lax.broadcasted_iota(jnp.int32, (c, c), 1)
    base = row & (-HG_SUB)
    return row, col, base, col <= row, col < base


def _hg_gates(qr, fr, alb):
    lbound = _sigmoid(alb[0:1, :] - alb[1:2, :])
    sig = _sigmoid(fr)
    forget = lbound + (1.0 - lbound) * sig
    kk = (1.0 - lbound) * _sigmoid(-fr)
    qt = qr * _sigmoid(qr) * (HG_DK ** -0.5)
    return qt, kk, jnp.log(forget), lbound, sig, forget


def _hg_scores(qt, kk, g, scores=True):
    c, nsub = HG_CHUNK, HG_CHUNK // HG_SUB
    row, col, base, causal, below = _hg_masks()
    b = _tdot(causal, g)
    rr = _tdot(below, g)
    bq = b - rr
    qh = qt * jnp.exp(bq)
    edecs = [None]
    parts = [jnp.zeros((HG_SUB, c), F32)]
    for i in range(1, nsub):
        edec = jnp.exp(jnp.minimum(rr[i * HG_SUB:i * HG_SUB + 1, :] - b, 0.0))
        edecs.append(edec)
        if scores:
            parts.append(_dot_nt(qh[i * HG_SUB:(i + 1) * HG_SUB, :], kk * edec))
    q3 = qt.reshape(nsub, HG_SUB, HG_DK)
    if not scores:
        return None, b, bq, qh, edecs, (b.reshape(nsub, HG_SUB, HG_DK), q3, kk.reshape(nsub, HG_SUB, HG_DK))
    a = jnp.where(below, jnp.concatenate(parts, axis=0), 0.0)
    b2 = b * LOG2_E
    b3 = b2.reshape(nsub, HG_SUB, HG_DK)
    c3 = (b2 - jnp.log2(kk)).reshape(nsub, HG_SUB, HG_DK)
    for j in range(HG_SUB):
        ek = jnp.exp2(b3 - c3[:, j:j + 1, :])
        colv = jnp.sum(q3 * ek, axis=-1, keepdims=True).reshape(c, 1)
        a = jnp.where(col == base + j, colv, a)
    a = jnp.where(causal, a, 0.0)
    return a, b, bq, qh, edecs, None


def _hg_norm(o, gr, gain):
    r = lax.rsqrt(jnp.mean(o * o, axis=-1, keepdims=True) + RMS_EPS)
    sg = _sigmoid(gr)
    return o * r * gain, r, sg


def _hgrn2_fwd(proj, alb, gain, *, rb):
    m, d4 = proj.shape
    d = d4 // 4
    heads = d // HG_DK
    hp = HG_HEADS_PER_STEP
    rb = min(rb, m)
    cpb = rb // HG_CHUNK
    nrb = m // rb

    def body(q_ref, f_ref, v_ref, g_ref, alb_ref, gain_ref, o_ref, og_ref, st_ref, a_ref, state):
        @pl.when(pl.program_id(1) == 0)
        def _():
            state[...] = jnp.zeros(state.shape, F32)

        def chunk(ci, carry):
            sl = pl.ds(pl.multiple_of(ci * HG_CHUNK, HG_CHUNK), HG_CHUNK)
            for u in range(hp):
                ln = slice(u * HG_DK, (u + 1) * HG_DK)
                qt, kk, g, _, _, _ = _hg_gates(q_ref[sl, ln], f_ref[sl, ln], alb_ref[:, ln])
                v = v_ref[sl, ln]
                st = state[u]
                st_ref[u, ci] = st
                a, b, _, _, _, _ = _hg_scores(qt, kk, g)
                a_ref[u, ci] = a.astype(a_ref.dtype)
                o = _dot(a, v) + _dot_nt(qt * jnp.exp(b), st)
                b_last = b[HG_CHUNK - 1:HG_CHUNK, :]
                state[u] = st * jnp.exp(b_last) + _hdot_tn(v, kk * jnp.exp(b_last - b))
                o_ref[sl, ln] = o
                n, _, sg = _hg_norm(o, g_ref[sl, ln], gain_ref[...])
                og_ref[sl, ln] = (n * g_ref[sl, ln] * sg).astype(og_ref.dtype)
            return carry

        lax.fori_loop(0, cpb, chunk, 0)

    def col(cidx):
        return pl.BlockSpec((rb, hp * HG_DK), lambda h, r: (r, cidx * (heads // hp) + h))

    return pl.pallas_call(
        body, name="hgrn2_fwd", grid=(heads // hp, nrb),
        in_specs=[col(0), col(1), col(2), col(3),
                  pl.BlockSpec((2, hp * HG_DK), lambda h, r: (0, h)),
                  pl.BlockSpec((1, HG_DK), lambda h, r: (0, 0))],
        out_specs=[pl.BlockSpec((rb, hp * HG_DK), lambda h, r: (r, h)),
                   pl.BlockSpec((rb, hp * HG_DK), lambda h, r: (r, h)),
                   pl.BlockSpec((hp, cpb, HG_DK, HG_DK), lambda h, r: (h, r, 0, 0)),
                   pl.BlockSpec((hp, cpb, HG_CHUNK, HG_CHUNK), lambda h, r: (h, r, 0, 0))],
        out_shape=[jax.ShapeDtypeStruct((m, d), F32), jax.ShapeDtypeStruct((m, d), BF16),
                   jax.ShapeDtypeStruct((heads, m // HG_CHUNK, HG_DK, HG_DK), F32),
                   jax.ShapeDtypeStruct((heads, m // HG_CHUNK, HG_CHUNK, HG_CHUNK), BF16)],
        scratch_shapes=[pltpu.VMEM((hp, HG_DK, HG_DK), F32)],
        compiler_params=_params(("parallel", "arbitrary")),
    )(proj, proj, proj, proj, alb, gain)


def _hgrn2_bwd(proj, o_pre, states, scores, dog, alb, gain, *, rb):
    m, d4 = proj.shape
    d = d4 // 4
    heads = d // HG_DK
    rb = min(rb, m)
    cpb = rb // HG_CHUNK
    nrb = m // rb
    c, nsub = HG_CHUNK, HG_CHUNK // HG_SUB

    def body(q_ref, f_ref, v_ref, g_ref, o_ref, st_ref, a_ref, dog_ref, alb_ref, gain_ref,
             dp_ref, dalb_ref, dgain_ref, dstate, carry_ref):
        first = (pl.program_id(0) == 0) & (pl.program_id(1) == 0)

        @pl.when(first)
        def _():
            dgain_ref[...] = jnp.zeros(dgain_ref.shape, F32)

        @pl.when(pl.program_id(1) == 0)
        def _():
            dstate[...] = jnp.zeros(dstate.shape, F32)
            carry_ref[...] = jnp.zeros(carry_ref.shape, F32)
            dalb_ref[...] = jnp.zeros(dalb_ref.shape, F32)

        row, col, base, causal, below = _hg_masks()
        sub_iota = lax.broadcasted_iota(jnp.int32, (nsub, HG_SUB, HG_DK), 1)
        row_k = lax.broadcasted_iota(jnp.int32, (c, HG_DK), 0)
        upper = col >= row

        def chunk(step, carry):
            ci = cpb - 1 - step
            sl = pl.ds(pl.multiple_of(ci * HG_CHUNK, HG_CHUNK), HG_CHUNK)
            qr, fr, v, gr = q_ref[sl, :], f_ref[sl, :], v_ref[sl, :], g_ref[sl, :]
            qt, kk, g, lbound, sig, forget = _hg_gates(qr, fr, alb_ref[...])
            o = o_ref[sl, :]
            dogv = dog_ref[sl, :]
            gain_v = gain_ref[...]
            n, r, sg = _hg_norm(o, gr, gain_v)
            dgr = dogv * n * sg * (1.0 + gr * (1.0 - sg))
            dn = dogv * gr * sg
            dgain_ref[...] += _col_sum(dn * o * r)
            u = dn * gain_v
            d_o = r * u - o * (r * r * r) * jnp.mean(u * o, axis=-1, keepdims=True)
            st0 = st_ref[ci]
            dst = dstate[...]
            _, b, bq, qh, edecs, (b3, q3, k3) = _hg_scores(qt, kk, g, scores=False)
            a = a_ref[ci]
            eb = jnp.exp(b)
            b_last = b[c - 1:c, :]
            kdl_dec = jnp.exp(b_last - b)
            kdl = kk * kdl_dec
            d_a = jnp.where(causal, _dot_nt(d_o, v), 0.0)
            d_at = _dot_nt(v, d_o)
            dv = _dot_tn(a, d_o) + _dot_nt(kdl, dst)
            dq = eb * _hdot(d_o, st0)
            dk = _hdot(v, dst) * kdl_dec
            d_a_below = jnp.where(below, d_a, 0.0)
            dq_parts = [jnp.zeros((HG_SUB, HG_DK), F32)]
            for i in range(1, nsub):
                lo, hi = i * HG_SUB, (i + 1) * HG_SUB
                dq_parts.append(_hdot(d_a_below[lo:hi, :], kk * edecs[i]))
                gi = _hdot(d_at[:, lo:hi], qh[lo:hi, :])
                dk = dk + jnp.where(row_k < lo, edecs[i] * gi, 0.0)
            dq = dq + jnp.concatenate(dq_parts, axis=0) * jnp.exp(bq)
            dq3 = jnp.zeros((nsub, HG_SUB, HG_DK), F32)
            dk3 = jnp.zeros((nsub, HG_SUB, HG_DK), F32)
            d_diag = jnp.concatenate([d_a[i * HG_SUB:(i + 1) * HG_SUB, i * HG_SUB:(i + 1) * HG_SUB]
                                      for i in range(nsub)], axis=0).reshape(nsub, HG_SUB, HG_SUB)
            for j in range(HG_SUB):
                e = jnp.exp(jnp.minimum(b3 - b3[:, j:j + 1, :], 0.0))
                t1 = d_diag[:, :, j:j + 1] * e
                dq3 = dq3 + t1 * k3[:, j:j + 1, :]
                dk3 = jnp.where(sub_iota == j, jnp.sum(t1 * q3, axis=1, keepdims=True), dk3)
            dq = dq + dq3.reshape(c, HG_DK)
            dk = dk + dk3.reshape(c, HG_DK)
            dstate[...] = dst * jnp.exp(b_last) + _hdot_tn(d_o, qt * eb)
            dglog = _tdot(upper, qt * dq - kk * dk) + carry_ref[...]
            carry_ref[...] = dglog[0:1, :]
            dforget = dglog / forget
            one_m_lb = 1.0 - lbound
            dsig = (dforget - dk) * one_m_lb
            sneg = _sigmoid(-fr)
            dlb = _col_sum(dforget * (1.0 - sig) - dk * sneg)
            dalb0 = dlb * lbound * one_m_lb
            dalb_ref[...] += jnp.concatenate([dalb0, -dalb0], axis=0)
            sq = _sigmoid(qr)
            dp_ref[0, sl, :] = (dq * (HG_DK ** -0.5) * sq * (1.0 + qr * (1.0 - sq))).astype(dp_ref.dtype)
            dp_ref[1, sl, :] = (dsig * sig * (1.0 - sig)).astype(dp_ref.dtype)
            dp_ref[2, sl, :] = dv.astype(dp_ref.dtype)
            dp_ref[3, sl, :] = dgr.astype(dp_ref.dtype)
            return carry

        lax.fori_loop(0, cpb, chunk, 0, unroll=2)

    def rev(r):
        return nrb - 1 - r

    def col(cidx):
        return pl.BlockSpec((rb, HG_DK), lambda h, r: (rev(r), cidx * heads + h))

    def head_rows():
        return pl.BlockSpec((rb, HG_DK), lambda h, r: (rev(r), h))

    return pl.pallas_call(
        body, name="hgrn2_bwd", grid=(heads, nrb),
        in_specs=[col(0), col(1), col(2), col(3), head_rows(),
                  pl.BlockSpec((None, cpb, HG_DK, HG_DK), lambda h, r: (h, rev(r), 0, 0)),
                  pl.BlockSpec((None, cpb, HG_CHUNK, HG_CHUNK), lambda h, r: (h, rev(r), 0, 0)),
                  head_rows(),
                  pl.BlockSpec((2, HG_DK), lambda h, r: (0, h)),
                  pl.BlockSpec((1, HG_DK), lambda h, r: (0, 0))],
        out_specs=[pl.BlockSpec((4, rb, HG_DK), lambda h, r: (0, rev(r), h)),
                   pl.BlockSpec((2, HG_DK), lambda h, r: (0, h)),
                   pl.BlockSpec((1, HG_DK), lambda h, r: (0, 0))],
        out_shape=[jax.ShapeDtypeStruct((4, m, d), BF16), jax.ShapeDtypeStruct((2, d), F32),
                   jax.ShapeDtypeStruct((1, HG_DK), F32)],
        scratch_shapes=[pltpu.VMEM((HG_DK, HG_DK), F32), pltpu.VMEM((1, HG_DK), F32)],
        compiler_params=_params(("arbitrary", "arbitrary")),
    )(proj, proj, proj, proj, o_pre, states, scores, dog, alb, gain)


def _swa_probs(qh, kp, kc, sink, slope, has_prev, lse=None):
    rows = qh.shape[0]
    qi = lax.broadcasted_iota(jnp.int32, (rows, WINDOW), 0) & (WINDOW - 1)
    si = lax.broadcasted_iota(jnp.int32, (rows, WINDOW), 1)
    scale = ATT_HD ** -0.5
    dist_c = (qi - si).astype(F32)
    s_p = _dot_nt(qh, kp) * scale - slope * (dist_c + float(WINDOW))
    s_c = _dot_nt(qh, kc) * scale - slope * dist_c
    s_p = jnp.where((si > qi) & has_prev, s_p, NEG)
    s_c = jnp.where(si <= qi, s_c, NEG)
    if lse is not None:
        return jnp.exp(s_p - lse), jnp.exp(s_c - lse), jnp.exp(sink - lse), lse
    mx = jnp.maximum(jnp.maximum(jnp.max(s_p, axis=-1, keepdims=True), jnp.max(s_c, axis=-1, keepdims=True)), sink)
    e_p, e_c, e_s = jnp.exp(s_p - mx), jnp.exp(s_c - mx), jnp.exp(sink - mx)
    total = jnp.sum(e_p, axis=-1, keepdims=True) + jnp.sum(e_c, axis=-1, keepdims=True) + e_s
    inv = 1.0 / total
    return e_p * inv, e_c * inv, e_s * inv, mx + jnp.log(total)


def _slope(h, n_heads):
    return float(2.0 ** (-8.0 * (h + 1) / n_heads))


def _swa_group(ref_vals, sink_ref, kh, n_heads):
    heads = [kh * ATT_G + g for g in range(ATT_G)]
    stacked = [jnp.concatenate([v[:, h * ATT_HD:(h + 1) * ATT_HD] for h in heads], axis=0) for v in ref_vals]
    grp = lax.shift_right_logical(lax.broadcasted_iota(jnp.int32, (ATT_G * WINDOW, 1), 0), WINDOW.bit_length() - 1)
    slope = jnp.zeros((ATT_G * WINDOW, 1), F32)
    sink = jnp.zeros((ATT_G * WINDOW, 1), F32)
    for g, h in enumerate(heads):
        slope = jnp.where(grp == g, _slope(h, n_heads), slope)
        sink = jnp.where(grp == g, sink_ref[:, h:h + 1], sink)
    return stacked, slope, sink


def _swa_fwd(q, kv, sinks):
    m, d = q.shape
    n_heads = d // ATT_HD
    kvh = n_heads // ATT_G
    kd = kvh * ATT_HD
    nb = m // WINDOW

    def body(q_ref, kvp_ref, kvc_ref, sink_ref, o_ref, lse_ref):
        has_prev = pl.program_id(0) > 0
        qv, kvp, kvc = q_ref[...], kvp_ref[...], kvc_ref[...]
        lane_h = lax.broadcasted_iota(jnp.int32, (WINDOW, n_heads), 1)
        outs, lse_all = [], jnp.zeros((WINDOW, n_heads), F32)
        for kh in range(kvh):
            ks = slice(kh * ATT_HD, (kh + 1) * ATT_HD)
            vs = slice(kd + kh * ATT_HD, kd + (kh + 1) * ATT_HD)
            (q4,), slope, sink = _swa_group([qv], sink_ref, kh, n_heads)
            p_p, p_c, _, lse = _swa_probs(q4, kvp[:, ks], kvc[:, ks], sink, slope, has_prev)
            o4 = _dot(p_p, kvp[:, vs]) + _dot(p_c, kvc[:, vs])
            for g in range(ATT_G):
                rows = slice(g * WINDOW, (g + 1) * WINDOW)
                outs.append(o4[rows, :])
                lse_all = jnp.where(lane_h == kh * ATT_G + g, lse[rows, :], lse_all)
        o_ref[...] = jnp.concatenate(outs, axis=-1).astype(o_ref.dtype)
        lse_ref[...] = lse_all

    return pl.pallas_call(
        body, name="swa_fwd", grid=(nb,),
        in_specs=[pl.BlockSpec((WINDOW, d), lambda n: (n, 0)),
                  pl.BlockSpec((WINDOW, 2 * kd), lambda n: (jnp.maximum(n - 1, 0), 0)),
                  pl.BlockSpec((WINDOW, 2 * kd), lambda n: (n, 0)),
                  pl.BlockSpec((1, n_heads), lambda n: (0, 0))],
        out_specs=[pl.BlockSpec((WINDOW, d), lambda n: (n, 0)), pl.BlockSpec((WINDOW, n_heads), lambda n: (n, 0))],
        out_shape=[jax.ShapeDtypeStruct((m, d), BF16), jax.ShapeDtypeStruct((m, n_heads), F32)],
        compiler_params=_params(("arbitrary",)),
    )(q, kv, kv, sinks)


def _swa_bwd(q, kv, sinks, lse, dao):
    m, d = q.shape
    n_heads = d // ATT_HD
    kvh = n_heads // ATT_G
    kd = kvh * ATT_HD
    nb = m // WINDOW
    scale = ATT_HD ** -0.5

    def body(q_ref, kvp_ref, kvc_ref, sink_ref, lse_ref, do_ref, dq_ref, dkvc_ref, dkvp_ref, dqsum_ref, dsink_ref):
        @pl.when(pl.program_id(0) == 0)
        def _():
            dqsum_ref[...] = jnp.zeros(dqsum_ref.shape, F32)
            dsink_ref[...] = jnp.zeros(dsink_ref.shape, F32)

        has_prev = pl.program_id(0) > 0
        qv, kvp, kvc, dov = q_ref[...], kvp_ref[...], kvc_ref[...], do_ref[...]
        lane_h = lax.broadcasted_iota(jnp.int32, (1, n_heads), 1)
        dsink = jnp.zeros((1, n_heads), F32)
        dq_parts, dk_p, dk_c, dv_p, dv_c = [], [], [], [], []
        for kh in range(kvh):
            ks = slice(kh * ATT_HD, (kh + 1) * ATT_HD)
            vs = slice(kd + kh * ATT_HD, kd + (kh + 1) * ATT_HD)
            kp, kc, vp, vc = kvp[:, ks], kvc[:, ks], kvp[:, vs], kvc[:, vs]
            (q4, do4), slope, sink = _swa_group([qv, dov], sink_ref, kh, n_heads)
            lse4 = jnp.concatenate([lse_ref[:, kh * ATT_G + g:kh * ATT_G + g + 1] for g in range(ATT_G)], axis=0)
            p_p, p_c, p_s, _ = _swa_probs(q4, kp, kc, sink, slope, has_prev, lse=lse4)
            dp_p, dp_c = _dot_nt(do4, vp), _dot_nt(do4, vc)
            delta = jnp.sum(p_p * dp_p, axis=-1, keepdims=True) + jnp.sum(p_c * dp_c, axis=-1, keepdims=True)
            ds_p, ds_c = p_p * (dp_p - delta), p_c * (dp_c - delta)
            sink_term = p_s * delta
            dq4 = (_dot(ds_p, kp) + _dot(ds_c, kc)) * scale
            for g in range(ATT_G):
                rows = slice(g * WINDOW, (g + 1) * WINDOW)
                dsink = dsink + jnp.where(lane_h == kh * ATT_G + g, -_col_sum(sink_term[rows, :]), 0.0)
                dq_parts.append(dq4[rows, :])
            dk_p.append(_dot_tn(ds_p, q4) * scale)
            dk_c.append(_dot_tn(ds_c, q4) * scale)
            dv_p.append(_dot_tn(p_p, do4))
            dv_c.append(_dot_tn(p_c, do4))
        dq = jnp.concatenate(dq_parts, axis=-1)
        dq_ref[...] = dq.astype(dq_ref.dtype)
        dqsum_ref[...] += _col_sum(dq)
        dsink_ref[...] += dsink
        dkvc_ref[...] = jnp.concatenate(dk_c + dv_c, axis=-1)
        dkvp_ref[...] = jnp.concatenate(dk_p + dv_p, axis=-1)

    return pl.pallas_call(
        body, name="swa_bwd", grid=(nb,),
        in_specs=[pl.BlockSpec((WINDOW, d), lambda n: (n, 0)),
                  pl.BlockSpec((WINDOW, 2 * kd), lambda n: (jnp.maximum(n - 1, 0), 0)),
                  pl.BlockSpec((WINDOW, 2 * kd), lambda n: (n, 0)),
                  pl.BlockSpec((1, n_heads), lambda n: (0, 0)),
                  pl.BlockSpec((WINDOW, n_heads), lambda n: (n, 0)),
                  pl.BlockSpec((WINDOW, d), lambda n: (n, 0))],
        out_specs=[pl.BlockSpec((WINDOW, d), lambda n: (n, 0)),
                   pl.BlockSpec((WINDOW, 2 * kd), lambda n: (n, 0)),
                   pl.BlockSpec((WINDOW, 2 * kd), lambda n: (n, 0)),
                   pl.BlockSpec((1, d), lambda n: (0, 0)),
                   pl.BlockSpec((1, n_heads), lambda n: (0, 0))],
        out_shape=[jax.ShapeDtypeStruct((m, d), BF16), jax.ShapeDtypeStruct((m, 2 * kd), F32),
                   jax.ShapeDtypeStruct((m, 2 * kd), F32), jax.ShapeDtypeStruct((1, d), F32),
                   jax.ShapeDtypeStruct((1, n_heads), F32)],
        compiler_params=_params(("arbitrary",)),
    )(q, kv, kv, sinks, lse, dao)


def _kv_grad_combine(dkv_cur, dkv_prev):
    m, w = dkv_cur.shape
    nb = m // WINDOW
    per = max(g for g in (1, 2, 4) if nb % g == 0)
    rows, steps = per * WINDOW, nb // per

    def body(cur_ref, same_ref, next_ref, o_ref, sum_ref):
        @pl.when(pl.program_id(0) == 0)
        def _():
            sum_ref[...] = jnp.zeros(sum_ref.shape, F32)

        after = jnp.where(pl.program_id(0) < steps - 1, next_ref[...], 0.0)
        total = cur_ref[...] + (jnp.concatenate([same_ref[WINDOW:, :], after], axis=0) if per > 1 else after)
        o_ref[...] = total.astype(o_ref.dtype)
        sum_ref[...] += _col_sum(total)

    return pl.pallas_call(
        body, name="kv_grad_combine", grid=(steps,),
        in_specs=[pl.BlockSpec((rows, w), lambda n: (n, 0)), pl.BlockSpec((rows, w), lambda n: (n, 0)),
                  pl.BlockSpec((WINDOW, w), lambda n: (jnp.minimum((n + 1) * per, nb - 1), 0))],
        out_specs=[pl.BlockSpec((rows, w), lambda n: (n, 0)), pl.BlockSpec((1, w), lambda n: (0, 0))],
        out_shape=[jax.ShapeDtypeStruct((m, w), BF16), jax.ShapeDtypeStruct((1, w), F32)],
        compiler_params=_params(("arbitrary",)),
    )(dkv_cur, dkv_prev, dkv_prev)


def _row(v):
    return v.reshape(1, -1)


def _local_step(x, p, target, wget, grad_sink, ln_gain, ln_bias, alb, norm_gain, kv_b, b_q, sinks, b_out, ple_b,
                small_sink=None):
    gs = {}
    gains = ln_gain.reshape(DEPTH * 3, -1)
    biases = ln_bias.reshape(DEPTH * 3, -1)
    sd = x.shape
    pending = [None]

    def mm(a, b, lb=0, **kw):
        after, pending[0] = pending[0], None
        return _mm(a, b, lb=lb, after=after, **kw)

    def mm_ln(a, wt, xin, i, j, nm, bias=None, pu=None):
        r = 3 * i + j
        if pu is None:
            fn, rows = (lambda h, xv, g, bv: (h,) + _ln_fwd_fn(xv, h, g[r:r + 1], bv[r:r + 1])), [xin]
        else:
            fn = lambda h, xv, puv, g, bv: (h,) + _ple_ln_fwd_fn(xv, h, puv, g[r:r + 1], bv[r:r + 1])
            rows = [xin, pu]
        h, y, yb = _mm(a, wt, lb=0, bias=bias, name=nm,
                       post=(fn, rows, [gains, biases], [(sd, F32), (sd, F32), (sd, BF16)], []))
        return h, (y, yb)

    def mm_ln_bwd(a, wt, add, xin, h, i, j, nm):
        r = 3 * i + j
        dx_part, dh, dg, db, dhsum = mm(a, wt, tb=True, add=add, name=nm,
                                        post=(lambda dy, xv, hv, g: _ln_bwd_fn(dy, xv, hv, g[r:r + 1]), [xin, h],
                                              [gains], [(sd, F32), (sd, BF16)], [((1, sd[1]), F32)] * 3))
        gs[f"ln_gain_{i}_{j}"], gs[f"ln_bias_{i}_{j}"] = dg, db
        return dx_part, dh, dhsum

    def tail_fwd(xa, i):
        wgu = wget("ffn_w_gate_up", i, xa[1])
        hid2 = wgu.shape[-1]
        gu, act = _mm(xa[1], wgu, lb=0, name=f"ffn_up_swiglu{i}", tile_cols=hid2 // 2,
                      post=(_swiglu_fwd_fn, [], [], [((sd[0], hid2), BF16), ((sd[0], hid2 // 2), BF16)], []))
        f, xb = mm_ln(act, wget("ffn_w_down", i, act), xa[0], i, 1, f"ffn_down_ln{i}")
        pu = _mm(p, wget("ple_w_up", i, act), la=i, lb=0, name=f"ple_up{i}")
        pg, xc = mm_ln(xb[1], wget("ple_w_gate", i, act), xb[0], i, 2, f"ple_gate_ln{i}", bias=_row(ple_b[i]), pu=pu)
        return dict(xa=xa, gu=gu, act=act, f=f, xb=xb, pg=pg, pu=pu), xc

    def tail_bwd(head, sv, i, mix_in, mix_h):
        xa, xb = sv["xa"], sv["xb"]
        r = 3 * i + 2
        dxb_part, dpg, dpu, dg2, db2, dbg = head(
            lambda dy, xv, pgv, puv, g: _ple_ln_bwd_fn(dy, xv, pgv, puv, g[r:r + 1]), [xb[0], sv["pg"], sv["pu"]],
            [gains], [(sd, F32), (sd, BF16), (sd, BF16)], [((1, sd[1]), F32)] * 3)[:6]
        gs[f"ple_b_{i}"] = dbg
        gs[f"ln_gain_{i}_2"], gs[f"ln_bias_{i}_2"] = dg2, db2
        grad_of("ple_w_gate", i, xb[1], dpg)
        grad_of("ple_w_up", i, p, dpu, la=i)
        dxa_part, df, _ = mm_ln_bwd(dpg, wget("ple_w_gate", i, None), dxb_part, xa[0], sv["f"], i, 1,
                                    f"ple_gate_dx_ln{i}")
        grad_of("ffn_w_down", i, sv["act"], df)
        gu = sv["gu"]
        dgu, = mm(df, wget("ffn_w_down", i, None), tb=True, name=f"ffn_down_dx_swiglu{i}", tile_cols=gu.shape[1] // 4,
                  post=(_swiglu_bwd_fn, [gu], [], [(gu.shape, BF16)], []))
        grad_of("ffn_w_gate_up", i, xa[1], dgu)
        return mm_ln_bwd(dgu, wget("ffn_w_gate_up", i, None), dxa_part, mix_in, mix_h, i, 0, f"ffn_up_dx_ln{i}")

    def grad_of(nm, i, act, dout, la=None, b_parts=None):
        grad = mm(act, dout, la=la, lb=None, ta=True, out_dtype=BF16, out_layers=1, out_layer=0,
                  name=f"grad_{nm}{i}", b_parts=b_parts)
        token = grad_sink(nm, i, grad)
        if token is not None:
            pending[0] = token

    proj = _mm(x, wget("a_w_in", 0, None), lb=0, name="hg_proj")
    o_pre, og, states, scores = _hgrn2_fwd(proj, alb, norm_gain, rb=HG_ROWS)
    h0, x1 = mm_ln(og, wget("a_w_out", 0, og), x, 0, 0, "hg_out_ln")
    sv0, x3 = tail_fwd(x1, 0)
    kv = _mm(x3[1], wget("kv_w", 0, x3[1]), lb=0, bias=_row(kv_b), out_dtype=BF16, name="kv_proj")
    q = _mm(x3[1], wget("b_w_q", 0, x3[1]), lb=0, bias=b_q, out_dtype=BF16, name="q_proj")
    ao, lse = _swa_fwd(q, kv, sinks)
    h1, x4 = mm_ln(ao, wget("b_w_out", 0, x3[1]), x3[0], 1, 0, "att_out_ln", bias=b_out)
    sv1, y = tail_fwd(x4, 1)

    loss_box = []

    def loss_head(fn, rows, whole, outs, sums):
        def with_loss(yv, tv, *rest):
            dy, part = _loss_fn(yv, tv)
            return fn(dy, *rest) + (part,)

        res = _rowwise(with_loss, [y[0], target] + rows, whole, outs, list(sums) + [((1, LANES), F32)],
                       name="loss_ln_ple_bwd1")
        loss_box.append(res[-1])
        return res

    dx3_part, dh1, dh1sum = tail_bwd(loss_head, sv1, 1, x3[0], h1)
    loss = loss_box[0]
    gs["b_out"] = dh1sum
    grad_of("b_w_out", 0, ao, dh1)
    dao = mm(dh1, wget("b_w_out", 0, None), tb=True, out_dtype=BF16, name="att_out_dx")
    dq, dkv_cur, dkv_prev, dqsum, dsinks = _swa_bwd(q, kv, sinks, lse, dao)
    gs["b_q"], gs["sinks"] = dqsum, dsinks
    dkv, dkvsum = _kv_grad_combine(dkv_cur, dkv_prev)
    gs["kv_b"] = dkvsum
    grad_of("b_w_q", 0, x3[1], dq)
    grad_of("kv_w", 0, x3[1], dkv)
    dx3 = mm(dq, wget("b_w_q", 0, None), tb=True, add=dx3_part, name="q_proj_dx")

    def kv_head(*post):
        return mm(dkv, wget("kv_w", 0, None), tb=True, add=dx3, name="kv_proj_dx_ln_ple_bwd0", post=post)

    dx_part, dh0, _ = tail_bwd(kv_head, sv0, 0, x, h0)
    grad_of("a_w_out", 0, og, dh0)
    dog = mm(dh0, wget("a_w_out", 0, None), tb=True, name="hg_out_dx")
    dproj, dalb, dgain = _hgrn2_bwd(proj, o_pre, states, scores, dog, alb, norm_gain, rb=HG_ROWS)
    gs["alb"], gs["norm_gain"] = dalb, dgain
    if small_sink is not None:
        pending[0] = small_sink(loss, gs)
    grad_of("a_w_in", 0, x, dproj, b_parts=4)
    grad_x = mm(dproj, wget("a_w_in", 0, None), tb=True, add=dx_part, name="hg_proj_dx", a_parts=4)
    return loss, grad_x, gs


HBM_SPEC = pl.BlockSpec(memory_space=pl.ANY)
HBM_ONLY = pl.BlockSpec(memory_space=pltpu.HBM)
SEM_SPEC = pl.BlockSpec(memory_space=pltpu.SEMAPHORE)
SIDE_EFFECT = pltpu.SideEffectType.DATAFLOW_SIDE_EFFECTING


def _slot(kind, j):
    return (j % 2) * 2 + j // 2 if kind == "colp" else j


def _piece(ref, kind, j):
    _, r, c = ref.shape
    if kind == "row":
        return ref.at[:, pl.ds(j * (r // N_CHIPS), r // N_CHIPS), :]
    return ref.at[:, :, pl.ds(_slot(kind, j) * (c // N_CHIPS), c // N_CHIPS)]


def _piece_dyn(ref, kind, j):
    _, r, c = ref.shape
    if kind == "row":
        return ref.at[:, pl.ds(pl.multiple_of(j * (r // N_CHIPS), 16), r // N_CHIPS), :]
    return ref.at[:, :, pl.ds(pl.multiple_of(_slot(kind, j) * (c // N_CHIPS), LANES), c // N_CHIPS)]


def _chip_of(j, c):
    return (j // 2, j % 2, c)


def _in_hbm(a):
    return pltpu.with_memory_space_constraint(a, pltpu.HBM)


PLACE_STEPS = 4


def _place(items, chip, *, name, after=None):
    n = len(items)
    in_specs, out_specs, out_shapes, blocks = [], [], [], []
    for src, layer, kind, out_dtype in items:
        _, r, c = src.shape
        nb = max(k for k in (1, 2, PLACE_STEPS) if r % (16 * k) == 0 or k == 1)
        blocks.append(nb)

        def src_idx(i, chip_ref, layer=layer, nb=nb):
            return (layer, jnp.minimum(i, nb - 1), 0)

        def full_idx(i, chip_ref, kind=kind, nb=nb):
            ib = jnp.minimum(i, nb - 1)
            return (0, chip_ref[0] * nb + ib, 0) if kind == "row" else (0, ib, _slot(kind, chip_ref[0]))

        in_specs.append(pl.BlockSpec((None, r // nb, c), src_idx))
        out_specs.append(pl.BlockSpec((None, r // nb, c), full_idx))
        out_shapes.append(jax.ShapeDtypeStruct((1, r * N_CHIPS, c) if kind == "row" else (1, r, c * N_CHIPS),
                                               out_dtype))
    operands = [it[0] for it in items]
    if after is not None:
        in_specs.append(HBM_SPEC)
        operands.append(after)

    def body(chip_ref, *refs):
        for a in range(n):
            refs[len(refs) - n + a][...] = refs[a][...].astype(refs[len(refs) - n + a].dtype)

    return pl.pallas_call(
        body, name=name,
        grid_spec=pltpu.PrefetchScalarGridSpec(num_scalar_prefetch=1, grid=(PLACE_STEPS,), in_specs=in_specs,
                                               out_specs=out_specs),
        out_shape=out_shapes,
        compiler_params=_params(("arbitrary",)),
    )(chip, *operands)


def _half(ref, c):
    h = ref.shape[1] // 2
    start = c * h if isinstance(c, int) else pl.multiple_of(c * h, 16)
    return ref.at[:, pl.ds(start, h), :]


def _sibling_handshake():
    barrier = pltpu.get_barrier_semaphore()
    sibling = (lax.axis_index("x"), lax.axis_index("y"), 1 - lax.axis_index("c"))
    pl.semaphore_signal(barrier, inc=1, device_id=sibling, device_id_type=MESH)
    pl.semaphore_wait(barrier, 1)


class _SiblingFill:
    def __init__(self, lands, kinds, name, collective_id):
        self.kinds, self.name, self.n = kinds, name, len(lands)
        n = self.n
        sem_shape = pltpu.SemaphoreType.DMA((n * N_CHIPS,))

        def body(*refs):
            land_refs, send_sems, recv_sems, token = refs[:n], refs[n], refs[n + 1], refs[-1]
            _sibling_handshake()
            for cp in self._copies(land_refs, send_sems, recv_sems):
                cp.start()
            token[...] = jnp.zeros(token.shape, token.dtype)

        outs = pl.pallas_call(
            body, name=name + "_start",
            in_specs=[HBM_ONLY] * n,
            out_specs=[SEM_SPEC, SEM_SPEC] + [HBM_ONLY] * n + [pl.BlockSpec(memory_space=pltpu.VMEM)],
            out_shape=[sem_shape, sem_shape] + [pltpu.HBM(a.shape, a.dtype) for a in lands]
                      + [jax.ShapeDtypeStruct((8, LANES), F32)],
            input_output_aliases={i: i + 2 for i in range(n)},
            compiler_params=pltpu.CompilerParams(has_side_effects=SIDE_EFFECT, collective_id=collective_id),
        )(*[_in_hbm(a) for a in lands])
        self.send_sems, self.recv_sems, self.lands, self.token = outs[0], outs[1], list(outs[2:2 + n]), outs[-1]

    def _copies(self, land_refs, send_sems, recv_sems):
        x, y, c = lax.axis_index("x"), lax.axis_index("y"), lax.axis_index("c")
        me = 2 * x + y
        copies = []
        for a in range(self.n):
            for k in range(1, N_CHIPS):
                t = (me + k) % N_CHIPS
                slice_t = _piece_dyn(land_refs[a], self.kinds[a], t)
                got = _half(slice_t, c)
                copies.append(pltpu.make_async_remote_copy(
                    src_ref=got, dst_ref=got, send_sem=send_sems.at[a * N_CHIPS + k],
                    recv_sem=recv_sems.at[a * N_CHIPS + k], device_id=(x, y, 1 - c), device_id_type=MESH))
        return copies

    def wait(self, after):
        n = self.n

        def body(*refs):
            land_refs, send_sems, recv_sems = refs[:n], refs[n], refs[n + 1]
            for cp in self._copies(land_refs, send_sems, recv_sems):
                cp.wait_send()
                cp.wait_recv()

        operands = [_in_hbm(a) for a in self.lands] + [self.send_sems, self.recv_sems]
        in_specs = [HBM_ONLY] * n + [SEM_SPEC, SEM_SPEC]
        if after is not None:
            operands.append(after)
            in_specs.append(HBM_SPEC)
        outs = pl.pallas_call(
            body, name=self.name + "_wait",
            in_specs=in_specs, out_specs=[HBM_ONLY] * n,
            out_shape=[pltpu.HBM(a.shape, a.dtype) for a in self.lands],
            input_output_aliases={i: i for i in range(n)},
            compiler_params=pltpu.CompilerParams(has_side_effects=SIDE_EFFECT),
        )(*operands)
        return list(outs)


class _Exchange:
    def __init__(self, mode, srcs, lands, kinds, layers, name, collective_id, after=None, halves=None):
        self.mode, self.kinds, self.layers, self.name, self.n = mode, kinds, layers, name, len(lands)
        self.halves = halves if halves is not None else [False] * len(lands)
        n, ns = self.n, len(srcs)
        n_in = ns + n + (after is not None)
        sem_shape = pltpu.SemaphoreType.DMA((n * N_CHIPS,))

        def body(*refs):
            src_refs, land_refs = refs[:ns], refs[ns:ns + n]
            send_sems, recv_sems = refs[n_in], refs[n_in + 1]
            token = refs[-1]
            c = lax.axis_index("c")
            me = 2 * lax.axis_index("x") + lax.axis_index("y")
            barrier = pltpu.get_barrier_semaphore()
            for k in range(1, N_CHIPS):
                t = (me + k) % N_CHIPS
                pl.semaphore_signal(barrier, inc=1, device_id=(t // 2, t % 2, c), device_id_type=MESH)
            pl.semaphore_wait(barrier, N_CHIPS - 1)
            for j in range(N_CHIPS):
                @pl.when(me == j)
                def _():
                    for a in range(n):
                        for t in range(N_CHIPS):
                            if t != j:
                                src, dst = self._ends(src_refs, land_refs, a, j, t, c)
                                pltpu.make_async_remote_copy(
                                    src_ref=src, dst_ref=dst, send_sem=send_sems.at[a * N_CHIPS + t],
                                    recv_sem=recv_sems.at[a * N_CHIPS + j],
                                    device_id=_chip_of(t, c), device_id_type=MESH).start()
            token[...] = jnp.zeros(token.shape, token.dtype)

        arrays = list(srcs) + list(lands)
        operands = [_in_hbm(a) for a in arrays]
        in_specs = [HBM_ONLY] * (ns + n)
        if after is not None:
            operands.append(after)
            in_specs.append(HBM_SPEC)
        outs = pl.pallas_call(
            body, name=name + "_start",
            in_specs=in_specs,
            out_specs=[SEM_SPEC, SEM_SPEC] + [HBM_ONLY] * (ns + n) + [pl.BlockSpec(memory_space=pltpu.VMEM)],
            out_shape=[sem_shape, sem_shape] + [pltpu.HBM(a.shape, a.dtype) for a in arrays]
                      + [jax.ShapeDtypeStruct((8, LANES), F32)],
            input_output_aliases={i: i + 2 for i in range(ns + n)},
            compiler_params=pltpu.CompilerParams(has_side_effects=SIDE_EFFECT, collective_id=collective_id),
        )(*operands)
        self.send_sems, self.recv_sems = outs[0], outs[1]
        self.srcs, self.lands = list(outs[2:2 + ns]), list(outs[2 + ns:2 + ns + n])
        self.token = outs[-1]

    def _ends(self, src_refs, land_refs, a, me_j, peer, c):
        if self.mode == "gather":
            mine = _piece(land_refs[a], self.kinds[a], me_j)
            if self.halves[a]:
                mine = _half(mine, c)
            return mine, mine
        return _piece(src_refs[a], self.kinds[a], peer), land_refs[a].at[me_j, pl.ds(self.layers[a], 1)]

    def wait(self, after, lands=None):
        n, ns = self.n, len(self.srcs)
        lands = self.lands if lands is None else lands

        def body(*refs):
            src_refs, land_refs = refs[:ns], refs[ns:ns + n]
            send_sems, recv_sems = refs[ns + n], refs[ns + n + 1]
            c = lax.axis_index("c")
            me = 2 * lax.axis_index("x") + lax.axis_index("y")
            for j in range(N_CHIPS):
                @pl.when(me != j)
                def _():
                    for a in range(n):
                        sent, _ = self._ends(src_refs, land_refs, a, 0, j, c)
                        _, landed = self._ends(src_refs, land_refs, a, j, 0, c)
                        cp = pltpu.make_async_remote_copy(
                            src_ref=sent, dst_ref=landed, send_sem=send_sems.at[a * N_CHIPS + j],
                            recv_sem=recv_sems.at[a * N_CHIPS + j],
                            device_id=_chip_of(j, c), device_id_type=MESH)
                        cp.wait_send()
                        cp.wait_recv()

        arrays = self.srcs + list(lands)
        operands = [_in_hbm(a) for a in arrays] + [self.send_sems, self.recv_sems]
        in_specs = [HBM_ONLY] * (ns + n) + [SEM_SPEC, SEM_SPEC]
        if after is not None:
            operands.append(after)
            in_specs.append(HBM_SPEC)
        outs = pl.pallas_call(
            body, name=self.name + "_wait",
            in_specs=in_specs, out_specs=[HBM_ONLY] * (ns + n),
            out_shape=[pltpu.HBM(a.shape, a.dtype) for a in arrays],
            input_output_aliases={i: i for i in range(ns + n)},
            compiler_params=pltpu.CompilerParams(has_side_effects=SIDE_EFFECT),
        )(*operands)
        return list(outs[:ns]), list(outs[ns:])


def _sum_arrivals(zone, own_grads, kind, chip, name, after=None):
    _, layers, r, c = zone.shape
    tm = _pick_rows(r, 256)
    if layers * (r // tm) == 1 and r % 32 == 0:
        tm = r // 2
    nb = r // tm

    def own_idx(u):
        def idx(l, i, chip_ref):
            ib = jnp.where(l == u, i, 0)
            return (0, chip_ref[0] * nb + ib, 0) if kind == "row" else (0, ib, _slot(kind, chip_ref[0]))
        return idx

    def slot_idx(k):
        return lambda l, i, chip_ref: ((chip_ref[0] + k) % N_CHIPS, l, i, 0)

    in_specs = [pl.BlockSpec((None, None, tm, c), slot_idx(k)) for k in range(1, N_CHIPS)]
    in_specs += [pl.BlockSpec((None, tm, c), own_idx(u)) for u in range(len(own_grads))]
    operands = [zone] * (N_CHIPS - 1) + list(own_grads)
    if after is not None:
        in_specs.append(HBM_SPEC)
        operands.append(after)

    def body(chip_ref, *refs):
        slot_refs, own_refs, o_ref = refs[:N_CHIPS - 1], refs[N_CHIPS - 1:N_CHIPS - 1 + layers], refs[-1]
        own = own_refs[0][...]
        for u in range(1, layers):
            own = jnp.where(pl.program_id(0) == u, own_refs[u][...], own)
        acc = own.astype(F32)
        for ref in slot_refs:
            acc = acc + ref[...].astype(F32)
        o_ref[...] = acc.astype(o_ref.dtype)

    return pl.pallas_call(
        body, name=name,
        grid_spec=pltpu.PrefetchScalarGridSpec(
            num_scalar_prefetch=1, grid=(layers, nb), in_specs=in_specs,
            out_specs=pl.BlockSpec((tm, c), lambda l, i, chip_ref: (l * nb + i, 0))),
        out_shape=jax.ShapeDtypeStruct((layers * r, c), BF16),
        compiler_params=_params(("arbitrary", "arbitrary")),
    )(chip, *operands)


class _SiblingSwap:
    def __init__(self, arrays, name, collective_id, after=None):
        self.name, self.n = name, len(arrays)
        n = self.n
        n_in = n + (after is not None)
        sem_shape = pltpu.SemaphoreType.DMA((n,))

        def body(*refs):
            ins, send_sems, recv_sems = refs[:n], refs[n_in], refs[n_in + 1]
            theirs, token = refs[n_in + 2 + n:n_in + 2 + 2 * n], refs[-1]
            _sibling_handshake()
            for cp in self._copies(ins, theirs, send_sems, recv_sems):
                cp.start()
            token[...] = jnp.zeros(token.shape, token.dtype)

        operands, in_specs = [_in_hbm(a) for a in arrays], [HBM_ONLY] * n
        if after is not None:
            operands.append(after)
            in_specs.append(HBM_SPEC)
        outs = pl.pallas_call(
            body, name=name + "_start",
            in_specs=in_specs,
            out_specs=[SEM_SPEC, SEM_SPEC] + [HBM_ONLY] * (2 * n) + [pl.BlockSpec(memory_space=pltpu.VMEM)],
            out_shape=[sem_shape, sem_shape] + [pltpu.HBM(a.shape, a.dtype) for a in arrays] * 2
                      + [jax.ShapeDtypeStruct((8, LANES), F32)],
            input_output_aliases={i: i + 2 for i in range(n)},
            compiler_params=pltpu.CompilerParams(has_side_effects=SIDE_EFFECT, collective_id=collective_id),
        )(*operands)
        self.send_sems, self.recv_sems = outs[0], outs[1]
        self.mine, self.theirs, self.token = list(outs[2:2 + n]), list(outs[2 + n:2 + 2 * n]), outs[-1]

    def _copies(self, mine, theirs, send_sems, recv_sems):
        sibling = (lax.axis_index("x"), lax.axis_index("y"), 1 - lax.axis_index("c"))
        return [pltpu.make_async_remote_copy(src_ref=mine[a], dst_ref=theirs[a], send_sem=send_sems.at[a],
                                             recv_sem=recv_sems.at[a], device_id=sibling, device_id_type=MESH)
                for a in range(self.n)]

    def wait(self, after):
        n = self.n

        def body(*refs):
            for cp in self._copies(refs[:n], refs[n:2 * n], refs[2 * n], refs[2 * n + 1]):
                cp.wait_send()
                cp.wait_recv()

        arrays = self.mine + self.theirs
        outs = pl.pallas_call(
            body, name=self.name + "_wait",
            in_specs=[HBM_ONLY] * (2 * n) + [SEM_SPEC, SEM_SPEC, HBM_SPEC], out_specs=[HBM_ONLY] * (2 * n),
            out_shape=[pltpu.HBM(a.shape, a.dtype) for a in arrays],
            input_output_aliases={i: i for i in range(2 * n)},
            compiler_params=pltpu.CompilerParams(has_side_effects=SIDE_EFFECT),
        )(*[_in_hbm(a) for a in arrays], self.send_sems, self.recv_sems, after)
        return list(outs[:n]), list(outs[n:])


class _GatherDevices:
    def __init__(self, vec):
        sem_shape = pltpu.SemaphoreType.DMA((N_DEV,))

        def body(in_ref, send_sems, recv_sems, vec_ref, out_ref, token):
            for cp in self._copies(in_ref, out_ref, send_sems, recv_sems):
                cp.start()
            token[...] = jnp.zeros(token.shape, token.dtype)

        outs = pl.pallas_call(
            body, name="gather_small_start",
            in_specs=[HBM_ONLY],
            out_specs=[SEM_SPEC, SEM_SPEC, HBM_ONLY, HBM_ONLY, pl.BlockSpec(memory_space=pltpu.VMEM)],
            out_shape=[sem_shape, sem_shape, pltpu.HBM(vec.shape, vec.dtype),
                       pltpu.HBM((N_DEV,) + vec.shape, vec.dtype), jax.ShapeDtypeStruct((8, LANES), F32)],
            input_output_aliases={0: 2},
            compiler_params=pltpu.CompilerParams(has_side_effects=SIDE_EFFECT),
        )(_in_hbm(vec))
        self.send_sems, self.recv_sems, self.vec, self.rows, self.token = outs

    def _copies(self, in_ref, out_ref, send_sems, recv_sems):
        x, y, c = lax.axis_index("x"), lax.axis_index("y"), lax.axis_index("c")
        me = 4 * x + 2 * y + c
        copies = [pltpu.make_async_copy(in_ref, out_ref.at[me], recv_sems.at[0])]
        for rel in range(1, N_DEV):
            peer = (x ^ (rel >> 2), y ^ ((rel >> 1) & 1), c ^ (rel & 1))
            copies.append(pltpu.make_async_remote_copy(
                src_ref=in_ref, dst_ref=out_ref.at[me], send_sem=send_sems.at[rel], recv_sem=recv_sems.at[rel],
                device_id=peer, device_id_type=MESH))
        return copies

    def wait(self, after):
        def body(vec_ref, rows_ref, send_sems, recv_sems, after_ref, vec_out, rows_out):
            copies = self._copies(vec_ref, rows_ref, send_sems, recv_sems)
            copies[0].wait()
            for cp in copies[1:]:
                cp.wait_send()
                cp.wait_recv()

        outs = pl.pallas_call(
            body, name="gather_small_wait",
            in_specs=[HBM_ONLY, HBM_ONLY, SEM_SPEC, SEM_SPEC, HBM_SPEC], out_specs=[HBM_ONLY, HBM_ONLY],
            out_shape=[pltpu.HBM(self.vec.shape, self.vec.dtype), pltpu.HBM(self.rows.shape, self.rows.dtype)],
            input_output_aliases={0: 0, 1: 1},
            compiler_params=pltpu.CompilerParams(has_side_effects=SIDE_EFFECT),
        )(_in_hbm(self.vec), _in_hbm(self.rows), self.send_sems, self.recv_sems, after)
        return outs[1]


BIG = [("a_w_in", "col"), ("a_w_out", "row"), ("kv_w", "row"), ("b_w_q", "row"), ("b_w_out", "row"),
       ("ffn_w_gate_up", "colp"), ("ffn_w_down", "row"), ("ple_w_up", "col"), ("ple_w_gate", "row")]
GATHER_GROUPS = [[("a_w_in", 0), ("small", 0)], [("a_w_out", 0)],
                 [("ffn_w_gate_up", 0), ("ffn_w_down", 0), ("ple_w_gate", 0), ("ple_w_up", 0)],
                 [("kv_w", 0), ("b_w_q", 0), ("b_w_out", 0)],
                 [("ffn_w_gate_up", 1)], [("ffn_w_down", 1), ("ple_w_gate", 1), ("ple_w_up", 1)]]
SCATTER_GROUPS = [[("ple_w_gate", 1), ("ple_w_up", 1), ("ffn_w_down", 1)], [("ffn_w_gate_up", 1)],
                  [("b_w_out", 0), ("b_w_q", 0), ("kv_w", 0)], [("ple_w_gate", 0), ("ple_w_up", 0), ("ffn_w_down", 0)],
                  [("ffn_w_gate_up", 0), ("a_w_out", 0)], [("a_w_in", 0)]]
COLLECTIVE_IDS = {"fill": 0, "swap": 6, "gather": 9, "scatter": 15}
SMALL_SHARDED = ["ln_gain", "ln_bias", "a_lower_bound"]
SMALL_REPLICATED = ["a_norm_gain", "kv_b", "b_b_q", "b_sinks", "b_b_out", "ple_b_gate"]
WEIGHT_ORDER = ["a_w_in", "a_lower_bound", "a_norm_gain", "a_w_out", "kv_w", "kv_b", "b_w_q", "b_b_q", "b_sinks",
                "b_w_out", "b_b_out", "ffn_w_gate_up", "ffn_w_down", "ple_w_up", "ple_w_gate", "ple_b_gate",
                "ln_gain", "ln_bias"]


def _as3(a):
    return a.reshape((-1,) + a.shape[-2:]) if a.ndim >= 3 else a.reshape((1,) + a.shape)


def _pad_lanes(v):
    n = v.shape[-1]
    return jnp.pad(v, ((0, 0), (0, (-n) % LANES)))


ADAM_MANY_STEPS = 4
ADAM_MANY_MAX = 1 << 19


def _adam_many(groups, name):
    in_specs, out_specs, out_shapes, arrays = [], [], [], []
    for group in groups:
        r, c = group[0].shape
        block = pl.BlockSpec((r // ADAM_MANY_STEPS, c), lambda i: (i, 0))
        in_specs += [block] * len(group)
        arrays += list(group)
        out_specs += [block] * 4
        out_shapes += [jax.ShapeDtypeStruct((r, c), F32)] * 4

    def body(*refs):
        ins, outs = refs[:len(arrays)], refs[len(arrays):]
        for k in range(len(groups)):
            res = _adam_fn(*[ref[...] for ref in ins[5 * k:5 * k + 5]])
            for out_ref, val in zip(outs[4 * k:4 * k + 4], res):
                out_ref[...] = val

    result = pl.pallas_call(
        body, name=name, grid=(ADAM_MANY_STEPS,), in_specs=in_specs, out_specs=out_specs, out_shape=out_shapes,
        compiler_params=_params(("parallel",)),
    )(*arrays)
    return [result[4 * k:4 * k + 4] for k in range(len(groups))]


def _adam_small(everyone, chip, items, loss_off):
    n_items = len(items)

    def body(chip_ref, every_ref, *refs):
        ins, outs = refs[:3 * n_items], refs[3 * n_items:]

        def total(off, width):
            acc = every_ref[0, :, off:off + width]
            for s in range(1, N_DEV):
                acc = acc + every_ref[s, :, off:off + width]
            return acc

        for a, (w, _, _, off, sharded) in enumerate(items):
            cols = w.shape[-1]
            for r in range(w.size // cols):
                at = (slice(r, r + 1),) if w.ndim == 2 else (r // w.shape[1], slice(r % w.shape[1], r % w.shape[1] + 1))
                if sharded:
                    full = total(off + r * N_CHIPS * cols, N_CHIPS * cols)
                    g = full[:, 0:cols]
                    for c in range(1, N_CHIPS):
                        g = jnp.where(chip_ref[0] == c, full[:, c * cols:(c + 1) * cols], g)
                else:
                    g = total(off + r * cols, cols)
                w_ref, m_ref, v_ref = ins[3 * a:3 * a + 3]
                res = _adam_fn(w_ref[at], m_ref[at], v_ref[at], g, jnp.zeros_like(g))
                for out_ref, val in zip(outs[4 * a:4 * a + 4], res):
                    out_ref[at] = val
        outs[-1][...] = total(loss_off, LANES)

    def whole(shape):
        return pl.BlockSpec(tuple(shape), lambda i, chip_ref: (0,) * len(shape))

    arrays = [arr for it in items for arr in it[:3]]
    out_shapes = [jax.ShapeDtypeStruct(it[0].shape, F32) for it in items for _ in range(4)]
    out_shapes.append(jax.ShapeDtypeStruct((1, LANES), F32))
    result = pl.pallas_call(
        body, name="adam_small",
        grid_spec=pltpu.PrefetchScalarGridSpec(
            num_scalar_prefetch=1, grid=(1,),
            in_specs=[whole(everyone.shape)] + [whole(arr.shape) for arr in arrays],
            out_specs=[whole(s.shape) for s in out_shapes]),
        out_shape=out_shapes,
        compiler_params=_params(("arbitrary",)),
    )(chip, everyone, *arrays)
    return [result[4 * a:4 * a + 4] for a in range(n_items)], result[-1]


def kernel(x, p, a_w_in, a_lower_bound, a_norm_gain, a_w_out, kv_w, kv_b, b_w_q, b_b_q, b_sinks, b_w_out, b_b_out, ffn_w_gate_up, ffn_w_down, ple_w_up, ple_w_gate, ple_b_gate, ln_gain, ln_bias, loss_target, m_a_w_in, m_a_lower_bound, m_a_norm_gain, m_a_w_out, m_kv_w, m_kv_b, m_b_w_q, m_b_b_q, m_b_sinks, m_b_w_out, m_b_b_out, m_ffn_w_gate_up, m_ffn_w_down, m_ple_w_up, m_ple_w_gate, m_ple_b_gate, m_ln_gain, m_ln_bias, v_a_w_in, v_a_lower_bound, v_a_norm_gain, v_a_w_out, v_kv_w, v_kv_b, v_b_w_q, v_b_b_q, v_b_sinks, v_b_w_out, v_b_b_out, v_ffn_w_gate_up, v_ffn_w_down, v_ple_w_up, v_ple_w_gate, v_ple_b_gate, v_ln_gain, v_ln_bias):
    args = dict(locals())
    wts = {n: args[n] for n in WEIGHT_ORDER}
    mom = {n: args["m_" + n] for n in WEIGHT_ORDER}
    vel = {n: args["v_" + n] for n in WEIGHT_ORDER}
    chip = 2 * lax.axis_index("x") + lax.axis_index("y")
    d = x.shape[-1]
    dq = d // N_CHIPS

    kind_of = dict(BIG)
    kind_of["small"] = "col"
    chip_arr = chip.reshape(1).astype(jnp.int32)
    small_pack = jnp.concatenate([wts[n].reshape(-1, dq) for n in SMALL_SHARDED], axis=0)[None]

    def place_item(key):
        n, layer = key
        if n == "small":
            return small_pack, 0, "col", F32
        return _as3(wts[n]), layer, kind_of[n], BF16

    gathers, fills, ready = [], {}, {}
    where = {k: gi for gi, group in enumerate(GATHER_GROUPS) for k in group}

    def start_gather(gi, placed, after):
        group = GATHER_GROUPS[gi]
        gathers.append(_Exchange("gather", [], placed, [kind_of[k[0]] for k in group], [0] * len(group),
                                 f"gather{gi}", COLLECTIVE_IDS["gather"] + gi, after=after,
                                 halves=[k[0] != "small" for k in group]))

    def pass_on(gi, after):
        if gi not in fills:
            group = GATHER_GROUPS[gi]
            outs = gathers[gi].wait(after)[1]
            split = [i for i, k in enumerate(group) if k[0] != "small"]
            fills[gi] = (outs, split, _SiblingFill([outs[i] for i in split], [kind_of[group[i][0]] for i in split],
                                                   f"fill{gi}", COLLECTIVE_IDS["fill"] + gi))

    placed, prev = [], None
    for gi, group in enumerate(GATHER_GROUPS):
        placed.append(_place([place_item(k) for k in group], chip_arr, name=f"place{gi}", after=prev))
        if gi == 0:
            start_gather(0, placed[0], None)
        prev = gathers[0].token if gi == 0 else placed[gi][0]
    pass_on(0, prev)
    prev = fills[0][2].token
    for gi in range(1, len(GATHER_GROUPS)):
        start_gather(gi, placed[gi], prev)
        prev = gathers[gi].token
    all_started = prev

    def wget(name, layer, after):
        key = (name, layer)
        if key not in ready:
            gi = where[key]
            after = all_started if gi == 0 else after
            pass_on(gi, after)
            if 1 <= gi < len(GATHER_GROUPS) - 1:
                pass_on(gi + 1, after)
                after = fills[gi + 1][2].token
            outs, split, fill = fills[gi]
            for i, arr in zip(split, fill.wait(after)):
                outs[i] = arr
            for k, arr in zip(GATHER_GROUPS[gi], outs):
                ready[k] = arr
        return ready[key]

    small_full = wget("small", 0, None)[0]
    ln_gain_f = small_full[0:6].reshape(DEPTH, 3, d)
    ln_bias_f = small_full[6:12].reshape(DEPTH, 3, d)
    alb_f = small_full[12:14]

    group_of = {k: gi for gi, group in enumerate(SCATTER_GROUPS) for k in group}
    grads_done, zones, scatters = {}, {}, []

    def grad_sink(name, layer, grad):
        grads_done[(name, layer)] = grad
        if name not in zones:
            zones[name] = lax.empty((N_CHIPS,) + _as3(wts[name]).shape, BF16)
        gi = group_of[(name, layer)]
        group = SCATTER_GROUPS[gi]
        if not all(k in grads_done for k in group):
            return None
        ex = _Exchange("scatter", [grads_done[k] for k in group], [zones[k[0]] for k in group],
                       [kind_of[k[0]] for k in group], [k[1] for k in group], f"scatter{gi}",
                       COLLECTIVE_IDS["scatter"] + gi)
        for k, zone in zip(group, ex.lands):
            zones[k[0]] = zone
        scatters.append((ex, group))
        return ex.token

    small = {}

    def small_sink(loss, gs):
        ln_g = jnp.concatenate([gs[f"ln_gain_{i}_{j}"] for i in range(DEPTH) for j in range(3)], axis=0)
        ln_b = jnp.concatenate([gs[f"ln_bias_{i}_{j}"] for i in range(DEPTH) for j in range(3)], axis=0)
        ple_bg = jnp.concatenate([gs[f"ple_b_{i}"] for i in range(DEPTH)], axis=0)
        small["list"] = [ln_g.reshape(1, -1), ln_b.reshape(1, -1), gs["alb"].reshape(1, -1), gs["norm_gain"],
                         gs["kv_b"], gs["b_q"], _pad_lanes(gs["sinks"]), gs["b_out"], ple_bg.reshape(1, -1), loss]
        small["gather"] = _GatherDevices(jnp.concatenate(small["list"], axis=1))
        return small["gather"].token

    loss, grad_x, gs = _local_step(
        x[0], p.reshape((p.shape[0],) + p.shape[2:]), loss_target[0], wget, grad_sink, ln_gain_f, ln_bias_f, alb_f, a_norm_gain, kv_b, b_b_q,
        b_sinks, b_b_out, ple_b_gate, small_sink)

    res = {}

    def arrive(batch, after):
        for ex, group in batch:
            srcs, outs = ex.wait(after, lands=[zones[k[0]] for k in group])
            for k, grad, zone in zip(group, srcs, outs):
                grads_done[k], zones[k[0]] = grad, zone

    def half_sums(names, batch, after):
        partial = []
        for n in names:
            own = [grads_done[(n, layer)] for layer in range(zones[n].shape[1])]
            partial.append(_sum_arrivals(zones[n], own, kind_of[n], chip_arr, f"sum_{n}", after=after))
        return _SiblingSwap(partial, f"swap{batch}", COLLECTIVE_IDS["swap"] + batch, after=after)

    def update(names, swap, after):
        flat = lambda a: a.reshape(-1, a.shape[-1])
        work = [(n, [flat(wts[n]), flat(mom[n]), flat(vel[n]), own, sib]) for n, own, sib in zip(names, *swap.wait(after))]
        many = [(n, ops) for n, ops in work if wts[n].size <= ADAM_MANY_MAX]
        if len(many) > 1:
            for (n, _), out in zip(many, _adam_many([ops for _, ops in many], f"adam_from_{many[0][0]}")):
                res[n] = [o.reshape(wts[n].shape) for o in out]
        for n, ops in work:
            if n not in res:
                out = _rowwise(_adam_fn, ops, [], [(ops[3].shape, F32)] * 4, name=f"adam_{n}")
                res[n] = [o.reshape(wts[n].shape) for o in out]
        return res[names[-1]][1]

    last_names = [k[0] for k in SCATTER_GROUPS[-1]]
    batches = [["ffn_w_gate_up"], [n for n, _ in BIG if n != "ffn_w_gate_up" and n not in last_names], last_names]
    arrive(scatters[:-1], grad_x)
    swap0 = half_sums(batches[0], 0, None)
    swap1 = half_sums(batches[1], 1, swap0.token)
    updated = update(batches[0], swap0, swap1.token)
    arrive(scatters[-1:], updated)
    swap2 = half_sums(batches[2], 2, swap1.token)
    updated = update(batches[1], swap1, swap2.token)
    update(batches[2], swap2, updated)

    everyone = small["gather"].wait(grad_x)
    offs, pos = [], 0
    for v in small["list"]:
        offs.append(pos)
        pos += v.shape[1]
    names = ["ln_gain", "ln_bias", "a_lower_bound", "a_norm_gain", "kv_b", "b_b_q", "b_sinks", "b_b_out", "ple_b_gate"]
    as_rows = lambda a: a.reshape(1, -1) if a.ndim == 1 else a
    items = [(as_rows(wts[n]), as_rows(mom[n]), as_rows(vel[n]), off, n in SMALL_SHARDED)
             for n, off in zip(names, offs)]
    updates, loss_row = _adam_small(everyone, chip_arr, items, offs[len(names)])
    for n, upd in zip(names, updates):
        res[n] = [u.reshape(wts[n].shape) for u in upd]

    outs = [loss_row[0, 0], grad_x[None]]
    for k in range(4):
        outs += [res[n][k] for n in WEIGHT_ORDER]
    return tuple(outs)
```

```python
import functools

import jax
import jax.numpy as jnp
from jax import lax
from jax.experimental import pallas as pl
from jax.experimental.pallas import tpu as pltpu

F32 = jnp.float32
BF16 = jnp.bfloat16
MESH = pl.DeviceIdType.MESH

LANES = 128
HG_DK = 128
HG_CHUNK = 64
HG_SUB = 16
HG_ROWS = 512
HG_HEADS_PER_STEP = 2
LOG2_E = 1.4426950408889634
ATT_HD = 64
ATT_G = 4
WINDOW = 128
DEPTH = 2
ALPHA = (2.0 * DEPTH) ** 0.25
LN_EPS = 1e-5
RMS_EPS = 1e-6
ADAM_LR, ADAM_B1, ADAM_B2, ADAM_EPS, ADAM_WD, ADAM_STEP = 0.001, 0.9, 0.999, 1e-08, 0.01, 10
N_CHIPS = 4
N_DEV = 8
VMEM_LIMIT = 56 * 1024 * 1024
NEG = -1e30


def _pick(n, cap):
    best = None
    for d in range(LANES, min(n, cap) + 1, LANES):
        if n % d == 0:
            best = d
    return n if best is None else best


def _pick_rows(m, cap):
    best = None
    for d in range(16, min(m, cap) + 1, 16):
        if m % d == 0:
            best = d
    return m if best is None else best


def _params(sem):
    return pltpu.CompilerParams(dimension_semantics=sem, vmem_limit_bytes=VMEM_LIMIT)


def _zeros_index(ndim, grid_rank=3):
    return (lambda i, j, kk: (0,) * ndim) if grid_rank == 3 else (lambda kk, i: (0,) * ndim)


def _mm(a, b, *, name, la=None, lb=None, ta=False, tb=False, bias=None, add=None, out_dtype=F32,
        out_layers=None, out_layer=None, after=None, post=None, tile_cols=None, caps=(1024, 1536, 2048),
        a_parts=None, b_parts=None):
    ar, ac = a.shape[-2:]
    br, bc = b.shape[-2:]
    assert a_parts is None or (not ta and la is None and a.shape[0] == a_parts)
    assert b_parts is None or (not tb and lb is None and b.shape[0] == b_parts)
    m, k = (ac, ar) if ta else (ar, ac * (a_parts or 1))
    k2, n = (bc, br) if tb else (br, bc * (b_parts or 1))
    assert k == k2, (a.shape, b.shape, ta, tb)
    if post is not None:
        caps = (512, n if tile_cols is None else tile_cols, caps[2])
    tm, tn, tk = _pick(m, caps[0]), _pick(bc if b_parts else n, caps[1]), _pick(ac if a_parts else k, caps[2])
    assert post is None or tn == caps[1]
    nk = k // tk
    gi, gj = m // tm, n // tn
    a_bytes, b_bytes = m * k * a.dtype.itemsize, k * n * b.dtype.itemsize
    rows_outer = (a_bytes + b_bytes * (gi if gj * nk > 1 else 1)) <= (b_bytes + a_bytes * (gj if gi * nk > 1 else 1))
    k_outer = post is not None and nk > 1 and gj == 1
    grid = (nk, gi) if k_outer else (gi, gj, nk) if rows_outer else (gj, gi, nk)
    keep_at = ta and nk == 1 and gj > 1 and rows_outer

    def bs(block, idx, late=False):
        if k_outer:
            return pl.BlockSpec(block, lambda kk, i: idx(jnp.where(kk == nk - 1, i, 0) if late else i, 0, kk))
        return pl.BlockSpec(block, idx if rows_outer else (lambda q, p, kk: idx(p, q, kk)))

    def spec(block, idx, layer):
        if layer is None:
            return bs(block, idx)
        return bs((None,) + block, lambda i, j, kk: (layer,) + idx(i, j, kk))

    a_spec = spec((tk, tm), lambda i, j, kk: (kk, i), la) if ta else spec((tm, tk), lambda i, j, kk: (i, kk), la)
    b_spec = spec((tn, tk), lambda i, j, kk: (j, kk), lb) if tb else spec((tk, tn), lambda i, j, kk: (kk, j), lb)
    if a_parts:
        a_spec = bs((None, tm, tk), lambda i, j, kk: (kk // (ac // tk), i, kk % (ac // tk)))
    if b_parts:
        b_spec = bs((None, tk, tn), lambda i, j, kk: (j // (bc // tn), kk, j % (bc // tn)))
    in_specs, operands = [a_spec, b_spec], [a, b]
    if bias is not None:
        in_specs.append(bs((1, tn), lambda i, j, kk: (0, j)))
        operands.append(bias)
    if add is not None:
        in_specs.append(bs((tm, tn), lambda i, j, kk: (i, j), late=True))
        operands.append(add)
    if after is not None:
        in_specs.append(pl.BlockSpec(memory_space=pl.ANY))
        operands.append(after)
    dims = (((0 if ta else 1,), (1 if tb else 0,)), ((), ()))
    has_bias, has_add = bias is not None, add is not None
    if post is None:
        fn, rows, whole, outs, sums = None, [], [], [], []
        out_shape = jax.ShapeDtypeStruct((m, n) if out_layers is None else (out_layers, m, n), out_dtype)
        out_specs = spec((tm, tn), lambda i, j, kk: (i, j), out_layer)
    else:
        fn, rows, whole, outs, sums = post
        in_specs += [bs((tm, r.shape[-1] // gj), lambda i, j, kk: (i, j), late=True) for r in rows]
        in_specs += [pl.BlockSpec(tuple(w.shape), _zeros_index(w.ndim, len(grid))) for w in whole]
        operands += list(rows) + list(whole)
        out_shape = [jax.ShapeDtypeStruct(sh, dt) for sh, dt in list(outs) + list(sums)]
        out_specs = ([bs((tm, sh[-1] // gj), lambda i, j, kk: (i, j), late=True) for sh, _ in outs]
                     + [pl.BlockSpec(tuple(sh), _zeros_index(len(sh), len(grid))) for sh, _ in sums])
    n_in, n_extra, n_outs, n_sums = len(operands), len(rows) + len(whole), len(outs), len(sums)

    def body(*refs):
        a_ref, b_ref = refs[0], refs[1]
        pos = 2
        bias_ref = add_ref = None
        if has_bias:
            bias_ref = refs[pos]
            pos += 1
        if has_add:
            add_ref = refs[pos]
            pos += 1
        extra_refs = refs[n_in - n_extra:n_in]
        out_refs = refs[n_in:n_in + max(n_outs, 1)]
        sum_refs = refs[n_in + n_outs:n_in + n_outs + n_sums]
        acc_ref = refs[-1] if nk > 1 else None
        if keep_at:
            at_ref = refs[-1]

            @pl.when(pl.program_id(1) == 0)
            def _():
                at_ref[...] = a_ref[...].astype(BF16).T

            part = lax.dot_general(at_ref[...], b_ref[...].astype(BF16), (((1,), (1 if tb else 0,)), ((), ())),
                                   preferred_element_type=F32)
        else:
            part = lax.dot_general(a_ref[...].astype(BF16), b_ref[...].astype(BF16), dims,
                                   preferred_element_type=F32)

        def finish(total):
            if has_bias:
                total = total + bias_ref[...]
            if has_add:
                total = total + add_ref[...]
            if fn is None:
                out_refs[0][...] = total.astype(out_refs[0].dtype)
                return
            res = fn(total, *[r[...] for r in extra_refs])
            for ref, val in zip(out_refs, res[:n_outs]):
                ref[...] = val.astype(ref.dtype)
            if n_sums:
                @pl.when(pl.program_id(1 if k_outer or not rows_outer else 0) == 0)
                def _():
                    for ref in sum_refs:
                        ref[...] = jnp.zeros(ref.shape, ref.dtype)

                for ref, val in zip(sum_refs, res[n_outs:]):
                    ref[...] += val

        if nk == 1:
            finish(part)
        elif k_outer:
            kk = pl.program_id(0)
            rows_i = pl.ds(pl.multiple_of(pl.program_id(1) * tm, tm), tm)

            @pl.when(kk == 0)
            def _():
                acc_ref[rows_i, :] = part

            @pl.when(kk > 0)
            def _():
                acc_ref[rows_i, :] += part

            @pl.when(kk == nk - 1)
            def _():
                finish(acc_ref[rows_i, :])
        else:
            kk = pl.program_id(2)

            @pl.when(kk == 0)
            def _():
                acc_ref[...] = part

            @pl.when(kk > 0)
            def _():
                acc_ref[...] += part

            @pl.when(kk == nk - 1)
            def _():
                finish(acc_ref[...])

    return pl.pallas_call(
        body, name=name, grid=grid, in_specs=in_specs, out_specs=out_specs, out_shape=out_shape,
        scratch_shapes=([pltpu.VMEM((m, n) if k_outer else (tm, tn), F32)] if nk > 1
                        else [pltpu.VMEM((tm, tk), BF16)] if keep_at else []),
        compiler_params=_params(("arbitrary", "arbitrary") if k_outer
                                else ("arbitrary" if n_sums else "parallel", "arbitrary" if keep_at else "parallel",
                                      "arbitrary") if rows_outer
                                else ("parallel", "arbitrary" if n_sums else "parallel", "arbitrary")),
    )(*operands)


def _rowwise(fn, rows, whole, outs, sums=(), *, name, tm=256):
    m = rows[0].shape[-2]
    tm = _pick_rows(m, tm)
    n_rows, n_whole, n_outs, n_sums = len(rows), len(whole), len(outs), len(sums)

    def rspec(shape):
        lead = len(shape) - 2
        return pl.BlockSpec(tuple(shape[:-2]) + (tm, shape[-1]), lambda i: (0,) * lead + (i, 0))

    def wspec(shape):
        return pl.BlockSpec(tuple(shape), lambda i: (0,) * len(shape))

    def body(*refs):
        vals = [r[...] for r in refs[:n_rows + n_whole]]
        out_refs = refs[n_rows + n_whole:n_rows + n_whole + n_outs]
        sum_refs = refs[n_rows + n_whole + n_outs:]
        res = fn(*vals)
        for ref, val in zip(out_refs, res[:n_outs]):
            ref[...] = val.astype(ref.dtype)
        if n_sums:
            @pl.when(pl.program_id(0) == 0)
            def _():
                for ref in sum_refs:
                    ref[...] = jnp.zeros(ref.shape, ref.dtype)

            for ref, val in zip(sum_refs, res[n_outs:]):
                ref[...] += val

    result = pl.pallas_call(
        body, name=name, grid=(m // tm,),
        in_specs=[rspec(r.shape) for r in rows] + [wspec(w.shape) for w in whole],
        out_specs=[rspec(s) for s, _ in outs] + [wspec(s) for s, _ in sums],
        out_shape=[jax.ShapeDtypeStruct(s, d) for s, d in list(outs) + list(sums)],
        compiler_params=_params(("arbitrary",)),
    )(*rows, *whole)
    return result


def _sigmoid(v):
    return jax.nn.sigmoid(v)


def _col_sum(v):
    return jnp.sum(v, axis=0, keepdims=True)


def _ln_stats(z):
    mu = jnp.mean(z, axis=-1, keepdims=True)
    zc = z - mu
    var = jnp.mean(zc * zc, axis=-1, keepdims=True)
    rstd = lax.rsqrt(var + LN_EPS)
    return zc * rstd, rstd


def _ln_fwd_fn(xin, h, gain, bias):
    xhat, _ = _ln_stats(ALPHA * xin + h)
    y = xhat * gain + bias
    return y, y


def _ple_ln_fwd_fn(xin, pg, pu, gain, bias):
    xhat, _ = _ln_stats(ALPHA * xin + _sigmoid(pg) * pu)
    y = xhat * gain + bias
    return y, y


def _ln_dz(dy, z, gain):
    xhat, rstd = _ln_stats(z)
    dxhat = dy * gain
    dz = rstd * (dxhat - jnp.mean(dxhat, axis=-1, keepdims=True)
                 - xhat * jnp.mean(dxhat * xhat, axis=-1, keepdims=True))
    return dz, _col_sum(dy * xhat), _col_sum(dy)


def _ln_bwd_fn(dy, xin, h, gain):
    dz, dgain, dbias = _ln_dz(dy, ALPHA * xin + h, gain)
    return ALPHA * dz, dz, dgain, dbias, _col_sum(dz)


def _ple_ln_bwd_fn(dy, xin, pg, pu, gain):
    sg = _sigmoid(pg)
    dz, dgain, dbias = _ln_dz(dy, ALPHA * xin + sg * pu, gain)
    dpg = dz * pu * sg * (1.0 - sg)
    return ALPHA * dz, dpg, dz * sg, dgain, dbias, _col_sum(dpg)


def _swiglu_fwd_fn(gu):
    hid = gu.shape[-1] // 2
    gate, up = gu[:, :hid], gu[:, hid:]
    return gu, gate * _sigmoid(gate) * up


def _swiglu_bwd_fn(dact, gu):
    gu = gu.astype(F32)
    hid = gu.shape[-1] // 2
    gate, up = gu[:, :hid], gu[:, hid:]
    sg = _sigmoid(gate)
    dgate = dact * up * sg * (1.0 + gate * (1.0 - sg))
    dup = dact * gate * sg
    return (jnp.concatenate([dgate, dup], axis=-1),)


def _loss_fn(y, target):
    err = y - target
    inv = 1.0 / y.shape[-1]
    part = 0.5 * inv * jnp.sum(jnp.sum(err * err, axis=-1, keepdims=True), axis=0, keepdims=True)
    return err * inv, jnp.broadcast_to(part, (1, LANES))


def _adam_fn(w, mom, vel, p_own, p_sib):
    g = p_own.astype(F32) + p_sib.astype(F32)
    m_new = ADAM_B1 * mom + (1.0 - ADAM_B1) * g
    v_new = ADAM_B2 * vel + (1.0 - ADAM_B2) * (g * g)
    m_hat = m_new / (1.0 - ADAM_B1 ** ADAM_STEP)
    v_hat = v_new / (1.0 - ADAM_B2 ** ADAM_STEP)
    delta = -ADAM_LR * (m_hat / (jnp.sqrt(v_hat) + ADAM_EPS) + ADAM_WD * w)
    return g, delta, m_new, v_new


def _split2(x):
    hi = x.astype(BF16)
    return hi, (x - hi.astype(F32)).astype(BF16)


def _dot3(a, b, dims):
    a_hi, a_lo = _split2(a)
    b_hi, b_lo = _split2(b)
    dn = (dims, ((), ()))
    return (lax.dot_general(a_hi, b_hi, dn, preferred_element_type=F32)
            + (lax.dot_general(a_hi, b_lo, dn, preferred_element_type=F32)
               + lax.dot_general(a_lo, b_hi, dn, preferred_element_type=F32)))


def _tdot(mask01, b):
    m = mask01.astype(BF16)
    b_hi = b.astype(BF16)
    rest = b - b_hi.astype(F32)
    b_mid = rest.astype(BF16)
    b_lo = (rest - b_mid.astype(F32)).astype(BF16)
    dn = (((1,), (0,)), ((), ()))
    return (lax.dot_general(m, b_hi, dn, preferred_element_type=F32)
            + (lax.dot_general(m, b_mid, dn, preferred_element_type=F32)
               + lax.dot_general(m, b_lo, dn, preferred_element_type=F32)))


def _hdot(a, b):
    return _dot3(a, b, ((1,), (0,)))


def _hdot_nt(a, b):
    return _dot3(a, b, ((1,), (1,)))


def _hdot_tn(a, b):
    return _dot3(a, b, ((0,), (0,)))


def _dot(a, b):
    return lax.dot_general(a.astype(BF16), b.astype(BF16), (((1,), (0,)), ((), ())), preferred_element_type=F32)


def _dot_nt(a, b):
    return lax.dot_general(a.astype(BF16), b.astype(BF16), (((1,), (1,)), ((), ())), preferred_element_type=F32)


def _dot_tn(a, b):
    return lax.dot_general(a.astype(BF16), b.astype(BF16), (((0,), (0,)), ((), ())), preferred_element_type=F32)


def _hg_masks():
    c = HG_CHUNK
    row = lax.broadcasted_iota(jnp.int32, (c, c), 0)
    col = lax.broadcasted_iota(jnp.int32, (c, c), 1)
    base = row & (-HG_SUB)
    return row, col, base, col <= row, col < base


def _hg_gates(qr, fr, alb):
    lbound = _sigmoid(alb[0:1, :] - alb[1:2, :])
    sig = _sigmoid(fr)
    forget = lbound + (1.0 - lbound) * sig
    kk = (1.0 - lbound) * _sigmoid(-fr)
    qt = qr * _sigmoid(qr) * (HG_DK ** -0.5)
    return qt, kk, jnp.log(forget), lbound, sig, forget


def _hg_scores(qt, kk, g, scores=True):
    c, nsub = HG_CHUNK, HG_CHUNK // HG_SUB
    row, col, base, causal, below = _hg_masks()
    b = _tdot(causal, g)
    rr = _tdot(below, g)
    bq = b - rr
    qh = qt * jnp.exp(bq)
    edecs = [None]
    parts = [jnp.zeros((HG_SUB, c), F32)]
    for i in range(1, nsub):
        edec = jnp.exp(jnp.minimum(rr[i * HG_SUB:i * HG_SUB + 1, :] - b, 0.0))
        edecs.append(edec)
        if scores:
            parts.append(_dot_nt(qh[i * HG_SUB:(i + 1) * HG_SUB, :], kk * edec))
    q3 = qt.reshape(nsub, HG_SUB, HG_DK)
    if not scores:
        return None, b, bq, qh, edecs, (b.reshape(nsub, HG_SUB, HG_DK), q3, kk.reshape(nsub, HG_SUB, HG_DK))
    a = jnp.where(below, jnp.concatenate(parts, axis=0), 0.0)
    b2 = b * LOG2_E
    b3 = b2.reshape(nsub, HG_SUB, HG_DK)
    c3 = (b2 - jnp.log2(kk)).reshape(nsub, HG_SUB, HG_DK)
    for j in range(HG_SUB):
        ek = jnp.exp2(b3 - c3[:, j:j + 1, :])
        colv = jnp.sum(q3 * ek, axis=-1, keepdims=True).reshape(c, 1)
        a = jnp.where(col == base + j, colv, a)
    a = jnp.where(causal, a, 0.0)
    return a, b, bq, qh, edecs, None


def _hg_norm(o, gr, gain):
    r = lax.rsqrt(jnp.mean(o * o, axis=-1, keepdims=True) + RMS_EPS)
    sg = _sigmoid(gr)
    return o * r * gain, r, sg


def _hgrn2_fwd(proj, alb, gain, *, rb):
    m, d4 = proj.shape
    d = d4 // 4
    heads = d // HG_DK
    hp = HG_HEADS_PER_STEP
    rb = min(rb, m)
    cpb = rb // HG_CHUNK
    nrb = m // rb

    def body(q_ref, f_ref, v_ref, g_ref, alb_ref, gain_ref, o_ref, og_ref, st_ref, a_ref, state):
        @pl.when(pl.program_id(1) == 0)
        def _():
            state[...] = jnp.zeros(state.shape, F32)

        def chunk(ci, carry):
            sl = pl.ds(pl.multiple_of(ci * HG_CHUNK, HG_CHUNK), HG_CHUNK)
            for u in range(hp):
                ln = slice(u * HG_DK, (u + 1) * HG_DK)
                qt, kk, g, _, _, _ = _hg_gates(q_ref[sl, ln], f_ref[sl, ln], alb_ref[:, ln])
                v = v_ref[sl, ln]
                st = state[u]
                st_ref[u, ci] = st
                a, b, _, _, _, _ = _hg_scores(qt, kk, g)
                a_ref[u, ci] = a.astype(a_ref.dtype)
                o = _dot(a, v) + _dot_nt(qt * jnp.exp(b), st)
                b_last = b[HG_CHUNK - 1:HG_CHUNK, :]
                state[u] = st * jnp.exp(b_last) + _hdot_tn(v, kk * jnp.exp(b_last - b))
                o_ref[sl, ln] = o
                n, _, sg = _hg_norm(o, g_ref[sl, ln], gain_ref[...])
                og_ref[sl, ln] = (n * g_ref[sl, ln] * sg).astype(og_ref.dtype)
            return carry

        lax.fori_loop(0, cpb, chunk, 0)

    def col(cidx):
        return pl.BlockSpec((rb, hp * HG_DK), lambda h, r: (r, cidx * (heads // hp) + h))

    return pl.pallas_call(
        body, name="hgrn2_fwd", grid=(heads // hp, nrb),
        in_specs=[col(0), col(1), col(2), col(3),
                  pl.BlockSpec((2, hp * HG_DK), lambda h, r: (0, h)),
                  pl.BlockSpec((1, HG_DK), lambda h, r: (0, 0))],
        out_specs=[pl.BlockSpec((rb, hp * HG_DK), lambda h, r: (r, h)),
                   pl.BlockSpec((rb, hp * HG_DK), lambda h, r: (r, h)),
                   pl.BlockSpec((hp, cpb, HG_DK, HG_DK), lambda h, r: (h, r, 0, 0)),
                   pl.BlockSpec((hp, cpb, HG_CHUNK, HG_CHUNK), lambda h, r: (h, r, 0, 0))],
        out_shape=[jax.ShapeDtypeStruct((m, d), F32), jax.ShapeDtypeStruct((m, d), BF16),
                   jax.ShapeDtypeStruct((heads, m // HG_CHUNK, HG_DK, HG_DK), F32),
                   jax.ShapeDtypeStruct((heads, m // HG_CHUNK, HG_CHUNK, HG_CHUNK), BF16)],
        scratch_shapes=[pltpu.VMEM((hp, HG_DK, HG_DK), F32)],
        compiler_params=_params(("parallel", "arbitrary")),
    )(proj, proj, proj, proj, alb, gain)


def _hgrn2_bwd(proj, o_pre, states, scores, dog, alb, gain, *, rb):
    m, d4 = proj.shape
    d = d4 // 4
    heads = d // HG_DK
    rb = min(rb, m)
    cpb = rb // HG_CHUNK
    nrb = m // rb
    c, nsub = HG_CHUNK, HG_CHUNK // HG_SUB

    def body(q_ref, f_ref, v_ref, g_ref, o_ref, st_ref, a_ref, dog_ref, alb_ref, gain_ref,
             dp_ref, dalb_ref, dgain_ref, dstate, carry_ref):
        first = (pl.program_id(0) == 0) & (pl.program_id(1) == 0)

        @pl.when(first)
        def _():
            dgain_ref[...] = jnp.zeros(dgain_ref.shape, F32)

        @pl.when(pl.program_id(1) == 0)
        def _():
            dstate[...] = jnp.zeros(dstate.shape, F32)
            carry_ref[...] = jnp.zeros(carry_ref.shape, F32)
            dalb_ref[...] = jnp.zeros(dalb_ref.shape, F32)

        row, col, base, causal, below = _hg_masks()
        sub_iota = lax.broadcasted_iota(jnp.int32, (nsub, HG_SUB, HG_DK), 1)
        row_k = lax.broadcasted_iota(jnp.int32, (c, HG_DK), 0)
        upper = col >= row

        def chunk(step, carry):
            ci = cpb - 1 - step
            sl = pl.ds(pl.multiple_of(ci * HG_CHUNK, HG_CHUNK), HG_CHUNK)
            qr, fr, v, gr = q_ref[sl, :], f_ref[sl, :], v_ref[sl, :], g_ref[sl, :]
            qt, kk, g, lbound, sig, forget = _hg_gates(qr, fr, alb_ref[...])
            o = o_ref[sl, :]
            dogv = dog_ref[sl, :]
            gain_v = gain_ref[...]
            n, r, sg = _hg_norm(o, gr, gain_v)
            dgr = dogv * n * sg * (1.0 + gr * (1.0 - sg))
            dn = dogv * gr * sg
            dgain_ref[...] += _col_sum(dn * o * r)
            u = dn * gain_v
            d_o = r * u - o * (r * r * r) * jnp.mean(u * o, axis=-1, keepdims=True)
            st0 = st_ref[ci]
            dst = dstate[...]
            _, b, bq, qh, edecs, (b3, q3, k3) = _hg_scores(qt, kk, g, scores=False)
            a = a_ref[ci]
            eb = jnp.exp(b)
            b_last = b[c - 1:c, :]
            kdl_dec = jnp.exp(b_last - b)
            kdl = kk * kdl_dec
            d_a = jnp.where(causal, _dot_nt(d_o, v), 0.0)
            d_at = _dot_nt(v, d_o)
            dv = _dot_tn(a, d_o) + _dot_nt(kdl, dst)
            dq = eb * _hdot(d_o, st0)
            dk = _hdot(v, dst) * kdl_dec
            d_a_below = jnp.where(below, d_a, 0.0)
            dq_parts = [jnp.zeros((HG_SUB, HG_DK), F32)]
            for i in range(1, nsub):
                lo, hi = i * HG_SUB, (i + 1) * HG_SUB
                dq_parts.append(_hdot(d_a_below[lo:hi, :], kk * edecs[i]))
                gi = _hdot(d_at[:, lo:hi], qh[lo:hi, :])
                dk = dk + jnp.where(row_k < lo, edecs[i] * gi, 0.0)
            dq = dq + jnp.concatenate(dq_parts, axis=0) * jnp.exp(bq)
            dq3 = jnp.zeros((nsub, HG_SUB, HG_DK), F32)
            dk3 = jnp.zeros((nsub, HG_SUB, HG_DK), F32)
            d_diag = jnp.concatenate([d_a[i * HG_SUB:(i + 1) * HG_SUB, i * HG_SUB:(i + 1) * HG_SUB]
                                      for i in range(nsub)], axis=0).reshape(nsub, HG_SUB, HG_SUB)
            for j in range(HG_SUB):
                e = jnp.exp(jnp.minimum(b3 - b3[:, j:j + 1, :], 0.0))
                t1 = d_diag[:, :, j:j + 1] * e
                dq3 = dq3 + t1 * k3[:, j:j + 1, :]
                dk3 = jnp.where(sub_iota == j, jnp.sum(t1 * q3, axis=1, keepdims=True), dk3)
            dq = dq + dq3.reshape(c, HG_DK)
            dk = dk + dk3.reshape(c, HG_DK)
            dstate[...] = dst * jnp.exp(b_last) + _hdot_tn(d_o, qt * eb)
            dglog = _tdot(upper, qt * dq - kk * dk) + carry_ref[...]
            carry_ref[...] = dglog[0:1, :]
            dforget = dglog / forget
            one_m_lb = 1.0 - lbound
            dsig = (dforget - dk) * one_m_lb
            sneg = _sigmoid(-fr)
            dlb = _col_sum(dforget * (1.0 - sig) - dk * sneg)
            dalb0 = dlb * lbound * one_m_lb
            dalb_ref[...] += jnp.concatenate([dalb0, -dalb0], axis=0)
            sq = _sigmoid(qr)
            dp_ref[0, sl, :] = (dq * (HG_DK ** -0.5) * sq * (1.0 + qr * (1.0 - sq))).astype(dp_ref.dtype)
            dp_ref[1, sl, :] = (dsig * sig * (1.0 - sig)).astype(dp_ref.dtype)
            dp_ref[2, sl, :] = dv.astype(dp_ref.dtype)
            dp_ref[3, sl, :] = dgr.astype(dp_ref.dtype)
            return carry

        lax.fori_loop(0, cpb, chunk, 0, unroll=2)

    def rev(r):
        return nrb - 1 - r

    def col(cidx):
        return pl.BlockSpec((rb, HG_DK), lambda h, r: (rev(r), cidx * heads + h))

    def head_rows():
        return pl.BlockSpec((rb, HG_DK), lambda h, r: (rev(r), h))

    return pl.pallas_call(
        body, name="hgrn2_bwd", grid=(heads, nrb),
        in_specs=[col(0), col(1), col(2), col(3), head_rows(),
                  pl.BlockSpec((None, cpb, HG_DK, HG_DK), lambda h, r: (h, rev(r), 0, 0)),
                  pl.BlockSpec((None, cpb, HG_CHUNK, HG_CHUNK), lambda h, r: (h, rev(r), 0, 0)),
                  head_rows(),
                  pl.BlockSpec((2, HG_DK), lambda h, r: (0, h)),
                  pl.BlockSpec((1, HG_DK), lambda h, r: (0, 0))],
        out_specs=[pl.BlockSpec((4, rb, HG_DK), lambda h, r: (0, rev(r), h)),
                   pl.BlockSpec((2, HG_DK), lambda h, r: (0, h)),
                   pl.BlockSpec((1, HG_DK), lambda h, r: (0, 0))],
        out_shape=[jax.ShapeDtypeStruct((4, m, d), BF16), jax.ShapeDtypeStruct((2, d), F32),
                   jax.ShapeDtypeStruct((1, HG_DK), F32)],
        scratch_shapes=[pltpu.VMEM((HG_DK, HG_DK), F32), pltpu.VMEM((1, HG_DK), F32)],
        compiler_params=_params(("arbitrary", "arbitrary")),
    )(proj, proj, proj, proj, o_pre, states, scores, dog, alb, gain)


def _swa_probs(qh, kp, kc, sink, slope, has_prev, lse=None):
    rows = qh.shape[0]
    qi = lax.broadcasted_iota(jnp.int32, (rows, WINDOW), 0) & (WINDOW - 1)
    si = lax.broadcasted_iota(jnp.int32, (rows, WINDOW), 1)
    scale = ATT_HD ** -0.5
    dist_c = (qi - si).astype(F32)
    s_p = _dot_nt(qh, kp) * scale - slope * (dist_c + float(WINDOW))
    s_c = _dot_nt(qh, kc) * scale - slope * dist_c
    s_p = jnp.where((si > qi) & has_prev, s_p, NEG)
    s_c = jnp.where(si <= qi, s_c, NEG)
    if lse is not None:
        return jnp.exp(s_p - lse), jnp.exp(s_c - lse), jnp.exp(sink - lse), lse
    mx = jnp.maximum(jnp.maximum(jnp.max(s_p, axis=-1, keepdims=True), jnp.max(s_c, axis=-1, keepdims=True)), sink)
    e_p, e_c, e_s = jnp.exp(s_p - mx), jnp.exp(s_c - mx), jnp.exp(sink - mx)
    total = jnp.sum(e_p, axis=-1, keepdims=True) + jnp.sum(e_c, axis=-1, keepdims=True) + e_s
    inv = 1.0 / total
    return e_p * inv, e_c * inv, e_s * inv, mx + jnp.log(total)


def _slope(h, n_heads):
    return float(2.0 ** (-8.0 * (h + 1) / n_heads))


def _swa_group(ref_vals, sink_ref, kh, n_heads):
    heads = [kh * ATT_G + g for g in range(ATT_G)]
    stacked = [jnp.concatenate([v[:, h * ATT_HD:(h + 1) * ATT_HD] for h in heads], axis=0) for v in ref_vals]
    grp = lax.shift_right_logical(lax.broadcasted_iota(jnp.int32, (ATT_G * WINDOW, 1), 0), WINDOW.bit_length() - 1)
    slope = jnp.zeros((ATT_G * WINDOW, 1), F32)
    sink = jnp.zeros((ATT_G * WINDOW, 1), F32)
    for g, h in enumerate(heads):
        slope = jnp.where(grp == g, _slope(h, n_heads), slope)
        sink = jnp.where(grp == g, sink_ref[:, h:h + 1], sink)
    return stacked, slope, sink


def _swa_fwd(q, kv, sinks):
    m, d = q.shape
    n_heads = d // ATT_HD
    kvh = n_heads // ATT_G
    kd = kvh * ATT_HD
    nb = m // WINDOW

    def body(q_ref, kvp_ref, kvc_ref, sink_ref, o_ref, lse_ref):
        has_prev = pl.program_id(0) > 0
        qv, kvp, kvc = q_ref[...], kvp_ref[...], kvc_ref[...]
        lane_h = lax.broadcasted_iota(jnp.int32, (WINDOW, n_heads), 1)
        outs, lse_all = [], jnp.zeros((WINDOW, n_heads), F32)
        for kh in range(kvh):
            ks = slice(kh * ATT_HD, (kh + 1) * ATT_HD)
            vs = slice(kd + kh * ATT_HD, kd + (kh + 1) * ATT_HD)
            (q4,), slope, sink = _swa_group([qv], sink_ref, kh, n_heads)
            p_p, p_c, _, lse = _swa_probs(q4, kvp[:, ks], kvc[:, ks], sink, slope, has_prev)
            o4 = _dot(p_p, kvp[:, vs]) + _dot(p_c, kvc[:, vs])
            for g in range(ATT_G):
                rows = slice(g * WINDOW, (g + 1) * WINDOW)
                outs.append(o4[rows, :])
                lse_all = jnp.where(lane_h == kh * ATT_G + g, lse[rows, :], lse_all)
        o_ref[...] = jnp.concatenate(outs, axis=-1).astype(o_ref.dtype)
        lse_ref[...] = lse_all

    return pl.pallas_call(
        body, name="swa_fwd", grid=(nb,),
        in_specs=[pl.BlockSpec((WINDOW, d), lambda n: (n, 0)),
                  pl.BlockSpec((WINDOW, 2 * kd), lambda n: (jnp.maximum(n - 1, 0), 0)),
                  pl.BlockSpec((WINDOW, 2 * kd), lambda n: (n, 0)),
                  pl.BlockSpec((1, n_heads), lambda n: (0, 0))],
        out_specs=[pl.BlockSpec((WINDOW, d), lambda n: (n, 0)), pl.BlockSpec((WINDOW, n_heads), lambda n: (n, 0))],
        out_shape=[jax.ShapeDtypeStruct((m, d), BF16), jax.ShapeDtypeStruct((m, n_heads), F32)],
        compiler_params=_params(("arbitrary",)),
    )(q, kv, kv, sinks)


def _swa_bwd(q, kv, sinks, lse, dao):
    m, d = q.shape
    n_heads = d // ATT_HD
    kvh = n_heads // ATT_G
    kd = kvh * ATT_HD
    nb = m // WINDOW
    scale = ATT_HD ** -0.5

    def body(q_ref, kvp_ref, kvc_ref, sink_ref, lse_ref, do_ref, dq_ref, dkvc_ref, dkvp_ref, dqsum_ref, dsink_ref):
        @pl.when(pl.program_id(0) == 0)
        def _():
            dqsum_ref[...] = jnp.zeros(dqsum_ref.shape, F32)
            dsink_ref[...] = jnp.zeros(dsink_ref.shape, F32)

        has_prev = pl.program_id(0) > 0
        qv, kvp, kvc, dov = q_ref[...], kvp_ref[...], kvc_ref[...], do_ref[...]
        lane_h = lax.broadcasted_iota(jnp.int32, (1, n_heads), 1)
        dsink = jnp.zeros((1, n_heads), F32)
        dq_parts, dk_p, dk_c, dv_p, dv_c = [], [], [], [], []
        for kh in range(kvh):
            ks = slice(kh * ATT_HD, (kh + 1) * ATT_HD)
            vs = slice(kd + kh * ATT_HD, kd + (kh + 1) * ATT_HD)
            kp, kc, vp, vc = kvp[:, ks], kvc[:, ks], kvp[:, vs], kvc[:, vs]
            (q4, do4), slope, sink = _swa_group([qv, dov], sink_ref, kh, n_heads)
            lse4 = jnp.concatenate([lse_ref[:, kh * ATT_G + g:kh * ATT_G + g + 1] for g in range(ATT_G)], axis=0)
            p_p, p_c, p_s, _ = _swa_probs(q4, kp, kc, sink, slope, has_prev, lse=lse4)
            dp_p, dp_c = _dot_nt(do4, vp), _dot_nt(do4, vc)
            delta = jnp.sum(p_p * dp_p, axis=-1, keepdims=True) + jnp.sum(p_c * dp_c, axis=-1, keepdims=True)
            ds_p, ds_c = p_p * (dp_p - delta), p_c * (dp_c - delta)
            sink_term = p_s * delta
            dq4 = (_dot(ds_p, kp) + _dot(ds_c, kc)) * scale
            for g in range(ATT_G):
                rows = slice(g * WINDOW, (g + 1) * WINDOW)
                dsink = dsink + jnp.where(lane_h == kh * ATT_G + g, -_col_sum(sink_term[rows, :]), 0.0)
                dq_parts.append(dq4[rows, :])
            dk_p.append(_dot_tn(ds_p, q4) * scale)
            dk_c.append(_dot_tn(ds_c, q4) * scale)
            dv_p.append(_dot_tn(p_p, do4))
            dv_c.append(_dot_tn(p_c, do4))
        dq = jnp.concatenate(dq_parts, axis=-1)
        dq_ref[...] = dq.astype(dq_ref.dtype)
        dqsum_ref[...] += _col_sum(dq)
        dsink_ref[...] += dsink
        dkvc_ref[...] = jnp.concatenate(dk_c + dv_c, axis=-1)
        dkvp_ref[...] = jnp.concatenate(dk_p + dv_p, axis=-1)

    return pl.pallas_call(
        body, name="swa_bwd", grid=(nb,),
        in_specs=[pl.BlockSpec((WINDOW, d), lambda n: (n, 0)),
                  pl.BlockSpec((WINDOW, 2 * kd), lambda n: (jnp.maximum(n - 1, 0), 0)),
                  pl.BlockSpec((WINDOW, 2 * kd), lambda n: (n, 0)),
                  pl.BlockSpec((1, n_heads), lambda n: (0, 0)),
                  pl.BlockSpec((WINDOW, n_heads), lambda n: (n, 0)),
                  pl.BlockSpec((WINDOW, d), lambda n: (n, 0))],
        out_specs=[pl.BlockSpec((WINDOW, d), lambda n: (n, 0)),
                   pl.BlockSpec((WINDOW, 2 * kd), lambda n: (n, 0)),
                   pl.BlockSpec((WINDOW, 2 * kd), lambda n: (n, 0)),
                   pl.BlockSpec((1, d), lambda n: (0, 0)),
                   pl.BlockSpec((1, n_heads), lambda n: (0, 0))],
        out_shape=[jax.ShapeDtypeStruct((m, d), BF16), jax.ShapeDtypeStruct((m, 2 * kd), F32),
                   jax.ShapeDtypeStruct((m, 2 * kd), F32), jax.ShapeDtypeStruct((1, d), F32),
                   jax.ShapeDtypeStruct((1, n_heads), F32)],
        compiler_params=_params(("arbitrary",)),
    )(q, kv, kv, sinks, lse, dao)


def _kv_grad_combine(dkv_cur, dkv_prev):
    m, w = dkv_cur.shape
    nb = m // WINDOW
    per = max(g for g in (1, 2, 4) if nb % g == 0)
    rows, steps = per * WINDOW, nb // per

    def body(cur_ref, same_ref, next_ref, o_ref, sum_ref):
        @pl.when(pl.program_id(0) == 0)
        def _():
            sum_ref[...] = jnp.zeros(sum_ref.shape, F32)

        after = jnp.where(pl.program_id(0) < steps - 1, next_ref[...], 0.0)
        total = cur_ref[...] + (jnp.concatenate([same_ref[WINDOW:, :], after], axis=0) if per > 1 else after)
        o_ref[...] = total.astype(o_ref.dtype)
        sum_ref[...] += _col_sum(total)

    return pl.pallas_call(
        body, name="kv_grad_combine", grid=(steps,),
        in_specs=[pl.BlockSpec((rows, w), lambda n: (n, 0)), pl.BlockSpec((rows, w), lambda n: (n, 0)),
                  pl.BlockSpec((WINDOW, w), lambda n: (jnp.minimum((n + 1) * per, nb - 1), 0))],
        out_specs=[pl.BlockSpec((rows, w), lambda n: (n, 0)), pl.BlockSpec((1, w), lambda n: (0, 0))],
        out_shape=[jax.ShapeDtypeStruct((m, w), BF16), jax.ShapeDtypeStruct((1, w), F32)],
        compiler_params=_params(("arbitrary",)),
    )(dkv_cur, dkv_prev, dkv_prev)


def _row(v):
    return v.reshape(1, -1)


def _local_step(x, p, target, wget, grad_sink, ln_gain, ln_bias, alb, norm_gain, kv_b, b_q, sinks, b_out, ple_b,
                small_sink=None):
    gs = {}
    gains = ln_gain.reshape(DEPTH * 3, -1)
    biases = ln_bias.reshape(DEPTH * 3, -1)
    sd = x.shape
    pending = [None]

    def mm(a, b, lb=0, **kw):
        after, pending[0] = pending[0], None
        return _mm(a, b, lb=lb, after=after, **kw)

    def mm_ln(a, wt, xin, i, j, nm, bias=None, pu=None):
        r = 3 * i + j
        if pu is None:
            fn, rows = (lambda h, xv, g, bv: (h,) + _ln_fwd_fn(xv, h, g[r:r + 1], bv[r:r + 1])), [xin]
        else:
            fn = lambda h, xv, puv, g, bv: (h,) + _ple_ln_fwd_fn(xv, h, puv, g[r:r + 1], bv[r:r + 1])
            rows = [xin, pu]
        h, y, yb = _mm(a, wt, lb=0, bias=bias, name=nm,
                       post=(fn, rows, [gains, biases], [(sd, F32), (sd, F32), (sd, BF16)], []))
        return h, (y, yb)

    def mm_ln_bwd(a, wt, add, xin, h, i, j, nm):
        r = 3 * i + j
        dx_part, dh, dg, db, dhsum = mm(a, wt, tb=True, add=add, name=nm,
                                        post=(lambda dy, xv, hv, g: _ln_bwd_fn(dy, xv, hv, g[r:r + 1]), [xin, h],
                                              [gains], [(sd, F32), (sd, BF16)], [((1, sd[1]), F32)] * 3))
        gs[f"ln_gain_{i}_{j}"], gs[f"ln_bias_{i}_{j}"] = dg, db
        return dx_part, dh, dhsum

    def tail_fwd(xa, i):
        wgu = wget("ffn_w_gate_up", i, xa[1])
        hid2 = wgu.shape[-1]
        gu, act = _mm(xa[1], wgu, lb=0, name=f"ffn_up_swiglu{i}", tile_cols=hid2 // 2,
                      post=(_swiglu_fwd_fn, [], [], [((sd[0], hid2), BF16), ((sd[0], hid2 // 2), BF16)], []))
        f, xb = mm_ln(act, wget("ffn_w_down", i, act), xa[0], i, 1, f"ffn_down_ln{i}")
        pu = _mm(p, wget("ple_w_up", i, act), la=i, lb=0, name=f"ple_up{i}")
        pg, xc = mm_ln(xb[1], wget("ple_w_gate", i, act), xb[0], i, 2, f"ple_gate_ln{i}", bias=_row(ple_b[i]), pu=pu)
        return dict(xa=xa, gu=gu, act=act, f=f, xb=xb, pg=pg, pu=pu), xc

    def tail_bwd(head, sv, i, mix_in, mix_h):
        xa, xb = sv["xa"], sv["xb"]
        r = 3 * i + 2
        dxb_part, dpg, dpu, dg2, db2, dbg = head(
            lambda dy, xv, pgv, puv, g: _ple_ln_bwd_fn(dy, xv, pgv, puv, g[r:r + 1]), [xb[0], sv["pg"], sv["pu"]],
            [gains], [(sd, F32), (sd, BF16), (sd, BF16)], [((1, sd[1]), F32)] * 3)[:6]
        gs[f"ple_b_{i}"] = dbg
        gs[f"ln_gain_{i}_2"], gs[f"ln_bias_{i}_2"] = dg2, db2
        grad_of("ple_w_gate", i, xb[1], dpg)
        grad_of("ple_w_up", i, p, dpu, la=i)
        dxa_part, df, _ = mm_ln_bwd(dpg, wget("ple_w_gate", i, None), dxb_part, xa[0], sv["f"], i, 1,
                                    f"ple_gate_dx_ln{i}")
        grad_of("ffn_w_down", i, sv["act"], df)
        gu = sv["gu"]
        dgu, = mm(df, wget("ffn_w_down", i, None), tb=True, name=f"ffn_down_dx_swiglu{i}", tile_cols=gu.shape[1] // 4,
                  post=(_swiglu_bwd_fn, [gu], [], [(gu.shape, BF16)], []))
        grad_of("ffn_w_gate_up", i, xa[1], dgu)
        return mm_ln_bwd(dgu, wget("ffn_w_gate_up", i, None), dxa_part, mix_in, mix_h, i, 0, f"ffn_up_dx_ln{i}")

    def grad_of(nm, i, act, dout, la=None, b_parts=None):
        grad = mm(act, dout, la=la, lb=None, ta=True, out_dtype=BF16, out_layers=1, out_layer=0,
                  name=f"grad_{nm}{i}", b_parts=b_parts)
        token = grad_sink(nm, i, grad)
        if token is not None:
            pending[0] = token

    proj = _mm(x, wget("a_w_in", 0, None), lb=0, name="hg_proj")
    o_pre, og, states, scores = _hgrn2_fwd(proj, alb, norm_gain, rb=HG_ROWS)
    h0, x1 = mm_ln(og, wget("a_w_out", 0, og), x, 0, 0, "hg_out_ln")
    sv0, x3 = tail_fwd(x1, 0)
    kv = _mm(x3[1], wget("kv_w", 0, x3[1]), lb=0, bias=_row(kv_b), out_dtype=BF16, name="kv_proj")
    q = _mm(x3[1], wget("b_w_q", 0, x3[1]), lb=0, bias=b_q, out_dtype=BF16, name="q_proj")
    ao, lse = _swa_fwd(q, kv, sinks)
    h1, x4 = mm_ln(ao, wget("b_w_out", 0, x3[1]), x3[0], 1, 0, "att_out_ln", bias=b_out)
    sv1, y = tail_fwd(x4, 1)

    loss_box = []

    def loss_head(fn, rows, whole, outs, sums):
        def with_loss(yv, tv, *rest):
            dy, part = _loss_fn(yv, tv)
            return fn(dy, *rest) + (part,)

        res = _rowwise(with_loss, [y[0], target] + rows, whole, outs, list(sums) + [((1, LANES), F32)],
                       name="loss_ln_ple_bwd1")
        loss_box.append(res[-1])
        return res

    dx3_part, dh1, dh1sum = tail_bwd(loss_head, sv1, 1, x3[0], h1)
    loss = loss_box[0]
    gs["b_out"] = dh1sum
    grad_of("b_w_out", 0, ao, dh1)
    dao = mm(dh1, wget("b_w_out", 0, None), tb=True, out_dtype=BF16, name="att_out_dx")
    dq, dkv_cur, dkv_prev, dqsum, dsinks = _swa_bwd(q, kv, sinks, lse, dao)
    gs["b_q"], gs["sinks"] = dqsum, dsinks
    dkv, dkvsum = _kv_grad_combine(dkv_cur, dkv_prev)
    gs["kv_b"] = dkvsum
    grad_of("b_w_q", 0, x3[1], dq)
    grad_of("kv_w", 0, x3[1], dkv)
    dx3 = mm(dq, wget("b_w_q", 0, None), tb=True, add=dx3_part, name="q_proj_dx")

    def kv_head(*post):
        return mm(dkv, wget("kv_w", 0, None), tb=True, add=dx3, name="kv_proj_dx_ln_ple_bwd0", post=post)

    dx_part, dh0, _ = tail_bwd(kv_head, sv0, 0, x, h0)
    grad_of("a_w_out", 0, og, dh0)
    dog = mm(dh0, wget("a_w_out", 0, None), tb=True, name="hg_out_dx")
    dproj, dalb, dgain = _hgrn2_bwd(proj, o_pre, states, scores, dog, alb, norm_gain, rb=HG_ROWS)
    gs["alb"], gs["norm_gain"] = dalb, dgain
    if small_sink is not None:
        pending[0] = small_sink(loss, gs)
    grad_of("a_w_in", 0, x, dproj, b_parts=4)
    grad_x = mm(dproj, wget("a_w_in", 0, None), tb=True, add=dx_part, name="hg_proj_dx", a_parts=4)
    return loss, grad_x, gs


HBM_SPEC = pl.BlockSpec(memory_space=pl.ANY)
HBM_ONLY = pl.BlockSpec(memory_space=pltpu.HBM)
SEM_SPEC = pl.BlockSpec(memory_space=pltpu.SEMAPHORE)
SIDE_EFFECT = pltpu.SideEffectType.DATAFLOW_SIDE_EFFECTING


def _slot(kind, j):
    return (j % 2) * 2 + j // 2 if kind == "colp" else j


def _piece(ref, kind, j):
    _, r, c = ref.shape
    if kind == "row":
        return ref.at[:, pl.ds(j * (r // N_CHIPS), r // N_CHIPS), :]
    return ref.at[:, :, pl.ds(_slot(kind, j) * (c // N_CHIPS), c // N_CHIPS)]


def _piece_dyn(ref, kind, j):
    _, r, c = ref.shape
    if kind == "row":
        return ref.at[:, pl.ds(pl.multiple_of(j * (r // N_CHIPS), 16), r // N_CHIPS), :]
    return ref.at[:, :, pl.ds(pl.multiple_of(_slot(kind, j) * (c // N_CHIPS), LANES), c // N_CHIPS)]


def _chip_of(j, c):
    return (j // 2, j % 2, c)


def _in_hbm(a):
    return pltpu.with_memory_space_constraint(a, pltpu.HBM)


PLACE_STEPS = 4


def _place(items, chip, *, name, after=None):
    n = len(items)
    in_specs, out_specs, out_shapes, blocks = [], [], [], []
    for src, layer, kind, out_dtype in items:
        _, r, c = src.shape
        nb = max(k for k in (1, 2, PLACE_STEPS) if r % (16 * k) == 0 or k == 1)
        blocks.append(nb)

        def src_idx(i, chip_ref, layer=layer, nb=nb):
            return (layer, jnp.minimum(i, nb - 1), 0)

        def full_idx(i, chip_ref, kind=kind, nb=nb):
            ib = jnp.minimum(i, nb - 1)
            return (0, chip_ref[0] * nb + ib, 0) if kind == "row" else (0, ib, _slot(kind, chip_ref[0]))

        in_specs.append(pl.BlockSpec((None, r // nb, c), src_idx))
        out_specs.append(pl.BlockSpec((None, r // nb, c), full_idx))
        out_shapes.append(jax.ShapeDtypeStruct((1, r * N_CHIPS, c) if kind == "row" else (1, r, c * N_CHIPS),
                                               out_dtype))
    operands = [it[0] for it in items]
    if after is not None:
        in_specs.append(HBM_SPEC)
        operands.append(after)

    def body(chip_ref, *refs):
        for a in range(n):
            refs[len(refs) - n + a][...] = refs[a][...].astype(refs[len(refs) - n + a].dtype)

    return pl.pallas_call(
        body, name=name,
        grid_spec=pltpu.PrefetchScalarGridSpec(num_scalar_prefetch=1, grid=(PLACE_STEPS,), in_specs=in_specs,
                                               out_specs=out_specs),
        out_shape=out_shapes,
        compiler_params=_params(("arbitrary",)),
    )(chip, *operands)


def _half(ref, c):
    h = ref.shape[1] // 2
    start = c * h if isinstance(c, int) else pl.multiple_of(c * h, 16)
    return ref.at[:, pl.ds(start, h), :]


def _sibling_handshake():
    barrier = pltpu.get_barrier_semaphore()
    sibling = (lax.axis_index("x"), lax.axis_index("y"), 1 - lax.axis_index("c"))
    pl.semaphore_signal(barrier, inc=1, device_id=sibling, device_id_type=MESH)
    pl.semaphore_wait(barrier, 1)


class _SiblingFill:
    def __init__(self, lands, kinds, name, collective_id):
        self.kinds, self.name, self.n = kinds, name, len(lands)
        n = self.n
        sem_shape = pltpu.SemaphoreType.DMA((n * N_CHIPS,))

        def body(*refs):
            land_refs, send_sems, recv_sems, token = refs[:n], refs[n], refs[n + 1], refs[-1]
            _sibling_handshake()
            for cp in self._copies(land_refs, send_sems, recv_sems):
                cp.start()
            token[...] = jnp.zeros(token.shape, token.dtype)

        outs = pl.pallas_call(
            body, name=name + "_start",
            in_specs=[HBM_ONLY] * n,
            out_specs=[SEM_SPEC, SEM_SPEC] + [HBM_ONLY] * n + [pl.BlockSpec(memory_space=pltpu.VMEM)],
            out_shape=[sem_shape, sem_shape] + [pltpu.HBM(a.shape, a.dtype) for a in lands]
                      + [jax.ShapeDtypeStruct((8, LANES), F32)],
            input_output_aliases={i: i + 2 for i in range(n)},
            compiler_params=pltpu.CompilerParams(has_side_effects=SIDE_EFFECT, collective_id=collective_id),
        )(*[_in_hbm(a) for a in lands])
        self.send_sems, self.recv_sems, self.lands, self.token = outs[0], outs[1], list(outs[2:2 + n]), outs[-1]

    def _copies(self, land_refs, send_sems, recv_sems):
        x, y, c = lax.axis_index("x"), lax.axis_index("y"), lax.axis_index("c")
        me = 2 * x + y
        copies = []
        for a in range(self.n):
            for k in range(1, N_CHIPS):
                t = (me + k) % N_CHIPS
                slice_t = _piece_dyn(land_refs[a], self.kinds[a], t)
                got = _half(slice_t, c)
                copies.append(pltpu.make_async_remote_copy(
                    src_ref=got, dst_ref=got, send_sem=send_sems.at[a * N_CHIPS + k],
                    recv_sem=recv_sems.at[a * N_CHIPS + k], device_id=(x, y, 1 - c), device_id_type=MESH))
        return copies

    def wait(self, after):
        n = self.n

        def body(*refs):
            land_refs, send_sems, recv_sems = refs[:n], refs[n], refs[n + 1]
            for cp in self._copies(land_refs, send_sems, recv_sems):
                cp.wait_send()
                cp.wait_recv()

        operands = [_in_hbm(a) for a in self.lands] + [self.send_sems, self.recv_sems]
        in_specs = [HBM_ONLY] * n + [SEM_SPEC, SEM_SPEC]
        if after is not None:
            operands.append(after)
            in_specs.append(HBM_SPEC)
        outs = pl.pallas_call(
            body, name=self.name + "_wait",
            in_specs=in_specs, out_specs=[HBM_ONLY] * n,
            out_shape=[pltpu.HBM(a.shape, a.dtype) for a in self.lands],
            input_output_aliases={i: i for i in range(n)},
            compiler_params=pltpu.CompilerParams(has_side_effects=SIDE_EFFECT),
        )(*operands)
        return list(outs)


class _Exchange:
    def __init__(self, mode, srcs, lands, kinds, layers, name, collective_id, after=None, halves=None):
        self.mode, self.kinds, self.layers, self.name, self.n = mode, kinds, layers, name, len(lands)
        self.halves = halves if halves is not None else [False] * len(lands)
        n, ns = self.n, len(srcs)
        n_in = ns + n + (after is not None)
        sem_shape = pltpu.SemaphoreType.DMA((n * N_CHIPS,))

        def body(*refs):
            src_refs, land_refs = refs[:ns], refs[ns:ns + n]
            send_sems, recv_sems = refs[n_in], refs[n_in + 1]
            token = refs[-1]
            c = lax.axis_index("c")
            me = 2 * lax.axis_index("x") + lax.axis_index("y")
            barrier = pltpu.get_barrier_semaphore()
            for k in range(1, N_CHIPS):
                t = (me + k) % N_CHIPS
                pl.semaphore_signal(barrier, inc=1, device_id=(t // 2, t % 2, c), device_id_type=MESH)
            pl.semaphore_wait(barrier, N_CHIPS - 1)
            for j in range(N_CHIPS):
                @pl.when(me == j)
                def _():
                    for a in range(n):
                        for t in range(N_CHIPS):
                            if t != j:
                                src, dst = self._ends(src_refs, land_refs, a, j, t, c)
                                pltpu.make_async_remote_copy(
                                    src_ref=src, dst_ref=dst, send_sem=send_sems.at[a * N_CHIPS + t],
                                    recv_sem=recv_sems.at[a * N_CHIPS + j],
                                    device_id=_chip_of(t, c), device_id_type=MESH).start()
            token[...] = jnp.zeros(token.shape, token.dtype)

        arrays = list(srcs) + list(lands)
        operands = [_in_hbm(a) for a in arrays]
        in_specs = [HBM_ONLY] * (ns + n)
        if after is not None:
            operands.append(after)
            in_specs.append(HBM_SPEC)
        outs = pl.pallas_call(
            body, name=name + "_start",
            in_specs=in_specs,
            out_specs=[SEM_SPEC, SEM_SPEC] + [HBM_ONLY] * (ns + n) + [pl.BlockSpec(memory_space=pltpu.VMEM)],
            out_shape=[sem_shape, sem_shape] + [pltpu.HBM(a.shape, a.dtype) for a in arrays]
                      + [jax.ShapeDtypeStruct((8, LANES), F32)],
            input_output_aliases={i: i + 2 for i in range(ns + n)},
            compiler_params=pltpu.CompilerParams(has_side_effects=SIDE_EFFECT, collective_id=collective_id),
        )(*operands)
        self.send_sems, self.recv_sems = outs[0], outs[1]
        self.srcs, self.lands = list(outs[2:2 + ns]), list(outs[2 + ns:2 + ns + n])
        self.token = outs[-1]

    def _ends(self, src_refs, land_refs, a, me_j, peer, c):
        if self.mode == "gather":
            mine = _piece(land_refs[a], self.kinds[a], me_j)
            if self.halves[a]:
                mine = _half(mine, c)
            return mine, mine
        return _piece(src_refs[a], self.kinds[a], peer), land_refs[a].at[me_j, pl.ds(self.layers[a], 1)]

    def wait(self, after, lands=None):
        n, ns = self.n, len(self.srcs)
        lands = self.lands if lands is None else lands

        def body(*refs):
            src_refs, land_refs = refs[:ns], refs[ns:ns + n]
            send_sems, recv_sems = refs[ns + n], refs[ns + n + 1]
            c = lax.axis_index("c")
            me = 2 * lax.axis_index("x") + lax.axis_index("y")
            for j in range(N_CHIPS):
                @pl.when(me != j)
                def _():
                    for a in range(n):
                        sent, _ = self._ends(src_refs, land_refs, a, 0, j, c)
                        _, landed = self._ends(src_refs, land_refs, a, j, 0, c)
                        cp = pltpu.make_async_remote_copy(
                            src_ref=sent, dst_ref=landed, send_sem=send_sems.at[a * N_CHIPS + j],
                            recv_sem=recv_sems.at[a * N_CHIPS + j],
                            device_id=_chip_of(j, c), device_id_type=MESH)
                        cp.wait_send()
                        cp.wait_recv()

        arrays = self.srcs + list(lands)
        operands = [_in_hbm(a) for a in arrays] + [self.send_sems, self.recv_sems]
        in_specs = [HBM_ONLY] * (ns + n) + [SEM_SPEC, SEM_SPEC]
        if after is not None:
            operands.append(after)
            in_specs.append(HBM_SPEC)
        outs = pl.pallas_call(
            body, name=self.name + "_wait",
            in_specs=in_specs, out_specs=[HBM_ONLY] * (ns + n),
            out_shape=[pltpu.HBM(a.shape, a.dtype) for a in arrays],
            input_output_aliases={i: i for i in range(ns + n)},
            compiler_params=pltpu.CompilerParams(has_side_effects=SIDE_EFFECT),
        )(*operands)
        return list(outs[:ns]), list(outs[ns:])


def _sum_arrivals(zone, own_grads, kind, chip, name, after=None):
    _, layers, r, c = zone.shape
    tm = _pick_rows(r, 512)
    if layers * (r // tm) == 1 and r % 32 == 0:
        tm = r // 2
    nb = r // tm

    def own_idx(u):
        def idx(l, i, chip_ref):
            ib = jnp.where(l == u, i, 0)
            return (0, chip_ref[0] * nb + ib, 0) if kind == "row" else (0, ib, _slot(kind, chip_ref[0]))
        return idx

    def slot_idx(k):
        return lambda l, i, chip_ref: ((chip_ref[0] + k) % N_CHIPS, l, i, 0)

    in_specs = [pl.BlockSpec((None, None, tm, c), slot_idx(k)) for k in range(1, N_CHIPS)]
    in_specs += [pl.BlockSpec((None, tm, c), own_idx(u)) for u in range(len(own_grads))]
    operands = [zone] * (N_CHIPS - 1) + list(own_grads)
    if after is not None:
        in_specs.append(HBM_SPEC)
        operands.append(after)

    def body(chip_ref, *refs):
        slot_refs, own_refs, o_ref = refs[:N_CHIPS - 1], refs[N_CHIPS - 1:N_CHIPS - 1 + layers], refs[-1]
        own = own_refs[0][...]
        for u in range(1, layers):
            own = jnp.where(pl.program_id(0) == u, own_refs[u][...], own)
        acc = own.astype(F32)
        for ref in slot_refs:
            acc = acc + ref[...].astype(F32)
        o_ref[...] = acc.astype(o_ref.dtype)

    return pl.pallas_call(
        body, name=name,
        grid_spec=pltpu.PrefetchScalarGridSpec(
            num_scalar_prefetch=1, grid=(layers, nb), in_specs=in_specs,
            out_specs=pl.BlockSpec((tm, c), lambda l, i, chip_ref: (l * nb + i, 0))),
        out_shape=jax.ShapeDtypeStruct((layers * r, c), BF16),
        compiler_params=_params(("arbitrary", "arbitrary")),
    )(chip, *operands)


class _SiblingSwap:
    def __init__(self, arrays, name, collective_id, after=None):
        self.name, self.n = name, len(arrays)
        n = self.n
        n_in = n + (after is not None)
        sem_shape = pltpu.SemaphoreType.DMA((n,))

        def body(*refs):
            ins, send_sems, recv_sems = refs[:n], refs[n_in], refs[n_in + 1]
            theirs, token = refs[n_in + 2 + n:n_in + 2 + 2 * n], refs[-1]
            _sibling_handshake()
            for cp in self._copies(ins, theirs, send_sems, recv_sems):
                cp.start()
            token[...] = jnp.zeros(token.shape, token.dtype)

        operands, in_specs = [_in_hbm(a) for a in arrays], [HBM_ONLY] * n
        if after is not None:
            operands.append(after)
            in_specs.append(HBM_SPEC)
        outs = pl.pallas_call(
            body, name=name + "_start",
            in_specs=in_specs,
            out_specs=[SEM_SPEC, SEM_SPEC] + [HBM_ONLY] * (2 * n) + [pl.BlockSpec(memory_space=pltpu.VMEM)],
            out_shape=[sem_shape, sem_shape] + [pltpu.HBM(a.shape, a.dtype) for a in arrays] * 2
                      + [jax.ShapeDtypeStruct((8, LANES), F32)],
            input_output_aliases={i: i + 2 for i in range(n)},
            compiler_params=pltpu.CompilerParams(has_side_effects=SIDE_EFFECT, collective_id=collective_id),
        )(*operands)
        self.send_sems, self.recv_sems = outs[0], outs[1]
        self.mine, self.theirs, self.token = list(outs[2:2 + n]), list(outs[2 + n:2 + 2 * n]), outs[-1]

    def _copies(self, mine, theirs, send_sems, recv_sems):
        sibling = (lax.axis_index("x"), lax.axis_index("y"), 1 - lax.axis_index("c"))
        return [pltpu.make_async_remote_copy(src_ref=mine[a], dst_ref=theirs[a], send_sem=send_sems.at[a],
                                             recv_sem=recv_sems.at[a], device_id=sibling, device_id_type=MESH)
                for a in range(self.n)]

    def wait(self, after):
        n = self.n

        def body(*refs):
            for cp in self._copies(refs[:n], refs[n:2 * n], refs[2 * n], refs[2 * n + 1]):
                cp.wait_send()
                cp.wait_recv()

        arrays = self.mine + self.theirs
        outs = pl.pallas_call(
            body, name=self.name + "_wait",
            in_specs=[HBM_ONLY] * (2 * n) + [SEM_SPEC, SEM_SPEC, HBM_SPEC], out_specs=[HBM_ONLY] * (2 * n),
            out_shape=[pltpu.HBM(a.shape, a.dtype) for a in arrays],
            input_output_aliases={i: i for i in range(2 * n)},
            compiler_params=pltpu.CompilerParams(has_side_effects=SIDE_EFFECT),
        )(*[_in_hbm(a) for a in arrays], self.send_sems, self.recv_sems, after)
        return list(outs[:n]), list(outs[n:])


class _GatherDevices:
    def __init__(self, vec):
        sem_shape = pltpu.SemaphoreType.DMA((N_DEV,))

        def body(in_ref, send_sems, recv_sems, vec_ref, out_ref, token):
            for cp in self._copies(in_ref, out_ref, send_sems, recv_sems):
                cp.start()
            token[...] = jnp.zeros(token.shape, token.dtype)

        outs = pl.pallas_call(
            body, name="gather_small_start",
            in_specs=[HBM_ONLY],
            out_specs=[SEM_SPEC, SEM_SPEC, HBM_ONLY, HBM_ONLY, pl.BlockSpec(memory_space=pltpu.VMEM)],
            out_shape=[sem_shape, sem_shape, pltpu.HBM(vec.shape, vec.dtype),
                       pltpu.HBM((N_DEV,) + vec.shape, vec.dtype), jax.ShapeDtypeStruct((8, LANES), F32)],
            input_output_aliases={0: 2},
            compiler_params=pltpu.CompilerParams(has_side_effects=SIDE_EFFECT),
        )(_in_hbm(vec))
        self.send_sems, self.recv_sems, self.vec, self.rows, self.token = outs

    def _copies(self, in_ref, out_ref, send_sems, recv_sems):
        x, y, c = lax.axis_index("x"), lax.axis_index("y"), lax.axis_index("c")
        me = 4 * x + 2 * y + c
        copies = [pltpu.make_async_copy(in_ref, out_ref.at[me], recv_sems.at[0])]
        for rel in range(1, N_DEV):
            peer = (x ^ (rel >> 2), y ^ ((rel >> 1) & 1), c ^ (rel & 1))
            copies.append(pltpu.make_async_remote_copy(
                src_ref=in_ref, dst_ref=out_ref.at[me], send_sem=send_sems.at[rel], recv_sem=recv_sems.at[rel],
                device_id=peer, device_id_type=MESH))
        return copies

    def wait(self, after):
        def body(vec_ref, rows_ref, send_sems, recv_sems, after_ref, vec_out, rows_out):
            copies = self._copies(vec_ref, rows_ref, send_sems, recv_sems)
            copies[0].wait()
            for cp in copies[1:]:
                cp.wait_send()
                cp.wait_recv()

        outs = pl.pallas_call(
            body, name="gather_small_wait",
            in_specs=[HBM_ONLY, HBM_ONLY, SEM_SPEC, SEM_SPEC, HBM_SPEC], out_specs=[HBM_ONLY, HBM_ONLY],
            out_shape=[pltpu.HBM(self.vec.shape, self.vec.dtype), pltpu.HBM(self.rows.shape, self.rows.dtype)],
            input_output_aliases={0: 0, 1: 1},
            compiler_params=pltpu.CompilerParams(has_side_effects=SIDE_EFFECT),
        )(_in_hbm(self.vec), _in_hbm(self.rows), self.send_sems, self.recv_sems, after)
        return outs[1]


BIG = [("a_w_in", "col"), ("a_w_out", "row"), ("kv_w", "row"), ("b_w_q", "row"), ("b_w_out", "row"),
       ("ffn_w_gate_up", "colp"), ("ffn_w_down", "row"), ("ple_w_up", "col"), ("ple_w_gate", "row")]
GATHER_GROUPS = [[("a_w_in", 0), ("small", 0)], [("a_w_out", 0)],
                 [("ffn_w_gate_up", 0), ("ffn_w_down", 0), ("ple_w_gate", 0), ("ple_w_up", 0)],
                 [("kv_w", 0), ("b_w_q", 0), ("b_w_out", 0)],
                 [("ffn_w_gate_up", 1)], [("ffn_w_down", 1), ("ple_w_gate", 1), ("ple_w_up", 1)]]
SCATTER_GROUPS = [[("ple_w_gate", 1), ("ple_w_up", 1), ("ffn_w_down", 1)], [("ffn_w_gate_up", 1)],
                  [("b_w_out", 0), ("b_w_q", 0), ("kv_w", 0)], [("ple_w_gate", 0), ("ple_w_up", 0), ("ffn_w_down", 0)],
                  [("ffn_w_gate_up", 0), ("a_w_out", 0)], [("a_w_in", 0)]]
COLLECTIVE_IDS = {"fill": 0, "swap": 6, "gather": 9, "scatter": 15}
SMALL_SHARDED = ["ln_gain", "ln_bias", "a_lower_bound"]
SMALL_REPLICATED = ["a_norm_gain", "kv_b", "b_b_q", "b_sinks", "b_b_out", "ple_b_gate"]
WEIGHT_ORDER = ["a_w_in", "a_lower_bound", "a_norm_gain", "a_w_out", "kv_w", "kv_b", "b_w_q", "b_b_q", "b_sinks",
                "b_w_out", "b_b_out", "ffn_w_gate_up", "ffn_w_down", "ple_w_up", "ple_w_gate", "ple_b_gate",
                "ln_gain", "ln_bias"]


def _as3(a):
    return a.reshape((-1,) + a.shape[-2:]) if a.ndim >= 3 else a.reshape((1,) + a.shape)


def _pad_lanes(v):
    n = v.shape[-1]
    return jnp.pad(v, ((0, 0), (0, (-n) % LANES)))


ADAM_MANY_STEPS = 2
ADAM_MANY_MAX = 1 << 19


def _adam_many(groups, name):
    in_specs, out_specs, out_shapes, arrays = [], [], [], []
    for group in groups:
        r, c = group[0].shape
        block = pl.BlockSpec((r // ADAM_MANY_STEPS, c), lambda i: (i, 0))
        in_specs += [block] * len(group)
        arrays += list(group)
        out_specs += [block] * 4
        out_shapes += [jax.ShapeDtypeStruct((r, c), F32)] * 4

    def body(*refs):
        ins, outs = refs[:len(arrays)], refs[len(arrays):]
        for k in range(len(groups)):
            res = _adam_fn(*[ref[...] for ref in ins[5 * k:5 * k + 5]])
            for out_ref, val in zip(outs[4 * k:4 * k + 4], res):
                out_ref[...] = val

    result = pl.pallas_call(
        body, name=name, grid=(ADAM_MANY_STEPS,), in_specs=in_specs, out_specs=out_specs, out_shape=out_shapes,
        compiler_params=_params(("parallel",)),
    )(*arrays)
    return [result[4 * k:4 * k + 4] for k in range(len(groups))]


def _adam_small(everyone, chip, items, loss_off):
    n_items = len(items)

    def body(chip_ref, every_ref, *refs):
        ins, outs = refs[:3 * n_items], refs[3 * n_items:]

        def total(off, width):
            acc = every_ref[0, :, off:off + width]
            for s in range(1, N_DEV):
                acc = acc + every_ref[s, :, off:off + width]
            return acc

        for a, (w, _, _, off, sharded) in enumerate(items):
            cols = w.shape[-1]
            for r in range(w.size // cols):
                at = (slice(r, r + 1),) if w.ndim == 2 else (r // w.shape[1], slice(r % w.shape[1], r % w.shape[1] + 1))
                if sharded:
                    full = total(off + r * N_CHIPS * cols, N_CHIPS * cols)
                    g = full[:, 0:cols]
                    for c in range(1, N_CHIPS):
                        g = jnp.where(chip_ref[0] == c, full[:, c * cols:(c + 1) * cols], g)
                else:
                    g = total(off + r * cols, cols)
                w_ref, m_ref, v_ref = ins[3 * a:3 * a + 3]
                res = _adam_fn(w_ref[at], m_ref[at], v_ref[at], g, jnp.zeros_like(g))
                for out_ref, val in zip(outs[4 * a:4 * a + 4], res):
                    out_ref[at] = val
        outs[-1][...] = total(loss_off, LANES)

    def whole(shape):
        return pl.BlockSpec(tuple(shape), lambda i, chip_ref: (0,) * len(shape))

    arrays = [arr for it in items for arr in it[:3]]
    out_shapes = [jax.ShapeDtypeStruct(it[0].shape, F32) for it in items for _ in range(4)]
    out_shapes.append(jax.ShapeDtypeStruct((1, LANES), F32))
    result = pl.pallas_call(
        body, name="adam_small",
        grid_spec=pltpu.PrefetchScalarGridSpec(
            num_scalar_prefetch=1, grid=(1,),
            in_specs=[whole(everyone.shape)] + [whole(arr.shape) for arr in arrays],
            out_specs=[whole(s.shape) for s in out_shapes]),
        out_shape=out_shapes,
        compiler_params=_params(("arbitrary",)),
    )(chip, everyone, *arrays)
    return [result[4 * a:4 * a + 4] for a in range(n_items)], result[-1]


def kernel(x, p, a_w_in, a_lower_bound, a_norm_gain, a_w_out, kv_w, kv_b, b_w_q, b_b_q, b_sinks, b_w_out, b_b_out, ffn_w_gate_up, ffn_w_down, ple_w_up, ple_w_gate, ple_b_gate, ln_gain, ln_bias, loss_target, m_a_w_in, m_a_lower_bound, m_a_norm_gain, m_a_w_out, m_kv_w, m_kv_b, m_b_w_q, m_b_b_q, m_b_sinks, m_b_w_out, m_b_b_out, m_ffn_w_gate_up, m_ffn_w_down, m_ple_w_up, m_ple_w_gate, m_ple_b_gate, m_ln_gain, m_ln_bias, v_a_w_in, v_a_lower_bound, v_a_norm_gain, v_a_w_out, v_kv_w, v_kv_b, v_b_w_q, v_b_b_q, v_b_sinks, v_b_w_out, v_b_b_out, v_ffn_w_gate_up, v_ffn_w_down, v_ple_w_up, v_ple_w_gate, v_ple_b_gate, v_ln_gain, v_ln_bias):
    args = dict(locals())
    wts = {n: args[n] for n in WEIGHT_ORDER}
    mom = {n: args["m_" + n] for n in WEIGHT_ORDER}
    vel = {n: args["v_" + n] for n in WEIGHT_ORDER}
    chip = 2 * lax.axis_index("x") + lax.axis_index("y")
    d = x.shape[-1]
    dq = d // N_CHIPS

    kind_of = dict(BIG)
    kind_of["small"] = "col"
    chip_arr = chip.reshape(1).astype(jnp.int32)
    small_pack = jnp.concatenate([wts[n].reshape(-1, dq) for n in SMALL_SHARDED], axis=0)[None]

    def place_item(key):
        n, layer = key
        if n == "small":
            return small_pack, 0, "col", F32
        return _as3(wts[n]), layer, kind_of[n], BF16

    gathers, where = [], {}
    for gi, group in enumerate(GATHER_GROUPS):
        prev = gathers[-1].token if gathers else None
        placed = _place([place_item(k) for k in group], chip_arr, name=f"place{gi}", after=prev)
        gathers.append(_Exchange("gather", [], placed, [kind_of[k[0]] for k in group],
                                 [0] * len(group), f"gather{gi}", COLLECTIVE_IDS["gather"] + gi, after=prev,
                                 halves=[k[0] != "small" for k in group]))
        for k in group:
            where[k] = gi
    all_started = gathers[-1].token
    ready = {}

    fills = {}

    def pass_on(gi, after):
        if gi not in fills:
            group = GATHER_GROUPS[gi]
            outs = gathers[gi].wait(after)[1]
            split = [i for i, k in enumerate(group) if k[0] != "small"]
            fills[gi] = (outs, split, _SiblingFill([outs[i] for i in split], [kind_of[group[i][0]] for i in split],
                                                   f"fill{gi}", COLLECTIVE_IDS["fill"] + gi))

    def wget(name, layer, after):
        key = (name, layer)
        if key not in ready:
            gi = where[key]
            after = all_started if gi == 0 else after
            pass_on(gi, after)
            if 1 <= gi < len(GATHER_GROUPS) - 1:
                pass_on(gi + 1, after)
                after = fills[gi + 1][2].token
            outs, split, fill = fills[gi]
            for i, arr in zip(split, fill.wait(after)):
                outs[i] = arr
            for k, arr in zip(GATHER_GROUPS[gi], outs):
                ready[k] = arr
        return ready[key]

    small_full = wget("small", 0, None)[0]
    ln_gain_f = small_full[0:6].reshape(DEPTH, 3, d)
    ln_bias_f = small_full[6:12].reshape(DEPTH, 3, d)
    alb_f = small_full[12:14]

    group_of = {k: gi for gi, group in enumerate(SCATTER_GROUPS) for k in group}
    grads_done, zones, scatters = {}, {}, []

    def grad_sink(name, layer, grad):
        grads_done[(name, layer)] = grad
        if name not in zones:
            zones[name] = lax.empty((N_CHIPS,) + _as3(wts[name]).shape, BF16)
        gi = group_of[(name, layer)]
        group = SCATTER_GROUPS[gi]
        if not all(k in grads_done for k in group):
            return None
        ex = _Exchange("scatter", [grads_done[k] for k in group], [zones[k[0]] for k in group],
                       [kind_of[k[0]] for k in group], [k[1] for k in group], f"scatter{gi}",
                       COLLECTIVE_IDS["scatter"] + gi)
        for k, zone in zip(group, ex.lands):
            zones[k[0]] = zone
        scatters.append((ex, group))
        return ex.token

    small = {}

    def small_sink(loss, gs):
        ln_g = jnp.concatenate([gs[f"ln_gain_{i}_{j}"] for i in range(DEPTH) for j in range(3)], axis=0)
        ln_b = jnp.concatenate([gs[f"ln_bias_{i}_{j}"] for i in range(DEPTH) for j in range(3)], axis=0)
        ple_bg = jnp.concatenate([gs[f"ple_b_{i}"] for i in range(DEPTH)], axis=0)
        small["list"] = [ln_g.reshape(1, -1), ln_b.reshape(1, -1), gs["alb"].reshape(1, -1), gs["norm_gain"],
                         gs["kv_b"], gs["b_q"], _pad_lanes(gs["sinks"]), gs["b_out"], ple_bg.reshape(1, -1), loss]
        small["gather"] = _GatherDevices(jnp.concatenate(small["list"], axis=1))
        return small["gather"].token

    loss, grad_x, gs = _local_step(
        x[0], p.reshape((p.shape[0],) + p.shape[2:]), loss_target[0], wget, grad_sink, ln_gain_f, ln_bias_f, alb_f, a_norm_gain, kv_b, b_b_q,
        b_sinks, b_b_out, ple_b_gate, small_sink)

    res = {}

    def arrive(batch, after):
        for ex, group in batch:
            srcs, outs = ex.wait(after, lands=[zones[k[0]] for k in group])
            for k, grad, zone in zip(group, srcs, outs):
                grads_done[k], zones[k[0]] = grad, zone

    def half_sums(names, batch, after):
        partial = []
        for n in names:
            own = [grads_done[(n, layer)] for layer in range(zones[n].shape[1])]
            partial.append(_sum_arrivals(zones[n], own, kind_of[n], chip_arr, f"sum_{n}", after=after))
        return _SiblingSwap(partial, f"swap{batch}", COLLECTIVE_IDS["swap"] + batch, after=after)

    def update(names, swap, after):
        flat = lambda a: a.reshape(-1, a.shape[-1])
        work = [(n, [flat(wts[n]), flat(mom[n]), flat(vel[n]), own, sib]) for n, own, sib in zip(names, *swap.wait(after))]
        many = [(n, ops) for n, ops in work if wts[n].size <= ADAM_MANY_MAX]
        if len(many) > 1:
            for (n, _), out in zip(many, _adam_many([ops for _, ops in many], f"adam_from_{many[0][0]}")):
                res[n] = [o.reshape(wts[n].shape) for o in out]
        for n, ops in work:
            if n not in res:
                out = _rowwise(_adam_fn, ops, [], [(ops[3].shape, F32)] * 4, name=f"adam_{n}")
                res[n] = [o.reshape(wts[n].shape) for o in out]
        return res[names[-1]][1]

    last_names = [k[0] for k in SCATTER_GROUPS[-1]]
    batches = [["ffn_w_gate_up"], [n for n, _ in BIG if n != "ffn_w_gate_up" and n not in last_names], last_names]
    arrive(scatters[:-1], grad_x)
    swap0 = half_sums(batches[0], 0, None)
    swap1 = half_sums(batches[1], 1, swap0.token)
    updated = update(batches[0], swap0, swap1.token)
    arrive(scatters[-1:], updated)
    swap2 = half_sums(batches[2], 2, swap1.token)
    updated = update(batches[1], swap1, swap2.token)
    update(batches[2], swap2, updated)

    everyone = small["gather"].wait(grad_x)
    offs, pos = [], 0
    for v in small["list"]:
        offs.append(pos)
        pos += v.shape[1]
    names = ["ln_gain", "ln_bias", "a_lower_bound", "a_norm_gain", "kv_b", "b_b_q", "b_sinks", "b_b_out", "ple_b_gate"]
    as_rows = lambda a: a.reshape(1, -1) if a.ndim == 1 else a
    items = [(as_rows(wts[n]), as_rows(mom[n]), as_rows(vel[n]), off, n in SMALL_SHARDED)
             for n, off in zip(names, offs)]
    updates, loss_row = _adam_small(everyone, chip_arr, items, offs[len(names)])
    for n, upd in zip(names, updates):
        res[n] = [u.reshape(wts[n].shape) for u in upd]

    outs = [loss_row[0, 0], grad_x[None]]
    for k in range(4):
        outs += [res[n][k] for n in WEIGHT_ORDER]
    return tuple(outs)
```

```python
import functools

import jax
import jax.numpy as jnp
from jax import lax
from jax.experimental import pallas as pl
from jax.experimental.pallas import tpu as pltpu

F32 = jnp.float32
BF16 = jnp.bfloat16
MESH = pl.DeviceIdType.MESH

LANES = 128
HG_DK = 128
HG_CHUNK = 64
HG_SUB = 16
HG_ROWS = 512
HG_HEADS_PER_STEP = 2
LOG2_E = 1.4426950408889634
ATT_HD = 64
ATT_G = 4
WINDOW = 128
DEPTH = 2
ALPHA = (2.0 * DEPTH) ** 0.25
LN_EPS = 1e-5
RMS_EPS = 1e-6
ADAM_LR, ADAM_B1, ADAM_B2, ADAM_EPS, ADAM_WD, ADAM_STEP = 0.001, 0.9, 0.999, 1e-08, 0.01, 10
N_CHIPS = 4
N_DEV = 8
VMEM_LIMIT = 56 * 1024 * 1024
NEG = -1e30


def _pick(n, cap):
    best = None
    for d in range(LANES, min(n, cap) + 1, LANES):
        if n % d == 0:
            best = d
    return n if best is None else best


def _pick_rows(m, cap):
    best = None
    for d in range(16, min(m, cap) + 1, 16):
        if m % d == 0:
            best = d
    return m if best is None else best


def _params(sem):
    return pltpu.CompilerParams(dimension_semantics=sem, vmem_limit_bytes=VMEM_LIMIT)


def _zeros_index(ndim, grid_rank=3):
    return (lambda i, j, kk: (0,) * ndim) if grid_rank == 3 else (lambda kk, i: (0,) * ndim)


def _mm(a, b, *, name, la=None, lb=None, ta=False, tb=False, bias=None, add=None, out_dtype=F32,
        out_layers=None, out_layer=None, after=None, post=None, tile_cols=None, caps=(1024, 1536, 2048),
        a_parts=None, b_parts=None):
    ar, ac = a.shape[-2:]
    br, bc = b.shape[-2:]
    assert a_parts is None or (not ta and la is None and a.shape[0] == a_parts)
    assert b_parts is None or (not tb and lb is None and b.shape[0] == b_parts)
    m, k = (ac, ar) if ta else (ar, ac * (a_parts or 1))
    k2, n = (bc, br) if tb else (br, bc * (b_parts or 1))
    assert k == k2, (a.shape, b.shape, ta, tb)
    if post is not None:
        caps = (512, n if tile_cols is None else tile_cols, caps[2])
    tm, tn, tk = _pick(m, caps[0]), _pick(bc if b_parts else n, caps[1]), _pick(ac if a_parts else k, caps[2])
    assert post is None or tn == caps[1]
    nk = k // tk
    gi, gj = m // tm, n // tn
    a_bytes, b_bytes = m * k * a.dtype.itemsize, k * n * b.dtype.itemsize
    rows_outer = (a_bytes + b_bytes * (gi if gj * nk > 1 else 1)) <= (b_bytes + a_bytes * (gj if gi * nk > 1 else 1))
    k_outer = post is not None and nk > 1 and gj == 1
    grid = (nk, gi) if k_outer else (gi, gj, nk) if rows_outer else (gj, gi, nk)
    keep_at = ta and nk == 1 and gj > 1 and rows_outer

    def bs(block, idx, late=False):
        if k_outer:
            return pl.BlockSpec(block, lambda kk, i: idx(jnp.where(kk == nk - 1, i, 0) if late else i, 0, kk))
        return pl.BlockSpec(block, idx if rows_outer else (lambda q, p, kk: idx(p, q, kk)))

    def spec(block, idx, layer):
        if layer is None:
            return bs(block, idx)
        return bs((None,) + block, lambda i, j, kk: (layer,) + idx(i, j, kk))

    a_spec = spec((tk, tm), lambda i, j, kk: (kk, i), la) if ta else spec((tm, tk), lambda i, j, kk: (i, kk), la)
    b_spec = spec((tn, tk), lambda i, j, kk: (j, kk), lb) if tb else spec((tk, tn), lambda i, j, kk: (kk, j), lb)
    if a_parts:
        a_spec = bs((None, tm, tk), lambda i, j, kk: (kk // (ac // tk), i, kk % (ac // tk)))
    if b_parts:
        b_spec = bs((None, tk, tn), lambda i, j, kk: (j // (bc // tn), kk, j % (bc // tn)))
    in_specs, operands = [a_spec, b_spec], [a, b]
    if bias is not None:
        in_specs.append(bs((1, tn), lambda i, j, kk: (0, j)))
        operands.append(bias)
    if add is not None:
        in_specs.append(bs((tm, tn), lambda i, j, kk: (i, j), late=True))
        operands.append(add)
    if after is not None:
        in_specs.append(pl.BlockSpec(memory_space=pl.ANY))
        operands.append(after)
    dims = (((0 if ta else 1,), (1 if tb else 0,)), ((), ()))
    has_bias, has_add = bias is not None, add is not None
    if post is None:
        fn, rows, whole, outs, sums = None, [], [], [], []
        out_shape = jax.ShapeDtypeStruct((m, n) if out_layers is None else (out_layers, m, n), out_dtype)
        out_specs = spec((tm, tn), lambda i, j, kk: (i, j), out_layer)
    else:
        fn, rows, whole, outs, sums = post
        in_specs += [bs((tm, r.shape[-1] // gj), lambda i, j, kk: (i, j), late=True) for r in rows]
        in_specs += [pl.BlockSpec(tuple(w.shape), _zeros_index(w.ndim, len(grid))) for w in whole]
        operands += list(rows) + list(whole)
        out_shape = [jax.ShapeDtypeStruct(sh, dt) for sh, dt in list(outs) + list(sums)]
        out_specs = ([bs((tm, sh[-1] // gj), lambda i, j, kk: (i, j), late=True) for sh, _ in outs]
                     + [pl.BlockSpec(tuple(sh), _zeros_index(len(sh), len(grid))) for sh, _ in sums])
    n_in, n_extra, n_outs, n_sums = len(operands), len(rows) + len(whole), len(outs), len(sums)

    def body(*refs):
        a_ref, b_ref = refs[0], refs[1]
        pos = 2
        bias_ref = add_ref = None
        if has_bias:
            bias_ref = refs[pos]
            pos += 1
        if has_add:
            add_ref = refs[pos]
            pos += 1
        extra_refs = refs[n_in - n_extra:n_in]
        out_refs = refs[n_in:n_in + max(n_outs, 1)]
        sum_refs = refs[n_in + n_outs:n_in + n_outs + n_sums]
        acc_ref = refs[-1] if nk > 1 else None
        if keep_at:
            at_ref = refs[-1]

            @pl.when(pl.program_id(1) == 0)
            def _():
                at_ref[...] = a_ref[...].astype(BF16).T

            part = lax.dot_general(at_ref[...], b_ref[...].astype(BF16), (((1,), (1 if tb else 0,)), ((), ())),
                                   preferred_element_type=F32)
        else:
            part = lax.dot_general(a_ref[...].astype(BF16), b_ref[...].astype(BF16), dims,
                                   preferred_element_type=F32)

        def finish(total):
            if has_bias:
                total = total + bias_ref[...]
            if has_add:
                total = total + add_ref[...]
            if fn is None:
                out_refs[0][...] = total.astype(out_refs[0].dtype)
                return
            res = fn(total, *[r[...] for r in extra_refs])
            for ref, val in zip(out_refs, res[:n_outs]):
                ref[...] = val.astype(ref.dtype)
            if n_sums:
                @pl.when(pl.program_id(1 if k_outer or not rows_outer else 0) == 0)
                def _():
                    for ref in sum_refs:
                        ref[...] = jnp.zeros(ref.shape, ref.dtype)

                for ref, val in zip(sum_refs, res[n_outs:]):
                    ref[...] += val

        if nk == 1:
            finish(part)
        elif k_outer:
            kk = pl.program_id(0)
            rows_i = pl.ds(pl.multiple_of(pl.program_id(1) * tm, tm), tm)

            @pl.when(kk == 0)
            def _():
                acc_ref[rows_i, :] = part

            @pl.when(kk > 0)
            def _():
                acc_ref[rows_i, :] += part

            @pl.when(kk == nk - 1)
            def _():
                finish(acc_ref[rows_i, :])
        else:
            kk = pl.program_id(2)

            @pl.when(kk == 0)
            def _():
                acc_ref[...] = part

            @pl.when(kk > 0)
            def _():
                acc_ref[...] += part

            @pl.when(kk == nk - 1)
            def _():
                finish(acc_ref[...])

    return pl.pallas_call(
        body, name=name, grid=grid, in_specs=in_specs, out_specs=out_specs, out_shape=out_shape,
        scratch_shapes=([pltpu.VMEM((m, n) if k_outer else (tm, tn), F32)] if nk > 1
                        else [pltpu.VMEM((tm, tk), BF16)] if keep_at else []),
        compiler_params=_params(("arbitrary", "arbitrary") if k_outer
                                else ("arbitrary" if n_sums else "parallel", "arbitrary" if keep_at else "parallel",
                                      "arbitrary") if rows_outer
                                else ("parallel", "arbitrary" if n_sums else "parallel", "arbitrary")),
    )(*operands)


def _rowwise(fn, rows, whole, outs, sums=(), *, name, tm=256):
    m = rows[0].shape[-2]
    tm = _pick_rows(m, tm)
    n_rows, n_whole, n_outs, n_sums = len(rows), len(whole), len(outs), len(sums)

    def rspec(shape):
        lead = len(shape) - 2
        return pl.BlockSpec(tuple(shape[:-2]) + (tm, shape[-1]), lambda i: (0,) * lead + (i, 0))

    def wspec(shape):
        return pl.BlockSpec(tuple(shape), lambda i: (0,) * len(shape))

    def body(*refs):
        vals = [r[...] for r in refs[:n_rows + n_whole]]
        out_refs = refs[n_rows + n_whole:n_rows + n_whole + n_outs]
        sum_refs = refs[n_rows + n_whole + n_outs:]
        res = fn(*vals)
        for ref, val in zip(out_refs, res[:n_outs]):
            ref[...] = val.astype(ref.dtype)
        if n_sums:
            @pl.when(pl.program_id(0) == 0)
            def _():
                for ref in sum_refs:
                    ref[...] = jnp.zeros(ref.shape, ref.dtype)

            for ref, val in zip(sum_refs, res[n_outs:]):
                ref[...] += val

    result = pl.pallas_call(
        body, name=name, grid=(m // tm,),
        in_specs=[rspec(r.shape) for r in rows] + [wspec(w.shape) for w in whole],
        out_specs=[rspec(s) for s, _ in outs] + [wspec(s) for s, _ in sums],
        out_shape=[jax.ShapeDtypeStruct(s, d) for s, d in list(outs) + list(sums)],
        compiler_params=_params(("arbitrary",)),
    )(*rows, *whole)
    return result


def _sigmoid(v):
    return jax.nn.sigmoid(v)


def _col_sum(v):
    return jnp.sum(v, axis=0, keepdims=True)


def _ln_stats(z):
    mu = jnp.mean(z, axis=-1, keepdims=True)
    zc = z - mu
    var = jnp.mean(zc * zc, axis=-1, keepdims=True)
    rstd = lax.rsqrt(var + LN_EPS)
    return zc * rstd, rstd


def _ln_fwd_fn(xin, h, gain, bias):
    xhat, _ = _ln_stats(ALPHA * xin + h)
    y = xhat * gain + bias
    return y, y


def _ple_ln_fwd_fn(xin, pg, pu, gain, bias):
    xhat, _ = _ln_stats(ALPHA * xin + _sigmoid(pg) * pu)
    y = xhat * gain + bias
    return y, y


def _ln_dz(dy, z, gain):
    xhat, rstd = _ln_stats(z)
    dxhat = dy * gain
    dz = rstd * (dxhat - jnp.mean(dxhat, axis=-1, keepdims=True)
                 - xhat * jnp.mean(dxhat * xhat, axis=-1, keepdims=True))
    return dz, _col_sum(dy * xhat), _col_sum(dy)


def _ln_bwd_fn(dy, xin, h, gain):
    dz, dgain, dbias = _ln_dz(dy, ALPHA * xin + h, gain)
    return ALPHA * dz, dz, dgain, dbias, _col_sum(dz)


def _ple_ln_bwd_fn(dy, xin, pg, pu, gain):
    sg = _sigmoid(pg)
    dz, dgain, dbias = _ln_dz(dy, ALPHA * xin + sg * pu, gain)
    dpg = dz * pu * sg * (1.0 - sg)
    return ALPHA * dz, dpg, dz * sg, dgain, dbias, _col_sum(dpg)


def _swiglu_fwd_fn(gu):
    hid = gu.shape[-1] // 2
    gate, up = gu[:, :hid], gu[:, hid:]
    return gu, gate * _sigmoid(gate) * up


def _swiglu_bwd_fn(dact, gu):
    gu = gu.astype(F32)
    hid = gu.shape[-1] // 2
    gate, up = gu[:, :hid], gu[:, hid:]
    sg = _sigmoid(gate)
    dgate = dact * up * sg * (1.0 + gate * (1.0 - sg))
    dup = dact * gate * sg
    return (jnp.concatenate([dgate, dup], axis=-1),)


def _loss_fn(y, target):
    err = y - target
    inv = 1.0 / y.shape[-1]
    part = 0.5 * inv * jnp.sum(jnp.sum(err * err, axis=-1, keepdims=True), axis=0, keepdims=True)
    return err * inv, jnp.broadcast_to(part, (1, LANES))


def _adam_fn(w, mom, vel, p_own, p_sib):
    g = p_own.astype(F32) + p_sib.astype(F32)
    m_new = ADAM_B1 * mom + (1.0 - ADAM_B1) * g
    v_new = ADAM_B2 * vel + (1.0 - ADAM_B2) * (g * g)
    m_hat = m_new / (1.0 - ADAM_B1 ** ADAM_STEP)
    v_hat = v_new / (1.0 - ADAM_B2 ** ADAM_STEP)
    delta = -ADAM_LR * (m_hat / (jnp.sqrt(v_hat) + ADAM_EPS) + ADAM_WD * w)
    return g, delta, m_new, v_new


def _split2(x):
    hi = x.astype(BF16)
    return hi, (x - hi.astype(F32)).astype(BF16)


def _dot3(a, b, dims):
    a_hi, a_lo = _split2(a)
    b_hi, b_lo = _split2(b)
    dn = (dims, ((), ()))
    return (lax.dot_general(a_hi, b_hi, dn, preferred_element_type=F32)
            + (lax.dot_general(a_hi, b_lo, dn, preferred_element_type=F32)
               + lax.dot_general(a_lo, b_hi, dn, preferred_element_type=F32)))


def _tdot(mask01, b):
    m = mask01.astype(BF16)
    b_hi = b.astype(BF16)
    rest = b - b_hi.astype(F32)
    b_mid = rest.astype(BF16)
    b_lo = (rest - b_mid.astype(F32)).astype(BF16)
    dn = (((1,), (0,)), ((), ()))
    return (lax.dot_general(m, b_hi, dn, preferred_element_type=F32)
            + (lax.dot_general(m, b_mid, dn, preferred_element_type=F32)
               + lax.dot_general(m, b_lo, dn, preferred_element_type=F32)))


def _hdot(a, b):
    return _dot3(a, b, ((1,), (0,)))


def _hdot_nt(a, b):
    return _dot3(a, b, ((1,), (1,)))


def _hdot_tn(a, b):
    return _dot3(a, b, ((0,), (0,)))


def _dot(a, b):
    return lax.dot_general(a.astype(BF16), b.astype(BF16), (((1,), (0,)), ((), ())), preferred_element_type=F32)


def _dot_nt(a, b):
    return lax.dot_general(a.astype(BF16), b.astype(BF16), (((1,), (1,)), ((), ())), preferred_element_type=F32)


def _dot_tn(a, b):
    return lax.dot_general(a.astype(BF16), b.astype(BF16), (((0,), (0,)), ((), ())), preferred_element_type=F32)


def _hg_masks():
    c = HG_CHUNK
    row = lax.broadcasted_iota(jnp.int32, (c, c), 0)
    col = lax.broadcasted_iota(jnp.int32, (c, c), 1)
    base = row & (-HG_SUB)
    return row, col, base, col <= row, col < base


def _hg_gates(qr, fr, alb):
    lbound = _sigmoid(alb[0:1, :] - alb[1:2, :])
    sig = _sigmoid(fr)
    forget = lbound + (1.0 - lbound) * sig
    kk = (1.0 - lbound) * _sigmoid(-fr)
    qt = qr * _sigmoid(qr) * (HG_DK ** -0.5)
    return qt, kk, jnp.log(forget), lbound, sig, forget


def _hg_scores(qt, kk, g, scores=True):
    c, nsub = HG_CHUNK, HG_CHUNK // HG_SUB
    row, col, base, causal, below = _hg_masks()
    b = _tdot(causal, g)
    rr = _tdot(below, g)
    bq = b - rr
    qh = qt * jnp.exp(bq)
    edecs = [None]
    parts = [jnp.zeros((HG_SUB, c), F32)]
    for i in range(1, nsub):
        edec = jnp.exp(jnp.minimum(rr[i * HG_SUB:i * HG_SUB + 1, :] - b, 0.0))
        edecs.append(edec)
        if scores:
            parts.append(_dot_nt(qh[i * HG_SUB:(i + 1) * HG_SUB, :], kk * edec))
    q3 = qt.reshape(nsub, HG_SUB, HG_DK)
    if not scores:
        return None, b, bq, qh, edecs, (b.reshape(nsub, HG_SUB, HG_DK), q3, kk.reshape(nsub, HG_SUB, HG_DK))
    a = jnp.where(below, jnp.concatenate(parts, axis=0), 0.0)
    b2 = b * LOG2_E
    b3 = b2.reshape(nsub, HG_SUB, HG_DK)
    c3 = (b2 - jnp.log2(kk)).reshape(nsub, HG_SUB, HG_DK)
    for j in range(HG_SUB):
        ek = jnp.exp2(b3 - c3[:, j:j + 1, :])
        colv = jnp.sum(q3 * ek, axis=-1, keepdims=True).reshape(c, 1)
        a = jnp.where(col == base + j, colv, a)
    a = jnp.where(causal, a, 0.0)
    return a, b, bq, qh, edecs, None


def _hg_norm(o, gr, gain):
    r = lax.rsqrt(jnp.mean(o * o, axis=-1, keepdims=True) + RMS_EPS)
    sg = _sigmoid(gr)
    return o * r * gain, r, sg


def _hgrn2_fwd(proj, alb, gain, *, rb):
    m, d4 = proj.shape
    d = d4 // 4
    heads = d // HG_DK
    hp = HG_HEADS_PER_STEP
    rb = min(rb, m)
    cpb = rb // HG_CHUNK
    nrb = m // rb

    def body(q_ref, f_ref, v_ref, g_ref, alb_ref, gain_ref, o_ref, og_ref, st_ref, a_ref, state):
        @pl.when(pl.program_id(1) == 0)
        def _():
            state[...] = jnp.zeros(state.shape, F32)

        def chunk(ci, carry):
            sl = pl.ds(pl.multiple_of(ci * HG_CHUNK, HG_CHUNK), HG_CHUNK)
            for u in range(hp):
                ln = slice(u * HG_DK, (u + 1) * HG_DK)
                qt, kk, g, _, _, _ = _hg_gates(q_ref[sl, ln], f_ref[sl, ln], alb_ref[:, ln])
                v = v_ref[sl, ln]
                st = state[u]
                st_ref[u, ci] = st
                a, b, _, _, _, _ = _hg_scores(qt, kk, g)
                a_ref[u, ci] = a.astype(a_ref.dtype)
                o = _dot(a, v) + _dot_nt(qt * jnp.exp(b), st)
                b_last = b[HG_CHUNK - 1:HG_CHUNK, :]
                state[u] = st * jnp.exp(b_last) + _hdot_tn(v, kk * jnp.exp(b_last - b))
                o_ref[sl, ln] = o
                n, _, sg = _hg_norm(o, g_ref[sl, ln], gain_ref[...])
                og_ref[sl, ln] = (n * g_ref[sl, ln] * sg).astype(og_ref.dtype)
            return carry

        lax.fori_loop(0, cpb, chunk, 0)

    def col(cidx):
        return pl.BlockSpec((rb, hp * HG_DK), lambda h, r: (r, cidx * (heads // hp) + h))

    return pl.pallas_call(
        body, name="hgrn2_fwd", grid=(heads // hp, nrb),
        in_specs=[col(0), col(1), col(2), col(3),
                  pl.BlockSpec((2, hp * HG_DK), lambda h, r: (0, h)),
                  pl.BlockSpec((1, HG_DK), lambda h, r: (0, 0))],
        out_specs=[pl.BlockSpec((rb, hp * HG_DK), lambda h, r: (r, h)),
                   pl.BlockSpec((rb, hp * HG_DK), lambda h, r: (r, h)),
                   pl.BlockSpec((hp, cpb, HG_DK, HG_DK), lambda h, r: (h, r, 0, 0)),
                   pl.BlockSpec((hp, cpb, HG_CHUNK, HG_CHUNK), lambda h, r: (h, r, 0, 0))],
        out_shape=[jax.ShapeDtypeStruct((m, d), F32), jax.ShapeDtypeStruct((m, d), BF16),
                   jax.ShapeDtypeStruct((heads, m // HG_CHUNK, HG_DK, HG_DK), F32),
                   jax.ShapeDtypeStruct((heads, m // HG_CHUNK, HG_CHUNK, HG_CHUNK), BF16)],
        scratch_shapes=[pltpu.VMEM((hp, HG_DK, HG_DK), F32)],
        compiler_params=_params(("parallel", "arbitrary")),
    )(proj, proj, proj, proj, alb, gain)


def _hgrn2_bwd(proj, o_pre, states, scores, dog, alb, gain, *, rb):
    m, d4 = proj.shape
    d = d4 // 4
    heads = d // HG_DK
    rb = min(rb, m)
    cpb = rb // HG_CHUNK
    nrb = m // rb
    c, nsub = HG_CHUNK, HG_CHUNK // HG_SUB

    def body(q_ref, f_ref, v_ref, g_ref, o_ref, st_ref, a_ref, dog_ref, alb_ref, gain_ref,
             dp_ref, dalb_ref, dgain_ref, dstate, carry_ref):
        first = (pl.program_id(0) == 0) & (pl.program_id(1) == 0)

        @pl.when(first)
        def _():
            dgain_ref[...] = jnp.zeros(dgain_ref.shape, F32)

        @pl.when(pl.program_id(1) == 0)
        def _():
            dstate[...] = jnp.zeros(dstate.shape, F32)
            carry_ref[...] = jnp.zeros(carry_ref.shape, F32)
            dalb_ref[...] = jnp.zeros(dalb_ref.shape, F32)

        row, col, base, causal, below = _hg_masks()
        sub_iota = lax.broadcasted_iota(jnp.int32, (nsub, HG_SUB, HG_DK), 1)
        row_k = lax.broadcasted_iota(jnp.int32, (c, HG_DK), 0)
        upper = col >= row

        def chunk(step, carry):
            ci = cpb - 1 - step
            sl = pl.ds(pl.multiple_of(ci * HG_CHUNK, HG_CHUNK), HG_CHUNK)
            qr, fr, v, gr = q_ref[sl, :], f_ref[sl, :], v_ref[sl, :], g_ref[sl, :]
            qt, kk, g, lbound, sig, forget = _hg_gates(qr, fr, alb_ref[...])
            o = o_ref[sl, :]
            dogv = dog_ref[sl, :]
            gain_v = gain_ref[...]
            n, r, sg = _hg_norm(o, gr, gain_v)
            dgr = dogv * n * sg * (1.0 + gr * (1.0 - sg))
            dn = dogv * gr * sg
            dgain_ref[...] += _col_sum(dn * o * r)
            u = dn * gain_v
            d_o = r * u - o * (r * r * r) * jnp.mean(u * o, axis=-1, keepdims=True)
            st0 = st_ref[ci]
            dst = dstate[...]
            _, b, bq, qh, edecs, (b3, q3, k3) = _hg_scores(qt, kk, g, scores=False)
            a = a_ref[ci]
            eb = jnp.exp(b)
            b_last = b[c - 1:c, :]
            kdl_dec = jnp.exp(b_last - b)
            kdl = kk * kdl_dec
            d_a = jnp.where(causal, _dot_nt(d_o, v), 0.0)
            d_at = _dot_nt(v, d_o)
            dv = _dot_tn(a, d_o) + _dot_nt(kdl, dst)
            dq = eb * _hdot(d_o, st0)
            dk = _hdot(v, dst) * kdl_dec
            d_a_below = jnp.where(below, d_a, 0.0)
            dq_parts = [jnp.zeros((HG_SUB, HG_DK), F32)]
            for i in range(1, nsub):
                lo, hi = i * HG_SUB, (i + 1) * HG_SUB
                dq_parts.append(_hdot(d_a_below[lo:hi, :], kk * edecs[i]))
                gi = _hdot(d_at[:, lo:hi], qh[lo:hi, :])
                dk = dk + jnp.where(row_k < lo, edecs[i] * gi, 0.0)
            dq = dq + jnp.concatenate(dq_parts, axis=0) * jnp.exp(bq)
            d_diag = jnp.concatenate([d_a[i * HG_SUB:(i + 1) * HG_SUB, i * HG_SUB:(i + 1) * HG_SUB]
                                      for i in range(nsub)], axis=0).reshape(nsub, HG_SUB, HG_SUB)
            dq3_halves, dk3_halves = [], []
            for half in (slice(0, nsub // 2), slice(nsub // 2, nsub)):
                b3h, q3h, k3h, ddh, iota_h = b3[half], q3[half], k3[half], d_diag[half], sub_iota[half]
                dq3 = jnp.zeros(b3h.shape, F32)
                dk3 = jnp.zeros(b3h.shape, F32)
                for j in range(HG_SUB):
                    e = jnp.exp(jnp.minimum(b3h - b3h[:, j:j + 1, :], 0.0))
                    t1 = ddh[:, :, j:j + 1] * e
                    dq3 = dq3 + t1 * k3h[:, j:j + 1, :]
                    dk3 = jnp.where(iota_h == j, jnp.sum(t1 * q3h, axis=1, keepdims=True), dk3)
                dq3_halves.append(dq3)
                dk3_halves.append(dk3)
            dq = dq + jnp.concatenate(dq3_halves, axis=0).reshape(c, HG_DK)
            dk = dk + jnp.concatenate(dk3_halves, axis=0).reshape(c, HG_DK)
            dstate[...] = dst * jnp.exp(b_last) + _hdot_tn(d_o, qt * eb)
            dglog = _tdot(upper, qt * dq - kk * dk) + carry_ref[...]
            carry_ref[...] = dglog[0:1, :]
            dforget = dglog / forget
            one_m_lb = 1.0 - lbound
            dsig = (dforget - dk) * one_m_lb
            sneg = _sigmoid(-fr)
            dlb = _col_sum(dforget * (1.0 - sig) - dk * sneg)
            dalb0 = dlb * lbound * one_m_lb
            dalb_ref[...] += jnp.concatenate([dalb0, -dalb0], axis=0)
            sq = _sigmoid(qr)
            dp_ref[0, sl, :] = (dq * (HG_DK ** -0.5) * sq * (1.0 + qr * (1.0 - sq))).astype(dp_ref.dtype)
            dp_ref[1, sl, :] = (dsig * sig * (1.0 - sig)).astype(dp_ref.dtype)
            dp_ref[2, sl, :] = dv.astype(dp_ref.dtype)
            dp_ref[3, sl, :] = dgr.astype(dp_ref.dtype)
            return carry

        lax.fori_loop(0, cpb, chunk, 0, unroll=2)

    def rev(r):
        return nrb - 1 - r

    def col(cidx):
        return pl.BlockSpec((rb, HG_DK), lambda h, r: (rev(r), cidx * heads + h))

    def head_rows():
        return pl.BlockSpec((rb, HG_DK), lambda h, r: (rev(r), h))

    return pl.pallas_call(
        body, name="hgrn2_bwd", grid=(heads, nrb),
        in_specs=[col(0), col(1), col(2), col(3), head_rows(),
                  pl.BlockSpec((None, cpb, HG_DK, HG_DK), lambda h, r: (h, rev(r), 0, 0)),
                  pl.BlockSpec((None, cpb, HG_CHUNK, HG_CHUNK), lambda h, r: (h, rev(r), 0, 0)),
                  head_rows(),
                  pl.BlockSpec((2, HG_DK), lambda h, r: (0, h)),
                  pl.BlockSpec((1, HG_DK), lambda h, r: (0, 0))],
        out_specs=[pl.BlockSpec((4, rb, HG_DK), lambda h, r: (0, rev(r), h)),
                   pl.BlockSpec((2, HG_DK), lambda h, r: (0, h)),
                   pl.BlockSpec((1, HG_DK), lambda h, r: (0, 0))],
        out_shape=[jax.ShapeDtypeStruct((4, m, d), BF16), jax.ShapeDtypeStruct((2, d), F32),
                   jax.ShapeDtypeStruct((1, HG_DK), F32)],
        scratch_shapes=[pltpu.VMEM((HG_DK, HG_DK), F32), pltpu.VMEM((1, HG_DK), F32)],
        compiler_params=_params(("arbitrary", "arbitrary")),
    )(proj, proj, proj, proj, o_pre, states, scores, dog, alb, gain)


def _swa_probs(qh, kp, kc, sink, slope, has_prev, lse=None):
    rows = qh.shape[0]
    qi = lax.broadcasted_iota(jnp.int32, (rows, WINDOW), 0) & (WINDOW - 1)
    si = lax.broadcasted_iota(jnp.int32, (rows, WINDOW), 1)
    scale = ATT_HD ** -0.5
    dist_c = (qi - si).astype(F32)
    s_p = _dot_nt(qh, kp) * scale - slope * (dist_c + float(WINDOW))
    s_c = _dot_nt(qh, kc) * scale - slope * dist_c
    s_p = jnp.where((si > qi) & has_prev, s_p, NEG)
    s_c = jnp.where(si <= qi, s_c, NEG)
    if lse is not None:
        return jnp.exp(s_p - lse), jnp.exp(s_c - lse), jnp.exp(sink - lse), lse
    mx = jnp.maximum(jnp.maximum(jnp.max(s_p, axis=-1, keepdims=True), jnp.max(s_c, axis=-1, keepdims=True)), sink)
    e_p, e_c, e_s = jnp.exp(s_p - mx), jnp.exp(s_c - mx), jnp.exp(sink - mx)
    total = jnp.sum(e_p, axis=-1, keepdims=True) + jnp.sum(e_c, axis=-1, keepdims=True) + e_s
    inv = 1.0 / total
    return e_p * inv, e_c * inv, e_s * inv, mx + jnp.log(total)


def _slope(h, n_heads):
    return float(2.0 ** (-8.0 * (h + 1) / n_heads))


def _swa_group(ref_vals, sink_ref, kh, n_heads):
    heads = [kh * ATT_G + g for g in range(ATT_G)]
    stacked = [jnp.concatenate([v[:, h * ATT_HD:(h + 1) * ATT_HD] for h in heads], axis=0) for v in ref_vals]
    grp = lax.shift_right_logical(lax.broadcasted_iota(jnp.int32, (ATT_G * WINDOW, 1), 0), WINDOW.bit_length() - 1)
    slope = jnp.zeros((ATT_G * WINDOW, 1), F32)
    sink = jnp.zeros((ATT_G * WINDOW, 1), F32)
    for g, h in enumerate(heads):
        slope = jnp.where(grp == g, _slope(h, n_heads), slope)
        sink = jnp.where(grp == g, sink_ref[:, h:h + 1], sink)
    return stacked, slope, sink


def _swa_fwd(q, kv, sinks):
    m, d = q.shape
    n_heads = d // ATT_HD
    kvh = n_heads // ATT_G
    kd = kvh * ATT_HD
    nb = m // WINDOW

    def body(q_ref, kvp_ref, kvc_ref, sink_ref, o_ref, lse_ref):
        has_prev = pl.program_id(0) > 0
        qv, kvp, kvc = q_ref[...], kvp_ref[...], kvc_ref[...]
        lane_h = lax.broadcasted_iota(jnp.int32, (WINDOW, n_heads), 1)
        outs, lse_all = [], jnp.zeros((WINDOW, n_heads), F32)
        for kh in range(kvh):
            ks = slice(kh * ATT_HD, (kh + 1) * ATT_HD)
            vs = slice(kd + kh * ATT_HD, kd + (kh + 1) * ATT_HD)
            (q4,), slope, sink = _swa_group([qv], sink_ref, kh, n_heads)
            p_p, p_c, _, lse = _swa_probs(q4, kvp[:, ks], kvc[:, ks], sink, slope, has_prev)
            o4 = _dot(p_p, kvp[:, vs]) + _dot(p_c, kvc[:, vs])
            for g in range(ATT_G):
                rows = slice(g * WINDOW, (g + 1) * WINDOW)
                outs.append(o4[rows, :])
                lse_all = jnp.where(lane_h == kh * ATT_G + g, lse[rows, :], lse_all)
        o_ref[...] = jnp.concatenate(outs, axis=-1).astype(o_ref.dtype)
        lse_ref[...] = lse_all

    return pl.pallas_call(
        body, name="swa_fwd", grid=(nb,),
        in_specs=[pl.BlockSpec((WINDOW, d), lambda n: (n, 0)),
                  pl.BlockSpec((WINDOW, 2 * kd), lambda n: (jnp.maximum(n - 1, 0), 0)),
                  pl.BlockSpec((WINDOW, 2 * kd), lambda n: (n, 0)),
                  pl.BlockSpec((1, n_heads), lambda n: (0, 0))],
        out_specs=[pl.BlockSpec((WINDOW, d), lambda n: (n, 0)), pl.BlockSpec((WINDOW, n_heads), lambda n: (n, 0))],
        out_shape=[jax.ShapeDtypeStruct((m, d), BF16), jax.ShapeDtypeStruct((m, n_heads), F32)],
        compiler_params=_params(("arbitrary",)),
    )(q, kv, kv, sinks)


def _swa_bwd(q, kv, sinks, lse, dao):
    m, d = q.shape
    n_heads = d // ATT_HD
    kvh = n_heads // ATT_G
    kd = kvh * ATT_HD
    nb = m // WINDOW
    scale = ATT_HD ** -0.5

    def body(q_ref, kvp_ref, kvc_ref, sink_ref, lse_ref, do_ref, dq_ref, dkvc_ref, dkvp_ref, dqsum_ref, dsink_ref):
        @pl.when(pl.program_id(0) == 0)
        def _():
            dqsum_ref[...] = jnp.zeros(dqsum_ref.shape, F32)
            dsink_ref[...] = jnp.zeros(dsink_ref.shape, F32)

        has_prev = pl.program_id(0) > 0
        qv, kvp, kvc, dov = q_ref[...], kvp_ref[...], kvc_ref[...], do_ref[...]
        lane_h = lax.broadcasted_iota(jnp.int32, (1, n_heads), 1)
        dsink = jnp.zeros((1, n_heads), F32)
        dq_parts, dk_p, dk_c, dv_p, dv_c = [], [], [], [], []
        for kh in range(kvh):
            ks = slice(kh * ATT_HD, (kh + 1) * ATT_HD)
            vs = slice(kd + kh * ATT_HD, kd + (kh + 1) * ATT_HD)
            kp, kc, vp, vc = kvp[:, ks], kvc[:, ks], kvp[:, vs], kvc[:, vs]
            (q4, do4), slope, sink = _swa_group([qv, dov], sink_ref, kh, n_heads)
            lse4 = jnp.concatenate([lse_ref[:, kh * ATT_G + g:kh * ATT_G + g + 1] for g in range(ATT_G)], axis=0)
            p_p, p_c, p_s, _ = _swa_probs(q4, kp, kc, sink, slope, has_prev, lse=lse4)
            dp_p, dp_c = _dot_nt(do4, vp), _dot_nt(do4, vc)
            delta = jnp.sum(p_p * dp_p, axis=-1, keepdims=True) + jnp.sum(p_c * dp_c, axis=-1, keepdims=True)
            ds_p, ds_c = p_p * (dp_p - delta), p_c * (dp_c - delta)
            sink_term = p_s * delta
            dq4 = (_dot(ds_p, kp) + _dot(ds_c, kc)) * scale
            for g in range(ATT_G):
                rows = slice(g * WINDOW, (g + 1) * WINDOW)
                dsink = dsink + jnp.where(lane_h == kh * ATT_G + g, -_col_sum(sink_term[rows, :]), 0.0)
                dq_parts.append(dq4[rows, :])
            dk_p.append(_dot_tn(ds_p, q4) * scale)
            dk_c.append(_dot_tn(ds_c, q4) * scale)
            dv_p.append(_dot_tn(p_p, do4))
            dv_c.append(_dot_tn(p_c, do4))
        dq = jnp.concatenate(dq_parts, axis=-1)
        dq_ref[...] = dq.astype(dq_ref.dtype)
        dqsum_ref[...] += _col_sum(dq)
        dsink_ref[...] += dsink
        dkvc_ref[...] = jnp.concatenate(dk_c + dv_c, axis=-1)
        dkvp_ref[...] = jnp.concatenate(dk_p + dv_p, axis=-1)

    return pl.pallas_call(
        body, name="swa_bwd", grid=(nb,),
        in_specs=[pl.BlockSpec((WINDOW, d), lambda n: (n, 0)),
                  pl.BlockSpec((WINDOW, 2 * kd), lambda n: (jnp.maximum(n - 1, 0), 0)),
                  pl.BlockSpec((WINDOW, 2 * kd), lambda n: (n, 0)),
                  pl.BlockSpec((1, n_heads), lambda n: (0, 0)),
                  pl.BlockSpec((WINDOW, n_heads), lambda n: (n, 0)),
                  pl.BlockSpec((WINDOW, d), lambda n: (n, 0))],
        out_specs=[pl.BlockSpec((WINDOW, d), lambda n: (n, 0)),
                   pl.BlockSpec((WINDOW, 2 * kd), lambda n: (n, 0)),
                   pl.BlockSpec((WINDOW, 2 * kd), lambda n: (n, 0)),
                   pl.BlockSpec((1, d), lambda n: (0, 0)),
                   pl.BlockSpec((1, n_heads), lambda n: (0, 0))],
        out_shape=[jax.ShapeDtypeStruct((m, d), BF16), jax.ShapeDtypeStruct((m, 2 * kd), F32),
                   jax.ShapeDtypeStruct((m, 2 * kd), F32), jax.ShapeDtypeStruct((1, d), F32),
                   jax.ShapeDtypeStruct((1, n_heads), F32)],
        compiler_params=_params(("arbitrary",)),
    )(q, kv, kv, sinks, lse, dao)


def _kv_grad_combine(dkv_cur, dkv_prev):
    m, w = dkv_cur.shape
    nb = m // WINDOW
    per = max(g for g in (1, 2, 4) if nb % g == 0)
    rows, steps = per * WINDOW, nb // per

    def body(cur_ref, same_ref, next_ref, o_ref, sum_ref):
        @pl.when(pl.program_id(0) == 0)
        def _():
            sum_ref[...] = jnp.zeros(sum_ref.shape, F32)

        after = jnp.where(pl.program_id(0) < steps - 1, next_ref[...], 0.0)
        total = cur_ref[...] + (jnp.concatenate([same_ref[WINDOW:, :], after], axis=0) if per > 1 else after)
        o_ref[...] = total.astype(o_ref.dtype)
        sum_ref[...] += _col_sum(total)

    return pl.pallas_call(
        body, name="kv_grad_combine", grid=(steps,),
        in_specs=[pl.BlockSpec((rows, w), lambda n: (n, 0)), pl.BlockSpec((rows, w), lambda n: (n, 0)),
                  pl.BlockSpec((WINDOW, w), lambda n: (jnp.minimum((n + 1) * per, nb - 1), 0))],
        out_specs=[pl.BlockSpec((rows, w), lambda n: (n, 0)), pl.BlockSpec((1, w), lambda n: (0, 0))],
        out_shape=[jax.ShapeDtypeStruct((m, w), BF16), jax.ShapeDtypeStruct((1, w), F32)],
        compiler_params=_params(("arbitrary",)),
    )(dkv_cur, dkv_prev, dkv_prev)


def _row(v):
    return v.reshape(1, -1)


def _local_step(x, p, target, wget, grad_sink, ln_gain, ln_bias, alb, norm_gain, kv_b, b_q, sinks, b_out, ple_b,
                small_sink=None):
    gs = {}
    gains = ln_gain.reshape(DEPTH * 3, -1)
    biases = ln_bias.reshape(DEPTH * 3, -1)
    sd = x.shape
    pending = [None]

    def mm(a, b, lb=0, **kw):
        after, pending[0] = pending[0], None
        return _mm(a, b, lb=lb, after=after, **kw)

    def mm_ln(a, wt, xin, i, j, nm, bias=None, pu=None):
        r = 3 * i + j
        if pu is None:
            fn, rows = (lambda h, xv, g, bv: (h,) + _ln_fwd_fn(xv, h, g[r:r + 1], bv[r:r + 1])), [xin]
        else:
            fn = lambda h, xv, puv, g, bv: (h,) + _ple_ln_fwd_fn(xv, h, puv, g[r:r + 1], bv[r:r + 1])
            rows = [xin, pu]
        h, y, yb = _mm(a, wt, lb=0, bias=bias, name=nm,
                       post=(fn, rows, [gains, biases], [(sd, F32), (sd, F32), (sd, BF16)], []))
        return h, (y, yb)

    def mm_ln_bwd(a, wt, add, xin, h, i, j, nm):
        r = 3 * i + j
        dx_part, dh, dg, db, dhsum = mm(a, wt, tb=True, add=add, name=nm,
                                        post=(lambda dy, xv, hv, g: _ln_bwd_fn(dy, xv, hv, g[r:r + 1]), [xin, h],
                                              [gains], [(sd, F32), (sd, BF16)], [((1, sd[1]), F32)] * 3))
        gs[f"ln_gain_{i}_{j}"], gs[f"ln_bias_{i}_{j}"] = dg, db
        return dx_part, dh, dhsum

    def tail_fwd(xa, i):
        wgu = wget("ffn_w_gate_up", i, xa[1])
        hid2 = wgu.shape[-1]
        gu, act = _mm(xa[1], wgu, lb=0, name=f"ffn_up_swiglu{i}", tile_cols=hid2 // 2,
                      post=(_swiglu_fwd_fn, [], [], [((sd[0], hid2), BF16), ((sd[0], hid2 // 2), BF16)], []))
        f, xb = mm_ln(act, wget("ffn_w_down", i, act), xa[0], i, 1, f"ffn_down_ln{i}")
        pu = _mm(p, wget("ple_w_up", i, act), la=i, lb=0, name=f"ple_up{i}")
        pg, xc = mm_ln(xb[1], wget("ple_w_gate", i, act), xb[0], i, 2, f"ple_gate_ln{i}", bias=_row(ple_b[i]), pu=pu)
        return dict(xa=xa, gu=gu, act=act, f=f, xb=xb, pg=pg, pu=pu), xc

    def tail_bwd(head, sv, i, mix_in, mix_h):
        xa, xb = sv["xa"], sv["xb"]
        r = 3 * i + 2
        dxb_part, dpg, dpu, dg2, db2, dbg = head(
            lambda dy, xv, pgv, puv, g: _ple_ln_bwd_fn(dy, xv, pgv, puv, g[r:r + 1]), [xb[0], sv["pg"], sv["pu"]],
            [gains], [(sd, F32), (sd, BF16), (sd, BF16)], [((1, sd[1]), F32)] * 3)[:6]
        gs[f"ple_b_{i}"] = dbg
        gs[f"ln_gain_{i}_2"], gs[f"ln_bias_{i}_2"] = dg2, db2
        grad_of("ple_w_gate", i, xb[1], dpg)
        grad_of("ple_w_up", i, p, dpu, la=i)
        dxa_part, df, _ = mm_ln_bwd(dpg, wget("ple_w_gate", i, None), dxb_part, xa[0], sv["f"], i, 1,
                                    f"ple_gate_dx_ln{i}")
        grad_of("ffn_w_down", i, sv["act"], df)
        gu = sv["gu"]
        dgu, = mm(df, wget("ffn_w_down", i, None), tb=True, name=f"ffn_down_dx_swiglu{i}", tile_cols=gu.shape[1] // 4,
                  post=(_swiglu_bwd_fn, [gu], [], [(gu.shape, BF16)], []))
        grad_of("ffn_w_gate_up", i, xa[1], dgu)
        return mm_ln_bwd(dgu, wget("ffn_w_gate_up", i, None), dxa_part, mix_in, mix_h, i, 0, f"ffn_up_dx_ln{i}")

    def grad_of(nm, i, act, dout, la=None, b_parts=None):
        grad = mm(act, dout, la=la, lb=None, ta=True, out_dtype=BF16, out_layers=1, out_layer=0,
                  name=f"grad_{nm}{i}", b_parts=b_parts)
        token = grad_sink(nm, i, grad)
        if token is not None:
            pending[0] = token

    proj = _mm(x, wget("a_w_in", 0, None), lb=0, name="hg_proj")
    o_pre, og, states, scores = _hgrn2_fwd(proj, alb, norm_gain, rb=HG_ROWS)
    h0, x1 = mm_ln(og, wget("a_w_out", 0, og), x, 0, 0, "hg_out_ln")
    sv0, x3 = tail_fwd(x1, 0)
    kv = _mm(x3[1], wget("kv_w", 0, x3[1]), lb=0, bias=_row(kv_b), out_dtype=BF16, name="kv_proj")
    q = _mm(x3[1], wget("b_w_q", 0, x3[1]), lb=0, bias=b_q, out_dtype=BF16, name="q_proj")
    ao, lse = _swa_fwd(q, kv, sinks)
    h1, x4 = mm_ln(ao, wget("b_w_out", 0, x3[1]), x3[0], 1, 0, "att_out_ln", bias=b_out)
    sv1, y = tail_fwd(x4, 1)

    loss_box = []

    def loss_head(fn, rows, whole, outs, sums):
        def with_loss(yv, tv, *rest):
            dy, part = _loss_fn(yv, tv)
            return fn(dy, *rest) + (part,)

        res = _rowwise(with_loss, [y[0], target] + rows, whole, outs, list(sums) + [((1, LANES), F32)],
                       name="loss_ln_ple_bwd1")
        loss_box.append(res[-1])
        return res

    dx3_part, dh1, dh1sum = tail_bwd(loss_head, sv1, 1, x3[0], h1)
    loss = loss_box[0]
    gs["b_out"] = dh1sum
    grad_of("b_w_out", 0, ao, dh1)
    dao = mm(dh1, wget("b_w_out", 0, None), tb=True, out_dtype=BF16, name="att_out_dx")
    dq, dkv_cur, dkv_prev, dqsum, dsinks = _swa_bwd(q, kv, sinks, lse, dao)
    gs["b_q"], gs["sinks"] = dqsum, dsinks
    dkv, dkvsum = _kv_grad_combine(dkv_cur, dkv_prev)
    gs["kv_b"] = dkvsum
    grad_of("b_w_q", 0, x3[1], dq)
    grad_of("kv_w", 0, x3[1], dkv)
    dx3 = mm(dq, wget("b_w_q", 0, None), tb=True, add=dx3_part, name="q_proj_dx")

    def kv_head(*post):
        return mm(dkv, wget("kv_w", 0, None), tb=True, add=dx3, name="kv_proj_dx_ln_ple_bwd0", post=post)

    dx_part, dh0, _ = tail_bwd(kv_head, sv0, 0, x, h0)
    grad_of("a_w_out", 0, og, dh0)
    dog = mm(dh0, wget("a_w_out", 0, None), tb=True, name="hg_out_dx")
    dproj, dalb, dgain = _hgrn2_bwd(proj, o_pre, states, scores, dog, alb, norm_gain, rb=HG_ROWS)
    gs["alb"], gs["norm_gain"] = dalb, dgain
    if small_sink is not None:
        pending[0] = small_sink(loss, gs)
    grad_of("a_w_in", 0, x, dproj, b_parts=4)
    grad_x = mm(dproj, wget("a_w_in", 0, None), tb=True, add=dx_part, name="hg_proj_dx", a_parts=4)
    return loss, grad_x, gs


HBM_SPEC = pl.BlockSpec(memory_space=pl.ANY)
HBM_ONLY = pl.BlockSpec(memory_space=pltpu.HBM)
SEM_SPEC = pl.BlockSpec(memory_space=pltpu.SEMAPHORE)
SIDE_EFFECT = pltpu.SideEffectType.DATAFLOW_SIDE_EFFECTING


def _slot(kind, j):
    return (j % 2) * 2 + j // 2 if kind == "colp" else j


def _piece(ref, kind, j):
    _, r, c = ref.shape
    if kind == "row":
        return ref.at[:, pl.ds(j * (r // N_CHIPS), r // N_CHIPS), :]
    return ref.at[:, :, pl.ds(_slot(kind, j) * (c // N_CHIPS), c // N_CHIPS)]


def _piece_dyn(ref, kind, j):
    _, r, c = ref.shape
    if kind == "row":
        return ref.at[:, pl.ds(pl.multiple_of(j * (r // N_CHIPS), 16), r // N_CHIPS), :]
    return ref.at[:, :, pl.ds(pl.multiple_of(_slot(kind, j) * (c // N_CHIPS), LANES), c // N_CHIPS)]


def _chip_of(j, c):
    return (j // 2, j % 2, c)


def _in_hbm(a):
    return pltpu.with_memory_space_constraint(a, pltpu.HBM)


PLACE_STEPS = 4


def _place(items, chip, *, name, after=None):
    n = len(items)
    in_specs, out_specs, out_shapes, blocks = [], [], [], []
    for src, layer, kind, out_dtype in items:
        _, r, c = src.shape
        nb = max(k for k in (1, 2, PLACE_STEPS) if r % (16 * k) == 0 or k == 1)
        blocks.append(nb)

        def src_idx(i, chip_ref, layer=layer, nb=nb):
            return (layer, jnp.minimum(i, nb - 1), 0)

        def full_idx(i, chip_ref, kind=kind, nb=nb):
            ib = jnp.minimum(i, nb - 1)
            return (0, chip_ref[0] * nb + ib, 0) if kind == "row" else (0, ib, _slot(kind, chip_ref[0]))

        in_specs.append(pl.BlockSpec((None, r // nb, c), src_idx))
        out_specs.append(pl.BlockSpec((None, r // nb, c), full_idx))
        out_shapes.append(jax.ShapeDtypeStruct((1, r * N_CHIPS, c) if kind == "row" else (1, r, c * N_CHIPS),
                                               out_dtype))
    operands = [it[0] for it in items]
    if after is not None:
        in_specs.append(HBM_SPEC)
        operands.append(after)

    def body(chip_ref, *refs):
        for a in range(n):
            refs[len(refs) - n + a][...] = refs[a][...].astype(refs[len(refs) - n + a].dtype)

    return pl.pallas_call(
        body, name=name,
        grid_spec=pltpu.PrefetchScalarGridSpec(num_scalar_prefetch=1, grid=(PLACE_STEPS,), in_specs=in_specs,
                                               out_specs=out_specs),
        out_shape=out_shapes,
        compiler_params=_params(("arbitrary",)),
    )(chip, *operands)


def _half(ref, c):
    h = ref.shape[1] // 2
    start = c * h if isinstance(c, int) else pl.multiple_of(c * h, 16)
    return ref.at[:, pl.ds(start, h), :]


def _sibling_handshake():
    barrier = pltpu.get_barrier_semaphore()
    sibling = (lax.axis_index("x"), lax.axis_index("y"), 1 - lax.axis_index("c"))
    pl.semaphore_signal(barrier, inc=1, device_id=sibling, device_id_type=MESH)
    pl.semaphore_wait(barrier, 1)


class _SiblingFill:
    def __init__(self, lands, kinds, name, collective_id):
        self.kinds, self.name, self.n = kinds, name, len(lands)
        n = self.n
        sem_shape = pltpu.SemaphoreType.DMA((n * N_CHIPS,))

        def body(*refs):
            land_refs, send_sems, recv_sems, token = refs[:n], refs[n], refs[n + 1], refs[-1]
            _sibling_handshake()
            for cp in self._copies(land_refs, send_sems, recv_sems):
                cp.start()
            token[...] = jnp.zeros(token.shape, token.dtype)

        outs = pl.pallas_call(
            body, name=name + "_start",
            in_specs=[HBM_ONLY] * n,
            out_specs=[SEM_SPEC, SEM_SPEC] + [HBM_ONLY] * n + [pl.BlockSpec(memory_space=pltpu.VMEM)],
            out_shape=[sem_shape, sem_shape] + [pltpu.HBM(a.shape, a.dtype) for a in lands]
                      + [jax.ShapeDtypeStruct((8, LANES), F32)],
            input_output_aliases={i: i + 2 for i in range(n)},
            compiler_params=pltpu.CompilerParams(has_side_effects=SIDE_EFFECT, collective_id=collective_id),
        )(*[_in_hbm(a) for a in lands])
        self.send_sems, self.recv_sems, self.lands, self.token = outs[0], outs[1], list(outs[2:2 + n]), outs[-1]

    def _copies(self, land_refs, send_sems, recv_sems):
        x, y, c = lax.axis_index("x"), lax.axis_index("y"), lax.axis_index("c")
        me = 2 * x + y
        copies = []
        for a in range(self.n):
            for k in range(1, N_CHIPS):
                t = (me + k) % N_CHIPS
                slice_t = _piece_dyn(land_refs[a], self.kinds[a], t)
                got = _half(slice_t, c)
                copies.append(pltpu.make_async_remote_copy(
                    src_ref=got, dst_ref=got, send_sem=send_sems.at[a * N_CHIPS + k],
                    recv_sem=recv_sems.at[a * N_CHIPS + k], device_id=(x, y, 1 - c), device_id_type=MESH))
        return copies

    def wait(self, after):
        n = self.n

        def body(*refs):
            land_refs, send_sems, recv_sems = refs[:n], refs[n], refs[n + 1]
            for cp in self._copies(land_refs, send_sems, recv_sems):
                cp.wait_send()
                cp.wait_recv()

        operands = [_in_hbm(a) for a in self.lands] + [self.send_sems, self.recv_sems]
        in_specs = [HBM_ONLY] * n + [SEM_SPEC, SEM_SPEC]
        if after is not None:
            operands.append(after)
            in_specs.append(HBM_SPEC)
        outs = pl.pallas_call(
            body, name=self.name + "_wait",
            in_specs=in_specs, out_specs=[HBM_ONLY] * n,
            out_shape=[pltpu.HBM(a.shape, a.dtype) for a in self.lands],
            input_output_aliases={i: i for i in range(n)},
            compiler_params=pltpu.CompilerParams(has_side_effects=SIDE_EFFECT),
        )(*operands)
        return list(outs)


class _Exchange:
    def __init__(self, mode, srcs, lands, kinds, layers, name, collective_id, after=None, halves=None):
        self.mode, self.kinds, self.layers, self.name, self.n = mode, kinds, layers, name, len(lands)
        self.halves = halves if halves is not None else [False] * len(lands)
        n, ns = self.n, len(srcs)
        n_in = ns + n + (after is not None)
        sem_shape = pltpu.SemaphoreType.DMA((n * N_CHIPS,))

        def body(*refs):
            src_refs, land_refs = refs[:ns], refs[ns:ns + n]
            send_sems, recv_sems = refs[n_in], refs[n_in + 1]
            token = refs[-1]
            c = lax.axis_index("c")
            me = 2 * lax.axis_index("x") + lax.axis_index("y")
            barrier = pltpu.get_barrier_semaphore()
            for k in range(1, N_CHIPS):
                t = (me + k) % N_CHIPS
                pl.semaphore_signal(barrier, inc=1, device_id=(t // 2, t % 2, c), device_id_type=MESH)
            pl.semaphore_wait(barrier, N_CHIPS - 1)
            for j in range(N_CHIPS):
                @pl.when(me == j)
                def _():
                    for a in range(n):
                        for t in range(N_CHIPS):
                            if t != j:
                                src, dst = self._ends(src_refs, land_refs, a, j, t, c)
                                pltpu.make_async_remote_copy(
                                    src_ref=src, dst_ref=dst, send_sem=send_sems.at[a * N_CHIPS + t],
                                    recv_sem=recv_sems.at[a * N_CHIPS + j],
                                    device_id=_chip_of(t, c), device_id_type=MESH).start()
            token[...] = jnp.zeros(token.shape, token.dtype)

        arrays = list(srcs) + list(lands)
        operands = [_in_hbm(a) for a in arrays]
        in_specs = [HBM_ONLY] * (ns + n)
        if after is not None:
            operands.append(after)
            in_specs.append(HBM_SPEC)
        outs = pl.pallas_call(
            body, name=name + "_start",
            in_specs=in_specs,
            out_specs=[SEM_SPEC, SEM_SPEC] + [HBM_ONLY] * (ns + n) + [pl.BlockSpec(memory_space=pltpu.VMEM)],
            out_shape=[sem_shape, sem_shape] + [pltpu.HBM(a.shape, a.dtype) for a in arrays]
                      + [jax.ShapeDtypeStruct((8, LANES), F32)],
            input_output_aliases={i: i + 2 for i in range(ns + n)},
            compiler_params=pltpu.CompilerParams(has_side_effects=SIDE_EFFECT, collective_id=collective_id),
        )(*operands)
        self.send_sems, self.recv_sems = outs[0], outs[1]
        self.srcs, self.lands = list(outs[2:2 + ns]), list(outs[2 + ns:2 + ns + n])
        self.token = outs[-1]

    def _ends(self, src_refs, land_refs, a, me_j, peer, c):
        if self.mode == "gather":
            mine = _piece(land_refs[a], self.kinds[a], me_j)
            if self.halves[a]:
                mine = _half(mine, c)
            return mine, mine
        return _piece(src_refs[a], self.kinds[a], peer), land_refs[a].at[me_j, pl.ds(self.layers[a], 1)]

    def wait(self, after, lands=None):
        n, ns = self.n, len(self.srcs)
        lands = self.lands if lands is None else lands

        def body(*refs):
            src_refs, land_refs = refs[:ns], refs[ns:ns + n]
            send_sems, recv_sems = refs[ns + n], refs[ns + n + 1]
            c = lax.axis_index("c")
            me = 2 * lax.axis_index("x") + lax.axis_index("y")
            for j in range(N_CHIPS):
                @pl.when(me != j)
                def _():
                    for a in range(n):
                        sent, _ = self._ends(src_refs, land_refs, a, 0, j, c)
                        _, landed = self._ends(src_refs, land_refs, a, j, 0, c)
                        cp = pltpu.make_async_remote_copy(
                            src_ref=sent, dst_ref=landed, send_sem=send_sems.at[a * N_CHIPS + j],
                            recv_sem=recv_sems.at[a * N_CHIPS + j],
                            device_id=_chip_of(j, c), device_id_type=MESH)
                        cp.wait_send()
                        cp.wait_recv()

        arrays = self.srcs + list(lands)
        operands = [_in_hbm(a) for a in arrays] + [self.send_sems, self.recv_sems]
        in_specs = [HBM_ONLY] * (ns + n) + [SEM_SPEC, SEM_SPEC]
        if after is not None:
            operands.append(after)
            in_specs.append(HBM_SPEC)
        outs = pl.pallas_call(
            body, name=self.name + "_wait",
            in_specs=in_specs, out_specs=[HBM_ONLY] * (ns + n),
            out_shape=[pltpu.HBM(a.shape, a.dtype) for a in arrays],
            input_output_aliases={i: i for i in range(ns + n)},
            compiler_params=pltpu.CompilerParams(has_side_effects=SIDE_EFFECT),
        )(*operands)
        return list(outs[:ns]), list(outs[ns:])


def _sum_arrivals(zone, own_grads, kind, chip, name, after=None):
    _, layers, r, c = zone.shape
    tm = _pick_rows(r, 512)
    if layers * (r // tm) == 1 and r % 32 == 0:
        tm = r // 2
    nb = r // tm

    def own_idx(u):
        def idx(l, i, chip_ref):
            ib = jnp.where(l == u, i, 0)
            return (0, chip_ref[0] * nb + ib, 0) if kind == "row" else (0, ib, _slot(kind, chip_ref[0]))
        return idx

    def slot_idx(k):
        return lambda l, i, chip_ref: ((chip_ref[0] + k) % N_CHIPS, l, i, 0)

    in_specs = [pl.BlockSpec((None, None, tm, c), slot_idx(k)) for k in range(1, N_CHIPS)]
    in_specs += [pl.BlockSpec((None, tm, c), own_idx(u)) for u in range(len(own_grads))]
    operands = [zone] * (N_CHIPS - 1) + list(own_grads)
    if after is not None:
        in_specs.append(HBM_SPEC)
        operands.append(after)

    def body(chip_ref, *refs):
        slot_refs, own_refs, o_ref = refs[:N_CHIPS - 1], refs[N_CHIPS - 1:N_CHIPS - 1 + layers], refs[-1]
        own = own_refs[0][...]
        for u in range(1, layers):
            own = jnp.where(pl.program_id(0) == u, own_refs[u][...], own)
        acc = own.astype(F32)
        for ref in slot_refs:
            acc = acc + ref[...].astype(F32)
        o_ref[...] = acc.astype(o_ref.dtype)

    return pl.pallas_call(
        body, name=name,
        grid_spec=pltpu.PrefetchScalarGridSpec(
            num_scalar_prefetch=1, grid=(layers, nb), in_specs=in_specs,
            out_specs=pl.BlockSpec((tm, c), lambda l, i, chip_ref: (l * nb + i, 0))),
        out_shape=jax.ShapeDtypeStruct((layers * r, c), BF16),
        compiler_params=_params(("arbitrary", "arbitrary")),
    )(chip, *operands)


class _SiblingSwap:
    def __init__(self, arrays, name, collective_id, after=None):
        self.name, self.n = name, len(arrays)
        n = self.n
        n_in = n + (after is not None)
        sem_shape = pltpu.SemaphoreType.DMA((n,))

        def body(*refs):
            ins, send_sems, recv_sems = refs[:n], refs[n_in], refs[n_in + 1]
            theirs, token = refs[n_in + 2 + n:n_in + 2 + 2 * n], refs[-1]
            _sibling_handshake()
            for cp in self._copies(ins, theirs, send_sems, recv_sems):
                cp.start()
            token[...] = jnp.zeros(token.shape, token.dtype)

        operands, in_specs = [_in_hbm(a) for a in arrays], [HBM_ONLY] * n
        if after is not None:
            operands.append(after)
            in_specs.append(HBM_SPEC)
        outs = pl.pallas_call(
            body, name=name + "_start",
            in_specs=in_specs,
            out_specs=[SEM_SPEC, SEM_SPEC] + [HBM_ONLY] * (2 * n) + [pl.BlockSpec(memory_space=pltpu.VMEM)],
            out_shape=[sem_shape, sem_shape] + [pltpu.HBM(a.shape, a.dtype) for a in arrays] * 2
                      + [jax.ShapeDtypeStruct((8, LANES), F32)],
            input_output_aliases={i: i + 2 for i in range(n)},
            compiler_params=pltpu.CompilerParams(has_side_effects=SIDE_EFFECT, collective_id=collective_id),
        )(*operands)
        self.send_sems, self.recv_sems = outs[0], outs[1]
        self.mine, self.theirs, self.token = list(outs[2:2 + n]), list(outs[2 + n:2 + 2 * n]), outs[-1]

    def _copies(self, mine, theirs, send_sems, recv_sems):
        sibling = (lax.axis_index("x"), lax.axis_index("y"), 1 - lax.axis_index("c"))
        return [pltpu.make_async_remote_copy(src_ref=mine[a], dst_ref=theirs[a], send_sem=send_sems.at[a],
                                             recv_sem=recv_sems.at[a], device_id=sibling, device_id_type=MESH)
                for a in range(self.n)]

    def wait(self, after):
        n = self.n

        def body(*refs):
            for cp in self._copies(refs[:n], refs[n:2 * n], refs[2 * n], refs[2 * n + 1]):
                cp.wait_send()
                cp.wait_recv()

        arrays = self.mine + self.theirs
        outs = pl.pallas_call(
            body, name=self.name + "_wait",
            in_specs=[HBM_ONLY] * (2 * n) + [SEM_SPEC, SEM_SPEC, HBM_SPEC], out_specs=[HBM_ONLY] * (2 * n),
            out_shape=[pltpu.HBM(a.shape, a.dtype) for a in arrays],
            input_output_aliases={i: i for i in range(2 * n)},
            compiler_params=pltpu.CompilerParams(has_side_effects=SIDE_EFFECT),
        )(*[_in_hbm(a) for a in arrays], self.send_sems, self.recv_sems, after)
        return list(outs[:n]), list(outs[n:])


class _GatherDevices:
    def __init__(self, vec):
        sem_shape = pltpu.SemaphoreType.DMA((N_DEV,))

        def body(in_ref, send_sems, recv_sems, vec_ref, out_ref, token):
            for cp in self._copies(in_ref, out_ref, send_sems, recv_sems):
                cp.start()
            token[...] = jnp.zeros(token.shape, token.dtype)

        outs = pl.pallas_call(
            body, name="gather_small_start",
            in_specs=[HBM_ONLY],
            out_specs=[SEM_SPEC, SEM_SPEC, HBM_ONLY, HBM_ONLY, pl.BlockSpec(memory_space=pltpu.VMEM)],
            out_shape=[sem_shape, sem_shape, pltpu.HBM(vec.shape, vec.dtype),
                       pltpu.HBM((N_DEV,) + vec.shape, vec.dtype), jax.ShapeDtypeStruct((8, LANES), F32)],
            input_output_aliases={0: 2},
            compiler_params=pltpu.CompilerParams(has_side_effects=SIDE_EFFECT),
        )(_in_hbm(vec))
        self.send_sems, self.recv_sems, self.vec, self.rows, self.token = outs

    def _copies(self, in_ref, out_ref, send_sems, recv_sems):
        x, y, c = lax.axis_index("x"), lax.axis_index("y"), lax.axis_index("c")
        me = 4 * x + 2 * y + c
        copies = [pltpu.make_async_copy(in_ref, out_ref.at[me], recv_sems.at[0])]
        for rel in range(1, N_DEV):
            peer = (x ^ (rel >> 2), y ^ ((rel >> 1) & 1), c ^ (rel & 1))
            copies.append(pltpu.make_async_remote_copy(
                src_ref=in_ref, dst_ref=out_ref.at[me], send_sem=send_sems.at[rel], recv_sem=recv_sems.at[rel],
                device_id=peer, device_id_type=MESH))
        return copies

    def wait(self, after):
        def body(vec_ref, rows_ref, send_sems, recv_sems, after_ref, vec_out, rows_out):
            copies = self._copies(vec_ref, rows_ref, send_sems, recv_sems)
            copies[0].wait()
            for cp in copies[1:]:
                cp.wait_send()
                cp.wait_recv()

        outs = pl.pallas_call(
            body, name="gather_small_wait",
            in_specs=[HBM_ONLY, HBM_ONLY, SEM_SPEC, SEM_SPEC, HBM_SPEC], out_specs=[HBM_ONLY, HBM_ONLY],
            out_shape=[pltpu.HBM(self.vec.shape, self.vec.dtype), pltpu.HBM(self.rows.shape, self.rows.dtype)],
            input_output_aliases={0: 0, 1: 1},
            compiler_params=pltpu.CompilerParams(has_side_effects=SIDE_EFFECT),
        )(_in_hbm(self.vec), _in_hbm(self.rows), self.send_sems, self.recv_sems, after)
        return outs[1]


BIG = [("a_w_in", "col"), ("a_w_out", "row"), ("kv_w", "row"), ("b_w_q", "row"), ("b_w_out", "row"),
       ("ffn_w_gate_up", "colp"), ("ffn_w_down", "row"), ("ple_w_up", "col"), ("ple_w_gate", "row")]
GATHER_GROUPS = [[("a_w_in", 0), ("small", 0)], [("a_w_out", 0)],
                 [("ffn_w_gate_up", 0), ("ffn_w_down", 0), ("ple_w_gate", 0), ("ple_w_up", 0)],
                 [("kv_w", 0), ("b_w_q", 0), ("b_w_out", 0)],
                 [("ffn_w_gate_up", 1)], [("ffn_w_down", 1), ("ple_w_gate", 1), ("ple_w_up", 1)]]
SCATTER_GROUPS = [[("ple_w_gate", 1), ("ple_w_up", 1), ("ffn_w_down", 1)], [("ffn_w_gate_up", 1)],
                  [("b_w_out", 0), ("b_w_q", 0), ("kv_w", 0)], [("ple_w_gate", 0), ("ple_w_up", 0), ("ffn_w_down", 0)],
                  [("ffn_w_gate_up", 0), ("a_w_out", 0)], [("a_w_in", 0)]]
COLLECTIVE_IDS = {"fill": 0, "swap": 6, "gather": 9, "scatter": 15}
SMALL_SHARDED = ["ln_gain", "ln_bias", "a_lower_bound"]
SMALL_REPLICATED = ["a_norm_gain", "kv_b", "b_b_q", "b_sinks", "b_b_out", "ple_b_gate"]
WEIGHT_ORDER = ["a_w_in", "a_lower_bound", "a_norm_gain", "a_w_out", "kv_w", "kv_b", "b_w_q", "b_b_q", "b_sinks",
                "b_w_out", "b_b_out", "ffn_w_gate_up", "ffn_w_down", "ple_w_up", "ple_w_gate", "ple_b_gate",
                "ln_gain", "ln_bias"]


def _as3(a):
    return a.reshape((-1,) + a.shape[-2:]) if a.ndim >= 3 else a.reshape((1,) + a.shape)


def _pad_lanes(v):
    n = v.shape[-1]
    return jnp.pad(v, ((0, 0), (0, (-n) % LANES)))


ADAM_MANY_STEPS = 2
ADAM_MANY_MAX = 1 << 19


def _adam_many(groups, name):
    in_specs, out_specs, out_shapes, arrays = [], [], [], []
    for group in groups:
        r, c = group[0].shape
        block = pl.BlockSpec((r // ADAM_MANY_STEPS, c), lambda i: (i, 0))
        in_specs += [block] * len(group)
        arrays += list(group)
        out_specs += [block] * 4
        out_shapes += [jax.ShapeDtypeStruct((r, c), F32)] * 4

    def body(*refs):
        ins, outs = refs[:len(arrays)], refs[len(arrays):]
        for k in range(len(groups)):
            res = _adam_fn(*[ref[...] for ref in ins[5 * k:5 * k + 5]])
            for out_ref, val in zip(outs[4 * k:4 * k + 4], res):
                out_ref[...] = val

    result = pl.pallas_call(
        body, name=name, grid=(ADAM_MANY_STEPS,), in_specs=in_specs, out_specs=out_specs, out_shape=out_shapes,
        compiler_params=_params(("parallel",)),
    )(*arrays)
    return [result[4 * k:4 * k + 4] for k in range(len(groups))]


def _adam_small(everyone, chip, items, loss_off):
    n_items = len(items)

    def body(chip_ref, every_ref, *refs):
        ins, outs = refs[:3 * n_items], refs[3 * n_items:]

        def total(off, width):
            acc = every_ref[0, :, off:off + width]
            for s in range(1, N_DEV):
                acc = acc + every_ref[s, :, off:off + width]
            return acc

        for a, (w, _, _, off, sharded) in enumerate(items):
            cols = w.shape[-1]
            for r in range(w.size // cols):
                at = (slice(r, r + 1),) if w.ndim == 2 else (r // w.shape[1], slice(r % w.shape[1], r % w.shape[1] + 1))
                if sharded:
                    full = total(off + r * N_CHIPS * cols, N_CHIPS * cols)
                    g = full[:, 0:cols]
                    for c in range(1, N_CHIPS):
                        g = jnp.where(chip_ref[0] == c, full[:, c * cols:(c + 1) * cols], g)
                else:
                    g = total(off + r * cols, cols)
                w_ref, m_ref, v_ref = ins[3 * a:3 * a + 3]
                res = _adam_fn(w_ref[at], m_ref[at], v_ref[at], g, jnp.zeros_like(g))
                for out_ref, val in zip(outs[4 * a:4 * a + 4], res):
                    out_ref[at] = val
        outs[-1][...] = total(loss_off, LANES)

    def whole(shape):
        return pl.BlockSpec(tuple(shape), lambda i, chip_ref: (0,) * len(shape))

    arrays = [arr for it in items for arr in it[:3]]
    out_shapes = [jax.ShapeDtypeStruct(it[0].shape, F32) for it in items for _ in range(4)]
    out_shapes.append(jax.ShapeDtypeStruct((1, LANES), F32))
    result = pl.pallas_call(
        body, name="adam_small",
        grid_spec=pltpu.PrefetchScalarGridSpec(
            num_scalar_prefetch=1, grid=(1,),
            in_specs=[whole(everyone.shape)] + [whole(arr.shape) for arr in arrays],
            out_specs=[whole(s.shape) for s in out_shapes]),
        out_shape=out_shapes,
        compiler_params=_params(("arbitrary",)),
    )(chip, everyone, *arrays)
    return [result[4 * a:4 * a + 4] for a in range(n_items)], result[-1]


def kernel(x, p, a_w_in, a_lower_bound, a_norm_gain, a_w_out, kv_w, kv_b, b_w_q, b_b_q, b_sinks, b_w_out, b_b_out, ffn_w_gate_up, ffn_w_down, ple_w_up, ple_w_gate, ple_b_gate, ln_gain, ln_bias, loss_target, m_a_w_in, m_a_lower_bound, m_a_norm_gain, m_a_w_out, m_kv_w, m_kv_b, m_b_w_q, m_b_b_q, m_b_sinks, m_b_w_out, m_b_b_out, m_ffn_w_gate_up, m_ffn_w_down, m_ple_w_up, m_ple_w_gate, m_ple_b_gate, m_ln_gain, m_ln_bias, v_a_w_in, v_a_lower_bound, v_a_norm_gain, v_a_w_out, v_kv_w, v_kv_b, v_b_w_q, v_b_b_q, v_b_sinks, v_b_w_out, v_b_b_out, v_ffn_w_gate_up, v_ffn_w_down, v_ple_w_up, v_ple_w_gate, v_ple_b_gate, v_ln_gain, v_ln_bias):
    args = dict(locals())
    wts = {n: args[n] for n in WEIGHT_ORDER}
    mom = {n: args["m_" + n] for n in WEIGHT_ORDER}
    vel = {n: args["v_" + n] for n in WEIGHT_ORDER}
    chip = 2 * lax.axis_index("x") + lax.axis_index("y")
    d = x.shape[-1]
    dq = d // N_CHIPS

    kind_of = dict(BIG)
    kind_of["small"] = "col"
    chip_arr = chip.reshape(1).astype(jnp.int32)
    small_pack = jnp.concatenate([wts[n].reshape(-1, dq) for n in SMALL_SHARDED], axis=0)[None]

    def place_item(key):
        n, layer = key
        if n == "small":
            return small_pack, 0, "col", F32
        return _as3(wts[n]), layer, kind_of[n], BF16

    gathers, where = [], {}
    for gi, group in enumerate(GATHER_GROUPS):
        prev = gathers[-1].token if gathers else None
        placed = _place([place_item(k) for k in group], chip_arr, name=f"place{gi}", after=prev)
        gathers.append(_Exchange("gather", [], placed, [kind_of[k[0]] for k in group],
                                 [0] * len(group), f"gather{gi}", COLLECTIVE_IDS["gather"] + gi, after=prev,
                                 halves=[k[0] != "small" for k in group]))
        for k in group:
            where[k] = gi
    all_started = gathers[-1].token
    ready = {}

    fills = {}

    def pass_on(gi, after):
        if gi not in fills:
            group = GATHER_GROUPS[gi]
            outs = gathers[gi].wait(after)[1]
            split = [i for i, k in enumerate(group) if k[0] != "small"]
            fills[gi] = (outs, split, _SiblingFill([outs[i] for i in split], [kind_of[group[i][0]] for i in split],
                                                   f"fill{gi}", COLLECTIVE_IDS["fill"] + gi))

    def wget(name, layer, after):
        key = (name, layer)
        if key not in ready:
            gi = where[key]
            after = all_started if gi == 0 else after
            pass_on(gi, after)
            if 1 <= gi < len(GATHER_GROUPS) - 1:
                pass_on(gi + 1, after)
                after = fills[gi + 1][2].token
            outs, split, fill = fills[gi]
            for i, arr in zip(split, fill.wait(after)):
                outs[i] = arr
            for k, arr in zip(GATHER_GROUPS[gi], outs):
                ready[k] = arr
        return ready[key]

    small_full = wget("small", 0, None)[0]
    ln_gain_f = small_full[0:6].reshape(DEPTH, 3, d)
    ln_bias_f = small_full[6:12].reshape(DEPTH, 3, d)
    alb_f = small_full[12:14]

    group_of = {k: gi for gi, group in enumerate(SCATTER_GROUPS) for k in group}
    grads_done, zones, scatters = {}, {}, []

    def grad_sink(name, layer, grad):
        grads_done[(name, layer)] = grad
        if name not in zones:
            zones[name] = lax.empty((N_CHIPS,) + _as3(wts[name]).shape, BF16)
        gi = group_of[(name, layer)]
        group = SCATTER_GROUPS[gi]
        if not all(k in grads_done for k in group):
            return None
        ex = _Exchange("scatter", [grads_done[k] for k in group], [zones[k[0]] for k in group],
                       [kind_of[k[0]] for k in group], [k[1] for k in group], f"scatter{gi}",
                       COLLECTIVE_IDS["scatter"] + gi)
        for k, zone in zip(group, ex.lands):
            zones[k[0]] = zone
        scatters.append((ex, group))
        return ex.token

    small = {}

    def small_sink(loss, gs):
        ln_g = jnp.concatenate([gs[f"ln_gain_{i}_{j}"] for i in range(DEPTH) for j in range(3)], axis=0)
        ln_b = jnp.concatenate([gs[f"ln_bias_{i}_{j}"] for i in range(DEPTH) for j in range(3)], axis=0)
        ple_bg = jnp.concatenate([gs[f"ple_b_{i}"] for i in range(DEPTH)], axis=0)
        small["list"] = [ln_g.reshape(1, -1), ln_b.reshape(1, -1), gs["alb"].reshape(1, -1), gs["norm_gain"],
                         gs["kv_b"], gs["b_q"], _pad_lanes(gs["sinks"]), gs["b_out"], ple_bg.reshape(1, -1), loss]
        small["gather"] = _GatherDevices(jnp.concatenate(small["list"], axis=1))
        return small["gather"].token

    loss, grad_x, gs = _local_step(
        x[0], p.reshape((p.shape[0],) + p.shape[2:]), loss_target[0], wget, grad_sink, ln_gain_f, ln_bias_f, alb_f, a_norm_gain, kv_b, b_b_q,
        b_sinks, b_b_out, ple_b_gate, small_sink)

    res = {}

    def arrive(batch, after):
        for ex, group in batch:
            srcs, outs = ex.wait(after, lands=[zones[k[0]] for k in group])
            for k, grad, zone in zip(group, srcs, outs):
                grads_done[k], zones[k[0]] = grad, zone

    def half_sums(names, batch, after):
        partial = []
        for n in names:
            own = [grads_done[(n, layer)] for layer in range(zones[n].shape[1])]
            partial.append(_sum_arrivals(zones[n], own, kind_of[n], chip_arr, f"sum_{n}", after=after))
        return _SiblingSwap(partial, f"swap{batch}", COLLECTIVE_IDS["swap"] + batch, after=after)

    def update(names, swap, after):
        flat = lambda a: a.reshape(-1, a.shape[-1])
        work = [(n, [flat(wts[n]), flat(mom[n]), flat(vel[n]), own, sib]) for n, own, sib in zip(names, *swap.wait(after))]
        many = [(n, ops) for n, ops in work if wts[n].size <= ADAM_MANY_MAX]
        if len(many) > 1:
            for (n, _), out in zip(many, _adam_many([ops for _, ops in many], f"adam_from_{many[0][0]}")):
                res[n] = [o.reshape(wts[n].shape) for o in out]
        for n, ops in work:
            if n not in res:
                out = _rowwise(_adam_fn, ops, [], [(ops[3].shape, F32)] * 4, name=f"adam_{n}")
                res[n] = [o.reshape(wts[n].shape) for o in out]
        return res[names[-1]][1]

    last_names = [k[0] for k in SCATTER_GROUPS[-1]]
    batches = [["ffn_w_gate_up"], [n for n, _ in BIG if n != "ffn_w_gate_up" and n not in last_names], last_names]
    arrive(scatters[:-1], grad_x)
    swap0 = half_sums(batches[0], 0, None)
    swap1 = half_sums(batches[1], 1, swap0.token)
    updated = update(batches[0], swap0, swap1.token)
    arrive(scatters[-1:], updated)
    swap2 = half_sums(batches[2], 2, swap1.token)
    updated = update(batches[1], swap1, swap2.token)
    update(batches[2], swap2, updated)

    everyone = small["gather"].wait(grad_x)
    offs, pos = [], 0
    for v in small["list"]:
        offs.append(pos)
        pos += v.shape[1]
    names = ["ln_gain", "ln_bias", "a_lower_bound", "a_norm_gain", "kv_b", "b_b_q", "b_sinks", "b_b_out", "ple_b_gate"]
    as_rows = lambda a: a.reshape(1, -1) if a.ndim == 1 else a
    items = [(as_rows(wts[n]), as_rows(mom[n]), as_rows(vel[n]), off, n in SMALL_SHARDED)
             for n, off in zip(names, offs)]
    updates, loss_row = _adam_small(everyone, chip_arr, items, offs[len(names)])
    for n, upd in zip(names, updates):
        res[n] = [u.reshape(wts[n].shape) for u in upd]

    outs = [loss_row[0, 0], grad_x[None]]
    for k in range(4):
        outs += [res[n][k] for n in WEIGHT_ORDER]
    return tuple(outs)
```

```python
import functools

import jax
import jax.numpy as jnp
from jax import lax
from jax.experimental import pallas as pl
from jax.experimental.pallas import tpu as pltpu

F32 = jnp.float32
BF16 = jnp.bfloat16
MESH = pl.DeviceIdType.MESH

LANES = 128
HG_DK = 128
HG_CHUNK = 64
HG_SUB = 16
HG_ROWS = 512
HG_HEADS_PER_STEP = 2
LOG2_E = 1.4426950408889634
ATT_HD = 64
ATT_G = 4
WINDOW = 128
DEPTH = 2
ALPHA = (2.0 * DEPTH) ** 0.25
LN_EPS = 1e-5
RMS_EPS = 1e-6
ADAM_LR, ADAM_B1, ADAM_B2, ADAM_EPS, ADAM_WD, ADAM_STEP = 0.001, 0.9, 0.999, 1e-08, 0.01, 10
N_CHIPS = 4
N_DEV = 8
VMEM_LIMIT = 56 * 1024 * 1024
NEG = -1e30


def _pick(n, cap):
    best = None
    for d in range(LANES, min(n, cap) + 1, LANES):
        if n % d == 0:
            best = d
    return n if best is None else best


def _pick_rows(m, cap):
    best = None
    for d in range(16, min(m, cap) + 1, 16):
        if m % d == 0:
            best = d
    return m if best is None else best


def _params(sem):
    return pltpu.CompilerParams(dimension_semantics=sem, vmem_limit_bytes=VMEM_LIMIT)


def _zeros_index(ndim, grid_rank=3):
    return (lambda i, j, kk: (0,) * ndim) if grid_rank == 3 else (lambda kk, i: (0,) * ndim)


def _mm(a, b, *, name, la=None, lb=None, ta=False, tb=False, bias=None, add=None, out_dtype=F32,
        out_layers=None, out_layer=None, after=None, post=None, tile_cols=None, caps=(1024, 1536, 2048),
        a_parts=None, b_parts=None):
    ar, ac = a.shape[-2:]
    br, bc = b.shape[-2:]
    assert a_parts is None or (not ta and la is None and a.shape[0] == a_parts)
    assert b_parts is None or (not tb and lb is None and b.shape[0] == b_parts)
    m, k = (ac, ar) if ta else (ar, ac * (a_parts or 1))
    k2, n = (bc, br) if tb else (br, bc * (b_parts or 1))
    assert k == k2, (a.shape, b.shape, ta, tb)
    if post is not None:
        caps = (512, n if tile_cols is None else tile_cols, caps[2])
    tm, tn, tk = _pick(m, caps[0]), _pick(bc if b_parts else n, caps[1]), _pick(ac if a_parts else k, caps[2])
    assert post is None or tn == caps[1]
    nk = k // tk
    gi, gj = m // tm, n // tn
    a_bytes, b_bytes = m * k * a.dtype.itemsize, k * n * b.dtype.itemsize
    rows_outer = (a_bytes + b_bytes * (gi if gj * nk > 1 else 1)) <= (b_bytes + a_bytes * (gj if gi * nk > 1 else 1))
    k_outer = post is not None and nk > 1 and gj == 1
    grid = (nk, gi) if k_outer else (gi, gj, nk) if rows_outer else (gj, gi, nk)
    keep_at = ta and nk == 1 and gj > 1 and rows_outer

    def bs(block, idx, late=False):
        if k_outer:
            return pl.BlockSpec(block, lambda kk, i: idx(jnp.where(kk == nk - 1, i, 0) if late else i, 0, kk))
        return pl.BlockSpec(block, idx if rows_outer else (lambda q, p, kk: idx(p, q, kk)))

    def spec(block, idx, layer):
        if layer is None:
            return bs(block, idx)
        return bs((None,) + block, lambda i, j, kk: (layer,) + idx(i, j, kk))

    a_spec = spec((tk, tm), lambda i, j, kk: (kk, i), la) if ta else spec((tm, tk), lambda i, j, kk: (i, kk), la)
    b_spec = spec((tn, tk), lambda i, j, kk: (j, kk), lb) if tb else spec((tk, tn), lambda i, j, kk: (kk, j), lb)
    if a_parts:
        a_spec = bs((None, tm, tk), lambda i, j, kk: (kk // (ac // tk), i, kk % (ac // tk)))
    if b_parts:
        b_spec = bs((None, tk, tn), lambda i, j, kk: (j // (bc // tn), kk, j % (bc // tn)))
    in_specs, operands = [a_spec, b_spec], [a, b]
    if bias is not None:
        in_specs.append(bs((1, tn), lambda i, j, kk: (0, j)))
        operands.append(bias)
    if add is not None:
        in_specs.append(bs((tm, tn), lambda i, j, kk: (i, j), late=True))
        operands.append(add)
    if after is not None:
        in_specs.append(pl.BlockSpec(memory_space=pl.ANY))
        operands.append(after)
    dims = (((0 if ta else 1,), (1 if tb else 0,)), ((), ()))
    has_bias, has_add = bias is not None, add is not None
    if post is None:
        fn, rows, whole, outs, sums = None, [], [], [], []
        out_shape = jax.ShapeDtypeStruct((m, n) if out_layers is None else (out_layers, m, n), out_dtype)
        out_specs = spec((tm, tn), lambda i, j, kk: (i, j), out_layer)
    else:
        fn, rows, whole, outs, sums = post
        in_specs += [bs((tm, r.shape[-1] // gj), lambda i, j, kk: (i, j), late=True) for r in rows]
        in_specs += [pl.BlockSpec(tuple(w.shape), _zeros_index(w.ndim, len(grid))) for w in whole]
        operands += list(rows) + list(whole)
        out_shape = [jax.ShapeDtypeStruct(sh, dt) for sh, dt in list(outs) + list(sums)]
        out_specs = ([bs((tm, sh[-1] // gj), lambda i, j, kk: (i, j), late=True) for sh, _ in outs]
                     + [pl.BlockSpec(tuple(sh), _zeros_index(len(sh), len(grid))) for sh, _ in sums])
    n_in, n_extra, n_outs, n_sums = len(operands), len(rows) + len(whole), len(outs), len(sums)

    def body(*refs):
        a_ref, b_ref = refs[0], refs[1]
        pos = 2
        bias_ref = add_ref = None
        if has_bias:
            bias_ref = refs[pos]
            pos += 1
        if has_add:
            add_ref = refs[pos]
            pos += 1
        extra_refs = refs[n_in - n_extra:n_in]
        out_refs = refs[n_in:n_in + max(n_outs, 1)]
        sum_refs = refs[n_in + n_outs:n_in + n_outs + n_sums]
        acc_ref = refs[-1] if nk > 1 else None
        if keep_at:
            at_ref = refs[-1]

            @pl.when(pl.program_id(1) == 0)
            def _():
                at_ref[...] = a_ref[...].astype(BF16).T

            part = lax.dot_general(at_ref[...], b_ref[...].astype(BF16), (((1,), (1 if tb else 0,)), ((), ())),
                                   preferred_element_type=F32)
        else:
            part = lax.dot_general(a_ref[...].astype(BF16), b_ref[...].astype(BF16), dims,
                                   preferred_element_type=F32)

        def finish(total):
            if has_bias:
                total = total + bias_ref[...]
            if has_add:
                total = total + add_ref[...]
            if fn is None:
                out_refs[0][...] = total.astype(out_refs[0].dtype)
                return
            res = fn(total, *[r[...] for r in extra_refs])
            for ref, val in zip(out_refs, res[:n_outs]):
                ref[...] = val.astype(ref.dtype)
            if n_sums:
                @pl.when(pl.program_id(1 if k_outer or not rows_outer else 0) == 0)
                def _():
                    for ref in sum_refs:
                        ref[...] = jnp.zeros(ref.shape, ref.dtype)

                for ref, val in zip(sum_refs, res[n_outs:]):
                    ref[...] += val

        if nk == 1:
            finish(part)
        elif k_outer:
            kk = pl.program_id(0)
            rows_i = pl.ds(pl.multiple_of(pl.program_id(1) * tm, tm), tm)

            @pl.when(kk == 0)
            def _():
                acc_ref[rows_i, :] = part

            @pl.when(kk > 0)
            def _():
                acc_ref[rows_i, :] += part

            @pl.when(kk == nk - 1)
            def _():
                finish(acc_ref[rows_i, :])
        else:
            kk = pl.program_id(2)

            @pl.when(kk == 0)
            def _():
                acc_ref[...] = part

            @pl.when(kk > 0)
            def _():
                acc_ref[...] += part

            @pl.when(kk == nk - 1)
            def _():
                finish(acc_ref[...])

    return pl.pallas_call(
        body, name=name, grid=grid, in_specs=in_specs, out_specs=out_specs, out_shape=out_shape,
        scratch_shapes=([pltpu.VMEM((m, n) if k_outer else (tm, tn), F32)] if nk > 1
                        else [pltpu.VMEM((tm, tk), BF16)] if keep_at else []),
        compiler_params=_params(("arbitrary", "arbitrary") if k_outer
                                else ("arbitrary" if n_sums else "parallel", "arbitrary" if keep_at else "parallel",
                                      "arbitrary") if rows_outer
                                else ("parallel", "arbitrary" if n_sums else "parallel", "arbitrary")),
    )(*operands)


def _rowwise(fn, rows, whole, outs, sums=(), *, name, tm=256):
    m = rows[0].shape[-2]
    tm = _pick_rows(m, tm)
    n_rows, n_whole, n_outs, n_sums = len(rows), len(whole), len(outs), len(sums)

    def rspec(shape):
        lead = len(shape) - 2
        return pl.BlockSpec(tuple(shape[:-2]) + (tm, shape[-1]), lambda i: (0,) * lead + (i, 0))

    def wspec(shape):
        return pl.BlockSpec(tuple(shape), lambda i: (0,) * len(shape))

    def body(*refs):
        vals = [r[...] for r in refs[:n_rows + n_whole]]
        out_refs = refs[n_rows + n_whole:n_rows + n_whole + n_outs]
        sum_refs = refs[n_rows + n_whole + n_outs:]
        res = fn(*vals)
        for ref, val in zip(out_refs, res[:n_outs]):
            ref[...] = val.astype(ref.dtype)
        if n_sums:
            @pl.when(pl.program_id(0) == 0)
            def _():
                for ref in sum_refs:
                    ref[...] = jnp.zeros(ref.shape, ref.dtype)

            for ref, val in zip(sum_refs, res[n_outs:]):
                ref[...] += val

    result = pl.pallas_call(
        body, name=name, grid=(m // tm,),
        in_specs=[rspec(r.shape) for r in rows] + [wspec(w.shape) for w in whole],
        out_specs=[rspec(s) for s, _ in outs] + [wspec(s) for s, _ in sums],
        out_shape=[jax.ShapeDtypeStruct(s, d) for s, d in list(outs) + list(sums)],
        compiler_params=_params(("arbitrary",)),
    )(*rows, *whole)
    return result


def _sigmoid(v):
    return jax.nn.sigmoid(v)


def _col_sum(v):
    return jnp.sum(v, axis=0, keepdims=True)


def _ln_stats(z):
    mu = jnp.mean(z, axis=-1, keepdims=True)
    zc = z - mu
    var = jnp.mean(zc * zc, axis=-1, keepdims=True)
    rstd = lax.rsqrt(var + LN_EPS)
    return zc * rstd, rstd


def _ln_fwd_fn(xin, h, gain, bias):
    xhat, _ = _ln_stats(ALPHA * xin + h)
    y = xhat * gain + bias
    return y, y


def _ple_ln_fwd_fn(xin, pg, pu, gain, bias):
    xhat, _ = _ln_stats(ALPHA * xin + _sigmoid(pg) * pu)
    y = xhat * gain + bias
    return y, y


def _ln_dz(dy, z, gain):
    xhat, rstd = _ln_stats(z)
    dxhat = dy * gain
    dz = rstd * (dxhat - jnp.mean(dxhat, axis=-1, keepdims=True)
                 - xhat * jnp.mean(dxhat * xhat, axis=-1, keepdims=True))
    return dz, _col_sum(dy * xhat), _col_sum(dy)


def _ln_bwd_fn(dy, xin, h, gain):
    dz, dgain, dbias = _ln_dz(dy, ALPHA * xin + h, gain)
    return ALPHA * dz, dz, dgain, dbias, _col_sum(dz)


def _ple_ln_bwd_fn(dy, xin, pg, pu, gain):
    sg = _sigmoid(pg)
    dz, dgain, dbias = _ln_dz(dy, ALPHA * xin + sg * pu, gain)
    dpg = dz * pu * sg * (1.0 - sg)
    return ALPHA * dz, dpg, dz * sg, dgain, dbias, _col_sum(dpg)


def _swiglu_fwd_fn(gu):
    hid = gu.shape[-1] // 2
    gate, up = gu[:, :hid], gu[:, hid:]
    return gu, gate * _sigmoid(gate) * up


def _swiglu_bwd_fn(dact, gu):
    gu = gu.astype(F32)
    hid = gu.shape[-1] // 2
    gate, up = gu[:, :hid], gu[:, hid:]
    sg = _sigmoid(gate)
    dgate = dact * up * sg * (1.0 + gate * (1.0 - sg))
    dup = dact * gate * sg
    return (jnp.concatenate([dgate, dup], axis=-1),)


def _loss_fn(y, target):
    err = y - target
    inv = 1.0 / y.shape[-1]
    part = 0.5 * inv * jnp.sum(jnp.sum(err * err, axis=-1, keepdims=True), axis=0, keepdims=True)
    return err * inv, jnp.broadcast_to(part, (1, LANES))


def _adam_fn(w, mom, vel, p_own, p_sib):
    g = p_own.astype(F32) + p_sib.astype(F32)
    m_new = ADAM_B1 * mom + (1.0 - ADAM_B1) * g
    v_new = ADAM_B2 * vel + (1.0 - ADAM_B2) * (g * g)
    m_hat = m_new / (1.0 - ADAM_B1 ** ADAM_STEP)
    v_hat = v_new / (1.0 - ADAM_B2 ** ADAM_STEP)
    delta = -ADAM_LR * (m_hat / (jnp.sqrt(v_hat) + ADAM_EPS) + ADAM_WD * w)
    return g, delta, m_new, v_new


def _split2(x):
    hi = x.astype(BF16)
    return hi, (x - hi.astype(F32)).astype(BF16)


def _dot3(a, b, dims):
    a_hi, a_lo = _split2(a)
    b_hi, b_lo = _split2(b)
    dn = (dims, ((), ()))
    return (lax.dot_general(a_hi, b_hi, dn, preferred_element_type=F32)
            + (lax.dot_general(a_hi, b_lo, dn, preferred_element_type=F32)
               + lax.dot_general(a_lo, b_hi, dn, preferred_element_type=F32)))


def _tdot(mask01, b):
    m = mask01.astype(BF16)
    b_hi = b.astype(BF16)
    rest = b - b_hi.astype(F32)
    b_mid = rest.astype(BF16)
    b_lo = (rest - b_mid.astype(F32)).astype(BF16)
    dn = (((1,), (0,)), ((), ()))
    return (lax.dot_general(m, b_hi, dn, preferred_element_type=F32)
            + (lax.dot_general(m, b_mid, dn, preferred_element_type=F32)
               + lax.dot_general(m, b_lo, dn, preferred_element_type=F32)))


def _hdot(a, b):
    return _dot3(a, b, ((1,), (0,)))


def _hdot_nt(a, b):
    return _dot3(a, b, ((1,), (1,)))


def _hdot_tn(a, b):
    return _dot3(a, b, ((0,), (0,)))


def _dot(a, b):
    return lax.dot_general(a.astype(BF16), b.astype(BF16), (((1,), (0,)), ((), ())), preferred_element_type=F32)


def _dot_nt(a, b):
    return lax.dot_general(a.astype(BF16), b.astype(BF16), (((1,), (1,)), ((), ())), preferred_element_type=F32)


def _dot_tn(a, b):
    return lax.dot_general(a.astype(BF16), b.astype(BF16), (((0,), (0,)), ((), ())), preferred_element_type=F32)


def _hg_masks():
    c = HG_CHUNK
    row = lax.broadcasted_iota(jnp.int32, (c, c), 0)
    col = lax.broadcasted_iota(jnp.int32, (c, c), 1)
    base = row & (-HG_SUB)
    return row, col, base, col <= row, col < base


def _hg_gates(qr, fr, alb):
    lbound = _sigmoid(alb[0:1, :] - alb[1:2, :])
    sig = _sigmoid(fr)
    forget = lbound + (1.0 - lbound) * sig
    kk = (1.0 - lbound) * _sigmoid(-fr)
    qt = qr * _sigmoid(qr) * (HG_DK ** -0.5)
    return qt, kk, jnp.log(forget), lbound, sig, forget


def _hg_scores(qt, kk, g, scores=True):
    c, nsub = HG_CHUNK, HG_CHUNK // HG_SUB
    row, col, base, causal, below = _hg_masks()
    b = _tdot(causal, g)
    rr = _tdot(below, g)
    bq = b - rr
    qh = qt * jnp.exp(bq)
    edecs = [None]
    parts = [jnp.zeros((HG_SUB, c), F32)]
    for i in range(1, nsub):
        edec = jnp.exp(jnp.minimum(rr[i * HG_SUB:i * HG_SUB + 1, :] - b, 0.0))
        edecs.append(edec)
        if scores:
            parts.append(_dot_nt(qh[i * HG_SUB:(i + 1) * HG_SUB, :], kk * edec))
    q3 = qt.reshape(nsub, HG_SUB, HG_DK)
    if not scores:
        return None, b, bq, qh, edecs, (b.reshape(nsub, HG_SUB, HG_DK), q3, kk.reshape(nsub, HG_SUB, HG_DK))
    a = jnp.where(below, jnp.concatenate(parts, axis=0), 0.0)
    b2 = b * LOG2_E
    b3 = b2.reshape(nsub, HG_SUB, HG_DK)
    c3 = (b2 - jnp.log2(kk)).reshape(nsub, HG_SUB, HG_DK)
    for j in range(HG_SUB):
        ek = jnp.exp2(b3 - c3[:, j:j + 1, :])
        colv = jnp.sum(q3 * ek, axis=-1, keepdims=True).reshape(c, 1)
        a = jnp.where(col == base + j, colv, a)
    a = jnp.where(causal, a, 0.0)
    return a, b, bq, qh, edecs, None


def _hg_norm(o, gr, gain):
    r = lax.rsqrt(jnp.mean(o * o, axis=-1, keepdims=True) + RMS_EPS)
    sg = _sigmoid(gr)
    return o * r * gain, r, sg


def _hgrn2_fwd(proj, alb, gain, *, rb):
    m, d4 = proj.shape
    d = d4 // 4
    heads = d // HG_DK
    hp = HG_HEADS_PER_STEP
    rb = min(rb, m)
    cpb = rb // HG_CHUNK
    nrb = m // rb

    def body(q_ref, f_ref, v_ref, g_ref, alb_ref, gain_ref, o_ref, og_ref, st_ref, a_ref, state):
        @pl.when(pl.program_id(1) == 0)
        def _():
            state[...] = jnp.zeros(state.shape, F32)

        def chunk(ci, carry):
            sl = pl.ds(pl.multiple_of(ci * HG_CHUNK, HG_CHUNK), HG_CHUNK)
            for u in range(hp):
                ln = slice(u * HG_DK, (u + 1) * HG_DK)
                qt, kk, g, _, _, _ = _hg_gates(q_ref[sl, ln], f_ref[sl, ln], alb_ref[:, ln])
                v = v_ref[sl, ln]
                st = state[u]
                st_ref[u, ci] = st
                a, b, _, _, _, _ = _hg_scores(qt, kk, g)
                a_ref[u, ci] = a.astype(a_ref.dtype)
                o = _dot(a, v) + _dot_nt(qt * jnp.exp(b), st)
                b_last = b[HG_CHUNK - 1:HG_CHUNK, :]
                state[u] = st * jnp.exp(b_last) + _hdot_tn(v, kk * jnp.exp(b_last - b))
                o_ref[sl, ln] = o
                n, _, sg = _hg_norm(o, g_ref[sl, ln], gain_ref[...])
                og_ref[sl, ln] = (n * g_ref[sl, ln] * sg).astype(og_ref.dtype)
            return carry

        lax.fori_loop(0, cpb, chunk, 0, unroll=2)

    def col(cidx):
        return pl.BlockSpec((rb, hp * HG_DK), lambda h, r: (r, cidx * (heads // hp) + h))

    return pl.pallas_call(
        body, name="hgrn2_fwd", grid=(heads // hp, nrb),
        in_specs=[col(0), col(1), col(2), col(3),
                  pl.BlockSpec((2, hp * HG_DK), lambda h, r: (0, h)),
                  pl.BlockSpec((1, HG_DK), lambda h, r: (0, 0))],
        out_specs=[pl.BlockSpec((rb, hp * HG_DK), lambda h, r: (r, h)),
                   pl.BlockSpec((rb, hp * HG_DK), lambda h, r: (r, h)),
                   pl.BlockSpec((hp, cpb, HG_DK, HG_DK), lambda h, r: (h, r, 0, 0)),
                   pl.BlockSpec((hp, cpb, HG_CHUNK, HG_CHUNK), lambda h, r: (h, r, 0, 0))],
        out_shape=[jax.ShapeDtypeStruct((m, d), F32), jax.ShapeDtypeStruct((m, d), BF16),
                   jax.ShapeDtypeStruct((heads, m // HG_CHUNK, HG_DK, HG_DK), F32),
                   jax.ShapeDtypeStruct((heads, m // HG_CHUNK, HG_CHUNK, HG_CHUNK), BF16)],
        scratch_shapes=[pltpu.VMEM((hp, HG_DK, HG_DK), F32)],
        compiler_params=_params(("parallel", "arbitrary")),
    )(proj, proj, proj, proj, alb, gain)


def _hgrn2_bwd(proj, o_pre, states, scores, dog, alb, gain, *, rb):
    m, d4 = proj.shape
    d = d4 // 4
    heads = d // HG_DK
    rb = min(rb, m)
    cpb = rb // HG_CHUNK
    nrb = m // rb
    c, nsub = HG_CHUNK, HG_CHUNK // HG_SUB

    def body(q_ref, f_ref, v_ref, g_ref, o_ref, st_ref, a_ref, dog_ref, alb_ref, gain_ref,
             dp_ref, dalb_ref, dgain_ref, dstate, carry_ref):
        first = (pl.program_id(0) == 0) & (pl.program_id(1) == 0)

        @pl.when(first)
        def _():
            dgain_ref[...] = jnp.zeros(dgain_ref.shape, F32)

        @pl.when(pl.program_id(1) == 0)
        def _():
            dstate[...] = jnp.zeros(dstate.shape, F32)
            carry_ref[...] = jnp.zeros(carry_ref.shape, F32)
            dalb_ref[...] = jnp.zeros(dalb_ref.shape, F32)

        row, col, base, causal, below = _hg_masks()
        sub_iota = lax.broadcasted_iota(jnp.int32, (nsub, HG_SUB, HG_DK), 1)
        row_k = lax.broadcasted_iota(jnp.int32, (c, HG_DK), 0)
        upper = col >= row

        def chunk(step, carry):
            ci = cpb - 1 - step
            sl = pl.ds(pl.multiple_of(ci * HG_CHUNK, HG_CHUNK), HG_CHUNK)
            qr, fr, v, gr = q_ref[sl, :], f_ref[sl, :], v_ref[sl, :], g_ref[sl, :]
            qt, kk, g, lbound, sig, forget = _hg_gates(qr, fr, alb_ref[...])
            o = o_ref[sl, :]
            dogv = dog_ref[sl, :]
            gain_v = gain_ref[...]
            n, r, sg = _hg_norm(o, gr, gain_v)
            dgr = dogv * n * sg * (1.0 + gr * (1.0 - sg))
            dn = dogv * gr * sg
            dgain_ref[...] += _col_sum(dn * o * r)
            u = dn * gain_v
            d_o = r * u - o * (r * r * r) * jnp.mean(u * o, axis=-1, keepdims=True)
            st0 = st_ref[ci]
            dst = dstate[...]
            _, b, bq, qh, edecs, (b3, q3, k3) = _hg_scores(qt, kk, g, scores=False)
            a = a_ref[ci]
            eb = jnp.exp(b)
            b_last = b[c - 1:c, :]
            kdl_dec = jnp.exp(b_last - b)
            kdl = kk * kdl_dec
            d_a = jnp.where(causal, _dot_nt(d_o, v), 0.0)
            d_at = _dot_nt(v, d_o)
            dv = _dot_tn(a, d_o) + _dot_nt(kdl, dst)
            dq = eb * _hdot(d_o, st0)
            dk = _hdot(v, dst) * kdl_dec
            d_a_below = jnp.where(below, d_a, 0.0)
            dq_parts = [jnp.zeros((HG_SUB, HG_DK), F32)]
            for i in range(1, nsub):
                lo, hi = i * HG_SUB, (i + 1) * HG_SUB
                dq_parts.append(_hdot(d_a_below[lo:hi, :], kk * edecs[i]))
                gi = _hdot(d_at[:, lo:hi], qh[lo:hi, :])
                dk = dk + jnp.where(row_k < lo, edecs[i] * gi, 0.0)
            dq = dq + jnp.concatenate(dq_parts, axis=0) * jnp.exp(bq)
            d_diag = jnp.concatenate([d_a[i * HG_SUB:(i + 1) * HG_SUB, i * HG_SUB:(i + 1) * HG_SUB]
                                      for i in range(nsub)], axis=0).reshape(nsub, HG_SUB, HG_SUB)
            dq3_halves, dk3_halves = [], []
            for half in (slice(0, nsub // 2), slice(nsub // 2, nsub)):
                b3h, q3h, k3h, ddh, iota_h = b3[half], q3[half], k3[half], d_diag[half], sub_iota[half]
                dq3 = jnp.zeros(b3h.shape, F32)
                dk3 = jnp.zeros(b3h.shape, F32)
                for j in range(HG_SUB):
                    e = jnp.exp(jnp.minimum(b3h - b3h[:, j:j + 1, :], 0.0))
                    t1 = ddh[:, :, j:j + 1] * e
                    dq3 = dq3 + t1 * k3h[:, j:j + 1, :]
                    dk3 = jnp.where(iota_h == j, jnp.sum(t1 * q3h, axis=1, keepdims=True), dk3)
                dq3_halves.append(dq3)
                dk3_halves.append(dk3)
            dq = dq + jnp.concatenate(dq3_halves, axis=0).reshape(c, HG_DK)
            dk = dk + jnp.concatenate(dk3_halves, axis=0).reshape(c, HG_DK)
            dstate[...] = dst * jnp.exp(b_last) + _hdot_tn(d_o, qt * eb)
            dglog = _tdot(upper, qt * dq - kk * dk) + carry_ref[...]
            carry_ref[...] = dglog[0:1, :]
            dforget = dglog / forget
            one_m_lb = 1.0 - lbound
            dsig = (dforget - dk) * one_m_lb
            sneg = _sigmoid(-fr)
            dlb = _col_sum(dforget * (1.0 - sig) - dk * sneg)
            dalb0 = dlb * lbound * one_m_lb
            dalb_ref[...] += jnp.concatenate([dalb0, -dalb0], axis=0)
            sq = _sigmoid(qr)
            dp_ref[0, sl, :] = (dq * (HG_DK ** -0.5) * sq * (1.0 + qr * (1.0 - sq))).astype(dp_ref.dtype)
            dp_ref[1, sl, :] = (dsig * sig * (1.0 - sig)).astype(dp_ref.dtype)
            dp_ref[2, sl, :] = dv.astype(dp_ref.dtype)
            dp_ref[3, sl, :] = dgr.astype(dp_ref.dtype)
            return carry

        lax.fori_loop(0, cpb, chunk, 0, unroll=2)

    def rev(r):
        return nrb - 1 - r

    def col(cidx):
        return pl.BlockSpec((rb, HG_DK), lambda h, r: (rev(r), cidx * heads + h))

    def head_rows():
        return pl.BlockSpec((rb, HG_DK), lambda h, r: (rev(r), h))

    return pl.pallas_call(
        body, name="hgrn2_bwd", grid=(heads, nrb),
        in_specs=[col(0), col(1), col(2), col(3), head_rows(),
                  pl.BlockSpec((None, cpb, HG_DK, HG_DK), lambda h, r: (h, rev(r), 0, 0)),
                  pl.BlockSpec((None, cpb, HG_CHUNK, HG_CHUNK), lambda h, r: (h, rev(r), 0, 0)),
                  head_rows(),
                  pl.BlockSpec((2, HG_DK), lambda h, r: (0, h)),
                  pl.BlockSpec((1, HG_DK), lambda h, r: (0, 0))],
        out_specs=[pl.BlockSpec((4, rb, HG_DK), lambda h, r: (0, rev(r), h)),
                   pl.BlockSpec((2, HG_DK), lambda h, r: (0, h)),
                   pl.BlockSpec((1, HG_DK), lambda h, r: (0, 0))],
        out_shape=[jax.ShapeDtypeStruct((4, m, d), BF16), jax.ShapeDtypeStruct((2, d), F32),
                   jax.ShapeDtypeStruct((1, HG_DK), F32)],
        scratch_shapes=[pltpu.VMEM((HG_DK, HG_DK), F32), pltpu.VMEM((1, HG_DK), F32)],
        compiler_params=_params(("arbitrary", "arbitrary")),
    )(proj, proj, proj, proj, o_pre, states, scores, dog, alb, gain)


def _swa_probs(qh, kp, kc, sink, slope, has_prev, lse=None):
    rows = qh.shape[0]
    qi = lax.broadcasted_iota(jnp.int32, (rows, WINDOW), 0) & (WINDOW - 1)
    si = lax.broadcasted_iota(jnp.int32, (rows, WINDOW), 1)
    scale = ATT_HD ** -0.5
    dist_c = (qi - si).astype(F32)
    s_p = _dot_nt(qh, kp) * scale - slope * (dist_c + float(WINDOW))
    s_c = _dot_nt(qh, kc) * scale - slope * dist_c
    s_p = jnp.where((si > qi) & has_prev, s_p, NEG)
    s_c = jnp.where(si <= qi, s_c, NEG)
    if lse is not None:
        return jnp.exp(s_p - lse), jnp.exp(s_c - lse), jnp.exp(sink - lse), lse
    mx = jnp.maximum(jnp.maximum(jnp.max(s_p, axis=-1, keepdims=True), jnp.max(s_c, axis=-1, keepdims=True)), sink)
    e_p, e_c, e_s = jnp.exp(s_p - mx), jnp.exp(s_c - mx), jnp.exp(sink - mx)
    total = jnp.sum(e_p, axis=-1, keepdims=True) + jnp.sum(e_c, axis=-1, keepdims=True) + e_s
    inv = 1.0 / total
    return e_p * inv, e_c * inv, e_s * inv, mx + jnp.log(total)


def _slope(h, n_heads):
    return float(2.0 ** (-8.0 * (h + 1) / n_heads))


def _swa_group(ref_vals, sink_ref, kh, n_heads):
    heads = [kh * ATT_G + g for g in range(ATT_G)]
    stacked = [jnp.concatenate([v[:, h * ATT_HD:(h + 1) * ATT_HD] for h in heads], axis=0) for v in ref_vals]
    grp = lax.shift_right_logical(lax.broadcasted_iota(jnp.int32, (ATT_G * WINDOW, 1), 0), WINDOW.bit_length() - 1)
    slope = jnp.zeros((ATT_G * WINDOW, 1), F32)
    sink = jnp.zeros((ATT_G * WINDOW, 1), F32)
    for g, h in enumerate(heads):
        slope = jnp.where(grp == g, _slope(h, n_heads), slope)
        sink = jnp.where(grp == g, sink_ref[:, h:h + 1], sink)
    return stacked, slope, sink


def _swa_fwd(q, kv, sinks):
    m, d = q.shape
    n_heads = d // ATT_HD
    kvh = n_heads // ATT_G
    kd = kvh * ATT_HD
    nb = m // WINDOW

    def body(q_ref, kvp_ref, kvc_ref, sink_ref, o_ref, lse_ref):
        has_prev = pl.program_id(0) > 0
        qv, kvp, kvc = q_ref[...], kvp_ref[...], kvc_ref[...]
        lane_h = lax.broadcasted_iota(jnp.int32, (WINDOW, n_heads), 1)
        outs, lse_all = [], jnp.zeros((WINDOW, n_heads), F32)
        for kh in range(kvh):
            ks = slice(kh * ATT_HD, (kh + 1) * ATT_HD)
            vs = slice(kd + kh * ATT_HD, kd + (kh + 1) * ATT_HD)
            (q4,), slope, sink = _swa_group([qv], sink_ref, kh, n_heads)
            p_p, p_c, _, lse = _swa_probs(q4, kvp[:, ks], kvc[:, ks], sink, slope, has_prev)
            o4 = _dot(p_p, kvp[:, vs]) + _dot(p_c, kvc[:, vs])
            for g in range(ATT_G):
                rows = slice(g * WINDOW, (g + 1) * WINDOW)
                outs.append(o4[rows, :])
                lse_all = jnp.where(lane_h == kh * ATT_G + g, lse[rows, :], lse_all)
        o_ref[...] = jnp.concatenate(outs, axis=-1).astype(o_ref.dtype)
        lse_ref[...] = lse_all

    return pl.pallas_call(
        body, name="swa_fwd", grid=(nb,),
        in_specs=[pl.BlockSpec((WINDOW, d), lambda n: (n, 0)),
                  pl.BlockSpec((WINDOW, 2 * kd), lambda n: (jnp.maximum(n - 1, 0), 0)),
                  pl.BlockSpec((WINDOW, 2 * kd), lambda n: (n, 0)),
                  pl.BlockSpec((1, n_heads), lambda n: (0, 0))],
        out_specs=[pl.BlockSpec((WINDOW, d), lambda n: (n, 0)), pl.BlockSpec((WINDOW, n_heads), lambda n: (n, 0))],
        out_shape=[jax.ShapeDtypeStruct((m, d), BF16), jax.ShapeDtypeStruct((m, n_heads), F32)],
        compiler_params=_params(("arbitrary",)),
    )(q, kv, kv, sinks)


def _swa_bwd(q, kv, sinks, lse, dao):
    m, d = q.shape
    n_heads = d // ATT_HD
    kvh = n_heads // ATT_G
    kd = kvh * ATT_HD
    nb = m // WINDOW
    scale = ATT_HD ** -0.5

    def body(q_ref, kvp_ref, kvc_ref, sink_ref, lse_ref, do_ref, dq_ref, dkvc_ref, dkvp_ref, dqsum_ref, dsink_ref):
        @pl.when(pl.program_id(0) == 0)
        def _():
            dqsum_ref[...] = jnp.zeros(dqsum_ref.shape, F32)
            dsink_ref[...] = jnp.zeros(dsink_ref.shape, F32)

        has_prev = pl.program_id(0) > 0
        qv, kvp, kvc, dov = q_ref[...], kvp_ref[...], kvc_ref[...], do_ref[...]
        lane_h = lax.broadcasted_iota(jnp.int32, (1, n_heads), 1)
        dsink = jnp.zeros((1, n_heads), F32)
        dq_parts, dk_p, dk_c, dv_p, dv_c = [], [], [], [], []
        for kh in range(kvh):
            ks = slice(kh * ATT_HD, (kh + 1) * ATT_HD)
            vs = slice(kd + kh * ATT_HD, kd + (kh + 1) * ATT_HD)
            kp, kc, vp, vc = kvp[:, ks], kvc[:, ks], kvp[:, vs], kvc[:, vs]
            (q4, do4), slope, sink = _swa_group([qv, dov], sink_ref, kh, n_heads)
            lse4 = jnp.concatenate([lse_ref[:, kh * ATT_G + g:kh * ATT_G + g + 1] for g in range(ATT_G)], axis=0)
            p_p, p_c, p_s, _ = _swa_probs(q4, kp, kc, sink, slope, has_prev, lse=lse4)
            dp_p, dp_c = _dot_nt(do4, vp), _dot_nt(do4, vc)
            delta = jnp.sum(p_p * dp_p, axis=-1, keepdims=True) + jnp.sum(p_c * dp_c, axis=-1, keepdims=True)
            ds_p, ds_c = p_p * (dp_p - delta), p_c * (dp_c - delta)
            sink_term = p_s * delta
            dq4 = (_dot(ds_p, kp) + _dot(ds_c, kc)) * scale
            for g in range(ATT_G):
                rows = slice(g * WINDOW, (g + 1) * WINDOW)
                dsink = dsink + jnp.where(lane_h == kh * ATT_G + g, -_col_sum(sink_term[rows, :]), 0.0)
                dq_parts.append(dq4[rows, :])
            dk_p.append(_dot_tn(ds_p, q4) * scale)
            dk_c.append(_dot_tn(ds_c, q4) * scale)
            dv_p.append(_dot_tn(p_p, do4))
            dv_c.append(_dot_tn(p_c, do4))
        dq = jnp.concatenate(dq_parts, axis=-1)
        dq_ref[...] = dq.astype(dq_ref.dtype)
        dqsum_ref[...] += _col_sum(dq)
        dsink_ref[...] += dsink
        dkvc_ref[...] = jnp.concatenate(dk_c + dv_c, axis=-1)
        dkvp_ref[...] = jnp.concatenate(dk_p + dv_p, axis=-1)

    return pl.pallas_call(
        body, name="swa_bwd", grid=(nb,),
        in_specs=[pl.BlockSpec((WINDOW, d), lambda n: (n, 0)),
                  pl.BlockSpec((WINDOW, 2 * kd), lambda n: (jnp.maximum(n - 1, 0), 0)),
                  pl.BlockSpec((WINDOW, 2 * kd), lambda n: (n, 0)),
                  pl.BlockSpec((1, n_heads), lambda n: (0, 0)),
                  pl.BlockSpec((WINDOW, n_heads), lambda n: (n, 0)),
                  pl.BlockSpec((WINDOW, d), lambda n: (n, 0))],
        out_specs=[pl.BlockSpec((WINDOW, d), lambda n: (n, 0)),
                   pl.BlockSpec((WINDOW, 2 * kd), lambda n: (n, 0)),
                   pl.BlockSpec((WINDOW, 2 * kd), lambda n: (n, 0)),
                   pl.BlockSpec((1, d), lambda n: (0, 0)),
                   pl.BlockSpec((1, n_heads), lambda n: (0, 0))],
        out_shape=[jax.ShapeDtypeStruct((m, d), BF16), jax.ShapeDtypeStruct((m, 2 * kd), F32),
                   jax.ShapeDtypeStruct((m, 2 * kd), F32), jax.ShapeDtypeStruct((1, d), F32),
                   jax.ShapeDtypeStruct((1, n_heads), F32)],
        compiler_params=_params(("arbitrary",)),
    )(q, kv, kv, sinks, lse, dao)


def _kv_grad_combine(dkv_cur, dkv_prev):
    m, w = dkv_cur.shape
    nb = m // WINDOW
    per = max(g for g in (1, 2, 4) if nb % g == 0)
    rows, steps = per * WINDOW, nb // per

    def body(cur_ref, same_ref, next_ref, o_ref, sum_ref):
        @pl.when(pl.program_id(0) == 0)
        def _():
            sum_ref[...] = jnp.zeros(sum_ref.shape, F32)

        after = jnp.where(pl.program_id(0) < steps - 1, next_ref[...], 0.0)
        total = cur_ref[...] + (jnp.concatenate([same_ref[WINDOW:, :], after], axis=0) if per > 1 else after)
        o_ref[...] = total.astype(o_ref.dtype)
        sum_ref[...] += _col_sum(total)

    return pl.pallas_call(
        body, name="kv_grad_combine", grid=(steps,),
        in_specs=[pl.BlockSpec((rows, w), lambda n: (n, 0)), pl.BlockSpec((rows, w), lambda n: (n, 0)),
                  pl.BlockSpec((WINDOW, w), lambda n: (jnp.minimum((n + 1) * per, nb - 1), 0))],
        out_specs=[pl.BlockSpec((rows, w), lambda n: (n, 0)), pl.BlockSpec((1, w), lambda n: (0, 0))],
        out_shape=[jax.ShapeDtypeStruct((m, w), BF16), jax.ShapeDtypeStruct((1, w), F32)],
        compiler_params=_params(("arbitrary",)),
    )(dkv_cur, dkv_prev, dkv_prev)


def _row(v):
    return v.reshape(1, -1)


def _local_step(x, p, target, wget, grad_sink, ln_gain, ln_bias, alb, norm_gain, kv_b, b_q, sinks, b_out, ple_b,
                small_sink=None):
    gs = {}
    gains = ln_gain.reshape(DEPTH * 3, -1)
    biases = ln_bias.reshape(DEPTH * 3, -1)
    sd = x.shape
    pending = [None]

    def mm(a, b, lb=0, **kw):
        after, pending[0] = pending[0], None
        return _mm(a, b, lb=lb, after=after, **kw)

    def mm_ln(a, wt, xin, i, j, nm, bias=None, pu=None):
        r = 3 * i + j
        if pu is None:
            fn, rows = (lambda h, xv, g, bv: (h,) + _ln_fwd_fn(xv, h, g[r:r + 1], bv[r:r + 1])), [xin]
        else:
            fn = lambda h, xv, puv, g, bv: (h,) + _ple_ln_fwd_fn(xv, h, puv, g[r:r + 1], bv[r:r + 1])
            rows = [xin, pu]
        h, y, yb = _mm(a, wt, lb=0, bias=bias, name=nm,
                       post=(fn, rows, [gains, biases], [(sd, F32), (sd, F32), (sd, BF16)], []))
        return h, (y, yb)

    def mm_ln_bwd(a, wt, add, xin, h, i, j, nm):
        r = 3 * i + j
        dx_part, dh, dg, db, dhsum = mm(a, wt, tb=True, add=add, name=nm,
                                        post=(lambda dy, xv, hv, g: _ln_bwd_fn(dy, xv, hv, g[r:r + 1]), [xin, h],
                                              [gains], [(sd, F32), (sd, BF16)], [((1, sd[1]), F32)] * 3))
        gs[f"ln_gain_{i}_{j}"], gs[f"ln_bias_{i}_{j}"] = dg, db
        return dx_part, dh, dhsum

    def tail_fwd(xa, i):
        wgu = wget("ffn_w_gate_up", i, xa[1])
        hid2 = wgu.shape[-1]
        gu, act = _mm(xa[1], wgu, lb=0, name=f"ffn_up_swiglu{i}", tile_cols=hid2 // 2,
                      post=(_swiglu_fwd_fn, [], [], [((sd[0], hid2), BF16), ((sd[0], hid2 // 2), BF16)], []))
        f, xb = mm_ln(act, wget("ffn_w_down", i, act), xa[0], i, 1, f"ffn_down_ln{i}")
        pu = _mm(p, wget("ple_w_up", i, act), la=i, lb=0, name=f"ple_up{i}")
        pg, xc = mm_ln(xb[1], wget("ple_w_gate", i, act), xb[0], i, 2, f"ple_gate_ln{i}", bias=_row(ple_b[i]), pu=pu)
        return dict(xa=xa, gu=gu, act=act, f=f, xb=xb, pg=pg, pu=pu), xc

    def tail_bwd(head, sv, i, mix_in, mix_h):
        xa, xb = sv["xa"], sv["xb"]
        r = 3 * i + 2
        dxb_part, dpg, dpu, dg2, db2, dbg = head(
            lambda dy, xv, pgv, puv, g: _ple_ln_bwd_fn(dy, xv, pgv, puv, g[r:r + 1]), [xb[0], sv["pg"], sv["pu"]],
            [gains], [(sd, F32), (sd, BF16), (sd, BF16)], [((1, sd[1]), F32)] * 3)[:6]
        gs[f"ple_b_{i}"] = dbg
        gs[f"ln_gain_{i}_2"], gs[f"ln_bias_{i}_2"] = dg2, db2
        grad_of("ple_w_gate", i, xb[1], dpg)
        grad_of("ple_w_up", i, p, dpu, la=i)
        dxa_part, df, _ = mm_ln_bwd(dpg, wget("ple_w_gate", i, None), dxb_part, xa[0], sv["f"], i, 1,
                                    f"ple_gate_dx_ln{i}")
        grad_of("ffn_w_down", i, sv["act"], df)
        gu = sv["gu"]
        dgu, = mm(df, wget("ffn_w_down", i, None), tb=True, name=f"ffn_down_dx_swiglu{i}", tile_cols=gu.shape[1] // 4,
                  post=(_swiglu_bwd_fn, [gu], [], [(gu.shape, BF16)], []))
        grad_of("ffn_w_gate_up", i, xa[1], dgu)
        return mm_ln_bwd(dgu, wget("ffn_w_gate_up", i, None), dxa_part, mix_in, mix_h, i, 0, f"ffn_up_dx_ln{i}")

    def grad_of(nm, i, act, dout, la=None, b_parts=None):
        grad = mm(act, dout, la=la, lb=None, ta=True, out_dtype=BF16, out_layers=1, out_layer=0,
                  name=f"grad_{nm}{i}", b_parts=b_parts)
        token = grad_sink(nm, i, grad)
        if token is not None:
            pending[0] = token

    proj = _mm(x, wget("a_w_in", 0, None), lb=0, name="hg_proj")
    o_pre, og, states, scores = _hgrn2_fwd(proj, alb, norm_gain, rb=HG_ROWS)
    h0, x1 = mm_ln(og, wget("a_w_out", 0, og), x, 0, 0, "hg_out_ln")
    sv0, x3 = tail_fwd(x1, 0)
    kv = _mm(x3[1], wget("kv_w", 0, x3[1]), lb=0, bias=_row(kv_b), out_dtype=BF16, name="kv_proj")
    q = _mm(x3[1], wget("b_w_q", 0, x3[1]), lb=0, bias=b_q, out_dtype=BF16, name="q_proj")
    ao, lse = _swa_fwd(q, kv, sinks)
    h1, x4 = mm_ln(ao, wget("b_w_out", 0, x3[1]), x3[0], 1, 0, "att_out_ln", bias=b_out)
    sv1, y = tail_fwd(x4, 1)

    loss_box = []

    def loss_head(fn, rows, whole, outs, sums):
        def with_loss(yv, tv, *rest):
            dy, part = _loss_fn(yv, tv)
            return fn(dy, *rest) + (part,)

        res = _rowwise(with_loss, [y[0], target] + rows, whole, outs, list(sums) + [((1, LANES), F32)],
                       name="loss_ln_ple_bwd1")
        loss_box.append(res[-1])
        return res

    dx3_part, dh1, dh1sum = tail_bwd(loss_head, sv1, 1, x3[0], h1)
    loss = loss_box[0]
    gs["b_out"] = dh1sum
    grad_of("b_w_out", 0, ao, dh1)
    dao = mm(dh1, wget("b_w_out", 0, None), tb=True, out_dtype=BF16, name="att_out_dx")
    dq, dkv_cur, dkv_prev, dqsum, dsinks = _swa_bwd(q, kv, sinks, lse, dao)
    gs["b_q"], gs["sinks"] = dqsum, dsinks
    dkv, dkvsum = _kv_grad_combine(dkv_cur, dkv_prev)
    gs["kv_b"] = dkvsum
    grad_of("b_w_q", 0, x3[1], dq)
    grad_of("kv_w", 0, x3[1], dkv)
    dx3 = mm(dq, wget("b_w_q", 0, None), tb=True, add=dx3_part, name="q_proj_dx")

    def kv_head(*post):
        return mm(dkv, wget("kv_w", 0, None), tb=True, add=dx3, name="kv_proj_dx_ln_ple_bwd0", post=post)

    dx_part, dh0, _ = tail_bwd(kv_head, sv0, 0, x, h0)
    grad_of("a_w_out", 0, og, dh0)
    dog = mm(dh0, wget("a_w_out", 0, None), tb=True, name="hg_out_dx")
    dproj, dalb, dgain = _hgrn2_bwd(proj, o_pre, states, scores, dog, alb, norm_gain, rb=HG_ROWS)
    gs["alb"], gs["norm_gain"] = dalb, dgain
    if small_sink is not None:
        pending[0] = small_sink(loss, gs)
    grad_of("a_w_in", 0, x, dproj, b_parts=4)
    grad_x = mm(dproj, wget("a_w_in", 0, None), tb=True, add=dx_part, name="hg_proj_dx", a_parts=4)
    return loss, grad_x, gs


HBM_SPEC = pl.BlockSpec(memory_space=pl.ANY)
HBM_ONLY = pl.BlockSpec(memory_space=pltpu.HBM)
SEM_SPEC = pl.BlockSpec(memory_space=pltpu.SEMAPHORE)
SIDE_EFFECT = pltpu.SideEffectType.DATAFLOW_SIDE_EFFECTING


def _slot(kind, j):
    return (j % 2) * 2 + j // 2 if kind == "colp" else j


def _piece(ref, kind, j):
    _, r, c = ref.shape
    if kind == "row":
        return ref.at[:, pl.ds(j * (r // N_CHIPS), r // N_CHIPS), :]
    return ref.at[:, :, pl.ds(_slot(kind, j) * (c // N_CHIPS), c // N_CHIPS)]


def _piece_dyn(ref, kind, j):
    _, r, c = ref.shape
    if kind == "row":
        return ref.at[:, pl.ds(pl.multiple_of(j * (r // N_CHIPS), 16), r // N_CHIPS), :]
    return ref.at[:, :, pl.ds(pl.multiple_of(_slot(kind, j) * (c // N_CHIPS), LANES), c // N_CHIPS)]


def _chip_of(j, c):
    return (j // 2, j % 2, c)


def _in_hbm(a):
    return pltpu.with_memory_space_constraint(a, pltpu.HBM)


PLACE_STEPS = 4


def _place(items, chip, *, name, after=None):
    n = len(items)
    in_specs, out_specs, out_shapes, blocks = [], [], [], []
    for src, layer, kind, out_dtype in items:
        _, r, c = src.shape
        nb = max(k for k in (1, 2, PLACE_STEPS) if r % (16 * k) == 0 or k == 1)
        blocks.append(nb)

        def src_idx(i, chip_ref, layer=layer, nb=nb):
            return (layer, jnp.minimum(i, nb - 1), 0)

        def full_idx(i, chip_ref, kind=kind, nb=nb):
            ib = jnp.minimum(i, nb - 1)
            return (0, chip_ref[0] * nb + ib, 0) if kind == "row" else (0, ib, _slot(kind, chip_ref[0]))

        in_specs.append(pl.BlockSpec((None, r // nb, c), src_idx))
        out_specs.append(pl.BlockSpec((None, r // nb, c), full_idx))
        out_shapes.append(jax.ShapeDtypeStruct((1, r * N_CHIPS, c) if kind == "row" else (1, r, c * N_CHIPS),
                                               out_dtype))
    operands = [it[0] for it in items]
    if after is not None:
        in_specs.append(HBM_SPEC)
        operands.append(after)

    def body(chip_ref, *refs):
        for a in range(n):
            refs[len(refs) - n + a][...] = refs[a][...].astype(refs[len(refs) - n + a].dtype)

    return pl.pallas_call(
        body, name=name,
        grid_spec=pltpu.PrefetchScalarGridSpec(num_scalar_prefetch=1, grid=(PLACE_STEPS,), in_specs=in_specs,
                                               out_specs=out_specs),
        out_shape=out_shapes,
        compiler_params=_params(("arbitrary",)),
    )(chip, *operands)


def _half(ref, c):
    h = ref.shape[1] // 2
    start = c * h if isinstance(c, int) else pl.multiple_of(c * h, 16)
    return ref.at[:, pl.ds(start, h), :]


def _sibling_handshake():
    barrier = pltpu.get_barrier_semaphore()
    sibling = (lax.axis_index("x"), lax.axis_index("y"), 1 - lax.axis_index("c"))
    pl.semaphore_signal(barrier, inc=1, device_id=sibling, device_id_type=MESH)
    pl.semaphore_wait(barrier, 1)


class _SiblingFill:
    def __init__(self, lands, kinds, name, collective_id):
        self.kinds, self.name, self.n = kinds, name, len(lands)
        n = self.n
        sem_shape = pltpu.SemaphoreType.DMA((n * N_CHIPS,))

        def body(*refs):
            land_refs, send_sems, recv_sems, token = refs[:n], refs[n], refs[n + 1], refs[-1]
            _sibling_handshake()
            for cp in self._copies(land_refs, send_sems, recv_sems):
                cp.start()
            token[...] = jnp.zeros(token.shape, token.dtype)

        outs = pl.pallas_call(
            body, name=name + "_start",
            in_specs=[HBM_ONLY] * n,
            out_specs=[SEM_SPEC, SEM_SPEC] + [HBM_ONLY] * n + [pl.BlockSpec(memory_space=pltpu.VMEM)],
            out_shape=[sem_shape, sem_shape] + [pltpu.HBM(a.shape, a.dtype) for a in lands]
                      + [jax.ShapeDtypeStruct((8, LANES), F32)],
            input_output_aliases={i: i + 2 for i in range(n)},
            compiler_params=pltpu.CompilerParams(has_side_effects=SIDE_EFFECT, collective_id=collective_id),
        )(*[_in_hbm(a) for a in lands])
        self.send_sems, self.recv_sems, self.lands, self.token = outs[0], outs[1], list(outs[2:2 + n]), outs[-1]

    def _copies(self, land_refs, send_sems, recv_sems):
        x, y, c = lax.axis_index("x"), lax.axis_index("y"), lax.axis_index("c")
        me = 2 * x + y
        copies = []
        for a in range(self.n):
            for k in range(1, N_CHIPS):
                t = (me + k) % N_CHIPS
                slice_t = _piece_dyn(land_refs[a], self.kinds[a], t)
                got = _half(slice_t, c)
                copies.append(pltpu.make_async_remote_copy(
                    src_ref=got, dst_ref=got, send_sem=send_sems.at[a * N_CHIPS + k],
                    recv_sem=recv_sems.at[a * N_CHIPS + k], device_id=(x, y, 1 - c), device_id_type=MESH))
        return copies

    def wait(self, after):
        n = self.n

        def body(*refs):
            land_refs, send_sems, recv_sems = refs[:n], refs[n], refs[n + 1]
            for cp in self._copies(land_refs, send_sems, recv_sems):
                cp.wait_send()
                cp.wait_recv()

        operands = [_in_hbm(a) for a in self.lands] + [self.send_sems, self.recv_sems]
        in_specs = [HBM_ONLY] * n + [SEM_SPEC, SEM_SPEC]
        if after is not None:
            operands.append(after)
            in_specs.append(HBM_SPEC)
        outs = pl.pallas_call(
            body, name=self.name + "_wait",
            in_specs=in_specs, out_specs=[HBM_ONLY] * n,
            out_shape=[pltpu.HBM(a.shape, a.dtype) for a in self.lands],
            input_output_aliases={i: i for i in range(n)},
            compiler_params=pltpu.CompilerParams(has_side_effects=SIDE_EFFECT),
        )(*operands)
        return list(outs)


class _Exchange:
    def __init__(self, mode, srcs, lands, kinds, layers, name, collective_id, after=None, halves=None):
        self.mode, self.kinds, self.layers, self.name, self.n = mode, kinds, layers, name, len(lands)
        self.halves = halves if halves is not None else [False] * len(lands)
        n, ns = self.n, len(srcs)
        n_in = ns + n + (after is not None)
        sem_shape = pltpu.SemaphoreType.DMA((n * N_CHIPS,))

        def body(*refs):
            src_refs, land_refs = refs[:ns], refs[ns:ns + n]
            send_sems, recv_sems = refs[n_in], refs[n_in + 1]
            token = refs[-1]
            c = lax.axis_index("c")
            me = 2 * lax.axis_index("x") + lax.axis_index("y")
            barrier = pltpu.get_barrier_semaphore()
            for k in range(1, N_CHIPS):
                t = (me + k) % N_CHIPS
                pl.semaphore_signal(barrier, inc=1, device_id=(t // 2, t % 2, c), device_id_type=MESH)
            pl.semaphore_wait(barrier, N_CHIPS - 1)
            for j in range(N_CHIPS):
                @pl.when(me == j)
                def _():
                    for a in range(n):
                        for t in range(N_CHIPS):
                            if t != j:
                                src, dst = self._ends(src_refs, land_refs, a, j, t, c)
                                pltpu.make_async_remote_copy(
                                    src_ref=src, dst_ref=dst, send_sem=send_sems.at[a * N_CHIPS + t],
                                    recv_sem=recv_sems.at[a * N_CHIPS + j],
                                    device_id=_chip_of(t, c), device_id_type=MESH).start()
            token[...] = jnp.zeros(token.shape, token.dtype)

        arrays = list(srcs) + list(lands)
        operands = [_in_hbm(a) for a in arrays]
        in_specs = [HBM_ONLY] * (ns + n)
        if after is not None:
            operands.append(after)
            in_specs.append(HBM_SPEC)
        outs = pl.pallas_call(
            body, name=name + "_start",
            in_specs=in_specs,
            out_specs=[SEM_SPEC, SEM_SPEC] + [HBM_ONLY] * (ns + n) + [pl.BlockSpec(memory_space=pltpu.VMEM)],
            out_shape=[sem_shape, sem_shape] + [pltpu.HBM(a.shape, a.dtype) for a in arrays]
                      + [jax.ShapeDtypeStruct((8, LANES), F32)],
            input_output_aliases={i: i + 2 for i in range(ns + n)},
            compiler_params=pltpu.CompilerParams(has_side_effects=SIDE_EFFECT, collective_id=collective_id),
        )(*operands)
        self.send_sems, self.recv_sems = outs[0], outs[1]
        self.srcs, self.lands = list(outs[2:2 + ns]), list(outs[2 + ns:2 + ns + n])
        self.token = outs[-1]

    def _ends(self, src_refs, land_refs, a, me_j, peer, c):
        if self.mode == "gather":
            mine = _piece(land_refs[a], self.kinds[a], me_j)
            if self.halves[a]:
                mine = _half(mine, c)
            return mine, mine
        return _piece(src_refs[a], self.kinds[a], peer), land_refs[a].at[me_j, pl.ds(self.layers[a], 1)]

    def wait(self, after, lands=None):
        n, ns = self.n, len(self.srcs)
        lands = self.lands if lands is None else lands

        def body(*refs):
            src_refs, land_refs = refs[:ns], refs[ns:ns + n]
            send_sems, recv_sems = refs[ns + n], refs[ns + n + 1]
            c = lax.axis_index("c")
            me = 2 * lax.axis_index("x") + lax.axis_index("y")
            for j in range(N_CHIPS):
                @pl.when(me != j)
                def _():
                    for a in range(n):
                        sent, _ = self._ends(src_refs, land_refs, a, 0, j, c)
                        _, landed = self._ends(src_refs, land_refs, a, j, 0, c)
                        cp = pltpu.make_async_remote_copy(
                            src_ref=sent, dst_ref=landed, send_sem=send_sems.at[a * N_CHIPS + j],
                            recv_sem=recv_sems.at[a * N_CHIPS + j],
                            device_id=_chip_of(j, c), device_id_type=MESH)
                        cp.wait_send()
                        cp.wait_recv()

        arrays = self.srcs + list(lands)
        operands = [_in_hbm(a) for a in arrays] + [self.send_sems, self.recv_sems]
        in_specs = [HBM_ONLY] * (ns + n) + [SEM_SPEC, SEM_SPEC]
        if after is not None:
            operands.append(after)
            in_specs.append(HBM_SPEC)
        outs = pl.pallas_call(
            body, name=self.name + "_wait",
            in_specs=in_specs, out_specs=[HBM_ONLY] * (ns + n),
            out_shape=[pltpu.HBM(a.shape, a.dtype) for a in arrays],
            input_output_aliases={i: i for i in range(ns + n)},
            compiler_params=pltpu.CompilerParams(has_side_effects=SIDE_EFFECT),
        )(*operands)
        return list(outs[:ns]), list(outs[ns:])


def _sum_arrivals(zone, own_grads, kind, chip, name, after=None):
    _, layers, r, c = zone.shape
    tm = _pick_rows(r, 512)
    if layers * (r // tm) == 1 and r % 32 == 0:
        tm = r // 2
    nb = r // tm

    def own_idx(u):
        def idx(l, i, chip_ref):
            ib = jnp.where(l == u, i, 0)
            return (0, chip_ref[0] * nb + ib, 0) if kind == "row" else (0, ib, _slot(kind, chip_ref[0]))
        return idx

    def slot_idx(k):
        return lambda l, i, chip_ref: ((chip_ref[0] + k) % N_CHIPS, l, i, 0)

    in_specs = [pl.BlockSpec((None, None, tm, c), slot_idx(k)) for k in range(1, N_CHIPS)]
    in_specs += [pl.BlockSpec((None, tm, c), own_idx(u)) for u in range(len(own_grads))]
    operands = [zone] * (N_CHIPS - 1) + list(own_grads)
    if after is not None:
        in_specs.append(HBM_SPEC)
        operands.append(after)

    def body(chip_ref, *refs):
        slot_refs, own_refs, o_ref = refs[:N_CHIPS - 1], refs[N_CHIPS - 1:N_CHIPS - 1 + layers], refs[-1]
        own = own_refs[0][...]
        for u in range(1, layers):
            own = jnp.where(pl.program_id(0) == u, own_refs[u][...], own)
        acc = own.astype(F32)
        for ref in slot_refs:
            acc = acc + ref[...].astype(F32)
        o_ref[...] = acc.astype(o_ref.dtype)

    return pl.pallas_call(
        body, name=name,
        grid_spec=pltpu.PrefetchScalarGridSpec(
            num_scalar_prefetch=1, grid=(layers, nb), in_specs=in_specs,
            out_specs=pl.BlockSpec((tm, c), lambda l, i, chip_ref: (l * nb + i, 0))),
        out_shape=jax.ShapeDtypeStruct((layers * r, c), BF16),
        compiler_params=_params(("arbitrary", "arbitrary")),
    )(chip, *operands)


class _SiblingSwap:
    def __init__(self, arrays, name, collective_id, after=None):
        self.name, self.n = name, len(arrays)
        n = self.n
        n_in = n + (after is not None)
        sem_shape = pltpu.SemaphoreType.DMA((n,))

        def body(*refs):
            ins, send_sems, recv_sems = refs[:n], refs[n_in], refs[n_in + 1]
            theirs, token = refs[n_in + 2 + n:n_in + 2 + 2 * n], refs[-1]
            _sibling_handshake()
            for cp in self._copies(ins, theirs, send_sems, recv_sems):
                cp.start()
            token[...] = jnp.zeros(token.shape, token.dtype)

        operands, in_specs = [_in_hbm(a) for a in arrays], [HBM_ONLY] * n
        if after is not None:
            operands.append(after)
            in_specs.append(HBM_SPEC)
        outs = pl.pallas_call(
            body, name=name + "_start",
            in_specs=in_specs,
            out_specs=[SEM_SPEC, SEM_SPEC] + [HBM_ONLY] * (2 * n) + [pl.BlockSpec(memory_space=pltpu.VMEM)],
            out_shape=[sem_shape, sem_shape] + [pltpu.HBM(a.shape, a.dtype) for a in arrays] * 2
                      + [jax.ShapeDtypeStruct((8, LANES), F32)],
            input_output_aliases={i: i + 2 for i in range(n)},
            compiler_params=pltpu.CompilerParams(has_side_effects=SIDE_EFFECT, collective_id=collective_id),
        )(*operands)
        self.send_sems, self.recv_sems = outs[0], outs[1]
        self.mine, self.theirs, self.token = list(outs[2:2 + n]), list(outs[2 + n:2 + 2 * n]), outs[-1]

    def _copies(self, mine, theirs, send_sems, recv_sems):
        sibling = (lax.axis_index("x"), lax.axis_index("y"), 1 - lax.axis_index("c"))
        return [pltpu.make_async_remote_copy(src_ref=mine[a], dst_ref=theirs[a], send_sem=send_sems.at[a],
                                             recv_sem=recv_sems.at[a], device_id=sibling, device_id_type=MESH)
                for a in range(self.n)]

    def wait(self, after):
        n = self.n

        def body(*refs):
            for cp in self._copies(refs[:n], refs[n:2 * n], refs[2 * n], refs[2 * n + 1]):
                cp.wait_send()
                cp.wait_recv()

        arrays = self.mine + self.theirs
        outs = pl.pallas_call(
            body, name=self.name + "_wait",
            in_specs=[HBM_ONLY] * (2 * n) + [SEM_SPEC, SEM_SPEC, HBM_SPEC], out_specs=[HBM_ONLY] * (2 * n),
            out_shape=[pltpu.HBM(a.shape, a.dtype) for a in arrays],
            input_output_aliases={i: i for i in range(2 * n)},
            compiler_params=pltpu.CompilerParams(has_side_effects=SIDE_EFFECT),
        )(*[_in_hbm(a) for a in arrays], self.send_sems, self.recv_sems, after)
        return list(outs[:n]), list(outs[n:])


class _GatherDevices:
    def __init__(self, vec):
        sem_shape = pltpu.SemaphoreType.DMA((N_DEV,))

        def body(in_ref, send_sems, recv_sems, vec_ref, out_ref, token):
            for cp in self._copies(in_ref, out_ref, send_sems, recv_sems):
                cp.start()
            token[...] = jnp.zeros(token.shape, token.dtype)

        outs = pl.pallas_call(
            body, name="gather_small_start",
            in_specs=[HBM_ONLY],
            out_specs=[SEM_SPEC, SEM_SPEC, HBM_ONLY, HBM_ONLY, pl.BlockSpec(memory_space=pltpu.VMEM)],
            out_shape=[sem_shape, sem_shape, pltpu.HBM(vec.shape, vec.dtype),
                       pltpu.HBM((N_DEV,) + vec.shape, vec.dtype), jax.ShapeDtypeStruct((8, LANES), F32)],
            input_output_aliases={0: 2},
            compiler_params=pltpu.CompilerParams(has_side_effects=SIDE_EFFECT),
        )(_in_hbm(vec))
        self.send_sems, self.recv_sems, self.vec, self.rows, self.token = outs

    def _copies(self, in_ref, out_ref, send_sems, recv_sems):
        x, y, c = lax.axis_index("x"), lax.axis_index("y"), lax.axis_index("c")
        me = 4 * x + 2 * y + c
        copies = [pltpu.make_async_copy(in_ref, out_ref.at[me], recv_sems.at[0])]
        for rel in range(1, N_DEV):
            peer = (x ^ (rel >> 2), y ^ ((rel >> 1) & 1), c ^ (rel & 1))
            copies.append(pltpu.make_async_remote_copy(
                src_ref=in_ref, dst_ref=out_ref.at[me], send_sem=send_sems.at[rel], recv_sem=recv_sems.at[rel],
                device_id=peer, device_id_type=MESH))
        return copies

    def wait(self, after):
        def body(vec_ref, rows_ref, send_sems, recv_sems, after_ref, vec_out, rows_out):
            copies = self._copies(vec_ref, rows_ref, send_sems, recv_sems)
            copies[0].wait()
            for cp in copies[1:]:
                cp.wait_send()
                cp.wait_recv()

        outs = pl.pallas_call(
            body, name="gather_small_wait",
            in_specs=[HBM_ONLY, HBM_ONLY, SEM_SPEC, SEM_SPEC, HBM_SPEC], out_specs=[HBM_ONLY, HBM_ONLY],
            out_shape=[pltpu.HBM(self.vec.shape, self.vec.dtype), pltpu.HBM(self.rows.shape, self.rows.dtype)],
            input_output_aliases={0: 0, 1: 1},
            compiler_params=pltpu.CompilerParams(has_side_effects=SIDE_EFFECT),
        )(_in_hbm(self.vec), _in_hbm(self.rows), self.send_sems, self.recv_sems, after)
        return outs[1]


BIG = [("a_w_in", "col"), ("a_w_out", "row"), ("kv_w", "row"), ("b_w_q", "row"), ("b_w_out", "row"),
       ("ffn_w_gate_up", "colp"), ("ffn_w_down", "row"), ("ple_w_up", "col"), ("ple_w_gate", "row")]
GATHER_GROUPS = [[("a_w_in", 0), ("small", 0)], [("a_w_out", 0)],
                 [("ffn_w_gate_up", 0), ("ffn_w_down", 0), ("ple_w_gate", 0), ("ple_w_up", 0)],
                 [("kv_w", 0), ("b_w_q", 0), ("b_w_out", 0)],
                 [("ffn_w_gate_up", 1)], [("ffn_w_down", 1), ("ple_w_gate", 1), ("ple_w_up", 1)]]
SCATTER_GROUPS = [[("ple_w_gate", 1), ("ple_w_up", 1), ("ffn_w_down", 1)], [("ffn_w_gate_up", 1)],
                  [("b_w_out", 0), ("b_w_q", 0), ("kv_w", 0)], [("ple_w_gate", 0), ("ple_w_up", 0), ("ffn_w_down", 0)],
                  [("ffn_w_gate_up", 0), ("a_w_out", 0)], [("a_w_in", 0)]]
COLLECTIVE_IDS = {"fill": 0, "swap": 6, "gather": 9, "scatter": 15}
SMALL_SHARDED = ["ln_gain", "ln_bias", "a_lower_bound"]
SMALL_REPLICATED = ["a_norm_gain", "kv_b", "b_b_q", "b_sinks", "b_b_out", "ple_b_gate"]
WEIGHT_ORDER = ["a_w_in", "a_lower_bound", "a_norm_gain", "a_w_out", "kv_w", "kv_b", "b_w_q", "b_b_q", "b_sinks",
                "b_w_out", "b_b_out", "ffn_w_gate_up", "ffn_w_down", "ple_w_up", "ple_w_gate", "ple_b_gate",
                "ln_gain", "ln_bias"]


def _as3(a):
    return a.reshape((-1,) + a.shape[-2:]) if a.ndim >= 3 else a.reshape((1,) + a.shape)


def _pad_lanes(v):
    n = v.shape[-1]
    return jnp.pad(v, ((0, 0), (0, (-n) % LANES)))


ADAM_MANY_STEPS = 2
ADAM_MANY_MAX = 1 << 19


def _adam_many(groups, name):
    in_specs, out_specs, out_shapes, arrays = [], [], [], []
    for group in groups:
        r, c = group[0].shape
        block = pl.BlockSpec((r // ADAM_MANY_STEPS, c), lambda i: (i, 0))
        in_specs += [block] * len(group)
        arrays += list(group)
        out_specs += [block] * 4
        out_shapes += [jax.ShapeDtypeStruct((r, c), F32)] * 4

    def body(*refs):
        ins, outs = refs[:len(arrays)], refs[len(arrays):]
        for k in range(len(groups)):
            res = _adam_fn(*[ref[...] for ref in ins[5 * k:5 * k + 5]])
            for out_ref, val in zip(outs[4 * k:4 * k + 4], res):
                out_ref[...] = val

    result = pl.pallas_call(
        body, name=name, grid=(ADAM_MANY_STEPS,), in_specs=in_specs, out_specs=out_specs, out_shape=out_shapes,
        compiler_params=_params(("parallel",)),
    )(*arrays)
    return [result[4 * k:4 * k + 4] for k in range(len(groups))]


def _adam_small(everyone, chip, items, loss_off):
    n_items = len(items)

    def body(chip_ref, every_ref, *refs):
        ins, outs = refs[:3 * n_items], refs[3 * n_items:]

        def total(off, width):
            acc = every_ref[0, :, off:off + width]
            for s in range(1, N_DEV):
                acc = acc + every_ref[s, :, off:off + width]
            return acc

        for a, (w, _, _, off, sharded) in enumerate(items):
            cols = w.shape[-1]
            for r in range(w.size // cols):
                at = (slice(r, r + 1),) if w.ndim == 2 else (r // w.shape[1], slice(r % w.shape[1], r % w.shape[1] + 1))
                if sharded:
                    full = total(off + r * N_CHIPS * cols, N_CHIPS * cols)
                    g = full[:, 0:cols]
                    for c in range(1, N_CHIPS):
                        g = jnp.where(chip_ref[0] == c, full[:, c * cols:(c + 1) * cols], g)
                else:
                    g = total(off + r * cols, cols)
                w_ref, m_ref, v_ref = ins[3 * a:3 * a + 3]
                res = _adam_fn(w_ref[at], m_ref[at], v_ref[at], g, jnp.zeros_like(g))
                for out_ref, val in zip(outs[4 * a:4 * a + 4], res):
                    out_ref[at] = val
        outs[-1][...] = total(loss_off, LANES)

    def whole(shape):
        return pl.BlockSpec(tuple(shape), lambda i, chip_ref: (0,) * len(shape))

    arrays = [arr for it in items for arr in it[:3]]
    out_shapes = [jax.ShapeDtypeStruct(it[0].shape, F32) for it in items for _ in range(4)]
    out_shapes.append(jax.ShapeDtypeStruct((1, LANES), F32))
    result = pl.pallas_call(
        body, name="adam_small",
        grid_spec=pltpu.PrefetchScalarGridSpec(
            num_scalar_prefetch=1, grid=(1,),
            in_specs=[whole(everyone.shape)] + [whole(arr.shape) for arr in arrays],
            out_specs=[whole(s.shape) for s in out_shapes]),
        out_shape=out_shapes,
        compiler_params=_params(("arbitrary",)),
    )(chip, everyone, *arrays)
    return [result[4 * a:4 * a + 4] for a in range(n_items)], result[-1]


def kernel(x, p, a_w_in, a_lower_bound, a_norm_gain, a_w_out, kv_w, kv_b, b_w_q, b_b_q, b_sinks, b_w_out, b_b_out, ffn_w_gate_up, ffn_w_down, ple_w_up, ple_w_gate, ple_b_gate, ln_gain, ln_bias, loss_target, m_a_w_in, m_a_lower_bound, m_a_norm_gain, m_a_w_out, m_kv_w, m_kv_b, m_b_w_q, m_b_b_q, m_b_sinks, m_b_w_out, m_b_b_out, m_ffn_w_gate_up, m_ffn_w_down, m_ple_w_up, m_ple_w_gate, m_ple_b_gate, m_ln_gain, m_ln_bias, v_a_w_in, v_a_lower_bound, v_a_norm_gain, v_a_w_out, v_kv_w, v_kv_b, v_b_w_q, v_b_b_q, v_b_sinks, v_b_w_out, v_b_b_out, v_ffn_w_gate_up, v_ffn_w_down, v_ple_w_up, v_ple_w_gate, v_ple_b_gate, v_ln_gain, v_ln_bias):
    args = dict(locals())
    wts = {n: args[n] for n in WEIGHT_ORDER}
    mom = {n: args["m_" + n] for n in WEIGHT_ORDER}
    vel = {n: args["v_" + n] for n in WEIGHT_ORDER}
    chip = 2 * lax.axis_index("x") + lax.axis_index("y")
    d = x.shape[-1]
    dq = d // N_CHIPS

    kind_of = dict(BIG)
    kind_of["small"] = "col"
    chip_arr = chip.reshape(1).astype(jnp.int32)
    small_pack = jnp.concatenate([wts[n].reshape(-1, dq) for n in SMALL_SHARDED], axis=0)[None]

    def place_item(key):
        n, layer = key
        if n == "small":
            return small_pack, 0, "col", F32
        return _as3(wts[n]), layer, kind_of[n], BF16

    gathers, where = [], {}
    for gi, group in enumerate(GATHER_GROUPS):
        prev = gathers[-1].token if gathers else None
        placed = _place([place_item(k) for k in group], chip_arr, name=f"place{gi}", after=prev)
        gathers.append(_Exchange("gather", [], placed, [kind_of[k[0]] for k in group],
                                 [0] * len(group), f"gather{gi}", COLLECTIVE_IDS["gather"] + gi, after=prev,
                                 halves=[k[0] != "small" for k in group]))
        for k in group:
            where[k] = gi
    all_started = gathers[-1].token
    ready = {}

    fills = {}

    def pass_on(gi, after):
        if gi not in fills:
            group = GATHER_GROUPS[gi]
            outs = gathers[gi].wait(after)[1]
            split = [i for i, k in enumerate(group) if k[0] != "small"]
            fills[gi] = (outs, split, _SiblingFill([outs[i] for i in split], [kind_of[group[i][0]] for i in split],
                                                   f"fill{gi}", COLLECTIVE_IDS["fill"] + gi))

    def wget(name, layer, after):
        key = (name, layer)
        if key not in ready:
            gi = where[key]
            after = all_started if gi == 0 else after
            pass_on(gi, after)
            if 1 <= gi < len(GATHER_GROUPS) - 1:
                pass_on(gi + 1, after)
                after = fills[gi + 1][2].token
            outs, split, fill = fills[gi]
            for i, arr in zip(split, fill.wait(after)):
                outs[i] = arr
            for k, arr in zip(GATHER_GROUPS[gi], outs):
                ready[k] = arr
        return ready[key]

    small_full = wget("small", 0, None)[0]
    ln_gain_f = small_full[0:6].reshape(DEPTH, 3, d)
    ln_bias_f = small_full[6:12].reshape(DEPTH, 3, d)
    alb_f = small_full[12:14]

    group_of = {k: gi for gi, group in enumerate(SCATTER_GROUPS) for k in group}
    grads_done, zones, scatters = {}, {}, []

    def grad_sink(name, layer, grad):
        grads_done[(name, layer)] = grad
        if name not in zones:
            zones[name] = lax.empty((N_CHIPS,) + _as3(wts[name]).shape, BF16)
        gi = group_of[(name, layer)]
        group = SCATTER_GROUPS[gi]
        if not all(k in grads_done for k in group):
            return None
        ex = _Exchange("scatter", [grads_done[k] for k in group], [zones[k[0]] for k in group],
                       [kind_of[k[0]] for k in group], [k[1] for k in group], f"scatter{gi}",
                       COLLECTIVE_IDS["scatter"] + gi)
        for k, zone in zip(group, ex.lands):
            zones[k[0]] = zone
        scatters.append((ex, group))
        return ex.token

    small = {}

    def small_sink(loss, gs):
        ln_g = jnp.concatenate([gs[f"ln_gain_{i}_{j}"] for i in range(DEPTH) for j in range(3)], axis=0)
        ln_b = jnp.concatenate([gs[f"ln_bias_{i}_{j}"] for i in range(DEPTH) for j in range(3)], axis=0)
        ple_bg = jnp.concatenate([gs[f"ple_b_{i}"] for i in range(DEPTH)], axis=0)
        small["list"] = [ln_g.reshape(1, -1), ln_b.reshape(1, -1), gs["alb"].reshape(1, -1), gs["norm_gain"],
                         gs["kv_b"], gs["b_q"], _pad_lanes(gs["sinks"]), gs["b_out"], ple_bg.reshape(1, -1), loss]
        small["gather"] = _GatherDevices(jnp.concatenate(small["list"], axis=1))
        return small["gather"].token

    loss, grad_x, gs = _local_step(
        x[0], p.reshape((p.shape[0],) + p.shape[2:]), loss_target[0], wget, grad_sink, ln_gain_f, ln_bias_f, alb_f, a_norm_gain, kv_b, b_b_q,
        b_sinks, b_b_out, ple_b_gate, small_sink)

    res = {}

    def arrive(batch, after):
        for ex, group in batch:
            srcs, outs = ex.wait(after, lands=[zones[k[0]] for k in group])
            for k, grad, zone in zip(group, srcs, outs):
                grads_done[k], zones[k[0]] = grad, zone

    def half_sums(names, batch, after):
        partial = []
        for n in names:
            own = [grads_done[(n, layer)] for layer in range(zones[n].shape[1])]
            partial.append(_sum_arrivals(zones[n], own, kind_of[n], chip_arr, f"sum_{n}", after=after))
        return _SiblingSwap(partial, f"swap{batch}", COLLECTIVE_IDS["swap"] + batch, after=after)

    def update(names, swap, after):
        flat = lambda a: a.reshape(-1, a.shape[-1])
        work = [(n, [flat(wts[n]), flat(mom[n]), flat(vel[n]), own, sib]) for n, own, sib in zip(names, *swap.wait(after))]
        many = [(n, ops) for n, ops in work if wts[n].size <= ADAM_MANY_MAX]
        if len(many) > 1:
            for (n, _), out in zip(many, _adam_many([ops for _, ops in many], f"adam_from_{many[0][0]}")):
                res[n] = [o.reshape(wts[n].shape) for o in out]
        for n, ops in work:
            if n not in res:
                out = _rowwise(_adam_fn, ops, [], [(ops[3].shape, F32)] * 4, name=f"adam_{n}")
                res[n] = [o.reshape(wts[n].shape) for o in out]
        return res[names[-1]][1]

    last_names = [k[0] for k in SCATTER_GROUPS[-1]]
    batches = [["ffn_w_gate_up"], [n for n, _ in BIG if n != "ffn_w_gate_up" and n not in last_names], last_names]
    arrive(scatters[:-1], grad_x)
    swap0 = half_sums(batches[0], 0, None)
    swap1 = half_sums(batches[1], 1, swap0.token)
    updated = update(batches[0], swap0, swap1.token)
    arrive(scatters[-1:], updated)
    swap2 = half_sums(batches[2], 2, swap1.token)
    updated = update(batches[1], swap1, swap2.token)
    update(batches[2], swap2, updated)

    everyone = small["gather"].wait(grad_x)
    offs, pos = [], 0
    for v in small["list"]:
        offs.append(pos)
        pos += v.shape[1]
    names = ["ln_gain", "ln_bias", "a_lower_bound", "a_norm_gain", "kv_b", "b_b_q", "b_sinks", "b_b_out", "ple_b_gate"]
    as_rows = lambda a: a.reshape(1, -1) if a.ndim == 1 else a
    items = [(as_rows(wts[n]), as_rows(mom[n]), as_rows(vel[n]), off, n in SMALL_SHARDED)
             for n, off in zip(names, offs)]
    updates, loss_row = _adam_small(everyone, chip_arr, items, offs[len(names)])
    for n, upd in zip(names, updates):
        res[n] = [u.reshape(wts[n].shape) for u in upd]

    outs = [loss_row[0, 0], grad_x[None]]
    for k in range(4):
        outs += [res[n][k] for n in WEIGHT_ORDER]
    return tuple(outs)
```
